```python
import math
import jax, jax.numpy as jnp
from jax import lax
import numpy as np

D_MODEL = 1024
BATCH = 8
SEQ = 4096
DEPTH = 1

ATTN_WINDOWS = (128, 512, 2048)
ATTN_DILATIONS = (1, 4, 16)
N_GROUPS = len(ATTN_WINDOWS)
ATTN_HEAD_DIM = 64
ATTN_HEADS_PER_GROUP = D_MODEL // 128
N_ATTN_HEADS = N_GROUPS * ATTN_HEADS_PER_GROUP
ATTN_OUT_WIDTH = ATTN_HEADS_PER_GROUP * ATTN_HEAD_DIM
ATTN_QKV_WIDTH = N_GROUPS * 3 * ATTN_HEADS_PER_GROUP * ATTN_HEAD_DIM
ATTN_BLOCK = 128
NEG_INF = -1e30
NUM_BUCKETS = 32
MAX_EXACT = NUM_BUCKETS // 2
MAX_DISTANCE = 2048
HGRN_HEADS = D_MODEL // 256
HGRN_DK = 128
HGRN_DV = 128
HGRN_WIDTH = HGRN_HEADS * HGRN_DK
HGRN_CHUNK = 32
GATE_WIDTH = 2 * D_MODEL
IN_WIDTH = ATTN_QKV_WIDTH + 4 * HGRN_WIDTH + GATE_WIDTH
D_FF = ((8 * D_MODEL // 3 + 127) // 128) * 128
CONV_WIDTH = 3
NORM_EPS = 1e-6

kernel_name = "hybrid_dilated_attn_hgrn2_convffn"


def rms_norm(x, w):
    xf = x.astype(jnp.float32)
    y = xf * lax.rsqrt(jnp.mean(xf * xf, axis=-1, keepdims=True) + NORM_EPS)
    return (y * w.astype(jnp.float32)).astype(x.dtype)


def t5_bucket(dist):
    n = jnp.maximum(dist, 0)
    nf = jnp.maximum(n, 1).astype(jnp.float32)
    large = MAX_EXACT + (jnp.log(nf / MAX_EXACT) / math.log(MAX_DISTANCE / MAX_EXACT)
                         * (NUM_BUCKETS - MAX_EXACT)).astype(jnp.int32)
    large = jnp.minimum(large, NUM_BUCKETS - 1)
    return jnp.where(n < MAX_EXACT, n, large)


def dilated_window_attention(q, k, v, bias_table, dilation, window):
    Bn, S, H, hd = q.shape
    blk = ATTN_BLOCK
    span = dilation * blk
    Sp = -(-S // span) * span
    pad = ((0, 0), (0, Sp - S), (0, 0), (0, 0))
    q, k, v = jnp.pad(q, pad), jnp.pad(k, pad), jnp.pad(v, pad)
    U = Sp // dilation
    NB = U // blk

    def to_sub(t):
        return t.reshape(Bn, U, dilation, H, hd).transpose(0, 2, 3, 1, 4)

    def band(t):
        ts = to_sub(t)
        prev = jnp.pad(ts, ((0, 0), (0, 0), (0, 0), (blk, 0), (0, 0)))[:, :, :, :U]
        return jnp.concatenate([prev.reshape(Bn, dilation, H, NB, blk, hd),
                                ts.reshape(Bn, dilation, H, NB, blk, hd)], axis=-2)

    qs = to_sub(q).reshape(Bn, dilation, H, NB, blk, hd)
    kb, vb = band(k), band(v)
    s = jnp.einsum('bdhnqe,bdhnke->bdhnqk', qs, kb).astype(jnp.float32) * (hd ** -0.5)

    rel = jnp.arange(blk)[:, None] + blk - jnp.arange(2 * blk)[None, :]
    in_win = (rel >= 0) & (rel <= window // dilation)
    key_ok = (jnp.arange(NB)[:, None] * blk - blk + jnp.arange(2 * blk)[None, :]) >= 0
    mask = in_win[None] & key_ok[:, None, :]
    bias = bias_table[t5_bucket(rel * dilation)].astype(jnp.float32)
    s = s + bias.transpose(2, 0, 1)[:, None]
    s = jnp.where(mask, s, NEG_INF)

    m = jnp.max(s, axis=-1, keepdims=True)
    p = jnp.exp(s - m)
    l = jnp.sum(p, axis=-1, keepdims=True)
    o = jnp.einsum('bdhnqk,bdhnke->bdhnqe', p.astype(vb.dtype), vb).astype(jnp.float32) / l
    lse = (m + jnp.log(l))[..., 0]

    o = o.reshape(Bn, dilation, H, U, hd).transpose(0, 3, 1, 2, 4).reshape(Bn, Sp, H, hd)[:, :S]
    lse = lse.reshape(Bn, dilation, H, U).transpose(0, 3, 1, 2).reshape(Bn, Sp, H)[:, :S]
    return o, lse


def attention_branch(attn_cols, rel_bias):
    Bn, S, _ = attn_cols.shape
    qkv = attn_cols.reshape(Bn, S, N_GROUPS, 3, ATTN_HEADS_PER_GROUP, ATTN_HEAD_DIM)
    outs, lses = [], []
    for g in range(N_GROUPS):
        bias_g = rel_bias[:, g * ATTN_HEADS_PER_GROUP:(g + 1) * ATTN_HEADS_PER_GROUP]
        o, lse = dilated_window_attention(qkv[:, :, g, 0], qkv[:, :, g, 1], qkv[:, :, g, 2],
                                          bias_g, ATTN_DILATIONS[g], ATTN_WINDOWS[g])
        outs.append(o)
        lses.append(lse)
    w = jax.nn.softmax(jnp.stack(lses, axis=0), axis=0)[..., None]
    y = jnp.sum(w * jnp.stack(outs, axis=0), axis=0)
    return y.reshape(Bn, S, ATTN_OUT_WIDTH).astype(attn_cols.dtype)


def hgrn2_chunked(q, k, g, v):
    Bn, H, S, dk = q.shape
    dv = v.shape[-1]
    C = HGRN_CHUNK
    NC = S // C
    q, k, g = (t.reshape(Bn, H, NC, C, dk) for t in (q, k, g))
    v = v.reshape(Bn, H, NC, C, dv)
    G = jnp.cumsum(g, axis=3)
    G_last = G[:, :, :, -1:, :]
    q_t = q * jnp.exp(G)
    k_t = k * jnp.exp(-G)
    k_dec = k * jnp.exp(G_last - G)
    tril = jnp.tril(jnp.ones((C, C), dtype=bool))
    A = jnp.where(tril, jnp.einsum('bhnik,bhnjk->bhnij', q_t, k_t), 0.0)
    o_intra = jnp.einsum('bhnij,bhnjv->bhniv', A, v)

    def step(state, xs):
        qc, kc, glc, vc = xs
        o = jnp.einsum('bhik,bhkv->bhiv', qc, state)
        state = jnp.exp(glc)[:, :, 0, :, None] * state + jnp.einsum('bhjk,bhjv->bhkv', kc, vc)
        return state, o

    xs = tuple(jnp.moveaxis(t, 2, 0) for t in (q_t, k_dec, G_last, v))
    state0 = jnp.zeros((Bn, H, dk, dv), jnp.float32)
    _, o_inter = lax.scan(step, state0, xs)
    o = o_intra + jnp.moveaxis(o_inter, 0, 2)
    return o.reshape(Bn, H, S, dv)


def hgrn2_branch(hg_cols, lb, norm_w):
    Bn, S, _ = hg_cols.shape
    q, f, i, og = jnp.split(hg_cols, 4, axis=-1)

    def heads(t):
        return t.reshape(Bn, S, HGRN_HEADS, -1).transpose(0, 2, 1, 3).astype(jnp.float32)

    lb_h = lb.astype(jnp.float32).reshape(HGRN_HEADS, 1, HGRN_DK)
    q = jax.nn.silu(heads(q))
    f = lb_h + (1.0 - lb_h) * jax.nn.sigmoid(heads(f))
    o = hgrn2_chunked(q, 1.0 - f, jnp.log(f), heads(i))
    o = o * lax.rsqrt(jnp.mean(o * o, axis=-1, keepdims=True) + NORM_EPS) * norm_w.astype(jnp.float32)
    o = o.transpose(0, 2, 1, 3).reshape(Bn, S, HGRN_WIDTH)
    return (o * jax.nn.silu(og.astype(jnp.float32))).astype(hg_cols.dtype)


def conv_ffn(h, w_up, conv_w, conv_b, w_down):
    S = h.shape[1]
    u = h @ w_up
    up = jnp.pad(u, ((0, 0), (CONV_WIDTH - 1, 0), (0, 0)))
    c = conv_b
    for j in range(CONV_WIDTH):
        c = c + conv_w[j] * up[:, j:j + S]
    gate, val = jnp.split(c, 2, axis=-1)
    return (jax.nn.gelu(gate, approximate=False) * val) @ w_down


def _fwd_setup_inputs(seed: int = 0) -> dict:
    key = jax.random.key(seed)
    ks = jax.random.split(key, 20)
    f32 = jnp.float32

    def nrm(k, shape, scale):
        return jax.random.normal(k, shape, f32) * scale

    def gain(k, shape):
        return 1.0 + 0.05 * jax.random.normal(k, shape, f32)

    return {
        "x": nrm(ks[0], (BATCH, SEQ, D_MODEL), 1.0),
        "pre_mix_norm": gain(ks[1], (DEPTH, D_MODEL)),
        "w_in": nrm(ks[2], (DEPTH, D_MODEL, IN_WIDTH), D_MODEL ** -0.5),
        "rel_bias": nrm(ks[3], (NUM_BUCKETS, N_ATTN_HEADS), 0.5),
        "hgrn_lb_raw": nrm(ks[4], (DEPTH + 1, HGRN_WIDTH), 0.1),
        "hgrn_norm": gain(ks[5], (DEPTH, HGRN_DV)),
        "w_branch_attn": nrm(ks[6], (DEPTH, ATTN_OUT_WIDTH, D_MODEL), ATTN_OUT_WIDTH ** -0.5),
        "w_branch_hgrn": nrm(ks[7], (DEPTH, HGRN_WIDTH, D_MODEL), HGRN_WIDTH ** -0.5),
        "w_out": nrm(ks[8], (DEPTH, D_MODEL, D_MODEL), D_MODEL ** -0.5),
        "post_mix_norm": gain(ks[9], (DEPTH, D_MODEL)),
        "pre_ffn_norm": gain(ks[10], (DEPTH, D_MODEL)),
        "w_up": nrm(ks[11], (DEPTH, D_MODEL, 2 * D_FF), D_MODEL ** -0.5),
        "conv_w": nrm(ks[12], (DEPTH, CONV_WIDTH, 2 * D_FF), CONV_WIDTH ** -0.5),
        "conv_b": nrm(ks[13], (DEPTH, 2 * D_FF), 0.02),
        "w_down": nrm(ks[14], (DEPTH, D_FF, D_MODEL), D_FF ** -0.5),
        "post_ffn_norm": gain(ks[15], (DEPTH, D_MODEL)),
    }


def _fwd_reference(x, pre_mix_norm, w_in, rel_bias, hgrn_lb_raw, hgrn_norm, w_branch_attn,
              w_branch_hgrn, w_out, post_mix_norm, pre_ffn_norm, w_up, conv_w, conv_b,
              w_down, post_ffn_norm):
    Bn, S, _ = x.shape
    lbs = jnp.cumsum(jax.nn.softmax(hgrn_lb_raw.astype(jnp.float32), axis=0), axis=0)
    split_at = [ATTN_QKV_WIDTH, ATTN_QKV_WIDTH + 4 * HGRN_WIDTH]
    for l in range(DEPTH):
        h = rms_norm(x, pre_mix_norm[l])
        proj = h @ w_in[l]
        attn_cols, hg_cols, gate_cols = jnp.split(proj, split_at, axis=-1)
        y_attn = attention_branch(attn_cols, rel_bias)
        y_hgrn = hgrn2_branch(hg_cols, lbs[l], hgrn_norm[l])
        gates = jax.nn.sigmoid(gate_cols.astype(jnp.float32)).reshape(Bn, S, 2, D_MODEL)
        merged = (gates[:, :, 0] * (y_attn @ w_branch_attn[l])
                  + gates[:, :, 1] * (y_hgrn @ w_branch_hgrn[l])).astype(x.dtype)
        x = x + rms_norm(merged @ w_out[l], post_mix_norm[l])
        h = rms_norm(x, pre_ffn_norm[l])
        x = x + rms_norm(conv_ffn(h, w_up[l], conv_w[l], conv_b[l], w_down[l]), post_ffn_norm[l])
    return x


import jax as _jax
import jax.numpy as _jnp

TWIN_FORMAT = 'train_step'
FWD_PARAMS = ['x', 'pre_mix_norm', 'w_in', 'rel_bias', 'hgrn_lb_raw', 'hgrn_norm', 'w_branch_attn', 'w_branch_hgrn', 'w_out', 'post_mix_norm', 'pre_ffn_norm', 'w_up', 'conv_w', 'conv_b', 'w_down', 'post_ffn_norm']
TWIN_WEIGHTS = ['pre_mix_norm', 'w_in', 'rel_bias', 'hgrn_lb_raw', 'hgrn_norm', 'w_branch_attn', 'w_branch_hgrn', 'w_out', 'post_mix_norm', 'pre_ffn_norm', 'w_up', 'conv_w', 'conv_b', 'w_down', 'post_ffn_norm']
TWIN_DIFF_INPUT = 'x'
TWIN_INPUTS = ['x', 'pre_mix_norm', 'w_in', 'rel_bias', 'hgrn_lb_raw', 'hgrn_norm', 'w_branch_attn', 'w_branch_hgrn', 'w_out', 'post_mix_norm', 'pre_ffn_norm', 'w_up', 'conv_w', 'conv_b', 'w_down', 'post_ffn_norm', 'loss_target', 'm_pre_mix_norm', 'm_w_in', 'm_rel_bias', 'm_hgrn_lb_raw', 'm_hgrn_norm', 'm_w_branch_attn', 'm_w_branch_hgrn', 'm_w_out', 'm_post_mix_norm', 'm_pre_ffn_norm', 'm_w_up', 'm_conv_w', 'm_conv_b', 'm_w_down', 'm_post_ffn_norm', 'v_pre_mix_norm', 'v_w_in', 'v_rel_bias', 'v_hgrn_lb_raw', 'v_hgrn_norm', 'v_w_branch_attn', 'v_w_branch_hgrn', 'v_w_out', 'v_post_mix_norm', 'v_pre_ffn_norm', 'v_w_up', 'v_conv_w', 'v_conv_b', 'v_w_down', 'v_post_ffn_norm']
TWIN_OUTPUTS = ['loss', 'grad_x', 'grad_pre_mix_norm', 'grad_w_in', 'grad_rel_bias', 'grad_hgrn_lb_raw', 'grad_hgrn_norm', 'grad_w_branch_attn', 'grad_w_branch_hgrn', 'grad_w_out', 'grad_post_mix_norm', 'grad_pre_ffn_norm', 'grad_w_up', 'grad_conv_w', 'grad_conv_b', 'grad_w_down', 'grad_post_ffn_norm', 'delta_pre_mix_norm', 'delta_w_in', 'delta_rel_bias', 'delta_hgrn_lb_raw', 'delta_hgrn_norm', 'delta_w_branch_attn', 'delta_w_branch_hgrn', 'delta_w_out', 'delta_post_mix_norm', 'delta_pre_ffn_norm', 'delta_w_up', 'delta_conv_w', 'delta_conv_b', 'delta_w_down', 'delta_post_ffn_norm', 'new_m_pre_mix_norm', 'new_m_w_in', 'new_m_rel_bias', 'new_m_hgrn_lb_raw', 'new_m_hgrn_norm', 'new_m_w_branch_attn', 'new_m_w_branch_hgrn', 'new_m_w_out', 'new_m_post_mix_norm', 'new_m_pre_ffn_norm', 'new_m_w_up', 'new_m_conv_w', 'new_m_conv_b', 'new_m_w_down', 'new_m_post_ffn_norm', 'new_v_pre_mix_norm', 'new_v_w_in', 'new_v_rel_bias', 'new_v_hgrn_lb_raw', 'new_v_hgrn_norm', 'new_v_w_branch_attn', 'new_v_w_branch_hgrn', 'new_v_w_out', 'new_v_post_mix_norm', 'new_v_pre_ffn_norm', 'new_v_w_up', 'new_v_conv_w', 'new_v_conv_b', 'new_v_w_down', 'new_v_post_ffn_norm']
TWIN_LEAF_KINDS = {'loss': 'loss', 'grad_x': 'grad_x', 'grad_pre_mix_norm': 'grad_w', 'grad_w_in': 'grad_w', 'grad_rel_bias': 'grad_w', 'grad_hgrn_lb_raw': 'grad_w', 'grad_hgrn_norm': 'grad_w', 'grad_w_branch_attn': 'grad_w', 'grad_w_branch_hgrn': 'grad_w', 'grad_w_out': 'grad_w', 'grad_post_mix_norm': 'grad_w', 'grad_pre_ffn_norm': 'grad_w', 'grad_w_up': 'grad_w', 'grad_conv_w': 'grad_w', 'grad_conv_b': 'grad_w', 'grad_w_down': 'grad_w', 'grad_post_ffn_norm': 'grad_w', 'delta_pre_mix_norm': 'delta_w', 'delta_w_in': 'delta_w', 'delta_rel_bias': 'delta_w', 'delta_hgrn_lb_raw': 'delta_w', 'delta_hgrn_norm': 'delta_w', 'delta_w_branch_attn': 'delta_w', 'delta_w_branch_hgrn': 'delta_w', 'delta_w_out': 'delta_w', 'delta_post_mix_norm': 'delta_w', 'delta_pre_ffn_norm': 'delta_w', 'delta_w_up': 'delta_w', 'delta_conv_w': 'delta_w', 'delta_conv_b': 'delta_w', 'delta_w_down': 'delta_w', 'delta_post_ffn_norm': 'delta_w', 'new_m_pre_mix_norm': 'new_m', 'new_m_w_in': 'new_m', 'new_m_rel_bias': 'new_m', 'new_m_hgrn_lb_raw': 'new_m', 'new_m_hgrn_norm': 'new_m', 'new_m_w_branch_attn': 'new_m', 'new_m_w_branch_hgrn': 'new_m', 'new_m_w_out': 'new_m', 'new_m_post_mix_norm': 'new_m', 'new_m_pre_ffn_norm': 'new_m', 'new_m_w_up': 'new_m', 'new_m_conv_w': 'new_m', 'new_m_conv_b': 'new_m', 'new_m_w_down': 'new_m', 'new_m_post_ffn_norm': 'new_m', 'new_v_pre_mix_norm': 'new_v', 'new_v_w_in': 'new_v', 'new_v_rel_bias': 'new_v', 'new_v_hgrn_lb_raw': 'new_v', 'new_v_hgrn_norm': 'new_v', 'new_v_w_branch_attn': 'new_v', 'new_v_w_branch_hgrn': 'new_v', 'new_v_w_out': 'new_v', 'new_v_post_mix_norm': 'new_v', 'new_v_pre_ffn_norm': 'new_v', 'new_v_w_up': 'new_v', 'new_v_conv_w': 'new_v', 'new_v_conv_b': 'new_v', 'new_v_w_down': 'new_v', 'new_v_post_ffn_norm': 'new_v'}


def _forward(args):
    return _fwd_reference(*[args[k] for k in FWD_PARAMS])


def _output_shape():
    out = _jax.eval_shape(lambda: _forward(_fwd_setup_inputs(0)))
    return out.shape, out.dtype

N_MICROBATCH = 1
ADAM_LR = 0.001
ADAM_B1 = 0.9
ADAM_B2 = 0.999
ADAM_EPS = 1e-08
ADAM_WD = 0.01
ADAM_STEP = 10
PER_EXAMPLE_BATCH_AXIS = {'x': 0, 'loss_target': 0}
SHARED_INPUTS = []
_WEIGHT_DTYPES = {'pre_mix_norm': _jnp.float32, 'w_in': _jnp.float32, 'rel_bias': _jnp.float32, 'hgrn_lb_raw': _jnp.float32, 'hgrn_norm': _jnp.float32, 'w_branch_attn': _jnp.float32, 'w_branch_hgrn': _jnp.float32, 'w_out': _jnp.float32, 'post_mix_norm': _jnp.float32, 'pre_ffn_norm': _jnp.float32, 'w_up': _jnp.float32, 'conv_w': _jnp.float32, 'conv_b': _jnp.float32, 'w_down': _jnp.float32, 'post_ffn_norm': _jnp.float32}
MOMENT_SCALE = {'pre_mix_norm': 6.884565e-01, 'w_in': 2.307790e-01, 'rel_bias': 1.261086e-01, 'hgrn_lb_raw': 5.469111e-02, 'hgrn_norm': 1.557726e+00, 'w_branch_attn': 1.363543e-01, 'w_branch_hgrn': 4.977456e-01, 'w_out': 5.127273e-01, 'post_mix_norm': 3.212569e+01, 'pre_ffn_norm': 5.411607e-01, 'w_up': 2.188680e-01, 'conv_w': 2.326930e-01, 'conv_b': 3.630540e-01, 'w_down': 4.099037e-01, 'post_ffn_norm': 3.198956e+01}


def _to_microbatches(a, axis):
    t = _jnp.moveaxis(a, axis, 0)
    t = t.reshape((N_MICROBATCH, t.shape[0] // N_MICROBATCH) + t.shape[1:])
    return _jnp.moveaxis(t, 1, axis + 1)


def setup_inputs(seed: int = 0) -> dict:
    inp = _fwd_setup_inputs(seed)
    key = _jax.random.fold_in(_jax.random.key(seed), 7919)
    shape, _ = _output_shape()
    out = dict(inp)
    out["loss_target"] = _jax.random.normal(_jax.random.fold_in(key, 0), shape, _jnp.float32)
    for i, name in enumerate(TWIN_WEIGHTS):
        w = inp[name].astype(_jnp.float32)
        if MOMENT_SCALE is None:
            s = _jnp.sqrt(_jnp.mean(_jnp.square(w)) + 1e-30)
        else:
            s = MOMENT_SCALE[name]
        km, kv = _jax.random.split(_jax.random.fold_in(key, i + 1))
        out[name] = w
        out["m_" + name] = s * _jax.random.normal(km, w.shape, _jnp.float32)
        out["v_" + name] = (s * s) * _jax.random.uniform(kv, w.shape, _jnp.float32, 0.5, 1.5)
    if N_MICROBATCH > 1:
        for name, axis in PER_EXAMPLE_BATCH_AXIS.items():
            out[name] = _to_microbatches(out[name], axis)
    return {'x': out['x'], 'pre_mix_norm': out['pre_mix_norm'], 'w_in': out['w_in'], 'rel_bias': out['rel_bias'], 'hgrn_lb_raw': out['hgrn_lb_raw'], 'hgrn_norm': out['hgrn_norm'], 'w_branch_attn': out['w_branch_attn'], 'w_branch_hgrn': out['w_branch_hgrn'], 'w_out': out['w_out'], 'post_mix_norm': out['post_mix_norm'], 'pre_ffn_norm': out['pre_ffn_norm'], 'w_up': out['w_up'], 'conv_w': out['conv_w'], 'conv_b': out['conv_b'], 'w_down': out['w_down'], 'post_ffn_norm': out['post_ffn_norm'], 'loss_target': out['loss_target'], 'm_pre_mix_norm': out['m_pre_mix_norm'], 'm_w_in': out['m_w_in'], 'm_rel_bias': out['m_rel_bias'], 'm_hgrn_lb_raw': out['m_hgrn_lb_raw'], 'm_hgrn_norm': out['m_hgrn_norm'], 'm_w_branch_attn': out['m_w_branch_attn'], 'm_w_branch_hgrn': out['m_w_branch_hgrn'], 'm_w_out': out['m_w_out'], 'm_post_mix_norm': out['m_post_mix_norm'], 'm_pre_ffn_norm': out['m_pre_ffn_norm'], 'm_w_up': out['m_w_up'], 'm_conv_w': out['m_conv_w'], 'm_conv_b': out['m_conv_b'], 'm_w_down': out['m_w_down'], 'm_post_ffn_norm': out['m_post_ffn_norm'], 'v_pre_mix_norm': out['v_pre_mix_norm'], 'v_w_in': out['v_w_in'], 'v_rel_bias': out['v_rel_bias'], 'v_hgrn_lb_raw': out['v_hgrn_lb_raw'], 'v_hgrn_norm': out['v_hgrn_norm'], 'v_w_branch_attn': out['v_w_branch_attn'], 'v_w_branch_hgrn': out['v_w_branch_hgrn'], 'v_w_out': out['v_w_out'], 'v_post_mix_norm': out['v_post_mix_norm'], 'v_pre_ffn_norm': out['v_pre_ffn_norm'], 'v_w_up': out['v_w_up'], 'v_conv_w': out['v_conv_w'], 'v_conv_b': out['v_conv_b'], 'v_w_down': out['v_w_down'], 'v_post_ffn_norm': out['v_post_ffn_norm']}


def _loss(weights, diff, rest, loss_target):
    with _jax.named_scope("forward"):
        args = {**rest, TWIN_DIFF_INPUT: diff, **{k: w.astype(_WEIGHT_DTYPES[k]) for k, w in weights.items()}}
        y = _forward(args)
    with _jax.named_scope("loss_head"):
        err = _jnp.square(y.astype(_jnp.float32) - loss_target)
        return 0.5 * _jnp.sum(_jnp.mean(err, axis=-1)) if err.ndim else 0.5 * err


def _adamw(w, g, m, v):
    m = ADAM_B1 * m + (1.0 - ADAM_B1) * g
    v = ADAM_B2 * v + (1.0 - ADAM_B2) * _jnp.square(g)
    m_hat = m / (1.0 - ADAM_B1 ** ADAM_STEP)
    v_hat = v / (1.0 - ADAM_B2 ** ADAM_STEP)
    delta = -ADAM_LR * (m_hat / (_jnp.sqrt(v_hat) + ADAM_EPS) + ADAM_WD * w)
    return delta, m, v


def reference(x, pre_mix_norm, w_in, rel_bias, hgrn_lb_raw, hgrn_norm, w_branch_attn, w_branch_hgrn, w_out, post_mix_norm, pre_ffn_norm, w_up, conv_w, conv_b, w_down, post_ffn_norm, loss_target, m_pre_mix_norm, m_w_in, m_rel_bias, m_hgrn_lb_raw, m_hgrn_norm, m_w_branch_attn, m_w_branch_hgrn, m_w_out, m_post_mix_norm, m_pre_ffn_norm, m_w_up, m_conv_w, m_conv_b, m_w_down, m_post_ffn_norm, v_pre_mix_norm, v_w_in, v_rel_bias, v_hgrn_lb_raw, v_hgrn_norm, v_w_branch_attn, v_w_branch_hgrn, v_w_out, v_post_mix_norm, v_pre_ffn_norm, v_w_up, v_conv_w, v_conv_b, v_w_down, v_post_ffn_norm):
    given = dict(x=x, pre_mix_norm=pre_mix_norm, w_in=w_in, rel_bias=rel_bias, hgrn_lb_raw=hgrn_lb_raw, hgrn_norm=hgrn_norm, w_branch_attn=w_branch_attn, w_branch_hgrn=w_branch_hgrn, w_out=w_out, post_mix_norm=post_mix_norm, pre_ffn_norm=pre_ffn_norm, w_up=w_up, conv_w=conv_w, conv_b=conv_b, w_down=w_down, post_ffn_norm=post_ffn_norm, loss_target=loss_target, m_pre_mix_norm=m_pre_mix_norm, m_w_in=m_w_in, m_rel_bias=m_rel_bias, m_hgrn_lb_raw=m_hgrn_lb_raw, m_hgrn_norm=m_hgrn_norm, m_w_branch_attn=m_w_branch_attn, m_w_branch_hgrn=m_w_branch_hgrn, m_w_out=m_w_out, m_post_mix_norm=m_post_mix_norm, m_pre_ffn_norm=m_pre_ffn_norm, m_w_up=m_w_up, m_conv_w=m_conv_w, m_conv_b=m_conv_b, m_w_down=m_w_down, m_post_ffn_norm=m_post_ffn_norm, v_pre_mix_norm=v_pre_mix_norm, v_w_in=v_w_in, v_rel_bias=v_rel_bias, v_hgrn_lb_raw=v_hgrn_lb_raw, v_hgrn_norm=v_hgrn_norm, v_w_branch_attn=v_w_branch_attn, v_w_branch_hgrn=v_w_branch_hgrn, v_w_out=v_w_out, v_post_mix_norm=v_post_mix_norm, v_pre_ffn_norm=v_pre_ffn_norm, v_w_up=v_w_up, v_conv_w=v_conv_w, v_conv_b=v_conv_b, v_w_down=v_w_down, v_post_ffn_norm=v_post_ffn_norm)
    weights = {n: given[n] for n in TWIN_WEIGHTS}
    shared = {n: given[n] for n in SHARED_INPUTS}
    per_example = {n: given[n] for n in ['x']}
    grad_fn = _jax.value_and_grad(_loss, argnums=(0, 1))

    def one_microbatch(ex, loss_target):
        ex = dict(ex)
        diff = ex.pop(TWIN_DIFF_INPUT)
        return grad_fn(weights, diff, {**shared, **ex}, loss_target)

    if N_MICROBATCH == 1:
        loss, (grad_w, grad_x) = one_microbatch(per_example, given["loss_target"])
    else:
        def body(carry, xs):
            loss_sum, grad_sum = carry
            l_k, (gw_k, gx_k) = one_microbatch(xs[0], xs[1])
            with _jax.named_scope("update"):
                return (loss_sum + l_k, _jax.tree.map(_jnp.add, grad_sum, gw_k)), gx_k

        init = (_jnp.zeros((), _jnp.float32), _jax.tree.map(_jnp.zeros_like, weights))
        (loss, grad_w), grad_x = _jax.lax.scan(body, init, (per_example, given["loss_target"]))
    with _jax.named_scope("update"):
        delta_w, new_m, new_v = {}, {}, {}
        for n in TWIN_WEIGHTS:
            delta_w[n], new_m[n], new_v[n] = _adamw(weights[n], grad_w[n], given["m_" + n], given["v_" + n])
    return (loss, grad_x, *[grad_w[n] for n in TWIN_WEIGHTS], *[delta_w[n] for n in TWIN_WEIGHTS],
            *[new_m[n] for n in TWIN_WEIGHTS], *[new_v[n] for n in TWIN_WEIGHTS])
```

```python
import functools
import math

import jax
import jax.numpy as jnp
from jax import lax
from jax.experimental import pallas as pl
from jax.experimental.pallas import tpu as pltpu

f32 = jnp.float32
bf16 = jnp.bfloat16
SDS = jax.ShapeDtypeStruct
HIGHEST = lax.Precision.HIGHEST
MESH = pl.DeviceIdType.MESH

NN = (((1,), (0,)), ((), ()))
NT = (((1,), (1,)), ((), ()))
TN = (((0,), (0,)), ((), ()))

D_MODEL = 1024
N_GROUPS = 3
DILATIONS = (1, 4, 16)
HEAD_DIM = 64
ATTN_BLOCK = 128
QKV_G = 1536
ATTN_OUT = 512
HGRN_W = 512
HGRN_CHUNK = 32
D_FF = 2816
NUM_BUCKETS = 32
MAX_EXACT = 16
MAX_DISTANCE = 2048
NEG_INF = -1e30
EPS = 1e-6
LANE = 128
SUBLANE = 8
VMEM_BIG = 48 * 1024 * 1024

ADAM_LR, ADAM_B1, ADAM_B2, ADAM_EPS, ADAM_WD, ADAM_STEP = 0.001, 0.9, 0.999, 1e-08, 0.01, 10


def _pick(n, pref):
    t = pref
    while t >= LANE:
        if n % t == 0:
            return t
        t //= 2
    return n


def _cparams(sem=None, vmem=None):
    kw = {}
    if sem is not None:
        kw["dimension_semantics"] = sem
    if vmem is not None:
        kw["vmem_limit_bytes"] = vmem
    return pltpu.CompilerParams(**kw)


def _sigmoid(x):
    return jax.nn.sigmoid(x)


def _colsum8(x):
    return x.reshape(x.shape[0] // SUBLANE, SUBLANE, x.shape[1]).sum(axis=0)


def _mm(a, b, mode, out_dtype, name, acc=None):
    if mode == "nn":
        (M, K), (_, N) = a.shape, b.shape
    elif mode == "nt":
        (M, K), (N, _) = a.shape, b.shape
    else:
        (K, M), (_, N) = a.shape, b.shape
    tm, tn = _pick(M, 512), _pick(N, 512)
    tk = K if K <= 1024 else _pick(K, 512)
    nk = K // tk
    dims = {"nn": NN, "nt": NT, "tn": TN}[mode]
    has_acc = acc is not None

    def body(*refs):
        if has_acc:
            a_ref, b_ref, c_ref = refs[:3]
            rest = refs[3:]
        else:
            a_ref, b_ref = refs[:2]
            c_ref = None
            rest = refs[2:]
        o_ref = rest[0]
        part = lax.dot_general(a_ref[...], b_ref[...], dims, preferred_element_type=f32)
        if nk == 1:
            if has_acc:
                part = part + c_ref[...]
            o_ref[...] = part.astype(out_dtype)
        else:
            acc_ref = rest[1]
            k = pl.program_id(2)

            @pl.when(k == 0)
            def _():
                acc_ref[...] = part

            @pl.when(k > 0)
            def _():
                acc_ref[...] += part

            @pl.when(k == nk - 1)
            def _():
                r = acc_ref[...]
                if has_acc:
                    r = r + c_ref[...]
                o_ref[...] = r.astype(out_dtype)

    if mode == "nn":
        specs = [pl.BlockSpec((tm, tk), lambda i, j, k: (i, k)), pl.BlockSpec((tk, tn), lambda i, j, k: (k, j))]
    elif mode == "nt":
        specs = [pl.BlockSpec((tm, tk), lambda i, j, k: (i, k)), pl.BlockSpec((tn, tk), lambda i, j, k: (j, k))]
    else:
        specs = [pl.BlockSpec((tk, tm), lambda i, j, k: (k, i)), pl.BlockSpec((tk, tn), lambda i, j, k: (k, j))]
    args = [a, b]
    aliases = {}
    if has_acc:
        specs.append(pl.BlockSpec((tm, tn), lambda i, j, k: (i, j)))
        args.append(acc)
        aliases = {2: 0}
    return pl.pallas_call(
        body,
        grid=(M // tm, N // tn, nk),
        in_specs=specs,
        out_specs=pl.BlockSpec((tm, tn), lambda i, j, k: (i, j)),
        out_shape=SDS((M, N), out_dtype),
        scratch_shapes=[] if nk == 1 else [pltpu.VMEM((tm, tn), f32)],
        input_output_aliases=aliases,
        compiler_params=_cparams(("parallel", "parallel", "arbitrary")),
        name=name,
    )(*args)


def _permute(x, d, name):
    if d == 1:
        return x
    S, C = x.shape
    U = S // d

    def body(x_ref, o_ref):
        for r in range(d):
            o_ref[r] = x_ref[pl.ds(r, ATTN_BLOCK, stride=d), :]

    out = pl.pallas_call(
        body,
        grid=(U // ATTN_BLOCK, C // LANE),
        in_specs=[pl.BlockSpec((ATTN_BLOCK * d, LANE), lambda i, j: (i, j))],
        out_specs=pl.BlockSpec((d, ATTN_BLOCK, LANE), lambda i, j: (0, i, j)),
        out_shape=SDS((d, U, C), x.dtype),
        compiler_params=_cparams(("parallel", "parallel")),
        name=name,
    )(x)
    return out.reshape(S, C)


def _unpermute(x, d, name):
    if d == 1:
        return x
    S, C = x.shape
    U = S // d

    def body(x_ref, o_ref):
        for r in range(d):
            o_ref[pl.ds(r, ATTN_BLOCK, stride=d), :] = x_ref[r]

    return pl.pallas_call(
        body,
        grid=(U // ATTN_BLOCK, C // LANE),
        in_specs=[pl.BlockSpec((d, ATTN_BLOCK, LANE), lambda i, j: (0, i, j))],
        out_specs=pl.BlockSpec((ATTN_BLOCK * d, LANE), lambda i, j: (i, j)),
        out_shape=SDS((S, C), x.dtype),
        compiler_params=_cparams(("parallel", "parallel")),
        name=name,
    )(x.reshape(d, U, C))


def _rms_parts(xv):
    r = lax.rsqrt(jnp.mean(xv * xv, axis=-1, keepdims=True) + EPS)
    return r, xv * r


def _rms_bwd(xhat, r, w, dy):
    dyw = dy * w
    return r * (dyw - xhat * jnp.mean(dyw * xhat, axis=-1, keepdims=True))


def _rmsnorm(x, w, name):
    S, D = x.shape
    tm = _pick(S, 512)

    def body(x_ref, w_ref, o_ref):
        _, xh = _rms_parts(x_ref[...])
        o_ref[...] = (xh * w_ref[...]).astype(bf16)

    return pl.pallas_call(
        body,
        grid=(S // tm,),
        in_specs=[pl.BlockSpec((tm, D), lambda i: (i, 0)), pl.BlockSpec((1, D), lambda i: (0, 0))],
        out_specs=pl.BlockSpec((tm, D), lambda i: (i, 0)),
        out_shape=SDS((S, D), bf16),
        compiler_params=_cparams(("parallel",)),
        name=name,
    )(x, w)


def _mid_fwd(x, mo, w_pm, w_pf):
    S, D = x.shape
    tm = _pick(S, 512)

    def body(x_ref, mo_ref, wpm_ref, wpf_ref, x1_ref, h2_ref):
        _, moh = _rms_parts(mo_ref[...])
        x1 = x_ref[...] + moh * wpm_ref[...]
        x1_ref[...] = x1
        _, x1h = _rms_parts(x1)
        h2_ref[...] = (x1h * wpf_ref[...]).astype(bf16)

    row = pl.BlockSpec((tm, D), lambda i: (i, 0))
    vec = pl.BlockSpec((1, D), lambda i: (0, 0))
    return pl.pallas_call(
        body,
        grid=(S // tm,),
        in_specs=[row, row, vec, vec],
        out_specs=[row, row],
        out_shape=[SDS((S, D), f32), SDS((S, D), bf16)],
        compiler_params=_cparams(("parallel",)),
        name="mid_fwd",
    )(x, mo, w_pm, w_pf)


def _final(x1, fo, tgt, w_pfn):
    S, D = x1.shape
    tm = _pick(S, 512)
    nt = S // tm

    def body(x1_ref, fo_ref, t_ref, w_ref, loss_ref, dy_ref, dfo_ref, gw_ref, lacc, gacc):
        i = pl.program_id(0)

        @pl.when(i == 0)
        def _():
            lacc[...] = jnp.zeros_like(lacc)
            gacc[...] = jnp.zeros_like(gacc)

        w = w_ref[...]
        r, foh = _rms_parts(fo_ref[...])
        y = x1_ref[...] + foh * w
        err = y - t_ref[...]
        lacc[...] += _colsum8(err * err)
        dy = err * (1.0 / D)
        dy_ref[...] = dy
        gacc[...] += _colsum8(dy * foh)
        dfo_ref[...] = _rms_bwd(foh, r, w, dy).astype(bf16)

        @pl.when(i == nt - 1)
        def _():
            loss_ref[...] = jnp.full((SUBLANE, LANE), 0.5 / D, f32) * jnp.sum(lacc[...])
            gw_ref[...] = jnp.sum(gacc[...], axis=0, keepdims=True)

    row = pl.BlockSpec((tm, D), lambda i: (i, 0))
    vec = pl.BlockSpec((1, D), lambda i: (0, 0))
    return pl.pallas_call(
        body,
        grid=(nt,),
        in_specs=[row, row, row, vec],
        out_specs=[pl.BlockSpec((SUBLANE, LANE), lambda i: (0, 0)), row, row, vec],
        out_shape=[SDS((SUBLANE, LANE), f32), SDS((S, D), f32), SDS((S, D), bf16), SDS((1, D), f32)],
        scratch_shapes=[pltpu.VMEM((SUBLANE, D), f32), pltpu.VMEM((SUBLANE, D), f32)],
        compiler_params=_cparams(("arbitrary",)),
        name="final_loss",
    )(x1, fo, tgt, w_pfn)


def _mid_bwd(dy, dh2, x1, mo, w_pf, w_pm):
    S, D = dy.shape
    tm = _pick(S, 512)
    nt = S // tm

    def body(dy_ref, dh2_ref, x1_ref, mo_ref, wpf_ref, wpm_ref, dx1_ref, dmo_ref, gpf_ref, gpm_ref, apf, apm):
        i = pl.program_id(0)

        @pl.when(i == 0)
        def _():
            apf[...] = jnp.zeros_like(apf)
            apm[...] = jnp.zeros_like(apm)

        r1, x1h = _rms_parts(x1_ref[...])
        dh2 = dh2_ref[...]
        apf[...] += _colsum8(dh2 * x1h)
        dx1 = dy_ref[...] + _rms_bwd(x1h, r1, wpf_ref[...], dh2)
        dx1_ref[...] = dx1
        rm, moh = _rms_parts(mo_ref[...])
        apm[...] += _colsum8(dx1 * moh)
        dmo_ref[...] = _rms_bwd(moh, rm, wpm_ref[...], dx1).astype(bf16)

        @pl.when(i == nt - 1)
        def _():
            gpf_ref[...] = jnp.sum(apf[...], axis=0, keepdims=True)
            gpm_ref[...] = jnp.sum(apm[...], axis=0, keepdims=True)

    row = pl.BlockSpec((tm, D), lambda i: (i, 0))
    vec = pl.BlockSpec((1, D), lambda i: (0, 0))
    return pl.pallas_call(
        body,
        grid=(nt,),
        in_specs=[row, row, row, row, vec, vec],
        out_specs=[row, row, vec, vec],
        out_shape=[SDS((S, D), f32), SDS((S, D), bf16), SDS((1, D), f32), SDS((1, D), f32)],
        scratch_shapes=[pltpu.VMEM((SUBLANE, D), f32), pltpu.VMEM((SUBLANE, D), f32)],
        compiler_params=_cparams(("arbitrary",)),
        name="mid_bwd",
    )(dy, dh2, x1, mo, w_pf, w_pm)


def _first_bwd(x, dx1, dh_a, dh_b, dh_c, w_pre):
    S, D = x.shape
    tm = _pick(S, 512)
    nt = S // tm

    def body(x_ref, dx1_ref, a_ref, b_ref, c_ref, w_ref, gx_ref, gw_ref, acc):
        i = pl.program_id(0)

        @pl.when(i == 0)
        def _():
            acc[...] = jnp.zeros_like(acc)

        r, xh = _rms_parts(x_ref[...])
        dh = (a_ref[...] + b_ref[...]) + c_ref[...]
        acc[...] += _colsum8(dh * xh)
        gx_ref[...] = dx1_ref[...] + _rms_bwd(xh, r, w_ref[...], dh)

        @pl.when(i == nt - 1)
        def _():
            gw_ref[...] = jnp.sum(acc[...], axis=0, keepdims=True)

    row = pl.BlockSpec((tm, D), lambda i: (i, 0))
    vec = pl.BlockSpec((1, D), lambda i: (0, 0))
    return pl.pallas_call(
        body,
        grid=(nt,),
        in_specs=[row, row, row, row, row, vec],
        out_specs=[row, vec],
        out_shape=[SDS((S, D), f32), SDS((1, D), f32)],
        scratch_shapes=[pltpu.VMEM((SUBLANE, D), f32)],
        compiler_params=_cparams(("arbitrary",)),
        name="first_bwd",
    )(x, dx1, dh_a, dh_b, dh_c, w_pre)


def _t5_bucket(dist):
    n = jnp.maximum(dist, 0)
    nf = jnp.maximum(n, 1).astype(f32)
    large = MAX_EXACT + (jnp.log(nf / MAX_EXACT) / math.log(MAX_DISTANCE / MAX_EXACT)
                         * (NUM_BUCKETS - MAX_EXACT)).astype(jnp.int32)
    large = jnp.minimum(large, NUM_BUCKETS - 1)
    return jnp.where(n < MAX_EXACT, n, large)


def _bias_consts(d):
    blk = ATTN_BLOCK
    rel = jnp.arange(blk)[:, None] + blk - jnp.arange(2 * blk)[None, :]
    in_win = (rel >= 0) & (rel <= blk)
    bucket = _t5_bucket(rel * d).reshape(1, -1)
    onehot = (bucket == jnp.arange(NUM_BUCKETS)[:, None]).astype(f32)
    return onehot, in_win.astype(f32).reshape(1, -1)


def _bias_build(tab_t, onehot, maskf, name):
    H = tab_t.shape[0]

    def body(t_ref, oh_ref, m_ref, o_ref):
        b = jnp.dot(t_ref[...], oh_ref[...], precision=HIGHEST, preferred_element_type=f32)
        o_ref[...] = jnp.where(m_ref[...] > 0.5, b, NEG_INF)

    return pl.pallas_call(body, out_shape=SDS((H, onehot.shape[1]), f32), name=name)(tab_t, onehot, maskf)


def _bias_grad(dbias_flat, onehot, name):
    H = dbias_flat.shape[0]

    def body(g_ref, oh_ref, o_ref):
        o_ref[...] = lax.dot_general(oh_ref[...], g_ref[...], NT, precision=HIGHEST, preferred_element_type=f32)

    return pl.pallas_call(body, out_shape=SDS((NUM_BUCKETS, H), f32), name=name)(dbias_flat, onehot)


def _qkv_specs(nb):
    blk = (ATTN_BLOCK, LANE)
    cur = lambda off: (lambda h, i: (jnp.minimum(i, nb - 1), off + h))
    prev = lambda off: (lambda h, i: (jnp.maximum(jnp.minimum(i, nb - 1) - 1, 0), off + h))
    return [pl.BlockSpec(blk, cur(0)), pl.BlockSpec(blk, prev(4)), pl.BlockSpec(blk, cur(4)),
            pl.BlockSpec(blk, prev(8)), pl.BlockSpec(blk, cur(8))]


def _head_masks():
    lane = lax.broadcasted_iota(jnp.int32, (ATTN_BLOCK, LANE), 1)
    return lane < HEAD_DIM


def _attn_fwd(qkv, bias, bps, name):
    S = qkv.shape[0]
    nb = S // ATTN_BLOCK
    scale = HEAD_DIM ** -0.5

    def body(q_ref, kp_ref, kc_ref, vp_ref, vc_ref, b_ref, o_ref, l_ref):
        i = pl.program_id(1)
        q2 = q_ref[...]
        kk = jnp.concatenate([kp_ref[...], kc_ref[...]], axis=0)
        vv = jnp.concatenate([vp_ref[...], vc_ref[...]], axis=0)
        low = _head_masks()
        col = lax.broadcasted_iota(jnp.int32, (ATTN_BLOCK, 2 * ATTN_BLOCK), 1)
        dead = jnp.logical_and(i % bps == 0, col < ATTN_BLOCK)
        outs, lses = [], []
        for h in range(2):
            hm = low if h == 0 else jnp.logical_not(low)
            qh = jnp.where(hm, q2, jnp.zeros_like(q2))
            s = lax.dot_general(qh, kk, NT, preferred_element_type=f32) * scale + b_ref[h]
            s = jnp.where(dead, NEG_INF, s)
            m = jnp.max(s, axis=-1, keepdims=True)
            p = jnp.exp(s - m)
            l = jnp.sum(p, axis=-1, keepdims=True)
            outs.append(jnp.dot(p.astype(bf16), vv, preferred_element_type=f32) / l)
            lses.append(m + jnp.log(l))
        o_ref[...] = jnp.where(low, outs[0], outs[1])
        l_ref[...] = jnp.where(low, lses[0], lses[1])

    blk = pl.BlockSpec((ATTN_BLOCK, LANE), lambda h, i: (i, h))
    return pl.pallas_call(
        body,
        grid=(4, nb),
        in_specs=_qkv_specs(nb) + [pl.BlockSpec((2, ATTN_BLOCK, 2 * ATTN_BLOCK), lambda h, i: (h, 0, 0))],
        out_specs=[blk, blk],
        out_shape=[SDS((S, ATTN_OUT), f32), SDS((S, ATTN_OUT), f32)],
        compiler_params=_cparams(("parallel", "parallel")),
        name=name,
    )(qkv, qkv, qkv, qkv, qkv, bias)


def _attn_bwd(qkv, bias, do, dvec, lse, bps, name):
    S = qkv.shape[0]
    nb = S // ATTN_BLOCK
    scale = HEAD_DIM ** -0.5

    def body(q_ref, kp_ref, kc_ref, vp_ref, vc_ref, b_ref, do_ref, dvec_ref, lse_ref,
             dq_ref, dk_ref, dv_ref, db_ref, ck, cv):
        i = pl.program_id(1)

        @pl.when(i == 0)
        def _():
            ck[...] = jnp.zeros_like(ck)
            cv[...] = jnp.zeros_like(cv)
            db_ref[...] = jnp.zeros_like(db_ref)

        @pl.when(i < nb)
        def _():
            q2 = q_ref[...]
            kk = jnp.concatenate([kp_ref[...], kc_ref[...]], axis=0)
            vv = jnp.concatenate([vp_ref[...], vc_ref[...]], axis=0)
            do2 = do_ref[...].astype(bf16)
            dvec2 = dvec_ref[...]
            lse2 = lse_ref[...]
            low = _head_masks()
            col = lax.broadcasted_iota(jnp.int32, (ATTN_BLOCK, 2 * ATTN_BLOCK), 1)
            dead = jnp.logical_and(i % bps == 0, col < ATTN_BLOCK)
            low2 = lax.broadcasted_iota(jnp.int32, (2 * ATTN_BLOCK, LANE), 1) < HEAD_DIM
            dqs, dks, dvs = [], [], []
            for h in range(2):
                hm = low if h == 0 else jnp.logical_not(low)
                c0 = h * HEAD_DIM
                qh = jnp.where(hm, q2, jnp.zeros_like(q2))
                doh = jnp.where(hm, do2, jnp.zeros_like(do2))
                s = lax.dot_general(qh, kk, NT, preferred_element_type=f32) * scale + b_ref[h]
                s = jnp.where(dead, NEG_INF, s)
                p = jnp.exp(s - lse2[:, c0:c0 + 1])
                dp = lax.dot_general(doh, vv, NT, preferred_element_type=f32)
                ds = p * (dp - dvec2[:, c0:c0 + 1])
                db_ref[h] += ds
                dsb = ds.astype(bf16)
                dqs.append(jnp.dot(dsb, kk, preferred_element_type=f32) * scale)
                dks.append(lax.dot_general(dsb, q2, TN, preferred_element_type=f32) * scale)
                dvs.append(lax.dot_general(p.astype(bf16), do2, TN, preferred_element_type=f32))
            dq_ref[...] = jnp.where(low, dqs[0], dqs[1]).astype(bf16)
            dk = jnp.where(low2, dks[0], dks[1])
            dv = jnp.where(low2, dvs[0], dvs[1])
            dk_ref[...] = (ck[...] + dk[:ATTN_BLOCK]).astype(bf16)
            dv_ref[...] = (cv[...] + dv[:ATTN_BLOCK]).astype(bf16)
            ck[...] = dk[ATTN_BLOCK:]
            cv[...] = dv[ATTN_BLOCK:]

        @pl.when(i == nb)
        def _():
            dk_ref[...] = ck[...].astype(bf16)
            dv_ref[...] = cv[...].astype(bf16)

    blk = (ATTN_BLOCK, LANE)
    cur = pl.BlockSpec(blk, lambda h, i: (jnp.minimum(i, nb - 1), h))
    lag = pl.BlockSpec(blk, lambda h, i: (jnp.maximum(i - 1, 0), h))
    bspec = pl.BlockSpec((2, ATTN_BLOCK, 2 * ATTN_BLOCK), lambda h, i: (h, 0, 0))
    return pl.pallas_call(
        body,
        grid=(4, nb + 1),
        in_specs=_qkv_specs(nb) + [bspec, cur, cur, cur],
        out_specs=[cur, lag, lag, bspec],
        out_shape=[SDS((S, ATTN_OUT), bf16), SDS((S, ATTN_OUT), bf16), SDS((S, ATTN_OUT), bf16),
                   SDS((8, ATTN_BLOCK, 2 * ATTN_BLOCK), f32)],
        scratch_shapes=[pltpu.VMEM(blk, f32), pltpu.VMEM(blk, f32)],
        compiler_params=_cparams(("parallel", "arbitrary")),
        name=name,
    )(qkv, qkv, qkv, qkv, qkv, bias, do, dvec, lse)


def _attn_merge(o0, o1, o2, l0, l1, l2):
    S, W = o0.shape
    tm = _pick(S, 512)

    def body(o0_ref, o1_ref, o2_ref, l0_ref, l1_ref, l2_ref, y_ref, yb_ref, w0_ref, w1_ref, w2_ref):
        a, b, c = l0_ref[...], l1_ref[...], l2_ref[...]
        m = jnp.maximum(jnp.maximum(a, b), c)
        ea, eb, ec = jnp.exp(a - m), jnp.exp(b - m), jnp.exp(c - m)
        den = (ea + eb) + ec
        w0, w1, w2 = ea / den, eb / den, ec / den
        y = (w0 * o0_ref[...] + w1 * o1_ref[...]) + w2 * o2_ref[...]
        y_ref[...] = y
        yb_ref[...] = y.astype(bf16)
        w0_ref[...] = w0
        w1_ref[...] = w1
        w2_ref[...] = w2

    row = pl.BlockSpec((tm, W), lambda i: (i, 0))
    return pl.pallas_call(
        body,
        grid=(S // tm,),
        in_specs=[row] * 6,
        out_specs=[row] * 5,
        out_shape=[SDS((S, W), f32), SDS((S, W), bf16)] + [SDS((S, W), f32)] * 3,
        compiler_params=_cparams(("parallel",)),
        name="attn_merge",
    )(o0, o1, o2, l0, l1, l2)


def _attn_merge_bwd(dy, y, w0, w1, w2):
    S, W = dy.shape
    tm = _pick(S, 512)

    def body(dy_ref, y_ref, w0_ref, w1_ref, w2_ref, a0, a1, a2, b0, b1, b2):
        dyv = dy_ref[...]
        r = lax.broadcasted_iota(jnp.int32, (LANE, LANE), 0) // HEAD_DIM
        c = lax.broadcasted_iota(jnp.int32, (LANE, LANE), 1) // HEAD_DIM
        seg = jnp.where(r == c, 1.0, 0.0).astype(f32)
        cbar = jnp.dot(dyv * y_ref[...], seg, precision=HIGHEST, preferred_element_type=f32)
        for w_ref, a_ref, b_ref in ((w0_ref, a0, b0), (w1_ref, a1, b1), (w2_ref, a2, b2)):
            w = w_ref[...]
            a_ref[...] = w * dyv
            b_ref[...] = w * cbar

    blk = pl.BlockSpec((tm, LANE), lambda i, j: (i, j))
    return pl.pallas_call(
        body,
        grid=(S // tm, W // LANE),
        in_specs=[blk] * 5,
        out_specs=[blk] * 6,
        out_shape=[SDS((S, W), f32)] * 6,
        compiler_params=_cparams(("parallel", "parallel")),
        name="attn_merge_bwd",
    )(dy, y, w0, w1, w2)


HGRN_SB = 256


def _chunk_masks(sb):
    r = lax.broadcasted_iota(jnp.int32, (sb, sb), 0)
    c = lax.broadcasted_iota(jnp.int32, (sb, sb), 1)
    same = (r // HGRN_CHUNK) == (c // HGRN_CHUNK)
    return same, jnp.logical_and(same, c <= r), jnp.logical_and(same, c >= r)


def _hgrn_prep(q_raw, f_raw, lbv, same, tril):
    sq = _sigmoid(q_raw)
    qs = q_raw * sq
    sig = _sigmoid(f_raw)
    f = lbv + (1.0 - lbv) * sig
    g = jnp.log(f)
    k = 1.0 - f
    G = jnp.dot(jnp.where(tril, 1.0, 0.0).astype(f32), g, precision=HIGHEST, preferred_element_type=f32)
    GL = jnp.dot(jnp.where(same, 1.0, 0.0).astype(f32), g, precision=HIGHEST, preferred_element_type=f32)
    eG = jnp.exp(G)
    einv = jnp.exp(-G)
    edec = jnp.exp(GL - G)
    return dict(sq=sq, qs=qs, sig=sig, f=f, k=k, eG=eG, einv=einv, edec=edec, eGL=jnp.exp(GL),
                qt=qs * eG, kt=k * einv, kd=k * edec)


def _hgrn_fwd(hg, lb, normw):
    S = hg.shape[0]
    sb = HGRN_SB
    nsb = S // sb
    nch = sb // HGRN_CHUNK

    def body(q_ref, f_ref, v_ref, og_ref, lb_ref, nw_ref, y_ref, o_ref, ck_ref, st):
        j = pl.program_id(1)

        @pl.when(j == 0)
        def _():
            st[...] = jnp.zeros_like(st)

        ST = st[...]
        ck_ref[0, 0] = ST
        same, tril, _ = _chunk_masks(sb)
        pr = _hgrn_prep(q_ref[...], f_ref[...], lb_ref[...], same, tril)
        qtb, ktb, kdb = pr["qt"].astype(bf16), pr["kt"].astype(bf16), pr["kd"].astype(bf16)
        eGL = pr["eGL"]
        vb = v_ref[...].astype(bf16)
        A = jnp.where(tril, lax.dot_general(qtb, ktb, NT, preferred_element_type=f32), 0.0)
        o = jnp.dot(A.astype(bf16), vb, preferred_element_type=f32)
        outs = []
        for ci in range(nch):
            lo = ci * HGRN_CHUNK
            sl = slice(lo, lo + HGRN_CHUNK)
            outs.append(o[sl] + lax.dot_general(qtb[sl], ST.astype(bf16), NT, preferred_element_type=f32))
            ST = ST * eGL[lo:lo + 1, :] + lax.dot_general(vb[sl], kdb[sl], TN, preferred_element_type=f32)
        st[...] = ST
        of = jnp.concatenate(outs, axis=0)
        o_ref[...] = of
        rms = lax.rsqrt(jnp.mean(of * of, axis=-1, keepdims=True) + EPS)
        ogv = og_ref[...]
        y_ref[...] = ((of * rms * nw_ref[...]) * (ogv * _sigmoid(ogv))).astype(bf16)

    col = lambda off: pl.BlockSpec((sb, LANE), lambda h, j: (j, off + h))
    return pl.pallas_call(
        body,
        grid=(4, nsb),
        in_specs=[col(0), col(4), col(8), col(12), pl.BlockSpec((1, LANE), lambda h, j: (0, h)),
                  pl.BlockSpec((1, LANE), lambda h, j: (0, 0))],
        out_specs=[col(0), col(0), pl.BlockSpec((1, 1, LANE, LANE), lambda h, j: (h, j, 0, 0))],
        out_shape=[SDS((S, HGRN_W), bf16), SDS((S, HGRN_W), f32), SDS((4, nsb, LANE, LANE), f32)],
        scratch_shapes=[pltpu.VMEM((LANE, LANE), f32)],
        compiler_params=_cparams(("parallel", "arbitrary")),
        name="hgrn_fwd",
    )(hg, hg, hg, hg, lb, normw)


def _hgrn_bwd(hg, o_raw, dy, ck, lb, normw):
    S = hg.shape[0]
    sb = HGRN_SB
    nsb = S // sb
    nch = sb // HGRN_CHUNK

    def body(q_ref, f_ref, v_ref, og_ref, o_ref, dy_ref, ck_ref, lb_ref, nw_ref,
             dq_ref, df_ref, dv_ref, dog_ref, glb_ref, gnw_ref, dst, alb, anw):
        j = pl.program_id(1)

        @pl.when(j == 0)
        def _():
            dst[...] = jnp.zeros_like(dst)
            alb[...] = jnp.zeros_like(alb)
            anw[...] = jnp.zeros_like(anw)

        same, tril, triu = _chunk_masks(sb)
        lbv = lb_ref[...]
        q_raw = q_ref[...]
        pr = _hgrn_prep(q_raw, f_ref[...], lbv, same, tril)
        qt, kt, kd, eGL = pr["qt"], pr["kt"], pr["kd"], pr["eGL"]
        qtb, ktb, kdb = qt.astype(bf16), kt.astype(bf16), kd.astype(bf16)
        vb = v_ref[...].astype(bf16)

        o = o_ref[...]
        ogv = og_ref[...]
        sog = _sigmoid(ogv)
        rms = lax.rsqrt(jnp.mean(o * o, axis=-1, keepdims=True) + EPS)
        oh = o * rms
        nw = nw_ref[...]
        dyv = dy_ref[...]
        dog_ref[...] = (dyv * (oh * nw) * (sog * (1.0 + ogv * (1.0 - sog)))).astype(bf16)
        dohw = dyv * (ogv * sog)
        anw[...] += _colsum8(dohw * oh)
        doh = dohw * nw
        do = rms * (doh - oh * jnp.mean(doh * oh, axis=-1, keepdims=True))
        dob = do.astype(bf16)

        Ab = jnp.where(tril, lax.dot_general(qtb, ktb, NT, preferred_element_type=f32), 0.0).astype(bf16)
        dAb = jnp.where(tril, lax.dot_general(dob, vb, NT, preferred_element_type=f32), 0.0).astype(bf16)
        dv_acc = lax.dot_general(Ab, dob, TN, preferred_element_type=f32)
        dqt = jnp.dot(dAb, ktb, preferred_element_type=f32)
        dkt = lax.dot_general(dAb, qtb, TN, preferred_element_type=f32)

        ST = ck_ref[0, 0]
        states = []
        for ci in range(nch):
            lo = ci * HGRN_CHUNK
            sl = slice(lo, lo + HGRN_CHUNK)
            states.append(ST)
            ST = ST * eGL[lo:lo + 1, :] + lax.dot_general(vb[sl], kdb[sl], TN, preferred_element_type=f32)

        dST = dst[...]
        dqt_i, dkd_i, dv_i, deg_i = [None] * nch, [None] * nch, [None] * nch, [None] * nch
        for ci in reversed(range(nch)):
            lo = ci * HGRN_CHUNK
            sl = slice(lo, lo + HGRN_CHUNK)
            ST0 = states[ci]
            dSTb = dST.astype(bf16)
            dv_i[ci] = lax.dot_general(kdb[sl], dSTb, NT, preferred_element_type=f32)
            dqt_i[ci] = jnp.dot(dob[sl], ST0.astype(bf16), preferred_element_type=f32)
            dkd_i[ci] = jnp.dot(vb[sl], dSTb, preferred_element_type=f32)
            deg_i[ci] = jnp.broadcast_to(jnp.sum(dST * ST0, axis=0, keepdims=True), (HGRN_CHUNK, LANE))
            dST = dST * eGL[lo:lo + 1, :] + lax.dot_general(dob[sl], qtb[sl], TN, preferred_element_type=f32)
        dst[...] = dST

        dqt = dqt + jnp.concatenate(dqt_i, axis=0)
        dkd = jnp.concatenate(dkd_i, axis=0)
        dv_ref[...] = (dv_acc + jnp.concatenate(dv_i, axis=0)).astype(bf16)
        deg = jnp.concatenate(deg_i, axis=0)

        dqs = dqt * pr["eG"]
        dkdkd = dkd * kd
        dG = dqt * qt - dkt * kt - dkdkd
        dk = dkt * pr["einv"] + dkd * pr["edec"]
        dGL = jnp.dot(jnp.where(same, 1.0, 0.0).astype(f32), dkdkd, precision=HIGHEST,
                      preferred_element_type=f32) + eGL * deg
        dg = jnp.dot(jnp.where(triu, 1.0, 0.0).astype(f32), dG, precision=HIGHEST,
                     preferred_element_type=f32) + dGL
        df = dg / pr["f"] - dk
        sig = pr["sig"]
        df_ref[...] = (df * (1.0 - lbv) * (sig * (1.0 - sig))).astype(bf16)
        alb[...] += _colsum8(df * (1.0 - sig))
        sq = pr["sq"]
        dq_ref[...] = (dqs * (sq * (1.0 + q_raw * (1.0 - sq)))).astype(bf16)

        @pl.when(j == nsb - 1)
        def _():
            glb_ref[...] = jnp.broadcast_to(jnp.sum(alb[...], axis=0, keepdims=True), (SUBLANE, LANE))
            gnw_ref[...] = jnp.broadcast_to(jnp.sum(anw[...], axis=0, keepdims=True), (SUBLANE, LANE))

    rev = lambda off: pl.BlockSpec((sb, LANE), lambda h, j: (nsb - 1 - j, off + h))
    stat = pl.BlockSpec((SUBLANE, LANE), lambda h, j: (0, h))
    return pl.pallas_call(
        body,
        grid=(4, nsb),
        in_specs=[rev(0), rev(4), rev(8), rev(12), rev(0), rev(0),
                  pl.BlockSpec((1, 1, LANE, LANE), lambda h, j: (h, nsb - 1 - j, 0, 0)),
                  pl.BlockSpec((1, LANE), lambda h, j: (0, h)), pl.BlockSpec((1, LANE), lambda h, j: (0, 0))],
        out_specs=[rev(0), rev(0), rev(0), rev(0), stat, stat],
        out_shape=[SDS((S, HGRN_W), bf16)] * 4 + [SDS((SUBLANE, HGRN_W), f32)] * 2,
        scratch_shapes=[pltpu.VMEM((LANE, LANE), f32), pltpu.VMEM((SUBLANE, LANE), f32),
                        pltpu.VMEM((SUBLANE, LANE), f32)],
        compiler_params=_cparams(("parallel", "arbitrary")),
        name="hgrn_bwd",
    )(hg, hg, hg, hg, o_raw, dy, ck, lb, normw)


def _lb_fwd(raw):
    def body(r_ref, o_ref):
        r = r_ref[...]
        m = jnp.max(r, axis=0, keepdims=True)
        e = jnp.exp(r - m)
        o_ref[...] = (e / jnp.sum(e, axis=0, keepdims=True))[0:1]

    return pl.pallas_call(body, out_shape=SDS((1, raw.shape[1]), f32), name="lb_fwd")(raw)


def _lb_bwd(raw, dlb):
    def body(r_ref, d_ref, o_ref):
        r = r_ref[...]
        m = jnp.max(r, axis=0, keepdims=True)
        e = jnp.exp(r - m)
        s = e / jnp.sum(e, axis=0, keepdims=True)
        s0 = s[0:1]
        onehot0 = jnp.where(lax.broadcasted_iota(jnp.int32, r.shape, 0) == 0, 1.0, 0.0)
        o_ref[...] = d_ref[...] * s0 * (onehot0 - s)

    return pl.pallas_call(body, out_shape=SDS(raw.shape, f32), name="lb_bwd")(raw, dlb)


def _gate_fwd(a, b, gc):
    S, D = a.shape
    tm = _pick(S, 512)

    def body(a_ref, b_ref, g0_ref, g1_ref, o_ref):
        o_ref[...] = (_sigmoid(g0_ref[...]) * a_ref[...] + _sigmoid(g1_ref[...]) * b_ref[...]).astype(bf16)

    row = pl.BlockSpec((tm, D), lambda i: (i, 0))
    return pl.pallas_call(
        body,
        grid=(S // tm,),
        in_specs=[row, row, row, pl.BlockSpec((tm, D), lambda i: (i, 1))],
        out_specs=row,
        out_shape=SDS((S, D), bf16),
        compiler_params=_cparams(("parallel",)),
        name="gate_fwd",
    )(a, b, gc, gc)


def _gate_bwd(dm, a, b, gc):
    S, D = a.shape
    tm = _pick(S, 512)

    def body(dm_ref, a_ref, b_ref, g0_ref, g1_ref, da_ref, db_ref, dg_ref):
        dmv = dm_ref[...]
        s0, s1 = _sigmoid(g0_ref[...]), _sigmoid(g1_ref[...])
        da_ref[...] = (dmv * s0).astype(bf16)
        db_ref[...] = (dmv * s1).astype(bf16)
        dg_ref[:, :D] = (dmv * a_ref[...] * (s0 * (1.0 - s0))).astype(bf16)
        dg_ref[:, D:] = (dmv * b_ref[...] * (s1 * (1.0 - s1))).astype(bf16)

    row = pl.BlockSpec((tm, D), lambda i: (i, 0))
    wide = pl.BlockSpec((tm, 2 * D), lambda i: (i, 0))
    return pl.pallas_call(
        body,
        grid=(S // tm,),
        in_specs=[row, row, row, row, pl.BlockSpec((tm, D), lambda i: (i, 1))],
        out_specs=[row, row, wide],
        out_shape=[SDS((S, D), bf16), SDS((S, D), bf16), SDS((S, 2 * D), bf16)],
        compiler_params=_cparams(("parallel",)),
        name="gate_bwd",
    )(dm, a, b, gc, gc)


CONV_ROWS = 512
INV_SQRT2 = 0.7071067811865476
INV_SQRT_2PI = 0.3989422804014327


def _tile8(a, rows):
    return jnp.tile(a, (rows // SUBLANE, 1))


def _conv_rows(u_ref, w, b, r0, first):
    R = CONV_ROWS
    cur = u_ref[pl.ds(r0, R), :]
    prev8 = u_ref[pl.ds(pl.multiple_of(jnp.maximum(r0 - SUBLANE, 0), SUBLANE), SUBLANE), :]
    prev8 = jnp.where(first, 0.0, prev8)
    row = lax.broadcasted_iota(jnp.int32, (R, LANE), 0)
    x1 = jnp.where(row < 1, _tile8(pltpu.roll(prev8, 1, 0), R), pltpu.roll(cur, 1, 0))
    x2 = jnp.where(row < 2, _tile8(pltpu.roll(prev8, 2, 0), R), pltpu.roll(cur, 2, 0))
    c = ((b + w[0:1] * x2) + w[1:2] * x1) + w[2:3] * cur
    return c, x2, x1, cur


def _conv_fwd(ug, uv, wg, wv, bg, bv):
    S, F = ug.shape
    nchunk = S // CONV_ROWS

    def body(ug_ref, uv_ref, wg_ref, wv_ref, bg_ref, bv_ref, o_ref):
        wgv, wvv, bgv, bvv = wg_ref[...], wv_ref[...], bg_ref[...], bv_ref[...]

        def step(ci, carry):
            r0 = pl.multiple_of(ci * CONV_ROWS, CONV_ROWS)
            cg = _conv_rows(ug_ref, wgv, bgv, r0, ci == 0)[0]
            cv = _conv_rows(uv_ref, wvv, bvv, r0, ci == 0)[0]
            gelu = 0.5 * cg * (1.0 + lax.erf(cg * INV_SQRT2))
            o_ref[pl.ds(r0, CONV_ROWS), :] = (gelu * cv).astype(bf16)
            return carry

        lax.fori_loop(0, nchunk, step, 0)

    col = pl.BlockSpec((S, LANE), lambda j: (0, j))
    w3 = pl.BlockSpec((3, LANE), lambda j: (0, j))
    b1 = pl.BlockSpec((1, LANE), lambda j: (0, j))
    return pl.pallas_call(
        body,
        grid=(F // LANE,),
        in_specs=[col, col, w3, w3, b1, b1],
        out_specs=col,
        out_shape=SDS((S, F), bf16),
        compiler_params=_cparams(("parallel",), VMEM_BIG),
        name="conv_fwd",
    )(ug, uv, wg, wv, bg, bv)


def _conv_bwd(ug, uv, dact, wg, wv, bg, bv):
    S, F = ug.shape
    R = CONV_ROWS
    nchunk = S // R

    def body(ug_ref, uv_ref, da_ref, wg_ref, wv_ref, bg_ref, bv_ref, dug_ref, duv_ref, sg_ref, sv_ref, dcg, dcv):
        wgv, wvv, bgv, bvv = wg_ref[...], wv_ref[...], bg_ref[...], bv_ref[...]
        zero = jnp.zeros((SUBLANE, LANE), f32)

        def fwd_step(ci, acc):
            r0 = pl.multiple_of(ci * R, R)
            cg, g2, g1, g0 = _conv_rows(ug_ref, wgv, bgv, r0, ci == 0)
            cv, v2, v1, v0 = _conv_rows(uv_ref, wvv, bvv, r0, ci == 0)
            da = da_ref[pl.ds(r0, R), :]
            cdf = 0.5 * (1.0 + lax.erf(cg * INV_SQRT2))
            pdf = INV_SQRT_2PI * jnp.exp(-0.5 * cg * cg)
            dg = da * cv * (cdf + cg * pdf)
            dv = da * (cg * cdf)
            dcg[pl.ds(r0, R), :] = dg
            dcv[pl.ds(r0, R), :] = dv
            new = (acc[0] + _colsum8(dg * g2), acc[1] + _colsum8(dg * g1), acc[2] + _colsum8(dg * g0),
                   acc[3] + _colsum8(dg),
                   acc[4] + _colsum8(dv * v2), acc[5] + _colsum8(dv * v1), acc[6] + _colsum8(dv * v0),
                   acc[7] + _colsum8(dv))
            return new

        acc = lax.fori_loop(0, nchunk, fwd_step, (zero,) * 8)
        rows = lax.broadcasted_iota(jnp.int32, (SUBLANE, LANE), 0)

        def stats(parts):
            out = jnp.zeros((SUBLANE, LANE), f32)
            for k, pt in enumerate(parts):
                out = jnp.where(rows == k, jnp.sum(pt, axis=0, keepdims=True), out)
            return out

        sg_ref[...] = stats(acc[0:4])
        sv_ref[...] = stats(acc[4:8])

        def du_rows(dc, w, r0, last):
            cur = dc[pl.ds(r0, R), :]
            nxt = dc[pl.ds(pl.multiple_of(jnp.minimum(r0 + R, S - SUBLANE), SUBLANE), SUBLANE), :]
            nxt = jnp.where(last, 0.0, nxt)
            row = lax.broadcasted_iota(jnp.int32, (R, LANE), 0)
            y1 = jnp.where(row >= R - 1, _tile8(pltpu.roll(nxt, SUBLANE - 1, 0), R), pltpu.roll(cur, R - 1, 0))
            y2 = jnp.where(row >= R - 2, _tile8(pltpu.roll(nxt, SUBLANE - 2, 0), R), pltpu.roll(cur, R - 2, 0))
            return w[2:3] * cur + w[1:2] * y1 + w[0:1] * y2

        def bwd_step(ci, carry):
            r0 = pl.multiple_of(ci * R, R)
            last = ci == nchunk - 1
            dug_ref[pl.ds(r0, R), :] = du_rows(dcg, wgv, r0, last).astype(bf16)
            duv_ref[pl.ds(r0, R), :] = du_rows(dcv, wvv, r0, last).astype(bf16)
            return carry

        lax.fori_loop(0, nchunk, bwd_step, 0)

    col = pl.BlockSpec((S, LANE), lambda j: (0, j))
    w3 = pl.BlockSpec((3, LANE), lambda j: (0, j))
    b1 = pl.BlockSpec((1, LANE), lambda j: (0, j))
    st = pl.BlockSpec((SUBLANE, LANE), lambda j: (0, j))
    return pl.pallas_call(
        body,
        grid=(F // LANE,),
        in_specs=[col, col, col, w3, w3, b1, b1],
        out_specs=[col, col, st, st],
        out_shape=[SDS((S, F), bf16), SDS((S, F), bf16), SDS((SUBLANE, F), f32), SDS((SUBLANE, F), f32)],
        scratch_shapes=[pltpu.VMEM((S, LANE), f32), pltpu.VMEM((S, LANE), f32)],
        compiler_params=_cparams(("parallel",), VMEM_BIG),
        name="conv_bwd",
    )(ug, uv, dact, wg, wv, bg, bv)


def _adam_math(w, g, m, v):
    m = ADAM_B1 * m + (1.0 - ADAM_B1) * g
    v = ADAM_B2 * v + (1.0 - ADAM_B2) * (g * g)
    m_hat = m / (1.0 - ADAM_B1 ** ADAM_STEP)
    v_hat = v / (1.0 - ADAM_B2 ** ADAM_STEP)
    delta = -ADAM_LR * (m_hat / (jnp.sqrt(v_hat) + ADAM_EPS) + ADAM_WD * w)
    return delta, m, v


def _adamw(w, m, v, g, name):
    R, C = w.shape
    parts = g.ndim == 3
    tr = R
    for t in (256, 128, 64, 32, 16):
        if R % t == 0 and R > t:
            tr = t
            break

    def body(w_ref, m_ref, v_ref, g_ref, go_ref, d_ref, mo_ref, vo_ref):
        if parts:
            gv = ((g_ref[0].astype(f32) + g_ref[1].astype(f32)) + g_ref[2].astype(f32)) + g_ref[3].astype(f32)
        else:
            gv = g_ref[...]
        go_ref[...] = gv
        d, mn, vn = _adam_math(w_ref[...], gv, m_ref[...], v_ref[...])
        d_ref[...] = d
        mo_ref[...] = mn
        vo_ref[...] = vn

    row = pl.BlockSpec((tr, C), lambda i: (i, 0))
    gspec = pl.BlockSpec((4, tr, C), lambda i: (0, i, 0)) if parts else row
    return pl.pallas_call(
        body,
        grid=(R // tr,),
        in_specs=[row, row, row, gspec],
        out_specs=[row] * 4,
        out_shape=[SDS((R, C), f32)] * 4,
        compiler_params=_cparams(("parallel",)),
        name=name,
    )(w, m, v, g)


def _sum8(parts, name):
    _, _, R, C = parts.shape

    def body(p_ref, o_ref):
        acc = p_ref[0, 0]
        for c in range(2):
            for k in range(4):
                if c or k:
                    acc = acc + p_ref[c, k]
        o_ref[...] = acc

    return pl.pallas_call(body, out_shape=SDS((R, C), f32), name=name)(parts)


def _pair_add(a, b, name):
    K, R, C = a.shape
    tr = 480 if R % 480 == 0 else R

    def body(a_ref, b_ref, o_ref):
        o_ref[...] = (a_ref[...].astype(f32) + b_ref[...].astype(f32)).astype(bf16)

    blk = pl.BlockSpec((1, tr, C), lambda k, i: (k, i, 0))
    return pl.pallas_call(
        body,
        grid=(K, R // tr),
        in_specs=[blk, blk],
        out_specs=blk,
        out_shape=SDS((K, R, C), bf16),
        compiler_params=_cparams(("parallel", "parallel")),
        name=name,
    )(a, b)


_ANY = pl.BlockSpec(memory_space=pl.ANY)


def _chip_comm(src, gather, name):
    blk_shape = src.shape if gather else src.shape[1:]

    def body(src_ref, out_ref, send_sems, recv_sems, local_sem):
        x, y, c = lax.axis_index("x"), lax.axis_index("y"), lax.axis_index("c")
        mine = 2 * x + y
        peers = [(1 - x, y), (x, 1 - y), (1 - x, 1 - y)]

        def piece(k):
            return src_ref if gather else src_ref.at[k]

        local = pltpu.make_async_copy(piece(mine), out_ref.at[mine], local_sem)
        local.start()
        sends = []
        for j, (px, py) in enumerate(peers):
            cp = pltpu.make_async_remote_copy(
                src_ref=piece(2 * px + py), dst_ref=out_ref.at[mine], send_sem=send_sems.at[j],
                recv_sem=recv_sems.at[j], device_id=(px, py, c), device_id_type=MESH)
            cp.start()
            sends.append(cp)
        for j, (px, py) in enumerate(peers):
            pltpu.make_async_remote_copy(
                src_ref=piece(mine), dst_ref=out_ref.at[2 * px + py], send_sem=send_sems.at[j],
                recv_sem=recv_sems.at[j], device_id=(px, py, c), device_id_type=MESH).wait_recv()
        for cp in sends:
            cp.wait_send()
        local.wait()

    return pl.pallas_call(
        body,
        in_specs=[_ANY],
        out_specs=_ANY,
        out_shape=SDS((4,) + tuple(blk_shape), src.dtype),
        scratch_shapes=[pltpu.SemaphoreType.DMA((3,)), pltpu.SemaphoreType.DMA((3,)), pltpu.SemaphoreType.DMA],
        name=name,
    )(src)


def _core_gather(src, name):
    def body(src_ref, out_ref, send_sem, recv_sem, local_sem):
        x, y, c = lax.axis_index("x"), lax.axis_index("y"), lax.axis_index("c")
        local = pltpu.make_async_copy(src_ref, out_ref.at[c], local_sem)
        local.start()
        cp = pltpu.make_async_remote_copy(src_ref=src_ref, dst_ref=out_ref.at[c], send_sem=send_sem,
                                          recv_sem=recv_sem, device_id=(x, y, 1 - c), device_id_type=MESH)
        cp.start()
        pltpu.make_async_remote_copy(src_ref=src_ref, dst_ref=out_ref.at[1 - c], send_sem=send_sem,
                                     recv_sem=recv_sem, device_id=(x, y, 1 - c), device_id_type=MESH).wait_recv()
        cp.wait_send()
        local.wait()

    return pl.pallas_call(
        body,
        in_specs=[_ANY],
        out_specs=_ANY,
        out_shape=SDS((2,) + tuple(src.shape), src.dtype),
        scratch_shapes=[pltpu.SemaphoreType.DMA, pltpu.SemaphoreType.DMA, pltpu.SemaphoreType.DMA],
        name=name,
    )(src)


def _core_swap(src, name):
    def body(src_ref, out_ref, send_sem, recv_sem):
        x, y, c = lax.axis_index("x"), lax.axis_index("y"), lax.axis_index("c")
        cp = pltpu.make_async_remote_copy(src_ref=src_ref.at[1 - c], dst_ref=out_ref, send_sem=send_sem,
                                          recv_sem=recv_sem, device_id=(x, y, 1 - c), device_id_type=MESH)
        cp.start()
        cp.wait()

    return pl.pallas_call(
        body,
        in_specs=[_ANY],
        out_specs=_ANY,
        out_shape=SDS(tuple(src.shape[1:]), src.dtype),
        scratch_shapes=[pltpu.SemaphoreType.DMA, pltpu.SemaphoreType.DMA],
        name=name,
    )(src)


def _all_gather(src, tag):
    by_chip = _chip_comm(src, True, tag + "_chips")
    both = _core_gather(by_chip, tag + "_cores")
    return jnp.swapaxes(both, 0, 1).reshape((8,) + tuple(src.shape))


def _pack_rows(w_in, w_ba, w_bh, w_out, w_up, w_down):
    n = w_in.shape[0]
    return jnp.concatenate([t.reshape(n, -1, D_MODEL) for t in (w_in, w_ba, w_bh, w_out, w_up, w_down)], axis=1)


_PACK_SIZES = (("w_in", (1024, 1088)), ("w_ba", (512, 128)), ("w_bh", (512, 128)), ("w_out", (128, 1024)),
               ("w_up", (1024, 704)), ("w_down", (352, 1024)))


def _unpack_rows(slab):
    n = slab.shape[0]
    out, lo = {}, 0
    for key, (r, c) in _PACK_SIZES:
        rows = r * c // D_MODEL
        out[key] = slab[:, lo:lo + rows].reshape(n, r, c)
        lo += rows
    return out


def _cols_to_full(t):
    return jnp.swapaxes(t, 0, 1).reshape(t.shape[1], -1)


def _full_to_cols(t):
    K = t.shape[0]
    return jnp.swapaxes(t.reshape(K, 8, -1), 0, 1)


_SMALL = (("pre_mix_norm", (1, 1024)), ("rel_bias", (32, 24)), ("hgrn_lb_raw", (2, 512)), ("hgrn_norm", (1, 128)),
          ("post_mix_norm", (1, 1024)), ("pre_ffn_norm", (1, 1024)), ("conv_b", (1, 5632)),
          ("post_ffn_norm", (1, 1024)))
_SMALL_ROWS = 96
_CONVW_ROWS = 136


def _pack_small(d):
    flat = jnp.concatenate([d[k].reshape(-1) for k, _ in _SMALL])
    flat = jnp.pad(flat, (0, _SMALL_ROWS * LANE - flat.shape[0]))
    return flat.reshape(_SMALL_ROWS, LANE)


def _unpack_small(p):
    flat = p.reshape(-1)
    out, lo = {}, 0
    for k, shp in _SMALL:
        n = shp[0] * shp[1]
        out[k] = flat[lo:lo + n].reshape(shp)
        lo += n
    return out


def _local_step(x, tgt, W, P):
    S = x.shape[0]
    lb = _lb_fwd(P["hgrn_lb_raw"])
    xs = [x, _permute(x, 4, "perm_x4"), _permute(x, 16, "perm_x16")]
    hs = [_rmsnorm(xs[g], P["pre_mix_norm"], f"norm_pre{g}") for g in range(N_GROUPS)]
    h1 = hs[0]
    consts = [_bias_consts(d) for d in DILATIONS]
    qkv, obuf, lbuf, biases = [], [], [], []
    for g, d in enumerate(DILATIONS):
        qkv_g = _mm(hs[g], W["w_qkv"][g], "nn", bf16, f"proj_qkv{g}")
        tab_t = P["rel_bias"][:, 8 * g:8 * g + 8].T
        bias_g = _bias_build(tab_t, consts[g][0], consts[g][1], f"bias_build{g}").reshape(8, ATTN_BLOCK, 2 * ATTN_BLOCK)
        o_g, l_g = _attn_fwd(qkv_g, bias_g, (S // d) // ATTN_BLOCK, f"attn_fwd{g}")
        qkv.append(qkv_g)
        biases.append(bias_g)
        lbuf.append(l_g)
        obuf.append(_unpermute(o_g, d, f"unperm_o{g}"))
    l_nat = [_unpermute(lbuf[g], d, f"unperm_l{g}") for g, d in enumerate(DILATIONS)]
    y_attn, y_attn_b, w0, w1, w2 = _attn_merge(obuf[0], obuf[1], obuf[2], l_nat[0], l_nat[1], l_nat[2])
    hg = _mm(h1, W["w_hg"], "nn", f32, "proj_hg")
    gc = _mm(h1, W["w_gate"], "nn", f32, "proj_gate")
    y_hgrn, o_raw, ck = _hgrn_fwd(hg, lb, P["hgrn_norm"])
    a = _mm(y_attn_b, W["w_ba"], "nn", f32, "branch_attn")
    b = _mm(y_hgrn, W["w_bh"], "nn", f32, "branch_hgrn")
    merged = _gate_fwd(a, b, gc)
    mo = _mm(merged, W["w_out"], "nn", f32, "out_proj")
    x1, h2 = _mid_fwd(x, mo, P["post_mix_norm"], P["pre_ffn_norm"])
    ug = _mm(h2, W["w_up_g"], "nn", f32, "up_gate")
    uv = _mm(h2, W["w_up_v"], "nn", f32, "up_val")
    cw_g, cw_v = P["conv_w"][:, :D_FF], P["conv_w"][:, D_FF:]
    cb_g, cb_v = P["conv_b"][:, :D_FF], P["conv_b"][:, D_FF:]
    act = _conv_fwd(ug, uv, cw_g, cw_v, cb_g, cb_v)
    fo = _mm(act, W["w_down"], "nn", f32, "down_proj")
    loss, dy, dfo, g_post_ffn = _final(x1, fo, tgt, P["post_ffn_norm"])
    dact = _mm(dfo, W["w_down"], "nt", f32, "d_act")
    gW_down = _mm(act, dfo, "tn", f32, "gw_down")
    dug, duv, st_g, st_v = _conv_bwd(ug, uv, dact, cw_g, cw_v, cb_g, cb_v)
    dh2 = _mm(dug, W["w_up_g"], "nt", f32, "dh2_gate")
    dh2 = _mm(duv, W["w_up_v"], "nt", f32, "dh2_val", acc=dh2)
    gW_up = jnp.concatenate([_mm(h2, dug, "tn", f32, "gw_up_gate"), _mm(h2, duv, "tn", f32, "gw_up_val")], axis=1)
    dx1, dmo, g_pre_ffn, g_post_mix = _mid_bwd(dy, dh2, x1, mo, P["pre_ffn_norm"], P["post_mix_norm"])
    dmerged = _mm(dmo, W["w_out"], "nt", f32, "d_merged")
    gW_out = _mm(merged, dmo, "tn", f32, "gw_out")
    da, db, dgc = _gate_bwd(dmerged, a, b, gc)
    dyattn = _mm(da, W["w_ba"], "nt", f32, "d_yattn")
    gW_ba = _mm(y_attn_b, da, "tn", f32, "gw_ba")
    dyhgrn = _mm(db, W["w_bh"], "nt", f32, "d_yhgrn")
    gW_bh = _mm(y_hgrn, db, "tn", f32, "gw_bh")
    dq_h, df_h, dv_h, dog_h, glb8, gnw8 = _hgrn_bwd(hg, o_raw, dyhgrn, ck, lb, P["hgrn_norm"])
    dhg = jnp.concatenate([dq_h, df_h, dv_h, dog_h], axis=1)
    g_lb_raw = _lb_bwd(P["hgrn_lb_raw"], glb8[0:1])
    gn = gnw8[0:1]
    g_hgrn_norm = (gn[:, 0:128] + gn[:, 128:256]) + (gn[:, 256:384] + gn[:, 384:512])
    dos = _attn_merge_bwd(dyattn, y_attn, w0, w1, w2)
    dh_parts, gW_qkv, g_rel = [], [], []
    for g, d in enumerate(DILATIONS):
        do_g = _permute(dos[g], d, f"perm_do{g}")
        dvec_g = _permute(dos[3 + g], d, f"perm_dvec{g}")
        dq, dk, dv, dbias = _attn_bwd(qkv[g], biases[g], do_g, dvec_g, lbuf[g], (S // d) // ATTN_BLOCK, f"attn_bwd{g}")
        dqkv = jnp.concatenate([dq, dk, dv], axis=1)
        gW_qkv.append(_mm(hs[g], dqkv, "tn", f32, f"gw_qkv{g}"))
        dh_g = _mm(dqkv, W["w_qkv"][g], "nt", f32, f"dh1_qkv{g}")
        dh_parts.append(_unpermute(dh_g, d, f"unperm_dh{g}"))
        g_rel.append(_bias_grad(dbias.reshape(8, -1), consts[g][0], f"bias_grad{g}"))
    dh_main = _mm(dhg, W["w_hg"], "nt", f32, "dh1_hg", acc=dh_parts[0])
    dh_main = _mm(dgc, W["w_gate"], "nt", f32, "dh1_gate", acc=dh_main)
    gW_hg = _mm(h1, dhg, "tn", f32, "gw_hg")
    gW_gate = _mm(h1, dgc, "tn", f32, "gw_gate")
    grad_x, g_pre_mix = _first_bwd(x, dx1, dh_main, dh_parts[1], dh_parts[2], P["pre_mix_norm"])

    big = dict(w_in=jnp.concatenate(gW_qkv + [gW_hg, gW_gate], axis=1), w_ba=gW_ba, w_bh=gW_bh, w_out=gW_out,
               w_up=gW_up, w_down=gW_down)
    g_conv_w = jnp.concatenate([st_g[0:3], st_v[0:3]], axis=1)
    g_conv_b = jnp.concatenate([st_g[3:4], st_v[3:4]], axis=1)
    small = dict(pre_mix_norm=g_pre_mix, rel_bias=jnp.concatenate(g_rel, axis=1), hgrn_lb_raw=g_lb_raw,
                 hgrn_norm=g_hgrn_norm, post_mix_norm=g_post_mix, pre_ffn_norm=g_pre_ffn, conv_b=g_conv_b,
                 post_ffn_norm=g_post_ffn, conv_w=g_conv_w)
    return loss, grad_x, big, small


def _full_weights(slabs):
    sh = _unpack_rows(slabs)
    w_in = _cols_to_full(sh["w_in"])
    w_up = _cols_to_full(sh["w_up"])
    return dict(
        w_qkv=[w_in[:, g * QKV_G:(g + 1) * QKV_G] for g in range(N_GROUPS)],
        w_hg=w_in[:, 3 * QKV_G:3 * QKV_G + 4 * HGRN_W],
        w_gate=w_in[:, 3 * QKV_G + 4 * HGRN_W:],
        w_ba=_cols_to_full(sh["w_ba"]),
        w_bh=_cols_to_full(sh["w_bh"]),
        w_out=sh["w_out"].reshape(D_MODEL, D_MODEL),
        w_up_g=w_up[:, :D_FF],
        w_up_v=w_up[:, D_FF:],
        w_down=sh["w_down"].reshape(D_FF, D_MODEL),
    )


def kernel(x, pre_mix_norm, w_in, rel_bias, hgrn_lb_raw, hgrn_norm, w_branch_attn, w_branch_hgrn, w_out, post_mix_norm, pre_ffn_norm, w_up, conv_w, conv_b, w_down, post_ffn_norm, loss_target, m_pre_mix_norm, m_w_in, m_rel_bias, m_hgrn_lb_raw, m_hgrn_norm, m_w_branch_attn, m_w_branch_hgrn, m_w_out, m_post_mix_norm, m_pre_ffn_norm, m_w_up, m_conv_w, m_conv_b, m_w_down, m_post_ffn_norm, v_pre_mix_norm, v_w_in, v_rel_bias, v_hgrn_lb_raw, v_hgrn_norm, v_w_branch_attn, v_w_branch_hgrn, v_w_out, v_post_mix_norm, v_pre_ffn_norm, v_w_up, v_conv_w, v_conv_b, v_w_down, v_post_ffn_norm):
    ci = lax.axis_index("c")
    dev = 4 * lax.axis_index("x") + 2 * lax.axis_index("y") + ci
    wts = dict(w_in=w_in[0], w_ba=w_branch_attn[0], w_bh=w_branch_hgrn[0], w_out=w_out[0], w_up=w_up[0],
               w_down=w_down[0])
    mom = dict(w_in=m_w_in[0], w_ba=m_w_branch_attn[0], w_bh=m_w_branch_hgrn[0], w_out=m_w_out[0], w_up=m_w_up[0],
               w_down=m_w_down[0])
    var = dict(w_in=v_w_in[0], w_ba=v_w_branch_attn[0], w_bh=v_w_branch_hgrn[0], w_out=v_w_out[0], w_up=v_w_up[0],
               w_down=v_w_down[0])
    small_w = dict(pre_mix_norm=pre_mix_norm, rel_bias=rel_bias, hgrn_lb_raw=hgrn_lb_raw, hgrn_norm=hgrn_norm,
                   post_mix_norm=post_mix_norm, pre_ffn_norm=pre_ffn_norm, conv_b=conv_b, post_ffn_norm=post_ffn_norm)
    small_m = dict(pre_mix_norm=m_pre_mix_norm, rel_bias=m_rel_bias, hgrn_lb_raw=m_hgrn_lb_raw, hgrn_norm=m_hgrn_norm,
                   post_mix_norm=m_post_mix_norm, pre_ffn_norm=m_pre_ffn_norm, conv_b=m_conv_b,
                   post_ffn_norm=m_post_ffn_norm)
    small_v = dict(pre_mix_norm=v_pre_mix_norm, rel_bias=v_rel_bias, hgrn_lb_raw=v_hgrn_lb_raw, hgrn_norm=v_hgrn_norm,
                   post_mix_norm=v_post_mix_norm, pre_ffn_norm=v_pre_ffn_norm, conv_b=v_conv_b,
                   post_ffn_norm=v_post_ffn_norm)

    slab = _pack_rows(*[wts[k].astype(bf16)[None] for k, _ in _PACK_SIZES])[0]
    W = _full_weights(_all_gather(slab, "ag_w"))
    cw_pad = jnp.pad(conv_w[0], ((0, SUBLANE - 3), (0, 768 - 704)))
    conv_w_full = _cols_to_full(_all_gather(cw_pad, "ag_convw")[:, 0:3, 0:704])
    P = dict(small_w)
    P["conv_w"] = conv_w_full

    loss8, grad_x, big, small = _local_step(x[0], loss_target[0], W, P)
    loss = lax.psum(loss8[0, 0], ("x", "y", "c"))

    cols = dict(w_in=_full_to_cols(big["w_in"]), w_ba=_full_to_cols(big["w_ba"]), w_bh=_full_to_cols(big["w_bh"]),
                w_out=big["w_out"].reshape(8, 128, D_MODEL), w_up=_full_to_cols(big["w_up"]),
                w_down=big["w_down"].reshape(8, 352, D_MODEL))
    gslab = _pack_rows(*[cols[k].astype(bf16) for k, _ in _PACK_SIZES])
    by_core = jnp.swapaxes(gslab.reshape(4, 2, 2400, D_MODEL), 0, 1)
    from_sib = _core_swap(by_core, "rs_cores")
    mine = lax.dynamic_index_in_dim(by_core, ci, axis=0, keepdims=False)
    chip_sum = _pair_add(mine, from_sib, "rs_pair_add")
    parts = _unpack_rows(_chip_comm(chip_sum, False, "rs_chips"))
    outs_big = {}
    for k, _ in _PACK_SIZES:
        outs_big[k] = _adamw(wts[k], mom[k], var[k], parts[k], "adamw_" + k)

    spack = jnp.concatenate([_pack_small(small),
                             jnp.pad(small["conv_w"].reshape(-1, LANE), ((0, _CONVW_ROWS - 132), (0, 0)))], axis=0)
    allp = _core_gather(_chip_comm(spack, True, "ag_small_chips"), "ag_small_cores")
    ssum = _sum8(allp, "small_sum")
    gs = ssum[:_SMALL_ROWS]
    res_small = _adamw(_pack_small(small_w), _pack_small(small_m), _pack_small(small_v), gs, "adamw_small")
    sm = [_unpack_small(t) for t in res_small]
    g_cw_full = ssum[_SMALL_ROWS:_SMALL_ROWS + 132].reshape(3, 2 * D_FF)
    g_cw = lax.dynamic_slice_in_dim(g_cw_full, dev * 704, 704, axis=1)
    res_cw = _adamw(conv_w[0], m_conv_w[0], v_conv_w[0], g_cw, "adamw_conv_w")

    def pick(i):
        def big_(k):
            return outs_big[k][i][None]
        return [sm[i]["pre_mix_norm"], big_("w_in"), sm[i]["rel_bias"], sm[i]["hgrn_lb_raw"], sm[i]["hgrn_norm"],
                big_("w_ba"), big_("w_bh"), big_("w_out"), sm[i]["post_mix_norm"], sm[i]["pre_ffn_norm"],
                big_("w_up"), res_cw[i][None], sm[i]["conv_b"], big_("w_down"), sm[i]["post_ffn_norm"]]

    return (loss, grad_x[None], *pick(0), *pick(1), *pick(2), *pick(3))
```

```python
import functools
import math

import jax
import jax.numpy as jnp
from jax import lax
from jax.experimental import pallas as pl
from jax.experimental.pallas import tpu as pltpu

f32 = jnp.float32
bf16 = jnp.bfloat16
SDS = jax.ShapeDtypeStruct
HIGHEST = lax.Precision.HIGHEST
MESH = pl.DeviceIdType.MESH

NN = (((1,), (0,)), ((), ()))
NT = (((1,), (1,)), ((), ()))
TN = (((0,), (0,)), ((), ()))

D_MODEL = 1024
N_GROUPS = 3
DILATIONS = (1, 4, 16)
HEAD_DIM = 64
ATTN_BLOCK = 128
QKV_G = 1536
ATTN_OUT = 512
HGRN_W = 512
HGRN_CHUNK = 32
D_FF = 2816
NUM_BUCKETS = 32
MAX_EXACT = 16
MAX_DISTANCE = 2048
NEG_INF = -1e30
EPS = 1e-6
LANE = 128
SUBLANE = 8
VMEM_BIG = 48 * 1024 * 1024
MM_ROWS = 512
MM_OUT_BYTES = 8 * 1024 * 1024

ADAM_LR, ADAM_B1, ADAM_B2, ADAM_EPS, ADAM_WD, ADAM_STEP = 0.001, 0.9, 0.999, 1e-08, 0.01, 10


def _pick(n, pref):
    t = pref
    while t >= LANE:
        if n % t == 0:
            return t
        t //= 2
    return n


def _cparams(sem=None, vmem=None):
    kw = {}
    if sem is not None:
        kw["dimension_semantics"] = sem
    if vmem is not None:
        kw["vmem_limit_bytes"] = vmem
    return pltpu.CompilerParams(**kw)


def _sigmoid(x):
    return jax.nn.sigmoid(x)


def _colsum8(x):
    return x.reshape(x.shape[0] // SUBLANE, SUBLANE, x.shape[1]).sum(axis=0)


def _mm(a, b, mode, out_dtype, name, acc=None):
    if mode == "nn":
        (M, K), (_, N) = a.shape, b.shape
    elif mode == "nt":
        (M, K), (N, _) = a.shape, b.shape
    else:
        (K, M), (_, N) = a.shape, b.shape
    dims = {"nn": NN, "nt": NT, "tn": TN}[mode]
    has_acc = acc is not None
    if mode == "tn":
        assert not has_acc
        tmm = M if M * N * 4 <= MM_OUT_BYTES else M // 2
        ts = _pick(K, MM_ROWS)
        nk = K // ts

        def body_tn(a_ref, b_ref, o_ref):
            k = pl.program_id(1)
            part = lax.dot_general(a_ref[...], b_ref[...], dims, preferred_element_type=f32)

            @pl.when(k == 0)
            def _():
                o_ref[...] = part

            @pl.when(k > 0)
            def _():
                o_ref[...] += part

        return pl.pallas_call(
            body_tn,
            grid=(M // tmm, nk),
            in_specs=[pl.BlockSpec((ts, tmm), lambda i, k: (k, i)), pl.BlockSpec((ts, N), lambda i, k: (k, 0))],
            out_specs=pl.BlockSpec((tmm, N), lambda i, k: (i, 0)),
            out_shape=SDS((M, N), out_dtype),
            compiler_params=_cparams(("parallel", "arbitrary"), VMEM_BIG),
            name=name,
        )(a, b)

    tm = _pick(M, MM_ROWS)

    def body(*refs):
        if has_acc:
            a_ref, b_ref, c_ref, o_ref = refs
        else:
            a_ref, b_ref, o_ref = refs
        part = lax.dot_general(a_ref[...], b_ref[...], dims, preferred_element_type=f32)
        if has_acc:
            part = part + c_ref[...]
        o_ref[...] = part.astype(out_dtype)

    specs = [pl.BlockSpec((tm, K), lambda i: (i, 0)), pl.BlockSpec(b.shape, lambda i: (0, 0))]
    args = [a, b]
    aliases = {}
    if has_acc:
        specs.append(pl.BlockSpec((tm, N), lambda i: (i, 0)))
        args.append(acc)
        aliases = {2: 0}
    return pl.pallas_call(
        body,
        grid=(M // tm,),
        in_specs=specs,
        out_specs=pl.BlockSpec((tm, N), lambda i: (i, 0)),
        out_shape=SDS((M, N), out_dtype),
        input_output_aliases=aliases,
        compiler_params=_cparams(("parallel",), VMEM_BIG),
        name=name,
    )(*args)


def _permute(x, d, name):
    if d == 1:
        return x
    S, C = x.shape
    U = S // d

    def body(x_ref, o_ref):
        for r in range(d):
            o_ref[r] = x_ref[pl.ds(r, ATTN_BLOCK, stride=d), :]

    out = pl.pallas_call(
        body,
        grid=(U // ATTN_BLOCK, C // LANE),
        in_specs=[pl.BlockSpec((ATTN_BLOCK * d, LANE), lambda i, j: (i, j))],
        out_specs=pl.BlockSpec((d, ATTN_BLOCK, LANE), lambda i, j: (0, i, j)),
        out_shape=SDS((d, U, C), x.dtype),
        compiler_params=_cparams(("parallel", "parallel")),
        name=name,
    )(x)
    return out.reshape(S, C)


def _unpermute(x, d, name):
    if d == 1:
        return x
    S, C = x.shape
    U = S // d

    def body(x_ref, o_ref):
        for r in range(d):
            o_ref[pl.ds(r, ATTN_BLOCK, stride=d), :] = x_ref[r]

    return pl.pallas_call(
        body,
        grid=(U // ATTN_BLOCK, C // LANE),
        in_specs=[pl.BlockSpec((d, ATTN_BLOCK, LANE), lambda i, j: (0, i, j))],
        out_specs=pl.BlockSpec((ATTN_BLOCK * d, LANE), lambda i, j: (i, j)),
        out_shape=SDS((S, C), x.dtype),
        compiler_params=_cparams(("parallel", "parallel")),
        name=name,
    )(x.reshape(d, U, C))


def _rms_parts(xv):
    r = lax.rsqrt(jnp.mean(xv * xv, axis=-1, keepdims=True) + EPS)
    return r, xv * r


def _rms_bwd(xhat, r, w, dy):
    dyw = dy * w
    return r * (dyw - xhat * jnp.mean(dyw * xhat, axis=-1, keepdims=True))


def _rmsnorm(x, w, name):
    S, D = x.shape
    tm = _pick(S, 512)

    def body(x_ref, w_ref, o_ref):
        _, xh = _rms_parts(x_ref[...])
        o_ref[...] = (xh * w_ref[...]).astype(bf16)

    return pl.pallas_call(
        body,
        grid=(S // tm,),
        in_specs=[pl.BlockSpec((tm, D), lambda i: (i, 0)), pl.BlockSpec((1, D), lambda i: (0, 0))],
        out_specs=pl.BlockSpec((tm, D), lambda i: (i, 0)),
        out_shape=SDS((S, D), bf16),
        compiler_params=_cparams(("parallel",)),
        name=name,
    )(x, w)


def _mid_fwd(x, mo, w_pm, w_pf):
    S, D = x.shape
    tm = _pick(S, 512)

    def body(x_ref, mo_ref, wpm_ref, wpf_ref, x1_ref, h2_ref):
        _, moh = _rms_parts(mo_ref[...])
        x1 = x_ref[...] + moh * wpm_ref[...]
        x1_ref[...] = x1
        _, x1h = _rms_parts(x1)
        h2_ref[...] = (x1h * wpf_ref[...]).astype(bf16)

    row = pl.BlockSpec((tm, D), lambda i: (i, 0))
    vec = pl.BlockSpec((1, D), lambda i: (0, 0))
    return pl.pallas_call(
        body,
        grid=(S // tm,),
        in_specs=[row, row, vec, vec],
        out_specs=[row, row],
        out_shape=[SDS((S, D), f32), SDS((S, D), bf16)],
        compiler_params=_cparams(("parallel",)),
        name="mid_fwd",
    )(x, mo, w_pm, w_pf)


def _final(x1, fo, tgt, w_pfn):
    S, D = x1.shape
    tm = _pick(S, 512)
    nt = S // tm

    def body(x1_ref, fo_ref, t_ref, w_ref, loss_ref, dy_ref, dfo_ref, gw_ref, lacc, gacc):
        i = pl.program_id(0)

        @pl.when(i == 0)
        def _():
            lacc[...] = jnp.zeros_like(lacc)
            gacc[...] = jnp.zeros_like(gacc)

        w = w_ref[...]
        r, foh = _rms_parts(fo_ref[...])
        y = x1_ref[...] + foh * w
        err = y - t_ref[...]
        lacc[...] += _colsum8(err * err)
        dy = err * (1.0 / D)
        dy_ref[...] = dy
        gacc[...] += _colsum8(dy * foh)
        dfo_ref[...] = _rms_bwd(foh, r, w, dy).astype(bf16)

        @pl.when(i == nt - 1)
        def _():
            loss_ref[...] = jnp.full((SUBLANE, LANE), 0.5 / D, f32) * jnp.sum(lacc[...])
            gw_ref[...] = jnp.sum(gacc[...], axis=0, keepdims=True)

    row = pl.BlockSpec((tm, D), lambda i: (i, 0))
    vec = pl.BlockSpec((1, D), lambda i: (0, 0))
    return pl.pallas_call(
        body,
        grid=(nt,),
        in_specs=[row, row, row, vec],
        out_specs=[pl.BlockSpec((SUBLANE, LANE), lambda i: (0, 0)), row, row, vec],
        out_shape=[SDS((SUBLANE, LANE), f32), SDS((S, D), f32), SDS((S, D), bf16), SDS((1, D), f32)],
        scratch_shapes=[pltpu.VMEM((SUBLANE, D), f32), pltpu.VMEM((SUBLANE, D), f32)],
        compiler_params=_cparams(("arbitrary",)),
        name="final_loss",
    )(x1, fo, tgt, w_pfn)


def _mid_bwd(dy, dh2, x1, mo, w_pf, w_pm):
    S, D = dy.shape
    tm = _pick(S, 512)
    nt = S // tm

    def body(dy_ref, dh2_ref, x1_ref, mo_ref, wpf_ref, wpm_ref, dx1_ref, dmo_ref, gpf_ref, gpm_ref, apf, apm):
        i = pl.program_id(0)

        @pl.when(i == 0)
        def _():
            apf[...] = jnp.zeros_like(apf)
            apm[...] = jnp.zeros_like(apm)

        r1, x1h = _rms_parts(x1_ref[...])
        dh2 = dh2_ref[...]
        apf[...] += _colsum8(dh2 * x1h)
        dx1 = dy_ref[...] + _rms_bwd(x1h, r1, wpf_ref[...], dh2)
        dx1_ref[...] = dx1
        rm, moh = _rms_parts(mo_ref[...])
        apm[...] += _colsum8(dx1 * moh)
        dmo_ref[...] = _rms_bwd(moh, rm, wpm_ref[...], dx1).astype(bf16)

        @pl.when(i == nt - 1)
        def _():
            gpf_ref[...] = jnp.sum(apf[...], axis=0, keepdims=True)
            gpm_ref[...] = jnp.sum(apm[...], axis=0, keepdims=True)

    row = pl.BlockSpec((tm, D), lambda i: (i, 0))
    vec = pl.BlockSpec((1, D), lambda i: (0, 0))
    return pl.pallas_call(
        body,
        grid=(nt,),
        in_specs=[row, row, row, row, vec, vec],
        out_specs=[row, row, vec, vec],
        out_shape=[SDS((S, D), f32), SDS((S, D), bf16), SDS((1, D), f32), SDS((1, D), f32)],
        scratch_shapes=[pltpu.VMEM((SUBLANE, D), f32), pltpu.VMEM((SUBLANE, D), f32)],
        compiler_params=_cparams(("arbitrary",)),
        name="mid_bwd",
    )(dy, dh2, x1, mo, w_pf, w_pm)


def _first_bwd(x, dx1, dh_a, dh_b, dh_c, w_pre):
    S, D = x.shape
    tm = _pick(S, 512)
    nt = S // tm

    def body(x_ref, dx1_ref, a_ref, b_ref, c_ref, w_ref, gx_ref, gw_ref, acc):
        i = pl.program_id(0)

        @pl.when(i == 0)
        def _():
            acc[...] = jnp.zeros_like(acc)

        r, xh = _rms_parts(x_ref[...])
        dh = (a_ref[...] + b_ref[...]) + c_ref[...]
        acc[...] += _colsum8(dh * xh)
        gx_ref[...] = dx1_ref[...] + _rms_bwd(xh, r, w_ref[...], dh)

        @pl.when(i == nt - 1)
        def _():
            gw_ref[...] = jnp.sum(acc[...], axis=0, keepdims=True)

    row = pl.BlockSpec((tm, D), lambda i: (i, 0))
    vec = pl.BlockSpec((1, D), lambda i: (0, 0))
    return pl.pallas_call(
        body,
        grid=(nt,),
        in_specs=[row, row, row, row, row, vec],
        out_specs=[row, vec],
        out_shape=[SDS((S, D), f32), SDS((1, D), f32)],
        scratch_shapes=[pltpu.VMEM((SUBLANE, D), f32)],
        compiler_params=_cparams(("arbitrary",)),
        name="first_bwd",
    )(x, dx1, dh_a, dh_b, dh_c, w_pre)


def _t5_bucket(dist):
    n = jnp.maximum(dist, 0)
    nf = jnp.maximum(n, 1).astype(f32)
    large = MAX_EXACT + (jnp.log(nf / MAX_EXACT) / math.log(MAX_DISTANCE / MAX_EXACT)
                         * (NUM_BUCKETS - MAX_EXACT)).astype(jnp.int32)
    large = jnp.minimum(large, NUM_BUCKETS - 1)
    return jnp.where(n < MAX_EXACT, n, large)


def _bias_consts(d):
    blk = ATTN_BLOCK
    rel = jnp.arange(blk)[:, None] + blk - jnp.arange(2 * blk)[None, :]
    in_win = (rel >= 0) & (rel <= blk)
    bucket = _t5_bucket(rel * d).reshape(1, -1)
    onehot = (bucket == jnp.arange(NUM_BUCKETS)[:, None]).astype(f32)
    return onehot, in_win.astype(f32).reshape(1, -1)


def _bias_build(tab_t, onehot, maskf, name):
    H = tab_t.shape[0]

    def body(t_ref, oh_ref, m_ref, o_ref):
        b = jnp.dot(t_ref[...], oh_ref[...], precision=HIGHEST, preferred_element_type=f32)
        o_ref[...] = jnp.where(m_ref[...] > 0.5, b, NEG_INF)

    return pl.pallas_call(body, out_shape=SDS((H, onehot.shape[1]), f32), name=name)(tab_t, onehot, maskf)


def _bias_grad(dbias_flat, onehot, name):
    H = dbias_flat.shape[0]

    def body(g_ref, oh_ref, o_ref):
        o_ref[...] = lax.dot_general(oh_ref[...], g_ref[...], NT, precision=HIGHEST, preferred_element_type=f32)

    return pl.pallas_call(body, out_shape=SDS((NUM_BUCKETS, H), f32), name=name)(dbias_flat, onehot)


def _qkv_specs(nb):
    blk = (ATTN_BLOCK, LANE)
    cur = lambda off: (lambda h, i: (jnp.minimum(i, nb - 1), off + h))
    prev = lambda off: (lambda h, i: (jnp.maximum(jnp.minimum(i, nb - 1) - 1, 0), off + h))
    return [pl.BlockSpec(blk, cur(0)), pl.BlockSpec(blk, prev(4)), pl.BlockSpec(blk, cur(4)),
            pl.BlockSpec(blk, prev(8)), pl.BlockSpec(blk, cur(8))]


def _head_masks():
    lane = lax.broadcasted_iota(jnp.int32, (ATTN_BLOCK, LANE), 1)
    return lane < HEAD_DIM


def _attn_fwd(qkv, bias, bps, name):
    S = qkv.shape[0]
    nb = S // ATTN_BLOCK
    scale = HEAD_DIM ** -0.5

    def body(q_ref, kp_ref, kc_ref, vp_ref, vc_ref, b_ref, o_ref, l_ref):
        i = pl.program_id(1)
        q2 = q_ref[...]
        kk = jnp.concatenate([kp_ref[...], kc_ref[...]], axis=0)
        vv = jnp.concatenate([vp_ref[...], vc_ref[...]], axis=0)
        low = _head_masks()
        col = lax.broadcasted_iota(jnp.int32, (ATTN_BLOCK, 2 * ATTN_BLOCK), 1)
        dead = jnp.logical_and(i % bps == 0, col < ATTN_BLOCK)
        outs, lses = [], []
        for h in range(2):
            hm = low if h == 0 else jnp.logical_not(low)
            qh = jnp.where(hm, q2, jnp.zeros_like(q2))
            s = lax.dot_general(qh, kk, NT, preferred_element_type=f32) * scale + b_ref[h]
            s = jnp.where(dead, NEG_INF, s)
            m = jnp.max(s, axis=-1, keepdims=True)
            p = jnp.exp(s - m)
            l = jnp.sum(p, axis=-1, keepdims=True)
            outs.append(jnp.dot(p.astype(bf16), vv, preferred_element_type=f32) / l)
            lses.append(m + jnp.log(l))
        o_ref[...] = jnp.where(low, outs[0], outs[1])
        l_ref[...] = jnp.where(low, lses[0], lses[1])

    blk = pl.BlockSpec((ATTN_BLOCK, LANE), lambda h, i: (i, h))
    return pl.pallas_call(
        body,
        grid=(4, nb),
        in_specs=_qkv_specs(nb) + [pl.BlockSpec((2, ATTN_BLOCK, 2 * ATTN_BLOCK), lambda h, i: (h, 0, 0))],
        out_specs=[blk, blk],
        out_shape=[SDS((S, ATTN_OUT), f32), SDS((S, ATTN_OUT), f32)],
        compiler_params=_cparams(("parallel", "parallel")),
        name=name,
    )(qkv, qkv, qkv, qkv, qkv, bias)


def _attn_bwd(qkv, bias, do, dvec, lse, bps, name):
    S = qkv.shape[0]
    nb = S // ATTN_BLOCK
    scale = HEAD_DIM ** -0.5

    def body(q_ref, kp_ref, kc_ref, vp_ref, vc_ref, b_ref, do_ref, dvec_ref, lse_ref,
             dq_ref, dk_ref, dv_ref, db_ref, ck, cv):
        i = pl.program_id(1)

        @pl.when(i == 0)
        def _():
            ck[...] = jnp.zeros_like(ck)
            cv[...] = jnp.zeros_like(cv)
            db_ref[...] = jnp.zeros_like(db_ref)

        @pl.when(i < nb)
        def _():
            q2 = q_ref[...]
            kk = jnp.concatenate([kp_ref[...], kc_ref[...]], axis=0)
            vv = jnp.concatenate([vp_ref[...], vc_ref[...]], axis=0)
            do2 = do_ref[...].astype(bf16)
            dvec2 = dvec_ref[...]
            lse2 = lse_ref[...]
            low = _head_masks()
            col = lax.broadcasted_iota(jnp.int32, (ATTN_BLOCK, 2 * ATTN_BLOCK), 1)
            dead = jnp.logical_and(i % bps == 0, col < ATTN_BLOCK)
            low2 = lax.broadcasted_iota(jnp.int32, (2 * ATTN_BLOCK, LANE), 1) < HEAD_DIM
            dqs, dks, dvs = [], [], []
            for h in range(2):
                hm = low if h == 0 else jnp.logical_not(low)
                c0 = h * HEAD_DIM
                qh = jnp.where(hm, q2, jnp.zeros_like(q2))
                doh = jnp.where(hm, do2, jnp.zeros_like(do2))
                s = lax.dot_general(qh, kk, NT, preferred_element_type=f32) * scale + b_ref[h]
                s = jnp.where(dead, NEG_INF, s)
                p = jnp.exp(s - lse2[:, c0:c0 + 1])
                dp = lax.dot_general(doh, vv, NT, preferred_element_type=f32)
                ds = p * (dp - dvec2[:, c0:c0 + 1])
                db_ref[h] += ds
                dsb = ds.astype(bf16)
                dqs.append(jnp.dot(dsb, kk, preferred_element_type=f32) * scale)
                dks.append(lax.dot_general(dsb, q2, TN, preferred_element_type=f32) * scale)
                dvs.append(lax.dot_general(p.astype(bf16), do2, TN, preferred_element_type=f32))
            dq_ref[...] = jnp.where(low, dqs[0], dqs[1]).astype(bf16)
            dk = jnp.where(low2, dks[0], dks[1])
            dv = jnp.where(low2, dvs[0], dvs[1])
            dk_ref[...] = (ck[...] + dk[:ATTN_BLOCK]).astype(bf16)
            dv_ref[...] = (cv[...] + dv[:ATTN_BLOCK]).astype(bf16)
            ck[...] = dk[ATTN_BLOCK:]
            cv[...] = dv[ATTN_BLOCK:]

        @pl.when(i == nb)
        def _():
            dk_ref[...] = ck[...].astype(bf16)
            dv_ref[...] = cv[...].astype(bf16)

    blk = (ATTN_BLOCK, LANE)
    cur = pl.BlockSpec(blk, lambda h, i: (jnp.minimum(i, nb - 1), h))
    lag = pl.BlockSpec(blk, lambda h, i: (jnp.maximum(i - 1, 0), h))
    bspec = pl.BlockSpec((2, ATTN_BLOCK, 2 * ATTN_BLOCK), lambda h, i: (h, 0, 0))
    return pl.pallas_call(
        body,
        grid=(4, nb + 1),
        in_specs=_qkv_specs(nb) + [bspec, cur, cur, cur],
        out_specs=[cur, lag, lag, bspec],
        out_shape=[SDS((S, ATTN_OUT), bf16), SDS((S, ATTN_OUT), bf16), SDS((S, ATTN_OUT), bf16),
                   SDS((8, ATTN_BLOCK, 2 * ATTN_BLOCK), f32)],
        scratch_shapes=[pltpu.VMEM(blk, f32), pltpu.VMEM(blk, f32)],
        compiler_params=_cparams(("parallel", "arbitrary")),
        name=name,
    )(qkv, qkv, qkv, qkv, qkv, bias, do, dvec, lse)


def _attn_merge(o0, o1, o2, l0, l1, l2):
    S, W = o0.shape
    tm = _pick(S, 512)

    def body(o0_ref, o1_ref, o2_ref, l0_ref, l1_ref, l2_ref, y_ref, yb_ref, w0_ref, w1_ref, w2_ref):
        a, b, c = l0_ref[...], l1_ref[...], l2_ref[...]
        m = jnp.maximum(jnp.maximum(a, b), c)
        ea, eb, ec = jnp.exp(a - m), jnp.exp(b - m), jnp.exp(c - m)
        den = (ea + eb) + ec
        w0, w1, w2 = ea / den, eb / den, ec / den
        y = (w0 * o0_ref[...] + w1 * o1_ref[...]) + w2 * o2_ref[...]
        y_ref[...] = y
        yb_ref[...] = y.astype(bf16)
        w0_ref[...] = w0
        w1_ref[...] = w1
        w2_ref[...] = w2

    row = pl.BlockSpec((tm, W), lambda i: (i, 0))
    return pl.pallas_call(
        body,
        grid=(S // tm,),
        in_specs=[row] * 6,
        out_specs=[row] * 5,
        out_shape=[SDS((S, W), f32), SDS((S, W), bf16)] + [SDS((S, W), f32)] * 3,
        compiler_params=_cparams(("parallel",)),
        name="attn_merge",
    )(o0, o1, o2, l0, l1, l2)


def _attn_merge_bwd(dy, y, w0, w1, w2):
    S, W = dy.shape
    tm = _pick(S, 512)

    def body(dy_ref, y_ref, w0_ref, w1_ref, w2_ref, a0, a1, a2, b0, b1, b2):
        dyv = dy_ref[...]
        r = lax.broadcasted_iota(jnp.int32, (LANE, LANE), 0) // HEAD_DIM
        c = lax.broadcasted_iota(jnp.int32, (LANE, LANE), 1) // HEAD_DIM
        seg = jnp.where(r == c, 1.0, 0.0).astype(f32)
        cbar = jnp.dot(dyv * y_ref[...], seg, precision=HIGHEST, preferred_element_type=f32)
        for w_ref, a_ref, b_ref in ((w0_ref, a0, b0), (w1_ref, a1, b1), (w2_ref, a2, b2)):
            w = w_ref[...]
            a_ref[...] = w * dyv
            b_ref[...] = w * cbar

    blk = pl.BlockSpec((tm, LANE), lambda i, j: (i, j))
    return pl.pallas_call(
        body,
        grid=(S // tm, W // LANE),
        in_specs=[blk] * 5,
        out_specs=[blk] * 6,
        out_shape=[SDS((S, W), f32)] * 6,
        compiler_params=_cparams(("parallel", "parallel")),
        name="attn_merge_bwd",
    )(dy, y, w0, w1, w2)


HGRN_SB = 256


def _chunk_masks(sb):
    r = lax.broadcasted_iota(jnp.int32, (sb, sb), 0)
    c = lax.broadcasted_iota(jnp.int32, (sb, sb), 1)
    same = (r // HGRN_CHUNK) == (c // HGRN_CHUNK)
    return same, jnp.logical_and(same, c <= r), jnp.logical_and(same, c >= r)


def _hgrn_prep(q_raw, f_raw, lbv, same, tril):
    sq = _sigmoid(q_raw)
    qs = q_raw * sq
    sig = _sigmoid(f_raw)
    f = lbv + (1.0 - lbv) * sig
    g = jnp.log(f)
    k = 1.0 - f
    G = jnp.dot(jnp.where(tril, 1.0, 0.0).astype(f32), g, precision=HIGHEST, preferred_element_type=f32)
    GL = jnp.dot(jnp.where(same, 1.0, 0.0).astype(f32), g, precision=HIGHEST, preferred_element_type=f32)
    eG = jnp.exp(G)
    einv = jnp.exp(-G)
    edec = jnp.exp(GL - G)
    return dict(sq=sq, qs=qs, sig=sig, f=f, k=k, eG=eG, einv=einv, edec=edec, eGL=jnp.exp(GL),
                qt=qs * eG, kt=k * einv, kd=k * edec)


def _hgrn_fwd(hg, lb, normw):
    S = hg.shape[0]
    sb = HGRN_SB
    nsb = S // sb
    nch = sb // HGRN_CHUNK

    def body(q_ref, f_ref, v_ref, og_ref, lb_ref, nw_ref, y_ref, o_ref, ck_ref, st):
        j = pl.program_id(1)

        @pl.when(j == 0)
        def _():
            st[...] = jnp.zeros_like(st)

        ST = st[...]
        ck_ref[0, 0] = ST
        same, tril, _ = _chunk_masks(sb)
        pr = _hgrn_prep(q_ref[...], f_ref[...], lb_ref[...], same, tril)
        qtb, ktb, kdb = pr["qt"].astype(bf16), pr["kt"].astype(bf16), pr["kd"].astype(bf16)
        eGL = pr["eGL"]
        vb = v_ref[...].astype(bf16)
        A = jnp.where(tril, lax.dot_general(qtb, ktb, NT, preferred_element_type=f32), 0.0)
        o = jnp.dot(A.astype(bf16), vb, preferred_element_type=f32)
        outs = []
        for ci in range(nch):
            lo = ci * HGRN_CHUNK
            sl = slice(lo, lo + HGRN_CHUNK)
            outs.append(o[sl] + lax.dot_general(qtb[sl], ST.astype(bf16), NT, preferred_element_type=f32))
            ST = ST * eGL[lo:lo + 1, :] + lax.dot_general(vb[sl], kdb[sl], TN, preferred_element_type=f32)
        st[...] = ST
        of = jnp.concatenate(outs, axis=0)
        o_ref[...] = of
        rms = lax.rsqrt(jnp.mean(of * of, axis=-1, keepdims=True) + EPS)
        ogv = og_ref[...]
        y_ref[...] = ((of * rms * nw_ref[...]) * (ogv * _sigmoid(ogv))).astype(bf16)

    col = lambda off: pl.BlockSpec((sb, LANE), lambda h, j: (j, off + h))
    return pl.pallas_call(
        body,
        grid=(4, nsb),
        in_specs=[col(0), col(4), col(8), col(12), pl.BlockSpec((1, LANE), lambda h, j: (0, h)),
                  pl.BlockSpec((1, LANE), lambda h, j: (0, 0))],
        out_specs=[col(0), col(0), pl.BlockSpec((1, 1, LANE, LANE), lambda h, j: (h, j, 0, 0))],
        out_shape=[SDS((S, HGRN_W), bf16), SDS((S, HGRN_W), f32), SDS((4, nsb, LANE, LANE), f32)],
        scratch_shapes=[pltpu.VMEM((LANE, LANE), f32)],
        compiler_params=_cparams(("parallel", "arbitrary")),
        name="hgrn_fwd",
    )(hg, hg, hg, hg, lb, normw)


def _hgrn_bwd(hg, o_raw, dy, ck, lb, normw):
    S = hg.shape[0]
    sb = HGRN_SB
    nsb = S // sb
    nch = sb // HGRN_CHUNK

    def body(q_ref, f_ref, v_ref, og_ref, o_ref, dy_ref, ck_ref, lb_ref, nw_ref,
             dq_ref, df_ref, dv_ref, dog_ref, glb_ref, gnw_ref, dst, alb, anw):
        j = pl.program_id(1)

        @pl.when(j == 0)
        def _():
            dst[...] = jnp.zeros_like(dst)
            alb[...] = jnp.zeros_like(alb)
            anw[...] = jnp.zeros_like(anw)

        same, tril, triu = _chunk_masks(sb)
        lbv = lb_ref[...]
        q_raw = q_ref[...]
        pr = _hgrn_prep(q_raw, f_ref[...], lbv, same, tril)
        qt, kt, kd, eGL = pr["qt"], pr["kt"], pr["kd"], pr["eGL"]
        qtb, ktb, kdb = qt.astype(bf16), kt.astype(bf16), kd.astype(bf16)
        vb = v_ref[...].astype(bf16)

        o = o_ref[...]
        ogv = og_ref[...]
        sog = _sigmoid(ogv)
        rms = lax.rsqrt(jnp.mean(o * o, axis=-1, keepdims=True) + EPS)
        oh = o * rms
        nw = nw_ref[...]
        dyv = dy_ref[...]
        dog_ref[...] = (dyv * (oh * nw) * (sog * (1.0 + ogv * (1.0 - sog)))).astype(bf16)
        dohw = dyv * (ogv * sog)
        anw[...] += _colsum8(dohw * oh)
        doh = dohw * nw
        do = rms * (doh - oh * jnp.mean(doh * oh, axis=-1, keepdims=True))
        dob = do.astype(bf16)

        Ab = jnp.where(tril, lax.dot_general(qtb, ktb, NT, preferred_element_type=f32), 0.0).astype(bf16)
        dAb = jnp.where(tril, lax.dot_general(dob, vb, NT, preferred_element_type=f32), 0.0).astype(bf16)
        dv_acc = lax.dot_general(Ab, dob, TN, preferred_element_type=f32)
        dqt = jnp.dot(dAb, ktb, preferred_element_type=f32)
        dkt = lax.dot_general(dAb, qtb, TN, preferred_element_type=f32)

        ST = ck_ref[0, 0]
        states = []
        for ci in range(nch):
            lo = ci * HGRN_CHUNK
            sl = slice(lo, lo + HGRN_CHUNK)
            states.append(ST)
            ST = ST * eGL[lo:lo + 1, :] + lax.dot_general(vb[sl], kdb[sl], TN, preferred_element_type=f32)

        dST = dst[...]
        dqt_i, dkd_i, dv_i, deg_i = [None] * nch, [None] * nch, [None] * nch, [None] * nch
        for ci in reversed(range(nch)):
            lo = ci * HGRN_CHUNK
            sl = slice(lo, lo + HGRN_CHUNK)
            ST0 = states[ci]
            dSTb = dST.astype(bf16)
            dv_i[ci] = lax.dot_general(kdb[sl], dSTb, NT, preferred_element_type=f32)
            dqt_i[ci] = jnp.dot(dob[sl], ST0.astype(bf16), preferred_element_type=f32)
            dkd_i[ci] = jnp.dot(vb[sl], dSTb, preferred_element_type=f32)
            deg_i[ci] = jnp.broadcast_to(jnp.sum(dST * ST0, axis=0, keepdims=True), (HGRN_CHUNK, LANE))
            dST = dST * eGL[lo:lo + 1, :] + lax.dot_general(dob[sl], qtb[sl], TN, preferred_element_type=f32)
        dst[...] = dST

        dqt = dqt + jnp.concatenate(dqt_i, axis=0)
        dkd = jnp.concatenate(dkd_i, axis=0)
        dv_ref[...] = (dv_acc + jnp.concatenate(dv_i, axis=0)).astype(bf16)
        deg = jnp.concatenate(deg_i, axis=0)

        dqs = dqt * pr["eG"]
        dkdkd = dkd * kd
        dG = dqt * qt - dkt * kt - dkdkd
        dk = dkt * pr["einv"] + dkd * pr["edec"]
        dGL = jnp.dot(jnp.where(same, 1.0, 0.0).astype(f32), dkdkd, precision=HIGHEST,
                      preferred_element_type=f32) + eGL * deg
        dg = jnp.dot(jnp.where(triu, 1.0, 0.0).astype(f32), dG, precision=HIGHEST,
                     preferred_element_type=f32) + dGL
        df = dg / pr["f"] - dk
        sig = pr["sig"]
        df_ref[...] = (df * (1.0 - lbv) * (sig * (1.0 - sig))).astype(bf16)
        alb[...] += _colsum8(df * (1.0 - sig))
        sq = pr["sq"]
        dq_ref[...] = (dqs * (sq * (1.0 + q_raw * (1.0 - sq)))).astype(bf16)

        @pl.when(j == nsb - 1)
        def _():
            glb_ref[...] = jnp.broadcast_to(jnp.sum(alb[...], axis=0, keepdims=True), (SUBLANE, LANE))
            gnw_ref[...] = jnp.broadcast_to(jnp.sum(anw[...], axis=0, keepdims=True), (SUBLANE, LANE))

    rev = lambda off: pl.BlockSpec((sb, LANE), lambda h, j: (nsb - 1 - j, off + h))
    stat = pl.BlockSpec((SUBLANE, LANE), lambda h, j: (0, h))
    return pl.pallas_call(
        body,
        grid=(4, nsb),
        in_specs=[rev(0), rev(4), rev(8), rev(12), rev(0), rev(0),
                  pl.BlockSpec((1, 1, LANE, LANE), lambda h, j: (h, nsb - 1 - j, 0, 0)),
                  pl.BlockSpec((1, LANE), lambda h, j: (0, h)), pl.BlockSpec((1, LANE), lambda h, j: (0, 0))],
        out_specs=[rev(0), rev(0), rev(0), rev(0), stat, stat],
        out_shape=[SDS((S, HGRN_W), bf16)] * 4 + [SDS((SUBLANE, HGRN_W), f32)] * 2,
        scratch_shapes=[pltpu.VMEM((LANE, LANE), f32), pltpu.VMEM((SUBLANE, LANE), f32),
                        pltpu.VMEM((SUBLANE, LANE), f32)],
        compiler_params=_cparams(("parallel", "arbitrary")),
        name="hgrn_bwd",
    )(hg, hg, hg, hg, o_raw, dy, ck, lb, normw)


def _lb_fwd(raw):
    def body(r_ref, o_ref):
        r = r_ref[...]
        m = jnp.max(r, axis=0, keepdims=True)
        e = jnp.exp(r - m)
        o_ref[...] = (e / jnp.sum(e, axis=0, keepdims=True))[0:1]

    return pl.pallas_call(body, out_shape=SDS((1, raw.shape[1]), f32), name="lb_fwd")(raw)


def _lb_bwd(raw, dlb):
    def body(r_ref, d_ref, o_ref):
        r = r_ref[...]
        m = jnp.max(r, axis=0, keepdims=True)
        e = jnp.exp(r - m)
        s = e / jnp.sum(e, axis=0, keepdims=True)
        s0 = s[0:1]
        onehot0 = jnp.where(lax.broadcasted_iota(jnp.int32, r.shape, 0) == 0, 1.0, 0.0)
        o_ref[...] = d_ref[...] * s0 * (onehot0 - s)

    return pl.pallas_call(body, out_shape=SDS(raw.shape, f32), name="lb_bwd")(raw, dlb)


def _gate_fwd(a, b, gc):
    S, D = a.shape
    tm = _pick(S, 512)

    def body(a_ref, b_ref, g0_ref, g1_ref, o_ref):
        s0, s1 = _sigmoid(g0_ref[...].astype(f32)), _sigmoid(g1_ref[...].astype(f32))
        o_ref[...] = (s0 * a_ref[...].astype(f32) + s1 * b_ref[...].astype(f32)).astype(bf16)

    row = pl.BlockSpec((tm, D), lambda i: (i, 0))
    return pl.pallas_call(
        body,
        grid=(S // tm,),
        in_specs=[row, row, row, pl.BlockSpec((tm, D), lambda i: (i, 1))],
        out_specs=row,
        out_shape=SDS((S, D), bf16),
        compiler_params=_cparams(("parallel",)),
        name="gate_fwd",
    )(a, b, gc, gc)


def _gate_bwd(dm, a, b, gc):
    S, D = a.shape
    tm = _pick(S, 512)

    def body(dm_ref, a_ref, b_ref, g0_ref, g1_ref, da_ref, db_ref, dg_ref):
        dmv = dm_ref[...].astype(f32)
        s0, s1 = _sigmoid(g0_ref[...].astype(f32)), _sigmoid(g1_ref[...].astype(f32))
        da_ref[...] = (dmv * s0).astype(bf16)
        db_ref[...] = (dmv * s1).astype(bf16)
        dg_ref[:, :D] = (dmv * a_ref[...].astype(f32) * (s0 * (1.0 - s0))).astype(bf16)
        dg_ref[:, D:] = (dmv * b_ref[...].astype(f32) * (s1 * (1.0 - s1))).astype(bf16)

    row = pl.BlockSpec((tm, D), lambda i: (i, 0))
    wide = pl.BlockSpec((tm, 2 * D), lambda i: (i, 0))
    return pl.pallas_call(
        body,
        grid=(S // tm,),
        in_specs=[row, row, row, row, pl.BlockSpec((tm, D), lambda i: (i, 1))],
        out_specs=[row, row, wide],
        out_shape=[SDS((S, D), bf16), SDS((S, D), bf16), SDS((S, 2 * D), bf16)],
        compiler_params=_cparams(("parallel",)),
        name="gate_bwd",
    )(dm, a, b, gc, gc)


CONV_ROWS = 512
INV_SQRT2 = 0.7071067811865476
INV_SQRT_2PI = 0.3989422804014327


CONV_HALO = 16


def _tile8(a, rows):
    return jnp.tile(a, (rows // a.shape[0], 1))


def _conv_rows(u_ref, w, b, r0, first):
    R = CONV_ROWS
    cur = u_ref[pl.ds(r0, R), :].astype(f32)
    prev8 = u_ref[pl.ds(pl.multiple_of(jnp.maximum(r0 - CONV_HALO, 0), CONV_HALO), CONV_HALO), :].astype(f32)
    prev8 = jnp.where(first, 0.0, prev8)
    row = lax.broadcasted_iota(jnp.int32, (R, LANE), 0)
    x1 = jnp.where(row < 1, _tile8(pltpu.roll(prev8, 1, 0), R), pltpu.roll(cur, 1, 0))
    x2 = jnp.where(row < 2, _tile8(pltpu.roll(prev8, 2, 0), R), pltpu.roll(cur, 2, 0))
    c = ((b + w[0:1] * x2) + w[1:2] * x1) + w[2:3] * cur
    return c, x2, x1, cur


def _conv_fwd(ug, uv, wg, wv, bg, bv):
    S, F = ug.shape
    nchunk = S // CONV_ROWS

    def body(ug_ref, uv_ref, wg_ref, wv_ref, bg_ref, bv_ref, o_ref):
        wgv, wvv, bgv, bvv = wg_ref[...], wv_ref[...], bg_ref[...], bv_ref[...]

        def step(ci, carry):
            r0 = pl.multiple_of(ci * CONV_ROWS, CONV_ROWS)
            cg = _conv_rows(ug_ref, wgv, bgv, r0, ci == 0)[0]
            cv = _conv_rows(uv_ref, wvv, bvv, r0, ci == 0)[0]
            gelu = 0.5 * cg * (1.0 + lax.erf(cg * INV_SQRT2))
            o_ref[pl.ds(r0, CONV_ROWS), :] = (gelu * cv).astype(bf16)
            return carry

        lax.fori_loop(0, nchunk, step, 0)

    col = pl.BlockSpec((S, LANE), lambda j: (0, j))
    w3 = pl.BlockSpec((3, LANE), lambda j: (0, j))
    b1 = pl.BlockSpec((1, LANE), lambda j: (0, j))
    return pl.pallas_call(
        body,
        grid=(F // LANE,),
        in_specs=[col, col, w3, w3, b1, b1],
        out_specs=col,
        out_shape=SDS((S, F), bf16),
        compiler_params=_cparams(("parallel",), VMEM_BIG),
        name="conv_fwd",
    )(ug, uv, wg, wv, bg, bv)


def _conv_bwd(ug, uv, dact, wg, wv, bg, bv):
    S, F = ug.shape
    R = CONV_ROWS
    nchunk = S // R

    def body(ug_ref, uv_ref, da_ref, wg_ref, wv_ref, bg_ref, bv_ref, dug_ref, duv_ref, sg_ref, sv_ref, dcg, dcv):
        wgv, wvv, bgv, bvv = wg_ref[...], wv_ref[...], bg_ref[...], bv_ref[...]
        zero = jnp.zeros((SUBLANE, LANE), f32)

        def fwd_step(ci, acc):
            r0 = pl.multiple_of(ci * R, R)
            cg, g2, g1, g0 = _conv_rows(ug_ref, wgv, bgv, r0, ci == 0)
            cv, v2, v1, v0 = _conv_rows(uv_ref, wvv, bvv, r0, ci == 0)
            da = da_ref[pl.ds(r0, R), :].astype(f32)
            cdf = 0.5 * (1.0 + lax.erf(cg * INV_SQRT2))
            pdf = INV_SQRT_2PI * jnp.exp(-0.5 * cg * cg)
            dg = da * cv * (cdf + cg * pdf)
            dv = da * (cg * cdf)
            dcg[pl.ds(r0, R), :] = dg
            dcv[pl.ds(r0, R), :] = dv
            new = (acc[0] + _colsum8(dg * g2), acc[1] + _colsum8(dg * g1), acc[2] + _colsum8(dg * g0),
                   acc[3] + _colsum8(dg),
                   acc[4] + _colsum8(dv * v2), acc[5] + _colsum8(dv * v1), acc[6] + _colsum8(dv * v0),
                   acc[7] + _colsum8(dv))
            return new

        acc = lax.fori_loop(0, nchunk, fwd_step, (zero,) * 8)
        rows = lax.broadcasted_iota(jnp.int32, (SUBLANE, LANE), 0)

        def stats(parts):
            out = jnp.zeros((SUBLANE, LANE), f32)
            for k, pt in enumerate(parts):
                out = jnp.where(rows == k, jnp.sum(pt, axis=0, keepdims=True), out)
            return out

        sg_ref[...] = stats(acc[0:4])
        sv_ref[...] = stats(acc[4:8])

        def du_rows(dc, w, r0, last):
            cur = dc[pl.ds(r0, R), :]
            nxt = dc[pl.ds(pl.multiple_of(jnp.minimum(r0 + R, S - SUBLANE), SUBLANE), SUBLANE), :]
            nxt = jnp.where(last, 0.0, nxt)
            row = lax.broadcasted_iota(jnp.int32, (R, LANE), 0)
            y1 = jnp.where(row >= R - 1, _tile8(pltpu.roll(nxt, SUBLANE - 1, 0), R), pltpu.roll(cur, R - 1, 0))
            y2 = jnp.where(row >= R - 2, _tile8(pltpu.roll(nxt, SUBLANE - 2, 0), R), pltpu.roll(cur, R - 2, 0))
            return w[2:3] * cur + w[1:2] * y1 + w[0:1] * y2

        def bwd_step(ci, carry):
            r0 = pl.multiple_of(ci * R, R)
            last = ci == nchunk - 1
            dug_ref[pl.ds(r0, R), :] = du_rows(dcg, wgv, r0, last).astype(bf16)
            duv_ref[pl.ds(r0, R), :] = du_rows(dcv, wvv, r0, last).astype(bf16)
            return carry

        lax.fori_loop(0, nchunk, bwd_step, 0)

    col = pl.BlockSpec((S, LANE), lambda j: (0, j))
    w3 = pl.BlockSpec((3, LANE), lambda j: (0, j))
    b1 = pl.BlockSpec((1, LANE), lambda j: (0, j))
    st = pl.BlockSpec((SUBLANE, LANE), lambda j: (0, j))
    return pl.pallas_call(
        body,
        grid=(F // LANE,),
        in_specs=[col, col, col, w3, w3, b1, b1],
        out_specs=[col, col, st, st],
        out_shape=[SDS((S, F), bf16), SDS((S, F), bf16), SDS((SUBLANE, F), f32), SDS((SUBLANE, F), f32)],
        scratch_shapes=[pltpu.VMEM((S, LANE), f32), pltpu.VMEM((S, LANE), f32)],
        compiler_params=_cparams(("parallel",), VMEM_BIG),
        name="conv_bwd",
    )(ug, uv, dact, wg, wv, bg, bv)


def _adam_math(w, g, m, v):
    m = ADAM_B1 * m + (1.0 - ADAM_B1) * g
    v = ADAM_B2 * v + (1.0 - ADAM_B2) * (g * g)
    m_hat = m / (1.0 - ADAM_B1 ** ADAM_STEP)
    v_hat = v / (1.0 - ADAM_B2 ** ADAM_STEP)
    delta = -ADAM_LR * (m_hat / (jnp.sqrt(v_hat) + ADAM_EPS) + ADAM_WD * w)
    return delta, m, v


def _adamw(w, m, v, g, name):
    R, C = w.shape
    parts = g.ndim == 3
    tr = R
    for t in (256, 128, 64, 32, 16):
        if R % t == 0 and R > t:
            tr = t
            break

    def body(w_ref, m_ref, v_ref, g_ref, go_ref, d_ref, mo_ref, vo_ref):
        if parts:
            gv = ((g_ref[0].astype(f32) + g_ref[1].astype(f32)) + g_ref[2].astype(f32)) + g_ref[3].astype(f32)
        else:
            gv = g_ref[...]
        go_ref[...] = gv
        d, mn, vn = _adam_math(w_ref[...], gv, m_ref[...], v_ref[...])
        d_ref[...] = d
        mo_ref[...] = mn
        vo_ref[...] = vn

    row = pl.BlockSpec((tr, C), lambda i: (i, 0))
    gspec = pl.BlockSpec((4, tr, C), lambda i: (0, i, 0)) if parts else row
    return pl.pallas_call(
        body,
        grid=(R // tr,),
        in_specs=[row, row, row, gspec],
        out_specs=[row] * 4,
        out_shape=[SDS((R, C), f32)] * 4,
        compiler_params=_cparams(("parallel",)),
        name=name,
    )(w, m, v, g)


def _sum8(parts, name):
    _, _, R, C = parts.shape

    def body(p_ref, o_ref):
        acc = p_ref[0, 0]
        for c in range(2):
            for k in range(4):
                if c or k:
                    acc = acc + p_ref[c, k]
        o_ref[...] = acc

    return pl.pallas_call(body, out_shape=SDS((R, C), f32), name=name)(parts)


def _pair_add(a, b, name):
    K, R, C = a.shape
    tr = 480 if R % 480 == 0 else R

    def body(a_ref, b_ref, o_ref):
        o_ref[...] = (a_ref[...].astype(f32) + b_ref[...].astype(f32)).astype(bf16)

    blk = pl.BlockSpec((1, tr, C), lambda k, i: (k, i, 0))
    return pl.pallas_call(
        body,
        grid=(K, R // tr),
        in_specs=[blk, blk],
        out_specs=blk,
        out_shape=SDS((K, R, C), bf16),
        compiler_params=_cparams(("parallel", "parallel")),
        name=name,
    )(a, b)


_ANY = pl.BlockSpec(memory_space=pl.ANY)


def _chip_comm(src, gather, name):
    blk_shape = src.shape if gather else src.shape[1:]

    def body(src_ref, out_ref, send_sems, recv_sems):
        x, y, c = lax.axis_index("x"), lax.axis_index("y"), lax.axis_index("c")
        mine = 2 * x + y
        peers = [(1 - x, y), (x, 1 - y), (1 - x, 1 - y)]

        def piece(k):
            return src_ref if gather else src_ref.at[k]

        sends = []
        for j, (px, py) in enumerate(peers):
            cp = pltpu.make_async_remote_copy(
                src_ref=piece(2 * px + py), dst_ref=out_ref.at[mine], send_sem=send_sems.at[j],
                recv_sem=recv_sems.at[j], device_id=(px, py, c), device_id_type=MESH)
            cp.start()
            sends.append(cp)
        for j, (px, py) in enumerate(peers):
            pltpu.make_async_remote_copy(
                src_ref=piece(mine), dst_ref=out_ref.at[2 * px + py], send_sem=send_sems.at[j],
                recv_sem=recv_sems.at[j], device_id=(px, py, c), device_id_type=MESH).wait_recv()
        for cp in sends:
            cp.wait_send()

    out = pl.pallas_call(
        body,
        in_specs=[_ANY],
        out_specs=_ANY,
        out_shape=SDS((4,) + tuple(blk_shape), src.dtype),
        scratch_shapes=[pltpu.SemaphoreType.DMA((3,)), pltpu.SemaphoreType.DMA((3,))],
        name=name,
    )(src)
    mine = 2 * lax.axis_index("x") + lax.axis_index("y")
    own = src if gather else lax.dynamic_index_in_dim(src, mine, axis=0, keepdims=False)
    return lax.dynamic_update_index_in_dim(out, own, mine, axis=0)


def _core_gather(src, name):
    def body(src_ref, out_ref, send_sem, recv_sem):
        x, y, c = lax.axis_index("x"), lax.axis_index("y"), lax.axis_index("c")
        cp = pltpu.make_async_remote_copy(src_ref=src_ref, dst_ref=out_ref.at[c], send_sem=send_sem,
                                          recv_sem=recv_sem, device_id=(x, y, 1 - c), device_id_type=MESH)
        cp.start()
        pltpu.make_async_remote_copy(src_ref=src_ref, dst_ref=out_ref.at[1 - c], send_sem=send_sem,
                                     recv_sem=recv_sem, device_id=(x, y, 1 - c), device_id_type=MESH).wait_recv()
        cp.wait_send()

    out = pl.pallas_call(
        body,
        in_specs=[_ANY],
        out_specs=_ANY,
        out_shape=SDS((2,) + tuple(src.shape), src.dtype),
        scratch_shapes=[pltpu.SemaphoreType.DMA, pltpu.SemaphoreType.DMA],
        name=name,
    )(src)
    return lax.dynamic_update_index_in_dim(out, src, lax.axis_index("c"), axis=0)


def _core_swap(src, name):
    def body(src_ref, out_ref, send_sem, recv_sem):
        x, y, c = lax.axis_index("x"), lax.axis_index("y"), lax.axis_index("c")
        cp = pltpu.make_async_remote_copy(src_ref=src_ref.at[1 - c], dst_ref=out_ref, send_sem=send_sem,
                                          recv_sem=recv_sem, device_id=(x, y, 1 - c), device_id_type=MESH)
        cp.start()
        cp.wait()

    return pl.pallas_call(
        body,
        in_specs=[_ANY],
        out_specs=_ANY,
        out_shape=SDS(tuple(src.shape[1:]), src.dtype),
        scratch_shapes=[pltpu.SemaphoreType.DMA, pltpu.SemaphoreType.DMA],
        name=name,
    )(src)


def _all_gather(src, tag):
    by_chip = _chip_comm(src, True, tag + "_chips")
    both = _core_gather(by_chip, tag + "_cores")
    return jnp.swapaxes(both, 0, 1).reshape((8,) + tuple(src.shape))


def _pack_rows(w_in, w_ba, w_bh, w_out, w_up, w_down):
    n = w_in.shape[0]
    return jnp.concatenate([t.reshape(n, -1, D_MODEL) for t in (w_in, w_ba, w_bh, w_out, w_up, w_down)], axis=1)


_PACK_SIZES = (("w_in", (1024, 1088)), ("w_ba", (512, 128)), ("w_bh", (512, 128)), ("w_out", (128, 1024)),
               ("w_up", (1024, 704)), ("w_down", (352, 1024)))


def _unpack_rows(slab):
    n = slab.shape[0]
    out, lo = {}, 0
    for key, (r, c) in _PACK_SIZES:
        rows = r * c // D_MODEL
        out[key] = slab[:, lo:lo + rows].reshape(n, r, c)
        lo += rows
    return out


def _cols_to_full(t):
    return jnp.swapaxes(t, 0, 1).reshape(t.shape[1], -1)


def _full_to_cols(t):
    K = t.shape[0]
    return jnp.swapaxes(t.reshape(K, 8, -1), 0, 1)


_SMALL = (("pre_mix_norm", (1, 1024)), ("rel_bias", (32, 24)), ("hgrn_lb_raw", (2, 512)), ("hgrn_norm", (1, 128)),
          ("post_mix_norm", (1, 1024)), ("pre_ffn_norm", (1, 1024)), ("conv_b", (1, 5632)),
          ("post_ffn_norm", (1, 1024)))
_SMALL_ROWS = 96
_CONVW_ROWS = 136


def _pack_small(d):
    flat = jnp.concatenate([d[k].reshape(-1) for k, _ in _SMALL])
    flat = jnp.pad(flat, (0, _SMALL_ROWS * LANE - flat.shape[0]))
    return flat.reshape(_SMALL_ROWS, LANE)


def _unpack_small(p):
    flat = p.reshape(-1)
    out, lo = {}, 0
    for k, shp in _SMALL:
        n = shp[0] * shp[1]
        out[k] = flat[lo:lo + n].reshape(shp)
        lo += n
    return out


def _local_step(x, tgt, W, P):
    S = x.shape[0]
    lb = _lb_fwd(P["hgrn_lb_raw"])
    xs = [x, _permute(x, 4, "perm_x4"), _permute(x, 16, "perm_x16")]
    hs = [_rmsnorm(xs[g], P["pre_mix_norm"], f"norm_pre{g}") for g in range(N_GROUPS)]
    h1 = hs[0]
    consts = [_bias_consts(d) for d in DILATIONS]
    qkv, obuf, lbuf, biases = [], [], [], []
    for g, d in enumerate(DILATIONS):
        qkv_g = _mm(hs[g], W["w_qkv"][g], "nn", bf16, f"proj_qkv{g}")
        tab_t = P["rel_bias"][:, 8 * g:8 * g + 8].T
        bias_g = _bias_build(tab_t, consts[g][0], consts[g][1], f"bias_build{g}").reshape(8, ATTN_BLOCK, 2 * ATTN_BLOCK)
        o_g, l_g = _attn_fwd(qkv_g, bias_g, (S // d) // ATTN_BLOCK, f"attn_fwd{g}")
        qkv.append(qkv_g)
        biases.append(bias_g)
        lbuf.append(l_g)
        obuf.append(_unpermute(o_g, d, f"unperm_o{g}"))
    l_nat = [_unpermute(lbuf[g], d, f"unperm_l{g}") for g, d in enumerate(DILATIONS)]
    y_attn, y_attn_b, w0, w1, w2 = _attn_merge(obuf[0], obuf[1], obuf[2], l_nat[0], l_nat[1], l_nat[2])
    hg = _mm(h1, W["w_hg"], "nn", f32, "proj_hg")
    gc = _mm(h1, W["w_gate"], "nn", bf16, "proj_gate")
    y_hgrn, o_raw, ck = _hgrn_fwd(hg, lb, P["hgrn_norm"])
    a = _mm(y_attn_b, W["w_ba"], "nn", bf16, "branch_attn")
    b = _mm(y_hgrn, W["w_bh"], "nn", bf16, "branch_hgrn")
    merged = _gate_fwd(a, b, gc)
    mo = _mm(merged, W["w_out"], "nn", f32, "out_proj")
    x1, h2 = _mid_fwd(x, mo, P["post_mix_norm"], P["pre_ffn_norm"])
    ug = _mm(h2, W["w_up_g"], "nn", bf16, "up_gate")
    uv = _mm(h2, W["w_up_v"], "nn", bf16, "up_val")
    cw_g, cw_v = P["conv_w"][:, :D_FF], P["conv_w"][:, D_FF:]
    cb_g, cb_v = P["conv_b"][:, :D_FF], P["conv_b"][:, D_FF:]
    act = _conv_fwd(ug, uv, cw_g, cw_v, cb_g, cb_v)
    fo = _mm(act, W["w_down"], "nn", f32, "down_proj")
    loss, dy, dfo, g_post_ffn = _final(x1, fo, tgt, P["post_ffn_norm"])
    dact = _mm(dfo, W["w_down"], "nt", bf16, "d_act")
    gW_down = _mm(act, dfo, "tn", f32, "gw_down")
    dug, duv, st_g, st_v = _conv_bwd(ug, uv, dact, cw_g, cw_v, cb_g, cb_v)
    dh2 = _mm(dug, W["w_up_g"], "nt", f32, "dh2_gate")
    dh2 = _mm(duv, W["w_up_v"], "nt", f32, "dh2_val", acc=dh2)
    gW_up = jnp.concatenate([_mm(h2, dug, "tn", f32, "gw_up_gate"), _mm(h2, duv, "tn", f32, "gw_up_val")], axis=1)
    dx1, dmo, g_pre_ffn, g_post_mix = _mid_bwd(dy, dh2, x1, mo, P["pre_ffn_norm"], P["post_mix_norm"])
    dmerged = _mm(dmo, W["w_out"], "nt", bf16, "d_merged")
    gW_out = _mm(merged, dmo, "tn", f32, "gw_out")
    da, db, dgc = _gate_bwd(dmerged, a, b, gc)
    dyattn = _mm(da, W["w_ba"], "nt", f32, "d_yattn")
    gW_ba = _mm(y_attn_b, da, "tn", f32, "gw_ba")
    dyhgrn = _mm(db, W["w_bh"], "nt", f32, "d_yhgrn")
    gW_bh = _mm(y_hgrn, db, "tn", f32, "gw_bh")
    dq_h, df_h, dv_h, dog_h, glb8, gnw8 = _hgrn_bwd(hg, o_raw, dyhgrn, ck, lb, P["hgrn_norm"])
    dhg = jnp.concatenate([dq_h, df_h, dv_h, dog_h], axis=1)
    g_lb_raw = _lb_bwd(P["hgrn_lb_raw"], glb8[0:1])
    gn = gnw8[0:1]
    g_hgrn_norm = (gn[:, 0:128] + gn[:, 128:256]) + (gn[:, 256:384] + gn[:, 384:512])
    dos = _attn_merge_bwd(dyattn, y_attn, w0, w1, w2)
    dh_parts, gW_qkv, g_rel = [], [], []
    for g, d in enumerate(DILATIONS):
        do_g = _permute(dos[g], d, f"perm_do{g}")
        dvec_g = _permute(dos[3 + g], d, f"perm_dvec{g}")
        dq, dk, dv, dbias = _attn_bwd(qkv[g], biases[g], do_g, dvec_g, lbuf[g], (S // d) // ATTN_BLOCK, f"attn_bwd{g}")
        dqkv = jnp.concatenate([dq, dk, dv], axis=1)
        gW_qkv.append(_mm(hs[g], dqkv, "tn", f32, f"gw_qkv{g}"))
        dh_g = _mm(dqkv, W["w_qkv"][g], "nt", f32, f"dh1_qkv{g}")
        dh_parts.append(_unpermute(dh_g, d, f"unperm_dh{g}"))
        g_rel.append(_bias_grad(dbias.reshape(8, -1), consts[g][0], f"bias_grad{g}"))
    dh_main = _mm(dhg, W["w_hg"], "nt", f32, "dh1_hg", acc=dh_parts[0])
    dh_main = _mm(dgc, W["w_gate"], "nt", f32, "dh1_gate", acc=dh_main)
    gW_hg = _mm(h1, dhg, "tn", f32, "gw_hg")
    gW_gate = _mm(h1, dgc, "tn", f32, "gw_gate")
    grad_x, g_pre_mix = _first_bwd(x, dx1, dh_main, dh_parts[1], dh_parts[2], P["pre_mix_norm"])

    big = dict(w_in=jnp.concatenate(gW_qkv + [gW_hg, gW_gate], axis=1), w_ba=gW_ba, w_bh=gW_bh, w_out=gW_out,
               w_up=gW_up, w_down=gW_down)
    g_conv_w = jnp.concatenate([st_g[0:3], st_v[0:3]], axis=1)
    g_conv_b = jnp.concatenate([st_g[3:4], st_v[3:4]], axis=1)
    small = dict(pre_mix_norm=g_pre_mix, rel_bias=jnp.concatenate(g_rel, axis=1), hgrn_lb_raw=g_lb_raw,
                 hgrn_norm=g_hgrn_norm, post_mix_norm=g_post_mix, pre_ffn_norm=g_pre_ffn, conv_b=g_conv_b,
                 post_ffn_norm=g_post_ffn, conv_w=g_conv_w)
    return loss, grad_x, big, small


def _full_weights(slabs):
    sh = _unpack_rows(slabs)
    w_in = _cols_to_full(sh["w_in"])
    w_up = _cols_to_full(sh["w_up"])
    return dict(
        w_qkv=[w_in[:, g * QKV_G:(g + 1) * QKV_G] for g in range(N_GROUPS)],
        w_hg=w_in[:, 3 * QKV_G:3 * QKV_G + 4 * HGRN_W],
        w_gate=w_in[:, 3 * QKV_G + 4 * HGRN_W:],
        w_ba=_cols_to_full(sh["w_ba"]),
        w_bh=_cols_to_full(sh["w_bh"]),
        w_out=sh["w_out"].reshape(D_MODEL, D_MODEL),
        w_up_g=w_up[:, :D_FF],
        w_up_v=w_up[:, D_FF:],
        w_down=sh["w_down"].reshape(D_FF, D_MODEL),
    )


def kernel(x, pre_mix_norm, w_in, rel_bias, hgrn_lb_raw, hgrn_norm, w_branch_attn, w_branch_hgrn, w_out, post_mix_norm, pre_ffn_norm, w_up, conv_w, conv_b, w_down, post_ffn_norm, loss_target, m_pre_mix_norm, m_w_in, m_rel_bias, m_hgrn_lb_raw, m_hgrn_norm, m_w_branch_attn, m_w_branch_hgrn, m_w_out, m_post_mix_norm, m_pre_ffn_norm, m_w_up, m_conv_w, m_conv_b, m_w_down, m_post_ffn_norm, v_pre_mix_norm, v_w_in, v_rel_bias, v_hgrn_lb_raw, v_hgrn_norm, v_w_branch_attn, v_w_branch_hgrn, v_w_out, v_post_mix_norm, v_pre_ffn_norm, v_w_up, v_conv_w, v_conv_b, v_w_down, v_post_ffn_norm):
    ci = lax.axis_index("c")
    dev = 4 * lax.axis_index("x") + 2 * lax.axis_index("y") + ci
    wts = dict(w_in=w_in[0], w_ba=w_branch_attn[0], w_bh=w_branch_hgrn[0], w_out=w_out[0], w_up=w_up[0],
               w_down=w_down[0])
    mom = dict(w_in=m_w_in[0], w_ba=m_w_branch_attn[0], w_bh=m_w_branch_hgrn[0], w_out=m_w_out[0], w_up=m_w_up[0],
               w_down=m_w_down[0])
    var = dict(w_in=v_w_in[0], w_ba=v_w_branch_attn[0], w_bh=v_w_branch_hgrn[0], w_out=v_w_out[0], w_up=v_w_up[0],
               w_down=v_w_down[0])
    small_w = dict(pre_mix_norm=pre_mix_norm, rel_bias=rel_bias, hgrn_lb_raw=hgrn_lb_raw, hgrn_norm=hgrn_norm,
                   post_mix_norm=post_mix_norm, pre_ffn_norm=pre_ffn_norm, conv_b=conv_b, post_ffn_norm=post_ffn_norm)
    small_m = dict(pre_mix_norm=m_pre_mix_norm, rel_bias=m_rel_bias, hgrn_lb_raw=m_hgrn_lb_raw, hgrn_norm=m_hgrn_norm,
                   post_mix_norm=m_post_mix_norm, pre_ffn_norm=m_pre_ffn_norm, conv_b=m_conv_b,
                   post_ffn_norm=m_post_ffn_norm)
    small_v = dict(pre_mix_norm=v_pre_mix_norm, rel_bias=v_rel_bias, hgrn_lb_raw=v_hgrn_lb_raw, hgrn_norm=v_hgrn_norm,
                   post_mix_norm=v_post_mix_norm, pre_ffn_norm=v_pre_ffn_norm, conv_b=v_conv_b,
                   post_ffn_norm=v_post_ffn_norm)

    slab = _pack_rows(*[wts[k].astype(bf16)[None] for k, _ in _PACK_SIZES])[0]
    W = _full_weights(_all_gather(slab, "ag_w"))
    cw_pad = jnp.pad(conv_w[0], ((0, SUBLANE - 3), (0, 768 - 704)))
    conv_w_full = _cols_to_full(_all_gather(cw_pad, "ag_convw")[:, 0:3, 0:704])
    P = dict(small_w)
    P["conv_w"] = conv_w_full

    loss8, grad_x, big, small = _local_step(x[0], loss_target[0], W, P)
    loss = lax.psum(loss8[0, 0], ("x", "y", "c"))

    cols = dict(w_in=_full_to_cols(big["w_in"]), w_ba=_full_to_cols(big["w_ba"]), w_bh=_full_to_cols(big["w_bh"]),
                w_out=big["w_out"].reshape(8, 128, D_MODEL), w_up=_full_to_cols(big["w_up"]),
                w_down=big["w_down"].reshape(8, 352, D_MODEL))
    gslab = _pack_rows(*[cols[k].astype(bf16) for k, _ in _PACK_SIZES])
    by_core = jnp.swapaxes(gslab.reshape(4, 2, 2400, D_MODEL), 0, 1)
    from_sib = _core_swap(by_core, "rs_cores")
    mine = lax.dynamic_index_in_dim(by_core, ci, axis=0, keepdims=False)
    chip_sum = _pair_add(mine, from_sib, "rs_pair_add")
    parts = _unpack_rows(_chip_comm(chip_sum, False, "rs_chips"))
    outs_big = {}
    for k, _ in _PACK_SIZES:
        outs_big[k] = _adamw(wts[k], mom[k], var[k], parts[k], "adamw_" + k)

    spack = jnp.concatenate([_pack_small(small),
                             jnp.pad(small["conv_w"].reshape(-1, LANE), ((0, _CONVW_ROWS - 132), (0, 0)))], axis=0)
    allp = _core_gather(_chip_comm(spack, True, "ag_small_chips"), "ag_small_cores")
    ssum = _sum8(allp, "small_sum")
    gs = ssum[:_SMALL_ROWS]
    res_small = _adamw(_pack_small(small_w), _pack_small(small_m), _pack_small(small_v), gs, "adamw_small")
    sm = [_unpack_small(t) for t in res_small]
    g_cw_full = ssum[_SMALL_ROWS:_SMALL_ROWS + 132].reshape(3, 2 * D_FF)
    g_cw = lax.dynamic_slice_in_dim(g_cw_full, dev * 704, 704, axis=1)
    res_cw = _adamw(conv_w[0], m_conv_w[0], v_conv_w[0], g_cw, "adamw_conv_w")

    def pick(i):
        def big_(k):
            return outs_big[k][i][None]
        return [sm[i]["pre_mix_norm"], big_("w_in"), sm[i]["rel_bias"], sm[i]["hgrn_lb_raw"], sm[i]["hgrn_norm"],
                big_("w_ba"), big_("w_bh"), big_("w_out"), sm[i]["post_mix_norm"], sm[i]["pre_ffn_norm"],
                big_("w_up"), res_cw[i][None], sm[i]["conv_b"], big_("w_down"), sm[i]["post_ffn_norm"]]

    return (loss, grad_x[None], *pick(0), *pick(1), *pick(2), *pick(3))
```

```python
import functools
import math

import jax
import jax.numpy as jnp
from jax import lax
from jax.experimental import pallas as pl
from jax.experimental.pallas import tpu as pltpu

f32 = jnp.float32
bf16 = jnp.bfloat16
SDS = jax.ShapeDtypeStruct
HIGHEST = lax.Precision.HIGHEST
MESH = pl.DeviceIdType.MESH

NN = (((1,), (0,)), ((), ()))
NT = (((1,), (1,)), ((), ()))
TN = (((0,), (0,)), ((), ()))

D_MODEL = 1024
N_GROUPS = 3
DILATIONS = (1, 4, 16)
HEAD_DIM = 64
ATTN_BLOCK = 128
QKV_G = 1536
ATTN_OUT = 512
HGRN_W = 512
HGRN_CHUNK = 32
D_FF = 2816
NUM_BUCKETS = 32
MAX_EXACT = 16
MAX_DISTANCE = 2048
NEG_INF = -1e30
EPS = 1e-6
LANE = 128
SUBLANE = 8
VMEM_BIG = 48 * 1024 * 1024
MM_ROWS = 512
MM_OUT_BYTES = 8 * 1024 * 1024

ADAM_LR, ADAM_B1, ADAM_B2, ADAM_EPS, ADAM_WD, ADAM_STEP = 0.001, 0.9, 0.999, 1e-08, 0.01, 10


def _pick(n, pref):
    t = pref
    while t >= LANE:
        if n % t == 0:
            return t
        t //= 2
    return n


def _cparams(sem=None, vmem=None):
    kw = {}
    if sem is not None:
        kw["dimension_semantics"] = sem
    if vmem is not None:
        kw["vmem_limit_bytes"] = vmem
    return pltpu.CompilerParams(**kw)


def _sigmoid(x):
    return jax.nn.sigmoid(x)


def _colsum8(x):
    return x.reshape(x.shape[0] // SUBLANE, SUBLANE, x.shape[1]).sum(axis=0)


def _mm(a, b, mode, out_dtype, name, acc=None):
    if mode == "nn":
        (M, K), (_, N) = a.shape, b.shape
    elif mode == "nt":
        (M, K), (N, _) = a.shape, b.shape
    else:
        (K, M), (_, N) = a.shape, b.shape
    dims = {"nn": NN, "nt": NT, "tn": TN}[mode]
    has_acc = acc is not None
    if mode == "tn":
        assert not has_acc
        tmm = M if M * N * 4 <= MM_OUT_BYTES else M // 2
        ts = _pick(K, MM_ROWS)
        nk = K // ts

        def body_tn(a_ref, b_ref, o_ref):
            k = pl.program_id(1)
            part = lax.dot_general(a_ref[...], b_ref[...], dims, preferred_element_type=f32)

            @pl.when(k == 0)
            def _():
                o_ref[...] = part

            @pl.when(k > 0)
            def _():
                o_ref[...] += part

        return pl.pallas_call(
            body_tn,
            grid=(M // tmm, nk),
            in_specs=[pl.BlockSpec((ts, tmm), lambda i, k: (k, i)), pl.BlockSpec((ts, N), lambda i, k: (k, 0))],
            out_specs=pl.BlockSpec((tmm, N), lambda i, k: (i, 0)),
            out_shape=SDS((M, N), out_dtype),
            compiler_params=_cparams(("parallel", "arbitrary"), VMEM_BIG),
            name=name,
        )(a, b)

    tm = _pick(M, MM_ROWS)

    def body(*refs):
        if has_acc:
            a_ref, b_ref, c_ref, o_ref = refs
        else:
            a_ref, b_ref, o_ref = refs
        part = lax.dot_general(a_ref[...], b_ref[...], dims, preferred_element_type=f32)
        if has_acc:
            part = part + c_ref[...]
        o_ref[...] = part.astype(out_dtype)

    specs = [pl.BlockSpec((tm, K), lambda i: (i, 0)), pl.BlockSpec(b.shape, lambda i: (0, 0))]
    args = [a, b]
    aliases = {}
    if has_acc:
        specs.append(pl.BlockSpec((tm, N), lambda i: (i, 0)))
        args.append(acc)
        aliases = {2: 0}
    return pl.pallas_call(
        body,
        grid=(M // tm,),
        in_specs=specs,
        out_specs=pl.BlockSpec((tm, N), lambda i: (i, 0)),
        out_shape=SDS((M, N), out_dtype),
        input_output_aliases=aliases,
        compiler_params=_cparams(("parallel",), VMEM_BIG),
        name=name,
    )(*args)


def _permute(x, d, name):
    if d == 1:
        return x
    S, C = x.shape
    U = S // d

    def body(x_ref, o_ref):
        for r in range(d):
            o_ref[r] = x_ref[pl.ds(r, ATTN_BLOCK, stride=d), :]

    out = pl.pallas_call(
        body,
        grid=(U // ATTN_BLOCK, C // LANE),
        in_specs=[pl.BlockSpec((ATTN_BLOCK * d, LANE), lambda i, j: (i, j))],
        out_specs=pl.BlockSpec((d, ATTN_BLOCK, LANE), lambda i, j: (0, i, j)),
        out_shape=SDS((d, U, C), x.dtype),
        compiler_params=_cparams(("parallel", "parallel")),
        name=name,
    )(x)
    return out.reshape(S, C)


def _unpermute(x, d, name):
    if d == 1:
        return x
    S, C = x.shape
    U = S // d

    def body(x_ref, o_ref):
        for r in range(d):
            o_ref[pl.ds(r, ATTN_BLOCK, stride=d), :] = x_ref[r]

    return pl.pallas_call(
        body,
        grid=(U // ATTN_BLOCK, C // LANE),
        in_specs=[pl.BlockSpec((d, ATTN_BLOCK, LANE), lambda i, j: (0, i, j))],
        out_specs=pl.BlockSpec((ATTN_BLOCK * d, LANE), lambda i, j: (i, j)),
        out_shape=SDS((S, C), x.dtype),
        compiler_params=_cparams(("parallel", "parallel")),
        name=name,
    )(x.reshape(d, U, C))


def _rms_parts(xv):
    r = lax.rsqrt(jnp.mean(xv * xv, axis=-1, keepdims=True) + EPS)
    return r, xv * r


def _rms_bwd(xhat, r, w, dy):
    dyw = dy * w
    return r * (dyw - xhat * jnp.mean(dyw * xhat, axis=-1, keepdims=True))


def _rmsnorm(x, w, name):
    S, D = x.shape
    tm = _pick(S, 512)

    def body(x_ref, w_ref, o_ref):
        _, xh = _rms_parts(x_ref[...])
        o_ref[...] = (xh * w_ref[...]).astype(bf16)

    return pl.pallas_call(
        body,
        grid=(S // tm,),
        in_specs=[pl.BlockSpec((tm, D), lambda i: (i, 0)), pl.BlockSpec((1, D), lambda i: (0, 0))],
        out_specs=pl.BlockSpec((tm, D), lambda i: (i, 0)),
        out_shape=SDS((S, D), bf16),
        compiler_params=_cparams(("parallel",)),
        name=name,
    )(x, w)


def _mid_fwd(x, mo, w_pm, w_pf):
    S, D = x.shape
    tm = _pick(S, 512)

    def body(x_ref, mo_ref, wpm_ref, wpf_ref, x1_ref, h2_ref):
        _, moh = _rms_parts(mo_ref[...])
        x1 = x_ref[...] + moh * wpm_ref[...]
        x1_ref[...] = x1
        _, x1h = _rms_parts(x1)
        h2_ref[...] = (x1h * wpf_ref[...]).astype(bf16)

    row = pl.BlockSpec((tm, D), lambda i: (i, 0))
    vec = pl.BlockSpec((1, D), lambda i: (0, 0))
    return pl.pallas_call(
        body,
        grid=(S // tm,),
        in_specs=[row, row, vec, vec],
        out_specs=[row, row],
        out_shape=[SDS((S, D), f32), SDS((S, D), bf16)],
        compiler_params=_cparams(("parallel",)),
        name="mid_fwd",
    )(x, mo, w_pm, w_pf)


def _final(x1, fo, tgt, w_pfn):
    S, D = x1.shape
    tm = _pick(S, 512)
    nt = S // tm

    def body(x1_ref, fo_ref, t_ref, w_ref, loss_ref, dy_ref, dfo_ref, gw_ref, lacc, gacc):
        i = pl.program_id(0)

        @pl.when(i == 0)
        def _():
            lacc[...] = jnp.zeros_like(lacc)
            gacc[...] = jnp.zeros_like(gacc)

        w = w_ref[...]
        r, foh = _rms_parts(fo_ref[...])
        y = x1_ref[...] + foh * w
        err = y - t_ref[...]
        lacc[...] += _colsum8(err * err)
        dy = err * (1.0 / D)
        dy_ref[...] = dy
        gacc[...] += _colsum8(dy * foh)
        dfo_ref[...] = _rms_bwd(foh, r, w, dy).astype(bf16)

        @pl.when(i == nt - 1)
        def _():
            loss_ref[...] = jnp.full((SUBLANE, LANE), 0.5 / D, f32) * jnp.sum(lacc[...])
            gw_ref[...] = jnp.sum(gacc[...], axis=0, keepdims=True)

    row = pl.BlockSpec((tm, D), lambda i: (i, 0))
    vec = pl.BlockSpec((1, D), lambda i: (0, 0))
    return pl.pallas_call(
        body,
        grid=(nt,),
        in_specs=[row, row, row, vec],
        out_specs=[pl.BlockSpec((SUBLANE, LANE), lambda i: (0, 0)), row, row, vec],
        out_shape=[SDS((SUBLANE, LANE), f32), SDS((S, D), f32), SDS((S, D), bf16), SDS((1, D), f32)],
        scratch_shapes=[pltpu.VMEM((SUBLANE, D), f32), pltpu.VMEM((SUBLANE, D), f32)],
        compiler_params=_cparams(("arbitrary",)),
        name="final_loss",
    )(x1, fo, tgt, w_pfn)


def _mid_bwd(dy, dh2, x1, mo, w_pf, w_pm):
    S, D = dy.shape
    tm = _pick(S, 512)
    nt = S // tm

    def body(dy_ref, dh2_ref, x1_ref, mo_ref, wpf_ref, wpm_ref, dx1_ref, dmo_ref, gpf_ref, gpm_ref, apf, apm):
        i = pl.program_id(0)

        @pl.when(i == 0)
        def _():
            apf[...] = jnp.zeros_like(apf)
            apm[...] = jnp.zeros_like(apm)

        r1, x1h = _rms_parts(x1_ref[...])
        dh2 = dh2_ref[...]
        apf[...] += _colsum8(dh2 * x1h)
        dx1 = dy_ref[...] + _rms_bwd(x1h, r1, wpf_ref[...], dh2)
        dx1_ref[...] = dx1
        rm, moh = _rms_parts(mo_ref[...])
        apm[...] += _colsum8(dx1 * moh)
        dmo_ref[...] = _rms_bwd(moh, rm, wpm_ref[...], dx1).astype(bf16)

        @pl.when(i == nt - 1)
        def _():
            gpf_ref[...] = jnp.sum(apf[...], axis=0, keepdims=True)
            gpm_ref[...] = jnp.sum(apm[...], axis=0, keepdims=True)

    row = pl.BlockSpec((tm, D), lambda i: (i, 0))
    vec = pl.BlockSpec((1, D), lambda i: (0, 0))
    return pl.pallas_call(
        body,
        grid=(nt,),
        in_specs=[row, row, row, row, vec, vec],
        out_specs=[row, row, vec, vec],
        out_shape=[SDS((S, D), f32), SDS((S, D), bf16), SDS((1, D), f32), SDS((1, D), f32)],
        scratch_shapes=[pltpu.VMEM((SUBLANE, D), f32), pltpu.VMEM((SUBLANE, D), f32)],
        compiler_params=_cparams(("arbitrary",)),
        name="mid_bwd",
    )(dy, dh2, x1, mo, w_pf, w_pm)


def _first_bwd(x, dx1, dh_a, dh_b, dh_c, w_pre):
    S, D = x.shape
    tm = _pick(S, 512)
    nt = S // tm

    def body(x_ref, dx1_ref, a_ref, b_ref, c_ref, w_ref, gx_ref, gw_ref, acc):
        i = pl.program_id(0)

        @pl.when(i == 0)
        def _():
            acc[...] = jnp.zeros_like(acc)

        r, xh = _rms_parts(x_ref[...])
        dh = (a_ref[...] + b_ref[...]) + c_ref[...]
        acc[...] += _colsum8(dh * xh)
        gx_ref[...] = dx1_ref[...] + _rms_bwd(xh, r, w_ref[...], dh)

        @pl.when(i == nt - 1)
        def _():
            gw_ref[...] = jnp.sum(acc[...], axis=0, keepdims=True)

    row = pl.BlockSpec((tm, D), lambda i: (i, 0))
    vec = pl.BlockSpec((1, D), lambda i: (0, 0))
    return pl.pallas_call(
        body,
        grid=(nt,),
        in_specs=[row, row, row, row, row, vec],
        out_specs=[row, vec],
        out_shape=[SDS((S, D), f32), SDS((1, D), f32)],
        scratch_shapes=[pltpu.VMEM((SUBLANE, D), f32)],
        compiler_params=_cparams(("arbitrary",)),
        name="first_bwd",
    )(x, dx1, dh_a, dh_b, dh_c, w_pre)


def _t5_bucket(dist):
    n = jnp.maximum(dist, 0)
    nf = jnp.maximum(n, 1).astype(f32)
    large = MAX_EXACT + (jnp.log(nf / MAX_EXACT) / math.log(MAX_DISTANCE / MAX_EXACT)
                         * (NUM_BUCKETS - MAX_EXACT)).astype(jnp.int32)
    large = jnp.minimum(large, NUM_BUCKETS - 1)
    return jnp.where(n < MAX_EXACT, n, large)


def _bias_consts(d):
    blk = ATTN_BLOCK
    rel = jnp.arange(blk)[:, None] + blk - jnp.arange(2 * blk)[None, :]
    in_win = (rel >= 0) & (rel <= blk)
    bucket = _t5_bucket(rel * d).reshape(1, -1)
    onehot = (bucket == jnp.arange(NUM_BUCKETS)[:, None]).astype(f32)
    return onehot, in_win.astype(f32).reshape(1, -1)


def _bias_build(tab_t, onehot, maskf, name):
    H = tab_t.shape[0]

    def body(t_ref, oh_ref, m_ref, o_ref):
        b = jnp.dot(t_ref[...], oh_ref[...], precision=HIGHEST, preferred_element_type=f32)
        o_ref[...] = jnp.where(m_ref[...] > 0.5, b, NEG_INF)

    return pl.pallas_call(body, out_shape=SDS((H, onehot.shape[1]), f32), name=name)(tab_t, onehot, maskf)


def _bias_grad(dbias_flat, onehot, name):
    H = dbias_flat.shape[0]

    def body(g_ref, oh_ref, o_ref):
        o_ref[...] = lax.dot_general(oh_ref[...], g_ref[...], NT, precision=HIGHEST, preferred_element_type=f32)

    return pl.pallas_call(body, out_shape=SDS((NUM_BUCKETS, H), f32), name=name)(dbias_flat, onehot)


ATTN_TILE = 512
ATTN_SUB = ATTN_TILE // ATTN_BLOCK


def _qkv_specs(nt):
    tile = (ATTN_TILE, LANE)
    blk = (ATTN_BLOCK, LANE)
    cur = lambda off: (lambda h, t: (jnp.minimum(t, nt - 1), off + h))
    prev = lambda off: (lambda h, t: (jnp.maximum(jnp.minimum(t, nt - 1) * ATTN_SUB - 1, 0), off + h))
    return [pl.BlockSpec(tile, cur(0)), pl.BlockSpec(blk, prev(4)), pl.BlockSpec(tile, cur(4)),
            pl.BlockSpec(blk, prev(8)), pl.BlockSpec(tile, cur(8))]


def _head_masks():
    lane = lax.broadcasted_iota(jnp.int32, (ATTN_BLOCK, LANE), 1)
    return lane < HEAD_DIM


def _attn_fwd(qkv, bias, bps, name):
    S = qkv.shape[0]
    nt = S // ATTN_TILE
    scale = HEAD_DIM ** -0.5

    def body(q_ref, kp_ref, kc_ref, vp_ref, vc_ref, b_ref, o_ref, l_ref):
        t = pl.program_id(1)
        kk = jnp.concatenate([kp_ref[...], kc_ref[...]], axis=0)
        vv = jnp.concatenate([vp_ref[...], vc_ref[...]], axis=0)
        low = _head_masks()
        col = lax.broadcasted_iota(jnp.int32, (ATTN_BLOCK, 2 * ATTN_BLOCK), 1)
        for b in range(ATTN_SUB):
            lo = b * ATTN_BLOCK
            rows = slice(lo, lo + ATTN_BLOCK)
            keys = slice(lo, lo + 2 * ATTN_BLOCK)
            dead = jnp.logical_and((t * ATTN_SUB + b) % bps == 0, col < ATTN_BLOCK)
            q2 = q_ref[rows, :]
            kb, vb = kk[keys], vv[keys]
            outs, lses = [], []
            for h in range(2):
                hm = low if h == 0 else jnp.logical_not(low)
                qh = jnp.where(hm, q2, jnp.zeros_like(q2))
                s = lax.dot_general(qh, kb, NT, preferred_element_type=f32) * scale + b_ref[h]
                s = jnp.where(dead, NEG_INF, s)
                m = jnp.max(s, axis=-1, keepdims=True)
                p = jnp.exp(s - m)
                l = jnp.sum(p, axis=-1, keepdims=True)
                outs.append(jnp.dot(p.astype(bf16), vb, preferred_element_type=f32) / l)
                lses.append(m + jnp.log(l))
            o_ref[rows, :] = jnp.where(low, outs[0], outs[1])
            l_ref[rows, :] = jnp.where(low, lses[0], lses[1])

    tile = pl.BlockSpec((ATTN_TILE, LANE), lambda h, t: (t, h))
    return pl.pallas_call(
        body,
        grid=(4, nt),
        in_specs=_qkv_specs(nt) + [pl.BlockSpec((2, ATTN_BLOCK, 2 * ATTN_BLOCK), lambda h, t: (h, 0, 0))],
        out_specs=[tile, tile],
        out_shape=[SDS((S, ATTN_OUT), f32), SDS((S, ATTN_OUT), f32)],
        compiler_params=_cparams(("parallel", "parallel")),
        name=name,
    )(qkv, qkv, qkv, qkv, qkv, bias)


def _attn_bwd(qkv, bias, do, dvec, lse, bps, name):
    S = qkv.shape[0]
    nt = S // ATTN_TILE
    scale = HEAD_DIM ** -0.5

    def assemble(parts):
        rows = [parts[0][:ATTN_BLOCK]]
        for b in range(ATTN_SUB - 1):
            rows.append(parts[b][ATTN_BLOCK:] + parts[b + 1][:ATTN_BLOCK])
        rows.append(parts[-1][ATTN_BLOCK:])
        return rows

    def body(q_ref, kp_ref, kc_ref, vp_ref, vc_ref, b_ref, do_ref, dvec_ref, lse_ref,
             dq_ref, dk_ref, dv_ref, db_ref, ck, cv):
        t = pl.program_id(1)
        last = ATTN_TILE - ATTN_BLOCK

        @pl.when(t == 0)
        def _():
            ck[...] = jnp.zeros_like(ck)
            cv[...] = jnp.zeros_like(cv)
            db_ref[...] = jnp.zeros_like(db_ref)

        @pl.when(t < nt)
        def _():
            kk = jnp.concatenate([kp_ref[...], kc_ref[...]], axis=0)
            vv = jnp.concatenate([vp_ref[...], vc_ref[...]], axis=0)
            low = _head_masks()
            col = lax.broadcasted_iota(jnp.int32, (ATTN_BLOCK, 2 * ATTN_BLOCK), 1)
            low2 = lax.broadcasted_iota(jnp.int32, (2 * ATTN_BLOCK, LANE), 1) < HEAD_DIM
            dk_parts, dv_parts = [], []
            dsum = [None, None]
            for b in range(ATTN_SUB):
                lo = b * ATTN_BLOCK
                rows = slice(lo, lo + ATTN_BLOCK)
                keys = slice(lo, lo + 2 * ATTN_BLOCK)
                dead = jnp.logical_and((t * ATTN_SUB + b) % bps == 0, col < ATTN_BLOCK)
                q2 = q_ref[rows, :]
                kb, vb = kk[keys], vv[keys]
                do2 = do_ref[rows, :].astype(bf16)
                dvec2 = dvec_ref[rows, :]
                lse2 = lse_ref[rows, :]
                dqs, dks, dvs = [], [], []
                for h in range(2):
                    hm = low if h == 0 else jnp.logical_not(low)
                    c0 = h * HEAD_DIM
                    qh = jnp.where(hm, q2, jnp.zeros_like(q2))
                    doh = jnp.where(hm, do2, jnp.zeros_like(do2))
                    s = lax.dot_general(qh, kb, NT, preferred_element_type=f32) * scale + b_ref[h]
                    s = jnp.where(dead, NEG_INF, s)
                    p = jnp.exp(s - lse2[:, c0:c0 + 1])
                    dp = lax.dot_general(doh, vb, NT, preferred_element_type=f32)
                    ds = p * (dp - dvec2[:, c0:c0 + 1])
                    dsum[h] = ds if dsum[h] is None else dsum[h] + ds
                    dsb = ds.astype(bf16)
                    dqs.append(jnp.dot(dsb, kb, preferred_element_type=f32) * scale)
                    dks.append(lax.dot_general(dsb, q2, TN, preferred_element_type=f32) * scale)
                    dvs.append(lax.dot_general(p.astype(bf16), do2, TN, preferred_element_type=f32))
                dq_ref[rows, :] = jnp.where(low, dqs[0], dqs[1]).astype(bf16)
                dk_parts.append(jnp.where(low2, dks[0], dks[1]))
                dv_parts.append(jnp.where(low2, dvs[0], dvs[1]))
            db_ref[0] += dsum[0]
            db_ref[1] += dsum[1]
            for parts, carry, out_ref in ((dk_parts, ck, dk_ref), (dv_parts, cv, dv_ref)):
                rws = assemble(parts)
                out_ref[:last, :] = carry[:last, :].astype(bf16)
                out_ref[last:, :] = (carry[last:, :] + rws[0]).astype(bf16)
                for b in range(ATTN_SUB):
                    carry[b * ATTN_BLOCK:(b + 1) * ATTN_BLOCK, :] = rws[b + 1]

        @pl.when(t == nt)
        def _():
            dk_ref[...] = ck[...].astype(bf16)
            dv_ref[...] = cv[...].astype(bf16)

    tile = (ATTN_TILE, LANE)
    cur = pl.BlockSpec(tile, lambda h, t: (jnp.minimum(t, nt - 1), h))
    lag = pl.BlockSpec(tile, lambda h, t: (jnp.maximum(t - 1, 0), h))
    bspec = pl.BlockSpec((2, ATTN_BLOCK, 2 * ATTN_BLOCK), lambda h, t: (h, 0, 0))
    return pl.pallas_call(
        body,
        grid=(4, nt + 1),
        in_specs=_qkv_specs(nt) + [bspec, cur, cur, cur],
        out_specs=[cur, lag, lag, bspec],
        out_shape=[SDS((S, ATTN_OUT), bf16), SDS((S, ATTN_OUT), bf16), SDS((S, ATTN_OUT), bf16),
                   SDS((8, ATTN_BLOCK, 2 * ATTN_BLOCK), f32)],
        scratch_shapes=[pltpu.VMEM(tile, f32), pltpu.VMEM(tile, f32)],
        compiler_params=_cparams(("parallel", "arbitrary")),
        name=name,
    )(qkv, qkv, qkv, qkv, qkv, bias, do, dvec, lse)


def _attn_merge(o0, o1, o2, l0, l1, l2):
    S, W = o0.shape
    tm = _pick(S, 512)

    def body(o0_ref, o1_ref, o2_ref, l0_ref, l1_ref, l2_ref, y_ref, yb_ref, w0_ref, w1_ref, w2_ref):
        a, b, c = l0_ref[...], l1_ref[...], l2_ref[...]
        m = jnp.maximum(jnp.maximum(a, b), c)
        ea, eb, ec = jnp.exp(a - m), jnp.exp(b - m), jnp.exp(c - m)
        den = (ea + eb) + ec
        w0, w1, w2 = ea / den, eb / den, ec / den
        y = (w0 * o0_ref[...] + w1 * o1_ref[...]) + w2 * o2_ref[...]
        y_ref[...] = y
        yb_ref[...] = y.astype(bf16)
        w0_ref[...] = w0
        w1_ref[...] = w1
        w2_ref[...] = w2

    row = pl.BlockSpec((tm, W), lambda i: (i, 0))
    return pl.pallas_call(
        body,
        grid=(S // tm,),
        in_specs=[row] * 6,
        out_specs=[row] * 5,
        out_shape=[SDS((S, W), f32), SDS((S, W), bf16)] + [SDS((S, W), f32)] * 3,
        compiler_params=_cparams(("parallel",)),
        name="attn_merge",
    )(o0, o1, o2, l0, l1, l2)


def _attn_merge_bwd(dy, y, w0, w1, w2):
    S, W = dy.shape
    tm = _pick(S, 512)

    def body(dy_ref, y_ref, w0_ref, w1_ref, w2_ref, a0, a1, a2, b0, b1, b2):
        dyv = dy_ref[...]
        r = lax.broadcasted_iota(jnp.int32, (LANE, LANE), 0) // HEAD_DIM
        c = lax.broadcasted_iota(jnp.int32, (LANE, LANE), 1) // HEAD_DIM
        seg = jnp.where(r == c, 1.0, 0.0).astype(f32)
        cbar = jnp.dot(dyv * y_ref[...], seg, precision=HIGHEST, preferred_element_type=f32)
        for w_ref, a_ref, b_ref in ((w0_ref, a0, b0), (w1_ref, a1, b1), (w2_ref, a2, b2)):
            w = w_ref[...]
            a_ref[...] = w * dyv
            b_ref[...] = w * cbar

    blk = pl.BlockSpec((tm, LANE), lambda i, j: (i, j))
    return pl.pallas_call(
        body,
        grid=(S // tm, W // LANE),
        in_specs=[blk] * 5,
        out_specs=[blk] * 6,
        out_shape=[SDS((S, W), f32)] * 6,
        compiler_params=_cparams(("parallel", "parallel")),
        name="attn_merge_bwd",
    )(dy, y, w0, w1, w2)


HGRN_SB = 256


def _chunk_masks(sb):
    r = lax.broadcasted_iota(jnp.int32, (sb, sb), 0)
    c = lax.broadcasted_iota(jnp.int32, (sb, sb), 1)
    same = (r // HGRN_CHUNK) == (c // HGRN_CHUNK)
    return same, jnp.logical_and(same, c <= r), jnp.logical_and(same, c >= r)


def _hgrn_prep(q_raw, f_raw, lbv, same, tril):
    sq = _sigmoid(q_raw)
    qs = q_raw * sq
    sig = _sigmoid(f_raw)
    f = lbv + (1.0 - lbv) * sig
    g = jnp.log(f)
    k = 1.0 - f
    G = jnp.dot(jnp.where(tril, 1.0, 0.0).astype(f32), g, precision=HIGHEST, preferred_element_type=f32)
    GL = jnp.dot(jnp.where(same, 1.0, 0.0).astype(f32), g, precision=HIGHEST, preferred_element_type=f32)
    eG = jnp.exp(G)
    einv = jnp.exp(-G)
    edec = jnp.exp(GL - G)
    return dict(sq=sq, qs=qs, sig=sig, f=f, k=k, eG=eG, einv=einv, edec=edec, eGL=jnp.exp(GL),
                qt=qs * eG, kt=k * einv, kd=k * edec)


def _hgrn_fwd(hg, lb, normw):
    S = hg.shape[0]
    sb = HGRN_SB
    nsb = S // sb
    nch = sb // HGRN_CHUNK

    def body(q_ref, f_ref, v_ref, og_ref, lb_ref, nw_ref, y_ref, o_ref, ck_ref, st):
        j = pl.program_id(1)

        @pl.when(j == 0)
        def _():
            st[...] = jnp.zeros_like(st)

        ST = st[...]
        ck_ref[0, 0] = ST
        same, tril, _ = _chunk_masks(sb)
        pr = _hgrn_prep(q_ref[...], f_ref[...], lb_ref[...], same, tril)
        qtb, ktb, kdb = pr["qt"].astype(bf16), pr["kt"].astype(bf16), pr["kd"].astype(bf16)
        eGL = pr["eGL"]
        vb = v_ref[...].astype(bf16)
        A = jnp.where(tril, lax.dot_general(qtb, ktb, NT, preferred_element_type=f32), 0.0)
        o = jnp.dot(A.astype(bf16), vb, preferred_element_type=f32)
        outs = []
        for ci in range(nch):
            lo = ci * HGRN_CHUNK
            sl = slice(lo, lo + HGRN_CHUNK)
            outs.append(o[sl] + lax.dot_general(qtb[sl], ST.astype(bf16), NT, preferred_element_type=f32))
            ST = ST * eGL[lo:lo + 1, :] + lax.dot_general(vb[sl], kdb[sl], TN, preferred_element_type=f32)
        st[...] = ST
        of = jnp.concatenate(outs, axis=0)
        o_ref[...] = of
        rms = lax.rsqrt(jnp.mean(of * of, axis=-1, keepdims=True) + EPS)
        ogv = og_ref[...]
        y_ref[...] = ((of * rms * nw_ref[...]) * (ogv * _sigmoid(ogv))).astype(bf16)

    col = lambda off: pl.BlockSpec((sb, LANE), lambda h, j: (j, off + h))
    return pl.pallas_call(
        body,
        grid=(4, nsb),
        in_specs=[col(0), col(4), col(8), col(12), pl.BlockSpec((1, LANE), lambda h, j: (0, h)),
                  pl.BlockSpec((1, LANE), lambda h, j: (0, 0))],
        out_specs=[col(0), col(0), pl.BlockSpec((1, 1, LANE, LANE), lambda h, j: (h, j, 0, 0))],
        out_shape=[SDS((S, HGRN_W), bf16), SDS((S, HGRN_W), f32), SDS((4, nsb, LANE, LANE), f32)],
        scratch_shapes=[pltpu.VMEM((LANE, LANE), f32)],
        compiler_params=_cparams(("parallel", "arbitrary")),
        name="hgrn_fwd",
    )(hg, hg, hg, hg, lb, normw)


def _hgrn_bwd(hg, o_raw, dy, ck, lb, normw):
    S = hg.shape[0]
    sb = HGRN_SB
    nsb = S // sb
    nch = sb // HGRN_CHUNK

    def body(q_ref, f_ref, v_ref, og_ref, o_ref, dy_ref, ck_ref, lb_ref, nw_ref,
             dq_ref, df_ref, dv_ref, dog_ref, glb_ref, gnw_ref, dst, alb, anw):
        j = pl.program_id(1)

        @pl.when(j == 0)
        def _():
            dst[...] = jnp.zeros_like(dst)
            alb[...] = jnp.zeros_like(alb)
            anw[...] = jnp.zeros_like(anw)

        same, tril, triu = _chunk_masks(sb)
        lbv = lb_ref[...]
        q_raw = q_ref[...]
        pr = _hgrn_prep(q_raw, f_ref[...], lbv, same, tril)
        qt, kt, kd, eGL = pr["qt"], pr["kt"], pr["kd"], pr["eGL"]
        qtb, ktb, kdb = qt.astype(bf16), kt.astype(bf16), kd.astype(bf16)
        vb = v_ref[...].astype(bf16)

        o = o_ref[...]
        ogv = og_ref[...]
        sog = _sigmoid(ogv)
        rms = lax.rsqrt(jnp.mean(o * o, axis=-1, keepdims=True) + EPS)
        oh = o * rms
        nw = nw_ref[...]
        dyv = dy_ref[...]
        dog_ref[...] = (dyv * (oh * nw) * (sog * (1.0 + ogv * (1.0 - sog)))).astype(bf16)
        dohw = dyv * (ogv * sog)
        anw[...] += _colsum8(dohw * oh)
        doh = dohw * nw
        do = rms * (doh - oh * jnp.mean(doh * oh, axis=-1, keepdims=True))
        dob = do.astype(bf16)

        Ab = jnp.where(tril, lax.dot_general(qtb, ktb, NT, preferred_element_type=f32), 0.0).astype(bf16)
        dAb = jnp.where(tril, lax.dot_general(dob, vb, NT, preferred_element_type=f32), 0.0).astype(bf16)
        dv_acc = lax.dot_general(Ab, dob, TN, preferred_element_type=f32)
        dqt = jnp.dot(dAb, ktb, preferred_element_type=f32)
        dkt = lax.dot_general(dAb, qtb, TN, preferred_element_type=f32)

        ST = ck_ref[0, 0]
        states = []
        for ci in range(nch):
            lo = ci * HGRN_CHUNK
            sl = slice(lo, lo + HGRN_CHUNK)
            states.append(ST)
            ST = ST * eGL[lo:lo + 1, :] + lax.dot_general(vb[sl], kdb[sl], TN, preferred_element_type=f32)

        dST = dst[...]
        dqt_i, dkd_i, dv_i, deg_i = [None] * nch, [None] * nch, [None] * nch, [None] * nch
        for ci in reversed(range(nch)):
            lo = ci * HGRN_CHUNK
            sl = slice(lo, lo + HGRN_CHUNK)
            ST0 = states[ci]
            dSTb = dST.astype(bf16)
            dv_i[ci] = lax.dot_general(kdb[sl], dSTb, NT, preferred_element_type=f32)
            dqt_i[ci] = jnp.dot(dob[sl], ST0.astype(bf16), preferred_element_type=f32)
            dkd_i[ci] = jnp.dot(vb[sl], dSTb, preferred_element_type=f32)
            deg_i[ci] = jnp.broadcast_to(jnp.sum(dST * ST0, axis=0, keepdims=True), (HGRN_CHUNK, LANE))
            dST = dST * eGL[lo:lo + 1, :] + lax.dot_general(dob[sl], qtb[sl], TN, preferred_element_type=f32)
        dst[...] = dST

        dqt = dqt + jnp.concatenate(dqt_i, axis=0)
        dkd = jnp.concatenate(dkd_i, axis=0)
        dv_ref[...] = (dv_acc + jnp.concatenate(dv_i, axis=0)).astype(bf16)
        deg = jnp.concatenate(deg_i, axis=0)

        dqs = dqt * pr["eG"]
        dkdkd = dkd * kd
        dG = dqt * qt - dkt * kt - dkdkd
        dk = dkt * pr["einv"] + dkd * pr["edec"]
        dGL = jnp.dot(jnp.where(same, 1.0, 0.0).astype(f32), dkdkd, precision=HIGHEST,
                      preferred_element_type=f32) + eGL * deg
        dg = jnp.dot(jnp.where(triu, 1.0, 0.0).astype(f32), dG, precision=HIGHEST,
                     preferred_element_type=f32) + dGL
        df = dg / pr["f"] - dk
        sig = pr["sig"]
        df_ref[...] = (df * (1.0 - lbv) * (sig * (1.0 - sig))).astype(bf16)
        alb[...] += _colsum8(df * (1.0 - sig))
        sq = pr["sq"]
        dq_ref[...] = (dqs * (sq * (1.0 + q_raw * (1.0 - sq)))).astype(bf16)

        @pl.when(j == nsb - 1)
        def _():
            glb_ref[...] = jnp.broadcast_to(jnp.sum(alb[...], axis=0, keepdims=True), (SUBLANE, LANE))
            gnw_ref[...] = jnp.broadcast_to(jnp.sum(anw[...], axis=0, keepdims=True), (SUBLANE, LANE))

    rev = lambda off: pl.BlockSpec((sb, LANE), lambda h, j: (nsb - 1 - j, off + h))
    stat = pl.BlockSpec((SUBLANE, LANE), lambda h, j: (0, h))
    return pl.pallas_call(
        body,
        grid=(4, nsb),
        in_specs=[rev(0), rev(4), rev(8), rev(12), rev(0), rev(0),
                  pl.BlockSpec((1, 1, LANE, LANE), lambda h, j: (h, nsb - 1 - j, 0, 0)),
                  pl.BlockSpec((1, LANE), lambda h, j: (0, h)), pl.BlockSpec((1, LANE), lambda h, j: (0, 0))],
        out_specs=[rev(0), rev(0), rev(0), rev(0), stat, stat],
        out_shape=[SDS((S, HGRN_W), bf16)] * 4 + [SDS((SUBLANE, HGRN_W), f32)] * 2,
        scratch_shapes=[pltpu.VMEM((LANE, LANE), f32), pltpu.VMEM((SUBLANE, LANE), f32),
                        pltpu.VMEM((SUBLANE, LANE), f32)],
        compiler_params=_cparams(("parallel", "arbitrary")),
        name="hgrn_bwd",
    )(hg, hg, hg, hg, o_raw, dy, ck, lb, normw)


def _lb_fwd(raw):
    def body(r_ref, o_ref):
        r = r_ref[...]
        m = jnp.max(r, axis=0, keepdims=True)
        e = jnp.exp(r - m)
        o_ref[...] = (e / jnp.sum(e, axis=0, keepdims=True))[0:1]

    return pl.pallas_call(body, out_shape=SDS((1, raw.shape[1]), f32), name="lb_fwd")(raw)


def _lb_bwd(raw, dlb):
    def body(r_ref, d_ref, o_ref):
        r = r_ref[...]
        m = jnp.max(r, axis=0, keepdims=True)
        e = jnp.exp(r - m)
        s = e / jnp.sum(e, axis=0, keepdims=True)
        s0 = s[0:1]
        onehot0 = jnp.where(lax.broadcasted_iota(jnp.int32, r.shape, 0) == 0, 1.0, 0.0)
        o_ref[...] = d_ref[...] * s0 * (onehot0 - s)

    return pl.pallas_call(body, out_shape=SDS(raw.shape, f32), name="lb_bwd")(raw, dlb)


def _gate_fwd(a, b, gc):
    S, D = a.shape
    tm = _pick(S, 512)

    def body(a_ref, b_ref, g0_ref, g1_ref, o_ref):
        s0, s1 = _sigmoid(g0_ref[...].astype(f32)), _sigmoid(g1_ref[...].astype(f32))
        o_ref[...] = (s0 * a_ref[...].astype(f32) + s1 * b_ref[...].astype(f32)).astype(bf16)

    row = pl.BlockSpec((tm, D), lambda i: (i, 0))
    return pl.pallas_call(
        body,
        grid=(S // tm,),
        in_specs=[row, row, row, pl.BlockSpec((tm, D), lambda i: (i, 1))],
        out_specs=row,
        out_shape=SDS((S, D), bf16),
        compiler_params=_cparams(("parallel",)),
        name="gate_fwd",
    )(a, b, gc, gc)


def _gate_bwd(dm, a, b, gc):
    S, D = a.shape
    tm = _pick(S, 512)

    def body(dm_ref, a_ref, b_ref, g0_ref, g1_ref, da_ref, db_ref, dg_ref):
        dmv = dm_ref[...].astype(f32)
        s0, s1 = _sigmoid(g0_ref[...].astype(f32)), _sigmoid(g1_ref[...].astype(f32))
        da_ref[...] = (dmv * s0).astype(bf16)
        db_ref[...] = (dmv * s1).astype(bf16)
        dg_ref[:, :D] = (dmv * a_ref[...].astype(f32) * (s0 * (1.0 - s0))).astype(bf16)
        dg_ref[:, D:] = (dmv * b_ref[...].astype(f32) * (s1 * (1.0 - s1))).astype(bf16)

    row = pl.BlockSpec((tm, D), lambda i: (i, 0))
    wide = pl.BlockSpec((tm, 2 * D), lambda i: (i, 0))
    return pl.pallas_call(
        body,
        grid=(S // tm,),
        in_specs=[row, row, row, row, pl.BlockSpec((tm, D), lambda i: (i, 1))],
        out_specs=[row, row, wide],
        out_shape=[SDS((S, D), bf16), SDS((S, D), bf16), SDS((S, 2 * D), bf16)],
        compiler_params=_cparams(("parallel",)),
        name="gate_bwd",
    )(dm, a, b, gc, gc)


CONV_ROWS = 512
INV_SQRT2 = 0.7071067811865476
INV_SQRT_2PI = 0.3989422804014327


CONV_HALO = 16


def _tile8(a, rows):
    return jnp.tile(a, (rows // a.shape[0], 1))


def _conv_rows(u_ref, w, b, r0, first):
    R = CONV_ROWS
    cur = u_ref[pl.ds(r0, R), :].astype(f32)
    prev8 = u_ref[pl.ds(pl.multiple_of(jnp.maximum(r0 - CONV_HALO, 0), CONV_HALO), CONV_HALO), :].astype(f32)
    prev8 = jnp.where(first, 0.0, prev8)
    row = lax.broadcasted_iota(jnp.int32, (R, LANE), 0)
    x1 = jnp.where(row < 1, _tile8(pltpu.roll(prev8, 1, 0), R), pltpu.roll(cur, 1, 0))
    x2 = jnp.where(row < 2, _tile8(pltpu.roll(prev8, 2, 0), R), pltpu.roll(cur, 2, 0))
    c = ((b + w[0:1] * x2) + w[1:2] * x1) + w[2:3] * cur
    return c, x2, x1, cur


def _conv_fwd(ug, uv, wg, wv, bg, bv):
    S, F = ug.shape
    nchunk = S // CONV_ROWS

    def body(ug_ref, uv_ref, wg_ref, wv_ref, bg_ref, bv_ref, o_ref):
        wgv, wvv, bgv, bvv = wg_ref[...], wv_ref[...], bg_ref[...], bv_ref[...]

        def step(ci, carry):
            r0 = pl.multiple_of(ci * CONV_ROWS, CONV_ROWS)
            cg = _conv_rows(ug_ref, wgv, bgv, r0, ci == 0)[0]
            cv = _conv_rows(uv_ref, wvv, bvv, r0, ci == 0)[0]
            gelu = 0.5 * cg * (1.0 + lax.erf(cg * INV_SQRT2))
            o_ref[pl.ds(r0, CONV_ROWS), :] = (gelu * cv).astype(bf16)
            return carry

        lax.fori_loop(0, nchunk, step, 0)

    col = pl.BlockSpec((S, LANE), lambda j: (0, j))
    w3 = pl.BlockSpec((3, LANE), lambda j: (0, j))
    b1 = pl.BlockSpec((1, LANE), lambda j: (0, j))
    return pl.pallas_call(
        body,
        grid=(F // LANE,),
        in_specs=[col, col, w3, w3, b1, b1],
        out_specs=col,
        out_shape=SDS((S, F), bf16),
        compiler_params=_cparams(("parallel",), VMEM_BIG),
        name="conv_fwd",
    )(ug, uv, wg, wv, bg, bv)


def _conv_bwd(ug, uv, dact, wg, wv, bg, bv):
    S, F = ug.shape
    R = CONV_ROWS
    nchunk = S // R

    def body(ug_ref, uv_ref, da_ref, wg_ref, wv_ref, bg_ref, bv_ref, dug_ref, duv_ref, sg_ref, sv_ref, dcg, dcv):
        wgv, wvv, bgv, bvv = wg_ref[...], wv_ref[...], bg_ref[...], bv_ref[...]
        zero = jnp.zeros((SUBLANE, LANE), f32)

        def fwd_step(ci, acc):
            r0 = pl.multiple_of(ci * R, R)
            cg, g2, g1, g0 = _conv_rows(ug_ref, wgv, bgv, r0, ci == 0)
            cv, v2, v1, v0 = _conv_rows(uv_ref, wvv, bvv, r0, ci == 0)
            da = da_ref[pl.ds(r0, R), :].astype(f32)
            cdf = 0.5 * (1.0 + lax.erf(cg * INV_SQRT2))
            pdf = INV_SQRT_2PI * jnp.exp(-0.5 * cg * cg)
            dg = da * cv * (cdf + cg * pdf)
            dv = da * (cg * cdf)
            dcg[pl.ds(r0, R), :] = dg
            dcv[pl.ds(r0, R), :] = dv
            new = (acc[0] + _colsum8(dg * g2), acc[1] + _colsum8(dg * g1), acc[2] + _colsum8(dg * g0),
                   acc[3] + _colsum8(dg),
                   acc[4] + _colsum8(dv * v2), acc[5] + _colsum8(dv * v1), acc[6] + _colsum8(dv * v0),
                   acc[7] + _colsum8(dv))
            return new

        acc = lax.fori_loop(0, nchunk, fwd_step, (zero,) * 8)
        rows = lax.broadcasted_iota(jnp.int32, (SUBLANE, LANE), 0)

        def stats(parts):
            out = jnp.zeros((SUBLANE, LANE), f32)
            for k, pt in enumerate(parts):
                out = jnp.where(rows == k, jnp.sum(pt, axis=0, keepdims=True), out)
            return out

        sg_ref[...] = stats(acc[0:4])
        sv_ref[...] = stats(acc[4:8])

        def du_rows(dc, w, r0, last):
            cur = dc[pl.ds(r0, R), :]
            nxt = dc[pl.ds(pl.multiple_of(jnp.minimum(r0 + R, S - SUBLANE), SUBLANE), SUBLANE), :]
            nxt = jnp.where(last, 0.0, nxt)
            row = lax.broadcasted_iota(jnp.int32, (R, LANE), 0)
            y1 = jnp.where(row >= R - 1, _tile8(pltpu.roll(nxt, SUBLANE - 1, 0), R), pltpu.roll(cur, R - 1, 0))
            y2 = jnp.where(row >= R - 2, _tile8(pltpu.roll(nxt, SUBLANE - 2, 0), R), pltpu.roll(cur, R - 2, 0))
            return w[2:3] * cur + w[1:2] * y1 + w[0:1] * y2

        def bwd_step(ci, carry):
            r0 = pl.multiple_of(ci * R, R)
            last = ci == nchunk - 1
            dug_ref[pl.ds(r0, R), :] = du_rows(dcg, wgv, r0, last).astype(bf16)
            duv_ref[pl.ds(r0, R), :] = du_rows(dcv, wvv, r0, last).astype(bf16)
            return carry

        lax.fori_loop(0, nchunk, bwd_step, 0)

    col = pl.BlockSpec((S, LANE), lambda j: (0, j))
    w3 = pl.BlockSpec((3, LANE), lambda j: (0, j))
    b1 = pl.BlockSpec((1, LANE), lambda j: (0, j))
    st = pl.BlockSpec((SUBLANE, LANE), lambda j: (0, j))
    return pl.pallas_call(
        body,
        grid=(F // LANE,),
        in_specs=[col, col, col, w3, w3, b1, b1],
        out_specs=[col, col, st, st],
        out_shape=[SDS((S, F), bf16), SDS((S, F), bf16), SDS((SUBLANE, F), f32), SDS((SUBLANE, F), f32)],
        scratch_shapes=[pltpu.VMEM((S, LANE), f32), pltpu.VMEM((S, LANE), f32)],
        compiler_params=_cparams(("parallel",), VMEM_BIG),
        name="conv_bwd",
    )(ug, uv, dact, wg, wv, bg, bv)


def _adam_math(w, g, m, v):
    m = ADAM_B1 * m + (1.0 - ADAM_B1) * g
    v = ADAM_B2 * v + (1.0 - ADAM_B2) * (g * g)
    m_hat = m / (1.0 - ADAM_B1 ** ADAM_STEP)
    v_hat = v / (1.0 - ADAM_B2 ** ADAM_STEP)
    delta = -ADAM_LR * (m_hat / (jnp.sqrt(v_hat) + ADAM_EPS) + ADAM_WD * w)
    return delta, m, v


def _adamw(w, m, v, g, name):
    R, C = w.shape
    parts = g.ndim == 3
    tr = R
    for t in (256, 128, 64, 32, 16):
        if R % t == 0 and R > t:
            tr = t
            break

    def body(w_ref, m_ref, v_ref, g_ref, go_ref, d_ref, mo_ref, vo_ref):
        if parts:
            gv = ((g_ref[0].astype(f32) + g_ref[1].astype(f32)) + g_ref[2].astype(f32)) + g_ref[3].astype(f32)
        else:
            gv = g_ref[...]
        go_ref[...] = gv
        d, mn, vn = _adam_math(w_ref[...], gv, m_ref[...], v_ref[...])
        d_ref[...] = d
        mo_ref[...] = mn
        vo_ref[...] = vn

    row = pl.BlockSpec((tr, C), lambda i: (i, 0))
    gspec = pl.BlockSpec((4, tr, C), lambda i: (0, i, 0)) if parts else row
    return pl.pallas_call(
        body,
        grid=(R // tr,),
        in_specs=[row, row, row, gspec],
        out_specs=[row] * 4,
        out_shape=[SDS((R, C), f32)] * 4,
        compiler_params=_cparams(("parallel",)),
        name=name,
    )(w, m, v, g)


def _sum8(parts, name):
    _, _, R, C = parts.shape

    def body(p_ref, o_ref):
        acc = p_ref[0, 0]
        for c in range(2):
            for k in range(4):
                if c or k:
                    acc = acc + p_ref[c, k]
        o_ref[...] = acc

    return pl.pallas_call(body, out_shape=SDS((R, C), f32), name=name)(parts)


def _pair_add(a, b, name):
    K, R, C = a.shape
    tr = 480 if R % 480 == 0 else R

    def body(a_ref, b_ref, o_ref):
        o_ref[...] = (a_ref[...].astype(f32) + b_ref[...].astype(f32)).astype(bf16)

    blk = pl.BlockSpec((1, tr, C), lambda k, i: (k, i, 0))
    return pl.pallas_call(
        body,
        grid=(K, R // tr),
        in_specs=[blk, blk],
        out_specs=blk,
        out_shape=SDS((K, R, C), bf16),
        compiler_params=_cparams(("parallel", "parallel")),
        name=name,
    )(a, b)


_ANY = pl.BlockSpec(memory_space=pl.ANY)


def _chip_comm(src, gather, name):
    blk_shape = src.shape if gather else src.shape[1:]

    def body(src_ref, out_ref, send_sems, recv_sems):
        x, y, c = lax.axis_index("x"), lax.axis_index("y"), lax.axis_index("c")
        mine = 2 * x + y
        peers = [(1 - x, y), (x, 1 - y), (1 - x, 1 - y)]

        def piece(k):
            return src_ref if gather else src_ref.at[k]

        sends = []
        for j, (px, py) in enumerate(peers):
            cp = pltpu.make_async_remote_copy(
                src_ref=piece(2 * px + py), dst_ref=out_ref.at[mine], send_sem=send_sems.at[j],
                recv_sem=recv_sems.at[j], device_id=(px, py, c), device_id_type=MESH)
            cp.start()
            sends.append(cp)
        for j, (px, py) in enumerate(peers):
            pltpu.make_async_remote_copy(
                src_ref=piece(mine), dst_ref=out_ref.at[2 * px + py], send_sem=send_sems.at[j],
                recv_sem=recv_sems.at[j], device_id=(px, py, c), device_id_type=MESH).wait_recv()
        for cp in sends:
            cp.wait_send()

    out = pl.pallas_call(
        body,
        in_specs=[_ANY],
        out_specs=_ANY,
        out_shape=SDS((4,) + tuple(blk_shape), src.dtype),
        scratch_shapes=[pltpu.SemaphoreType.DMA((3,)), pltpu.SemaphoreType.DMA((3,))],
        name=name,
    )(src)
    mine = 2 * lax.axis_index("x") + lax.axis_index("y")
    own = src if gather else lax.dynamic_index_in_dim(src, mine, axis=0, keepdims=False)
    return lax.dynamic_update_index_in_dim(out, own, mine, axis=0)


def _core_gather(src, name):
    def body(src_ref, out_ref, send_sem, recv_sem):
        x, y, c = lax.axis_index("x"), lax.axis_index("y"), lax.axis_index("c")
        cp = pltpu.make_async_remote_copy(src_ref=src_ref, dst_ref=out_ref.at[c], send_sem=send_sem,
                                          recv_sem=recv_sem, device_id=(x, y, 1 - c), device_id_type=MESH)
        cp.start()
        pltpu.make_async_remote_copy(src_ref=src_ref, dst_ref=out_ref.at[1 - c], send_sem=send_sem,
                                     recv_sem=recv_sem, device_id=(x, y, 1 - c), device_id_type=MESH).wait_recv()
        cp.wait_send()

    out = pl.pallas_call(
        body,
        in_specs=[_ANY],
        out_specs=_ANY,
        out_shape=SDS((2,) + tuple(src.shape), src.dtype),
        scratch_shapes=[pltpu.SemaphoreType.DMA, pltpu.SemaphoreType.DMA],
        name=name,
    )(src)
    return lax.dynamic_update_index_in_dim(out, src, lax.axis_index("c"), axis=0)


def _core_swap(src, name):
    def body(src_ref, out_ref, send_sem, recv_sem):
        x, y, c = lax.axis_index("x"), lax.axis_index("y"), lax.axis_index("c")
        cp = pltpu.make_async_remote_copy(src_ref=src_ref.at[1 - c], dst_ref=out_ref, send_sem=send_sem,
                                          recv_sem=recv_sem, device_id=(x, y, 1 - c), device_id_type=MESH)
        cp.start()
        cp.wait()

    return pl.pallas_call(
        body,
        in_specs=[_ANY],
        out_specs=_ANY,
        out_shape=SDS(tuple(src.shape[1:]), src.dtype),
        scratch_shapes=[pltpu.SemaphoreType.DMA, pltpu.SemaphoreType.DMA],
        name=name,
    )(src)


def _all_gather(src, tag):
    by_chip = _chip_comm(src, True, tag + "_chips")
    both = _core_gather(by_chip, tag + "_cores")
    return jnp.swapaxes(both, 0, 1).reshape((8,) + tuple(src.shape))


def _pack_rows(w_in, w_ba, w_bh, w_out, w_up, w_down):
    n = w_in.shape[0]
    return jnp.concatenate([t.reshape(n, -1, D_MODEL) for t in (w_in, w_ba, w_bh, w_out, w_up, w_down)], axis=1)


_PACK_SIZES = (("w_in", (1024, 1088)), ("w_ba", (512, 128)), ("w_bh", (512, 128)), ("w_out", (128, 1024)),
               ("w_up", (1024, 704)), ("w_down", (352, 1024)))


def _unpack_rows(slab):
    n = slab.shape[0]
    out, lo = {}, 0
    for key, (r, c) in _PACK_SIZES:
        rows = r * c // D_MODEL
        out[key] = slab[:, lo:lo + rows].reshape(n, r, c)
        lo += rows
    return out


def _cols_to_full(t):
    return jnp.swapaxes(t, 0, 1).reshape(t.shape[1], -1)


def _full_to_cols(t):
    K = t.shape[0]
    return jnp.swapaxes(t.reshape(K, 8, -1), 0, 1)


_SMALL = (("pre_mix_norm", (1, 1024)), ("rel_bias", (32, 24)), ("hgrn_lb_raw", (2, 512)), ("hgrn_norm", (1, 128)),
          ("post_mix_norm", (1, 1024)), ("pre_ffn_norm", (1, 1024)), ("conv_b", (1, 5632)),
          ("post_ffn_norm", (1, 1024)))
_SMALL_ROWS = 96
_CONVW_ROWS = 136


def _pack_small(d):
    flat = jnp.concatenate([d[k].reshape(-1) for k, _ in _SMALL])
    flat = jnp.pad(flat, (0, _SMALL_ROWS * LANE - flat.shape[0]))
    return flat.reshape(_SMALL_ROWS, LANE)


def _unpack_small(p):
    flat = p.reshape(-1)
    out, lo = {}, 0
    for k, shp in _SMALL:
        n = shp[0] * shp[1]
        out[k] = flat[lo:lo + n].reshape(shp)
        lo += n
    return out


def _local_step(x, tgt, W, P):
    S = x.shape[0]
    lb = _lb_fwd(P["hgrn_lb_raw"])
    xs = [x, _permute(x, 4, "perm_x4"), _permute(x, 16, "perm_x16")]
    hs = [_rmsnorm(xs[g], P["pre_mix_norm"], f"norm_pre{g}") for g in range(N_GROUPS)]
    h1 = hs[0]
    consts = [_bias_consts(d) for d in DILATIONS]
    qkv, obuf, lbuf, biases = [], [], [], []
    for g, d in enumerate(DILATIONS):
        qkv_g = _mm(hs[g], W["w_qkv"][g], "nn", bf16, f"proj_qkv{g}")
        tab_t = P["rel_bias"][:, 8 * g:8 * g + 8].T
        bias_g = _bias_build(tab_t, consts[g][0], consts[g][1], f"bias_build{g}").reshape(8, ATTN_BLOCK, 2 * ATTN_BLOCK)
        o_g, l_g = _attn_fwd(qkv_g, bias_g, (S // d) // ATTN_BLOCK, f"attn_fwd{g}")
        qkv.append(qkv_g)
        biases.append(bias_g)
        lbuf.append(l_g)
        obuf.append(_unpermute(o_g, d, f"unperm_o{g}"))
    l_nat = [_unpermute(lbuf[g], d, f"unperm_l{g}") for g, d in enumerate(DILATIONS)]
    y_attn, y_attn_b, w0, w1, w2 = _attn_merge(obuf[0], obuf[1], obuf[2], l_nat[0], l_nat[1], l_nat[2])
    hg = _mm(h1, W["w_hg"], "nn", f32, "proj_hg")
    gc = _mm(h1, W["w_gate"], "nn", bf16, "proj_gate")
    y_hgrn, o_raw, ck = _hgrn_fwd(hg, lb, P["hgrn_norm"])
    a = _mm(y_attn_b, W["w_ba"], "nn", bf16, "branch_attn")
    b = _mm(y_hgrn, W["w_bh"], "nn", bf16, "branch_hgrn")
    merged = _gate_fwd(a, b, gc)
    mo = _mm(merged, W["w_out"], "nn", f32, "out_proj")
    x1, h2 = _mid_fwd(x, mo, P["post_mix_norm"], P["pre_ffn_norm"])
    ug = _mm(h2, W["w_up_g"], "nn", bf16, "up_gate")
    uv = _mm(h2, W["w_up_v"], "nn", bf16, "up_val")
    cw_g, cw_v = P["conv_w"][:, :D_FF], P["conv_w"][:, D_FF:]
    cb_g, cb_v = P["conv_b"][:, :D_FF], P["conv_b"][:, D_FF:]
    act = _conv_fwd(ug, uv, cw_g, cw_v, cb_g, cb_v)
    fo = _mm(act, W["w_down"], "nn", f32, "down_proj")
    loss, dy, dfo, g_post_ffn = _final(x1, fo, tgt, P["post_ffn_norm"])
    dact = _mm(dfo, W["w_down"], "nt", bf16, "d_act")
    gW_down = _mm(act, dfo, "tn", f32, "gw_down")
    dug, duv, st_g, st_v = _conv_bwd(ug, uv, dact, cw_g, cw_v, cb_g, cb_v)
    dh2 = _mm(dug, W["w_up_g"], "nt", f32, "dh2_gate")
    dh2 = _mm(duv, W["w_up_v"], "nt", f32, "dh2_val", acc=dh2)
    gW_up = jnp.concatenate([_mm(h2, dug, "tn", f32, "gw_up_gate"), _mm(h2, duv, "tn", f32, "gw_up_val")], axis=1)
    dx1, dmo, g_pre_ffn, g_post_mix = _mid_bwd(dy, dh2, x1, mo, P["pre_ffn_norm"], P["post_mix_norm"])
    dmerged = _mm(dmo, W["w_out"], "nt", bf16, "d_merged")
    gW_out = _mm(merged, dmo, "tn", f32, "gw_out")
    da, db, dgc = _gate_bwd(dmerged, a, b, gc)
    dyattn = _mm(da, W["w_ba"], "nt", f32, "d_yattn")
    gW_ba = _mm(y_attn_b, da, "tn", f32, "gw_ba")
    dyhgrn = _mm(db, W["w_bh"], "nt", f32, "d_yhgrn")
    gW_bh = _mm(y_hgrn, db, "tn", f32, "gw_bh")
    dq_h, df_h, dv_h, dog_h, glb8, gnw8 = _hgrn_bwd(hg, o_raw, dyhgrn, ck, lb, P["hgrn_norm"])
    dhg = jnp.concatenate([dq_h, df_h, dv_h, dog_h], axis=1)
    g_lb_raw = _lb_bwd(P["hgrn_lb_raw"], glb8[0:1])
    gn = gnw8[0:1]
    g_hgrn_norm = (gn[:, 0:128] + gn[:, 128:256]) + (gn[:, 256:384] + gn[:, 384:512])
    dos = _attn_merge_bwd(dyattn, y_attn, w0, w1, w2)
    dh_parts, gW_qkv, g_rel = [], [], []
    for g, d in enumerate(DILATIONS):
        do_g = _permute(dos[g], d, f"perm_do{g}")
        dvec_g = _permute(dos[3 + g], d, f"perm_dvec{g}")
        dq, dk, dv, dbias = _attn_bwd(qkv[g], biases[g], do_g, dvec_g, lbuf[g], (S // d) // ATTN_BLOCK, f"attn_bwd{g}")
        dqkv = jnp.concatenate([dq, dk, dv], axis=1)
        gW_qkv.append(_mm(hs[g], dqkv, "tn", f32, f"gw_qkv{g}"))
        dh_g = _mm(dqkv, W["w_qkv"][g], "nt", f32, f"dh1_qkv{g}")
        dh_parts.append(_unpermute(dh_g, d, f"unperm_dh{g}"))
        g_rel.append(_bias_grad(dbias.reshape(8, -1), consts[g][0], f"bias_grad{g}"))
    dh_main = _mm(dhg, W["w_hg"], "nt", f32, "dh1_hg", acc=dh_parts[0])
    dh_main = _mm(dgc, W["w_gate"], "nt", f32, "dh1_gate", acc=dh_main)
    gW_hg = _mm(h1, dhg, "tn", f32, "gw_hg")
    gW_gate = _mm(h1, dgc, "tn", f32, "gw_gate")
    grad_x, g_pre_mix = _first_bwd(x, dx1, dh_main, dh_parts[1], dh_parts[2], P["pre_mix_norm"])

    big = dict(w_in=jnp.concatenate(gW_qkv + [gW_hg, gW_gate], axis=1), w_ba=gW_ba, w_bh=gW_bh, w_out=gW_out,
               w_up=gW_up, w_down=gW_down)
    g_conv_w = jnp.concatenate([st_g[0:3], st_v[0:3]], axis=1)
    g_conv_b = jnp.concatenate([st_g[3:4], st_v[3:4]], axis=1)
    small = dict(pre_mix_norm=g_pre_mix, rel_bias=jnp.concatenate(g_rel, axis=1), hgrn_lb_raw=g_lb_raw,
                 hgrn_norm=g_hgrn_norm, post_mix_norm=g_post_mix, pre_ffn_norm=g_pre_ffn, conv_b=g_conv_b,
                 post_ffn_norm=g_post_ffn, conv_w=g_conv_w)
    return loss, grad_x, big, small


def _full_weights(slabs):
    sh = _unpack_rows(slabs)
    w_in = _cols_to_full(sh["w_in"])
    w_up = _cols_to_full(sh["w_up"])
    return dict(
        w_qkv=[w_in[:, g * QKV_G:(g + 1) * QKV_G] for g in range(N_GROUPS)],
        w_hg=w_in[:, 3 * QKV_G:3 * QKV_G + 4 * HGRN_W],
        w_gate=w_in[:, 3 * QKV_G + 4 * HGRN_W:],
        w_ba=_cols_to_full(sh["w_ba"]),
        w_bh=_cols_to_full(sh["w_bh"]),
        w_out=sh["w_out"].reshape(D_MODEL, D_MODEL),
        w_up_g=w_up[:, :D_FF],
        w_up_v=w_up[:, D_FF:],
        w_down=sh["w_down"].reshape(D_FF, D_MODEL),
    )


def kernel(x, pre_mix_norm, w_in, rel_bias, hgrn_lb_raw, hgrn_norm, w_branch_attn, w_branch_hgrn, w_out, post_mix_norm, pre_ffn_norm, w_up, conv_w, conv_b, w_down, post_ffn_norm, loss_target, m_pre_mix_norm, m_w_in, m_rel_bias, m_hgrn_lb_raw, m_hgrn_norm, m_w_branch_attn, m_w_branch_hgrn, m_w_out, m_post_mix_norm, m_pre_ffn_norm, m_w_up, m_conv_w, m_conv_b, m_w_down, m_post_ffn_norm, v_pre_mix_norm, v_w_in, v_rel_bias, v_hgrn_lb_raw, v_hgrn_norm, v_w_branch_attn, v_w_branch_hgrn, v_w_out, v_post_mix_norm, v_pre_ffn_norm, v_w_up, v_conv_w, v_conv_b, v_w_down, v_post_ffn_norm):
    ci = lax.axis_index("c")
    dev = 4 * lax.axis_index("x") + 2 * lax.axis_index("y") + ci
    wts = dict(w_in=w_in[0], w_ba=w_branch_attn[0], w_bh=w_branch_hgrn[0], w_out=w_out[0], w_up=w_up[0],
               w_down=w_down[0])
    mom = dict(w_in=m_w_in[0], w_ba=m_w_branch_attn[0], w_bh=m_w_branch_hgrn[0], w_out=m_w_out[0], w_up=m_w_up[0],
               w_down=m_w_down[0])
    var = dict(w_in=v_w_in[0], w_ba=v_w_branch_attn[0], w_bh=v_w_branch_hgrn[0], w_out=v_w_out[0], w_up=v_w_up[0],
               w_down=v_w_down[0])
    small_w = dict(pre_mix_norm=pre_mix_norm, rel_bias=rel_bias, hgrn_lb_raw=hgrn_lb_raw, hgrn_norm=hgrn_norm,
                   post_mix_norm=post_mix_norm, pre_ffn_norm=pre_ffn_norm, conv_b=conv_b, post_ffn_norm=post_ffn_norm)
    small_m = dict(pre_mix_norm=m_pre_mix_norm, rel_bias=m_rel_bias, hgrn_lb_raw=m_hgrn_lb_raw, hgrn_norm=m_hgrn_norm,
                   post_mix_norm=m_post_mix_norm, pre_ffn_norm=m_pre_ffn_norm, conv_b=m_conv_b,
                   post_ffn_norm=m_post_ffn_norm)
    small_v = dict(pre_mix_norm=v_pre_mix_norm, rel_bias=v_rel_bias, hgrn_lb_raw=v_hgrn_lb_raw, hgrn_norm=v_hgrn_norm,
                   post_mix_norm=v_post_mix_norm, pre_ffn_norm=v_pre_ffn_norm, conv_b=v_conv_b,
                   post_ffn_norm=v_post_ffn_norm)

    slab = _pack_rows(*[wts[k].astype(bf16)[None] for k, _ in _PACK_SIZES])[0]
    W = _full_weights(_all_gather(slab, "ag_w"))
    cw_pad = jnp.pad(conv_w[0], ((0, SUBLANE - 3), (0, 768 - 704)))
    conv_w_full = _cols_to_full(_all_gather(cw_pad, "ag_convw")[:, 0:3, 0:704])
    P = dict(small_w)
    P["conv_w"] = conv_w_full

    loss8, grad_x, big, small = _local_step(x[0], loss_target[0], W, P)
    loss = lax.psum(loss8[0, 0], ("x", "y", "c"))

    cols = dict(w_in=_full_to_cols(big["w_in"]), w_ba=_full_to_cols(big["w_ba"]), w_bh=_full_to_cols(big["w_bh"]),
                w_out=big["w_out"].reshape(8, 128, D_MODEL), w_up=_full_to_cols(big["w_up"]),
                w_down=big["w_down"].reshape(8, 352, D_MODEL))
    gslab = _pack_rows(*[cols[k].astype(bf16) for k, _ in _PACK_SIZES])
    by_core = jnp.swapaxes(gslab.reshape(4, 2, 2400, D_MODEL), 0, 1)
    from_sib = _core_swap(by_core, "rs_cores")
    mine = lax.dynamic_index_in_dim(by_core, ci, axis=0, keepdims=False)
    chip_sum = _pair_add(mine, from_sib, "rs_pair_add")
    parts = _unpack_rows(_chip_comm(chip_sum, False, "rs_chips"))
    outs_big = {}
    for k, _ in _PACK_SIZES:
        outs_big[k] = _adamw(wts[k], mom[k], var[k], parts[k], "adamw_" + k)

    spack = jnp.concatenate([_pack_small(small),
                             jnp.pad(small["conv_w"].reshape(-1, LANE), ((0, _CONVW_ROWS - 132), (0, 0)))], axis=0)
    allp = _core_gather(_chip_comm(spack, True, "ag_small_chips"), "ag_small_cores")
    ssum = _sum8(allp, "small_sum")
    gs = ssum[:_SMALL_ROWS]
    res_small = _adamw(_pack_small(small_w), _pack_small(small_m), _pack_small(small_v), gs, "adamw_small")
    sm = [_unpack_small(t) for t in res_small]
    g_cw_full = ssum[_SMALL_ROWS:_SMALL_ROWS + 132].reshape(3, 2 * D_FF)
    g_cw = lax.dynamic_slice_in_dim(g_cw_full, dev * 704, 704, axis=1)
    res_cw = _adamw(conv_w[0], m_conv_w[0], v_conv_w[0], g_cw, "adamw_conv_w")

    def pick(i):
        def big_(k):
            return outs_big[k][i][None]
        return [sm[i]["pre_mix_norm"], big_("w_in"), sm[i]["rel_bias"], sm[i]["hgrn_lb_raw"], sm[i]["hgrn_norm"],
                big_("w_ba"), big_("w_bh"), big_("w_out"), sm[i]["post_mix_norm"], sm[i]["pre_ffn_norm"],
                big_("w_up"), res_cw[i][None], sm[i]["conv_b"], big_("w_down"), sm[i]["post_ffn_norm"]]

    return (loss, grad_x[None], *pick(0), *pick(1), *pick(2), *pick(3))
```

```python
import functools
import math

import jax
import jax.numpy as jnp
from jax import lax
from jax.experimental import pallas as pl
from jax.experimental.pallas import tpu as pltpu

f32 = jnp.float32
bf16 = jnp.bfloat16
SDS = jax.ShapeDtypeStruct
HIGHEST = lax.Precision.HIGHEST
MESH = pl.DeviceIdType.MESH

NN = (((1,), (0,)), ((), ()))
NT = (((1,), (1,)), ((), ()))
TN = (((0,), (0,)), ((), ()))

D_MODEL = 1024
N_GROUPS = 3
DILATIONS = (1, 4, 16)
HEAD_DIM = 64
ATTN_BLOCK = 128
QKV_G = 1536
ATTN_OUT = 512
HGRN_W = 512
HGRN_CHUNK = 32
D_FF = 2816
NUM_BUCKETS = 32
MAX_EXACT = 16
MAX_DISTANCE = 2048
NEG_INF = -1e30
EPS = 1e-6
LANE = 128
SUBLANE = 8
VMEM_BIG = 48 * 1024 * 1024
MM_ROWS = 512
MM_OUT_BYTES = 8 * 1024 * 1024

ADAM_LR, ADAM_B1, ADAM_B2, ADAM_EPS, ADAM_WD, ADAM_STEP = 0.001, 0.9, 0.999, 1e-08, 0.01, 10


def _pick(n, pref):
    t = pref
    while t >= LANE:
        if n % t == 0:
            return t
        t //= 2
    return n


def _cparams(sem=None, vmem=None):
    kw = {}
    if sem is not None:
        kw["dimension_semantics"] = sem
    if vmem is not None:
        kw["vmem_limit_bytes"] = vmem
    return pltpu.CompilerParams(**kw)


def _sigmoid(x):
    return jax.nn.sigmoid(x)


def _colsum8(x):
    return x.reshape(x.shape[0] // SUBLANE, SUBLANE, x.shape[1]).sum(axis=0)


def _mm(a, b, mode, out_dtype, name, acc=None):
    if mode == "nn":
        (M, K), (_, N) = a.shape, b.shape
    elif mode == "nt":
        (M, K), (N, _) = a.shape, b.shape
    else:
        (K, M), (_, N) = a.shape, b.shape
    dims = {"nn": NN, "nt": NT, "tn": TN}[mode]
    has_acc = acc is not None
    if mode == "tn":
        assert not has_acc
        tmm = M if M * N * 4 <= MM_OUT_BYTES else M // 2
        ts = _pick(K, MM_ROWS)
        nk = K // ts

        def body_tn(a_ref, b_ref, o_ref):
            k = pl.program_id(1)
            part = lax.dot_general(a_ref[...], b_ref[...], dims, preferred_element_type=f32)

            @pl.when(k == 0)
            def _():
                o_ref[...] = part

            @pl.when(k > 0)
            def _():
                o_ref[...] += part

        return pl.pallas_call(
            body_tn,
            grid=(M // tmm, nk),
            in_specs=[pl.BlockSpec((ts, tmm), lambda i, k: (k, i)), pl.BlockSpec((ts, N), lambda i, k: (k, 0))],
            out_specs=pl.BlockSpec((tmm, N), lambda i, k: (i, 0)),
            out_shape=SDS((M, N), out_dtype),
            compiler_params=_cparams(("parallel", "arbitrary"), VMEM_BIG),
            name=name,
        )(a, b)

    tm = _pick(M, MM_ROWS)

    def body(*refs):
        if has_acc:
            a_ref, b_ref, c_ref, o_ref = refs
        else:
            a_ref, b_ref, o_ref = refs
        part = lax.dot_general(a_ref[...], b_ref[...], dims, preferred_element_type=f32)
        if has_acc:
            part = part + c_ref[...]
        o_ref[...] = part.astype(out_dtype)

    specs = [pl.BlockSpec((tm, K), lambda i: (i, 0)), pl.BlockSpec(b.shape, lambda i: (0, 0))]
    args = [a, b]
    aliases = {}
    if has_acc:
        specs.append(pl.BlockSpec((tm, N), lambda i: (i, 0)))
        args.append(acc)
        aliases = {2: 0}
    return pl.pallas_call(
        body,
        grid=(M // tm,),
        in_specs=specs,
        out_specs=pl.BlockSpec((tm, N), lambda i: (i, 0)),
        out_shape=SDS((M, N), out_dtype),
        input_output_aliases=aliases,
        compiler_params=_cparams(("parallel",), VMEM_BIG),
        name=name,
    )(*args)


def _permute(x, d, name):
    if d == 1:
        return x
    S, C = x.shape
    U = S // d

    def body(x_ref, o_ref):
        for r in range(d):
            o_ref[r] = x_ref[pl.ds(r, ATTN_BLOCK, stride=d), :]

    out = pl.pallas_call(
        body,
        grid=(U // ATTN_BLOCK, C // LANE),
        in_specs=[pl.BlockSpec((ATTN_BLOCK * d, LANE), lambda i, j: (i, j))],
        out_specs=pl.BlockSpec((d, ATTN_BLOCK, LANE), lambda i, j: (0, i, j)),
        out_shape=SDS((d, U, C), x.dtype),
        compiler_params=_cparams(("parallel", "parallel")),
        name=name,
    )(x)
    return out.reshape(S, C)


def _unpermute(x, d, name):
    if d == 1:
        return x
    S, C = x.shape
    U = S // d

    def body(x_ref, o_ref):
        for r in range(d):
            o_ref[pl.ds(r, ATTN_BLOCK, stride=d), :] = x_ref[r]

    return pl.pallas_call(
        body,
        grid=(U // ATTN_BLOCK, C // LANE),
        in_specs=[pl.BlockSpec((d, ATTN_BLOCK, LANE), lambda i, j: (0, i, j))],
        out_specs=pl.BlockSpec((ATTN_BLOCK * d, LANE), lambda i, j: (i, j)),
        out_shape=SDS((S, C), x.dtype),
        compiler_params=_cparams(("parallel", "parallel")),
        name=name,
    )(x.reshape(d, U, C))


def _rms_parts(xv):
    r = lax.rsqrt(jnp.mean(xv * xv, axis=-1, keepdims=True) + EPS)
    return r, xv * r


def _rms_bwd(xhat, r, w, dy):
    dyw = dy * w
    return r * (dyw - xhat * jnp.mean(dyw * xhat, axis=-1, keepdims=True))


def _rmsnorm(x, w, name):
    S, D = x.shape
    tm = _pick(S, 512)

    def body(x_ref, w_ref, o_ref):
        _, xh = _rms_parts(x_ref[...])
        o_ref[...] = (xh * w_ref[...]).astype(bf16)

    return pl.pallas_call(
        body,
        grid=(S // tm,),
        in_specs=[pl.BlockSpec((tm, D), lambda i: (i, 0)), pl.BlockSpec((1, D), lambda i: (0, 0))],
        out_specs=pl.BlockSpec((tm, D), lambda i: (i, 0)),
        out_shape=SDS((S, D), bf16),
        compiler_params=_cparams(("parallel",)),
        name=name,
    )(x, w)


def _mid_fwd(x, mo, w_pm, w_pf):
    S, D = x.shape
    tm = _pick(S, 512)

    def body(x_ref, mo_ref, wpm_ref, wpf_ref, x1_ref, h2_ref):
        _, moh = _rms_parts(mo_ref[...])
        x1 = x_ref[...] + moh * wpm_ref[...]
        x1_ref[...] = x1
        _, x1h = _rms_parts(x1)
        h2_ref[...] = (x1h * wpf_ref[...]).astype(bf16)

    row = pl.BlockSpec((tm, D), lambda i: (i, 0))
    vec = pl.BlockSpec((1, D), lambda i: (0, 0))
    return pl.pallas_call(
        body,
        grid=(S // tm,),
        in_specs=[row, row, vec, vec],
        out_specs=[row, row],
        out_shape=[SDS((S, D), f32), SDS((S, D), bf16)],
        compiler_params=_cparams(("parallel",)),
        name="mid_fwd",
    )(x, mo, w_pm, w_pf)


def _final(x1, fo, tgt, w_pfn):
    S, D = x1.shape
    tm = _pick(S, 512)
    nt = S // tm

    def body(x1_ref, fo_ref, t_ref, w_ref, loss_ref, dy_ref, dfo_ref, gw_ref, lacc, gacc):
        i = pl.program_id(0)

        @pl.when(i == 0)
        def _():
            lacc[...] = jnp.zeros_like(lacc)
            gacc[...] = jnp.zeros_like(gacc)

        w = w_ref[...]
        r, foh = _rms_parts(fo_ref[...])
        y = x1_ref[...] + foh * w
        err = y - t_ref[...]
        lacc[...] += _colsum8(err * err)
        dy = err * (1.0 / D)
        dy_ref[...] = dy
        gacc[...] += _colsum8(dy * foh)
        dfo_ref[...] = _rms_bwd(foh, r, w, dy).astype(bf16)

        @pl.when(i == nt - 1)
        def _():
            loss_ref[...] = jnp.full((SUBLANE, LANE), 0.5 / D, f32) * jnp.sum(lacc[...])
            gw_ref[...] = jnp.sum(gacc[...], axis=0, keepdims=True)

    row = pl.BlockSpec((tm, D), lambda i: (i, 0))
    vec = pl.BlockSpec((1, D), lambda i: (0, 0))
    return pl.pallas_call(
        body,
        grid=(nt,),
        in_specs=[row, row, row, vec],
        out_specs=[pl.BlockSpec((SUBLANE, LANE), lambda i: (0, 0)), row, row, vec],
        out_shape=[SDS((SUBLANE, LANE), f32), SDS((S, D), f32), SDS((S, D), bf16), SDS((1, D), f32)],
        scratch_shapes=[pltpu.VMEM((SUBLANE, D), f32), pltpu.VMEM((SUBLANE, D), f32)],
        compiler_params=_cparams(("arbitrary",)),
        name="final_loss",
    )(x1, fo, tgt, w_pfn)


def _mid_bwd(dy, dh2, x1, mo, w_pf, w_pm):
    S, D = dy.shape
    tm = _pick(S, 512)
    nt = S // tm

    def body(dy_ref, dh2_ref, x1_ref, mo_ref, wpf_ref, wpm_ref, dx1_ref, dmo_ref, gpf_ref, gpm_ref, apf, apm):
        i = pl.program_id(0)

        @pl.when(i == 0)
        def _():
            apf[...] = jnp.zeros_like(apf)
            apm[...] = jnp.zeros_like(apm)

        r1, x1h = _rms_parts(x1_ref[...])
        dh2 = dh2_ref[...]
        apf[...] += _colsum8(dh2 * x1h)
        dx1 = dy_ref[...] + _rms_bwd(x1h, r1, wpf_ref[...], dh2)
        dx1_ref[...] = dx1
        rm, moh = _rms_parts(mo_ref[...])
        apm[...] += _colsum8(dx1 * moh)
        dmo_ref[...] = _rms_bwd(moh, rm, wpm_ref[...], dx1).astype(bf16)

        @pl.when(i == nt - 1)
        def _():
            gpf_ref[...] = jnp.sum(apf[...], axis=0, keepdims=True)
            gpm_ref[...] = jnp.sum(apm[...], axis=0, keepdims=True)

    row = pl.BlockSpec((tm, D), lambda i: (i, 0))
    vec = pl.BlockSpec((1, D), lambda i: (0, 0))
    return pl.pallas_call(
        body,
        grid=(nt,),
        in_specs=[row, row, row, row, vec, vec],
        out_specs=[row, row, vec, vec],
        out_shape=[SDS((S, D), f32), SDS((S, D), bf16), SDS((1, D), f32), SDS((1, D), f32)],
        scratch_shapes=[pltpu.VMEM((SUBLANE, D), f32), pltpu.VMEM((SUBLANE, D), f32)],
        compiler_params=_cparams(("arbitrary",)),
        name="mid_bwd",
    )(dy, dh2, x1, mo, w_pf, w_pm)


def _first_bwd(x, dx1, dh_a, dh_b, dh_c, w_pre):
    S, D = x.shape
    tm = _pick(S, 512)
    nt = S // tm

    def body(x_ref, dx1_ref, a_ref, b_ref, c_ref, w_ref, gx_ref, gw_ref, acc):
        i = pl.program_id(0)

        @pl.when(i == 0)
        def _():
            acc[...] = jnp.zeros_like(acc)

        r, xh = _rms_parts(x_ref[...])
        dh = (a_ref[...] + b_ref[...]) + c_ref[...]
        acc[...] += _colsum8(dh * xh)
        gx_ref[...] = dx1_ref[...] + _rms_bwd(xh, r, w_ref[...], dh)

        @pl.when(i == nt - 1)
        def _():
            gw_ref[...] = jnp.sum(acc[...], axis=0, keepdims=True)

    row = pl.BlockSpec((tm, D), lambda i: (i, 0))
    vec = pl.BlockSpec((1, D), lambda i: (0, 0))
    return pl.pallas_call(
        body,
        grid=(nt,),
        in_specs=[row, row, row, row, row, vec],
        out_specs=[row, vec],
        out_shape=[SDS((S, D), f32), SDS((1, D), f32)],
        scratch_shapes=[pltpu.VMEM((SUBLANE, D), f32)],
        compiler_params=_cparams(("arbitrary",)),
        name="first_bwd",
    )(x, dx1, dh_a, dh_b, dh_c, w_pre)


def _t5_bucket(dist):
    n = jnp.maximum(dist, 0)
    nf = jnp.maximum(n, 1).astype(f32)
    large = MAX_EXACT + (jnp.log(nf / MAX_EXACT) / math.log(MAX_DISTANCE / MAX_EXACT)
                         * (NUM_BUCKETS - MAX_EXACT)).astype(jnp.int32)
    large = jnp.minimum(large, NUM_BUCKETS - 1)
    return jnp.where(n < MAX_EXACT, n, large)


def _bias_consts(d):
    blk = ATTN_BLOCK
    rel = jnp.arange(blk)[:, None] + blk - jnp.arange(2 * blk)[None, :]
    in_win = (rel >= 0) & (rel <= blk)
    bucket = _t5_bucket(rel * d).reshape(1, -1)
    onehot = (bucket == jnp.arange(NUM_BUCKETS)[:, None]).astype(f32)
    return onehot, in_win.astype(f32).reshape(1, -1)


def _bias_build(tab_t, onehot, maskf, name):
    H = tab_t.shape[0]

    def body(t_ref, oh_ref, m_ref, o_ref):
        b = jnp.dot(t_ref[...], oh_ref[...], precision=HIGHEST, preferred_element_type=f32)
        o_ref[...] = jnp.where(m_ref[...] > 0.5, b, NEG_INF)

    return pl.pallas_call(body, out_shape=SDS((H, onehot.shape[1]), f32), name=name)(tab_t, onehot, maskf)


def _bias_grad(dbias_flat, onehot, name):
    H = dbias_flat.shape[0]

    def body(g_ref, oh_ref, o_ref):
        o_ref[...] = lax.dot_general(oh_ref[...], g_ref[...], NT, precision=HIGHEST, preferred_element_type=f32)

    return pl.pallas_call(body, out_shape=SDS((NUM_BUCKETS, H), f32), name=name)(dbias_flat, onehot)


ATTN_TILE = 512
ATTN_SUB = ATTN_TILE // ATTN_BLOCK


def _qkv_specs(nt):
    tile = (ATTN_TILE, LANE)
    blk = (ATTN_BLOCK, LANE)
    cur = lambda off: (lambda h, t: (jnp.minimum(t, nt - 1), off + h))
    prev = lambda off: (lambda h, t: (jnp.maximum(jnp.minimum(t, nt - 1) * ATTN_SUB - 1, 0), off + h))
    return [pl.BlockSpec(tile, cur(0)), pl.BlockSpec(blk, prev(4)), pl.BlockSpec(tile, cur(4)),
            pl.BlockSpec(blk, prev(8)), pl.BlockSpec(tile, cur(8))]


def _head_masks():
    lane = lax.broadcasted_iota(jnp.int32, (ATTN_BLOCK, LANE), 1)
    return lane < HEAD_DIM


def _attn_fwd(qkv, bias, bps, name):
    S = qkv.shape[0]
    nt = S // ATTN_TILE
    scale = HEAD_DIM ** -0.5

    def body(q_ref, kp_ref, kc_ref, vp_ref, vc_ref, b_ref, o_ref, l_ref):
        t = pl.program_id(1)
        kk = jnp.concatenate([kp_ref[...], kc_ref[...]], axis=0)
        vv = jnp.concatenate([vp_ref[...], vc_ref[...]], axis=0)
        low = _head_masks()
        col = lax.broadcasted_iota(jnp.int32, (ATTN_BLOCK, 2 * ATTN_BLOCK), 1)
        for b in range(ATTN_SUB):
            lo = b * ATTN_BLOCK
            rows = slice(lo, lo + ATTN_BLOCK)
            keys = slice(lo, lo + 2 * ATTN_BLOCK)
            dead = jnp.logical_and((t * ATTN_SUB + b) % bps == 0, col < ATTN_BLOCK)
            q2 = q_ref[rows, :]
            kb, vb = kk[keys], vv[keys]
            outs, lses = [], []
            for h in range(2):
                hm = low if h == 0 else jnp.logical_not(low)
                qh = jnp.where(hm, q2, jnp.zeros_like(q2))
                s = lax.dot_general(qh, kb, NT, preferred_element_type=f32) * scale + b_ref[h]
                s = jnp.where(dead, NEG_INF, s)
                m = jnp.max(s, axis=-1, keepdims=True)
                p = jnp.exp(s - m)
                l = jnp.sum(p, axis=-1, keepdims=True)
                outs.append(jnp.dot(p.astype(bf16), vb, preferred_element_type=f32) / l)
                lses.append(m + jnp.log(l))
            o_ref[rows, :] = jnp.where(low, outs[0], outs[1])
            l_ref[rows, :] = jnp.where(low, lses[0], lses[1])

    tile = pl.BlockSpec((ATTN_TILE, LANE), lambda h, t: (t, h))
    return pl.pallas_call(
        body,
        grid=(4, nt),
        in_specs=_qkv_specs(nt) + [pl.BlockSpec((2, ATTN_BLOCK, 2 * ATTN_BLOCK), lambda h, t: (h, 0, 0))],
        out_specs=[tile, tile],
        out_shape=[SDS((S, ATTN_OUT), f32), SDS((S, ATTN_OUT), f32)],
        compiler_params=_cparams(("parallel", "parallel")),
        name=name,
    )(qkv, qkv, qkv, qkv, qkv, bias)


def _attn_bwd(qkv, bias, do, dvec, lse, bps, name):
    S = qkv.shape[0]
    nt = S // ATTN_TILE
    scale = HEAD_DIM ** -0.5

    def assemble(parts):
        rows = [parts[0][:ATTN_BLOCK]]
        for b in range(ATTN_SUB - 1):
            rows.append(parts[b][ATTN_BLOCK:] + parts[b + 1][:ATTN_BLOCK])
        rows.append(parts[-1][ATTN_BLOCK:])
        return rows

    def body(q_ref, kp_ref, kc_ref, vp_ref, vc_ref, b_ref, do_ref, dvec_ref, lse_ref,
             dq_ref, dk_ref, dv_ref, db_ref, ck, cv):
        t = pl.program_id(1)
        last = ATTN_TILE - ATTN_BLOCK

        @pl.when(t == 0)
        def _():
            ck[...] = jnp.zeros_like(ck)
            cv[...] = jnp.zeros_like(cv)
            db_ref[...] = jnp.zeros_like(db_ref)

        @pl.when(t < nt)
        def _():
            kk = jnp.concatenate([kp_ref[...], kc_ref[...]], axis=0)
            vv = jnp.concatenate([vp_ref[...], vc_ref[...]], axis=0)
            low = _head_masks()
            col = lax.broadcasted_iota(jnp.int32, (ATTN_BLOCK, 2 * ATTN_BLOCK), 1)
            low2 = lax.broadcasted_iota(jnp.int32, (2 * ATTN_BLOCK, LANE), 1) < HEAD_DIM
            dk_parts, dv_parts = [], []
            dsum = [None, None]
            for b in range(ATTN_SUB):
                lo = b * ATTN_BLOCK
                rows = slice(lo, lo + ATTN_BLOCK)
                keys = slice(lo, lo + 2 * ATTN_BLOCK)
                dead = jnp.logical_and((t * ATTN_SUB + b) % bps == 0, col < ATTN_BLOCK)
                q2 = q_ref[rows, :]
                kb, vb = kk[keys], vv[keys]
                do2 = do_ref[rows, :].astype(bf16)
                dvec2 = dvec_ref[rows, :]
                lse2 = lse_ref[rows, :]
                dqs, dks, dvs = [], [], []
                for h in range(2):
                    hm = low if h == 0 else jnp.logical_not(low)
                    c0 = h * HEAD_DIM
                    qh = jnp.where(hm, q2, jnp.zeros_like(q2))
                    doh = jnp.where(hm, do2, jnp.zeros_like(do2))
                    s = lax.dot_general(qh, kb, NT, preferred_element_type=f32) * scale + b_ref[h]
                    s = jnp.where(dead, NEG_INF, s)
                    p = jnp.exp(s - lse2[:, c0:c0 + 1])
                    dp = lax.dot_general(doh, vb, NT, preferred_element_type=f32)
                    ds = p * (dp - dvec2[:, c0:c0 + 1])
                    dsum[h] = ds if dsum[h] is None else dsum[h] + ds
                    dsb = ds.astype(bf16)
                    dqs.append(jnp.dot(dsb, kb, preferred_element_type=f32) * scale)
                    dks.append(lax.dot_general(dsb, q2, TN, preferred_element_type=f32) * scale)
                    dvs.append(lax.dot_general(p.astype(bf16), do2, TN, preferred_element_type=f32))
                dq_ref[rows, :] = jnp.where(low, dqs[0], dqs[1]).astype(bf16)
                dk_parts.append(jnp.where(low2, dks[0], dks[1]))
                dv_parts.append(jnp.where(low2, dvs[0], dvs[1]))
            db_ref[0] += dsum[0]
            db_ref[1] += dsum[1]
            for parts, carry, out_ref in ((dk_parts, ck, dk_ref), (dv_parts, cv, dv_ref)):
                rws = assemble(parts)
                out_ref[:last, :] = carry[:last, :].astype(bf16)
                out_ref[last:, :] = (carry[last:, :] + rws[0]).astype(bf16)
                for b in range(ATTN_SUB):
                    carry[b * ATTN_BLOCK:(b + 1) * ATTN_BLOCK, :] = rws[b + 1]

        @pl.when(t == nt)
        def _():
            dk_ref[...] = ck[...].astype(bf16)
            dv_ref[...] = cv[...].astype(bf16)

    tile = (ATTN_TILE, LANE)
    cur = pl.BlockSpec(tile, lambda h, t: (jnp.minimum(t, nt - 1), h))
    lag = pl.BlockSpec(tile, lambda h, t: (jnp.maximum(t - 1, 0), h))
    bspec = pl.BlockSpec((2, ATTN_BLOCK, 2 * ATTN_BLOCK), lambda h, t: (h, 0, 0))
    return pl.pallas_call(
        body,
        grid=(4, nt + 1),
        in_specs=_qkv_specs(nt) + [bspec, cur, cur, cur],
        out_specs=[cur, lag, lag, bspec],
        out_shape=[SDS((S, ATTN_OUT), bf16), SDS((S, ATTN_OUT), bf16), SDS((S, ATTN_OUT), bf16),
                   SDS((8, ATTN_BLOCK, 2 * ATTN_BLOCK), f32)],
        scratch_shapes=[pltpu.VMEM(tile, f32), pltpu.VMEM(tile, f32)],
        compiler_params=_cparams(("parallel", "arbitrary")),
        name=name,
    )(qkv, qkv, qkv, qkv, qkv, bias, do, dvec, lse)


def _attn_merge(o0, o1, o2, l0, l1, l2):
    S, W = o0.shape
    tm = _pick(S, 512)

    def body(o0_ref, o1_ref, o2_ref, l0_ref, l1_ref, l2_ref, y_ref, yb_ref, w0_ref, w1_ref, w2_ref):
        a, b, c = l0_ref[...], l1_ref[...], l2_ref[...]
        m = jnp.maximum(jnp.maximum(a, b), c)
        ea, eb, ec = jnp.exp(a - m), jnp.exp(b - m), jnp.exp(c - m)
        den = (ea + eb) + ec
        w0, w1, w2 = ea / den, eb / den, ec / den
        y = (w0 * o0_ref[...] + w1 * o1_ref[...]) + w2 * o2_ref[...]
        y_ref[...] = y
        yb_ref[...] = y.astype(bf16)
        w0_ref[...] = w0
        w1_ref[...] = w1
        w2_ref[...] = w2

    row = pl.BlockSpec((tm, W), lambda i: (i, 0))
    return pl.pallas_call(
        body,
        grid=(S // tm,),
        in_specs=[row] * 6,
        out_specs=[row] * 5,
        out_shape=[SDS((S, W), f32), SDS((S, W), bf16)] + [SDS((S, W), f32)] * 3,
        compiler_params=_cparams(("parallel",)),
        name="attn_merge",
    )(o0, o1, o2, l0, l1, l2)


def _attn_merge_bwd(dy, y, w0, w1, w2):
    S, W = dy.shape
    tm = _pick(S, 512)

    def body(dy_ref, y_ref, w0_ref, w1_ref, w2_ref, a0, a1, a2, b0, b1, b2):
        dyv = dy_ref[...]
        r = lax.broadcasted_iota(jnp.int32, (LANE, LANE), 0) // HEAD_DIM
        c = lax.broadcasted_iota(jnp.int32, (LANE, LANE), 1) // HEAD_DIM
        seg = jnp.where(r == c, 1.0, 0.0).astype(f32)
        cbar = jnp.dot(dyv * y_ref[...], seg, precision=HIGHEST, preferred_element_type=f32)
        for w_ref, a_ref, b_ref in ((w0_ref, a0, b0), (w1_ref, a1, b1), (w2_ref, a2, b2)):
            w = w_ref[...]
            a_ref[...] = w * dyv
            b_ref[...] = w * cbar

    blk = pl.BlockSpec((tm, LANE), lambda i, j: (i, j))
    return pl.pallas_call(
        body,
        grid=(S // tm, W // LANE),
        in_specs=[blk] * 5,
        out_specs=[blk] * 6,
        out_shape=[SDS((S, W), f32)] * 6,
        compiler_params=_cparams(("parallel", "parallel")),
        name="attn_merge_bwd",
    )(dy, y, w0, w1, w2)


HGRN_SB = 256


def _chunk_masks(sb):
    r = lax.broadcasted_iota(jnp.int32, (sb, sb), 0)
    c = lax.broadcasted_iota(jnp.int32, (sb, sb), 1)
    same = (r // HGRN_CHUNK) == (c // HGRN_CHUNK)
    return same, jnp.logical_and(same, c <= r), jnp.logical_and(same, c >= r)


def _hgrn_prep(q_raw, f_raw, lbv, same, tril):
    sq = _sigmoid(q_raw)
    qs = q_raw * sq
    sig = _sigmoid(f_raw)
    f = lbv + (1.0 - lbv) * sig
    g = jnp.log(f)
    k = 1.0 - f
    G = jnp.dot(jnp.where(tril, 1.0, 0.0).astype(f32), g, precision=HIGHEST, preferred_element_type=f32)
    GL = jnp.dot(jnp.where(same, 1.0, 0.0).astype(f32), g, precision=HIGHEST, preferred_element_type=f32)
    eG = jnp.exp(G)
    einv = jnp.exp(-G)
    edec = jnp.exp(GL - G)
    return dict(sq=sq, qs=qs, sig=sig, f=f, k=k, eG=eG, einv=einv, edec=edec, eGL=jnp.exp(GL),
                qt=qs * eG, kt=k * einv, kd=k * edec)


def _ride_split(ride, rest, n_out, n_scratch):
    if ride is None:
        return None, rest[:n_out], None, rest[n_out:], None
    return rest[0], rest[1:1 + n_out], rest[1 + n_out], rest[2 + n_out:2 + n_out + n_scratch], rest[2 + n_out + n_scratch:]


def _hgrn_fwd(hg, lb, normw, ride=None):
    S = hg.shape[0]
    sb = HGRN_SB
    nsb = S // sb
    nch = sb // HGRN_CHUNK

    def body(q_ref, f_ref, v_ref, og_ref, lb_ref, nw_ref, *rest):
        src_ref, (y_ref, o_ref, ck_ref), got_ref, (st,), sems = _ride_split(ride, rest, 3, 1)
        j = pl.program_id(1)
        if ride is not None:
            @pl.when(jnp.logical_and(pl.program_id(0) == 0, j == 0))
            def _():
                _chip_start(src_ref, got_ref, sems[0], sems[1], ride[1])

        @pl.when(j == 0)
        def _():
            st[...] = jnp.zeros_like(st)

        ST = st[...]
        ck_ref[0, 0] = ST
        same, tril, _ = _chunk_masks(sb)
        pr = _hgrn_prep(q_ref[...], f_ref[...], lb_ref[...], same, tril)
        qtb, ktb, kdb = pr["qt"].astype(bf16), pr["kt"].astype(bf16), pr["kd"].astype(bf16)
        eGL = pr["eGL"]
        vb = v_ref[...].astype(bf16)
        A = jnp.where(tril, lax.dot_general(qtb, ktb, NT, preferred_element_type=f32), 0.0)
        o = jnp.dot(A.astype(bf16), vb, preferred_element_type=f32)
        outs = []
        for ci in range(nch):
            lo = ci * HGRN_CHUNK
            sl = slice(lo, lo + HGRN_CHUNK)
            outs.append(o[sl] + lax.dot_general(qtb[sl], ST.astype(bf16), NT, preferred_element_type=f32))
            ST = ST * eGL[lo:lo + 1, :] + lax.dot_general(vb[sl], kdb[sl], TN, preferred_element_type=f32)
        st[...] = ST
        of = jnp.concatenate(outs, axis=0)
        o_ref[...] = of
        rms = lax.rsqrt(jnp.mean(of * of, axis=-1, keepdims=True) + EPS)
        ogv = og_ref[...]
        y_ref[...] = ((of * rms * nw_ref[...]) * (ogv * _sigmoid(ogv))).astype(bf16)

        if ride is not None:
            @pl.when(jnp.logical_and(pl.program_id(0) == 3, j == nsb - 1))
            def _():
                _chip_finish(src_ref, got_ref, sems[0], sems[1], ride[1])

    col = lambda off: pl.BlockSpec((sb, LANE), lambda h, j: (j, off + h))
    riding = ride is not None
    res = pl.pallas_call(
        body,
        grid=(4, nsb),
        in_specs=[col(0), col(4), col(8), col(12), pl.BlockSpec((1, LANE), lambda h, j: (0, h)),
                  pl.BlockSpec((1, LANE), lambda h, j: (0, 0))] + ([_ANY] if riding else []),
        out_specs=[col(0), col(0), pl.BlockSpec((1, 1, LANE, LANE), lambda h, j: (h, j, 0, 0))]
        + ([_ANY] if riding else []),
        out_shape=[SDS((S, HGRN_W), bf16), SDS((S, HGRN_W), f32), SDS((4, nsb, LANE, LANE), f32)]
        + ([_chip_out_shape(*ride)] if riding else []),
        scratch_shapes=[pltpu.VMEM((LANE, LANE), f32)] + (list(_CHIP_SEMS) if riding else []),
        compiler_params=_cparams(("arbitrary", "arbitrary") if riding else ("parallel", "arbitrary")),
        name="hgrn_fwd",
    )(hg, hg, hg, hg, lb, normw, *([ride[0]] if riding else []))
    return tuple(res) if riding else (*res, None)


def _hgrn_bwd(hg, o_raw, dy, ck, lb, normw, ride=None):
    S = hg.shape[0]
    sb = HGRN_SB
    nsb = S // sb
    nch = sb // HGRN_CHUNK

    def body(q_ref, f_ref, v_ref, og_ref, o_ref, dy_ref, ck_ref, lb_ref, nw_ref, *rest):
        src_ref, outs, got_ref, (dst, alb, anw), sems = _ride_split(ride, rest, 6, 3)
        dq_ref, df_ref, dv_ref, dog_ref, glb_ref, gnw_ref = outs
        j = pl.program_id(1)
        if ride is not None:
            @pl.when(jnp.logical_and(pl.program_id(0) == 0, j == 0))
            def _():
                _chip_start(src_ref, got_ref, sems[0], sems[1], ride[1])

        @pl.when(j == 0)
        def _():
            dst[...] = jnp.zeros_like(dst)
            alb[...] = jnp.zeros_like(alb)
            anw[...] = jnp.zeros_like(anw)

        same, tril, triu = _chunk_masks(sb)
        lbv = lb_ref[...]
        q_raw = q_ref[...]
        pr = _hgrn_prep(q_raw, f_ref[...], lbv, same, tril)
        qt, kt, kd, eGL = pr["qt"], pr["kt"], pr["kd"], pr["eGL"]
        qtb, ktb, kdb = qt.astype(bf16), kt.astype(bf16), kd.astype(bf16)
        vb = v_ref[...].astype(bf16)

        o = o_ref[...]
        ogv = og_ref[...]
        sog = _sigmoid(ogv)
        rms = lax.rsqrt(jnp.mean(o * o, axis=-1, keepdims=True) + EPS)
        oh = o * rms
        nw = nw_ref[...]
        dyv = dy_ref[...]
        dog_ref[...] = (dyv * (oh * nw) * (sog * (1.0 + ogv * (1.0 - sog)))).astype(bf16)
        dohw = dyv * (ogv * sog)
        anw[...] += _colsum8(dohw * oh)
        doh = dohw * nw
        do = rms * (doh - oh * jnp.mean(doh * oh, axis=-1, keepdims=True))
        dob = do.astype(bf16)

        Ab = jnp.where(tril, lax.dot_general(qtb, ktb, NT, preferred_element_type=f32), 0.0).astype(bf16)
        dAb = jnp.where(tril, lax.dot_general(dob, vb, NT, preferred_element_type=f32), 0.0).astype(bf16)
        dv_acc = lax.dot_general(Ab, dob, TN, preferred_element_type=f32)
        dqt = jnp.dot(dAb, ktb, preferred_element_type=f32)
        dkt = lax.dot_general(dAb, qtb, TN, preferred_element_type=f32)

        ST = ck_ref[0, 0]
        states = []
        for ci in range(nch):
            lo = ci * HGRN_CHUNK
            sl = slice(lo, lo + HGRN_CHUNK)
            states.append(ST)
            ST = ST * eGL[lo:lo + 1, :] + lax.dot_general(vb[sl], kdb[sl], TN, preferred_element_type=f32)

        dST = dst[...]
        dqt_i, dkd_i, dv_i, deg_i = [None] * nch, [None] * nch, [None] * nch, [None] * nch
        for ci in reversed(range(nch)):
            lo = ci * HGRN_CHUNK
            sl = slice(lo, lo + HGRN_CHUNK)
            ST0 = states[ci]
            dSTb = dST.astype(bf16)
            dv_i[ci] = lax.dot_general(kdb[sl], dSTb, NT, preferred_element_type=f32)
            dqt_i[ci] = jnp.dot(dob[sl], ST0.astype(bf16), preferred_element_type=f32)
            dkd_i[ci] = jnp.dot(vb[sl], dSTb, preferred_element_type=f32)
            deg_i[ci] = jnp.broadcast_to(jnp.sum(dST * ST0, axis=0, keepdims=True), (HGRN_CHUNK, LANE))
            dST = dST * eGL[lo:lo + 1, :] + lax.dot_general(dob[sl], qtb[sl], TN, preferred_element_type=f32)
        dst[...] = dST

        dqt = dqt + jnp.concatenate(dqt_i, axis=0)
        dkd = jnp.concatenate(dkd_i, axis=0)
        dv_ref[...] = (dv_acc + jnp.concatenate(dv_i, axis=0)).astype(bf16)
        deg = jnp.concatenate(deg_i, axis=0)

        dqs = dqt * pr["eG"]
        dkdkd = dkd * kd
        dG = dqt * qt - dkt * kt - dkdkd
        dk = dkt * pr["einv"] + dkd * pr["edec"]
        dGL = jnp.dot(jnp.where(same, 1.0, 0.0).astype(f32), dkdkd, precision=HIGHEST,
                      preferred_element_type=f32) + eGL * deg
        dg = jnp.dot(jnp.where(triu, 1.0, 0.0).astype(f32), dG, precision=HIGHEST,
                     preferred_element_type=f32) + dGL
        df = dg / pr["f"] - dk
        sig = pr["sig"]
        df_ref[...] = (df * (1.0 - lbv) * (sig * (1.0 - sig))).astype(bf16)
        alb[...] += _colsum8(df * (1.0 - sig))
        sq = pr["sq"]
        dq_ref[...] = (dqs * (sq * (1.0 + q_raw * (1.0 - sq)))).astype(bf16)

        @pl.when(j == nsb - 1)
        def _():
            glb_ref[...] = jnp.broadcast_to(jnp.sum(alb[...], axis=0, keepdims=True), (SUBLANE, LANE))
            gnw_ref[...] = jnp.broadcast_to(jnp.sum(anw[...], axis=0, keepdims=True), (SUBLANE, LANE))

        if ride is not None:
            @pl.when(jnp.logical_and(pl.program_id(0) == 3, j == nsb - 1))
            def _():
                _chip_finish(src_ref, got_ref, sems[0], sems[1], ride[1])

    rev = lambda off: pl.BlockSpec((sb, LANE), lambda h, j: (nsb - 1 - j, off + h))
    stat = pl.BlockSpec((SUBLANE, LANE), lambda h, j: (0, h))
    riding = ride is not None
    res = pl.pallas_call(
        body,
        grid=(4, nsb),
        in_specs=[rev(0), rev(4), rev(8), rev(12), rev(0), rev(0),
                  pl.BlockSpec((1, 1, LANE, LANE), lambda h, j: (h, nsb - 1 - j, 0, 0)),
                  pl.BlockSpec((1, LANE), lambda h, j: (0, h)), pl.BlockSpec((1, LANE), lambda h, j: (0, 0))]
        + ([_ANY] if riding else []),
        out_specs=[rev(0), rev(0), rev(0), rev(0), stat, stat] + ([_ANY] if riding else []),
        out_shape=[SDS((S, HGRN_W), bf16)] * 4 + [SDS((SUBLANE, HGRN_W), f32)] * 2
        + ([_chip_out_shape(*ride)] if riding else []),
        scratch_shapes=[pltpu.VMEM((LANE, LANE), f32), pltpu.VMEM((SUBLANE, LANE), f32),
                        pltpu.VMEM((SUBLANE, LANE), f32)] + (list(_CHIP_SEMS) if riding else []),
        compiler_params=_cparams(("arbitrary", "arbitrary") if riding else ("parallel", "arbitrary")),
        name="hgrn_bwd",
    )(hg, hg, hg, hg, o_raw, dy, ck, lb, normw, *([ride[0]] if riding else []))
    return tuple(res) if riding else (*res, None)


def _lb_fwd(raw):
    def body(r_ref, o_ref):
        r = r_ref[...]
        m = jnp.max(r, axis=0, keepdims=True)
        e = jnp.exp(r - m)
        o_ref[...] = (e / jnp.sum(e, axis=0, keepdims=True))[0:1]

    return pl.pallas_call(body, out_shape=SDS((1, raw.shape[1]), f32), name="lb_fwd")(raw)


def _lb_bwd(raw, dlb):
    def body(r_ref, d_ref, o_ref):
        r = r_ref[...]
        m = jnp.max(r, axis=0, keepdims=True)
        e = jnp.exp(r - m)
        s = e / jnp.sum(e, axis=0, keepdims=True)
        s0 = s[0:1]
        onehot0 = jnp.where(lax.broadcasted_iota(jnp.int32, r.shape, 0) == 0, 1.0, 0.0)
        o_ref[...] = d_ref[...] * s0 * (onehot0 - s)

    return pl.pallas_call(body, out_shape=SDS(raw.shape, f32), name="lb_bwd")(raw, dlb)


def _gate_fwd(a, b, gc):
    S, D = a.shape
    tm = _pick(S, 512)

    def body(a_ref, b_ref, g0_ref, g1_ref, o_ref):
        s0, s1 = _sigmoid(g0_ref[...].astype(f32)), _sigmoid(g1_ref[...].astype(f32))
        o_ref[...] = (s0 * a_ref[...].astype(f32) + s1 * b_ref[...].astype(f32)).astype(bf16)

    row = pl.BlockSpec((tm, D), lambda i: (i, 0))
    return pl.pallas_call(
        body,
        grid=(S // tm,),
        in_specs=[row, row, row, pl.BlockSpec((tm, D), lambda i: (i, 1))],
        out_specs=row,
        out_shape=SDS((S, D), bf16),
        compiler_params=_cparams(("parallel",)),
        name="gate_fwd",
    )(a, b, gc, gc)


def _gate_bwd(dm, a, b, gc):
    S, D = a.shape
    tm = _pick(S, 512)

    def body(dm_ref, a_ref, b_ref, g0_ref, g1_ref, da_ref, db_ref, dg_ref):
        dmv = dm_ref[...].astype(f32)
        s0, s1 = _sigmoid(g0_ref[...].astype(f32)), _sigmoid(g1_ref[...].astype(f32))
        da_ref[...] = (dmv * s0).astype(bf16)
        db_ref[...] = (dmv * s1).astype(bf16)
        dg_ref[:, :D] = (dmv * a_ref[...].astype(f32) * (s0 * (1.0 - s0))).astype(bf16)
        dg_ref[:, D:] = (dmv * b_ref[...].astype(f32) * (s1 * (1.0 - s1))).astype(bf16)

    row = pl.BlockSpec((tm, D), lambda i: (i, 0))
    wide = pl.BlockSpec((tm, 2 * D), lambda i: (i, 0))
    return pl.pallas_call(
        body,
        grid=(S // tm,),
        in_specs=[row, row, row, row, pl.BlockSpec((tm, D), lambda i: (i, 1))],
        out_specs=[row, row, wide],
        out_shape=[SDS((S, D), bf16), SDS((S, D), bf16), SDS((S, 2 * D), bf16)],
        compiler_params=_cparams(("parallel",)),
        name="gate_bwd",
    )(dm, a, b, gc, gc)


CONV_ROWS = 512
INV_SQRT2 = 0.7071067811865476
INV_SQRT_2PI = 0.3989422804014327


CONV_HALO = 16


def _tile8(a, rows):
    return jnp.tile(a, (rows // a.shape[0], 1))


def _conv_rows(u_ref, w, b, r0, first):
    R = CONV_ROWS
    cur = u_ref[pl.ds(r0, R), :].astype(f32)
    prev8 = u_ref[pl.ds(pl.multiple_of(jnp.maximum(r0 - CONV_HALO, 0), CONV_HALO), CONV_HALO), :].astype(f32)
    prev8 = jnp.where(first, 0.0, prev8)
    row = lax.broadcasted_iota(jnp.int32, (R, LANE), 0)
    x1 = jnp.where(row < 1, _tile8(pltpu.roll(prev8, 1, 0), R), pltpu.roll(cur, 1, 0))
    x2 = jnp.where(row < 2, _tile8(pltpu.roll(prev8, 2, 0), R), pltpu.roll(cur, 2, 0))
    c = ((b + w[0:1] * x2) + w[1:2] * x1) + w[2:3] * cur
    return c, x2, x1, cur


def _conv_fwd(ug, uv, wg, wv, bg, bv):
    S, F = ug.shape
    nchunk = S // CONV_ROWS

    def body(ug_ref, uv_ref, wg_ref, wv_ref, bg_ref, bv_ref, o_ref):
        wgv, wvv, bgv, bvv = wg_ref[...], wv_ref[...], bg_ref[...], bv_ref[...]

        def step(ci, carry):
            r0 = pl.multiple_of(ci * CONV_ROWS, CONV_ROWS)
            cg = _conv_rows(ug_ref, wgv, bgv, r0, ci == 0)[0]
            cv = _conv_rows(uv_ref, wvv, bvv, r0, ci == 0)[0]
            gelu = 0.5 * cg * (1.0 + lax.erf(cg * INV_SQRT2))
            o_ref[pl.ds(r0, CONV_ROWS), :] = (gelu * cv).astype(bf16)
            return carry

        lax.fori_loop(0, nchunk, step, 0)

    col = pl.BlockSpec((S, LANE), lambda j: (0, j))
    w3 = pl.BlockSpec((3, LANE), lambda j: (0, j))
    b1 = pl.BlockSpec((1, LANE), lambda j: (0, j))
    return pl.pallas_call(
        body,
        grid=(F // LANE,),
        in_specs=[col, col, w3, w3, b1, b1],
        out_specs=col,
        out_shape=SDS((S, F), bf16),
        compiler_params=_cparams(("parallel",), VMEM_BIG),
        name="conv_fwd",
    )(ug, uv, wg, wv, bg, bv)


def _conv_bwd(ug, uv, dact, wg, wv, bg, bv):
    S, F = ug.shape
    R = CONV_ROWS
    nchunk = S // R

    def body(ug_ref, uv_ref, da_ref, wg_ref, wv_ref, bg_ref, bv_ref, dug_ref, duv_ref, sg_ref, sv_ref, dcg, dcv):
        wgv, wvv, bgv, bvv = wg_ref[...], wv_ref[...], bg_ref[...], bv_ref[...]
        zero = jnp.zeros((SUBLANE, LANE), f32)

        def fwd_step(ci, acc):
            r0 = pl.multiple_of(ci * R, R)
            cg, g2, g1, g0 = _conv_rows(ug_ref, wgv, bgv, r0, ci == 0)
            cv, v2, v1, v0 = _conv_rows(uv_ref, wvv, bvv, r0, ci == 0)
            da = da_ref[pl.ds(r0, R), :].astype(f32)
            cdf = 0.5 * (1.0 + lax.erf(cg * INV_SQRT2))
            pdf = INV_SQRT_2PI * jnp.exp(-0.5 * cg * cg)
            dg = da * cv * (cdf + cg * pdf)
            dv = da * (cg * cdf)
            dcg[pl.ds(r0, R), :] = dg
            dcv[pl.ds(r0, R), :] = dv
            new = (acc[0] + _colsum8(dg * g2), acc[1] + _colsum8(dg * g1), acc[2] + _colsum8(dg * g0),
                   acc[3] + _colsum8(dg),
                   acc[4] + _colsum8(dv * v2), acc[5] + _colsum8(dv * v1), acc[6] + _colsum8(dv * v0),
                   acc[7] + _colsum8(dv))
            return new

        acc = lax.fori_loop(0, nchunk, fwd_step, (zero,) * 8)
        rows = lax.broadcasted_iota(jnp.int32, (SUBLANE, LANE), 0)

        def stats(parts):
            out = jnp.zeros((SUBLANE, LANE), f32)
            for k, pt in enumerate(parts):
                out = jnp.where(rows == k, jnp.sum(pt, axis=0, keepdims=True), out)
            return out

        sg_ref[...] = stats(acc[0:4])
        sv_ref[...] = stats(acc[4:8])

        def du_rows(dc, w, r0, last):
            cur = dc[pl.ds(r0, R), :]
            nxt = dc[pl.ds(pl.multiple_of(jnp.minimum(r0 + R, S - SUBLANE), SUBLANE), SUBLANE), :]
            nxt = jnp.where(last, 0.0, nxt)
            row = lax.broadcasted_iota(jnp.int32, (R, LANE), 0)
            y1 = jnp.where(row >= R - 1, _tile8(pltpu.roll(nxt, SUBLANE - 1, 0), R), pltpu.roll(cur, R - 1, 0))
            y2 = jnp.where(row >= R - 2, _tile8(pltpu.roll(nxt, SUBLANE - 2, 0), R), pltpu.roll(cur, R - 2, 0))
            return w[2:3] * cur + w[1:2] * y1 + w[0:1] * y2

        def bwd_step(ci, carry):
            r0 = pl.multiple_of(ci * R, R)
            last = ci == nchunk - 1
            dug_ref[pl.ds(r0, R), :] = du_rows(dcg, wgv, r0, last).astype(bf16)
            duv_ref[pl.ds(r0, R), :] = du_rows(dcv, wvv, r0, last).astype(bf16)
            return carry

        lax.fori_loop(0, nchunk, bwd_step, 0)

    col = pl.BlockSpec((S, LANE), lambda j: (0, j))
    w3 = pl.BlockSpec((3, LANE), lambda j: (0, j))
    b1 = pl.BlockSpec((1, LANE), lambda j: (0, j))
    st = pl.BlockSpec((SUBLANE, LANE), lambda j: (0, j))
    return pl.pallas_call(
        body,
        grid=(F // LANE,),
        in_specs=[col, col, col, w3, w3, b1, b1],
        out_specs=[col, col, st, st],
        out_shape=[SDS((S, F), bf16), SDS((S, F), bf16), SDS((SUBLANE, F), f32), SDS((SUBLANE, F), f32)],
        scratch_shapes=[pltpu.VMEM((S, LANE), f32), pltpu.VMEM((S, LANE), f32)],
        compiler_params=_cparams(("parallel",), VMEM_BIG),
        name="conv_bwd",
    )(ug, uv, dact, wg, wv, bg, bv)


def _adam_math(w, g, m, v):
    m = ADAM_B1 * m + (1.0 - ADAM_B1) * g
    v = ADAM_B2 * v + (1.0 - ADAM_B2) * (g * g)
    m_hat = m / (1.0 - ADAM_B1 ** ADAM_STEP)
    v_hat = v / (1.0 - ADAM_B2 ** ADAM_STEP)
    delta = -ADAM_LR * (m_hat / (jnp.sqrt(v_hat) + ADAM_EPS) + ADAM_WD * w)
    return delta, m, v


def _adamw(w, m, v, g, name):
    R, C = w.shape
    parts = g.ndim == 3
    tr = R
    for t in (256, 128, 64, 32, 16):
        if R % t == 0 and R > t:
            tr = t
            break

    def body(w_ref, m_ref, v_ref, g_ref, go_ref, d_ref, mo_ref, vo_ref):
        if parts:
            gv = ((g_ref[0].astype(f32) + g_ref[1].astype(f32)) + g_ref[2].astype(f32)) + g_ref[3].astype(f32)
        else:
            gv = g_ref[...]
        go_ref[...] = gv
        d, mn, vn = _adam_math(w_ref[...], gv, m_ref[...], v_ref[...])
        d_ref[...] = d
        mo_ref[...] = mn
        vo_ref[...] = vn

    row = pl.BlockSpec((tr, C), lambda i: (i, 0))
    gspec = pl.BlockSpec((4, tr, C), lambda i: (0, i, 0)) if parts else row
    return pl.pallas_call(
        body,
        grid=(R // tr,),
        in_specs=[row, row, row, gspec],
        out_specs=[row] * 4,
        out_shape=[SDS((R, C), f32)] * 4,
        compiler_params=_cparams(("parallel",)),
        name=name,
    )(w, m, v, g)


def _sum8(parts, name):
    _, _, R, C = parts.shape

    def body(p_ref, o_ref):
        acc = p_ref[0, 0]
        for c in range(2):
            for k in range(4):
                if c or k:
                    acc = acc + p_ref[c, k]
        o_ref[...] = acc

    return pl.pallas_call(body, out_shape=SDS((R, C), f32), name=name)(parts)


def _pair_add(a, b, name):
    K, R, C = a.shape
    tr = R // 2 if R % 32 == 0 else R

    def body(a_ref, b_ref, o_ref):
        o_ref[...] = (a_ref[...].astype(f32) + b_ref[...].astype(f32)).astype(bf16)

    blk = pl.BlockSpec((1, tr, C), lambda k, i: (k, i, 0))
    return pl.pallas_call(
        body,
        grid=(K, R // tr),
        in_specs=[blk, blk],
        out_specs=blk,
        out_shape=SDS((K, R, C), bf16),
        compiler_params=_cparams(("parallel", "parallel")),
        name=name,
    )(a, b)


_ANY = pl.BlockSpec(memory_space=pl.ANY)


def _chip_copies(src_ref, out_ref, send_sems, recv_sems, gather):
    x, y, c = lax.axis_index("x"), lax.axis_index("y"), lax.axis_index("c")
    mine = 2 * x + y

    def piece(k):
        return src_ref if gather else src_ref.at[k]

    sends, recvs = [], []
    for j, (px, py) in enumerate([(1 - x, y), (x, 1 - y), (1 - x, 1 - y)]):
        sends.append(pltpu.make_async_remote_copy(
            src_ref=piece(2 * px + py), dst_ref=out_ref.at[mine], send_sem=send_sems.at[j],
            recv_sem=recv_sems.at[j], device_id=(px, py, c), device_id_type=MESH))
        recvs.append(pltpu.make_async_remote_copy(
            src_ref=piece(mine), dst_ref=out_ref.at[2 * px + py], send_sem=send_sems.at[j],
            recv_sem=recv_sems.at[j], device_id=(px, py, c), device_id_type=MESH))
    return sends, recvs


def _chip_start(src_ref, out_ref, send_sems, recv_sems, gather):
    for cp in _chip_copies(src_ref, out_ref, send_sems, recv_sems, gather)[0]:
        cp.start()


def _chip_finish(src_ref, out_ref, send_sems, recv_sems, gather):
    sends, recvs = _chip_copies(src_ref, out_ref, send_sems, recv_sems, gather)
    for cp in recvs:
        cp.wait_recv()
    for cp in sends:
        cp.wait_send()


def _chip_out_shape(src, gather):
    return SDS((4,) + tuple(src.shape if gather else src.shape[1:]), src.dtype)


_CHIP_SEMS = [pltpu.SemaphoreType.DMA((3,)), pltpu.SemaphoreType.DMA((3,))]


def _fill_own(out, src, gather):
    mine = 2 * lax.axis_index("x") + lax.axis_index("y")
    own = src if gather else lax.dynamic_index_in_dim(src, mine, axis=0, keepdims=False)
    return lax.dynamic_update_index_in_dim(out, own, mine, axis=0)


def _chip_comm(src, gather, name):
    def body(src_ref, out_ref, send_sems, recv_sems):
        _chip_start(src_ref, out_ref, send_sems, recv_sems, gather)
        _chip_finish(src_ref, out_ref, send_sems, recv_sems, gather)

    out = pl.pallas_call(
        body,
        in_specs=[_ANY],
        out_specs=_ANY,
        out_shape=_chip_out_shape(src, gather),
        scratch_shapes=list(_CHIP_SEMS),
        name=name,
    )(src)
    return _fill_own(out, src, gather)


def _core_gather(src, name):
    def body(src_ref, out_ref, send_sem, recv_sem):
        x, y, c = lax.axis_index("x"), lax.axis_index("y"), lax.axis_index("c")
        cp = pltpu.make_async_remote_copy(src_ref=src_ref, dst_ref=out_ref.at[c], send_sem=send_sem,
                                          recv_sem=recv_sem, device_id=(x, y, 1 - c), device_id_type=MESH)
        cp.start()
        pltpu.make_async_remote_copy(src_ref=src_ref, dst_ref=out_ref.at[1 - c], send_sem=send_sem,
                                     recv_sem=recv_sem, device_id=(x, y, 1 - c), device_id_type=MESH).wait_recv()
        cp.wait_send()

    out = pl.pallas_call(
        body,
        in_specs=[_ANY],
        out_specs=_ANY,
        out_shape=SDS((2,) + tuple(src.shape), src.dtype),
        scratch_shapes=[pltpu.SemaphoreType.DMA, pltpu.SemaphoreType.DMA],
        name=name,
    )(src)
    return lax.dynamic_update_index_in_dim(out, src, lax.axis_index("c"), axis=0)


def _core_swap(src, name):
    def body(src_ref, out_ref, send_sem, recv_sem):
        x, y, c = lax.axis_index("x"), lax.axis_index("y"), lax.axis_index("c")
        cp = pltpu.make_async_remote_copy(src_ref=src_ref.at[1 - c], dst_ref=out_ref, send_sem=send_sem,
                                          recv_sem=recv_sem, device_id=(x, y, 1 - c), device_id_type=MESH)
        cp.start()
        cp.wait()

    return pl.pallas_call(
        body,
        in_specs=[_ANY],
        out_specs=_ANY,
        out_shape=SDS(tuple(src.shape[1:]), src.dtype),
        scratch_shapes=[pltpu.SemaphoreType.DMA, pltpu.SemaphoreType.DMA],
        name=name,
    )(src)


def _all_gather(src, tag):
    by_chip = _chip_comm(src, True, tag + "_chips")
    both = _core_gather(by_chip, tag + "_cores")
    return jnp.swapaxes(both, 0, 1).reshape((8,) + tuple(src.shape))


_PACK_A = (("w_in", (1024, 1088)),)
_PACK_B = (("w_ba", (512, 128)), ("w_bh", (512, 128)), ("w_out", (128, 1024)), ("w_up", (1024, 704)),
           ("w_down", (352, 1024)))
_PACK_SIZES = _PACK_A + _PACK_B


def _slab_rows(sizes):
    return sum(r * c for _, (r, c) in sizes) // D_MODEL


def _pack_rows(d, sizes):
    n = d[sizes[0][0]].shape[0]
    return jnp.concatenate([d[k].reshape(n, -1, D_MODEL) for k, _ in sizes], axis=1)


def _unpack_rows(slab, sizes):
    n = slab.shape[0]
    out, lo = {}, 0
    for key, (r, c) in sizes:
        rows = r * c // D_MODEL
        out[key] = slab[:, lo:lo + rows].reshape(n, r, c)
        lo += rows
    return out


def _by_core(gslab):
    return jnp.swapaxes(gslab.reshape((4, 2) + gslab.shape[1:]), 0, 1)


def _pair_sum(by_core, tag):
    from_sib = _core_swap(by_core, tag + "_cores")
    mine = lax.dynamic_index_in_dim(by_core, lax.axis_index("c"), axis=0, keepdims=False)
    return _pair_add(mine, from_sib, tag + "_pair_add")


def _cols_to_full(t):
    return jnp.swapaxes(t, 0, 1).reshape(t.shape[1], -1)


def _full_to_cols(t):
    K = t.shape[0]
    return jnp.swapaxes(t.reshape(K, 8, -1), 0, 1)


_SMALL = (("pre_mix_norm", (1, 1024)), ("rel_bias", (32, 24)), ("hgrn_lb_raw", (2, 512)), ("hgrn_norm", (1, 128)),
          ("post_mix_norm", (1, 1024)), ("pre_ffn_norm", (1, 1024)), ("conv_b", (1, 5632)),
          ("post_ffn_norm", (1, 1024)))
_SMALL_ROWS = 96
_CONVW_ROWS = 136


def _pack_small(d):
    flat = jnp.concatenate([d[k].reshape(-1) for k, _ in _SMALL])
    flat = jnp.pad(flat, (0, _SMALL_ROWS * LANE - flat.shape[0]))
    return flat.reshape(_SMALL_ROWS, LANE)


def _unpack_small(p):
    flat = p.reshape(-1)
    out, lo = {}, 0
    for k, shp in _SMALL:
        n = shp[0] * shp[1]
        out[k] = flat[lo:lo + n].reshape(shp)
        lo += n
    return out


def _local_step(x, tgt, WA, P, plan):
    S = x.shape[0]
    W = dict(WA)
    lb = _lb_fwd(P["hgrn_lb_raw"])
    xs = [x, _permute(x, 4, "perm_x4"), _permute(x, 16, "perm_x16")]
    hs = [_rmsnorm(xs[g], P["pre_mix_norm"], f"norm_pre{g}") for g in range(N_GROUPS)]
    h1 = hs[0]
    consts = [_bias_consts(d) for d in DILATIONS]
    qkv, obuf, lbuf, biases = [], [], [], []
    for g, d in enumerate(DILATIONS):
        qkv_g = _mm(hs[g], W["w_qkv"][g], "nn", bf16, f"proj_qkv{g}")
        tab_t = P["rel_bias"][:, 8 * g:8 * g + 8].T
        bias_g = _bias_build(tab_t, consts[g][0], consts[g][1], f"bias_build{g}").reshape(8, ATTN_BLOCK, 2 * ATTN_BLOCK)
        o_g, l_g = _attn_fwd(qkv_g, bias_g, (S // d) // ATTN_BLOCK, f"attn_fwd{g}")
        qkv.append(qkv_g)
        biases.append(bias_g)
        lbuf.append(l_g)
        obuf.append(_unpermute(o_g, d, f"unperm_o{g}"))
    l_nat = [_unpermute(lbuf[g], d, f"unperm_l{g}") for g, d in enumerate(DILATIONS)]
    y_attn, y_attn_b, w0, w1, w2 = _attn_merge(obuf[0], obuf[1], obuf[2], l_nat[0], l_nat[1], l_nat[2])
    hg = _mm(h1, W["w_hg"], "nn", f32, "proj_hg")
    gc = _mm(h1, W["w_gate"], "nn", bf16, "proj_gate")
    y_hgrn, o_raw, ck, got = _hgrn_fwd(hg, lb, P["hgrn_norm"], plan.fwd_ride())
    W.update(plan.weights(got))
    a = _mm(y_attn_b, W["w_ba"], "nn", bf16, "branch_attn")
    b = _mm(y_hgrn, W["w_bh"], "nn", bf16, "branch_hgrn")
    merged = _gate_fwd(a, b, gc)
    mo = _mm(merged, W["w_out"], "nn", f32, "out_proj")
    x1, h2 = _mid_fwd(x, mo, P["post_mix_norm"], P["pre_ffn_norm"])
    ug = _mm(h2, W["w_up_g"], "nn", bf16, "up_gate")
    uv = _mm(h2, W["w_up_v"], "nn", bf16, "up_val")
    cw_g, cw_v = P["conv_w"][:, :D_FF], P["conv_w"][:, D_FF:]
    cb_g, cb_v = P["conv_b"][:, :D_FF], P["conv_b"][:, D_FF:]
    act = _conv_fwd(ug, uv, cw_g, cw_v, cb_g, cb_v)
    fo = _mm(act, W["w_down"], "nn", f32, "down_proj")
    loss, dy, dfo, g_post_ffn = _final(x1, fo, tgt, P["post_ffn_norm"])
    dact = _mm(dfo, W["w_down"], "nt", bf16, "d_act")
    gW_down = _mm(act, dfo, "tn", f32, "gw_down")
    dug, duv, st_g, st_v = _conv_bwd(ug, uv, dact, cw_g, cw_v, cb_g, cb_v)
    dh2 = _mm(dug, W["w_up_g"], "nt", f32, "dh2_gate")
    dh2 = _mm(duv, W["w_up_v"], "nt", f32, "dh2_val", acc=dh2)
    gW_up = jnp.concatenate([_mm(h2, dug, "tn", f32, "gw_up_gate"), _mm(h2, duv, "tn", f32, "gw_up_val")], axis=1)
    dx1, dmo, g_pre_ffn, g_post_mix = _mid_bwd(dy, dh2, x1, mo, P["pre_ffn_norm"], P["post_mix_norm"])
    dmerged = _mm(dmo, W["w_out"], "nt", bf16, "d_merged")
    gW_out = _mm(merged, dmo, "tn", f32, "gw_out")
    da, db, dgc = _gate_bwd(dmerged, a, b, gc)
    dyattn = _mm(da, W["w_ba"], "nt", f32, "d_yattn")
    gW_ba = _mm(y_attn_b, da, "tn", f32, "gw_ba")
    dyhgrn = _mm(db, W["w_bh"], "nt", f32, "d_yhgrn")
    gW_bh = _mm(y_hgrn, db, "tn", f32, "gw_bh")
    big_b = dict(w_ba=gW_ba, w_bh=gW_bh, w_out=gW_out, w_up=gW_up, w_down=gW_down)
    dq_h, df_h, dv_h, dog_h, glb8, gnw8, got_b = _hgrn_bwd(hg, o_raw, dyhgrn, ck, lb, P["hgrn_norm"],
                                                          plan.bwd_ride(big_b))
    dhg = jnp.concatenate([dq_h, df_h, dv_h, dog_h], axis=1)
    g_lb_raw = _lb_bwd(P["hgrn_lb_raw"], glb8[0:1])
    gn = gnw8[0:1]
    g_hgrn_norm = (gn[:, 0:128] + gn[:, 128:256]) + (gn[:, 256:384] + gn[:, 384:512])
    dos = _attn_merge_bwd(dyattn, y_attn, w0, w1, w2)
    dh_parts, gW_qkv, g_rel = [], [], []
    for g, d in enumerate(DILATIONS):
        do_g = _permute(dos[g], d, f"perm_do{g}")
        dvec_g = _permute(dos[3 + g], d, f"perm_dvec{g}")
        dq, dk, dv, dbias = _attn_bwd(qkv[g], biases[g], do_g, dvec_g, lbuf[g], (S // d) // ATTN_BLOCK, f"attn_bwd{g}")
        dqkv = jnp.concatenate([dq, dk, dv], axis=1)
        gW_qkv.append(_mm(hs[g], dqkv, "tn", f32, f"gw_qkv{g}"))
        dh_g = _mm(dqkv, W["w_qkv"][g], "nt", f32, f"dh1_qkv{g}")
        dh_parts.append(_unpermute(dh_g, d, f"unperm_dh{g}"))
        g_rel.append(_bias_grad(dbias.reshape(8, -1), consts[g][0], f"bias_grad{g}"))
    dh_main = _mm(dhg, W["w_hg"], "nt", f32, "dh1_hg", acc=dh_parts[0])
    dh_main = _mm(dgc, W["w_gate"], "nt", f32, "dh1_gate", acc=dh_main)
    gW_hg = _mm(h1, dhg, "tn", f32, "gw_hg")
    gW_gate = _mm(h1, dgc, "tn", f32, "gw_gate")
    grad_x, g_pre_mix = _first_bwd(x, dx1, dh_main, dh_parts[1], dh_parts[2], P["pre_mix_norm"])

    gW_in = jnp.concatenate(gW_qkv + [gW_hg, gW_gate], axis=1)
    g_conv_w = jnp.concatenate([st_g[0:3], st_v[0:3]], axis=1)
    g_conv_b = jnp.concatenate([st_g[3:4], st_v[3:4]], axis=1)
    small = dict(pre_mix_norm=g_pre_mix, rel_bias=jnp.concatenate(g_rel, axis=1), hgrn_lb_raw=g_lb_raw,
                 hgrn_norm=g_hgrn_norm, post_mix_norm=g_post_mix, pre_ffn_norm=g_pre_ffn, conv_b=g_conv_b,
                 post_ffn_norm=g_post_ffn, conv_w=g_conv_w)
    return loss, grad_x, gW_in, big_b, got_b, small


def _weights_a(slabs):
    w_in = _cols_to_full(_unpack_rows(slabs, _PACK_A)["w_in"])
    return dict(
        w_qkv=[w_in[:, g * QKV_G:(g + 1) * QKV_G] for g in range(N_GROUPS)],
        w_hg=w_in[:, 3 * QKV_G:3 * QKV_G + 4 * HGRN_W],
        w_gate=w_in[:, 3 * QKV_G + 4 * HGRN_W:],
    )


def _weights_b(slabs):
    sh = _unpack_rows(slabs, _PACK_B)
    w_up = _cols_to_full(sh["w_up"])
    return dict(
        w_ba=_cols_to_full(sh["w_ba"]),
        w_bh=_cols_to_full(sh["w_bh"]),
        w_out=sh["w_out"].reshape(D_MODEL, D_MODEL),
        w_up_g=w_up[:, :D_FF],
        w_up_v=w_up[:, D_FF:],
        w_down=sh["w_down"].reshape(D_FF, D_MODEL),
    )


def _grad_slabs(g, sizes):
    cut = dict(w_in=_full_to_cols, w_ba=_full_to_cols, w_bh=_full_to_cols, w_up=_full_to_cols,
               w_out=lambda t: t.reshape(8, 128, D_MODEL), w_down=lambda t: t.reshape(8, 352, D_MODEL))
    return _pack_rows({k: cut[k](g[k]).astype(bf16) for k, _ in sizes}, sizes)


class _SlabB:
    def __init__(self, slab):
        self.slab = slab
        self.chip_sum = None

    def fwd_ride(self):
        return (self.slab, True)

    def weights(self, got):
        both = _core_gather(_fill_own(got, self.slab, True), "ag_b_cores")
        return _weights_b(jnp.swapaxes(both, 0, 1).reshape((8,) + tuple(self.slab.shape)))

    def bwd_ride(self, grads):
        self.chip_sum = _pair_sum(_by_core(_grad_slabs(grads, _PACK_B)), "rs_b")
        return (self.chip_sum, False)

    def parts(self, got_b):
        return _unpack_rows(_fill_own(got_b, self.chip_sum, False), _PACK_B)


def kernel(x, pre_mix_norm, w_in, rel_bias, hgrn_lb_raw, hgrn_norm, w_branch_attn, w_branch_hgrn, w_out, post_mix_norm, pre_ffn_norm, w_up, conv_w, conv_b, w_down, post_ffn_norm, loss_target, m_pre_mix_norm, m_w_in, m_rel_bias, m_hgrn_lb_raw, m_hgrn_norm, m_w_branch_attn, m_w_branch_hgrn, m_w_out, m_post_mix_norm, m_pre_ffn_norm, m_w_up, m_conv_w, m_conv_b, m_w_down, m_post_ffn_norm, v_pre_mix_norm, v_w_in, v_rel_bias, v_hgrn_lb_raw, v_hgrn_norm, v_w_branch_attn, v_w_branch_hgrn, v_w_out, v_post_mix_norm, v_pre_ffn_norm, v_w_up, v_conv_w, v_conv_b, v_w_down, v_post_ffn_norm):
    ci = lax.axis_index("c")
    dev = 4 * lax.axis_index("x") + 2 * lax.axis_index("y") + ci
    wts = dict(w_in=w_in[0], w_ba=w_branch_attn[0], w_bh=w_branch_hgrn[0], w_out=w_out[0], w_up=w_up[0],
               w_down=w_down[0])
    mom = dict(w_in=m_w_in[0], w_ba=m_w_branch_attn[0], w_bh=m_w_branch_hgrn[0], w_out=m_w_out[0], w_up=m_w_up[0],
               w_down=m_w_down[0])
    var = dict(w_in=v_w_in[0], w_ba=v_w_branch_attn[0], w_bh=v_w_branch_hgrn[0], w_out=v_w_out[0], w_up=v_w_up[0],
               w_down=v_w_down[0])
    small_w = dict(pre_mix_norm=pre_mix_norm, rel_bias=rel_bias, hgrn_lb_raw=hgrn_lb_raw, hgrn_norm=hgrn_norm,
                   post_mix_norm=post_mix_norm, pre_ffn_norm=pre_ffn_norm, conv_b=conv_b, post_ffn_norm=post_ffn_norm)
    small_m = dict(pre_mix_norm=m_pre_mix_norm, rel_bias=m_rel_bias, hgrn_lb_raw=m_hgrn_lb_raw, hgrn_norm=m_hgrn_norm,
                   post_mix_norm=m_post_mix_norm, pre_ffn_norm=m_pre_ffn_norm, conv_b=m_conv_b,
                   post_ffn_norm=m_post_ffn_norm)
    small_v = dict(pre_mix_norm=v_pre_mix_norm, rel_bias=v_rel_bias, hgrn_lb_raw=v_hgrn_lb_raw, hgrn_norm=v_hgrn_norm,
                   post_mix_norm=v_post_mix_norm, pre_ffn_norm=v_pre_ffn_norm, conv_b=v_conv_b,
                   post_ffn_norm=v_post_ffn_norm)

    shard16 = {k: wts[k].astype(bf16)[None] for k, _ in _PACK_SIZES}
    WA = _weights_a(_all_gather(_pack_rows(shard16, _PACK_A)[0], "ag_a"))
    plan = _SlabB(_pack_rows(shard16, _PACK_B)[0])
    cw_pad = jnp.pad(conv_w[0], ((0, SUBLANE - 3), (0, 768 - 704)))
    conv_w_full = _cols_to_full(_all_gather(cw_pad, "ag_convw")[:, 0:3, 0:704])
    P = dict(small_w)
    P["conv_w"] = conv_w_full

    loss8, grad_x, gW_in, _, got_b, small = _local_step(x[0], loss_target[0], WA, P, plan)
    loss = lax.psum(loss8[0, 0], ("x", "y", "c"))

    chip_sum_a = _pair_sum(_by_core(_grad_slabs(dict(w_in=gW_in), _PACK_A)), "rs_a")
    parts = _unpack_rows(_chip_comm(chip_sum_a, False, "rs_a_chips"), _PACK_A)
    parts.update(plan.parts(got_b))
    outs_big = {}
    for k, _ in _PACK_SIZES:
        outs_big[k] = _adamw(wts[k], mom[k], var[k], parts[k], "adamw_" + k)

    spack = jnp.concatenate([_pack_small(small),
                             jnp.pad(small["conv_w"].reshape(-1, LANE), ((0, _CONVW_ROWS - 132), (0, 0)))], axis=0)
    allp = _core_gather(_chip_comm(spack, True, "ag_small_chips"), "ag_small_cores")
    ssum = _sum8(allp, "small_sum")
    gs = ssum[:_SMALL_ROWS]
    res_small = _adamw(_pack_small(small_w), _pack_small(small_m), _pack_small(small_v), gs, "adamw_small")
    sm = [_unpack_small(t) for t in res_small]
    g_cw_full = ssum[_SMALL_ROWS:_SMALL_ROWS + 132].reshape(3, 2 * D_FF)
    g_cw = lax.dynamic_slice_in_dim(g_cw_full, dev * 704, 704, axis=1)
    res_cw = _adamw(conv_w[0], m_conv_w[0], v_conv_w[0], g_cw, "adamw_conv_w")

    def pick(i):
        def big_(k):
            return outs_big[k][i][None]
        return [sm[i]["pre_mix_norm"], big_("w_in"), sm[i]["rel_bias"], sm[i]["hgrn_lb_raw"], sm[i]["hgrn_norm"],
                big_("w_ba"), big_("w_bh"), big_("w_out"), sm[i]["post_mix_norm"], sm[i]["pre_ffn_norm"],
                big_("w_up"), res_cw[i][None], sm[i]["conv_b"], big_("w_down"), sm[i]["post_ffn_norm"]]

    return (loss, grad_x[None], *pick(0), *pick(1), *pick(2), *pick(3))
```

```python
import functools
import math

import jax
import jax.numpy as jnp
from jax import lax
from jax.experimental import pallas as pl
from jax.experimental.pallas import tpu as pltpu

f32 = jnp.float32
bf16 = jnp.bfloat16
SDS = jax.ShapeDtypeStruct
HIGHEST = lax.Precision.HIGHEST
MESH = pl.DeviceIdType.MESH

NN = (((1,), (0,)), ((), ()))
NT = (((1,), (1,)), ((), ()))
TN = (((0,), (0,)), ((), ()))

D_MODEL = 1024
N_GROUPS = 3
DILATIONS = (1, 4, 16)
HEAD_DIM = 64
ATTN_BLOCK = 128
QKV_G = 1536
ATTN_OUT = 512
HGRN_W = 512
HGRN_CHUNK = 32
D_FF = 2816
NUM_BUCKETS = 32
MAX_EXACT = 16
MAX_DISTANCE = 2048
NEG_INF = -1e30
EPS = 1e-6
LANE = 128
SUBLANE = 8
VMEM_BIG = 48 * 1024 * 1024
MM_ROWS = 512
MM_OUT_BYTES = 8 * 1024 * 1024

ADAM_LR, ADAM_B1, ADAM_B2, ADAM_EPS, ADAM_WD, ADAM_STEP = 0.001, 0.9, 0.999, 1e-08, 0.01, 10


def _pick(n, pref):
    t = pref
    while t >= LANE:
        if n % t == 0:
            return t
        t //= 2
    return n


def _cparams(sem=None, vmem=None):
    kw = {}
    if sem is not None:
        kw["dimension_semantics"] = sem
    if vmem is not None:
        kw["vmem_limit_bytes"] = vmem
    return pltpu.CompilerParams(**kw)


def _sigmoid(x):
    return jax.nn.sigmoid(x)


def _colsum8(x):
    return x.reshape(x.shape[0] // SUBLANE, SUBLANE, x.shape[1]).sum(axis=0)


def _mm(a, b, mode, out_dtype, name, acc=None):
    dims = {"nn": NN, "nt": NT, "tn": TN}[mode]
    has_acc = acc is not None
    if mode == "tn":
        assert not has_acc
        bs = list(b) if isinstance(b, (list, tuple)) else [b]
        K, M = a.shape
        widths = [t.shape[1] for t in bs]
        N = sum(widths)
        tmm = M if M * N * 4 <= MM_OUT_BYTES else M // 2
        ts = _pick(K, MM_ROWS)
        nk = K // ts

        def body_tn(a_ref, *refs):
            o_ref = refs[-1]
            k = pl.program_id(1)
            av = a_ref[...]
            lo = 0
            for b_ref, w in zip(refs[:-1], widths):
                part = lax.dot_general(av, b_ref[...], dims, preferred_element_type=f32)
                cols = slice(lo, lo + w)
                lo += w

                @pl.when(k == 0)
                def _(part=part, cols=cols):
                    o_ref[:, cols] = part

                @pl.when(k > 0)
                def _(part=part, cols=cols):
                    o_ref[:, cols] += part

        return pl.pallas_call(
            body_tn,
            grid=(M // tmm, nk),
            in_specs=[pl.BlockSpec((ts, tmm), lambda i, k: (k, i))]
            + [pl.BlockSpec((ts, w), lambda i, k: (k, 0)) for w in widths],
            out_specs=pl.BlockSpec((tmm, N), lambda i, k: (i, 0)),
            out_shape=SDS((M, N), out_dtype),
            compiler_params=_cparams(("parallel", "arbitrary"), VMEM_BIG),
            name=name,
        )(a, *bs)

    parts = list(a) if isinstance(a, (list, tuple)) else [a]
    assert mode == "nt" or len(parts) == 1
    widths = [t.shape[1] for t in parts]
    M = parts[0].shape[0]
    N = b.shape[1] if mode == "nn" else b.shape[0]
    tm = _pick(M, MM_ROWS)
    npart = len(parts)

    def body(*refs):
        a_refs, b_ref = refs[:npart], refs[npart]
        c_ref = refs[npart + 1] if has_acc else None
        o_ref = refs[-1]
        if npart == 1:
            part = lax.dot_general(a_refs[0][...], b_ref[...], dims, preferred_element_type=f32)
        else:
            part, lo = None, 0
            for a_ref, w in zip(a_refs, widths):
                t = lax.dot_general(a_ref[...], b_ref[:, lo:lo + w], dims, preferred_element_type=f32)
                part = t if part is None else part + t
                lo += w
        if has_acc:
            part = part + c_ref[...]
        o_ref[...] = part.astype(out_dtype)

    specs = [pl.BlockSpec((tm, w), lambda i: (i, 0)) for w in widths] + [pl.BlockSpec(b.shape, lambda i: (0, 0))]
    args = parts + [b]
    aliases = {}
    if has_acc:
        specs.append(pl.BlockSpec((tm, N), lambda i: (i, 0)))
        args.append(acc)
        aliases = {npart + 1: 0}
    return pl.pallas_call(
        body,
        grid=(M // tm,),
        in_specs=specs,
        out_specs=pl.BlockSpec((tm, N), lambda i: (i, 0)),
        out_shape=SDS((M, N), out_dtype),
        input_output_aliases=aliases,
        compiler_params=_cparams(("parallel",), VMEM_BIG),
        name=name,
    )(*args)


def _permute(x, d, name):
    if d == 1:
        return x
    S, C = x.shape
    U = S // d

    def body(x_ref, o_ref):
        for r in range(d):
            o_ref[r] = x_ref[pl.ds(r, ATTN_BLOCK, stride=d), :]

    out = pl.pallas_call(
        body,
        grid=(U // ATTN_BLOCK, C // LANE),
        in_specs=[pl.BlockSpec((ATTN_BLOCK * d, LANE), lambda i, j: (i, j))],
        out_specs=pl.BlockSpec((d, ATTN_BLOCK, LANE), lambda i, j: (0, i, j)),
        out_shape=SDS((d, U, C), x.dtype),
        compiler_params=_cparams(("parallel", "parallel")),
        name=name,
    )(x)
    return out.reshape(S, C)


def _unpermute(x, d, name):
    if d == 1:
        return x
    S, C = x.shape
    U = S // d

    def body(x_ref, o_ref):
        for r in range(d):
            o_ref[pl.ds(r, ATTN_BLOCK, stride=d), :] = x_ref[r]

    return pl.pallas_call(
        body,
        grid=(U // ATTN_BLOCK, C // LANE),
        in_specs=[pl.BlockSpec((d, ATTN_BLOCK, LANE), lambda i, j: (0, i, j))],
        out_specs=pl.BlockSpec((ATTN_BLOCK * d, LANE), lambda i, j: (i, j)),
        out_shape=SDS((S, C), x.dtype),
        compiler_params=_cparams(("parallel", "parallel")),
        name=name,
    )(x.reshape(d, U, C))


def _rms_parts(xv):
    r = lax.rsqrt(jnp.mean(xv * xv, axis=-1, keepdims=True) + EPS)
    return r, xv * r


def _rms_bwd(xhat, r, w, dy):
    dyw = dy * w
    return r * (dyw - xhat * jnp.mean(dyw * xhat, axis=-1, keepdims=True))


def _rmsnorm(x, w, name):
    S, D = x.shape
    tm = _pick(S, 512)

    def body(x_ref, w_ref, o_ref):
        _, xh = _rms_parts(x_ref[...])
        o_ref[...] = (xh * w_ref[...]).astype(bf16)

    return pl.pallas_call(
        body,
        grid=(S // tm,),
        in_specs=[pl.BlockSpec((tm, D), lambda i: (i, 0)), pl.BlockSpec((1, D), lambda i: (0, 0))],
        out_specs=pl.BlockSpec((tm, D), lambda i: (i, 0)),
        out_shape=SDS((S, D), bf16),
        compiler_params=_cparams(("parallel",)),
        name=name,
    )(x, w)


def _mid_fwd(x, mo, w_pm, w_pf):
    S, D = x.shape
    tm = _pick(S, 512)

    def body(x_ref, mo_ref, wpm_ref, wpf_ref, x1_ref, h2_ref):
        _, moh = _rms_parts(mo_ref[...])
        x1 = x_ref[...] + moh * wpm_ref[...]
        x1_ref[...] = x1
        _, x1h = _rms_parts(x1)
        h2_ref[...] = (x1h * wpf_ref[...]).astype(bf16)

    row = pl.BlockSpec((tm, D), lambda i: (i, 0))
    vec = pl.BlockSpec((1, D), lambda i: (0, 0))
    return pl.pallas_call(
        body,
        grid=(S // tm,),
        in_specs=[row, row, vec, vec],
        out_specs=[row, row],
        out_shape=[SDS((S, D), f32), SDS((S, D), bf16)],
        compiler_params=_cparams(("parallel",)),
        name="mid_fwd",
    )(x, mo, w_pm, w_pf)


def _final(x1, fo, tgt, w_pfn):
    S, D = x1.shape
    tm = _pick(S, 512)
    nt = S // tm

    def body(x1_ref, fo_ref, t_ref, w_ref, loss_ref, dy_ref, dfo_ref, gw_ref, lacc, gacc):
        i = pl.program_id(0)

        @pl.when(i == 0)
        def _():
            lacc[...] = jnp.zeros_like(lacc)
            gacc[...] = jnp.zeros_like(gacc)

        w = w_ref[...]
        r, foh = _rms_parts(fo_ref[...])
        y = x1_ref[...] + foh * w
        err = y - t_ref[...]
        lacc[...] += _colsum8(err * err)
        dy = err * (1.0 / D)
        dy_ref[...] = dy
        gacc[...] += _colsum8(dy * foh)
        dfo_ref[...] = _rms_bwd(foh, r, w, dy).astype(bf16)

        @pl.when(i == nt - 1)
        def _():
            loss_ref[...] = jnp.full((SUBLANE, LANE), 0.5 / D, f32) * jnp.sum(lacc[...])
            gw_ref[...] = jnp.sum(gacc[...], axis=0, keepdims=True)

    row = pl.BlockSpec((tm, D), lambda i: (i, 0))
    vec = pl.BlockSpec((1, D), lambda i: (0, 0))
    return pl.pallas_call(
        body,
        grid=(nt,),
        in_specs=[row, row, row, vec],
        out_specs=[pl.BlockSpec((SUBLANE, LANE), lambda i: (0, 0)), row, row, vec],
        out_shape=[SDS((SUBLANE, LANE), f32), SDS((S, D), f32), SDS((S, D), bf16), SDS((1, D), f32)],
        scratch_shapes=[pltpu.VMEM((SUBLANE, D), f32), pltpu.VMEM((SUBLANE, D), f32)],
        compiler_params=_cparams(("arbitrary",)),
        name="final_loss",
    )(x1, fo, tgt, w_pfn)


def _mid_bwd(dy, dh2, x1, mo, w_pf, w_pm):
    S, D = dy.shape
    tm = _pick(S, 512)
    nt = S // tm

    def body(dy_ref, dh2_ref, x1_ref, mo_ref, wpf_ref, wpm_ref, dx1_ref, dmo_ref, gpf_ref, gpm_ref, apf, apm):
        i = pl.program_id(0)

        @pl.when(i == 0)
        def _():
            apf[...] = jnp.zeros_like(apf)
            apm[...] = jnp.zeros_like(apm)

        r1, x1h = _rms_parts(x1_ref[...])
        dh2 = dh2_ref[...]
        apf[...] += _colsum8(dh2 * x1h)
        dx1 = dy_ref[...] + _rms_bwd(x1h, r1, wpf_ref[...], dh2)
        dx1_ref[...] = dx1
        rm, moh = _rms_parts(mo_ref[...])
        apm[...] += _colsum8(dx1 * moh)
        dmo_ref[...] = _rms_bwd(moh, rm, wpm_ref[...], dx1).astype(bf16)

        @pl.when(i == nt - 1)
        def _():
            gpf_ref[...] = jnp.sum(apf[...], axis=0, keepdims=True)
            gpm_ref[...] = jnp.sum(apm[...], axis=0, keepdims=True)

    row = pl.BlockSpec((tm, D), lambda i: (i, 0))
    vec = pl.BlockSpec((1, D), lambda i: (0, 0))
    return pl.pallas_call(
        body,
        grid=(nt,),
        in_specs=[row, row, row, row, vec, vec],
        out_specs=[row, row, vec, vec],
        out_shape=[SDS((S, D), f32), SDS((S, D), bf16), SDS((1, D), f32), SDS((1, D), f32)],
        scratch_shapes=[pltpu.VMEM((SUBLANE, D), f32), pltpu.VMEM((SUBLANE, D), f32)],
        compiler_params=_cparams(("arbitrary",)),
        name="mid_bwd",
    )(dy, dh2, x1, mo, w_pf, w_pm)


def _first_bwd(x, dx1, dh_a, dh_b, dh_c, w_pre):
    S, D = x.shape
    tm = _pick(S, 512)
    nt = S // tm

    def body(x_ref, dx1_ref, a_ref, b_ref, c_ref, w_ref, gx_ref, gw_ref, acc):
        i = pl.program_id(0)

        @pl.when(i == 0)
        def _():
            acc[...] = jnp.zeros_like(acc)

        r, xh = _rms_parts(x_ref[...])
        dh = (a_ref[...] + b_ref[...]) + c_ref[...]
        acc[...] += _colsum8(dh * xh)
        gx_ref[...] = dx1_ref[...] + _rms_bwd(xh, r, w_ref[...], dh)

        @pl.when(i == nt - 1)
        def _():
            gw_ref[...] = jnp.sum(acc[...], axis=0, keepdims=True)

    row = pl.BlockSpec((tm, D), lambda i: (i, 0))
    vec = pl.BlockSpec((1, D), lambda i: (0, 0))
    return pl.pallas_call(
        body,
        grid=(nt,),
        in_specs=[row, row, row, row, row, vec],
        out_specs=[row, vec],
        out_shape=[SDS((S, D), f32), SDS((1, D), f32)],
        scratch_shapes=[pltpu.VMEM((SUBLANE, D), f32)],
        compiler_params=_cparams(("arbitrary",)),
        name="first_bwd",
    )(x, dx1, dh_a, dh_b, dh_c, w_pre)


def _t5_bucket(dist):
    n = jnp.maximum(dist, 0)
    nf = jnp.maximum(n, 1).astype(f32)
    large = MAX_EXACT + (jnp.log(nf / MAX_EXACT) / math.log(MAX_DISTANCE / MAX_EXACT)
                         * (NUM_BUCKETS - MAX_EXACT)).astype(jnp.int32)
    large = jnp.minimum(large, NUM_BUCKETS - 1)
    return jnp.where(n < MAX_EXACT, n, large)


def _bias_consts(d):
    blk = ATTN_BLOCK
    rel = jnp.arange(blk)[:, None] + blk - jnp.arange(2 * blk)[None, :]
    in_win = (rel >= 0) & (rel <= blk)
    bucket = _t5_bucket(rel * d).reshape(1, -1)
    onehot = (bucket == jnp.arange(NUM_BUCKETS)[:, None]).astype(f32)
    return onehot, in_win.astype(f32).reshape(1, -1)


def _bias_build(tab_t, onehot, maskf, name):
    H = tab_t.shape[0]

    def body(t_ref, oh_ref, m_ref, o_ref):
        b = jnp.dot(t_ref[...], oh_ref[...], precision=HIGHEST, preferred_element_type=f32)
        o_ref[...] = jnp.where(m_ref[...] > 0.5, b, NEG_INF)

    return pl.pallas_call(body, out_shape=SDS((H, onehot.shape[1]), f32), name=name)(tab_t, onehot, maskf)


def _bias_grad(dbias_flat, onehot, name):
    H = dbias_flat.shape[0]

    def body(g_ref, oh_ref, o_ref):
        o_ref[...] = lax.dot_general(oh_ref[...], g_ref[...], NT, precision=HIGHEST, preferred_element_type=f32)

    return pl.pallas_call(body, out_shape=SDS((NUM_BUCKETS, H), f32), name=name)(dbias_flat, onehot)


ATTN_TILE = 512
ATTN_SUB = ATTN_TILE // ATTN_BLOCK


def _qkv_specs(nt):
    tile = (ATTN_TILE, LANE)
    blk = (ATTN_BLOCK, LANE)
    cur = lambda off: (lambda h, t: (jnp.minimum(t, nt - 1), off + h))
    prev = lambda off: (lambda h, t: (jnp.maximum(jnp.minimum(t, nt - 1) * ATTN_SUB - 1, 0), off + h))
    return [pl.BlockSpec(tile, cur(0)), pl.BlockSpec(blk, prev(4)), pl.BlockSpec(tile, cur(4)),
            pl.BlockSpec(blk, prev(8)), pl.BlockSpec(tile, cur(8))]


def _head_masks():
    lane = lax.broadcasted_iota(jnp.int32, (ATTN_BLOCK, LANE), 1)
    return lane < HEAD_DIM


def _attn_fwd(qkv, bias, bps, name):
    S = qkv.shape[0]
    nt = S // ATTN_TILE
    scale = HEAD_DIM ** -0.5

    def body(q_ref, kp_ref, kc_ref, vp_ref, vc_ref, b_ref, o_ref, l_ref):
        t = pl.program_id(1)
        kk = jnp.concatenate([kp_ref[...], kc_ref[...]], axis=0)
        vv = jnp.concatenate([vp_ref[...], vc_ref[...]], axis=0)
        low = _head_masks()
        col = lax.broadcasted_iota(jnp.int32, (ATTN_BLOCK, 2 * ATTN_BLOCK), 1)
        for b in range(ATTN_SUB):
            lo = b * ATTN_BLOCK
            rows = slice(lo, lo + ATTN_BLOCK)
            keys = slice(lo, lo + 2 * ATTN_BLOCK)
            dead = jnp.logical_and((t * ATTN_SUB + b) % bps == 0, col < ATTN_BLOCK)
            q2 = q_ref[rows, :]
            kb, vb = kk[keys], vv[keys]
            outs, lses = [], []
            for h in range(2):
                hm = low if h == 0 else jnp.logical_not(low)
                qh = jnp.where(hm, q2, jnp.zeros_like(q2))
                s = lax.dot_general(qh, kb, NT, preferred_element_type=f32) * scale + b_ref[h]
                s = jnp.where(dead, NEG_INF, s)
                m = jnp.max(s, axis=-1, keepdims=True)
                p = jnp.exp(s - m)
                l = jnp.sum(p, axis=-1, keepdims=True)
                outs.append(jnp.dot(p.astype(bf16), vb, preferred_element_type=f32) / l)
                lses.append(m + jnp.log(l))
            o_ref[rows, :] = jnp.where(low, outs[0], outs[1])
            l_ref[rows, :] = jnp.where(low, lses[0], lses[1])

    tile = pl.BlockSpec((ATTN_TILE, LANE), lambda h, t: (t, h))
    return pl.pallas_call(
        body,
        grid=(4, nt),
        in_specs=_qkv_specs(nt) + [pl.BlockSpec((2, ATTN_BLOCK, 2 * ATTN_BLOCK), lambda h, t: (h, 0, 0))],
        out_specs=[tile, tile],
        out_shape=[SDS((S, ATTN_OUT), f32), SDS((S, ATTN_OUT), f32)],
        compiler_params=_cparams(("parallel", "parallel")),
        name=name,
    )(qkv, qkv, qkv, qkv, qkv, bias)


def _attn_bwd(qkv, bias, do, dvec, lse, bps, name):
    S = qkv.shape[0]
    nt = S // ATTN_TILE
    scale = HEAD_DIM ** -0.5

    def assemble(parts):
        rows = [parts[0][:ATTN_BLOCK]]
        for b in range(ATTN_SUB - 1):
            rows.append(parts[b][ATTN_BLOCK:] + parts[b + 1][:ATTN_BLOCK])
        rows.append(parts[-1][ATTN_BLOCK:])
        return rows

    def body(q_ref, kp_ref, kc_ref, vp_ref, vc_ref, b_ref, do_ref, dvec_ref, lse_ref,
             dq_ref, dk_ref, dv_ref, db_ref, ck, cv):
        t = pl.program_id(1)
        last = ATTN_TILE - ATTN_BLOCK

        @pl.when(t == 0)
        def _():
            ck[...] = jnp.zeros_like(ck)
            cv[...] = jnp.zeros_like(cv)
            db_ref[...] = jnp.zeros_like(db_ref)

        @pl.when(t < nt)
        def _():
            kk = jnp.concatenate([kp_ref[...], kc_ref[...]], axis=0)
            vv = jnp.concatenate([vp_ref[...], vc_ref[...]], axis=0)
            low = _head_masks()
            col = lax.broadcasted_iota(jnp.int32, (ATTN_BLOCK, 2 * ATTN_BLOCK), 1)
            low2 = lax.broadcasted_iota(jnp.int32, (2 * ATTN_BLOCK, LANE), 1) < HEAD_DIM
            dk_parts, dv_parts = [], []
            dsum = [None, None]
            for b in range(ATTN_SUB):
                lo = b * ATTN_BLOCK
                rows = slice(lo, lo + ATTN_BLOCK)
                keys = slice(lo, lo + 2 * ATTN_BLOCK)
                dead = jnp.logical_and((t * ATTN_SUB + b) % bps == 0, col < ATTN_BLOCK)
                q2 = q_ref[rows, :]
                kb, vb = kk[keys], vv[keys]
                do2 = do_ref[rows, :].astype(bf16)
                dvec2 = dvec_ref[rows, :]
                lse2 = lse_ref[rows, :]
                dqs, dks, dvs = [], [], []
                for h in range(2):
                    hm = low if h == 0 else jnp.logical_not(low)
                    c0 = h * HEAD_DIM
                    qh = jnp.where(hm, q2, jnp.zeros_like(q2))
                    doh = jnp.where(hm, do2, jnp.zeros_like(do2))
                    s = lax.dot_general(qh, kb, NT, preferred_element_type=f32) * scale + b_ref[h]
                    s = jnp.where(dead, NEG_INF, s)
                    p = jnp.exp(s - lse2[:, c0:c0 + 1])
                    dp = lax.dot_general(doh, vb, NT, preferred_element_type=f32)
                    ds = p * (dp - dvec2[:, c0:c0 + 1])
                    dsum[h] = ds if dsum[h] is None else dsum[h] + ds
                    dsb = ds.astype(bf16)
                    dqs.append(jnp.dot(dsb, kb, preferred_element_type=f32) * scale)
                    dks.append(lax.dot_general(dsb, q2, TN, preferred_element_type=f32) * scale)
                    dvs.append(lax.dot_general(p.astype(bf16), do2, TN, preferred_element_type=f32))
                dq_ref[rows, :] = jnp.where(low, dqs[0], dqs[1]).astype(bf16)
                dk_parts.append(jnp.where(low2, dks[0], dks[1]))
                dv_parts.append(jnp.where(low2, dvs[0], dvs[1]))
            db_ref[0] += dsum[0]
            db_ref[1] += dsum[1]
            for parts, carry, out_ref in ((dk_parts, ck, dk_ref), (dv_parts, cv, dv_ref)):
                rws = assemble(parts)
                out_ref[:last, :] = carry[:last, :].astype(bf16)
                out_ref[last:, :] = (carry[last:, :] + rws[0]).astype(bf16)
                for b in range(ATTN_SUB):
                    carry[b * ATTN_BLOCK:(b + 1) * ATTN_BLOCK, :] = rws[b + 1]

        @pl.when(t == nt)
        def _():
            dk_ref[...] = ck[...].astype(bf16)
            dv_ref[...] = cv[...].astype(bf16)

    tile = (ATTN_TILE, LANE)
    cur = pl.BlockSpec(tile, lambda h, t: (jnp.minimum(t, nt - 1), h))
    lag = pl.BlockSpec(tile, lambda h, t: (jnp.maximum(t - 1, 0), h))
    bspec = pl.BlockSpec((2, ATTN_BLOCK, 2 * ATTN_BLOCK), lambda h, t: (h, 0, 0))
    return pl.pallas_call(
        body,
        grid=(4, nt + 1),
        in_specs=_qkv_specs(nt) + [bspec, cur, cur, cur],
        out_specs=[cur, lag, lag, bspec],
        out_shape=[SDS((S, ATTN_OUT), bf16), SDS((S, ATTN_OUT), bf16), SDS((S, ATTN_OUT), bf16),
                   SDS((8, ATTN_BLOCK, 2 * ATTN_BLOCK), f32)],
        scratch_shapes=[pltpu.VMEM(tile, f32), pltpu.VMEM(tile, f32)],
        compiler_params=_cparams(("parallel", "arbitrary")),
        name=name,
    )(qkv, qkv, qkv, qkv, qkv, bias, do, dvec, lse)


def _attn_merge(o0, o1, o2, l0, l1, l2):
    S, W = o0.shape
    tm = _pick(S, 512)

    def body(o0_ref, o1_ref, o2_ref, l0_ref, l1_ref, l2_ref, y_ref, yb_ref, w0_ref, w1_ref, w2_ref):
        a, b, c = l0_ref[...], l1_ref[...], l2_ref[...]
        m = jnp.maximum(jnp.maximum(a, b), c)
        ea, eb, ec = jnp.exp(a - m), jnp.exp(b - m), jnp.exp(c - m)
        den = (ea + eb) + ec
        w0, w1, w2 = ea / den, eb / den, ec / den
        y = (w0 * o0_ref[...] + w1 * o1_ref[...]) + w2 * o2_ref[...]
        y_ref[...] = y
        yb_ref[...] = y.astype(bf16)
        w0_ref[...] = w0
        w1_ref[...] = w1
        w2_ref[...] = w2

    row = pl.BlockSpec((tm, W), lambda i: (i, 0))
    return pl.pallas_call(
        body,
        grid=(S // tm,),
        in_specs=[row] * 6,
        out_specs=[row] * 5,
        out_shape=[SDS((S, W), f32), SDS((S, W), bf16)] + [SDS((S, W), f32)] * 3,
        compiler_params=_cparams(("parallel",)),
        name="attn_merge",
    )(o0, o1, o2, l0, l1, l2)


def _attn_merge_bwd(dy, y, w0, w1, w2):
    S, W = dy.shape
    tm = _pick(S, 512)

    def body(dy_ref, y_ref, w0_ref, w1_ref, w2_ref, a0, a1, a2, b0, b1, b2):
        dyv = dy_ref[...]
        r = lax.broadcasted_iota(jnp.int32, (LANE, LANE), 0) // HEAD_DIM
        c = lax.broadcasted_iota(jnp.int32, (LANE, LANE), 1) // HEAD_DIM
        seg = jnp.where(r == c, 1.0, 0.0).astype(f32)
        cbar = jnp.dot(dyv * y_ref[...], seg, precision=HIGHEST, preferred_element_type=f32)
        for w_ref, a_ref, b_ref in ((w0_ref, a0, b0), (w1_ref, a1, b1), (w2_ref, a2, b2)):
            w = w_ref[...]
            a_ref[...] = w * dyv
            b_ref[...] = w * cbar

    blk = pl.BlockSpec((tm, LANE), lambda i, j: (i, j))
    return pl.pallas_call(
        body,
        grid=(S // tm, W // LANE),
        in_specs=[blk] * 5,
        out_specs=[blk] * 6,
        out_shape=[SDS((S, W), f32)] * 6,
        compiler_params=_cparams(("parallel", "parallel")),
        name="attn_merge_bwd",
    )(dy, y, w0, w1, w2)


HGRN_SB = 256


def _chunk_masks(sb):
    r = lax.broadcasted_iota(jnp.int32, (sb, sb), 0)
    c = lax.broadcasted_iota(jnp.int32, (sb, sb), 1)
    same = (r // HGRN_CHUNK) == (c // HGRN_CHUNK)
    return same, jnp.logical_and(same, c <= r), jnp.logical_and(same, c >= r)


def _hgrn_prep(q_raw, f_raw, lbv, same, tril):
    sq = _sigmoid(q_raw)
    qs = q_raw * sq
    sig = _sigmoid(f_raw)
    f = lbv + (1.0 - lbv) * sig
    g = jnp.log(f)
    k = 1.0 - f
    G = jnp.dot(jnp.where(tril, 1.0, 0.0).astype(f32), g, precision=HIGHEST, preferred_element_type=f32)
    GL = jnp.dot(jnp.where(same, 1.0, 0.0).astype(f32), g, precision=HIGHEST, preferred_element_type=f32)
    eG = jnp.exp(G)
    einv = jnp.exp(-G)
    edec = jnp.exp(GL - G)
    return dict(sq=sq, qs=qs, sig=sig, f=f, k=k, eG=eG, einv=einv, edec=edec, eGL=jnp.exp(GL),
                qt=qs * eG, kt=k * einv, kd=k * edec)


def _ride_split(ride, rest, n_out, n_scratch):
    if ride is None:
        return None, rest[:n_out], None, rest[n_out:], None
    return rest[0], rest[1:1 + n_out], rest[1 + n_out], rest[2 + n_out:2 + n_out + n_scratch], rest[2 + n_out + n_scratch:]


def _hgrn_fwd(hg, lb, normw, ride=None):
    S = hg.shape[0]
    sb = HGRN_SB
    nsb = S // sb
    nch = sb // HGRN_CHUNK

    def body(q_ref, f_ref, v_ref, og_ref, lb_ref, nw_ref, *rest):
        src_ref, (y_ref, o_ref, ck_ref), got_ref, (st,), sems = _ride_split(ride, rest, 3, 1)
        j = pl.program_id(1)
        if ride is not None:
            @pl.when(jnp.logical_and(pl.program_id(0) == 0, j == 0))
            def _():
                _chip_start(src_ref, got_ref, sems[0], sems[1], ride[1])

        @pl.when(j == 0)
        def _():
            st[...] = jnp.zeros_like(st)

        ST = st[...]
        ck_ref[0, 0] = ST
        same, tril, _ = _chunk_masks(sb)
        pr = _hgrn_prep(q_ref[...], f_ref[...], lb_ref[...], same, tril)
        qtb, ktb, kdb = pr["qt"].astype(bf16), pr["kt"].astype(bf16), pr["kd"].astype(bf16)
        eGL = pr["eGL"]
        vb = v_ref[...].astype(bf16)
        A = jnp.where(tril, lax.dot_general(qtb, ktb, NT, preferred_element_type=f32), 0.0)
        o = jnp.dot(A.astype(bf16), vb, preferred_element_type=f32)
        outs = []
        for ci in range(nch):
            lo = ci * HGRN_CHUNK
            sl = slice(lo, lo + HGRN_CHUNK)
            outs.append(o[sl] + lax.dot_general(qtb[sl], ST.astype(bf16), NT, preferred_element_type=f32))
            ST = ST * eGL[lo:lo + 1, :] + lax.dot_general(vb[sl], kdb[sl], TN, preferred_element_type=f32)
        st[...] = ST
        of = jnp.concatenate(outs, axis=0)
        o_ref[...] = of
        rms = lax.rsqrt(jnp.mean(of * of, axis=-1, keepdims=True) + EPS)
        ogv = og_ref[...]
        y_ref[...] = ((of * rms * nw_ref[...]) * (ogv * _sigmoid(ogv))).astype(bf16)

        if ride is not None:
            @pl.when(jnp.logical_and(pl.program_id(0) == 3, j == nsb - 1))
            def _():
                _chip_finish(src_ref, got_ref, sems[0], sems[1], ride[1])

    col = lambda off: pl.BlockSpec((sb, LANE), lambda h, j: (j, off + h))
    riding = ride is not None
    res = pl.pallas_call(
        body,
        grid=(4, nsb),
        in_specs=[col(0), col(4), col(8), col(12), pl.BlockSpec((1, LANE), lambda h, j: (0, h)),
                  pl.BlockSpec((1, LANE), lambda h, j: (0, 0))] + ([_ANY] if riding else []),
        out_specs=[col(0), col(0), pl.BlockSpec((1, 1, LANE, LANE), lambda h, j: (h, j, 0, 0))]
        + ([_ANY] if riding else []),
        out_shape=[SDS((S, HGRN_W), bf16), SDS((S, HGRN_W), f32), SDS((4, nsb, LANE, LANE), f32)]
        + ([_chip_out_shape(*ride)] if riding else []),
        scratch_shapes=[pltpu.VMEM((LANE, LANE), f32)] + (list(_CHIP_SEMS) if riding else []),
        compiler_params=_cparams(("arbitrary", "arbitrary") if riding else ("parallel", "arbitrary")),
        name="hgrn_fwd",
    )(hg, hg, hg, hg, lb, normw, *([ride[0]] if riding else []))
    return tuple(res) if riding else (*res, None)


def _hgrn_bwd(hg, o_raw, dy, ck, lb, normw, ride=None):
    S = hg.shape[0]
    sb = HGRN_SB
    nsb = S // sb
    nch = sb // HGRN_CHUNK

    def body(q_ref, f_ref, v_ref, og_ref, o_ref, dy_ref, ck_ref, lb_ref, nw_ref, *rest):
        src_ref, outs, got_ref, (dst, alb, anw), sems = _ride_split(ride, rest, 6, 3)
        dq_ref, df_ref, dv_ref, dog_ref, glb_ref, gnw_ref = outs
        j = pl.program_id(1)
        if ride is not None:
            @pl.when(jnp.logical_and(pl.program_id(0) == 0, j == 0))
            def _():
                _chip_start(src_ref, got_ref, sems[0], sems[1], ride[1])

        @pl.when(j == 0)
        def _():
            dst[...] = jnp.zeros_like(dst)
            alb[...] = jnp.zeros_like(alb)
            anw[...] = jnp.zeros_like(anw)

        same, tril, triu = _chunk_masks(sb)
        lbv = lb_ref[...]
        q_raw = q_ref[...]
        pr = _hgrn_prep(q_raw, f_ref[...], lbv, same, tril)
        qt, kt, kd, eGL = pr["qt"], pr["kt"], pr["kd"], pr["eGL"]
        qtb, ktb, kdb = qt.astype(bf16), kt.astype(bf16), kd.astype(bf16)
        vb = v_ref[...].astype(bf16)

        o = o_ref[...]
        ogv = og_ref[...]
        sog = _sigmoid(ogv)
        rms = lax.rsqrt(jnp.mean(o * o, axis=-1, keepdims=True) + EPS)
        oh = o * rms
        nw = nw_ref[...]
        dyv = dy_ref[...]
        dog_ref[...] = (dyv * (oh * nw) * (sog * (1.0 + ogv * (1.0 - sog)))).astype(bf16)
        dohw = dyv * (ogv * sog)
        anw[...] += _colsum8(dohw * oh)
        doh = dohw * nw
        do = rms * (doh - oh * jnp.mean(doh * oh, axis=-1, keepdims=True))
        dob = do.astype(bf16)

        Ab = jnp.where(tril, lax.dot_general(qtb, ktb, NT, preferred_element_type=f32), 0.0).astype(bf16)
        dAb = jnp.where(tril, lax.dot_general(dob, vb, NT, preferred_element_type=f32), 0.0).astype(bf16)
        dv_acc = lax.dot_general(Ab, dob, TN, preferred_element_type=f32)
        dqt = jnp.dot(dAb, ktb, preferred_element_type=f32)
        dkt = lax.dot_general(dAb, qtb, TN, preferred_element_type=f32)

        ST = ck_ref[0, 0]
        states = []
        for ci in range(nch):
            lo = ci * HGRN_CHUNK
            sl = slice(lo, lo + HGRN_CHUNK)
            states.append(ST)
            ST = ST * eGL[lo:lo + 1, :] + lax.dot_general(vb[sl], kdb[sl], TN, preferred_element_type=f32)

        dST = dst[...]
        dqt_i, dkd_i, dv_i, deg_i = [None] * nch, [None] * nch, [None] * nch, [None] * nch
        for ci in reversed(range(nch)):
            lo = ci * HGRN_CHUNK
            sl = slice(lo, lo + HGRN_CHUNK)
            ST0 = states[ci]
            dSTb = dST.astype(bf16)
            dv_i[ci] = lax.dot_general(kdb[sl], dSTb, NT, preferred_element_type=f32)
            dqt_i[ci] = jnp.dot(dob[sl], ST0.astype(bf16), preferred_element_type=f32)
            dkd_i[ci] = jnp.dot(vb[sl], dSTb, preferred_element_type=f32)
            deg_i[ci] = jnp.broadcast_to(jnp.sum(dST * ST0, axis=0, keepdims=True), (HGRN_CHUNK, LANE))
            dST = dST * eGL[lo:lo + 1, :] + lax.dot_general(dob[sl], qtb[sl], TN, preferred_element_type=f32)
        dst[...] = dST

        dqt = dqt + jnp.concatenate(dqt_i, axis=0)
        dkd = jnp.concatenate(dkd_i, axis=0)
        dv_ref[...] = (dv_acc + jnp.concatenate(dv_i, axis=0)).astype(bf16)
        deg = jnp.concatenate(deg_i, axis=0)

        dqs = dqt * pr["eG"]
        dkdkd = dkd * kd
        dG = dqt * qt - dkt * kt - dkdkd
        dk = dkt * pr["einv"] + dkd * pr["edec"]
        dGL = jnp.dot(jnp.where(same, 1.0, 0.0).astype(f32), dkdkd, precision=HIGHEST,
                      preferred_element_type=f32) + eGL * deg
        dg = jnp.dot(jnp.where(triu, 1.0, 0.0).astype(f32), dG, precision=HIGHEST,
                     preferred_element_type=f32) + dGL
        df = dg / pr["f"] - dk
        sig = pr["sig"]
        df_ref[...] = (df * (1.0 - lbv) * (sig * (1.0 - sig))).astype(bf16)
        alb[...] += _colsum8(df * (1.0 - sig))
        sq = pr["sq"]
        dq_ref[...] = (dqs * (sq * (1.0 + q_raw * (1.0 - sq)))).astype(bf16)

        @pl.when(j == nsb - 1)
        def _():
            glb_ref[...] = jnp.broadcast_to(jnp.sum(alb[...], axis=0, keepdims=True), (SUBLANE, LANE))
            gnw_ref[...] = jnp.broadcast_to(jnp.sum(anw[...], axis=0, keepdims=True), (SUBLANE, LANE))

        if ride is not None:
            @pl.when(jnp.logical_and(pl.program_id(0) == 3, j == nsb - 1))
            def _():
                _chip_finish(src_ref, got_ref, sems[0], sems[1], ride[1])

    rev = lambda off: pl.BlockSpec((sb, LANE), lambda h, j: (nsb - 1 - j, off + h))
    stat = pl.BlockSpec((SUBLANE, LANE), lambda h, j: (0, h))
    riding = ride is not None
    res = pl.pallas_call(
        body,
        grid=(4, nsb),
        in_specs=[rev(0), rev(4), rev(8), rev(12), rev(0), rev(0),
                  pl.BlockSpec((1, 1, LANE, LANE), lambda h, j: (h, nsb - 1 - j, 0, 0)),
                  pl.BlockSpec((1, LANE), lambda h, j: (0, h)), pl.BlockSpec((1, LANE), lambda h, j: (0, 0))]
        + ([_ANY] if riding else []),
        out_specs=[rev(0), rev(0), rev(0), rev(0), stat, stat] + ([_ANY] if riding else []),
        out_shape=[SDS((S, HGRN_W), bf16)] * 4 + [SDS((SUBLANE, HGRN_W), f32)] * 2
        + ([_chip_out_shape(*ride)] if riding else []),
        scratch_shapes=[pltpu.VMEM((LANE, LANE), f32), pltpu.VMEM((SUBLANE, LANE), f32),
                        pltpu.VMEM((SUBLANE, LANE), f32)] + (list(_CHIP_SEMS) if riding else []),
        compiler_params=_cparams(("arbitrary", "arbitrary") if riding else ("parallel", "arbitrary")),
        name="hgrn_bwd",
    )(hg, hg, hg, hg, o_raw, dy, ck, lb, normw, *([ride[0]] if riding else []))
    return tuple(res) if riding else (*res, None)


def _lb_fwd(raw):
    def body(r_ref, o_ref):
        r = r_ref[...]
        m = jnp.max(r, axis=0, keepdims=True)
        e = jnp.exp(r - m)
        o_ref[...] = (e / jnp.sum(e, axis=0, keepdims=True))[0:1]

    return pl.pallas_call(body, out_shape=SDS((1, raw.shape[1]), f32), name="lb_fwd")(raw)


def _lb_bwd(raw, dlb):
    def body(r_ref, d_ref, o_ref):
        r = r_ref[...]
        m = jnp.max(r, axis=0, keepdims=True)
        e = jnp.exp(r - m)
        s = e / jnp.sum(e, axis=0, keepdims=True)
        s0 = s[0:1]
        onehot0 = jnp.where(lax.broadcasted_iota(jnp.int32, r.shape, 0) == 0, 1.0, 0.0)
        o_ref[...] = d_ref[...] * s0 * (onehot0 - s)

    return pl.pallas_call(body, out_shape=SDS(raw.shape, f32), name="lb_bwd")(raw, dlb)


def _gate_fwd(a, b, gc):
    S, D = a.shape
    tm = _pick(S, 512)

    def body(a_ref, b_ref, g0_ref, g1_ref, o_ref):
        s0, s1 = _sigmoid(g0_ref[...].astype(f32)), _sigmoid(g1_ref[...].astype(f32))
        o_ref[...] = (s0 * a_ref[...].astype(f32) + s1 * b_ref[...].astype(f32)).astype(bf16)

    row = pl.BlockSpec((tm, D), lambda i: (i, 0))
    return pl.pallas_call(
        body,
        grid=(S // tm,),
        in_specs=[row, row, row, pl.BlockSpec((tm, D), lambda i: (i, 1))],
        out_specs=row,
        out_shape=SDS((S, D), bf16),
        compiler_params=_cparams(("parallel",)),
        name="gate_fwd",
    )(a, b, gc, gc)


def _gate_bwd(dm, a, b, gc):
    S, D = a.shape
    tm = _pick(S, 512)

    def body(dm_ref, a_ref, b_ref, g0_ref, g1_ref, da_ref, db_ref, dg_ref):
        dmv = dm_ref[...].astype(f32)
        s0, s1 = _sigmoid(g0_ref[...].astype(f32)), _sigmoid(g1_ref[...].astype(f32))
        da_ref[...] = (dmv * s0).astype(bf16)
        db_ref[...] = (dmv * s1).astype(bf16)
        dg_ref[:, :D] = (dmv * a_ref[...].astype(f32) * (s0 * (1.0 - s0))).astype(bf16)
        dg_ref[:, D:] = (dmv * b_ref[...].astype(f32) * (s1 * (1.0 - s1))).astype(bf16)

    row = pl.BlockSpec((tm, D), lambda i: (i, 0))
    wide = pl.BlockSpec((tm, 2 * D), lambda i: (i, 0))
    return pl.pallas_call(
        body,
        grid=(S // tm,),
        in_specs=[row, row, row, row, pl.BlockSpec((tm, D), lambda i: (i, 1))],
        out_specs=[row, row, wide],
        out_shape=[SDS((S, D), bf16), SDS((S, D), bf16), SDS((S, 2 * D), bf16)],
        compiler_params=_cparams(("parallel",)),
        name="gate_bwd",
    )(dm, a, b, gc, gc)


CONV_ROWS = 512
INV_SQRT2 = 0.7071067811865476
INV_SQRT_2PI = 0.3989422804014327


CONV_HALO = 16


def _tile8(a, rows):
    return jnp.tile(a, (rows // a.shape[0], 1))


def _conv_rows(u_ref, w, b, r0, first):
    R = CONV_ROWS
    cur = u_ref[pl.ds(r0, R), :].astype(f32)
    prev8 = u_ref[pl.ds(pl.multiple_of(jnp.maximum(r0 - CONV_HALO, 0), CONV_HALO), CONV_HALO), :].astype(f32)
    prev8 = jnp.where(first, 0.0, prev8)
    row = lax.broadcasted_iota(jnp.int32, (R, LANE), 0)
    x1 = jnp.where(row < 1, _tile8(pltpu.roll(prev8, 1, 0), R), pltpu.roll(cur, 1, 0))
    x2 = jnp.where(row < 2, _tile8(pltpu.roll(prev8, 2, 0), R), pltpu.roll(cur, 2, 0))
    c = ((b + w[0:1] * x2) + w[1:2] * x1) + w[2:3] * cur
    return c, x2, x1, cur


def _conv_fwd(ug, uv, wg, wv, bg, bv):
    S, F = ug.shape
    nchunk = S // CONV_ROWS

    def body(ug_ref, uv_ref, wg_ref, wv_ref, bg_ref, bv_ref, o_ref):
        wgv, wvv, bgv, bvv = wg_ref[...], wv_ref[...], bg_ref[...], bv_ref[...]

        def step(ci, carry):
            r0 = pl.multiple_of(ci * CONV_ROWS, CONV_ROWS)
            cg = _conv_rows(ug_ref, wgv, bgv, r0, ci == 0)[0]
            cv = _conv_rows(uv_ref, wvv, bvv, r0, ci == 0)[0]
            gelu = 0.5 * cg * (1.0 + lax.erf(cg * INV_SQRT2))
            o_ref[pl.ds(r0, CONV_ROWS), :] = (gelu * cv).astype(bf16)
            return carry

        lax.fori_loop(0, nchunk, step, 0)

    col = pl.BlockSpec((S, LANE), lambda j: (0, j))
    w3 = pl.BlockSpec((3, LANE), lambda j: (0, j))
    b1 = pl.BlockSpec((1, LANE), lambda j: (0, j))
    return pl.pallas_call(
        body,
        grid=(F // LANE,),
        in_specs=[col, col, w3, w3, b1, b1],
        out_specs=col,
        out_shape=SDS((S, F), bf16),
        compiler_params=_cparams(("parallel",), VMEM_BIG),
        name="conv_fwd",
    )(ug, uv, wg, wv, bg, bv)


def _conv_bwd(ug, uv, dact, wg, wv, bg, bv):
    S, F = ug.shape
    R = CONV_ROWS
    nchunk = S // R

    def body(ug_ref, uv_ref, da_ref, wg_ref, wv_ref, bg_ref, bv_ref, dug_ref, duv_ref, sg_ref, sv_ref, dcg, dcv):
        wgv, wvv, bgv, bvv = wg_ref[...], wv_ref[...], bg_ref[...], bv_ref[...]
        zero = jnp.zeros((SUBLANE, LANE), f32)

        def fwd_step(ci, acc):
            r0 = pl.multiple_of(ci * R, R)
            cg, g2, g1, g0 = _conv_rows(ug_ref, wgv, bgv, r0, ci == 0)
            cv, v2, v1, v0 = _conv_rows(uv_ref, wvv, bvv, r0, ci == 0)
            da = da_ref[pl.ds(r0, R), :].astype(f32)
            cdf = 0.5 * (1.0 + lax.erf(cg * INV_SQRT2))
            pdf = INV_SQRT_2PI * jnp.exp(-0.5 * cg * cg)
            dg = da * cv * (cdf + cg * pdf)
            dv = da * (cg * cdf)
            dcg[pl.ds(r0, R), :] = dg
            dcv[pl.ds(r0, R), :] = dv
            new = (acc[0] + _colsum8(dg * g2), acc[1] + _colsum8(dg * g1), acc[2] + _colsum8(dg * g0),
                   acc[3] + _colsum8(dg),
                   acc[4] + _colsum8(dv * v2), acc[5] + _colsum8(dv * v1), acc[6] + _colsum8(dv * v0),
                   acc[7] + _colsum8(dv))
            return new

        acc = lax.fori_loop(0, nchunk, fwd_step, (zero,) * 8)
        rows = lax.broadcasted_iota(jnp.int32, (SUBLANE, LANE), 0)

        def stats(parts):
            out = jnp.zeros((SUBLANE, LANE), f32)
            for k, pt in enumerate(parts):
                out = jnp.where(rows == k, jnp.sum(pt, axis=0, keepdims=True), out)
            return out

        sg_ref[...] = stats(acc[0:4])
        sv_ref[...] = stats(acc[4:8])

        def du_rows(dc, w, r0, last):
            cur = dc[pl.ds(r0, R), :]
            nxt = dc[pl.ds(pl.multiple_of(jnp.minimum(r0 + R, S - SUBLANE), SUBLANE), SUBLANE), :]
            nxt = jnp.where(last, 0.0, nxt)
            row = lax.broadcasted_iota(jnp.int32, (R, LANE), 0)
            y1 = jnp.where(row >= R - 1, _tile8(pltpu.roll(nxt, SUBLANE - 1, 0), R), pltpu.roll(cur, R - 1, 0))
            y2 = jnp.where(row >= R - 2, _tile8(pltpu.roll(nxt, SUBLANE - 2, 0), R), pltpu.roll(cur, R - 2, 0))
            return w[2:3] * cur + w[1:2] * y1 + w[0:1] * y2

        def bwd_step(ci, carry):
            r0 = pl.multiple_of(ci * R, R)
            last = ci == nchunk - 1
            dug_ref[pl.ds(r0, R), :] = du_rows(dcg, wgv, r0, last).astype(bf16)
            duv_ref[pl.ds(r0, R), :] = du_rows(dcv, wvv, r0, last).astype(bf16)
            return carry

        lax.fori_loop(0, nchunk, bwd_step, 0)

    col = pl.BlockSpec((S, LANE), lambda j: (0, j))
    w3 = pl.BlockSpec((3, LANE), lambda j: (0, j))
    b1 = pl.BlockSpec((1, LANE), lambda j: (0, j))
    st = pl.BlockSpec((SUBLANE, LANE), lambda j: (0, j))
    return pl.pallas_call(
        body,
        grid=(F // LANE,),
        in_specs=[col, col, col, w3, w3, b1, b1],
        out_specs=[col, col, st, st],
        out_shape=[SDS((S, F), bf16), SDS((S, F), bf16), SDS((SUBLANE, F), f32), SDS((SUBLANE, F), f32)],
        scratch_shapes=[pltpu.VMEM((S, LANE), f32), pltpu.VMEM((S, LANE), f32)],
        compiler_params=_cparams(("parallel",), VMEM_BIG),
        name="conv_bwd",
    )(ug, uv, dact, wg, wv, bg, bv)


def _adam_math(w, g, m, v):
    m = ADAM_B1 * m + (1.0 - ADAM_B1) * g
    v = ADAM_B2 * v + (1.0 - ADAM_B2) * (g * g)
    m_hat = m / (1.0 - ADAM_B1 ** ADAM_STEP)
    v_hat = v / (1.0 - ADAM_B2 ** ADAM_STEP)
    delta = -ADAM_LR * (m_hat / (jnp.sqrt(v_hat) + ADAM_EPS) + ADAM_WD * w)
    return delta, m, v


def _adamw(w, m, v, g, name):
    R, C = w.shape
    parts = g.ndim == 3
    tr = R
    for t in (256, 128, 64, 32, 16):
        if R % t == 0 and R > t:
            tr = t
            break

    def body(w_ref, m_ref, v_ref, g_ref, go_ref, d_ref, mo_ref, vo_ref):
        if parts:
            gv = ((g_ref[0].astype(f32) + g_ref[1].astype(f32)) + g_ref[2].astype(f32)) + g_ref[3].astype(f32)
        else:
            gv = g_ref[...]
        go_ref[...] = gv
        d, mn, vn = _adam_math(w_ref[...], gv, m_ref[...], v_ref[...])
        d_ref[...] = d
        mo_ref[...] = mn
        vo_ref[...] = vn

    row = pl.BlockSpec((tr, C), lambda i: (i, 0))
    gspec = pl.BlockSpec((4, tr, C), lambda i: (0, i, 0)) if parts else row
    return pl.pallas_call(
        body,
        grid=(R // tr,),
        in_specs=[row, row, row, gspec],
        out_specs=[row] * 4,
        out_shape=[SDS((R, C), f32)] * 4,
        compiler_params=_cparams(("parallel",)),
        name=name,
    )(w, m, v, g)


def _sum8(parts, name):
    _, _, R, C = parts.shape

    def body(p_ref, o_ref):
        acc = p_ref[0, 0]
        for c in range(2):
            for k in range(4):
                if c or k:
                    acc = acc + p_ref[c, k]
        o_ref[...] = acc

    return pl.pallas_call(body, out_shape=SDS((R, C), f32), name=name)(parts)


def _pair_add(by_core, b, name):
    _, K, R, C = by_core.shape
    tr = R // 2 if R % 32 == 0 else R

    def body(c_ref, a_ref, b_ref, o_ref):
        o_ref[...] = (a_ref[0].astype(f32) + b_ref[...].astype(f32)).astype(bf16)

    blk = pl.BlockSpec((1, tr, C), lambda k, i, c: (k, i, 0))
    return pl.pallas_call(
        body,
        grid_spec=pltpu.PrefetchScalarGridSpec(
            num_scalar_prefetch=1,
            grid=(K, R // tr),
            in_specs=[pl.BlockSpec((1, 1, tr, C), lambda k, i, c: (c[0], k, i, 0)), blk],
            out_specs=blk,
        ),
        out_shape=SDS((K, R, C), bf16),
        compiler_params=_cparams(("parallel", "parallel")),
        name=name,
    )(lax.axis_index("c").astype(jnp.int32).reshape(1), by_core, b)


_ANY = pl.BlockSpec(memory_space=pl.ANY)


def _chip_copies(src_ref, out_ref, send_sems, recv_sems, gather):
    x, y, c = lax.axis_index("x"), lax.axis_index("y"), lax.axis_index("c")
    mine = 2 * x + y

    def piece(k):
        return src_ref if gather else src_ref.at[k]

    sends, recvs = [], []
    for j, (px, py) in enumerate([(1 - x, y), (x, 1 - y), (1 - x, 1 - y)]):
        sends.append(pltpu.make_async_remote_copy(
            src_ref=piece(2 * px + py), dst_ref=out_ref.at[mine], send_sem=send_sems.at[j],
            recv_sem=recv_sems.at[j], device_id=(px, py, c), device_id_type=MESH))
        recvs.append(pltpu.make_async_remote_copy(
            src_ref=piece(mine), dst_ref=out_ref.at[2 * px + py], send_sem=send_sems.at[j],
            recv_sem=recv_sems.at[j], device_id=(px, py, c), device_id_type=MESH))
    return sends, recvs


def _chip_start(src_ref, out_ref, send_sems, recv_sems, gather):
    for cp in _chip_copies(src_ref, out_ref, send_sems, recv_sems, gather)[0]:
        cp.start()


def _chip_finish(src_ref, out_ref, send_sems, recv_sems, gather):
    sends, recvs = _chip_copies(src_ref, out_ref, send_sems, recv_sems, gather)
    for cp in recvs:
        cp.wait_recv()
    for cp in sends:
        cp.wait_send()


def _chip_out_shape(src, gather):
    return SDS((4,) + tuple(src.shape if gather else src.shape[1:]), src.dtype)


_CHIP_SEMS = [pltpu.SemaphoreType.DMA((3,)), pltpu.SemaphoreType.DMA((3,))]


def _fill_own(out, src, gather):
    mine = 2 * lax.axis_index("x") + lax.axis_index("y")
    own = src if gather else lax.dynamic_index_in_dim(src, mine, axis=0, keepdims=False)
    return lax.dynamic_update_index_in_dim(out, own, mine, axis=0)


def _chip_comm(src, gather, name):
    def body(src_ref, out_ref, send_sems, recv_sems):
        _chip_start(src_ref, out_ref, send_sems, recv_sems, gather)
        _chip_finish(src_ref, out_ref, send_sems, recv_sems, gather)

    out = pl.pallas_call(
        body,
        in_specs=[_ANY],
        out_specs=_ANY,
        out_shape=_chip_out_shape(src, gather),
        scratch_shapes=list(_CHIP_SEMS),
        name=name,
    )(src)
    return _fill_own(out, src, gather)


def _core_gather(src, name):
    def body(src_ref, out_ref, send_sem, recv_sem):
        x, y, c = lax.axis_index("x"), lax.axis_index("y"), lax.axis_index("c")
        cp = pltpu.make_async_remote_copy(src_ref=src_ref, dst_ref=out_ref.at[c], send_sem=send_sem,
                                          recv_sem=recv_sem, device_id=(x, y, 1 - c), device_id_type=MESH)
        cp.start()
        pltpu.make_async_remote_copy(src_ref=src_ref, dst_ref=out_ref.at[1 - c], send_sem=send_sem,
                                     recv_sem=recv_sem, device_id=(x, y, 1 - c), device_id_type=MESH).wait_recv()
        cp.wait_send()

    out = pl.pallas_call(
        body,
        in_specs=[_ANY],
        out_specs=_ANY,
        out_shape=SDS((2,) + tuple(src.shape), src.dtype),
        scratch_shapes=[pltpu.SemaphoreType.DMA, pltpu.SemaphoreType.DMA],
        name=name,
    )(src)
    return lax.dynamic_update_index_in_dim(out, src, lax.axis_index("c"), axis=0)


def _core_swap(src, name):
    def body(src_ref, out_ref, send_sem, recv_sem):
        x, y, c = lax.axis_index("x"), lax.axis_index("y"), lax.axis_index("c")
        cp = pltpu.make_async_remote_copy(src_ref=src_ref.at[1 - c], dst_ref=out_ref, send_sem=send_sem,
                                          recv_sem=recv_sem, device_id=(x, y, 1 - c), device_id_type=MESH)
        cp.start()
        cp.wait()

    return pl.pallas_call(
        body,
        in_specs=[_ANY],
        out_specs=_ANY,
        out_shape=SDS(tuple(src.shape[1:]), src.dtype),
        scratch_shapes=[pltpu.SemaphoreType.DMA, pltpu.SemaphoreType.DMA],
        name=name,
    )(src)


def _all_gather(src, tag):
    by_chip = _chip_comm(src, True, tag + "_chips")
    both = _core_gather(by_chip, tag + "_cores")
    return jnp.swapaxes(both, 0, 1).reshape((8,) + tuple(src.shape))


_PACK_A = (("w_in", (1024, 1088)),)
_PACK_B = (("w_ba", (512, 128)), ("w_bh", (512, 128)), ("w_out", (128, 1024)), ("w_up", (1024, 704)),
           ("w_down", (352, 1024)))
_PACK_SIZES = _PACK_A + _PACK_B


def _slab_rows(sizes):
    return sum(r * c for _, (r, c) in sizes) // D_MODEL


def _pack_rows(d, sizes):
    n = d[sizes[0][0]].shape[0]
    return jnp.concatenate([d[k].reshape(n, -1, D_MODEL) for k, _ in sizes], axis=1)


def _unpack_rows(slab, sizes):
    n = slab.shape[0]
    out, lo = {}, 0
    for key, (r, c) in sizes:
        rows = r * c // D_MODEL
        out[key] = slab[:, lo:lo + rows].reshape(n, r, c)
        lo += rows
    return out


def _by_core(gslab):
    return jnp.swapaxes(gslab.reshape((4, 2) + gslab.shape[1:]), 0, 1)


def _pair_sum(by_core, tag):
    return _pair_add(by_core, _core_swap(by_core, tag + "_cores"), tag + "_pair_add")


def _cols_to_full(t):
    return jnp.swapaxes(t, 0, 1).reshape(t.shape[1], -1)


def _full_to_cols(t):
    K = t.shape[0]
    return jnp.swapaxes(t.reshape(K, 8, -1), 0, 1)


_SMALL = (("pre_mix_norm", (1, 1024)), ("rel_bias", (32, 24)), ("hgrn_lb_raw", (2, 512)), ("hgrn_norm", (1, 128)),
          ("post_mix_norm", (1, 1024)), ("pre_ffn_norm", (1, 1024)), ("conv_b", (1, 5632)),
          ("post_ffn_norm", (1, 1024)))
_SMALL_ROWS = 96
_CONVW_ROWS = 136


def _pack_small(d):
    flat = jnp.concatenate([d[k].reshape(-1) for k, _ in _SMALL])
    flat = jnp.pad(flat, (0, _SMALL_ROWS * LANE - flat.shape[0]))
    return flat.reshape(_SMALL_ROWS, LANE)


def _unpack_small(p):
    flat = p.reshape(-1)
    out, lo = {}, 0
    for k, shp in _SMALL:
        n = shp[0] * shp[1]
        out[k] = flat[lo:lo + n].reshape(shp)
        lo += n
    return out


def _local_step(x, tgt, WA, P, plan):
    S = x.shape[0]
    W = dict(WA)
    lb = _lb_fwd(P["hgrn_lb_raw"])
    xs = [x, _permute(x, 4, "perm_x4"), _permute(x, 16, "perm_x16")]
    hs = [_rmsnorm(xs[g], P["pre_mix_norm"], f"norm_pre{g}") for g in range(N_GROUPS)]
    h1 = hs[0]
    consts = [_bias_consts(d) for d in DILATIONS]
    qkv, obuf, lbuf, biases = [], [], [], []
    for g, d in enumerate(DILATIONS):
        qkv_g = _mm(hs[g], W["w_qkv"][g], "nn", bf16, f"proj_qkv{g}")
        tab_t = P["rel_bias"][:, 8 * g:8 * g + 8].T
        bias_g = _bias_build(tab_t, consts[g][0], consts[g][1], f"bias_build{g}").reshape(8, ATTN_BLOCK, 2 * ATTN_BLOCK)
        o_g, l_g = _attn_fwd(qkv_g, bias_g, (S // d) // ATTN_BLOCK, f"attn_fwd{g}")
        qkv.append(qkv_g)
        biases.append(bias_g)
        lbuf.append(l_g)
        obuf.append(_unpermute(o_g, d, f"unperm_o{g}"))
    l_nat = [_unpermute(lbuf[g], d, f"unperm_l{g}") for g, d in enumerate(DILATIONS)]
    y_attn, y_attn_b, w0, w1, w2 = _attn_merge(obuf[0], obuf[1], obuf[2], l_nat[0], l_nat[1], l_nat[2])
    hg = _mm(h1, W["w_hg"], "nn", f32, "proj_hg")
    gc = _mm(h1, W["w_gate"], "nn", bf16, "proj_gate")
    y_hgrn, o_raw, ck, got = _hgrn_fwd(hg, lb, P["hgrn_norm"], plan.fwd_ride())
    W.update(plan.weights(got))
    a = _mm(y_attn_b, W["w_ba"], "nn", bf16, "branch_attn")
    b = _mm(y_hgrn, W["w_bh"], "nn", bf16, "branch_hgrn")
    merged = _gate_fwd(a, b, gc)
    mo = _mm(merged, W["w_out"], "nn", f32, "out_proj")
    x1, h2 = _mid_fwd(x, mo, P["post_mix_norm"], P["pre_ffn_norm"])
    ug = _mm(h2, W["w_up_g"], "nn", bf16, "up_gate")
    uv = _mm(h2, W["w_up_v"], "nn", bf16, "up_val")
    cw_g, cw_v = P["conv_w"][:, :D_FF], P["conv_w"][:, D_FF:]
    cb_g, cb_v = P["conv_b"][:, :D_FF], P["conv_b"][:, D_FF:]
    act = _conv_fwd(ug, uv, cw_g, cw_v, cb_g, cb_v)
    fo = _mm(act, W["w_down"], "nn", f32, "down_proj")
    loss, dy, dfo, g_post_ffn = _final(x1, fo, tgt, P["post_ffn_norm"])
    dact = _mm(dfo, W["w_down"], "nt", bf16, "d_act")
    gW_down = _mm(act, dfo, "tn", f32, "gw_down")
    dug, duv, st_g, st_v = _conv_bwd(ug, uv, dact, cw_g, cw_v, cb_g, cb_v)
    dh2 = _mm(dug, W["w_up_g"], "nt", f32, "dh2_gate")
    dh2 = _mm(duv, W["w_up_v"], "nt", f32, "dh2_val", acc=dh2)
    gW_up_g = _mm(h2, dug, "tn", f32, "gw_up_gate")
    gW_up_v = _mm(h2, duv, "tn", f32, "gw_up_val")
    dx1, dmo, g_pre_ffn, g_post_mix = _mid_bwd(dy, dh2, x1, mo, P["pre_ffn_norm"], P["post_mix_norm"])
    dmerged = _mm(dmo, W["w_out"], "nt", bf16, "d_merged")
    gW_out = _mm(merged, dmo, "tn", f32, "gw_out")
    da, db, dgc = _gate_bwd(dmerged, a, b, gc)
    dyattn = _mm(da, W["w_ba"], "nt", f32, "d_yattn")
    gW_ba = _mm(y_attn_b, da, "tn", f32, "gw_ba")
    dyhgrn = _mm(db, W["w_bh"], "nt", f32, "d_yhgrn")
    gW_bh = _mm(y_hgrn, db, "tn", f32, "gw_bh")
    big_b = dict(w_ba=gW_ba, w_bh=gW_bh, w_out=gW_out, w_up=[gW_up_g, gW_up_v], w_down=gW_down)
    dq_h, df_h, dv_h, dog_h, glb8, gnw8, got_b = _hgrn_bwd(hg, o_raw, dyhgrn, ck, lb, P["hgrn_norm"],
                                                          plan.bwd_ride(big_b))
    dhg = [dq_h, df_h, dv_h, dog_h]
    g_lb_raw = _lb_bwd(P["hgrn_lb_raw"], glb8[0:1])
    gn = gnw8[0:1]
    g_hgrn_norm = (gn[:, 0:128] + gn[:, 128:256]) + (gn[:, 256:384] + gn[:, 384:512])
    dos = _attn_merge_bwd(dyattn, y_attn, w0, w1, w2)
    dh_parts, gW_qkv, g_rel = [], [], []
    for g, d in enumerate(DILATIONS):
        do_g = _permute(dos[g], d, f"perm_do{g}")
        dvec_g = _permute(dos[3 + g], d, f"perm_dvec{g}")
        dq, dk, dv, dbias = _attn_bwd(qkv[g], biases[g], do_g, dvec_g, lbuf[g], (S // d) // ATTN_BLOCK, f"attn_bwd{g}")
        dqkv = [dq, dk, dv]
        gW_qkv.append(_mm(hs[g], dqkv, "tn", f32, f"gw_qkv{g}"))
        dh_g = _mm(dqkv, W["w_qkv"][g], "nt", f32, f"dh1_qkv{g}")
        dh_parts.append(_unpermute(dh_g, d, f"unperm_dh{g}"))
        g_rel.append(_bias_grad(dbias.reshape(8, -1), consts[g][0], f"bias_grad{g}"))
    dh_main = _mm(dhg, W["w_hg"], "nt", f32, "dh1_hg", acc=dh_parts[0])
    dh_main = _mm(dgc, W["w_gate"], "nt", f32, "dh1_gate", acc=dh_main)
    gW_hg = _mm(h1, dhg, "tn", f32, "gw_hg")
    gW_gate = _mm(h1, dgc, "tn", f32, "gw_gate")
    grad_x, g_pre_mix = _first_bwd(x, dx1, dh_main, dh_parts[1], dh_parts[2], P["pre_mix_norm"])

    gW_in = gW_qkv + [gW_hg, gW_gate]
    g_conv_w = jnp.concatenate([st_g[0:3], st_v[0:3]], axis=1)
    g_conv_b = jnp.concatenate([st_g[3:4], st_v[3:4]], axis=1)
    small = dict(pre_mix_norm=g_pre_mix, rel_bias=jnp.concatenate(g_rel, axis=1), hgrn_lb_raw=g_lb_raw,
                 hgrn_norm=g_hgrn_norm, post_mix_norm=g_post_mix, pre_ffn_norm=g_pre_ffn, conv_b=g_conv_b,
                 post_ffn_norm=g_post_ffn, conv_w=g_conv_w)
    return loss, grad_x, gW_in, big_b, got_b, small


def _weights_a(both):
    w_in = jnp.transpose(both, (2, 1, 0, 3)).reshape(D_MODEL, -1)
    return dict(
        w_qkv=[w_in[:, g * QKV_G:(g + 1) * QKV_G] for g in range(N_GROUPS)],
        w_hg=w_in[:, 3 * QKV_G:3 * QKV_G + 4 * HGRN_W],
        w_gate=w_in[:, 3 * QKV_G + 4 * HGRN_W:],
    )


def _weights_b(slabs):
    sh = _unpack_rows(slabs, _PACK_B)
    w_up = _cols_to_full(sh["w_up"])
    return dict(
        w_ba=_cols_to_full(sh["w_ba"]),
        w_bh=_cols_to_full(sh["w_bh"]),
        w_out=sh["w_out"].reshape(D_MODEL, D_MODEL),
        w_up_g=w_up[:, :D_FF],
        w_up_v=w_up[:, D_FF:],
        w_down=sh["w_down"].reshape(D_FF, D_MODEL),
    )


def _dest_cols(sections, width):
    out = []
    for j in range(8):
        lo, hi, off, pieces = j * width, (j + 1) * width, 0, []
        for s in sections:
            a, b = max(lo, off), min(hi, off + s.shape[1])
            if a < b:
                pieces.append(s[:, a - off:b - off])
            off += s.shape[1]
        out.append(pieces[0] if len(pieces) == 1 else jnp.concatenate(pieces, axis=1))
    return out


def _grad_blocks_a(sections):
    cols = _dest_cols(sections, 1088)
    return jnp.stack([jnp.stack([cols[2 * k + c].astype(bf16) for k in range(4)]) for c in range(2)])


def _grad_slab_b(g):
    shards = dict(w_ba=_full_to_cols(g["w_ba"]), w_bh=_full_to_cols(g["w_bh"]), w_out=g["w_out"].reshape(8, 128, D_MODEL),
                  w_up=jnp.stack(_dest_cols(g["w_up"], 704)), w_down=g["w_down"].reshape(8, 352, D_MODEL))
    return _pack_rows({k: v.astype(bf16) for k, v in shards.items()}, _PACK_B)


class _SlabB:
    def __init__(self, slab):
        self.slab = slab
        self.chip_sum = None

    def fwd_ride(self):
        return (self.slab, True)

    def weights(self, got):
        both = _core_gather(_fill_own(got, self.slab, True), "ag_b_cores")
        return _weights_b(jnp.swapaxes(both, 0, 1).reshape((8,) + tuple(self.slab.shape)))

    def bwd_ride(self, grads):
        self.chip_sum = _pair_sum(_by_core(_grad_slab_b(grads)), "rs_b")
        return (self.chip_sum, False)

    def parts(self, got_b):
        return _unpack_rows(_fill_own(got_b, self.chip_sum, False), _PACK_B)


def kernel(x, pre_mix_norm, w_in, rel_bias, hgrn_lb_raw, hgrn_norm, w_branch_attn, w_branch_hgrn, w_out, post_mix_norm, pre_ffn_norm, w_up, conv_w, conv_b, w_down, post_ffn_norm, loss_target, m_pre_mix_norm, m_w_in, m_rel_bias, m_hgrn_lb_raw, m_hgrn_norm, m_w_branch_attn, m_w_branch_hgrn, m_w_out, m_post_mix_norm, m_pre_ffn_norm, m_w_up, m_conv_w, m_conv_b, m_w_down, m_post_ffn_norm, v_pre_mix_norm, v_w_in, v_rel_bias, v_hgrn_lb_raw, v_hgrn_norm, v_w_branch_attn, v_w_branch_hgrn, v_w_out, v_post_mix_norm, v_pre_ffn_norm, v_w_up, v_conv_w, v_conv_b, v_w_down, v_post_ffn_norm):
    ci = lax.axis_index("c")
    dev = 4 * lax.axis_index("x") + 2 * lax.axis_index("y") + ci
    wts = dict(w_in=w_in[0], w_ba=w_branch_attn[0], w_bh=w_branch_hgrn[0], w_out=w_out[0], w_up=w_up[0],
               w_down=w_down[0])
    mom = dict(w_in=m_w_in[0], w_ba=m_w_branch_attn[0], w_bh=m_w_branch_hgrn[0], w_out=m_w_out[0], w_up=m_w_up[0],
               w_down=m_w_down[0])
    var = dict(w_in=v_w_in[0], w_ba=v_w_branch_attn[0], w_bh=v_w_branch_hgrn[0], w_out=v_w_out[0], w_up=v_w_up[0],
               w_down=v_w_down[0])
    small_w = dict(pre_mix_norm=pre_mix_norm, rel_bias=rel_bias, hgrn_lb_raw=hgrn_lb_raw, hgrn_norm=hgrn_norm,
                   post_mix_norm=post_mix_norm, pre_ffn_norm=pre_ffn_norm, conv_b=conv_b, post_ffn_norm=post_ffn_norm)
    small_m = dict(pre_mix_norm=m_pre_mix_norm, rel_bias=m_rel_bias, hgrn_lb_raw=m_hgrn_lb_raw, hgrn_norm=m_hgrn_norm,
                   post_mix_norm=m_post_mix_norm, pre_ffn_norm=m_pre_ffn_norm, conv_b=m_conv_b,
                   post_ffn_norm=m_post_ffn_norm)
    small_v = dict(pre_mix_norm=v_pre_mix_norm, rel_bias=v_rel_bias, hgrn_lb_raw=v_hgrn_lb_raw, hgrn_norm=v_hgrn_norm,
                   post_mix_norm=v_post_mix_norm, pre_ffn_norm=v_pre_ffn_norm, conv_b=v_conv_b,
                   post_ffn_norm=v_post_ffn_norm)

    slab_a = wts["w_in"].astype(bf16)
    WA = _weights_a(_core_gather(_chip_comm(slab_a, True, "ag_a_chips"), "ag_a_cores"))
    plan = _SlabB(_pack_rows({k: wts[k].astype(bf16)[None] for k, _ in _PACK_B}, _PACK_B)[0])
    cw_pad = jnp.pad(conv_w[0], ((0, SUBLANE - 3), (0, 768 - 704)))
    conv_w_full = _cols_to_full(_all_gather(cw_pad, "ag_convw")[:, 0:3, 0:704])
    P = dict(small_w)
    P["conv_w"] = conv_w_full

    loss8, grad_x, gW_in, _, got_b, small = _local_step(x[0], loss_target[0], WA, P, plan)
    loss = lax.psum(loss8[0, 0], ("x", "y", "c"))

    chip_sum_a = _pair_sum(_grad_blocks_a(gW_in), "rs_a")
    parts = dict(w_in=_chip_comm(chip_sum_a, False, "rs_a_chips"))
    parts.update(plan.parts(got_b))
    outs_big = {}
    for k, _ in _PACK_SIZES:
        outs_big[k] = _adamw(wts[k], mom[k], var[k], parts[k], "adamw_" + k)

    spack = jnp.concatenate([_pack_small(small),
                             jnp.pad(small["conv_w"].reshape(-1, LANE), ((0, _CONVW_ROWS - 132), (0, 0)))], axis=0)
    allp = _core_gather(_chip_comm(spack, True, "ag_small_chips"), "ag_small_cores")
    ssum = _sum8(allp, "small_sum")
    gs = ssum[:_SMALL_ROWS]
    res_small = _adamw(_pack_small(small_w), _pack_small(small_m), _pack_small(small_v), gs, "adamw_small")
    sm = [_unpack_small(t) for t in res_small]
    g_cw_full = ssum[_SMALL_ROWS:_SMALL_ROWS + 132].reshape(3, 2 * D_FF)
    g_cw = lax.dynamic_slice_in_dim(g_cw_full, dev * 704, 704, axis=1)
    res_cw = _adamw(conv_w[0], m_conv_w[0], v_conv_w[0], g_cw, "adamw_conv_w")

    def pick(i):
        def big_(k):
            return outs_big[k][i][None]
        return [sm[i]["pre_mix_norm"], big_("w_in"), sm[i]["rel_bias"], sm[i]["hgrn_lb_raw"], sm[i]["hgrn_norm"],
                big_("w_ba"), big_("w_bh"), big_("w_out"), sm[i]["post_mix_norm"], sm[i]["pre_ffn_norm"],
                big_("w_up"), res_cw[i][None], sm[i]["conv_b"], big_("w_down"), sm[i]["post_ffn_norm"]]

    return (loss, grad_x[None], *pick(0), *pick(1), *pick(2), *pick(3))
```

```python
import functools
import math

import jax
import jax.numpy as jnp
from jax import lax
from jax.experimental import pallas as pl
from jax.experimental.pallas import tpu as pltpu

f32 = jnp.float32
bf16 = jnp.bfloat16
SDS = jax.ShapeDtypeStruct
HIGHEST = lax.Precision.HIGHEST
MESH = pl.DeviceIdType.MESH

NN = (((1,), (0,)), ((), ()))
NT = (((1,), (1,)), ((), ()))
TN = (((0,), (0,)), ((), ()))

D_MODEL = 1024
N_GROUPS = 3
DILATIONS = (1, 4, 16)
HEAD_DIM = 64
ATTN_BLOCK = 128
QKV_G = 1536
ATTN_OUT = 512
HGRN_W = 512
HGRN_CHUNK = 32
D_FF = 2816
NUM_BUCKETS = 32
MAX_EXACT = 16
MAX_DISTANCE = 2048
NEG_INF = -1e30
EPS = 1e-6
LANE = 128
SUBLANE = 8
VMEM_BIG = 48 * 1024 * 1024
MM_ROWS = 512
MM_OUT_BYTES = 8 * 1024 * 1024

ADAM_LR, ADAM_B1, ADAM_B2, ADAM_EPS, ADAM_WD, ADAM_STEP = 0.001, 0.9, 0.999, 1e-08, 0.01, 10


def _pick(n, pref):
    t = pref
    while t >= LANE:
        if n % t == 0:
            return t
        t //= 2
    return n


def _cparams(sem=None, vmem=None):
    kw = {}
    if sem is not None:
        kw["dimension_semantics"] = sem
    if vmem is not None:
        kw["vmem_limit_bytes"] = vmem
    return pltpu.CompilerParams(**kw)


def _sigmoid(x):
    return jax.nn.sigmoid(x)


def _colsum8(x):
    return x.reshape(x.shape[0] // SUBLANE, SUBLANE, x.shape[1]).sum(axis=0)


def _mm(a, b, mode, out_dtype, name, acc=None):
    dims = {"nn": NN, "nt": NT, "tn": TN}[mode]
    has_acc = acc is not None
    if mode == "tn":
        assert not has_acc
        bs = list(b) if isinstance(b, (list, tuple)) else [b]
        K, M = a.shape
        widths = [t.shape[1] for t in bs]
        N = sum(widths)
        tmm = M if M * N * 4 <= MM_OUT_BYTES else M // 2
        ts = _pick(K, MM_ROWS)
        nk = K // ts

        def body_tn(a_ref, *refs):
            o_ref = refs[-1]
            k = pl.program_id(1)
            av = a_ref[...]
            lo = 0
            for b_ref, w in zip(refs[:-1], widths):
                part = lax.dot_general(av, b_ref[...], dims, preferred_element_type=f32)
                cols = slice(lo, lo + w)
                lo += w

                @pl.when(k == 0)
                def _(part=part, cols=cols):
                    o_ref[:, cols] = part

                @pl.when(k > 0)
                def _(part=part, cols=cols):
                    o_ref[:, cols] += part

        return pl.pallas_call(
            body_tn,
            grid=(M // tmm, nk),
            in_specs=[pl.BlockSpec((ts, tmm), lambda i, k: (k, i))]
            + [pl.BlockSpec((ts, w), lambda i, k: (k, 0)) for w in widths],
            out_specs=pl.BlockSpec((tmm, N), lambda i, k: (i, 0)),
            out_shape=SDS((M, N), out_dtype),
            compiler_params=_cparams(("parallel", "arbitrary"), VMEM_BIG),
            name=name,
        )(a, *bs)

    parts = list(a) if isinstance(a, (list, tuple)) else [a]
    assert mode == "nt" or len(parts) == 1
    widths = [t.shape[1] for t in parts]
    M = parts[0].shape[0]
    N = b.shape[1] if mode == "nn" else b.shape[0]
    tm = _pick(M, MM_ROWS)
    npart = len(parts)

    def body(*refs):
        a_refs, b_ref = refs[:npart], refs[npart]
        c_ref = refs[npart + 1] if has_acc else None
        o_ref = refs[-1]
        if npart == 1:
            part = lax.dot_general(a_refs[0][...], b_ref[...], dims, preferred_element_type=f32)
        else:
            part, lo = None, 0
            for a_ref, w in zip(a_refs, widths):
                t = lax.dot_general(a_ref[...], b_ref[:, lo:lo + w], dims, preferred_element_type=f32)
                part = t if part is None else part + t
                lo += w
        if has_acc:
            part = part + c_ref[...]
        o_ref[...] = part.astype(out_dtype)

    specs = [pl.BlockSpec((tm, w), lambda i: (i, 0)) for w in widths] + [pl.BlockSpec(b.shape, lambda i: (0, 0))]
    args = parts + [b]
    aliases = {}
    if has_acc:
        specs.append(pl.BlockSpec((tm, N), lambda i: (i, 0)))
        args.append(acc)
        aliases = {npart + 1: 0}
    return pl.pallas_call(
        body,
        grid=(M // tm,),
        in_specs=specs,
        out_specs=pl.BlockSpec((tm, N), lambda i: (i, 0)),
        out_shape=SDS((M, N), out_dtype),
        input_output_aliases=aliases,
        compiler_params=_cparams(("parallel",), VMEM_BIG),
        name=name,
    )(*args)


PERM_ROWS = 1024


def _perm_spec(d, cols=LANE):
    return pl.BlockSpec((d, PERM_ROWS // d, cols), lambda i, j: (0, i, j))


def _to_natural(src_ref, dst_ref, d):
    n = src_ref.shape[1]
    for r in range(d):
        dst_ref[pl.ds(r, n, stride=d), :] = src_ref[r]


def _prep(x, w):
    S, D = x.shape
    R = PERM_ROWS
    nc = D // LANE

    def body(*refs):
        x_refs, w_ref = refs[:nc], refs[nc]
        h_ref, h4_ref, h16_ref, rs = refs[nc + 1:]
        ssq = None
        for xr in x_refs:
            v = xr[...]
            t = jnp.sum(v * v, axis=-1, keepdims=True)
            ssq = t if ssq is None else ssq + t
        rinv = lax.rsqrt(ssq * (1.0 / D) + EPS)
        rs[...] = jnp.broadcast_to(rinv, (R, LANE))
        for j, xr in enumerate(x_refs):
            cols = slice(j * LANE, (j + 1) * LANE)
            wj = w_ref[:, cols]
            h_ref[:, cols] = ((xr[...] * rinv) * wj).astype(bf16)
            for d, o_ref in ((4, h4_ref), (16, h16_ref)):
                n = R // d
                for r in range(d):
                    rows = pl.ds(r, n, stride=d)
                    o_ref[r, :, cols] = ((xr[rows, :] * rs[rows, :]) * wj).astype(bf16)

    col = lambda j: pl.BlockSpec((R, LANE), lambda i, j=j: (i, j))
    h, h4, h16 = pl.pallas_call(
        body,
        grid=(S // R,),
        in_specs=[col(j) for j in range(nc)] + [pl.BlockSpec((1, D), lambda i: (0, 0))],
        out_specs=[pl.BlockSpec((R, D), lambda i: (i, 0)), pl.BlockSpec((4, R // 4, D), lambda i: (0, i, 0)),
                   pl.BlockSpec((16, R // 16, D), lambda i: (0, i, 0))],
        out_shape=[SDS((S, D), bf16), SDS((4, S // 4, D), bf16), SDS((16, S // 16, D), bf16)],
        scratch_shapes=[pltpu.VMEM((R, LANE), f32)],
        compiler_params=_cparams(("parallel",), VMEM_BIG),
        name="prep_norm_perm",
    )(*([x] * nc), w)
    return [h, h4.reshape(S, D), h16.reshape(S, D)]


def _dh_sum(a, b, c):
    S, D = a.shape
    R = PERM_ROWS

    def body(a_ref, b_ref, c_ref, o_ref, sb, sc):
        _to_natural(b_ref, sb, 4)
        _to_natural(c_ref, sc, 16)
        o_ref[...] = (a_ref[...] + sb[...]) + sc[...]

    nat = pl.BlockSpec((R, LANE), lambda i, j: (i, j))
    return pl.pallas_call(
        body,
        grid=(S // R, D // LANE),
        in_specs=[nat, _perm_spec(4), _perm_spec(16)],
        out_specs=nat,
        out_shape=SDS((S, D), f32),
        scratch_shapes=[pltpu.VMEM((R, LANE), f32)] * 2,
        compiler_params=_cparams(("parallel", "parallel")),
        name="dh_sum",
    )(a, b.reshape(4, S // 4, D), c.reshape(16, S // 16, D))


def _rms_parts(xv):
    r = lax.rsqrt(jnp.mean(xv * xv, axis=-1, keepdims=True) + EPS)
    return r, xv * r


def _rms_bwd(xhat, r, w, dy):
    dyw = dy * w
    return r * (dyw - xhat * jnp.mean(dyw * xhat, axis=-1, keepdims=True))


def _mid_fwd(x, mo, w_pm, w_pf):
    S, D = x.shape
    tm = _pick(S, 512)

    def body(x_ref, mo_ref, wpm_ref, wpf_ref, x1_ref, h2_ref):
        _, moh = _rms_parts(mo_ref[...])
        x1 = x_ref[...] + moh * wpm_ref[...]
        x1_ref[...] = x1
        _, x1h = _rms_parts(x1)
        h2_ref[...] = (x1h * wpf_ref[...]).astype(bf16)

    row = pl.BlockSpec((tm, D), lambda i: (i, 0))
    vec = pl.BlockSpec((1, D), lambda i: (0, 0))
    return pl.pallas_call(
        body,
        grid=(S // tm,),
        in_specs=[row, row, vec, vec],
        out_specs=[row, row],
        out_shape=[SDS((S, D), f32), SDS((S, D), bf16)],
        compiler_params=_cparams(("parallel",)),
        name="mid_fwd",
    )(x, mo, w_pm, w_pf)


def _final(x1, fo, tgt, w_pfn):
    S, D = x1.shape
    tm = _pick(S, 512)
    nt = S // tm

    def body(x1_ref, fo_ref, t_ref, w_ref, loss_ref, dy_ref, dfo_ref, gw_ref, lacc, gacc):
        i = pl.program_id(0)

        @pl.when(i == 0)
        def _():
            lacc[...] = jnp.zeros_like(lacc)
            gacc[...] = jnp.zeros_like(gacc)

        w = w_ref[...]
        r, foh = _rms_parts(fo_ref[...])
        y = x1_ref[...] + foh * w
        err = y - t_ref[...]
        lacc[...] += _colsum8(err * err)
        dy = err * (1.0 / D)
        dy_ref[...] = dy
        gacc[...] += _colsum8(dy * foh)
        dfo_ref[...] = _rms_bwd(foh, r, w, dy).astype(bf16)

        @pl.when(i == nt - 1)
        def _():
            loss_ref[...] = jnp.full((SUBLANE, LANE), 0.5 / D, f32) * jnp.sum(lacc[...])
            gw_ref[...] = jnp.sum(gacc[...], axis=0, keepdims=True)

    row = pl.BlockSpec((tm, D), lambda i: (i, 0))
    vec = pl.BlockSpec((1, D), lambda i: (0, 0))
    return pl.pallas_call(
        body,
        grid=(nt,),
        in_specs=[row, row, row, vec],
        out_specs=[pl.BlockSpec((SUBLANE, LANE), lambda i: (0, 0)), row, row, vec],
        out_shape=[SDS((SUBLANE, LANE), f32), SDS((S, D), f32), SDS((S, D), bf16), SDS((1, D), f32)],
        scratch_shapes=[pltpu.VMEM((SUBLANE, D), f32), pltpu.VMEM((SUBLANE, D), f32)],
        compiler_params=_cparams(("arbitrary",)),
        name="final_loss",
    )(x1, fo, tgt, w_pfn)


def _mid_bwd(dy, dh2, x1, mo, w_pf, w_pm):
    S, D = dy.shape
    tm = _pick(S, 512)
    nt = S // tm

    def body(dy_ref, dh2_ref, x1_ref, mo_ref, wpf_ref, wpm_ref, dx1_ref, dmo_ref, gpf_ref, gpm_ref, apf, apm):
        i = pl.program_id(0)

        @pl.when(i == 0)
        def _():
            apf[...] = jnp.zeros_like(apf)
            apm[...] = jnp.zeros_like(apm)

        r1, x1h = _rms_parts(x1_ref[...])
        dh2 = dh2_ref[...]
        apf[...] += _colsum8(dh2 * x1h)
        dx1 = dy_ref[...] + _rms_bwd(x1h, r1, wpf_ref[...], dh2)
        dx1_ref[...] = dx1
        rm, moh = _rms_parts(mo_ref[...])
        apm[...] += _colsum8(dx1 * moh)
        dmo_ref[...] = _rms_bwd(moh, rm, wpm_ref[...], dx1).astype(bf16)

        @pl.when(i == nt - 1)
        def _():
            gpf_ref[...] = jnp.sum(apf[...], axis=0, keepdims=True)
            gpm_ref[...] = jnp.sum(apm[...], axis=0, keepdims=True)

    row = pl.BlockSpec((tm, D), lambda i: (i, 0))
    vec = pl.BlockSpec((1, D), lambda i: (0, 0))
    return pl.pallas_call(
        body,
        grid=(nt,),
        in_specs=[row, row, row, row, vec, vec],
        out_specs=[row, row, vec, vec],
        out_shape=[SDS((S, D), f32), SDS((S, D), bf16), SDS((1, D), f32), SDS((1, D), f32)],
        scratch_shapes=[pltpu.VMEM((SUBLANE, D), f32), pltpu.VMEM((SUBLANE, D), f32)],
        compiler_params=_cparams(("arbitrary",)),
        name="mid_bwd",
    )(dy, dh2, x1, mo, w_pf, w_pm)


def _first_bwd(x, dx1, dh, w_pre):
    S, D = x.shape
    tm = _pick(S, 512)
    nt = S // tm

    def body(x_ref, dx1_ref, a_ref, w_ref, gx_ref, gw_ref, acc):
        i = pl.program_id(0)

        @pl.when(i == 0)
        def _():
            acc[...] = jnp.zeros_like(acc)

        r, xh = _rms_parts(x_ref[...])
        dh = a_ref[...]
        acc[...] += _colsum8(dh * xh)
        gx_ref[...] = dx1_ref[...] + _rms_bwd(xh, r, w_ref[...], dh)

        @pl.when(i == nt - 1)
        def _():
            gw_ref[...] = jnp.sum(acc[...], axis=0, keepdims=True)

    row = pl.BlockSpec((tm, D), lambda i: (i, 0))
    vec = pl.BlockSpec((1, D), lambda i: (0, 0))
    return pl.pallas_call(
        body,
        grid=(nt,),
        in_specs=[row, row, row, vec],
        out_specs=[row, vec],
        out_shape=[SDS((S, D), f32), SDS((1, D), f32)],
        scratch_shapes=[pltpu.VMEM((SUBLANE, D), f32)],
        compiler_params=_cparams(("arbitrary",)),
        name="first_bwd",
    )(x, dx1, dh, w_pre)


def _t5_bucket(dist):
    n = jnp.maximum(dist, 0)
    nf = jnp.maximum(n, 1).astype(f32)
    large = MAX_EXACT + (jnp.log(nf / MAX_EXACT) / math.log(MAX_DISTANCE / MAX_EXACT)
                         * (NUM_BUCKETS - MAX_EXACT)).astype(jnp.int32)
    large = jnp.minimum(large, NUM_BUCKETS - 1)
    return jnp.where(n < MAX_EXACT, n, large)


def _bias_consts(d):
    blk = ATTN_BLOCK
    rel = jnp.arange(blk)[:, None] + blk - jnp.arange(2 * blk)[None, :]
    in_win = (rel >= 0) & (rel <= blk)
    bucket = _t5_bucket(rel * d).reshape(1, -1)
    onehot = (bucket == jnp.arange(NUM_BUCKETS)[:, None]).astype(f32)
    return onehot, in_win.astype(f32).reshape(1, -1)


def _bias_build(tab_t, onehot, maskf, name):
    H = tab_t.shape[0]

    def body(t_ref, oh_ref, m_ref, o_ref):
        b = jnp.dot(t_ref[...], oh_ref[...], precision=HIGHEST, preferred_element_type=f32)
        o_ref[...] = jnp.where(m_ref[...] > 0.5, b, NEG_INF)

    return pl.pallas_call(body, out_shape=SDS((H, onehot.shape[1]), f32), name=name)(tab_t, onehot, maskf)


def _bias_grad(dbias_flat, onehot, name):
    H = dbias_flat.shape[0]

    def body(g_ref, oh_ref, o_ref):
        o_ref[...] = lax.dot_general(oh_ref[...], g_ref[...], NT, precision=HIGHEST, preferred_element_type=f32)

    return pl.pallas_call(body, out_shape=SDS((NUM_BUCKETS, H), f32), name=name)(dbias_flat, onehot)


ATTN_TILE = 512
ATTN_SUB = ATTN_TILE // ATTN_BLOCK


def _qkv_specs(nt):
    tile = (ATTN_TILE, LANE)
    blk = (ATTN_BLOCK, LANE)
    cur = lambda off: (lambda h, t: (jnp.minimum(t, nt - 1), off + h))
    prev = lambda off: (lambda h, t: (jnp.maximum(jnp.minimum(t, nt - 1) * ATTN_SUB - 1, 0), off + h))
    return [pl.BlockSpec(tile, cur(0)), pl.BlockSpec(blk, prev(4)), pl.BlockSpec(tile, cur(4)),
            pl.BlockSpec(blk, prev(8)), pl.BlockSpec(tile, cur(8))]


def _head_masks():
    lane = lax.broadcasted_iota(jnp.int32, (ATTN_BLOCK, LANE), 1)
    return lane < HEAD_DIM


def _attn_fwd(qkv, bias, bps, name):
    S = qkv.shape[0]
    nt = S // ATTN_TILE
    scale = HEAD_DIM ** -0.5

    def body(q_ref, kp_ref, kc_ref, vp_ref, vc_ref, b_ref, o_ref, l_ref):
        t = pl.program_id(1)
        kk = jnp.concatenate([kp_ref[...], kc_ref[...]], axis=0)
        vv = jnp.concatenate([vp_ref[...], vc_ref[...]], axis=0)
        low = _head_masks()
        col = lax.broadcasted_iota(jnp.int32, (ATTN_BLOCK, 2 * ATTN_BLOCK), 1)
        for b in range(ATTN_SUB):
            lo = b * ATTN_BLOCK
            rows = slice(lo, lo + ATTN_BLOCK)
            keys = slice(lo, lo + 2 * ATTN_BLOCK)
            dead = jnp.logical_and((t * ATTN_SUB + b) % bps == 0, col < ATTN_BLOCK)
            q2 = q_ref[rows, :]
            kb, vb = kk[keys], vv[keys]
            outs, lses = [], []
            for h in range(2):
                hm = low if h == 0 else jnp.logical_not(low)
                qh = jnp.where(hm, q2, jnp.zeros_like(q2))
                s = lax.dot_general(qh, kb, NT, preferred_element_type=f32) * scale + b_ref[h]
                s = jnp.where(dead, NEG_INF, s)
                m = jnp.max(s, axis=-1, keepdims=True)
                p = jnp.exp(s - m)
                l = jnp.sum(p, axis=-1, keepdims=True)
                outs.append(jnp.dot(p.astype(bf16), vb, preferred_element_type=f32) / l)
                lses.append(m + jnp.log(l))
            o_ref[rows, :] = jnp.where(low, outs[0], outs[1])
            l_ref[rows, :] = jnp.where(low, lses[0], lses[1])

    tile = pl.BlockSpec((ATTN_TILE, LANE), lambda h, t: (t, h))
    return pl.pallas_call(
        body,
        grid=(4, nt),
        in_specs=_qkv_specs(nt) + [pl.BlockSpec((2, ATTN_BLOCK, 2 * ATTN_BLOCK), lambda h, t: (h, 0, 0))],
        out_specs=[tile, tile],
        out_shape=[SDS((S, ATTN_OUT), f32), SDS((S, ATTN_OUT), f32)],
        compiler_params=_cparams(("parallel", "parallel")),
        name=name,
    )(qkv, qkv, qkv, qkv, qkv, bias)


def _attn_bwd(qkv, bias, do, dvec, lse, bps, name):
    S = qkv.shape[0]
    nt = S // ATTN_TILE
    scale = HEAD_DIM ** -0.5

    def assemble(parts):
        rows = [parts[0][:ATTN_BLOCK]]
        for b in range(ATTN_SUB - 1):
            rows.append(parts[b][ATTN_BLOCK:] + parts[b + 1][:ATTN_BLOCK])
        rows.append(parts[-1][ATTN_BLOCK:])
        return rows

    def body(q_ref, kp_ref, kc_ref, vp_ref, vc_ref, b_ref, do_ref, dvec_ref, lse_ref,
             dq_ref, dk_ref, dv_ref, db_ref, ck, cv):
        t = pl.program_id(1)
        last = ATTN_TILE - ATTN_BLOCK

        @pl.when(t == 0)
        def _():
            ck[...] = jnp.zeros_like(ck)
            cv[...] = jnp.zeros_like(cv)
            db_ref[...] = jnp.zeros_like(db_ref)

        @pl.when(t < nt)
        def _():
            kk = jnp.concatenate([kp_ref[...], kc_ref[...]], axis=0)
            vv = jnp.concatenate([vp_ref[...], vc_ref[...]], axis=0)
            low = _head_masks()
            col = lax.broadcasted_iota(jnp.int32, (ATTN_BLOCK, 2 * ATTN_BLOCK), 1)
            low2 = lax.broadcasted_iota(jnp.int32, (2 * ATTN_BLOCK, LANE), 1) < HEAD_DIM
            dk_parts, dv_parts = [], []
            dsum = [None, None]
            for b in range(ATTN_SUB):
                lo = b * ATTN_BLOCK
                rows = slice(lo, lo + ATTN_BLOCK)
                keys = slice(lo, lo + 2 * ATTN_BLOCK)
                dead = jnp.logical_and((t * ATTN_SUB + b) % bps == 0, col < ATTN_BLOCK)
                q2 = q_ref[rows, :]
                kb, vb = kk[keys], vv[keys]
                do2 = do_ref[rows, :].astype(bf16)
                dvec2 = dvec_ref[rows, :]
                lse2 = lse_ref[rows, :]
                dqs, dks, dvs = [], [], []
                for h in range(2):
                    hm = low if h == 0 else jnp.logical_not(low)
                    c0 = h * HEAD_DIM
                    qh = jnp.where(hm, q2, jnp.zeros_like(q2))
                    doh = jnp.where(hm, do2, jnp.zeros_like(do2))
                    s = lax.dot_general(qh, kb, NT, preferred_element_type=f32) * scale + b_ref[h]
                    s = jnp.where(dead, NEG_INF, s)
                    p = jnp.exp(s - lse2[:, c0:c0 + 1])
                    dp = lax.dot_general(doh, vb, NT, preferred_element_type=f32)
                    ds = p * (dp - dvec2[:, c0:c0 + 1])
                    dsum[h] = ds if dsum[h] is None else dsum[h] + ds
                    dsb = ds.astype(bf16)
                    dqs.append(jnp.dot(dsb, kb, preferred_element_type=f32) * scale)
                    dks.append(lax.dot_general(dsb, q2, TN, preferred_element_type=f32) * scale)
                    dvs.append(lax.dot_general(p.astype(bf16), do2, TN, preferred_element_type=f32))
                dq_ref[rows, :] = jnp.where(low, dqs[0], dqs[1]).astype(bf16)
                dk_parts.append(jnp.where(low2, dks[0], dks[1]))
                dv_parts.append(jnp.where(low2, dvs[0], dvs[1]))
            db_ref[0] += dsum[0]
            db_ref[1] += dsum[1]
            for parts, carry, out_ref in ((dk_parts, ck, dk_ref), (dv_parts, cv, dv_ref)):
                rws = assemble(parts)
                out_ref[:last, :] = carry[:last, :].astype(bf16)
                out_ref[last:, :] = (carry[last:, :] + rws[0]).astype(bf16)
                for b in range(ATTN_SUB):
                    carry[b * ATTN_BLOCK:(b + 1) * ATTN_BLOCK, :] = rws[b + 1]

        @pl.when(t == nt)
        def _():
            dk_ref[...] = ck[...].astype(bf16)
            dv_ref[...] = cv[...].astype(bf16)

    tile = (ATTN_TILE, LANE)
    cur = pl.BlockSpec(tile, lambda h, t: (jnp.minimum(t, nt - 1), h))
    lag = pl.BlockSpec(tile, lambda h, t: (jnp.maximum(t - 1, 0), h))
    bspec = pl.BlockSpec((2, ATTN_BLOCK, 2 * ATTN_BLOCK), lambda h, t: (h, 0, 0))
    return pl.pallas_call(
        body,
        grid=(4, nt + 1),
        in_specs=_qkv_specs(nt) + [bspec, cur, cur, cur],
        out_specs=[cur, lag, lag, bspec],
        out_shape=[SDS((S, ATTN_OUT), bf16), SDS((S, ATTN_OUT), bf16), SDS((S, ATTN_OUT), bf16),
                   SDS((8, ATTN_BLOCK, 2 * ATTN_BLOCK), f32)],
        scratch_shapes=[pltpu.VMEM(tile, f32), pltpu.VMEM(tile, f32)],
        compiler_params=_cparams(("parallel", "arbitrary")),
        name=name,
    )(qkv, qkv, qkv, qkv, qkv, bias, do, dvec, lse)


def _attn_merge(o0, o1, o2, l0, l1, l2):
    S, W = o0.shape
    R = PERM_ROWS

    def body(o0_ref, o1_ref, o2_ref, l0_ref, l1_ref, l2_ref, y_ref, yb_ref, w0_ref, w1_ref, w2_ref,
             so1, so2, sl1, sl2):
        _to_natural(o1_ref, so1, 4)
        _to_natural(l1_ref, sl1, 4)
        _to_natural(o2_ref, so2, 16)
        _to_natural(l2_ref, sl2, 16)
        a, b, c = l0_ref[...], sl1[...], sl2[...]
        m = jnp.maximum(jnp.maximum(a, b), c)
        ea, eb, ec = jnp.exp(a - m), jnp.exp(b - m), jnp.exp(c - m)
        den = (ea + eb) + ec
        w0, w1, w2 = ea / den, eb / den, ec / den
        y = (w0 * o0_ref[...] + w1 * so1[...]) + w2 * so2[...]
        y_ref[...] = y
        yb_ref[...] = y.astype(bf16)
        w0_ref[...] = w0
        w1_ref[...] = w1
        w2_ref[...] = w2

    nat = pl.BlockSpec((R, LANE), lambda i, j: (i, j))
    v4 = lambda t: t.reshape(4, S // 4, W)
    v16 = lambda t: t.reshape(16, S // 16, W)
    return pl.pallas_call(
        body,
        grid=(S // R, W // LANE),
        in_specs=[nat, _perm_spec(4), _perm_spec(16)] * 2,
        out_specs=[nat] * 5,
        out_shape=[SDS((S, W), f32), SDS((S, W), bf16)] + [SDS((S, W), f32)] * 3,
        scratch_shapes=[pltpu.VMEM((R, LANE), f32)] * 4,
        compiler_params=_cparams(("parallel", "parallel")),
        name="attn_merge",
    )(o0, v4(o1), v16(o2), l0, v4(l1), v16(l2))


def _attn_merge_bwd(dy, y, w0, w1, w2):
    S, W = dy.shape
    R = PERM_ROWS

    def body(dy_ref, y_ref, w0_ref, w1_ref, w2_ref, a0, a1, a2, b0, b1, b2, sa, sb):
        dyv = dy_ref[...]
        r = lax.broadcasted_iota(jnp.int32, (LANE, LANE), 0) // HEAD_DIM
        c = lax.broadcasted_iota(jnp.int32, (LANE, LANE), 1) // HEAD_DIM
        seg = jnp.where(r == c, 1.0, 0.0).astype(f32)
        cbar = jnp.dot(dyv * y_ref[...], seg, precision=HIGHEST, preferred_element_type=f32)
        w = w0_ref[...]
        a0[...] = (w * dyv).astype(bf16)
        b0[...] = w * cbar
        for d, w_ref, a_ref, b_ref in ((4, w1_ref, a1, b1), (16, w2_ref, a2, b2)):
            w = w_ref[...]
            sa[...] = w * dyv
            sb[...] = w * cbar
            n = R // d
            for k in range(d):
                rows = pl.ds(k, n, stride=d)
                a_ref[k] = sa[rows, :].astype(bf16)
                b_ref[k] = sb[rows, :]

    nat = pl.BlockSpec((R, LANE), lambda i, j: (i, j))
    shapes = lambda dt: [SDS((S, W), dt), SDS((4, S // 4, W), dt), SDS((16, S // 16, W), dt)]
    outs = pl.pallas_call(
        body,
        grid=(S // R, W // LANE),
        in_specs=[nat] * 5,
        out_specs=[nat, _perm_spec(4), _perm_spec(16)] * 2,
        out_shape=shapes(bf16) + shapes(f32),
        scratch_shapes=[pltpu.VMEM((R, LANE), f32)] * 2,
        compiler_params=_cparams(("parallel", "parallel")),
        name="attn_merge_bwd",
    )(dy, y, w0, w1, w2)
    return [t.reshape(S, W) for t in outs]


HGRN_SB = 256


def _chunk_masks(sb):
    r = lax.broadcasted_iota(jnp.int32, (sb, sb), 0)
    c = lax.broadcasted_iota(jnp.int32, (sb, sb), 1)
    same = (r // HGRN_CHUNK) == (c // HGRN_CHUNK)
    return same, jnp.logical_and(same, c <= r), jnp.logical_and(same, c >= r)


def _hgrn_prep(q_raw, f_raw, lbv, same, tril):
    sq = _sigmoid(q_raw)
    qs = q_raw * sq
    sig = _sigmoid(f_raw)
    f = lbv + (1.0 - lbv) * sig
    g = jnp.log(f)
    k = 1.0 - f
    G = jnp.dot(jnp.where(tril, 1.0, 0.0).astype(f32), g, precision=HIGHEST, preferred_element_type=f32)
    GL = jnp.dot(jnp.where(same, 1.0, 0.0).astype(f32), g, precision=HIGHEST, preferred_element_type=f32)
    eG = jnp.exp(G)
    einv = jnp.exp(-G)
    edec = jnp.exp(GL - G)
    return dict(sq=sq, qs=qs, sig=sig, f=f, k=k, eG=eG, einv=einv, edec=edec, eGL=jnp.exp(GL),
                qt=qs * eG, kt=k * einv, kd=k * edec)


def _ride_split(ride, rest, n_out, n_scratch):
    if ride is None:
        return None, rest[:n_out], None, rest[n_out:], None
    return rest[0], rest[1:1 + n_out], rest[1 + n_out], rest[2 + n_out:2 + n_out + n_scratch], rest[2 + n_out + n_scratch:]


def _hgrn_fwd(hg, lb, normw, ride=None):
    S = hg.shape[0]
    sb = HGRN_SB
    nsb = S // sb
    nch = sb // HGRN_CHUNK

    def body(q_ref, f_ref, v_ref, og_ref, lb_ref, nw_ref, *rest):
        src_ref, (y_ref, o_ref, ck_ref), got_ref, (st,), sems = _ride_split(ride, rest, 3, 1)
        j = pl.program_id(1)
        if ride is not None:
            @pl.when(jnp.logical_and(pl.program_id(0) == 0, j == 0))
            def _():
                _chip_start(src_ref, got_ref, sems[0], sems[1], ride[1])

        @pl.when(j == 0)
        def _():
            st[...] = jnp.zeros_like(st)

        ST = st[...]
        ck_ref[0, 0] = ST
        same, tril, _ = _chunk_masks(sb)
        pr = _hgrn_prep(q_ref[...], f_ref[...], lb_ref[...], same, tril)
        qtb, ktb, kdb = pr["qt"].astype(bf16), pr["kt"].astype(bf16), pr["kd"].astype(bf16)
        eGL = pr["eGL"]
        vb = v_ref[...].astype(bf16)
        A = jnp.where(tril, lax.dot_general(qtb, ktb, NT, preferred_element_type=f32), 0.0)
        o = jnp.dot(A.astype(bf16), vb, preferred_element_type=f32)
        outs = []
        for ci in range(nch):
            lo = ci * HGRN_CHUNK
            sl = slice(lo, lo + HGRN_CHUNK)
            outs.append(o[sl] + lax.dot_general(qtb[sl], ST.astype(bf16), NT, preferred_element_type=f32))
            ST = ST * eGL[lo:lo + 1, :] + lax.dot_general(vb[sl], kdb[sl], TN, preferred_element_type=f32)
        st[...] = ST
        of = jnp.concatenate(outs, axis=0)
        o_ref[...] = of
        rms = lax.rsqrt(jnp.mean(of * of, axis=-1, keepdims=True) + EPS)
        ogv = og_ref[...]
        y_ref[...] = ((of * rms * nw_ref[...]) * (ogv * _sigmoid(ogv))).astype(bf16)

        if ride is not None:
            @pl.when(jnp.logical_and(pl.program_id(0) == 3, j == nsb - 1))
            def _():
                _chip_finish(src_ref, got_ref, sems[0], sems[1], ride[1])

    col = lambda off: pl.BlockSpec((sb, LANE), lambda h, j: (j, off + h))
    riding = ride is not None
    res = pl.pallas_call(
        body,
        grid=(4, nsb),
        in_specs=[col(0), col(4), col(8), col(12), pl.BlockSpec((1, LANE), lambda h, j: (0, h)),
                  pl.BlockSpec((1, LANE), lambda h, j: (0, 0))] + ([_ANY] if riding else []),
        out_specs=[col(0), col(0), pl.BlockSpec((1, 1, LANE, LANE), lambda h, j: (h, j, 0, 0))]
        + ([_ANY] if riding else []),
        out_shape=[SDS((S, HGRN_W), bf16), SDS((S, HGRN_W), f32), SDS((4, nsb, LANE, LANE), f32)]
        + ([_chip_out_shape(*ride)] if riding else []),
        scratch_shapes=[pltpu.VMEM((LANE, LANE), f32)] + (list(_CHIP_SEMS) if riding else []),
        compiler_params=_cparams(("arbitrary", "arbitrary") if riding else ("parallel", "arbitrary")),
        name="hgrn_fwd",
    )(hg, hg, hg, hg, lb, normw, *([ride[0]] if riding else []))
    return tuple(res) if riding else (*res, None)


def _hgrn_bwd(hg, o_raw, dy, ck, lb, normw, ride=None):
    S = hg.shape[0]
    sb = HGRN_SB
    nsb = S // sb
    nch = sb // HGRN_CHUNK

    def body(q_ref, f_ref, v_ref, og_ref, o_ref, dy_ref, ck_ref, lb_ref, nw_ref, *rest):
        src_ref, outs, got_ref, (dst, alb, anw), sems = _ride_split(ride, rest, 6, 3)
        dq_ref, df_ref, dv_ref, dog_ref, glb_ref, gnw_ref = outs
        j = pl.program_id(1)
        if ride is not None:
            @pl.when(jnp.logical_and(pl.program_id(0) == 0, j == 0))
            def _():
                _chip_start(src_ref, got_ref, sems[0], sems[1], ride[1])

        @pl.when(j == 0)
        def _():
            dst[...] = jnp.zeros_like(dst)
            alb[...] = jnp.zeros_like(alb)
            anw[...] = jnp.zeros_like(anw)

        same, tril, triu = _chunk_masks(sb)
        lbv = lb_ref[...]
        q_raw = q_ref[...]
        pr = _hgrn_prep(q_raw, f_ref[...], lbv, same, tril)
        qt, kt, kd, eGL = pr["qt"], pr["kt"], pr["kd"], pr["eGL"]
        qtb, ktb, kdb = qt.astype(bf16), kt.astype(bf16), kd.astype(bf16)
        vb = v_ref[...].astype(bf16)

        o = o_ref[...]
        ogv = og_ref[...]
        sog = _sigmoid(ogv)
        rms = lax.rsqrt(jnp.mean(o * o, axis=-1, keepdims=True) + EPS)
        oh = o * rms
        nw = nw_ref[...]
        dyv = dy_ref[...]
        dog_ref[...] = (dyv * (oh * nw) * (sog * (1.0 + ogv * (1.0 - sog)))).astype(bf16)
        dohw = dyv * (ogv * sog)
        anw[...] += _colsum8(dohw * oh)
        doh = dohw * nw
        do = rms * (doh - oh * jnp.mean(doh * oh, axis=-1, keepdims=True))
        dob = do.astype(bf16)

        Ab = jnp.where(tril, lax.dot_general(qtb, ktb, NT, preferred_element_type=f32), 0.0).astype(bf16)
        dAb = jnp.where(tril, lax.dot_general(dob, vb, NT, preferred_element_type=f32), 0.0).astype(bf16)
        dv_acc = lax.dot_general(Ab, dob, TN, preferred_element_type=f32)
        dqt = jnp.dot(dAb, ktb, preferred_element_type=f32)
        dkt = lax.dot_general(dAb, qtb, TN, preferred_element_type=f32)

        ST = ck_ref[0, 0]
        states = []
        for ci in range(nch):
            lo = ci * HGRN_CHUNK
            sl = slice(lo, lo + HGRN_CHUNK)
            states.append(ST)
            ST = ST * eGL[lo:lo + 1, :] + lax.dot_general(vb[sl], kdb[sl], TN, preferred_element_type=f32)

        dST = dst[...]
        dqt_i, dkd_i, dv_i, deg_i = [None] * nch, [None] * nch, [None] * nch, [None] * nch
        for ci in reversed(range(nch)):
            lo = ci * HGRN_CHUNK
            sl = slice(lo, lo + HGRN_CHUNK)
            ST0 = states[ci]
            dSTb = dST.astype(bf16)
            dv_i[ci] = lax.dot_general(kdb[sl], dSTb, NT, preferred_element_type=f32)
            dqt_i[ci] = jnp.dot(dob[sl], ST0.astype(bf16), preferred_element_type=f32)
            dkd_i[ci] = jnp.dot(vb[sl], dSTb, preferred_element_type=f32)
            deg_i[ci] = jnp.broadcast_to(jnp.sum(dST * ST0, axis=0, keepdims=True), (HGRN_CHUNK, LANE))
            dST = dST * eGL[lo:lo + 1, :] + lax.dot_general(dob[sl], qtb[sl], TN, preferred_element_type=f32)
        dst[...] = dST

        dqt = dqt + jnp.concatenate(dqt_i, axis=0)
        dkd = jnp.concatenate(dkd_i, axis=0)
        dv_ref[...] = (dv_acc + jnp.concatenate(dv_i, axis=0)).astype(bf16)
        deg = jnp.concatenate(deg_i, axis=0)

        dqs = dqt * pr["eG"]
        dkdkd = dkd * kd
        dG = dqt * qt - dkt * kt - dkdkd
        dk = dkt * pr["einv"] + dkd * pr["edec"]
        dGL = jnp.dot(jnp.where(same, 1.0, 0.0).astype(f32), dkdkd, precision=HIGHEST,
                      preferred_element_type=f32) + eGL * deg
        dg = jnp.dot(jnp.where(triu, 1.0, 0.0).astype(f32), dG, precision=HIGHEST,
                     preferred_element_type=f32) + dGL
        df = dg / pr["f"] - dk
        sig = pr["sig"]
        df_ref[...] = (df * (1.0 - lbv) * (sig * (1.0 - sig))).astype(bf16)
        alb[...] += _colsum8(df * (1.0 - sig))
        sq = pr["sq"]
        dq_ref[...] = (dqs * (sq * (1.0 + q_raw * (1.0 - sq)))).astype(bf16)

        @pl.when(j == nsb - 1)
        def _():
            glb_ref[...] = jnp.broadcast_to(jnp.sum(alb[...], axis=0, keepdims=True), (SUBLANE, LANE))
            gnw_ref[...] = jnp.broadcast_to(jnp.sum(anw[...], axis=0, keepdims=True), (SUBLANE, LANE))

        if ride is not None:
            @pl.when(jnp.logical_and(pl.program_id(0) == 3, j == nsb - 1))
            def _():
                _chip_finish(src_ref, got_ref, sems[0], sems[1], ride[1])

    rev = lambda off: pl.BlockSpec((sb, LANE), lambda h, j: (nsb - 1 - j, off + h))
    stat = pl.BlockSpec((SUBLANE, LANE), lambda h, j: (0, h))
    riding = ride is not None
    res = pl.pallas_call(
        body,
        grid=(4, nsb),
        in_specs=[rev(0), rev(4), rev(8), rev(12), rev(0), rev(0),
                  pl.BlockSpec((1, 1, LANE, LANE), lambda h, j: (h, nsb - 1 - j, 0, 0)),
                  pl.BlockSpec((1, LANE), lambda h, j: (0, h)), pl.BlockSpec((1, LANE), lambda h, j: (0, 0))]
        + ([_ANY] if riding else []),
        out_specs=[rev(0), rev(0), rev(0), rev(0), stat, stat] + ([_ANY] if riding else []),
        out_shape=[SDS((S, HGRN_W), bf16)] * 4 + [SDS((SUBLANE, HGRN_W), f32)] * 2
        + ([_chip_out_shape(*ride)] if riding else []),
        scratch_shapes=[pltpu.VMEM((LANE, LANE), f32), pltpu.VMEM((SUBLANE, LANE), f32),
                        pltpu.VMEM((SUBLANE, LANE), f32)] + (list(_CHIP_SEMS) if riding else []),
        compiler_params=_cparams(("arbitrary", "arbitrary") if riding else ("parallel", "arbitrary")),
        name="hgrn_bwd",
    )(hg, hg, hg, hg, o_raw, dy, ck, lb, normw, *([ride[0]] if riding else []))
    return tuple(res) if riding else (*res, None)


def _lb_fwd(raw):
    def body(r_ref, o_ref):
        r = r_ref[...]
        m = jnp.max(r, axis=0, keepdims=True)
        e = jnp.exp(r - m)
        o_ref[...] = (e / jnp.sum(e, axis=0, keepdims=True))[0:1]

    return pl.pallas_call(body, out_shape=SDS((1, raw.shape[1]), f32), name="lb_fwd")(raw)


def _lb_bwd(raw, dlb):
    def body(r_ref, d_ref, o_ref):
        r = r_ref[...]
        m = jnp.max(r, axis=0, keepdims=True)
        e = jnp.exp(r - m)
        s = e / jnp.sum(e, axis=0, keepdims=True)
        s0 = s[0:1]
        onehot0 = jnp.where(lax.broadcasted_iota(jnp.int32, r.shape, 0) == 0, 1.0, 0.0)
        o_ref[...] = d_ref[...] * s0 * (onehot0 - s)

    return pl.pallas_call(body, out_shape=SDS(raw.shape, f32), name="lb_bwd")(raw, dlb)


def _gate_fwd(a, b, gc):
    S, D = a.shape
    tm = _pick(S, 512)

    def body(a_ref, b_ref, g0_ref, g1_ref, o_ref):
        s0, s1 = _sigmoid(g0_ref[...].astype(f32)), _sigmoid(g1_ref[...].astype(f32))
        o_ref[...] = (s0 * a_ref[...].astype(f32) + s1 * b_ref[...].astype(f32)).astype(bf16)

    row = pl.BlockSpec((tm, D), lambda i: (i, 0))
    return pl.pallas_call(
        body,
        grid=(S // tm,),
        in_specs=[row, row, row, pl.BlockSpec((tm, D), lambda i: (i, 1))],
        out_specs=row,
        out_shape=SDS((S, D), bf16),
        compiler_params=_cparams(("parallel",)),
        name="gate_fwd",
    )(a, b, gc, gc)


def _gate_bwd(dm, a, b, gc):
    S, D = a.shape
    tm = _pick(S, 512)

    def body(dm_ref, a_ref, b_ref, g0_ref, g1_ref, da_ref, db_ref, dg_ref):
        dmv = dm_ref[...].astype(f32)
        s0, s1 = _sigmoid(g0_ref[...].astype(f32)), _sigmoid(g1_ref[...].astype(f32))
        da_ref[...] = (dmv * s0).astype(bf16)
        db_ref[...] = (dmv * s1).astype(bf16)
        dg_ref[:, :D] = (dmv * a_ref[...].astype(f32) * (s0 * (1.0 - s0))).astype(bf16)
        dg_ref[:, D:] = (dmv * b_ref[...].astype(f32) * (s1 * (1.0 - s1))).astype(bf16)

    row = pl.BlockSpec((tm, D), lambda i: (i, 0))
    wide = pl.BlockSpec((tm, 2 * D), lambda i: (i, 0))
    return pl.pallas_call(
        body,
        grid=(S // tm,),
        in_specs=[row, row, row, row, pl.BlockSpec((tm, D), lambda i: (i, 1))],
        out_specs=[row, row, wide],
        out_shape=[SDS((S, D), bf16), SDS((S, D), bf16), SDS((S, 2 * D), bf16)],
        compiler_params=_cparams(("parallel",)),
        name="gate_bwd",
    )(dm, a, b, gc, gc)


CONV_ROWS = 512
INV_SQRT2 = 0.7071067811865476
INV_SQRT_2PI = 0.3989422804014327


CONV_HALO = 16


def _tile8(a, rows):
    return jnp.tile(a, (rows // a.shape[0], 1))


def _conv_rows(u_ref, w, b, r0, first):
    R = CONV_ROWS
    cur = u_ref[pl.ds(r0, R), :].astype(f32)
    prev8 = u_ref[pl.ds(pl.multiple_of(jnp.maximum(r0 - CONV_HALO, 0), CONV_HALO), CONV_HALO), :].astype(f32)
    prev8 = jnp.where(first, 0.0, prev8)
    row = lax.broadcasted_iota(jnp.int32, (R, LANE), 0)
    x1 = jnp.where(row < 1, _tile8(pltpu.roll(prev8, 1, 0), R), pltpu.roll(cur, 1, 0))
    x2 = jnp.where(row < 2, _tile8(pltpu.roll(prev8, 2, 0), R), pltpu.roll(cur, 2, 0))
    c = ((b + w[0:1] * x2) + w[1:2] * x1) + w[2:3] * cur
    return c, x2, x1, cur


def _conv_fwd(ug, uv, wg, wv, bg, bv):
    S, F = ug.shape
    nchunk = S // CONV_ROWS

    def body(ug_ref, uv_ref, wg_ref, wv_ref, bg_ref, bv_ref, o_ref):
        wgv, wvv, bgv, bvv = wg_ref[...], wv_ref[...], bg_ref[...], bv_ref[...]

        def step(ci, carry):
            r0 = pl.multiple_of(ci * CONV_ROWS, CONV_ROWS)
            cg = _conv_rows(ug_ref, wgv, bgv, r0, ci == 0)[0]
            cv = _conv_rows(uv_ref, wvv, bvv, r0, ci == 0)[0]
            gelu = 0.5 * cg * (1.0 + lax.erf(cg * INV_SQRT2))
            o_ref[pl.ds(r0, CONV_ROWS), :] = (gelu * cv).astype(bf16)
            return carry

        lax.fori_loop(0, nchunk, step, 0)

    col = pl.BlockSpec((S, LANE), lambda j: (0, j))
    w3 = pl.BlockSpec((3, LANE), lambda j: (0, j))
    b1 = pl.BlockSpec((1, LANE), lambda j: (0, j))
    return pl.pallas_call(
        body,
        grid=(F // LANE,),
        in_specs=[col, col, w3, w3, b1, b1],
        out_specs=col,
        out_shape=SDS((S, F), bf16),
        compiler_params=_cparams(("parallel",), VMEM_BIG),
        name="conv_fwd",
    )(ug, uv, wg, wv, bg, bv)


def _conv_bwd(ug, uv, dact, wg, wv, bg, bv):
    S, F = ug.shape
    R = CONV_ROWS
    nchunk = S // R

    def body(ug_ref, uv_ref, da_ref, wg_ref, wv_ref, bg_ref, bv_ref, dug_ref, duv_ref, sg_ref, sv_ref, dcg, dcv):
        wgv, wvv, bgv, bvv = wg_ref[...], wv_ref[...], bg_ref[...], bv_ref[...]
        zero = jnp.zeros((SUBLANE, LANE), f32)

        def fwd_step(ci, acc):
            r0 = pl.multiple_of(ci * R, R)
            cg, g2, g1, g0 = _conv_rows(ug_ref, wgv, bgv, r0, ci == 0)
            cv, v2, v1, v0 = _conv_rows(uv_ref, wvv, bvv, r0, ci == 0)
            da = da_ref[pl.ds(r0, R), :].astype(f32)
            cdf = 0.5 * (1.0 + lax.erf(cg * INV_SQRT2))
            pdf = INV_SQRT_2PI * jnp.exp(-0.5 * cg * cg)
            dg = da * cv * (cdf + cg * pdf)
            dv = da * (cg * cdf)
            dcg[pl.ds(r0, R), :] = dg
            dcv[pl.ds(r0, R), :] = dv
            new = (acc[0] + _colsum8(dg * g2), acc[1] + _colsum8(dg * g1), acc[2] + _colsum8(dg * g0),
                   acc[3] + _colsum8(dg),
                   acc[4] + _colsum8(dv * v2), acc[5] + _colsum8(dv * v1), acc[6] + _colsum8(dv * v0),
                   acc[7] + _colsum8(dv))
            return new

        acc = lax.fori_loop(0, nchunk, fwd_step, (zero,) * 8)
        rows = lax.broadcasted_iota(jnp.int32, (SUBLANE, LANE), 0)

        def stats(parts):
            out = jnp.zeros((SUBLANE, LANE), f32)
            for k, pt in enumerate(parts):
                out = jnp.where(rows == k, jnp.sum(pt, axis=0, keepdims=True), out)
            return out

        sg_ref[...] = stats(acc[0:4])
        sv_ref[...] = stats(acc[4:8])

        def du_rows(dc, w, r0, last):
            cur = dc[pl.ds(r0, R), :]
            nxt = dc[pl.ds(pl.multiple_of(jnp.minimum(r0 + R, S - SUBLANE), SUBLANE), SUBLANE), :]
            nxt = jnp.where(last, 0.0, nxt)
            row = lax.broadcasted_iota(jnp.int32, (R, LANE), 0)
            y1 = jnp.where(row >= R - 1, _tile8(pltpu.roll(nxt, SUBLANE - 1, 0), R), pltpu.roll(cur, R - 1, 0))
            y2 = jnp.where(row >= R - 2, _tile8(pltpu.roll(nxt, SUBLANE - 2, 0), R), pltpu.roll(cur, R - 2, 0))
            return w[2:3] * cur + w[1:2] * y1 + w[0:1] * y2

        def bwd_step(ci, carry):
            r0 = pl.multiple_of(ci * R, R)
            last = ci == nchunk - 1
            dug_ref[pl.ds(r0, R), :] = du_rows(dcg, wgv, r0, last).astype(bf16)
            duv_ref[pl.ds(r0, R), :] = du_rows(dcv, wvv, r0, last).astype(bf16)
            return carry

        lax.fori_loop(0, nchunk, bwd_step, 0)

    col = pl.BlockSpec((S, LANE), lambda j: (0, j))
    w3 = pl.BlockSpec((3, LANE), lambda j: (0, j))
    b1 = pl.BlockSpec((1, LANE), lambda j: (0, j))
    st = pl.BlockSpec((SUBLANE, LANE), lambda j: (0, j))
    return pl.pallas_call(
        body,
        grid=(F // LANE,),
        in_specs=[col, col, col, w3, w3, b1, b1],
        out_specs=[col, col, st, st],
        out_shape=[SDS((S, F), bf16), SDS((S, F), bf16), SDS((SUBLANE, F), f32), SDS((SUBLANE, F), f32)],
        scratch_shapes=[pltpu.VMEM((S, LANE), f32), pltpu.VMEM((S, LANE), f32)],
        compiler_params=_cparams(("parallel",), VMEM_BIG),
        name="conv_bwd",
    )(ug, uv, dact, wg, wv, bg, bv)


def _adam_math(w, g, m, v):
    m = ADAM_B1 * m + (1.0 - ADAM_B1) * g
    v = ADAM_B2 * v + (1.0 - ADAM_B2) * (g * g)
    m_hat = m / (1.0 - ADAM_B1 ** ADAM_STEP)
    v_hat = v / (1.0 - ADAM_B2 ** ADAM_STEP)
    delta = -ADAM_LR * (m_hat / (jnp.sqrt(v_hat) + ADAM_EPS) + ADAM_WD * w)
    return delta, m, v


def _adamw(w, m, v, g, name):
    R, C = w.shape
    parts = g.ndim == 3
    tr = R
    for t in (256, 128, 64, 32, 16):
        if R % t == 0 and R > t:
            tr = t
            break

    def body(w_ref, m_ref, v_ref, g_ref, go_ref, d_ref, mo_ref, vo_ref):
        if parts:
            gv = ((g_ref[0].astype(f32) + g_ref[1].astype(f32)) + g_ref[2].astype(f32)) + g_ref[3].astype(f32)
        else:
            gv = g_ref[...]
        go_ref[...] = gv
        d, mn, vn = _adam_math(w_ref[...], gv, m_ref[...], v_ref[...])
        d_ref[...] = d
        mo_ref[...] = mn
        vo_ref[...] = vn

    row = pl.BlockSpec((tr, C), lambda i: (i, 0))
    gspec = pl.BlockSpec((4, tr, C), lambda i: (0, i, 0)) if parts else row
    return pl.pallas_call(
        body,
        grid=(R // tr,),
        in_specs=[row, row, row, gspec],
        out_specs=[row] * 4,
        out_shape=[SDS((R, C), f32)] * 4,
        compiler_params=_cparams(("parallel",)),
        name=name,
    )(w, m, v, g)


def _sum8(parts, name):
    _, _, R, C = parts.shape

    def body(p_ref, o_ref):
        acc = p_ref[0, 0]
        for c in range(2):
            for k in range(4):
                if c or k:
                    acc = acc + p_ref[c, k]
        o_ref[...] = acc

    return pl.pallas_call(body, out_shape=SDS((R, C), f32), name=name)(parts)


def _pair_add(by_core, b, name):
    _, K, R, C = by_core.shape
    tr = R // 2 if R % 32 == 0 else R

    def body(c_ref, a_ref, b_ref, o_ref):
        o_ref[...] = (a_ref[0].astype(f32) + b_ref[...].astype(f32)).astype(bf16)

    blk = pl.BlockSpec((1, tr, C), lambda k, i, c: (k, i, 0))
    return pl.pallas_call(
        body,
        grid_spec=pltpu.PrefetchScalarGridSpec(
            num_scalar_prefetch=1,
            grid=(K, R // tr),
            in_specs=[pl.BlockSpec((1, 1, tr, C), lambda k, i, c: (c[0], k, i, 0)), blk],
            out_specs=blk,
        ),
        out_shape=SDS((K, R, C), bf16),
        compiler_params=_cparams(("parallel", "parallel")),
        name=name,
    )(lax.axis_index("c").astype(jnp.int32).reshape(1), by_core, b)


_ANY = pl.BlockSpec(memory_space=pl.ANY)


def _chip_copies(src_ref, out_ref, send_sems, recv_sems, gather):
    x, y, c = lax.axis_index("x"), lax.axis_index("y"), lax.axis_index("c")
    mine = 2 * x + y

    def piece(k):
        return src_ref if gather else src_ref.at[k]

    sends, recvs = [], []
    for j, (px, py) in enumerate([(1 - x, y), (x, 1 - y), (1 - x, 1 - y)]):
        sends.append(pltpu.make_async_remote_copy(
            src_ref=piece(2 * px + py), dst_ref=out_ref.at[mine], send_sem=send_sems.at[j],
            recv_sem=recv_sems.at[j], device_id=(px, py, c), device_id_type=MESH))
        recvs.append(pltpu.make_async_remote_copy(
            src_ref=piece(mine), dst_ref=out_ref.at[2 * px + py], send_sem=send_sems.at[j],
            recv_sem=recv_sems.at[j], device_id=(px, py, c), device_id_type=MESH))
    return sends, recvs


def _chip_start(src_ref, out_ref, send_sems, recv_sems, gather):
    for cp in _chip_copies(src_ref, out_ref, send_sems, recv_sems, gather)[0]:
        cp.start()


def _chip_finish(src_ref, out_ref, send_sems, recv_sems, gather):
    sends, recvs = _chip_copies(src_ref, out_ref, send_sems, recv_sems, gather)
    for cp in recvs:
        cp.wait_recv()
    for cp in sends:
        cp.wait_send()


def _chip_out_shape(src, gather):
    return SDS((4,) + tuple(src.shape if gather else src.shape[1:]), src.dtype)


_CHIP_SEMS = [pltpu.SemaphoreType.DMA((3,)), pltpu.SemaphoreType.DMA((3,))]


def _fill_own(out, src, gather):
    mine = 2 * lax.axis_index("x") + lax.axis_index("y")
    own = src if gather else lax.dynamic_index_in_dim(src, mine, axis=0, keepdims=False)
    return lax.dynamic_update_index_in_dim(out, own, mine, axis=0)


def _chip_comm(src, gather, name):
    def body(src_ref, out_ref, send_sems, recv_sems):
        _chip_start(src_ref, out_ref, send_sems, recv_sems, gather)
        _chip_finish(src_ref, out_ref, send_sems, recv_sems, gather)

    out = pl.pallas_call(
        body,
        in_specs=[_ANY],
        out_specs=_ANY,
        out_shape=_chip_out_shape(src, gather),
        scratch_shapes=list(_CHIP_SEMS),
        name=name,
    )(src)
    return _fill_own(out, src, gather)


def _core_gather(src, name):
    def body(src_ref, out_ref, send_sem, recv_sem):
        x, y, c = lax.axis_index("x"), lax.axis_index("y"), lax.axis_index("c")
        cp = pltpu.make_async_remote_copy(src_ref=src_ref, dst_ref=out_ref.at[c], send_sem=send_sem,
                                          recv_sem=recv_sem, device_id=(x, y, 1 - c), device_id_type=MESH)
        cp.start()
        pltpu.make_async_remote_copy(src_ref=src_ref, dst_ref=out_ref.at[1 - c], send_sem=send_sem,
                                     recv_sem=recv_sem, device_id=(x, y, 1 - c), device_id_type=MESH).wait_recv()
        cp.wait_send()

    out = pl.pallas_call(
        body,
        in_specs=[_ANY],
        out_specs=_ANY,
        out_shape=SDS((2,) + tuple(src.shape), src.dtype),
        scratch_shapes=[pltpu.SemaphoreType.DMA, pltpu.SemaphoreType.DMA],
        name=name,
    )(src)
    return lax.dynamic_update_index_in_dim(out, src, lax.axis_index("c"), axis=0)


def _core_swap(src, name):
    def body(src_ref, out_ref, send_sem, recv_sem):
        x, y, c = lax.axis_index("x"), lax.axis_index("y"), lax.axis_index("c")
        cp = pltpu.make_async_remote_copy(src_ref=src_ref.at[1 - c], dst_ref=out_ref, send_sem=send_sem,
                                          recv_sem=recv_sem, device_id=(x, y, 1 - c), device_id_type=MESH)
        cp.start()
        cp.wait()

    return pl.pallas_call(
        body,
        in_specs=[_ANY],
        out_specs=_ANY,
        out_shape=SDS(tuple(src.shape[1:]), src.dtype),
        scratch_shapes=[pltpu.SemaphoreType.DMA, pltpu.SemaphoreType.DMA],
        name=name,
    )(src)


def _all_gather(src, tag):
    by_chip = _chip_comm(src, True, tag + "_chips")
    both = _core_gather(by_chip, tag + "_cores")
    return jnp.swapaxes(both, 0, 1).reshape((8,) + tuple(src.shape))


_PACK_A = (("w_in", (1024, 1088)),)
_PACK_B = (("w_ba", (512, 128)), ("w_bh", (512, 128)), ("w_out", (128, 1024)), ("w_up", (1024, 704)),
           ("w_down", (352, 1024)))
_PACK_SIZES = _PACK_A + _PACK_B


def _slab_rows(sizes):
    return sum(r * c for _, (r, c) in sizes) // D_MODEL


def _pack_rows(d, sizes):
    n = d[sizes[0][0]].shape[0]
    return jnp.concatenate([d[k].reshape(n, -1, D_MODEL) for k, _ in sizes], axis=1)


def _unpack_rows(slab, sizes):
    n = slab.shape[0]
    out, lo = {}, 0
    for key, (r, c) in sizes:
        rows = r * c // D_MODEL
        out[key] = slab[:, lo:lo + rows].reshape(n, r, c)
        lo += rows
    return out


def _by_core(gslab):
    return jnp.swapaxes(gslab.reshape((4, 2) + gslab.shape[1:]), 0, 1)


def _pair_sum(by_core, tag):
    return _pair_add(by_core, _core_swap(by_core, tag + "_cores"), tag + "_pair_add")


def _cols_to_full(t):
    return jnp.swapaxes(t, 0, 1).reshape(t.shape[1], -1)


def _full_to_cols(t):
    K = t.shape[0]
    return jnp.swapaxes(t.reshape(K, 8, -1), 0, 1)


_SMALL = (("pre_mix_norm", (1, 1024)), ("rel_bias", (32, 24)), ("hgrn_lb_raw", (2, 512)), ("hgrn_norm", (1, 128)),
          ("post_mix_norm", (1, 1024)), ("pre_ffn_norm", (1, 1024)), ("conv_b", (1, 5632)),
          ("post_ffn_norm", (1, 1024)))
_SMALL_ROWS = 96
_CONVW_ROWS = 136


_SMALL_USED = sum(r * c for _, (r, c) in _SMALL)


def _pack_small(d, extra=None):
    flat = jnp.concatenate([d[k].reshape(-1) for k, _ in _SMALL] + ([] if extra is None else [extra.reshape(-1)]))
    flat = jnp.pad(flat, (0, _SMALL_ROWS * LANE - flat.shape[0]))
    return flat.reshape(_SMALL_ROWS, LANE)


def _unpack_small(p):
    flat = p.reshape(-1)
    out, lo = {}, 0
    for k, shp in _SMALL:
        n = shp[0] * shp[1]
        out[k] = flat[lo:lo + n].reshape(shp)
        lo += n
    return out


def _local_step(x, tgt, WA, P, plan):
    S = x.shape[0]
    W = dict(WA)
    lb = _lb_fwd(P["hgrn_lb_raw"])
    hs = _prep(x, P["pre_mix_norm"])
    h1 = hs[0]
    consts = [_bias_consts(d) for d in DILATIONS]
    qkv, obuf, lbuf, biases = [], [], [], []
    for g, d in enumerate(DILATIONS):
        qkv_g = _mm(hs[g], W["w_qkv"][g], "nn", bf16, f"proj_qkv{g}")
        tab_t = P["rel_bias"][:, 8 * g:8 * g + 8].T
        bias_g = _bias_build(tab_t, consts[g][0], consts[g][1], f"bias_build{g}").reshape(8, ATTN_BLOCK, 2 * ATTN_BLOCK)
        o_g, l_g = _attn_fwd(qkv_g, bias_g, (S // d) // ATTN_BLOCK, f"attn_fwd{g}")
        qkv.append(qkv_g)
        biases.append(bias_g)
        lbuf.append(l_g)
        obuf.append(o_g)
    y_attn, y_attn_b, w0, w1, w2 = _attn_merge(obuf[0], obuf[1], obuf[2], lbuf[0], lbuf[1], lbuf[2])
    hg = _mm(h1, W["w_hg"], "nn", f32, "proj_hg")
    gc = _mm(h1, W["w_gate"], "nn", bf16, "proj_gate")
    y_hgrn, o_raw, ck, got = _hgrn_fwd(hg, lb, P["hgrn_norm"], plan.fwd_ride())
    W.update(plan.weights(got))
    a = _mm(y_attn_b, W["w_ba"], "nn", bf16, "branch_attn")
    b = _mm(y_hgrn, W["w_bh"], "nn", bf16, "branch_hgrn")
    merged = _gate_fwd(a, b, gc)
    mo = _mm(merged, W["w_out"], "nn", f32, "out_proj")
    x1, h2 = _mid_fwd(x, mo, P["post_mix_norm"], P["pre_ffn_norm"])
    ug = _mm(h2, W["w_up_g"], "nn", bf16, "up_gate")
    uv = _mm(h2, W["w_up_v"], "nn", bf16, "up_val")
    cw_g, cw_v = P["conv_w"][:, :D_FF], P["conv_w"][:, D_FF:]
    cb_g, cb_v = P["conv_b"][:, :D_FF], P["conv_b"][:, D_FF:]
    act = _conv_fwd(ug, uv, cw_g, cw_v, cb_g, cb_v)
    fo = _mm(act, W["w_down"], "nn", f32, "down_proj")
    loss, dy, dfo, g_post_ffn = _final(x1, fo, tgt, P["post_ffn_norm"])
    dact = _mm(dfo, W["w_down"], "nt", bf16, "d_act")
    gW_down = _mm(act, dfo, "tn", f32, "gw_down")
    dug, duv, st_g, st_v = _conv_bwd(ug, uv, dact, cw_g, cw_v, cb_g, cb_v)
    dh2 = _mm(dug, W["w_up_g"], "nt", f32, "dh2_gate")
    dh2 = _mm(duv, W["w_up_v"], "nt", f32, "dh2_val", acc=dh2)
    gW_up_g = _mm(h2, dug, "tn", f32, "gw_up_gate")
    gW_up_v = _mm(h2, duv, "tn", f32, "gw_up_val")
    dx1, dmo, g_pre_ffn, g_post_mix = _mid_bwd(dy, dh2, x1, mo, P["pre_ffn_norm"], P["post_mix_norm"])
    dmerged = _mm(dmo, W["w_out"], "nt", bf16, "d_merged")
    gW_out = _mm(merged, dmo, "tn", f32, "gw_out")
    da, db, dgc = _gate_bwd(dmerged, a, b, gc)
    dyattn = _mm(da, W["w_ba"], "nt", f32, "d_yattn")
    gW_ba = _mm(y_attn_b, da, "tn", f32, "gw_ba")
    dyhgrn = _mm(db, W["w_bh"], "nt", f32, "d_yhgrn")
    gW_bh = _mm(y_hgrn, db, "tn", f32, "gw_bh")
    big_b = dict(w_ba=gW_ba, w_bh=gW_bh, w_out=gW_out, w_up=[gW_up_g, gW_up_v], w_down=gW_down)
    dq_h, df_h, dv_h, dog_h, glb8, gnw8, got_b = _hgrn_bwd(hg, o_raw, dyhgrn, ck, lb, P["hgrn_norm"],
                                                          plan.bwd_ride(big_b))
    dhg = [dq_h, df_h, dv_h, dog_h]
    g_lb_raw = _lb_bwd(P["hgrn_lb_raw"], glb8[0:1])
    gn = gnw8[0:1]
    g_hgrn_norm = (gn[:, 0:128] + gn[:, 128:256]) + (gn[:, 256:384] + gn[:, 384:512])
    dos = _attn_merge_bwd(dyattn, y_attn, w0, w1, w2)
    dh_parts, gW_qkv, g_rel = [], [], []
    for g, d in enumerate(DILATIONS):
        dq, dk, dv, dbias = _attn_bwd(qkv[g], biases[g], dos[g], dos[3 + g], lbuf[g], (S // d) // ATTN_BLOCK,
                                      f"attn_bwd{g}")
        dqkv = [dq, dk, dv]
        gW_qkv.append(_mm(hs[g], dqkv, "tn", f32, f"gw_qkv{g}"))
        dh_g = _mm(dqkv, W["w_qkv"][g], "nt", f32, f"dh1_qkv{g}")
        dh_parts.append(dh_g)
        g_rel.append(_bias_grad(dbias.reshape(8, -1), consts[g][0], f"bias_grad{g}"))
    dh_main = _mm(dhg, W["w_hg"], "nt", f32, "dh1_hg", acc=dh_parts[0])
    dh_main = _mm(dgc, W["w_gate"], "nt", f32, "dh1_gate", acc=dh_main)
    gW_hg = _mm(h1, dhg, "tn", f32, "gw_hg")
    gW_gate = _mm(h1, dgc, "tn", f32, "gw_gate")
    grad_x, g_pre_mix = _first_bwd(x, dx1, _dh_sum(dh_main, dh_parts[1], dh_parts[2]), P["pre_mix_norm"])

    gW_in = gW_qkv + [gW_hg, gW_gate]
    g_conv_w = jnp.concatenate([st_g[0:3], st_v[0:3]], axis=1)
    g_conv_b = jnp.concatenate([st_g[3:4], st_v[3:4]], axis=1)
    small = dict(pre_mix_norm=g_pre_mix, rel_bias=jnp.concatenate(g_rel, axis=1), hgrn_lb_raw=g_lb_raw,
                 hgrn_norm=g_hgrn_norm, post_mix_norm=g_post_mix, pre_ffn_norm=g_pre_ffn, conv_b=g_conv_b,
                 post_ffn_norm=g_post_ffn, conv_w=g_conv_w)
    return loss, grad_x, gW_in, big_b, got_b, small


def _weights_a(both):
    w_in = jnp.transpose(both, (2, 1, 0, 3)).reshape(D_MODEL, -1)
    return dict(
        w_qkv=[w_in[:, g * QKV_G:(g + 1) * QKV_G] for g in range(N_GROUPS)],
        w_hg=w_in[:, 3 * QKV_G:3 * QKV_G + 4 * HGRN_W],
        w_gate=w_in[:, 3 * QKV_G + 4 * HGRN_W:],
    )


def _weights_b(slabs):
    sh = _unpack_rows(slabs, _PACK_B)
    w_up = _cols_to_full(sh["w_up"])
    return dict(
        w_ba=_cols_to_full(sh["w_ba"]),
        w_bh=_cols_to_full(sh["w_bh"]),
        w_out=sh["w_out"].reshape(D_MODEL, D_MODEL),
        w_up_g=w_up[:, :D_FF],
        w_up_v=w_up[:, D_FF:],
        w_down=sh["w_down"].reshape(D_FF, D_MODEL),
    )


def _dest_cols(sections, width):
    out = []
    for j in range(8):
        lo, hi, off, pieces = j * width, (j + 1) * width, 0, []
        for s in sections:
            a, b = max(lo, off), min(hi, off + s.shape[1])
            if a < b:
                pieces.append(s[:, a - off:b - off])
            off += s.shape[1]
        out.append(pieces[0] if len(pieces) == 1 else jnp.concatenate(pieces, axis=1))
    return out


def _grad_blocks_a(sections):
    cols = _dest_cols(sections, 1088)
    return jnp.stack([jnp.stack([cols[2 * k + c].astype(bf16) for k in range(4)]) for c in range(2)])


def _grad_slab_b(g):
    shards = dict(w_ba=_full_to_cols(g["w_ba"]), w_bh=_full_to_cols(g["w_bh"]), w_out=g["w_out"].reshape(8, 128, D_MODEL),
                  w_up=jnp.stack(_dest_cols(g["w_up"], 704)), w_down=g["w_down"].reshape(8, 352, D_MODEL))
    return _pack_rows({k: v.astype(bf16) for k, v in shards.items()}, _PACK_B)


class _SlabB:
    def __init__(self, slab):
        self.slab = slab
        self.chip_sum = None

    def fwd_ride(self):
        return (self.slab, True)

    def weights(self, got):
        both = _core_gather(_fill_own(got, self.slab, True), "ag_b_cores")
        return _weights_b(jnp.swapaxes(both, 0, 1).reshape((8,) + tuple(self.slab.shape)))

    def bwd_ride(self, grads):
        self.chip_sum = _pair_sum(_by_core(_grad_slab_b(grads)), "rs_b")
        return (self.chip_sum, False)

    def parts(self, got_b):
        return _unpack_rows(_fill_own(got_b, self.chip_sum, False), _PACK_B)


def kernel(x, pre_mix_norm, w_in, rel_bias, hgrn_lb_raw, hgrn_norm, w_branch_attn, w_branch_hgrn, w_out, post_mix_norm, pre_ffn_norm, w_up, conv_w, conv_b, w_down, post_ffn_norm, loss_target, m_pre_mix_norm, m_w_in, m_rel_bias, m_hgrn_lb_raw, m_hgrn_norm, m_w_branch_attn, m_w_branch_hgrn, m_w_out, m_post_mix_norm, m_pre_ffn_norm, m_w_up, m_conv_w, m_conv_b, m_w_down, m_post_ffn_norm, v_pre_mix_norm, v_w_in, v_rel_bias, v_hgrn_lb_raw, v_hgrn_norm, v_w_branch_attn, v_w_branch_hgrn, v_w_out, v_post_mix_norm, v_pre_ffn_norm, v_w_up, v_conv_w, v_conv_b, v_w_down, v_post_ffn_norm):
    ci = lax.axis_index("c")
    dev = 4 * lax.axis_index("x") + 2 * lax.axis_index("y") + ci
    wts = dict(w_in=w_in[0], w_ba=w_branch_attn[0], w_bh=w_branch_hgrn[0], w_out=w_out[0], w_up=w_up[0],
               w_down=w_down[0])
    mom = dict(w_in=m_w_in[0], w_ba=m_w_branch_attn[0], w_bh=m_w_branch_hgrn[0], w_out=m_w_out[0], w_up=m_w_up[0],
               w_down=m_w_down[0])
    var = dict(w_in=v_w_in[0], w_ba=v_w_branch_attn[0], w_bh=v_w_branch_hgrn[0], w_out=v_w_out[0], w_up=v_w_up[0],
               w_down=v_w_down[0])
    small_w = dict(pre_mix_norm=pre_mix_norm, rel_bias=rel_bias, hgrn_lb_raw=hgrn_lb_raw, hgrn_norm=hgrn_norm,
                   post_mix_norm=post_mix_norm, pre_ffn_norm=pre_ffn_norm, conv_b=conv_b, post_ffn_norm=post_ffn_norm)
    small_m = dict(pre_mix_norm=m_pre_mix_norm, rel_bias=m_rel_bias, hgrn_lb_raw=m_hgrn_lb_raw, hgrn_norm=m_hgrn_norm,
                   post_mix_norm=m_post_mix_norm, pre_ffn_norm=m_pre_ffn_norm, conv_b=m_conv_b,
                   post_ffn_norm=m_post_ffn_norm)
    small_v = dict(pre_mix_norm=v_pre_mix_norm, rel_bias=v_rel_bias, hgrn_lb_raw=v_hgrn_lb_raw, hgrn_norm=v_hgrn_norm,
                   post_mix_norm=v_post_mix_norm, pre_ffn_norm=v_pre_ffn_norm, conv_b=v_conv_b,
                   post_ffn_norm=v_post_ffn_norm)

    slab_a = wts["w_in"].astype(bf16)
    WA = _weights_a(_core_gather(_chip_comm(slab_a, True, "ag_a_chips"), "ag_a_cores"))
    plan = _SlabB(_pack_rows({k: wts[k].astype(bf16)[None] for k, _ in _PACK_B}, _PACK_B)[0])
    cw_pad = jnp.pad(conv_w[0], ((0, SUBLANE - 3), (0, 768 - 704)))
    conv_w_full = _cols_to_full(_all_gather(cw_pad, "ag_convw")[:, 0:3, 0:704])
    P = dict(small_w)
    P["conv_w"] = conv_w_full

    loss8, grad_x, gW_in, _, got_b, small = _local_step(x[0], loss_target[0], WA, P, plan)

    chip_sum_a = _pair_sum(_grad_blocks_a(gW_in), "rs_a")
    parts = dict(w_in=_chip_comm(chip_sum_a, False, "rs_a_chips"))
    parts.update(plan.parts(got_b))
    outs_big = {}
    for k, _ in _PACK_SIZES:
        outs_big[k] = _adamw(wts[k], mom[k], var[k], parts[k], "adamw_" + k)

    spack = jnp.concatenate([_pack_small(small, loss8[0, 0:1]),
                             jnp.pad(small["conv_w"].reshape(-1, LANE), ((0, _CONVW_ROWS - 132), (0, 0)))], axis=0)
    allp = _core_gather(_chip_comm(spack, True, "ag_small_chips"), "ag_small_cores")
    ssum = _sum8(allp, "small_sum")
    gs = ssum[:_SMALL_ROWS]
    loss = ssum[_SMALL_USED // LANE, _SMALL_USED % LANE]
    res_small = _adamw(_pack_small(small_w), _pack_small(small_m), _pack_small(small_v), gs, "adamw_small")
    sm = [_unpack_small(t) for t in res_small]
    g_cw_full = ssum[_SMALL_ROWS:_SMALL_ROWS + 132].reshape(3, 2 * D_FF)
    g_cw = lax.dynamic_slice_in_dim(g_cw_full, dev * 704, 704, axis=1)
    res_cw = _adamw(conv_w[0], m_conv_w[0], v_conv_w[0], g_cw, "adamw_conv_w")

    def pick(i):
        def big_(k):
            return outs_big[k][i][None]
        return [sm[i]["pre_mix_norm"], big_("w_in"), sm[i]["rel_bias"], sm[i]["hgrn_lb_raw"], sm[i]["hgrn_norm"],
                big_("w_ba"), big_("w_bh"), big_("w_out"), sm[i]["post_mix_norm"], sm[i]["pre_ffn_norm"],
                big_("w_up"), res_cw[i][None], sm[i]["conv_b"], big_("w_down"), sm[i]["post_ffn_norm"]]

    return (loss, grad_x[None], *pick(0), *pick(1), *pick(2), *pick(3))
```

```python
import functools
import math

import jax
import jax.numpy as jnp
from jax import lax
from jax.experimental import pallas as pl
from jax.experimental.pallas import tpu as pltpu

f32 = jnp.float32
bf16 = jnp.bfloat16
SDS = jax.ShapeDtypeStruct
HIGHEST = lax.Precision.HIGHEST
MESH = pl.DeviceIdType.MESH

NN = (((1,), (0,)), ((), ()))
NT = (((1,), (1,)), ((), ()))
TN = (((0,), (0,)), ((), ()))

D_MODEL = 1024
N_GROUPS = 3
DILATIONS = (1, 4, 16)
HEAD_DIM = 64
ATTN_BLOCK = 128
QKV_G = 1536
ATTN_OUT = 512
HGRN_W = 512
HGRN_CHUNK = 32
D_FF = 2816
NUM_BUCKETS = 32
MAX_EXACT = 16
MAX_DISTANCE = 2048
NEG_INF = -1e30
EPS = 1e-6
LANE = 128
SUBLANE = 8
VMEM_BIG = 48 * 1024 * 1024
MM_ROWS = 512
MM_OUT_BYTES = 8 * 1024 * 1024

ADAM_LR, ADAM_B1, ADAM_B2, ADAM_EPS, ADAM_WD, ADAM_STEP = 0.001, 0.9, 0.999, 1e-08, 0.01, 10


def _pick(n, pref):
    t = pref
    while t >= LANE:
        if n % t == 0:
            return t
        t //= 2
    return n


def _cparams(sem=None, vmem=None):
    kw = {}
    if sem is not None:
        kw["dimension_semantics"] = sem
    if vmem is not None:
        kw["vmem_limit_bytes"] = vmem
    return pltpu.CompilerParams(**kw)


def _sigmoid(x):
    return jax.nn.sigmoid(x)


def _colsum8(x):
    return x.reshape(x.shape[0] // SUBLANE, SUBLANE, x.shape[1]).sum(axis=0)


def _mm(a, b, mode, out_dtype, name, acc=None):
    dims = {"nn": NN, "nt": NT, "tn": TN}[mode]
    has_acc = acc is not None
    if mode == "tn":
        assert not has_acc
        bs = list(b) if isinstance(b, (list, tuple)) else [b]
        K, M = a.shape
        widths = [t.shape[1] for t in bs]
        N = sum(widths)
        tmm = M if M * N * 4 <= MM_OUT_BYTES else M // 2
        ts = _pick(K, MM_ROWS)
        nk = K // ts

        def body_tn(a_ref, *refs):
            o_ref = refs[-1]
            k = pl.program_id(1)
            av = a_ref[...]
            lo = 0
            for b_ref, w in zip(refs[:-1], widths):
                part = lax.dot_general(av, b_ref[...], dims, preferred_element_type=f32)
                cols = slice(lo, lo + w)
                lo += w

                @pl.when(k == 0)
                def _(part=part, cols=cols):
                    o_ref[:, cols] = part

                @pl.when(k > 0)
                def _(part=part, cols=cols):
                    o_ref[:, cols] += part

        return pl.pallas_call(
            body_tn,
            grid=(M // tmm, nk),
            in_specs=[pl.BlockSpec((ts, tmm), lambda i, k: (k, i))]
            + [pl.BlockSpec((ts, w), lambda i, k: (k, 0)) for w in widths],
            out_specs=pl.BlockSpec((tmm, N), lambda i, k: (i, 0)),
            out_shape=SDS((M, N), out_dtype),
            compiler_params=_cparams(("parallel", "arbitrary"), VMEM_BIG),
            name=name,
        )(a, *bs)

    parts = list(a) if isinstance(a, (list, tuple)) else [a]
    assert mode == "nt" or len(parts) == 1
    widths = [t.shape[1] for t in parts]
    M = parts[0].shape[0]
    N = b.shape[1] if mode == "nn" else b.shape[0]
    tm = _pick(M, MM_ROWS)
    npart = len(parts)

    def body(*refs):
        a_refs, b_ref = refs[:npart], refs[npart]
        c_ref = refs[npart + 1] if has_acc else None
        o_ref = refs[-1]
        if npart == 1:
            part = lax.dot_general(a_refs[0][...], b_ref[...], dims, preferred_element_type=f32)
        else:
            part, lo = None, 0
            for a_ref, w in zip(a_refs, widths):
                t = lax.dot_general(a_ref[...], b_ref[:, lo:lo + w], dims, preferred_element_type=f32)
                part = t if part is None else part + t
                lo += w
        if has_acc:
            part = part + c_ref[...]
        o_ref[...] = part.astype(out_dtype)

    specs = [pl.BlockSpec((tm, w), lambda i: (i, 0)) for w in widths] + [pl.BlockSpec(b.shape, lambda i: (0, 0))]
    args = parts + [b]
    aliases = {}
    if has_acc:
        specs.append(pl.BlockSpec((tm, N), lambda i: (i, 0)))
        args.append(acc)
        aliases = {npart + 1: 0}
    return pl.pallas_call(
        body,
        grid=(M // tm,),
        in_specs=specs,
        out_specs=pl.BlockSpec((tm, N), lambda i: (i, 0)),
        out_shape=SDS((M, N), out_dtype),
        input_output_aliases=aliases,
        compiler_params=_cparams(("parallel",), VMEM_BIG),
        name=name,
    )(*args)


PERM_ROWS = 1024


def _perm_spec(d, cols=LANE):
    return pl.BlockSpec((d, PERM_ROWS // d, cols), lambda i, j: (0, i, j))


def _to_natural(src_ref, dst_ref, d):
    n = src_ref.shape[1]
    for r in range(d):
        dst_ref[pl.ds(r, n, stride=d), :] = src_ref[r]


def _prep(x, w):
    S, D = x.shape
    R = PERM_ROWS
    nc = D // LANE

    def body(*refs):
        x_refs, w_ref = refs[:nc], refs[nc]
        h_ref, h4_ref, h16_ref, rs = refs[nc + 1:]
        ssq = None
        for xr in x_refs:
            v = xr[...]
            t = jnp.sum(v * v, axis=-1, keepdims=True)
            ssq = t if ssq is None else ssq + t
        rinv = lax.rsqrt(ssq * (1.0 / D) + EPS)
        rs[...] = jnp.broadcast_to(rinv, (R, LANE))
        for j, xr in enumerate(x_refs):
            cols = slice(j * LANE, (j + 1) * LANE)
            wj = w_ref[:, cols]
            h_ref[:, cols] = ((xr[...] * rinv) * wj).astype(bf16)
            for d, o_ref in ((4, h4_ref), (16, h16_ref)):
                n = R // d
                for r in range(d):
                    rows = pl.ds(r, n, stride=d)
                    o_ref[r, :, cols] = ((xr[rows, :] * rs[rows, :]) * wj).astype(bf16)

    col = lambda j: pl.BlockSpec((R, LANE), lambda i, j=j: (i, j))
    h, h4, h16 = pl.pallas_call(
        body,
        grid=(S // R,),
        in_specs=[col(j) for j in range(nc)] + [pl.BlockSpec((1, D), lambda i: (0, 0))],
        out_specs=[pl.BlockSpec((R, D), lambda i: (i, 0)), pl.BlockSpec((4, R // 4, D), lambda i: (0, i, 0)),
                   pl.BlockSpec((16, R // 16, D), lambda i: (0, i, 0))],
        out_shape=[SDS((S, D), bf16), SDS((4, S // 4, D), bf16), SDS((16, S // 16, D), bf16)],
        scratch_shapes=[pltpu.VMEM((R, LANE), f32)],
        compiler_params=_cparams(("parallel",), VMEM_BIG),
        name="prep_norm_perm",
    )(*([x] * nc), w)
    return [h, h4.reshape(S, D), h16.reshape(S, D)]


def _dh_sum(a, b, c):
    S, D = a.shape
    R = PERM_ROWS

    def body(a_ref, b_ref, c_ref, o_ref, sb, sc):
        _to_natural(b_ref, sb, 4)
        _to_natural(c_ref, sc, 16)
        o_ref[...] = (a_ref[...] + sb[...]) + sc[...]

    nat = pl.BlockSpec((R, LANE), lambda i, j: (i, j))
    return pl.pallas_call(
        body,
        grid=(S // R, D // LANE),
        in_specs=[nat, _perm_spec(4), _perm_spec(16)],
        out_specs=nat,
        out_shape=SDS((S, D), f32),
        scratch_shapes=[pltpu.VMEM((R, LANE), f32)] * 2,
        compiler_params=_cparams(("parallel", "parallel")),
        name="dh_sum",
    )(a, b.reshape(4, S // 4, D), c.reshape(16, S // 16, D))


def _rms_parts(xv):
    r = lax.rsqrt(jnp.mean(xv * xv, axis=-1, keepdims=True) + EPS)
    return r, xv * r


def _rms_bwd(xhat, r, w, dy):
    dyw = dy * w
    return r * (dyw - xhat * jnp.mean(dyw * xhat, axis=-1, keepdims=True))


def _mid_fwd(x, mo, w_pm, w_pf):
    S, D = x.shape
    tm = _pick(S, 512)

    def body(x_ref, mo_ref, wpm_ref, wpf_ref, x1_ref, h2_ref):
        _, moh = _rms_parts(mo_ref[...])
        x1 = x_ref[...] + moh * wpm_ref[...]
        x1_ref[...] = x1
        _, x1h = _rms_parts(x1)
        h2_ref[...] = (x1h * wpf_ref[...]).astype(bf16)

    row = pl.BlockSpec((tm, D), lambda i: (i, 0))
    vec = pl.BlockSpec((1, D), lambda i: (0, 0))
    return pl.pallas_call(
        body,
        grid=(S // tm,),
        in_specs=[row, row, vec, vec],
        out_specs=[row, row],
        out_shape=[SDS((S, D), f32), SDS((S, D), bf16)],
        compiler_params=_cparams(("parallel",)),
        name="mid_fwd",
    )(x, mo, w_pm, w_pf)


def _final(x1, fo, tgt, w_pfn):
    S, D = x1.shape
    tm = _pick(S, 512)
    nt = S // tm

    def body(x1_ref, fo_ref, t_ref, w_ref, loss_ref, dy_ref, dfo_ref, gw_ref, lacc, gacc):
        i = pl.program_id(0)

        @pl.when(i == 0)
        def _():
            lacc[...] = jnp.zeros_like(lacc)
            gacc[...] = jnp.zeros_like(gacc)

        w = w_ref[...]
        r, foh = _rms_parts(fo_ref[...])
        y = x1_ref[...] + foh * w
        err = y - t_ref[...]
        lacc[...] += _colsum8(err * err)
        dy = err * (1.0 / D)
        dy_ref[...] = dy
        gacc[...] += _colsum8(dy * foh)
        dfo_ref[...] = _rms_bwd(foh, r, w, dy).astype(bf16)

        @pl.when(i == nt - 1)
        def _():
            loss_ref[...] = jnp.full((SUBLANE, LANE), 0.5 / D, f32) * jnp.sum(lacc[...])
            gw_ref[...] = jnp.sum(gacc[...], axis=0, keepdims=True)

    row = pl.BlockSpec((tm, D), lambda i: (i, 0))
    vec = pl.BlockSpec((1, D), lambda i: (0, 0))
    return pl.pallas_call(
        body,
        grid=(nt,),
        in_specs=[row, row, row, vec],
        out_specs=[pl.BlockSpec((SUBLANE, LANE), lambda i: (0, 0)), row, row, vec],
        out_shape=[SDS((SUBLANE, LANE), f32), SDS((S, D), f32), SDS((S, D), bf16), SDS((1, D), f32)],
        scratch_shapes=[pltpu.VMEM((SUBLANE, D), f32), pltpu.VMEM((SUBLANE, D), f32)],
        compiler_params=_cparams(("arbitrary",)),
        name="final_loss",
    )(x1, fo, tgt, w_pfn)


def _mid_bwd(dy, dh2, x1, mo, w_pf, w_pm):
    S, D = dy.shape
    tm = _pick(S, 512)
    nt = S // tm

    def body(dy_ref, dh2_ref, x1_ref, mo_ref, wpf_ref, wpm_ref, dx1_ref, dmo_ref, gpf_ref, gpm_ref, apf, apm):
        i = pl.program_id(0)

        @pl.when(i == 0)
        def _():
            apf[...] = jnp.zeros_like(apf)
            apm[...] = jnp.zeros_like(apm)

        r1, x1h = _rms_parts(x1_ref[...])
        dh2 = dh2_ref[...]
        apf[...] += _colsum8(dh2 * x1h)
        dx1 = dy_ref[...] + _rms_bwd(x1h, r1, wpf_ref[...], dh2)
        dx1_ref[...] = dx1
        rm, moh = _rms_parts(mo_ref[...])
        apm[...] += _colsum8(dx1 * moh)
        dmo_ref[...] = _rms_bwd(moh, rm, wpm_ref[...], dx1).astype(bf16)

        @pl.when(i == nt - 1)
        def _():
            gpf_ref[...] = jnp.sum(apf[...], axis=0, keepdims=True)
            gpm_ref[...] = jnp.sum(apm[...], axis=0, keepdims=True)

    row = pl.BlockSpec((tm, D), lambda i: (i, 0))
    vec = pl.BlockSpec((1, D), lambda i: (0, 0))
    return pl.pallas_call(
        body,
        grid=(nt,),
        in_specs=[row, row, row, row, vec, vec],
        out_specs=[row, row, vec, vec],
        out_shape=[SDS((S, D), f32), SDS((S, D), bf16), SDS((1, D), f32), SDS((1, D), f32)],
        scratch_shapes=[pltpu.VMEM((SUBLANE, D), f32), pltpu.VMEM((SUBLANE, D), f32)],
        compiler_params=_cparams(("arbitrary",)),
        name="mid_bwd",
    )(dy, dh2, x1, mo, w_pf, w_pm)


def _first_bwd(x, dx1, dh, w_pre):
    S, D = x.shape
    tm = _pick(S, 512)
    nt = S // tm

    def body(x_ref, dx1_ref, a_ref, w_ref, gx_ref, gw_ref, acc):
        i = pl.program_id(0)

        @pl.when(i == 0)
        def _():
            acc[...] = jnp.zeros_like(acc)

        r, xh = _rms_parts(x_ref[...])
        dh = a_ref[...]
        acc[...] += _colsum8(dh * xh)
        gx_ref[...] = dx1_ref[...] + _rms_bwd(xh, r, w_ref[...], dh)

        @pl.when(i == nt - 1)
        def _():
            gw_ref[...] = jnp.sum(acc[...], axis=0, keepdims=True)

    row = pl.BlockSpec((tm, D), lambda i: (i, 0))
    vec = pl.BlockSpec((1, D), lambda i: (0, 0))
    return pl.pallas_call(
        body,
        grid=(nt,),
        in_specs=[row, row, row, vec],
        out_specs=[row, vec],
        out_shape=[SDS((S, D), f32), SDS((1, D), f32)],
        scratch_shapes=[pltpu.VMEM((SUBLANE, D), f32)],
        compiler_params=_cparams(("arbitrary",)),
        name="first_bwd",
    )(x, dx1, dh, w_pre)


def _t5_bucket(dist):
    n = jnp.maximum(dist, 0)
    nf = jnp.maximum(n, 1).astype(f32)
    large = MAX_EXACT + (jnp.log(nf / MAX_EXACT) / math.log(MAX_DISTANCE / MAX_EXACT)
                         * (NUM_BUCKETS - MAX_EXACT)).astype(jnp.int32)
    large = jnp.minimum(large, NUM_BUCKETS - 1)
    return jnp.where(n < MAX_EXACT, n, large)


def _bias_consts(d):
    blk = ATTN_BLOCK
    rel = jnp.arange(blk)[:, None] + blk - jnp.arange(2 * blk)[None, :]
    in_win = (rel >= 0) & (rel <= blk)
    bucket = _t5_bucket(rel * d).reshape(1, -1)
    onehot = (bucket == jnp.arange(NUM_BUCKETS)[:, None]).astype(f32)
    return onehot, in_win.astype(f32).reshape(1, -1)


def _bias_build(tab_t, onehot, maskf, name):
    H = tab_t.shape[0]

    def body(t_ref, oh_ref, m_ref, o_ref):
        b = jnp.dot(t_ref[...], oh_ref[...], precision=HIGHEST, preferred_element_type=f32)
        o_ref[...] = jnp.where(m_ref[...] > 0.5, b, NEG_INF)

    return pl.pallas_call(body, out_shape=SDS((H, onehot.shape[1]), f32), name=name)(tab_t, onehot, maskf)


def _bias_grad(dbias_flat, onehot, name):
    H = dbias_flat.shape[0]

    def body(g_ref, oh_ref, o_ref):
        o_ref[...] = lax.dot_general(oh_ref[...], g_ref[...], NT, precision=HIGHEST, preferred_element_type=f32)

    return pl.pallas_call(body, out_shape=SDS((NUM_BUCKETS, H), f32), name=name)(dbias_flat, onehot)


ATTN_TILE = 512
ATTN_SUB = ATTN_TILE // ATTN_BLOCK


def _qkv_specs(nt):
    tile = (ATTN_TILE, LANE)
    blk = (ATTN_BLOCK, LANE)
    cur = lambda off: (lambda h, t: (jnp.minimum(t, nt - 1), off + h))
    prev = lambda off: (lambda h, t: (jnp.maximum(jnp.minimum(t, nt - 1) * ATTN_SUB - 1, 0), off + h))
    return [pl.BlockSpec(tile, cur(0)), pl.BlockSpec(blk, prev(4)), pl.BlockSpec(tile, cur(4)),
            pl.BlockSpec(blk, prev(8)), pl.BlockSpec(tile, cur(8))]


def _head_masks():
    lane = lax.broadcasted_iota(jnp.int32, (ATTN_BLOCK, LANE), 1)
    return lane < HEAD_DIM


def _attn_fwd(qkv, bias, bps, name):
    S = qkv.shape[0]
    nt = S // ATTN_TILE
    scale = HEAD_DIM ** -0.5

    def body(q_ref, kp_ref, kc_ref, vp_ref, vc_ref, b_ref, o_ref, l_ref):
        t = pl.program_id(1)
        kk = jnp.concatenate([kp_ref[...], kc_ref[...]], axis=0)
        vv = jnp.concatenate([vp_ref[...], vc_ref[...]], axis=0)
        low = _head_masks()
        col = lax.broadcasted_iota(jnp.int32, (ATTN_BLOCK, 2 * ATTN_BLOCK), 1)
        for b in range(ATTN_SUB):
            lo = b * ATTN_BLOCK
            rows = slice(lo, lo + ATTN_BLOCK)
            keys = slice(lo, lo + 2 * ATTN_BLOCK)
            dead = jnp.logical_and((t * ATTN_SUB + b) % bps == 0, col < ATTN_BLOCK)
            q2 = q_ref[rows, :]
            kb, vb = kk[keys], vv[keys]
            outs, lses = [], []
            for h in range(2):
                hm = low if h == 0 else jnp.logical_not(low)
                qh = jnp.where(hm, q2, jnp.zeros_like(q2))
                s = lax.dot_general(qh, kb, NT, preferred_element_type=f32) * scale + b_ref[h]
                s = jnp.where(dead, NEG_INF, s)
                m = jnp.max(s, axis=-1, keepdims=True)
                p = jnp.exp(s - m)
                l = jnp.sum(p, axis=-1, keepdims=True)
                outs.append(jnp.dot(p.astype(bf16), vb, preferred_element_type=f32) / l)
                lses.append(m + jnp.log(l))
            o_ref[rows, :] = jnp.where(low, outs[0], outs[1])
            l_ref[rows, :] = jnp.where(low, lses[0], lses[1])

    tile = pl.BlockSpec((ATTN_TILE, LANE), lambda h, t: (t, h))
    return pl.pallas_call(
        body,
        grid=(4, nt),
        in_specs=_qkv_specs(nt) + [pl.BlockSpec((2, ATTN_BLOCK, 2 * ATTN_BLOCK), lambda h, t: (h, 0, 0))],
        out_specs=[tile, tile],
        out_shape=[SDS((S, ATTN_OUT), f32), SDS((S, ATTN_OUT), f32)],
        compiler_params=_cparams(("parallel", "parallel")),
        name=name,
    )(qkv, qkv, qkv, qkv, qkv, bias)


def _attn_bwd(qkv, bias, do, dvec, lse, bps, name):
    S = qkv.shape[0]
    nt = S // ATTN_TILE
    scale = HEAD_DIM ** -0.5

    def assemble(parts):
        rows = [parts[0][:ATTN_BLOCK]]
        for b in range(ATTN_SUB - 1):
            rows.append(parts[b][ATTN_BLOCK:] + parts[b + 1][:ATTN_BLOCK])
        rows.append(parts[-1][ATTN_BLOCK:])
        return rows

    def body(q_ref, kp_ref, kc_ref, vp_ref, vc_ref, b_ref, do_ref, dvec_ref, lse_ref,
             dq_ref, dk_ref, dv_ref, db_ref, ck, cv):
        t = pl.program_id(1)
        last = ATTN_TILE - ATTN_BLOCK

        @pl.when(t == 0)
        def _():
            ck[...] = jnp.zeros_like(ck)
            cv[...] = jnp.zeros_like(cv)
            db_ref[...] = jnp.zeros_like(db_ref)

        @pl.when(t < nt)
        def _():
            kk = jnp.concatenate([kp_ref[...], kc_ref[...]], axis=0)
            vv = jnp.concatenate([vp_ref[...], vc_ref[...]], axis=0)
            low = _head_masks()
            col = lax.broadcasted_iota(jnp.int32, (ATTN_BLOCK, 2 * ATTN_BLOCK), 1)
            low2 = lax.broadcasted_iota(jnp.int32, (2 * ATTN_BLOCK, LANE), 1) < HEAD_DIM
            dk_parts, dv_parts = [], []
            dsum = [None, None]
            for b in range(ATTN_SUB):
                lo = b * ATTN_BLOCK
                rows = slice(lo, lo + ATTN_BLOCK)
                keys = slice(lo, lo + 2 * ATTN_BLOCK)
                dead = jnp.logical_and((t * ATTN_SUB + b) % bps == 0, col < ATTN_BLOCK)
                q2 = q_ref[rows, :]
                kb, vb = kk[keys], vv[keys]
                do2 = do_ref[rows, :].astype(bf16)
                dvec2 = dvec_ref[rows, :]
                lse2 = lse_ref[rows, :]
                dqs, dks, dvs = [], [], []
                for h in range(2):
                    hm = low if h == 0 else jnp.logical_not(low)
                    c0 = h * HEAD_DIM
                    qh = jnp.where(hm, q2, jnp.zeros_like(q2))
                    doh = jnp.where(hm, do2, jnp.zeros_like(do2))
                    s = lax.dot_general(qh, kb, NT, preferred_element_type=f32) * scale + b_ref[h]
                    s = jnp.where(dead, NEG_INF, s)
                    p = jnp.exp(s - lse2[:, c0:c0 + 1])
                    dp = lax.dot_general(doh, vb, NT, preferred_element_type=f32)
                    ds = p * (dp - dvec2[:, c0:c0 + 1])
                    dsum[h] = ds if dsum[h] is None else dsum[h] + ds
                    dsb = ds.astype(bf16)
                    dqs.append(jnp.dot(dsb, kb, preferred_element_type=f32) * scale)
                    dks.append(lax.dot_general(dsb, q2, TN, preferred_element_type=f32) * scale)
                    dvs.append(lax.dot_general(p.astype(bf16), do2, TN, preferred_element_type=f32))
                dq_ref[rows, :] = jnp.where(low, dqs[0], dqs[1]).astype(bf16)
                dk_parts.append(jnp.where(low2, dks[0], dks[1]))
                dv_parts.append(jnp.where(low2, dvs[0], dvs[1]))
            db_ref[0] += dsum[0]
            db_ref[1] += dsum[1]
            for parts, carry, out_ref in ((dk_parts, ck, dk_ref), (dv_parts, cv, dv_ref)):
                rws = assemble(parts)
                out_ref[:last, :] = carry[:last, :].astype(bf16)
                out_ref[last:, :] = (carry[last:, :] + rws[0]).astype(bf16)
                for b in range(ATTN_SUB):
                    carry[b * ATTN_BLOCK:(b + 1) * ATTN_BLOCK, :] = rws[b + 1]

        @pl.when(t == nt)
        def _():
            dk_ref[...] = ck[...].astype(bf16)
            dv_ref[...] = cv[...].astype(bf16)

    tile = (ATTN_TILE, LANE)
    cur = pl.BlockSpec(tile, lambda h, t: (jnp.minimum(t, nt - 1), h))
    lag = pl.BlockSpec(tile, lambda h, t: (jnp.maximum(t - 1, 0), h))
    bspec = pl.BlockSpec((2, ATTN_BLOCK, 2 * ATTN_BLOCK), lambda h, t: (h, 0, 0))
    return pl.pallas_call(
        body,
        grid=(4, nt + 1),
        in_specs=_qkv_specs(nt) + [bspec, cur, cur, cur],
        out_specs=[cur, lag, lag, bspec],
        out_shape=[SDS((S, ATTN_OUT), bf16), SDS((S, ATTN_OUT), bf16), SDS((S, ATTN_OUT), bf16),
                   SDS((8, ATTN_BLOCK, 2 * ATTN_BLOCK), f32)],
        scratch_shapes=[pltpu.VMEM(tile, f32), pltpu.VMEM(tile, f32)],
        compiler_params=_cparams(("parallel", "arbitrary")),
        name=name,
    )(qkv, qkv, qkv, qkv, qkv, bias, do, dvec, lse)


def _attn_merge(o0, o1, o2, l0, l1, l2):
    S, W = o0.shape
    R = PERM_ROWS

    def body(o0_ref, o1_ref, o2_ref, l0_ref, l1_ref, l2_ref, y_ref, yb_ref, w0_ref, w1_ref, w2_ref,
             so1, so2, sl1, sl2):
        _to_natural(o1_ref, so1, 4)
        _to_natural(l1_ref, sl1, 4)
        _to_natural(o2_ref, so2, 16)
        _to_natural(l2_ref, sl2, 16)
        a, b, c = l0_ref[...], sl1[...], sl2[...]
        m = jnp.maximum(jnp.maximum(a, b), c)
        ea, eb, ec = jnp.exp(a - m), jnp.exp(b - m), jnp.exp(c - m)
        den = (ea + eb) + ec
        w0, w1, w2 = ea / den, eb / den, ec / den
        y = (w0 * o0_ref[...] + w1 * so1[...]) + w2 * so2[...]
        y_ref[...] = y
        yb_ref[...] = y.astype(bf16)
        w0_ref[...] = w0
        w1_ref[...] = w1
        w2_ref[...] = w2

    nat = pl.BlockSpec((R, LANE), lambda i, j: (i, j))
    v4 = lambda t: t.reshape(4, S // 4, W)
    v16 = lambda t: t.reshape(16, S // 16, W)
    return pl.pallas_call(
        body,
        grid=(S // R, W // LANE),
        in_specs=[nat, _perm_spec(4), _perm_spec(16)] * 2,
        out_specs=[nat] * 5,
        out_shape=[SDS((S, W), f32), SDS((S, W), bf16)] + [SDS((S, W), f32)] * 3,
        scratch_shapes=[pltpu.VMEM((R, LANE), f32)] * 4,
        compiler_params=_cparams(("parallel", "parallel")),
        name="attn_merge",
    )(o0, v4(o1), v16(o2), l0, v4(l1), v16(l2))


def _attn_merge_bwd(dy, y, w0, w1, w2):
    S, W = dy.shape
    R = PERM_ROWS

    def body(dy_ref, y_ref, w0_ref, w1_ref, w2_ref, a0, a1, a2, b0, b1, b2, sa, sb):
        dyv = dy_ref[...]
        r = lax.broadcasted_iota(jnp.int32, (LANE, LANE), 0) // HEAD_DIM
        c = lax.broadcasted_iota(jnp.int32, (LANE, LANE), 1) // HEAD_DIM
        seg = jnp.where(r == c, 1.0, 0.0).astype(f32)
        cbar = jnp.dot(dyv * y_ref[...], seg, precision=HIGHEST, preferred_element_type=f32)
        w = w0_ref[...]
        a0[...] = (w * dyv).astype(bf16)
        b0[...] = w * cbar
        for d, w_ref, a_ref, b_ref in ((4, w1_ref, a1, b1), (16, w2_ref, a2, b2)):
            w = w_ref[...]
            sa[...] = w * dyv
            sb[...] = w * cbar
            n = R // d
            for k in range(d):
                rows = pl.ds(k, n, stride=d)
                a_ref[k] = sa[rows, :].astype(bf16)
                b_ref[k] = sb[rows, :]

    nat = pl.BlockSpec((R, LANE), lambda i, j: (i, j))
    shapes = lambda dt: [SDS((S, W), dt), SDS((4, S // 4, W), dt), SDS((16, S // 16, W), dt)]
    outs = pl.pallas_call(
        body,
        grid=(S // R, W // LANE),
        in_specs=[nat] * 5,
        out_specs=[nat, _perm_spec(4), _perm_spec(16)] * 2,
        out_shape=shapes(bf16) + shapes(f32),
        scratch_shapes=[pltpu.VMEM((R, LANE), f32)] * 2,
        compiler_params=_cparams(("parallel", "parallel")),
        name="attn_merge_bwd",
    )(dy, y, w0, w1, w2)
    return [t.reshape(S, W) for t in outs]


HGRN_SB = 256


def _chunk_masks():
    r = jnp.arange(HGRN_SB)[:, None]
    c = jnp.arange(HGRN_SB)[None, :]
    same = (r // HGRN_CHUNK) == (c // HGRN_CHUNK)
    return jnp.stack([same & (c <= r), same, same & (c >= r)]).astype(bf16)


def _mask_dot(mask, x):
    hi = x.astype(bf16)
    r1 = x - hi.astype(f32)
    mid = r1.astype(bf16)
    lo = (r1 - mid.astype(f32)).astype(bf16)
    p = jnp.dot(mask, jnp.concatenate([hi, mid, lo], axis=1), preferred_element_type=f32)
    n = x.shape[1]
    return (p[:, :n] + p[:, n:2 * n]) + p[:, 2 * n:]


def _hgrn_prep(q_raw, f_raw, lbv, tril, same):
    sq = _sigmoid(q_raw)
    qs = q_raw * sq
    sig = _sigmoid(f_raw)
    f = lbv + (1.0 - lbv) * sig
    g = jnp.log(f)
    k = 1.0 - f
    G = _mask_dot(tril, g)
    GL = _mask_dot(same, g)
    eG = jnp.exp(G)
    einv = jnp.exp(-G)
    edec = jnp.exp(GL - G)
    return dict(sq=sq, qs=qs, sig=sig, f=f, k=k, eG=eG, einv=einv, edec=edec, eGL=jnp.exp(GL),
                qt=qs * eG, kt=k * einv, kd=k * edec)


def _ride_split(ride, rest, n_out, n_scratch):
    if ride is None:
        return None, rest[:n_out], None, rest[n_out:], None
    return rest[0], rest[1:1 + n_out], rest[1 + n_out], rest[2 + n_out:2 + n_out + n_scratch], rest[2 + n_out + n_scratch:]


def _hgrn_fwd(hg, lb, normw, ride=None):
    S = hg.shape[0]
    sb = HGRN_SB
    nsb = S // sb
    nch = sb // HGRN_CHUNK

    def body(q_ref, f_ref, v_ref, og_ref, lb_ref, nw_ref, m_ref, *rest):
        src_ref, (y_ref, o_ref, ck_ref), got_ref, (st,), sems = _ride_split(ride, rest, 3, 1)
        j = pl.program_id(1)
        if ride is not None:
            @pl.when(jnp.logical_and(pl.program_id(0) == 0, j == 0))
            def _():
                _chip_start(src_ref, got_ref, sems[0], sems[1], ride[1])

        @pl.when(j == 0)
        def _():
            st[...] = jnp.zeros_like(st)

        ST = st[...]
        ck_ref[0, 0] = ST
        tril_m = m_ref[0]
        tril = tril_m.astype(f32) > 0.5
        pr = _hgrn_prep(q_ref[...], f_ref[...], lb_ref[...], tril_m, m_ref[1])
        qtb, ktb, kdb = pr["qt"].astype(bf16), pr["kt"].astype(bf16), pr["kd"].astype(bf16)
        eGL = pr["eGL"]
        vb = v_ref[...].astype(bf16)
        A = jnp.where(tril, lax.dot_general(qtb, ktb, NT, preferred_element_type=f32), 0.0)
        o = jnp.dot(A.astype(bf16), vb, preferred_element_type=f32)
        outs = []
        for ci in range(nch):
            lo = ci * HGRN_CHUNK
            sl = slice(lo, lo + HGRN_CHUNK)
            outs.append(o[sl] + lax.dot_general(qtb[sl], ST.astype(bf16), NT, preferred_element_type=f32))
            ST = ST * eGL[lo:lo + 1, :] + lax.dot_general(vb[sl], kdb[sl], TN, preferred_element_type=f32)
        st[...] = ST
        of = jnp.concatenate(outs, axis=0)
        o_ref[...] = of
        rms = lax.rsqrt(jnp.mean(of * of, axis=-1, keepdims=True) + EPS)
        ogv = og_ref[...]
        y_ref[...] = ((of * rms * nw_ref[...]) * (ogv * _sigmoid(ogv))).astype(bf16)

        if ride is not None:
            @pl.when(jnp.logical_and(pl.program_id(0) == 3, j == nsb - 1))
            def _():
                _chip_finish(src_ref, got_ref, sems[0], sems[1], ride[1])

    col = lambda off: pl.BlockSpec((sb, LANE), lambda h, j: (j, off + h))
    riding = ride is not None
    res = pl.pallas_call(
        body,
        grid=(4, nsb),
        in_specs=[col(0), col(4), col(8), col(12), pl.BlockSpec((1, LANE), lambda h, j: (0, h)),
                  pl.BlockSpec((1, LANE), lambda h, j: (0, 0)),
                  pl.BlockSpec((3, sb, sb), lambda h, j: (0, 0, 0))] + ([_ANY] if riding else []),
        out_specs=[col(0), col(0), pl.BlockSpec((1, 1, LANE, LANE), lambda h, j: (h, j, 0, 0))]
        + ([_ANY] if riding else []),
        out_shape=[SDS((S, HGRN_W), bf16), SDS((S, HGRN_W), f32), SDS((4, nsb, LANE, LANE), f32)]
        + ([_chip_out_shape(*ride)] if riding else []),
        scratch_shapes=[pltpu.VMEM((LANE, LANE), f32)] + (list(_CHIP_SEMS) if riding else []),
        compiler_params=_cparams(("arbitrary", "arbitrary") if riding else ("parallel", "arbitrary")),
        name="hgrn_fwd",
    )(hg, hg, hg, hg, lb, normw, _chunk_masks(), *([ride[0]] if riding else []))
    return tuple(res) if riding else (*res, None)


def _hgrn_bwd(hg, o_raw, dy, ck, lb, normw, ride=None):
    S = hg.shape[0]
    sb = HGRN_SB
    nsb = S // sb
    nch = sb // HGRN_CHUNK

    def body(q_ref, f_ref, v_ref, og_ref, o_ref, dy_ref, ck_ref, lb_ref, nw_ref, m_ref, *rest):
        src_ref, outs, got_ref, (dst, alb, anw), sems = _ride_split(ride, rest, 6, 3)
        dq_ref, df_ref, dv_ref, dog_ref, glb_ref, gnw_ref = outs
        j = pl.program_id(1)
        if ride is not None:
            @pl.when(jnp.logical_and(pl.program_id(0) == 0, j == 0))
            def _():
                _chip_start(src_ref, got_ref, sems[0], sems[1], ride[1])

        @pl.when(j == 0)
        def _():
            dst[...] = jnp.zeros_like(dst)
            alb[...] = jnp.zeros_like(alb)
            anw[...] = jnp.zeros_like(anw)

        tril_m = m_ref[0]
        tril = tril_m.astype(f32) > 0.5
        lbv = lb_ref[...]
        q_raw = q_ref[...]
        pr = _hgrn_prep(q_raw, f_ref[...], lbv, tril_m, m_ref[1])
        qt, kt, kd, eGL = pr["qt"], pr["kt"], pr["kd"], pr["eGL"]
        qtb, ktb, kdb = qt.astype(bf16), kt.astype(bf16), kd.astype(bf16)
        vb = v_ref[...].astype(bf16)

        o = o_ref[...]
        ogv = og_ref[...]
        sog = _sigmoid(ogv)
        rms = lax.rsqrt(jnp.mean(o * o, axis=-1, keepdims=True) + EPS)
        oh = o * rms
        nw = nw_ref[...]
        dyv = dy_ref[...]
        dog_ref[...] = (dyv * (oh * nw) * (sog * (1.0 + ogv * (1.0 - sog)))).astype(bf16)
        dohw = dyv * (ogv * sog)
        anw[...] += _colsum8(dohw * oh)
        doh = dohw * nw
        do = rms * (doh - oh * jnp.mean(doh * oh, axis=-1, keepdims=True))
        dob = do.astype(bf16)

        Ab = jnp.where(tril, lax.dot_general(qtb, ktb, NT, preferred_element_type=f32), 0.0).astype(bf16)
        dAb = jnp.where(tril, lax.dot_general(dob, vb, NT, preferred_element_type=f32), 0.0).astype(bf16)
        dv_acc = lax.dot_general(Ab, dob, TN, preferred_element_type=f32)
        dqt = jnp.dot(dAb, ktb, preferred_element_type=f32)
        dkt = lax.dot_general(dAb, qtb, TN, preferred_element_type=f32)

        ST = ck_ref[0, 0]
        states = []
        for ci in range(nch):
            lo = ci * HGRN_CHUNK
            sl = slice(lo, lo + HGRN_CHUNK)
            states.append(ST)
            ST = ST * eGL[lo:lo + 1, :] + lax.dot_general(vb[sl], kdb[sl], TN, preferred_element_type=f32)

        dST = dst[...]
        dqt_i, dkd_i, dv_i, deg_i = [None] * nch, [None] * nch, [None] * nch, [None] * nch
        for ci in reversed(range(nch)):
            lo = ci * HGRN_CHUNK
            sl = slice(lo, lo + HGRN_CHUNK)
            ST0 = states[ci]
            dSTb = dST.astype(bf16)
            dv_i[ci] = lax.dot_general(kdb[sl], dSTb, NT, preferred_element_type=f32)
            dqt_i[ci] = jnp.dot(dob[sl], ST0.astype(bf16), preferred_element_type=f32)
            dkd_i[ci] = jnp.dot(vb[sl], dSTb, preferred_element_type=f32)
            deg_i[ci] = jnp.broadcast_to(jnp.sum(dST * ST0, axis=0, keepdims=True), (HGRN_CHUNK, LANE))
            dST = dST * eGL[lo:lo + 1, :] + lax.dot_general(dob[sl], qtb[sl], TN, preferred_element_type=f32)
        dst[...] = dST

        dqt = dqt + jnp.concatenate(dqt_i, axis=0)
        dkd = jnp.concatenate(dkd_i, axis=0)
        dv_ref[...] = (dv_acc + jnp.concatenate(dv_i, axis=0)).astype(bf16)
        deg = jnp.concatenate(deg_i, axis=0)

        dqs = dqt * pr["eG"]
        dkdkd = dkd * kd
        dG = dqt * qt - dkt * kt - dkdkd
        dk = dkt * pr["einv"] + dkd * pr["edec"]
        dGL = _mask_dot(m_ref[1], dkdkd) + eGL * deg
        dg = _mask_dot(m_ref[2], dG) + dGL
        df = dg / pr["f"] - dk
        sig = pr["sig"]
        df_ref[...] = (df * (1.0 - lbv) * (sig * (1.0 - sig))).astype(bf16)
        alb[...] += _colsum8(df * (1.0 - sig))
        sq = pr["sq"]
        dq_ref[...] = (dqs * (sq * (1.0 + q_raw * (1.0 - sq)))).astype(bf16)

        @pl.when(j == nsb - 1)
        def _():
            glb_ref[...] = jnp.broadcast_to(jnp.sum(alb[...], axis=0, keepdims=True), (SUBLANE, LANE))
            gnw_ref[...] = jnp.broadcast_to(jnp.sum(anw[...], axis=0, keepdims=True), (SUBLANE, LANE))

        if ride is not None:
            @pl.when(jnp.logical_and(pl.program_id(0) == 3, j == nsb - 1))
            def _():
                _chip_finish(src_ref, got_ref, sems[0], sems[1], ride[1])

    rev = lambda off: pl.BlockSpec((sb, LANE), lambda h, j: (nsb - 1 - j, off + h))
    stat = pl.BlockSpec((SUBLANE, LANE), lambda h, j: (0, h))
    riding = ride is not None
    res = pl.pallas_call(
        body,
        grid=(4, nsb),
        in_specs=[rev(0), rev(4), rev(8), rev(12), rev(0), rev(0),
                  pl.BlockSpec((1, 1, LANE, LANE), lambda h, j: (h, nsb - 1 - j, 0, 0)),
                  pl.BlockSpec((1, LANE), lambda h, j: (0, h)), pl.BlockSpec((1, LANE), lambda h, j: (0, 0)),
                  pl.BlockSpec((3, sb, sb), lambda h, j: (0, 0, 0))]
        + ([_ANY] if riding else []),
        out_specs=[rev(0), rev(0), rev(0), rev(0), stat, stat] + ([_ANY] if riding else []),
        out_shape=[SDS((S, HGRN_W), bf16)] * 4 + [SDS((SUBLANE, HGRN_W), f32)] * 2
        + ([_chip_out_shape(*ride)] if riding else []),
        scratch_shapes=[pltpu.VMEM((LANE, LANE), f32), pltpu.VMEM((SUBLANE, LANE), f32),
                        pltpu.VMEM((SUBLANE, LANE), f32)] + (list(_CHIP_SEMS) if riding else []),
        compiler_params=_cparams(("arbitrary", "arbitrary") if riding else ("parallel", "arbitrary")),
        name="hgrn_bwd",
    )(hg, hg, hg, hg, o_raw, dy, ck, lb, normw, _chunk_masks(), *([ride[0]] if riding else []))
    return tuple(res) if riding else (*res, None)


def _lb_fwd(raw):
    def body(r_ref, o_ref):
        r = r_ref[...]
        m = jnp.max(r, axis=0, keepdims=True)
        e = jnp.exp(r - m)
        o_ref[...] = (e / jnp.sum(e, axis=0, keepdims=True))[0:1]

    return pl.pallas_call(body, out_shape=SDS((1, raw.shape[1]), f32), name="lb_fwd")(raw)


def _lb_bwd(raw, dlb):
    def body(r_ref, d_ref, o_ref):
        r = r_ref[...]
        m = jnp.max(r, axis=0, keepdims=True)
        e = jnp.exp(r - m)
        s = e / jnp.sum(e, axis=0, keepdims=True)
        s0 = s[0:1]
        onehot0 = jnp.where(lax.broadcasted_iota(jnp.int32, r.shape, 0) == 0, 1.0, 0.0)
        o_ref[...] = d_ref[...] * s0 * (onehot0 - s)

    return pl.pallas_call(body, out_shape=SDS(raw.shape, f32), name="lb_bwd")(raw, dlb)


def _gate_fwd(a, b, gc):
    S, D = a.shape
    tm = _pick(S, 512)

    def body(a_ref, b_ref, g0_ref, g1_ref, o_ref):
        s0, s1 = _sigmoid(g0_ref[...].astype(f32)), _sigmoid(g1_ref[...].astype(f32))
        o_ref[...] = (s0 * a_ref[...].astype(f32) + s1 * b_ref[...].astype(f32)).astype(bf16)

    row = pl.BlockSpec((tm, D), lambda i: (i, 0))
    return pl.pallas_call(
        body,
        grid=(S // tm,),
        in_specs=[row, row, row, pl.BlockSpec((tm, D), lambda i: (i, 1))],
        out_specs=row,
        out_shape=SDS((S, D), bf16),
        compiler_params=_cparams(("parallel",)),
        name="gate_fwd",
    )(a, b, gc, gc)


def _gate_bwd(dm, a, b, gc):
    S, D = a.shape
    tm = _pick(S, 512)

    def body(dm_ref, a_ref, b_ref, g0_ref, g1_ref, da_ref, db_ref, dg_ref):
        dmv = dm_ref[...].astype(f32)
        s0, s1 = _sigmoid(g0_ref[...].astype(f32)), _sigmoid(g1_ref[...].astype(f32))
        da_ref[...] = (dmv * s0).astype(bf16)
        db_ref[...] = (dmv * s1).astype(bf16)
        dg_ref[:, :D] = (dmv * a_ref[...].astype(f32) * (s0 * (1.0 - s0))).astype(bf16)
        dg_ref[:, D:] = (dmv * b_ref[...].astype(f32) * (s1 * (1.0 - s1))).astype(bf16)

    row = pl.BlockSpec((tm, D), lambda i: (i, 0))
    wide = pl.BlockSpec((tm, 2 * D), lambda i: (i, 0))
    return pl.pallas_call(
        body,
        grid=(S // tm,),
        in_specs=[row, row, row, row, pl.BlockSpec((tm, D), lambda i: (i, 1))],
        out_specs=[row, row, wide],
        out_shape=[SDS((S, D), bf16), SDS((S, D), bf16), SDS((S, 2 * D), bf16)],
        compiler_params=_cparams(("parallel",)),
        name="gate_bwd",
    )(dm, a, b, gc, gc)


CONV_ROWS = 512
INV_SQRT2 = 0.7071067811865476
INV_SQRT_2PI = 0.3989422804014327


CONV_HALO = 16


def _tile8(a, rows):
    return jnp.tile(a, (rows // a.shape[0], 1))


def _conv_rows(u_ref, w, b, r0, first):
    R = CONV_ROWS
    cur = u_ref[pl.ds(r0, R), :].astype(f32)
    prev8 = u_ref[pl.ds(pl.multiple_of(jnp.maximum(r0 - CONV_HALO, 0), CONV_HALO), CONV_HALO), :].astype(f32)
    prev8 = jnp.where(first, 0.0, prev8)
    row = lax.broadcasted_iota(jnp.int32, (R, LANE), 0)
    x1 = jnp.where(row < 1, _tile8(pltpu.roll(prev8, 1, 0), R), pltpu.roll(cur, 1, 0))
    x2 = jnp.where(row < 2, _tile8(pltpu.roll(prev8, 2, 0), R), pltpu.roll(cur, 2, 0))
    c = ((b + w[0:1] * x2) + w[1:2] * x1) + w[2:3] * cur
    return c, x2, x1, cur


def _conv_fwd(ug, uv, wg, wv, bg, bv):
    S, F = ug.shape
    nchunk = S // CONV_ROWS

    def body(ug_ref, uv_ref, wg_ref, wv_ref, bg_ref, bv_ref, o_ref):
        wgv, wvv, bgv, bvv = wg_ref[...], wv_ref[...], bg_ref[...], bv_ref[...]

        def step(ci, carry):
            r0 = pl.multiple_of(ci * CONV_ROWS, CONV_ROWS)
            cg = _conv_rows(ug_ref, wgv, bgv, r0, ci == 0)[0]
            cv = _conv_rows(uv_ref, wvv, bvv, r0, ci == 0)[0]
            gelu = 0.5 * cg * (1.0 + lax.erf(cg * INV_SQRT2))
            o_ref[pl.ds(r0, CONV_ROWS), :] = (gelu * cv).astype(bf16)
            return carry

        lax.fori_loop(0, nchunk, step, 0)

    col = pl.BlockSpec((S, LANE), lambda j: (0, j))
    w3 = pl.BlockSpec((3, LANE), lambda j: (0, j))
    b1 = pl.BlockSpec((1, LANE), lambda j: (0, j))
    return pl.pallas_call(
        body,
        grid=(F // LANE,),
        in_specs=[col, col, w3, w3, b1, b1],
        out_specs=col,
        out_shape=SDS((S, F), bf16),
        compiler_params=_cparams(("parallel",), VMEM_BIG),
        name="conv_fwd",
    )(ug, uv, wg, wv, bg, bv)


def _conv_bwd(ug, uv, dact, wg, wv, bg, bv):
    S, F = ug.shape
    R = CONV_ROWS
    nchunk = S // R

    def body(ug_ref, uv_ref, da_ref, wg_ref, wv_ref, bg_ref, bv_ref, dug_ref, duv_ref, sg_ref, sv_ref, dcg, dcv):
        wgv, wvv, bgv, bvv = wg_ref[...], wv_ref[...], bg_ref[...], bv_ref[...]
        zero = jnp.zeros((SUBLANE, LANE), f32)

        def fwd_step(ci, acc):
            r0 = pl.multiple_of(ci * R, R)
            cg, g2, g1, g0 = _conv_rows(ug_ref, wgv, bgv, r0, ci == 0)
            cv, v2, v1, v0 = _conv_rows(uv_ref, wvv, bvv, r0, ci == 0)
            da = da_ref[pl.ds(r0, R), :].astype(f32)
            cdf = 0.5 * (1.0 + lax.erf(cg * INV_SQRT2))
            pdf = INV_SQRT_2PI * jnp.exp(-0.5 * cg * cg)
            dg = da * cv * (cdf + cg * pdf)
            dv = da * (cg * cdf)
            dcg[pl.ds(r0, R), :] = dg
            dcv[pl.ds(r0, R), :] = dv
            new = (acc[0] + _colsum8(dg * g2), acc[1] + _colsum8(dg * g1), acc[2] + _colsum8(dg * g0),
                   acc[3] + _colsum8(dg),
                   acc[4] + _colsum8(dv * v2), acc[5] + _colsum8(dv * v1), acc[6] + _colsum8(dv * v0),
                   acc[7] + _colsum8(dv))
            return new

        acc = lax.fori_loop(0, nchunk, fwd_step, (zero,) * 8)
        rows = lax.broadcasted_iota(jnp.int32, (SUBLANE, LANE), 0)

        def stats(parts):
            out = jnp.zeros((SUBLANE, LANE), f32)
            for k, pt in enumerate(parts):
                out = jnp.where(rows == k, jnp.sum(pt, axis=0, keepdims=True), out)
            return out

        sg_ref[...] = stats(acc[0:4])
        sv_ref[...] = stats(acc[4:8])

        def du_rows(dc, w, r0, last):
            cur = dc[pl.ds(r0, R), :]
            nxt = dc[pl.ds(pl.multiple_of(jnp.minimum(r0 + R, S - SUBLANE), SUBLANE), SUBLANE), :]
            nxt = jnp.where(last, 0.0, nxt)
            row = lax.broadcasted_iota(jnp.int32, (R, LANE), 0)
            y1 = jnp.where(row >= R - 1, _tile8(pltpu.roll(nxt, SUBLANE - 1, 0), R), pltpu.roll(cur, R - 1, 0))
            y2 = jnp.where(row >= R - 2, _tile8(pltpu.roll(nxt, SUBLANE - 2, 0), R), pltpu.roll(cur, R - 2, 0))
            return w[2:3] * cur + w[1:2] * y1 + w[0:1] * y2

        def bwd_step(ci, carry):
            r0 = pl.multiple_of(ci * R, R)
            last = ci == nchunk - 1
            dug_ref[pl.ds(r0, R), :] = du_rows(dcg, wgv, r0, last).astype(bf16)
            duv_ref[pl.ds(r0, R), :] = du_rows(dcv, wvv, r0, last).astype(bf16)
            return carry

        lax.fori_loop(0, nchunk, bwd_step, 0)

    col = pl.BlockSpec((S, LANE), lambda j: (0, j))
    w3 = pl.BlockSpec((3, LANE), lambda j: (0, j))
    b1 = pl.BlockSpec((1, LANE), lambda j: (0, j))
    st = pl.BlockSpec((SUBLANE, LANE), lambda j: (0, j))
    return pl.pallas_call(
        body,
        grid=(F // LANE,),
        in_specs=[col, col, col, w3, w3, b1, b1],
        out_specs=[col, col, st, st],
        out_shape=[SDS((S, F), bf16), SDS((S, F), bf16), SDS((SUBLANE, F), f32), SDS((SUBLANE, F), f32)],
        scratch_shapes=[pltpu.VMEM((S, LANE), f32), pltpu.VMEM((S, LANE), f32)],
        compiler_params=_cparams(("parallel",), VMEM_BIG),
        name="conv_bwd",
    )(ug, uv, dact, wg, wv, bg, bv)


def _adam_math(w, g, m, v):
    m = ADAM_B1 * m + (1.0 - ADAM_B1) * g
    v = ADAM_B2 * v + (1.0 - ADAM_B2) * (g * g)
    m_hat = m / (1.0 - ADAM_B1 ** ADAM_STEP)
    v_hat = v / (1.0 - ADAM_B2 ** ADAM_STEP)
    delta = -ADAM_LR * (m_hat / (jnp.sqrt(v_hat) + ADAM_EPS) + ADAM_WD * w)
    return delta, m, v


def _adamw(w, m, v, g, name):
    R, C = w.shape
    parts = g.ndim == 3
    tr = R
    for t in (256, 128, 64, 32, 16):
        if R % t == 0 and R > t:
            tr = t
            break

    def body(w_ref, m_ref, v_ref, g_ref, go_ref, d_ref, mo_ref, vo_ref):
        if parts:
            gv = ((g_ref[0].astype(f32) + g_ref[1].astype(f32)) + g_ref[2].astype(f32)) + g_ref[3].astype(f32)
        else:
            gv = g_ref[...]
        go_ref[...] = gv
        d, mn, vn = _adam_math(w_ref[...], gv, m_ref[...], v_ref[...])
        d_ref[...] = d
        mo_ref[...] = mn
        vo_ref[...] = vn

    row = pl.BlockSpec((tr, C), lambda i: (i, 0))
    gspec = pl.BlockSpec((4, tr, C), lambda i: (0, i, 0)) if parts else row
    return pl.pallas_call(
        body,
        grid=(R // tr,),
        in_specs=[row, row, row, gspec],
        out_specs=[row] * 4,
        out_shape=[SDS((R, C), f32)] * 4,
        compiler_params=_cparams(("parallel",)),
        name=name,
    )(w, m, v, g)


def _sum8(parts, name):
    _, _, R, C = parts.shape

    def body(p_ref, o_ref):
        acc = p_ref[0, 0]
        for c in range(2):
            for k in range(4):
                if c or k:
                    acc = acc + p_ref[c, k]
        o_ref[...] = acc

    return pl.pallas_call(body, out_shape=SDS((R, C), f32), name=name)(parts)


def _pair_add(by_core, b, name):
    _, K, R, C = by_core.shape
    tr = R // 2 if R % 32 == 0 else R

    def body(c_ref, a_ref, b_ref, o_ref):
        o_ref[...] = (a_ref[0].astype(f32) + b_ref[...].astype(f32)).astype(bf16)

    blk = pl.BlockSpec((1, tr, C), lambda k, i, c: (k, i, 0))
    return pl.pallas_call(
        body,
        grid_spec=pltpu.PrefetchScalarGridSpec(
            num_scalar_prefetch=1,
            grid=(K, R // tr),
            in_specs=[pl.BlockSpec((1, 1, tr, C), lambda k, i, c: (c[0], k, i, 0)), blk],
            out_specs=blk,
        ),
        out_shape=SDS((K, R, C), bf16),
        compiler_params=_cparams(("parallel", "parallel")),
        name=name,
    )(lax.axis_index("c").astype(jnp.int32).reshape(1), by_core, b)


_ANY = pl.BlockSpec(memory_space=pl.ANY)


def _chip_copies(src_ref, out_ref, send_sems, recv_sems, gather):
    x, y, c = lax.axis_index("x"), lax.axis_index("y"), lax.axis_index("c")
    mine = 2 * x + y

    def piece(k):
        return src_ref if gather else src_ref.at[k]

    sends, recvs = [], []
    for j, (px, py) in enumerate([(1 - x, y), (x, 1 - y), (1 - x, 1 - y)]):
        sends.append(pltpu.make_async_remote_copy(
            src_ref=piece(2 * px + py), dst_ref=out_ref.at[mine], send_sem=send_sems.at[j],
            recv_sem=recv_sems.at[j], device_id=(px, py, c), device_id_type=MESH))
        recvs.append(pltpu.make_async_remote_copy(
            src_ref=piece(mine), dst_ref=out_ref.at[2 * px + py], send_sem=send_sems.at[j],
            recv_sem=recv_sems.at[j], device_id=(px, py, c), device_id_type=MESH))
    return sends, recvs


def _chip_start(src_ref, out_ref, send_sems, recv_sems, gather):
    for cp in _chip_copies(src_ref, out_ref, send_sems, recv_sems, gather)[0]:
        cp.start()


def _chip_finish(src_ref, out_ref, send_sems, recv_sems, gather):
    sends, recvs = _chip_copies(src_ref, out_ref, send_sems, recv_sems, gather)
    for cp in recvs:
        cp.wait_recv()
    for cp in sends:
        cp.wait_send()


def _chip_out_shape(src, gather):
    return SDS((4,) + tuple(src.shape if gather else src.shape[1:]), src.dtype)


_CHIP_SEMS = [pltpu.SemaphoreType.DMA((3,)), pltpu.SemaphoreType.DMA((3,))]


def _fill_own(out, src, gather):
    mine = 2 * lax.axis_index("x") + lax.axis_index("y")
    own = src if gather else lax.dynamic_index_in_dim(src, mine, axis=0, keepdims=False)
    return lax.dynamic_update_index_in_dim(out, own, mine, axis=0)


def _chip_comm(src, gather, name):
    def body(src_ref, out_ref, send_sems, recv_sems):
        _chip_start(src_ref, out_ref, send_sems, recv_sems, gather)
        _chip_finish(src_ref, out_ref, send_sems, recv_sems, gather)

    out = pl.pallas_call(
        body,
        in_specs=[_ANY],
        out_specs=_ANY,
        out_shape=_chip_out_shape(src, gather),
        scratch_shapes=list(_CHIP_SEMS),
        name=name,
    )(src)
    return _fill_own(out, src, gather)


def _core_gather(src, name):
    def body(src_ref, out_ref, send_sem, recv_sem):
        x, y, c = lax.axis_index("x"), lax.axis_index("y"), lax.axis_index("c")
        cp = pltpu.make_async_remote_copy(src_ref=src_ref, dst_ref=out_ref.at[c], send_sem=send_sem,
                                          recv_sem=recv_sem, device_id=(x, y, 1 - c), device_id_type=MESH)
        cp.start()
        pltpu.make_async_remote_copy(src_ref=src_ref, dst_ref=out_ref.at[1 - c], send_sem=send_sem,
                                     recv_sem=recv_sem, device_id=(x, y, 1 - c), device_id_type=MESH).wait_recv()
        cp.wait_send()

    out = pl.pallas_call(
        body,
        in_specs=[_ANY],
        out_specs=_ANY,
        out_shape=SDS((2,) + tuple(src.shape), src.dtype),
        scratch_shapes=[pltpu.SemaphoreType.DMA, pltpu.SemaphoreType.DMA],
        name=name,
    )(src)
    return lax.dynamic_update_index_in_dim(out, src, lax.axis_index("c"), axis=0)


def _core_swap(src, name):
    def body(src_ref, out_ref, send_sem, recv_sem):
        x, y, c = lax.axis_index("x"), lax.axis_index("y"), lax.axis_index("c")
        cp = pltpu.make_async_remote_copy(src_ref=src_ref.at[1 - c], dst_ref=out_ref, send_sem=send_sem,
                                          recv_sem=recv_sem, device_id=(x, y, 1 - c), device_id_type=MESH)
        cp.start()
        cp.wait()

    return pl.pallas_call(
        body,
        in_specs=[_ANY],
        out_specs=_ANY,
        out_shape=SDS(tuple(src.shape[1:]), src.dtype),
        scratch_shapes=[pltpu.SemaphoreType.DMA, pltpu.SemaphoreType.DMA],
        name=name,
    )(src)


def _all_gather(src, tag):
    by_chip = _chip_comm(src, True, tag + "_chips")
    both = _core_gather(by_chip, tag + "_cores")
    return jnp.swapaxes(both, 0, 1).reshape((8,) + tuple(src.shape))


_PACK_A = (("w_in", (1024, 1088)),)
_PACK_B = (("w_ba", (512, 128)), ("w_bh", (512, 128)), ("w_out", (128, 1024)), ("w_up", (1024, 704)),
           ("w_down", (352, 1024)))
_PACK_SIZES = _PACK_A + _PACK_B


def _slab_rows(sizes):
    return sum(r * c for _, (r, c) in sizes) // D_MODEL


def _pack_rows(d, sizes):
    n = d[sizes[0][0]].shape[0]
    return jnp.concatenate([d[k].reshape(n, -1, D_MODEL) for k, _ in sizes], axis=1)


def _unpack_rows(slab, sizes):
    n = slab.shape[0]
    out, lo = {}, 0
    for key, (r, c) in sizes:
        rows = r * c // D_MODEL
        out[key] = slab[:, lo:lo + rows].reshape(n, r, c)
        lo += rows
    return out


def _by_core(gslab):
    return jnp.swapaxes(gslab.reshape((4, 2) + gslab.shape[1:]), 0, 1)


def _pair_sum(by_core, tag):
    return _pair_add(by_core, _core_swap(by_core, tag + "_cores"), tag + "_pair_add")


def _cols_to_full(t):
    return jnp.swapaxes(t, 0, 1).reshape(t.shape[1], -1)


def _full_to_cols(t):
    K = t.shape[0]
    return jnp.swapaxes(t.reshape(K, 8, -1), 0, 1)


_SMALL = (("pre_mix_norm", (1, 1024)), ("rel_bias", (32, 24)), ("hgrn_lb_raw", (2, 512)), ("hgrn_norm", (1, 128)),
          ("post_mix_norm", (1, 1024)), ("pre_ffn_norm", (1, 1024)), ("conv_b", (1, 5632)),
          ("post_ffn_norm", (1, 1024)))
_SMALL_ROWS = 96
_CONVW_ROWS = 136


_SMALL_USED = sum(r * c for _, (r, c) in _SMALL)


def _pack_small(d, extra=None):
    flat = jnp.concatenate([d[k].reshape(-1) for k, _ in _SMALL] + ([] if extra is None else [extra.reshape(-1)]))
    flat = jnp.pad(flat, (0, _SMALL_ROWS * LANE - flat.shape[0]))
    return flat.reshape(_SMALL_ROWS, LANE)


def _unpack_small(p):
    flat = p.reshape(-1)
    out, lo = {}, 0
    for k, shp in _SMALL:
        n = shp[0] * shp[1]
        out[k] = flat[lo:lo + n].reshape(shp)
        lo += n
    return out


def _local_step(x, tgt, WA, P, plan):
    S = x.shape[0]
    W = dict(WA)
    lb = _lb_fwd(P["hgrn_lb_raw"])
    hs = _prep(x, P["pre_mix_norm"])
    h1 = hs[0]
    consts = [_bias_consts(d) for d in DILATIONS]
    qkv, obuf, lbuf, biases = [], [], [], []
    for g, d in enumerate(DILATIONS):
        qkv_g = _mm(hs[g], W["w_qkv"][g], "nn", bf16, f"proj_qkv{g}")
        tab_t = P["rel_bias"][:, 8 * g:8 * g + 8].T
        bias_g = _bias_build(tab_t, consts[g][0], consts[g][1], f"bias_build{g}").reshape(8, ATTN_BLOCK, 2 * ATTN_BLOCK)
        o_g, l_g = _attn_fwd(qkv_g, bias_g, (S // d) // ATTN_BLOCK, f"attn_fwd{g}")
        qkv.append(qkv_g)
        biases.append(bias_g)
        lbuf.append(l_g)
        obuf.append(o_g)
    y_attn, y_attn_b, w0, w1, w2 = _attn_merge(obuf[0], obuf[1], obuf[2], lbuf[0], lbuf[1], lbuf[2])
    hg = _mm(h1, W["w_hg"], "nn", f32, "proj_hg")
    gc = _mm(h1, W["w_gate"], "nn", bf16, "proj_gate")
    y_hgrn, o_raw, ck, got = _hgrn_fwd(hg, lb, P["hgrn_norm"], plan.fwd_ride())
    W.update(plan.weights(got))
    a = _mm(y_attn_b, W["w_ba"], "nn", bf16, "branch_attn")
    b = _mm(y_hgrn, W["w_bh"], "nn", bf16, "branch_hgrn")
    merged = _gate_fwd(a, b, gc)
    mo = _mm(merged, W["w_out"], "nn", f32, "out_proj")
    x1, h2 = _mid_fwd(x, mo, P["post_mix_norm"], P["pre_ffn_norm"])
    ug = _mm(h2, W["w_up_g"], "nn", bf16, "up_gate")
    uv = _mm(h2, W["w_up_v"], "nn", bf16, "up_val")
    cw_g, cw_v = P["conv_w"][:, :D_FF], P["conv_w"][:, D_FF:]
    cb_g, cb_v = P["conv_b"][:, :D_FF], P["conv_b"][:, D_FF:]
    act = _conv_fwd(ug, uv, cw_g, cw_v, cb_g, cb_v)
    fo = _mm(act, W["w_down"], "nn", f32, "down_proj")
    loss, dy, dfo, g_post_ffn = _final(x1, fo, tgt, P["post_ffn_norm"])
    dact = _mm(dfo, W["w_down"], "nt", bf16, "d_act")
    gW_down = _mm(act, dfo, "tn", f32, "gw_down")
    dug, duv, st_g, st_v = _conv_bwd(ug, uv, dact, cw_g, cw_v, cb_g, cb_v)
    dh2 = _mm(dug, W["w_up_g"], "nt", f32, "dh2_gate")
    dh2 = _mm(duv, W["w_up_v"], "nt", f32, "dh2_val", acc=dh2)
    gW_up_g = _mm(h2, dug, "tn", f32, "gw_up_gate")
    gW_up_v = _mm(h2, duv, "tn", f32, "gw_up_val")
    dx1, dmo, g_pre_ffn, g_post_mix = _mid_bwd(dy, dh2, x1, mo, P["pre_ffn_norm"], P["post_mix_norm"])
    dmerged = _mm(dmo, W["w_out"], "nt", bf16, "d_merged")
    gW_out = _mm(merged, dmo, "tn", f32, "gw_out")
    da, db, dgc = _gate_bwd(dmerged, a, b, gc)
    dyattn = _mm(da, W["w_ba"], "nt", f32, "d_yattn")
    gW_ba = _mm(y_attn_b, da, "tn", f32, "gw_ba")
    dyhgrn = _mm(db, W["w_bh"], "nt", f32, "d_yhgrn")
    gW_bh = _mm(y_hgrn, db, "tn", f32, "gw_bh")
    big_b = dict(w_ba=gW_ba, w_bh=gW_bh, w_out=gW_out, w_up=[gW_up_g, gW_up_v], w_down=gW_down)
    dq_h, df_h, dv_h, dog_h, glb8, gnw8, got_b = _hgrn_bwd(hg, o_raw, dyhgrn, ck, lb, P["hgrn_norm"],
                                                          plan.bwd_ride(big_b))
    dhg = [dq_h, df_h, dv_h, dog_h]
    g_lb_raw = _lb_bwd(P["hgrn_lb_raw"], glb8[0:1])
    gn = gnw8[0:1]
    g_hgrn_norm = (gn[:, 0:128] + gn[:, 128:256]) + (gn[:, 256:384] + gn[:, 384:512])
    dos = _attn_merge_bwd(dyattn, y_attn, w0, w1, w2)
    dh_parts, gW_qkv, g_rel = [], [], []
    for g, d in enumerate(DILATIONS):
        dq, dk, dv, dbias = _attn_bwd(qkv[g], biases[g], dos[g], dos[3 + g], lbuf[g], (S // d) // ATTN_BLOCK,
                                      f"attn_bwd{g}")
        dqkv = [dq, dk, dv]
        gW_qkv.append(_mm(hs[g], dqkv, "tn", f32, f"gw_qkv{g}"))
        dh_g = _mm(dqkv, W["w_qkv"][g], "nt", f32, f"dh1_qkv{g}")
        dh_parts.append(dh_g)
        g_rel.append(_bias_grad(dbias.reshape(8, -1), consts[g][0], f"bias_grad{g}"))
    dh_main = _mm(dhg, W["w_hg"], "nt", f32, "dh1_hg", acc=dh_parts[0])
    dh_main = _mm(dgc, W["w_gate"], "nt", f32, "dh1_gate", acc=dh_main)
    gW_hg = _mm(h1, dhg, "tn", f32, "gw_hg")
    gW_gate = _mm(h1, dgc, "tn", f32, "gw_gate")
    grad_x, g_pre_mix = _first_bwd(x, dx1, _dh_sum(dh_main, dh_parts[1], dh_parts[2]), P["pre_mix_norm"])

    gW_in = gW_qkv + [gW_hg, gW_gate]
    g_conv_w = jnp.concatenate([st_g[0:3], st_v[0:3]], axis=1)
    g_conv_b = jnp.concatenate([st_g[3:4], st_v[3:4]], axis=1)
    small = dict(pre_mix_norm=g_pre_mix, rel_bias=jnp.concatenate(g_rel, axis=1), hgrn_lb_raw=g_lb_raw,
                 hgrn_norm=g_hgrn_norm, post_mix_norm=g_post_mix, pre_ffn_norm=g_pre_ffn, conv_b=g_conv_b,
                 post_ffn_norm=g_post_ffn, conv_w=g_conv_w)
    return loss, grad_x, gW_in, big_b, got_b, small


def _weights_a(both):
    w_in = jnp.transpose(both, (2, 1, 0, 3)).reshape(D_MODEL, -1)
    return dict(
        w_qkv=[w_in[:, g * QKV_G:(g + 1) * QKV_G] for g in range(N_GROUPS)],
        w_hg=w_in[:, 3 * QKV_G:3 * QKV_G + 4 * HGRN_W],
        w_gate=w_in[:, 3 * QKV_G + 4 * HGRN_W:],
    )


def _weights_b(slabs):
    sh = _unpack_rows(slabs, _PACK_B)
    w_up = _cols_to_full(sh["w_up"])
    return dict(
        w_ba=_cols_to_full(sh["w_ba"]),
        w_bh=_cols_to_full(sh["w_bh"]),
        w_out=sh["w_out"].reshape(D_MODEL, D_MODEL),
        w_up_g=w_up[:, :D_FF],
        w_up_v=w_up[:, D_FF:],
        w_down=sh["w_down"].reshape(D_FF, D_MODEL),
    )


def _dest_cols(sections, width):
    out = []
    for j in range(8):
        lo, hi, off, pieces = j * width, (j + 1) * width, 0, []
        for s in sections:
            a, b = max(lo, off), min(hi, off + s.shape[1])
            if a < b:
                pieces.append(s[:, a - off:b - off])
            off += s.shape[1]
        out.append(pieces[0] if len(pieces) == 1 else jnp.concatenate(pieces, axis=1))
    return out


def _grad_blocks_a(sections):
    cols = _dest_cols(sections, 1088)
    return jnp.stack([jnp.stack([cols[2 * k + c].astype(bf16) for k in range(4)]) for c in range(2)])


def _grad_slab_b(g):
    shards = dict(w_ba=_full_to_cols(g["w_ba"]), w_bh=_full_to_cols(g["w_bh"]), w_out=g["w_out"].reshape(8, 128, D_MODEL),
                  w_up=jnp.stack(_dest_cols(g["w_up"], 704)), w_down=g["w_down"].reshape(8, 352, D_MODEL))
    return _pack_rows({k: v.astype(bf16) for k, v in shards.items()}, _PACK_B)


class _SlabB:
    def __init__(self, slab):
        self.slab = slab
        self.chip_sum = None

    def fwd_ride(self):
        return (self.slab, True)

    def weights(self, got):
        both = _core_gather(_fill_own(got, self.slab, True), "ag_b_cores")
        return _weights_b(jnp.swapaxes(both, 0, 1).reshape((8,) + tuple(self.slab.shape)))

    def bwd_ride(self, grads):
        self.chip_sum = _pair_sum(_by_core(_grad_slab_b(grads)), "rs_b")
        return (self.chip_sum, False)

    def parts(self, got_b):
        return _unpack_rows(_fill_own(got_b, self.chip_sum, False), _PACK_B)


def kernel(x, pre_mix_norm, w_in, rel_bias, hgrn_lb_raw, hgrn_norm, w_branch_attn, w_branch_hgrn, w_out, post_mix_norm, pre_ffn_norm, w_up, conv_w, conv_b, w_down, post_ffn_norm, loss_target, m_pre_mix_norm, m_w_in, m_rel_bias, m_hgrn_lb_raw, m_hgrn_norm, m_w_branch_attn, m_w_branch_hgrn, m_w_out, m_post_mix_norm, m_pre_ffn_norm, m_w_up, m_conv_w, m_conv_b, m_w_down, m_post_ffn_norm, v_pre_mix_norm, v_w_in, v_rel_bias, v_hgrn_lb_raw, v_hgrn_norm, v_w_branch_attn, v_w_branch_hgrn, v_w_out, v_post_mix_norm, v_pre_ffn_norm, v_w_up, v_conv_w, v_conv_b, v_w_down, v_post_ffn_norm):
    ci = lax.axis_index("c")
    dev = 4 * lax.axis_index("x") + 2 * lax.axis_index("y") + ci
    wts = dict(w_in=w_in[0], w_ba=w_branch_attn[0], w_bh=w_branch_hgrn[0], w_out=w_out[0], w_up=w_up[0],
               w_down=w_down[0])
    mom = dict(w_in=m_w_in[0], w_ba=m_w_branch_attn[0], w_bh=m_w_branch_hgrn[0], w_out=m_w_out[0], w_up=m_w_up[0],
               w_down=m_w_down[0])
    var = dict(w_in=v_w_in[0], w_ba=v_w_branch_attn[0], w_bh=v_w_branch_hgrn[0], w_out=v_w_out[0], w_up=v_w_up[0],
               w_down=v_w_down[0])
    small_w = dict(pre_mix_norm=pre_mix_norm, rel_bias=rel_bias, hgrn_lb_raw=hgrn_lb_raw, hgrn_norm=hgrn_norm,
                   post_mix_norm=post_mix_norm, pre_ffn_norm=pre_ffn_norm, conv_b=conv_b, post_ffn_norm=post_ffn_norm)
    small_m = dict(pre_mix_norm=m_pre_mix_norm, rel_bias=m_rel_bias, hgrn_lb_raw=m_hgrn_lb_raw, hgrn_norm=m_hgrn_norm,
                   post_mix_norm=m_post_mix_norm, pre_ffn_norm=m_pre_ffn_norm, conv_b=m_conv_b,
                   post_ffn_norm=m_post_ffn_norm)
    small_v = dict(pre_mix_norm=v_pre_mix_norm, rel_bias=v_rel_bias, hgrn_lb_raw=v_hgrn_lb_raw, hgrn_norm=v_hgrn_norm,
                   post_mix_norm=v_post_mix_norm, pre_ffn_norm=v_pre_ffn_norm, conv_b=v_conv_b,
                   post_ffn_norm=v_post_ffn_norm)

    slab_a = wts["w_in"].astype(bf16)
    WA = _weights_a(_core_gather(_chip_comm(slab_a, True, "ag_a_chips"), "ag_a_cores"))
    plan = _SlabB(_pack_rows({k: wts[k].astype(bf16)[None] for k, _ in _PACK_B}, _PACK_B)[0])
    cw_pad = jnp.pad(conv_w[0], ((0, SUBLANE - 3), (0, 768 - 704)))
    conv_w_full = _cols_to_full(_all_gather(cw_pad, "ag_convw")[:, 0:3, 0:704])
    P = dict(small_w)
    P["conv_w"] = conv_w_full

    loss8, grad_x, gW_in, _, got_b, small = _local_step(x[0], loss_target[0], WA, P, plan)

    chip_sum_a = _pair_sum(_grad_blocks_a(gW_in), "rs_a")
    parts = dict(w_in=_chip_comm(chip_sum_a, False, "rs_a_chips"))
    parts.update(plan.parts(got_b))
    outs_big = {}
    for k, _ in _PACK_SIZES:
        outs_big[k] = _adamw(wts[k], mom[k], var[k], parts[k], "adamw_" + k)

    spack = jnp.concatenate([_pack_small(small, loss8[0, 0:1]),
                             jnp.pad(small["conv_w"].reshape(-1, LANE), ((0, _CONVW_ROWS - 132), (0, 0)))], axis=0)
    allp = _core_gather(_chip_comm(spack, True, "ag_small_chips"), "ag_small_cores")
    ssum = _sum8(allp, "small_sum")
    gs = ssum[:_SMALL_ROWS]
    loss = ssum[_SMALL_USED // LANE, _SMALL_USED % LANE]
    res_small = _adamw(_pack_small(small_w), _pack_small(small_m), _pack_small(small_v), gs, "adamw_small")
    sm = [_unpack_small(t) for t in res_small]
    g_cw_full = ssum[_SMALL_ROWS:_SMALL_ROWS + 132].reshape(3, 2 * D_FF)
    g_cw = lax.dynamic_slice_in_dim(g_cw_full, dev * 704, 704, axis=1)
    res_cw = _adamw(conv_w[0], m_conv_w[0], v_conv_w[0], g_cw, "adamw_conv_w")

    def pick(i):
        def big_(k):
            return outs_big[k][i][None]
        return [sm[i]["pre_mix_norm"], big_("w_in"), sm[i]["rel_bias"], sm[i]["hgrn_lb_raw"], sm[i]["hgrn_norm"],
                big_("w_ba"), big_("w_bh"), big_("w_out"), sm[i]["post_mix_norm"], sm[i]["pre_ffn_norm"],
                big_("w_up"), res_cw[i][None], sm[i]["conv_b"], big_("w_down"), sm[i]["post_ffn_norm"]]

    return (loss, grad_x[None], *pick(0), *pick(1), *pick(2), *pick(3))
```

```python
import functools
import math

import jax
import jax.numpy as jnp
from jax import lax
from jax.experimental import pallas as pl
from jax.experimental.pallas import tpu as pltpu

f32 = jnp.float32
bf16 = jnp.bfloat16
SDS = jax.ShapeDtypeStruct
HIGHEST = lax.Precision.HIGHEST
MESH = pl.DeviceIdType.MESH

NN = (((1,), (0,)), ((), ()))
NT = (((1,), (1,)), ((), ()))
TN = (((0,), (0,)), ((), ()))

D_MODEL = 1024
N_GROUPS = 3
DILATIONS = (1, 4, 16)
HEAD_DIM = 64
ATTN_BLOCK = 128
QKV_G = 1536
ATTN_OUT = 512
HGRN_W = 512
HGRN_CHUNK = 32
D_FF = 2816
NUM_BUCKETS = 32
MAX_EXACT = 16
MAX_DISTANCE = 2048
NEG_INF = -1e30
EPS = 1e-6
LANE = 128
SUBLANE = 8
VMEM_BIG = 48 * 1024 * 1024
MM_ROWS = 512
MM_OUT_BYTES = 8 * 1024 * 1024

ADAM_LR, ADAM_B1, ADAM_B2, ADAM_EPS, ADAM_WD, ADAM_STEP = 0.001, 0.9, 0.999, 1e-08, 0.01, 10


def _pick(n, pref):
    t = pref
    while t >= LANE:
        if n % t == 0:
            return t
        t //= 2
    return n


def _cparams(sem=None, vmem=None):
    kw = {}
    if sem is not None:
        kw["dimension_semantics"] = sem
    if vmem is not None:
        kw["vmem_limit_bytes"] = vmem
    return pltpu.CompilerParams(**kw)


def _sigmoid(x):
    return jax.nn.sigmoid(x)


def _colsum8(x):
    return x.reshape(x.shape[0] // SUBLANE, SUBLANE, x.shape[1]).sum(axis=0)


def _mm(a, b, mode, out_dtype, name, acc=None, after=None):
    dims = {"nn": NN, "nt": NT, "tn": TN}[mode]
    has_acc = acc is not None
    if mode == "tn":
        assert not has_acc
        bs = list(b) if isinstance(b, (list, tuple)) else [b]
        K, M = a.shape
        widths = [t.shape[1] for t in bs]
        N = sum(widths)
        tmm = M if M * N * 4 <= MM_OUT_BYTES else M // 2
        ts = _pick(K, MM_ROWS)
        nk = K // ts

        def body_tn(a_ref, *refs):
            o_ref = refs[-1]
            k = pl.program_id(1)
            av = a_ref[...]
            lo = 0
            for b_ref, w in zip(refs[:-1], widths):
                part = lax.dot_general(av, b_ref[...], dims, preferred_element_type=f32)
                cols = slice(lo, lo + w)
                lo += w

                @pl.when(k == 0)
                def _(part=part, cols=cols):
                    o_ref[:, cols] = part

                @pl.when(k > 0)
                def _(part=part, cols=cols):
                    o_ref[:, cols] += part

        return pl.pallas_call(
            body_tn,
            grid=(M // tmm, nk),
            in_specs=[pl.BlockSpec((ts, tmm), lambda i, k: (k, i))]
            + [pl.BlockSpec((ts, w), lambda i, k: (k, 0)) for w in widths],
            out_specs=pl.BlockSpec((tmm, N), lambda i, k: (i, 0)),
            out_shape=SDS((M, N), out_dtype),
            compiler_params=_cparams(("parallel", "arbitrary"), VMEM_BIG),
            name=name,
        )(a, *bs)

    parts = list(a) if isinstance(a, (list, tuple)) else [a]
    assert mode == "nt" or len(parts) == 1
    widths = [t.shape[1] for t in parts]
    M = parts[0].shape[0]
    N = b.shape[1] if mode == "nn" else b.shape[0]
    tm = _pick(M, MM_ROWS)
    npart = len(parts)

    def body(*refs):
        a_refs, b_ref = refs[:npart], refs[npart]
        c_ref = refs[npart + 1] if has_acc else None
        o_ref = refs[-1]
        if npart == 1:
            part = lax.dot_general(a_refs[0][...], b_ref[...], dims, preferred_element_type=f32)
        else:
            part, lo = None, 0
            for a_ref, w in zip(a_refs, widths):
                t = lax.dot_general(a_ref[...], b_ref[:, lo:lo + w], dims, preferred_element_type=f32)
                part = t if part is None else part + t
                lo += w
        if has_acc:
            part = part + c_ref[...]
        o_ref[...] = part.astype(out_dtype)

    specs = [pl.BlockSpec((tm, w), lambda i: (i, 0)) for w in widths] + [pl.BlockSpec(b.shape, lambda i: (0, 0))]
    args = parts + [b]
    aliases = {}
    if has_acc:
        specs.append(pl.BlockSpec((tm, N), lambda i: (i, 0)))
        args.append(acc)
        aliases = {npart + 1: 0}
    if after is not None:
        specs.append(pl.BlockSpec(memory_space=pl.ANY))
        args.append(after)
    return pl.pallas_call(
        body,
        grid=(M // tm,),
        in_specs=specs,
        out_specs=pl.BlockSpec((tm, N), lambda i: (i, 0)),
        out_shape=SDS((M, N), out_dtype),
        input_output_aliases=aliases,
        compiler_params=_cparams(("parallel",), VMEM_BIG),
        name=name,
    )(*args)


PERM_ROWS = 1024


def _perm_spec(d, cols=LANE):
    return pl.BlockSpec((d, PERM_ROWS // d, cols), lambda i, j: (0, i, j))


def _to_natural(src_ref, dst_ref, d):
    n = src_ref.shape[1]
    for r in range(d):
        dst_ref[pl.ds(r, n, stride=d), :] = src_ref[r]


def _prep(x, w):
    S, D = x.shape
    R = PERM_ROWS
    nc = D // LANE

    def body(*refs):
        x_refs, w_ref = refs[:nc], refs[nc]
        h_ref, h4_ref, h16_ref, rs = refs[nc + 1:]
        ssq = None
        for xr in x_refs:
            v = xr[...]
            t = jnp.sum(v * v, axis=-1, keepdims=True)
            ssq = t if ssq is None else ssq + t
        rinv = lax.rsqrt(ssq * (1.0 / D) + EPS)
        rs[...] = jnp.broadcast_to(rinv, (R, LANE))
        for j, xr in enumerate(x_refs):
            cols = slice(j * LANE, (j + 1) * LANE)
            wj = w_ref[:, cols]
            h_ref[:, cols] = ((xr[...] * rinv) * wj).astype(bf16)
            for d, o_ref in ((4, h4_ref), (16, h16_ref)):
                n = R // d
                for r in range(d):
                    rows = pl.ds(r, n, stride=d)
                    o_ref[r, :, cols] = ((xr[rows, :] * rs[rows, :]) * wj).astype(bf16)

    col = lambda j: pl.BlockSpec((R, LANE), lambda i, j=j: (i, j))
    h, h4, h16 = pl.pallas_call(
        body,
        grid=(S // R,),
        in_specs=[col(j) for j in range(nc)] + [pl.BlockSpec((1, D), lambda i: (0, 0))],
        out_specs=[pl.BlockSpec((R, D), lambda i: (i, 0)), pl.BlockSpec((4, R // 4, D), lambda i: (0, i, 0)),
                   pl.BlockSpec((16, R // 16, D), lambda i: (0, i, 0))],
        out_shape=[SDS((S, D), bf16), SDS((4, S // 4, D), bf16), SDS((16, S // 16, D), bf16)],
        scratch_shapes=[pltpu.VMEM((R, LANE), f32)],
        compiler_params=_cparams(("parallel",), VMEM_BIG),
        name="prep_norm_perm",
    )(*([x] * nc), w)
    return [h, h4.reshape(S, D), h16.reshape(S, D)]


def _dh_sum(a, b, c):
    S, D = a.shape
    R = PERM_ROWS

    def body(a_ref, b_ref, c_ref, o_ref, sb, sc):
        _to_natural(b_ref, sb, 4)
        _to_natural(c_ref, sc, 16)
        o_ref[...] = (a_ref[...] + sb[...]) + sc[...]

    nat = pl.BlockSpec((R, LANE), lambda i, j: (i, j))
    return pl.pallas_call(
        body,
        grid=(S // R, D // LANE),
        in_specs=[nat, _perm_spec(4), _perm_spec(16)],
        out_specs=nat,
        out_shape=SDS((S, D), f32),
        scratch_shapes=[pltpu.VMEM((R, LANE), f32)] * 2,
        compiler_params=_cparams(("parallel", "parallel")),
        name="dh_sum",
    )(a, b.reshape(4, S // 4, D), c.reshape(16, S // 16, D))


def _rms_parts(xv):
    r = lax.rsqrt(jnp.mean(xv * xv, axis=-1, keepdims=True) + EPS)
    return r, xv * r


def _rms_bwd(xhat, r, w, dy):
    dyw = dy * w
    return r * (dyw - xhat * jnp.mean(dyw * xhat, axis=-1, keepdims=True))


def _mid_fwd(x, mo, w_pm, w_pf):
    S, D = x.shape
    tm = _pick(S, 512)

    def body(x_ref, mo_ref, wpm_ref, wpf_ref, x1_ref, h2_ref):
        _, moh = _rms_parts(mo_ref[...])
        x1 = x_ref[...] + moh * wpm_ref[...]
        x1_ref[...] = x1
        _, x1h = _rms_parts(x1)
        h2_ref[...] = (x1h * wpf_ref[...]).astype(bf16)

    row = pl.BlockSpec((tm, D), lambda i: (i, 0))
    vec = pl.BlockSpec((1, D), lambda i: (0, 0))
    return pl.pallas_call(
        body,
        grid=(S // tm,),
        in_specs=[row, row, vec, vec],
        out_specs=[row, row],
        out_shape=[SDS((S, D), f32), SDS((S, D), bf16)],
        compiler_params=_cparams(("parallel",)),
        name="mid_fwd",
    )(x, mo, w_pm, w_pf)


def _final(x1, fo, tgt, w_pfn):
    S, D = x1.shape
    tm = _pick(S, 512)
    nt = S // tm

    def body(x1_ref, fo_ref, t_ref, w_ref, loss_ref, dy_ref, dfo_ref, gw_ref, lacc, gacc):
        i = pl.program_id(0)

        @pl.when(i == 0)
        def _():
            lacc[...] = jnp.zeros_like(lacc)
            gacc[...] = jnp.zeros_like(gacc)

        w = w_ref[...]
        r, foh = _rms_parts(fo_ref[...])
        y = x1_ref[...] + foh * w
        err = y - t_ref[...]
        lacc[...] += _colsum8(err * err)
        dy = err * (1.0 / D)
        dy_ref[...] = dy
        gacc[...] += _colsum8(dy * foh)
        dfo_ref[...] = _rms_bwd(foh, r, w, dy).astype(bf16)

        @pl.when(i == nt - 1)
        def _():
            loss_ref[...] = jnp.full((SUBLANE, LANE), 0.5 / D, f32) * jnp.sum(lacc[...])
            gw_ref[...] = jnp.sum(gacc[...], axis=0, keepdims=True)

    row = pl.BlockSpec((tm, D), lambda i: (i, 0))
    vec = pl.BlockSpec((1, D), lambda i: (0, 0))
    return pl.pallas_call(
        body,
        grid=(nt,),
        in_specs=[row, row, row, vec],
        out_specs=[pl.BlockSpec((SUBLANE, LANE), lambda i: (0, 0)), row, row, vec],
        out_shape=[SDS((SUBLANE, LANE), f32), SDS((S, D), f32), SDS((S, D), bf16), SDS((1, D), f32)],
        scratch_shapes=[pltpu.VMEM((SUBLANE, D), f32), pltpu.VMEM((SUBLANE, D), f32)],
        compiler_params=_cparams(("arbitrary",)),
        name="final_loss",
    )(x1, fo, tgt, w_pfn)


def _mid_bwd(dy, dh2, x1, mo, w_pf, w_pm):
    S, D = dy.shape
    tm = _pick(S, 512)
    nt = S // tm

    def body(dy_ref, dh2_ref, x1_ref, mo_ref, wpf_ref, wpm_ref, dx1_ref, dmo_ref, gpf_ref, gpm_ref, apf, apm):
        i = pl.program_id(0)

        @pl.when(i == 0)
        def _():
            apf[...] = jnp.zeros_like(apf)
            apm[...] = jnp.zeros_like(apm)

        r1, x1h = _rms_parts(x1_ref[...])
        dh2 = dh2_ref[...]
        apf[...] += _colsum8(dh2 * x1h)
        dx1 = dy_ref[...] + _rms_bwd(x1h, r1, wpf_ref[...], dh2)
        dx1_ref[...] = dx1
        rm, moh = _rms_parts(mo_ref[...])
        apm[...] += _colsum8(dx1 * moh)
        dmo_ref[...] = _rms_bwd(moh, rm, wpm_ref[...], dx1).astype(bf16)

        @pl.when(i == nt - 1)
        def _():
            gpf_ref[...] = jnp.sum(apf[...], axis=0, keepdims=True)
            gpm_ref[...] = jnp.sum(apm[...], axis=0, keepdims=True)

    row = pl.BlockSpec((tm, D), lambda i: (i, 0))
    vec = pl.BlockSpec((1, D), lambda i: (0, 0))
    return pl.pallas_call(
        body,
        grid=(nt,),
        in_specs=[row, row, row, row, vec, vec],
        out_specs=[row, row, vec, vec],
        out_shape=[SDS((S, D), f32), SDS((S, D), bf16), SDS((1, D), f32), SDS((1, D), f32)],
        scratch_shapes=[pltpu.VMEM((SUBLANE, D), f32), pltpu.VMEM((SUBLANE, D), f32)],
        compiler_params=_cparams(("arbitrary",)),
        name="mid_bwd",
    )(dy, dh2, x1, mo, w_pf, w_pm)


def _first_bwd(x, dx1, dh, w_pre):
    S, D = x.shape
    tm = _pick(S, 512)
    nt = S // tm

    def body(x_ref, dx1_ref, a_ref, w_ref, gx_ref, gw_ref, acc):
        i = pl.program_id(0)

        @pl.when(i == 0)
        def _():
            acc[...] = jnp.zeros_like(acc)

        r, xh = _rms_parts(x_ref[...])
        dh = a_ref[...]
        acc[...] += _colsum8(dh * xh)
        gx_ref[...] = dx1_ref[...] + _rms_bwd(xh, r, w_ref[...], dh)

        @pl.when(i == nt - 1)
        def _():
            gw_ref[...] = jnp.sum(acc[...], axis=0, keepdims=True)

    row = pl.BlockSpec((tm, D), lambda i: (i, 0))
    vec = pl.BlockSpec((1, D), lambda i: (0, 0))
    return pl.pallas_call(
        body,
        grid=(nt,),
        in_specs=[row, row, row, vec],
        out_specs=[row, vec],
        out_shape=[SDS((S, D), f32), SDS((1, D), f32)],
        scratch_shapes=[pltpu.VMEM((SUBLANE, D), f32)],
        compiler_params=_cparams(("arbitrary",)),
        name="first_bwd",
    )(x, dx1, dh, w_pre)


def _t5_bucket(dist):
    n = jnp.maximum(dist, 0)
    nf = jnp.maximum(n, 1).astype(f32)
    large = MAX_EXACT + (jnp.log(nf / MAX_EXACT) / math.log(MAX_DISTANCE / MAX_EXACT)
                         * (NUM_BUCKETS - MAX_EXACT)).astype(jnp.int32)
    large = jnp.minimum(large, NUM_BUCKETS - 1)
    return jnp.where(n < MAX_EXACT, n, large)


def _bias_consts(d):
    blk = ATTN_BLOCK
    rel = jnp.arange(blk)[:, None] + blk - jnp.arange(2 * blk)[None, :]
    in_win = (rel >= 0) & (rel <= blk)
    bucket = _t5_bucket(rel * d).reshape(1, -1)
    onehot = (bucket == jnp.arange(NUM_BUCKETS)[:, None]).astype(f32)
    return onehot, in_win.astype(f32).reshape(1, -1)


def _bias_build(tab_t, onehot, maskf, name):
    H = tab_t.shape[0]

    def body(t_ref, oh_ref, m_ref, o_ref):
        b = jnp.dot(t_ref[...], oh_ref[...], precision=HIGHEST, preferred_element_type=f32)
        o_ref[...] = jnp.where(m_ref[...] > 0.5, b, NEG_INF)

    return pl.pallas_call(body, out_shape=SDS((H, onehot.shape[1]), f32), name=name)(tab_t, onehot, maskf)


def _bias_grad(dbias_flat, onehot, name):
    H = dbias_flat.shape[0]

    def body(g_ref, oh_ref, o_ref):
        o_ref[...] = lax.dot_general(oh_ref[...], g_ref[...], NT, precision=HIGHEST, preferred_element_type=f32)

    return pl.pallas_call(body, out_shape=SDS((NUM_BUCKETS, H), f32), name=name)(dbias_flat, onehot)


ATTN_TILE = 512
ATTN_SUB = ATTN_TILE // ATTN_BLOCK


def _qkv_specs(nt):
    tile = (ATTN_TILE, LANE)
    blk = (ATTN_BLOCK, LANE)
    cur = lambda off: (lambda h, t: (jnp.minimum(t, nt - 1), off + h))
    prev = lambda off: (lambda h, t: (jnp.maximum(jnp.minimum(t, nt - 1) * ATTN_SUB - 1, 0), off + h))
    return [pl.BlockSpec(tile, cur(0)), pl.BlockSpec(blk, prev(4)), pl.BlockSpec(tile, cur(4)),
            pl.BlockSpec(blk, prev(8)), pl.BlockSpec(tile, cur(8))]


def _head_masks():
    lane = lax.broadcasted_iota(jnp.int32, (ATTN_BLOCK, LANE), 1)
    return lane < HEAD_DIM


def _attn_fwd(qkv, bias, bps, name):
    S = qkv.shape[0]
    nt = S // ATTN_TILE
    scale = HEAD_DIM ** -0.5

    def body(q_ref, kp_ref, kc_ref, vp_ref, vc_ref, b_ref, o_ref, l_ref):
        t = pl.program_id(1)
        kk = jnp.concatenate([kp_ref[...], kc_ref[...]], axis=0)
        vv = jnp.concatenate([vp_ref[...], vc_ref[...]], axis=0)
        low = _head_masks()
        col = lax.broadcasted_iota(jnp.int32, (ATTN_BLOCK, 2 * ATTN_BLOCK), 1)
        for b in range(ATTN_SUB):
            lo = b * ATTN_BLOCK
            rows = slice(lo, lo + ATTN_BLOCK)
            keys = slice(lo, lo + 2 * ATTN_BLOCK)
            dead = jnp.logical_and((t * ATTN_SUB + b) % bps == 0, col < ATTN_BLOCK)
            q2 = q_ref[rows, :]
            kb, vb = kk[keys], vv[keys]
            outs, lses = [], []
            for h in range(2):
                hm = low if h == 0 else jnp.logical_not(low)
                qh = jnp.where(hm, q2, jnp.zeros_like(q2))
                s = lax.dot_general(qh, kb, NT, preferred_element_type=f32) * scale + b_ref[h]
                s = jnp.where(dead, NEG_INF, s)
                m = jnp.max(s, axis=-1, keepdims=True)
                p = jnp.exp(s - m)
                l = jnp.sum(p, axis=-1, keepdims=True)
                outs.append(jnp.dot(p.astype(bf16), vb, preferred_element_type=f32) / l)
                lses.append(m + jnp.log(l))
            o_ref[rows, :] = jnp.where(low, outs[0], outs[1])
            l_ref[rows, :] = jnp.where(low, lses[0], lses[1])

    tile = pl.BlockSpec((ATTN_TILE, LANE), lambda h, t: (t, h))
    return pl.pallas_call(
        body,
        grid=(4, nt),
        in_specs=_qkv_specs(nt) + [pl.BlockSpec((2, ATTN_BLOCK, 2 * ATTN_BLOCK), lambda h, t: (h, 0, 0))],
        out_specs=[tile, tile],
        out_shape=[SDS((S, ATTN_OUT), f32), SDS((S, ATTN_OUT), f32)],
        compiler_params=_cparams(("parallel", "parallel")),
        name=name,
    )(qkv, qkv, qkv, qkv, qkv, bias)


def _attn_bwd(qkv, bias, do, dvec, lse, bps, name):
    S = qkv.shape[0]
    nt = S // ATTN_TILE
    scale = HEAD_DIM ** -0.5

    def assemble(parts):
        rows = [parts[0][:ATTN_BLOCK]]
        for b in range(ATTN_SUB - 1):
            rows.append(parts[b][ATTN_BLOCK:] + parts[b + 1][:ATTN_BLOCK])
        rows.append(parts[-1][ATTN_BLOCK:])
        return rows

    def body(q_ref, kp_ref, kc_ref, vp_ref, vc_ref, b_ref, do_ref, dvec_ref, lse_ref,
             dq_ref, dk_ref, dv_ref, db_ref, ck, cv):
        t = pl.program_id(1)
        last = ATTN_TILE - ATTN_BLOCK

        @pl.when(t == 0)
        def _():
            ck[...] = jnp.zeros_like(ck)
            cv[...] = jnp.zeros_like(cv)
            db_ref[...] = jnp.zeros_like(db_ref)

        @pl.when(t < nt)
        def _():
            kk = jnp.concatenate([kp_ref[...], kc_ref[...]], axis=0)
            vv = jnp.concatenate([vp_ref[...], vc_ref[...]], axis=0)
            low = _head_masks()
            col = lax.broadcasted_iota(jnp.int32, (ATTN_BLOCK, 2 * ATTN_BLOCK), 1)
            low2 = lax.broadcasted_iota(jnp.int32, (2 * ATTN_BLOCK, LANE), 1) < HEAD_DIM
            dk_parts, dv_parts = [], []
            dsum = [None, None]
            for b in range(ATTN_SUB):
                lo = b * ATTN_BLOCK
                rows = slice(lo, lo + ATTN_BLOCK)
                keys = slice(lo, lo + 2 * ATTN_BLOCK)
                dead = jnp.logical_and((t * ATTN_SUB + b) % bps == 0, col < ATTN_BLOCK)
                q2 = q_ref[rows, :]
                kb, vb = kk[keys], vv[keys]
                do2 = do_ref[rows, :].astype(bf16)
                dvec2 = dvec_ref[rows, :]
                lse2 = lse_ref[rows, :]
                dqs, dks, dvs = [], [], []
                for h in range(2):
                    hm = low if h == 0 else jnp.logical_not(low)
                    c0 = h * HEAD_DIM
                    qh = jnp.where(hm, q2, jnp.zeros_like(q2))
                    doh = jnp.where(hm, do2, jnp.zeros_like(do2))
                    s = lax.dot_general(qh, kb, NT, preferred_element_type=f32) * scale + b_ref[h]
                    s = jnp.where(dead, NEG_INF, s)
                    p = jnp.exp(s - lse2[:, c0:c0 + 1])
                    dp = lax.dot_general(doh, vb, NT, preferred_element_type=f32)
                    ds = p * (dp - dvec2[:, c0:c0 + 1])
                    dsum[h] = ds if dsum[h] is None else dsum[h] + ds
                    dsb = ds.astype(bf16)
                    dqs.append(jnp.dot(dsb, kb, preferred_element_type=f32) * scale)
                    dks.append(lax.dot_general(dsb, q2, TN, preferred_element_type=f32) * scale)
                    dvs.append(lax.dot_general(p.astype(bf16), do2, TN, preferred_element_type=f32))
                dq_ref[rows, :] = jnp.where(low, dqs[0], dqs[1]).astype(bf16)
                dk_parts.append(jnp.where(low2, dks[0], dks[1]))
                dv_parts.append(jnp.where(low2, dvs[0], dvs[1]))
            db_ref[0] += dsum[0]
            db_ref[1] += dsum[1]
            for parts, carry, out_ref in ((dk_parts, ck, dk_ref), (dv_parts, cv, dv_ref)):
                rws = assemble(parts)
                out_ref[:last, :] = carry[:last, :].astype(bf16)
                out_ref[last:, :] = (carry[last:, :] + rws[0]).astype(bf16)
                for b in range(ATTN_SUB):
                    carry[b * ATTN_BLOCK:(b + 1) * ATTN_BLOCK, :] = rws[b + 1]

        @pl.when(t == nt)
        def _():
            dk_ref[...] = ck[...].astype(bf16)
            dv_ref[...] = cv[...].astype(bf16)

    tile = (ATTN_TILE, LANE)
    cur = pl.BlockSpec(tile, lambda h, t: (jnp.minimum(t, nt - 1), h))
    lag = pl.BlockSpec(tile, lambda h, t: (jnp.maximum(t - 1, 0), h))
    bspec = pl.BlockSpec((2, ATTN_BLOCK, 2 * ATTN_BLOCK), lambda h, t: (h, 0, 0))
    return pl.pallas_call(
        body,
        grid=(4, nt + 1),
        in_specs=_qkv_specs(nt) + [bspec, cur, cur, cur],
        out_specs=[cur, lag, lag, bspec],
        out_shape=[SDS((S, ATTN_OUT), bf16), SDS((S, ATTN_OUT), bf16), SDS((S, ATTN_OUT), bf16),
                   SDS((8, ATTN_BLOCK, 2 * ATTN_BLOCK), f32)],
        scratch_shapes=[pltpu.VMEM(tile, f32), pltpu.VMEM(tile, f32)],
        compiler_params=_cparams(("parallel", "arbitrary")),
        name=name,
    )(qkv, qkv, qkv, qkv, qkv, bias, do, dvec, lse)


def _attn_merge(o0, o1, o2, l0, l1, l2):
    S, W = o0.shape
    R = PERM_ROWS

    def body(o0_ref, o1_ref, o2_ref, l0_ref, l1_ref, l2_ref, y_ref, yb_ref, w0_ref, w1_ref, w2_ref,
             so1, so2, sl1, sl2):
        _to_natural(o1_ref, so1, 4)
        _to_natural(l1_ref, sl1, 4)
        _to_natural(o2_ref, so2, 16)
        _to_natural(l2_ref, sl2, 16)
        a, b, c = l0_ref[...], sl1[...], sl2[...]
        m = jnp.maximum(jnp.maximum(a, b), c)
        ea, eb, ec = jnp.exp(a - m), jnp.exp(b - m), jnp.exp(c - m)
        den = (ea + eb) + ec
        w0, w1, w2 = ea / den, eb / den, ec / den
        y = (w0 * o0_ref[...] + w1 * so1[...]) + w2 * so2[...]
        y_ref[...] = y
        yb_ref[...] = y.astype(bf16)
        w0_ref[...] = w0
        w1_ref[...] = w1
        w2_ref[...] = w2

    nat = pl.BlockSpec((R, LANE), lambda i, j: (i, j))
    v4 = lambda t: t.reshape(4, S // 4, W)
    v16 = lambda t: t.reshape(16, S // 16, W)
    return pl.pallas_call(
        body,
        grid=(S // R, W // LANE),
        in_specs=[nat, _perm_spec(4), _perm_spec(16)] * 2,
        out_specs=[nat] * 5,
        out_shape=[SDS((S, W), f32), SDS((S, W), bf16)] + [SDS((S, W), f32)] * 3,
        scratch_shapes=[pltpu.VMEM((R, LANE), f32)] * 4,
        compiler_params=_cparams(("parallel", "parallel")),
        name="attn_merge",
    )(o0, v4(o1), v16(o2), l0, v4(l1), v16(l2))


def _attn_merge_bwd(dy, y, w0, w1, w2):
    S, W = dy.shape
    R = PERM_ROWS

    def body(dy_ref, y_ref, w0_ref, w1_ref, w2_ref, a0, a1, a2, b0, b1, b2, sa, sb):
        dyv = dy_ref[...]
        r = lax.broadcasted_iota(jnp.int32, (LANE, LANE), 0) // HEAD_DIM
        c = lax.broadcasted_iota(jnp.int32, (LANE, LANE), 1) // HEAD_DIM
        seg = jnp.where(r == c, 1.0, 0.0).astype(f32)
        cbar = jnp.dot(dyv * y_ref[...], seg, precision=HIGHEST, preferred_element_type=f32)
        w = w0_ref[...]
        a0[...] = (w * dyv).astype(bf16)
        b0[...] = w * cbar
        for d, w_ref, a_ref, b_ref in ((4, w1_ref, a1, b1), (16, w2_ref, a2, b2)):
            w = w_ref[...]
            sa[...] = w * dyv
            sb[...] = w * cbar
            n = R // d
            for k in range(d):
                rows = pl.ds(k, n, stride=d)
                a_ref[k] = sa[rows, :].astype(bf16)
                b_ref[k] = sb[rows, :]

    nat = pl.BlockSpec((R, LANE), lambda i, j: (i, j))
    shapes = lambda dt: [SDS((S, W), dt), SDS((4, S // 4, W), dt), SDS((16, S // 16, W), dt)]
    outs = pl.pallas_call(
        body,
        grid=(S // R, W // LANE),
        in_specs=[nat] * 5,
        out_specs=[nat, _perm_spec(4), _perm_spec(16)] * 2,
        out_shape=shapes(bf16) + shapes(f32),
        scratch_shapes=[pltpu.VMEM((R, LANE), f32)] * 2,
        compiler_params=_cparams(("parallel", "parallel")),
        name="attn_merge_bwd",
    )(dy, y, w0, w1, w2)
    return [t.reshape(S, W) for t in outs]


HGRN_SB = 256


def _chunk_masks():
    r = jnp.arange(HGRN_SB)[:, None]
    c = jnp.arange(HGRN_SB)[None, :]
    same = (r // HGRN_CHUNK) == (c // HGRN_CHUNK)
    return jnp.stack([same & (c <= r), same, same & (c >= r)]).astype(bf16)


def _mask_dot(mask, x):
    hi = x.astype(bf16)
    r1 = x - hi.astype(f32)
    mid = r1.astype(bf16)
    lo = (r1 - mid.astype(f32)).astype(bf16)
    p = jnp.dot(mask, jnp.concatenate([hi, mid, lo], axis=1), preferred_element_type=f32)
    n = x.shape[1]
    return (p[:, :n] + p[:, n:2 * n]) + p[:, 2 * n:]


def _hgrn_prep(q_raw, f_raw, lbv, tril, same):
    sq = _sigmoid(q_raw)
    qs = q_raw * sq
    sig = _sigmoid(f_raw)
    f = lbv + (1.0 - lbv) * sig
    g = jnp.log(f)
    k = 1.0 - f
    G = _mask_dot(tril, g)
    GL = _mask_dot(same, g)
    eG = jnp.exp(G)
    einv = jnp.exp(-G)
    edec = jnp.exp(GL - G)
    return dict(sq=sq, qs=qs, sig=sig, f=f, k=k, eG=eG, einv=einv, edec=edec, eGL=jnp.exp(GL),
                qt=qs * eG, kt=k * einv, kd=k * edec)


def _ride_split(ride, rest, n_out, n_scratch):
    if ride is None:
        return None, rest[:n_out], None, rest[n_out:], None
    return rest[0], rest[1:1 + n_out], rest[1 + n_out], rest[2 + n_out:2 + n_out + n_scratch], rest[2 + n_out + n_scratch:]


def _hgrn_fwd(hg, lb, normw, ride=None):
    S = hg.shape[0]
    sb = HGRN_SB
    nsb = S // sb
    nch = sb // HGRN_CHUNK

    def body(q_ref, f_ref, v_ref, og_ref, lb_ref, nw_ref, m_ref, *rest):
        src_ref, (y_ref, o_ref, ck_ref), got_ref, (st,), sems = _ride_split(ride, rest, 3, 1)
        j = pl.program_id(1)
        if ride is not None:
            @pl.when(jnp.logical_and(pl.program_id(0) == 0, j == 0))
            def _():
                _chip_start(src_ref, got_ref, sems[0], sems[1], ride[1])

        @pl.when(j == 0)
        def _():
            st[...] = jnp.zeros_like(st)

        ST = st[...]
        ck_ref[0, 0] = ST
        tril_m = m_ref[0]
        tril = tril_m.astype(f32) > 0.5
        pr = _hgrn_prep(q_ref[...], f_ref[...], lb_ref[...], tril_m, m_ref[1])
        qtb, ktb, kdb = pr["qt"].astype(bf16), pr["kt"].astype(bf16), pr["kd"].astype(bf16)
        eGL = pr["eGL"]
        vb = v_ref[...].astype(bf16)
        A = jnp.where(tril, lax.dot_general(qtb, ktb, NT, preferred_element_type=f32), 0.0)
        o = jnp.dot(A.astype(bf16), vb, preferred_element_type=f32)
        outs = []
        for ci in range(nch):
            lo = ci * HGRN_CHUNK
            sl = slice(lo, lo + HGRN_CHUNK)
            outs.append(o[sl] + lax.dot_general(qtb[sl], ST.astype(bf16), NT, preferred_element_type=f32))
            ST = ST * eGL[lo:lo + 1, :] + lax.dot_general(vb[sl], kdb[sl], TN, preferred_element_type=f32)
        st[...] = ST
        of = jnp.concatenate(outs, axis=0)
        o_ref[...] = of
        rms = lax.rsqrt(jnp.mean(of * of, axis=-1, keepdims=True) + EPS)
        ogv = og_ref[...]
        y_ref[...] = ((of * rms * nw_ref[...]) * (ogv * _sigmoid(ogv))).astype(bf16)

        if ride is not None:
            @pl.when(jnp.logical_and(pl.program_id(0) == 3, j == nsb - 1))
            def _():
                _chip_finish(src_ref, got_ref, sems[0], sems[1], ride[1])

    col = lambda off: pl.BlockSpec((sb, LANE), lambda h, j: (j, off + h))
    riding = ride is not None
    res = pl.pallas_call(
        body,
        grid=(4, nsb),
        in_specs=[col(0), col(4), col(8), col(12), pl.BlockSpec((1, LANE), lambda h, j: (0, h)),
                  pl.BlockSpec((1, LANE), lambda h, j: (0, 0)),
                  pl.BlockSpec((3, sb, sb), lambda h, j: (0, 0, 0))] + ([_ANY] if riding else []),
        out_specs=[col(0), col(0), pl.BlockSpec((1, 1, LANE, LANE), lambda h, j: (h, j, 0, 0))]
        + ([_ANY] if riding else []),
        out_shape=[SDS((S, HGRN_W), bf16), SDS((S, HGRN_W), f32), SDS((4, nsb, LANE, LANE), f32)]
        + ([_chip_out_shape(*ride)] if riding else []),
        scratch_shapes=[pltpu.VMEM((LANE, LANE), f32)] + (list(_CHIP_SEMS) if riding else []),
        compiler_params=_cparams(("arbitrary", "arbitrary") if riding else ("parallel", "arbitrary")),
        name="hgrn_fwd",
    )(hg, hg, hg, hg, lb, normw, _chunk_masks(), *([ride[0]] if riding else []))
    return tuple(res) if riding else (*res, None)


def _hgrn_bwd(hg, o_raw, dy, ck, lb, normw, ride=None):
    S = hg.shape[0]
    sb = HGRN_SB
    nsb = S // sb
    nch = sb // HGRN_CHUNK

    def body(q_ref, f_ref, v_ref, og_ref, o_ref, dy_ref, ck_ref, lb_ref, nw_ref, m_ref, *rest):
        src_ref, outs, got_ref, (dst, alb, anw), sems = _ride_split(ride, rest, 6, 3)
        dq_ref, df_ref, dv_ref, dog_ref, glb_ref, gnw_ref = outs
        j = pl.program_id(1)
        if ride is not None:
            @pl.when(jnp.logical_and(pl.program_id(0) == 0, j == 0))
            def _():
                _chip_start(src_ref, got_ref, sems[0], sems[1], ride[1])

        @pl.when(j == 0)
        def _():
            dst[...] = jnp.zeros_like(dst)
            alb[...] = jnp.zeros_like(alb)
            anw[...] = jnp.zeros_like(anw)

        tril_m = m_ref[0]
        tril = tril_m.astype(f32) > 0.5
        lbv = lb_ref[...]
        q_raw = q_ref[...]
        pr = _hgrn_prep(q_raw, f_ref[...], lbv, tril_m, m_ref[1])
        qt, kt, kd, eGL = pr["qt"], pr["kt"], pr["kd"], pr["eGL"]
        qtb, ktb, kdb = qt.astype(bf16), kt.astype(bf16), kd.astype(bf16)
        vb = v_ref[...].astype(bf16)

        o = o_ref[...]
        ogv = og_ref[...]
        sog = _sigmoid(ogv)
        rms = lax.rsqrt(jnp.mean(o * o, axis=-1, keepdims=True) + EPS)
        oh = o * rms
        nw = nw_ref[...]
        dyv = dy_ref[...]
        dog_ref[...] = (dyv * (oh * nw) * (sog * (1.0 + ogv * (1.0 - sog)))).astype(bf16)
        dohw = dyv * (ogv * sog)
        anw[...] += _colsum8(dohw * oh)
        doh = dohw * nw
        do = rms * (doh - oh * jnp.mean(doh * oh, axis=-1, keepdims=True))
        dob = do.astype(bf16)

        Ab = jnp.where(tril, lax.dot_general(qtb, ktb, NT, preferred_element_type=f32), 0.0).astype(bf16)
        dAb = jnp.where(tril, lax.dot_general(dob, vb, NT, preferred_element_type=f32), 0.0).astype(bf16)
        dv_acc = lax.dot_general(Ab, dob, TN, preferred_element_type=f32)
        dqt = jnp.dot(dAb, ktb, preferred_element_type=f32)
        dkt = lax.dot_general(dAb, qtb, TN, preferred_element_type=f32)

        ST = ck_ref[0, 0]
        states = []
        for ci in range(nch):
            lo = ci * HGRN_CHUNK
            sl = slice(lo, lo + HGRN_CHUNK)
            states.append(ST)
            ST = ST * eGL[lo:lo + 1, :] + lax.dot_general(vb[sl], kdb[sl], TN, preferred_element_type=f32)

        dST = dst[...]
        dqt_i, dkd_i, dv_i, deg_i = [None] * nch, [None] * nch, [None] * nch, [None] * nch
        for ci in reversed(range(nch)):
            lo = ci * HGRN_CHUNK
            sl = slice(lo, lo + HGRN_CHUNK)
            ST0 = states[ci]
            dSTb = dST.astype(bf16)
            dv_i[ci] = lax.dot_general(kdb[sl], dSTb, NT, preferred_element_type=f32)
            dqt_i[ci] = jnp.dot(dob[sl], ST0.astype(bf16), preferred_element_type=f32)
            dkd_i[ci] = jnp.dot(vb[sl], dSTb, preferred_element_type=f32)
            deg_i[ci] = jnp.broadcast_to(jnp.sum(dST * ST0, axis=0, keepdims=True), (HGRN_CHUNK, LANE))
            dST = dST * eGL[lo:lo + 1, :] + lax.dot_general(dob[sl], qtb[sl], TN, preferred_element_type=f32)
        dst[...] = dST

        dqt = dqt + jnp.concatenate(dqt_i, axis=0)
        dkd = jnp.concatenate(dkd_i, axis=0)
        dv_ref[...] = (dv_acc + jnp.concatenate(dv_i, axis=0)).astype(bf16)
        deg = jnp.concatenate(deg_i, axis=0)

        dqs = dqt * pr["eG"]
        dkdkd = dkd * kd
        dG = dqt * qt - dkt * kt - dkdkd
        dk = dkt * pr["einv"] + dkd * pr["edec"]
        dGL = _mask_dot(m_ref[1], dkdkd) + eGL * deg
        dg = _mask_dot(m_ref[2], dG) + dGL
        df = dg / pr["f"] - dk
        sig = pr["sig"]
        df_ref[...] = (df * (1.0 - lbv) * (sig * (1.0 - sig))).astype(bf16)
        alb[...] += _colsum8(df * (1.0 - sig))
        sq = pr["sq"]
        dq_ref[...] = (dqs * (sq * (1.0 + q_raw * (1.0 - sq)))).astype(bf16)

        @pl.when(j == nsb - 1)
        def _():
            glb_ref[...] = jnp.broadcast_to(jnp.sum(alb[...], axis=0, keepdims=True), (SUBLANE, LANE))
            gnw_ref[...] = jnp.broadcast_to(jnp.sum(anw[...], axis=0, keepdims=True), (SUBLANE, LANE))

        if ride is not None:
            @pl.when(jnp.logical_and(pl.program_id(0) == 3, j == nsb - 1))
            def _():
                _chip_finish(src_ref, got_ref, sems[0], sems[1], ride[1])

    rev = lambda off: pl.BlockSpec((sb, LANE), lambda h, j: (nsb - 1 - j, off + h))
    stat = pl.BlockSpec((SUBLANE, LANE), lambda h, j: (0, h))
    riding = ride is not None
    res = pl.pallas_call(
        body,
        grid=(4, nsb),
        in_specs=[rev(0), rev(4), rev(8), rev(12), rev(0), rev(0),
                  pl.BlockSpec((1, 1, LANE, LANE), lambda h, j: (h, nsb - 1 - j, 0, 0)),
                  pl.BlockSpec((1, LANE), lambda h, j: (0, h)), pl.BlockSpec((1, LANE), lambda h, j: (0, 0)),
                  pl.BlockSpec((3, sb, sb), lambda h, j: (0, 0, 0))]
        + ([_ANY] if riding else []),
        out_specs=[rev(0), rev(0), rev(0), rev(0), stat, stat] + ([_ANY] if riding else []),
        out_shape=[SDS((S, HGRN_W), bf16)] * 4 + [SDS((SUBLANE, HGRN_W), f32)] * 2
        + ([_chip_out_shape(*ride)] if riding else []),
        scratch_shapes=[pltpu.VMEM((LANE, LANE), f32), pltpu.VMEM((SUBLANE, LANE), f32),
                        pltpu.VMEM((SUBLANE, LANE), f32)] + (list(_CHIP_SEMS) if riding else []),
        compiler_params=_cparams(("arbitrary", "arbitrary") if riding else ("parallel", "arbitrary")),
        name="hgrn_bwd",
    )(hg, hg, hg, hg, o_raw, dy, ck, lb, normw, _chunk_masks(), *([ride[0]] if riding else []))
    return tuple(res) if riding else (*res, None)


def _lb_fwd(raw):
    def body(r_ref, o_ref):
        r = r_ref[...]
        m = jnp.max(r, axis=0, keepdims=True)
        e = jnp.exp(r - m)
        o_ref[...] = (e / jnp.sum(e, axis=0, keepdims=True))[0:1]

    return pl.pallas_call(body, out_shape=SDS((1, raw.shape[1]), f32), name="lb_fwd")(raw)


def _lb_bwd(raw, dlb):
    def body(r_ref, d_ref, o_ref):
        r = r_ref[...]
        m = jnp.max(r, axis=0, keepdims=True)
        e = jnp.exp(r - m)
        s = e / jnp.sum(e, axis=0, keepdims=True)
        s0 = s[0:1]
        onehot0 = jnp.where(lax.broadcasted_iota(jnp.int32, r.shape, 0) == 0, 1.0, 0.0)
        o_ref[...] = d_ref[...] * s0 * (onehot0 - s)

    return pl.pallas_call(body, out_shape=SDS(raw.shape, f32), name="lb_bwd")(raw, dlb)


def _gate_fwd(a, b, gc):
    S, D = a.shape
    tm = _pick(S, 512)

    def body(a_ref, b_ref, g0_ref, g1_ref, o_ref):
        s0, s1 = _sigmoid(g0_ref[...].astype(f32)), _sigmoid(g1_ref[...].astype(f32))
        o_ref[...] = (s0 * a_ref[...].astype(f32) + s1 * b_ref[...].astype(f32)).astype(bf16)

    row = pl.BlockSpec((tm, D), lambda i: (i, 0))
    return pl.pallas_call(
        body,
        grid=(S // tm,),
        in_specs=[row, row, row, pl.BlockSpec((tm, D), lambda i: (i, 1))],
        out_specs=row,
        out_shape=SDS((S, D), bf16),
        compiler_params=_cparams(("parallel",)),
        name="gate_fwd",
    )(a, b, gc, gc)


def _gate_bwd(dm, a, b, gc):
    S, D = a.shape
    tm = _pick(S, 512)

    def body(dm_ref, a_ref, b_ref, g0_ref, g1_ref, da_ref, db_ref, dg_ref):
        dmv = dm_ref[...].astype(f32)
        s0, s1 = _sigmoid(g0_ref[...].astype(f32)), _sigmoid(g1_ref[...].astype(f32))
        da_ref[...] = (dmv * s0).astype(bf16)
        db_ref[...] = (dmv * s1).astype(bf16)
        dg_ref[:, :D] = (dmv * a_ref[...].astype(f32) * (s0 * (1.0 - s0))).astype(bf16)
        dg_ref[:, D:] = (dmv * b_ref[...].astype(f32) * (s1 * (1.0 - s1))).astype(bf16)

    row = pl.BlockSpec((tm, D), lambda i: (i, 0))
    wide = pl.BlockSpec((tm, 2 * D), lambda i: (i, 0))
    return pl.pallas_call(
        body,
        grid=(S // tm,),
        in_specs=[row, row, row, row, pl.BlockSpec((tm, D), lambda i: (i, 1))],
        out_specs=[row, row, wide],
        out_shape=[SDS((S, D), bf16), SDS((S, D), bf16), SDS((S, 2 * D), bf16)],
        compiler_params=_cparams(("parallel",)),
        name="gate_bwd",
    )(dm, a, b, gc, gc)


CONV_ROWS = 512
INV_SQRT2 = 0.7071067811865476
INV_SQRT_2PI = 0.3989422804014327


CONV_HALO = 16


def _tile8(a, rows):
    return jnp.tile(a, (rows // a.shape[0], 1))


def _conv_rows(u_ref, w, b, r0, first):
    R = CONV_ROWS
    cur = u_ref[pl.ds(r0, R), :].astype(f32)
    prev8 = u_ref[pl.ds(pl.multiple_of(jnp.maximum(r0 - CONV_HALO, 0), CONV_HALO), CONV_HALO), :].astype(f32)
    prev8 = jnp.where(first, 0.0, prev8)
    row = lax.broadcasted_iota(jnp.int32, (R, LANE), 0)
    x1 = jnp.where(row < 1, _tile8(pltpu.roll(prev8, 1, 0), R), pltpu.roll(cur, 1, 0))
    x2 = jnp.where(row < 2, _tile8(pltpu.roll(prev8, 2, 0), R), pltpu.roll(cur, 2, 0))
    c = ((b + w[0:1] * x2) + w[1:2] * x1) + w[2:3] * cur
    return c, x2, x1, cur


def _conv_fwd(ug, uv, wg, wv, bg, bv):
    S, F = ug.shape
    nchunk = S // CONV_ROWS

    def body(ug_ref, uv_ref, wg_ref, wv_ref, bg_ref, bv_ref, o_ref):
        wgv, wvv, bgv, bvv = wg_ref[...], wv_ref[...], bg_ref[...], bv_ref[...]

        def step(ci, carry):
            r0 = pl.multiple_of(ci * CONV_ROWS, CONV_ROWS)
            cg = _conv_rows(ug_ref, wgv, bgv, r0, ci == 0)[0]
            cv = _conv_rows(uv_ref, wvv, bvv, r0, ci == 0)[0]
            gelu = 0.5 * cg * (1.0 + lax.erf(cg * INV_SQRT2))
            o_ref[pl.ds(r0, CONV_ROWS), :] = (gelu * cv).astype(bf16)
            return carry

        lax.fori_loop(0, nchunk, step, 0)

    col = pl.BlockSpec((S, LANE), lambda j: (0, j))
    w3 = pl.BlockSpec((3, LANE), lambda j: (0, j))
    b1 = pl.BlockSpec((1, LANE), lambda j: (0, j))
    return pl.pallas_call(
        body,
        grid=(F // LANE,),
        in_specs=[col, col, w3, w3, b1, b1],
        out_specs=col,
        out_shape=SDS((S, F), bf16),
        compiler_params=_cparams(("parallel",), VMEM_BIG),
        name="conv_fwd",
    )(ug, uv, wg, wv, bg, bv)


def _conv_bwd(ug, uv, dact, wg, wv, bg, bv):
    S, F = ug.shape
    R = CONV_ROWS
    nchunk = S // R

    def body(ug_ref, uv_ref, da_ref, wg_ref, wv_ref, bg_ref, bv_ref, dug_ref, duv_ref, sg_ref, sv_ref, dcg, dcv):
        wgv, wvv, bgv, bvv = wg_ref[...], wv_ref[...], bg_ref[...], bv_ref[...]
        zero = jnp.zeros((SUBLANE, LANE), f32)

        def fwd_step(ci, acc):
            r0 = pl.multiple_of(ci * R, R)
            cg, g2, g1, g0 = _conv_rows(ug_ref, wgv, bgv, r0, ci == 0)
            cv, v2, v1, v0 = _conv_rows(uv_ref, wvv, bvv, r0, ci == 0)
            da = da_ref[pl.ds(r0, R), :].astype(f32)
            cdf = 0.5 * (1.0 + lax.erf(cg * INV_SQRT2))
            pdf = INV_SQRT_2PI * jnp.exp(-0.5 * cg * cg)
            dg = da * cv * (cdf + cg * pdf)
            dv = da * (cg * cdf)
            dcg[pl.ds(r0, R), :] = dg
            dcv[pl.ds(r0, R), :] = dv
            new = (acc[0] + _colsum8(dg * g2), acc[1] + _colsum8(dg * g1), acc[2] + _colsum8(dg * g0),
                   acc[3] + _colsum8(dg),
                   acc[4] + _colsum8(dv * v2), acc[5] + _colsum8(dv * v1), acc[6] + _colsum8(dv * v0),
                   acc[7] + _colsum8(dv))
            return new

        acc = lax.fori_loop(0, nchunk, fwd_step, (zero,) * 8)
        rows = lax.broadcasted_iota(jnp.int32, (SUBLANE, LANE), 0)

        def stats(parts):
            out = jnp.zeros((SUBLANE, LANE), f32)
            for k, pt in enumerate(parts):
                out = jnp.where(rows == k, jnp.sum(pt, axis=0, keepdims=True), out)
            return out

        sg_ref[...] = stats(acc[0:4])
        sv_ref[...] = stats(acc[4:8])

        def du_rows(dc, w, r0, last):
            cur = dc[pl.ds(r0, R), :]
            nxt = dc[pl.ds(pl.multiple_of(jnp.minimum(r0 + R, S - SUBLANE), SUBLANE), SUBLANE), :]
            nxt = jnp.where(last, 0.0, nxt)
            row = lax.broadcasted_iota(jnp.int32, (R, LANE), 0)
            y1 = jnp.where(row >= R - 1, _tile8(pltpu.roll(nxt, SUBLANE - 1, 0), R), pltpu.roll(cur, R - 1, 0))
            y2 = jnp.where(row >= R - 2, _tile8(pltpu.roll(nxt, SUBLANE - 2, 0), R), pltpu.roll(cur, R - 2, 0))
            return w[2:3] * cur + w[1:2] * y1 + w[0:1] * y2

        def bwd_step(ci, carry):
            r0 = pl.multiple_of(ci * R, R)
            last = ci == nchunk - 1
            dug_ref[pl.ds(r0, R), :] = du_rows(dcg, wgv, r0, last).astype(bf16)
            duv_ref[pl.ds(r0, R), :] = du_rows(dcv, wvv, r0, last).astype(bf16)
            return carry

        lax.fori_loop(0, nchunk, bwd_step, 0)

    col = pl.BlockSpec((S, LANE), lambda j: (0, j))
    w3 = pl.BlockSpec((3, LANE), lambda j: (0, j))
    b1 = pl.BlockSpec((1, LANE), lambda j: (0, j))
    st = pl.BlockSpec((SUBLANE, LANE), lambda j: (0, j))
    return pl.pallas_call(
        body,
        grid=(F // LANE,),
        in_specs=[col, col, col, w3, w3, b1, b1],
        out_specs=[col, col, st, st],
        out_shape=[SDS((S, F), bf16), SDS((S, F), bf16), SDS((SUBLANE, F), f32), SDS((SUBLANE, F), f32)],
        scratch_shapes=[pltpu.VMEM((S, LANE), f32), pltpu.VMEM((S, LANE), f32)],
        compiler_params=_cparams(("parallel",), VMEM_BIG),
        name="conv_bwd",
    )(ug, uv, dact, wg, wv, bg, bv)


def _adam_math(w, g, m, v):
    m = ADAM_B1 * m + (1.0 - ADAM_B1) * g
    v = ADAM_B2 * v + (1.0 - ADAM_B2) * (g * g)
    m_hat = m / (1.0 - ADAM_B1 ** ADAM_STEP)
    v_hat = v / (1.0 - ADAM_B2 ** ADAM_STEP)
    delta = -ADAM_LR * (m_hat / (jnp.sqrt(v_hat) + ADAM_EPS) + ADAM_WD * w)
    return delta, m, v


def _adamw(w, m, v, g, name):
    R, C = w.shape
    parts = g.ndim == 3
    tr = R
    for t in (256, 128, 64, 32, 16):
        if R % t == 0 and R > t:
            tr = t
            break

    def body(w_ref, m_ref, v_ref, g_ref, go_ref, d_ref, mo_ref, vo_ref):
        if parts:
            gv = ((g_ref[0].astype(f32) + g_ref[1].astype(f32)) + g_ref[2].astype(f32)) + g_ref[3].astype(f32)
        else:
            gv = g_ref[...]
        go_ref[...] = gv
        d, mn, vn = _adam_math(w_ref[...], gv, m_ref[...], v_ref[...])
        d_ref[...] = d
        mo_ref[...] = mn
        vo_ref[...] = vn

    row = pl.BlockSpec((tr, C), lambda i: (i, 0))
    gspec = pl.BlockSpec((4, tr, C), lambda i: (0, i, 0)) if parts else row
    return pl.pallas_call(
        body,
        grid=(R // tr,),
        in_specs=[row, row, row, gspec],
        out_specs=[row] * 4,
        out_shape=[SDS((R, C), f32)] * 4,
        compiler_params=_cparams(("parallel",)),
        name=name,
    )(w, m, v, g)


def _sum8(parts, name):
    _, _, R, C = parts.shape

    def body(p_ref, o_ref):
        acc = p_ref[0, 0]
        for c in range(2):
            for k in range(4):
                if c or k:
                    acc = acc + p_ref[c, k]
        o_ref[...] = acc

    return pl.pallas_call(body, out_shape=SDS((R, C), f32), name=name)(parts)


def _pair_add(by_core, b, name):
    _, K, R, C = by_core.shape
    tr = R // 2 if R % 32 == 0 else R

    def body(c_ref, a_ref, b_ref, o_ref):
        o_ref[...] = (a_ref[0].astype(f32) + b_ref[...].astype(f32)).astype(bf16)

    blk = pl.BlockSpec((1, tr, C), lambda k, i, c: (k, i, 0))
    return pl.pallas_call(
        body,
        grid_spec=pltpu.PrefetchScalarGridSpec(
            num_scalar_prefetch=1,
            grid=(K, R // tr),
            in_specs=[pl.BlockSpec((1, 1, tr, C), lambda k, i, c: (c[0], k, i, 0)), blk],
            out_specs=blk,
        ),
        out_shape=SDS((K, R, C), bf16),
        compiler_params=_cparams(("parallel", "parallel")),
        name=name,
    )(lax.axis_index("c").astype(jnp.int32).reshape(1), by_core, b)


_ANY = pl.BlockSpec(memory_space=pl.ANY)


def _chip_copies(src_ref, out_ref, send_sems, recv_sems, gather):
    x, y, c = lax.axis_index("x"), lax.axis_index("y"), lax.axis_index("c")
    mine = 2 * x + y

    def piece(k):
        return src_ref if gather else src_ref.at[k]

    sends, recvs = [], []
    for j, (px, py) in enumerate([(1 - x, y), (x, 1 - y), (1 - x, 1 - y)]):
        sends.append(pltpu.make_async_remote_copy(
            src_ref=piece(2 * px + py), dst_ref=out_ref.at[mine], send_sem=send_sems.at[j],
            recv_sem=recv_sems.at[j], device_id=(px, py, c), device_id_type=MESH))
        recvs.append(pltpu.make_async_remote_copy(
            src_ref=piece(mine), dst_ref=out_ref.at[2 * px + py], send_sem=send_sems.at[j],
            recv_sem=recv_sems.at[j], device_id=(px, py, c), device_id_type=MESH))
    return sends, recvs


def _chip_start(src_ref, out_ref, send_sems, recv_sems, gather):
    for cp in _chip_copies(src_ref, out_ref, send_sems, recv_sems, gather)[0]:
        cp.start()


def _chip_finish(src_ref, out_ref, send_sems, recv_sems, gather):
    sends, recvs = _chip_copies(src_ref, out_ref, send_sems, recv_sems, gather)
    for cp in recvs:
        cp.wait_recv()
    for cp in sends:
        cp.wait_send()


def _chip_out_shape(src, gather):
    return SDS((4,) + tuple(src.shape if gather else src.shape[1:]), src.dtype)


_CHIP_SEMS = [pltpu.SemaphoreType.DMA((3,)), pltpu.SemaphoreType.DMA((3,))]


def _fill_own(out, src, gather):
    mine = 2 * lax.axis_index("x") + lax.axis_index("y")
    own = src if gather else lax.dynamic_index_in_dim(src, mine, axis=0, keepdims=False)
    return lax.dynamic_update_index_in_dim(out, own, mine, axis=0)


def _chip_comm(src, gather, name):
    def body(src_ref, out_ref, send_sems, recv_sems):
        _chip_start(src_ref, out_ref, send_sems, recv_sems, gather)
        _chip_finish(src_ref, out_ref, send_sems, recv_sems, gather)

    out = pl.pallas_call(
        body,
        in_specs=[_ANY],
        out_specs=_ANY,
        out_shape=_chip_out_shape(src, gather),
        scratch_shapes=list(_CHIP_SEMS),
        name=name,
    )(src)
    return _fill_own(out, src, gather)


_HBM = pl.BlockSpec(memory_space=pltpu.HBM)
_SEM = pl.BlockSpec(memory_space=pltpu.SEMAPHORE)
_EFFECT = pltpu.SideEffectType.DATAFLOW_SIDE_EFFECTING
_N_SPLIT_SEMS = 6


def _split_copies(src_ref, land_ref, sems, gather):
    x, y, c = lax.axis_index("x"), lax.axis_index("y"), lax.axis_index("c")
    mine = 2 * x + y

    def piece(k):
        return src_ref if gather else src_ref.at[k]

    sends, recvs = [], []
    for j, (px, py) in enumerate([(1 - x, y), (x, 1 - y), (1 - x, 1 - y)]):
        sends.append(pltpu.make_async_remote_copy(
            src_ref=piece(2 * px + py), dst_ref=land_ref.at[mine], send_sem=sems[j], recv_sem=sems[3 + j],
            device_id=(px, py, c), device_id_type=MESH))
        recvs.append(pltpu.make_async_remote_copy(
            src_ref=piece(mine), dst_ref=land_ref.at[2 * px + py], send_sem=sems[j], recv_sem=sems[3 + j],
            device_id=(px, py, c), device_id_type=MESH))
    return sends, recvs


def _chip_comm_start(src, gather, name):
    land = _chip_out_shape(src, gather)

    def body(src_ref, land_ref, *outs):
        for cp in _split_copies(src_ref, land_ref, outs[:_N_SPLIT_SEMS], gather)[0]:
            cp.start()
        token = outs[_N_SPLIT_SEMS + 2]
        token[...] = jnp.zeros_like(token)

    res = pl.pallas_call(
        body,
        name=name,
        out_shape=(pltpu.SemaphoreType.DMA(()),) * _N_SPLIT_SEMS
        + (pltpu.HBM(src.shape, src.dtype), pltpu.HBM(land.shape, land.dtype), SDS((SUBLANE, LANE), f32)),
        in_specs=(_HBM, _HBM),
        out_specs=(_SEM,) * _N_SPLIT_SEMS + (_HBM, _HBM, pl.BlockSpec(memory_space=pltpu.VMEM)),
        input_output_aliases={0: _N_SPLIT_SEMS, 1: _N_SPLIT_SEMS + 1},
        compiler_params=pltpu.CompilerParams(has_side_effects=_EFFECT),
    )(pltpu.with_memory_space_constraint(src, pltpu.HBM),
      pltpu.with_memory_space_constraint(lax.empty(land.shape, land.dtype), pltpu.HBM))
    return (res[:_N_SPLIT_SEMS], res[_N_SPLIT_SEMS], res[_N_SPLIT_SEMS + 1]), res[_N_SPLIT_SEMS + 2]


def _chip_comm_wait(state, after, gather, name):
    sems, src_thru, land_thru = state

    def body(src_ref, land_ref, *rest):
        sends, recvs = _split_copies(src_ref, land_ref, rest[:_N_SPLIT_SEMS], gather)
        for cp in recvs:
            cp.wait_recv()
        for cp in sends:
            cp.wait_send()

    out = pl.pallas_call(
        body,
        name=name,
        out_shape=(pltpu.HBM(src_thru.shape, src_thru.dtype), pltpu.HBM(land_thru.shape, land_thru.dtype)),
        in_specs=(_HBM, _HBM) + (_SEM,) * _N_SPLIT_SEMS + (_ANY,),
        out_specs=(_HBM, _HBM),
        input_output_aliases={0: 0, 1: 1},
        compiler_params=pltpu.CompilerParams(has_side_effects=_EFFECT),
    )(src_thru, land_thru, *sems, after)
    return _fill_own(out[1], out[0], gather)


def _core_gather(src, name):
    def body(src_ref, out_ref, send_sem, recv_sem):
        x, y, c = lax.axis_index("x"), lax.axis_index("y"), lax.axis_index("c")
        cp = pltpu.make_async_remote_copy(src_ref=src_ref, dst_ref=out_ref.at[c], send_sem=send_sem,
                                          recv_sem=recv_sem, device_id=(x, y, 1 - c), device_id_type=MESH)
        cp.start()
        pltpu.make_async_remote_copy(src_ref=src_ref, dst_ref=out_ref.at[1 - c], send_sem=send_sem,
                                     recv_sem=recv_sem, device_id=(x, y, 1 - c), device_id_type=MESH).wait_recv()
        cp.wait_send()

    out = pl.pallas_call(
        body,
        in_specs=[_ANY],
        out_specs=_ANY,
        out_shape=SDS((2,) + tuple(src.shape), src.dtype),
        scratch_shapes=[pltpu.SemaphoreType.DMA, pltpu.SemaphoreType.DMA],
        name=name,
    )(src)
    return lax.dynamic_update_index_in_dim(out, src, lax.axis_index("c"), axis=0)


def _core_swap(src, name):
    def body(src_ref, out_ref, send_sem, recv_sem):
        x, y, c = lax.axis_index("x"), lax.axis_index("y"), lax.axis_index("c")
        cp = pltpu.make_async_remote_copy(src_ref=src_ref.at[1 - c], dst_ref=out_ref, send_sem=send_sem,
                                          recv_sem=recv_sem, device_id=(x, y, 1 - c), device_id_type=MESH)
        cp.start()
        cp.wait()

    return pl.pallas_call(
        body,
        in_specs=[_ANY],
        out_specs=_ANY,
        out_shape=SDS(tuple(src.shape[1:]), src.dtype),
        scratch_shapes=[pltpu.SemaphoreType.DMA, pltpu.SemaphoreType.DMA],
        name=name,
    )(src)


def _all_gather(src, tag):
    by_chip = _chip_comm(src, True, tag + "_chips")
    both = _core_gather(by_chip, tag + "_cores")
    return jnp.swapaxes(both, 0, 1).reshape((8,) + tuple(src.shape))


_PACK_A = (("w_in", (1024, 1088)),)
_PACK_B = (("w_ba", (512, 128)), ("w_bh", (512, 128)), ("w_out", (128, 1024)), ("w_up", (1024, 704)),
           ("w_down", (352, 1024)))
_PACK_SIZES = _PACK_A + _PACK_B


def _slab_rows(sizes):
    return sum(r * c for _, (r, c) in sizes) // D_MODEL


def _pack_rows(d, sizes):
    n = d[sizes[0][0]].shape[0]
    return jnp.concatenate([d[k].reshape(n, -1, D_MODEL) for k, _ in sizes], axis=1)


def _unpack_rows(slab, sizes):
    n = slab.shape[0]
    out, lo = {}, 0
    for key, (r, c) in sizes:
        rows = r * c // D_MODEL
        out[key] = slab[:, lo:lo + rows].reshape(n, r, c)
        lo += rows
    return out


def _by_core(gslab):
    return jnp.swapaxes(gslab.reshape((4, 2) + gslab.shape[1:]), 0, 1)


def _pair_sum(by_core, tag):
    return _pair_add(by_core, _core_swap(by_core, tag + "_cores"), tag + "_pair_add")


def _cols_to_full(t):
    return jnp.swapaxes(t, 0, 1).reshape(t.shape[1], -1)


def _full_to_cols(t):
    K = t.shape[0]
    return jnp.swapaxes(t.reshape(K, 8, -1), 0, 1)


_SMALL = (("pre_mix_norm", (1, 1024)), ("rel_bias", (32, 24)), ("hgrn_lb_raw", (2, 512)), ("hgrn_norm", (1, 128)),
          ("post_mix_norm", (1, 1024)), ("pre_ffn_norm", (1, 1024)), ("conv_b", (1, 5632)),
          ("post_ffn_norm", (1, 1024)))
_SMALL_ROWS = 96
_CONVW_ROWS = 136


_SMALL_USED = sum(r * c for _, (r, c) in _SMALL)


def _pack_small(d, extra=None):
    flat = jnp.concatenate([d[k].reshape(-1) for k, _ in _SMALL] + ([] if extra is None else [extra.reshape(-1)]))
    flat = jnp.pad(flat, (0, _SMALL_ROWS * LANE - flat.shape[0]))
    return flat.reshape(_SMALL_ROWS, LANE)


def _unpack_small(p):
    flat = p.reshape(-1)
    out, lo = {}, 0
    for k, shp in _SMALL:
        n = shp[0] * shp[1]
        out[k] = flat[lo:lo + n].reshape(shp)
        lo += n
    return out


def _local_step(x, tgt, WA, P, plan):
    S = x.shape[0]
    W = dict(WA)
    lb = _lb_fwd(P["hgrn_lb_raw"])
    hs = _prep(x, P["pre_mix_norm"])
    h1 = hs[0]
    consts = [_bias_consts(d) for d in DILATIONS]
    qkv, obuf, lbuf, biases = [], [], [], []
    for g, d in enumerate(DILATIONS):
        qkv_g = _mm(hs[g], W["w_qkv"][g], "nn", bf16, f"proj_qkv{g}")
        tab_t = P["rel_bias"][:, 8 * g:8 * g + 8].T
        bias_g = _bias_build(tab_t, consts[g][0], consts[g][1], f"bias_build{g}").reshape(8, ATTN_BLOCK, 2 * ATTN_BLOCK)
        o_g, l_g = _attn_fwd(qkv_g, bias_g, (S // d) // ATTN_BLOCK, f"attn_fwd{g}")
        qkv.append(qkv_g)
        biases.append(bias_g)
        lbuf.append(l_g)
        obuf.append(o_g)
    y_attn, y_attn_b, w0, w1, w2 = _attn_merge(obuf[0], obuf[1], obuf[2], lbuf[0], lbuf[1], lbuf[2])
    hg = _mm(h1, W["w_hg"], "nn", f32, "proj_hg")
    gc = _mm(h1, W["w_gate"], "nn", bf16, "proj_gate")
    y_hgrn, o_raw, ck, got = _hgrn_fwd(hg, lb, P["hgrn_norm"], plan.fwd_ride())
    W.update(plan.weights(got))
    a = _mm(y_attn_b, W["w_ba"], "nn", bf16, "branch_attn")
    b = _mm(y_hgrn, W["w_bh"], "nn", bf16, "branch_hgrn")
    merged = _gate_fwd(a, b, gc)
    mo = _mm(merged, W["w_out"], "nn", f32, "out_proj")
    x1, h2 = _mid_fwd(x, mo, P["post_mix_norm"], P["pre_ffn_norm"])
    ug = _mm(h2, W["w_up_g"], "nn", bf16, "up_gate")
    uv = _mm(h2, W["w_up_v"], "nn", bf16, "up_val")
    cw_g, cw_v = P["conv_w"][:, :D_FF], P["conv_w"][:, D_FF:]
    cb_g, cb_v = P["conv_b"][:, :D_FF], P["conv_b"][:, D_FF:]
    act = _conv_fwd(ug, uv, cw_g, cw_v, cb_g, cb_v)
    fo = _mm(act, W["w_down"], "nn", f32, "down_proj")
    loss, dy, dfo, g_post_ffn = _final(x1, fo, tgt, P["post_ffn_norm"])
    dact = _mm(dfo, W["w_down"], "nt", bf16, "d_act")
    gW_down = _mm(act, dfo, "tn", f32, "gw_down")
    dug, duv, st_g, st_v = _conv_bwd(ug, uv, dact, cw_g, cw_v, cb_g, cb_v)
    dh2 = _mm(dug, W["w_up_g"], "nt", f32, "dh2_gate")
    dh2 = _mm(duv, W["w_up_v"], "nt", f32, "dh2_val", acc=dh2)
    gW_up_g = _mm(h2, dug, "tn", f32, "gw_up_gate")
    gW_up_v = _mm(h2, duv, "tn", f32, "gw_up_val")
    dx1, dmo, g_pre_ffn, g_post_mix = _mid_bwd(dy, dh2, x1, mo, P["pre_ffn_norm"], P["post_mix_norm"])
    dmerged = _mm(dmo, W["w_out"], "nt", bf16, "d_merged")
    gW_out = _mm(merged, dmo, "tn", f32, "gw_out")
    da, db, dgc = _gate_bwd(dmerged, a, b, gc)
    dyattn = _mm(da, W["w_ba"], "nt", f32, "d_yattn")
    gW_ba = _mm(y_attn_b, da, "tn", f32, "gw_ba")
    dyhgrn = _mm(db, W["w_bh"], "nt", f32, "d_yhgrn")
    gW_bh = _mm(y_hgrn, db, "tn", f32, "gw_bh")
    big_b = dict(w_ba=gW_ba, w_bh=gW_bh, w_out=gW_out, w_up=[gW_up_g, gW_up_v], w_down=gW_down)
    dq_h, df_h, dv_h, dog_h, glb8, gnw8, got_b = _hgrn_bwd(hg, o_raw, dyhgrn, ck, lb, P["hgrn_norm"],
                                                          plan.bwd_ride(big_b))
    dhg = [dq_h, df_h, dv_h, dog_h]
    g_lb_raw = _lb_bwd(P["hgrn_lb_raw"], glb8[0:1])
    gn = gnw8[0:1]
    g_hgrn_norm = (gn[:, 0:128] + gn[:, 128:256]) + (gn[:, 256:384] + gn[:, 384:512])
    dos = _attn_merge_bwd(dyattn, y_attn, w0, w1, w2)
    dqkvs, gW_qkv, g_rel = [], [], []
    for g, d in enumerate(DILATIONS):
        dq, dk, dv, dbias = _attn_bwd(qkv[g], biases[g], dos[g], dos[3 + g], lbuf[g], (S // d) // ATTN_BLOCK,
                                      f"attn_bwd{g}")
        dqkvs.append([dq, dk, dv])
        gW_qkv.append(_mm(hs[g], dqkvs[g], "tn", f32, f"gw_qkv{g}"))
        g_rel.append(_bias_grad(dbias.reshape(8, -1), consts[g][0], f"bias_grad{g}"))
    gW_hg = _mm(h1, dhg, "tn", f32, "gw_hg")
    gW_gate = _mm(h1, dgc, "tn", f32, "gw_gate")
    gW_in = gW_qkv + [gW_hg, gW_gate]
    token = plan.grads_a_start(gW_in)
    dh_parts = [_mm(dqkvs[g], W["w_qkv"][g], "nt", f32, f"dh1_qkv{g}", after=token) for g in range(N_GROUPS)]
    dh_main = _mm(dhg, W["w_hg"], "nt", f32, "dh1_hg", acc=dh_parts[0], after=token)
    dh_main = _mm(dgc, W["w_gate"], "nt", f32, "dh1_gate", acc=dh_main, after=token)
    grad_x, g_pre_mix = _first_bwd(x, dx1, _dh_sum(dh_main, dh_parts[1], dh_parts[2]), P["pre_mix_norm"])

    g_conv_w = jnp.concatenate([st_g[0:3], st_v[0:3]], axis=1)
    g_conv_b = jnp.concatenate([st_g[3:4], st_v[3:4]], axis=1)
    small = dict(pre_mix_norm=g_pre_mix, rel_bias=jnp.concatenate(g_rel, axis=1), hgrn_lb_raw=g_lb_raw,
                 hgrn_norm=g_hgrn_norm, post_mix_norm=g_post_mix, pre_ffn_norm=g_pre_ffn, conv_b=g_conv_b,
                 post_ffn_norm=g_post_ffn, conv_w=g_conv_w)
    return loss, grad_x, gW_in, big_b, got_b, small


def _weights_a(both):
    w_in = jnp.transpose(both, (2, 1, 0, 3)).reshape(D_MODEL, -1)
    return dict(
        w_qkv=[w_in[:, g * QKV_G:(g + 1) * QKV_G] for g in range(N_GROUPS)],
        w_hg=w_in[:, 3 * QKV_G:3 * QKV_G + 4 * HGRN_W],
        w_gate=w_in[:, 3 * QKV_G + 4 * HGRN_W:],
    )


def _weights_b(slabs):
    sh = _unpack_rows(slabs, _PACK_B)
    w_up = _cols_to_full(sh["w_up"])
    return dict(
        w_ba=_cols_to_full(sh["w_ba"]),
        w_bh=_cols_to_full(sh["w_bh"]),
        w_out=sh["w_out"].reshape(D_MODEL, D_MODEL),
        w_up_g=w_up[:, :D_FF],
        w_up_v=w_up[:, D_FF:],
        w_down=sh["w_down"].reshape(D_FF, D_MODEL),
    )


def _dest_cols(sections, width):
    out = []
    for j in range(8):
        lo, hi, off, pieces = j * width, (j + 1) * width, 0, []
        for s in sections:
            a, b = max(lo, off), min(hi, off + s.shape[1])
            if a < b:
                pieces.append(s[:, a - off:b - off])
            off += s.shape[1]
        out.append(pieces[0] if len(pieces) == 1 else jnp.concatenate(pieces, axis=1))
    return out


def _grad_blocks_a(sections):
    cols = _dest_cols(sections, 1088)
    return jnp.stack([jnp.stack([cols[2 * k + c].astype(bf16) for k in range(4)]) for c in range(2)])


def _grad_slab_b(g):
    shards = dict(w_ba=_full_to_cols(g["w_ba"]), w_bh=_full_to_cols(g["w_bh"]), w_out=g["w_out"].reshape(8, 128, D_MODEL),
                  w_up=jnp.stack(_dest_cols(g["w_up"], 704)), w_down=g["w_down"].reshape(8, 352, D_MODEL))
    return _pack_rows({k: v.astype(bf16) for k, v in shards.items()}, _PACK_B)


class _SlabB:
    def __init__(self, slab):
        self.slab = slab
        self.chip_sum = None
        self.state_a = None

    def fwd_ride(self):
        return (self.slab, True)

    def weights(self, got):
        both = _core_gather(_fill_own(got, self.slab, True), "ag_b_cores")
        return _weights_b(jnp.swapaxes(both, 0, 1).reshape((8,) + tuple(self.slab.shape)))

    def bwd_ride(self, grads):
        self.chip_sum = _pair_sum(_by_core(_grad_slab_b(grads)), "rs_b")
        return (self.chip_sum, False)

    def grads_a_start(self, sections):
        self.state_a, token = _chip_comm_start(_pair_sum(_grad_blocks_a(sections), "rs_a"), False, "rs_a_start")
        return token

    def parts(self, got_b, after):
        parts = _unpack_rows(_fill_own(got_b, self.chip_sum, False), _PACK_B)
        parts["w_in"] = _chip_comm_wait(self.state_a, after, False, "rs_a_wait")
        return parts


def kernel(x, pre_mix_norm, w_in, rel_bias, hgrn_lb_raw, hgrn_norm, w_branch_attn, w_branch_hgrn, w_out, post_mix_norm, pre_ffn_norm, w_up, conv_w, conv_b, w_down, post_ffn_norm, loss_target, m_pre_mix_norm, m_w_in, m_rel_bias, m_hgrn_lb_raw, m_hgrn_norm, m_w_branch_attn, m_w_branch_hgrn, m_w_out, m_post_mix_norm, m_pre_ffn_norm, m_w_up, m_conv_w, m_conv_b, m_w_down, m_post_ffn_norm, v_pre_mix_norm, v_w_in, v_rel_bias, v_hgrn_lb_raw, v_hgrn_norm, v_w_branch_attn, v_w_branch_hgrn, v_w_out, v_post_mix_norm, v_pre_ffn_norm, v_w_up, v_conv_w, v_conv_b, v_w_down, v_post_ffn_norm):
    ci = lax.axis_index("c")
    dev = 4 * lax.axis_index("x") + 2 * lax.axis_index("y") + ci
    wts = dict(w_in=w_in[0], w_ba=w_branch_attn[0], w_bh=w_branch_hgrn[0], w_out=w_out[0], w_up=w_up[0],
               w_down=w_down[0])
    mom = dict(w_in=m_w_in[0], w_ba=m_w_branch_attn[0], w_bh=m_w_branch_hgrn[0], w_out=m_w_out[0], w_up=m_w_up[0],
               w_down=m_w_down[0])
    var = dict(w_in=v_w_in[0], w_ba=v_w_branch_attn[0], w_bh=v_w_branch_hgrn[0], w_out=v_w_out[0], w_up=v_w_up[0],
               w_down=v_w_down[0])
    small_w = dict(pre_mix_norm=pre_mix_norm, rel_bias=rel_bias, hgrn_lb_raw=hgrn_lb_raw, hgrn_norm=hgrn_norm,
                   post_mix_norm=post_mix_norm, pre_ffn_norm=pre_ffn_norm, conv_b=conv_b, post_ffn_norm=post_ffn_norm)
    small_m = dict(pre_mix_norm=m_pre_mix_norm, rel_bias=m_rel_bias, hgrn_lb_raw=m_hgrn_lb_raw, hgrn_norm=m_hgrn_norm,
                   post_mix_norm=m_post_mix_norm, pre_ffn_norm=m_pre_ffn_norm, conv_b=m_conv_b,
                   post_ffn_norm=m_post_ffn_norm)
    small_v = dict(pre_mix_norm=v_pre_mix_norm, rel_bias=v_rel_bias, hgrn_lb_raw=v_hgrn_lb_raw, hgrn_norm=v_hgrn_norm,
                   post_mix_norm=v_post_mix_norm, pre_ffn_norm=v_pre_ffn_norm, conv_b=v_conv_b,
                   post_ffn_norm=v_post_ffn_norm)

    slab_a = wts["w_in"].astype(bf16)
    WA = _weights_a(_core_gather(_chip_comm(slab_a, True, "ag_a_chips"), "ag_a_cores"))
    plan = _SlabB(_pack_rows({k: wts[k].astype(bf16)[None] for k, _ in _PACK_B}, _PACK_B)[0])
    cw_pad = jnp.pad(conv_w[0], ((0, SUBLANE - 3), (0, 768 - 704)))
    conv_w_full = _cols_to_full(_all_gather(cw_pad, "ag_convw")[:, 0:3, 0:704])
    P = dict(small_w)
    P["conv_w"] = conv_w_full

    loss8, grad_x, _, _, got_b, small = _local_step(x[0], loss_target[0], WA, P, plan)
    parts = plan.parts(got_b, grad_x)
    outs_big = {}
    for k, _ in _PACK_SIZES:
        outs_big[k] = _adamw(wts[k], mom[k], var[k], parts[k], "adamw_" + k)

    spack = jnp.concatenate([_pack_small(small, loss8[0, 0:1]),
                             jnp.pad(small["conv_w"].reshape(-1, LANE), ((0, _CONVW_ROWS - 132), (0, 0)))], axis=0)
    allp = _core_gather(_chip_comm(spack, True, "ag_small_chips"), "ag_small_cores")
    ssum = _sum8(allp, "small_sum")
    gs = ssum[:_SMALL_ROWS]
    loss = ssum[_SMALL_USED // LANE, _SMALL_USED % LANE]
    res_small = _adamw(_pack_small(small_w), _pack_small(small_m), _pack_small(small_v), gs, "adamw_small")
    sm = [_unpack_small(t) for t in res_small]
    g_cw_full = ssum[_SMALL_ROWS:_SMALL_ROWS + 132].reshape(3, 2 * D_FF)
    g_cw = lax.dynamic_slice_in_dim(g_cw_full, dev * 704, 704, axis=1)
    res_cw = _adamw(conv_w[0], m_conv_w[0], v_conv_w[0], g_cw, "adamw_conv_w")

    def pick(i):
        def big_(k):
            return outs_big[k][i][None]
        return [sm[i]["pre_mix_norm"], big_("w_in"), sm[i]["rel_bias"], sm[i]["hgrn_lb_raw"], sm[i]["hgrn_norm"],
                big_("w_ba"), big_("w_bh"), big_("w_out"), sm[i]["post_mix_norm"], sm[i]["pre_ffn_norm"],
                big_("w_up"), res_cw[i][None], sm[i]["conv_b"], big_("w_down"), sm[i]["post_ffn_norm"]]

    return (loss, grad_x[None], *pick(0), *pick(1), *pick(2), *pick(3))
```

```python
import functools
import math

import jax
import jax.numpy as jnp
from jax import lax
from jax.experimental import pallas as pl
from jax.experimental.pallas import tpu as pltpu

f32 = jnp.float32
bf16 = jnp.bfloat16
SDS = jax.ShapeDtypeStruct
HIGHEST = lax.Precision.HIGHEST
MESH = pl.DeviceIdType.MESH

NN = (((1,), (0,)), ((), ()))
NT = (((1,), (1,)), ((), ()))
TN = (((0,), (0,)), ((), ()))

D_MODEL = 1024
N_GROUPS = 3
DILATIONS = (1, 4, 16)
HEAD_DIM = 64
ATTN_BLOCK = 128
QKV_G = 1536
ATTN_OUT = 512
HGRN_W = 512
HGRN_CHUNK = 32
D_FF = 2816
NUM_BUCKETS = 32
MAX_EXACT = 16
MAX_DISTANCE = 2048
NEG_INF = -1e30
EPS = 1e-6
LANE = 128
SUBLANE = 8
VMEM_BIG = 48 * 1024 * 1024
MM_ROWS = 512
MM_OUT_BYTES = 8 * 1024 * 1024

ADAM_LR, ADAM_B1, ADAM_B2, ADAM_EPS, ADAM_WD, ADAM_STEP = 0.001, 0.9, 0.999, 1e-08, 0.01, 10


def _pick(n, pref):
    t = pref
    while t >= LANE:
        if n % t == 0:
            return t
        t //= 2
    return n


def _cparams(sem=None, vmem=None):
    kw = {}
    if sem is not None:
        kw["dimension_semantics"] = sem
    if vmem is not None:
        kw["vmem_limit_bytes"] = vmem
    return pltpu.CompilerParams(**kw)


def _sigmoid(x):
    return jax.nn.sigmoid(x)


def _colsum8(x):
    return x.reshape(x.shape[0] // SUBLANE, SUBLANE, x.shape[1]).sum(axis=0)


def _mm(a, b, mode, out_dtype, name, acc=None, after=None):
    dims = {"nn": NN, "nt": NT, "tn": TN}[mode]
    has_acc = acc is not None
    if mode == "tn":
        assert not has_acc
        bs = list(b) if isinstance(b, (list, tuple)) else [b]
        K, M = a.shape
        widths = [t.shape[1] for t in bs]
        N = sum(widths)
        tmm = M if M * N * 4 <= MM_OUT_BYTES else M // 2
        ts = _pick(K, MM_ROWS)
        nk = K // ts

        def body_tn(a_ref, *refs):
            o_ref = refs[-1]
            k = pl.program_id(1)
            av = a_ref[...]
            lo = 0
            for b_ref, w in zip(refs[:-1], widths):
                part = lax.dot_general(av, b_ref[...], dims, preferred_element_type=f32)
                cols = slice(lo, lo + w)
                lo += w

                @pl.when(k == 0)
                def _(part=part, cols=cols):
                    o_ref[:, cols] = part

                @pl.when(k > 0)
                def _(part=part, cols=cols):
                    o_ref[:, cols] += part

        return pl.pallas_call(
            body_tn,
            grid=(M // tmm, nk),
            in_specs=[pl.BlockSpec((ts, tmm), lambda i, k: (k, i))]
            + [pl.BlockSpec((ts, w), lambda i, k: (k, 0)) for w in widths],
            out_specs=pl.BlockSpec((tmm, N), lambda i, k: (i, 0)),
            out_shape=SDS((M, N), out_dtype),
            compiler_params=_cparams(("parallel", "arbitrary"), VMEM_BIG),
            name=name,
        )(a, *bs)

    parts = list(a) if isinstance(a, (list, tuple)) else [a]
    assert mode == "nt" or len(parts) == 1
    widths = [t.shape[1] for t in parts]
    M = parts[0].shape[0]
    N = b.shape[1] if mode == "nn" else b.shape[0]
    tm = _pick(M, MM_ROWS)
    npart = len(parts)

    def body(*refs):
        a_refs, b_ref = refs[:npart], refs[npart]
        c_ref = refs[npart + 1] if has_acc else None
        o_ref = refs[-1]
        if npart == 1:
            part = lax.dot_general(a_refs[0][...], b_ref[...], dims, preferred_element_type=f32)
        else:
            part, lo = None, 0
            for a_ref, w in zip(a_refs, widths):
                t = lax.dot_general(a_ref[...], b_ref[:, lo:lo + w], dims, preferred_element_type=f32)
                part = t if part is None else part + t
                lo += w
        if has_acc:
            part = part + c_ref[...]
        o_ref[...] = part.astype(out_dtype)

    specs = [pl.BlockSpec((tm, w), lambda i: (i, 0)) for w in widths] + [pl.BlockSpec(b.shape, lambda i: (0, 0))]
    args = parts + [b]
    aliases = {}
    if has_acc:
        specs.append(pl.BlockSpec((tm, N), lambda i: (i, 0)))
        args.append(acc)
        aliases = {npart + 1: 0}
    if after is not None:
        specs.append(pl.BlockSpec(memory_space=pl.ANY))
        args.append(after)
    return pl.pallas_call(
        body,
        grid=(M // tm,),
        in_specs=specs,
        out_specs=pl.BlockSpec((tm, N), lambda i: (i, 0)),
        out_shape=SDS((M, N), out_dtype),
        input_output_aliases=aliases,
        compiler_params=_cparams(("parallel",), VMEM_BIG),
        name=name,
    )(*args)


PERM_ROWS = 1024


def _perm_spec(d, cols=LANE):
    return pl.BlockSpec((d, PERM_ROWS // d, cols), lambda i, j: (0, i, j))


def _to_natural(src_ref, dst_ref, d):
    n = src_ref.shape[1]
    for r in range(d):
        dst_ref[pl.ds(r, n, stride=d), :] = src_ref[r]


def _prep(x, w, after=None):
    S, D = x.shape
    R = PERM_ROWS
    nc = D // LANE
    n_in = nc + 1 + (after is not None)

    def body(*refs):
        x_refs, w_ref = refs[:nc], refs[nc]
        h_ref, h4_ref, h16_ref, rs = refs[n_in:]
        ssq = None
        for xr in x_refs:
            v = xr[...]
            t = jnp.sum(v * v, axis=-1, keepdims=True)
            ssq = t if ssq is None else ssq + t
        rinv = lax.rsqrt(ssq * (1.0 / D) + EPS)
        rs[...] = jnp.broadcast_to(rinv, (R, LANE))
        for j, xr in enumerate(x_refs):
            cols = slice(j * LANE, (j + 1) * LANE)
            wj = w_ref[:, cols]
            h_ref[:, cols] = ((xr[...] * rinv) * wj).astype(bf16)
            for d, o_ref in ((4, h4_ref), (16, h16_ref)):
                n = R // d
                for r in range(d):
                    rows = pl.ds(r, n, stride=d)
                    o_ref[r, :, cols] = ((xr[rows, :] * rs[rows, :]) * wj).astype(bf16)

    col = lambda j: pl.BlockSpec((R, LANE), lambda i, j=j: (i, j))
    h, h4, h16 = pl.pallas_call(
        body,
        grid=(S // R,),
        in_specs=[col(j) for j in range(nc)] + [pl.BlockSpec((1, D), lambda i: (0, 0))]
        + ([] if after is None else [pl.BlockSpec(memory_space=pl.ANY)]),
        out_specs=[pl.BlockSpec((R, D), lambda i: (i, 0)), pl.BlockSpec((4, R // 4, D), lambda i: (0, i, 0)),
                   pl.BlockSpec((16, R // 16, D), lambda i: (0, i, 0))],
        out_shape=[SDS((S, D), bf16), SDS((4, S // 4, D), bf16), SDS((16, S // 16, D), bf16)],
        scratch_shapes=[pltpu.VMEM((R, LANE), f32)],
        compiler_params=_cparams(("parallel",), VMEM_BIG),
        name="prep_norm_perm",
    )(*([x] * nc), w, *([] if after is None else [after]))
    return [h, h4.reshape(S, D), h16.reshape(S, D)]


def _dh_sum(a, b, c):
    S, D = a.shape
    R = PERM_ROWS

    def body(a_ref, b_ref, c_ref, o_ref, sb, sc):
        _to_natural(b_ref, sb, 4)
        _to_natural(c_ref, sc, 16)
        o_ref[...] = (a_ref[...] + sb[...]) + sc[...]

    nat = pl.BlockSpec((R, LANE), lambda i, j: (i, j))
    return pl.pallas_call(
        body,
        grid=(S // R, D // LANE),
        in_specs=[nat, _perm_spec(4), _perm_spec(16)],
        out_specs=nat,
        out_shape=SDS((S, D), f32),
        scratch_shapes=[pltpu.VMEM((R, LANE), f32)] * 2,
        compiler_params=_cparams(("parallel", "parallel")),
        name="dh_sum",
    )(a, b.reshape(4, S // 4, D), c.reshape(16, S // 16, D))


def _rms_parts(xv):
    r = lax.rsqrt(jnp.mean(xv * xv, axis=-1, keepdims=True) + EPS)
    return r, xv * r


def _rms_bwd(xhat, r, w, dy):
    dyw = dy * w
    return r * (dyw - xhat * jnp.mean(dyw * xhat, axis=-1, keepdims=True))


def _mid_fwd(x, mo, w_pm, w_pf):
    S, D = x.shape
    tm = _pick(S, 512)

    def body(x_ref, mo_ref, wpm_ref, wpf_ref, x1_ref, h2_ref):
        _, moh = _rms_parts(mo_ref[...])
        x1 = x_ref[...] + moh * wpm_ref[...]
        x1_ref[...] = x1
        _, x1h = _rms_parts(x1)
        h2_ref[...] = (x1h * wpf_ref[...]).astype(bf16)

    row = pl.BlockSpec((tm, D), lambda i: (i, 0))
    vec = pl.BlockSpec((1, D), lambda i: (0, 0))
    return pl.pallas_call(
        body,
        grid=(S // tm,),
        in_specs=[row, row, vec, vec],
        out_specs=[row, row],
        out_shape=[SDS((S, D), f32), SDS((S, D), bf16)],
        compiler_params=_cparams(("parallel",)),
        name="mid_fwd",
    )(x, mo, w_pm, w_pf)


def _final(x1, fo, tgt, w_pfn):
    S, D = x1.shape
    tm = _pick(S, 512)
    nt = S // tm

    def body(x1_ref, fo_ref, t_ref, w_ref, loss_ref, dy_ref, dfo_ref, gw_ref, lacc, gacc):
        i = pl.program_id(0)

        @pl.when(i == 0)
        def _():
            lacc[...] = jnp.zeros_like(lacc)
            gacc[...] = jnp.zeros_like(gacc)

        w = w_ref[...]
        r, foh = _rms_parts(fo_ref[...])
        y = x1_ref[...] + foh * w
        err = y - t_ref[...]
        lacc[...] += _colsum8(err * err)
        dy = err * (1.0 / D)
        dy_ref[...] = dy
        gacc[...] += _colsum8(dy * foh)
        dfo_ref[...] = _rms_bwd(foh, r, w, dy).astype(bf16)

        @pl.when(i == nt - 1)
        def _():
            loss_ref[...] = jnp.full((SUBLANE, LANE), 0.5 / D, f32) * jnp.sum(lacc[...])
            gw_ref[...] = jnp.sum(gacc[...], axis=0, keepdims=True)

    row = pl.BlockSpec((tm, D), lambda i: (i, 0))
    vec = pl.BlockSpec((1, D), lambda i: (0, 0))
    return pl.pallas_call(
        body,
        grid=(nt,),
        in_specs=[row, row, row, vec],
        out_specs=[pl.BlockSpec((SUBLANE, LANE), lambda i: (0, 0)), row, row, vec],
        out_shape=[SDS((SUBLANE, LANE), f32), SDS((S, D), f32), SDS((S, D), bf16), SDS((1, D), f32)],
        scratch_shapes=[pltpu.VMEM((SUBLANE, D), f32), pltpu.VMEM((SUBLANE, D), f32)],
        compiler_params=_cparams(("arbitrary",)),
        name="final_loss",
    )(x1, fo, tgt, w_pfn)


def _mid_bwd(dy, dh2, x1, mo, w_pf, w_pm):
    S, D = dy.shape
    tm = _pick(S, 512)
    nt = S // tm

    def body(dy_ref, dh2_ref, x1_ref, mo_ref, wpf_ref, wpm_ref, dx1_ref, dmo_ref, gpf_ref, gpm_ref, apf, apm):
        i = pl.program_id(0)

        @pl.when(i == 0)
        def _():
            apf[...] = jnp.zeros_like(apf)
            apm[...] = jnp.zeros_like(apm)

        r1, x1h = _rms_parts(x1_ref[...])
        dh2 = dh2_ref[...]
        apf[...] += _colsum8(dh2 * x1h)
        dx1 = dy_ref[...] + _rms_bwd(x1h, r1, wpf_ref[...], dh2)
        dx1_ref[...] = dx1
        rm, moh = _rms_parts(mo_ref[...])
        apm[...] += _colsum8(dx1 * moh)
        dmo_ref[...] = _rms_bwd(moh, rm, wpm_ref[...], dx1).astype(bf16)

        @pl.when(i == nt - 1)
        def _():
            gpf_ref[...] = jnp.sum(apf[...], axis=0, keepdims=True)
            gpm_ref[...] = jnp.sum(apm[...], axis=0, keepdims=True)

    row = pl.BlockSpec((tm, D), lambda i: (i, 0))
    vec = pl.BlockSpec((1, D), lambda i: (0, 0))
    return pl.pallas_call(
        body,
        grid=(nt,),
        in_specs=[row, row, row, row, vec, vec],
        out_specs=[row, row, vec, vec],
        out_shape=[SDS((S, D), f32), SDS((S, D), bf16), SDS((1, D), f32), SDS((1, D), f32)],
        scratch_shapes=[pltpu.VMEM((SUBLANE, D), f32), pltpu.VMEM((SUBLANE, D), f32)],
        compiler_params=_cparams(("arbitrary",)),
        name="mid_bwd",
    )(dy, dh2, x1, mo, w_pf, w_pm)


def _first_bwd(x, dx1, dh, w_pre):
    S, D = x.shape
    tm = _pick(S, 512)
    nt = S // tm

    def body(x_ref, dx1_ref, a_ref, w_ref, gx_ref, gw_ref, acc):
        i = pl.program_id(0)

        @pl.when(i == 0)
        def _():
            acc[...] = jnp.zeros_like(acc)

        r, xh = _rms_parts(x_ref[...])
        dh = a_ref[...]
        acc[...] += _colsum8(dh * xh)
        gx_ref[...] = dx1_ref[...] + _rms_bwd(xh, r, w_ref[...], dh)

        @pl.when(i == nt - 1)
        def _():
            gw_ref[...] = jnp.sum(acc[...], axis=0, keepdims=True)

    row = pl.BlockSpec((tm, D), lambda i: (i, 0))
    vec = pl.BlockSpec((1, D), lambda i: (0, 0))
    return pl.pallas_call(
        body,
        grid=(nt,),
        in_specs=[row, row, row, vec],
        out_specs=[row, vec],
        out_shape=[SDS((S, D), f32), SDS((1, D), f32)],
        scratch_shapes=[pltpu.VMEM((SUBLANE, D), f32)],
        compiler_params=_cparams(("arbitrary",)),
        name="first_bwd",
    )(x, dx1, dh, w_pre)


def _t5_bucket(dist):
    n = jnp.maximum(dist, 0)
    nf = jnp.maximum(n, 1).astype(f32)
    large = MAX_EXACT + (jnp.log(nf / MAX_EXACT) / math.log(MAX_DISTANCE / MAX_EXACT)
                         * (NUM_BUCKETS - MAX_EXACT)).astype(jnp.int32)
    large = jnp.minimum(large, NUM_BUCKETS - 1)
    return jnp.where(n < MAX_EXACT, n, large)


def _bias_consts(d):
    blk = ATTN_BLOCK
    rel = jnp.arange(blk)[:, None] + blk - jnp.arange(2 * blk)[None, :]
    in_win = (rel >= 0) & (rel <= blk)
    bucket = _t5_bucket(rel * d).reshape(1, -1)
    onehot = (bucket == jnp.arange(NUM_BUCKETS)[:, None]).astype(f32)
    return onehot, in_win.astype(f32).reshape(1, -1)


def _bias_build(tab_t, onehot, maskf, name):
    H = tab_t.shape[0]

    def body(t_ref, oh_ref, m_ref, o_ref):
        b = jnp.dot(t_ref[...], oh_ref[...], precision=HIGHEST, preferred_element_type=f32)
        o_ref[...] = jnp.where(m_ref[...] > 0.5, b, NEG_INF)

    return pl.pallas_call(body, out_shape=SDS((H, onehot.shape[1]), f32), name=name)(tab_t, onehot, maskf)


def _bias_grad(dbias_flat, onehot, name):
    H = dbias_flat.shape[0]

    def body(g_ref, oh_ref, o_ref):
        o_ref[...] = lax.dot_general(oh_ref[...], g_ref[...], NT, precision=HIGHEST, preferred_element_type=f32)

    return pl.pallas_call(body, out_shape=SDS((NUM_BUCKETS, H), f32), name=name)(dbias_flat, onehot)


ATTN_TILE = 512
ATTN_SUB = ATTN_TILE // ATTN_BLOCK


def _qkv_specs(nt):
    tile = (ATTN_TILE, LANE)
    blk = (ATTN_BLOCK, LANE)
    cur = lambda off: (lambda h, t: (jnp.minimum(t, nt - 1), off + h))
    prev = lambda off: (lambda h, t: (jnp.maximum(jnp.minimum(t, nt - 1) * ATTN_SUB - 1, 0), off + h))
    return [pl.BlockSpec(tile, cur(0)), pl.BlockSpec(blk, prev(4)), pl.BlockSpec(tile, cur(4)),
            pl.BlockSpec(blk, prev(8)), pl.BlockSpec(tile, cur(8))]


def _head_masks():
    lane = lax.broadcasted_iota(jnp.int32, (ATTN_BLOCK, LANE), 1)
    return lane < HEAD_DIM


def _attn_fwd(qkv, bias, bps, name):
    S = qkv.shape[0]
    nt = S // ATTN_TILE
    scale = HEAD_DIM ** -0.5

    def body(q_ref, kp_ref, kc_ref, vp_ref, vc_ref, b_ref, o_ref, l_ref):
        t = pl.program_id(1)
        kk = jnp.concatenate([kp_ref[...], kc_ref[...]], axis=0)
        vv = jnp.concatenate([vp_ref[...], vc_ref[...]], axis=0)
        low = _head_masks()
        col = lax.broadcasted_iota(jnp.int32, (ATTN_BLOCK, 2 * ATTN_BLOCK), 1)
        for b in range(ATTN_SUB):
            lo = b * ATTN_BLOCK
            rows = slice(lo, lo + ATTN_BLOCK)
            keys = slice(lo, lo + 2 * ATTN_BLOCK)
            dead = jnp.logical_and((t * ATTN_SUB + b) % bps == 0, col < ATTN_BLOCK)
            q2 = q_ref[rows, :]
            kb, vb = kk[keys], vv[keys]
            outs, lses = [], []
            for h in range(2):
                hm = low if h == 0 else jnp.logical_not(low)
                qh = jnp.where(hm, q2, jnp.zeros_like(q2))
                s = lax.dot_general(qh, kb, NT, preferred_element_type=f32) * scale + b_ref[h]
                s = jnp.where(dead, NEG_INF, s)
                m = jnp.max(s, axis=-1, keepdims=True)
                p = jnp.exp(s - m)
                l = jnp.sum(p, axis=-1, keepdims=True)
                outs.append(jnp.dot(p.astype(bf16), vb, preferred_element_type=f32) / l)
                lses.append(m + jnp.log(l))
            o_ref[rows, :] = jnp.where(low, outs[0], outs[1])
            l_ref[rows, :] = jnp.where(low, lses[0], lses[1])

    tile = pl.BlockSpec((ATTN_TILE, LANE), lambda h, t: (t, h))
    return pl.pallas_call(
        body,
        grid=(4, nt),
        in_specs=_qkv_specs(nt) + [pl.BlockSpec((2, ATTN_BLOCK, 2 * ATTN_BLOCK), lambda h, t: (h, 0, 0))],
        out_specs=[tile, tile],
        out_shape=[SDS((S, ATTN_OUT), f32), SDS((S, ATTN_OUT), f32)],
        compiler_params=_cparams(("parallel", "parallel")),
        name=name,
    )(qkv, qkv, qkv, qkv, qkv, bias)


def _attn_bwd(qkv, bias, do, dvec, lse, bps, name):
    S = qkv.shape[0]
    nt = S // ATTN_TILE
    scale = HEAD_DIM ** -0.5

    def assemble(parts):
        rows = [parts[0][:ATTN_BLOCK]]
        for b in range(ATTN_SUB - 1):
            rows.append(parts[b][ATTN_BLOCK:] + parts[b + 1][:ATTN_BLOCK])
        rows.append(parts[-1][ATTN_BLOCK:])
        return rows

    def body(q_ref, kp_ref, kc_ref, vp_ref, vc_ref, b_ref, do_ref, dvec_ref, lse_ref,
             dq_ref, dk_ref, dv_ref, db_ref, ck, cv):
        t = pl.program_id(1)
        last = ATTN_TILE - ATTN_BLOCK

        @pl.when(t == 0)
        def _():
            ck[...] = jnp.zeros_like(ck)
            cv[...] = jnp.zeros_like(cv)
            db_ref[...] = jnp.zeros_like(db_ref)

        @pl.when(t < nt)
        def _():
            kk = jnp.concatenate([kp_ref[...], kc_ref[...]], axis=0)
            vv = jnp.concatenate([vp_ref[...], vc_ref[...]], axis=0)
            low = _head_masks()
            col = lax.broadcasted_iota(jnp.int32, (ATTN_BLOCK, 2 * ATTN_BLOCK), 1)
            low2 = lax.broadcasted_iota(jnp.int32, (2 * ATTN_BLOCK, LANE), 1) < HEAD_DIM
            dk_parts, dv_parts = [], []
            dsum = [None, None]
            for b in range(ATTN_SUB):
                lo = b * ATTN_BLOCK
                rows = slice(lo, lo + ATTN_BLOCK)
                keys = slice(lo, lo + 2 * ATTN_BLOCK)
                dead = jnp.logical_and((t * ATTN_SUB + b) % bps == 0, col < ATTN_BLOCK)
                q2 = q_ref[rows, :]
                kb, vb = kk[keys], vv[keys]
                do2 = do_ref[rows, :].astype(bf16)
                dvec2 = dvec_ref[rows, :]
                lse2 = lse_ref[rows, :]
                dqs, dks, dvs = [], [], []
                for h in range(2):
                    hm = low if h == 0 else jnp.logical_not(low)
                    c0 = h * HEAD_DIM
                    qh = jnp.where(hm, q2, jnp.zeros_like(q2))
                    doh = jnp.where(hm, do2, jnp.zeros_like(do2))
                    s = lax.dot_general(qh, kb, NT, preferred_element_type=f32) * scale + b_ref[h]
                    s = jnp.where(dead, NEG_INF, s)
                    p = jnp.exp(s - lse2[:, c0:c0 + 1])
                    dp = lax.dot_general(doh, vb, NT, preferred_element_type=f32)
                    ds = p * (dp - dvec2[:, c0:c0 + 1])
                    dsum[h] = ds if dsum[h] is None else dsum[h] + ds
                    dsb = ds.astype(bf16)
                    dqs.append(jnp.dot(dsb, kb, preferred_element_type=f32) * scale)
                    dks.append(lax.dot_general(dsb, q2, TN, preferred_element_type=f32) * scale)
                    dvs.append(lax.dot_general(p.astype(bf16), do2, TN, preferred_element_type=f32))
                dq_ref[rows, :] = jnp.where(low, dqs[0], dqs[1]).astype(bf16)
                dk_parts.append(jnp.where(low2, dks[0], dks[1]))
                dv_parts.append(jnp.where(low2, dvs[0], dvs[1]))
            db_ref[0] += dsum[0]
            db_ref[1] += dsum[1]
            for parts, carry, out_ref in ((dk_parts, ck, dk_ref), (dv_parts, cv, dv_ref)):
                rws = assemble(parts)
                out_ref[:last, :] = carry[:last, :].astype(bf16)
                out_ref[last:, :] = (carry[last:, :] + rws[0]).astype(bf16)
                for b in range(ATTN_SUB):
                    carry[b * ATTN_BLOCK:(b + 1) * ATTN_BLOCK, :] = rws[b + 1]

        @pl.when(t == nt)
        def _():
            dk_ref[...] = ck[...].astype(bf16)
            dv_ref[...] = cv[...].astype(bf16)

    tile = (ATTN_TILE, LANE)
    cur = pl.BlockSpec(tile, lambda h, t: (jnp.minimum(t, nt - 1), h))
    lag = pl.BlockSpec(tile, lambda h, t: (jnp.maximum(t - 1, 0), h))
    bspec = pl.BlockSpec((2, ATTN_BLOCK, 2 * ATTN_BLOCK), lambda h, t: (h, 0, 0))
    return pl.pallas_call(
        body,
        grid=(4, nt + 1),
        in_specs=_qkv_specs(nt) + [bspec, cur, cur, cur],
        out_specs=[cur, lag, lag, bspec],
        out_shape=[SDS((S, ATTN_OUT), bf16), SDS((S, ATTN_OUT), bf16), SDS((S, ATTN_OUT), bf16),
                   SDS((8, ATTN_BLOCK, 2 * ATTN_BLOCK), f32)],
        scratch_shapes=[pltpu.VMEM(tile, f32), pltpu.VMEM(tile, f32)],
        compiler_params=_cparams(("parallel", "arbitrary")),
        name=name,
    )(qkv, qkv, qkv, qkv, qkv, bias, do, dvec, lse)


def _attn_merge(o0, o1, o2, l0, l1, l2):
    S, W = o0.shape
    R = PERM_ROWS

    def body(o0_ref, o1_ref, o2_ref, l0_ref, l1_ref, l2_ref, y_ref, yb_ref, w0_ref, w1_ref, w2_ref,
             so1, so2, sl1, sl2):
        _to_natural(o1_ref, so1, 4)
        _to_natural(l1_ref, sl1, 4)
        _to_natural(o2_ref, so2, 16)
        _to_natural(l2_ref, sl2, 16)
        a, b, c = l0_ref[...], sl1[...], sl2[...]
        m = jnp.maximum(jnp.maximum(a, b), c)
        ea, eb, ec = jnp.exp(a - m), jnp.exp(b - m), jnp.exp(c - m)
        den = (ea + eb) + ec
        w0, w1, w2 = ea / den, eb / den, ec / den
        y = (w0 * o0_ref[...] + w1 * so1[...]) + w2 * so2[...]
        y_ref[...] = y
        yb_ref[...] = y.astype(bf16)
        w0_ref[...] = w0
        w1_ref[...] = w1
        w2_ref[...] = w2

    nat = pl.BlockSpec((R, LANE), lambda i, j: (i, j))
    v4 = lambda t: t.reshape(4, S // 4, W)
    v16 = lambda t: t.reshape(16, S // 16, W)
    return pl.pallas_call(
        body,
        grid=(S // R, W // LANE),
        in_specs=[nat, _perm_spec(4), _perm_spec(16)] * 2,
        out_specs=[nat] * 5,
        out_shape=[SDS((S, W), f32), SDS((S, W), bf16)] + [SDS((S, W), f32)] * 3,
        scratch_shapes=[pltpu.VMEM((R, LANE), f32)] * 4,
        compiler_params=_cparams(("parallel", "parallel")),
        name="attn_merge",
    )(o0, v4(o1), v16(o2), l0, v4(l1), v16(l2))


def _attn_merge_bwd(dy, y, w0, w1, w2):
    S, W = dy.shape
    R = PERM_ROWS

    def body(dy_ref, y_ref, w0_ref, w1_ref, w2_ref, a0, a1, a2, b0, b1, b2, sa, sb):
        dyv = dy_ref[...]
        r = lax.broadcasted_iota(jnp.int32, (LANE, LANE), 0) // HEAD_DIM
        c = lax.broadcasted_iota(jnp.int32, (LANE, LANE), 1) // HEAD_DIM
        seg = jnp.where(r == c, 1.0, 0.0).astype(f32)
        cbar = jnp.dot(dyv * y_ref[...], seg, precision=HIGHEST, preferred_element_type=f32)
        w = w0_ref[...]
        a0[...] = (w * dyv).astype(bf16)
        b0[...] = w * cbar
        for d, w_ref, a_ref, b_ref in ((4, w1_ref, a1, b1), (16, w2_ref, a2, b2)):
            w = w_ref[...]
            sa[...] = w * dyv
            sb[...] = w * cbar
            n = R // d
            for k in range(d):
                rows = pl.ds(k, n, stride=d)
                a_ref[k] = sa[rows, :].astype(bf16)
                b_ref[k] = sb[rows, :]

    nat = pl.BlockSpec((R, LANE), lambda i, j: (i, j))
    shapes = lambda dt: [SDS((S, W), dt), SDS((4, S // 4, W), dt), SDS((16, S // 16, W), dt)]
    outs = pl.pallas_call(
        body,
        grid=(S // R, W // LANE),
        in_specs=[nat] * 5,
        out_specs=[nat, _perm_spec(4), _perm_spec(16)] * 2,
        out_shape=shapes(bf16) + shapes(f32),
        scratch_shapes=[pltpu.VMEM((R, LANE), f32)] * 2,
        compiler_params=_cparams(("parallel", "parallel")),
        name="attn_merge_bwd",
    )(dy, y, w0, w1, w2)
    return [t.reshape(S, W) for t in outs]


HGRN_SB = 256


def _chunk_masks():
    r = jnp.arange(HGRN_SB)[:, None]
    c = jnp.arange(HGRN_SB)[None, :]
    same = (r // HGRN_CHUNK) == (c // HGRN_CHUNK)
    return jnp.stack([same & (c <= r), same, same & (c >= r)]).astype(bf16)


def _mask_dot(mask, x):
    hi = x.astype(bf16)
    r1 = x - hi.astype(f32)
    mid = r1.astype(bf16)
    lo = (r1 - mid.astype(f32)).astype(bf16)
    p = jnp.dot(mask, jnp.concatenate([hi, mid, lo], axis=1), preferred_element_type=f32)
    n = x.shape[1]
    return (p[:, :n] + p[:, n:2 * n]) + p[:, 2 * n:]


def _hgrn_prep(q_raw, f_raw, lbv, tril, same):
    sq = _sigmoid(q_raw)
    qs = q_raw * sq
    sig = _sigmoid(f_raw)
    f = lbv + (1.0 - lbv) * sig
    g = jnp.log(f)
    k = 1.0 - f
    G = _mask_dot(tril, g)
    GL = _mask_dot(same, g)
    eG = jnp.exp(G)
    einv = jnp.exp(-G)
    edec = jnp.exp(GL - G)
    return dict(sq=sq, qs=qs, sig=sig, f=f, k=k, eG=eG, einv=einv, edec=edec, eGL=jnp.exp(GL),
                qt=qs * eG, kt=k * einv, kd=k * edec)


def _ride_split(ride, rest, n_out, n_scratch):
    if ride is None:
        return None, rest[:n_out], None, rest[n_out:], None
    return rest[0], rest[1:1 + n_out], rest[1 + n_out], rest[2 + n_out:2 + n_out + n_scratch], rest[2 + n_out + n_scratch:]


def _hgrn_fwd(hg, lb, normw, ride=None):
    S = hg.shape[0]
    sb = HGRN_SB
    nsb = S // sb
    nch = sb // HGRN_CHUNK

    def body(q_ref, f_ref, v_ref, og_ref, lb_ref, nw_ref, m_ref, *rest):
        src_ref, (y_ref, o_ref, ck_ref), got_ref, (st,), sems = _ride_split(ride, rest, 3, 1)
        j = pl.program_id(1)
        if ride is not None:
            @pl.when(jnp.logical_and(pl.program_id(0) == 0, j == 0))
            def _():
                _chip_start(src_ref, got_ref, sems[0], sems[1], ride[1])

        @pl.when(j == 0)
        def _():
            st[...] = jnp.zeros_like(st)

        ST = st[...]
        ck_ref[0, 0] = ST
        tril_m = m_ref[0]
        tril = tril_m.astype(f32) > 0.5
        pr = _hgrn_prep(q_ref[...], f_ref[...], lb_ref[...], tril_m, m_ref[1])
        qtb, ktb, kdb = pr["qt"].astype(bf16), pr["kt"].astype(bf16), pr["kd"].astype(bf16)
        eGL = pr["eGL"]
        vb = v_ref[...].astype(bf16)
        A = jnp.where(tril, lax.dot_general(qtb, ktb, NT, preferred_element_type=f32), 0.0)
        o = jnp.dot(A.astype(bf16), vb, preferred_element_type=f32)
        outs = []
        for ci in range(nch):
            lo = ci * HGRN_CHUNK
            sl = slice(lo, lo + HGRN_CHUNK)
            outs.append(o[sl] + lax.dot_general(qtb[sl], ST.astype(bf16), NT, preferred_element_type=f32))
            ST = ST * eGL[lo:lo + 1, :] + lax.dot_general(vb[sl], kdb[sl], TN, preferred_element_type=f32)
        st[...] = ST
        of = jnp.concatenate(outs, axis=0)
        o_ref[...] = of
        rms = lax.rsqrt(jnp.mean(of * of, axis=-1, keepdims=True) + EPS)
        ogv = og_ref[...]
        y_ref[...] = ((of * rms * nw_ref[...]) * (ogv * _sigmoid(ogv))).astype(bf16)

        if ride is not None:
            @pl.when(jnp.logical_and(pl.program_id(0) == 3, j == nsb - 1))
            def _():
                _chip_finish(src_ref, got_ref, sems[0], sems[1], ride[1])

    col = lambda off: pl.BlockSpec((sb, LANE), lambda h, j: (j, off + h))
    riding = ride is not None
    res = pl.pallas_call(
        body,
        grid=(4, nsb),
        in_specs=[col(0), col(4), col(8), col(12), pl.BlockSpec((1, LANE), lambda h, j: (0, h)),
                  pl.BlockSpec((1, LANE), lambda h, j: (0, 0)),
                  pl.BlockSpec((3, sb, sb), lambda h, j: (0, 0, 0))] + ([_ANY] if riding else []),
        out_specs=[col(0), col(0), pl.BlockSpec((1, 1, LANE, LANE), lambda h, j: (h, j, 0, 0))]
        + ([_ANY] if riding else []),
        out_shape=[SDS((S, HGRN_W), bf16), SDS((S, HGRN_W), f32), SDS((4, nsb, LANE, LANE), f32)]
        + ([_chip_out_shape(*ride)] if riding else []),
        scratch_shapes=[pltpu.VMEM((LANE, LANE), f32)] + (list(_CHIP_SEMS) if riding else []),
        compiler_params=_cparams(("arbitrary", "arbitrary") if riding else ("parallel", "arbitrary")),
        name="hgrn_fwd",
    )(hg, hg, hg, hg, lb, normw, _chunk_masks(), *([ride[0]] if riding else []))
    return tuple(res) if riding else (*res, None)


def _hgrn_bwd(hg, o_raw, dy, ck, lb, normw, ride=None):
    S = hg.shape[0]
    sb = HGRN_SB
    nsb = S // sb
    nch = sb // HGRN_CHUNK

    def body(q_ref, f_ref, v_ref, og_ref, o_ref, dy_ref, ck_ref, lb_ref, nw_ref, m_ref, *rest):
        src_ref, outs, got_ref, (dst, alb, anw), sems = _ride_split(ride, rest, 6, 3)
        dq_ref, df_ref, dv_ref, dog_ref, glb_ref, gnw_ref = outs
        j = pl.program_id(1)
        if ride is not None:
            @pl.when(jnp.logical_and(pl.program_id(0) == 0, j == 0))
            def _():
                _chip_start(src_ref, got_ref, sems[0], sems[1], ride[1])

        @pl.when(j == 0)
        def _():
            dst[...] = jnp.zeros_like(dst)
            alb[...] = jnp.zeros_like(alb)
            anw[...] = jnp.zeros_like(anw)

        tril_m = m_ref[0]
        tril = tril_m.astype(f32) > 0.5
        lbv = lb_ref[...]
        q_raw = q_ref[...]
        pr = _hgrn_prep(q_raw, f_ref[...], lbv, tril_m, m_ref[1])
        qt, kt, kd, eGL = pr["qt"], pr["kt"], pr["kd"], pr["eGL"]
        qtb, ktb, kdb = qt.astype(bf16), kt.astype(bf16), kd.astype(bf16)
        vb = v_ref[...].astype(bf16)

        o = o_ref[...]
        ogv = og_ref[...]
        sog = _sigmoid(ogv)
        rms = lax.rsqrt(jnp.mean(o * o, axis=-1, keepdims=True) + EPS)
        oh = o * rms
        nw = nw_ref[...]
        dyv = dy_ref[...]
        dog_ref[...] = (dyv * (oh * nw) * (sog * (1.0 + ogv * (1.0 - sog)))).astype(bf16)
        dohw = dyv * (ogv * sog)
        anw[...] += _colsum8(dohw * oh)
        doh = dohw * nw
        do = rms * (doh - oh * jnp.mean(doh * oh, axis=-1, keepdims=True))
        dob = do.astype(bf16)

        Ab = jnp.where(tril, lax.dot_general(qtb, ktb, NT, preferred_element_type=f32), 0.0).astype(bf16)
        dAb = jnp.where(tril, lax.dot_general(dob, vb, NT, preferred_element_type=f32), 0.0).astype(bf16)
        dv_acc = lax.dot_general(Ab, dob, TN, preferred_element_type=f32)
        dqt = jnp.dot(dAb, ktb, preferred_element_type=f32)
        dkt = lax.dot_general(dAb, qtb, TN, preferred_element_type=f32)

        ST = ck_ref[0, 0]
        states = []
        for ci in range(nch):
            lo = ci * HGRN_CHUNK
            sl = slice(lo, lo + HGRN_CHUNK)
            states.append(ST)
            ST = ST * eGL[lo:lo + 1, :] + lax.dot_general(vb[sl], kdb[sl], TN, preferred_element_type=f32)

        dST = dst[...]
        dqt_i, dkd_i, dv_i, deg_i = [None] * nch, [None] * nch, [None] * nch, [None] * nch
        for ci in reversed(range(nch)):
            lo = ci * HGRN_CHUNK
            sl = slice(lo, lo + HGRN_CHUNK)
            ST0 = states[ci]
            dSTb = dST.astype(bf16)
            dv_i[ci] = lax.dot_general(kdb[sl], dSTb, NT, preferred_element_type=f32)
            dqt_i[ci] = jnp.dot(dob[sl], ST0.astype(bf16), preferred_element_type=f32)
            dkd_i[ci] = jnp.dot(vb[sl], dSTb, preferred_element_type=f32)
            deg_i[ci] = jnp.broadcast_to(jnp.sum(dST * ST0, axis=0, keepdims=True), (HGRN_CHUNK, LANE))
            dST = dST * eGL[lo:lo + 1, :] + lax.dot_general(dob[sl], qtb[sl], TN, preferred_element_type=f32)
        dst[...] = dST

        dqt = dqt + jnp.concatenate(dqt_i, axis=0)
        dkd = jnp.concatenate(dkd_i, axis=0)
        dv_ref[...] = (dv_acc + jnp.concatenate(dv_i, axis=0)).astype(bf16)
        deg = jnp.concatenate(deg_i, axis=0)

        dqs = dqt * pr["eG"]
        dkdkd = dkd * kd
        dG = dqt * qt - dkt * kt - dkdkd
        dk = dkt * pr["einv"] + dkd * pr["edec"]
        dGL = _mask_dot(m_ref[1], dkdkd) + eGL * deg
        dg = _mask_dot(m_ref[2], dG) + dGL
        df = dg / pr["f"] - dk
        sig = pr["sig"]
        df_ref[...] = (df * (1.0 - lbv) * (sig * (1.0 - sig))).astype(bf16)
        alb[...] += _colsum8(df * (1.0 - sig))
        sq = pr["sq"]
        dq_ref[...] = (dqs * (sq * (1.0 + q_raw * (1.0 - sq)))).astype(bf16)

        @pl.when(j == nsb - 1)
        def _():
            glb_ref[...] = jnp.broadcast_to(jnp.sum(alb[...], axis=0, keepdims=True), (SUBLANE, LANE))
            gnw_ref[...] = jnp.broadcast_to(jnp.sum(anw[...], axis=0, keepdims=True), (SUBLANE, LANE))

        if ride is not None:
            @pl.when(jnp.logical_and(pl.program_id(0) == 3, j == nsb - 1))
            def _():
                _chip_finish(src_ref, got_ref, sems[0], sems[1], ride[1])

    rev = lambda off: pl.BlockSpec((sb, LANE), lambda h, j: (nsb - 1 - j, off + h))
    stat = pl.BlockSpec((SUBLANE, LANE), lambda h, j: (0, h))
    riding = ride is not None
    res = pl.pallas_call(
        body,
        grid=(4, nsb),
        in_specs=[rev(0), rev(4), rev(8), rev(12), rev(0), rev(0),
                  pl.BlockSpec((1, 1, LANE, LANE), lambda h, j: (h, nsb - 1 - j, 0, 0)),
                  pl.BlockSpec((1, LANE), lambda h, j: (0, h)), pl.BlockSpec((1, LANE), lambda h, j: (0, 0)),
                  pl.BlockSpec((3, sb, sb), lambda h, j: (0, 0, 0))]
        + ([_ANY] if riding else []),
        out_specs=[rev(0), rev(0), rev(0), rev(0), stat, stat] + ([_ANY] if riding else []),
        out_shape=[SDS((S, HGRN_W), bf16)] * 4 + [SDS((SUBLANE, HGRN_W), f32)] * 2
        + ([_chip_out_shape(*ride)] if riding else []),
        scratch_shapes=[pltpu.VMEM((LANE, LANE), f32), pltpu.VMEM((SUBLANE, LANE), f32),
                        pltpu.VMEM((SUBLANE, LANE), f32)] + (list(_CHIP_SEMS) if riding else []),
        compiler_params=_cparams(("arbitrary", "arbitrary") if riding else ("parallel", "arbitrary")),
        name="hgrn_bwd",
    )(hg, hg, hg, hg, o_raw, dy, ck, lb, normw, _chunk_masks(), *([ride[0]] if riding else []))
    return tuple(res) if riding else (*res, None)


def _lb_fwd(raw):
    def body(r_ref, o_ref):
        r = r_ref[...]
        m = jnp.max(r, axis=0, keepdims=True)
        e = jnp.exp(r - m)
        o_ref[...] = (e / jnp.sum(e, axis=0, keepdims=True))[0:1]

    return pl.pallas_call(body, out_shape=SDS((1, raw.shape[1]), f32), name="lb_fwd")(raw)


def _lb_bwd(raw, dlb):
    def body(r_ref, d_ref, o_ref):
        r = r_ref[...]
        m = jnp.max(r, axis=0, keepdims=True)
        e = jnp.exp(r - m)
        s = e / jnp.sum(e, axis=0, keepdims=True)
        s0 = s[0:1]
        onehot0 = jnp.where(lax.broadcasted_iota(jnp.int32, r.shape, 0) == 0, 1.0, 0.0)
        o_ref[...] = d_ref[...] * s0 * (onehot0 - s)

    return pl.pallas_call(body, out_shape=SDS(raw.shape, f32), name="lb_bwd")(raw, dlb)


def _gate_fwd(a, b, gc):
    S, D = a.shape
    tm = _pick(S, 512)

    def body(a_ref, b_ref, g0_ref, g1_ref, o_ref):
        s0, s1 = _sigmoid(g0_ref[...].astype(f32)), _sigmoid(g1_ref[...].astype(f32))
        o_ref[...] = (s0 * a_ref[...].astype(f32) + s1 * b_ref[...].astype(f32)).astype(bf16)

    row = pl.BlockSpec((tm, D), lambda i: (i, 0))
    return pl.pallas_call(
        body,
        grid=(S // tm,),
        in_specs=[row, row, row, pl.BlockSpec((tm, D), lambda i: (i, 1))],
        out_specs=row,
        out_shape=SDS((S, D), bf16),
        compiler_params=_cparams(("parallel",)),
        name="gate_fwd",
    )(a, b, gc, gc)


def _gate_bwd(dm, a, b, gc):
    S, D = a.shape
    tm = _pick(S, 512)

    def body(dm_ref, a_ref, b_ref, g0_ref, g1_ref, da_ref, db_ref, dg_ref):
        dmv = dm_ref[...].astype(f32)
        s0, s1 = _sigmoid(g0_ref[...].astype(f32)), _sigmoid(g1_ref[...].astype(f32))
        da_ref[...] = (dmv * s0).astype(bf16)
        db_ref[...] = (dmv * s1).astype(bf16)
        dg_ref[:, :D] = (dmv * a_ref[...].astype(f32) * (s0 * (1.0 - s0))).astype(bf16)
        dg_ref[:, D:] = (dmv * b_ref[...].astype(f32) * (s1 * (1.0 - s1))).astype(bf16)

    row = pl.BlockSpec((tm, D), lambda i: (i, 0))
    wide = pl.BlockSpec((tm, 2 * D), lambda i: (i, 0))
    return pl.pallas_call(
        body,
        grid=(S // tm,),
        in_specs=[row, row, row, row, pl.BlockSpec((tm, D), lambda i: (i, 1))],
        out_specs=[row, row, wide],
        out_shape=[SDS((S, D), bf16), SDS((S, D), bf16), SDS((S, 2 * D), bf16)],
        compiler_params=_cparams(("parallel",)),
        name="gate_bwd",
    )(dm, a, b, gc, gc)


CONV_ROWS = 512
INV_SQRT2 = 0.7071067811865476
INV_SQRT_2PI = 0.3989422804014327


CONV_HALO = 16


def _tile8(a, rows):
    return jnp.tile(a, (rows // a.shape[0], 1))


def _conv_rows(u_ref, w, b, r0, first):
    R = CONV_ROWS
    cur = u_ref[pl.ds(r0, R), :].astype(f32)
    prev8 = u_ref[pl.ds(pl.multiple_of(jnp.maximum(r0 - CONV_HALO, 0), CONV_HALO), CONV_HALO), :].astype(f32)
    prev8 = jnp.where(first, 0.0, prev8)
    row = lax.broadcasted_iota(jnp.int32, (R, LANE), 0)
    x1 = jnp.where(row < 1, _tile8(pltpu.roll(prev8, 1, 0), R), pltpu.roll(cur, 1, 0))
    x2 = jnp.where(row < 2, _tile8(pltpu.roll(prev8, 2, 0), R), pltpu.roll(cur, 2, 0))
    c = ((b + w[0:1] * x2) + w[1:2] * x1) + w[2:3] * cur
    return c, x2, x1, cur


def _conv_fwd(ug, uv, wg, wv, bg, bv):
    S, F = ug.shape
    nchunk = S // CONV_ROWS

    def body(ug_ref, uv_ref, wg_ref, wv_ref, bg_ref, bv_ref, o_ref):
        wgv, wvv, bgv, bvv = wg_ref[...], wv_ref[...], bg_ref[...], bv_ref[...]

        def step(ci, carry):
            r0 = pl.multiple_of(ci * CONV_ROWS, CONV_ROWS)
            cg = _conv_rows(ug_ref, wgv, bgv, r0, ci == 0)[0]
            cv = _conv_rows(uv_ref, wvv, bvv, r0, ci == 0)[0]
            gelu = 0.5 * cg * (1.0 + lax.erf(cg * INV_SQRT2))
            o_ref[pl.ds(r0, CONV_ROWS), :] = (gelu * cv).astype(bf16)
            return carry

        lax.fori_loop(0, nchunk, step, 0)

    col = pl.BlockSpec((S, LANE), lambda j: (0, j))
    w3 = pl.BlockSpec((3, LANE), lambda j: (0, j))
    b1 = pl.BlockSpec((1, LANE), lambda j: (0, j))
    return pl.pallas_call(
        body,
        grid=(F // LANE,),
        in_specs=[col, col, w3, w3, b1, b1],
        out_specs=col,
        out_shape=SDS((S, F), bf16),
        compiler_params=_cparams(("parallel",), VMEM_BIG),
        name="conv_fwd",
    )(ug, uv, wg, wv, bg, bv)


def _conv_bwd(ug, uv, dact, wg, wv, bg, bv):
    S, F = ug.shape
    R = CONV_ROWS
    nchunk = S // R

    def body(ug_ref, uv_ref, da_ref, wg_ref, wv_ref, bg_ref, bv_ref, dug_ref, duv_ref, sg_ref, sv_ref, dcg, dcv):
        wgv, wvv, bgv, bvv = wg_ref[...], wv_ref[...], bg_ref[...], bv_ref[...]
        zero = jnp.zeros((SUBLANE, LANE), f32)

        def fwd_step(ci, acc):
            r0 = pl.multiple_of(ci * R, R)
            cg, g2, g1, g0 = _conv_rows(ug_ref, wgv, bgv, r0, ci == 0)
            cv, v2, v1, v0 = _conv_rows(uv_ref, wvv, bvv, r0, ci == 0)
            da = da_ref[pl.ds(r0, R), :].astype(f32)
            cdf = 0.5 * (1.0 + lax.erf(cg * INV_SQRT2))
            pdf = INV_SQRT_2PI * jnp.exp(-0.5 * cg * cg)
            dg = da * cv * (cdf + cg * pdf)
            dv = da * (cg * cdf)
            dcg[pl.ds(r0, R), :] = dg
            dcv[pl.ds(r0, R), :] = dv
            new = (acc[0] + _colsum8(dg * g2), acc[1] + _colsum8(dg * g1), acc[2] + _colsum8(dg * g0),
                   acc[3] + _colsum8(dg),
                   acc[4] + _colsum8(dv * v2), acc[5] + _colsum8(dv * v1), acc[6] + _colsum8(dv * v0),
                   acc[7] + _colsum8(dv))
            return new

        acc = lax.fori_loop(0, nchunk, fwd_step, (zero,) * 8)
        rows = lax.broadcasted_iota(jnp.int32, (SUBLANE, LANE), 0)

        def stats(parts):
            out = jnp.zeros((SUBLANE, LANE), f32)
            for k, pt in enumerate(parts):
                out = jnp.where(rows == k, jnp.sum(pt, axis=0, keepdims=True), out)
            return out

        sg_ref[...] = stats(acc[0:4])
        sv_ref[...] = stats(acc[4:8])

        def du_rows(dc, w, r0, last):
            cur = dc[pl.ds(r0, R), :]
            nxt = dc[pl.ds(pl.multiple_of(jnp.minimum(r0 + R, S - SUBLANE), SUBLANE), SUBLANE), :]
            nxt = jnp.where(last, 0.0, nxt)
            row = lax.broadcasted_iota(jnp.int32, (R, LANE), 0)
            y1 = jnp.where(row >= R - 1, _tile8(pltpu.roll(nxt, SUBLANE - 1, 0), R), pltpu.roll(cur, R - 1, 0))
            y2 = jnp.where(row >= R - 2, _tile8(pltpu.roll(nxt, SUBLANE - 2, 0), R), pltpu.roll(cur, R - 2, 0))
            return w[2:3] * cur + w[1:2] * y1 + w[0:1] * y2

        def bwd_step(ci, carry):
            r0 = pl.multiple_of(ci * R, R)
            last = ci == nchunk - 1
            dug_ref[pl.ds(r0, R), :] = du_rows(dcg, wgv, r0, last).astype(bf16)
            duv_ref[pl.ds(r0, R), :] = du_rows(dcv, wvv, r0, last).astype(bf16)
            return carry

        lax.fori_loop(0, nchunk, bwd_step, 0)

    col = pl.BlockSpec((S, LANE), lambda j: (0, j))
    w3 = pl.BlockSpec((3, LANE), lambda j: (0, j))
    b1 = pl.BlockSpec((1, LANE), lambda j: (0, j))
    st = pl.BlockSpec((SUBLANE, LANE), lambda j: (0, j))
    return pl.pallas_call(
        body,
        grid=(F // LANE,),
        in_specs=[col, col, col, w3, w3, b1, b1],
        out_specs=[col, col, st, st],
        out_shape=[SDS((S, F), bf16), SDS((S, F), bf16), SDS((SUBLANE, F), f32), SDS((SUBLANE, F), f32)],
        scratch_shapes=[pltpu.VMEM((S, LANE), f32), pltpu.VMEM((S, LANE), f32)],
        compiler_params=_cparams(("parallel",), VMEM_BIG),
        name="conv_bwd",
    )(ug, uv, dact, wg, wv, bg, bv)


def _adam_math(w, g, m, v):
    m = ADAM_B1 * m + (1.0 - ADAM_B1) * g
    v = ADAM_B2 * v + (1.0 - ADAM_B2) * (g * g)
    m_hat = m / (1.0 - ADAM_B1 ** ADAM_STEP)
    v_hat = v / (1.0 - ADAM_B2 ** ADAM_STEP)
    delta = -ADAM_LR * (m_hat / (jnp.sqrt(v_hat) + ADAM_EPS) + ADAM_WD * w)
    return delta, m, v


def _adamw(w, m, v, g, name):
    R, C = w.shape
    parts = g.ndim == 3
    tr = R
    for t in (256, 128, 64, 32, 16):
        if R % t == 0 and R > t:
            tr = t
            break

    def body(w_ref, m_ref, v_ref, g_ref, go_ref, d_ref, mo_ref, vo_ref):
        if parts:
            gv = ((g_ref[0].astype(f32) + g_ref[1].astype(f32)) + g_ref[2].astype(f32)) + g_ref[3].astype(f32)
        else:
            gv = g_ref[...]
        go_ref[...] = gv
        d, mn, vn = _adam_math(w_ref[...], gv, m_ref[...], v_ref[...])
        d_ref[...] = d
        mo_ref[...] = mn
        vo_ref[...] = vn

    row = pl.BlockSpec((tr, C), lambda i: (i, 0))
    gspec = pl.BlockSpec((4, tr, C), lambda i: (0, i, 0)) if parts else row
    return pl.pallas_call(
        body,
        grid=(R // tr,),
        in_specs=[row, row, row, gspec],
        out_specs=[row] * 4,
        out_shape=[SDS((R, C), f32)] * 4,
        compiler_params=_cparams(("parallel",)),
        name=name,
    )(w, m, v, g)


def _sum8(parts, name):
    _, _, R, C = parts.shape

    def body(p_ref, o_ref):
        acc = p_ref[0, 0]
        for c in range(2):
            for k in range(4):
                if c or k:
                    acc = acc + p_ref[c, k]
        o_ref[...] = acc

    return pl.pallas_call(body, out_shape=SDS((R, C), f32), name=name)(parts)


def _pair_add(by_core, b, name):
    _, K, R, C = by_core.shape
    tr = R // 2 if R % 32 == 0 else R

    def body(c_ref, a_ref, b_ref, o_ref):
        o_ref[...] = (a_ref[0].astype(f32) + b_ref[...].astype(f32)).astype(bf16)

    blk = pl.BlockSpec((1, tr, C), lambda k, i, c: (k, i, 0))
    return pl.pallas_call(
        body,
        grid_spec=pltpu.PrefetchScalarGridSpec(
            num_scalar_prefetch=1,
            grid=(K, R // tr),
            in_specs=[pl.BlockSpec((1, 1, tr, C), lambda k, i, c: (c[0], k, i, 0)), blk],
            out_specs=blk,
        ),
        out_shape=SDS((K, R, C), bf16),
        compiler_params=_cparams(("parallel", "parallel")),
        name=name,
    )(lax.axis_index("c").astype(jnp.int32).reshape(1), by_core, b)


_ANY = pl.BlockSpec(memory_space=pl.ANY)


def _chip_copies(src_ref, out_ref, send_sems, recv_sems, gather):
    x, y, c = lax.axis_index("x"), lax.axis_index("y"), lax.axis_index("c")
    mine = 2 * x + y

    def piece(k):
        return src_ref if gather else src_ref.at[k]

    sends, recvs = [], []
    for j, (px, py) in enumerate([(1 - x, y), (x, 1 - y), (1 - x, 1 - y)]):
        sends.append(pltpu.make_async_remote_copy(
            src_ref=piece(2 * px + py), dst_ref=out_ref.at[mine], send_sem=send_sems.at[j],
            recv_sem=recv_sems.at[j], device_id=(px, py, c), device_id_type=MESH))
        recvs.append(pltpu.make_async_remote_copy(
            src_ref=piece(mine), dst_ref=out_ref.at[2 * px + py], send_sem=send_sems.at[j],
            recv_sem=recv_sems.at[j], device_id=(px, py, c), device_id_type=MESH))
    return sends, recvs


def _chip_start(src_ref, out_ref, send_sems, recv_sems, gather):
    for cp in _chip_copies(src_ref, out_ref, send_sems, recv_sems, gather)[0]:
        cp.start()


def _chip_finish(src_ref, out_ref, send_sems, recv_sems, gather):
    sends, recvs = _chip_copies(src_ref, out_ref, send_sems, recv_sems, gather)
    for cp in recvs:
        cp.wait_recv()
    for cp in sends:
        cp.wait_send()


def _chip_out_shape(src, gather):
    return SDS((4,) + tuple(src.shape if gather else src.shape[1:]), src.dtype)


_CHIP_SEMS = [pltpu.SemaphoreType.DMA((3,)), pltpu.SemaphoreType.DMA((3,))]


def _fill_own(out, src, gather):
    mine = 2 * lax.axis_index("x") + lax.axis_index("y")
    own = src if gather else lax.dynamic_index_in_dim(src, mine, axis=0, keepdims=False)
    return lax.dynamic_update_index_in_dim(out, own, mine, axis=0)


def _chip_comm(src, gather, name):
    def body(src_ref, out_ref, send_sems, recv_sems):
        _chip_start(src_ref, out_ref, send_sems, recv_sems, gather)
        _chip_finish(src_ref, out_ref, send_sems, recv_sems, gather)

    out = pl.pallas_call(
        body,
        in_specs=[_ANY],
        out_specs=_ANY,
        out_shape=_chip_out_shape(src, gather),
        scratch_shapes=list(_CHIP_SEMS),
        name=name,
    )(src)
    return _fill_own(out, src, gather)


_HBM = pl.BlockSpec(memory_space=pltpu.HBM)
_SEM = pl.BlockSpec(memory_space=pltpu.SEMAPHORE)
_EFFECT = pltpu.SideEffectType.DATAFLOW_SIDE_EFFECTING
_N_SPLIT_SEMS = 6


def _split_copies(src_ref, land_ref, sems, gather):
    x, y, c = lax.axis_index("x"), lax.axis_index("y"), lax.axis_index("c")
    mine = 2 * x + y

    def piece(k):
        return src_ref if gather else src_ref.at[k]

    sends, recvs = [], []
    for j, (px, py) in enumerate([(1 - x, y), (x, 1 - y), (1 - x, 1 - y)]):
        sends.append(pltpu.make_async_remote_copy(
            src_ref=piece(2 * px + py), dst_ref=land_ref.at[mine], send_sem=sems[j], recv_sem=sems[3 + j],
            device_id=(px, py, c), device_id_type=MESH))
        recvs.append(pltpu.make_async_remote_copy(
            src_ref=piece(mine), dst_ref=land_ref.at[2 * px + py], send_sem=sems[j], recv_sem=sems[3 + j],
            device_id=(px, py, c), device_id_type=MESH))
    return sends, recvs


def _chip_comm_start(src, gather, name, after=None):
    land = _chip_out_shape(src, gather)
    n_in = 2 if after is None else 3

    def body(*refs):
        src_ref, land_ref = refs[:2]
        outs = refs[n_in:]
        for cp in _split_copies(src_ref, land_ref, outs[:_N_SPLIT_SEMS], gather)[0]:
            cp.start()
        token = outs[_N_SPLIT_SEMS + 2]
        token[...] = jnp.zeros_like(token)

    res = pl.pallas_call(
        body,
        name=name,
        out_shape=(pltpu.SemaphoreType.DMA(()),) * _N_SPLIT_SEMS
        + (pltpu.HBM(src.shape, src.dtype), pltpu.HBM(land.shape, land.dtype), SDS((SUBLANE, LANE), f32)),
        in_specs=(_HBM, _HBM) + (() if after is None else (_ANY,)),
        out_specs=(_SEM,) * _N_SPLIT_SEMS + (_HBM, _HBM, pl.BlockSpec(memory_space=pltpu.VMEM)),
        input_output_aliases={0: _N_SPLIT_SEMS, 1: _N_SPLIT_SEMS + 1},
        compiler_params=pltpu.CompilerParams(has_side_effects=_EFFECT),
    )(pltpu.with_memory_space_constraint(src, pltpu.HBM),
      pltpu.with_memory_space_constraint(lax.empty(land.shape, land.dtype), pltpu.HBM),
      *(() if after is None else (after,)))
    return (res[:_N_SPLIT_SEMS], res[_N_SPLIT_SEMS], res[_N_SPLIT_SEMS + 1]), res[_N_SPLIT_SEMS + 2]


def _chip_comm_wait(state, after, gather, name):
    sems, src_thru, land_thru = state

    def body(src_ref, land_ref, *rest):
        sends, recvs = _split_copies(src_ref, land_ref, rest[:_N_SPLIT_SEMS], gather)
        for cp in recvs:
            cp.wait_recv()
        for cp in sends:
            cp.wait_send()

    out = pl.pallas_call(
        body,
        name=name,
        out_shape=(pltpu.HBM(src_thru.shape, src_thru.dtype), pltpu.HBM(land_thru.shape, land_thru.dtype)),
        in_specs=(_HBM, _HBM) + (_SEM,) * _N_SPLIT_SEMS + (_ANY,),
        out_specs=(_HBM, _HBM),
        input_output_aliases={0: 0, 1: 1},
        compiler_params=pltpu.CompilerParams(has_side_effects=_EFFECT),
    )(src_thru, land_thru, *sems, after)
    return _fill_own(out[1], out[0], gather)


def _core_gather(src, name):
    def body(src_ref, out_ref, send_sem, recv_sem):
        x, y, c = lax.axis_index("x"), lax.axis_index("y"), lax.axis_index("c")
        cp = pltpu.make_async_remote_copy(src_ref=src_ref, dst_ref=out_ref.at[c], send_sem=send_sem,
                                          recv_sem=recv_sem, device_id=(x, y, 1 - c), device_id_type=MESH)
        cp.start()
        pltpu.make_async_remote_copy(src_ref=src_ref, dst_ref=out_ref.at[1 - c], send_sem=send_sem,
                                     recv_sem=recv_sem, device_id=(x, y, 1 - c), device_id_type=MESH).wait_recv()
        cp.wait_send()

    out = pl.pallas_call(
        body,
        in_specs=[_ANY],
        out_specs=_ANY,
        out_shape=SDS((2,) + tuple(src.shape), src.dtype),
        scratch_shapes=[pltpu.SemaphoreType.DMA, pltpu.SemaphoreType.DMA],
        name=name,
    )(src)
    return lax.dynamic_update_index_in_dim(out, src, lax.axis_index("c"), axis=0)


def _core_swap(src, name):
    def body(src_ref, out_ref, send_sem, recv_sem):
        x, y, c = lax.axis_index("x"), lax.axis_index("y"), lax.axis_index("c")
        cp = pltpu.make_async_remote_copy(src_ref=src_ref.at[1 - c], dst_ref=out_ref, send_sem=send_sem,
                                          recv_sem=recv_sem, device_id=(x, y, 1 - c), device_id_type=MESH)
        cp.start()
        cp.wait()

    return pl.pallas_call(
        body,
        in_specs=[_ANY],
        out_specs=_ANY,
        out_shape=SDS(tuple(src.shape[1:]), src.dtype),
        scratch_shapes=[pltpu.SemaphoreType.DMA, pltpu.SemaphoreType.DMA],
        name=name,
    )(src)


def _all_gather(src, tag):
    by_chip = _chip_comm(src, True, tag + "_chips")
    both = _core_gather(by_chip, tag + "_cores")
    return jnp.swapaxes(both, 0, 1).reshape((8,) + tuple(src.shape))


_PACK_A = (("w_in", (1024, 1088)),)
_PACK_B = (("w_ba", (512, 128)), ("w_bh", (512, 128)), ("w_out", (128, 1024)), ("w_up", (1024, 704)),
           ("w_down", (352, 1024)))
_PACK_SIZES = _PACK_A + _PACK_B


def _slab_rows(sizes):
    return sum(r * c for _, (r, c) in sizes) // D_MODEL


def _pack_rows(d, sizes):
    n = d[sizes[0][0]].shape[0]
    return jnp.concatenate([d[k].reshape(n, -1, D_MODEL) for k, _ in sizes], axis=1)


def _unpack_rows(slab, sizes):
    n = slab.shape[0]
    out, lo = {}, 0
    for key, (r, c) in sizes:
        rows = r * c // D_MODEL
        out[key] = slab[:, lo:lo + rows].reshape(n, r, c)
        lo += rows
    return out


def _by_core(gslab):
    return jnp.swapaxes(gslab.reshape((4, 2) + gslab.shape[1:]), 0, 1)


def _pair_sum(by_core, tag):
    return _pair_add(by_core, _core_swap(by_core, tag + "_cores"), tag + "_pair_add")


def _cols_to_full(t):
    return jnp.swapaxes(t, 0, 1).reshape(t.shape[1], -1)


def _full_to_cols(t):
    K = t.shape[0]
    return jnp.swapaxes(t.reshape(K, 8, -1), 0, 1)


_SMALL = (("pre_mix_norm", (1, 1024)), ("rel_bias", (32, 24)), ("hgrn_lb_raw", (2, 512)), ("hgrn_norm", (1, 128)),
          ("post_mix_norm", (1, 1024)), ("pre_ffn_norm", (1, 1024)), ("conv_b", (1, 5632)),
          ("post_ffn_norm", (1, 1024)))
_SMALL_ROWS = 96
_CONVW_ROWS = 136


_SMALL_USED = sum(r * c for _, (r, c) in _SMALL)


def _pack_small(d, extra=None):
    flat = jnp.concatenate([d[k].reshape(-1) for k, _ in _SMALL] + ([] if extra is None else [extra.reshape(-1)]))
    flat = jnp.pad(flat, (0, _SMALL_ROWS * LANE - flat.shape[0]))
    return flat.reshape(_SMALL_ROWS, LANE)


def _unpack_small(p):
    flat = p.reshape(-1)
    out, lo = {}, 0
    for k, shp in _SMALL:
        n = shp[0] * shp[1]
        out[k] = flat[lo:lo + n].reshape(shp)
        lo += n
    return out


def _local_step(x, tgt, P, plan):
    S = x.shape[0]
    P = dict(P)
    lb = _lb_fwd(P["hgrn_lb_raw"])
    hs = _prep(x, P["pre_mix_norm"], plan.start_token())
    h1 = hs[0]
    W = dict(plan.weights_a(h1))
    consts = [_bias_consts(d) for d in DILATIONS]
    qkv, obuf, lbuf, biases = [], [], [], []
    for g, d in enumerate(DILATIONS):
        qkv_g = _mm(hs[g], W["w_qkv"][g], "nn", bf16, f"proj_qkv{g}")
        tab_t = P["rel_bias"][:, 8 * g:8 * g + 8].T
        bias_g = _bias_build(tab_t, consts[g][0], consts[g][1], f"bias_build{g}").reshape(8, ATTN_BLOCK, 2 * ATTN_BLOCK)
        o_g, l_g = _attn_fwd(qkv_g, bias_g, (S // d) // ATTN_BLOCK, f"attn_fwd{g}")
        qkv.append(qkv_g)
        biases.append(bias_g)
        lbuf.append(l_g)
        obuf.append(o_g)
    y_attn, y_attn_b, w0, w1, w2 = _attn_merge(obuf[0], obuf[1], obuf[2], lbuf[0], lbuf[1], lbuf[2])
    hg = _mm(h1, W["w_hg"], "nn", f32, "proj_hg")
    gc = _mm(h1, W["w_gate"], "nn", bf16, "proj_gate")
    y_hgrn, o_raw, ck, _ = _hgrn_fwd(hg, lb, P["hgrn_norm"])
    wb = plan.weights_b(y_hgrn)
    P["conv_w"] = wb.pop("conv_w")
    W.update(wb)
    a = _mm(y_attn_b, W["w_ba"], "nn", bf16, "branch_attn")
    b = _mm(y_hgrn, W["w_bh"], "nn", bf16, "branch_hgrn")
    merged = _gate_fwd(a, b, gc)
    mo = _mm(merged, W["w_out"], "nn", f32, "out_proj")
    x1, h2 = _mid_fwd(x, mo, P["post_mix_norm"], P["pre_ffn_norm"])
    ug = _mm(h2, W["w_up_g"], "nn", bf16, "up_gate")
    uv = _mm(h2, W["w_up_v"], "nn", bf16, "up_val")
    cw_g, cw_v = P["conv_w"][:, :D_FF], P["conv_w"][:, D_FF:]
    cb_g, cb_v = P["conv_b"][:, :D_FF], P["conv_b"][:, D_FF:]
    act = _conv_fwd(ug, uv, cw_g, cw_v, cb_g, cb_v)
    fo = _mm(act, W["w_down"], "nn", f32, "down_proj")
    loss, dy, dfo, g_post_ffn = _final(x1, fo, tgt, P["post_ffn_norm"])
    dact = _mm(dfo, W["w_down"], "nt", bf16, "d_act")
    gW_down = _mm(act, dfo, "tn", f32, "gw_down")
    dug, duv, st_g, st_v = _conv_bwd(ug, uv, dact, cw_g, cw_v, cb_g, cb_v)
    dh2 = _mm(dug, W["w_up_g"], "nt", f32, "dh2_gate")
    dh2 = _mm(duv, W["w_up_v"], "nt", f32, "dh2_val", acc=dh2)
    gW_up_g = _mm(h2, dug, "tn", f32, "gw_up_gate")
    gW_up_v = _mm(h2, duv, "tn", f32, "gw_up_val")
    dx1, dmo, g_pre_ffn, g_post_mix = _mid_bwd(dy, dh2, x1, mo, P["pre_ffn_norm"], P["post_mix_norm"])
    dmerged = _mm(dmo, W["w_out"], "nt", bf16, "d_merged")
    gW_out = _mm(merged, dmo, "tn", f32, "gw_out")
    da, db, dgc = _gate_bwd(dmerged, a, b, gc)
    dyattn = _mm(da, W["w_ba"], "nt", f32, "d_yattn")
    gW_ba = _mm(y_attn_b, da, "tn", f32, "gw_ba")
    dyhgrn = _mm(db, W["w_bh"], "nt", f32, "d_yhgrn")
    gW_bh = _mm(y_hgrn, db, "tn", f32, "gw_bh")
    big_b = dict(w_ba=gW_ba, w_bh=gW_bh, w_out=gW_out, w_up=[gW_up_g, gW_up_v], w_down=gW_down)
    dq_h, df_h, dv_h, dog_h, glb8, gnw8, got_b = _hgrn_bwd(hg, o_raw, dyhgrn, ck, lb, P["hgrn_norm"],
                                                          plan.bwd_ride(big_b))
    dhg = [dq_h, df_h, dv_h, dog_h]
    g_lb_raw = _lb_bwd(P["hgrn_lb_raw"], glb8[0:1])
    gn = gnw8[0:1]
    g_hgrn_norm = (gn[:, 0:128] + gn[:, 128:256]) + (gn[:, 256:384] + gn[:, 384:512])
    dos = _attn_merge_bwd(dyattn, y_attn, w0, w1, w2)
    dqkvs, gW_qkv, g_rel = [], [], []
    for g, d in enumerate(DILATIONS):
        dq, dk, dv, dbias = _attn_bwd(qkv[g], biases[g], dos[g], dos[3 + g], lbuf[g], (S // d) // ATTN_BLOCK,
                                      f"attn_bwd{g}")
        dqkvs.append([dq, dk, dv])
        gW_qkv.append(_mm(hs[g], dqkvs[g], "tn", f32, f"gw_qkv{g}"))
        g_rel.append(_bias_grad(dbias.reshape(8, -1), consts[g][0], f"bias_grad{g}"))
    gW_hg = _mm(h1, dhg, "tn", f32, "gw_hg")
    gW_gate = _mm(h1, dgc, "tn", f32, "gw_gate")
    gW_in = gW_qkv + [gW_hg, gW_gate]
    token = plan.grads_a_start(gW_in)
    dh_parts = [_mm(dqkvs[g], W["w_qkv"][g], "nt", f32, f"dh1_qkv{g}", after=token) for g in range(N_GROUPS)]
    dh_main = _mm(dhg, W["w_hg"], "nt", f32, "dh1_hg", acc=dh_parts[0], after=token)
    dh_main = _mm(dgc, W["w_gate"], "nt", f32, "dh1_gate", acc=dh_main, after=token)
    grad_x, g_pre_mix = _first_bwd(x, dx1, _dh_sum(dh_main, dh_parts[1], dh_parts[2]), P["pre_mix_norm"])

    g_conv_w = jnp.concatenate([st_g[0:3], st_v[0:3]], axis=1)
    g_conv_b = jnp.concatenate([st_g[3:4], st_v[3:4]], axis=1)
    small = dict(pre_mix_norm=g_pre_mix, rel_bias=jnp.concatenate(g_rel, axis=1), hgrn_lb_raw=g_lb_raw,
                 hgrn_norm=g_hgrn_norm, post_mix_norm=g_post_mix, pre_ffn_norm=g_pre_ffn, conv_b=g_conv_b,
                 post_ffn_norm=g_post_ffn, conv_w=g_conv_w)
    return loss, grad_x, gW_in, big_b, got_b, small


def _weights_a(both):
    w_in = jnp.transpose(both, (2, 1, 0, 3)).reshape(D_MODEL, -1)
    return dict(
        w_qkv=[w_in[:, g * QKV_G:(g + 1) * QKV_G] for g in range(N_GROUPS)],
        w_hg=w_in[:, 3 * QKV_G:3 * QKV_G + 4 * HGRN_W],
        w_gate=w_in[:, 3 * QKV_G + 4 * HGRN_W:],
    )


def _weights_b(slabs):
    sh = _unpack_rows(slabs, _PACK_B)
    w_up = _cols_to_full(sh["w_up"])
    return dict(
        w_ba=_cols_to_full(sh["w_ba"]),
        w_bh=_cols_to_full(sh["w_bh"]),
        w_out=sh["w_out"].reshape(D_MODEL, D_MODEL),
        w_up_g=w_up[:, :D_FF],
        w_up_v=w_up[:, D_FF:],
        w_down=sh["w_down"].reshape(D_FF, D_MODEL),
    )


def _dest_cols(sections, width):
    out = []
    for j in range(8):
        lo, hi, off, pieces = j * width, (j + 1) * width, 0, []
        for s in sections:
            a, b = max(lo, off), min(hi, off + s.shape[1])
            if a < b:
                pieces.append(s[:, a - off:b - off])
            off += s.shape[1]
        out.append(pieces[0] if len(pieces) == 1 else jnp.concatenate(pieces, axis=1))
    return out


def _grad_blocks_a(sections):
    cols = _dest_cols(sections, 1088)
    return jnp.stack([jnp.stack([cols[2 * k + c].astype(bf16) for k in range(4)]) for c in range(2)])


def _grad_slab_b(g):
    shards = dict(w_ba=_full_to_cols(g["w_ba"]), w_bh=_full_to_cols(g["w_bh"]), w_out=g["w_out"].reshape(8, 128, D_MODEL),
                  w_up=jnp.stack(_dest_cols(g["w_up"], 704)), w_down=g["w_down"].reshape(8, 352, D_MODEL))
    return _pack_rows({k: v.astype(bf16) for k, v in shards.items()}, _PACK_B)


_CONVW_SLAB_ROWS = 16


class _Traffic:
    def __init__(self, slab_a, slab_b, conv_w):
        hi = conv_w.astype(bf16)
        r1 = conv_w - hi.astype(f32)
        mid = r1.astype(bf16)
        lo = (r1 - mid.astype(f32)).astype(bf16)
        bits = jnp.stack([hi, mid, lo]).reshape(-1)
        tail = jnp.pad(bits, (0, _CONVW_SLAB_ROWS * D_MODEL - bits.shape[0])).reshape(_CONVW_SLAB_ROWS, D_MODEL)
        self.slab_b = jnp.concatenate([slab_b, tail], axis=0)
        self.state_a, tok = _chip_comm_start(slab_a, True, "ag_a_start")
        self.state_b, self.token = _chip_comm_start(self.slab_b, True, "ag_b_start", after=tok)
        self.chip_sum = None
        self.state_ga = None

    def start_token(self):
        return self.token

    def weights_a(self, after):
        return _weights_a(_core_gather(_chip_comm_wait(self.state_a, after, True, "ag_a_wait"), "ag_a_cores"))

    def weights_b(self, after):
        both = _core_gather(_chip_comm_wait(self.state_b, after, True, "ag_b_wait"), "ag_b_cores")
        slabs = jnp.swapaxes(both, 0, 1).reshape((8,) + tuple(self.slab_b.shape))
        rows = _slab_rows(_PACK_B)
        out = _weights_b(slabs[:, :rows])
        pieces = slabs[:, rows:].reshape(8, -1)[:, :3 * 3 * 704].reshape(8, 3, 3, 704).astype(f32)
        out["conv_w"] = _cols_to_full((pieces[:, 0] + pieces[:, 1]) + pieces[:, 2])
        return out

    def bwd_ride(self, grads):
        self.chip_sum = _pair_sum(_by_core(_grad_slab_b(grads)), "rs_b")
        return (self.chip_sum, False)

    def grads_a_start(self, sections):
        self.state_ga, token = _chip_comm_start(_pair_sum(_grad_blocks_a(sections), "rs_a"), False, "rs_a_start")
        return token

    def parts(self, got_b, after):
        parts = _unpack_rows(_fill_own(got_b, self.chip_sum, False), _PACK_B)
        parts["w_in"] = _chip_comm_wait(self.state_ga, after, False, "rs_a_wait")
        return parts


def kernel(x, pre_mix_norm, w_in, rel_bias, hgrn_lb_raw, hgrn_norm, w_branch_attn, w_branch_hgrn, w_out, post_mix_norm, pre_ffn_norm, w_up, conv_w, conv_b, w_down, post_ffn_norm, loss_target, m_pre_mix_norm, m_w_in, m_rel_bias, m_hgrn_lb_raw, m_hgrn_norm, m_w_branch_attn, m_w_branch_hgrn, m_w_out, m_post_mix_norm, m_pre_ffn_norm, m_w_up, m_conv_w, m_conv_b, m_w_down, m_post_ffn_norm, v_pre_mix_norm, v_w_in, v_rel_bias, v_hgrn_lb_raw, v_hgrn_norm, v_w_branch_attn, v_w_branch_hgrn, v_w_out, v_post_mix_norm, v_pre_ffn_norm, v_w_up, v_conv_w, v_conv_b, v_w_down, v_post_ffn_norm):
    ci = lax.axis_index("c")
    dev = 4 * lax.axis_index("x") + 2 * lax.axis_index("y") + ci
    wts = dict(w_in=w_in[0], w_ba=w_branch_attn[0], w_bh=w_branch_hgrn[0], w_out=w_out[0], w_up=w_up[0],
               w_down=w_down[0])
    mom = dict(w_in=m_w_in[0], w_ba=m_w_branch_attn[0], w_bh=m_w_branch_hgrn[0], w_out=m_w_out[0], w_up=m_w_up[0],
               w_down=m_w_down[0])
    var = dict(w_in=v_w_in[0], w_ba=v_w_branch_attn[0], w_bh=v_w_branch_hgrn[0], w_out=v_w_out[0], w_up=v_w_up[0],
               w_down=v_w_down[0])
    small_w = dict(pre_mix_norm=pre_mix_norm, rel_bias=rel_bias, hgrn_lb_raw=hgrn_lb_raw, hgrn_norm=hgrn_norm,
                   post_mix_norm=post_mix_norm, pre_ffn_norm=pre_ffn_norm, conv_b=conv_b, post_ffn_norm=post_ffn_norm)
    small_m = dict(pre_mix_norm=m_pre_mix_norm, rel_bias=m_rel_bias, hgrn_lb_raw=m_hgrn_lb_raw, hgrn_norm=m_hgrn_norm,
                   post_mix_norm=m_post_mix_norm, pre_ffn_norm=m_pre_ffn_norm, conv_b=m_conv_b,
                   post_ffn_norm=m_post_ffn_norm)
    small_v = dict(pre_mix_norm=v_pre_mix_norm, rel_bias=v_rel_bias, hgrn_lb_raw=v_hgrn_lb_raw, hgrn_norm=v_hgrn_norm,
                   post_mix_norm=v_post_mix_norm, pre_ffn_norm=v_pre_ffn_norm, conv_b=v_conv_b,
                   post_ffn_norm=v_post_ffn_norm)

    plan = _Traffic(wts["w_in"].astype(bf16),
                    _pack_rows({k: wts[k].astype(bf16)[None] for k, _ in _PACK_B}, _PACK_B)[0], conv_w[0])

    loss8, grad_x, _, _, got_b, small = _local_step(x[0], loss_target[0], small_w, plan)
    parts = plan.parts(got_b, grad_x)
    outs_big = {}
    for k, _ in _PACK_SIZES:
        outs_big[k] = _adamw(wts[k], mom[k], var[k], parts[k], "adamw_" + k)

    spack = jnp.concatenate([_pack_small(small, loss8[0, 0:1]),
                             jnp.pad(small["conv_w"].reshape(-1, LANE), ((0, _CONVW_ROWS - 132), (0, 0)))], axis=0)
    allp = _core_gather(_chip_comm(spack, True, "ag_small_chips"), "ag_small_cores")
    ssum = _sum8(allp, "small_sum")
    gs = ssum[:_SMALL_ROWS]
    loss = ssum[_SMALL_USED // LANE, _SMALL_USED % LANE]
    res_small = _adamw(_pack_small(small_w), _pack_small(small_m), _pack_small(small_v), gs, "adamw_small")
    sm = [_unpack_small(t) for t in res_small]
    g_cw_full = ssum[_SMALL_ROWS:_SMALL_ROWS + 132].reshape(3, 2 * D_FF)
    g_cw = lax.dynamic_slice_in_dim(g_cw_full, dev * 704, 704, axis=1)
    res_cw = _adamw(conv_w[0], m_conv_w[0], v_conv_w[0], g_cw, "adamw_conv_w")

    def pick(i):
        def big_(k):
            return outs_big[k][i][None]
        return [sm[i]["pre_mix_norm"], big_("w_in"), sm[i]["rel_bias"], sm[i]["hgrn_lb_raw"], sm[i]["hgrn_norm"],
                big_("w_ba"), big_("w_bh"), big_("w_out"), sm[i]["post_mix_norm"], sm[i]["pre_ffn_norm"],
                big_("w_up"), res_cw[i][None], sm[i]["conv_b"], big_("w_down"), sm[i]["post_ffn_norm"]]

    return (loss, grad_x[None], *pick(0), *pick(1), *pick(2), *pick(3))
```

```python
import functools
import math

import jax
import jax.numpy as jnp
from jax import lax
from jax.experimental import pallas as pl
from jax.experimental.pallas import tpu as pltpu

f32 = jnp.float32
bf16 = jnp.bfloat16
SDS = jax.ShapeDtypeStruct
HIGHEST = lax.Precision.HIGHEST
MESH = pl.DeviceIdType.MESH

NN = (((1,), (0,)), ((), ()))
NT = (((1,), (1,)), ((), ()))
TN = (((0,), (0,)), ((), ()))

D_MODEL = 1024
N_GROUPS = 3
DILATIONS = (1, 4, 16)
HEAD_DIM = 64
ATTN_BLOCK = 128
QKV_G = 1536
ATTN_OUT = 512
HGRN_W = 512
HGRN_CHUNK = 32
D_FF = 2816
NUM_BUCKETS = 32
MAX_EXACT = 16
MAX_DISTANCE = 2048
NEG_INF = -1e30
EPS = 1e-6
LANE = 128
SUBLANE = 8
VMEM_BIG = 48 * 1024 * 1024
MM_ROWS = 512
MM_OUT_BYTES = 8 * 1024 * 1024

ADAM_LR, ADAM_B1, ADAM_B2, ADAM_EPS, ADAM_WD, ADAM_STEP = 0.001, 0.9, 0.999, 1e-08, 0.01, 10


def _pick(n, pref):
    t = pref
    while t >= LANE:
        if n % t == 0:
            return t
        t //= 2
    return n


def _cparams(sem=None, vmem=None):
    kw = {}
    if sem is not None:
        kw["dimension_semantics"] = sem
    if vmem is not None:
        kw["vmem_limit_bytes"] = vmem
    return pltpu.CompilerParams(**kw)


def _sigmoid(x):
    return jax.nn.sigmoid(x)


def _colsum8(x):
    return x.reshape(x.shape[0] // SUBLANE, SUBLANE, x.shape[1]).sum(axis=0)


def _mm(a, b, mode, out_dtype, name, acc=None, after=None):
    dims = {"nn": NN, "nt": NT, "tn": TN}[mode]
    has_acc = acc is not None
    if mode == "tn":
        assert not has_acc
        bs = list(b) if isinstance(b, (list, tuple)) else [b]
        K, M = a.shape
        widths = [t.shape[1] for t in bs]
        N = sum(widths)
        tmm = M if M * N * 4 <= MM_OUT_BYTES else M // 2
        ts = _pick(K, 2 * MM_ROWS)
        nk = K // ts

        def body_tn(a_ref, *refs):
            o_ref = refs[-1]
            k = pl.program_id(1)
            av = a_ref[...]
            lo = 0
            for b_ref, w in zip(refs[:-1], widths):
                part = lax.dot_general(av, b_ref[...], dims, preferred_element_type=f32)
                cols = slice(lo, lo + w)
                lo += w

                @pl.when(k == 0)
                def _(part=part, cols=cols):
                    o_ref[:, cols] = part

                @pl.when(k > 0)
                def _(part=part, cols=cols):
                    o_ref[:, cols] += part

        return pl.pallas_call(
            body_tn,
            grid=(M // tmm, nk),
            in_specs=[pl.BlockSpec((ts, tmm), lambda i, k: (k, i))]
            + [pl.BlockSpec((ts, w), lambda i, k: (k, 0)) for w in widths],
            out_specs=pl.BlockSpec((tmm, N), lambda i, k: (i, 0)),
            out_shape=SDS((M, N), out_dtype),
            compiler_params=_cparams(("parallel", "arbitrary"), VMEM_BIG),
            name=name,
        )(a, *bs)

    parts = list(a) if isinstance(a, (list, tuple)) else [a]
    assert mode == "nt" or len(parts) == 1
    widths = [t.shape[1] for t in parts]
    M = parts[0].shape[0]
    N = b.shape[1] if mode == "nn" else b.shape[0]
    tm = _pick(M, MM_ROWS)
    npart = len(parts)

    def body(*refs):
        a_refs, b_ref = refs[:npart], refs[npart]
        c_ref = refs[npart + 1] if has_acc else None
        o_ref = refs[-1]
        if npart == 1:
            part = lax.dot_general(a_refs[0][...], b_ref[...], dims, preferred_element_type=f32)
        else:
            part, lo = None, 0
            for a_ref, w in zip(a_refs, widths):
                t = lax.dot_general(a_ref[...], b_ref[:, lo:lo + w], dims, preferred_element_type=f32)
                part = t if part is None else part + t
                lo += w
        if has_acc:
            part = part + c_ref[...]
        o_ref[...] = part.astype(out_dtype)

    specs = [pl.BlockSpec((tm, w), lambda i: (i, 0)) for w in widths] + [pl.BlockSpec(b.shape, lambda i: (0, 0))]
    args = parts + [b]
    aliases = {}
    if has_acc:
        specs.append(pl.BlockSpec((tm, N), lambda i: (i, 0)))
        args.append(acc)
        aliases = {npart + 1: 0}
    if after is not None:
        specs.append(pl.BlockSpec(memory_space=pl.ANY))
        args.append(after)
    return pl.pallas_call(
        body,
        grid=(M // tm,),
        in_specs=specs,
        out_specs=pl.BlockSpec((tm, N), lambda i: (i, 0)),
        out_shape=SDS((M, N), out_dtype),
        input_output_aliases=aliases,
        compiler_params=_cparams(("parallel",), VMEM_BIG),
        name=name,
    )(*args)


PERM_ROWS = 1024


def _perm_spec(d, cols=LANE):
    return pl.BlockSpec((d, PERM_ROWS // d, cols), lambda i, j: (0, i, j))


def _to_natural(src_ref, dst_ref, d):
    n = src_ref.shape[1]
    for r in range(d):
        dst_ref[pl.ds(r, n, stride=d), :] = src_ref[r]


def _prep(x, w, after=None):
    S, D = x.shape
    R = PERM_ROWS
    nc = D // LANE
    n_in = nc + 1 + (after is not None)

    def body(*refs):
        x_refs, w_ref = refs[:nc], refs[nc]
        h_ref, h4_ref, h16_ref, rs = refs[n_in:]
        ssq = None
        for xr in x_refs:
            v = xr[...]
            t = jnp.sum(v * v, axis=-1, keepdims=True)
            ssq = t if ssq is None else ssq + t
        rinv = lax.rsqrt(ssq * (1.0 / D) + EPS)
        rs[...] = jnp.broadcast_to(rinv, (R, LANE))
        for j, xr in enumerate(x_refs):
            cols = slice(j * LANE, (j + 1) * LANE)
            wj = w_ref[:, cols]
            h_ref[:, cols] = ((xr[...] * rinv) * wj).astype(bf16)
            for d, o_ref in ((4, h4_ref), (16, h16_ref)):
                n = R // d
                for r in range(d):
                    rows = pl.ds(r, n, stride=d)
                    o_ref[r, :, cols] = ((xr[rows, :] * rs[rows, :]) * wj).astype(bf16)

    col = lambda j: pl.BlockSpec((R, LANE), lambda i, j=j: (i, j))
    h, h4, h16 = pl.pallas_call(
        body,
        grid=(S // R,),
        in_specs=[col(j) for j in range(nc)] + [pl.BlockSpec((1, D), lambda i: (0, 0))]
        + ([] if after is None else [pl.BlockSpec(memory_space=pl.ANY)]),
        out_specs=[pl.BlockSpec((R, D), lambda i: (i, 0)), pl.BlockSpec((4, R // 4, D), lambda i: (0, i, 0)),
                   pl.BlockSpec((16, R // 16, D), lambda i: (0, i, 0))],
        out_shape=[SDS((S, D), bf16), SDS((4, S // 4, D), bf16), SDS((16, S // 16, D), bf16)],
        scratch_shapes=[pltpu.VMEM((R, LANE), f32)],
        compiler_params=_cparams(("parallel",), VMEM_BIG),
        name="prep_norm_perm",
    )(*([x] * nc), w, *([] if after is None else [after]))
    return [h, h4.reshape(S, D), h16.reshape(S, D)]


def _dh_sum(a, b, c):
    S, D = a.shape
    R = PERM_ROWS

    def body(a_ref, b_ref, c_ref, o_ref, sb, sc):
        _to_natural(b_ref, sb, 4)
        _to_natural(c_ref, sc, 16)
        o_ref[...] = (a_ref[...] + sb[...]) + sc[...]

    nat = pl.BlockSpec((R, LANE), lambda i, j: (i, j))
    return pl.pallas_call(
        body,
        grid=(S // R, D // LANE),
        in_specs=[nat, _perm_spec(4), _perm_spec(16)],
        out_specs=nat,
        out_shape=SDS((S, D), f32),
        scratch_shapes=[pltpu.VMEM((R, LANE), f32)] * 2,
        compiler_params=_cparams(("parallel", "parallel")),
        name="dh_sum",
    )(a, b.reshape(4, S // 4, D), c.reshape(16, S // 16, D))


def _rms_parts(xv):
    r = lax.rsqrt(jnp.mean(xv * xv, axis=-1, keepdims=True) + EPS)
    return r, xv * r


def _rms_bwd(xhat, r, w, dy):
    dyw = dy * w
    return r * (dyw - xhat * jnp.mean(dyw * xhat, axis=-1, keepdims=True))


def _mid_fwd(x, mo, w_pm, w_pf):
    S, D = x.shape
    tm = _pick(S, 512)

    def body(x_ref, mo_ref, wpm_ref, wpf_ref, x1_ref, h2_ref):
        _, moh = _rms_parts(mo_ref[...])
        x1 = x_ref[...] + moh * wpm_ref[...]
        x1_ref[...] = x1
        _, x1h = _rms_parts(x1)
        h2_ref[...] = (x1h * wpf_ref[...]).astype(bf16)

    row = pl.BlockSpec((tm, D), lambda i: (i, 0))
    vec = pl.BlockSpec((1, D), lambda i: (0, 0))
    return pl.pallas_call(
        body,
        grid=(S // tm,),
        in_specs=[row, row, vec, vec],
        out_specs=[row, row],
        out_shape=[SDS((S, D), f32), SDS((S, D), bf16)],
        compiler_params=_cparams(("parallel",)),
        name="mid_fwd",
    )(x, mo, w_pm, w_pf)


def _final(x1, fo, tgt, w_pfn):
    S, D = x1.shape
    tm = _pick(S, 512)
    nt = S // tm

    def body(x1_ref, fo_ref, t_ref, w_ref, loss_ref, dy_ref, dfo_ref, gw_ref, lacc, gacc):
        i = pl.program_id(0)

        @pl.when(i == 0)
        def _():
            lacc[...] = jnp.zeros_like(lacc)
            gacc[...] = jnp.zeros_like(gacc)

        w = w_ref[...]
        r, foh = _rms_parts(fo_ref[...])
        y = x1_ref[...] + foh * w
        err = y - t_ref[...]
        lacc[...] += _colsum8(err * err)
        dy = err * (1.0 / D)
        dy_ref[...] = dy
        gacc[...] += _colsum8(dy * foh)
        dfo_ref[...] = _rms_bwd(foh, r, w, dy).astype(bf16)

        @pl.when(i == nt - 1)
        def _():
            loss_ref[...] = jnp.full((SUBLANE, LANE), 0.5 / D, f32) * jnp.sum(lacc[...])
            gw_ref[...] = jnp.sum(gacc[...], axis=0, keepdims=True)

    row = pl.BlockSpec((tm, D), lambda i: (i, 0))
    vec = pl.BlockSpec((1, D), lambda i: (0, 0))
    return pl.pallas_call(
        body,
        grid=(nt,),
        in_specs=[row, row, row, vec],
        out_specs=[pl.BlockSpec((SUBLANE, LANE), lambda i: (0, 0)), row, row, vec],
        out_shape=[SDS((SUBLANE, LANE), f32), SDS((S, D), f32), SDS((S, D), bf16), SDS((1, D), f32)],
        scratch_shapes=[pltpu.VMEM((SUBLANE, D), f32), pltpu.VMEM((SUBLANE, D), f32)],
        compiler_params=_cparams(("arbitrary",)),
        name="final_loss",
    )(x1, fo, tgt, w_pfn)


def _mid_bwd(dy, dh2, x1, mo, w_pf, w_pm):
    S, D = dy.shape
    tm = _pick(S, 512)
    nt = S // tm

    def body(dy_ref, dh2_ref, x1_ref, mo_ref, wpf_ref, wpm_ref, dx1_ref, dmo_ref, gpf_ref, gpm_ref, apf, apm):
        i = pl.program_id(0)

        @pl.when(i == 0)
        def _():
            apf[...] = jnp.zeros_like(apf)
            apm[...] = jnp.zeros_like(apm)

        r1, x1h = _rms_parts(x1_ref[...])
        dh2 = dh2_ref[...]
        apf[...] += _colsum8(dh2 * x1h)
        dx1 = dy_ref[...] + _rms_bwd(x1h, r1, wpf_ref[...], dh2)
        dx1_ref[...] = dx1
        rm, moh = _rms_parts(mo_ref[...])
        apm[...] += _colsum8(dx1 * moh)
        dmo_ref[...] = _rms_bwd(moh, rm, wpm_ref[...], dx1).astype(bf16)

        @pl.when(i == nt - 1)
        def _():
            gpf_ref[...] = jnp.sum(apf[...], axis=0, keepdims=True)
            gpm_ref[...] = jnp.sum(apm[...], axis=0, keepdims=True)

    row = pl.BlockSpec((tm, D), lambda i: (i, 0))
    vec = pl.BlockSpec((1, D), lambda i: (0, 0))
    return pl.pallas_call(
        body,
        grid=(nt,),
        in_specs=[row, row, row, row, vec, vec],
        out_specs=[row, row, vec, vec],
        out_shape=[SDS((S, D), f32), SDS((S, D), bf16), SDS((1, D), f32), SDS((1, D), f32)],
        scratch_shapes=[pltpu.VMEM((SUBLANE, D), f32), pltpu.VMEM((SUBLANE, D), f32)],
        compiler_params=_cparams(("arbitrary",)),
        name="mid_bwd",
    )(dy, dh2, x1, mo, w_pf, w_pm)


def _first_bwd(x, dx1, dh, w_pre):
    S, D = x.shape
    tm = _pick(S, 512)
    nt = S // tm

    def body(x_ref, dx1_ref, a_ref, w_ref, gx_ref, gw_ref, acc):
        i = pl.program_id(0)

        @pl.when(i == 0)
        def _():
            acc[...] = jnp.zeros_like(acc)

        r, xh = _rms_parts(x_ref[...])
        dh = a_ref[...]
        acc[...] += _colsum8(dh * xh)
        gx_ref[...] = dx1_ref[...] + _rms_bwd(xh, r, w_ref[...], dh)

        @pl.when(i == nt - 1)
        def _():
            gw_ref[...] = jnp.sum(acc[...], axis=0, keepdims=True)

    row = pl.BlockSpec((tm, D), lambda i: (i, 0))
    vec = pl.BlockSpec((1, D), lambda i: (0, 0))
    return pl.pallas_call(
        body,
        grid=(nt,),
        in_specs=[row, row, row, vec],
        out_specs=[row, vec],
        out_shape=[SDS((S, D), f32), SDS((1, D), f32)],
        scratch_shapes=[pltpu.VMEM((SUBLANE, D), f32)],
        compiler_params=_cparams(("arbitrary",)),
        name="first_bwd",
    )(x, dx1, dh, w_pre)


def _t5_bucket(dist):
    n = jnp.maximum(dist, 0)
    nf = jnp.maximum(n, 1).astype(f32)
    large = MAX_EXACT + (jnp.log(nf / MAX_EXACT) / math.log(MAX_DISTANCE / MAX_EXACT)
                         * (NUM_BUCKETS - MAX_EXACT)).astype(jnp.int32)
    large = jnp.minimum(large, NUM_BUCKETS - 1)
    return jnp.where(n < MAX_EXACT, n, large)


def _bias_consts(d):
    blk = ATTN_BLOCK
    rel = jnp.arange(blk)[:, None] + blk - jnp.arange(2 * blk)[None, :]
    in_win = (rel >= 0) & (rel <= blk)
    bucket = _t5_bucket(rel * d).reshape(1, -1)
    onehot = (bucket == jnp.arange(NUM_BUCKETS)[:, None]).astype(f32)
    return onehot, in_win.astype(f32).reshape(1, -1)


def _bias_build(tab_t, onehot, maskf, name):
    H = tab_t.shape[0]

    def body(t_ref, oh_ref, m_ref, o_ref):
        b = jnp.dot(t_ref[...], oh_ref[...], precision=HIGHEST, preferred_element_type=f32)
        o_ref[...] = jnp.where(m_ref[...] > 0.5, b, NEG_INF)

    return pl.pallas_call(body, out_shape=SDS((H, onehot.shape[1]), f32), name=name)(tab_t, onehot, maskf)


def _bias_grad(dbias_flat, onehot, name):
    H = dbias_flat.shape[0]

    def body(g_ref, oh_ref, o_ref):
        o_ref[...] = lax.dot_general(oh_ref[...], g_ref[...], NT, precision=HIGHEST, preferred_element_type=f32)

    return pl.pallas_call(body, out_shape=SDS((NUM_BUCKETS, H), f32), name=name)(dbias_flat, onehot)


ATTN_TILE = 512
ATTN_SUB = ATTN_TILE // ATTN_BLOCK


def _qkv_specs(nt):
    tile = (ATTN_TILE, LANE)
    blk = (ATTN_BLOCK, LANE)
    cur = lambda off: (lambda h, t: (jnp.minimum(t, nt - 1), off + h))
    prev = lambda off: (lambda h, t: (jnp.maximum(jnp.minimum(t, nt - 1) * ATTN_SUB - 1, 0), off + h))
    return [pl.BlockSpec(tile, cur(0)), pl.BlockSpec(blk, prev(4)), pl.BlockSpec(tile, cur(4)),
            pl.BlockSpec(blk, prev(8)), pl.BlockSpec(tile, cur(8))]


def _head_masks():
    lane = lax.broadcasted_iota(jnp.int32, (ATTN_BLOCK, LANE), 1)
    return lane < HEAD_DIM


def _attn_fwd(qkv, bias, bps, name, after=None):
    S = qkv.shape[0]
    nt = S // ATTN_TILE
    scale = HEAD_DIM ** -0.5

    def body(q_ref, kp_ref, kc_ref, vp_ref, vc_ref, b_ref, *rest):
        o_ref, l_ref = rest[-2:]
        t = pl.program_id(1)
        kk = jnp.concatenate([kp_ref[...], kc_ref[...]], axis=0)
        vv = jnp.concatenate([vp_ref[...], vc_ref[...]], axis=0)
        low = _head_masks()
        col = lax.broadcasted_iota(jnp.int32, (ATTN_BLOCK, 2 * ATTN_BLOCK), 1)
        for b in range(ATTN_SUB):
            lo = b * ATTN_BLOCK
            rows = slice(lo, lo + ATTN_BLOCK)
            keys = slice(lo, lo + 2 * ATTN_BLOCK)
            dead = jnp.logical_and((t * ATTN_SUB + b) % bps == 0, col < ATTN_BLOCK)
            q2 = q_ref[rows, :]
            kb, vb = kk[keys], vv[keys]
            outs, lses = [], []
            for h in range(2):
                hm = low if h == 0 else jnp.logical_not(low)
                qh = jnp.where(hm, q2, jnp.zeros_like(q2))
                s = lax.dot_general(qh, kb, NT, preferred_element_type=f32) * scale + b_ref[h]
                s = jnp.where(dead, NEG_INF, s)
                m = jnp.max(s, axis=-1, keepdims=True)
                p = jnp.exp(s - m)
                l = jnp.sum(p, axis=-1, keepdims=True)
                outs.append(jnp.dot(p.astype(bf16), vb, preferred_element_type=f32) / l)
                lses.append(m + jnp.log(l))
            o_ref[rows, :] = jnp.where(low, outs[0], outs[1])
            l_ref[rows, :] = jnp.where(low, lses[0], lses[1])

    tile = pl.BlockSpec((ATTN_TILE, LANE), lambda h, t: (t, h))
    return pl.pallas_call(
        body,
        grid=(4, nt),
        in_specs=_qkv_specs(nt) + [pl.BlockSpec((2, ATTN_BLOCK, 2 * ATTN_BLOCK), lambda h, t: (h, 0, 0))]
        + ([] if after is None else [pl.BlockSpec(memory_space=pl.ANY)]),
        out_specs=[tile, tile],
        out_shape=[SDS((S, ATTN_OUT), f32), SDS((S, ATTN_OUT), f32)],
        compiler_params=_cparams(("parallel", "parallel")),
        name=name,
    )(qkv, qkv, qkv, qkv, qkv, bias, *([] if after is None else [after]))


def _attn_bwd(qkv, bias, do, dvec, lse, bps, name):
    S = qkv.shape[0]
    nt = S // ATTN_TILE
    scale = HEAD_DIM ** -0.5

    def assemble(parts):
        rows = [parts[0][:ATTN_BLOCK]]
        for b in range(ATTN_SUB - 1):
            rows.append(parts[b][ATTN_BLOCK:] + parts[b + 1][:ATTN_BLOCK])
        rows.append(parts[-1][ATTN_BLOCK:])
        return rows

    def body(q_ref, kp_ref, kc_ref, vp_ref, vc_ref, b_ref, do_ref, dvec_ref, lse_ref,
             dq_ref, dk_ref, dv_ref, db_ref, ck, cv):
        t = pl.program_id(1)
        last = ATTN_TILE - ATTN_BLOCK

        @pl.when(t == 0)
        def _():
            ck[...] = jnp.zeros_like(ck)
            cv[...] = jnp.zeros_like(cv)
            db_ref[...] = jnp.zeros_like(db_ref)

        @pl.when(t < nt)
        def _():
            kk = jnp.concatenate([kp_ref[...], kc_ref[...]], axis=0)
            vv = jnp.concatenate([vp_ref[...], vc_ref[...]], axis=0)
            low = _head_masks()
            col = lax.broadcasted_iota(jnp.int32, (ATTN_BLOCK, 2 * ATTN_BLOCK), 1)
            low2 = lax.broadcasted_iota(jnp.int32, (2 * ATTN_BLOCK, LANE), 1) < HEAD_DIM
            dk_parts, dv_parts = [], []
            dsum = [None, None]
            for b in range(ATTN_SUB):
                lo = b * ATTN_BLOCK
                rows = slice(lo, lo + ATTN_BLOCK)
                keys = slice(lo, lo + 2 * ATTN_BLOCK)
                dead = jnp.logical_and((t * ATTN_SUB + b) % bps == 0, col < ATTN_BLOCK)
                q2 = q_ref[rows, :]
                kb, vb = kk[keys], vv[keys]
                do2 = do_ref[rows, :].astype(bf16)
                dvec2 = dvec_ref[rows, :]
                lse2 = lse_ref[rows, :]
                dqs, dks, dvs = [], [], []
                for h in range(2):
                    hm = low if h == 0 else jnp.logical_not(low)
                    c0 = h * HEAD_DIM
                    qh = jnp.where(hm, q2, jnp.zeros_like(q2))
                    doh = jnp.where(hm, do2, jnp.zeros_like(do2))
                    s = lax.dot_general(qh, kb, NT, preferred_element_type=f32) * scale + b_ref[h]
                    s = jnp.where(dead, NEG_INF, s)
                    p = jnp.exp(s - lse2[:, c0:c0 + 1])
                    dp = lax.dot_general(doh, vb, NT, preferred_element_type=f32)
                    ds = p * (dp - dvec2[:, c0:c0 + 1])
                    dsum[h] = ds if dsum[h] is None else dsum[h] + ds
                    dsb = ds.astype(bf16)
                    dqs.append(jnp.dot(dsb, kb, preferred_element_type=f32) * scale)
                    dks.append(lax.dot_general(dsb, q2, TN, preferred_element_type=f32) * scale)
                    dvs.append(lax.dot_general(p.astype(bf16), do2, TN, preferred_element_type=f32))
                dq_ref[rows, :] = jnp.where(low, dqs[0], dqs[1]).astype(bf16)
                dk_parts.append(jnp.where(low2, dks[0], dks[1]))
                dv_parts.append(jnp.where(low2, dvs[0], dvs[1]))
            db_ref[0] += dsum[0]
            db_ref[1] += dsum[1]
            for parts, carry, out_ref in ((dk_parts, ck, dk_ref), (dv_parts, cv, dv_ref)):
                rws = assemble(parts)
                out_ref[:last, :] = carry[:last, :].astype(bf16)
                out_ref[last:, :] = (carry[last:, :] + rws[0]).astype(bf16)
                for b in range(ATTN_SUB):
                    carry[b * ATTN_BLOCK:(b + 1) * ATTN_BLOCK, :] = rws[b + 1]

        @pl.when(t == nt)
        def _():
            dk_ref[...] = ck[...].astype(bf16)
            dv_ref[...] = cv[...].astype(bf16)

    tile = (ATTN_TILE, LANE)
    cur = pl.BlockSpec(tile, lambda h, t: (jnp.minimum(t, nt - 1), h))
    lag = pl.BlockSpec(tile, lambda h, t: (jnp.maximum(t - 1, 0), h))
    bspec = pl.BlockSpec((2, ATTN_BLOCK, 2 * ATTN_BLOCK), lambda h, t: (h, 0, 0))
    return pl.pallas_call(
        body,
        grid=(4, nt + 1),
        in_specs=_qkv_specs(nt) + [bspec, cur, cur, cur],
        out_specs=[cur, lag, lag, bspec],
        out_shape=[SDS((S, ATTN_OUT), bf16), SDS((S, ATTN_OUT), bf16), SDS((S, ATTN_OUT), bf16),
                   SDS((8, ATTN_BLOCK, 2 * ATTN_BLOCK), f32)],
        scratch_shapes=[pltpu.VMEM(tile, f32), pltpu.VMEM(tile, f32)],
        compiler_params=_cparams(("parallel", "arbitrary")),
        name=name,
    )(qkv, qkv, qkv, qkv, qkv, bias, do, dvec, lse)


def _attn_merge(o0, o1, o2, l0, l1, l2):
    S, W = o0.shape
    R = PERM_ROWS

    def body(o0_ref, o1_ref, o2_ref, l0_ref, l1_ref, l2_ref, y_ref, yb_ref, w0_ref, w1_ref, w2_ref,
             so1, so2, sl1, sl2):
        _to_natural(o1_ref, so1, 4)
        _to_natural(l1_ref, sl1, 4)
        _to_natural(o2_ref, so2, 16)
        _to_natural(l2_ref, sl2, 16)
        a, b, c = l0_ref[...], sl1[...], sl2[...]
        m = jnp.maximum(jnp.maximum(a, b), c)
        ea, eb, ec = jnp.exp(a - m), jnp.exp(b - m), jnp.exp(c - m)
        den = (ea + eb) + ec
        w0, w1, w2 = ea / den, eb / den, ec / den
        y = (w0 * o0_ref[...] + w1 * so1[...]) + w2 * so2[...]
        y_ref[...] = y
        yb_ref[...] = y.astype(bf16)
        w0_ref[...] = w0
        w1_ref[...] = w1
        w2_ref[...] = w2

    nat = pl.BlockSpec((R, LANE), lambda i, j: (i, j))
    v4 = lambda t: t.reshape(4, S // 4, W)
    v16 = lambda t: t.reshape(16, S // 16, W)
    return pl.pallas_call(
        body,
        grid=(S // R, W // LANE),
        in_specs=[nat, _perm_spec(4), _perm_spec(16)] * 2,
        out_specs=[nat] * 5,
        out_shape=[SDS((S, W), f32), SDS((S, W), bf16)] + [SDS((S, W), f32)] * 3,
        scratch_shapes=[pltpu.VMEM((R, LANE), f32)] * 4,
        compiler_params=_cparams(("parallel", "parallel")),
        name="attn_merge",
    )(o0, v4(o1), v16(o2), l0, v4(l1), v16(l2))


def _attn_merge_bwd(dy, y, w0, w1, w2):
    S, W = dy.shape
    R = PERM_ROWS

    def body(dy_ref, y_ref, w0_ref, w1_ref, w2_ref, a0, a1, a2, b0, b1, b2, sa, sb):
        dyv = dy_ref[...]
        r = lax.broadcasted_iota(jnp.int32, (LANE, LANE), 0) // HEAD_DIM
        c = lax.broadcasted_iota(jnp.int32, (LANE, LANE), 1) // HEAD_DIM
        seg = jnp.where(r == c, 1.0, 0.0).astype(f32)
        cbar = jnp.dot(dyv * y_ref[...], seg, precision=HIGHEST, preferred_element_type=f32)
        w = w0_ref[...]
        a0[...] = (w * dyv).astype(bf16)
        b0[...] = w * cbar
        for d, w_ref, a_ref, b_ref in ((4, w1_ref, a1, b1), (16, w2_ref, a2, b2)):
            w = w_ref[...]
            sa[...] = w * dyv
            sb[...] = w * cbar
            n = R // d
            for k in range(d):
                rows = pl.ds(k, n, stride=d)
                a_ref[k] = sa[rows, :].astype(bf16)
                b_ref[k] = sb[rows, :]

    nat = pl.BlockSpec((R, LANE), lambda i, j: (i, j))
    shapes = lambda dt: [SDS((S, W), dt), SDS((4, S // 4, W), dt), SDS((16, S // 16, W), dt)]
    outs = pl.pallas_call(
        body,
        grid=(S // R, W // LANE),
        in_specs=[nat] * 5,
        out_specs=[nat, _perm_spec(4), _perm_spec(16)] * 2,
        out_shape=shapes(bf16) + shapes(f32),
        scratch_shapes=[pltpu.VMEM((R, LANE), f32)] * 2,
        compiler_params=_cparams(("parallel", "parallel")),
        name="attn_merge_bwd",
    )(dy, y, w0, w1, w2)
    return [t.reshape(S, W) for t in outs]


HGRN_SB = 256


def _chunk_masks():
    r = jnp.arange(HGRN_SB)[:, None]
    c = jnp.arange(HGRN_SB)[None, :]
    same = (r // HGRN_CHUNK) == (c // HGRN_CHUNK)
    return jnp.stack([same & (c <= r), same, same & (c >= r)]).astype(bf16)


def _mask_dot(mask, x):
    hi = x.astype(bf16)
    r1 = x - hi.astype(f32)
    mid = r1.astype(bf16)
    lo = (r1 - mid.astype(f32)).astype(bf16)
    p = jnp.dot(mask, jnp.concatenate([hi, mid, lo], axis=1), preferred_element_type=f32)
    n = x.shape[1]
    return (p[:, :n] + p[:, n:2 * n]) + p[:, 2 * n:]


def _hgrn_prep(q_raw, f_raw, lbv, tril, same):
    sq = _sigmoid(q_raw)
    qs = q_raw * sq
    sig = _sigmoid(f_raw)
    f = lbv + (1.0 - lbv) * sig
    g = jnp.log(f)
    k = 1.0 - f
    G = _mask_dot(tril, g)
    GL = _mask_dot(same, g)
    eG = jnp.exp(G)
    einv = jnp.exp(-G)
    edec = jnp.exp(GL - G)
    return dict(sq=sq, qs=qs, sig=sig, f=f, k=k, eG=eG, einv=einv, edec=edec, eGL=jnp.exp(GL),
                qt=qs * eG, kt=k * einv, kd=k * edec)


def _ride_split(ride, rest, n_out, n_scratch):
    if ride is None:
        return None, rest[:n_out], None, rest[n_out:], None
    return rest[0], rest[1:1 + n_out], rest[1 + n_out], rest[2 + n_out:2 + n_out + n_scratch], rest[2 + n_out + n_scratch:]


def _hgrn_fwd(hg, lb, normw, ride=None):
    S = hg.shape[0]
    sb = HGRN_SB
    nsb = S // sb
    nch = sb // HGRN_CHUNK

    def body(q_ref, f_ref, v_ref, og_ref, lb_ref, nw_ref, m_ref, *rest):
        src_ref, (y_ref, o_ref, ck_ref), got_ref, (st,), sems = _ride_split(ride, rest, 3, 1)
        j = pl.program_id(1)
        if ride is not None:
            @pl.when(jnp.logical_and(pl.program_id(0) == 0, j == 0))
            def _():
                _chip_start(src_ref, got_ref, sems[0], sems[1], ride[1])

        @pl.when(j == 0)
        def _():
            st[...] = jnp.zeros_like(st)

        ST = st[...]
        ck_ref[0, 0] = ST
        tril_m = m_ref[0]
        tril = tril_m.astype(f32) > 0.5
        pr = _hgrn_prep(q_ref[...], f_ref[...], lb_ref[...], tril_m, m_ref[1])
        qtb, ktb, kdb = pr["qt"].astype(bf16), pr["kt"].astype(bf16), pr["kd"].astype(bf16)
        eGL = pr["eGL"]
        vb = v_ref[...].astype(bf16)
        A = jnp.where(tril, lax.dot_general(qtb, ktb, NT, preferred_element_type=f32), 0.0)
        o = jnp.dot(A.astype(bf16), vb, preferred_element_type=f32)
        outs = []
        for ci in range(nch):
            lo = ci * HGRN_CHUNK
            sl = slice(lo, lo + HGRN_CHUNK)
            outs.append(o[sl] + lax.dot_general(qtb[sl], ST.astype(bf16), NT, preferred_element_type=f32))
            ST = ST * eGL[lo:lo + 1, :] + lax.dot_general(vb[sl], kdb[sl], TN, preferred_element_type=f32)
        st[...] = ST
        of = jnp.concatenate(outs, axis=0)
        o_ref[...] = of
        rms = lax.rsqrt(jnp.mean(of * of, axis=-1, keepdims=True) + EPS)
        ogv = og_ref[...]
        y_ref[...] = ((of * rms * nw_ref[...]) * (ogv * _sigmoid(ogv))).astype(bf16)

        if ride is not None:
            @pl.when(jnp.logical_and(pl.program_id(0) == 3, j == nsb - 1))
            def _():
                _chip_finish(src_ref, got_ref, sems[0], sems[1], ride[1])

    col = lambda off: pl.BlockSpec((sb, LANE), lambda h, j: (j, off + h))
    riding = ride is not None
    res = pl.pallas_call(
        body,
        grid=(4, nsb),
        in_specs=[col(0), col(4), col(8), col(12), pl.BlockSpec((1, LANE), lambda h, j: (0, h)),
                  pl.BlockSpec((1, LANE), lambda h, j: (0, 0)),
                  pl.BlockSpec((3, sb, sb), lambda h, j: (0, 0, 0))] + ([_ANY] if riding else []),
        out_specs=[col(0), col(0), pl.BlockSpec((1, 1, LANE, LANE), lambda h, j: (h, j, 0, 0))]
        + ([_ANY] if riding else []),
        out_shape=[SDS((S, HGRN_W), bf16), SDS((S, HGRN_W), f32), SDS((4, nsb, LANE, LANE), f32)]
        + ([_chip_out_shape(*ride)] if riding else []),
        scratch_shapes=[pltpu.VMEM((LANE, LANE), f32)] + (list(_CHIP_SEMS) if riding else []),
        compiler_params=_cparams(("arbitrary", "arbitrary") if riding else ("parallel", "arbitrary")),
        name="hgrn_fwd",
    )(hg, hg, hg, hg, lb, normw, _chunk_masks(), *([ride[0]] if riding else []))
    return tuple(res) if riding else (*res, None)


def _hgrn_bwd(hg, o_raw, dy, ck, lb, normw, ride=None):
    S = hg.shape[0]
    sb = HGRN_SB
    nsb = S // sb
    nch = sb // HGRN_CHUNK

    def body(q_ref, f_ref, v_ref, og_ref, o_ref, dy_ref, ck_ref, lb_ref, nw_ref, m_ref, *rest):
        src_ref, outs, got_ref, (dst, alb, anw), sems = _ride_split(ride, rest, 6, 3)
        dq_ref, df_ref, dv_ref, dog_ref, glb_ref, gnw_ref = outs
        j = pl.program_id(1)
        if ride is not None:
            @pl.when(jnp.logical_and(pl.program_id(0) == 0, j == 0))
            def _():
                _chip_start(src_ref, got_ref, sems[0], sems[1], ride[1])

        @pl.when(j == 0)
        def _():
            dst[...] = jnp.zeros_like(dst)
            alb[...] = jnp.zeros_like(alb)
            anw[...] = jnp.zeros_like(anw)

        tril_m = m_ref[0]
        tril = tril_m.astype(f32) > 0.5
        lbv = lb_ref[...]
        q_raw = q_ref[...]
        pr = _hgrn_prep(q_raw, f_ref[...], lbv, tril_m, m_ref[1])
        qt, kt, kd, eGL = pr["qt"], pr["kt"], pr["kd"], pr["eGL"]
        qtb, ktb, kdb = qt.astype(bf16), kt.astype(bf16), kd.astype(bf16)
        vb = v_ref[...].astype(bf16)

        o = o_ref[...]
        ogv = og_ref[...]
        sog = _sigmoid(ogv)
        rms = lax.rsqrt(jnp.mean(o * o, axis=-1, keepdims=True) + EPS)
        oh = o * rms
        nw = nw_ref[...]
        dyv = dy_ref[...]
        dog_ref[...] = (dyv * (oh * nw) * (sog * (1.0 + ogv * (1.0 - sog)))).astype(bf16)
        dohw = dyv * (ogv * sog)
        anw[...] += _colsum8(dohw * oh)
        doh = dohw * nw
        do = rms * (doh - oh * jnp.mean(doh * oh, axis=-1, keepdims=True))
        dob = do.astype(bf16)

        Ab = jnp.where(tril, lax.dot_general(qtb, ktb, NT, preferred_element_type=f32), 0.0).astype(bf16)
        dAb = jnp.where(tril, lax.dot_general(dob, vb, NT, preferred_element_type=f32), 0.0).astype(bf16)
        dv_acc = lax.dot_general(Ab, dob, TN, preferred_element_type=f32)
        dqt = jnp.dot(dAb, ktb, preferred_element_type=f32)
        dkt = lax.dot_general(dAb, qtb, TN, preferred_element_type=f32)

        ST = ck_ref[0, 0]
        states = []
        for ci in range(nch):
            lo = ci * HGRN_CHUNK
            sl = slice(lo, lo + HGRN_CHUNK)
            states.append(ST)
            ST = ST * eGL[lo:lo + 1, :] + lax.dot_general(vb[sl], kdb[sl], TN, preferred_element_type=f32)

        dST = dst[...]
        dqt_i, dkd_i, dv_i, deg_i = [None] * nch, [None] * nch, [None] * nch, [None] * nch
        for ci in reversed(range(nch)):
            lo = ci * HGRN_CHUNK
            sl = slice(lo, lo + HGRN_CHUNK)
            ST0 = states[ci]
            dSTb = dST.astype(bf16)
            dv_i[ci] = lax.dot_general(kdb[sl], dSTb, NT, preferred_element_type=f32)
            dqt_i[ci] = jnp.dot(dob[sl], ST0.astype(bf16), preferred_element_type=f32)
            dkd_i[ci] = jnp.dot(vb[sl], dSTb, preferred_element_type=f32)
            deg_i[ci] = jnp.broadcast_to(jnp.sum(dST * ST0, axis=0, keepdims=True), (HGRN_CHUNK, LANE))
            dST = dST * eGL[lo:lo + 1, :] + lax.dot_general(dob[sl], qtb[sl], TN, preferred_element_type=f32)
        dst[...] = dST

        dqt = dqt + jnp.concatenate(dqt_i, axis=0)
        dkd = jnp.concatenate(dkd_i, axis=0)
        dv_ref[...] = (dv_acc + jnp.concatenate(dv_i, axis=0)).astype(bf16)
        deg = jnp.concatenate(deg_i, axis=0)

        dqs = dqt * pr["eG"]
        dkdkd = dkd * kd
        dG = dqt * qt - dkt * kt - dkdkd
        dk = dkt * pr["einv"] + dkd * pr["edec"]
        dGL = _mask_dot(m_ref[1], dkdkd) + eGL * deg
        dg = _mask_dot(m_ref[2], dG) + dGL
        df = dg / pr["f"] - dk
        sig = pr["sig"]
        df_ref[...] = (df * (1.0 - lbv) * (sig * (1.0 - sig))).astype(bf16)
        alb[...] += _colsum8(df * (1.0 - sig))
        sq = pr["sq"]
        dq_ref[...] = (dqs * (sq * (1.0 + q_raw * (1.0 - sq)))).astype(bf16)

        @pl.when(j == nsb - 1)
        def _():
            glb_ref[...] = jnp.broadcast_to(jnp.sum(alb[...], axis=0, keepdims=True), (SUBLANE, LANE))
            gnw_ref[...] = jnp.broadcast_to(jnp.sum(anw[...], axis=0, keepdims=True), (SUBLANE, LANE))

        if ride is not None:
            @pl.when(jnp.logical_and(pl.program_id(0) == 3, j == nsb - 1))
            def _():
                _chip_finish(src_ref, got_ref, sems[0], sems[1], ride[1])

    rev = lambda off: pl.BlockSpec((sb, LANE), lambda h, j: (nsb - 1 - j, off + h))
    stat = pl.BlockSpec((SUBLANE, LANE), lambda h, j: (0, h))
    riding = ride is not None
    res = pl.pallas_call(
        body,
        grid=(4, nsb),
        in_specs=[rev(0), rev(4), rev(8), rev(12), rev(0), rev(0),
                  pl.BlockSpec((1, 1, LANE, LANE), lambda h, j: (h, nsb - 1 - j, 0, 0)),
                  pl.BlockSpec((1, LANE), lambda h, j: (0, h)), pl.BlockSpec((1, LANE), lambda h, j: (0, 0)),
                  pl.BlockSpec((3, sb, sb), lambda h, j: (0, 0, 0))]
        + ([_ANY] if riding else []),
        out_specs=[rev(0), rev(0), rev(0), rev(0), stat, stat] + ([_ANY] if riding else []),
        out_shape=[SDS((S, HGRN_W), bf16)] * 4 + [SDS((SUBLANE, HGRN_W), f32)] * 2
        + ([_chip_out_shape(*ride)] if riding else []),
        scratch_shapes=[pltpu.VMEM((LANE, LANE), f32), pltpu.VMEM((SUBLANE, LANE), f32),
                        pltpu.VMEM((SUBLANE, LANE), f32)] + (list(_CHIP_SEMS) if riding else []),
        compiler_params=_cparams(("arbitrary", "arbitrary") if riding else ("parallel", "arbitrary")),
        name="hgrn_bwd",
    )(hg, hg, hg, hg, o_raw, dy, ck, lb, normw, _chunk_masks(), *([ride[0]] if riding else []))
    return tuple(res) if riding else (*res, None)


def _lb_fwd(raw):
    def body(r_ref, o_ref):
        r = r_ref[...]
        m = jnp.max(r, axis=0, keepdims=True)
        e = jnp.exp(r - m)
        o_ref[...] = (e / jnp.sum(e, axis=0, keepdims=True))[0:1]

    return pl.pallas_call(body, out_shape=SDS((1, raw.shape[1]), f32), name="lb_fwd")(raw)


def _lb_bwd(raw, dlb):
    def body(r_ref, d_ref, o_ref):
        r = r_ref[...]
        m = jnp.max(r, axis=0, keepdims=True)
        e = jnp.exp(r - m)
        s = e / jnp.sum(e, axis=0, keepdims=True)
        s0 = s[0:1]
        onehot0 = jnp.where(lax.broadcasted_iota(jnp.int32, r.shape, 0) == 0, 1.0, 0.0)
        o_ref[...] = d_ref[...] * s0 * (onehot0 - s)

    return pl.pallas_call(body, out_shape=SDS(raw.shape, f32), name="lb_bwd")(raw, dlb)


def _gate_fwd(a, b, gc):
    S, D = a.shape
    tm = _pick(S, 512)

    def body(a_ref, b_ref, g0_ref, g1_ref, o_ref):
        s0, s1 = _sigmoid(g0_ref[...].astype(f32)), _sigmoid(g1_ref[...].astype(f32))
        o_ref[...] = (s0 * a_ref[...].astype(f32) + s1 * b_ref[...].astype(f32)).astype(bf16)

    row = pl.BlockSpec((tm, D), lambda i: (i, 0))
    return pl.pallas_call(
        body,
        grid=(S // tm,),
        in_specs=[row, row, row, pl.BlockSpec((tm, D), lambda i: (i, 1))],
        out_specs=row,
        out_shape=SDS((S, D), bf16),
        compiler_params=_cparams(("parallel",)),
        name="gate_fwd",
    )(a, b, gc, gc)


def _gate_bwd(dm, a, b, gc):
    S, D = a.shape
    tm = _pick(S, 512)

    def body(dm_ref, a_ref, b_ref, g0_ref, g1_ref, da_ref, db_ref, dg_ref):
        dmv = dm_ref[...].astype(f32)
        s0, s1 = _sigmoid(g0_ref[...].astype(f32)), _sigmoid(g1_ref[...].astype(f32))
        da_ref[...] = (dmv * s0).astype(bf16)
        db_ref[...] = (dmv * s1).astype(bf16)
        dg_ref[:, :D] = (dmv * a_ref[...].astype(f32) * (s0 * (1.0 - s0))).astype(bf16)
        dg_ref[:, D:] = (dmv * b_ref[...].astype(f32) * (s1 * (1.0 - s1))).astype(bf16)

    row = pl.BlockSpec((tm, D), lambda i: (i, 0))
    wide = pl.BlockSpec((tm, 2 * D), lambda i: (i, 0))
    return pl.pallas_call(
        body,
        grid=(S // tm,),
        in_specs=[row, row, row, row, pl.BlockSpec((tm, D), lambda i: (i, 1))],
        out_specs=[row, row, wide],
        out_shape=[SDS((S, D), bf16), SDS((S, D), bf16), SDS((S, 2 * D), bf16)],
        compiler_params=_cparams(("parallel",)),
        name="gate_bwd",
    )(dm, a, b, gc, gc)


CONV_ROWS = 512
INV_SQRT2 = 0.7071067811865476
INV_SQRT_2PI = 0.3989422804014327


CONV_HALO = 16


def _tile8(a, rows):
    return jnp.tile(a, (rows // a.shape[0], 1))


def _conv_rows(u_ref, w, b, r0, first):
    R = CONV_ROWS
    cur = u_ref[pl.ds(r0, R), :].astype(f32)
    prev8 = u_ref[pl.ds(pl.multiple_of(jnp.maximum(r0 - CONV_HALO, 0), CONV_HALO), CONV_HALO), :].astype(f32)
    prev8 = jnp.where(first, 0.0, prev8)
    row = lax.broadcasted_iota(jnp.int32, (R, LANE), 0)
    x1 = jnp.where(row < 1, _tile8(pltpu.roll(prev8, 1, 0), R), pltpu.roll(cur, 1, 0))
    x2 = jnp.where(row < 2, _tile8(pltpu.roll(prev8, 2, 0), R), pltpu.roll(cur, 2, 0))
    c = ((b + w[0:1] * x2) + w[1:2] * x1) + w[2:3] * cur
    return c, x2, x1, cur


def _conv_fwd(ug, uv, wg, wv, bg, bv):
    S, F = ug.shape
    nchunk = S // CONV_ROWS

    def body(ug_ref, uv_ref, wg_ref, wv_ref, bg_ref, bv_ref, o_ref):
        wgv, wvv, bgv, bvv = wg_ref[...], wv_ref[...], bg_ref[...], bv_ref[...]

        def step(ci, carry):
            r0 = pl.multiple_of(ci * CONV_ROWS, CONV_ROWS)
            cg = _conv_rows(ug_ref, wgv, bgv, r0, ci == 0)[0]
            cv = _conv_rows(uv_ref, wvv, bvv, r0, ci == 0)[0]
            gelu = 0.5 * cg * (1.0 + lax.erf(cg * INV_SQRT2))
            o_ref[pl.ds(r0, CONV_ROWS), :] = (gelu * cv).astype(bf16)
            return carry

        lax.fori_loop(0, nchunk, step, 0)

    col = pl.BlockSpec((S, LANE), lambda j: (0, j))
    w3 = pl.BlockSpec((3, LANE), lambda j: (0, j))
    b1 = pl.BlockSpec((1, LANE), lambda j: (0, j))
    return pl.pallas_call(
        body,
        grid=(F // LANE,),
        in_specs=[col, col, w3, w3, b1, b1],
        out_specs=col,
        out_shape=SDS((S, F), bf16),
        compiler_params=_cparams(("parallel",), VMEM_BIG),
        name="conv_fwd",
    )(ug, uv, wg, wv, bg, bv)


def _conv_bwd(ug, uv, dact, wg, wv, bg, bv):
    S, F = ug.shape
    R = CONV_ROWS
    nchunk = S // R

    def body(ug_ref, uv_ref, da_ref, wg_ref, wv_ref, bg_ref, bv_ref, dug_ref, duv_ref, sg_ref, sv_ref, dcg, dcv):
        wgv, wvv, bgv, bvv = wg_ref[...], wv_ref[...], bg_ref[...], bv_ref[...]
        zero = jnp.zeros((SUBLANE, LANE), f32)

        def fwd_step(ci, acc):
            r0 = pl.multiple_of(ci * R, R)
            cg, g2, g1, g0 = _conv_rows(ug_ref, wgv, bgv, r0, ci == 0)
            cv, v2, v1, v0 = _conv_rows(uv_ref, wvv, bvv, r0, ci == 0)
            da = da_ref[pl.ds(r0, R), :].astype(f32)
            cdf = 0.5 * (1.0 + lax.erf(cg * INV_SQRT2))
            pdf = INV_SQRT_2PI * jnp.exp(-0.5 * cg * cg)
            dg = da * cv * (cdf + cg * pdf)
            dv = da * (cg * cdf)
            dcg[pl.ds(r0, R), :] = dg
            dcv[pl.ds(r0, R), :] = dv
            new = (acc[0] + _colsum8(dg * g2), acc[1] + _colsum8(dg * g1), acc[2] + _colsum8(dg * g0),
                   acc[3] + _colsum8(dg),
                   acc[4] + _colsum8(dv * v2), acc[5] + _colsum8(dv * v1), acc[6] + _colsum8(dv * v0),
                   acc[7] + _colsum8(dv))
            return new

        acc = lax.fori_loop(0, nchunk, fwd_step, (zero,) * 8)
        rows = lax.broadcasted_iota(jnp.int32, (SUBLANE, LANE), 0)

        def stats(parts):
            out = jnp.zeros((SUBLANE, LANE), f32)
            for k, pt in enumerate(parts):
                out = jnp.where(rows == k, jnp.sum(pt, axis=0, keepdims=True), out)
            return out

        sg_ref[...] = stats(acc[0:4])
        sv_ref[...] = stats(acc[4:8])

        def du_rows(dc, w, r0, last):
            cur = dc[pl.ds(r0, R), :]
            nxt = dc[pl.ds(pl.multiple_of(jnp.minimum(r0 + R, S - SUBLANE), SUBLANE), SUBLANE), :]
            nxt = jnp.where(last, 0.0, nxt)
            row = lax.broadcasted_iota(jnp.int32, (R, LANE), 0)
            y1 = jnp.where(row >= R - 1, _tile8(pltpu.roll(nxt, SUBLANE - 1, 0), R), pltpu.roll(cur, R - 1, 0))
            y2 = jnp.where(row >= R - 2, _tile8(pltpu.roll(nxt, SUBLANE - 2, 0), R), pltpu.roll(cur, R - 2, 0))
            return w[2:3] * cur + w[1:2] * y1 + w[0:1] * y2

        def bwd_step(ci, carry):
            r0 = pl.multiple_of(ci * R, R)
            last = ci == nchunk - 1
            dug_ref[pl.ds(r0, R), :] = du_rows(dcg, wgv, r0, last).astype(bf16)
            duv_ref[pl.ds(r0, R), :] = du_rows(dcv, wvv, r0, last).astype(bf16)
            return carry

        lax.fori_loop(0, nchunk, bwd_step, 0)

    col = pl.BlockSpec((S, LANE), lambda j: (0, j))
    w3 = pl.BlockSpec((3, LANE), lambda j: (0, j))
    b1 = pl.BlockSpec((1, LANE), lambda j: (0, j))
    st = pl.BlockSpec((SUBLANE, LANE), lambda j: (0, j))
    return pl.pallas_call(
        body,
        grid=(F // LANE,),
        in_specs=[col, col, col, w3, w3, b1, b1],
        out_specs=[col, col, st, st],
        out_shape=[SDS((S, F), bf16), SDS((S, F), bf16), SDS((SUBLANE, F), f32), SDS((SUBLANE, F), f32)],
        scratch_shapes=[pltpu.VMEM((S, LANE), f32), pltpu.VMEM((S, LANE), f32)],
        compiler_params=_cparams(("parallel",), VMEM_BIG),
        name="conv_bwd",
    )(ug, uv, dact, wg, wv, bg, bv)


def _adam_math(w, g, m, v):
    m = ADAM_B1 * m + (1.0 - ADAM_B1) * g
    v = ADAM_B2 * v + (1.0 - ADAM_B2) * (g * g)
    m_hat = m / (1.0 - ADAM_B1 ** ADAM_STEP)
    v_hat = v / (1.0 - ADAM_B2 ** ADAM_STEP)
    delta = -ADAM_LR * (m_hat / (jnp.sqrt(v_hat) + ADAM_EPS) + ADAM_WD * w)
    return delta, m, v


def _adamw(w, m, v, g, name):
    R, C = w.shape
    parts = g.ndim == 3
    tr = R
    for t in (256, 128, 64, 32, 16):
        if R % t == 0 and R > t:
            tr = t
            break

    def body(w_ref, m_ref, v_ref, g_ref, go_ref, d_ref, mo_ref, vo_ref):
        if parts:
            gv = ((g_ref[0].astype(f32) + g_ref[1].astype(f32)) + g_ref[2].astype(f32)) + g_ref[3].astype(f32)
        else:
            gv = g_ref[...]
        go_ref[...] = gv
        d, mn, vn = _adam_math(w_ref[...], gv, m_ref[...], v_ref[...])
        d_ref[...] = d
        mo_ref[...] = mn
        vo_ref[...] = vn

    row = pl.BlockSpec((tr, C), lambda i: (i, 0))
    gspec = pl.BlockSpec((4, tr, C), lambda i: (0, i, 0)) if parts else row
    return pl.pallas_call(
        body,
        grid=(R // tr,),
        in_specs=[row, row, row, gspec],
        out_specs=[row] * 4,
        out_shape=[SDS((R, C), f32)] * 4,
        compiler_params=_cparams(("parallel",)),
        name=name,
    )(w, m, v, g)


def _sum8(parts, name):
    _, _, R, C = parts.shape

    def body(p_ref, o_ref):
        acc = p_ref[0, 0]
        for c in range(2):
            for k in range(4):
                if c or k:
                    acc = acc + p_ref[c, k]
        o_ref[...] = acc

    return pl.pallas_call(body, out_shape=SDS((R, C), f32), name=name)(parts)


def _pair_add(by_core, b, name):
    _, K, R, C = by_core.shape
    tr = R // 2 if R % 32 == 0 else R

    def body(c_ref, a_ref, b_ref, o_ref):
        o_ref[...] = (a_ref[0].astype(f32) + b_ref[...].astype(f32)).astype(bf16)

    blk = pl.BlockSpec((1, tr, C), lambda k, i, c: (k, i, 0))
    return pl.pallas_call(
        body,
        grid_spec=pltpu.PrefetchScalarGridSpec(
            num_scalar_prefetch=1,
            grid=(K, R // tr),
            in_specs=[pl.BlockSpec((1, 1, tr, C), lambda k, i, c: (c[0], k, i, 0)), blk],
            out_specs=blk,
        ),
        out_shape=SDS((K, R, C), bf16),
        compiler_params=_cparams(("parallel", "parallel")),
        name=name,
    )(lax.axis_index("c").astype(jnp.int32).reshape(1), by_core, b)


_ANY = pl.BlockSpec(memory_space=pl.ANY)


def _chip_copies(src_ref, out_ref, send_sems, recv_sems, gather):
    x, y, c = lax.axis_index("x"), lax.axis_index("y"), lax.axis_index("c")
    mine = 2 * x + y

    def piece(k):
        return src_ref if gather else src_ref.at[k]

    sends, recvs = [], []
    for j, (px, py) in enumerate([(1 - x, y), (x, 1 - y), (1 - x, 1 - y)]):
        sends.append(pltpu.make_async_remote_copy(
            src_ref=piece(2 * px + py), dst_ref=out_ref.at[mine], send_sem=send_sems.at[j],
            recv_sem=recv_sems.at[j], device_id=(px, py, c), device_id_type=MESH))
        recvs.append(pltpu.make_async_remote_copy(
            src_ref=piece(mine), dst_ref=out_ref.at[2 * px + py], send_sem=send_sems.at[j],
            recv_sem=recv_sems.at[j], device_id=(px, py, c), device_id_type=MESH))
    return sends, recvs


def _chip_start(src_ref, out_ref, send_sems, recv_sems, gather):
    for cp in _chip_copies(src_ref, out_ref, send_sems, recv_sems, gather)[0]:
        cp.start()


def _chip_finish(src_ref, out_ref, send_sems, recv_sems, gather):
    sends, recvs = _chip_copies(src_ref, out_ref, send_sems, recv_sems, gather)
    for cp in recvs:
        cp.wait_recv()
    for cp in sends:
        cp.wait_send()


def _chip_out_shape(src, gather):
    return SDS((4,) + tuple(src.shape if gather else src.shape[1:]), src.dtype)


_CHIP_SEMS = [pltpu.SemaphoreType.DMA((3,)), pltpu.SemaphoreType.DMA((3,))]


def _fill_own(out, src, gather):
    mine = 2 * lax.axis_index("x") + lax.axis_index("y")
    own = src if gather else lax.dynamic_index_in_dim(src, mine, axis=0, keepdims=False)
    return lax.dynamic_update_index_in_dim(out, own, mine, axis=0)


def _chip_comm(src, gather, name):
    def body(src_ref, out_ref, send_sems, recv_sems):
        _chip_start(src_ref, out_ref, send_sems, recv_sems, gather)
        _chip_finish(src_ref, out_ref, send_sems, recv_sems, gather)

    out = pl.pallas_call(
        body,
        in_specs=[_ANY],
        out_specs=_ANY,
        out_shape=_chip_out_shape(src, gather),
        scratch_shapes=list(_CHIP_SEMS),
        name=name,
    )(src)
    return _fill_own(out, src, gather)


_HBM = pl.BlockSpec(memory_space=pltpu.HBM)
_SEM = pl.BlockSpec(memory_space=pltpu.SEMAPHORE)
_EFFECT = pltpu.SideEffectType.DATAFLOW_SIDE_EFFECTING
_SPLIT_PEERS = {"chip_gather": 3, "chip_xchg": 3, "core_gather": 1, "core_swap": 1}


def _split_land(src, kind):
    if kind == "core_gather":
        return SDS((2,) + tuple(src.shape), src.dtype)
    if kind == "core_swap":
        return SDS(tuple(src.shape[1:]), src.dtype)
    return _chip_out_shape(src, kind == "chip_gather")


def _split_copies(src_ref, land_ref, sems, kind):
    x, y, c = lax.axis_index("x"), lax.axis_index("y"), lax.axis_index("c")
    n = _SPLIT_PEERS[kind]
    if kind == "core_gather":
        routes = [((x, y, 1 - c), src_ref, land_ref.at[c], land_ref.at[1 - c])]
    elif kind == "core_swap":
        routes = [((x, y, 1 - c), src_ref.at[1 - c], land_ref, land_ref)]
    else:
        mine = 2 * x + y
        gather = kind == "chip_gather"
        routes = [((px, py, c), src_ref if gather else src_ref.at[2 * px + py], land_ref.at[mine],
                   land_ref.at[2 * px + py]) for px, py in [(1 - x, y), (x, 1 - y), (1 - x, 1 - y)]]
    sends, recvs = [], []
    for j, (peer, piece, there, here) in enumerate(routes):
        sends.append(pltpu.make_async_remote_copy(src_ref=piece, dst_ref=there, send_sem=sems[j],
                                                  recv_sem=sems[n + j], device_id=peer, device_id_type=MESH))
        recvs.append(pltpu.make_async_remote_copy(src_ref=piece, dst_ref=here, send_sem=sems[j],
                                                  recv_sem=sems[n + j], device_id=peer, device_id_type=MESH))
    return sends, recvs


def _split_start(src, kind, name, after=None):
    land = _split_land(src, kind)
    ns = 2 * _SPLIT_PEERS[kind]
    n_in = 2 if after is None else 3

    def body(*refs):
        src_ref, land_ref = refs[:2]
        outs = refs[n_in:]
        for cp in _split_copies(src_ref, land_ref, outs[:ns], kind)[0]:
            cp.start()
        token = outs[ns + 2]
        token[...] = jnp.zeros_like(token)

    res = pl.pallas_call(
        body,
        name=name,
        out_shape=(pltpu.SemaphoreType.DMA(()),) * ns
        + (pltpu.HBM(src.shape, src.dtype), pltpu.HBM(land.shape, land.dtype), SDS((SUBLANE, LANE), f32)),
        in_specs=(_HBM, _HBM) + (() if after is None else (_ANY,)),
        out_specs=(_SEM,) * ns + (_HBM, _HBM, pl.BlockSpec(memory_space=pltpu.VMEM)),
        input_output_aliases={0: ns, 1: ns + 1},
        compiler_params=pltpu.CompilerParams(has_side_effects=_EFFECT),
    )(pltpu.with_memory_space_constraint(src, pltpu.HBM),
      pltpu.with_memory_space_constraint(lax.empty(land.shape, land.dtype), pltpu.HBM),
      *(() if after is None else (after,)))
    return (res[:ns], res[ns], res[ns + 1]), res[ns + 2]


def _split_wait(state, after, kind, name):
    sems, src_thru, land_thru = state
    ns = 2 * _SPLIT_PEERS[kind]

    def body(src_ref, land_ref, *rest):
        sends, recvs = _split_copies(src_ref, land_ref, rest[:ns], kind)
        for cp in recvs:
            cp.wait_recv()
        for cp in sends:
            cp.wait_send()

    src_out, got = pl.pallas_call(
        body,
        name=name,
        out_shape=(pltpu.HBM(src_thru.shape, src_thru.dtype), pltpu.HBM(land_thru.shape, land_thru.dtype)),
        in_specs=(_HBM, _HBM) + (_SEM,) * ns + (_ANY,),
        out_specs=(_HBM, _HBM),
        input_output_aliases={0: 0, 1: 1},
        compiler_params=pltpu.CompilerParams(has_side_effects=_EFFECT),
    )(src_thru, land_thru, *sems, after)
    if kind == "core_swap":
        return got, src_out
    if kind == "core_gather":
        return lax.dynamic_update_index_in_dim(got, src_out, lax.axis_index("c"), axis=0)
    return _fill_own(got, src_out, kind == "chip_gather")


def _core_gather(src, name):
    def body(src_ref, out_ref, send_sem, recv_sem):
        x, y, c = lax.axis_index("x"), lax.axis_index("y"), lax.axis_index("c")
        cp = pltpu.make_async_remote_copy(src_ref=src_ref, dst_ref=out_ref.at[c], send_sem=send_sem,
                                          recv_sem=recv_sem, device_id=(x, y, 1 - c), device_id_type=MESH)
        cp.start()
        pltpu.make_async_remote_copy(src_ref=src_ref, dst_ref=out_ref.at[1 - c], send_sem=send_sem,
                                     recv_sem=recv_sem, device_id=(x, y, 1 - c), device_id_type=MESH).wait_recv()
        cp.wait_send()

    out = pl.pallas_call(
        body,
        in_specs=[_ANY],
        out_specs=_ANY,
        out_shape=SDS((2,) + tuple(src.shape), src.dtype),
        scratch_shapes=[pltpu.SemaphoreType.DMA, pltpu.SemaphoreType.DMA],
        name=name,
    )(src)
    return lax.dynamic_update_index_in_dim(out, src, lax.axis_index("c"), axis=0)


def _core_swap(src, name):
    def body(src_ref, out_ref, send_sem, recv_sem):
        x, y, c = lax.axis_index("x"), lax.axis_index("y"), lax.axis_index("c")
        cp = pltpu.make_async_remote_copy(src_ref=src_ref.at[1 - c], dst_ref=out_ref, send_sem=send_sem,
                                          recv_sem=recv_sem, device_id=(x, y, 1 - c), device_id_type=MESH)
        cp.start()
        cp.wait()

    return pl.pallas_call(
        body,
        in_specs=[_ANY],
        out_specs=_ANY,
        out_shape=SDS(tuple(src.shape[1:]), src.dtype),
        scratch_shapes=[pltpu.SemaphoreType.DMA, pltpu.SemaphoreType.DMA],
        name=name,
    )(src)


_PACK_A = (("w_in", (1024, 1088)),)
_PACK_B = (("w_ba", (512, 128)), ("w_bh", (512, 128)), ("w_out", (128, 1024)), ("w_up", (1024, 704)),
           ("w_down", (352, 1024)))
_PACK_SIZES = _PACK_A + _PACK_B


def _slab_rows(sizes):
    return sum(r * c for _, (r, c) in sizes) // D_MODEL


def _pack_rows(d, sizes):
    n = d[sizes[0][0]].shape[0]
    return jnp.concatenate([d[k].reshape(n, -1, D_MODEL) for k, _ in sizes], axis=1)


def _unpack_rows(slab, sizes):
    n = slab.shape[0]
    out, lo = {}, 0
    for key, (r, c) in sizes:
        rows = r * c // D_MODEL
        out[key] = slab[:, lo:lo + rows].reshape(n, r, c)
        lo += rows
    return out


def _by_core(gslab):
    return jnp.swapaxes(gslab.reshape((4, 2) + gslab.shape[1:]), 0, 1)


def _pair_sum(by_core, tag):
    return _pair_add(by_core, _core_swap(by_core, tag + "_cores"), tag + "_pair_add")


def _cols_to_full(t):
    return jnp.swapaxes(t, 0, 1).reshape(t.shape[1], -1)


def _full_to_cols(t):
    K = t.shape[0]
    return jnp.swapaxes(t.reshape(K, 8, -1), 0, 1)


_SMALL = (("pre_mix_norm", (1, 1024)), ("rel_bias", (32, 24)), ("hgrn_lb_raw", (2, 512)), ("hgrn_norm", (1, 128)),
          ("post_mix_norm", (1, 1024)), ("pre_ffn_norm", (1, 1024)), ("conv_b", (1, 5632)),
          ("post_ffn_norm", (1, 1024)))
_SMALL_ROWS = 96
_CONVW_ROWS = 136


_SMALL_USED = sum(r * c for _, (r, c) in _SMALL)


def _pack_small(d, extra=None):
    flat = jnp.concatenate([d[k].reshape(-1) for k, _ in _SMALL] + ([] if extra is None else [extra.reshape(-1)]))
    flat = jnp.pad(flat, (0, _SMALL_ROWS * LANE - flat.shape[0]))
    return flat.reshape(_SMALL_ROWS, LANE)


def _unpack_small(p):
    flat = p.reshape(-1)
    out, lo = {}, 0
    for k, shp in _SMALL:
        n = shp[0] * shp[1]
        out[k] = flat[lo:lo + n].reshape(shp)
        lo += n
    return out


def _local_step(x, tgt, P, plan):
    S = x.shape[0]
    P = dict(P)
    lb = _lb_fwd(P["hgrn_lb_raw"])
    hs = _prep(x, P["pre_mix_norm"], plan.start_token())
    h1 = hs[0]
    W = dict(plan.weights_a(h1))
    consts = [_bias_consts(d) for d in DILATIONS]
    qkv = [_mm(hs[g], W["w_qkv"][g], "nn", bf16, f"proj_qkv{g}") for g in range(N_GROUPS)]
    hg = _mm(h1, W["w_hg"], "nn", f32, "proj_hg")
    gc = _mm(h1, W["w_gate"], "nn", bf16, "proj_gate")
    token = plan.forward_b(gc)
    obuf, lbuf, biases = [], [], []
    for g, d in enumerate(DILATIONS):
        tab_t = P["rel_bias"][:, 8 * g:8 * g + 8].T
        bias_g = _bias_build(tab_t, consts[g][0], consts[g][1], f"bias_build{g}").reshape(8, ATTN_BLOCK, 2 * ATTN_BLOCK)
        o_g, l_g = _attn_fwd(qkv[g], bias_g, (S // d) // ATTN_BLOCK, f"attn_fwd{g}", after=token)
        biases.append(bias_g)
        lbuf.append(l_g)
        obuf.append(o_g)
    y_attn, y_attn_b, w0, w1, w2 = _attn_merge(obuf[0], obuf[1], obuf[2], lbuf[0], lbuf[1], lbuf[2])
    y_hgrn, o_raw, ck, _ = _hgrn_fwd(hg, lb, P["hgrn_norm"])
    wb = plan.weights_b(y_hgrn)
    P["conv_w"] = wb.pop("conv_w")
    W.update(wb)
    a = _mm(y_attn_b, W["w_ba"], "nn", bf16, "branch_attn")
    b = _mm(y_hgrn, W["w_bh"], "nn", bf16, "branch_hgrn")
    merged = _gate_fwd(a, b, gc)
    mo = _mm(merged, W["w_out"], "nn", f32, "out_proj")
    x1, h2 = _mid_fwd(x, mo, P["post_mix_norm"], P["pre_ffn_norm"])
    ug = _mm(h2, W["w_up_g"], "nn", bf16, "up_gate")
    uv = _mm(h2, W["w_up_v"], "nn", bf16, "up_val")
    cw_g, cw_v = P["conv_w"][:, :D_FF], P["conv_w"][:, D_FF:]
    cb_g, cb_v = P["conv_b"][:, :D_FF], P["conv_b"][:, D_FF:]
    act = _conv_fwd(ug, uv, cw_g, cw_v, cb_g, cb_v)
    fo = _mm(act, W["w_down"], "nn", f32, "down_proj")
    loss, dy, dfo, g_post_ffn = _final(x1, fo, tgt, P["post_ffn_norm"])
    dact = _mm(dfo, W["w_down"], "nt", bf16, "d_act")
    gW_down = _mm(act, dfo, "tn", f32, "gw_down")
    dug, duv, st_g, st_v = _conv_bwd(ug, uv, dact, cw_g, cw_v, cb_g, cb_v)
    dh2 = _mm(dug, W["w_up_g"], "nt", f32, "dh2_gate")
    dh2 = _mm(duv, W["w_up_v"], "nt", f32, "dh2_val", acc=dh2)
    gW_up_g = _mm(h2, dug, "tn", f32, "gw_up_gate")
    gW_up_v = _mm(h2, duv, "tn", f32, "gw_up_val")
    dx1, dmo, g_pre_ffn, g_post_mix = _mid_bwd(dy, dh2, x1, mo, P["pre_ffn_norm"], P["post_mix_norm"])
    dmerged = _mm(dmo, W["w_out"], "nt", bf16, "d_merged")
    gW_out = _mm(merged, dmo, "tn", f32, "gw_out")
    da, db, dgc = _gate_bwd(dmerged, a, b, gc)
    dyattn = _mm(da, W["w_ba"], "nt", f32, "d_yattn")
    gW_ba = _mm(y_attn_b, da, "tn", f32, "gw_ba")
    dyhgrn = _mm(db, W["w_bh"], "nt", f32, "d_yhgrn")
    gW_bh = _mm(y_hgrn, db, "tn", f32, "gw_bh")
    big_b = dict(w_ba=gW_ba, w_bh=gW_bh, w_out=gW_out, w_up=[gW_up_g, gW_up_v], w_down=gW_down)
    dq_h, df_h, dv_h, dog_h, glb8, gnw8, got_b = _hgrn_bwd(hg, o_raw, dyhgrn, ck, lb, P["hgrn_norm"],
                                                          plan.bwd_ride(big_b))
    dhg = [dq_h, df_h, dv_h, dog_h]
    g_lb_raw = _lb_bwd(P["hgrn_lb_raw"], glb8[0:1])
    gn = gnw8[0:1]
    g_hgrn_norm = (gn[:, 0:128] + gn[:, 128:256]) + (gn[:, 256:384] + gn[:, 384:512])
    dos = _attn_merge_bwd(dyattn, y_attn, w0, w1, w2)
    dqkvs, gW_qkv, g_rel = [], [], []
    for g, d in enumerate(DILATIONS):
        dq, dk, dv, dbias = _attn_bwd(qkv[g], biases[g], dos[g], dos[3 + g], lbuf[g], (S // d) // ATTN_BLOCK,
                                      f"attn_bwd{g}")
        dqkvs.append([dq, dk, dv])
        gW_qkv.append(_mm(hs[g], dqkvs[g], "tn", f32, f"gw_qkv{g}"))
        g_rel.append(_bias_grad(dbias.reshape(8, -1), consts[g][0], f"bias_grad{g}"))
    gW_hg = _mm(h1, dhg, "tn", f32, "gw_hg")
    gW_gate = _mm(h1, dgc, "tn", f32, "gw_gate")
    gW_in = gW_qkv + [gW_hg, gW_gate]
    token = plan.grads_a_start(gW_in)
    dh_parts = [_mm(dqkvs[g], W["w_qkv"][g], "nt", f32, f"dh1_qkv{g}", after=token) for g in range(N_GROUPS)]
    token = plan.grads_a_exchange(dh_parts[2])
    dh_main = _mm(dhg, W["w_hg"], "nt", f32, "dh1_hg", acc=dh_parts[0], after=token)
    dh_main = _mm(dgc, W["w_gate"], "nt", f32, "dh1_gate", acc=dh_main, after=token)
    grad_x, g_pre_mix = _first_bwd(x, dx1, _dh_sum(dh_main, dh_parts[1], dh_parts[2]), P["pre_mix_norm"])

    g_conv_w = jnp.concatenate([st_g[0:3], st_v[0:3]], axis=1)
    g_conv_b = jnp.concatenate([st_g[3:4], st_v[3:4]], axis=1)
    small = dict(pre_mix_norm=g_pre_mix, rel_bias=jnp.concatenate(g_rel, axis=1), hgrn_lb_raw=g_lb_raw,
                 hgrn_norm=g_hgrn_norm, post_mix_norm=g_post_mix, pre_ffn_norm=g_pre_ffn, conv_b=g_conv_b,
                 post_ffn_norm=g_post_ffn, conv_w=g_conv_w)
    return loss, grad_x, gW_in, big_b, got_b, small


def _weights_a(both):
    w_in = jnp.transpose(both, (2, 1, 0, 3)).reshape(D_MODEL, -1)
    return dict(
        w_qkv=[w_in[:, g * QKV_G:(g + 1) * QKV_G] for g in range(N_GROUPS)],
        w_hg=w_in[:, 3 * QKV_G:3 * QKV_G + 4 * HGRN_W],
        w_gate=w_in[:, 3 * QKV_G + 4 * HGRN_W:],
    )


def _weights_b(slabs):
    sh = _unpack_rows(slabs, _PACK_B)
    w_up = _cols_to_full(sh["w_up"])
    return dict(
        w_ba=_cols_to_full(sh["w_ba"]),
        w_bh=_cols_to_full(sh["w_bh"]),
        w_out=sh["w_out"].reshape(D_MODEL, D_MODEL),
        w_up_g=w_up[:, :D_FF],
        w_up_v=w_up[:, D_FF:],
        w_down=sh["w_down"].reshape(D_FF, D_MODEL),
    )


def _dest_cols(sections, width):
    out = []
    for j in range(8):
        lo, hi, off, pieces = j * width, (j + 1) * width, 0, []
        for s in sections:
            a, b = max(lo, off), min(hi, off + s.shape[1])
            if a < b:
                pieces.append(s[:, a - off:b - off])
            off += s.shape[1]
        out.append(pieces[0] if len(pieces) == 1 else jnp.concatenate(pieces, axis=1))
    return out


def _grad_blocks_a(sections):
    cols = _dest_cols(sections, 1088)
    return jnp.stack([jnp.stack([cols[2 * k + c].astype(bf16) for k in range(4)]) for c in range(2)])


def _grad_slab_b(g):
    shards = dict(w_ba=_full_to_cols(g["w_ba"]), w_bh=_full_to_cols(g["w_bh"]), w_out=g["w_out"].reshape(8, 128, D_MODEL),
                  w_up=jnp.stack(_dest_cols(g["w_up"], 704)), w_down=g["w_down"].reshape(8, 352, D_MODEL))
    return _pack_rows({k: v.astype(bf16) for k, v in shards.items()}, _PACK_B)


_CONVW_SLAB_ROWS = 16


class _Traffic:
    def __init__(self, slab_a, slab_b, conv_w):
        hi = conv_w.astype(bf16)
        r1 = conv_w - hi.astype(f32)
        mid = r1.astype(bf16)
        lo = (r1 - mid.astype(f32)).astype(bf16)
        bits = jnp.stack([hi, mid, lo]).reshape(-1)
        tail = jnp.pad(bits, (0, _CONVW_SLAB_ROWS * D_MODEL - bits.shape[0])).reshape(_CONVW_SLAB_ROWS, D_MODEL)
        self.slab_b = jnp.concatenate([slab_b, tail], axis=0)
        self.state_a, tok = _split_start(slab_a, "chip_gather", "ag_a_start")
        self.state_b, self.token = _split_start(self.slab_b, "chip_gather", "ag_b_start", after=tok)
        self.chip_sum = None
        self.state = None

    def start_token(self):
        return self.token

    def weights_a(self, after):
        by_chip = _split_wait(self.state_a, after, "chip_gather", "ag_a_wait")
        return _weights_a(_core_gather(by_chip, "ag_a_cores"))

    def forward_b(self, after):
        by_chip = _split_wait(self.state_b, after, "chip_gather", "ag_b_wait")
        self.state, token = _split_start(by_chip, "core_gather", "ag_b_cores_start")
        return token

    def weights_b(self, after):
        both = _split_wait(self.state, after, "core_gather", "ag_b_cores_wait")
        slabs = jnp.swapaxes(both, 0, 1).reshape((8,) + tuple(self.slab_b.shape))
        rows = _slab_rows(_PACK_B)
        out = _weights_b(slabs[:, :rows])
        pieces = slabs[:, rows:].reshape(8, -1)[:, :3 * 3 * 704].reshape(8, 3, 3, 704).astype(f32)
        out["conv_w"] = _cols_to_full((pieces[:, 0] + pieces[:, 1]) + pieces[:, 2])
        return out

    def bwd_ride(self, grads):
        self.chip_sum = _pair_sum(_by_core(_grad_slab_b(grads)), "rs_b")
        return (self.chip_sum, False)

    def grads_a_start(self, sections):
        self.state, token = _split_start(_grad_blocks_a(sections), "core_swap", "rs_a_cores_start")
        return token

    def grads_a_exchange(self, after):
        from_sib, by_core = _split_wait(self.state, after, "core_swap", "rs_a_cores_wait")
        self.state, token = _split_start(_pair_add(by_core, from_sib, "rs_a_pair_add"), "chip_xchg", "rs_a_start")
        return token

    def parts(self, got_b, after):
        parts = _unpack_rows(_fill_own(got_b, self.chip_sum, False), _PACK_B)
        parts["w_in"] = _split_wait(self.state, after, "chip_xchg", "rs_a_wait")
        return parts


def kernel(x, pre_mix_norm, w_in, rel_bias, hgrn_lb_raw, hgrn_norm, w_branch_attn, w_branch_hgrn, w_out, post_mix_norm, pre_ffn_norm, w_up, conv_w, conv_b, w_down, post_ffn_norm, loss_target, m_pre_mix_norm, m_w_in, m_rel_bias, m_hgrn_lb_raw, m_hgrn_norm, m_w_branch_attn, m_w_branch_hgrn, m_w_out, m_post_mix_norm, m_pre_ffn_norm, m_w_up, m_conv_w, m_conv_b, m_w_down, m_post_ffn_norm, v_pre_mix_norm, v_w_in, v_rel_bias, v_hgrn_lb_raw, v_hgrn_norm, v_w_branch_attn, v_w_branch_hgrn, v_w_out, v_post_mix_norm, v_pre_ffn_norm, v_w_up, v_conv_w, v_conv_b, v_w_down, v_post_ffn_norm):
    ci = lax.axis_index("c")
    dev = 4 * lax.axis_index("x") + 2 * lax.axis_index("y") + ci
    wts = dict(w_in=w_in[0], w_ba=w_branch_attn[0], w_bh=w_branch_hgrn[0], w_out=w_out[0], w_up=w_up[0],
               w_down=w_down[0])
    mom = dict(w_in=m_w_in[0], w_ba=m_w_branch_attn[0], w_bh=m_w_branch_hgrn[0], w_out=m_w_out[0], w_up=m_w_up[0],
               w_down=m_w_down[0])
    var = dict(w_in=v_w_in[0], w_ba=v_w_branch_attn[0], w_bh=v_w_branch_hgrn[0], w_out=v_w_out[0], w_up=v_w_up[0],
               w_down=v_w_down[0])
    small_w = dict(pre_mix_norm=pre_mix_norm, rel_bias=rel_bias, hgrn_lb_raw=hgrn_lb_raw, hgrn_norm=hgrn_norm,
                   post_mix_norm=post_mix_norm, pre_ffn_norm=pre_ffn_norm, conv_b=conv_b, post_ffn_norm=post_ffn_norm)
    small_m = dict(pre_mix_norm=m_pre_mix_norm, rel_bias=m_rel_bias, hgrn_lb_raw=m_hgrn_lb_raw, hgrn_norm=m_hgrn_norm,
                   post_mix_norm=m_post_mix_norm, pre_ffn_norm=m_pre_ffn_norm, conv_b=m_conv_b,
                   post_ffn_norm=m_post_ffn_norm)
    small_v = dict(pre_mix_norm=v_pre_mix_norm, rel_bias=v_rel_bias, hgrn_lb_raw=v_hgrn_lb_raw, hgrn_norm=v_hgrn_norm,
                   post_mix_norm=v_post_mix_norm, pre_ffn_norm=v_pre_ffn_norm, conv_b=v_conv_b,
                   post_ffn_norm=v_post_ffn_norm)

    plan = _Traffic(wts["w_in"].astype(bf16),
                    _pack_rows({k: wts[k].astype(bf16)[None] for k, _ in _PACK_B}, _PACK_B)[0], conv_w[0])

    loss8, grad_x, _, _, got_b, small = _local_step(x[0], loss_target[0], small_w, plan)
    parts = plan.parts(got_b, grad_x)
    outs_big = {}
    for k, _ in _PACK_SIZES:
        outs_big[k] = _adamw(wts[k], mom[k], var[k], parts[k], "adamw_" + k)

    spack = jnp.concatenate([_pack_small(small, loss8[0, 0:1]),
                             jnp.pad(small["conv_w"].reshape(-1, LANE), ((0, _CONVW_ROWS - 132), (0, 0)))], axis=0)
    allp = _core_gather(_chip_comm(spack, True, "ag_small_chips"), "ag_small_cores")
    ssum = _sum8(allp, "small_sum")
    gs = ssum[:_SMALL_ROWS]
    loss = ssum[_SMALL_USED // LANE, _SMALL_USED % LANE]
    res_small = _adamw(_pack_small(small_w), _pack_small(small_m), _pack_small(small_v), gs, "adamw_small")
    sm = [_unpack_small(t) for t in res_small]
    g_cw_full = ssum[_SMALL_ROWS:_SMALL_ROWS + 132].reshape(3, 2 * D_FF)
    g_cw = lax.dynamic_slice_in_dim(g_cw_full, dev * 704, 704, axis=1)
    res_cw = _adamw(conv_w[0], m_conv_w[0], v_conv_w[0], g_cw, "adamw_conv_w")

    def pick(i):
        def big_(k):
            return outs_big[k][i][None]
        return [sm[i]["pre_mix_norm"], big_("w_in"), sm[i]["rel_bias"], sm[i]["hgrn_lb_raw"], sm[i]["hgrn_norm"],
                big_("w_ba"), big_("w_bh"), big_("w_out"), sm[i]["post_mix_norm"], sm[i]["pre_ffn_norm"],
                big_("w_up"), res_cw[i][None], sm[i]["conv_b"], big_("w_down"), sm[i]["post_ffn_norm"]]

    return (loss, grad_x[None], *pick(0), *pick(1), *pick(2), *pick(3))
```

```python
import functools
import math

import jax
import jax.numpy as jnp
from jax import lax
from jax.experimental import pallas as pl
from jax.experimental.pallas import tpu as pltpu

f32 = jnp.float32
bf16 = jnp.bfloat16
SDS = jax.ShapeDtypeStruct
HIGHEST = lax.Precision.HIGHEST
MESH = pl.DeviceIdType.MESH

NN = (((1,), (0,)), ((), ()))
NT = (((1,), (1,)), ((), ()))
TN = (((0,), (0,)), ((), ()))

D_MODEL = 1024
N_GROUPS = 3
DILATIONS = (1, 4, 16)
HEAD_DIM = 64
ATTN_BLOCK = 128
QKV_G = 1536
ATTN_OUT = 512
HGRN_W = 512
HGRN_CHUNK = 32
D_FF = 2816
NUM_BUCKETS = 32
MAX_EXACT = 16
MAX_DISTANCE = 2048
NEG_INF = -1e30
EPS = 1e-6
LANE = 128
SUBLANE = 8
VMEM_BIG = 48 * 1024 * 1024
MM_ROWS = 512
MM_OUT_BYTES = 8 * 1024 * 1024

ADAM_LR, ADAM_B1, ADAM_B2, ADAM_EPS, ADAM_WD, ADAM_STEP = 0.001, 0.9, 0.999, 1e-08, 0.01, 10


def _pick(n, pref):
    t = pref
    while t >= LANE:
        if n % t == 0:
            return t
        t //= 2
    return n


def _cparams(sem=None, vmem=None):
    kw = {}
    if sem is not None:
        kw["dimension_semantics"] = sem
    if vmem is not None:
        kw["vmem_limit_bytes"] = vmem
    return pltpu.CompilerParams(**kw)


def _sigmoid(x):
    return jax.nn.sigmoid(x)


def _colsum8(x):
    return x.reshape(x.shape[0] // SUBLANE, SUBLANE, x.shape[1]).sum(axis=0)


def _mm(a, b, mode, out_dtype, name, acc=None, after=None):
    dims = {"nn": NN, "nt": NT, "tn": TN}[mode]
    has_acc = acc is not None
    parts = list(a) if isinstance(a, (list, tuple)) else [a]
    if mode == "tn":
        assert not has_acc
        K, N = b.shape
        widths = [t.shape[1] for t in parts]
        M = sum(widths)
        whole = M * N * 4 <= MM_OUT_BYTES
        assert whole or len(parts) == 1
        tmm = M if whole else M // 2
        ts = _pick(K, 2 * MM_ROWS)
        nk = K // ts

        def body_tn(*refs):
            b_ref, o_ref = refs[-2], refs[-1]
            k = pl.program_id(1)
            bv = b_ref[...]
            lo = 0
            for a_ref, w in zip(refs[:-2], widths if whole else [tmm]):
                part = lax.dot_general(a_ref[...], bv, dims, preferred_element_type=f32)
                rows = slice(lo, lo + w)
                lo += w

                @pl.when(k == 0)
                def _(part=part, rows=rows):
                    o_ref[rows, :] = part

                @pl.when(k > 0)
                def _(part=part, rows=rows):
                    o_ref[rows, :] += part

        return pl.pallas_call(
            body_tn,
            grid=(M // tmm, nk),
            in_specs=[pl.BlockSpec((ts, w if whole else tmm), lambda i, k: (k, i)) for w in widths]
            + [pl.BlockSpec((ts, N), lambda i, k: (k, 0))],
            out_specs=pl.BlockSpec((tmm, N), lambda i, k: (i, 0)),
            out_shape=SDS((M, N), out_dtype),
            compiler_params=_cparams(("parallel", "arbitrary"), VMEM_BIG),
            name=name,
        )(*parts, b)

    widths = [t.shape[1] for t in parts]
    M = parts[0].shape[0]
    N = b.shape[1] if mode == "nn" else b.shape[0]
    tm = _pick(M, MM_ROWS)
    npart = len(parts)

    def body(*refs):
        a_refs, b_ref = refs[:npart], refs[npart]
        c_ref = refs[npart + 1] if has_acc else None
        o_ref = refs[-1]
        if npart == 1:
            part = lax.dot_general(a_refs[0][...], b_ref[...], dims, preferred_element_type=f32)
        else:
            part, lo = None, 0
            for a_ref, w in zip(a_refs, widths):
                bk = b_ref[:, lo:lo + w] if mode == "nt" else b_ref[lo:lo + w, :]
                t = lax.dot_general(a_ref[...], bk, dims, preferred_element_type=f32)
                part = t if part is None else part + t
                lo += w
        if has_acc:
            part = part + c_ref[...]
        o_ref[...] = part.astype(out_dtype)

    specs = [pl.BlockSpec((tm, w), lambda i: (i, 0)) for w in widths] + [pl.BlockSpec(b.shape, lambda i: (0, 0))]
    args = parts + [b]
    aliases = {}
    if has_acc:
        specs.append(pl.BlockSpec((tm, N), lambda i: (i, 0)))
        args.append(acc)
        aliases = {npart + 1: 0}
    if after is not None:
        specs.append(pl.BlockSpec(memory_space=pl.ANY))
        args.append(after)
    return pl.pallas_call(
        body,
        grid=(M // tm,),
        in_specs=specs,
        out_specs=pl.BlockSpec((tm, N), lambda i: (i, 0)),
        out_shape=SDS((M, N), out_dtype),
        input_output_aliases=aliases,
        compiler_params=_cparams(("parallel",), VMEM_BIG),
        name=name,
    )(*args)


PERM_ROWS = 1024


def _perm_spec(d, cols=LANE):
    return pl.BlockSpec((d, PERM_ROWS // d, cols), lambda i, j: (0, i, j))


def _to_natural(src_ref, dst_ref, d):
    n = src_ref.shape[1]
    for r in range(d):
        dst_ref[pl.ds(r, n, stride=d), :] = src_ref[r]


def _prep(x, w, after=None):
    S, D = x.shape
    R = PERM_ROWS
    nc = D // LANE
    n_in = nc + 1 + (after is not None)

    def body(*refs):
        x_refs, w_ref = refs[:nc], refs[nc]
        h_ref, h4_ref, h16_ref, rs = refs[n_in:]
        ssq = None
        for xr in x_refs:
            v = xr[...]
            t = jnp.sum(v * v, axis=-1, keepdims=True)
            ssq = t if ssq is None else ssq + t
        rinv = lax.rsqrt(ssq * (1.0 / D) + EPS)
        rs[...] = jnp.broadcast_to(rinv, (R, LANE))
        for j, xr in enumerate(x_refs):
            cols = slice(j * LANE, (j + 1) * LANE)
            wj = w_ref[:, cols]
            h_ref[:, cols] = ((xr[...] * rinv) * wj).astype(bf16)
            for d, o_ref in ((4, h4_ref), (16, h16_ref)):
                n = R // d
                for r in range(d):
                    rows = pl.ds(r, n, stride=d)
                    o_ref[r, :, cols] = ((xr[rows, :] * rs[rows, :]) * wj).astype(bf16)

    col = lambda j: pl.BlockSpec((R, LANE), lambda i, j=j: (i, j))
    h, h4, h16 = pl.pallas_call(
        body,
        grid=(S // R,),
        in_specs=[col(j) for j in range(nc)] + [pl.BlockSpec((1, D), lambda i: (0, 0))]
        + ([] if after is None else [pl.BlockSpec(memory_space=pl.ANY)]),
        out_specs=[pl.BlockSpec((R, D), lambda i: (i, 0)), pl.BlockSpec((4, R // 4, D), lambda i: (0, i, 0)),
                   pl.BlockSpec((16, R // 16, D), lambda i: (0, i, 0))],
        out_shape=[SDS((S, D), bf16), SDS((4, S // 4, D), bf16), SDS((16, S // 16, D), bf16)],
        scratch_shapes=[pltpu.VMEM((R, LANE), f32)],
        compiler_params=_cparams(("parallel",), VMEM_BIG),
        name="prep_norm_perm",
    )(*([x] * nc), w, *([] if after is None else [after]))
    return [h, h4.reshape(S, D), h16.reshape(S, D)]


def _dh_sum(a, b, c):
    S, D = a.shape
    R = PERM_ROWS

    def body(a_ref, b_ref, c_ref, o_ref, sb, sc):
        _to_natural(b_ref, sb, 4)
        _to_natural(c_ref, sc, 16)
        o_ref[...] = (a_ref[...] + sb[...]) + sc[...]

    nat = pl.BlockSpec((R, LANE), lambda i, j: (i, j))
    return pl.pallas_call(
        body,
        grid=(S // R, D // LANE),
        in_specs=[nat, _perm_spec(4), _perm_spec(16)],
        out_specs=nat,
        out_shape=SDS((S, D), f32),
        scratch_shapes=[pltpu.VMEM((R, LANE), f32)] * 2,
        compiler_params=_cparams(("parallel", "parallel")),
        name="dh_sum",
    )(a, b.reshape(4, S // 4, D), c.reshape(16, S // 16, D))


def _rms_parts(xv):
    r = lax.rsqrt(jnp.mean(xv * xv, axis=-1, keepdims=True) + EPS)
    return r, xv * r


def _rms_bwd(xhat, r, w, dy):
    dyw = dy * w
    return r * (dyw - xhat * jnp.mean(dyw * xhat, axis=-1, keepdims=True))


def _mid_fwd(x, mo, w_pm, w_pf):
    S, D = x.shape
    tm = _pick(S, 512)

    def body(x_ref, mo_ref, wpm_ref, wpf_ref, x1_ref, h2_ref):
        _, moh = _rms_parts(mo_ref[...])
        x1 = x_ref[...] + moh * wpm_ref[...]
        x1_ref[...] = x1
        _, x1h = _rms_parts(x1)
        h2_ref[...] = (x1h * wpf_ref[...]).astype(bf16)

    row = pl.BlockSpec((tm, D), lambda i: (i, 0))
    vec = pl.BlockSpec((1, D), lambda i: (0, 0))
    return pl.pallas_call(
        body,
        grid=(S // tm,),
        in_specs=[row, row, vec, vec],
        out_specs=[row, row],
        out_shape=[SDS((S, D), f32), SDS((S, D), bf16)],
        compiler_params=_cparams(("parallel",)),
        name="mid_fwd",
    )(x, mo, w_pm, w_pf)


def _final(x1, fo, tgt, w_pfn):
    S, D = x1.shape
    tm = _pick(S, 512)
    nt = S // tm

    def body(x1_ref, fo_ref, t_ref, w_ref, loss_ref, dy_ref, dfo_ref, gw_ref, lacc, gacc):
        i = pl.program_id(0)

        @pl.when(i == 0)
        def _():
            lacc[...] = jnp.zeros_like(lacc)
            gacc[...] = jnp.zeros_like(gacc)

        w = w_ref[...]
        r, foh = _rms_parts(fo_ref[...])
        y = x1_ref[...] + foh * w
        err = y - t_ref[...]
        lacc[...] += _colsum8(err * err)
        dy = err * (1.0 / D)
        dy_ref[...] = dy
        gacc[...] += _colsum8(dy * foh)
        dfo_ref[...] = _rms_bwd(foh, r, w, dy).astype(bf16)

        @pl.when(i == nt - 1)
        def _():
            loss_ref[...] = jnp.full((SUBLANE, LANE), 0.5 / D, f32) * jnp.sum(lacc[...])
            gw_ref[...] = jnp.sum(gacc[...], axis=0, keepdims=True)

    row = pl.BlockSpec((tm, D), lambda i: (i, 0))
    vec = pl.BlockSpec((1, D), lambda i: (0, 0))
    return pl.pallas_call(
        body,
        grid=(nt,),
        in_specs=[row, row, row, vec],
        out_specs=[pl.BlockSpec((SUBLANE, LANE), lambda i: (0, 0)), row, row, vec],
        out_shape=[SDS((SUBLANE, LANE), f32), SDS((S, D), f32), SDS((S, D), bf16), SDS((1, D), f32)],
        scratch_shapes=[pltpu.VMEM((SUBLANE, D), f32), pltpu.VMEM((SUBLANE, D), f32)],
        compiler_params=_cparams(("arbitrary",)),
        name="final_loss",
    )(x1, fo, tgt, w_pfn)


def _mid_bwd(dy, dh2, x1, mo, w_pf, w_pm):
    S, D = dy.shape
    tm = _pick(S, 512)
    nt = S // tm

    def body(dy_ref, dh2_ref, x1_ref, mo_ref, wpf_ref, wpm_ref, dx1_ref, dmo_ref, gpf_ref, gpm_ref, apf, apm):
        i = pl.program_id(0)

        @pl.when(i == 0)
        def _():
            apf[...] = jnp.zeros_like(apf)
            apm[...] = jnp.zeros_like(apm)

        r1, x1h = _rms_parts(x1_ref[...])
        dh2 = dh2_ref[...]
        apf[...] += _colsum8(dh2 * x1h)
        dx1 = dy_ref[...] + _rms_bwd(x1h, r1, wpf_ref[...], dh2)
        dx1_ref[...] = dx1
        rm, moh = _rms_parts(mo_ref[...])
        apm[...] += _colsum8(dx1 * moh)
        dmo_ref[...] = _rms_bwd(moh, rm, wpm_ref[...], dx1).astype(bf16)

        @pl.when(i == nt - 1)
        def _():
            gpf_ref[...] = jnp.sum(apf[...], axis=0, keepdims=True)
            gpm_ref[...] = jnp.sum(apm[...], axis=0, keepdims=True)

    row = pl.BlockSpec((tm, D), lambda i: (i, 0))
    vec = pl.BlockSpec((1, D), lambda i: (0, 0))
    return pl.pallas_call(
        body,
        grid=(nt,),
        in_specs=[row, row, row, row, vec, vec],
        out_specs=[row, row, vec, vec],
        out_shape=[SDS((S, D), f32), SDS((S, D), bf16), SDS((1, D), f32), SDS((1, D), f32)],
        scratch_shapes=[pltpu.VMEM((SUBLANE, D), f32), pltpu.VMEM((SUBLANE, D), f32)],
        compiler_params=_cparams(("arbitrary",)),
        name="mid_bwd",
    )(dy, dh2, x1, mo, w_pf, w_pm)


def _first_bwd(x, dx1, dh, w_pre):
    S, D = x.shape
    tm = _pick(S, 512)
    nt = S // tm

    def body(x_ref, dx1_ref, a_ref, w_ref, gx_ref, gw_ref, acc):
        i = pl.program_id(0)

        @pl.when(i == 0)
        def _():
            acc[...] = jnp.zeros_like(acc)

        r, xh = _rms_parts(x_ref[...])
        dh = a_ref[...]
        acc[...] += _colsum8(dh * xh)
        gx_ref[...] = dx1_ref[...] + _rms_bwd(xh, r, w_ref[...], dh)

        @pl.when(i == nt - 1)
        def _():
            gw_ref[...] = jnp.sum(acc[...], axis=0, keepdims=True)

    row = pl.BlockSpec((tm, D), lambda i: (i, 0))
    vec = pl.BlockSpec((1, D), lambda i: (0, 0))
    return pl.pallas_call(
        body,
        grid=(nt,),
        in_specs=[row, row, row, vec],
        out_specs=[row, vec],
        out_shape=[SDS((S, D), f32), SDS((1, D), f32)],
        scratch_shapes=[pltpu.VMEM((SUBLANE, D), f32)],
        compiler_params=_cparams(("arbitrary",)),
        name="first_bwd",
    )(x, dx1, dh, w_pre)


def _t5_bucket(dist):
    n = jnp.maximum(dist, 0)
    nf = jnp.maximum(n, 1).astype(f32)
    large = MAX_EXACT + (jnp.log(nf / MAX_EXACT) / math.log(MAX_DISTANCE / MAX_EXACT)
                         * (NUM_BUCKETS - MAX_EXACT)).astype(jnp.int32)
    large = jnp.minimum(large, NUM_BUCKETS - 1)
    return jnp.where(n < MAX_EXACT, n, large)


def _bias_consts(d):
    blk = ATTN_BLOCK
    rel = jnp.arange(blk)[:, None] + blk - jnp.arange(2 * blk)[None, :]
    in_win = (rel >= 0) & (rel <= blk)
    bucket = _t5_bucket(rel * d).reshape(1, -1)
    onehot = (bucket == jnp.arange(NUM_BUCKETS)[:, None]).astype(f32)
    return onehot, in_win.astype(f32).reshape(1, -1)


def _bias_build(tab_t, onehot, maskf, name):
    H = tab_t.shape[0]

    def body(t_ref, oh_ref, m_ref, o_ref):
        b = jnp.dot(t_ref[...], oh_ref[...], precision=HIGHEST, preferred_element_type=f32)
        o_ref[...] = jnp.where(m_ref[...] > 0.5, b, NEG_INF)

    return pl.pallas_call(body, out_shape=SDS((H, onehot.shape[1]), f32), name=name)(tab_t, onehot, maskf)


def _bias_grad(dbias_flat, onehot, name):
    H = dbias_flat.shape[0]

    def body(g_ref, oh_ref, o_ref):
        o_ref[...] = lax.dot_general(oh_ref[...], g_ref[...], NT, precision=HIGHEST, preferred_element_type=f32)

    return pl.pallas_call(body, out_shape=SDS((NUM_BUCKETS, H), f32), name=name)(dbias_flat, onehot)


ATTN_TILE = 512
ATTN_SUB = ATTN_TILE // ATTN_BLOCK


def _qkv_specs(nt):
    tile = (ATTN_TILE, LANE)
    blk = (ATTN_BLOCK, LANE)
    cur = lambda off: (lambda h, t: (jnp.minimum(t, nt - 1), off + h))
    prev = lambda off: (lambda h, t: (jnp.maximum(jnp.minimum(t, nt - 1) * ATTN_SUB - 1, 0), off + h))
    return [pl.BlockSpec(tile, cur(0)), pl.BlockSpec(blk, prev(4)), pl.BlockSpec(tile, cur(4)),
            pl.BlockSpec(blk, prev(8)), pl.BlockSpec(tile, cur(8))]


def _head_masks():
    lane = lax.broadcasted_iota(jnp.int32, (ATTN_BLOCK, LANE), 1)
    return lane < HEAD_DIM


def _attn_fwd(qkv, bias, bps, name, after=None):
    S = qkv.shape[0]
    nt = S // ATTN_TILE
    scale = HEAD_DIM ** -0.5

    def body(q_ref, kp_ref, kc_ref, vp_ref, vc_ref, b_ref, *rest):
        o_ref, l_ref = rest[-2:]
        t = pl.program_id(1)
        kk = jnp.concatenate([kp_ref[...], kc_ref[...]], axis=0)
        vv = jnp.concatenate([vp_ref[...], vc_ref[...]], axis=0)
        low = _head_masks()
        col = lax.broadcasted_iota(jnp.int32, (ATTN_BLOCK, 2 * ATTN_BLOCK), 1)
        for b in range(ATTN_SUB):
            lo = b * ATTN_BLOCK
            rows = slice(lo, lo + ATTN_BLOCK)
            keys = slice(lo, lo + 2 * ATTN_BLOCK)
            dead = jnp.logical_and((t * ATTN_SUB + b) % bps == 0, col < ATTN_BLOCK)
            q2 = q_ref[rows, :]
            kb, vb = kk[keys], vv[keys]
            outs, lses = [], []
            for h in range(2):
                hm = low if h == 0 else jnp.logical_not(low)
                qh = jnp.where(hm, q2, jnp.zeros_like(q2))
                s = lax.dot_general(qh, kb, NT, preferred_element_type=f32) * scale + b_ref[h]
                s = jnp.where(dead, NEG_INF, s)
                m = jnp.max(s, axis=-1, keepdims=True)
                p = jnp.exp(s - m)
                l = jnp.sum(p, axis=-1, keepdims=True)
                outs.append(jnp.dot(p.astype(bf16), vb, preferred_element_type=f32) / l)
                lses.append(m + jnp.log(l))
            o_ref[rows, :] = jnp.where(low, outs[0], outs[1])
            l_ref[rows, :] = jnp.where(low, lses[0], lses[1])

    tile = pl.BlockSpec((ATTN_TILE, LANE), lambda h, t: (t, h))
    return pl.pallas_call(
        body,
        grid=(4, nt),
        in_specs=_qkv_specs(nt) + [pl.BlockSpec((2, ATTN_BLOCK, 2 * ATTN_BLOCK), lambda h, t: (h, 0, 0))]
        + ([] if after is None else [pl.BlockSpec(memory_space=pl.ANY)]),
        out_specs=[tile, tile],
        out_shape=[SDS((S, ATTN_OUT), f32), SDS((S, ATTN_OUT), f32)],
        compiler_params=_cparams(("parallel", "parallel")),
        name=name,
    )(qkv, qkv, qkv, qkv, qkv, bias, *([] if after is None else [after]))


def _attn_bwd(qkv, bias, do, dvec, lse, bps, name):
    S = qkv.shape[0]
    nt = S // ATTN_TILE
    scale = HEAD_DIM ** -0.5

    def assemble(parts):
        rows = [parts[0][:ATTN_BLOCK]]
        for b in range(ATTN_SUB - 1):
            rows.append(parts[b][ATTN_BLOCK:] + parts[b + 1][:ATTN_BLOCK])
        rows.append(parts[-1][ATTN_BLOCK:])
        return rows

    def body(q_ref, kp_ref, kc_ref, vp_ref, vc_ref, b_ref, do_ref, dvec_ref, lse_ref,
             dq_ref, dk_ref, dv_ref, db_ref, ck, cv):
        t = pl.program_id(1)
        last = ATTN_TILE - ATTN_BLOCK

        @pl.when(t == 0)
        def _():
            ck[...] = jnp.zeros_like(ck)
            cv[...] = jnp.zeros_like(cv)
            db_ref[...] = jnp.zeros_like(db_ref)

        @pl.when(t < nt)
        def _():
            kk = jnp.concatenate([kp_ref[...], kc_ref[...]], axis=0)
            vv = jnp.concatenate([vp_ref[...], vc_ref[...]], axis=0)
            low = _head_masks()
            col = lax.broadcasted_iota(jnp.int32, (ATTN_BLOCK, 2 * ATTN_BLOCK), 1)
            low2 = lax.broadcasted_iota(jnp.int32, (2 * ATTN_BLOCK, LANE), 1) < HEAD_DIM
            dk_parts, dv_parts = [], []
            dsum = [None, None]
            for b in range(ATTN_SUB):
                lo = b * ATTN_BLOCK
                rows = slice(lo, lo + ATTN_BLOCK)
                keys = slice(lo, lo + 2 * ATTN_BLOCK)
                dead = jnp.logical_and((t * ATTN_SUB + b) % bps == 0, col < ATTN_BLOCK)
                q2 = q_ref[rows, :]
                kb, vb = kk[keys], vv[keys]
                do2 = do_ref[rows, :].astype(bf16)
                dvec2 = dvec_ref[rows, :]
                lse2 = lse_ref[rows, :]
                dqs, dks, dvs = [], [], []
                for h in range(2):
                    hm = low if h == 0 else jnp.logical_not(low)
                    c0 = h * HEAD_DIM
                    qh = jnp.where(hm, q2, jnp.zeros_like(q2))
                    doh = jnp.where(hm, do2, jnp.zeros_like(do2))
                    s = lax.dot_general(qh, kb, NT, preferred_element_type=f32) * scale + b_ref[h]
                    s = jnp.where(dead, NEG_INF, s)
                    p = jnp.exp(s - lse2[:, c0:c0 + 1])
                    dp = lax.dot_general(doh, vb, NT, preferred_element_type=f32)
                    ds = p * (dp - dvec2[:, c0:c0 + 1])
                    dsum[h] = ds if dsum[h] is None else dsum[h] + ds
                    dsb = ds.astype(bf16)
                    dqs.append(jnp.dot(dsb, kb, preferred_element_type=f32) * scale)
                    dks.append(lax.dot_general(dsb, q2, TN, preferred_element_type=f32) * scale)
                    dvs.append(lax.dot_general(p.astype(bf16), do2, TN, preferred_element_type=f32))
                dq_ref[rows, :] = jnp.where(low, dqs[0], dqs[1]).astype(bf16)
                dk_parts.append(jnp.where(low2, dks[0], dks[1]))
                dv_parts.append(jnp.where(low2, dvs[0], dvs[1]))
            db_ref[0] += dsum[0]
            db_ref[1] += dsum[1]
            for parts, carry, out_ref in ((dk_parts, ck, dk_ref), (dv_parts, cv, dv_ref)):
                rws = assemble(parts)
                out_ref[:last, :] = carry[:last, :].astype(bf16)
                out_ref[last:, :] = (carry[last:, :] + rws[0]).astype(bf16)
                for b in range(ATTN_SUB):
                    carry[b * ATTN_BLOCK:(b + 1) * ATTN_BLOCK, :] = rws[b + 1]

        @pl.when(t == nt)
        def _():
            dk_ref[...] = ck[...].astype(bf16)
            dv_ref[...] = cv[...].astype(bf16)

    tile = (ATTN_TILE, LANE)
    cur = pl.BlockSpec(tile, lambda h, t: (jnp.minimum(t, nt - 1), h))
    lag = pl.BlockSpec(tile, lambda h, t: (jnp.maximum(t - 1, 0), h))
    bspec = pl.BlockSpec((2, ATTN_BLOCK, 2 * ATTN_BLOCK), lambda h, t: (h, 0, 0))
    return pl.pallas_call(
        body,
        grid=(4, nt + 1),
        in_specs=_qkv_specs(nt) + [bspec, cur, cur, cur],
        out_specs=[cur, lag, lag, bspec],
        out_shape=[SDS((S, ATTN_OUT), bf16), SDS((S, ATTN_OUT), bf16), SDS((S, ATTN_OUT), bf16),
                   SDS((8, ATTN_BLOCK, 2 * ATTN_BLOCK), f32)],
        scratch_shapes=[pltpu.VMEM(tile, f32), pltpu.VMEM(tile, f32)],
        compiler_params=_cparams(("parallel", "arbitrary")),
        name=name,
    )(qkv, qkv, qkv, qkv, qkv, bias, do, dvec, lse)


def _attn_merge(o0, o1, o2, l0, l1, l2):
    S, W = o0.shape
    R = PERM_ROWS

    def body(o0_ref, o1_ref, o2_ref, l0_ref, l1_ref, l2_ref, y_ref, yb_ref, w0_ref, w1_ref, w2_ref,
             so1, so2, sl1, sl2):
        _to_natural(o1_ref, so1, 4)
        _to_natural(l1_ref, sl1, 4)
        _to_natural(o2_ref, so2, 16)
        _to_natural(l2_ref, sl2, 16)
        a, b, c = l0_ref[...], sl1[...], sl2[...]
        m = jnp.maximum(jnp.maximum(a, b), c)
        ea, eb, ec = jnp.exp(a - m), jnp.exp(b - m), jnp.exp(c - m)
        den = (ea + eb) + ec
        w0, w1, w2 = ea / den, eb / den, ec / den
        y = (w0 * o0_ref[...] + w1 * so1[...]) + w2 * so2[...]
        y_ref[...] = y
        yb_ref[...] = y.astype(bf16)
        w0_ref[...] = w0
        w1_ref[...] = w1
        w2_ref[...] = w2

    nat = pl.BlockSpec((R, LANE), lambda i, j: (i, j))
    v4 = lambda t: t.reshape(4, S // 4, W)
    v16 = lambda t: t.reshape(16, S // 16, W)
    return pl.pallas_call(
        body,
        grid=(S // R, W // LANE),
        in_specs=[nat, _perm_spec(4), _perm_spec(16)] * 2,
        out_specs=[nat] * 5,
        out_shape=[SDS((S, W), f32), SDS((S, W), bf16)] + [SDS((S, W), f32)] * 3,
        scratch_shapes=[pltpu.VMEM((R, LANE), f32)] * 4,
        compiler_params=_cparams(("parallel", "parallel")),
        name="attn_merge",
    )(o0, v4(o1), v16(o2), l0, v4(l1), v16(l2))


def _attn_merge_bwd(dy, y, w0, w1, w2):
    S, W = dy.shape
    R = PERM_ROWS

    def body(dy_ref, y_ref, w0_ref, w1_ref, w2_ref, a0, a1, a2, b0, b1, b2, sa, sb):
        dyv = dy_ref[...]
        r = lax.broadcasted_iota(jnp.int32, (LANE, LANE), 0) // HEAD_DIM
        c = lax.broadcasted_iota(jnp.int32, (LANE, LANE), 1) // HEAD_DIM
        seg = jnp.where(r == c, 1.0, 0.0).astype(f32)
        cbar = jnp.dot(dyv * y_ref[...], seg, precision=HIGHEST, preferred_element_type=f32)
        w = w0_ref[...]
        a0[...] = (w * dyv).astype(bf16)
        b0[...] = w * cbar
        for d, w_ref, a_ref, b_ref in ((4, w1_ref, a1, b1), (16, w2_ref, a2, b2)):
            w = w_ref[...]
            sa[...] = w * dyv
            sb[...] = w * cbar
            n = R // d
            for k in range(d):
                rows = pl.ds(k, n, stride=d)
                a_ref[k] = sa[rows, :].astype(bf16)
                b_ref[k] = sb[rows, :]

    nat = pl.BlockSpec((R, LANE), lambda i, j: (i, j))
    shapes = lambda dt: [SDS((S, W), dt), SDS((4, S // 4, W), dt), SDS((16, S // 16, W), dt)]
    outs = pl.pallas_call(
        body,
        grid=(S // R, W // LANE),
        in_specs=[nat] * 5,
        out_specs=[nat, _perm_spec(4), _perm_spec(16)] * 2,
        out_shape=shapes(bf16) + shapes(f32),
        scratch_shapes=[pltpu.VMEM((R, LANE), f32)] * 2,
        compiler_params=_cparams(("parallel", "parallel")),
        name="attn_merge_bwd",
    )(dy, y, w0, w1, w2)
    return [t.reshape(S, W) for t in outs]


HGRN_SB = 256


def _chunk_masks():
    r = jnp.arange(HGRN_SB)[:, None]
    c = jnp.arange(HGRN_SB)[None, :]
    same = (r // HGRN_CHUNK) == (c // HGRN_CHUNK)
    return jnp.stack([same & (c <= r), same, same & (c >= r)]).astype(bf16)


def _mask_dot(mask, x):
    hi = x.astype(bf16)
    r1 = x - hi.astype(f32)
    mid = r1.astype(bf16)
    lo = (r1 - mid.astype(f32)).astype(bf16)
    p = jnp.dot(mask, jnp.concatenate([hi, mid, lo], axis=1), preferred_element_type=f32)
    n = x.shape[1]
    return (p[:, :n] + p[:, n:2 * n]) + p[:, 2 * n:]


def _hgrn_prep(q_raw, f_raw, lbv, tril, same):
    sq = _sigmoid(q_raw)
    qs = q_raw * sq
    sig = _sigmoid(f_raw)
    f = lbv + (1.0 - lbv) * sig
    g = jnp.log(f)
    k = 1.0 - f
    G = _mask_dot(tril, g)
    GL = _mask_dot(same, g)
    eG = jnp.exp(G)
    einv = jnp.exp(-G)
    edec = jnp.exp(GL - G)
    return dict(sq=sq, qs=qs, sig=sig, f=f, k=k, eG=eG, einv=einv, edec=edec, eGL=jnp.exp(GL),
                qt=qs * eG, kt=k * einv, kd=k * edec)


def _ride_split(ride, rest, n_out, n_scratch):
    if ride is None:
        return None, rest[:n_out], None, rest[n_out:], None
    return rest[0], rest[1:1 + n_out], rest[1 + n_out], rest[2 + n_out:2 + n_out + n_scratch], rest[2 + n_out + n_scratch:]


def _hgrn_fwd(hg, lb, normw, ride=None):
    S = hg.shape[0]
    sb = HGRN_SB
    nsb = S // sb
    nch = sb // HGRN_CHUNK

    def body(q_ref, f_ref, v_ref, og_ref, lb_ref, nw_ref, m_ref, *rest):
        src_ref, (y_ref, o_ref, ck_ref), got_ref, (st,), sems = _ride_split(ride, rest, 3, 1)
        j = pl.program_id(1)
        if ride is not None:
            @pl.when(jnp.logical_and(pl.program_id(0) == 0, j == 0))
            def _():
                _chip_start(src_ref, got_ref, sems[0], sems[1], ride[1])

        @pl.when(j == 0)
        def _():
            st[...] = jnp.zeros_like(st)

        ST = st[...]
        ck_ref[0, 0] = ST
        tril_m = m_ref[0]
        tril = tril_m.astype(f32) > 0.5
        pr = _hgrn_prep(q_ref[...], f_ref[...], lb_ref[...], tril_m, m_ref[1])
        qtb, ktb, kdb = pr["qt"].astype(bf16), pr["kt"].astype(bf16), pr["kd"].astype(bf16)
        eGL = pr["eGL"]
        vb = v_ref[...].astype(bf16)
        A = jnp.where(tril, lax.dot_general(qtb, ktb, NT, preferred_element_type=f32), 0.0)
        o = jnp.dot(A.astype(bf16), vb, preferred_element_type=f32)
        outs = []
        for ci in range(nch):
            lo = ci * HGRN_CHUNK
            sl = slice(lo, lo + HGRN_CHUNK)
            outs.append(o[sl] + lax.dot_general(qtb[sl], ST.astype(bf16), NT, preferred_element_type=f32))
            ST = ST * eGL[lo:lo + 1, :] + lax.dot_general(vb[sl], kdb[sl], TN, preferred_element_type=f32)
        st[...] = ST
        of = jnp.concatenate(outs, axis=0)
        o_ref[...] = of
        rms = lax.rsqrt(jnp.mean(of * of, axis=-1, keepdims=True) + EPS)
        ogv = og_ref[...]
        y_ref[...] = ((of * rms * nw_ref[...]) * (ogv * _sigmoid(ogv))).astype(bf16)

        if ride is not None:
            @pl.when(jnp.logical_and(pl.program_id(0) == 3, j == nsb - 1))
            def _():
                _chip_finish(src_ref, got_ref, sems[0], sems[1], ride[1])

    col = lambda off: pl.BlockSpec((sb, LANE), lambda h, j: (j, off + h))
    riding = ride is not None
    res = pl.pallas_call(
        body,
        grid=(4, nsb),
        in_specs=[col(0), col(4), col(8), col(12), pl.BlockSpec((1, LANE), lambda h, j: (0, h)),
                  pl.BlockSpec((1, LANE), lambda h, j: (0, 0)),
                  pl.BlockSpec((3, sb, sb), lambda h, j: (0, 0, 0))] + ([_ANY] if riding else []),
        out_specs=[col(0), col(0), pl.BlockSpec((1, 1, LANE, LANE), lambda h, j: (h, j, 0, 0))]
        + ([_ANY] if riding else []),
        out_shape=[SDS((S, HGRN_W), bf16), SDS((S, HGRN_W), f32), SDS((4, nsb, LANE, LANE), f32)]
        + ([_chip_out_shape(*ride)] if riding else []),
        scratch_shapes=[pltpu.VMEM((LANE, LANE), f32)] + (list(_CHIP_SEMS) if riding else []),
        compiler_params=_cparams(("arbitrary", "arbitrary") if riding else ("parallel", "arbitrary")),
        name="hgrn_fwd",
    )(hg, hg, hg, hg, lb, normw, _chunk_masks(), *([ride[0]] if riding else []))
    return tuple(res) if riding else (*res, None)


def _hgrn_bwd(hg, o_raw, dy, ck, lb, normw, ride=None):
    S = hg.shape[0]
    sb = HGRN_SB
    nsb = S // sb
    nch = sb // HGRN_CHUNK

    def body(q_ref, f_ref, v_ref, og_ref, o_ref, dy_ref, ck_ref, lb_ref, nw_ref, m_ref, *rest):
        src_ref, outs, got_ref, (dst, alb, anw), sems = _ride_split(ride, rest, 6, 3)
        dq_ref, df_ref, dv_ref, dog_ref, glb_ref, gnw_ref = outs
        j = pl.program_id(1)
        if ride is not None:
            @pl.when(jnp.logical_and(pl.program_id(0) == 0, j == 0))
            def _():
                _chip_start(src_ref, got_ref, sems[0], sems[1], ride[1])

        @pl.when(j == 0)
        def _():
            dst[...] = jnp.zeros_like(dst)
            alb[...] = jnp.zeros_like(alb)
            anw[...] = jnp.zeros_like(anw)

        tril_m = m_ref[0]
        tril = tril_m.astype(f32) > 0.5
        lbv = lb_ref[...]
        q_raw = q_ref[...]
        pr = _hgrn_prep(q_raw, f_ref[...], lbv, tril_m, m_ref[1])
        qt, kt, kd, eGL = pr["qt"], pr["kt"], pr["kd"], pr["eGL"]
        qtb, ktb, kdb = qt.astype(bf16), kt.astype(bf16), kd.astype(bf16)
        vb = v_ref[...].astype(bf16)

        o = o_ref[...]
        ogv = og_ref[...]
        sog = _sigmoid(ogv)
        rms = lax.rsqrt(jnp.mean(o * o, axis=-1, keepdims=True) + EPS)
        oh = o * rms
        nw = nw_ref[...]
        dyv = dy_ref[...]
        dog_ref[...] = (dyv * (oh * nw) * (sog * (1.0 + ogv * (1.0 - sog)))).astype(bf16)
        dohw = dyv * (ogv * sog)
        anw[...] += _colsum8(dohw * oh)
        doh = dohw * nw
        do = rms * (doh - oh * jnp.mean(doh * oh, axis=-1, keepdims=True))
        dob = do.astype(bf16)

        Ab = jnp.where(tril, lax.dot_general(qtb, ktb, NT, preferred_element_type=f32), 0.0).astype(bf16)
        dAb = jnp.where(tril, lax.dot_general(dob, vb, NT, preferred_element_type=f32), 0.0).astype(bf16)
        dv_acc = lax.dot_general(Ab, dob, TN, preferred_element_type=f32)
        dqt = jnp.dot(dAb, ktb, preferred_element_type=f32)
        dkt = lax.dot_general(dAb, qtb, TN, preferred_element_type=f32)

        ST = ck_ref[0, 0]
        states = []
        for ci in range(nch):
            lo = ci * HGRN_CHUNK
            sl = slice(lo, lo + HGRN_CHUNK)
            states.append(ST)
            ST = ST * eGL[lo:lo + 1, :] + lax.dot_general(vb[sl], kdb[sl], TN, preferred_element_type=f32)

        dST = dst[...]
        dqt_i, dkd_i, dv_i, deg_i = [None] * nch, [None] * nch, [None] * nch, [None] * nch
        for ci in reversed(range(nch)):
            lo = ci * HGRN_CHUNK
            sl = slice(lo, lo + HGRN_CHUNK)
            ST0 = states[ci]
            dSTb = dST.astype(bf16)
            dv_i[ci] = lax.dot_general(kdb[sl], dSTb, NT, preferred_element_type=f32)
            dqt_i[ci] = jnp.dot(dob[sl], ST0.astype(bf16), preferred_element_type=f32)
            dkd_i[ci] = jnp.dot(vb[sl], dSTb, preferred_element_type=f32)
            deg_i[ci] = jnp.broadcast_to(jnp.sum(dST * ST0, axis=0, keepdims=True), (HGRN_CHUNK, LANE))
            dST = dST * eGL[lo:lo + 1, :] + lax.dot_general(dob[sl], qtb[sl], TN, preferred_element_type=f32)
        dst[...] = dST

        dqt = dqt + jnp.concatenate(dqt_i, axis=0)
        dkd = jnp.concatenate(dkd_i, axis=0)
        dv_ref[...] = (dv_acc + jnp.concatenate(dv_i, axis=0)).astype(bf16)
        deg = jnp.concatenate(deg_i, axis=0)

        dqs = dqt * pr["eG"]
        dkdkd = dkd * kd
        dG = dqt * qt - dkt * kt - dkdkd
        dk = dkt * pr["einv"] + dkd * pr["edec"]
        dGL = _mask_dot(m_ref[1], dkdkd) + eGL * deg
        dg = _mask_dot(m_ref[2], dG) + dGL
        df = dg / pr["f"] - dk
        sig = pr["sig"]
        df_ref[...] = (df * (1.0 - lbv) * (sig * (1.0 - sig))).astype(bf16)
        alb[...] += _colsum8(df * (1.0 - sig))
        sq = pr["sq"]
        dq_ref[...] = (dqs * (sq * (1.0 + q_raw * (1.0 - sq)))).astype(bf16)

        @pl.when(j == nsb - 1)
        def _():
            glb_ref[...] = jnp.broadcast_to(jnp.sum(alb[...], axis=0, keepdims=True), (SUBLANE, LANE))
            gnw_ref[...] = jnp.broadcast_to(jnp.sum(anw[...], axis=0, keepdims=True), (SUBLANE, LANE))

        if ride is not None:
            @pl.when(jnp.logical_and(pl.program_id(0) == 3, j == nsb - 1))
            def _():
                _chip_finish(src_ref, got_ref, sems[0], sems[1], ride[1])

    rev = lambda off: pl.BlockSpec((sb, LANE), lambda h, j: (nsb - 1 - j, off + h))
    stat = pl.BlockSpec((SUBLANE, LANE), lambda h, j: (0, h))
    riding = ride is not None
    res = pl.pallas_call(
        body,
        grid=(4, nsb),
        in_specs=[rev(0), rev(4), rev(8), rev(12), rev(0), rev(0),
                  pl.BlockSpec((1, 1, LANE, LANE), lambda h, j: (h, nsb - 1 - j, 0, 0)),
                  pl.BlockSpec((1, LANE), lambda h, j: (0, h)), pl.BlockSpec((1, LANE), lambda h, j: (0, 0)),
                  pl.BlockSpec((3, sb, sb), lambda h, j: (0, 0, 0))]
        + ([_ANY] if riding else []),
        out_specs=[rev(0), rev(0), rev(0), rev(0), stat, stat] + ([_ANY] if riding else []),
        out_shape=[SDS((S, HGRN_W), bf16)] * 4 + [SDS((SUBLANE, HGRN_W), f32)] * 2
        + ([_chip_out_shape(*ride)] if riding else []),
        scratch_shapes=[pltpu.VMEM((LANE, LANE), f32), pltpu.VMEM((SUBLANE, LANE), f32),
                        pltpu.VMEM((SUBLANE, LANE), f32)] + (list(_CHIP_SEMS) if riding else []),
        compiler_params=_cparams(("arbitrary", "arbitrary") if riding else ("parallel", "arbitrary")),
        name="hgrn_bwd",
    )(hg, hg, hg, hg, o_raw, dy, ck, lb, normw, _chunk_masks(), *([ride[0]] if riding else []))
    return tuple(res) if riding else (*res, None)


def _lb_fwd(raw):
    def body(r_ref, o_ref):
        r = r_ref[...]
        m = jnp.max(r, axis=0, keepdims=True)
        e = jnp.exp(r - m)
        o_ref[...] = (e / jnp.sum(e, axis=0, keepdims=True))[0:1]

    return pl.pallas_call(body, out_shape=SDS((1, raw.shape[1]), f32), name="lb_fwd")(raw)


def _lb_bwd(raw, dlb):
    def body(r_ref, d_ref, o_ref):
        r = r_ref[...]
        m = jnp.max(r, axis=0, keepdims=True)
        e = jnp.exp(r - m)
        s = e / jnp.sum(e, axis=0, keepdims=True)
        s0 = s[0:1]
        onehot0 = jnp.where(lax.broadcasted_iota(jnp.int32, r.shape, 0) == 0, 1.0, 0.0)
        o_ref[...] = d_ref[...] * s0 * (onehot0 - s)

    return pl.pallas_call(body, out_shape=SDS(raw.shape, f32), name="lb_bwd")(raw, dlb)


def _gate_fwd(a, b, gc):
    S, D = a.shape
    tm = _pick(S, 512)

    def body(a_ref, b_ref, g0_ref, g1_ref, o_ref):
        s0, s1 = _sigmoid(g0_ref[...].astype(f32)), _sigmoid(g1_ref[...].astype(f32))
        o_ref[...] = (s0 * a_ref[...].astype(f32) + s1 * b_ref[...].astype(f32)).astype(bf16)

    row = pl.BlockSpec((tm, D), lambda i: (i, 0))
    return pl.pallas_call(
        body,
        grid=(S // tm,),
        in_specs=[row, row, row, pl.BlockSpec((tm, D), lambda i: (i, 1))],
        out_specs=row,
        out_shape=SDS((S, D), bf16),
        compiler_params=_cparams(("parallel",)),
        name="gate_fwd",
    )(a, b, gc, gc)


def _gate_bwd(dm, a, b, gc):
    S, D = a.shape
    tm = _pick(S, 512)

    def body(dm_ref, a_ref, b_ref, g0_ref, g1_ref, da_ref, db_ref, dg_ref):
        dmv = dm_ref[...].astype(f32)
        s0, s1 = _sigmoid(g0_ref[...].astype(f32)), _sigmoid(g1_ref[...].astype(f32))
        da_ref[...] = (dmv * s0).astype(bf16)
        db_ref[...] = (dmv * s1).astype(bf16)
        dg_ref[:, :D] = (dmv * a_ref[...].astype(f32) * (s0 * (1.0 - s0))).astype(bf16)
        dg_ref[:, D:] = (dmv * b_ref[...].astype(f32) * (s1 * (1.0 - s1))).astype(bf16)

    row = pl.BlockSpec((tm, D), lambda i: (i, 0))
    wide = pl.BlockSpec((tm, 2 * D), lambda i: (i, 0))
    return pl.pallas_call(
        body,
        grid=(S // tm,),
        in_specs=[row, row, row, row, pl.BlockSpec((tm, D), lambda i: (i, 1))],
        out_specs=[row, row, wide],
        out_shape=[SDS((S, D), bf16), SDS((S, D), bf16), SDS((S, 2 * D), bf16)],
        compiler_params=_cparams(("parallel",)),
        name="gate_bwd",
    )(dm, a, b, gc, gc)


CONV_ROWS = 512
INV_SQRT2 = 0.7071067811865476
INV_SQRT_2PI = 0.3989422804014327


CONV_HALO = 16


def _tile8(a, rows):
    return jnp.tile(a, (rows // a.shape[0], 1))


def _conv_rows(u_ref, w, b, r0, first):
    R = CONV_ROWS
    cur = u_ref[pl.ds(r0, R), :].astype(f32)
    prev8 = u_ref[pl.ds(pl.multiple_of(jnp.maximum(r0 - CONV_HALO, 0), CONV_HALO), CONV_HALO), :].astype(f32)
    prev8 = jnp.where(first, 0.0, prev8)
    row = lax.broadcasted_iota(jnp.int32, (R, LANE), 0)
    x1 = jnp.where(row < 1, _tile8(pltpu.roll(prev8, 1, 0), R), pltpu.roll(cur, 1, 0))
    x2 = jnp.where(row < 2, _tile8(pltpu.roll(prev8, 2, 0), R), pltpu.roll(cur, 2, 0))
    c = ((b + w[0:1] * x2) + w[1:2] * x1) + w[2:3] * cur
    return c, x2, x1, cur


def _conv_fwd(ug, uv, wg, wv, bg, bv):
    S, F = ug.shape
    nchunk = S // CONV_ROWS

    def body(ug_ref, uv_ref, wg_ref, wv_ref, bg_ref, bv_ref, o_ref):
        wgv, wvv, bgv, bvv = wg_ref[...], wv_ref[...], bg_ref[...], bv_ref[...]

        def step(ci, carry):
            r0 = pl.multiple_of(ci * CONV_ROWS, CONV_ROWS)
            cg = _conv_rows(ug_ref, wgv, bgv, r0, ci == 0)[0]
            cv = _conv_rows(uv_ref, wvv, bvv, r0, ci == 0)[0]
            gelu = 0.5 * cg * (1.0 + lax.erf(cg * INV_SQRT2))
            o_ref[pl.ds(r0, CONV_ROWS), :] = (gelu * cv).astype(bf16)
            return carry

        lax.fori_loop(0, nchunk, step, 0)

    col = pl.BlockSpec((S, LANE), lambda j: (0, j))
    w3 = pl.BlockSpec((3, LANE), lambda j: (0, j))
    b1 = pl.BlockSpec((1, LANE), lambda j: (0, j))
    return pl.pallas_call(
        body,
        grid=(F // LANE,),
        in_specs=[col, col, w3, w3, b1, b1],
        out_specs=col,
        out_shape=SDS((S, F), bf16),
        compiler_params=_cparams(("parallel",), VMEM_BIG),
        name="conv_fwd",
    )(ug, uv, wg, wv, bg, bv)


def _conv_bwd(ug, uv, dact, wg, wv, bg, bv):
    S, F = ug.shape
    R = CONV_ROWS
    nchunk = S // R

    def body(ug_ref, uv_ref, da_ref, wg_ref, wv_ref, bg_ref, bv_ref, dug_ref, duv_ref, sg_ref, sv_ref, dcg, dcv):
        wgv, wvv, bgv, bvv = wg_ref[...], wv_ref[...], bg_ref[...], bv_ref[...]
        zero = jnp.zeros((SUBLANE, LANE), f32)

        def fwd_step(ci, acc):
            r0 = pl.multiple_of(ci * R, R)
            cg, g2, g1, g0 = _conv_rows(ug_ref, wgv, bgv, r0, ci == 0)
            cv, v2, v1, v0 = _conv_rows(uv_ref, wvv, bvv, r0, ci == 0)
            da = da_ref[pl.ds(r0, R), :].astype(f32)
            cdf = 0.5 * (1.0 + lax.erf(cg * INV_SQRT2))
            pdf = INV_SQRT_2PI * jnp.exp(-0.5 * cg * cg)
            dg = da * cv * (cdf + cg * pdf)
            dv = da * (cg * cdf)
            dcg[pl.ds(r0, R), :] = dg
            dcv[pl.ds(r0, R), :] = dv
            new = (acc[0] + _colsum8(dg * g2), acc[1] + _colsum8(dg * g1), acc[2] + _colsum8(dg * g0),
                   acc[3] + _colsum8(dg),
                   acc[4] + _colsum8(dv * v2), acc[5] + _colsum8(dv * v1), acc[6] + _colsum8(dv * v0),
                   acc[7] + _colsum8(dv))
            return new

        acc = lax.fori_loop(0, nchunk, fwd_step, (zero,) * 8)
        rows = lax.broadcasted_iota(jnp.int32, (SUBLANE, LANE), 0)

        def stats(parts):
            out = jnp.zeros((SUBLANE, LANE), f32)
            for k, pt in enumerate(parts):
                out = jnp.where(rows == k, jnp.sum(pt, axis=0, keepdims=True), out)
            return out

        sg_ref[...] = stats(acc[0:4])
        sv_ref[...] = stats(acc[4:8])

        def du_rows(dc, w, r0, last):
            cur = dc[pl.ds(r0, R), :]
            nxt = dc[pl.ds(pl.multiple_of(jnp.minimum(r0 + R, S - SUBLANE), SUBLANE), SUBLANE), :]
            nxt = jnp.where(last, 0.0, nxt)
            row = lax.broadcasted_iota(jnp.int32, (R, LANE), 0)
            y1 = jnp.where(row >= R - 1, _tile8(pltpu.roll(nxt, SUBLANE - 1, 0), R), pltpu.roll(cur, R - 1, 0))
            y2 = jnp.where(row >= R - 2, _tile8(pltpu.roll(nxt, SUBLANE - 2, 0), R), pltpu.roll(cur, R - 2, 0))
            return w[2:3] * cur + w[1:2] * y1 + w[0:1] * y2

        def bwd_step(ci, carry):
            r0 = pl.multiple_of(ci * R, R)
            last = ci == nchunk - 1
            dug_ref[pl.ds(r0, R), :] = du_rows(dcg, wgv, r0, last).astype(bf16)
            duv_ref[pl.ds(r0, R), :] = du_rows(dcv, wvv, r0, last).astype(bf16)
            return carry

        lax.fori_loop(0, nchunk, bwd_step, 0)

    col = pl.BlockSpec((S, LANE), lambda j: (0, j))
    w3 = pl.BlockSpec((3, LANE), lambda j: (0, j))
    b1 = pl.BlockSpec((1, LANE), lambda j: (0, j))
    st = pl.BlockSpec((SUBLANE, LANE), lambda j: (0, j))
    return pl.pallas_call(
        body,
        grid=(F // LANE,),
        in_specs=[col, col, col, w3, w3, b1, b1],
        out_specs=[col, col, st, st],
        out_shape=[SDS((S, F), bf16), SDS((S, F), bf16), SDS((SUBLANE, F), f32), SDS((SUBLANE, F), f32)],
        scratch_shapes=[pltpu.VMEM((S, LANE), f32), pltpu.VMEM((S, LANE), f32)],
        compiler_params=_cparams(("parallel",), VMEM_BIG),
        name="conv_bwd",
    )(ug, uv, dact, wg, wv, bg, bv)


def _adam_math(w, g, m, v):
    m = ADAM_B1 * m + (1.0 - ADAM_B1) * g
    v = ADAM_B2 * v + (1.0 - ADAM_B2) * (g * g)
    m_hat = m / (1.0 - ADAM_B1 ** ADAM_STEP)
    v_hat = v / (1.0 - ADAM_B2 ** ADAM_STEP)
    delta = -ADAM_LR * (m_hat / (jnp.sqrt(v_hat) + ADAM_EPS) + ADAM_WD * w)
    return delta, m, v


def _adamw(w, m, v, g, name):
    R, C = w.shape
    parts = g.ndim == 3
    tr = R
    for t in (256, 128, 64, 32, 16):
        if R % t == 0 and R > t:
            tr = t
            break

    def body(w_ref, m_ref, v_ref, g_ref, go_ref, d_ref, mo_ref, vo_ref):
        if parts:
            gv = ((g_ref[0].astype(f32) + g_ref[1].astype(f32)) + g_ref[2].astype(f32)) + g_ref[3].astype(f32)
        else:
            gv = g_ref[...]
        go_ref[...] = gv
        d, mn, vn = _adam_math(w_ref[...], gv, m_ref[...], v_ref[...])
        d_ref[...] = d
        mo_ref[...] = mn
        vo_ref[...] = vn

    row = pl.BlockSpec((tr, C), lambda i: (i, 0))
    gspec = pl.BlockSpec((4, tr, C), lambda i: (0, i, 0)) if parts else row
    return pl.pallas_call(
        body,
        grid=(R // tr,),
        in_specs=[row, row, row, gspec],
        out_specs=[row] * 4,
        out_shape=[SDS((R, C), f32)] * 4,
        compiler_params=_cparams(("parallel",)),
        name=name,
    )(w, m, v, g)


def _sum8(parts, name):
    _, _, R, C = parts.shape

    def body(p_ref, o_ref):
        acc = p_ref[0, 0]
        for c in range(2):
            for k in range(4):
                if c or k:
                    acc = acc + p_ref[c, k]
        o_ref[...] = acc

    return pl.pallas_call(body, out_shape=SDS((R, C), f32), name=name)(parts)


def _pair_add(by_core, b, name):
    _, K, R, C = by_core.shape
    tr = R // 2 if R % 32 == 0 else R

    def body(c_ref, a_ref, b_ref, o_ref):
        o_ref[...] = (a_ref[0].astype(f32) + b_ref[...].astype(f32)).astype(bf16)

    blk = pl.BlockSpec((1, tr, C), lambda k, i, c: (k, i, 0))
    return pl.pallas_call(
        body,
        grid_spec=pltpu.PrefetchScalarGridSpec(
            num_scalar_prefetch=1,
            grid=(K, R // tr),
            in_specs=[pl.BlockSpec((1, 1, tr, C), lambda k, i, c: (c[0], k, i, 0)), blk],
            out_specs=blk,
        ),
        out_shape=SDS((K, R, C), bf16),
        compiler_params=_cparams(("parallel", "parallel")),
        name=name,
    )(lax.axis_index("c").astype(jnp.int32).reshape(1), by_core, b)


_ANY = pl.BlockSpec(memory_space=pl.ANY)


def _chip_copies(src_ref, out_ref, send_sems, recv_sems, gather):
    x, y, c = lax.axis_index("x"), lax.axis_index("y"), lax.axis_index("c")
    mine = 2 * x + y

    def piece(k):
        return src_ref if gather else src_ref.at[k]

    sends, recvs = [], []
    for j, (px, py) in enumerate([(1 - x, y), (x, 1 - y), (1 - x, 1 - y)]):
        sends.append(pltpu.make_async_remote_copy(
            src_ref=piece(2 * px + py), dst_ref=out_ref.at[mine], send_sem=send_sems.at[j],
            recv_sem=recv_sems.at[j], device_id=(px, py, c), device_id_type=MESH))
        recvs.append(pltpu.make_async_remote_copy(
            src_ref=piece(mine), dst_ref=out_ref.at[2 * px + py], send_sem=send_sems.at[j],
            recv_sem=recv_sems.at[j], device_id=(px, py, c), device_id_type=MESH))
    return sends, recvs


def _chip_start(src_ref, out_ref, send_sems, recv_sems, gather):
    for cp in _chip_copies(src_ref, out_ref, send_sems, recv_sems, gather)[0]:
        cp.start()


def _chip_finish(src_ref, out_ref, send_sems, recv_sems, gather):
    sends, recvs = _chip_copies(src_ref, out_ref, send_sems, recv_sems, gather)
    for cp in recvs:
        cp.wait_recv()
    for cp in sends:
        cp.wait_send()


def _chip_out_shape(src, gather):
    return SDS((4,) + tuple(src.shape if gather else src.shape[1:]), src.dtype)


_CHIP_SEMS = [pltpu.SemaphoreType.DMA((3,)), pltpu.SemaphoreType.DMA((3,))]


def _fill_own(out, src, gather):
    mine = 2 * lax.axis_index("x") + lax.axis_index("y")
    own = src if gather else lax.dynamic_index_in_dim(src, mine, axis=0, keepdims=False)
    return lax.dynamic_update_index_in_dim(out, own, mine, axis=0)


def _chip_comm(src, gather, name):
    def body(src_ref, out_ref, send_sems, recv_sems):
        _chip_start(src_ref, out_ref, send_sems, recv_sems, gather)
        _chip_finish(src_ref, out_ref, send_sems, recv_sems, gather)

    out = pl.pallas_call(
        body,
        in_specs=[_ANY],
        out_specs=_ANY,
        out_shape=_chip_out_shape(src, gather),
        scratch_shapes=list(_CHIP_SEMS),
        name=name,
    )(src)
    return _fill_own(out, src, gather)


_HBM = pl.BlockSpec(memory_space=pltpu.HBM)
_SEM = pl.BlockSpec(memory_space=pltpu.SEMAPHORE)
_EFFECT = pltpu.SideEffectType.DATAFLOW_SIDE_EFFECTING
_SPLIT_PEERS = {"chip_gather": 3, "chip_xchg": 3, "core_gather": 1, "core_swap": 1}


def _split_land(src, kind):
    if kind == "core_gather":
        return SDS((2,) + tuple(src.shape), src.dtype)
    if kind == "core_swap":
        return SDS(tuple(src.shape[1:]), src.dtype)
    return _chip_out_shape(src, kind == "chip_gather")


def _split_copies(src_ref, land_ref, sems, kind):
    x, y, c = lax.axis_index("x"), lax.axis_index("y"), lax.axis_index("c")
    n = _SPLIT_PEERS[kind]
    if kind == "core_gather":
        routes = [((x, y, 1 - c), src_ref, land_ref.at[c], land_ref.at[1 - c])]
    elif kind == "core_swap":
        routes = [((x, y, 1 - c), src_ref.at[1 - c], land_ref, land_ref)]
    else:
        mine = 2 * x + y
        gather = kind == "chip_gather"
        routes = [((px, py, c), src_ref if gather else src_ref.at[2 * px + py], land_ref.at[mine],
                   land_ref.at[2 * px + py]) for px, py in [(1 - x, y), (x, 1 - y), (1 - x, 1 - y)]]
    sends, recvs = [], []
    for j, (peer, piece, there, here) in enumerate(routes):
        sends.append(pltpu.make_async_remote_copy(src_ref=piece, dst_ref=there, send_sem=sems[j],
                                                  recv_sem=sems[n + j], device_id=peer, device_id_type=MESH))
        recvs.append(pltpu.make_async_remote_copy(src_ref=piece, dst_ref=here, send_sem=sems[j],
                                                  recv_sem=sems[n + j], device_id=peer, device_id_type=MESH))
    return sends, recvs


def _split_start(src, kind, name, after=None):
    land = _split_land(src, kind)
    ns = 2 * _SPLIT_PEERS[kind]
    n_in = 2 if after is None else 3

    def body(*refs):
        src_ref, land_ref = refs[:2]
        outs = refs[n_in:]
        for cp in _split_copies(src_ref, land_ref, outs[:ns], kind)[0]:
            cp.start()
        token = outs[ns + 2]
        token[...] = jnp.zeros_like(token)

    res = pl.pallas_call(
        body,
        name=name,
        out_shape=(pltpu.SemaphoreType.DMA(()),) * ns
        + (pltpu.HBM(src.shape, src.dtype), pltpu.HBM(land.shape, land.dtype), SDS((SUBLANE, LANE), f32)),
        in_specs=(_HBM, _HBM) + (() if after is None else (_ANY,)),
        out_specs=(_SEM,) * ns + (_HBM, _HBM, pl.BlockSpec(memory_space=pltpu.VMEM)),
        input_output_aliases={0: ns, 1: ns + 1},
        compiler_params=pltpu.CompilerParams(has_side_effects=_EFFECT),
    )(pltpu.with_memory_space_constraint(src, pltpu.HBM),
      pltpu.with_memory_space_constraint(lax.empty(land.shape, land.dtype), pltpu.HBM),
      *(() if after is None else (after,)))
    return (res[:ns], res[ns], res[ns + 1]), res[ns + 2]


def _split_wait(state, after, kind, name):
    sems, src_thru, land_thru = state
    ns = 2 * _SPLIT_PEERS[kind]

    def body(src_ref, land_ref, *rest):
        sends, recvs = _split_copies(src_ref, land_ref, rest[:ns], kind)
        for cp in recvs:
            cp.wait_recv()
        for cp in sends:
            cp.wait_send()

    src_out, got = pl.pallas_call(
        body,
        name=name,
        out_shape=(pltpu.HBM(src_thru.shape, src_thru.dtype), pltpu.HBM(land_thru.shape, land_thru.dtype)),
        in_specs=(_HBM, _HBM) + (_SEM,) * ns + (_ANY,),
        out_specs=(_HBM, _HBM),
        input_output_aliases={0: 0, 1: 1},
        compiler_params=pltpu.CompilerParams(has_side_effects=_EFFECT),
    )(src_thru, land_thru, *sems, after)
    if kind == "core_swap":
        return got, src_out
    if kind == "core_gather":
        return lax.dynamic_update_index_in_dim(got, src_out, lax.axis_index("c"), axis=0)
    return _fill_own(got, src_out, kind == "chip_gather")


def _core_gather(src, name):
    def body(src_ref, out_ref, send_sem, recv_sem):
        x, y, c = lax.axis_index("x"), lax.axis_index("y"), lax.axis_index("c")
        cp = pltpu.make_async_remote_copy(src_ref=src_ref, dst_ref=out_ref.at[c], send_sem=send_sem,
                                          recv_sem=recv_sem, device_id=(x, y, 1 - c), device_id_type=MESH)
        cp.start()
        pltpu.make_async_remote_copy(src_ref=src_ref, dst_ref=out_ref.at[1 - c], send_sem=send_sem,
                                     recv_sem=recv_sem, device_id=(x, y, 1 - c), device_id_type=MESH).wait_recv()
        cp.wait_send()

    out = pl.pallas_call(
        body,
        in_specs=[_ANY],
        out_specs=_ANY,
        out_shape=SDS((2,) + tuple(src.shape), src.dtype),
        scratch_shapes=[pltpu.SemaphoreType.DMA, pltpu.SemaphoreType.DMA],
        name=name,
    )(src)
    return lax.dynamic_update_index_in_dim(out, src, lax.axis_index("c"), axis=0)


def _core_swap(src, name):
    def body(src_ref, out_ref, send_sem, recv_sem):
        x, y, c = lax.axis_index("x"), lax.axis_index("y"), lax.axis_index("c")
        cp = pltpu.make_async_remote_copy(src_ref=src_ref.at[1 - c], dst_ref=out_ref, send_sem=send_sem,
                                          recv_sem=recv_sem, device_id=(x, y, 1 - c), device_id_type=MESH)
        cp.start()
        cp.wait()

    return pl.pallas_call(
        body,
        in_specs=[_ANY],
        out_specs=_ANY,
        out_shape=SDS(tuple(src.shape[1:]), src.dtype),
        scratch_shapes=[pltpu.SemaphoreType.DMA, pltpu.SemaphoreType.DMA],
        name=name,
    )(src)


_PACK_A = (("w_in", (1088, 1024)),)
_PACK_B = (("w_ba", (512, 128)), ("w_bh", (512, 128)), ("w_out", (128, 1024)), ("w_up", (704, 1024)),
           ("w_down", (352, 1024)))
_PACK_SIZES = _PACK_A + _PACK_B
_TRANSPOSED = ("w_in", "w_up")


def _slab_rows(sizes):
    return sum(r * c for _, (r, c) in sizes) // D_MODEL


def _pack_rows(d, sizes):
    n = d[sizes[0][0]].shape[0]
    return jnp.concatenate([d[k].reshape(n, -1, D_MODEL) for k, _ in sizes], axis=1)


def _unpack_rows(slab, sizes):
    n = slab.shape[0]
    out, lo = {}, 0
    for key, (r, c) in sizes:
        rows = r * c // D_MODEL
        out[key] = slab[:, lo:lo + rows].reshape(n, r, c)
        lo += rows
    return out


def _by_core(gslab):
    return jnp.swapaxes(gslab.reshape((4, 2) + gslab.shape[1:]), 0, 1)


def _pair_sum(by_core, tag):
    return _pair_add(by_core, _core_swap(by_core, tag + "_cores"), tag + "_pair_add")


def _cols_to_full(t):
    return jnp.swapaxes(t, 0, 1).reshape(t.shape[1], -1)


def _full_to_cols(t):
    K = t.shape[0]
    return jnp.swapaxes(t.reshape(K, 8, -1), 0, 1)


_SMALL = (("pre_mix_norm", (1, 1024)), ("rel_bias", (32, 24)), ("hgrn_lb_raw", (2, 512)), ("hgrn_norm", (1, 128)),
          ("post_mix_norm", (1, 1024)), ("pre_ffn_norm", (1, 1024)), ("conv_b", (1, 5632)),
          ("post_ffn_norm", (1, 1024)))
_SMALL_ROWS = 96
_CONVW_ROWS = 136


_SMALL_USED = sum(r * c for _, (r, c) in _SMALL)


def _pack_small(d, extra=None):
    flat = jnp.concatenate([d[k].reshape(-1) for k, _ in _SMALL] + ([] if extra is None else [extra.reshape(-1)]))
    flat = jnp.pad(flat, (0, _SMALL_ROWS * LANE - flat.shape[0]))
    return flat.reshape(_SMALL_ROWS, LANE)


def _unpack_small(p):
    flat = p.reshape(-1)
    out, lo = {}, 0
    for k, shp in _SMALL:
        n = shp[0] * shp[1]
        out[k] = flat[lo:lo + n].reshape(shp)
        lo += n
    return out


def _local_step(x, tgt, P, plan):
    S = x.shape[0]
    P = dict(P)
    lb = _lb_fwd(P["hgrn_lb_raw"])
    hs = _prep(x, P["pre_mix_norm"], plan.start_token())
    h1 = hs[0]
    W = dict(plan.weights_a(h1))
    consts = [_bias_consts(d) for d in DILATIONS]
    qkv = [_mm(hs[g], W["wt_qkv"][g], "nt", bf16, f"proj_qkv{g}") for g in range(N_GROUPS)]
    hg = _mm(h1, W["wt_hg"], "nt", f32, "proj_hg")
    gc = _mm(h1, W["wt_gate"], "nt", bf16, "proj_gate")
    token = plan.forward_b(gc)
    obuf, lbuf, biases = [], [], []
    for g, d in enumerate(DILATIONS):
        tab_t = P["rel_bias"][:, 8 * g:8 * g + 8].T
        bias_g = _bias_build(tab_t, consts[g][0], consts[g][1], f"bias_build{g}").reshape(8, ATTN_BLOCK, 2 * ATTN_BLOCK)
        o_g, l_g = _attn_fwd(qkv[g], bias_g, (S // d) // ATTN_BLOCK, f"attn_fwd{g}", after=token)
        biases.append(bias_g)
        lbuf.append(l_g)
        obuf.append(o_g)
    y_attn, y_attn_b, w0, w1, w2 = _attn_merge(obuf[0], obuf[1], obuf[2], lbuf[0], lbuf[1], lbuf[2])
    y_hgrn, o_raw, ck, _ = _hgrn_fwd(hg, lb, P["hgrn_norm"])
    wb = plan.weights_b(y_hgrn)
    P["conv_w"] = wb.pop("conv_w")
    W.update(wb)
    a = _mm(y_attn_b, W["w_ba"], "nn", bf16, "branch_attn")
    b = _mm(y_hgrn, W["w_bh"], "nn", bf16, "branch_hgrn")
    merged = _gate_fwd(a, b, gc)
    mo = _mm(merged, W["w_out"], "nn", f32, "out_proj")
    x1, h2 = _mid_fwd(x, mo, P["post_mix_norm"], P["pre_ffn_norm"])
    ug = _mm(h2, W["wt_up_g"], "nt", bf16, "up_gate")
    uv = _mm(h2, W["wt_up_v"], "nt", bf16, "up_val")
    cw_g, cw_v = P["conv_w"][:, :D_FF], P["conv_w"][:, D_FF:]
    cb_g, cb_v = P["conv_b"][:, :D_FF], P["conv_b"][:, D_FF:]
    act = _conv_fwd(ug, uv, cw_g, cw_v, cb_g, cb_v)
    fo = _mm(act, W["w_down"], "nn", f32, "down_proj")
    loss, dy, dfo, g_post_ffn = _final(x1, fo, tgt, P["post_ffn_norm"])
    dact = _mm(dfo, W["w_down"], "nt", bf16, "d_act")
    gW_down = _mm(act, dfo, "tn", f32, "gw_down")
    dug, duv, st_g, st_v = _conv_bwd(ug, uv, dact, cw_g, cw_v, cb_g, cb_v)
    dh2 = _mm(dug, W["wt_up_g"], "nn", f32, "dh2_gate")
    dh2 = _mm(duv, W["wt_up_v"], "nn", f32, "dh2_val", acc=dh2)
    gW_up_g = _mm(dug, h2, "tn", f32, "gw_up_gate")
    gW_up_v = _mm(duv, h2, "tn", f32, "gw_up_val")
    dx1, dmo, g_pre_ffn, g_post_mix = _mid_bwd(dy, dh2, x1, mo, P["pre_ffn_norm"], P["post_mix_norm"])
    dmerged = _mm(dmo, W["w_out"], "nt", bf16, "d_merged")
    gW_out = _mm(merged, dmo, "tn", f32, "gw_out")
    da, db, dgc = _gate_bwd(dmerged, a, b, gc)
    dyattn = _mm(da, W["w_ba"], "nt", f32, "d_yattn")
    gW_ba = _mm(y_attn_b, da, "tn", f32, "gw_ba")
    dyhgrn = _mm(db, W["w_bh"], "nt", f32, "d_yhgrn")
    gW_bh = _mm(y_hgrn, db, "tn", f32, "gw_bh")
    big_b = dict(w_ba=gW_ba, w_bh=gW_bh, w_out=gW_out, w_up=[gW_up_g, gW_up_v], w_down=gW_down)
    dq_h, df_h, dv_h, dog_h, glb8, gnw8, got_b = _hgrn_bwd(hg, o_raw, dyhgrn, ck, lb, P["hgrn_norm"],
                                                          plan.bwd_ride(big_b))
    dhg = [dq_h, df_h, dv_h, dog_h]
    g_lb_raw = _lb_bwd(P["hgrn_lb_raw"], glb8[0:1])
    gn = gnw8[0:1]
    g_hgrn_norm = (gn[:, 0:128] + gn[:, 128:256]) + (gn[:, 256:384] + gn[:, 384:512])
    dos = _attn_merge_bwd(dyattn, y_attn, w0, w1, w2)
    dqkvs, gW_qkv, g_rel = [], [], []
    for g, d in enumerate(DILATIONS):
        dq, dk, dv, dbias = _attn_bwd(qkv[g], biases[g], dos[g], dos[3 + g], lbuf[g], (S // d) // ATTN_BLOCK,
                                      f"attn_bwd{g}")
        dqkvs.append([dq, dk, dv])
        gW_qkv.append(_mm(dqkvs[g], hs[g], "tn", f32, f"gw_qkv{g}"))
        g_rel.append(_bias_grad(dbias.reshape(8, -1), consts[g][0], f"bias_grad{g}"))
    gW_hg = _mm(dhg, h1, "tn", f32, "gw_hg")
    gW_gate = _mm(dgc, h1, "tn", f32, "gw_gate")
    gW_in = gW_qkv + [gW_hg, gW_gate]
    token = plan.grads_a_start(gW_in)
    dh_parts = [_mm(dqkvs[g], W["wt_qkv"][g], "nn", f32, f"dh1_qkv{g}", after=token) for g in range(N_GROUPS)]
    token = plan.grads_a_exchange(dh_parts[2])
    dh_main = _mm(dhg, W["wt_hg"], "nn", f32, "dh1_hg", acc=dh_parts[0], after=token)
    dh_main = _mm(dgc, W["wt_gate"], "nn", f32, "dh1_gate", acc=dh_main, after=token)
    grad_x, g_pre_mix = _first_bwd(x, dx1, _dh_sum(dh_main, dh_parts[1], dh_parts[2]), P["pre_mix_norm"])

    g_conv_w = jnp.concatenate([st_g[0:3], st_v[0:3]], axis=1)
    g_conv_b = jnp.concatenate([st_g[3:4], st_v[3:4]], axis=1)
    small = dict(pre_mix_norm=g_pre_mix, rel_bias=jnp.concatenate(g_rel, axis=1), hgrn_lb_raw=g_lb_raw,
                 hgrn_norm=g_hgrn_norm, post_mix_norm=g_post_mix, pre_ffn_norm=g_pre_ffn, conv_b=g_conv_b,
                 post_ffn_norm=g_post_ffn, conv_w=g_conv_w)
    return loss, grad_x, gW_in, big_b, got_b, small


def _weights_a(both):
    wt = jnp.swapaxes(both, 0, 1).reshape(-1, D_MODEL)
    return dict(
        wt_qkv=[wt[g * QKV_G:(g + 1) * QKV_G] for g in range(N_GROUPS)],
        wt_hg=wt[3 * QKV_G:3 * QKV_G + 4 * HGRN_W],
        wt_gate=wt[3 * QKV_G + 4 * HGRN_W:],
    )


def _weights_b(slabs):
    sh = _unpack_rows(slabs, _PACK_B)
    wt_up = sh["w_up"].reshape(-1, D_MODEL)
    return dict(
        w_ba=_cols_to_full(sh["w_ba"]),
        w_bh=_cols_to_full(sh["w_bh"]),
        w_out=sh["w_out"].reshape(D_MODEL, D_MODEL),
        wt_up_g=wt_up[:D_FF],
        wt_up_v=wt_up[D_FF:],
        w_down=sh["w_down"].reshape(D_FF, D_MODEL),
    )


def _dest_rows(sections, height):
    out = []
    for j in range(8):
        lo, hi, off, pieces = j * height, (j + 1) * height, 0, []
        for s in sections:
            a, b = max(lo, off), min(hi, off + s.shape[0])
            if a < b:
                pieces.append(s[a - off:b - off])
            off += s.shape[0]
        out.append(pieces[0] if len(pieces) == 1 else jnp.concatenate(pieces, axis=0))
    return out


def _grad_blocks_a(sections):
    rows = _dest_rows(sections, 1088)
    return jnp.stack([jnp.stack([rows[2 * k + c].astype(bf16) for k in range(4)]) for c in range(2)])


def _grad_slab_b(g):
    shards = dict(w_ba=_full_to_cols(g["w_ba"]), w_bh=_full_to_cols(g["w_bh"]), w_out=g["w_out"].reshape(8, 128, D_MODEL),
                  w_up=jnp.stack(_dest_rows(g["w_up"], 704)), w_down=g["w_down"].reshape(8, 352, D_MODEL))
    return _pack_rows({k: v.astype(bf16) for k, v in shards.items()}, _PACK_B)


_CONVW_SLAB_ROWS = 16


class _Traffic:
    def __init__(self, slab_a, slab_b, conv_w):
        hi = conv_w.astype(bf16)
        r1 = conv_w - hi.astype(f32)
        mid = r1.astype(bf16)
        lo = (r1 - mid.astype(f32)).astype(bf16)
        bits = jnp.stack([hi, mid, lo]).reshape(-1)
        tail = jnp.pad(bits, (0, _CONVW_SLAB_ROWS * D_MODEL - bits.shape[0])).reshape(_CONVW_SLAB_ROWS, D_MODEL)
        self.slab_b = jnp.concatenate([slab_b, tail], axis=0)
        self.state_a, tok = _split_start(slab_a, "chip_gather", "ag_a_start")
        self.state_b, self.token = _split_start(self.slab_b, "chip_gather", "ag_b_start", after=tok)
        self.chip_sum = None
        self.state = None

    def start_token(self):
        return self.token

    def weights_a(self, after):
        by_chip = _split_wait(self.state_a, after, "chip_gather", "ag_a_wait")
        return _weights_a(_core_gather(by_chip, "ag_a_cores"))

    def forward_b(self, after):
        by_chip = _split_wait(self.state_b, after, "chip_gather", "ag_b_wait")
        self.state, token = _split_start(by_chip, "core_gather", "ag_b_cores_start")
        return token

    def weights_b(self, after):
        both = _split_wait(self.state, after, "core_gather", "ag_b_cores_wait")
        slabs = jnp.swapaxes(both, 0, 1).reshape((8,) + tuple(self.slab_b.shape))
        rows = _slab_rows(_PACK_B)
        out = _weights_b(slabs[:, :rows])
        pieces = slabs[:, rows:].reshape(8, -1)[:, :3 * 3 * 704].reshape(8, 3, 3, 704).astype(f32)
        out["conv_w"] = _cols_to_full((pieces[:, 0] + pieces[:, 1]) + pieces[:, 2])
        return out

    def bwd_ride(self, grads):
        self.chip_sum = _pair_sum(_by_core(_grad_slab_b(grads)), "rs_b")
        return (self.chip_sum, False)

    def grads_a_start(self, sections):
        self.state, token = _split_start(_grad_blocks_a(sections), "core_swap", "rs_a_cores_start")
        return token

    def grads_a_exchange(self, after):
        from_sib, by_core = _split_wait(self.state, after, "core_swap", "rs_a_cores_wait")
        self.state, token = _split_start(_pair_add(by_core, from_sib, "rs_a_pair_add"), "chip_xchg", "rs_a_start")
        return token

    def parts(self, got_b, after):
        parts = _unpack_rows(_fill_own(got_b, self.chip_sum, False), _PACK_B)
        parts["w_in"] = _split_wait(self.state, after, "chip_xchg", "rs_a_wait")
        return parts


def kernel(x, pre_mix_norm, w_in, rel_bias, hgrn_lb_raw, hgrn_norm, w_branch_attn, w_branch_hgrn, w_out, post_mix_norm, pre_ffn_norm, w_up, conv_w, conv_b, w_down, post_ffn_norm, loss_target, m_pre_mix_norm, m_w_in, m_rel_bias, m_hgrn_lb_raw, m_hgrn_norm, m_w_branch_attn, m_w_branch_hgrn, m_w_out, m_post_mix_norm, m_pre_ffn_norm, m_w_up, m_conv_w, m_conv_b, m_w_down, m_post_ffn_norm, v_pre_mix_norm, v_w_in, v_rel_bias, v_hgrn_lb_raw, v_hgrn_norm, v_w_branch_attn, v_w_branch_hgrn, v_w_out, v_post_mix_norm, v_pre_ffn_norm, v_w_up, v_conv_w, v_conv_b, v_w_down, v_post_ffn_norm):
    ci = lax.axis_index("c")
    dev = 4 * lax.axis_index("x") + 2 * lax.axis_index("y") + ci
    tr = lambda t: jnp.swapaxes(t[0], 0, 1)
    wts = dict(w_in=tr(w_in), w_ba=w_branch_attn[0], w_bh=w_branch_hgrn[0], w_out=w_out[0], w_up=tr(w_up),
               w_down=w_down[0])
    mom = dict(w_in=tr(m_w_in), w_ba=m_w_branch_attn[0], w_bh=m_w_branch_hgrn[0], w_out=m_w_out[0], w_up=tr(m_w_up),
               w_down=m_w_down[0])
    var = dict(w_in=tr(v_w_in), w_ba=v_w_branch_attn[0], w_bh=v_w_branch_hgrn[0], w_out=v_w_out[0], w_up=tr(v_w_up),
               w_down=v_w_down[0])
    small_w = dict(pre_mix_norm=pre_mix_norm, rel_bias=rel_bias, hgrn_lb_raw=hgrn_lb_raw, hgrn_norm=hgrn_norm,
                   post_mix_norm=post_mix_norm, pre_ffn_norm=pre_ffn_norm, conv_b=conv_b, post_ffn_norm=post_ffn_norm)
    small_m = dict(pre_mix_norm=m_pre_mix_norm, rel_bias=m_rel_bias, hgrn_lb_raw=m_hgrn_lb_raw, hgrn_norm=m_hgrn_norm,
                   post_mix_norm=m_post_mix_norm, pre_ffn_norm=m_pre_ffn_norm, conv_b=m_conv_b,
                   post_ffn_norm=m_post_ffn_norm)
    small_v = dict(pre_mix_norm=v_pre_mix_norm, rel_bias=v_rel_bias, hgrn_lb_raw=v_hgrn_lb_raw, hgrn_norm=v_hgrn_norm,
                   post_mix_norm=v_post_mix_norm, pre_ffn_norm=v_pre_ffn_norm, conv_b=v_conv_b,
                   post_ffn_norm=v_post_ffn_norm)

    plan = _Traffic(wts["w_in"].astype(bf16),
                    _pack_rows({k: wts[k].astype(bf16)[None] for k, _ in _PACK_B}, _PACK_B)[0], conv_w[0])

    loss8, grad_x, _, _, got_b, small = _local_step(x[0], loss_target[0], small_w, plan)
    parts = plan.parts(got_b, grad_x)
    outs_big = {}
    for k, _ in _PACK_SIZES:
        outs_big[k] = _adamw(wts[k], mom[k], var[k], parts[k], "adamw_" + k)

    spack = jnp.concatenate([_pack_small(small, loss8[0, 0:1]),
                             jnp.pad(small["conv_w"].reshape(-1, LANE), ((0, _CONVW_ROWS - 132), (0, 0)))], axis=0)
    allp = _core_gather(_chip_comm(spack, True, "ag_small_chips"), "ag_small_cores")
    ssum = _sum8(allp, "small_sum")
    gs = ssum[:_SMALL_ROWS]
    loss = ssum[_SMALL_USED // LANE, _SMALL_USED % LANE]
    res_small = _adamw(_pack_small(small_w), _pack_small(small_m), _pack_small(small_v), gs, "adamw_small")
    sm = [_unpack_small(t) for t in res_small]
    g_cw_full = ssum[_SMALL_ROWS:_SMALL_ROWS + 132].reshape(3, 2 * D_FF)
    g_cw = lax.dynamic_slice_in_dim(g_cw_full, dev * 704, 704, axis=1)
    res_cw = _adamw(conv_w[0], m_conv_w[0], v_conv_w[0], g_cw, "adamw_conv_w")

    def pick(i):
        def big_(k):
            t = outs_big[k][i]
            return (jnp.swapaxes(t, 0, 1) if k in _TRANSPOSED else t)[None]
        return [sm[i]["pre_mix_norm"], big_("w_in"), sm[i]["rel_bias"], sm[i]["hgrn_lb_raw"], sm[i]["hgrn_norm"],
                big_("w_ba"), big_("w_bh"), big_("w_out"), sm[i]["post_mix_norm"], sm[i]["pre_ffn_norm"],
                big_("w_up"), res_cw[i][None], sm[i]["conv_b"], big_("w_down"), sm[i]["post_ffn_norm"]]

    return (loss, grad_x[None], *pick(0), *pick(1), *pick(2), *pick(3))
```

```python
import functools
import math

import jax
import jax.numpy as jnp
from jax import lax
from jax.experimental import pallas as pl
from jax.experimental.pallas import tpu as pltpu

f32 = jnp.float32
bf16 = jnp.bfloat16
SDS = jax.ShapeDtypeStruct
HIGHEST = lax.Precision.HIGHEST
MESH = pl.DeviceIdType.MESH

NN = (((1,), (0,)), ((), ()))
NT = (((1,), (1,)), ((), ()))
TN = (((0,), (0,)), ((), ()))

D_MODEL = 1024
N_GROUPS = 3
DILATIONS = (1, 4, 16)
HEAD_DIM = 64
ATTN_BLOCK = 128
QKV_G = 1536
ATTN_OUT = 512
HGRN_W = 512
HGRN_CHUNK = 32
D_FF = 2816
NUM_BUCKETS = 32
MAX_EXACT = 16
MAX_DISTANCE = 2048
NEG_INF = -1e30
EPS = 1e-6
LANE = 128
SUBLANE = 8
VMEM_BIG = 48 * 1024 * 1024
MM_ROWS = 512
MM_OUT_BYTES = 8 * 1024 * 1024

ADAM_LR, ADAM_B1, ADAM_B2, ADAM_EPS, ADAM_WD, ADAM_STEP = 0.001, 0.9, 0.999, 1e-08, 0.01, 10


def _pick(n, pref):
    t = pref
    while t >= LANE:
        if n % t == 0:
            return t
        t //= 2
    return n


def _cparams(sem=None, vmem=None):
    kw = {}
    if sem is not None:
        kw["dimension_semantics"] = sem
    if vmem is not None:
        kw["vmem_limit_bytes"] = vmem
    return pltpu.CompilerParams(**kw)


def _sigmoid(x):
    return jax.nn.sigmoid(x)


def _colsum8(x):
    return x.reshape(x.shape[0] // SUBLANE, SUBLANE, x.shape[1]).sum(axis=0)


def _mm(a, b, mode, out_dtype, name, acc=None, after=None):
    dims = {"nn": NN, "nt": NT, "tn": TN}[mode]
    has_acc = acc is not None
    parts = list(a) if isinstance(a, (list, tuple)) else [a]
    if mode == "tn":
        assert not has_acc
        K, N = b.shape
        widths = [t.shape[1] for t in parts]
        M = sum(widths)
        whole = M * N * 4 <= MM_OUT_BYTES
        assert whole or len(parts) == 1
        tmm = M if whole else M // 2
        ts = _pick(K, 2 * MM_ROWS)
        nk = K // ts

        def body_tn(*refs):
            b_ref, o_ref = refs[-2], refs[-1]
            k = pl.program_id(1)
            bv = b_ref[...]
            lo = 0
            for a_ref, w in zip(refs[:-2], widths if whole else [tmm]):
                part = lax.dot_general(a_ref[...], bv, dims, preferred_element_type=f32)
                rows = slice(lo, lo + w)
                lo += w

                @pl.when(k == 0)
                def _(part=part, rows=rows):
                    o_ref[rows, :] = part

                @pl.when(k > 0)
                def _(part=part, rows=rows):
                    o_ref[rows, :] += part

        return pl.pallas_call(
            body_tn,
            grid=(M // tmm, nk),
            in_specs=[pl.BlockSpec((ts, w if whole else tmm), lambda i, k: (k, i)) for w in widths]
            + [pl.BlockSpec((ts, N), lambda i, k: (k, 0))],
            out_specs=pl.BlockSpec((tmm, N), lambda i, k: (i, 0)),
            out_shape=SDS((M, N), out_dtype),
            compiler_params=_cparams(("parallel", "arbitrary"), VMEM_BIG),
            name=name,
        )(*parts, b)

    widths = [t.shape[1] for t in parts]
    M = parts[0].shape[0]
    N = b.shape[1] if mode == "nn" else b.shape[0]
    tm = _pick(M, MM_ROWS)
    npart = len(parts)

    def body(*refs):
        a_refs, b_ref = refs[:npart], refs[npart]
        c_ref = refs[npart + 1] if has_acc else None
        o_ref = refs[-1]
        if npart == 1:
            part = lax.dot_general(a_refs[0][...], b_ref[...], dims, preferred_element_type=f32)
        else:
            part, lo = None, 0
            for a_ref, w in zip(a_refs, widths):
                bk = b_ref[:, lo:lo + w] if mode == "nt" else b_ref[lo:lo + w, :]
                t = lax.dot_general(a_ref[...], bk, dims, preferred_element_type=f32)
                part = t if part is None else part + t
                lo += w
        if has_acc:
            part = part + c_ref[...]
        o_ref[...] = part.astype(out_dtype)

    specs = [pl.BlockSpec((tm, w), lambda i: (i, 0)) for w in widths] + [pl.BlockSpec(b.shape, lambda i: (0, 0))]
    args = parts + [b]
    aliases = {}
    if has_acc:
        specs.append(pl.BlockSpec((tm, N), lambda i: (i, 0)))
        args.append(acc)
        aliases = {npart + 1: 0}
    if after is not None:
        specs.append(pl.BlockSpec(memory_space=pl.ANY))
        args.append(after)
    return pl.pallas_call(
        body,
        grid=(M // tm,),
        in_specs=specs,
        out_specs=pl.BlockSpec((tm, N), lambda i: (i, 0)),
        out_shape=SDS((M, N), out_dtype),
        input_output_aliases=aliases,
        compiler_params=_cparams(("parallel",), VMEM_BIG),
        name=name,
    )(*args)


PERM_ROWS = 1024


def _perm_spec(d, cols=LANE):
    return pl.BlockSpec((d, PERM_ROWS // d, cols), lambda i, j: (0, i, j))


def _to_natural(src_ref, dst_ref, d):
    n = src_ref.shape[1]
    for r in range(d):
        dst_ref[pl.ds(r, n, stride=d), :] = src_ref[r]


def _prep(x, w, after=None):
    S, D = x.shape
    R = PERM_ROWS
    nc = D // LANE
    n_in = nc + 1 + (after is not None)

    def body(*refs):
        x_refs, w_ref = refs[:nc], refs[nc]
        h_ref, h4_ref, h16_ref, rs = refs[n_in:]
        ssq = None
        for xr in x_refs:
            v = xr[...]
            t = jnp.sum(v * v, axis=-1, keepdims=True)
            ssq = t if ssq is None else ssq + t
        rinv = lax.rsqrt(ssq * (1.0 / D) + EPS)
        rs[...] = jnp.broadcast_to(rinv, (R, LANE))
        for j, xr in enumerate(x_refs):
            cols = slice(j * LANE, (j + 1) * LANE)
            wj = w_ref[:, cols]
            h_ref[:, cols] = ((xr[...] * rinv) * wj).astype(bf16)
            for d, o_ref in ((4, h4_ref), (16, h16_ref)):
                n = R // d
                for r in range(d):
                    rows = pl.ds(r, n, stride=d)
                    o_ref[r, :, cols] = ((xr[rows, :] * rs[rows, :]) * wj).astype(bf16)

    col = lambda j: pl.BlockSpec((R, LANE), lambda i, j=j: (i, j))
    h, h4, h16 = pl.pallas_call(
        body,
        grid=(S // R,),
        in_specs=[col(j) for j in range(nc)] + [pl.BlockSpec((1, D), lambda i: (0, 0))]
        + ([] if after is None else [pl.BlockSpec(memory_space=pl.ANY)]),
        out_specs=[pl.BlockSpec((R, D), lambda i: (i, 0)), pl.BlockSpec((4, R // 4, D), lambda i: (0, i, 0)),
                   pl.BlockSpec((16, R // 16, D), lambda i: (0, i, 0))],
        out_shape=[SDS((S, D), bf16), SDS((4, S // 4, D), bf16), SDS((16, S // 16, D), bf16)],
        scratch_shapes=[pltpu.VMEM((R, LANE), f32)],
        compiler_params=_cparams(("parallel",), VMEM_BIG),
        name="prep_norm_perm",
    )(*([x] * nc), w, *([] if after is None else [after]))
    return [h, h4.reshape(S, D), h16.reshape(S, D)]


def _dh_sum(a, b, c):
    S, D = a.shape
    R = PERM_ROWS

    def body(a_ref, b_ref, c_ref, o_ref, sb, sc):
        _to_natural(b_ref, sb, 4)
        _to_natural(c_ref, sc, 16)
        o_ref[...] = (a_ref[...] + sb[...]) + sc[...]

    nat = pl.BlockSpec((R, LANE), lambda i, j: (i, j))
    return pl.pallas_call(
        body,
        grid=(S // R, D // LANE),
        in_specs=[nat, _perm_spec(4), _perm_spec(16)],
        out_specs=nat,
        out_shape=SDS((S, D), f32),
        scratch_shapes=[pltpu.VMEM((R, LANE), f32)] * 2,
        compiler_params=_cparams(("parallel", "parallel")),
        name="dh_sum",
    )(a, b.reshape(4, S // 4, D), c.reshape(16, S // 16, D))


def _rms_parts(xv):
    r = lax.rsqrt(jnp.mean(xv * xv, axis=-1, keepdims=True) + EPS)
    return r, xv * r


def _rms_bwd(xhat, r, w, dy):
    dyw = dy * w
    return r * (dyw - xhat * jnp.mean(dyw * xhat, axis=-1, keepdims=True))


def _mid_fwd(x, mo, w_pm, w_pf):
    S, D = x.shape
    tm = _pick(S, 512)

    def body(x_ref, mo_ref, wpm_ref, wpf_ref, x1_ref, h2_ref):
        _, moh = _rms_parts(mo_ref[...])
        x1 = x_ref[...] + moh * wpm_ref[...]
        x1_ref[...] = x1
        _, x1h = _rms_parts(x1)
        h2_ref[...] = (x1h * wpf_ref[...]).astype(bf16)

    row = pl.BlockSpec((tm, D), lambda i: (i, 0))
    vec = pl.BlockSpec((1, D), lambda i: (0, 0))
    return pl.pallas_call(
        body,
        grid=(S // tm,),
        in_specs=[row, row, vec, vec],
        out_specs=[row, row],
        out_shape=[SDS((S, D), f32), SDS((S, D), bf16)],
        compiler_params=_cparams(("parallel",)),
        name="mid_fwd",
    )(x, mo, w_pm, w_pf)


def _final(x1, fo, tgt, w_pfn):
    S, D = x1.shape
    tm = _pick(S, 512)
    nt = S // tm

    def body(x1_ref, fo_ref, t_ref, w_ref, loss_ref, dy_ref, dfo_ref, gw_ref, lacc, gacc):
        i = pl.program_id(0)

        @pl.when(i == 0)
        def _():
            lacc[...] = jnp.zeros_like(lacc)
            gacc[...] = jnp.zeros_like(gacc)

        w = w_ref[...]
        r, foh = _rms_parts(fo_ref[...])
        y = x1_ref[...] + foh * w
        err = y - t_ref[...]
        lacc[...] += _colsum8(err * err)
        dy = err * (1.0 / D)
        dy_ref[...] = dy
        gacc[...] += _colsum8(dy * foh)
        dfo_ref[...] = _rms_bwd(foh, r, w, dy).astype(bf16)

        @pl.when(i == nt - 1)
        def _():
            loss_ref[...] = jnp.full((SUBLANE, LANE), 0.5 / D, f32) * jnp.sum(lacc[...])
            gw_ref[...] = jnp.sum(gacc[...], axis=0, keepdims=True)

    row = pl.BlockSpec((tm, D), lambda i: (i, 0))
    vec = pl.BlockSpec((1, D), lambda i: (0, 0))
    return pl.pallas_call(
        body,
        grid=(nt,),
        in_specs=[row, row, row, vec],
        out_specs=[pl.BlockSpec((SUBLANE, LANE), lambda i: (0, 0)), row, row, vec],
        out_shape=[SDS((SUBLANE, LANE), f32), SDS((S, D), f32), SDS((S, D), bf16), SDS((1, D), f32)],
        scratch_shapes=[pltpu.VMEM((SUBLANE, D), f32), pltpu.VMEM((SUBLANE, D), f32)],
        compiler_params=_cparams(("arbitrary",)),
        name="final_loss",
    )(x1, fo, tgt, w_pfn)


def _mid_bwd(dy, dh2, x1, mo, w_pf, w_pm):
    S, D = dy.shape
    tm = _pick(S, 512)
    nt = S // tm

    def body(dy_ref, dh2_ref, x1_ref, mo_ref, wpf_ref, wpm_ref, dx1_ref, dmo_ref, gpf_ref, gpm_ref, apf, apm):
        i = pl.program_id(0)

        @pl.when(i == 0)
        def _():
            apf[...] = jnp.zeros_like(apf)
            apm[...] = jnp.zeros_like(apm)

        r1, x1h = _rms_parts(x1_ref[...])
        dh2 = dh2_ref[...]
        apf[...] += _colsum8(dh2 * x1h)
        dx1 = dy_ref[...] + _rms_bwd(x1h, r1, wpf_ref[...], dh2)
        dx1_ref[...] = dx1
        rm, moh = _rms_parts(mo_ref[...])
        apm[...] += _colsum8(dx1 * moh)
        dmo_ref[...] = _rms_bwd(moh, rm, wpm_ref[...], dx1).astype(bf16)

        @pl.when(i == nt - 1)
        def _():
            gpf_ref[...] = jnp.sum(apf[...], axis=0, keepdims=True)
            gpm_ref[...] = jnp.sum(apm[...], axis=0, keepdims=True)

    row = pl.BlockSpec((tm, D), lambda i: (i, 0))
    vec = pl.BlockSpec((1, D), lambda i: (0, 0))
    return pl.pallas_call(
        body,
        grid=(nt,),
        in_specs=[row, row, row, row, vec, vec],
        out_specs=[row, row, vec, vec],
        out_shape=[SDS((S, D), f32), SDS((S, D), bf16), SDS((1, D), f32), SDS((1, D), f32)],
        scratch_shapes=[pltpu.VMEM((SUBLANE, D), f32), pltpu.VMEM((SUBLANE, D), f32)],
        compiler_params=_cparams(("arbitrary",)),
        name="mid_bwd",
    )(dy, dh2, x1, mo, w_pf, w_pm)


def _first_bwd(x, dx1, dh, w_pre):
    S, D = x.shape
    tm = _pick(S, 512)
    nt = S // tm

    def body(x_ref, dx1_ref, a_ref, w_ref, gx_ref, gw_ref, acc):
        i = pl.program_id(0)

        @pl.when(i == 0)
        def _():
            acc[...] = jnp.zeros_like(acc)

        r, xh = _rms_parts(x_ref[...])
        dh = a_ref[...]
        acc[...] += _colsum8(dh * xh)
        gx_ref[...] = dx1_ref[...] + _rms_bwd(xh, r, w_ref[...], dh)

        @pl.when(i == nt - 1)
        def _():
            gw_ref[...] = jnp.sum(acc[...], axis=0, keepdims=True)

    row = pl.BlockSpec((tm, D), lambda i: (i, 0))
    vec = pl.BlockSpec((1, D), lambda i: (0, 0))
    return pl.pallas_call(
        body,
        grid=(nt,),
        in_specs=[row, row, row, vec],
        out_specs=[row, vec],
        out_shape=[SDS((S, D), f32), SDS((1, D), f32)],
        scratch_shapes=[pltpu.VMEM((SUBLANE, D), f32)],
        compiler_params=_cparams(("arbitrary",)),
        name="first_bwd",
    )(x, dx1, dh, w_pre)


def _t5_bucket(dist):
    n = jnp.maximum(dist, 0)
    nf = jnp.maximum(n, 1).astype(f32)
    large = MAX_EXACT + (jnp.log(nf / MAX_EXACT) / math.log(MAX_DISTANCE / MAX_EXACT)
                         * (NUM_BUCKETS - MAX_EXACT)).astype(jnp.int32)
    large = jnp.minimum(large, NUM_BUCKETS - 1)
    return jnp.where(n < MAX_EXACT, n, large)


def _bias_consts(d):
    blk = ATTN_BLOCK
    rel = jnp.arange(blk)[:, None] + blk - jnp.arange(2 * blk)[None, :]
    in_win = (rel >= 0) & (rel <= blk)
    bucket = _t5_bucket(rel * d).reshape(1, -1)
    onehot = (bucket == jnp.arange(NUM_BUCKETS)[:, None]).astype(f32)
    return onehot, in_win.astype(f32).reshape(1, -1)


def _bias_build(tab_t, onehot, maskf, name):
    H = tab_t.shape[0]

    def body(t_ref, oh_ref, m_ref, o_ref):
        b = jnp.dot(t_ref[...], oh_ref[...], precision=HIGHEST, preferred_element_type=f32)
        o_ref[...] = jnp.where(m_ref[...] > 0.5, b, NEG_INF)

    return pl.pallas_call(body, out_shape=SDS((H, onehot.shape[1]), f32), name=name)(tab_t, onehot, maskf)


def _bias_grad(dbias_flat, onehot, name):
    H = dbias_flat.shape[0]

    def body(g_ref, oh_ref, o_ref):
        o_ref[...] = lax.dot_general(oh_ref[...], g_ref[...], NT, precision=HIGHEST, preferred_element_type=f32)

    return pl.pallas_call(body, out_shape=SDS((NUM_BUCKETS, H), f32), name=name)(dbias_flat, onehot)


ATTN_TILE = 512
ATTN_SUB = ATTN_TILE // ATTN_BLOCK


def _qkv_specs(nt):
    tile = (ATTN_TILE, LANE)
    blk = (ATTN_BLOCK, LANE)
    cur = lambda off: (lambda h, t: (jnp.minimum(t, nt - 1), off + h))
    prev = lambda off: (lambda h, t: (jnp.maximum(jnp.minimum(t, nt - 1) * ATTN_SUB - 1, 0), off + h))
    return [pl.BlockSpec(tile, cur(0)), pl.BlockSpec(blk, prev(4)), pl.BlockSpec(tile, cur(4)),
            pl.BlockSpec(blk, prev(8)), pl.BlockSpec(tile, cur(8))]


def _head_masks():
    lane = lax.broadcasted_iota(jnp.int32, (ATTN_BLOCK, LANE), 1)
    return lane < HEAD_DIM


def _stack_heads(x2, low):
    zero = jnp.zeros_like(x2)
    return jnp.concatenate([jnp.where(low, x2, zero), jnp.where(low, zero, x2)], axis=0)


def _attn_fwd(qkv, bias, bps, name, after=None):
    S = qkv.shape[0]
    nt = S // ATTN_TILE
    scale = HEAD_DIM ** -0.5

    def body(q_ref, kp_ref, kc_ref, vp_ref, vc_ref, b_ref, *rest):
        o_ref, l_ref = rest[-2:]
        t = pl.program_id(1)
        kk = jnp.concatenate([kp_ref[...], kc_ref[...]], axis=0)
        vv = jnp.concatenate([vp_ref[...], vc_ref[...]], axis=0)
        low = _head_masks()
        col = lax.broadcasted_iota(jnp.int32, (2 * ATTN_BLOCK, 2 * ATTN_BLOCK), 1)
        bias2 = b_ref[...].reshape(2 * ATTN_BLOCK, 2 * ATTN_BLOCK)
        for b in range(ATTN_SUB):
            lo = b * ATTN_BLOCK
            rows = slice(lo, lo + ATTN_BLOCK)
            keys = slice(lo, lo + 2 * ATTN_BLOCK)
            dead = jnp.logical_and((t * ATTN_SUB + b) % bps == 0, col < ATTN_BLOCK)
            q2 = _stack_heads(q_ref[rows, :], low)
            kb, vb = kk[keys], vv[keys]
            s = lax.dot_general(q2, kb, NT, preferred_element_type=f32) * scale + bias2
            s = jnp.where(dead, NEG_INF, s)
            m = jnp.max(s, axis=-1, keepdims=True)
            p = jnp.exp(s - m)
            l = jnp.sum(p, axis=-1, keepdims=True)
            o2 = jnp.dot(p.astype(bf16), vb, preferred_element_type=f32) / l
            lse = m + jnp.log(l)
            o_ref[rows, :] = jnp.where(low, o2[:ATTN_BLOCK], o2[ATTN_BLOCK:])
            l_ref[rows, :] = jnp.where(low, lse[:ATTN_BLOCK], lse[ATTN_BLOCK:])

    tile = pl.BlockSpec((ATTN_TILE, LANE), lambda h, t: (t, h))
    return pl.pallas_call(
        body,
        grid=(4, nt),
        in_specs=_qkv_specs(nt) + [pl.BlockSpec((2, ATTN_BLOCK, 2 * ATTN_BLOCK), lambda h, t: (h, 0, 0))]
        + ([] if after is None else [pl.BlockSpec(memory_space=pl.ANY)]),
        out_specs=[tile, tile],
        out_shape=[SDS((S, ATTN_OUT), f32), SDS((S, ATTN_OUT), f32)],
        compiler_params=_cparams(("parallel", "parallel")),
        name=name,
    )(qkv, qkv, qkv, qkv, qkv, bias, *([] if after is None else [after]))


def _attn_bwd(qkv, bias, do, dvec, lse, bps, name):
    S = qkv.shape[0]
    nt = S // ATTN_TILE
    scale = HEAD_DIM ** -0.5

    def assemble(parts):
        rows = [parts[0][:ATTN_BLOCK]]
        for b in range(ATTN_SUB - 1):
            rows.append(parts[b][ATTN_BLOCK:] + parts[b + 1][:ATTN_BLOCK])
        rows.append(parts[-1][ATTN_BLOCK:])
        return rows

    def body(q_ref, kp_ref, kc_ref, vp_ref, vc_ref, b_ref, do_ref, dvec_ref, lse_ref,
             dq_ref, dk_ref, dv_ref, db_ref, ck, cv):
        t = pl.program_id(1)
        last = ATTN_TILE - ATTN_BLOCK

        @pl.when(t == 0)
        def _():
            ck[...] = jnp.zeros_like(ck)
            cv[...] = jnp.zeros_like(cv)
            db_ref[...] = jnp.zeros_like(db_ref)

        @pl.when(t < nt)
        def _():
            kk = jnp.concatenate([kp_ref[...], kc_ref[...]], axis=0)
            vv = jnp.concatenate([vp_ref[...], vc_ref[...]], axis=0)
            low = _head_masks()
            col = lax.broadcasted_iota(jnp.int32, (2 * ATTN_BLOCK, 2 * ATTN_BLOCK), 1)
            bias2 = b_ref[...].reshape(2 * ATTN_BLOCK, 2 * ATTN_BLOCK)
            dk_parts, dv_parts = [], []
            dsum = None
            for b in range(ATTN_SUB):
                lo = b * ATTN_BLOCK
                rows = slice(lo, lo + ATTN_BLOCK)
                keys = slice(lo, lo + 2 * ATTN_BLOCK)
                dead = jnp.logical_and((t * ATTN_SUB + b) % bps == 0, col < ATTN_BLOCK)
                q2 = _stack_heads(q_ref[rows, :], low)
                do2 = _stack_heads(do_ref[rows, :].astype(bf16), low)
                kb, vb = kk[keys], vv[keys]
                dvec2 = dvec_ref[rows, :]
                lse2 = lse_ref[rows, :]
                per_row = lambda t2: jnp.concatenate([t2[:, 0:1], t2[:, HEAD_DIM:HEAD_DIM + 1]], axis=0)
                s = lax.dot_general(q2, kb, NT, preferred_element_type=f32) * scale + bias2
                s = jnp.where(dead, NEG_INF, s)
                p = jnp.exp(s - per_row(lse2))
                dp = lax.dot_general(do2, vb, NT, preferred_element_type=f32)
                ds = p * (dp - per_row(dvec2))
                dsum = ds if dsum is None else dsum + ds
                dsb = ds.astype(bf16)
                dq2 = jnp.dot(dsb, kb, preferred_element_type=f32) * scale
                dq_ref[rows, :] = jnp.where(low, dq2[:ATTN_BLOCK], dq2[ATTN_BLOCK:]).astype(bf16)
                dk_parts.append(lax.dot_general(dsb, q2, TN, preferred_element_type=f32) * scale)
                dv_parts.append(lax.dot_general(p.astype(bf16), do2, TN, preferred_element_type=f32))
            db_ref[...] += dsum.reshape(2, ATTN_BLOCK, 2 * ATTN_BLOCK)
            for parts, carry, out_ref in ((dk_parts, ck, dk_ref), (dv_parts, cv, dv_ref)):
                rws = assemble(parts)
                out_ref[:last, :] = carry[:last, :].astype(bf16)
                out_ref[last:, :] = (carry[last:, :] + rws[0]).astype(bf16)
                for b in range(ATTN_SUB):
                    carry[b * ATTN_BLOCK:(b + 1) * ATTN_BLOCK, :] = rws[b + 1]

        @pl.when(t == nt)
        def _():
            dk_ref[...] = ck[...].astype(bf16)
            dv_ref[...] = cv[...].astype(bf16)

    tile = (ATTN_TILE, LANE)
    cur = pl.BlockSpec(tile, lambda h, t: (jnp.minimum(t, nt - 1), h))
    lag = pl.BlockSpec(tile, lambda h, t: (jnp.maximum(t - 1, 0), h))
    bspec = pl.BlockSpec((2, ATTN_BLOCK, 2 * ATTN_BLOCK), lambda h, t: (h, 0, 0))
    return pl.pallas_call(
        body,
        grid=(4, nt + 1),
        in_specs=_qkv_specs(nt) + [bspec, cur, cur, cur],
        out_specs=[cur, lag, lag, bspec],
        out_shape=[SDS((S, ATTN_OUT), bf16), SDS((S, ATTN_OUT), bf16), SDS((S, ATTN_OUT), bf16),
                   SDS((8, ATTN_BLOCK, 2 * ATTN_BLOCK), f32)],
        scratch_shapes=[pltpu.VMEM(tile, f32), pltpu.VMEM(tile, f32)],
        compiler_params=_cparams(("parallel", "arbitrary")),
        name=name,
    )(qkv, qkv, qkv, qkv, qkv, bias, do, dvec, lse)


def _attn_merge(o0, o1, o2, l0, l1, l2):
    S, W = o0.shape
    R = PERM_ROWS

    def body(o0_ref, o1_ref, o2_ref, l0_ref, l1_ref, l2_ref, y_ref, yb_ref, w0_ref, w1_ref, w2_ref,
             so1, so2, sl1, sl2):
        _to_natural(o1_ref, so1, 4)
        _to_natural(l1_ref, sl1, 4)
        _to_natural(o2_ref, so2, 16)
        _to_natural(l2_ref, sl2, 16)
        a, b, c = l0_ref[...], sl1[...], sl2[...]
        m = jnp.maximum(jnp.maximum(a, b), c)
        ea, eb, ec = jnp.exp(a - m), jnp.exp(b - m), jnp.exp(c - m)
        den = (ea + eb) + ec
        w0, w1, w2 = ea / den, eb / den, ec / den
        y = (w0 * o0_ref[...] + w1 * so1[...]) + w2 * so2[...]
        y_ref[...] = y
        yb_ref[...] = y.astype(bf16)
        w0_ref[...] = w0
        w1_ref[...] = w1
        w2_ref[...] = w2

    nat = pl.BlockSpec((R, LANE), lambda i, j: (i, j))
    v4 = lambda t: t.reshape(4, S // 4, W)
    v16 = lambda t: t.reshape(16, S // 16, W)
    return pl.pallas_call(
        body,
        grid=(S // R, W // LANE),
        in_specs=[nat, _perm_spec(4), _perm_spec(16)] * 2,
        out_specs=[nat] * 5,
        out_shape=[SDS((S, W), f32), SDS((S, W), bf16)] + [SDS((S, W), f32)] * 3,
        scratch_shapes=[pltpu.VMEM((R, LANE), f32)] * 4,
        compiler_params=_cparams(("parallel", "parallel")),
        name="attn_merge",
    )(o0, v4(o1), v16(o2), l0, v4(l1), v16(l2))


def _attn_merge_bwd(dy, y, w0, w1, w2):
    S, W = dy.shape
    R = PERM_ROWS

    def body(dy_ref, y_ref, w0_ref, w1_ref, w2_ref, a0, a1, a2, b0, b1, b2, sa, sb):
        dyv = dy_ref[...]
        r = lax.broadcasted_iota(jnp.int32, (LANE, LANE), 0) // HEAD_DIM
        c = lax.broadcasted_iota(jnp.int32, (LANE, LANE), 1) // HEAD_DIM
        seg = jnp.where(r == c, 1.0, 0.0).astype(f32)
        cbar = jnp.dot(dyv * y_ref[...], seg, precision=HIGHEST, preferred_element_type=f32)
        w = w0_ref[...]
        a0[...] = (w * dyv).astype(bf16)
        b0[...] = w * cbar
        for d, w_ref, a_ref, b_ref in ((4, w1_ref, a1, b1), (16, w2_ref, a2, b2)):
            w = w_ref[...]
            sa[...] = w * dyv
            sb[...] = w * cbar
            n = R // d
            for k in range(d):
                rows = pl.ds(k, n, stride=d)
                a_ref[k] = sa[rows, :].astype(bf16)
                b_ref[k] = sb[rows, :]

    nat = pl.BlockSpec((R, LANE), lambda i, j: (i, j))
    shapes = lambda dt: [SDS((S, W), dt), SDS((4, S // 4, W), dt), SDS((16, S // 16, W), dt)]
    outs = pl.pallas_call(
        body,
        grid=(S // R, W // LANE),
        in_specs=[nat] * 5,
        out_specs=[nat, _perm_spec(4), _perm_spec(16)] * 2,
        out_shape=shapes(bf16) + shapes(f32),
        scratch_shapes=[pltpu.VMEM((R, LANE), f32)] * 2,
        compiler_params=_cparams(("parallel", "parallel")),
        name="attn_merge_bwd",
    )(dy, y, w0, w1, w2)
    return [t.reshape(S, W) for t in outs]


HGRN_SB = 256


def _chunk_masks():
    r = jnp.arange(HGRN_SB)[:, None]
    c = jnp.arange(HGRN_SB)[None, :]
    same = (r // HGRN_CHUNK) == (c // HGRN_CHUNK)
    return jnp.stack([same & (c <= r), same, same & (c >= r)]).astype(bf16)


def _mask_dot(mask, x):
    hi = x.astype(bf16)
    r1 = x - hi.astype(f32)
    mid = r1.astype(bf16)
    lo = (r1 - mid.astype(f32)).astype(bf16)
    p = jnp.dot(mask, jnp.concatenate([hi, mid, lo], axis=1), preferred_element_type=f32)
    n = x.shape[1]
    return (p[:, :n] + p[:, n:2 * n]) + p[:, 2 * n:]


def _hgrn_prep(q_raw, f_raw, lbv, tril, same):
    sq = _sigmoid(q_raw)
    qs = q_raw * sq
    sig = _sigmoid(f_raw)
    f = lbv + (1.0 - lbv) * sig
    g = jnp.log(f)
    k = 1.0 - f
    G = _mask_dot(tril, g)
    GL = _mask_dot(same, g)
    eG = jnp.exp(G)
    einv = jnp.exp(-G)
    edec = jnp.exp(GL - G)
    return dict(sq=sq, qs=qs, sig=sig, f=f, k=k, eG=eG, einv=einv, edec=edec, eGL=jnp.exp(GL),
                qt=qs * eG, kt=k * einv, kd=k * edec)


def _ride_split(ride, rest, n_out, n_scratch):
    if ride is None:
        return None, rest[:n_out], None, rest[n_out:], None
    return rest[0], rest[1:1 + n_out], rest[1 + n_out], rest[2 + n_out:2 + n_out + n_scratch], rest[2 + n_out + n_scratch:]


def _hgrn_fwd(hg, lb, normw, ride=None):
    S = hg.shape[0]
    sb = HGRN_SB
    nsb = S // sb
    nch = sb // HGRN_CHUNK

    def body(q_ref, f_ref, v_ref, og_ref, lb_ref, nw_ref, m_ref, *rest):
        src_ref, (y_ref, o_ref, ck_ref), got_ref, (st,), sems = _ride_split(ride, rest, 3, 1)
        j = pl.program_id(1)
        if ride is not None:
            @pl.when(jnp.logical_and(pl.program_id(0) == 0, j == 0))
            def _():
                _chip_start(src_ref, got_ref, sems[0], sems[1], ride[1])

        @pl.when(j == 0)
        def _():
            st[...] = jnp.zeros_like(st)

        ST = st[...]
        ck_ref[0, 0] = ST
        tril_m = m_ref[0]
        tril = tril_m.astype(f32) > 0.5
        pr = _hgrn_prep(q_ref[...], f_ref[...], lb_ref[...], tril_m, m_ref[1])
        qtb, ktb, kdb = pr["qt"].astype(bf16), pr["kt"].astype(bf16), pr["kd"].astype(bf16)
        eGL = pr["eGL"]
        vb = v_ref[...].astype(bf16)
        A = jnp.where(tril, lax.dot_general(qtb, ktb, NT, preferred_element_type=f32), 0.0)
        o = jnp.dot(A.astype(bf16), vb, preferred_element_type=f32)
        outs = []
        for ci in range(nch):
            lo = ci * HGRN_CHUNK
            sl = slice(lo, lo + HGRN_CHUNK)
            outs.append(o[sl] + lax.dot_general(qtb[sl], ST.astype(bf16), NT, preferred_element_type=f32))
            ST = ST * eGL[lo:lo + 1, :] + lax.dot_general(vb[sl], kdb[sl], TN, preferred_element_type=f32)
        st[...] = ST
        of = jnp.concatenate(outs, axis=0)
        o_ref[...] = of
        rms = lax.rsqrt(jnp.mean(of * of, axis=-1, keepdims=True) + EPS)
        ogv = og_ref[...]
        y_ref[...] = ((of * rms * nw_ref[...]) * (ogv * _sigmoid(ogv))).astype(bf16)

        if ride is not None:
            @pl.when(jnp.logical_and(pl.program_id(0) == 3, j == nsb - 1))
            def _():
                _chip_finish(src_ref, got_ref, sems[0], sems[1], ride[1])

    col = lambda off: pl.BlockSpec((sb, LANE), lambda h, j: (j, off + h))
    riding = ride is not None
    res = pl.pallas_call(
        body,
        grid=(4, nsb),
        in_specs=[col(0), col(4), col(8), col(12), pl.BlockSpec((1, LANE), lambda h, j: (0, h)),
                  pl.BlockSpec((1, LANE), lambda h, j: (0, 0)),
                  pl.BlockSpec((3, sb, sb), lambda h, j: (0, 0, 0))] + ([_ANY] if riding else []),
        out_specs=[col(0), col(0), pl.BlockSpec((1, 1, LANE, LANE), lambda h, j: (h, j, 0, 0))]
        + ([_ANY] if riding else []),
        out_shape=[SDS((S, HGRN_W), bf16), SDS((S, HGRN_W), f32), SDS((4, nsb, LANE, LANE), f32)]
        + ([_chip_out_shape(*ride)] if riding else []),
        scratch_shapes=[pltpu.VMEM((LANE, LANE), f32)] + (list(_CHIP_SEMS) if riding else []),
        compiler_params=_cparams(("arbitrary", "arbitrary") if riding else ("parallel", "arbitrary")),
        name="hgrn_fwd",
    )(hg, hg, hg, hg, lb, normw, _chunk_masks(), *([ride[0]] if riding else []))
    return tuple(res) if riding else (*res, None)


def _hgrn_bwd(hg, o_raw, dy, ck, lb, normw, ride=None):
    S = hg.shape[0]
    sb = HGRN_SB
    nsb = S // sb
    nch = sb // HGRN_CHUNK

    def body(q_ref, f_ref, v_ref, og_ref, o_ref, dy_ref, ck_ref, lb_ref, nw_ref, m_ref, *rest):
        src_ref, outs, got_ref, (dst, alb, anw), sems = _ride_split(ride, rest, 6, 3)
        dq_ref, df_ref, dv_ref, dog_ref, glb_ref, gnw_ref = outs
        j = pl.program_id(1)
        if ride is not None:
            @pl.when(jnp.logical_and(pl.program_id(0) == 0, j == 0))
            def _():
                _chip_start(src_ref, got_ref, sems[0], sems[1], ride[1])

        @pl.when(j == 0)
        def _():
            dst[...] = jnp.zeros_like(dst)
            alb[...] = jnp.zeros_like(alb)
            anw[...] = jnp.zeros_like(anw)

        tril_m = m_ref[0]
        tril = tril_m.astype(f32) > 0.5
        lbv = lb_ref[...]
        q_raw = q_ref[...]
        pr = _hgrn_prep(q_raw, f_ref[...], lbv, tril_m, m_ref[1])
        qt, kt, kd, eGL = pr["qt"], pr["kt"], pr["kd"], pr["eGL"]
        qtb, ktb, kdb = qt.astype(bf16), kt.astype(bf16), kd.astype(bf16)
        vb = v_ref[...].astype(bf16)

        o = o_ref[...]
        ogv = og_ref[...]
        sog = _sigmoid(ogv)
        rms = lax.rsqrt(jnp.mean(o * o, axis=-1, keepdims=True) + EPS)
        oh = o * rms
        nw = nw_ref[...]
        dyv = dy_ref[...]
        dog_ref[...] = (dyv * (oh * nw) * (sog * (1.0 + ogv * (1.0 - sog)))).astype(bf16)
        dohw = dyv * (ogv * sog)
        anw[...] += _colsum8(dohw * oh)
        doh = dohw * nw
        do = rms * (doh - oh * jnp.mean(doh * oh, axis=-1, keepdims=True))
        dob = do.astype(bf16)

        Ab = jnp.where(tril, lax.dot_general(qtb, ktb, NT, preferred_element_type=f32), 0.0).astype(bf16)
        dAb = jnp.where(tril, lax.dot_general(dob, vb, NT, preferred_element_type=f32), 0.0).astype(bf16)
        dv_acc = lax.dot_general(Ab, dob, TN, preferred_element_type=f32)
        dqt = jnp.dot(dAb, ktb, preferred_element_type=f32)
        dkt = lax.dot_general(dAb, qtb, TN, preferred_element_type=f32)

        ST = ck_ref[0, 0]
        states = []
        for ci in range(nch):
            lo = ci * HGRN_CHUNK
            sl = slice(lo, lo + HGRN_CHUNK)
            states.append(ST)
            ST = ST * eGL[lo:lo + 1, :] + lax.dot_general(vb[sl], kdb[sl], TN, preferred_element_type=f32)

        dST = dst[...]
        dqt_i, dkd_i, dv_i, deg_i = [None] * nch, [None] * nch, [None] * nch, [None] * nch
        for ci in reversed(range(nch)):
            lo = ci * HGRN_CHUNK
            sl = slice(lo, lo + HGRN_CHUNK)
            ST0 = states[ci]
            dSTb = dST.astype(bf16)
            dv_i[ci] = lax.dot_general(kdb[sl], dSTb, NT, preferred_element_type=f32)
            dqt_i[ci] = jnp.dot(dob[sl], ST0.astype(bf16), preferred_element_type=f32)
            dkd_i[ci] = jnp.dot(vb[sl], dSTb, preferred_element_type=f32)
            deg_i[ci] = jnp.broadcast_to(jnp.sum(dST * ST0, axis=0, keepdims=True), (HGRN_CHUNK, LANE))
            dST = dST * eGL[lo:lo + 1, :] + lax.dot_general(dob[sl], qtb[sl], TN, preferred_element_type=f32)
        dst[...] = dST

        dqt = dqt + jnp.concatenate(dqt_i, axis=0)
        dkd = jnp.concatenate(dkd_i, axis=0)
        dv_ref[...] = (dv_acc + jnp.concatenate(dv_i, axis=0)).astype(bf16)
        deg = jnp.concatenate(deg_i, axis=0)

        dqs = dqt * pr["eG"]
        dkdkd = dkd * kd
        dG = dqt * qt - dkt * kt - dkdkd
        dk = dkt * pr["einv"] + dkd * pr["edec"]
        dGL = _mask_dot(m_ref[1], dkdkd) + eGL * deg
        dg = _mask_dot(m_ref[2], dG) + dGL
        df = dg / pr["f"] - dk
        sig = pr["sig"]
        df_ref[...] = (df * (1.0 - lbv) * (sig * (1.0 - sig))).astype(bf16)
        alb[...] += _colsum8(df * (1.0 - sig))
        sq = pr["sq"]
        dq_ref[...] = (dqs * (sq * (1.0 + q_raw * (1.0 - sq)))).astype(bf16)

        @pl.when(j == nsb - 1)
        def _():
            glb_ref[...] = jnp.broadcast_to(jnp.sum(alb[...], axis=0, keepdims=True), (SUBLANE, LANE))
            gnw_ref[...] = jnp.broadcast_to(jnp.sum(anw[...], axis=0, keepdims=True), (SUBLANE, LANE))

        if ride is not None:
            @pl.when(jnp.logical_and(pl.program_id(0) == 3, j == nsb - 1))
            def _():
                _chip_finish(src_ref, got_ref, sems[0], sems[1], ride[1])

    rev = lambda off: pl.BlockSpec((sb, LANE), lambda h, j: (nsb - 1 - j, off + h))
    stat = pl.BlockSpec((SUBLANE, LANE), lambda h, j: (0, h))
    riding = ride is not None
    res = pl.pallas_call(
        body,
        grid=(4, nsb),
        in_specs=[rev(0), rev(4), rev(8), rev(12), rev(0), rev(0),
                  pl.BlockSpec((1, 1, LANE, LANE), lambda h, j: (h, nsb - 1 - j, 0, 0)),
                  pl.BlockSpec((1, LANE), lambda h, j: (0, h)), pl.BlockSpec((1, LANE), lambda h, j: (0, 0)),
                  pl.BlockSpec((3, sb, sb), lambda h, j: (0, 0, 0))]
        + ([_ANY] if riding else []),
        out_specs=[rev(0), rev(0), rev(0), rev(0), stat, stat] + ([_ANY] if riding else []),
        out_shape=[SDS((S, HGRN_W), bf16)] * 4 + [SDS((SUBLANE, HGRN_W), f32)] * 2
        + ([_chip_out_shape(*ride)] if riding else []),
        scratch_shapes=[pltpu.VMEM((LANE, LANE), f32), pltpu.VMEM((SUBLANE, LANE), f32),
                        pltpu.VMEM((SUBLANE, LANE), f32)] + (list(_CHIP_SEMS) if riding else []),
        compiler_params=_cparams(("arbitrary", "arbitrary") if riding else ("parallel", "arbitrary")),
        name="hgrn_bwd",
    )(hg, hg, hg, hg, o_raw, dy, ck, lb, normw, _chunk_masks(), *([ride[0]] if riding else []))
    return tuple(res) if riding else (*res, None)


def _lb_fwd(raw):
    def body(r_ref, o_ref):
        r = r_ref[...]
        m = jnp.max(r, axis=0, keepdims=True)
        e = jnp.exp(r - m)
        o_ref[...] = (e / jnp.sum(e, axis=0, keepdims=True))[0:1]

    return pl.pallas_call(body, out_shape=SDS((1, raw.shape[1]), f32), name="lb_fwd")(raw)


def _lb_bwd(raw, dlb):
    def body(r_ref, d_ref, o_ref):
        r = r_ref[...]
        m = jnp.max(r, axis=0, keepdims=True)
        e = jnp.exp(r - m)
        s = e / jnp.sum(e, axis=0, keepdims=True)
        s0 = s[0:1]
        onehot0 = jnp.where(lax.broadcasted_iota(jnp.int32, r.shape, 0) == 0, 1.0, 0.0)
        o_ref[...] = d_ref[...] * s0 * (onehot0 - s)

    return pl.pallas_call(body, out_shape=SDS(raw.shape, f32), name="lb_bwd")(raw, dlb)


def _gate_fwd(a, b, gc):
    S, D = a.shape
    tm = _pick(S, 512)

    def body(a_ref, b_ref, g0_ref, g1_ref, o_ref):
        s0, s1 = _sigmoid(g0_ref[...].astype(f32)), _sigmoid(g1_ref[...].astype(f32))
        o_ref[...] = (s0 * a_ref[...].astype(f32) + s1 * b_ref[...].astype(f32)).astype(bf16)

    row = pl.BlockSpec((tm, D), lambda i: (i, 0))
    return pl.pallas_call(
        body,
        grid=(S // tm,),
        in_specs=[row, row, row, pl.BlockSpec((tm, D), lambda i: (i, 1))],
        out_specs=row,
        out_shape=SDS((S, D), bf16),
        compiler_params=_cparams(("parallel",)),
        name="gate_fwd",
    )(a, b, gc, gc)


def _gate_bwd(dm, a, b, gc):
    S, D = a.shape
    tm = _pick(S, 512)

    def body(dm_ref, a_ref, b_ref, g0_ref, g1_ref, da_ref, db_ref, dg_ref):
        dmv = dm_ref[...].astype(f32)
        s0, s1 = _sigmoid(g0_ref[...].astype(f32)), _sigmoid(g1_ref[...].astype(f32))
        da_ref[...] = (dmv * s0).astype(bf16)
        db_ref[...] = (dmv * s1).astype(bf16)
        dg_ref[:, :D] = (dmv * a_ref[...].astype(f32) * (s0 * (1.0 - s0))).astype(bf16)
        dg_ref[:, D:] = (dmv * b_ref[...].astype(f32) * (s1 * (1.0 - s1))).astype(bf16)

    row = pl.BlockSpec((tm, D), lambda i: (i, 0))
    wide = pl.BlockSpec((tm, 2 * D), lambda i: (i, 0))
    return pl.pallas_call(
        body,
        grid=(S // tm,),
        in_specs=[row, row, row, row, pl.BlockSpec((tm, D), lambda i: (i, 1))],
        out_specs=[row, row, wide],
        out_shape=[SDS((S, D), bf16), SDS((S, D), bf16), SDS((S, 2 * D), bf16)],
        compiler_params=_cparams(("parallel",)),
        name="gate_bwd",
    )(dm, a, b, gc, gc)


CONV_ROWS = 512
INV_SQRT2 = 0.7071067811865476
INV_SQRT_2PI = 0.3989422804014327


CONV_HALO = 16


def _tile8(a, rows):
    return jnp.tile(a, (rows // a.shape[0], 1))


def _conv_rows(u_ref, w, b, r0, first):
    R = CONV_ROWS
    cur = u_ref[pl.ds(r0, R), :].astype(f32)
    prev8 = u_ref[pl.ds(pl.multiple_of(jnp.maximum(r0 - CONV_HALO, 0), CONV_HALO), CONV_HALO), :].astype(f32)
    prev8 = jnp.where(first, 0.0, prev8)
    row = lax.broadcasted_iota(jnp.int32, (R, LANE), 0)
    x1 = jnp.where(row < 1, _tile8(pltpu.roll(prev8, 1, 0), R), pltpu.roll(cur, 1, 0))
    x2 = jnp.where(row < 2, _tile8(pltpu.roll(prev8, 2, 0), R), pltpu.roll(cur, 2, 0))
    c = ((b + w[0:1] * x2) + w[1:2] * x1) + w[2:3] * cur
    return c, x2, x1, cur


def _conv_fwd(ug, uv, wg, wv, bg, bv):
    S, F = ug.shape
    nchunk = S // CONV_ROWS

    def body(ug_ref, uv_ref, wg_ref, wv_ref, bg_ref, bv_ref, o_ref):
        wgv, wvv, bgv, bvv = wg_ref[...], wv_ref[...], bg_ref[...], bv_ref[...]

        def step(ci, carry):
            r0 = pl.multiple_of(ci * CONV_ROWS, CONV_ROWS)
            cg = _conv_rows(ug_ref, wgv, bgv, r0, ci == 0)[0]
            cv = _conv_rows(uv_ref, wvv, bvv, r0, ci == 0)[0]
            gelu = 0.5 * cg * (1.0 + lax.erf(cg * INV_SQRT2))
            o_ref[pl.ds(r0, CONV_ROWS), :] = (gelu * cv).astype(bf16)
            return carry

        lax.fori_loop(0, nchunk, step, 0)

    col = pl.BlockSpec((S, LANE), lambda j: (0, j))
    w3 = pl.BlockSpec((3, LANE), lambda j: (0, j))
    b1 = pl.BlockSpec((1, LANE), lambda j: (0, j))
    return pl.pallas_call(
        body,
        grid=(F // LANE,),
        in_specs=[col, col, w3, w3, b1, b1],
        out_specs=col,
        out_shape=SDS((S, F), bf16),
        compiler_params=_cparams(("parallel",), VMEM_BIG),
        name="conv_fwd",
    )(ug, uv, wg, wv, bg, bv)


def _conv_bwd(ug, uv, dact, wg, wv, bg, bv):
    S, F = ug.shape
    R = CONV_ROWS
    nchunk = S // R

    def body(ug_ref, uv_ref, da_ref, wg_ref, wv_ref, bg_ref, bv_ref, dug_ref, duv_ref, sg_ref, sv_ref, dcg, dcv):
        wgv, wvv, bgv, bvv = wg_ref[...], wv_ref[...], bg_ref[...], bv_ref[...]
        zero = jnp.zeros((SUBLANE, LANE), f32)

        def fwd_step(ci, acc):
            r0 = pl.multiple_of(ci * R, R)
            cg, g2, g1, g0 = _conv_rows(ug_ref, wgv, bgv, r0, ci == 0)
            cv, v2, v1, v0 = _conv_rows(uv_ref, wvv, bvv, r0, ci == 0)
            da = da_ref[pl.ds(r0, R), :].astype(f32)
            cdf = 0.5 * (1.0 + lax.erf(cg * INV_SQRT2))
            pdf = INV_SQRT_2PI * jnp.exp(-0.5 * cg * cg)
            dg = da * cv * (cdf + cg * pdf)
            dv = da * (cg * cdf)
            dcg[pl.ds(r0, R), :] = dg
            dcv[pl.ds(r0, R), :] = dv
            new = (acc[0] + _colsum8(dg * g2), acc[1] + _colsum8(dg * g1), acc[2] + _colsum8(dg * g0),
                   acc[3] + _colsum8(dg),
                   acc[4] + _colsum8(dv * v2), acc[5] + _colsum8(dv * v1), acc[6] + _colsum8(dv * v0),
                   acc[7] + _colsum8(dv))
            return new

        acc = lax.fori_loop(0, nchunk, fwd_step, (zero,) * 8)
        rows = lax.broadcasted_iota(jnp.int32, (SUBLANE, LANE), 0)

        def stats(parts):
            out = jnp.zeros((SUBLANE, LANE), f32)
            for k, pt in enumerate(parts):
                out = jnp.where(rows == k, jnp.sum(pt, axis=0, keepdims=True), out)
            return out

        sg_ref[...] = stats(acc[0:4])
        sv_ref[...] = stats(acc[4:8])

        def du_rows(dc, w, r0, last):
            cur = dc[pl.ds(r0, R), :]
            nxt = dc[pl.ds(pl.multiple_of(jnp.minimum(r0 + R, S - SUBLANE), SUBLANE), SUBLANE), :]
            nxt = jnp.where(last, 0.0, nxt)
            row = lax.broadcasted_iota(jnp.int32, (R, LANE), 0)
            y1 = jnp.where(row >= R - 1, _tile8(pltpu.roll(nxt, SUBLANE - 1, 0), R), pltpu.roll(cur, R - 1, 0))
            y2 = jnp.where(row >= R - 2, _tile8(pltpu.roll(nxt, SUBLANE - 2, 0), R), pltpu.roll(cur, R - 2, 0))
            return w[2:3] * cur + w[1:2] * y1 + w[0:1] * y2

        def bwd_step(ci, carry):
            r0 = pl.multiple_of(ci * R, R)
            last = ci == nchunk - 1
            dug_ref[pl.ds(r0, R), :] = du_rows(dcg, wgv, r0, last).astype(bf16)
            duv_ref[pl.ds(r0, R), :] = du_rows(dcv, wvv, r0, last).astype(bf16)
            return carry

        lax.fori_loop(0, nchunk, bwd_step, 0)

    col = pl.BlockSpec((S, LANE), lambda j: (0, j))
    w3 = pl.BlockSpec((3, LANE), lambda j: (0, j))
    b1 = pl.BlockSpec((1, LANE), lambda j: (0, j))
    st = pl.BlockSpec((SUBLANE, LANE), lambda j: (0, j))
    return pl.pallas_call(
        body,
        grid=(F // LANE,),
        in_specs=[col, col, col, w3, w3, b1, b1],
        out_specs=[col, col, st, st],
        out_shape=[SDS((S, F), bf16), SDS((S, F), bf16), SDS((SUBLANE, F), f32), SDS((SUBLANE, F), f32)],
        scratch_shapes=[pltpu.VMEM((S, LANE), f32), pltpu.VMEM((S, LANE), f32)],
        compiler_params=_cparams(("parallel",), VMEM_BIG),
        name="conv_bwd",
    )(ug, uv, dact, wg, wv, bg, bv)


def _adam_math(w, g, m, v):
    m = ADAM_B1 * m + (1.0 - ADAM_B1) * g
    v = ADAM_B2 * v + (1.0 - ADAM_B2) * (g * g)
    m_hat = m / (1.0 - ADAM_B1 ** ADAM_STEP)
    v_hat = v / (1.0 - ADAM_B2 ** ADAM_STEP)
    delta = -ADAM_LR * (m_hat / (jnp.sqrt(v_hat) + ADAM_EPS) + ADAM_WD * w)
    return delta, m, v


def _adamw(w, m, v, g, name):
    R, C = w.shape
    parts = g.ndim == 3
    tr = R
    for t in (256, 128, 64, 32, 16):
        if R % t == 0 and R > t:
            tr = t
            break

    def body(w_ref, m_ref, v_ref, g_ref, go_ref, d_ref, mo_ref, vo_ref):
        if parts:
            gv = ((g_ref[0].astype(f32) + g_ref[1].astype(f32)) + g_ref[2].astype(f32)) + g_ref[3].astype(f32)
        else:
            gv = g_ref[...]
        go_ref[...] = gv
        d, mn, vn = _adam_math(w_ref[...], gv, m_ref[...], v_ref[...])
        d_ref[...] = d
        mo_ref[...] = mn
        vo_ref[...] = vn

    row = pl.BlockSpec((tr, C), lambda i: (i, 0))
    gspec = pl.BlockSpec((4, tr, C), lambda i: (0, i, 0)) if parts else row
    return pl.pallas_call(
        body,
        grid=(R // tr,),
        in_specs=[row, row, row, gspec],
        out_specs=[row] * 4,
        out_shape=[SDS((R, C), f32)] * 4,
        compiler_params=_cparams(("parallel",)),
        name=name,
    )(w, m, v, g)


def _sum8(parts, name):
    _, _, R, C = parts.shape

    def body(p_ref, o_ref):
        acc = p_ref[0, 0]
        for c in range(2):
            for k in range(4):
                if c or k:
                    acc = acc + p_ref[c, k]
        o_ref[...] = acc

    return pl.pallas_call(body, out_shape=SDS((R, C), f32), name=name)(parts)


def _pair_add(by_core, b, name):
    _, K, R, C = by_core.shape
    tr = R // 2 if R % 32 == 0 else R

    def body(c_ref, a_ref, b_ref, o_ref):
        o_ref[...] = (a_ref[0].astype(f32) + b_ref[...].astype(f32)).astype(bf16)

    blk = pl.BlockSpec((1, tr, C), lambda k, i, c: (k, i, 0))
    return pl.pallas_call(
        body,
        grid_spec=pltpu.PrefetchScalarGridSpec(
            num_scalar_prefetch=1,
            grid=(K, R // tr),
            in_specs=[pl.BlockSpec((1, 1, tr, C), lambda k, i, c: (c[0], k, i, 0)), blk],
            out_specs=blk,
        ),
        out_shape=SDS((K, R, C), bf16),
        compiler_params=_cparams(("parallel", "parallel")),
        name=name,
    )(lax.axis_index("c").astype(jnp.int32).reshape(1), by_core, b)


_ANY = pl.BlockSpec(memory_space=pl.ANY)


def _chip_copies(src_ref, out_ref, send_sems, recv_sems, gather):
    x, y, c = lax.axis_index("x"), lax.axis_index("y"), lax.axis_index("c")
    mine = 2 * x + y

    def piece(k):
        return src_ref if gather else src_ref.at[k]

    sends, recvs = [], []
    for j, (px, py) in enumerate([(1 - x, y), (x, 1 - y), (1 - x, 1 - y)]):
        sends.append(pltpu.make_async_remote_copy(
            src_ref=piece(2 * px + py), dst_ref=out_ref.at[mine], send_sem=send_sems.at[j],
            recv_sem=recv_sems.at[j], device_id=(px, py, c), device_id_type=MESH))
        recvs.append(pltpu.make_async_remote_copy(
            src_ref=piece(mine), dst_ref=out_ref.at[2 * px + py], send_sem=send_sems.at[j],
            recv_sem=recv_sems.at[j], device_id=(px, py, c), device_id_type=MESH))
    return sends, recvs


def _chip_start(src_ref, out_ref, send_sems, recv_sems, gather):
    for cp in _chip_copies(src_ref, out_ref, send_sems, recv_sems, gather)[0]:
        cp.start()


def _chip_finish(src_ref, out_ref, send_sems, recv_sems, gather):
    sends, recvs = _chip_copies(src_ref, out_ref, send_sems, recv_sems, gather)
    for cp in recvs:
        cp.wait_recv()
    for cp in sends:
        cp.wait_send()


def _chip_out_shape(src, gather):
    return SDS((4,) + tuple(src.shape if gather else src.shape[1:]), src.dtype)


_CHIP_SEMS = [pltpu.SemaphoreType.DMA((3,)), pltpu.SemaphoreType.DMA((3,))]


def _fill_own(out, src, gather):
    mine = 2 * lax.axis_index("x") + lax.axis_index("y")
    own = src if gather else lax.dynamic_index_in_dim(src, mine, axis=0, keepdims=False)
    return lax.dynamic_update_index_in_dim(out, own, mine, axis=0)


def _chip_comm(src, gather, name):
    def body(src_ref, out_ref, send_sems, recv_sems):
        _chip_start(src_ref, out_ref, send_sems, recv_sems, gather)
        _chip_finish(src_ref, out_ref, send_sems, recv_sems, gather)

    out = pl.pallas_call(
        body,
        in_specs=[_ANY],
        out_specs=_ANY,
        out_shape=_chip_out_shape(src, gather),
        scratch_shapes=list(_CHIP_SEMS),
        name=name,
    )(src)
    return _fill_own(out, src, gather)


_HBM = pl.BlockSpec(memory_space=pltpu.HBM)
_SEM = pl.BlockSpec(memory_space=pltpu.SEMAPHORE)
_EFFECT = pltpu.SideEffectType.DATAFLOW_SIDE_EFFECTING
_SPLIT_PEERS = {"chip_gather": 3, "chip_xchg": 3, "core_gather": 1, "core_swap": 1}


def _split_land(src, kind):
    if kind == "core_gather":
        return SDS((2,) + tuple(src.shape), src.dtype)
    if kind == "core_swap":
        return SDS(tuple(src.shape[1:]), src.dtype)
    return _chip_out_shape(src, kind == "chip_gather")


def _split_copies(src_ref, land_ref, sems, kind):
    x, y, c = lax.axis_index("x"), lax.axis_index("y"), lax.axis_index("c")
    n = _SPLIT_PEERS[kind]
    if kind == "core_gather":
        routes = [((x, y, 1 - c), src_ref, land_ref.at[c], land_ref.at[1 - c])]
    elif kind == "core_swap":
        routes = [((x, y, 1 - c), src_ref.at[1 - c], land_ref, land_ref)]
    else:
        mine = 2 * x + y
        gather = kind == "chip_gather"
        routes = [((px, py, c), src_ref if gather else src_ref.at[2 * px + py], land_ref.at[mine],
                   land_ref.at[2 * px + py]) for px, py in [(1 - x, y), (x, 1 - y), (1 - x, 1 - y)]]
    sends, recvs = [], []
    for j, (peer, piece, there, here) in enumerate(routes):
        sends.append(pltpu.make_async_remote_copy(src_ref=piece, dst_ref=there, send_sem=sems[j],
                                                  recv_sem=sems[n + j], device_id=peer, device_id_type=MESH))
        recvs.append(pltpu.make_async_remote_copy(src_ref=piece, dst_ref=here, send_sem=sems[j],
                                                  recv_sem=sems[n + j], device_id=peer, device_id_type=MESH))
    return sends, recvs


def _split_start(src, kind, name, after=None):
    land = _split_land(src, kind)
    ns = 2 * _SPLIT_PEERS[kind]
    n_in = 2 if after is None else 3

    def body(*refs):
        src_ref, land_ref = refs[:2]
        outs = refs[n_in:]
        for cp in _split_copies(src_ref, land_ref, outs[:ns], kind)[0]:
            cp.start()
        token = outs[ns + 2]
        token[...] = jnp.zeros_like(token)

    res = pl.pallas_call(
        body,
        name=name,
        out_shape=(pltpu.SemaphoreType.DMA(()),) * ns
        + (pltpu.HBM(src.shape, src.dtype), pltpu.HBM(land.shape, land.dtype), SDS((SUBLANE, LANE), f32)),
        in_specs=(_HBM, _HBM) + (() if after is None else (_ANY,)),
        out_specs=(_SEM,) * ns + (_HBM, _HBM, pl.BlockSpec(memory_space=pltpu.VMEM)),
        input_output_aliases={0: ns, 1: ns + 1},
        compiler_params=pltpu.CompilerParams(has_side_effects=_EFFECT),
    )(pltpu.with_memory_space_constraint(src, pltpu.HBM),
      pltpu.with_memory_space_constraint(lax.empty(land.shape, land.dtype), pltpu.HBM),
      *(() if after is None else (after,)))
    return (res[:ns], res[ns], res[ns + 1]), res[ns + 2]


def _split_wait(state, after, kind, name):
    sems, src_thru, land_thru = state
    ns = 2 * _SPLIT_PEERS[kind]

    def body(src_ref, land_ref, *rest):
        sends, recvs = _split_copies(src_ref, land_ref, rest[:ns], kind)
        for cp in recvs:
            cp.wait_recv()
        for cp in sends:
            cp.wait_send()

    src_out, got = pl.pallas_call(
        body,
        name=name,
        out_shape=(pltpu.HBM(src_thru.shape, src_thru.dtype), pltpu.HBM(land_thru.shape, land_thru.dtype)),
        in_specs=(_HBM, _HBM) + (_SEM,) * ns + (_ANY,),
        out_specs=(_HBM, _HBM),
        input_output_aliases={0: 0, 1: 1},
        compiler_params=pltpu.CompilerParams(has_side_effects=_EFFECT),
    )(src_thru, land_thru, *sems, after)
    if kind == "core_swap":
        return got, src_out
    if kind == "core_gather":
        return lax.dynamic_update_index_in_dim(got, src_out, lax.axis_index("c"), axis=0)
    return _fill_own(got, src_out, kind == "chip_gather")


def _core_gather(src, name):
    def body(src_ref, out_ref, send_sem, recv_sem):
        x, y, c = lax.axis_index("x"), lax.axis_index("y"), lax.axis_index("c")
        cp = pltpu.make_async_remote_copy(src_ref=src_ref, dst_ref=out_ref.at[c], send_sem=send_sem,
                                          recv_sem=recv_sem, device_id=(x, y, 1 - c), device_id_type=MESH)
        cp.start()
        pltpu.make_async_remote_copy(src_ref=src_ref, dst_ref=out_ref.at[1 - c], send_sem=send_sem,
                                     recv_sem=recv_sem, device_id=(x, y, 1 - c), device_id_type=MESH).wait_recv()
        cp.wait_send()

    out = pl.pallas_call(
        body,
        in_specs=[_ANY],
        out_specs=_ANY,
        out_shape=SDS((2,) + tuple(src.shape), src.dtype),
        scratch_shapes=[pltpu.SemaphoreType.DMA, pltpu.SemaphoreType.DMA],
        name=name,
    )(src)
    return lax.dynamic_update_index_in_dim(out, src, lax.axis_index("c"), axis=0)


def _core_swap(src, name):
    def body(src_ref, out_ref, send_sem, recv_sem):
        x, y, c = lax.axis_index("x"), lax.axis_index("y"), lax.axis_index("c")
        cp = pltpu.make_async_remote_copy(src_ref=src_ref.at[1 - c], dst_ref=out_ref, send_sem=send_sem,
                                          recv_sem=recv_sem, device_id=(x, y, 1 - c), device_id_type=MESH)
        cp.start()
        cp.wait()

    return pl.pallas_call(
        body,
        in_specs=[_ANY],
        out_specs=_ANY,
        out_shape=SDS(tuple(src.shape[1:]), src.dtype),
        scratch_shapes=[pltpu.SemaphoreType.DMA, pltpu.SemaphoreType.DMA],
        name=name,
    )(src)


_PACK_A = (("w_in", (1088, 1024)),)
_PACK_B = (("w_ba", (512, 128)), ("w_bh", (512, 128)), ("w_out", (128, 1024)), ("w_up", (704, 1024)),
           ("w_down", (352, 1024)))
_PACK_SIZES = _PACK_A + _PACK_B
_TRANSPOSED = ("w_in", "w_up")


def _slab_rows(sizes):
    return sum(r * c for _, (r, c) in sizes) // D_MODEL


def _pack_rows(d, sizes):
    n = d[sizes[0][0]].shape[0]
    return jnp.concatenate([d[k].reshape(n, -1, D_MODEL) for k, _ in sizes], axis=1)


def _unpack_rows(slab, sizes):
    n = slab.shape[0]
    out, lo = {}, 0
    for key, (r, c) in sizes:
        rows = r * c // D_MODEL
        out[key] = slab[:, lo:lo + rows].reshape(n, r, c)
        lo += rows
    return out


def _by_core(gslab):
    return jnp.swapaxes(gslab.reshape((4, 2) + gslab.shape[1:]), 0, 1)


def _pair_sum(by_core, tag):
    return _pair_add(by_core, _core_swap(by_core, tag + "_cores"), tag + "_pair_add")


def _cols_to_full(t):
    return jnp.swapaxes(t, 0, 1).reshape(t.shape[1], -1)


def _full_to_cols(t):
    K = t.shape[0]
    return jnp.swapaxes(t.reshape(K, 8, -1), 0, 1)


_SMALL = (("pre_mix_norm", (1, 1024)), ("rel_bias", (32, 24)), ("hgrn_lb_raw", (2, 512)), ("hgrn_norm", (1, 128)),
          ("post_mix_norm", (1, 1024)), ("pre_ffn_norm", (1, 1024)), ("conv_b", (1, 5632)),
          ("post_ffn_norm", (1, 1024)))
_SMALL_ROWS = 96
_CONVW_ROWS = 136


_SMALL_USED = sum(r * c for _, (r, c) in _SMALL)


def _pack_small(d, extra=None):
    flat = jnp.concatenate([d[k].reshape(-1) for k, _ in _SMALL] + ([] if extra is None else [extra.reshape(-1)]))
    flat = jnp.pad(flat, (0, _SMALL_ROWS * LANE - flat.shape[0]))
    return flat.reshape(_SMALL_ROWS, LANE)


def _unpack_small(p):
    flat = p.reshape(-1)
    out, lo = {}, 0
    for k, shp in _SMALL:
        n = shp[0] * shp[1]
        out[k] = flat[lo:lo + n].reshape(shp)
        lo += n
    return out


def _local_step(x, tgt, P, plan):
    S = x.shape[0]
    P = dict(P)
    lb = _lb_fwd(P["hgrn_lb_raw"])
    hs = _prep(x, P["pre_mix_norm"], plan.start_token())
    h1 = hs[0]
    W = dict(plan.weights_a(h1))
    consts = [_bias_consts(d) for d in DILATIONS]
    qkv = [_mm(hs[g], W["wt_qkv"][g], "nt", bf16, f"proj_qkv{g}") for g in range(N_GROUPS)]
    hg = _mm(h1, W["wt_hg"], "nt", f32, "proj_hg")
    gc = _mm(h1, W["wt_gate"], "nt", bf16, "proj_gate")
    token = plan.forward_b(gc)
    obuf, lbuf, biases = [], [], []
    for g, d in enumerate(DILATIONS):
        tab_t = P["rel_bias"][:, 8 * g:8 * g + 8].T
        bias_g = _bias_build(tab_t, consts[g][0], consts[g][1], f"bias_build{g}").reshape(8, ATTN_BLOCK, 2 * ATTN_BLOCK)
        o_g, l_g = _attn_fwd(qkv[g], bias_g, (S // d) // ATTN_BLOCK, f"attn_fwd{g}", after=token)
        biases.append(bias_g)
        lbuf.append(l_g)
        obuf.append(o_g)
    y_attn, y_attn_b, w0, w1, w2 = _attn_merge(obuf[0], obuf[1], obuf[2], lbuf[0], lbuf[1], lbuf[2])
    y_hgrn, o_raw, ck, _ = _hgrn_fwd(hg, lb, P["hgrn_norm"])
    wb = plan.weights_b(y_hgrn)
    P["conv_w"] = wb.pop("conv_w")
    W.update(wb)
    a = _mm(y_attn_b, W["w_ba"], "nn", bf16, "branch_attn")
    b = _mm(y_hgrn, W["w_bh"], "nn", bf16, "branch_hgrn")
    merged = _gate_fwd(a, b, gc)
    mo = _mm(merged, W["w_out"], "nn", f32, "out_proj")
    x1, h2 = _mid_fwd(x, mo, P["post_mix_norm"], P["pre_ffn_norm"])
    ug = _mm(h2, W["wt_up_g"], "nt", bf16, "up_gate")
    uv = _mm(h2, W["wt_up_v"], "nt", bf16, "up_val")
    cw_g, cw_v = P["conv_w"][:, :D_FF], P["conv_w"][:, D_FF:]
    cb_g, cb_v = P["conv_b"][:, :D_FF], P["conv_b"][:, D_FF:]
    act = _conv_fwd(ug, uv, cw_g, cw_v, cb_g, cb_v)
    fo = _mm(act, W["w_down"], "nn", f32, "down_proj")
    loss, dy, dfo, g_post_ffn = _final(x1, fo, tgt, P["post_ffn_norm"])
    dact = _mm(dfo, W["w_down"], "nt", bf16, "d_act")
    gW_down = _mm(act, dfo, "tn", f32, "gw_down")
    dug, duv, st_g, st_v = _conv_bwd(ug, uv, dact, cw_g, cw_v, cb_g, cb_v)
    dh2 = _mm(dug, W["wt_up_g"], "nn", f32, "dh2_gate")
    dh2 = _mm(duv, W["wt_up_v"], "nn", f32, "dh2_val", acc=dh2)
    gW_up_g = _mm(dug, h2, "tn", f32, "gw_up_gate")
    gW_up_v = _mm(duv, h2, "tn", f32, "gw_up_val")
    dx1, dmo, g_pre_ffn, g_post_mix = _mid_bwd(dy, dh2, x1, mo, P["pre_ffn_norm"], P["post_mix_norm"])
    dmerged = _mm(dmo, W["w_out"], "nt", bf16, "d_merged")
    gW_out = _mm(merged, dmo, "tn", f32, "gw_out")
    da, db, dgc = _gate_bwd(dmerged, a, b, gc)
    dyattn = _mm(da, W["w_ba"], "nt", f32, "d_yattn")
    gW_ba = _mm(y_attn_b, da, "tn", f32, "gw_ba")
    dyhgrn = _mm(db, W["w_bh"], "nt", f32, "d_yhgrn")
    gW_bh = _mm(y_hgrn, db, "tn", f32, "gw_bh")
    big_b = dict(w_ba=gW_ba, w_bh=gW_bh, w_out=gW_out, w_up=[gW_up_g, gW_up_v], w_down=gW_down)
    dq_h, df_h, dv_h, dog_h, glb8, gnw8, got_b = _hgrn_bwd(hg, o_raw, dyhgrn, ck, lb, P["hgrn_norm"],
                                                          plan.bwd_ride(big_b))
    dhg = [dq_h, df_h, dv_h, dog_h]
    g_lb_raw = _lb_bwd(P["hgrn_lb_raw"], glb8[0:1])
    gn = gnw8[0:1]
    g_hgrn_norm = (gn[:, 0:128] + gn[:, 128:256]) + (gn[:, 256:384] + gn[:, 384:512])
    dos = _attn_merge_bwd(dyattn, y_attn, w0, w1, w2)
    dqkvs, gW_qkv, g_rel = [], [], []
    for g, d in enumerate(DILATIONS):
        dq, dk, dv, dbias = _attn_bwd(qkv[g], biases[g], dos[g], dos[3 + g], lbuf[g], (S // d) // ATTN_BLOCK,
                                      f"attn_bwd{g}")
        dqkvs.append([dq, dk, dv])
        gW_qkv.append(_mm(dqkvs[g], hs[g], "tn", f32, f"gw_qkv{g}"))
        g_rel.append(_bias_grad(dbias.reshape(8, -1), consts[g][0], f"bias_grad{g}"))
    gW_hg = _mm(dhg, h1, "tn", f32, "gw_hg")
    gW_gate = _mm(dgc, h1, "tn", f32, "gw_gate")
    gW_in = gW_qkv + [gW_hg, gW_gate]
    token = plan.grads_a_start(gW_in)
    dh_parts = [_mm(dqkvs[g], W["wt_qkv"][g], "nn", f32, f"dh1_qkv{g}", after=token) for g in range(N_GROUPS)]
    token = plan.grads_a_exchange(dh_parts[2])
    dh_main = _mm(dhg, W["wt_hg"], "nn", f32, "dh1_hg", acc=dh_parts[0], after=token)
    dh_main = _mm(dgc, W["wt_gate"], "nn", f32, "dh1_gate", acc=dh_main, after=token)
    grad_x, g_pre_mix = _first_bwd(x, dx1, _dh_sum(dh_main, dh_parts[1], dh_parts[2]), P["pre_mix_norm"])

    g_conv_w = jnp.concatenate([st_g[0:3], st_v[0:3]], axis=1)
    g_conv_b = jnp.concatenate([st_g[3:4], st_v[3:4]], axis=1)
    small = dict(pre_mix_norm=g_pre_mix, rel_bias=jnp.concatenate(g_rel, axis=1), hgrn_lb_raw=g_lb_raw,
                 hgrn_norm=g_hgrn_norm, post_mix_norm=g_post_mix, pre_ffn_norm=g_pre_ffn, conv_b=g_conv_b,
                 post_ffn_norm=g_post_ffn, conv_w=g_conv_w)
    return loss, grad_x, gW_in, big_b, got_b, small


def _weights_a(both):
    wt = jnp.swapaxes(both, 0, 1).reshape(-1, D_MODEL)
    return dict(
        wt_qkv=[wt[g * QKV_G:(g + 1) * QKV_G] for g in range(N_GROUPS)],
        wt_hg=wt[3 * QKV_G:3 * QKV_G + 4 * HGRN_W],
        wt_gate=wt[3 * QKV_G + 4 * HGRN_W:],
    )


def _weights_b(slabs):
    sh = _unpack_rows(slabs, _PACK_B)
    wt_up = sh["w_up"].reshape(-1, D_MODEL)
    return dict(
        w_ba=_cols_to_full(sh["w_ba"]),
        w_bh=_cols_to_full(sh["w_bh"]),
        w_out=sh["w_out"].reshape(D_MODEL, D_MODEL),
        wt_up_g=wt_up[:D_FF],
        wt_up_v=wt_up[D_FF:],
        w_down=sh["w_down"].reshape(D_FF, D_MODEL),
    )


def _dest_rows(sections, height):
    out = []
    for j in range(8):
        lo, hi, off, pieces = j * height, (j + 1) * height, 0, []
        for s in sections:
            a, b = max(lo, off), min(hi, off + s.shape[0])
            if a < b:
                pieces.append(s[a - off:b - off])
            off += s.shape[0]
        out.append(pieces[0] if len(pieces) == 1 else jnp.concatenate(pieces, axis=0))
    return out


def _grad_blocks_a(sections):
    rows = _dest_rows(sections, 1088)
    return jnp.stack([jnp.stack([rows[2 * k + c].astype(bf16) for k in range(4)]) for c in range(2)])


def _grad_slab_b(g):
    shards = dict(w_ba=_full_to_cols(g["w_ba"]), w_bh=_full_to_cols(g["w_bh"]), w_out=g["w_out"].reshape(8, 128, D_MODEL),
                  w_up=jnp.stack(_dest_rows(g["w_up"], 704)), w_down=g["w_down"].reshape(8, 352, D_MODEL))
    return _pack_rows({k: v.astype(bf16) for k, v in shards.items()}, _PACK_B)


_CONVW_SLAB_ROWS = 16


class _Traffic:
    def __init__(self, slab_a, slab_b, conv_w):
        hi = conv_w.astype(bf16)
        r1 = conv_w - hi.astype(f32)
        mid = r1.astype(bf16)
        lo = (r1 - mid.astype(f32)).astype(bf16)
        bits = jnp.stack([hi, mid, lo]).reshape(-1)
        tail = jnp.pad(bits, (0, _CONVW_SLAB_ROWS * D_MODEL - bits.shape[0])).reshape(_CONVW_SLAB_ROWS, D_MODEL)
        self.slab_b = jnp.concatenate([slab_b, tail], axis=0)
        self.state_a, tok = _split_start(slab_a, "chip_gather", "ag_a_start")
        self.state_b, self.token = _split_start(self.slab_b, "chip_gather", "ag_b_start", after=tok)
        self.chip_sum = None
        self.state = None

    def start_token(self):
        return self.token

    def weights_a(self, after):
        by_chip = _split_wait(self.state_a, after, "chip_gather", "ag_a_wait")
        return _weights_a(_core_gather(by_chip, "ag_a_cores"))

    def forward_b(self, after):
        by_chip = _split_wait(self.state_b, after, "chip_gather", "ag_b_wait")
        self.state, token = _split_start(by_chip, "core_gather", "ag_b_cores_start")
        return token

    def weights_b(self, after):
        both = _split_wait(self.state, after, "core_gather", "ag_b_cores_wait")
        slabs = jnp.swapaxes(both, 0, 1).reshape((8,) + tuple(self.slab_b.shape))
        rows = _slab_rows(_PACK_B)
        out = _weights_b(slabs[:, :rows])
        pieces = slabs[:, rows:].reshape(8, -1)[:, :3 * 3 * 704].reshape(8, 3, 3, 704).astype(f32)
        out["conv_w"] = _cols_to_full((pieces[:, 0] + pieces[:, 1]) + pieces[:, 2])
        return out

    def bwd_ride(self, grads):
        self.chip_sum = _pair_sum(_by_core(_grad_slab_b(grads)), "rs_b")
        return (self.chip_sum, False)

    def grads_a_start(self, sections):
        self.state, token = _split_start(_grad_blocks_a(sections), "core_swap", "rs_a_cores_start")
        return token

    def grads_a_exchange(self, after):
        from_sib, by_core = _split_wait(self.state, after, "core_swap", "rs_a_cores_wait")
        self.state, token = _split_start(_pair_add(by_core, from_sib, "rs_a_pair_add"), "chip_xchg", "rs_a_start")
        return token

    def parts(self, got_b, after):
        parts = _unpack_rows(_fill_own(got_b, self.chip_sum, False), _PACK_B)
        parts["w_in"] = _split_wait(self.state, after, "chip_xchg", "rs_a_wait")
        return parts


def kernel(x, pre_mix_norm, w_in, rel_bias, hgrn_lb_raw, hgrn_norm, w_branch_attn, w_branch_hgrn, w_out, post_mix_norm, pre_ffn_norm, w_up, conv_w, conv_b, w_down, post_ffn_norm, loss_target, m_pre_mix_norm, m_w_in, m_rel_bias, m_hgrn_lb_raw, m_hgrn_norm, m_w_branch_attn, m_w_branch_hgrn, m_w_out, m_post_mix_norm, m_pre_ffn_norm, m_w_up, m_conv_w, m_conv_b, m_w_down, m_post_ffn_norm, v_pre_mix_norm, v_w_in, v_rel_bias, v_hgrn_lb_raw, v_hgrn_norm, v_w_branch_attn, v_w_branch_hgrn, v_w_out, v_post_mix_norm, v_pre_ffn_norm, v_w_up, v_conv_w, v_conv_b, v_w_down, v_post_ffn_norm):
    ci = lax.axis_index("c")
    dev = 4 * lax.axis_index("x") + 2 * lax.axis_index("y") + ci
    tr = lambda t: jnp.swapaxes(t[0], 0, 1)
    wts = dict(w_in=tr(w_in), w_ba=w_branch_attn[0], w_bh=w_branch_hgrn[0], w_out=w_out[0], w_up=tr(w_up),
               w_down=w_down[0])
    mom = dict(w_in=tr(m_w_in), w_ba=m_w_branch_attn[0], w_bh=m_w_branch_hgrn[0], w_out=m_w_out[0], w_up=tr(m_w_up),
               w_down=m_w_down[0])
    var = dict(w_in=tr(v_w_in), w_ba=v_w_branch_attn[0], w_bh=v_w_branch_hgrn[0], w_out=v_w_out[0], w_up=tr(v_w_up),
               w_down=v_w_down[0])
    small_w = dict(pre_mix_norm=pre_mix_norm, rel_bias=rel_bias, hgrn_lb_raw=hgrn_lb_raw, hgrn_norm=hgrn_norm,
                   post_mix_norm=post_mix_norm, pre_ffn_norm=pre_ffn_norm, conv_b=conv_b, post_ffn_norm=post_ffn_norm)
    small_m = dict(pre_mix_norm=m_pre_mix_norm, rel_bias=m_rel_bias, hgrn_lb_raw=m_hgrn_lb_raw, hgrn_norm=m_hgrn_norm,
                   post_mix_norm=m_post_mix_norm, pre_ffn_norm=m_pre_ffn_norm, conv_b=m_conv_b,
                   post_ffn_norm=m_post_ffn_norm)
    small_v = dict(pre_mix_norm=v_pre_mix_norm, rel_bias=v_rel_bias, hgrn_lb_raw=v_hgrn_lb_raw, hgrn_norm=v_hgrn_norm,
                   post_mix_norm=v_post_mix_norm, pre_ffn_norm=v_pre_ffn_norm, conv_b=v_conv_b,
                   post_ffn_norm=v_post_ffn_norm)

    plan = _Traffic(wts["w_in"].astype(bf16),
                    _pack_rows({k: wts[k].astype(bf16)[None] for k, _ in _PACK_B}, _PACK_B)[0], conv_w[0])

    loss8, grad_x, _, _, got_b, small = _local_step(x[0], loss_target[0], small_w, plan)
    parts = plan.parts(got_b, grad_x)
    outs_big = {}
    for k, _ in _PACK_SIZES:
        outs_big[k] = _adamw(wts[k], mom[k], var[k], parts[k], "adamw_" + k)

    spack = jnp.concatenate([_pack_small(small, loss8[0, 0:1]),
                             jnp.pad(small["conv_w"].reshape(-1, LANE), ((0, _CONVW_ROWS - 132), (0, 0)))], axis=0)
    allp = _core_gather(_chip_comm(spack, True, "ag_small_chips"), "ag_small_cores")
    ssum = _sum8(allp, "small_sum")
    gs = ssum[:_SMALL_ROWS]
    loss = ssum[_SMALL_USED // LANE, _SMALL_USED % LANE]
    res_small = _adamw(_pack_small(small_w), _pack_small(small_m), _pack_small(small_v), gs, "adamw_small")
    sm = [_unpack_small(t) for t in res_small]
    g_cw_full = ssum[_SMALL_ROWS:_SMALL_ROWS + 132].reshape(3, 2 * D_FF)
    g_cw = lax.dynamic_slice_in_dim(g_cw_full, dev * 704, 704, axis=1)
    res_cw = _adamw(conv_w[0], m_conv_w[0], v_conv_w[0], g_cw, "adamw_conv_w")

    def pick(i):
        def big_(k):
            t = outs_big[k][i]
            return (jnp.swapaxes(t, 0, 1) if k in _TRANSPOSED else t)[None]
        return [sm[i]["pre_mix_norm"], big_("w_in"), sm[i]["rel_bias"], sm[i]["hgrn_lb_raw"], sm[i]["hgrn_norm"],
                big_("w_ba"), big_("w_bh"), big_("w_out"), sm[i]["post_mix_norm"], sm[i]["pre_ffn_norm"],
                big_("w_up"), res_cw[i][None], sm[i]["conv_b"], big_("w_down"), sm[i]["post_ffn_norm"]]

    return (loss, grad_x[None], *pick(0), *pick(1), *pick(2), *pick(3))
```

```python
import functools
import math

import jax
import jax.numpy as jnp
from jax import lax
from jax.experimental import pallas as pl
from jax.experimental.pallas import tpu as pltpu

f32 = jnp.float32
bf16 = jnp.bfloat16
SDS = jax.ShapeDtypeStruct
HIGHEST = lax.Precision.HIGHEST
MESH = pl.DeviceIdType.MESH

NN = (((1,), (0,)), ((), ()))
NT = (((1,), (1,)), ((), ()))
TN = (((0,), (0,)), ((), ()))

D_MODEL = 1024
N_GROUPS = 3
DILATIONS = (1, 4, 16)
HEAD_DIM = 64
ATTN_BLOCK = 128
QKV_G = 1536
ATTN_OUT = 512
HGRN_W = 512
HGRN_CHUNK = 32
D_FF = 2816
NUM_BUCKETS = 32
MAX_EXACT = 16
MAX_DISTANCE = 2048
NEG_INF = -1e30
EPS = 1e-6
LANE = 128
SUBLANE = 8
VMEM_BIG = 48 * 1024 * 1024
MM_ROWS = 512
MM_OUT_BYTES = 8 * 1024 * 1024

ADAM_LR, ADAM_B1, ADAM_B2, ADAM_EPS, ADAM_WD, ADAM_STEP = 0.001, 0.9, 0.999, 1e-08, 0.01, 10


def _pick(n, pref):
    t = pref
    while t >= LANE:
        if n % t == 0:
            return t
        t //= 2
    return n


def _cparams(sem=None, vmem=None):
    kw = {}
    if sem is not None:
        kw["dimension_semantics"] = sem
    if vmem is not None:
        kw["vmem_limit_bytes"] = vmem
    return pltpu.CompilerParams(**kw)


def _sigmoid(x):
    return jax.nn.sigmoid(x)


def _colsum8(x):
    return x.reshape(x.shape[0] // SUBLANE, SUBLANE, x.shape[1]).sum(axis=0)


def _mm(a, b, mode, out_dtype, name, acc=None, after=None):
    dims = {"nn": NN, "nt": NT, "tn": TN}[mode]
    has_acc = acc is not None
    parts = list(a) if isinstance(a, (list, tuple)) else [a]
    if mode == "tn":
        assert not has_acc
        K, N = b.shape
        widths = [t.shape[1] for t in parts]
        M = sum(widths)
        whole = M * N * 4 <= MM_OUT_BYTES
        assert whole or len(parts) == 1
        tmm = M if whole else M // 2
        ts = _pick(K, 2 * MM_ROWS)
        nk = K // ts

        def body_tn(*refs):
            b_ref, o_ref = refs[-2], refs[-1]
            k = pl.program_id(1)
            bv = b_ref[...]
            lo = 0
            for a_ref, w in zip(refs[:-2], widths if whole else [tmm]):
                part = lax.dot_general(a_ref[...], bv, dims, preferred_element_type=f32)
                rows = slice(lo, lo + w)
                lo += w

                @pl.when(k == 0)
                def _(part=part, rows=rows):
                    o_ref[rows, :] = part

                @pl.when(k > 0)
                def _(part=part, rows=rows):
                    o_ref[rows, :] += part

        return pl.pallas_call(
            body_tn,
            grid=(M // tmm, nk),
            in_specs=[pl.BlockSpec((ts, w if whole else tmm), lambda i, k: (k, i)) for w in widths]
            + [pl.BlockSpec((ts, N), lambda i, k: (k, 0))],
            out_specs=pl.BlockSpec((tmm, N), lambda i, k: (i, 0)),
            out_shape=SDS((M, N), out_dtype),
            compiler_params=_cparams(("parallel", "arbitrary"), VMEM_BIG),
            name=name,
        )(*parts, b)

    bs = list(b) if isinstance(b, (list, tuple)) else [b]
    widths = [t.shape[1] for t in parts]
    M = parts[0].shape[0]
    kdim = 0 if mode == "nn" else 1
    N = bs[0].shape[1 - kdim]
    tm = _pick(M, MM_ROWS)
    npart, nb = len(parts), len(bs)
    place, bi, lo = [], 0, 0
    for w in widths:
        place.append((bi, lo))
        lo += w
        if lo == bs[bi].shape[kdim]:
            bi, lo = bi + 1, 0
    assert bi == nb and lo == 0

    def body(*refs):
        a_refs, b_refs = refs[:npart], refs[npart:npart + nb]
        c_ref = refs[npart + nb] if has_acc else None
        o_ref = refs[-1]
        part = None
        for a_ref, w, (bi, lo) in zip(a_refs, widths, place):
            b_ref = b_refs[bi]
            if w == bs[bi].shape[kdim]:
                bk = b_ref[...]
            else:
                bk = b_ref[:, lo:lo + w] if mode == "nt" else b_ref[lo:lo + w, :]
            t = lax.dot_general(a_ref[...], bk, dims, preferred_element_type=f32)
            part = t if part is None else part + t
        if has_acc:
            part = part + c_ref[...]
        o_ref[...] = part.astype(out_dtype)

    specs = [pl.BlockSpec((tm, w), lambda i: (i, 0)) for w in widths] \
        + [pl.BlockSpec(t.shape, lambda i: (0, 0)) for t in bs]
    args = parts + bs
    aliases = {}
    if has_acc:
        specs.append(pl.BlockSpec((tm, N), lambda i: (i, 0)))
        args.append(acc)
        aliases = {npart + nb: 0}
    if after is not None:
        specs.append(pl.BlockSpec(memory_space=pl.ANY))
        args.append(after)
    return pl.pallas_call(
        body,
        grid=(M // tm,),
        in_specs=specs,
        out_specs=pl.BlockSpec((tm, N), lambda i: (i, 0)),
        out_shape=SDS((M, N), out_dtype),
        input_output_aliases=aliases,
        compiler_params=_cparams(("parallel",), VMEM_BIG),
        name=name,
    )(*args)


PERM_ROWS = 1024


def _perm_spec(d, cols=LANE):
    return pl.BlockSpec((d, PERM_ROWS // d, cols), lambda i, j: (0, i, j))


def _to_natural(src_ref, dst_ref, d):
    n = src_ref.shape[1]
    for r in range(d):
        dst_ref[pl.ds(r, n, stride=d), :] = src_ref[r]


def _prep(x, w, after=None):
    S, D = x.shape
    R = PERM_ROWS
    nc = D // LANE
    n_in = nc + 1 + (after is not None)

    def body(*refs):
        x_refs, w_ref = refs[:nc], refs[nc]
        h_ref, h4_ref, h16_ref, rs = refs[n_in:]
        ssq = None
        for xr in x_refs:
            v = xr[...]
            t = jnp.sum(v * v, axis=-1, keepdims=True)
            ssq = t if ssq is None else ssq + t
        rinv = lax.rsqrt(ssq * (1.0 / D) + EPS)
        rs[...] = jnp.broadcast_to(rinv, (R, LANE))
        for j, xr in enumerate(x_refs):
            cols = slice(j * LANE, (j + 1) * LANE)
            wj = w_ref[:, cols]
            h_ref[:, cols] = ((xr[...] * rinv) * wj).astype(bf16)
            for d, o_ref in ((4, h4_ref), (16, h16_ref)):
                n = R // d
                for r in range(d):
                    rows = pl.ds(r, n, stride=d)
                    o_ref[r, :, cols] = ((xr[rows, :] * rs[rows, :]) * wj).astype(bf16)

    col = lambda j: pl.BlockSpec((R, LANE), lambda i, j=j: (i, j))
    h, h4, h16 = pl.pallas_call(
        body,
        grid=(S // R,),
        in_specs=[col(j) for j in range(nc)] + [pl.BlockSpec((1, D), lambda i: (0, 0))]
        + ([] if after is None else [pl.BlockSpec(memory_space=pl.ANY)]),
        out_specs=[pl.BlockSpec((R, D), lambda i: (i, 0)), pl.BlockSpec((4, R // 4, D), lambda i: (0, i, 0)),
                   pl.BlockSpec((16, R // 16, D), lambda i: (0, i, 0))],
        out_shape=[SDS((S, D), bf16), SDS((4, S // 4, D), bf16), SDS((16, S // 16, D), bf16)],
        scratch_shapes=[pltpu.VMEM((R, LANE), f32)],
        compiler_params=_cparams(("parallel",), VMEM_BIG),
        name="prep_norm_perm",
    )(*([x] * nc), w, *([] if after is None else [after]))
    return [h, h4.reshape(S, D), h16.reshape(S, D)]


def _dh_sum(a, b, c):
    S, D = a.shape
    R = PERM_ROWS

    def body(a_ref, b_ref, c_ref, o_ref, sb, sc):
        _to_natural(b_ref, sb, 4)
        _to_natural(c_ref, sc, 16)
        o_ref[...] = (a_ref[...] + sb[...]) + sc[...]

    nat = pl.BlockSpec((R, LANE), lambda i, j: (i, j))
    return pl.pallas_call(
        body,
        grid=(S // R, D // LANE),
        in_specs=[nat, _perm_spec(4), _perm_spec(16)],
        out_specs=nat,
        out_shape=SDS((S, D), f32),
        scratch_shapes=[pltpu.VMEM((R, LANE), f32)] * 2,
        compiler_params=_cparams(("parallel", "parallel")),
        name="dh_sum",
    )(a, b.reshape(4, S // 4, D), c.reshape(16, S // 16, D))


def _rms_parts(xv):
    r = lax.rsqrt(jnp.mean(xv * xv, axis=-1, keepdims=True) + EPS)
    return r, xv * r


def _rms_bwd(xhat, r, w, dy):
    dyw = dy * w
    return r * (dyw - xhat * jnp.mean(dyw * xhat, axis=-1, keepdims=True))


def _mid_fwd(x, merged, w_out, w_pm, w_pf):
    S, D = x.shape
    tm = _pick(S, MM_ROWS)

    def body(x_ref, m_ref, wo_ref, wpm_ref, wpf_ref, mo_ref, x1_ref, h2_ref):
        mo = jnp.dot(m_ref[...], wo_ref[...], preferred_element_type=f32)
        mo_ref[...] = mo
        _, moh = _rms_parts(mo)
        x1 = x_ref[...] + moh * wpm_ref[...]
        x1_ref[...] = x1
        _, x1h = _rms_parts(x1)
        h2_ref[...] = (x1h * wpf_ref[...]).astype(bf16)

    row = pl.BlockSpec((tm, D), lambda i: (i, 0))
    vec = pl.BlockSpec((1, D), lambda i: (0, 0))
    return pl.pallas_call(
        body,
        grid=(S // tm,),
        in_specs=[row, pl.BlockSpec((tm, merged.shape[1]), lambda i: (i, 0)),
                  pl.BlockSpec(w_out.shape, lambda i: (0, 0)), vec, vec],
        out_specs=[row, row, row],
        out_shape=[SDS((S, D), f32), SDS((S, D), f32), SDS((S, D), bf16)],
        compiler_params=_cparams(("parallel",), VMEM_BIG),
        name="out_proj_mid_fwd",
    )(x, merged, w_out, w_pm, w_pf)


def _final(x1, act, w_down, tgt, w_pfn):
    S, D = x1.shape
    tm = _pick(S, MM_ROWS)
    nt = S // tm

    def body(x1_ref, a_ref, wd_ref, t_ref, w_ref, loss_ref, dy_ref, dfo_ref, gw_ref, lacc, gacc):
        i = pl.program_id(0)

        @pl.when(i == 0)
        def _():
            lacc[...] = jnp.zeros_like(lacc)
            gacc[...] = jnp.zeros_like(gacc)

        w = w_ref[...]
        r, foh = _rms_parts(jnp.dot(a_ref[...], wd_ref[...], preferred_element_type=f32))
        y = x1_ref[...] + foh * w
        err = y - t_ref[...]
        lacc[...] += _colsum8(err * err)
        dy = err * (1.0 / D)
        dy_ref[...] = dy
        gacc[...] += _colsum8(dy * foh)
        dfo_ref[...] = _rms_bwd(foh, r, w, dy).astype(bf16)

        @pl.when(i == nt - 1)
        def _():
            loss_ref[...] = jnp.full((SUBLANE, LANE), 0.5 / D, f32) * jnp.sum(lacc[...])
            gw_ref[...] = jnp.sum(gacc[...], axis=0, keepdims=True)

    row = pl.BlockSpec((tm, D), lambda i: (i, 0))
    vec = pl.BlockSpec((1, D), lambda i: (0, 0))
    return pl.pallas_call(
        body,
        grid=(nt,),
        in_specs=[row, pl.BlockSpec((tm, act.shape[1]), lambda i: (i, 0)),
                  pl.BlockSpec(w_down.shape, lambda i: (0, 0)), row, vec],
        out_specs=[pl.BlockSpec((SUBLANE, LANE), lambda i: (0, 0)), row, row, vec],
        out_shape=[SDS((SUBLANE, LANE), f32), SDS((S, D), f32), SDS((S, D), bf16), SDS((1, D), f32)],
        scratch_shapes=[pltpu.VMEM((SUBLANE, D), f32), pltpu.VMEM((SUBLANE, D), f32)],
        compiler_params=_cparams(("arbitrary",), VMEM_BIG),
        name="down_proj_final_loss",
    )(x1, act, w_down, tgt, w_pfn)


def _mid_bwd(dy, dh2, x1, mo, w_pf, w_pm):
    S, D = dy.shape
    tm = _pick(S, 512)
    nt = S // tm

    def body(dy_ref, dh2_ref, x1_ref, mo_ref, wpf_ref, wpm_ref, dx1_ref, dmo_ref, gpf_ref, gpm_ref, apf, apm):
        i = pl.program_id(0)

        @pl.when(i == 0)
        def _():
            apf[...] = jnp.zeros_like(apf)
            apm[...] = jnp.zeros_like(apm)

        r1, x1h = _rms_parts(x1_ref[...])
        dh2 = dh2_ref[...]
        apf[...] += _colsum8(dh2 * x1h)
        dx1 = dy_ref[...] + _rms_bwd(x1h, r1, wpf_ref[...], dh2)
        dx1_ref[...] = dx1
        rm, moh = _rms_parts(mo_ref[...])
        apm[...] += _colsum8(dx1 * moh)
        dmo_ref[...] = _rms_bwd(moh, rm, wpm_ref[...], dx1).astype(bf16)

        @pl.when(i == nt - 1)
        def _():
            gpf_ref[...] = jnp.sum(apf[...], axis=0, keepdims=True)
            gpm_ref[...] = jnp.sum(apm[...], axis=0, keepdims=True)

    row = pl.BlockSpec((tm, D), lambda i: (i, 0))
    vec = pl.BlockSpec((1, D), lambda i: (0, 0))
    return pl.pallas_call(
        body,
        grid=(nt,),
        in_specs=[row, row, row, row, vec, vec],
        out_specs=[row, row, vec, vec],
        out_shape=[SDS((S, D), f32), SDS((S, D), bf16), SDS((1, D), f32), SDS((1, D), f32)],
        scratch_shapes=[pltpu.VMEM((SUBLANE, D), f32), pltpu.VMEM((SUBLANE, D), f32)],
        compiler_params=_cparams(("arbitrary",)),
        name="mid_bwd",
    )(dy, dh2, x1, mo, w_pf, w_pm)


def _first_bwd(x, dx1, dh, w_pre):
    S, D = x.shape
    tm = _pick(S, 512)
    nt = S // tm

    def body(x_ref, dx1_ref, a_ref, w_ref, gx_ref, gw_ref, acc):
        i = pl.program_id(0)

        @pl.when(i == 0)
        def _():
            acc[...] = jnp.zeros_like(acc)

        r, xh = _rms_parts(x_ref[...])
        dh = a_ref[...]
        acc[...] += _colsum8(dh * xh)
        gx_ref[...] = dx1_ref[...] + _rms_bwd(xh, r, w_ref[...], dh)

        @pl.when(i == nt - 1)
        def _():
            gw_ref[...] = jnp.sum(acc[...], axis=0, keepdims=True)

    row = pl.BlockSpec((tm, D), lambda i: (i, 0))
    vec = pl.BlockSpec((1, D), lambda i: (0, 0))
    return pl.pallas_call(
        body,
        grid=(nt,),
        in_specs=[row, row, row, vec],
        out_specs=[row, vec],
        out_shape=[SDS((S, D), f32), SDS((1, D), f32)],
        scratch_shapes=[pltpu.VMEM((SUBLANE, D), f32)],
        compiler_params=_cparams(("arbitrary",)),
        name="first_bwd",
    )(x, dx1, dh, w_pre)


def _t5_bucket(dist):
    n = jnp.maximum(dist, 0)
    nf = jnp.maximum(n, 1).astype(f32)
    large = MAX_EXACT + (jnp.log(nf / MAX_EXACT) / math.log(MAX_DISTANCE / MAX_EXACT)
                         * (NUM_BUCKETS - MAX_EXACT)).astype(jnp.int32)
    large = jnp.minimum(large, NUM_BUCKETS - 1)
    return jnp.where(n < MAX_EXACT, n, large)


def _bias_consts(d):
    blk = ATTN_BLOCK
    rel = jnp.arange(blk)[:, None] + blk - jnp.arange(2 * blk)[None, :]
    in_win = (rel >= 0) & (rel <= blk)
    bucket = _t5_bucket(rel * d).reshape(1, -1)
    onehot = (bucket == jnp.arange(NUM_BUCKETS)[:, None]).astype(f32)
    return onehot, in_win.astype(f32).reshape(1, -1)


def _bias_build(tab_t, onehot, maskf, name, after):
    H = tab_t.shape[0]

    def body(t_ref, oh_ref, m_ref, after_ref, o_ref):
        b = jnp.dot(t_ref[...], oh_ref[...], precision=HIGHEST, preferred_element_type=f32)
        o_ref[...] = jnp.where(m_ref[...] > 0.5, b, NEG_INF)

    vm = pl.BlockSpec(memory_space=pltpu.VMEM)
    return pl.pallas_call(body, out_shape=SDS((H, onehot.shape[1]), f32), name=name,
                          in_specs=[vm, vm, vm, pl.BlockSpec(memory_space=pl.ANY)], out_specs=vm,
                          )(tab_t, onehot, maskf, after)


def _bias_grad(dbias_flat, onehot, name):
    H = dbias_flat.shape[0]

    def body(g_ref, oh_ref, o_ref):
        o_ref[...] = lax.dot_general(oh_ref[...], g_ref[...], NT, precision=HIGHEST, preferred_element_type=f32)

    return pl.pallas_call(body, out_shape=SDS((NUM_BUCKETS, H), f32), name=name)(dbias_flat, onehot)


ATTN_TILE = 512
ATTN_SUB = ATTN_TILE // ATTN_BLOCK


def _qkv_specs(nt):
    tile = (ATTN_TILE, LANE)
    blk = (ATTN_BLOCK, LANE)
    cur = lambda off: (lambda h, t: (jnp.minimum(t, nt - 1), off + h))
    prev = lambda off: (lambda h, t: (jnp.maximum(jnp.minimum(t, nt - 1) * ATTN_SUB - 1, 0), off + h))
    return [pl.BlockSpec(tile, cur(0)), pl.BlockSpec(blk, prev(4)), pl.BlockSpec(tile, cur(4)),
            pl.BlockSpec(blk, prev(8)), pl.BlockSpec(tile, cur(8))]


def _head_masks():
    lane = lax.broadcasted_iota(jnp.int32, (ATTN_BLOCK, LANE), 1)
    return lane < HEAD_DIM


def _stack_heads(x2, low):
    zero = jnp.zeros_like(x2)
    return jnp.concatenate([jnp.where(low, x2, zero), jnp.where(low, zero, x2)], axis=0)


def _attn_fwd(qkv, bias, bps, name, after=None):
    S = qkv.shape[0]
    nt = S // ATTN_TILE
    scale = HEAD_DIM ** -0.5

    def body(q_ref, kp_ref, kc_ref, vp_ref, vc_ref, b_ref, *rest):
        o_ref, l_ref = rest[-2:]
        t = pl.program_id(1)
        kk = jnp.concatenate([kp_ref[...], kc_ref[...]], axis=0)
        vv = jnp.concatenate([vp_ref[...], vc_ref[...]], axis=0)
        low = _head_masks()
        col = lax.broadcasted_iota(jnp.int32, (2 * ATTN_BLOCK, 2 * ATTN_BLOCK), 1)
        bias2 = b_ref[...].reshape(2 * ATTN_BLOCK, 2 * ATTN_BLOCK)
        for b in range(ATTN_SUB):
            lo = b * ATTN_BLOCK
            rows = slice(lo, lo + ATTN_BLOCK)
            keys = slice(lo, lo + 2 * ATTN_BLOCK)
            dead = jnp.logical_and((t * ATTN_SUB + b) % bps == 0, col < ATTN_BLOCK)
            q2 = _stack_heads(q_ref[rows, :], low)
            kb, vb = kk[keys], vv[keys]
            s = lax.dot_general(q2, kb, NT, preferred_element_type=f32) * scale + bias2
            s = jnp.where(dead, NEG_INF, s)
            m = jnp.max(s, axis=-1, keepdims=True)
            p = jnp.exp(s - m)
            l = jnp.sum(p, axis=-1, keepdims=True)
            o2 = jnp.dot(p.astype(bf16), vb, preferred_element_type=f32) / l
            lse = m + jnp.log(l)
            o_ref[rows, :] = jnp.where(low, o2[:ATTN_BLOCK], o2[ATTN_BLOCK:])
            l_ref[rows, :] = jnp.where(low, lse[:ATTN_BLOCK], lse[ATTN_BLOCK:])

    tile = pl.BlockSpec((ATTN_TILE, LANE), lambda h, t: (t, h))
    return pl.pallas_call(
        body,
        grid=(4, nt),
        in_specs=_qkv_specs(nt) + [pl.BlockSpec((2, ATTN_BLOCK, 2 * ATTN_BLOCK), lambda h, t: (h, 0, 0))]
        + ([] if after is None else [pl.BlockSpec(memory_space=pl.ANY)]),
        out_specs=[tile, tile],
        out_shape=[SDS((S, ATTN_OUT), f32), SDS((S, ATTN_OUT), f32)],
        compiler_params=_cparams(("parallel", "parallel")),
        name=name,
    )(qkv, qkv, qkv, qkv, qkv, bias, *([] if after is None else [after]))


def _attn_bwd(qkv, bias, do, dvec, lse, bps, name):
    S = qkv.shape[0]
    nt = S // ATTN_TILE
    scale = HEAD_DIM ** -0.5

    def assemble(parts):
        rows = [parts[0][:ATTN_BLOCK]]
        for b in range(ATTN_SUB - 1):
            rows.append(parts[b][ATTN_BLOCK:] + parts[b + 1][:ATTN_BLOCK])
        rows.append(parts[-1][ATTN_BLOCK:])
        return rows

    def body(q_ref, kp_ref, kc_ref, vp_ref, vc_ref, b_ref, do_ref, dvec_ref, lse_ref,
             dq_ref, dk_ref, dv_ref, db_ref, ck, cv):
        t = pl.program_id(1)
        last = ATTN_TILE - ATTN_BLOCK

        @pl.when(t == 0)
        def _():
            ck[...] = jnp.zeros_like(ck)
            cv[...] = jnp.zeros_like(cv)
            db_ref[...] = jnp.zeros_like(db_ref)

        @pl.when(t < nt)
        def _():
            kk = jnp.concatenate([kp_ref[...], kc_ref[...]], axis=0)
            vv = jnp.concatenate([vp_ref[...], vc_ref[...]], axis=0)
            low = _head_masks()
            col = lax.broadcasted_iota(jnp.int32, (2 * ATTN_BLOCK, 2 * ATTN_BLOCK), 1)
            bias2 = b_ref[...].reshape(2 * ATTN_BLOCK, 2 * ATTN_BLOCK)
            dk_parts, dv_parts = [], []
            dsum = None
            for b in range(ATTN_SUB):
                lo = b * ATTN_BLOCK
                rows = slice(lo, lo + ATTN_BLOCK)
                keys = slice(lo, lo + 2 * ATTN_BLOCK)
                dead = jnp.logical_and((t * ATTN_SUB + b) % bps == 0, col < ATTN_BLOCK)
                q2 = _stack_heads(q_ref[rows, :], low)
                do2 = _stack_heads(do_ref[rows, :].astype(bf16), low)
                kb, vb = kk[keys], vv[keys]
                dvec2 = dvec_ref[rows, :]
                lse2 = lse_ref[rows, :]
                per_row = lambda t2: jnp.concatenate([t2[:, 0:1], t2[:, HEAD_DIM:HEAD_DIM + 1]], axis=0)
                s = lax.dot_general(q2, kb, NT, preferred_element_type=f32) * scale + bias2
                s = jnp.where(dead, NEG_INF, s)
                p = jnp.exp(s - per_row(lse2))
                dp = lax.dot_general(do2, vb, NT, preferred_element_type=f32)
                ds = p * (dp - per_row(dvec2))
                dsum = ds if dsum is None else dsum + ds
                dsb = ds.astype(bf16)
                dq2 = jnp.dot(dsb, kb, preferred_element_type=f32) * scale
                dq_ref[rows, :] = jnp.where(low, dq2[:ATTN_BLOCK], dq2[ATTN_BLOCK:]).astype(bf16)
                dk_parts.append(lax.dot_general(dsb, q2, TN, preferred_element_type=f32) * scale)
                dv_parts.append(lax.dot_general(p.astype(bf16), do2, TN, preferred_element_type=f32))
            db_ref[...] += dsum.reshape(2, ATTN_BLOCK, 2 * ATTN_BLOCK)
            for parts, carry, out_ref in ((dk_parts, ck, dk_ref), (dv_parts, cv, dv_ref)):
                rws = assemble(parts)
                out_ref[:last, :] = carry[:last, :].astype(bf16)
                out_ref[last:, :] = (carry[last:, :] + rws[0]).astype(bf16)
                for b in range(ATTN_SUB):
                    carry[b * ATTN_BLOCK:(b + 1) * ATTN_BLOCK, :] = rws[b + 1]

        @pl.when(t == nt)
        def _():
            dk_ref[...] = ck[...].astype(bf16)
            dv_ref[...] = cv[...].astype(bf16)

    tile = (ATTN_TILE, LANE)
    cur = pl.BlockSpec(tile, lambda h, t: (jnp.minimum(t, nt - 1), h))
    lag = pl.BlockSpec(tile, lambda h, t: (jnp.maximum(t - 1, 0), h))
    bspec = pl.BlockSpec((2, ATTN_BLOCK, 2 * ATTN_BLOCK), lambda h, t: (h, 0, 0))
    return pl.pallas_call(
        body,
        grid=(4, nt + 1),
        in_specs=_qkv_specs(nt) + [bspec, cur, cur, cur],
        out_specs=[cur, lag, lag, bspec],
        out_shape=[SDS((S, ATTN_OUT), bf16), SDS((S, ATTN_OUT), bf16), SDS((S, ATTN_OUT), bf16),
                   SDS((8, ATTN_BLOCK, 2 * ATTN_BLOCK), f32)],
        scratch_shapes=[pltpu.VMEM(tile, f32), pltpu.VMEM(tile, f32)],
        compiler_params=_cparams(("parallel", "arbitrary")),
        name=name,
    )(qkv, qkv, qkv, qkv, qkv, bias, do, dvec, lse)


def _attn_merge(o0, o1, o2, l0, l1, l2):
    S, W = o0.shape
    R = PERM_ROWS

    def body(o0_ref, o1_ref, o2_ref, l0_ref, l1_ref, l2_ref, y_ref, yb_ref, w0_ref, w1_ref, w2_ref,
             so1, so2, sl1, sl2):
        _to_natural(o1_ref, so1, 4)
        _to_natural(l1_ref, sl1, 4)
        _to_natural(o2_ref, so2, 16)
        _to_natural(l2_ref, sl2, 16)
        a, b, c = l0_ref[...], sl1[...], sl2[...]
        m = jnp.maximum(jnp.maximum(a, b), c)
        ea, eb, ec = jnp.exp(a - m), jnp.exp(b - m), jnp.exp(c - m)
        den = (ea + eb) + ec
        w0, w1, w2 = ea / den, eb / den, ec / den
        y = (w0 * o0_ref[...] + w1 * so1[...]) + w2 * so2[...]
        y_ref[...] = y
        yb_ref[...] = y.astype(bf16)
        w0_ref[...] = w0
        w1_ref[...] = w1
        w2_ref[...] = w2

    nat = pl.BlockSpec((R, LANE), lambda i, j: (i, j))
    v4 = lambda t: t.reshape(4, S // 4, W)
    v16 = lambda t: t.reshape(16, S // 16, W)
    return pl.pallas_call(
        body,
        grid=(S // R, W // LANE),
        in_specs=[nat, _perm_spec(4), _perm_spec(16)] * 2,
        out_specs=[nat] * 5,
        out_shape=[SDS((S, W), f32), SDS((S, W), bf16)] + [SDS((S, W), f32)] * 3,
        scratch_shapes=[pltpu.VMEM((R, LANE), f32)] * 4,
        compiler_params=_cparams(("parallel", "parallel")),
        name="attn_merge",
    )(o0, v4(o1), v16(o2), l0, v4(l1), v16(l2))


def _attn_merge_bwd(dy, y, w0, w1, w2):
    S, W = dy.shape
    R = PERM_ROWS

    def body(dy_ref, y_ref, w0_ref, w1_ref, w2_ref, a0, a1, a2, b0, b1, b2, sa, sb):
        dyv = dy_ref[...]
        r = lax.broadcasted_iota(jnp.int32, (LANE, LANE), 0) // HEAD_DIM
        c = lax.broadcasted_iota(jnp.int32, (LANE, LANE), 1) // HEAD_DIM
        seg = jnp.where(r == c, 1.0, 0.0).astype(f32)
        cbar = jnp.dot(dyv * y_ref[...], seg, precision=HIGHEST, preferred_element_type=f32)
        w = w0_ref[...]
        a0[...] = (w * dyv).astype(bf16)
        b0[...] = w * cbar
        for d, w_ref, a_ref, b_ref in ((4, w1_ref, a1, b1), (16, w2_ref, a2, b2)):
            w = w_ref[...]
            sa[...] = w * dyv
            sb[...] = w * cbar
            n = R // d
            for k in range(d):
                rows = pl.ds(k, n, stride=d)
                a_ref[k] = sa[rows, :].astype(bf16)
                b_ref[k] = sb[rows, :]

    nat = pl.BlockSpec((R, LANE), lambda i, j: (i, j))
    shapes = lambda dt: [SDS((S, W), dt), SDS((4, S // 4, W), dt), SDS((16, S // 16, W), dt)]
    outs = pl.pallas_call(
        body,
        grid=(S // R, W // LANE),
        in_specs=[nat] * 5,
        out_specs=[nat, _perm_spec(4), _perm_spec(16)] * 2,
        out_shape=shapes(bf16) + shapes(f32),
        scratch_shapes=[pltpu.VMEM((R, LANE), f32)] * 2,
        compiler_params=_cparams(("parallel", "parallel")),
        name="attn_merge_bwd",
    )(dy, y, w0, w1, w2)
    return [t.reshape(S, W) for t in outs]


HGRN_SB = 256


def _chunk_masks():
    r = jnp.arange(HGRN_SB)[:, None]
    c = jnp.arange(HGRN_SB)[None, :]
    same = (r // HGRN_CHUNK) == (c // HGRN_CHUNK)
    return jnp.stack([same & (c <= r), same, same & (c >= r)]).astype(bf16)


def _mask_dot(mask, x):
    hi = x.astype(bf16)
    r1 = x - hi.astype(f32)
    mid = r1.astype(bf16)
    lo = (r1 - mid.astype(f32)).astype(bf16)
    p = jnp.dot(mask, jnp.concatenate([hi, mid, lo], axis=1), preferred_element_type=f32)
    n = x.shape[1]
    return (p[:, :n] + p[:, n:2 * n]) + p[:, 2 * n:]


def _hgrn_prep(q_raw, f_raw, lbv, tril, same):
    sq = _sigmoid(q_raw)
    qs = q_raw * sq
    sig = _sigmoid(f_raw)
    f = lbv + (1.0 - lbv) * sig
    g = jnp.log(f)
    k = 1.0 - f
    G = _mask_dot(tril, g)
    GL = _mask_dot(same, g)
    eG = jnp.exp(G)
    einv = jnp.exp(-G)
    edec = jnp.exp(GL - G)
    return dict(sq=sq, qs=qs, sig=sig, f=f, k=k, eG=eG, einv=einv, edec=edec, eGL=jnp.exp(GL),
                qt=qs * eG, kt=k * einv, kd=k * edec)


def _ride_split(ride, rest, n_out, n_scratch):
    if ride is None:
        return None, rest[:n_out], None, rest[n_out:], None
    return rest[0], rest[1:1 + n_out], rest[1 + n_out], rest[2 + n_out:2 + n_out + n_scratch], rest[2 + n_out + n_scratch:]


def _hgrn_fwd(hg, lb, normw, ride=None):
    S = hg.shape[0]
    sb = HGRN_SB
    nsb = S // sb
    nch = sb // HGRN_CHUNK

    def body(q_ref, f_ref, v_ref, og_ref, lb_ref, nw_ref, m_ref, *rest):
        src_ref, (y_ref, o_ref, ck_ref), got_ref, (st,), sems = _ride_split(ride, rest, 3, 1)
        j = pl.program_id(1)
        if ride is not None:
            @pl.when(jnp.logical_and(pl.program_id(0) == 0, j == 0))
            def _():
                _chip_start(src_ref, got_ref, sems[0], sems[1], ride[1])

        @pl.when(j == 0)
        def _():
            st[...] = jnp.zeros_like(st)

        ST = st[...]
        ck_ref[0, 0] = ST
        tril_m = m_ref[0]
        tril = tril_m.astype(f32) > 0.5
        pr = _hgrn_prep(q_ref[...], f_ref[...], lb_ref[...], tril_m, m_ref[1])
        qtb, ktb, kdb = pr["qt"].astype(bf16), pr["kt"].astype(bf16), pr["kd"].astype(bf16)
        eGL = pr["eGL"]
        vb = v_ref[...].astype(bf16)
        A = jnp.where(tril, lax.dot_general(qtb, ktb, NT, preferred_element_type=f32), 0.0)
        o = jnp.dot(A.astype(bf16), vb, preferred_element_type=f32)
        outs = []
        for ci in range(nch):
            lo = ci * HGRN_CHUNK
            sl = slice(lo, lo + HGRN_CHUNK)
            outs.append(o[sl] + lax.dot_general(qtb[sl], ST.astype(bf16), NT, preferred_element_type=f32))
            ST = ST * eGL[lo:lo + 1, :] + lax.dot_general(vb[sl], kdb[sl], TN, preferred_element_type=f32)
        st[...] = ST
        of = jnp.concatenate(outs, axis=0)
        o_ref[...] = of
        rms = lax.rsqrt(jnp.mean(of * of, axis=-1, keepdims=True) + EPS)
        ogv = og_ref[...]
        y_ref[...] = ((of * rms * nw_ref[...]) * (ogv * _sigmoid(ogv))).astype(bf16)

        if ride is not None:
            @pl.when(jnp.logical_and(pl.program_id(0) == 3, j == nsb - 1))
            def _():
                _chip_finish(src_ref, got_ref, sems[0], sems[1], ride[1])

    col = lambda off: pl.BlockSpec((sb, LANE), lambda h, j: (j, off + h))
    riding = ride is not None
    res = pl.pallas_call(
        body,
        grid=(4, nsb),
        in_specs=[col(0), col(4), col(8), col(12), pl.BlockSpec((1, LANE), lambda h, j: (0, h)),
                  pl.BlockSpec((1, LANE), lambda h, j: (0, 0)),
                  pl.BlockSpec((3, sb, sb), lambda h, j: (0, 0, 0))] + ([_ANY] if riding else []),
        out_specs=[col(0), col(0), pl.BlockSpec((1, 1, LANE, LANE), lambda h, j: (h, j, 0, 0))]
        + ([_ANY] if riding else []),
        out_shape=[SDS((S, HGRN_W), bf16), SDS((S, HGRN_W), f32), SDS((4, nsb, LANE, LANE), f32)]
        + ([_chip_out_shape(*ride)] if riding else []),
        scratch_shapes=[pltpu.VMEM((LANE, LANE), f32)] + (list(_CHIP_SEMS) if riding else []),
        compiler_params=_cparams(("arbitrary", "arbitrary") if riding else ("parallel", "arbitrary")),
        name="hgrn_fwd",
    )(hg, hg, hg, hg, lb, normw, _chunk_masks(), *([ride[0]] if riding else []))
    return tuple(res) if riding else (*res, None)


def _hgrn_bwd(hg, o_raw, dy, ck, lb, normw, ride=None):
    S = hg.shape[0]
    sb = HGRN_SB
    nsb = S // sb
    nch = sb // HGRN_CHUNK

    def body(q_ref, f_ref, v_ref, og_ref, o_ref, dy_ref, ck_ref, lb_ref, nw_ref, m_ref, *rest):
        src_ref, outs, got_ref, (dst, alb, anw), sems = _ride_split(ride, rest, 6, 3)
        dq_ref, df_ref, dv_ref, dog_ref, glb_ref, gnw_ref = outs
        j = pl.program_id(1)
        if ride is not None:
            @pl.when(jnp.logical_and(pl.program_id(0) == 0, j == 0))
            def _():
                _chip_start(src_ref, got_ref, sems[0], sems[1], ride[1])

        @pl.when(j == 0)
        def _():
            dst[...] = jnp.zeros_like(dst)
            alb[...] = jnp.zeros_like(alb)
            anw[...] = jnp.zeros_like(anw)

        tril_m = m_ref[0]
        tril = tril_m.astype(f32) > 0.5
        lbv = lb_ref[...]
        q_raw = q_ref[...]
        pr = _hgrn_prep(q_raw, f_ref[...], lbv, tril_m, m_ref[1])
        qt, kt, kd, eGL = pr["qt"], pr["kt"], pr["kd"], pr["eGL"]
        qtb, ktb, kdb = qt.astype(bf16), kt.astype(bf16), kd.astype(bf16)
        vb = v_ref[...].astype(bf16)

        o = o_ref[...]
        ogv = og_ref[...]
        sog = _sigmoid(ogv)
        rms = lax.rsqrt(jnp.mean(o * o, axis=-1, keepdims=True) + EPS)
        oh = o * rms
        nw = nw_ref[...]
        dyv = dy_ref[...]
        dog_ref[...] = (dyv * (oh * nw) * (sog * (1.0 + ogv * (1.0 - sog)))).astype(bf16)
        dohw = dyv * (ogv * sog)
        anw[...] += _colsum8(dohw * oh)
        doh = dohw * nw
        do = rms * (doh - oh * jnp.mean(doh * oh, axis=-1, keepdims=True))
        dob = do.astype(bf16)

        Ab = jnp.where(tril, lax.dot_general(qtb, ktb, NT, preferred_element_type=f32), 0.0).astype(bf16)
        dAb = jnp.where(tril, lax.dot_general(dob, vb, NT, preferred_element_type=f32), 0.0).astype(bf16)
        dv_acc = lax.dot_general(Ab, dob, TN, preferred_element_type=f32)
        dqt = jnp.dot(dAb, ktb, preferred_element_type=f32)
        dkt = lax.dot_general(dAb, qtb, TN, preferred_element_type=f32)

        ST = ck_ref[0, 0]
        states = []
        for ci in range(nch):
            lo = ci * HGRN_CHUNK
            sl = slice(lo, lo + HGRN_CHUNK)
            states.append(ST)
            ST = ST * eGL[lo:lo + 1, :] + lax.dot_general(vb[sl], kdb[sl], TN, preferred_element_type=f32)

        dST = dst[...]
        dqt_i, dkd_i, dv_i, deg_i = [None] * nch, [None] * nch, [None] * nch, [None] * nch
        for ci in reversed(range(nch)):
            lo = ci * HGRN_CHUNK
            sl = slice(lo, lo + HGRN_CHUNK)
            ST0 = states[ci]
            dSTb = dST.astype(bf16)
            dv_i[ci] = lax.dot_general(kdb[sl], dSTb, NT, preferred_element_type=f32)
            dqt_i[ci] = jnp.dot(dob[sl], ST0.astype(bf16), preferred_element_type=f32)
            dkd_i[ci] = jnp.dot(vb[sl], dSTb, preferred_element_type=f32)
            deg_i[ci] = jnp.broadcast_to(jnp.sum(dST * ST0, axis=0, keepdims=True), (HGRN_CHUNK, LANE))
            dST = dST * eGL[lo:lo + 1, :] + lax.dot_general(dob[sl], qtb[sl], TN, preferred_element_type=f32)
        dst[...] = dST

        dqt = dqt + jnp.concatenate(dqt_i, axis=0)
        dkd = jnp.concatenate(dkd_i, axis=0)
        dv_ref[...] = (dv_acc + jnp.concatenate(dv_i, axis=0)).astype(bf16)
        deg = jnp.concatenate(deg_i, axis=0)

        dqs = dqt * pr["eG"]
        dkdkd = dkd * kd
        dG = dqt * qt - dkt * kt - dkdkd
        dk = dkt * pr["einv"] + dkd * pr["edec"]
        dGL = _mask_dot(m_ref[1], dkdkd) + eGL * deg
        dg = _mask_dot(m_ref[2], dG) + dGL
        df = dg / pr["f"] - dk
        sig = pr["sig"]
        df_ref[...] = (df * (1.0 - lbv) * (sig * (1.0 - sig))).astype(bf16)
        alb[...] += _colsum8(df * (1.0 - sig))
        sq = pr["sq"]
        dq_ref[...] = (dqs * (sq * (1.0 + q_raw * (1.0 - sq)))).astype(bf16)

        @pl.when(j == nsb - 1)
        def _():
            glb_ref[...] = jnp.broadcast_to(jnp.sum(alb[...], axis=0, keepdims=True), (SUBLANE, LANE))
            gnw_ref[...] = jnp.broadcast_to(jnp.sum(anw[...], axis=0, keepdims=True), (SUBLANE, LANE))

        if ride is not None:
            @pl.when(jnp.logical_and(pl.program_id(0) == 3, j == nsb - 1))
            def _():
                _chip_finish(src_ref, got_ref, sems[0], sems[1], ride[1])

    rev = lambda off: pl.BlockSpec((sb, LANE), lambda h, j: (nsb - 1 - j, off + h))
    stat = pl.BlockSpec((SUBLANE, LANE), lambda h, j: (0, h))
    riding = ride is not None
    res = pl.pallas_call(
        body,
        grid=(4, nsb),
        in_specs=[rev(0), rev(4), rev(8), rev(12), rev(0), rev(0),
                  pl.BlockSpec((1, 1, LANE, LANE), lambda h, j: (h, nsb - 1 - j, 0, 0)),
                  pl.BlockSpec((1, LANE), lambda h, j: (0, h)), pl.BlockSpec((1, LANE), lambda h, j: (0, 0)),
                  pl.BlockSpec((3, sb, sb), lambda h, j: (0, 0, 0))]
        + ([_ANY] if riding else []),
        out_specs=[rev(0), rev(0), rev(0), rev(0), stat, stat] + ([_ANY] if riding else []),
        out_shape=[SDS((S, HGRN_W), bf16)] * 4 + [SDS((SUBLANE, HGRN_W), f32)] * 2
        + ([_chip_out_shape(*ride)] if riding else []),
        scratch_shapes=[pltpu.VMEM((LANE, LANE), f32), pltpu.VMEM((SUBLANE, LANE), f32),
                        pltpu.VMEM((SUBLANE, LANE), f32)] + (list(_CHIP_SEMS) if riding else []),
        compiler_params=_cparams(("arbitrary", "arbitrary") if riding else ("parallel", "arbitrary")),
        name="hgrn_bwd",
    )(hg, hg, hg, hg, o_raw, dy, ck, lb, normw, _chunk_masks(), *([ride[0]] if riding else []))
    return tuple(res) if riding else (*res, None)


def _lb_fwd(raw):
    def body(r_ref, o_ref):
        r = r_ref[...]
        m = jnp.max(r, axis=0, keepdims=True)
        e = jnp.exp(r - m)
        o_ref[...] = (e / jnp.sum(e, axis=0, keepdims=True))[0:1]

    return pl.pallas_call(body, out_shape=SDS((1, raw.shape[1]), f32), name="lb_fwd")(raw)


def _lb_bwd(raw, dlb):
    def body(r_ref, d_ref, o_ref):
        r = r_ref[...]
        m = jnp.max(r, axis=0, keepdims=True)
        e = jnp.exp(r - m)
        s = e / jnp.sum(e, axis=0, keepdims=True)
        s0 = s[0:1]
        onehot0 = jnp.where(lax.broadcasted_iota(jnp.int32, r.shape, 0) == 0, 1.0, 0.0)
        o_ref[...] = d_ref[...] * s0 * (onehot0 - s)

    return pl.pallas_call(body, out_shape=SDS(raw.shape, f32), name="lb_bwd")(raw, dlb)


def _gate_fwd(a, b, gc):
    S, D = a.shape
    tm = _pick(S, 512)

    def body(a_ref, b_ref, g0_ref, g1_ref, o_ref):
        s0, s1 = _sigmoid(g0_ref[...].astype(f32)), _sigmoid(g1_ref[...].astype(f32))
        o_ref[...] = (s0 * a_ref[...].astype(f32) + s1 * b_ref[...].astype(f32)).astype(bf16)

    row = pl.BlockSpec((tm, D), lambda i: (i, 0))
    return pl.pallas_call(
        body,
        grid=(S // tm,),
        in_specs=[row, row, row, pl.BlockSpec((tm, D), lambda i: (i, 1))],
        out_specs=row,
        out_shape=SDS((S, D), bf16),
        compiler_params=_cparams(("parallel",)),
        name="gate_fwd",
    )(a, b, gc, gc)


def _gate_bwd(dm, a, b, gc):
    S, D = a.shape
    tm = _pick(S, 512)

    def body(dm_ref, a_ref, b_ref, g0_ref, g1_ref, da_ref, db_ref, dg_ref):
        dmv = dm_ref[...].astype(f32)
        s0, s1 = _sigmoid(g0_ref[...].astype(f32)), _sigmoid(g1_ref[...].astype(f32))
        da_ref[...] = (dmv * s0).astype(bf16)
        db_ref[...] = (dmv * s1).astype(bf16)
        dg_ref[:, :D] = (dmv * a_ref[...].astype(f32) * (s0 * (1.0 - s0))).astype(bf16)
        dg_ref[:, D:] = (dmv * b_ref[...].astype(f32) * (s1 * (1.0 - s1))).astype(bf16)

    row = pl.BlockSpec((tm, D), lambda i: (i, 0))
    wide = pl.BlockSpec((tm, 2 * D), lambda i: (i, 0))
    return pl.pallas_call(
        body,
        grid=(S // tm,),
        in_specs=[row, row, row, row, pl.BlockSpec((tm, D), lambda i: (i, 1))],
        out_specs=[row, row, wide],
        out_shape=[SDS((S, D), bf16), SDS((S, D), bf16), SDS((S, 2 * D), bf16)],
        compiler_params=_cparams(("parallel",)),
        name="gate_bwd",
    )(dm, a, b, gc, gc)


CONV_ROWS = 512
INV_SQRT2 = 0.7071067811865476
INV_SQRT_2PI = 0.3989422804014327


CONV_HALO = 16


def _tile8(a, rows):
    return jnp.tile(a, (rows // a.shape[0], 1))


def _conv_rows(u_ref, w, b, r0, first):
    R = CONV_ROWS
    cur = u_ref[pl.ds(r0, R), :].astype(f32)
    prev8 = u_ref[pl.ds(pl.multiple_of(jnp.maximum(r0 - CONV_HALO, 0), CONV_HALO), CONV_HALO), :].astype(f32)
    prev8 = jnp.where(first, 0.0, prev8)
    row = lax.broadcasted_iota(jnp.int32, (R, LANE), 0)
    x1 = jnp.where(row < 1, _tile8(pltpu.roll(prev8, 1, 0), R), pltpu.roll(cur, 1, 0))
    x2 = jnp.where(row < 2, _tile8(pltpu.roll(prev8, 2, 0), R), pltpu.roll(cur, 2, 0))
    c = ((b + w[0:1] * x2) + w[1:2] * x1) + w[2:3] * cur
    return c, x2, x1, cur


def _conv_fwd(ug, uv, wg, wv, bg, bv):
    S, F = ug.shape
    nchunk = S // CONV_ROWS

    def body(ug_ref, uv_ref, wg_ref, wv_ref, bg_ref, bv_ref, o_ref):
        wgv, wvv, bgv, bvv = wg_ref[...], wv_ref[...], bg_ref[...], bv_ref[...]

        def step(ci, carry):
            r0 = pl.multiple_of(ci * CONV_ROWS, CONV_ROWS)
            cg = _conv_rows(ug_ref, wgv, bgv, r0, ci == 0)[0]
            cv = _conv_rows(uv_ref, wvv, bvv, r0, ci == 0)[0]
            gelu = 0.5 * cg * (1.0 + lax.erf(cg * INV_SQRT2))
            o_ref[pl.ds(r0, CONV_ROWS), :] = (gelu * cv).astype(bf16)
            return carry

        lax.fori_loop(0, nchunk, step, 0)

    col = pl.BlockSpec((S, LANE), lambda j: (0, j))
    w3 = pl.BlockSpec((3, LANE), lambda j: (0, j))
    b1 = pl.BlockSpec((1, LANE), lambda j: (0, j))
    return pl.pallas_call(
        body,
        grid=(F // LANE,),
        in_specs=[col, col, w3, w3, b1, b1],
        out_specs=col,
        out_shape=SDS((S, F), bf16),
        compiler_params=_cparams(("parallel",), VMEM_BIG),
        name="conv_fwd",
    )(ug, uv, wg, wv, bg, bv)


def _conv_bwd(ug, uv, dact, wg, wv, bg, bv):
    S, F = ug.shape
    R = CONV_ROWS
    nchunk = S // R

    def body(ug_ref, uv_ref, da_ref, wg_ref, wv_ref, bg_ref, bv_ref, dug_ref, duv_ref, sg_ref, sv_ref, dcg, dcv):
        wgv, wvv, bgv, bvv = wg_ref[...], wv_ref[...], bg_ref[...], bv_ref[...]
        zero = jnp.zeros((SUBLANE, LANE), f32)

        def fwd_step(ci, acc):
            r0 = pl.multiple_of(ci * R, R)
            cg, g2, g1, g0 = _conv_rows(ug_ref, wgv, bgv, r0, ci == 0)
            cv, v2, v1, v0 = _conv_rows(uv_ref, wvv, bvv, r0, ci == 0)
            da = da_ref[pl.ds(r0, R), :].astype(f32)
            cdf = 0.5 * (1.0 + lax.erf(cg * INV_SQRT2))
            pdf = INV_SQRT_2PI * jnp.exp(-0.5 * cg * cg)
            dg = da * cv * (cdf + cg * pdf)
            dv = da * (cg * cdf)
            dcg[pl.ds(r0, R), :] = dg
            dcv[pl.ds(r0, R), :] = dv
            new = (acc[0] + _colsum8(dg * g2), acc[1] + _colsum8(dg * g1), acc[2] + _colsum8(dg * g0),
                   acc[3] + _colsum8(dg),
                   acc[4] + _colsum8(dv * v2), acc[5] + _colsum8(dv * v1), acc[6] + _colsum8(dv * v0),
                   acc[7] + _colsum8(dv))
            return new

        acc = lax.fori_loop(0, nchunk, fwd_step, (zero,) * 8)
        rows = lax.broadcasted_iota(jnp.int32, (SUBLANE, LANE), 0)

        def stats(parts):
            out = jnp.zeros((SUBLANE, LANE), f32)
            for k, pt in enumerate(parts):
                out = jnp.where(rows == k, jnp.sum(pt, axis=0, keepdims=True), out)
            return out

        sg_ref[...] = stats(acc[0:4])
        sv_ref[...] = stats(acc[4:8])

        def du_rows(dc, w, r0, last):
            cur = dc[pl.ds(r0, R), :]
            nxt = dc[pl.ds(pl.multiple_of(jnp.minimum(r0 + R, S - SUBLANE), SUBLANE), SUBLANE), :]
            nxt = jnp.where(last, 0.0, nxt)
            row = lax.broadcasted_iota(jnp.int32, (R, LANE), 0)
            y1 = jnp.where(row >= R - 1, _tile8(pltpu.roll(nxt, SUBLANE - 1, 0), R), pltpu.roll(cur, R - 1, 0))
            y2 = jnp.where(row >= R - 2, _tile8(pltpu.roll(nxt, SUBLANE - 2, 0), R), pltpu.roll(cur, R - 2, 0))
            return w[2:3] * cur + w[1:2] * y1 + w[0:1] * y2

        def bwd_step(ci, carry):
            r0 = pl.multiple_of(ci * R, R)
            last = ci == nchunk - 1
            dug_ref[pl.ds(r0, R), :] = du_rows(dcg, wgv, r0, last).astype(bf16)
            duv_ref[pl.ds(r0, R), :] = du_rows(dcv, wvv, r0, last).astype(bf16)
            return carry

        lax.fori_loop(0, nchunk, bwd_step, 0)

    col = pl.BlockSpec((S, LANE), lambda j: (0, j))
    w3 = pl.BlockSpec((3, LANE), lambda j: (0, j))
    b1 = pl.BlockSpec((1, LANE), lambda j: (0, j))
    st = pl.BlockSpec((SUBLANE, LANE), lambda j: (0, j))
    return pl.pallas_call(
        body,
        grid=(F // LANE,),
        in_specs=[col, col, col, w3, w3, b1, b1],
        out_specs=[col, col, st, st],
        out_shape=[SDS((S, F), bf16), SDS((S, F), bf16), SDS((SUBLANE, F), f32), SDS((SUBLANE, F), f32)],
        scratch_shapes=[pltpu.VMEM((S, LANE), f32), pltpu.VMEM((S, LANE), f32)],
        compiler_params=_cparams(("parallel",), VMEM_BIG),
        name="conv_bwd",
    )(ug, uv, dact, wg, wv, bg, bv)


def _adam_math(w, g, m, v):
    m = ADAM_B1 * m + (1.0 - ADAM_B1) * g
    v = ADAM_B2 * v + (1.0 - ADAM_B2) * (g * g)
    m_hat = m / (1.0 - ADAM_B1 ** ADAM_STEP)
    v_hat = v / (1.0 - ADAM_B2 ** ADAM_STEP)
    delta = -ADAM_LR * (m_hat / (jnp.sqrt(v_hat) + ADAM_EPS) + ADAM_WD * w)
    return delta, m, v


def _adamw(w, m, v, g, name):
    R, C = w.shape
    parts = g.ndim == 3
    tr = R
    for t in (256, 128, 64, 32, 16):
        if R % t == 0 and R > t:
            tr = t
            break

    def body(w_ref, m_ref, v_ref, g_ref, go_ref, d_ref, mo_ref, vo_ref):
        if parts:
            gv = ((g_ref[0].astype(f32) + g_ref[1].astype(f32)) + g_ref[2].astype(f32)) + g_ref[3].astype(f32)
        else:
            gv = g_ref[...]
        go_ref[...] = gv
        d, mn, vn = _adam_math(w_ref[...], gv, m_ref[...], v_ref[...])
        d_ref[...] = d
        mo_ref[...] = mn
        vo_ref[...] = vn

    row = pl.BlockSpec((tr, C), lambda i: (i, 0))
    gspec = pl.BlockSpec((4, tr, C), lambda i: (0, i, 0)) if parts else row
    return pl.pallas_call(
        body,
        grid=(R // tr,),
        in_specs=[row, row, row, gspec],
        out_specs=[row] * 4,
        out_shape=[SDS((R, C), f32)] * 4,
        compiler_params=_cparams(("parallel",)),
        name=name,
    )(w, m, v, g)


def _sum8(parts, name):
    _, _, R, C = parts.shape

    def body(p_ref, o_ref):
        acc = p_ref[0, 0]
        for c in range(2):
            for k in range(4):
                if c or k:
                    acc = acc + p_ref[c, k]
        o_ref[...] = acc

    return pl.pallas_call(body, out_shape=SDS((R, C), f32), name=name)(parts)


def _pair_add(by_core, b, name):
    _, K, R, C = by_core.shape
    tr = R // 2 if R % 32 == 0 else R

    def body(c_ref, a_ref, b_ref, o_ref):
        o_ref[...] = (a_ref[0].astype(f32) + b_ref[...].astype(f32)).astype(bf16)

    blk = pl.BlockSpec((1, tr, C), lambda k, i, c: (k, i, 0))
    return pl.pallas_call(
        body,
        grid_spec=pltpu.PrefetchScalarGridSpec(
            num_scalar_prefetch=1,
            grid=(K, R // tr),
            in_specs=[pl.BlockSpec((1, 1, tr, C), lambda k, i, c: (c[0], k, i, 0)), blk],
            out_specs=blk,
        ),
        out_shape=SDS((K, R, C), bf16),
        compiler_params=_cparams(("parallel", "parallel")),
        name=name,
    )(lax.axis_index("c").astype(jnp.int32).reshape(1), by_core, b)


_ANY = pl.BlockSpec(memory_space=pl.ANY)


def _chip_copies(src_ref, out_ref, send_sems, recv_sems, gather):
    x, y, c = lax.axis_index("x"), lax.axis_index("y"), lax.axis_index("c")
    mine = 2 * x + y

    def piece(k):
        return src_ref if gather else src_ref.at[k]

    sends, recvs = [], []
    for j, (px, py) in enumerate([(1 - x, y), (x, 1 - y), (1 - x, 1 - y)]):
        sends.append(pltpu.make_async_remote_copy(
            src_ref=piece(2 * px + py), dst_ref=out_ref.at[mine], send_sem=send_sems.at[j],
            recv_sem=recv_sems.at[j], device_id=(px, py, c), device_id_type=MESH))
        recvs.append(pltpu.make_async_remote_copy(
            src_ref=piece(mine), dst_ref=out_ref.at[2 * px + py], send_sem=send_sems.at[j],
            recv_sem=recv_sems.at[j], device_id=(px, py, c), device_id_type=MESH))
    return sends, recvs


def _chip_start(src_ref, out_ref, send_sems, recv_sems, gather):
    for cp in _chip_copies(src_ref, out_ref, send_sems, recv_sems, gather)[0]:
        cp.start()


def _chip_finish(src_ref, out_ref, send_sems, recv_sems, gather):
    sends, recvs = _chip_copies(src_ref, out_ref, send_sems, recv_sems, gather)
    for cp in recvs:
        cp.wait_recv()
    for cp in sends:
        cp.wait_send()


def _chip_out_shape(src, gather):
    return SDS((4,) + tuple(src.shape if gather else src.shape[1:]), src.dtype)


_CHIP_SEMS = [pltpu.SemaphoreType.DMA((3,)), pltpu.SemaphoreType.DMA((3,))]


def _fill_own(out, src, gather):
    mine = 2 * lax.axis_index("x") + lax.axis_index("y")
    own = src if gather else lax.dynamic_index_in_dim(src, mine, axis=0, keepdims=False)
    return lax.dynamic_update_index_in_dim(out, own, mine, axis=0)


def _chip_comm(src, gather, name):
    def body(src_ref, out_ref, send_sems, recv_sems):
        _chip_start(src_ref, out_ref, send_sems, recv_sems, gather)
        _chip_finish(src_ref, out_ref, send_sems, recv_sems, gather)

    out = pl.pallas_call(
        body,
        in_specs=[_ANY],
        out_specs=_ANY,
        out_shape=_chip_out_shape(src, gather),
        scratch_shapes=list(_CHIP_SEMS),
        name=name,
    )(src)
    return _fill_own(out, src, gather)


_HBM = pl.BlockSpec(memory_space=pltpu.HBM)
_SEM = pl.BlockSpec(memory_space=pltpu.SEMAPHORE)
_EFFECT = pltpu.SideEffectType.DATAFLOW_SIDE_EFFECTING
_SPLIT_PEERS = {"chip_gather": 3, "chip_xchg": 3, "core_gather": 1, "core_swap": 1}


def _split_land(src, kind):
    if kind == "core_gather":
        return SDS((2,) + tuple(src.shape), src.dtype)
    if kind == "core_swap":
        return SDS(tuple(src.shape[1:]), src.dtype)
    return _chip_out_shape(src, kind == "chip_gather")


def _split_copies(src_ref, land_ref, sems, kind):
    x, y, c = lax.axis_index("x"), lax.axis_index("y"), lax.axis_index("c")
    n = _SPLIT_PEERS[kind]
    if kind == "core_gather":
        routes = [((x, y, 1 - c), src_ref, land_ref.at[c], land_ref.at[1 - c])]
    elif kind == "core_swap":
        routes = [((x, y, 1 - c), src_ref.at[1 - c], land_ref, land_ref)]
    else:
        mine = 2 * x + y
        gather = kind == "chip_gather"
        routes = [((px, py, c), src_ref if gather else src_ref.at[2 * px + py], land_ref.at[mine],
                   land_ref.at[2 * px + py]) for px, py in [(1 - x, y), (x, 1 - y), (1 - x, 1 - y)]]
    sends, recvs = [], []
    for j, (peer, piece, there, here) in enumerate(routes):
        sends.append(pltpu.make_async_remote_copy(src_ref=piece, dst_ref=there, send_sem=sems[j],
                                                  recv_sem=sems[n + j], device_id=peer, device_id_type=MESH))
        recvs.append(pltpu.make_async_remote_copy(src_ref=piece, dst_ref=here, send_sem=sems[j],
                                                  recv_sem=sems[n + j], device_id=peer, device_id_type=MESH))
    return sends, recvs


def _split_start(src, kind, name, after=None):
    land = _split_land(src, kind)
    ns = 2 * _SPLIT_PEERS[kind]
    n_in = 2 if after is None else 3

    def body(*refs):
        src_ref, land_ref = refs[:2]
        outs = refs[n_in:]
        for cp in _split_copies(src_ref, land_ref, outs[:ns], kind)[0]:
            cp.start()
        token = outs[ns + 2]
        token[...] = jnp.zeros_like(token)

    res = pl.pallas_call(
        body,
        name=name,
        out_shape=(pltpu.SemaphoreType.DMA(()),) * ns
        + (pltpu.HBM(src.shape, src.dtype), pltpu.HBM(land.shape, land.dtype), SDS((SUBLANE, LANE), f32)),
        in_specs=(_HBM, _HBM) + (() if after is None else (_ANY,)),
        out_specs=(_SEM,) * ns + (_HBM, _HBM, pl.BlockSpec(memory_space=pltpu.VMEM)),
        input_output_aliases={0: ns, 1: ns + 1},
        compiler_params=pltpu.CompilerParams(has_side_effects=_EFFECT),
    )(pltpu.with_memory_space_constraint(src, pltpu.HBM),
      pltpu.with_memory_space_constraint(lax.empty(land.shape, land.dtype), pltpu.HBM),
      *(() if after is None else (after,)))
    return (res[:ns], res[ns], res[ns + 1]), res[ns + 2]


def _split_wait(state, after, kind, name):
    sems, src_thru, land_thru = state
    ns = 2 * _SPLIT_PEERS[kind]

    def body(src_ref, land_ref, *rest):
        sends, recvs = _split_copies(src_ref, land_ref, rest[:ns], kind)
        for cp in recvs:
            cp.wait_recv()
        for cp in sends:
            cp.wait_send()

    src_out, got = pl.pallas_call(
        body,
        name=name,
        out_shape=(pltpu.HBM(src_thru.shape, src_thru.dtype), pltpu.HBM(land_thru.shape, land_thru.dtype)),
        in_specs=(_HBM, _HBM) + (_SEM,) * ns + (_ANY,),
        out_specs=(_HBM, _HBM),
        input_output_aliases={0: 0, 1: 1},
        compiler_params=pltpu.CompilerParams(has_side_effects=_EFFECT),
    )(src_thru, land_thru, *sems, after)
    if kind == "core_swap":
        return got, src_out
    if kind == "core_gather":
        return lax.dynamic_update_index_in_dim(got, src_out, lax.axis_index("c"), axis=0)
    return _fill_own(got, src_out, kind == "chip_gather")


def _core_gather(src, name):
    def body(src_ref, out_ref, send_sem, recv_sem):
        x, y, c = lax.axis_index("x"), lax.axis_index("y"), lax.axis_index("c")
        cp = pltpu.make_async_remote_copy(src_ref=src_ref, dst_ref=out_ref.at[c], send_sem=send_sem,
                                          recv_sem=recv_sem, device_id=(x, y, 1 - c), device_id_type=MESH)
        cp.start()
        pltpu.make_async_remote_copy(src_ref=src_ref, dst_ref=out_ref.at[1 - c], send_sem=send_sem,
                                     recv_sem=recv_sem, device_id=(x, y, 1 - c), device_id_type=MESH).wait_recv()
        cp.wait_send()

    out = pl.pallas_call(
        body,
        in_specs=[_ANY],
        out_specs=_ANY,
        out_shape=SDS((2,) + tuple(src.shape), src.dtype),
        scratch_shapes=[pltpu.SemaphoreType.DMA, pltpu.SemaphoreType.DMA],
        name=name,
    )(src)
    return lax.dynamic_update_index_in_dim(out, src, lax.axis_index("c"), axis=0)


def _core_swap(src, name):
    def body(src_ref, out_ref, send_sem, recv_sem):
        x, y, c = lax.axis_index("x"), lax.axis_index("y"), lax.axis_index("c")
        cp = pltpu.make_async_remote_copy(src_ref=src_ref.at[1 - c], dst_ref=out_ref, send_sem=send_sem,
                                          recv_sem=recv_sem, device_id=(x, y, 1 - c), device_id_type=MESH)
        cp.start()
        cp.wait()

    return pl.pallas_call(
        body,
        in_specs=[_ANY],
        out_specs=_ANY,
        out_shape=SDS(tuple(src.shape[1:]), src.dtype),
        scratch_shapes=[pltpu.SemaphoreType.DMA, pltpu.SemaphoreType.DMA],
        name=name,
    )(src)


_PACK_A = (("w_in", (1088, 1024)),)
_PACK_B = (("w_ba", (512, 128)), ("w_bh", (512, 128)), ("w_out", (128, 1024)), ("w_up", (704, 1024)),
           ("w_down", (352, 1024)))
_PACK_SIZES = _PACK_A + _PACK_B
_TRANSPOSED = ("w_in", "w_up")


def _slab_rows(sizes):
    return sum(r * c for _, (r, c) in sizes) // D_MODEL


def _pack_rows(d, sizes):
    n = d[sizes[0][0]].shape[0]
    return jnp.concatenate([d[k].reshape(n, -1, D_MODEL) for k, _ in sizes], axis=1)


def _unpack_rows(slab, sizes):
    n = slab.shape[0]
    out, lo = {}, 0
    for key, (r, c) in sizes:
        rows = r * c // D_MODEL
        out[key] = slab[:, lo:lo + rows].reshape(n, r, c)
        lo += rows
    return out


def _by_core(gslab):
    return jnp.swapaxes(gslab.reshape((4, 2) + gslab.shape[1:]), 0, 1)


def _pair_sum(by_core, tag):
    return _pair_add(by_core, _core_swap(by_core, tag + "_cores"), tag + "_pair_add")


def _cols_to_full(t):
    return jnp.swapaxes(t, 0, 1).reshape(t.shape[1], -1)


def _full_to_cols(t):
    K = t.shape[0]
    return jnp.swapaxes(t.reshape(K, 8, -1), 0, 1)


_SMALL = (("pre_mix_norm", (1, 1024)), ("rel_bias", (32, 24)), ("hgrn_lb_raw", (2, 512)), ("hgrn_norm", (1, 128)),
          ("post_mix_norm", (1, 1024)), ("pre_ffn_norm", (1, 1024)), ("conv_b", (1, 5632)),
          ("post_ffn_norm", (1, 1024)))
_SMALL_ROWS = 96
_CONVW_ROWS = 136


_SMALL_USED = sum(r * c for _, (r, c) in _SMALL)


def _pack_small(d, extra=None):
    flat = jnp.concatenate([d[k].reshape(-1) for k, _ in _SMALL] + ([] if extra is None else [extra.reshape(-1)]))
    flat = jnp.pad(flat, (0, _SMALL_ROWS * LANE - flat.shape[0]))
    return flat.reshape(_SMALL_ROWS, LANE)


def _unpack_small(p):
    flat = p.reshape(-1)
    out, lo = {}, 0
    for k, shp in _SMALL:
        n = shp[0] * shp[1]
        out[k] = flat[lo:lo + n].reshape(shp)
        lo += n
    return out


def _local_step(x, tgt, P, plan):
    S = x.shape[0]
    P = dict(P)
    lb = _lb_fwd(P["hgrn_lb_raw"])
    hs = _prep(x, P["pre_mix_norm"], plan.start_token())
    h1 = hs[0]
    consts = [_bias_consts(d) for d in DILATIONS]
    biases, dep = [], h1
    for g in range(N_GROUPS):
        tab_t = P["rel_bias"][:, 8 * g:8 * g + 8].T
        dep = _bias_build(tab_t, consts[g][0], consts[g][1], f"bias_build{g}", dep)
        biases.append(dep.reshape(8, ATTN_BLOCK, 2 * ATTN_BLOCK))
    W = dict(plan.weights_a(dep))
    qkv = [_mm(hs[g], W["wt_qkv"][g], "nt", bf16, f"proj_qkv{g}") for g in range(N_GROUPS)]
    hg = _mm(h1, W["wt_hg"], "nt", f32, "proj_hg")
    gc = _mm(h1, W["wt_gate"], "nt", bf16, "proj_gate")
    token = plan.forward_b(gc)
    obuf, lbuf = [], []
    for g, d in enumerate(DILATIONS):
        o_g, l_g = _attn_fwd(qkv[g], biases[g], (S // d) // ATTN_BLOCK, f"attn_fwd{g}", after=token)
        lbuf.append(l_g)
        obuf.append(o_g)
    y_attn, y_attn_b, w0, w1, w2 = _attn_merge(obuf[0], obuf[1], obuf[2], lbuf[0], lbuf[1], lbuf[2])
    y_hgrn, o_raw, ck, _ = _hgrn_fwd(hg, lb, P["hgrn_norm"])
    wb = plan.weights_b(y_hgrn)
    P["conv_w"] = wb.pop("conv_w")
    W.update(wb)
    a = _mm(y_attn_b, W["w_ba"], "nn", bf16, "branch_attn")
    b = _mm(y_hgrn, W["w_bh"], "nn", bf16, "branch_hgrn")
    merged = _gate_fwd(a, b, gc)
    mo, x1, h2 = _mid_fwd(x, merged, W["w_out"], P["post_mix_norm"], P["pre_ffn_norm"])
    ug = _mm(h2, W["wt_up_g"], "nt", bf16, "up_gate")
    uv = _mm(h2, W["wt_up_v"], "nt", bf16, "up_val")
    cw_g, cw_v = P["conv_w"][:, :D_FF], P["conv_w"][:, D_FF:]
    cb_g, cb_v = P["conv_b"][:, :D_FF], P["conv_b"][:, D_FF:]
    act = _conv_fwd(ug, uv, cw_g, cw_v, cb_g, cb_v)
    loss, dy, dfo, g_post_ffn = _final(x1, act, W["w_down"], tgt, P["post_ffn_norm"])
    dact = _mm(dfo, W["w_down"], "nt", bf16, "d_act")
    gW_down = _mm(act, dfo, "tn", f32, "gw_down")
    dug, duv, st_g, st_v = _conv_bwd(ug, uv, dact, cw_g, cw_v, cb_g, cb_v)
    dh2 = _mm([dug, duv], [W["wt_up_g"], W["wt_up_v"]], "nn", f32, "dh2")
    gW_up_g = _mm(dug, h2, "tn", f32, "gw_up_gate")
    gW_up_v = _mm(duv, h2, "tn", f32, "gw_up_val")
    dx1, dmo, g_pre_ffn, g_post_mix = _mid_bwd(dy, dh2, x1, mo, P["pre_ffn_norm"], P["post_mix_norm"])
    dmerged = _mm(dmo, W["w_out"], "nt", bf16, "d_merged")
    gW_out = _mm(merged, dmo, "tn", f32, "gw_out")
    da, db, dgc = _gate_bwd(dmerged, a, b, gc)
    dyattn = _mm(da, W["w_ba"], "nt", f32, "d_yattn")
    gW_ba = _mm(y_attn_b, da, "tn", f32, "gw_ba")
    dyhgrn = _mm(db, W["w_bh"], "nt", f32, "d_yhgrn")
    gW_bh = _mm(y_hgrn, db, "tn", f32, "gw_bh")
    big_b = dict(w_ba=gW_ba, w_bh=gW_bh, w_out=gW_out, w_up=[gW_up_g, gW_up_v], w_down=gW_down)
    dq_h, df_h, dv_h, dog_h, glb8, gnw8, got_b = _hgrn_bwd(hg, o_raw, dyhgrn, ck, lb, P["hgrn_norm"],
                                                          plan.bwd_ride(big_b))
    dhg = [dq_h, df_h, dv_h, dog_h]
    g_lb_raw = _lb_bwd(P["hgrn_lb_raw"], glb8[0:1])
    gn = gnw8[0:1]
    g_hgrn_norm = (gn[:, 0:128] + gn[:, 128:256]) + (gn[:, 256:384] + gn[:, 384:512])
    dos = _attn_merge_bwd(dyattn, y_attn, w0, w1, w2)
    dqkvs, gW_qkv, g_rel = [], [], []
    for g, d in enumerate(DILATIONS):
        dq, dk, dv, dbias = _attn_bwd(qkv[g], biases[g], dos[g], dos[3 + g], lbuf[g], (S // d) // ATTN_BLOCK,
                                      f"attn_bwd{g}")
        dqkvs.append([dq, dk, dv])
        gW_qkv.append(_mm(dqkvs[g], hs[g], "tn", f32, f"gw_qkv{g}"))
        g_rel.append(_bias_grad(dbias.reshape(8, -1), consts[g][0], f"bias_grad{g}"))
    gW_hg = _mm(dhg, h1, "tn", f32, "gw_hg")
    gW_gate = _mm(dgc, h1, "tn", f32, "gw_gate")
    gW_in = gW_qkv + [gW_hg, gW_gate]
    token = plan.grads_a_start(gW_in)
    dh_perm = [_mm(dqkvs[g], W["wt_qkv"][g], "nn", f32, f"dh1_qkv{g}", after=token) for g in (1, 2)]
    token = plan.grads_a_exchange(dh_perm[1])
    dh_main = _mm(dqkvs[0] + dhg + [dgc], [W["wt_qkv"][0], W["wt_hg"], W["wt_gate"]], "nn", f32, "dh1_main",
                  after=token)
    grad_x, g_pre_mix = _first_bwd(x, dx1, _dh_sum(dh_main, dh_perm[0], dh_perm[1]), P["pre_mix_norm"])

    g_conv_w = jnp.concatenate([st_g[0:3], st_v[0:3]], axis=1)
    g_conv_b = jnp.concatenate([st_g[3:4], st_v[3:4]], axis=1)
    small = dict(pre_mix_norm=g_pre_mix, rel_bias=jnp.concatenate(g_rel, axis=1), hgrn_lb_raw=g_lb_raw,
                 hgrn_norm=g_hgrn_norm, post_mix_norm=g_post_mix, pre_ffn_norm=g_pre_ffn, conv_b=g_conv_b,
                 post_ffn_norm=g_post_ffn, conv_w=g_conv_w)
    return loss, grad_x, gW_in, big_b, got_b, small


def _weights_a(both):
    wt = jnp.swapaxes(both, 0, 1).reshape(-1, D_MODEL)
    return dict(
        wt_qkv=[wt[g * QKV_G:(g + 1) * QKV_G] for g in range(N_GROUPS)],
        wt_hg=wt[3 * QKV_G:3 * QKV_G + 4 * HGRN_W],
        wt_gate=wt[3 * QKV_G + 4 * HGRN_W:],
    )


def _weights_b(slabs):
    sh = _unpack_rows(slabs, _PACK_B)
    wt_up = sh["w_up"].reshape(-1, D_MODEL)
    return dict(
        w_ba=_cols_to_full(sh["w_ba"]),
        w_bh=_cols_to_full(sh["w_bh"]),
        w_out=sh["w_out"].reshape(D_MODEL, D_MODEL),
        wt_up_g=wt_up[:D_FF],
        wt_up_v=wt_up[D_FF:],
        w_down=sh["w_down"].reshape(D_FF, D_MODEL),
    )


def _dest_rows(sections, height):
    out = []
    for j in range(8):
        lo, hi, off, pieces = j * height, (j + 1) * height, 0, []
        for s in sections:
            a, b = max(lo, off), min(hi, off + s.shape[0])
            if a < b:
                pieces.append(s[a - off:b - off])
            off += s.shape[0]
        out.append(pieces[0] if len(pieces) == 1 else jnp.concatenate(pieces, axis=0))
    return out


def _grad_blocks_a(sections):
    rows = _dest_rows(sections, 1088)
    return jnp.stack([jnp.stack([rows[2 * k + c].astype(bf16) for k in range(4)]) for c in range(2)])


def _grad_slab_b(g):
    shards = dict(w_ba=_full_to_cols(g["w_ba"]), w_bh=_full_to_cols(g["w_bh"]), w_out=g["w_out"].reshape(8, 128, D_MODEL),
                  w_up=jnp.stack(_dest_rows(g["w_up"], 704)), w_down=g["w_down"].reshape(8, 352, D_MODEL))
    return _pack_rows({k: v.astype(bf16) for k, v in shards.items()}, _PACK_B)


_CONVW_SLAB_ROWS = 16


class _Traffic:
    def __init__(self, slab_a, slab_b, conv_w):
        hi = conv_w.astype(bf16)
        r1 = conv_w - hi.astype(f32)
        mid = r1.astype(bf16)
        lo = (r1 - mid.astype(f32)).astype(bf16)
        bits = jnp.stack([hi, mid, lo]).reshape(-1)
        tail = jnp.pad(bits, (0, _CONVW_SLAB_ROWS * D_MODEL - bits.shape[0])).reshape(_CONVW_SLAB_ROWS, D_MODEL)
        self.slab_b = jnp.concatenate([slab_b, tail], axis=0)
        self.state_a, tok = _split_start(slab_a, "chip_gather", "ag_a_start")
        self.state_b, self.token = _split_start(self.slab_b, "chip_gather", "ag_b_start", after=tok)
        self.chip_sum = None
        self.state = None

    def start_token(self):
        return self.token

    def weights_a(self, after):
        by_chip = _split_wait(self.state_a, after, "chip_gather", "ag_a_wait")
        return _weights_a(_core_gather(by_chip, "ag_a_cores"))

    def forward_b(self, after):
        by_chip = _split_wait(self.state_b, after, "chip_gather", "ag_b_wait")
        self.state, token = _split_start(by_chip, "core_gather", "ag_b_cores_start")
        return token

    def weights_b(self, after):
        both = _split_wait(self.state, after, "core_gather", "ag_b_cores_wait")
        slabs = jnp.swapaxes(both, 0, 1).reshape((8,) + tuple(self.slab_b.shape))
        rows = _slab_rows(_PACK_B)
        out = _weights_b(slabs[:, :rows])
        pieces = slabs[:, rows:].reshape(8, -1)[:, :3 * 3 * 704].reshape(8, 3, 3, 704).astype(f32)
        out["conv_w"] = _cols_to_full((pieces[:, 0] + pieces[:, 1]) + pieces[:, 2])
        return out

    def bwd_ride(self, grads):
        self.chip_sum = _pair_sum(_by_core(_grad_slab_b(grads)), "rs_b")
        return (self.chip_sum, False)

    def grads_a_start(self, sections):
        self.state, token = _split_start(_grad_blocks_a(sections), "core_swap", "rs_a_cores_start")
        return token

    def grads_a_exchange(self, after):
        from_sib, by_core = _split_wait(self.state, after, "core_swap", "rs_a_cores_wait")
        self.state, token = _split_start(_pair_add(by_core, from_sib, "rs_a_pair_add"), "chip_xchg", "rs_a_start")
        return token

    def parts(self, got_b, after):
        parts = _unpack_rows(_fill_own(got_b, self.chip_sum, False), _PACK_B)
        parts["w_in"] = _split_wait(self.state, after, "chip_xchg", "rs_a_wait")
        return parts


def kernel(x, pre_mix_norm, w_in, rel_bias, hgrn_lb_raw, hgrn_norm, w_branch_attn, w_branch_hgrn, w_out, post_mix_norm, pre_ffn_norm, w_up, conv_w, conv_b, w_down, post_ffn_norm, loss_target, m_pre_mix_norm, m_w_in, m_rel_bias, m_hgrn_lb_raw, m_hgrn_norm, m_w_branch_attn, m_w_branch_hgrn, m_w_out, m_post_mix_norm, m_pre_ffn_norm, m_w_up, m_conv_w, m_conv_b, m_w_down, m_post_ffn_norm, v_pre_mix_norm, v_w_in, v_rel_bias, v_hgrn_lb_raw, v_hgrn_norm, v_w_branch_attn, v_w_branch_hgrn, v_w_out, v_post_mix_norm, v_pre_ffn_norm, v_w_up, v_conv_w, v_conv_b, v_w_down, v_post_ffn_norm):
    ci = lax.axis_index("c")
    dev = 4 * lax.axis_index("x") + 2 * lax.axis_index("y") + ci
    tr = lambda t: jnp.swapaxes(t[0], 0, 1)
    wts = dict(w_in=tr(w_in), w_ba=w_branch_attn[0], w_bh=w_branch_hgrn[0], w_out=w_out[0], w_up=tr(w_up),
               w_down=w_down[0])
    mom = dict(w_in=tr(m_w_in), w_ba=m_w_branch_attn[0], w_bh=m_w_branch_hgrn[0], w_out=m_w_out[0], w_up=tr(m_w_up),
               w_down=m_w_down[0])
    var = dict(w_in=tr(v_w_in), w_ba=v_w_branch_attn[0], w_bh=v_w_branch_hgrn[0], w_out=v_w_out[0], w_up=tr(v_w_up),
               w_down=v_w_down[0])
    small_w = dict(pre_mix_norm=pre_mix_norm, rel_bias=rel_bias, hgrn_lb_raw=hgrn_lb_raw, hgrn_norm=hgrn_norm,
                   post_mix_norm=post_mix_norm, pre_ffn_norm=pre_ffn_norm, conv_b=conv_b, post_ffn_norm=post_ffn_norm)
    small_m = dict(pre_mix_norm=m_pre_mix_norm, rel_bias=m_rel_bias, hgrn_lb_raw=m_hgrn_lb_raw, hgrn_norm=m_hgrn_norm,
                   post_mix_norm=m_post_mix_norm, pre_ffn_norm=m_pre_ffn_norm, conv_b=m_conv_b,
                   post_ffn_norm=m_post_ffn_norm)
    small_v = dict(pre_mix_norm=v_pre_mix_norm, rel_bias=v_rel_bias, hgrn_lb_raw=v_hgrn_lb_raw, hgrn_norm=v_hgrn_norm,
                   post_mix_norm=v_post_mix_norm, pre_ffn_norm=v_pre_ffn_norm, conv_b=v_conv_b,
                   post_ffn_norm=v_post_ffn_norm)

    plan = _Traffic(wts["w_in"].astype(bf16),
                    _pack_rows({k: wts[k].astype(bf16)[None] for k, _ in _PACK_B}, _PACK_B)[0], conv_w[0])

    loss8, grad_x, _, _, got_b, small = _local_step(x[0], loss_target[0], small_w, plan)
    parts = plan.parts(got_b, grad_x)
    outs_big = {}
    for k, _ in _PACK_SIZES:
        outs_big[k] = _adamw(wts[k], mom[k], var[k], parts[k], "adamw_" + k)

    spack = jnp.concatenate([_pack_small(small, loss8[0, 0:1]),
                             jnp.pad(small["conv_w"].reshape(-1, LANE), ((0, _CONVW_ROWS - 132), (0, 0)))], axis=0)
    allp = _core_gather(_chip_comm(spack, True, "ag_small_chips"), "ag_small_cores")
    ssum = _sum8(allp, "small_sum")
    gs = ssum[:_SMALL_ROWS]
    loss = ssum[_SMALL_USED // LANE, _SMALL_USED % LANE]
    res_small = _adamw(_pack_small(small_w), _pack_small(small_m), _pack_small(small_v), gs, "adamw_small")
    sm = [_unpack_small(t) for t in res_small]
    g_cw_full = ssum[_SMALL_ROWS:_SMALL_ROWS + 132].reshape(3, 2 * D_FF)
    g_cw = lax.dynamic_slice_in_dim(g_cw_full, dev * 704, 704, axis=1)
    res_cw = _adamw(conv_w[0], m_conv_w[0], v_conv_w[0], g_cw, "adamw_conv_w")

    def pick(i):
        def big_(k):
            t = outs_big[k][i]
            return (jnp.swapaxes(t, 0, 1) if k in _TRANSPOSED else t)[None]
        return [sm[i]["pre_mix_norm"], big_("w_in"), sm[i]["rel_bias"], sm[i]["hgrn_lb_raw"], sm[i]["hgrn_norm"],
                big_("w_ba"), big_("w_bh"), big_("w_out"), sm[i]["post_mix_norm"], sm[i]["pre_ffn_norm"],
                big_("w_up"), res_cw[i][None], sm[i]["conv_b"], big_("w_down"), sm[i]["post_ffn_norm"]]

    return (loss, grad_x[None], *pick(0), *pick(1), *pick(2), *pick(3))
```

```python
import functools
import math

import jax
import jax.numpy as jnp
from jax import lax
from jax.experimental import pallas as pl
from jax.experimental.pallas import tpu as pltpu

f32 = jnp.float32
bf16 = jnp.bfloat16
SDS = jax.ShapeDtypeStruct
HIGHEST = lax.Precision.HIGHEST
MESH = pl.DeviceIdType.MESH

NN = (((1,), (0,)), ((), ()))
NT = (((1,), (1,)), ((), ()))
TN = (((0,), (0,)), ((), ()))

D_MODEL = 1024
N_GROUPS = 3
DILATIONS = (1, 4, 16)
HEAD_DIM = 64
ATTN_BLOCK = 128
QKV_G = 1536
ATTN_OUT = 512
HGRN_W = 512
HGRN_CHUNK = 32
D_FF = 2816
NUM_BUCKETS = 32
MAX_EXACT = 16
MAX_DISTANCE = 2048
NEG_INF = -1e30
EPS = 1e-6
LANE = 128
SUBLANE = 8
VMEM_BIG = 48 * 1024 * 1024
MM_ROWS = 512
MM_OUT_BYTES = 8 * 1024 * 1024

ADAM_LR, ADAM_B1, ADAM_B2, ADAM_EPS, ADAM_WD, ADAM_STEP = 0.001, 0.9, 0.999, 1e-08, 0.01, 10


def _pick(n, pref):
    t = pref
    while t >= LANE:
        if n % t == 0:
            return t
        t //= 2
    return n


def _cparams(sem=None, vmem=None):
    kw = {}
    if sem is not None:
        kw["dimension_semantics"] = sem
    if vmem is not None:
        kw["vmem_limit_bytes"] = vmem
    return pltpu.CompilerParams(**kw)


def _sigmoid(x):
    return jax.nn.sigmoid(x)


def _colsum8(x):
    return x.reshape(x.shape[0] // SUBLANE, SUBLANE, x.shape[1]).sum(axis=0)


def _mm(a, b, mode, out_dtype, name, acc=None, after=None):
    dims = {"nn": NN, "nt": NT, "tn": TN}[mode]
    has_acc = acc is not None
    parts = list(a) if isinstance(a, (list, tuple)) else [a]
    if mode == "tn":
        assert not has_acc
        K, N = b.shape
        widths = [t.shape[1] for t in parts]
        M = sum(widths)
        whole = M * N * 4 <= MM_OUT_BYTES
        assert whole or len(parts) == 1
        tmm = M if whole else M // 2
        ts = _pick(K, 2 * MM_ROWS)
        nk = K // ts

        def body_tn(*refs):
            b_ref, o_ref = refs[-2], refs[-1]
            k = pl.program_id(1)
            bv = b_ref[...]
            lo = 0
            for a_ref, w in zip(refs[:-2], widths if whole else [tmm]):
                part = lax.dot_general(a_ref[...], bv, dims, preferred_element_type=f32)
                rows = slice(lo, lo + w)
                lo += w

                @pl.when(k == 0)
                def _(part=part, rows=rows):
                    o_ref[rows, :] = part

                @pl.when(k > 0)
                def _(part=part, rows=rows):
                    o_ref[rows, :] += part

        return pl.pallas_call(
            body_tn,
            grid=(M // tmm, nk),
            in_specs=[pl.BlockSpec((ts, w if whole else tmm), lambda i, k: (k, i)) for w in widths]
            + [pl.BlockSpec((ts, N), lambda i, k: (k, 0))],
            out_specs=pl.BlockSpec((tmm, N), lambda i, k: (i, 0)),
            out_shape=SDS((M, N), out_dtype),
            compiler_params=_cparams(("parallel", "arbitrary"), VMEM_BIG),
            name=name,
        )(*parts, b)

    bs = list(b) if isinstance(b, (list, tuple)) else [b]
    widths = [t.shape[1] for t in parts]
    M = parts[0].shape[0]
    kdim = 0 if mode == "nn" else 1
    N = bs[0].shape[1 - kdim]
    tm = _pick(M, MM_ROWS)
    npart, nb = len(parts), len(bs)
    place, bi, lo = [], 0, 0
    for w in widths:
        place.append((bi, lo))
        lo += w
        if lo == bs[bi].shape[kdim]:
            bi, lo = bi + 1, 0
    assert bi == nb and lo == 0

    def body(*refs):
        a_refs, b_refs = refs[:npart], refs[npart:npart + nb]
        c_ref = refs[npart + nb] if has_acc else None
        o_ref = refs[-1]
        part = None
        for a_ref, w, (bi, lo) in zip(a_refs, widths, place):
            b_ref = b_refs[bi]
            if w == bs[bi].shape[kdim]:
                bk = b_ref[...]
            else:
                bk = b_ref[:, lo:lo + w] if mode == "nt" else b_ref[lo:lo + w, :]
            t = lax.dot_general(a_ref[...], bk, dims, preferred_element_type=f32)
            part = t if part is None else part + t
        if has_acc:
            part = part + c_ref[...]
        o_ref[...] = part.astype(out_dtype)

    specs = [pl.BlockSpec((tm, w), lambda i: (i, 0)) for w in widths] \
        + [pl.BlockSpec(t.shape, lambda i: (0, 0)) for t in bs]
    args = parts + bs
    aliases = {}
    if has_acc:
        specs.append(pl.BlockSpec((tm, N), lambda i: (i, 0)))
        args.append(acc)
        aliases = {npart + nb: 0}
    if after is not None:
        specs.append(pl.BlockSpec(memory_space=pl.ANY))
        args.append(after)
    return pl.pallas_call(
        body,
        grid=(M // tm,),
        in_specs=specs,
        out_specs=pl.BlockSpec((tm, N), lambda i: (i, 0)),
        out_shape=SDS((M, N), out_dtype),
        input_output_aliases=aliases,
        compiler_params=_cparams(("parallel",), VMEM_BIG),
        name=name,
    )(*args)


PERM_ROWS = 1024


def _perm_spec(d, cols=LANE):
    return pl.BlockSpec((d, PERM_ROWS // d, cols), lambda i, j: (0, i, j))


def _to_natural(src_ref, dst_ref, d):
    n = src_ref.shape[1]
    for r in range(d):
        dst_ref[pl.ds(r, n, stride=d), :] = src_ref[r]


def _prep(x, w, after=None):
    S, D = x.shape
    R = PERM_ROWS
    nc = D // LANE
    n_in = nc + 1 + (after is not None)

    def body(*refs):
        x_refs, w_ref = refs[:nc], refs[nc]
        h_ref, h4_ref, h16_ref, rs = refs[n_in:]
        ssq = None
        for xr in x_refs:
            v = xr[...]
            t = jnp.sum(v * v, axis=-1, keepdims=True)
            ssq = t if ssq is None else ssq + t
        rinv = lax.rsqrt(ssq * (1.0 / D) + EPS)
        rs[...] = jnp.broadcast_to(rinv, (R, LANE))
        for j, xr in enumerate(x_refs):
            cols = slice(j * LANE, (j + 1) * LANE)
            wj = w_ref[:, cols]
            h_ref[:, cols] = ((xr[...] * rinv) * wj).astype(bf16)
            for d, o_ref in ((4, h4_ref), (16, h16_ref)):
                n = R // d
                for r in range(d):
                    rows = pl.ds(r, n, stride=d)
                    o_ref[r, :, cols] = ((xr[rows, :] * rs[rows, :]) * wj).astype(bf16)

    col = lambda j: pl.BlockSpec((R, LANE), lambda i, j=j: (i, j))
    h, h4, h16 = pl.pallas_call(
        body,
        grid=(S // R,),
        in_specs=[col(j) for j in range(nc)] + [pl.BlockSpec((1, D), lambda i: (0, 0))]
        + ([] if after is None else [pl.BlockSpec(memory_space=pl.ANY)]),
        out_specs=[pl.BlockSpec((R, D), lambda i: (i, 0)), pl.BlockSpec((4, R // 4, D), lambda i: (0, i, 0)),
                   pl.BlockSpec((16, R // 16, D), lambda i: (0, i, 0))],
        out_shape=[SDS((S, D), bf16), SDS((4, S // 4, D), bf16), SDS((16, S // 16, D), bf16)],
        scratch_shapes=[pltpu.VMEM((R, LANE), f32)],
        compiler_params=_cparams(("parallel",), VMEM_BIG),
        name="prep_norm_perm",
    )(*([x] * nc), w, *([] if after is None else [after]))
    return [h, h4.reshape(S, D), h16.reshape(S, D)]


def _dh_sum(a, b, c):
    S, D = a.shape
    R = PERM_ROWS

    def body(a_ref, b_ref, c_ref, o_ref, sb, sc):
        _to_natural(b_ref, sb, 4)
        _to_natural(c_ref, sc, 16)
        o_ref[...] = (a_ref[...] + sb[...]) + sc[...]

    nat = pl.BlockSpec((R, LANE), lambda i, j: (i, j))
    return pl.pallas_call(
        body,
        grid=(S // R, D // LANE),
        in_specs=[nat, _perm_spec(4), _perm_spec(16)],
        out_specs=nat,
        out_shape=SDS((S, D), f32),
        scratch_shapes=[pltpu.VMEM((R, LANE), f32)] * 2,
        compiler_params=_cparams(("parallel", "parallel")),
        name="dh_sum",
    )(a, b.reshape(4, S // 4, D), c.reshape(16, S // 16, D))


def _rms_parts(xv):
    r = lax.rsqrt(jnp.mean(xv * xv, axis=-1, keepdims=True) + EPS)
    return r, xv * r


def _rms_bwd(xhat, r, w, dy):
    dyw = dy * w
    return r * (dyw - xhat * jnp.mean(dyw * xhat, axis=-1, keepdims=True))


def _mid_fwd(x, merged, w_out, w_pm, w_pf):
    S, D = x.shape
    tm = _pick(S, MM_ROWS)

    def body(x_ref, m_ref, wo_ref, wpm_ref, wpf_ref, mo_ref, x1_ref, h2_ref):
        mo = jnp.dot(m_ref[...], wo_ref[...], preferred_element_type=f32)
        mo_ref[...] = mo
        _, moh = _rms_parts(mo)
        x1 = x_ref[...] + moh * wpm_ref[...]
        x1_ref[...] = x1
        _, x1h = _rms_parts(x1)
        h2_ref[...] = (x1h * wpf_ref[...]).astype(bf16)

    row = pl.BlockSpec((tm, D), lambda i: (i, 0))
    vec = pl.BlockSpec((1, D), lambda i: (0, 0))
    return pl.pallas_call(
        body,
        grid=(S // tm,),
        in_specs=[row, pl.BlockSpec((tm, merged.shape[1]), lambda i: (i, 0)),
                  pl.BlockSpec(w_out.shape, lambda i: (0, 0)), vec, vec],
        out_specs=[row, row, row],
        out_shape=[SDS((S, D), f32), SDS((S, D), f32), SDS((S, D), bf16)],
        compiler_params=_cparams(("parallel",), VMEM_BIG),
        name="out_proj_mid_fwd",
    )(x, merged, w_out, w_pm, w_pf)


def _final(x1, act, w_down, tgt, w_pfn):
    S, D = x1.shape
    tm = _pick(S, MM_ROWS)
    nt = S // tm

    def body(x1_ref, a_ref, wd_ref, t_ref, w_ref, loss_ref, dy_ref, dfo_ref, gw_ref, lacc, gacc):
        i = pl.program_id(0)

        @pl.when(i == 0)
        def _():
            lacc[...] = jnp.zeros_like(lacc)
            gacc[...] = jnp.zeros_like(gacc)

        w = w_ref[...]
        r, foh = _rms_parts(jnp.dot(a_ref[...], wd_ref[...], preferred_element_type=f32))
        y = x1_ref[...] + foh * w
        err = y - t_ref[...]
        lacc[...] += _colsum8(err * err)
        dy = err * (1.0 / D)
        dy_ref[...] = dy
        gacc[...] += _colsum8(dy * foh)
        dfo_ref[...] = _rms_bwd(foh, r, w, dy).astype(bf16)

        @pl.when(i == nt - 1)
        def _():
            loss_ref[...] = jnp.full((SUBLANE, LANE), 0.5 / D, f32) * jnp.sum(lacc[...])
            gw_ref[...] = jnp.sum(gacc[...], axis=0, keepdims=True)

    row = pl.BlockSpec((tm, D), lambda i: (i, 0))
    vec = pl.BlockSpec((1, D), lambda i: (0, 0))
    return pl.pallas_call(
        body,
        grid=(nt,),
        in_specs=[row, pl.BlockSpec((tm, act.shape[1]), lambda i: (i, 0)),
                  pl.BlockSpec(w_down.shape, lambda i: (0, 0)), row, vec],
        out_specs=[pl.BlockSpec((SUBLANE, LANE), lambda i: (0, 0)), row, row, vec],
        out_shape=[SDS((SUBLANE, LANE), f32), SDS((S, D), f32), SDS((S, D), bf16), SDS((1, D), f32)],
        scratch_shapes=[pltpu.VMEM((SUBLANE, D), f32), pltpu.VMEM((SUBLANE, D), f32)],
        compiler_params=_cparams(("arbitrary",), VMEM_BIG),
        name="down_proj_final_loss",
    )(x1, act, w_down, tgt, w_pfn)


def _mid_bwd(dy, dh2, x1, mo, w_pf, w_pm):
    S, D = dy.shape
    tm = _pick(S, 512)
    nt = S // tm

    def body(dy_ref, dh2_ref, x1_ref, mo_ref, wpf_ref, wpm_ref, dx1_ref, dmo_ref, gpf_ref, gpm_ref, apf, apm):
        i = pl.program_id(0)

        @pl.when(i == 0)
        def _():
            apf[...] = jnp.zeros_like(apf)
            apm[...] = jnp.zeros_like(apm)

        r1, x1h = _rms_parts(x1_ref[...])
        dh2 = dh2_ref[...]
        apf[...] += _colsum8(dh2 * x1h)
        dx1 = dy_ref[...] + _rms_bwd(x1h, r1, wpf_ref[...], dh2)
        dx1_ref[...] = dx1
        rm, moh = _rms_parts(mo_ref[...])
        apm[...] += _colsum8(dx1 * moh)
        dmo_ref[...] = _rms_bwd(moh, rm, wpm_ref[...], dx1).astype(bf16)

        @pl.when(i == nt - 1)
        def _():
            gpf_ref[...] = jnp.sum(apf[...], axis=0, keepdims=True)
            gpm_ref[...] = jnp.sum(apm[...], axis=0, keepdims=True)

    row = pl.BlockSpec((tm, D), lambda i: (i, 0))
    vec = pl.BlockSpec((1, D), lambda i: (0, 0))
    return pl.pallas_call(
        body,
        grid=(nt,),
        in_specs=[row, row, row, row, vec, vec],
        out_specs=[row, row, vec, vec],
        out_shape=[SDS((S, D), f32), SDS((S, D), bf16), SDS((1, D), f32), SDS((1, D), f32)],
        scratch_shapes=[pltpu.VMEM((SUBLANE, D), f32), pltpu.VMEM((SUBLANE, D), f32)],
        compiler_params=_cparams(("arbitrary",)),
        name="mid_bwd",
    )(dy, dh2, x1, mo, w_pf, w_pm)


def _first_bwd(x, dx1, dh, w_pre):
    S, D = x.shape
    tm = _pick(S, 512)
    nt = S // tm

    def body(x_ref, dx1_ref, a_ref, w_ref, gx_ref, gw_ref, acc):
        i = pl.program_id(0)

        @pl.when(i == 0)
        def _():
            acc[...] = jnp.zeros_like(acc)

        r, xh = _rms_parts(x_ref[...])
        dh = a_ref[...]
        acc[...] += _colsum8(dh * xh)
        gx_ref[...] = dx1_ref[...] + _rms_bwd(xh, r, w_ref[...], dh)

        @pl.when(i == nt - 1)
        def _():
            gw_ref[...] = jnp.sum(acc[...], axis=0, keepdims=True)

    row = pl.BlockSpec((tm, D), lambda i: (i, 0))
    vec = pl.BlockSpec((1, D), lambda i: (0, 0))
    return pl.pallas_call(
        body,
        grid=(nt,),
        in_specs=[row, row, row, vec],
        out_specs=[row, vec],
        out_shape=[SDS((S, D), f32), SDS((1, D), f32)],
        scratch_shapes=[pltpu.VMEM((SUBLANE, D), f32)],
        compiler_params=_cparams(("arbitrary",)),
        name="first_bwd",
    )(x, dx1, dh, w_pre)


def _t5_bucket(dist):
    n = jnp.maximum(dist, 0)
    nf = jnp.maximum(n, 1).astype(f32)
    large = MAX_EXACT + (jnp.log(nf / MAX_EXACT) / math.log(MAX_DISTANCE / MAX_EXACT)
                         * (NUM_BUCKETS - MAX_EXACT)).astype(jnp.int32)
    large = jnp.minimum(large, NUM_BUCKETS - 1)
    return jnp.where(n < MAX_EXACT, n, large)


def _bias_consts(d):
    blk = ATTN_BLOCK
    rel = jnp.arange(blk)[:, None] + blk - jnp.arange(2 * blk)[None, :]
    in_win = (rel >= 0) & (rel <= blk)
    bucket = _t5_bucket(rel * d).reshape(1, -1)
    onehot = (bucket == jnp.arange(NUM_BUCKETS)[:, None]).astype(f32)
    return onehot, in_win.astype(f32).reshape(1, -1)


def _bias_build(tab_t, onehot, maskf, name, after):
    H = tab_t.shape[0]

    def body(t_ref, oh_ref, m_ref, after_ref, o_ref):
        b = jnp.dot(t_ref[...], oh_ref[...], precision=HIGHEST, preferred_element_type=f32)
        o_ref[...] = jnp.where(m_ref[...] > 0.5, b, NEG_INF)

    vm = pl.BlockSpec(memory_space=pltpu.VMEM)
    return pl.pallas_call(body, out_shape=SDS((H, onehot.shape[1]), f32), name=name,
                          in_specs=[vm, vm, vm, pl.BlockSpec(memory_space=pl.ANY)], out_specs=vm,
                          )(tab_t, onehot, maskf, after)


def _bias_grad(dbias_flat, onehot, name):
    H = dbias_flat.shape[0]

    def body(g_ref, oh_ref, o_ref):
        o_ref[...] = lax.dot_general(oh_ref[...], g_ref[...], NT, precision=HIGHEST, preferred_element_type=f32)

    return pl.pallas_call(body, out_shape=SDS((NUM_BUCKETS, H), f32), name=name)(dbias_flat, onehot)


ATTN_TILE = 512
ATTN_SUB = ATTN_TILE // ATTN_BLOCK


def _qkv_specs(nt):
    tile = (ATTN_TILE, LANE)
    blk = (ATTN_BLOCK, LANE)
    cur = lambda off: (lambda h, t: (jnp.minimum(t, nt - 1), off + h))
    prev = lambda off: (lambda h, t: (jnp.maximum(jnp.minimum(t, nt - 1) * ATTN_SUB - 1, 0), off + h))
    return [pl.BlockSpec(tile, cur(0)), pl.BlockSpec(blk, prev(4)), pl.BlockSpec(tile, cur(4)),
            pl.BlockSpec(blk, prev(8)), pl.BlockSpec(tile, cur(8))]


def _head_masks():
    lane = lax.broadcasted_iota(jnp.int32, (ATTN_BLOCK, LANE), 1)
    return lane < HEAD_DIM


def _stack_heads(x2, low):
    zero = jnp.zeros_like(x2)
    return jnp.concatenate([jnp.where(low, x2, zero), jnp.where(low, zero, x2)], axis=0)


def _attn_fwd(qkv, bias, bps, name, after=None):
    S = qkv.shape[0]
    nt = S // ATTN_TILE
    scale = HEAD_DIM ** -0.5

    def body(q_ref, kp_ref, kc_ref, vp_ref, vc_ref, b_ref, *rest):
        o_ref, l_ref = rest[-2:]
        t = pl.program_id(1)
        kk = jnp.concatenate([kp_ref[...], kc_ref[...]], axis=0)
        vv = jnp.concatenate([vp_ref[...], vc_ref[...]], axis=0)
        low = _head_masks()
        col = lax.broadcasted_iota(jnp.int32, (2 * ATTN_BLOCK, 2 * ATTN_BLOCK), 1)
        bias2 = b_ref[...].reshape(2 * ATTN_BLOCK, 2 * ATTN_BLOCK)
        for b in range(ATTN_SUB):
            lo = b * ATTN_BLOCK
            rows = slice(lo, lo + ATTN_BLOCK)
            keys = slice(lo, lo + 2 * ATTN_BLOCK)
            dead = jnp.logical_and((t * ATTN_SUB + b) % bps == 0, col < ATTN_BLOCK)
            q2 = _stack_heads(q_ref[rows, :], low)
            kb, vb = kk[keys], vv[keys]
            s = lax.dot_general(q2, kb, NT, preferred_element_type=f32) * scale + bias2
            s = jnp.where(dead, NEG_INF, s)
            m = jnp.max(s, axis=-1, keepdims=True)
            p = jnp.exp(s - m)
            l = jnp.sum(p, axis=-1, keepdims=True)
            o2 = jnp.dot(p.astype(bf16), vb, preferred_element_type=f32) / l
            lse = m + jnp.log(l)
            o_ref[rows, :] = jnp.where(low, o2[:ATTN_BLOCK], o2[ATTN_BLOCK:])
            l_ref[rows, :] = jnp.where(low, lse[:ATTN_BLOCK], lse[ATTN_BLOCK:])

    tile = pl.BlockSpec((ATTN_TILE, LANE), lambda h, t: (t, h))
    return pl.pallas_call(
        body,
        grid=(4, nt),
        in_specs=_qkv_specs(nt) + [pl.BlockSpec((2, ATTN_BLOCK, 2 * ATTN_BLOCK), lambda h, t: (h, 0, 0))]
        + ([] if after is None else [pl.BlockSpec(memory_space=pl.ANY)]),
        out_specs=[tile, tile],
        out_shape=[SDS((S, ATTN_OUT), f32), SDS((S, ATTN_OUT), f32)],
        compiler_params=_cparams(("parallel", "parallel")),
        name=name,
    )(qkv, qkv, qkv, qkv, qkv, bias, *([] if after is None else [after]))


def _attn_bwd(qkv, bias, do, dvec, lse, bps, name):
    S = qkv.shape[0]
    nt = S // ATTN_TILE
    scale = HEAD_DIM ** -0.5

    def assemble(parts):
        rows = [parts[0][:ATTN_BLOCK]]
        for b in range(ATTN_SUB - 1):
            rows.append(parts[b][ATTN_BLOCK:] + parts[b + 1][:ATTN_BLOCK])
        rows.append(parts[-1][ATTN_BLOCK:])
        return rows

    def body(q_ref, kp_ref, kc_ref, vp_ref, vc_ref, b_ref, do_ref, dvec_ref, lse_ref,
             dq_ref, dk_ref, dv_ref, db_ref, ck, cv):
        t = pl.program_id(1)
        last = ATTN_TILE - ATTN_BLOCK

        @pl.when(t == 0)
        def _():
            ck[...] = jnp.zeros_like(ck)
            cv[...] = jnp.zeros_like(cv)
            db_ref[...] = jnp.zeros_like(db_ref)

        @pl.when(t < nt)
        def _():
            kk = jnp.concatenate([kp_ref[...], kc_ref[...]], axis=0)
            vv = jnp.concatenate([vp_ref[...], vc_ref[...]], axis=0)
            low = _head_masks()
            col = lax.broadcasted_iota(jnp.int32, (2 * ATTN_BLOCK, 2 * ATTN_BLOCK), 1)
            bias2 = b_ref[...].reshape(2 * ATTN_BLOCK, 2 * ATTN_BLOCK)
            dk_parts, dv_parts = [], []
            dsum = None
            for b in range(ATTN_SUB):
                lo = b * ATTN_BLOCK
                rows = slice(lo, lo + ATTN_BLOCK)
                keys = slice(lo, lo + 2 * ATTN_BLOCK)
                dead = jnp.logical_and((t * ATTN_SUB + b) % bps == 0, col < ATTN_BLOCK)
                q2 = _stack_heads(q_ref[rows, :], low)
                do2 = _stack_heads(do_ref[rows, :].astype(bf16), low)
                kb, vb = kk[keys], vv[keys]
                dvec2 = dvec_ref[rows, :]
                lse2 = lse_ref[rows, :]
                per_row = lambda t2: jnp.concatenate([t2[:, 0:1], t2[:, HEAD_DIM:HEAD_DIM + 1]], axis=0)
                s = lax.dot_general(q2, kb, NT, preferred_element_type=f32) * scale + bias2
                s = jnp.where(dead, NEG_INF, s)
                p = jnp.exp(s - per_row(lse2))
                dp = lax.dot_general(do2, vb, NT, preferred_element_type=f32)
                ds = p * (dp - per_row(dvec2))
                dsum = ds if dsum is None else dsum + ds
                dsb = ds.astype(bf16)
                dq2 = jnp.dot(dsb, kb, preferred_element_type=f32) * scale
                dq_ref[rows, :] = jnp.where(low, dq2[:ATTN_BLOCK], dq2[ATTN_BLOCK:]).astype(bf16)
                dk_parts.append(lax.dot_general(dsb, q2, TN, preferred_element_type=f32) * scale)
                dv_parts.append(lax.dot_general(p.astype(bf16), do2, TN, preferred_element_type=f32))
            db_ref[...] += dsum.reshape(2, ATTN_BLOCK, 2 * ATTN_BLOCK)
            for parts, carry, out_ref in ((dk_parts, ck, dk_ref), (dv_parts, cv, dv_ref)):
                rws = assemble(parts)
                out_ref[:last, :] = carry[:last, :].astype(bf16)
                out_ref[last:, :] = (carry[last:, :] + rws[0]).astype(bf16)
                for b in range(ATTN_SUB):
                    carry[b * ATTN_BLOCK:(b + 1) * ATTN_BLOCK, :] = rws[b + 1]

        @pl.when(t == nt)
        def _():
            dk_ref[...] = ck[...].astype(bf16)
            dv_ref[...] = cv[...].astype(bf16)

    tile = (ATTN_TILE, LANE)
    cur = pl.BlockSpec(tile, lambda h, t: (jnp.minimum(t, nt - 1), h))
    lag = pl.BlockSpec(tile, lambda h, t: (jnp.maximum(t - 1, 0), h))
    bspec = pl.BlockSpec((2, ATTN_BLOCK, 2 * ATTN_BLOCK), lambda h, t: (h, 0, 0))
    return pl.pallas_call(
        body,
        grid=(4, nt + 1),
        in_specs=_qkv_specs(nt) + [bspec, cur, cur, cur],
        out_specs=[cur, lag, lag, bspec],
        out_shape=[SDS((S, ATTN_OUT), bf16), SDS((S, ATTN_OUT), bf16), SDS((S, ATTN_OUT), bf16),
                   SDS((8, ATTN_BLOCK, 2 * ATTN_BLOCK), f32)],
        scratch_shapes=[pltpu.VMEM(tile, f32), pltpu.VMEM(tile, f32)],
        compiler_params=_cparams(("parallel", "arbitrary")),
        name=name,
    )(qkv, qkv, qkv, qkv, qkv, bias, do, dvec, lse)


def _attn_merge(o0, o1, o2, l0, l1, l2):
    S, W = o0.shape
    R = PERM_ROWS

    def body(o0_ref, o1_ref, o2_ref, l0_ref, l1_ref, l2_ref, y_ref, yb_ref, w0_ref, w1_ref, w2_ref,
             so1, so2, sl1, sl2):
        _to_natural(o1_ref, so1, 4)
        _to_natural(l1_ref, sl1, 4)
        _to_natural(o2_ref, so2, 16)
        _to_natural(l2_ref, sl2, 16)
        a, b, c = l0_ref[...], sl1[...], sl2[...]
        m = jnp.maximum(jnp.maximum(a, b), c)
        ea, eb, ec = jnp.exp(a - m), jnp.exp(b - m), jnp.exp(c - m)
        den = (ea + eb) + ec
        w0, w1, w2 = ea / den, eb / den, ec / den
        y = (w0 * o0_ref[...] + w1 * so1[...]) + w2 * so2[...]
        y_ref[...] = y
        yb_ref[...] = y.astype(bf16)
        w0_ref[...] = w0
        w1_ref[...] = w1
        w2_ref[...] = w2

    nat = pl.BlockSpec((R, LANE), lambda i, j: (i, j))
    v4 = lambda t: t.reshape(4, S // 4, W)
    v16 = lambda t: t.reshape(16, S // 16, W)
    return pl.pallas_call(
        body,
        grid=(S // R, W // LANE),
        in_specs=[nat, _perm_spec(4), _perm_spec(16)] * 2,
        out_specs=[nat] * 5,
        out_shape=[SDS((S, W), f32), SDS((S, W), bf16)] + [SDS((S, W), f32)] * 3,
        scratch_shapes=[pltpu.VMEM((R, LANE), f32)] * 4,
        compiler_params=_cparams(("parallel", "parallel")),
        name="attn_merge",
    )(o0, v4(o1), v16(o2), l0, v4(l1), v16(l2))


def _attn_merge_bwd(dy, y, w0, w1, w2, after=None):
    S, W = dy.shape
    R = PERM_ROWS

    def body(dy_ref, y_ref, w0_ref, w1_ref, w2_ref, *rest):
        a0, a1, a2, b0, b1, b2, sa, sb = rest[-8:]
        dyv = dy_ref[...]
        r = lax.broadcasted_iota(jnp.int32, (LANE, LANE), 0) // HEAD_DIM
        c = lax.broadcasted_iota(jnp.int32, (LANE, LANE), 1) // HEAD_DIM
        seg = jnp.where(r == c, 1.0, 0.0).astype(f32)
        cbar = jnp.dot(dyv * y_ref[...], seg, precision=HIGHEST, preferred_element_type=f32)
        w = w0_ref[...]
        a0[...] = (w * dyv).astype(bf16)
        b0[...] = w * cbar
        for d, w_ref, a_ref, b_ref in ((4, w1_ref, a1, b1), (16, w2_ref, a2, b2)):
            w = w_ref[...]
            sa[...] = w * dyv
            sb[...] = w * cbar
            n = R // d
            for k in range(d):
                rows = pl.ds(k, n, stride=d)
                a_ref[k] = sa[rows, :].astype(bf16)
                b_ref[k] = sb[rows, :]

    nat = pl.BlockSpec((R, LANE), lambda i, j: (i, j))
    shapes = lambda dt: [SDS((S, W), dt), SDS((4, S // 4, W), dt), SDS((16, S // 16, W), dt)]
    outs = pl.pallas_call(
        body,
        grid=(S // R, W // LANE),
        in_specs=[nat] * 5 + ([] if after is None else [pl.BlockSpec(memory_space=pl.ANY)]),
        out_specs=[nat, _perm_spec(4), _perm_spec(16)] * 2,
        out_shape=shapes(bf16) + shapes(f32),
        scratch_shapes=[pltpu.VMEM((R, LANE), f32)] * 2,
        compiler_params=_cparams(("parallel", "parallel")),
        name="attn_merge_bwd",
    )(dy, y, w0, w1, w2, *([] if after is None else [after]))
    return [t.reshape(S, W) for t in outs]


HGRN_SB = 256


def _chunk_masks():
    r = jnp.arange(HGRN_SB)[:, None]
    c = jnp.arange(HGRN_SB)[None, :]
    same = (r // HGRN_CHUNK) == (c // HGRN_CHUNK)
    return jnp.stack([same & (c <= r), same, same & (c >= r)]).astype(bf16)


def _mask_dot(mask, x):
    hi = x.astype(bf16)
    r1 = x - hi.astype(f32)
    mid = r1.astype(bf16)
    lo = (r1 - mid.astype(f32)).astype(bf16)
    p = jnp.dot(mask, jnp.concatenate([hi, mid, lo], axis=1), preferred_element_type=f32)
    n = x.shape[1]
    return (p[:, :n] + p[:, n:2 * n]) + p[:, 2 * n:]


def _hgrn_prep(q_raw, f_raw, lbv, tril, same):
    sq = _sigmoid(q_raw)
    qs = q_raw * sq
    sig = _sigmoid(f_raw)
    f = lbv + (1.0 - lbv) * sig
    g = jnp.log(f)
    k = 1.0 - f
    G = _mask_dot(tril, g)
    GL = _mask_dot(same, g)
    eG = jnp.exp(G)
    einv = jnp.exp(-G)
    edec = jnp.exp(GL - G)
    return dict(sq=sq, qs=qs, sig=sig, f=f, k=k, eG=eG, einv=einv, edec=edec, eGL=jnp.exp(GL),
                qt=qs * eG, kt=k * einv, kd=k * edec)


def _ride_split(ride, rest, n_out, n_scratch):
    if ride is None:
        return None, rest[:n_out], None, rest[n_out:], None
    return rest[0], rest[1:1 + n_out], rest[1 + n_out], rest[2 + n_out:2 + n_out + n_scratch], rest[2 + n_out + n_scratch:]


def _hgrn_fwd(hg, lb, normw, ride=None):
    S = hg.shape[0]
    sb = HGRN_SB
    nsb = S // sb
    nch = sb // HGRN_CHUNK

    def body(q_ref, f_ref, v_ref, og_ref, lb_ref, nw_ref, m_ref, *rest):
        src_ref, (y_ref, o_ref, ck_ref), got_ref, (st,), sems = _ride_split(ride, rest, 3, 1)
        j = pl.program_id(1)
        if ride is not None:
            @pl.when(jnp.logical_and(pl.program_id(0) == 0, j == 0))
            def _():
                _chip_start(src_ref, got_ref, sems[0], sems[1], ride[1])

        @pl.when(j == 0)
        def _():
            st[...] = jnp.zeros_like(st)

        ST = st[...]
        ck_ref[0, 0] = ST
        tril_m = m_ref[0]
        tril = tril_m.astype(f32) > 0.5
        pr = _hgrn_prep(q_ref[...], f_ref[...], lb_ref[...], tril_m, m_ref[1])
        qtb, ktb, kdb = pr["qt"].astype(bf16), pr["kt"].astype(bf16), pr["kd"].astype(bf16)
        eGL = pr["eGL"]
        vb = v_ref[...].astype(bf16)
        A = jnp.where(tril, lax.dot_general(qtb, ktb, NT, preferred_element_type=f32), 0.0)
        o = jnp.dot(A.astype(bf16), vb, preferred_element_type=f32)
        outs = []
        for ci in range(nch):
            lo = ci * HGRN_CHUNK
            sl = slice(lo, lo + HGRN_CHUNK)
            outs.append(o[sl] + lax.dot_general(qtb[sl], ST.astype(bf16), NT, preferred_element_type=f32))
            ST = ST * eGL[lo:lo + 1, :] + lax.dot_general(vb[sl], kdb[sl], TN, preferred_element_type=f32)
        st[...] = ST
        of = jnp.concatenate(outs, axis=0)
        o_ref[...] = of
        rms = lax.rsqrt(jnp.mean(of * of, axis=-1, keepdims=True) + EPS)
        ogv = og_ref[...]
        y_ref[...] = ((of * rms * nw_ref[...]) * (ogv * _sigmoid(ogv))).astype(bf16)

        if ride is not None:
            @pl.when(jnp.logical_and(pl.program_id(0) == 3, j == nsb - 1))
            def _():
                _chip_finish(src_ref, got_ref, sems[0], sems[1], ride[1])

    col = lambda off: pl.BlockSpec((sb, LANE), lambda h, j: (j, off + h))
    riding = ride is not None
    res = pl.pallas_call(
        body,
        grid=(4, nsb),
        in_specs=[col(0), col(4), col(8), col(12), pl.BlockSpec((1, LANE), lambda h, j: (0, h)),
                  pl.BlockSpec((1, LANE), lambda h, j: (0, 0)),
                  pl.BlockSpec((3, sb, sb), lambda h, j: (0, 0, 0))] + ([_ANY] if riding else []),
        out_specs=[col(0), col(0), pl.BlockSpec((1, 1, LANE, LANE), lambda h, j: (h, j, 0, 0))]
        + ([_ANY] if riding else []),
        out_shape=[SDS((S, HGRN_W), bf16), SDS((S, HGRN_W), f32), SDS((4, nsb, LANE, LANE), f32)]
        + ([_chip_out_shape(*ride)] if riding else []),
        scratch_shapes=[pltpu.VMEM((LANE, LANE), f32)] + (list(_CHIP_SEMS) if riding else []),
        compiler_params=_cparams(("arbitrary", "arbitrary") if riding else ("parallel", "arbitrary")),
        name="hgrn_fwd",
    )(hg, hg, hg, hg, lb, normw, _chunk_masks(), *([ride[0]] if riding else []))
    return tuple(res) if riding else (*res, None)


def _hgrn_bwd(hg, o_raw, dy, ck, lb, normw, ride=None):
    S = hg.shape[0]
    sb = HGRN_SB
    nsb = S // sb
    nch = sb // HGRN_CHUNK

    def body(q_ref, f_ref, v_ref, og_ref, o_ref, dy_ref, ck_ref, lb_ref, nw_ref, m_ref, *rest):
        src_ref, outs, got_ref, (dst, alb, anw), sems = _ride_split(ride, rest, 6, 3)
        dq_ref, df_ref, dv_ref, dog_ref, glb_ref, gnw_ref = outs
        j = pl.program_id(1)
        if ride is not None:
            @pl.when(jnp.logical_and(pl.program_id(0) == 0, j == 0))
            def _():
                _chip_start(src_ref, got_ref, sems[0], sems[1], ride[1])

        @pl.when(j == 0)
        def _():
            dst[...] = jnp.zeros_like(dst)
            alb[...] = jnp.zeros_like(alb)
            anw[...] = jnp.zeros_like(anw)

        tril_m = m_ref[0]
        tril = tril_m.astype(f32) > 0.5
        lbv = lb_ref[...]
        q_raw = q_ref[...]
        pr = _hgrn_prep(q_raw, f_ref[...], lbv, tril_m, m_ref[1])
        qt, kt, kd, eGL = pr["qt"], pr["kt"], pr["kd"], pr["eGL"]
        qtb, ktb, kdb = qt.astype(bf16), kt.astype(bf16), kd.astype(bf16)
        vb = v_ref[...].astype(bf16)

        o = o_ref[...]
        ogv = og_ref[...]
        sog = _sigmoid(ogv)
        rms = lax.rsqrt(jnp.mean(o * o, axis=-1, keepdims=True) + EPS)
        oh = o * rms
        nw = nw_ref[...]
        dyv = dy_ref[...]
        dog_ref[...] = (dyv * (oh * nw) * (sog * (1.0 + ogv * (1.0 - sog)))).astype(bf16)
        dohw = dyv * (ogv * sog)
        anw[...] += _colsum8(dohw * oh)
        doh = dohw * nw
        do = rms * (doh - oh * jnp.mean(doh * oh, axis=-1, keepdims=True))
        dob = do.astype(bf16)

        Ab = jnp.where(tril, lax.dot_general(qtb, ktb, NT, preferred_element_type=f32), 0.0).astype(bf16)
        dAb = jnp.where(tril, lax.dot_general(dob, vb, NT, preferred_element_type=f32), 0.0).astype(bf16)
        dv_acc = lax.dot_general(Ab, dob, TN, preferred_element_type=f32)
        dqt = jnp.dot(dAb, ktb, preferred_element_type=f32)
        dkt = lax.dot_general(dAb, qtb, TN, preferred_element_type=f32)

        ST = ck_ref[0, 0]
        states = []
        for ci in range(nch):
            lo = ci * HGRN_CHUNK
            sl = slice(lo, lo + HGRN_CHUNK)
            states.append(ST)
            ST = ST * eGL[lo:lo + 1, :] + lax.dot_general(vb[sl], kdb[sl], TN, preferred_element_type=f32)

        dST = dst[...]
        dqt_i, dkd_i, dv_i, deg_i = [None] * nch, [None] * nch, [None] * nch, [None] * nch
        for ci in reversed(range(nch)):
            lo = ci * HGRN_CHUNK
            sl = slice(lo, lo + HGRN_CHUNK)
            ST0 = states[ci]
            dSTb = dST.astype(bf16)
            dv_i[ci] = lax.dot_general(kdb[sl], dSTb, NT, preferred_element_type=f32)
            dqt_i[ci] = jnp.dot(dob[sl], ST0.astype(bf16), preferred_element_type=f32)
            dkd_i[ci] = jnp.dot(vb[sl], dSTb, preferred_element_type=f32)
            deg_i[ci] = jnp.broadcast_to(jnp.sum(dST * ST0, axis=0, keepdims=True), (HGRN_CHUNK, LANE))
            dST = dST * eGL[lo:lo + 1, :] + lax.dot_general(dob[sl], qtb[sl], TN, preferred_element_type=f32)
        dst[...] = dST

        dqt = dqt + jnp.concatenate(dqt_i, axis=0)
        dkd = jnp.concatenate(dkd_i, axis=0)
        dv_ref[...] = (dv_acc + jnp.concatenate(dv_i, axis=0)).astype(bf16)
        deg = jnp.concatenate(deg_i, axis=0)

        dqs = dqt * pr["eG"]
        dkdkd = dkd * kd
        dG = dqt * qt - dkt * kt - dkdkd
        dk = dkt * pr["einv"] + dkd * pr["edec"]
        dGL = _mask_dot(m_ref[1], dkdkd) + eGL * deg
        dg = _mask_dot(m_ref[2], dG) + dGL
        df = dg / pr["f"] - dk
        sig = pr["sig"]
        df_ref[...] = (df * (1.0 - lbv) * (sig * (1.0 - sig))).astype(bf16)
        alb[...] += _colsum8(df * (1.0 - sig))
        sq = pr["sq"]
        dq_ref[...] = (dqs * (sq * (1.0 + q_raw * (1.0 - sq)))).astype(bf16)

        @pl.when(j == nsb - 1)
        def _():
            glb_ref[...] = jnp.broadcast_to(jnp.sum(alb[...], axis=0, keepdims=True), (SUBLANE, LANE))
            gnw_ref[...] = jnp.broadcast_to(jnp.sum(anw[...], axis=0, keepdims=True), (SUBLANE, LANE))

        if ride is not None:
            @pl.when(jnp.logical_and(pl.program_id(0) == 3, j == nsb - 1))
            def _():
                _chip_finish(src_ref, got_ref, sems[0], sems[1], ride[1])

    rev = lambda off: pl.BlockSpec((sb, LANE), lambda h, j: (nsb - 1 - j, off + h))
    stat = pl.BlockSpec((SUBLANE, LANE), lambda h, j: (0, h))
    riding = ride is not None
    res = pl.pallas_call(
        body,
        grid=(4, nsb),
        in_specs=[rev(0), rev(4), rev(8), rev(12), rev(0), rev(0),
                  pl.BlockSpec((1, 1, LANE, LANE), lambda h, j: (h, nsb - 1 - j, 0, 0)),
                  pl.BlockSpec((1, LANE), lambda h, j: (0, h)), pl.BlockSpec((1, LANE), lambda h, j: (0, 0)),
                  pl.BlockSpec((3, sb, sb), lambda h, j: (0, 0, 0))]
        + ([_ANY] if riding else []),
        out_specs=[rev(0), rev(0), rev(0), rev(0), stat, stat] + ([_ANY] if riding else []),
        out_shape=[SDS((S, HGRN_W), bf16)] * 4 + [SDS((SUBLANE, HGRN_W), f32)] * 2
        + ([_chip_out_shape(*ride)] if riding else []),
        scratch_shapes=[pltpu.VMEM((LANE, LANE), f32), pltpu.VMEM((SUBLANE, LANE), f32),
                        pltpu.VMEM((SUBLANE, LANE), f32)] + (list(_CHIP_SEMS) if riding else []),
        compiler_params=_cparams(("arbitrary", "arbitrary") if riding else ("parallel", "arbitrary")),
        name="hgrn_bwd",
    )(hg, hg, hg, hg, o_raw, dy, ck, lb, normw, _chunk_masks(), *([ride[0]] if riding else []))
    return tuple(res) if riding else (*res, None)


def _lb_fwd(raw):
    def body(r_ref, o_ref):
        r = r_ref[...]
        m = jnp.max(r, axis=0, keepdims=True)
        e = jnp.exp(r - m)
        o_ref[...] = (e / jnp.sum(e, axis=0, keepdims=True))[0:1]

    return pl.pallas_call(body, out_shape=SDS((1, raw.shape[1]), f32), name="lb_fwd")(raw)


def _lb_bwd(raw, dlb):
    def body(r_ref, d_ref, o_ref):
        r = r_ref[...]
        m = jnp.max(r, axis=0, keepdims=True)
        e = jnp.exp(r - m)
        s = e / jnp.sum(e, axis=0, keepdims=True)
        s0 = s[0:1]
        onehot0 = jnp.where(lax.broadcasted_iota(jnp.int32, r.shape, 0) == 0, 1.0, 0.0)
        o_ref[...] = d_ref[...] * s0 * (onehot0 - s)

    return pl.pallas_call(body, out_shape=SDS(raw.shape, f32), name="lb_bwd")(raw, dlb)


def _gate_fwd(a, b, gc):
    S, D = a.shape
    tm = _pick(S, 512)

    def body(a_ref, b_ref, g0_ref, g1_ref, o_ref):
        s0, s1 = _sigmoid(g0_ref[...].astype(f32)), _sigmoid(g1_ref[...].astype(f32))
        o_ref[...] = (s0 * a_ref[...].astype(f32) + s1 * b_ref[...].astype(f32)).astype(bf16)

    row = pl.BlockSpec((tm, D), lambda i: (i, 0))
    return pl.pallas_call(
        body,
        grid=(S // tm,),
        in_specs=[row, row, row, pl.BlockSpec((tm, D), lambda i: (i, 1))],
        out_specs=row,
        out_shape=SDS((S, D), bf16),
        compiler_params=_cparams(("parallel",)),
        name="gate_fwd",
    )(a, b, gc, gc)


def _gate_bwd(dm, a, b, gc):
    S, D = a.shape
    tm = _pick(S, 512)

    def body(dm_ref, a_ref, b_ref, g0_ref, g1_ref, da_ref, db_ref, dg_ref):
        dmv = dm_ref[...].astype(f32)
        s0, s1 = _sigmoid(g0_ref[...].astype(f32)), _sigmoid(g1_ref[...].astype(f32))
        da_ref[...] = (dmv * s0).astype(bf16)
        db_ref[...] = (dmv * s1).astype(bf16)
        dg_ref[:, :D] = (dmv * a_ref[...].astype(f32) * (s0 * (1.0 - s0))).astype(bf16)
        dg_ref[:, D:] = (dmv * b_ref[...].astype(f32) * (s1 * (1.0 - s1))).astype(bf16)

    row = pl.BlockSpec((tm, D), lambda i: (i, 0))
    wide = pl.BlockSpec((tm, 2 * D), lambda i: (i, 0))
    return pl.pallas_call(
        body,
        grid=(S // tm,),
        in_specs=[row, row, row, row, pl.BlockSpec((tm, D), lambda i: (i, 1))],
        out_specs=[row, row, wide],
        out_shape=[SDS((S, D), bf16), SDS((S, D), bf16), SDS((S, 2 * D), bf16)],
        compiler_params=_cparams(("parallel",)),
        name="gate_bwd",
    )(dm, a, b, gc, gc)


CONV_ROWS = 512
INV_SQRT2 = 0.7071067811865476
INV_SQRT_2PI = 0.3989422804014327


CONV_HALO = 16


def _tile8(a, rows):
    return jnp.tile(a, (rows // a.shape[0], 1))


def _conv_rows(u_ref, w, b, r0, first):
    R = CONV_ROWS
    cur = u_ref[pl.ds(r0, R), :].astype(f32)
    prev8 = u_ref[pl.ds(pl.multiple_of(jnp.maximum(r0 - CONV_HALO, 0), CONV_HALO), CONV_HALO), :].astype(f32)
    prev8 = jnp.where(first, 0.0, prev8)
    row = lax.broadcasted_iota(jnp.int32, (R, LANE), 0)
    x1 = jnp.where(row < 1, _tile8(pltpu.roll(prev8, 1, 0), R), pltpu.roll(cur, 1, 0))
    x2 = jnp.where(row < 2, _tile8(pltpu.roll(prev8, 2, 0), R), pltpu.roll(cur, 2, 0))
    c = ((b + w[0:1] * x2) + w[1:2] * x1) + w[2:3] * cur
    return c, x2, x1, cur


def _conv_fwd(ug, uv, wg, wv, bg, bv):
    S, F = ug.shape
    nchunk = S // CONV_ROWS

    def body(ug_ref, uv_ref, wg_ref, wv_ref, bg_ref, bv_ref, o_ref):
        wgv, wvv, bgv, bvv = wg_ref[...], wv_ref[...], bg_ref[...], bv_ref[...]

        def step(ci, carry):
            r0 = pl.multiple_of(ci * CONV_ROWS, CONV_ROWS)
            cg = _conv_rows(ug_ref, wgv, bgv, r0, ci == 0)[0]
            cv = _conv_rows(uv_ref, wvv, bvv, r0, ci == 0)[0]
            gelu = 0.5 * cg * (1.0 + lax.erf(cg * INV_SQRT2))
            o_ref[pl.ds(r0, CONV_ROWS), :] = (gelu * cv).astype(bf16)
            return carry

        lax.fori_loop(0, nchunk, step, 0)

    col = pl.BlockSpec((S, LANE), lambda j: (0, j))
    w3 = pl.BlockSpec((3, LANE), lambda j: (0, j))
    b1 = pl.BlockSpec((1, LANE), lambda j: (0, j))
    return pl.pallas_call(
        body,
        grid=(F // LANE,),
        in_specs=[col, col, w3, w3, b1, b1],
        out_specs=col,
        out_shape=SDS((S, F), bf16),
        compiler_params=_cparams(("parallel",), VMEM_BIG),
        name="conv_fwd",
    )(ug, uv, wg, wv, bg, bv)


def _conv_bwd(ug, uv, dact, wg, wv, bg, bv):
    S, F = ug.shape
    R = CONV_ROWS
    nchunk = S // R

    def body(ug_ref, uv_ref, da_ref, wg_ref, wv_ref, bg_ref, bv_ref, dug_ref, duv_ref, sg_ref, sv_ref, dcg, dcv):
        wgv, wvv, bgv, bvv = wg_ref[...], wv_ref[...], bg_ref[...], bv_ref[...]
        zero = jnp.zeros((SUBLANE, LANE), f32)

        def fwd_step(ci, acc):
            r0 = pl.multiple_of(ci * R, R)
            cg, g2, g1, g0 = _conv_rows(ug_ref, wgv, bgv, r0, ci == 0)
            cv, v2, v1, v0 = _conv_rows(uv_ref, wvv, bvv, r0, ci == 0)
            da = da_ref[pl.ds(r0, R), :].astype(f32)
            cdf = 0.5 * (1.0 + lax.erf(cg * INV_SQRT2))
            pdf = INV_SQRT_2PI * jnp.exp(-0.5 * cg * cg)
            dg = da * cv * (cdf + cg * pdf)
            dv = da * (cg * cdf)
            dcg[pl.ds(r0, R), :] = dg
            dcv[pl.ds(r0, R), :] = dv
            new = (acc[0] + _colsum8(dg * g2), acc[1] + _colsum8(dg * g1), acc[2] + _colsum8(dg * g0),
                   acc[3] + _colsum8(dg),
                   acc[4] + _colsum8(dv * v2), acc[5] + _colsum8(dv * v1), acc[6] + _colsum8(dv * v0),
                   acc[7] + _colsum8(dv))
            return new

        acc = lax.fori_loop(0, nchunk, fwd_step, (zero,) * 8)
        rows = lax.broadcasted_iota(jnp.int32, (SUBLANE, LANE), 0)

        def stats(parts):
            out = jnp.zeros((SUBLANE, LANE), f32)
            for k, pt in enumerate(parts):
                out = jnp.where(rows == k, jnp.sum(pt, axis=0, keepdims=True), out)
            return out

        sg_ref[...] = stats(acc[0:4])
        sv_ref[...] = stats(acc[4:8])

        def du_rows(dc, w, r0, last):
            cur = dc[pl.ds(r0, R), :]
            nxt = dc[pl.ds(pl.multiple_of(jnp.minimum(r0 + R, S - SUBLANE), SUBLANE), SUBLANE), :]
            nxt = jnp.where(last, 0.0, nxt)
            row = lax.broadcasted_iota(jnp.int32, (R, LANE), 0)
            y1 = jnp.where(row >= R - 1, _tile8(pltpu.roll(nxt, SUBLANE - 1, 0), R), pltpu.roll(cur, R - 1, 0))
            y2 = jnp.where(row >= R - 2, _tile8(pltpu.roll(nxt, SUBLANE - 2, 0), R), pltpu.roll(cur, R - 2, 0))
            return w[2:3] * cur + w[1:2] * y1 + w[0:1] * y2

        def bwd_step(ci, carry):
            r0 = pl.multiple_of(ci * R, R)
            last = ci == nchunk - 1
            dug_ref[pl.ds(r0, R), :] = du_rows(dcg, wgv, r0, last).astype(bf16)
            duv_ref[pl.ds(r0, R), :] = du_rows(dcv, wvv, r0, last).astype(bf16)
            return carry

        lax.fori_loop(0, nchunk, bwd_step, 0)

    col = pl.BlockSpec((S, LANE), lambda j: (0, j))
    w3 = pl.BlockSpec((3, LANE), lambda j: (0, j))
    b1 = pl.BlockSpec((1, LANE), lambda j: (0, j))
    st = pl.BlockSpec((SUBLANE, LANE), lambda j: (0, j))
    return pl.pallas_call(
        body,
        grid=(F // LANE,),
        in_specs=[col, col, col, w3, w3, b1, b1],
        out_specs=[col, col, st, st],
        out_shape=[SDS((S, F), bf16), SDS((S, F), bf16), SDS((SUBLANE, F), f32), SDS((SUBLANE, F), f32)],
        scratch_shapes=[pltpu.VMEM((S, LANE), f32), pltpu.VMEM((S, LANE), f32)],
        compiler_params=_cparams(("parallel",), VMEM_BIG),
        name="conv_bwd",
    )(ug, uv, dact, wg, wv, bg, bv)


def _adam_math(w, g, m, v):
    m = ADAM_B1 * m + (1.0 - ADAM_B1) * g
    v = ADAM_B2 * v + (1.0 - ADAM_B2) * (g * g)
    m_hat = m / (1.0 - ADAM_B1 ** ADAM_STEP)
    v_hat = v / (1.0 - ADAM_B2 ** ADAM_STEP)
    delta = -ADAM_LR * (m_hat / (jnp.sqrt(v_hat) + ADAM_EPS) + ADAM_WD * w)
    return delta, m, v


def _adamw(w, m, v, g, name):
    R, C = w.shape
    parts = g.ndim == 3
    tr = R
    for t in (256, 128, 64, 32, 16):
        if R % t == 0 and R > t:
            tr = t
            break

    def body(w_ref, m_ref, v_ref, g_ref, go_ref, d_ref, mo_ref, vo_ref):
        if parts:
            gv = ((g_ref[0].astype(f32) + g_ref[1].astype(f32)) + g_ref[2].astype(f32)) + g_ref[3].astype(f32)
        else:
            gv = g_ref[...]
        go_ref[...] = gv
        d, mn, vn = _adam_math(w_ref[...], gv, m_ref[...], v_ref[...])
        d_ref[...] = d
        mo_ref[...] = mn
        vo_ref[...] = vn

    row = pl.BlockSpec((tr, C), lambda i: (i, 0))
    gspec = pl.BlockSpec((4, tr, C), lambda i: (0, i, 0)) if parts else row
    return pl.pallas_call(
        body,
        grid=(R // tr,),
        in_specs=[row, row, row, gspec],
        out_specs=[row] * 4,
        out_shape=[SDS((R, C), f32)] * 4,
        compiler_params=_cparams(("parallel",)),
        name=name,
    )(w, m, v, g)


def _sum8(parts, name):
    _, _, R, C = parts.shape

    def body(p_ref, o_ref):
        acc = p_ref[0, 0]
        for c in range(2):
            for k in range(4):
                if c or k:
                    acc = acc + p_ref[c, k]
        o_ref[...] = acc

    return pl.pallas_call(body, out_shape=SDS((R, C), f32), name=name)(parts)


def _pair_add(by_core, b, name):
    _, K, R, C = by_core.shape
    tr = R // 2 if R % 32 == 0 else R

    def body(c_ref, a_ref, b_ref, o_ref):
        o_ref[...] = (a_ref[0].astype(f32) + b_ref[...].astype(f32)).astype(bf16)

    blk = pl.BlockSpec((1, tr, C), lambda k, i, c: (k, i, 0))
    return pl.pallas_call(
        body,
        grid_spec=pltpu.PrefetchScalarGridSpec(
            num_scalar_prefetch=1,
            grid=(K, R // tr),
            in_specs=[pl.BlockSpec((1, 1, tr, C), lambda k, i, c: (c[0], k, i, 0)), blk],
            out_specs=blk,
        ),
        out_shape=SDS((K, R, C), bf16),
        compiler_params=_cparams(("parallel", "parallel")),
        name=name,
    )(lax.axis_index("c").astype(jnp.int32).reshape(1), by_core, b)


_ANY = pl.BlockSpec(memory_space=pl.ANY)


def _chip_copies(src_ref, out_ref, send_sems, recv_sems, gather):
    x, y, c = lax.axis_index("x"), lax.axis_index("y"), lax.axis_index("c")
    mine = 2 * x + y

    def piece(k):
        return src_ref if gather else src_ref.at[k]

    sends, recvs = [], []
    for j, (px, py) in enumerate([(1 - x, y), (x, 1 - y), (1 - x, 1 - y)]):
        sends.append(pltpu.make_async_remote_copy(
            src_ref=piece(2 * px + py), dst_ref=out_ref.at[mine], send_sem=send_sems.at[j],
            recv_sem=recv_sems.at[j], device_id=(px, py, c), device_id_type=MESH))
        recvs.append(pltpu.make_async_remote_copy(
            src_ref=piece(mine), dst_ref=out_ref.at[2 * px + py], send_sem=send_sems.at[j],
            recv_sem=recv_sems.at[j], device_id=(px, py, c), device_id_type=MESH))
    return sends, recvs


def _chip_start(src_ref, out_ref, send_sems, recv_sems, gather):
    for cp in _chip_copies(src_ref, out_ref, send_sems, recv_sems, gather)[0]:
        cp.start()


def _chip_finish(src_ref, out_ref, send_sems, recv_sems, gather):
    sends, recvs = _chip_copies(src_ref, out_ref, send_sems, recv_sems, gather)
    for cp in recvs:
        cp.wait_recv()
    for cp in sends:
        cp.wait_send()


def _chip_out_shape(src, gather):
    return SDS((4,) + tuple(src.shape if gather else src.shape[1:]), src.dtype)


_CHIP_SEMS = [pltpu.SemaphoreType.DMA((3,)), pltpu.SemaphoreType.DMA((3,))]


def _fill_own(out, src, gather):
    mine = 2 * lax.axis_index("x") + lax.axis_index("y")
    own = src if gather else lax.dynamic_index_in_dim(src, mine, axis=0, keepdims=False)
    return lax.dynamic_update_index_in_dim(out, own, mine, axis=0)


def _chip_comm(src, gather, name):
    def body(src_ref, out_ref, send_sems, recv_sems):
        _chip_start(src_ref, out_ref, send_sems, recv_sems, gather)
        _chip_finish(src_ref, out_ref, send_sems, recv_sems, gather)

    out = pl.pallas_call(
        body,
        in_specs=[_ANY],
        out_specs=_ANY,
        out_shape=_chip_out_shape(src, gather),
        scratch_shapes=list(_CHIP_SEMS),
        name=name,
    )(src)
    return _fill_own(out, src, gather)


_HBM = pl.BlockSpec(memory_space=pltpu.HBM)
_SEM = pl.BlockSpec(memory_space=pltpu.SEMAPHORE)
_EFFECT = pltpu.SideEffectType.DATAFLOW_SIDE_EFFECTING
_SPLIT_PEERS = {"chip_gather": 3, "chip_xchg": 3, "core_gather": 1, "core_swap": 1}


def _split_land(src, kind):
    if kind == "core_gather":
        return SDS((2,) + tuple(src.shape), src.dtype)
    if kind == "core_swap":
        return SDS(tuple(src.shape[1:]), src.dtype)
    return _chip_out_shape(src, kind == "chip_gather")


def _split_copies(src_ref, land_ref, sems, kind):
    x, y, c = lax.axis_index("x"), lax.axis_index("y"), lax.axis_index("c")
    n = _SPLIT_PEERS[kind]
    if kind == "core_gather":
        routes = [((x, y, 1 - c), src_ref, land_ref.at[c], land_ref.at[1 - c])]
    elif kind == "core_swap":
        routes = [((x, y, 1 - c), src_ref.at[1 - c], land_ref, land_ref)]
    else:
        mine = 2 * x + y
        gather = kind == "chip_gather"
        routes = [((px, py, c), src_ref if gather else src_ref.at[2 * px + py], land_ref.at[mine],
                   land_ref.at[2 * px + py]) for px, py in [(1 - x, y), (x, 1 - y), (1 - x, 1 - y)]]
    sends, recvs = [], []
    for j, (peer, piece, there, here) in enumerate(routes):
        sends.append(pltpu.make_async_remote_copy(src_ref=piece, dst_ref=there, send_sem=sems[j],
                                                  recv_sem=sems[n + j], device_id=peer, device_id_type=MESH))
        recvs.append(pltpu.make_async_remote_copy(src_ref=piece, dst_ref=here, send_sem=sems[j],
                                                  recv_sem=sems[n + j], device_id=peer, device_id_type=MESH))
    return sends, recvs


def _split_start(src, kind, name, after=None):
    land = _split_land(src, kind)
    ns = 2 * _SPLIT_PEERS[kind]
    n_in = 2 if after is None else 3

    def body(*refs):
        src_ref, land_ref = refs[:2]
        outs = refs[n_in:]
        for cp in _split_copies(src_ref, land_ref, outs[:ns], kind)[0]:
            cp.start()
        token = outs[ns + 2]
        token[...] = jnp.zeros_like(token)

    res = pl.pallas_call(
        body,
        name=name,
        out_shape=(pltpu.SemaphoreType.DMA(()),) * ns
        + (pltpu.HBM(src.shape, src.dtype), pltpu.HBM(land.shape, land.dtype), SDS((SUBLANE, LANE), f32)),
        in_specs=(_HBM, _HBM) + (() if after is None else (_ANY,)),
        out_specs=(_SEM,) * ns + (_HBM, _HBM, pl.BlockSpec(memory_space=pltpu.VMEM)),
        input_output_aliases={0: ns, 1: ns + 1},
        compiler_params=pltpu.CompilerParams(has_side_effects=_EFFECT),
    )(pltpu.with_memory_space_constraint(src, pltpu.HBM),
      pltpu.with_memory_space_constraint(lax.empty(land.shape, land.dtype), pltpu.HBM),
      *(() if after is None else (after,)))
    return (res[:ns], res[ns], res[ns + 1]), res[ns + 2]


def _split_wait(state, after, kind, name):
    sems, src_thru, land_thru = state
    ns = 2 * _SPLIT_PEERS[kind]

    def body(src_ref, land_ref, *rest):
        sends, recvs = _split_copies(src_ref, land_ref, rest[:ns], kind)
        for cp in recvs:
            cp.wait_recv()
        for cp in sends:
            cp.wait_send()

    src_out, got = pl.pallas_call(
        body,
        name=name,
        out_shape=(pltpu.HBM(src_thru.shape, src_thru.dtype), pltpu.HBM(land_thru.shape, land_thru.dtype)),
        in_specs=(_HBM, _HBM) + (_SEM,) * ns + (_ANY,),
        out_specs=(_HBM, _HBM),
        input_output_aliases={0: 0, 1: 1},
        compiler_params=pltpu.CompilerParams(has_side_effects=_EFFECT),
    )(src_thru, land_thru, *sems, after)
    if kind == "core_swap":
        return got, src_out
    if kind == "core_gather":
        return lax.dynamic_update_index_in_dim(got, src_out, lax.axis_index("c"), axis=0)
    return _fill_own(got, src_out, kind == "chip_gather")


def _core_gather(src, name):
    def body(src_ref, out_ref, send_sem, recv_sem):
        x, y, c = lax.axis_index("x"), lax.axis_index("y"), lax.axis_index("c")
        cp = pltpu.make_async_remote_copy(src_ref=src_ref, dst_ref=out_ref.at[c], send_sem=send_sem,
                                          recv_sem=recv_sem, device_id=(x, y, 1 - c), device_id_type=MESH)
        cp.start()
        pltpu.make_async_remote_copy(src_ref=src_ref, dst_ref=out_ref.at[1 - c], send_sem=send_sem,
                                     recv_sem=recv_sem, device_id=(x, y, 1 - c), device_id_type=MESH).wait_recv()
        cp.wait_send()

    out = pl.pallas_call(
        body,
        in_specs=[_ANY],
        out_specs=_ANY,
        out_shape=SDS((2,) + tuple(src.shape), src.dtype),
        scratch_shapes=[pltpu.SemaphoreType.DMA, pltpu.SemaphoreType.DMA],
        name=name,
    )(src)
    return lax.dynamic_update_index_in_dim(out, src, lax.axis_index("c"), axis=0)


_PACK_A = (("w_in", (1088, 1024)),)
_PACK_B = (("w_ba", (512, 128)), ("w_bh", (512, 128)), ("w_out", (128, 1024)), ("w_up", (704, 1024)),
           ("w_down", (352, 1024)))
_PACK_SIZES = _PACK_A + _PACK_B
_TRANSPOSED = ("w_in", "w_up")


def _slab_rows(sizes):
    return sum(r * c for _, (r, c) in sizes) // D_MODEL


def _pack_rows(d, sizes):
    n = d[sizes[0][0]].shape[0]
    return jnp.concatenate([d[k].reshape(n, -1, D_MODEL) for k, _ in sizes], axis=1)


def _unpack_rows(slab, sizes):
    n = slab.shape[0]
    out, lo = {}, 0
    for key, (r, c) in sizes:
        rows = r * c // D_MODEL
        out[key] = slab[:, lo:lo + rows].reshape(n, r, c)
        lo += rows
    return out


def _by_core(gslab):
    return jnp.swapaxes(gslab.reshape((4, 2) + gslab.shape[1:]), 0, 1)


def _cols_to_full(t):
    return jnp.swapaxes(t, 0, 1).reshape(t.shape[1], -1)


def _full_to_cols(t):
    K = t.shape[0]
    return jnp.swapaxes(t.reshape(K, 8, -1), 0, 1)


_SMALL = (("pre_mix_norm", (1, 1024)), ("rel_bias", (32, 24)), ("hgrn_lb_raw", (2, 512)), ("hgrn_norm", (1, 128)),
          ("post_mix_norm", (1, 1024)), ("pre_ffn_norm", (1, 1024)), ("conv_b", (1, 5632)),
          ("post_ffn_norm", (1, 1024)))
_SMALL_ROWS = 96
_CONVW_ROWS = 136


_SMALL_USED = sum(r * c for _, (r, c) in _SMALL)


def _pack_small(d, extra=None):
    flat = jnp.concatenate([d[k].reshape(-1) for k, _ in _SMALL] + ([] if extra is None else [extra.reshape(-1)]))
    flat = jnp.pad(flat, (0, _SMALL_ROWS * LANE - flat.shape[0]))
    return flat.reshape(_SMALL_ROWS, LANE)


def _unpack_small(p):
    flat = p.reshape(-1)
    out, lo = {}, 0
    for k, shp in _SMALL:
        n = shp[0] * shp[1]
        out[k] = flat[lo:lo + n].reshape(shp)
        lo += n
    return out


def _local_step(x, tgt, P, plan):
    S = x.shape[0]
    P = dict(P)
    lb = _lb_fwd(P["hgrn_lb_raw"])
    hs = _prep(x, P["pre_mix_norm"], plan.start_token())
    h1 = hs[0]
    consts = [_bias_consts(d) for d in DILATIONS]
    biases, dep = [], h1
    for g in range(N_GROUPS):
        tab_t = P["rel_bias"][:, 8 * g:8 * g + 8].T
        dep = _bias_build(tab_t, consts[g][0], consts[g][1], f"bias_build{g}", dep)
        biases.append(dep.reshape(8, ATTN_BLOCK, 2 * ATTN_BLOCK))
    W = dict(plan.weights_a(dep))
    qkv = [_mm(hs[g], W["wt_qkv"][g], "nt", bf16, f"proj_qkv{g}") for g in range(N_GROUPS)]
    hg = _mm(h1, W["wt_hg"], "nt", f32, "proj_hg")
    gc = _mm(h1, W["wt_gate"], "nt", bf16, "proj_gate")
    obuf, lbuf, token = [], [], None
    for g, d in enumerate(DILATIONS):
        o_g, l_g = _attn_fwd(qkv[g], biases[g], (S // d) // ATTN_BLOCK, f"attn_fwd{g}", after=token)
        lbuf.append(l_g)
        obuf.append(o_g)
        if g == 0:
            token = plan.forward_b(o_g)
    y_attn, y_attn_b, w0, w1, w2 = _attn_merge(obuf[0], obuf[1], obuf[2], lbuf[0], lbuf[1], lbuf[2])
    y_hgrn, o_raw, ck, _ = _hgrn_fwd(hg, lb, P["hgrn_norm"])
    wb = plan.weights_b(y_hgrn)
    P["conv_w"] = wb.pop("conv_w")
    W.update(wb)
    a = _mm(y_attn_b, W["w_ba"], "nn", bf16, "branch_attn")
    b = _mm(y_hgrn, W["w_bh"], "nn", bf16, "branch_hgrn")
    merged = _gate_fwd(a, b, gc)
    mo, x1, h2 = _mid_fwd(x, merged, W["w_out"], P["post_mix_norm"], P["pre_ffn_norm"])
    ug = _mm(h2, W["wt_up_g"], "nt", bf16, "up_gate")
    uv = _mm(h2, W["wt_up_v"], "nt", bf16, "up_val")
    cw_g, cw_v = P["conv_w"][:, :D_FF], P["conv_w"][:, D_FF:]
    cb_g, cb_v = P["conv_b"][:, :D_FF], P["conv_b"][:, D_FF:]
    act = _conv_fwd(ug, uv, cw_g, cw_v, cb_g, cb_v)
    loss, dy, dfo, g_post_ffn = _final(x1, act, W["w_down"], tgt, P["post_ffn_norm"])
    dact = _mm(dfo, W["w_down"], "nt", bf16, "d_act")
    gW_down = _mm(act, dfo, "tn", f32, "gw_down")
    dug, duv, st_g, st_v = _conv_bwd(ug, uv, dact, cw_g, cw_v, cb_g, cb_v)
    dh2 = _mm([dug, duv], [W["wt_up_g"], W["wt_up_v"]], "nn", f32, "dh2")
    gW_up_g = _mm(dug, h2, "tn", f32, "gw_up_gate")
    gW_up_v = _mm(duv, h2, "tn", f32, "gw_up_val")
    dx1, dmo, g_pre_ffn, g_post_mix = _mid_bwd(dy, dh2, x1, mo, P["pre_ffn_norm"], P["post_mix_norm"])
    dmerged = _mm(dmo, W["w_out"], "nt", bf16, "d_merged")
    gW_out = _mm(merged, dmo, "tn", f32, "gw_out")
    da, db, dgc = _gate_bwd(dmerged, a, b, gc)
    dyattn = _mm(da, W["w_ba"], "nt", f32, "d_yattn")
    gW_ba = _mm(y_attn_b, da, "tn", f32, "gw_ba")
    dyhgrn = _mm(db, W["w_bh"], "nt", f32, "d_yhgrn")
    gW_bh = _mm(y_hgrn, db, "tn", f32, "gw_bh")
    big_b = dict(w_ba=gW_ba, w_bh=gW_bh, w_out=gW_out, w_up=[gW_up_g, gW_up_v], w_down=gW_down)
    dos = _attn_merge_bwd(dyattn, y_attn, w0, w1, w2, after=plan.grads_b_start(big_b))
    dq_h, df_h, dv_h, dog_h, glb8, gnw8, got_b = _hgrn_bwd(hg, o_raw, dyhgrn, ck, lb, P["hgrn_norm"],
                                                          plan.bwd_ride(dos[5]))
    dhg = [dq_h, df_h, dv_h, dog_h]
    g_lb_raw = _lb_bwd(P["hgrn_lb_raw"], glb8[0:1])
    gn = gnw8[0:1]
    g_hgrn_norm = (gn[:, 0:128] + gn[:, 128:256]) + (gn[:, 256:384] + gn[:, 384:512])
    dqkvs, gW_qkv, g_rel = [], [], []
    for g, d in enumerate(DILATIONS):
        dq, dk, dv, dbias = _attn_bwd(qkv[g], biases[g], dos[g], dos[3 + g], lbuf[g], (S // d) // ATTN_BLOCK,
                                      f"attn_bwd{g}")
        dqkvs.append([dq, dk, dv])
        gW_qkv.append(_mm(dqkvs[g], hs[g], "tn", f32, f"gw_qkv{g}"))
        g_rel.append(_bias_grad(dbias.reshape(8, -1), consts[g][0], f"bias_grad{g}"))
    gW_hg = _mm(dhg, h1, "tn", f32, "gw_hg")
    gW_gate = _mm(dgc, h1, "tn", f32, "gw_gate")
    gW_in = gW_qkv + [gW_hg, gW_gate]
    token = plan.grads_a_start(gW_in)
    dh_perm = [_mm(dqkvs[g], W["wt_qkv"][g], "nn", f32, f"dh1_qkv{g}", after=token) for g in (1, 2)]
    token = plan.grads_a_exchange(dh_perm[1])
    dh_main = _mm(dqkvs[0] + dhg + [dgc], [W["wt_qkv"][0], W["wt_hg"], W["wt_gate"]], "nn", f32, "dh1_main",
                  after=token)
    grad_x, g_pre_mix = _first_bwd(x, dx1, _dh_sum(dh_main, dh_perm[0], dh_perm[1]), P["pre_mix_norm"])

    g_conv_w = jnp.concatenate([st_g[0:3], st_v[0:3]], axis=1)
    g_conv_b = jnp.concatenate([st_g[3:4], st_v[3:4]], axis=1)
    small = dict(pre_mix_norm=g_pre_mix, rel_bias=jnp.concatenate(g_rel, axis=1), hgrn_lb_raw=g_lb_raw,
                 hgrn_norm=g_hgrn_norm, post_mix_norm=g_post_mix, pre_ffn_norm=g_pre_ffn, conv_b=g_conv_b,
                 post_ffn_norm=g_post_ffn, conv_w=g_conv_w)
    return loss, grad_x, gW_in, big_b, got_b, small


def _weights_a(both):
    wt = jnp.swapaxes(both, 0, 1).reshape(-1, D_MODEL)
    return dict(
        wt_qkv=[wt[g * QKV_G:(g + 1) * QKV_G] for g in range(N_GROUPS)],
        wt_hg=wt[3 * QKV_G:3 * QKV_G + 4 * HGRN_W],
        wt_gate=wt[3 * QKV_G + 4 * HGRN_W:],
    )


def _weights_b(slabs):
    sh = _unpack_rows(slabs, _PACK_B)
    wt_up = sh["w_up"].reshape(-1, D_MODEL)
    return dict(
        w_ba=_cols_to_full(sh["w_ba"]),
        w_bh=_cols_to_full(sh["w_bh"]),
        w_out=sh["w_out"].reshape(D_MODEL, D_MODEL),
        wt_up_g=wt_up[:D_FF],
        wt_up_v=wt_up[D_FF:],
        w_down=sh["w_down"].reshape(D_FF, D_MODEL),
    )


def _dest_rows(sections, height):
    out = []
    for j in range(8):
        lo, hi, off, pieces = j * height, (j + 1) * height, 0, []
        for s in sections:
            a, b = max(lo, off), min(hi, off + s.shape[0])
            if a < b:
                pieces.append(s[a - off:b - off])
            off += s.shape[0]
        out.append(pieces[0] if len(pieces) == 1 else jnp.concatenate(pieces, axis=0))
    return out


def _grad_blocks_a(sections):
    rows = _dest_rows(sections, 1088)
    return jnp.stack([jnp.stack([rows[2 * k + c].astype(bf16) for k in range(4)]) for c in range(2)])


def _grad_slab_b(g):
    shards = dict(w_ba=_full_to_cols(g["w_ba"]), w_bh=_full_to_cols(g["w_bh"]), w_out=g["w_out"].reshape(8, 128, D_MODEL),
                  w_up=jnp.stack(_dest_rows(g["w_up"], 704)), w_down=g["w_down"].reshape(8, 352, D_MODEL))
    return _pack_rows({k: v.astype(bf16) for k, v in shards.items()}, _PACK_B)


_CONVW_SLAB_ROWS = 16


class _Traffic:
    def __init__(self, slab_a, slab_b, conv_w):
        hi = conv_w.astype(bf16)
        r1 = conv_w - hi.astype(f32)
        mid = r1.astype(bf16)
        lo = (r1 - mid.astype(f32)).astype(bf16)
        bits = jnp.stack([hi, mid, lo]).reshape(-1)
        tail = jnp.pad(bits, (0, _CONVW_SLAB_ROWS * D_MODEL - bits.shape[0])).reshape(_CONVW_SLAB_ROWS, D_MODEL)
        self.slab_b = jnp.concatenate([slab_b, tail], axis=0)
        self.state_a, tok = _split_start(slab_a, "chip_gather", "ag_a_start")
        self.state_b, self.token = _split_start(self.slab_b, "chip_gather", "ag_b_start", after=tok)
        self.chip_sum = None
        self.state = None

    def start_token(self):
        return self.token

    def weights_a(self, after):
        by_chip = _split_wait(self.state_a, after, "chip_gather", "ag_a_wait")
        return _weights_a(_core_gather(by_chip, "ag_a_cores"))

    def forward_b(self, after):
        by_chip = _split_wait(self.state_b, after, "chip_gather", "ag_b_wait")
        self.state, token = _split_start(by_chip, "core_gather", "ag_b_cores_start")
        return token

    def weights_b(self, after):
        both = _split_wait(self.state, after, "core_gather", "ag_b_cores_wait")
        slabs = jnp.swapaxes(both, 0, 1).reshape((8,) + tuple(self.slab_b.shape))
        rows = _slab_rows(_PACK_B)
        out = _weights_b(slabs[:, :rows])
        pieces = slabs[:, rows:].reshape(8, -1)[:, :3 * 3 * 704].reshape(8, 3, 3, 704).astype(f32)
        out["conv_w"] = _cols_to_full((pieces[:, 0] + pieces[:, 1]) + pieces[:, 2])
        return out

    def grads_b_start(self, grads):
        self.state, token = _split_start(_by_core(_grad_slab_b(grads)), "core_swap", "rs_b_cores_start")
        return token

    def bwd_ride(self, after):
        from_sib, by_core = _split_wait(self.state, after, "core_swap", "rs_b_cores_wait")
        self.chip_sum = _pair_add(by_core, from_sib, "rs_b_pair_add")
        return (self.chip_sum, False)

    def grads_a_start(self, sections):
        self.state, token = _split_start(_grad_blocks_a(sections), "core_swap", "rs_a_cores_start")
        return token

    def grads_a_exchange(self, after):
        from_sib, by_core = _split_wait(self.state, after, "core_swap", "rs_a_cores_wait")
        self.state, token = _split_start(_pair_add(by_core, from_sib, "rs_a_pair_add"), "chip_xchg", "rs_a_start")
        return token

    def parts(self, got_b, after):
        parts = _unpack_rows(_fill_own(got_b, self.chip_sum, False), _PACK_B)
        parts["w_in"] = _split_wait(self.state, after, "chip_xchg", "rs_a_wait")
        return parts


def kernel(x, pre_mix_norm, w_in, rel_bias, hgrn_lb_raw, hgrn_norm, w_branch_attn, w_branch_hgrn, w_out, post_mix_norm, pre_ffn_norm, w_up, conv_w, conv_b, w_down, post_ffn_norm, loss_target, m_pre_mix_norm, m_w_in, m_rel_bias, m_hgrn_lb_raw, m_hgrn_norm, m_w_branch_attn, m_w_branch_hgrn, m_w_out, m_post_mix_norm, m_pre_ffn_norm, m_w_up, m_conv_w, m_conv_b, m_w_down, m_post_ffn_norm, v_pre_mix_norm, v_w_in, v_rel_bias, v_hgrn_lb_raw, v_hgrn_norm, v_w_branch_attn, v_w_branch_hgrn, v_w_out, v_post_mix_norm, v_pre_ffn_norm, v_w_up, v_conv_w, v_conv_b, v_w_down, v_post_ffn_norm):
    ci = lax.axis_index("c")
    dev = 4 * lax.axis_index("x") + 2 * lax.axis_index("y") + ci
    tr = lambda t: jnp.swapaxes(t[0], 0, 1)
    wts = dict(w_in=tr(w_in), w_ba=w_branch_attn[0], w_bh=w_branch_hgrn[0], w_out=w_out[0], w_up=tr(w_up),
               w_down=w_down[0])
    mom = dict(w_in=tr(m_w_in), w_ba=m_w_branch_attn[0], w_bh=m_w_branch_hgrn[0], w_out=m_w_out[0], w_up=tr(m_w_up),
               w_down=m_w_down[0])
    var = dict(w_in=tr(v_w_in), w_ba=v_w_branch_attn[0], w_bh=v_w_branch_hgrn[0], w_out=v_w_out[0], w_up=tr(v_w_up),
               w_down=v_w_down[0])
    small_w = dict(pre_mix_norm=pre_mix_norm, rel_bias=rel_bias, hgrn_lb_raw=hgrn_lb_raw, hgrn_norm=hgrn_norm,
                   post_mix_norm=post_mix_norm, pre_ffn_norm=pre_ffn_norm, conv_b=conv_b, post_ffn_norm=post_ffn_norm)
    small_m = dict(pre_mix_norm=m_pre_mix_norm, rel_bias=m_rel_bias, hgrn_lb_raw=m_hgrn_lb_raw, hgrn_norm=m_hgrn_norm,
                   post_mix_norm=m_post_mix_norm, pre_ffn_norm=m_pre_ffn_norm, conv_b=m_conv_b,
                   post_ffn_norm=m_post_ffn_norm)
    small_v = dict(pre_mix_norm=v_pre_mix_norm, rel_bias=v_rel_bias, hgrn_lb_raw=v_hgrn_lb_raw, hgrn_norm=v_hgrn_norm,
                   post_mix_norm=v_post_mix_norm, pre_ffn_norm=v_pre_ffn_norm, conv_b=v_conv_b,
                   post_ffn_norm=v_post_ffn_norm)

    plan = _Traffic(wts["w_in"].astype(bf16),
                    _pack_rows({k: wts[k].astype(bf16)[None] for k, _ in _PACK_B}, _PACK_B)[0], conv_w[0])

    loss8, grad_x, _, _, got_b, small = _local_step(x[0], loss_target[0], small_w, plan)
    parts = plan.parts(got_b, grad_x)
    outs_big = {}
    for k, _ in _PACK_SIZES:
        outs_big[k] = _adamw(wts[k], mom[k], var[k], parts[k], "adamw_" + k)

    spack = jnp.concatenate([_pack_small(small, loss8[0, 0:1]),
                             jnp.pad(small["conv_w"].reshape(-1, LANE), ((0, _CONVW_ROWS - 132), (0, 0)))], axis=0)
    allp = _core_gather(_chip_comm(spack, True, "ag_small_chips"), "ag_small_cores")
    ssum = _sum8(allp, "small_sum")
    gs = ssum[:_SMALL_ROWS]
    loss = ssum[_SMALL_USED // LANE, _SMALL_USED % LANE]
    res_small = _adamw(_pack_small(small_w), _pack_small(small_m), _pack_small(small_v), gs, "adamw_small")
    sm = [_unpack_small(t) for t in res_small]
    g_cw_full = ssum[_SMALL_ROWS:_SMALL_ROWS + 132].reshape(3, 2 * D_FF)
    g_cw = lax.dynamic_slice_in_dim(g_cw_full, dev * 704, 704, axis=1)
    res_cw = _adamw(conv_w[0], m_conv_w[0], v_conv_w[0], g_cw, "adamw_conv_w")

    def pick(i):
        def big_(k):
            t = outs_big[k][i]
            return (jnp.swapaxes(t, 0, 1) if k in _TRANSPOSED else t)[None]
        return [sm[i]["pre_mix_norm"], big_("w_in"), sm[i]["rel_bias"], sm[i]["hgrn_lb_raw"], sm[i]["hgrn_norm"],
                big_("w_ba"), big_("w_bh"), big_("w_out"), sm[i]["post_mix_norm"], sm[i]["pre_ffn_norm"],
                big_("w_up"), res_cw[i][None], sm[i]["conv_b"], big_("w_down"), sm[i]["post_ffn_norm"]]

    return (loss, grad_x[None], *pick(0), *pick(1), *pick(2), *pick(3))
```

```python
import functools
import math

import jax
import jax.numpy as jnp
from jax import lax
from jax.experimental import pallas as pl
from jax.experimental.pallas import tpu as pltpu

f32 = jnp.float32
bf16 = jnp.bfloat16
SDS = jax.ShapeDtypeStruct
HIGHEST = lax.Precision.HIGHEST
MESH = pl.DeviceIdType.MESH

NN = (((1,), (0,)), ((), ()))
NT = (((1,), (1,)), ((), ()))
TN = (((0,), (0,)), ((), ()))

D_MODEL = 1024
N_GROUPS = 3
DILATIONS = (1, 4, 16)
HEAD_DIM = 64
ATTN_BLOCK = 128
QKV_G = 1536
ATTN_OUT = 512
HGRN_W = 512
HGRN_CHUNK = 32
D_FF = 2816
NUM_BUCKETS = 32
MAX_EXACT = 16
MAX_DISTANCE = 2048
NEG_INF = -1e30
EPS = 1e-6
LANE = 128
SUBLANE = 8
VMEM_BIG = 48 * 1024 * 1024
MM_ROWS = 512
MM_OUT_BYTES = 8 * 1024 * 1024

ADAM_LR, ADAM_B1, ADAM_B2, ADAM_EPS, ADAM_WD, ADAM_STEP = 0.001, 0.9, 0.999, 1e-08, 0.01, 10


def _pick(n, pref):
    t = pref
    while t >= LANE:
        if n % t == 0:
            return t
        t //= 2
    return n


def _cparams(sem=None, vmem=None):
    kw = {}
    if sem is not None:
        kw["dimension_semantics"] = sem
    if vmem is not None:
        kw["vmem_limit_bytes"] = vmem
    return pltpu.CompilerParams(**kw)


def _sigmoid(x):
    return jax.nn.sigmoid(x)


def _colsum8(x):
    return x.reshape(x.shape[0] // SUBLANE, SUBLANE, x.shape[1]).sum(axis=0)


def _mm(a, b, mode, out_dtype, name, acc=None, after=None):
    dims = {"nn": NN, "nt": NT, "tn": TN}[mode]
    has_acc = acc is not None
    parts = list(a) if isinstance(a, (list, tuple)) else [a]
    if mode == "tn":
        assert not has_acc
        K, N = b.shape
        widths = [t.shape[1] for t in parts]
        M = sum(widths)
        whole = M * N * 4 <= MM_OUT_BYTES
        assert whole or len(parts) == 1
        tmm = M if whole else M // 2
        ts = _pick(K, 2 * MM_ROWS)
        nk = K // ts

        def body_tn(*refs):
            b_ref, o_ref = refs[-2], refs[-1]
            k = pl.program_id(1)
            bv = b_ref[...]
            lo = 0
            for a_ref, w in zip(refs[:-2], widths if whole else [tmm]):
                part = lax.dot_general(a_ref[...], bv, dims, preferred_element_type=f32)
                rows = slice(lo, lo + w)
                lo += w

                @pl.when(k == 0)
                def _(part=part, rows=rows):
                    o_ref[rows, :] = part

                @pl.when(k > 0)
                def _(part=part, rows=rows):
                    o_ref[rows, :] += part

        return pl.pallas_call(
            body_tn,
            grid=(M // tmm, nk),
            in_specs=[pl.BlockSpec((ts, w if whole else tmm), lambda i, k: (k, i)) for w in widths]
            + [pl.BlockSpec((ts, N), lambda i, k: (k, 0))],
            out_specs=pl.BlockSpec((tmm, N), lambda i, k: (i, 0)),
            out_shape=SDS((M, N), out_dtype),
            compiler_params=_cparams(("parallel", "arbitrary"), VMEM_BIG),
            name=name,
        )(*parts, b)

    bs = list(b) if isinstance(b, (list, tuple)) else [b]
    widths = [t.shape[1] for t in parts]
    M = parts[0].shape[0]
    kdim = 0 if mode == "nn" else 1
    N = bs[0].shape[1 - kdim]
    tm = _pick(M, MM_ROWS)
    npart, nb = len(parts), len(bs)
    place, bi, lo = [], 0, 0
    for w in widths:
        place.append((bi, lo))
        lo += w
        if lo == bs[bi].shape[kdim]:
            bi, lo = bi + 1, 0
    assert bi == nb and lo == 0

    def body(*refs):
        a_refs, b_refs = refs[:npart], refs[npart:npart + nb]
        c_ref = refs[npart + nb] if has_acc else None
        o_ref = refs[-1]
        part = None
        for a_ref, w, (bi, lo) in zip(a_refs, widths, place):
            b_ref = b_refs[bi]
            if w == bs[bi].shape[kdim]:
                bk = b_ref[...]
            else:
                bk = b_ref[:, lo:lo + w] if mode == "nt" else b_ref[lo:lo + w, :]
            t = lax.dot_general(a_ref[...], bk, dims, preferred_element_type=f32)
            part = t if part is None else part + t
        if has_acc:
            part = part + c_ref[...]
        o_ref[...] = part.astype(out_dtype)

    specs = [pl.BlockSpec((tm, w), lambda i: (i, 0)) for w in widths] \
        + [pl.BlockSpec(t.shape, lambda i: (0, 0)) for t in bs]
    args = parts + bs
    aliases = {}
    if has_acc:
        specs.append(pl.BlockSpec((tm, N), lambda i: (i, 0)))
        args.append(acc)
        aliases = {npart + nb: 0}
    if after is not None:
        specs.append(pl.BlockSpec(memory_space=pl.ANY))
        args.append(after)
    return pl.pallas_call(
        body,
        grid=(M // tm,),
        in_specs=specs,
        out_specs=pl.BlockSpec((tm, N), lambda i: (i, 0)),
        out_shape=SDS((M, N), out_dtype),
        input_output_aliases=aliases,
        compiler_params=_cparams(("parallel",), VMEM_BIG),
        name=name,
    )(*args)


PERM_ROWS = 1024


def _perm_spec(d, cols=LANE):
    return pl.BlockSpec((d, PERM_ROWS // d, cols), lambda i, j: (0, i, j))


def _to_natural(src_ref, dst_ref, d):
    n = src_ref.shape[1]
    for r in range(d):
        dst_ref[pl.ds(r, n, stride=d), :] = src_ref[r]


def _prep(x, w, after=None):
    S, D = x.shape
    R = PERM_ROWS
    nc = D // LANE
    n_in = nc + 1 + (after is not None)

    def body(*refs):
        x_refs, w_ref = refs[:nc], refs[nc]
        h_ref, h4_ref, h16_ref, rs = refs[n_in:]
        ssq = None
        for xr in x_refs:
            v = xr[...]
            t = jnp.sum(v * v, axis=-1, keepdims=True)
            ssq = t if ssq is None else ssq + t
        rinv = lax.rsqrt(ssq * (1.0 / D) + EPS)
        rs[...] = jnp.broadcast_to(rinv, (R, LANE))
        for j, xr in enumerate(x_refs):
            cols = slice(j * LANE, (j + 1) * LANE)
            wj = w_ref[:, cols]
            h_ref[:, cols] = ((xr[...] * rinv) * wj).astype(bf16)
            for d, o_ref in ((4, h4_ref), (16, h16_ref)):
                n = R // d
                for r in range(d):
                    rows = pl.ds(r, n, stride=d)
                    o_ref[r, :, cols] = ((xr[rows, :] * rs[rows, :]) * wj).astype(bf16)

    col = lambda j: pl.BlockSpec((R, LANE), lambda i, j=j: (i, j))
    h, h4, h16 = pl.pallas_call(
        body,
        grid=(S // R,),
        in_specs=[col(j) for j in range(nc)] + [pl.BlockSpec((1, D), lambda i: (0, 0))]
        + ([] if after is None else [pl.BlockSpec(memory_space=pl.ANY)]),
        out_specs=[pl.BlockSpec((R, D), lambda i: (i, 0)), pl.BlockSpec((4, R // 4, D), lambda i: (0, i, 0)),
                   pl.BlockSpec((16, R // 16, D), lambda i: (0, i, 0))],
        out_shape=[SDS((S, D), bf16), SDS((4, S // 4, D), bf16), SDS((16, S // 16, D), bf16)],
        scratch_shapes=[pltpu.VMEM((R, LANE), f32)],
        compiler_params=_cparams(("parallel",), VMEM_BIG),
        name="prep_norm_perm",
    )(*([x] * nc), w, *([] if after is None else [after]))
    return [h, h4.reshape(S, D), h16.reshape(S, D)]


def _dh_sum(a, b, c):
    S, D = a.shape
    R = PERM_ROWS

    def body(a_ref, b_ref, c_ref, o_ref, sb, sc):
        _to_natural(b_ref, sb, 4)
        _to_natural(c_ref, sc, 16)
        o_ref[...] = (a_ref[...] + sb[...]) + sc[...]

    nat = pl.BlockSpec((R, LANE), lambda i, j: (i, j))
    return pl.pallas_call(
        body,
        grid=(S // R, D // LANE),
        in_specs=[nat, _perm_spec(4), _perm_spec(16)],
        out_specs=nat,
        out_shape=SDS((S, D), f32),
        scratch_shapes=[pltpu.VMEM((R, LANE), f32)] * 2,
        compiler_params=_cparams(("parallel", "parallel")),
        name="dh_sum",
    )(a, b.reshape(4, S // 4, D), c.reshape(16, S // 16, D))


def _rms_parts(xv):
    r = lax.rsqrt(jnp.mean(xv * xv, axis=-1, keepdims=True) + EPS)
    return r, xv * r


def _rms_bwd(xhat, r, w, dy):
    dyw = dy * w
    return r * (dyw - xhat * jnp.mean(dyw * xhat, axis=-1, keepdims=True))


def _mid_fwd(x, merged, w_out, w_pm, w_pf):
    S, D = x.shape
    tm = _pick(S, MM_ROWS)

    def body(x_ref, m_ref, wo_ref, wpm_ref, wpf_ref, mo_ref, x1_ref, h2_ref):
        mo = jnp.dot(m_ref[...], wo_ref[...], preferred_element_type=f32)
        mo_ref[...] = mo
        _, moh = _rms_parts(mo)
        x1 = x_ref[...] + moh * wpm_ref[...]
        x1_ref[...] = x1
        _, x1h = _rms_parts(x1)
        h2_ref[...] = (x1h * wpf_ref[...]).astype(bf16)

    row = pl.BlockSpec((tm, D), lambda i: (i, 0))
    vec = pl.BlockSpec((1, D), lambda i: (0, 0))
    return pl.pallas_call(
        body,
        grid=(S // tm,),
        in_specs=[row, pl.BlockSpec((tm, merged.shape[1]), lambda i: (i, 0)),
                  pl.BlockSpec(w_out.shape, lambda i: (0, 0)), vec, vec],
        out_specs=[row, row, row],
        out_shape=[SDS((S, D), f32), SDS((S, D), f32), SDS((S, D), bf16)],
        compiler_params=_cparams(("parallel",), VMEM_BIG),
        name="out_proj_mid_fwd",
    )(x, merged, w_out, w_pm, w_pf)


def _final(x1, act, w_down, tgt, w_pfn):
    S, D = x1.shape
    tm = _pick(S, MM_ROWS)
    nt = S // tm

    def body(x1_ref, a_ref, wd_ref, t_ref, w_ref, loss_ref, dy_ref, dfo_ref, gw_ref, lacc, gacc):
        i = pl.program_id(0)

        @pl.when(i == 0)
        def _():
            lacc[...] = jnp.zeros_like(lacc)
            gacc[...] = jnp.zeros_like(gacc)

        w = w_ref[...]
        r, foh = _rms_parts(jnp.dot(a_ref[...], wd_ref[...], preferred_element_type=f32))
        y = x1_ref[...] + foh * w
        err = y - t_ref[...]
        lacc[...] += _colsum8(err * err)
        dy = err * (1.0 / D)
        dy_ref[...] = dy
        gacc[...] += _colsum8(dy * foh)
        dfo_ref[...] = _rms_bwd(foh, r, w, dy).astype(bf16)

        @pl.when(i == nt - 1)
        def _():
            loss_ref[...] = jnp.full((SUBLANE, LANE), 0.5 / D, f32) * jnp.sum(lacc[...])
            gw_ref[...] = jnp.sum(gacc[...], axis=0, keepdims=True)

    row = pl.BlockSpec((tm, D), lambda i: (i, 0))
    vec = pl.BlockSpec((1, D), lambda i: (0, 0))
    return pl.pallas_call(
        body,
        grid=(nt,),
        in_specs=[row, pl.BlockSpec((tm, act.shape[1]), lambda i: (i, 0)),
                  pl.BlockSpec(w_down.shape, lambda i: (0, 0)), row, vec],
        out_specs=[pl.BlockSpec((SUBLANE, LANE), lambda i: (0, 0)), row, row, vec],
        out_shape=[SDS((SUBLANE, LANE), f32), SDS((S, D), f32), SDS((S, D), bf16), SDS((1, D), f32)],
        scratch_shapes=[pltpu.VMEM((SUBLANE, D), f32), pltpu.VMEM((SUBLANE, D), f32)],
        compiler_params=_cparams(("arbitrary",), VMEM_BIG),
        name="down_proj_final_loss",
    )(x1, act, w_down, tgt, w_pfn)


def _mid_bwd(dy, dh2, x1, mo, w_pf, w_pm):
    S, D = dy.shape
    tm = _pick(S, 512)
    nt = S // tm

    def body(dy_ref, dh2_ref, x1_ref, mo_ref, wpf_ref, wpm_ref, dx1_ref, dmo_ref, gpf_ref, gpm_ref, apf, apm):
        i = pl.program_id(0)

        @pl.when(i == 0)
        def _():
            apf[...] = jnp.zeros_like(apf)
            apm[...] = jnp.zeros_like(apm)

        r1, x1h = _rms_parts(x1_ref[...])
        dh2 = dh2_ref[...]
        apf[...] += _colsum8(dh2 * x1h)
        dx1 = dy_ref[...] + _rms_bwd(x1h, r1, wpf_ref[...], dh2)
        dx1_ref[...] = dx1
        rm, moh = _rms_parts(mo_ref[...])
        apm[...] += _colsum8(dx1 * moh)
        dmo_ref[...] = _rms_bwd(moh, rm, wpm_ref[...], dx1).astype(bf16)

        @pl.when(i == nt - 1)
        def _():
            gpf_ref[...] = jnp.sum(apf[...], axis=0, keepdims=True)
            gpm_ref[...] = jnp.sum(apm[...], axis=0, keepdims=True)

    row = pl.BlockSpec((tm, D), lambda i: (i, 0))
    vec = pl.BlockSpec((1, D), lambda i: (0, 0))
    return pl.pallas_call(
        body,
        grid=(nt,),
        in_specs=[row, row, row, row, vec, vec],
        out_specs=[row, row, vec, vec],
        out_shape=[SDS((S, D), f32), SDS((S, D), bf16), SDS((1, D), f32), SDS((1, D), f32)],
        scratch_shapes=[pltpu.VMEM((SUBLANE, D), f32), pltpu.VMEM((SUBLANE, D), f32)],
        compiler_params=_cparams(("arbitrary",)),
        name="mid_bwd",
    )(dy, dh2, x1, mo, w_pf, w_pm)


def _first_bwd(x, dx1, dh, w_pre):
    S, D = x.shape
    tm = _pick(S, 512)
    nt = S // tm

    def body(x_ref, dx1_ref, a_ref, w_ref, gx_ref, gw_ref, acc):
        i = pl.program_id(0)

        @pl.when(i == 0)
        def _():
            acc[...] = jnp.zeros_like(acc)

        r, xh = _rms_parts(x_ref[...])
        dh = a_ref[...]
        acc[...] += _colsum8(dh * xh)
        gx_ref[...] = dx1_ref[...] + _rms_bwd(xh, r, w_ref[...], dh)

        @pl.when(i == nt - 1)
        def _():
            gw_ref[...] = jnp.sum(acc[...], axis=0, keepdims=True)

    row = pl.BlockSpec((tm, D), lambda i: (i, 0))
    vec = pl.BlockSpec((1, D), lambda i: (0, 0))
    return pl.pallas_call(
        body,
        grid=(nt,),
        in_specs=[row, row, row, vec],
        out_specs=[row, vec],
        out_shape=[SDS((S, D), f32), SDS((1, D), f32)],
        scratch_shapes=[pltpu.VMEM((SUBLANE, D), f32)],
        compiler_params=_cparams(("arbitrary",)),
        name="first_bwd",
    )(x, dx1, dh, w_pre)


def _t5_bucket(dist):
    n = jnp.maximum(dist, 0)
    nf = jnp.maximum(n, 1).astype(f32)
    large = MAX_EXACT + (jnp.log(nf / MAX_EXACT) / math.log(MAX_DISTANCE / MAX_EXACT)
                         * (NUM_BUCKETS - MAX_EXACT)).astype(jnp.int32)
    large = jnp.minimum(large, NUM_BUCKETS - 1)
    return jnp.where(n < MAX_EXACT, n, large)


def _bias_consts(d):
    blk = ATTN_BLOCK
    rel = jnp.arange(blk)[:, None] + blk - jnp.arange(2 * blk)[None, :]
    in_win = (rel >= 0) & (rel <= blk)
    bucket = _t5_bucket(rel * d).reshape(1, -1)
    onehot = (bucket == jnp.arange(NUM_BUCKETS)[:, None]).astype(f32)
    return onehot, in_win.astype(f32).reshape(1, -1)


def _bias_build(tab_t, onehot, maskf, name, after):
    H = tab_t.shape[0]

    def body(t_ref, oh_ref, m_ref, after_ref, o_ref):
        b = jnp.dot(t_ref[...], oh_ref[...], precision=HIGHEST, preferred_element_type=f32)
        o_ref[...] = jnp.where(m_ref[...] > 0.5, b, NEG_INF)

    vm = pl.BlockSpec(memory_space=pltpu.VMEM)
    return pl.pallas_call(body, out_shape=SDS((H, onehot.shape[1]), f32), name=name,
                          in_specs=[vm, vm, vm, pl.BlockSpec(memory_space=pl.ANY)], out_specs=vm,
                          )(tab_t, onehot, maskf, after)


def _bias_grad(dbias_flat, onehot, name):
    H = dbias_flat.shape[0]

    def body(g_ref, oh_ref, o_ref):
        o_ref[...] = lax.dot_general(oh_ref[...], g_ref[...], NT, precision=HIGHEST, preferred_element_type=f32)

    return pl.pallas_call(body, out_shape=SDS((NUM_BUCKETS, H), f32), name=name)(dbias_flat, onehot)


ATTN_TILE = 512
ATTN_SUB = ATTN_TILE // ATTN_BLOCK


def _qkv_specs(nt):
    tile = (ATTN_TILE, LANE)
    blk = (ATTN_BLOCK, LANE)
    cur = lambda off: (lambda h, t: (jnp.minimum(t, nt - 1), off + h))
    prev = lambda off: (lambda h, t: (jnp.maximum(jnp.minimum(t, nt - 1) * ATTN_SUB - 1, 0), off + h))
    return [pl.BlockSpec(tile, cur(0)), pl.BlockSpec(blk, prev(4)), pl.BlockSpec(tile, cur(4)),
            pl.BlockSpec(blk, prev(8)), pl.BlockSpec(tile, cur(8))]


def _head_masks():
    lane = lax.broadcasted_iota(jnp.int32, (ATTN_BLOCK, LANE), 1)
    return lane < HEAD_DIM


def _stack_heads(x2, low):
    zero = jnp.zeros_like(x2)
    return jnp.concatenate([jnp.where(low, x2, zero), jnp.where(low, zero, x2)], axis=0)


def _attn_fwd(qkv, bias, bps, name, after=None):
    S = qkv.shape[0]
    nt = S // ATTN_TILE
    scale = HEAD_DIM ** -0.5

    def body(q_ref, kp_ref, kc_ref, vp_ref, vc_ref, b_ref, *rest):
        o_ref, l_ref = rest[-2:]
        t = pl.program_id(1)
        kk = jnp.concatenate([kp_ref[...], kc_ref[...]], axis=0)
        vv = jnp.concatenate([vp_ref[...], vc_ref[...]], axis=0)
        low = _head_masks()
        col = lax.broadcasted_iota(jnp.int32, (2 * ATTN_BLOCK, 2 * ATTN_BLOCK), 1)
        bias2 = b_ref[...].reshape(2 * ATTN_BLOCK, 2 * ATTN_BLOCK)
        for b in range(ATTN_SUB):
            lo = b * ATTN_BLOCK
            rows = slice(lo, lo + ATTN_BLOCK)
            keys = slice(lo, lo + 2 * ATTN_BLOCK)
            dead = jnp.logical_and((t * ATTN_SUB + b) % bps == 0, col < ATTN_BLOCK)
            q2 = _stack_heads(q_ref[rows, :], low)
            kb, vb = kk[keys], vv[keys]
            s = lax.dot_general(q2, kb, NT, preferred_element_type=f32) * scale + bias2
            s = jnp.where(dead, NEG_INF, s)
            m = jnp.max(s, axis=-1, keepdims=True)
            p = jnp.exp(s - m)
            l = jnp.sum(p, axis=-1, keepdims=True)
            o2 = jnp.dot(p.astype(bf16), vb, preferred_element_type=f32) / l
            lse = m + jnp.log(l)
            o_ref[rows, :] = jnp.where(low, o2[:ATTN_BLOCK], o2[ATTN_BLOCK:])
            l_ref[rows, :] = jnp.where(low, lse[:ATTN_BLOCK], lse[ATTN_BLOCK:])

    tile = pl.BlockSpec((ATTN_TILE, LANE), lambda h, t: (t, h))
    return pl.pallas_call(
        body,
        grid=(4, nt),
        in_specs=_qkv_specs(nt) + [pl.BlockSpec((2, ATTN_BLOCK, 2 * ATTN_BLOCK), lambda h, t: (h, 0, 0))]
        + ([] if after is None else [pl.BlockSpec(memory_space=pl.ANY)]),
        out_specs=[tile, tile],
        out_shape=[SDS((S, ATTN_OUT), f32), SDS((S, ATTN_OUT), f32)],
        compiler_params=_cparams(("parallel", "parallel")),
        name=name,
    )(qkv, qkv, qkv, qkv, qkv, bias, *([] if after is None else [after]))


def _attn_bwd(qkv, bias, do, dvec, lse, bps, name):
    S = qkv.shape[0]
    nt = S // ATTN_TILE
    scale = HEAD_DIM ** -0.5

    def assemble(parts):
        rows = [parts[0][:ATTN_BLOCK]]
        for b in range(ATTN_SUB - 1):
            rows.append(parts[b][ATTN_BLOCK:] + parts[b + 1][:ATTN_BLOCK])
        rows.append(parts[-1][ATTN_BLOCK:])
        return rows

    def body(q_ref, kp_ref, kc_ref, vp_ref, vc_ref, b_ref, do_ref, dvec_ref, lse_ref,
             dq_ref, dk_ref, dv_ref, db_ref, ck, cv):
        t = pl.program_id(1)
        last = ATTN_TILE - ATTN_BLOCK

        @pl.when(t == 0)
        def _():
            ck[...] = jnp.zeros_like(ck)
            cv[...] = jnp.zeros_like(cv)
            db_ref[...] = jnp.zeros_like(db_ref)

        @pl.when(t < nt)
        def _():
            kk = jnp.concatenate([kp_ref[...], kc_ref[...]], axis=0)
            vv = jnp.concatenate([vp_ref[...], vc_ref[...]], axis=0)
            low = _head_masks()
            col = lax.broadcasted_iota(jnp.int32, (2 * ATTN_BLOCK, 2 * ATTN_BLOCK), 1)
            bias2 = b_ref[...].reshape(2 * ATTN_BLOCK, 2 * ATTN_BLOCK)
            dk_parts, dv_parts = [], []
            dsum = None
            for b in range(ATTN_SUB):
                lo = b * ATTN_BLOCK
                rows = slice(lo, lo + ATTN_BLOCK)
                keys = slice(lo, lo + 2 * ATTN_BLOCK)
                dead = jnp.logical_and((t * ATTN_SUB + b) % bps == 0, col < ATTN_BLOCK)
                q2 = _stack_heads(q_ref[rows, :], low)
                do2 = _stack_heads(do_ref[rows, :].astype(bf16), low)
                kb, vb = kk[keys], vv[keys]
                dvec2 = dvec_ref[rows, :]
                lse2 = lse_ref[rows, :]
                per_row = lambda t2: jnp.concatenate([t2[:, 0:1], t2[:, HEAD_DIM:HEAD_DIM + 1]], axis=0)
                s = lax.dot_general(q2, kb, NT, preferred_element_type=f32) * scale + bias2
                s = jnp.where(dead, NEG_INF, s)
                p = jnp.exp(s - per_row(lse2))
                dp = lax.dot_general(do2, vb, NT, preferred_element_type=f32)
                ds = p * (dp - per_row(dvec2))
                dsum = ds if dsum is None else dsum + ds
                dsb = ds.astype(bf16)
                dq2 = jnp.dot(dsb, kb, preferred_element_type=f32) * scale
                dq_ref[rows, :] = jnp.where(low, dq2[:ATTN_BLOCK], dq2[ATTN_BLOCK:]).astype(bf16)
                dk_parts.append(lax.dot_general(dsb, q2, TN, preferred_element_type=f32) * scale)
                dv_parts.append(lax.dot_general(p.astype(bf16), do2, TN, preferred_element_type=f32))
            db_ref[...] += dsum.reshape(2, ATTN_BLOCK, 2 * ATTN_BLOCK)
            for parts, carry, out_ref in ((dk_parts, ck, dk_ref), (dv_parts, cv, dv_ref)):
                rws = assemble(parts)
                out_ref[:last, :] = carry[:last, :].astype(bf16)
                out_ref[last:, :] = (carry[last:, :] + rws[0]).astype(bf16)
                for b in range(ATTN_SUB):
                    carry[b * ATTN_BLOCK:(b + 1) * ATTN_BLOCK, :] = rws[b + 1]

        @pl.when(t == nt)
        def _():
            dk_ref[...] = ck[...].astype(bf16)
            dv_ref[...] = cv[...].astype(bf16)

    tile = (ATTN_TILE, LANE)
    cur = pl.BlockSpec(tile, lambda h, t: (jnp.minimum(t, nt - 1), h))
    lag = pl.BlockSpec(tile, lambda h, t: (jnp.maximum(t - 1, 0), h))
    bspec = pl.BlockSpec((2, ATTN_BLOCK, 2 * ATTN_BLOCK), lambda h, t: (h, 0, 0))
    return pl.pallas_call(
        body,
        grid=(4, nt + 1),
        in_specs=_qkv_specs(nt) + [bspec, cur, cur, cur],
        out_specs=[cur, lag, lag, bspec],
        out_shape=[SDS((S, ATTN_OUT), bf16), SDS((S, ATTN_OUT), bf16), SDS((S, ATTN_OUT), bf16),
                   SDS((8, ATTN_BLOCK, 2 * ATTN_BLOCK), f32)],
        scratch_shapes=[pltpu.VMEM(tile, f32), pltpu.VMEM(tile, f32)],
        compiler_params=_cparams(("parallel", "arbitrary")),
        name=name,
    )(qkv, qkv, qkv, qkv, qkv, bias, do, dvec, lse)


def _attn_merge(o0, o1, o2, l0, l1, l2):
    S, W = o0.shape
    R = PERM_ROWS

    def body(o0_ref, o1_ref, o2_ref, l0_ref, l1_ref, l2_ref, y_ref, yb_ref, w0_ref, w1_ref, w2_ref,
             so1, so2, sl1, sl2):
        _to_natural(o1_ref, so1, 4)
        _to_natural(l1_ref, sl1, 4)
        _to_natural(o2_ref, so2, 16)
        _to_natural(l2_ref, sl2, 16)
        a, b, c = l0_ref[...], sl1[...], sl2[...]
        m = jnp.maximum(jnp.maximum(a, b), c)
        ea, eb, ec = jnp.exp(a - m), jnp.exp(b - m), jnp.exp(c - m)
        den = (ea + eb) + ec
        w0, w1, w2 = ea / den, eb / den, ec / den
        y = (w0 * o0_ref[...] + w1 * so1[...]) + w2 * so2[...]
        y_ref[...] = y
        yb_ref[...] = y.astype(bf16)
        w0_ref[...] = w0
        w1_ref[...] = w1
        w2_ref[...] = w2

    nat = pl.BlockSpec((R, LANE), lambda i, j: (i, j))
    v4 = lambda t: t.reshape(4, S // 4, W)
    v16 = lambda t: t.reshape(16, S // 16, W)
    return pl.pallas_call(
        body,
        grid=(S // R, W // LANE),
        in_specs=[nat, _perm_spec(4), _perm_spec(16)] * 2,
        out_specs=[nat] * 5,
        out_shape=[SDS((S, W), f32), SDS((S, W), bf16)] + [SDS((S, W), f32)] * 3,
        scratch_shapes=[pltpu.VMEM((R, LANE), f32)] * 4,
        compiler_params=_cparams(("parallel", "parallel")),
        name="attn_merge",
    )(o0, v4(o1), v16(o2), l0, v4(l1), v16(l2))


def _attn_merge_bwd(dy, y, w0, w1, w2, after=None):
    S, W = dy.shape
    R = PERM_ROWS

    def body(dy_ref, y_ref, w0_ref, w1_ref, w2_ref, *rest):
        a0, a1, a2, b0, b1, b2, sa, sb = rest[-8:]
        dyv = dy_ref[...]
        r = lax.broadcasted_iota(jnp.int32, (LANE, LANE), 0) // HEAD_DIM
        c = lax.broadcasted_iota(jnp.int32, (LANE, LANE), 1) // HEAD_DIM
        seg = jnp.where(r == c, 1.0, 0.0).astype(f32)
        cbar = jnp.dot(dyv * y_ref[...], seg, precision=HIGHEST, preferred_element_type=f32)
        w = w0_ref[...]
        a0[...] = (w * dyv).astype(bf16)
        b0[...] = w * cbar
        for d, w_ref, a_ref, b_ref in ((4, w1_ref, a1, b1), (16, w2_ref, a2, b2)):
            w = w_ref[...]
            sa[...] = w * dyv
            sb[...] = w * cbar
            n = R // d
            for k in range(d):
                rows = pl.ds(k, n, stride=d)
                a_ref[k] = sa[rows, :].astype(bf16)
                b_ref[k] = sb[rows, :]

    nat = pl.BlockSpec((R, LANE), lambda i, j: (i, j))
    shapes = lambda dt: [SDS((S, W), dt), SDS((4, S // 4, W), dt), SDS((16, S // 16, W), dt)]
    outs = pl.pallas_call(
        body,
        grid=(S // R, W // LANE),
        in_specs=[nat] * 5 + ([] if after is None else [pl.BlockSpec(memory_space=pl.ANY)]),
        out_specs=[nat, _perm_spec(4), _perm_spec(16)] * 2,
        out_shape=shapes(bf16) + shapes(f32),
        scratch_shapes=[pltpu.VMEM((R, LANE), f32)] * 2,
        compiler_params=_cparams(("parallel", "parallel")),
        name="attn_merge_bwd",
    )(dy, y, w0, w1, w2, *([] if after is None else [after]))
    return [t.reshape(S, W) for t in outs]


HGRN_SB = 256
HGRN_PAIR = 4


def _chunk_masks():
    r = jnp.arange(HGRN_SB)[:, None]
    c = jnp.arange(HGRN_SB)[None, :]
    same = (r // HGRN_CHUNK) == (c // HGRN_CHUNK)
    return jnp.stack([same & (c <= r), same, same & (c >= r)]).astype(bf16)


def _mask_dot(mask, x):
    hi = x.astype(bf16)
    r1 = x - hi.astype(f32)
    mid = r1.astype(bf16)
    lo = (r1 - mid.astype(f32)).astype(bf16)
    p = jnp.dot(mask, jnp.concatenate([hi, mid, lo], axis=1), preferred_element_type=f32)
    n = x.shape[1]
    return (p[:, :n] + p[:, n:2 * n]) + p[:, 2 * n:]


def _hgrn_prep(q_raw, f_raw, lbv, tril, same):
    sq = _sigmoid(q_raw)
    qs = q_raw * sq
    sig = _sigmoid(f_raw)
    f = lbv + (1.0 - lbv) * sig
    g = jnp.log(f)
    k = 1.0 - f
    G = _mask_dot(tril, g)
    GL = _mask_dot(same, g)
    eG = jnp.exp(G)
    einv = jnp.exp(-G)
    edec = jnp.exp(GL - G)
    return dict(sq=sq, qs=qs, sig=sig, f=f, k=k, eG=eG, einv=einv, edec=edec, eGL=jnp.exp(GL),
                qt=qs * eG, kt=k * einv, kd=k * edec)


def _ride_split(ride, rest, n_out, n_scratch):
    if ride is None:
        return None, rest[:n_out], None, rest[n_out:], None
    return rest[0], rest[1:1 + n_out], rest[1 + n_out], rest[2 + n_out:2 + n_out + n_scratch], rest[2 + n_out + n_scratch:]


def _hgrn_fwd(hg, lb, normw, ride=None):
    S = hg.shape[0]
    sb = HGRN_SB
    nsb = S // sb
    nch = sb // HGRN_CHUNK

    def body(q_ref, f_ref, v_ref, og_ref, lb_ref, nw_ref, m_ref, *rest):
        src_ref, (y_ref, o_ref, ck_ref), got_ref, (st,), sems = _ride_split(ride, rest, 3, 1)
        j = pl.program_id(1)
        if ride is not None:
            @pl.when(jnp.logical_and(pl.program_id(0) == 0, j == 0))
            def _():
                _chip_start(src_ref, got_ref, sems[0], sems[1], ride[1])

        @pl.when(j == 0)
        def _():
            st[...] = jnp.zeros_like(st)

        tril_m = m_ref[0]
        tril = tril_m.astype(f32) > 0.5

        def one_head(hh):
            cols = slice(hh * LANE, (hh + 1) * LANE)
            ST = st[hh]
            ck_ref[hh, 0] = ST
            pr = _hgrn_prep(q_ref[:, cols], f_ref[:, cols], lb_ref[:, cols], tril_m, m_ref[1])
            qtb, ktb, kdb = pr["qt"].astype(bf16), pr["kt"].astype(bf16), pr["kd"].astype(bf16)
            eGL = pr["eGL"]
            vb = v_ref[:, cols].astype(bf16)
            A = jnp.where(tril, lax.dot_general(qtb, ktb, NT, preferred_element_type=f32), 0.0)
            o = jnp.dot(A.astype(bf16), vb, preferred_element_type=f32)
            outs = []
            for ci in range(nch):
                lo = ci * HGRN_CHUNK
                sl = slice(lo, lo + HGRN_CHUNK)
                outs.append(o[sl] + lax.dot_general(qtb[sl], ST.astype(bf16), NT, preferred_element_type=f32))
                ST = ST * eGL[lo:lo + 1, :] + lax.dot_general(vb[sl], kdb[sl], TN, preferred_element_type=f32)
            st[hh] = ST
            of = jnp.concatenate(outs, axis=0)
            o_ref[:, cols] = of
            rms = lax.rsqrt(jnp.mean(of * of, axis=-1, keepdims=True) + EPS)
            ogv = og_ref[:, cols]
            y_ref[:, cols] = ((of * rms * nw_ref[...]) * (ogv * _sigmoid(ogv))).astype(bf16)

        for hh in range(HGRN_PAIR):
            one_head(hh)

        if ride is not None:
            @pl.when(jnp.logical_and(pl.program_id(0) == ngrp - 1, j == nsb - 1))
            def _():
                _chip_finish(src_ref, got_ref, sems[0], sems[1], ride[1])

    wide = HGRN_PAIR * LANE
    ngrp = 4 // HGRN_PAIR
    col = lambda off: pl.BlockSpec((sb, wide), lambda h, j: (j, off // HGRN_PAIR + h))
    riding = ride is not None
    res = pl.pallas_call(
        body,
        grid=(ngrp, nsb),
        in_specs=[col(0), col(4), col(8), col(12), pl.BlockSpec((1, wide), lambda h, j: (0, h)),
                  pl.BlockSpec((1, LANE), lambda h, j: (0, 0)),
                  pl.BlockSpec((3, sb, sb), lambda h, j: (0, 0, 0))] + ([_ANY] if riding else []),
        out_specs=[col(0), col(0), pl.BlockSpec((HGRN_PAIR, 1, LANE, LANE), lambda h, j: (h, j, 0, 0))]
        + ([_ANY] if riding else []),
        out_shape=[SDS((S, HGRN_W), bf16), SDS((S, HGRN_W), f32), SDS((4, nsb, LANE, LANE), f32)]
        + ([_chip_out_shape(*ride)] if riding else []),
        scratch_shapes=[pltpu.VMEM((HGRN_PAIR, LANE, LANE), f32)] + (list(_CHIP_SEMS) if riding else []),
        compiler_params=_cparams(("arbitrary", "arbitrary") if riding else ("parallel", "arbitrary")),
        name="hgrn_fwd",
    )(hg, hg, hg, hg, lb, normw, _chunk_masks(), *([ride[0]] if riding else []))
    return tuple(res) if riding else (*res, None)


def _hgrn_bwd(hg, o_raw, dy, ck, lb, normw, ride=None):
    S = hg.shape[0]
    sb = HGRN_SB
    nsb = S // sb
    nch = sb // HGRN_CHUNK

    def body(q_ref, f_ref, v_ref, og_ref, o_ref, dy_ref, ck_ref, lb_ref, nw_ref, m_ref, *rest):
        src_ref, outs, got_ref, (dst, alb, anw), sems = _ride_split(ride, rest, 6, 3)
        dq_ref, df_ref, dv_ref, dog_ref, glb_ref, gnw_ref = outs
        j = pl.program_id(1)
        if ride is not None:
            @pl.when(jnp.logical_and(pl.program_id(0) == 0, j == 0))
            def _():
                _chip_start(src_ref, got_ref, sems[0], sems[1], ride[1])

        @pl.when(j == 0)
        def _():
            dst[...] = jnp.zeros_like(dst)
            alb[...] = jnp.zeros_like(alb)
            anw[...] = jnp.zeros_like(anw)

        tril_m = m_ref[0]
        tril = tril_m.astype(f32) > 0.5
        nw = nw_ref[...]

        def one_head(hh):
            cols = slice(hh * LANE, (hh + 1) * LANE)
            lbv = lb_ref[:, cols]
            q_raw = q_ref[:, cols]
            pr = _hgrn_prep(q_raw, f_ref[:, cols], lbv, tril_m, m_ref[1])
            qt, kt, kd, eGL = pr["qt"], pr["kt"], pr["kd"], pr["eGL"]
            qtb, ktb, kdb = qt.astype(bf16), kt.astype(bf16), kd.astype(bf16)
            vb = v_ref[:, cols].astype(bf16)

            o = o_ref[:, cols]
            ogv = og_ref[:, cols]
            sog = _sigmoid(ogv)
            rms = lax.rsqrt(jnp.mean(o * o, axis=-1, keepdims=True) + EPS)
            oh = o * rms
            dyv = dy_ref[:, cols]
            dog_ref[:, cols] = (dyv * (oh * nw) * (sog * (1.0 + ogv * (1.0 - sog)))).astype(bf16)
            dohw = dyv * (ogv * sog)
            anw[:, cols] += _colsum8(dohw * oh)
            doh = dohw * nw
            do = rms * (doh - oh * jnp.mean(doh * oh, axis=-1, keepdims=True))
            dob = do.astype(bf16)

            Ab = jnp.where(tril, lax.dot_general(qtb, ktb, NT, preferred_element_type=f32), 0.0).astype(bf16)
            dAb = jnp.where(tril, lax.dot_general(dob, vb, NT, preferred_element_type=f32), 0.0).astype(bf16)
            dv_acc = lax.dot_general(Ab, dob, TN, preferred_element_type=f32)
            dqt = jnp.dot(dAb, ktb, preferred_element_type=f32)
            dkt = lax.dot_general(dAb, qtb, TN, preferred_element_type=f32)

            ST = ck_ref[hh, 0]
            states = []
            for ci in range(nch):
                lo = ci * HGRN_CHUNK
                sl = slice(lo, lo + HGRN_CHUNK)
                states.append(ST)
                ST = ST * eGL[lo:lo + 1, :] + lax.dot_general(vb[sl], kdb[sl], TN, preferred_element_type=f32)

            dST = dst[hh]
            dqt_i, dkd_i, dv_i, deg_i = [None] * nch, [None] * nch, [None] * nch, [None] * nch
            for ci in reversed(range(nch)):
                lo = ci * HGRN_CHUNK
                sl = slice(lo, lo + HGRN_CHUNK)
                ST0 = states[ci]
                dSTb = dST.astype(bf16)
                dv_i[ci] = lax.dot_general(kdb[sl], dSTb, NT, preferred_element_type=f32)
                dqt_i[ci] = jnp.dot(dob[sl], ST0.astype(bf16), preferred_element_type=f32)
                dkd_i[ci] = jnp.dot(vb[sl], dSTb, preferred_element_type=f32)
                deg_i[ci] = jnp.broadcast_to(jnp.sum(dST * ST0, axis=0, keepdims=True), (HGRN_CHUNK, LANE))
                dST = dST * eGL[lo:lo + 1, :] + lax.dot_general(dob[sl], qtb[sl], TN, preferred_element_type=f32)
            dst[hh] = dST

            dqt = dqt + jnp.concatenate(dqt_i, axis=0)
            dkd = jnp.concatenate(dkd_i, axis=0)
            dv_ref[:, cols] = (dv_acc + jnp.concatenate(dv_i, axis=0)).astype(bf16)
            deg = jnp.concatenate(deg_i, axis=0)

            dqs = dqt * pr["eG"]
            dkdkd = dkd * kd
            dG = dqt * qt - dkt * kt - dkdkd
            dk = dkt * pr["einv"] + dkd * pr["edec"]
            dGL = _mask_dot(m_ref[1], dkdkd) + eGL * deg
            dg = _mask_dot(m_ref[2], dG) + dGL
            df = dg / pr["f"] - dk
            sig = pr["sig"]
            df_ref[:, cols] = (df * (1.0 - lbv) * (sig * (1.0 - sig))).astype(bf16)
            alb[:, cols] += _colsum8(df * (1.0 - sig))
            sq = pr["sq"]
            dq_ref[:, cols] = (dqs * (sq * (1.0 + q_raw * (1.0 - sq)))).astype(bf16)

        for hh in range(HGRN_PAIR):
            one_head(hh)

        @pl.when(j == nsb - 1)
        def _():
            glb_ref[...] = jnp.broadcast_to(jnp.sum(alb[...], axis=0, keepdims=True), (SUBLANE, wide))
            gnw_ref[...] = jnp.broadcast_to(jnp.sum(anw[...], axis=0, keepdims=True), (SUBLANE, wide))

        if ride is not None:
            @pl.when(jnp.logical_and(pl.program_id(0) == ngrp - 1, j == nsb - 1))
            def _():
                _chip_finish(src_ref, got_ref, sems[0], sems[1], ride[1])

    wide = HGRN_PAIR * LANE
    ngrp = 4 // HGRN_PAIR
    rev = lambda off: pl.BlockSpec((sb, wide), lambda h, j: (nsb - 1 - j, off // HGRN_PAIR + h))
    stat = pl.BlockSpec((SUBLANE, wide), lambda h, j: (0, h))
    riding = ride is not None
    res = pl.pallas_call(
        body,
        grid=(ngrp, nsb),
        in_specs=[rev(0), rev(4), rev(8), rev(12), rev(0), rev(0),
                  pl.BlockSpec((HGRN_PAIR, 1, LANE, LANE), lambda h, j: (h, nsb - 1 - j, 0, 0)),
                  pl.BlockSpec((1, wide), lambda h, j: (0, h)), pl.BlockSpec((1, LANE), lambda h, j: (0, 0)),
                  pl.BlockSpec((3, sb, sb), lambda h, j: (0, 0, 0))]
        + ([_ANY] if riding else []),
        out_specs=[rev(0), rev(0), rev(0), rev(0), stat, stat] + ([_ANY] if riding else []),
        out_shape=[SDS((S, HGRN_W), bf16)] * 4 + [SDS((SUBLANE, HGRN_W), f32)] * 2
        + ([_chip_out_shape(*ride)] if riding else []),
        scratch_shapes=[pltpu.VMEM((HGRN_PAIR, LANE, LANE), f32), pltpu.VMEM((SUBLANE, wide), f32),
                        pltpu.VMEM((SUBLANE, wide), f32)] + (list(_CHIP_SEMS) if riding else []),
        compiler_params=_cparams(("arbitrary", "arbitrary") if riding else ("parallel", "arbitrary")),
        name="hgrn_bwd",
    )(hg, hg, hg, hg, o_raw, dy, ck, lb, normw, _chunk_masks(), *([ride[0]] if riding else []))
    return tuple(res) if riding else (*res, None)


def _lb_fwd(raw):
    def body(r_ref, o_ref):
        r = r_ref[...]
        m = jnp.max(r, axis=0, keepdims=True)
        e = jnp.exp(r - m)
        o_ref[...] = (e / jnp.sum(e, axis=0, keepdims=True))[0:1]

    return pl.pallas_call(body, out_shape=SDS((1, raw.shape[1]), f32), name="lb_fwd")(raw)


def _lb_bwd(raw, dlb):
    def body(r_ref, d_ref, o_ref):
        r = r_ref[...]
        m = jnp.max(r, axis=0, keepdims=True)
        e = jnp.exp(r - m)
        s = e / jnp.sum(e, axis=0, keepdims=True)
        s0 = s[0:1]
        onehot0 = jnp.where(lax.broadcasted_iota(jnp.int32, r.shape, 0) == 0, 1.0, 0.0)
        o_ref[...] = d_ref[...] * s0 * (onehot0 - s)

    return pl.pallas_call(body, out_shape=SDS(raw.shape, f32), name="lb_bwd")(raw, dlb)


def _gate_fwd(a, b, gc):
    S, D = a.shape
    tm = _pick(S, 512)

    def body(a_ref, b_ref, g0_ref, g1_ref, o_ref):
        s0, s1 = _sigmoid(g0_ref[...].astype(f32)), _sigmoid(g1_ref[...].astype(f32))
        o_ref[...] = (s0 * a_ref[...].astype(f32) + s1 * b_ref[...].astype(f32)).astype(bf16)

    row = pl.BlockSpec((tm, D), lambda i: (i, 0))
    return pl.pallas_call(
        body,
        grid=(S // tm,),
        in_specs=[row, row, row, pl.BlockSpec((tm, D), lambda i: (i, 1))],
        out_specs=row,
        out_shape=SDS((S, D), bf16),
        compiler_params=_cparams(("parallel",)),
        name="gate_fwd",
    )(a, b, gc, gc)


def _gate_bwd(dm, a, b, gc):
    S, D = a.shape
    tm = _pick(S, 512)

    def body(dm_ref, a_ref, b_ref, g0_ref, g1_ref, da_ref, db_ref, dg_ref):
        dmv = dm_ref[...].astype(f32)
        s0, s1 = _sigmoid(g0_ref[...].astype(f32)), _sigmoid(g1_ref[...].astype(f32))
        da_ref[...] = (dmv * s0).astype(bf16)
        db_ref[...] = (dmv * s1).astype(bf16)
        dg_ref[:, :D] = (dmv * a_ref[...].astype(f32) * (s0 * (1.0 - s0))).astype(bf16)
        dg_ref[:, D:] = (dmv * b_ref[...].astype(f32) * (s1 * (1.0 - s1))).astype(bf16)

    row = pl.BlockSpec((tm, D), lambda i: (i, 0))
    wide = pl.BlockSpec((tm, 2 * D), lambda i: (i, 0))
    return pl.pallas_call(
        body,
        grid=(S // tm,),
        in_specs=[row, row, row, row, pl.BlockSpec((tm, D), lambda i: (i, 1))],
        out_specs=[row, row, wide],
        out_shape=[SDS((S, D), bf16), SDS((S, D), bf16), SDS((S, 2 * D), bf16)],
        compiler_params=_cparams(("parallel",)),
        name="gate_bwd",
    )(dm, a, b, gc, gc)


CONV_ROWS = 512
INV_SQRT2 = 0.7071067811865476
INV_SQRT_2PI = 0.3989422804014327


CONV_HALO = 16


def _tile8(a, rows):
    return jnp.tile(a, (rows // a.shape[0], 1))


def _conv_rows(u_ref, w, b, r0, first):
    R = CONV_ROWS
    cur = u_ref[pl.ds(r0, R), :].astype(f32)
    prev8 = u_ref[pl.ds(pl.multiple_of(jnp.maximum(r0 - CONV_HALO, 0), CONV_HALO), CONV_HALO), :].astype(f32)
    prev8 = jnp.where(first, 0.0, prev8)
    row = lax.broadcasted_iota(jnp.int32, (R, LANE), 0)
    x1 = jnp.where(row < 1, _tile8(pltpu.roll(prev8, 1, 0), R), pltpu.roll(cur, 1, 0))
    x2 = jnp.where(row < 2, _tile8(pltpu.roll(prev8, 2, 0), R), pltpu.roll(cur, 2, 0))
    c = ((b + w[0:1] * x2) + w[1:2] * x1) + w[2:3] * cur
    return c, x2, x1, cur


def _conv_fwd(ug, uv, wg, wv, bg, bv):
    S, F = ug.shape
    nchunk = S // CONV_ROWS

    def body(ug_ref, uv_ref, wg_ref, wv_ref, bg_ref, bv_ref, o_ref):
        wgv, wvv, bgv, bvv = wg_ref[...], wv_ref[...], bg_ref[...], bv_ref[...]

        def step(ci, carry):
            r0 = pl.multiple_of(ci * CONV_ROWS, CONV_ROWS)
            cg = _conv_rows(ug_ref, wgv, bgv, r0, ci == 0)[0]
            cv = _conv_rows(uv_ref, wvv, bvv, r0, ci == 0)[0]
            gelu = 0.5 * cg * (1.0 + lax.erf(cg * INV_SQRT2))
            o_ref[pl.ds(r0, CONV_ROWS), :] = (gelu * cv).astype(bf16)
            return carry

        lax.fori_loop(0, nchunk, step, 0)

    col = pl.BlockSpec((S, LANE), lambda j: (0, j))
    w3 = pl.BlockSpec((3, LANE), lambda j: (0, j))
    b1 = pl.BlockSpec((1, LANE), lambda j: (0, j))
    return pl.pallas_call(
        body,
        grid=(F // LANE,),
        in_specs=[col, col, w3, w3, b1, b1],
        out_specs=col,
        out_shape=SDS((S, F), bf16),
        compiler_params=_cparams(("parallel",), VMEM_BIG),
        name="conv_fwd",
    )(ug, uv, wg, wv, bg, bv)


def _conv_bwd(ug, uv, dact, wg, wv, bg, bv):
    S, F = ug.shape
    R = CONV_ROWS
    nchunk = S // R

    def body(ug_ref, uv_ref, da_ref, wg_ref, wv_ref, bg_ref, bv_ref, dug_ref, duv_ref, sg_ref, sv_ref, dcg, dcv):
        wgv, wvv, bgv, bvv = wg_ref[...], wv_ref[...], bg_ref[...], bv_ref[...]
        zero = jnp.zeros((SUBLANE, LANE), f32)

        def fwd_step(ci, acc):
            r0 = pl.multiple_of(ci * R, R)
            cg, g2, g1, g0 = _conv_rows(ug_ref, wgv, bgv, r0, ci == 0)
            cv, v2, v1, v0 = _conv_rows(uv_ref, wvv, bvv, r0, ci == 0)
            da = da_ref[pl.ds(r0, R), :].astype(f32)
            cdf = 0.5 * (1.0 + lax.erf(cg * INV_SQRT2))
            pdf = INV_SQRT_2PI * jnp.exp(-0.5 * cg * cg)
            dg = da * cv * (cdf + cg * pdf)
            dv = da * (cg * cdf)
            dcg[pl.ds(r0, R), :] = dg
            dcv[pl.ds(r0, R), :] = dv
            new = (acc[0] + _colsum8(dg * g2), acc[1] + _colsum8(dg * g1), acc[2] + _colsum8(dg * g0),
                   acc[3] + _colsum8(dg),
                   acc[4] + _colsum8(dv * v2), acc[5] + _colsum8(dv * v1), acc[6] + _colsum8(dv * v0),
                   acc[7] + _colsum8(dv))
            return new

        acc = lax.fori_loop(0, nchunk, fwd_step, (zero,) * 8)
        rows = lax.broadcasted_iota(jnp.int32, (SUBLANE, LANE), 0)

        def stats(parts):
            out = jnp.zeros((SUBLANE, LANE), f32)
            for k, pt in enumerate(parts):
                out = jnp.where(rows == k, jnp.sum(pt, axis=0, keepdims=True), out)
            return out

        sg_ref[...] = stats(acc[0:4])
        sv_ref[...] = stats(acc[4:8])

        def du_rows(dc, w, r0, last):
            cur = dc[pl.ds(r0, R), :]
            nxt = dc[pl.ds(pl.multiple_of(jnp.minimum(r0 + R, S - SUBLANE), SUBLANE), SUBLANE), :]
            nxt = jnp.where(last, 0.0, nxt)
            row = lax.broadcasted_iota(jnp.int32, (R, LANE), 0)
            y1 = jnp.where(row >= R - 1, _tile8(pltpu.roll(nxt, SUBLANE - 1, 0), R), pltpu.roll(cur, R - 1, 0))
            y2 = jnp.where(row >= R - 2, _tile8(pltpu.roll(nxt, SUBLANE - 2, 0), R), pltpu.roll(cur, R - 2, 0))
            return w[2:3] * cur + w[1:2] * y1 + w[0:1] * y2

        def bwd_step(ci, carry):
            r0 = pl.multiple_of(ci * R, R)
            last = ci == nchunk - 1
            dug_ref[pl.ds(r0, R), :] = du_rows(dcg, wgv, r0, last).astype(bf16)
            duv_ref[pl.ds(r0, R), :] = du_rows(dcv, wvv, r0, last).astype(bf16)
            return carry

        lax.fori_loop(0, nchunk, bwd_step, 0)

    col = pl.BlockSpec((S, LANE), lambda j: (0, j))
    w3 = pl.BlockSpec((3, LANE), lambda j: (0, j))
    b1 = pl.BlockSpec((1, LANE), lambda j: (0, j))
    st = pl.BlockSpec((SUBLANE, LANE), lambda j: (0, j))
    return pl.pallas_call(
        body,
        grid=(F // LANE,),
        in_specs=[col, col, col, w3, w3, b1, b1],
        out_specs=[col, col, st, st],
        out_shape=[SDS((S, F), bf16), SDS((S, F), bf16), SDS((SUBLANE, F), f32), SDS((SUBLANE, F), f32)],
        scratch_shapes=[pltpu.VMEM((S, LANE), f32), pltpu.VMEM((S, LANE), f32)],
        compiler_params=_cparams(("parallel",), VMEM_BIG),
        name="conv_bwd",
    )(ug, uv, dact, wg, wv, bg, bv)


def _adam_math(w, g, m, v):
    m = ADAM_B1 * m + (1.0 - ADAM_B1) * g
    v = ADAM_B2 * v + (1.0 - ADAM_B2) * (g * g)
    m_hat = m / (1.0 - ADAM_B1 ** ADAM_STEP)
    v_hat = v / (1.0 - ADAM_B2 ** ADAM_STEP)
    delta = -ADAM_LR * (m_hat / (jnp.sqrt(v_hat) + ADAM_EPS) + ADAM_WD * w)
    return delta, m, v


def _adamw(w, m, v, g, name):
    R, C = w.shape
    parts = g.ndim == 3
    tr = R
    for t in (256, 128, 64, 32, 16):
        if R % t == 0 and R > t:
            tr = t
            break

    def body(w_ref, m_ref, v_ref, g_ref, go_ref, d_ref, mo_ref, vo_ref):
        if parts:
            gv = ((g_ref[0].astype(f32) + g_ref[1].astype(f32)) + g_ref[2].astype(f32)) + g_ref[3].astype(f32)
        else:
            gv = g_ref[...]
        go_ref[...] = gv
        d, mn, vn = _adam_math(w_ref[...], gv, m_ref[...], v_ref[...])
        d_ref[...] = d
        mo_ref[...] = mn
        vo_ref[...] = vn

    row = pl.BlockSpec((tr, C), lambda i: (i, 0))
    gspec = pl.BlockSpec((4, tr, C), lambda i: (0, i, 0)) if parts else row
    return pl.pallas_call(
        body,
        grid=(R // tr,),
        in_specs=[row, row, row, gspec],
        out_specs=[row] * 4,
        out_shape=[SDS((R, C), f32)] * 4,
        compiler_params=_cparams(("parallel",)),
        name=name,
    )(w, m, v, g)


def _sum8(parts, name):
    _, _, R, C = parts.shape

    def body(p_ref, o_ref):
        acc = p_ref[0, 0]
        for c in range(2):
            for k in range(4):
                if c or k:
                    acc = acc + p_ref[c, k]
        o_ref[...] = acc

    return pl.pallas_call(body, out_shape=SDS((R, C), f32), name=name)(parts)


def _pair_add(by_core, b, name):
    _, K, R, C = by_core.shape
    tr = R // 2 if R % 32 == 0 else R

    def body(c_ref, a_ref, b_ref, o_ref):
        o_ref[...] = (a_ref[0].astype(f32) + b_ref[...].astype(f32)).astype(bf16)

    blk = pl.BlockSpec((1, tr, C), lambda k, i, c: (k, i, 0))
    return pl.pallas_call(
        body,
        grid_spec=pltpu.PrefetchScalarGridSpec(
            num_scalar_prefetch=1,
            grid=(K, R // tr),
            in_specs=[pl.BlockSpec((1, 1, tr, C), lambda k, i, c: (c[0], k, i, 0)), blk],
            out_specs=blk,
        ),
        out_shape=SDS((K, R, C), bf16),
        compiler_params=_cparams(("parallel", "parallel")),
        name=name,
    )(lax.axis_index("c").astype(jnp.int32).reshape(1), by_core, b)


_ANY = pl.BlockSpec(memory_space=pl.ANY)


def _chip_copies(src_ref, out_ref, send_sems, recv_sems, gather):
    x, y, c = lax.axis_index("x"), lax.axis_index("y"), lax.axis_index("c")
    mine = 2 * x + y

    def piece(k):
        return src_ref if gather else src_ref.at[k]

    sends, recvs = [], []
    for j, (px, py) in enumerate([(1 - x, y), (x, 1 - y), (1 - x, 1 - y)]):
        sends.append(pltpu.make_async_remote_copy(
            src_ref=piece(2 * px + py), dst_ref=out_ref.at[mine], send_sem=send_sems.at[j],
            recv_sem=recv_sems.at[j], device_id=(px, py, c), device_id_type=MESH))
        recvs.append(pltpu.make_async_remote_copy(
            src_ref=piece(mine), dst_ref=out_ref.at[2 * px + py], send_sem=send_sems.at[j],
            recv_sem=recv_sems.at[j], device_id=(px, py, c), device_id_type=MESH))
    return sends, recvs


def _chip_start(src_ref, out_ref, send_sems, recv_sems, gather):
    for cp in _chip_copies(src_ref, out_ref, send_sems, recv_sems, gather)[0]:
        cp.start()


def _chip_finish(src_ref, out_ref, send_sems, recv_sems, gather):
    sends, recvs = _chip_copies(src_ref, out_ref, send_sems, recv_sems, gather)
    for cp in recvs:
        cp.wait_recv()
    for cp in sends:
        cp.wait_send()


def _chip_out_shape(src, gather):
    return SDS((4,) + tuple(src.shape if gather else src.shape[1:]), src.dtype)


_CHIP_SEMS = [pltpu.SemaphoreType.DMA((3,)), pltpu.SemaphoreType.DMA((3,))]


def _fill_own(out, src, gather):
    mine = 2 * lax.axis_index("x") + lax.axis_index("y")
    own = src if gather else lax.dynamic_index_in_dim(src, mine, axis=0, keepdims=False)
    return lax.dynamic_update_index_in_dim(out, own, mine, axis=0)


def _chip_comm(src, gather, name):
    def body(src_ref, out_ref, send_sems, recv_sems):
        _chip_start(src_ref, out_ref, send_sems, recv_sems, gather)
        _chip_finish(src_ref, out_ref, send_sems, recv_sems, gather)

    out = pl.pallas_call(
        body,
        in_specs=[_ANY],
        out_specs=_ANY,
        out_shape=_chip_out_shape(src, gather),
        scratch_shapes=list(_CHIP_SEMS),
        name=name,
    )(src)
    return _fill_own(out, src, gather)


_HBM = pl.BlockSpec(memory_space=pltpu.HBM)
_SEM = pl.BlockSpec(memory_space=pltpu.SEMAPHORE)
_EFFECT = pltpu.SideEffectType.DATAFLOW_SIDE_EFFECTING
_SPLIT_PEERS = {"chip_gather": 3, "chip_xchg": 3, "core_gather": 1, "core_swap": 1}


def _split_land(src, kind):
    if kind == "core_gather":
        return SDS((2,) + tuple(src.shape), src.dtype)
    if kind == "core_swap":
        return SDS(tuple(src.shape[1:]), src.dtype)
    return _chip_out_shape(src, kind == "chip_gather")


def _split_copies(src_ref, land_ref, sems, kind):
    x, y, c = lax.axis_index("x"), lax.axis_index("y"), lax.axis_index("c")
    n = _SPLIT_PEERS[kind]
    if kind == "core_gather":
        routes = [((x, y, 1 - c), src_ref, land_ref.at[c], land_ref.at[1 - c])]
    elif kind == "core_swap":
        routes = [((x, y, 1 - c), src_ref.at[1 - c], land_ref, land_ref)]
    else:
        mine = 2 * x + y
        gather = kind == "chip_gather"
        routes = [((px, py, c), src_ref if gather else src_ref.at[2 * px + py], land_ref.at[mine],
                   land_ref.at[2 * px + py]) for px, py in [(1 - x, y), (x, 1 - y), (1 - x, 1 - y)]]
    sends, recvs = [], []
    for j, (peer, piece, there, here) in enumerate(routes):
        sends.append(pltpu.make_async_remote_copy(src_ref=piece, dst_ref=there, send_sem=sems[j],
                                                  recv_sem=sems[n + j], device_id=peer, device_id_type=MESH))
        recvs.append(pltpu.make_async_remote_copy(src_ref=piece, dst_ref=here, send_sem=sems[j],
                                                  recv_sem=sems[n + j], device_id=peer, device_id_type=MESH))
    return sends, recvs


def _split_start(src, kind, name, after=None):
    land = _split_land(src, kind)
    ns = 2 * _SPLIT_PEERS[kind]
    n_in = 2 if after is None else 3

    def body(*refs):
        src_ref, land_ref = refs[:2]
        outs = refs[n_in:]
        for cp in _split_copies(src_ref, land_ref, outs[:ns], kind)[0]:
            cp.start()
        token = outs[ns + 2]
        token[...] = jnp.zeros_like(token)

    res = pl.pallas_call(
        body,
        name=name,
        out_shape=(pltpu.SemaphoreType.DMA(()),) * ns
        + (pltpu.HBM(src.shape, src.dtype), pltpu.HBM(land.shape, land.dtype), SDS((SUBLANE, LANE), f32)),
        in_specs=(_HBM, _HBM) + (() if after is None else (_ANY,)),
        out_specs=(_SEM,) * ns + (_HBM, _HBM, pl.BlockSpec(memory_space=pltpu.VMEM)),
        input_output_aliases={0: ns, 1: ns + 1},
        compiler_params=pltpu.CompilerParams(has_side_effects=_EFFECT),
    )(pltpu.with_memory_space_constraint(src, pltpu.HBM),
      pltpu.with_memory_space_constraint(lax.empty(land.shape, land.dtype), pltpu.HBM),
      *(() if after is None else (after,)))
    return (res[:ns], res[ns], res[ns + 1]), res[ns + 2]


def _split_wait(state, after, kind, name):
    sems, src_thru, land_thru = state
    ns = 2 * _SPLIT_PEERS[kind]

    def body(src_ref, land_ref, *rest):
        sends, recvs = _split_copies(src_ref, land_ref, rest[:ns], kind)
        for cp in recvs:
            cp.wait_recv()
        for cp in sends:
            cp.wait_send()

    src_out, got = pl.pallas_call(
        body,
        name=name,
        out_shape=(pltpu.HBM(src_thru.shape, src_thru.dtype), pltpu.HBM(land_thru.shape, land_thru.dtype)),
        in_specs=(_HBM, _HBM) + (_SEM,) * ns + (_ANY,),
        out_specs=(_HBM, _HBM),
        input_output_aliases={0: 0, 1: 1},
        compiler_params=pltpu.CompilerParams(has_side_effects=_EFFECT),
    )(src_thru, land_thru, *sems, after)
    if kind == "core_swap":
        return got, src_out
    if kind == "core_gather":
        return lax.dynamic_update_index_in_dim(got, src_out, lax.axis_index("c"), axis=0)
    return _fill_own(got, src_out, kind == "chip_gather")


def _core_gather(src, name):
    def body(src_ref, out_ref, send_sem, recv_sem):
        x, y, c = lax.axis_index("x"), lax.axis_index("y"), lax.axis_index("c")
        cp = pltpu.make_async_remote_copy(src_ref=src_ref, dst_ref=out_ref.at[c], send_sem=send_sem,
                                          recv_sem=recv_sem, device_id=(x, y, 1 - c), device_id_type=MESH)
        cp.start()
        pltpu.make_async_remote_copy(src_ref=src_ref, dst_ref=out_ref.at[1 - c], send_sem=send_sem,
                                     recv_sem=recv_sem, device_id=(x, y, 1 - c), device_id_type=MESH).wait_recv()
        cp.wait_send()

    out = pl.pallas_call(
        body,
        in_specs=[_ANY],
        out_specs=_ANY,
        out_shape=SDS((2,) + tuple(src.shape), src.dtype),
        scratch_shapes=[pltpu.SemaphoreType.DMA, pltpu.SemaphoreType.DMA],
        name=name,
    )(src)
    return lax.dynamic_update_index_in_dim(out, src, lax.axis_index("c"), axis=0)


_PACK_A = (("w_in", (1088, 1024)),)
_PACK_B = (("w_ba", (512, 128)), ("w_bh", (512, 128)), ("w_out", (128, 1024)), ("w_up", (704, 1024)),
           ("w_down", (352, 1024)))
_PACK_SIZES = _PACK_A + _PACK_B
_TRANSPOSED = ("w_in", "w_up")


def _slab_rows(sizes):
    return sum(r * c for _, (r, c) in sizes) // D_MODEL


def _pack_rows(d, sizes):
    n = d[sizes[0][0]].shape[0]
    return jnp.concatenate([d[k].reshape(n, -1, D_MODEL) for k, _ in sizes], axis=1)


def _unpack_rows(slab, sizes):
    n = slab.shape[0]
    out, lo = {}, 0
    for key, (r, c) in sizes:
        rows = r * c // D_MODEL
        out[key] = slab[:, lo:lo + rows].reshape(n, r, c)
        lo += rows
    return out


def _by_core(gslab):
    return jnp.swapaxes(gslab.reshape((4, 2) + gslab.shape[1:]), 0, 1)


def _cols_to_full(t):
    return jnp.swapaxes(t, 0, 1).reshape(t.shape[1], -1)


def _full_to_cols(t):
    K = t.shape[0]
    return jnp.swapaxes(t.reshape(K, 8, -1), 0, 1)


_SMALL = (("pre_mix_norm", (1, 1024)), ("rel_bias", (32, 24)), ("hgrn_lb_raw", (2, 512)), ("hgrn_norm", (1, 128)),
          ("post_mix_norm", (1, 1024)), ("pre_ffn_norm", (1, 1024)), ("conv_b", (1, 5632)),
          ("post_ffn_norm", (1, 1024)))
_SMALL_ROWS = 96
_CONVW_ROWS = 136


_SMALL_USED = sum(r * c for _, (r, c) in _SMALL)


def _pack_small(d, extra=None):
    flat = jnp.concatenate([d[k].reshape(-1) for k, _ in _SMALL] + ([] if extra is None else [extra.reshape(-1)]))
    flat = jnp.pad(flat, (0, _SMALL_ROWS * LANE - flat.shape[0]))
    return flat.reshape(_SMALL_ROWS, LANE)


def _unpack_small(p):
    flat = p.reshape(-1)
    out, lo = {}, 0
    for k, shp in _SMALL:
        n = shp[0] * shp[1]
        out[k] = flat[lo:lo + n].reshape(shp)
        lo += n
    return out


def _local_step(x, tgt, P, plan):
    S = x.shape[0]
    P = dict(P)
    lb = _lb_fwd(P["hgrn_lb_raw"])
    hs = _prep(x, P["pre_mix_norm"], plan.start_token())
    h1 = hs[0]
    consts = [_bias_consts(d) for d in DILATIONS]
    biases, dep = [], h1
    for g in range(N_GROUPS):
        tab_t = P["rel_bias"][:, 8 * g:8 * g + 8].T
        dep = _bias_build(tab_t, consts[g][0], consts[g][1], f"bias_build{g}", dep)
        biases.append(dep.reshape(8, ATTN_BLOCK, 2 * ATTN_BLOCK))
    W = dict(plan.weights_a(dep))
    qkv = [_mm(hs[g], W["wt_qkv"][g], "nt", bf16, f"proj_qkv{g}") for g in range(N_GROUPS)]
    hg = _mm(h1, W["wt_hg"], "nt", f32, "proj_hg")
    gc = _mm(h1, W["wt_gate"], "nt", bf16, "proj_gate")
    obuf, lbuf, token = [], [], None
    for g, d in enumerate(DILATIONS):
        o_g, l_g = _attn_fwd(qkv[g], biases[g], (S // d) // ATTN_BLOCK, f"attn_fwd{g}", after=token)
        lbuf.append(l_g)
        obuf.append(o_g)
        if g == 0:
            token = plan.forward_b(o_g)
    y_attn, y_attn_b, w0, w1, w2 = _attn_merge(obuf[0], obuf[1], obuf[2], lbuf[0], lbuf[1], lbuf[2])
    y_hgrn, o_raw, ck, _ = _hgrn_fwd(hg, lb, P["hgrn_norm"])
    wb = plan.weights_b(y_hgrn)
    P["conv_w"] = wb.pop("conv_w")
    W.update(wb)
    a = _mm(y_attn_b, W["w_ba"], "nn", bf16, "branch_attn")
    b = _mm(y_hgrn, W["w_bh"], "nn", bf16, "branch_hgrn")
    merged = _gate_fwd(a, b, gc)
    mo, x1, h2 = _mid_fwd(x, merged, W["w_out"], P["post_mix_norm"], P["pre_ffn_norm"])
    ug = _mm(h2, W["wt_up_g"], "nt", bf16, "up_gate")
    uv = _mm(h2, W["wt_up_v"], "nt", bf16, "up_val")
    cw_g, cw_v = P["conv_w"][:, :D_FF], P["conv_w"][:, D_FF:]
    cb_g, cb_v = P["conv_b"][:, :D_FF], P["conv_b"][:, D_FF:]
    act = _conv_fwd(ug, uv, cw_g, cw_v, cb_g, cb_v)
    loss, dy, dfo, g_post_ffn = _final(x1, act, W["w_down"], tgt, P["post_ffn_norm"])
    dact = _mm(dfo, W["w_down"], "nt", bf16, "d_act")
    gW_down = _mm(act, dfo, "tn", f32, "gw_down")
    dug, duv, st_g, st_v = _conv_bwd(ug, uv, dact, cw_g, cw_v, cb_g, cb_v)
    dh2 = _mm([dug, duv], [W["wt_up_g"], W["wt_up_v"]], "nn", f32, "dh2")
    gW_up_g = _mm(dug, h2, "tn", f32, "gw_up_gate")
    gW_up_v = _mm(duv, h2, "tn", f32, "gw_up_val")
    dx1, dmo, g_pre_ffn, g_post_mix = _mid_bwd(dy, dh2, x1, mo, P["pre_ffn_norm"], P["post_mix_norm"])
    dmerged = _mm(dmo, W["w_out"], "nt", bf16, "d_merged")
    gW_out = _mm(merged, dmo, "tn", f32, "gw_out")
    da, db, dgc = _gate_bwd(dmerged, a, b, gc)
    dyattn = _mm(da, W["w_ba"], "nt", f32, "d_yattn")
    gW_ba = _mm(y_attn_b, da, "tn", f32, "gw_ba")
    dyhgrn = _mm(db, W["w_bh"], "nt", f32, "d_yhgrn")
    gW_bh = _mm(y_hgrn, db, "tn", f32, "gw_bh")
    big_b = dict(w_ba=gW_ba, w_bh=gW_bh, w_out=gW_out, w_up=[gW_up_g, gW_up_v], w_down=gW_down)
    dos = _attn_merge_bwd(dyattn, y_attn, w0, w1, w2, after=plan.grads_b_start(big_b))
    dq_h, df_h, dv_h, dog_h, glb8, gnw8, got_b = _hgrn_bwd(hg, o_raw, dyhgrn, ck, lb, P["hgrn_norm"],
                                                          plan.bwd_ride(dos[5]))
    dhg = [dq_h, df_h, dv_h, dog_h]
    g_lb_raw = _lb_bwd(P["hgrn_lb_raw"], glb8[0:1])
    gn = gnw8[0:1]
    g_hgrn_norm = (gn[:, 0:128] + gn[:, 128:256]) + (gn[:, 256:384] + gn[:, 384:512])
    dqkvs, gW_qkv, g_rel = [], [], []
    for g, d in enumerate(DILATIONS):
        dq, dk, dv, dbias = _attn_bwd(qkv[g], biases[g], dos[g], dos[3 + g], lbuf[g], (S // d) // ATTN_BLOCK,
                                      f"attn_bwd{g}")
        dqkvs.append([dq, dk, dv])
        gW_qkv.append(_mm(dqkvs[g], hs[g], "tn", f32, f"gw_qkv{g}"))
        g_rel.append(_bias_grad(dbias.reshape(8, -1), consts[g][0], f"bias_grad{g}"))
    gW_hg = _mm(dhg, h1, "tn", f32, "gw_hg")
    gW_gate = _mm(dgc, h1, "tn", f32, "gw_gate")
    gW_in = gW_qkv + [gW_hg, gW_gate]
    token = plan.grads_a_start(gW_in)
    dh_perm = [_mm(dqkvs[g], W["wt_qkv"][g], "nn", f32, f"dh1_qkv{g}", after=token) for g in (1, 2)]
    token = plan.grads_a_exchange(dh_perm[1])
    dh_main = _mm(dqkvs[0] + dhg + [dgc], [W["wt_qkv"][0], W["wt_hg"], W["wt_gate"]], "nn", f32, "dh1_main",
                  after=token)
    grad_x, g_pre_mix = _first_bwd(x, dx1, _dh_sum(dh_main, dh_perm[0], dh_perm[1]), P["pre_mix_norm"])

    g_conv_w = jnp.concatenate([st_g[0:3], st_v[0:3]], axis=1)
    g_conv_b = jnp.concatenate([st_g[3:4], st_v[3:4]], axis=1)
    small = dict(pre_mix_norm=g_pre_mix, rel_bias=jnp.concatenate(g_rel, axis=1), hgrn_lb_raw=g_lb_raw,
                 hgrn_norm=g_hgrn_norm, post_mix_norm=g_post_mix, pre_ffn_norm=g_pre_ffn, conv_b=g_conv_b,
                 post_ffn_norm=g_post_ffn, conv_w=g_conv_w)
    return loss, grad_x, gW_in, big_b, got_b, small


def _weights_a(both):
    wt = jnp.swapaxes(both, 0, 1).reshape(-1, D_MODEL)
    return dict(
        wt_qkv=[wt[g * QKV_G:(g + 1) * QKV_G] for g in range(N_GROUPS)],
        wt_hg=wt[3 * QKV_G:3 * QKV_G + 4 * HGRN_W],
        wt_gate=wt[3 * QKV_G + 4 * HGRN_W:],
    )


def _weights_b(slabs):
    sh = _unpack_rows(slabs, _PACK_B)
    wt_up = sh["w_up"].reshape(-1, D_MODEL)
    return dict(
        w_ba=_cols_to_full(sh["w_ba"]),
        w_bh=_cols_to_full(sh["w_bh"]),
        w_out=sh["w_out"].reshape(D_MODEL, D_MODEL),
        wt_up_g=wt_up[:D_FF],
        wt_up_v=wt_up[D_FF:],
        w_down=sh["w_down"].reshape(D_FF, D_MODEL),
    )


def _dest_rows(sections, height):
    out = []
    for j in range(8):
        lo, hi, off, pieces = j * height, (j + 1) * height, 0, []
        for s in sections:
            a, b = max(lo, off), min(hi, off + s.shape[0])
            if a < b:
                pieces.append(s[a - off:b - off])
            off += s.shape[0]
        out.append(pieces[0] if len(pieces) == 1 else jnp.concatenate(pieces, axis=0))
    return out


def _grad_blocks_a(sections):
    rows = _dest_rows(sections, 1088)
    return jnp.stack([jnp.stack([rows[2 * k + c].astype(bf16) for k in range(4)]) for c in range(2)])


def _grad_slab_b(g):
    shards = dict(w_ba=_full_to_cols(g["w_ba"]), w_bh=_full_to_cols(g["w_bh"]), w_out=g["w_out"].reshape(8, 128, D_MODEL),
                  w_up=jnp.stack(_dest_rows(g["w_up"], 704)), w_down=g["w_down"].reshape(8, 352, D_MODEL))
    return _pack_rows({k: v.astype(bf16) for k, v in shards.items()}, _PACK_B)


_CONVW_SLAB_ROWS = 16


class _Traffic:
    def __init__(self, slab_a, slab_b, conv_w):
        hi = conv_w.astype(bf16)
        r1 = conv_w - hi.astype(f32)
        mid = r1.astype(bf16)
        lo = (r1 - mid.astype(f32)).astype(bf16)
        bits = jnp.stack([hi, mid, lo]).reshape(-1)
        tail = jnp.pad(bits, (0, _CONVW_SLAB_ROWS * D_MODEL - bits.shape[0])).reshape(_CONVW_SLAB_ROWS, D_MODEL)
        self.slab_b = jnp.concatenate([slab_b, tail], axis=0)
        self.state_a, tok = _split_start(slab_a, "chip_gather", "ag_a_start")
        self.state_b, self.token = _split_start(self.slab_b, "chip_gather", "ag_b_start", after=tok)
        self.chip_sum = None
        self.state = None

    def start_token(self):
        return self.token

    def weights_a(self, after):
        by_chip = _split_wait(self.state_a, after, "chip_gather", "ag_a_wait")
        return _weights_a(_core_gather(by_chip, "ag_a_cores"))

    def forward_b(self, after):
        by_chip = _split_wait(self.state_b, after, "chip_gather", "ag_b_wait")
        self.state, token = _split_start(by_chip, "core_gather", "ag_b_cores_start")
        return token

    def weights_b(self, after):
        both = _split_wait(self.state, after, "core_gather", "ag_b_cores_wait")
        slabs = jnp.swapaxes(both, 0, 1).reshape((8,) + tuple(self.slab_b.shape))
        rows = _slab_rows(_PACK_B)
        out = _weights_b(slabs[:, :rows])
        pieces = slabs[:, rows:].reshape(8, -1)[:, :3 * 3 * 704].reshape(8, 3, 3, 704).astype(f32)
        out["conv_w"] = _cols_to_full((pieces[:, 0] + pieces[:, 1]) + pieces[:, 2])
        return out

    def grads_b_start(self, grads):
        self.state, token = _split_start(_by_core(_grad_slab_b(grads)), "core_swap", "rs_b_cores_start")
        return token

    def bwd_ride(self, after):
        from_sib, by_core = _split_wait(self.state, after, "core_swap", "rs_b_cores_wait")
        self.chip_sum = _pair_add(by_core, from_sib, "rs_b_pair_add")
        return (self.chip_sum, False)

    def grads_a_start(self, sections):
        self.state, token = _split_start(_grad_blocks_a(sections), "core_swap", "rs_a_cores_start")
        return token

    def grads_a_exchange(self, after):
        from_sib, by_core = _split_wait(self.state, after, "core_swap", "rs_a_cores_wait")
        self.state, token = _split_start(_pair_add(by_core, from_sib, "rs_a_pair_add"), "chip_xchg", "rs_a_start")
        return token

    def parts(self, got_b, after):
        parts = _unpack_rows(_fill_own(got_b, self.chip_sum, False), _PACK_B)
        parts["w_in"] = _split_wait(self.state, after, "chip_xchg", "rs_a_wait")
        return parts


def kernel(x, pre_mix_norm, w_in, rel_bias, hgrn_lb_raw, hgrn_norm, w_branch_attn, w_branch_hgrn, w_out, post_mix_norm, pre_ffn_norm, w_up, conv_w, conv_b, w_down, post_ffn_norm, loss_target, m_pre_mix_norm, m_w_in, m_rel_bias, m_hgrn_lb_raw, m_hgrn_norm, m_w_branch_attn, m_w_branch_hgrn, m_w_out, m_post_mix_norm, m_pre_ffn_norm, m_w_up, m_conv_w, m_conv_b, m_w_down, m_post_ffn_norm, v_pre_mix_norm, v_w_in, v_rel_bias, v_hgrn_lb_raw, v_hgrn_norm, v_w_branch_attn, v_w_branch_hgrn, v_w_out, v_post_mix_norm, v_pre_ffn_norm, v_w_up, v_conv_w, v_conv_b, v_w_down, v_post_ffn_norm):
    ci = lax.axis_index("c")
    dev = 4 * lax.axis_index("x") + 2 * lax.axis_index("y") + ci
    tr = lambda t: jnp.swapaxes(t[0], 0, 1)
    wts = dict(w_in=tr(w_in), w_ba=w_branch_attn[0], w_bh=w_branch_hgrn[0], w_out=w_out[0], w_up=tr(w_up),
               w_down=w_down[0])
    mom = dict(w_in=tr(m_w_in), w_ba=m_w_branch_attn[0], w_bh=m_w_branch_hgrn[0], w_out=m_w_out[0], w_up=tr(m_w_up),
               w_down=m_w_down[0])
    var = dict(w_in=tr(v_w_in), w_ba=v_w_branch_attn[0], w_bh=v_w_branch_hgrn[0], w_out=v_w_out[0], w_up=tr(v_w_up),
               w_down=v_w_down[0])
    small_w = dict(pre_mix_norm=pre_mix_norm, rel_bias=rel_bias, hgrn_lb_raw=hgrn_lb_raw, hgrn_norm=hgrn_norm,
                   post_mix_norm=post_mix_norm, pre_ffn_norm=pre_ffn_norm, conv_b=conv_b, post_ffn_norm=post_ffn_norm)
    small_m = dict(pre_mix_norm=m_pre_mix_norm, rel_bias=m_rel_bias, hgrn_lb_raw=m_hgrn_lb_raw, hgrn_norm=m_hgrn_norm,
                   post_mix_norm=m_post_mix_norm, pre_ffn_norm=m_pre_ffn_norm, conv_b=m_conv_b,
                   post_ffn_norm=m_post_ffn_norm)
    small_v = dict(pre_mix_norm=v_pre_mix_norm, rel_bias=v_rel_bias, hgrn_lb_raw=v_hgrn_lb_raw, hgrn_norm=v_hgrn_norm,
                   post_mix_norm=v_post_mix_norm, pre_ffn_norm=v_pre_ffn_norm, conv_b=v_conv_b,
                   post_ffn_norm=v_post_ffn_norm)

    plan = _Traffic(wts["w_in"].astype(bf16),
                    _pack_rows({k: wts[k].astype(bf16)[None] for k, _ in _PACK_B}, _PACK_B)[0], conv_w[0])

    loss8, grad_x, _, _, got_b, small = _local_step(x[0], loss_target[0], small_w, plan)
    parts = plan.parts(got_b, grad_x)
    outs_big = {}
    for k, _ in _PACK_SIZES:
        outs_big[k] = _adamw(wts[k], mom[k], var[k], parts[k], "adamw_" + k)

    spack = jnp.concatenate([_pack_small(small, loss8[0, 0:1]),
                             jnp.pad(small["conv_w"].reshape(-1, LANE), ((0, _CONVW_ROWS - 132), (0, 0)))], axis=0)
    allp = _core_gather(_chip_comm(spack, True, "ag_small_chips"), "ag_small_cores")
    ssum = _sum8(allp, "small_sum")
    gs = ssum[:_SMALL_ROWS]
    loss = ssum[_SMALL_USED // LANE, _SMALL_USED % LANE]
    res_small = _adamw(_pack_small(small_w), _pack_small(small_m), _pack_small(small_v), gs, "adamw_small")
    sm = [_unpack_small(t) for t in res_small]
    g_cw_full = ssum[_SMALL_ROWS:_SMALL_ROWS + 132].reshape(3, 2 * D_FF)
    g_cw = lax.dynamic_slice_in_dim(g_cw_full, dev * 704, 704, axis=1)
    res_cw = _adamw(conv_w[0], m_conv_w[0], v_conv_w[0], g_cw, "adamw_conv_w")

    def pick(i):
        def big_(k):
            t = outs_big[k][i]
            return (jnp.swapaxes(t, 0, 1) if k in _TRANSPOSED else t)[None]
        return [sm[i]["pre_mix_norm"], big_("w_in"), sm[i]["rel_bias"], sm[i]["hgrn_lb_raw"], sm[i]["hgrn_norm"],
                big_("w_ba"), big_("w_bh"), big_("w_out"), sm[i]["post_mix_norm"], sm[i]["pre_ffn_norm"],
                big_("w_up"), res_cw[i][None], sm[i]["conv_b"], big_("w_down"), sm[i]["post_ffn_norm"]]

    return (loss, grad_x[None], *pick(0), *pick(1), *pick(2), *pick(3))
```

```python
import functools
import math

import jax
import jax.numpy as jnp
from jax import lax
from jax.experimental import pallas as pl
from jax.experimental.pallas import tpu as pltpu

f32 = jnp.float32
bf16 = jnp.bfloat16
SDS = jax.ShapeDtypeStruct
HIGHEST = lax.Precision.HIGHEST
MESH = pl.DeviceIdType.MESH

NN = (((1,), (0,)), ((), ()))
NT = (((1,), (1,)), ((), ()))
TN = (((0,), (0,)), ((), ()))

D_MODEL = 1024
N_GROUPS = 3
DILATIONS = (1, 4, 16)
HEAD_DIM = 64
ATTN_BLOCK = 128
QKV_G = 1536
ATTN_OUT = 512
HGRN_W = 512
HGRN_CHUNK = 32
D_FF = 2816
NUM_BUCKETS = 32
MAX_EXACT = 16
MAX_DISTANCE = 2048
NEG_INF = -1e30
EPS = 1e-6
LANE = 128
SUBLANE = 8
VMEM_BIG = 48 * 1024 * 1024
MM_ROWS = 512
MM_OUT_BYTES = 8 * 1024 * 1024

ADAM_LR, ADAM_B1, ADAM_B2, ADAM_EPS, ADAM_WD, ADAM_STEP = 0.001, 0.9, 0.999, 1e-08, 0.01, 10


def _pick(n, pref):
    t = pref
    while t >= LANE:
        if n % t == 0:
            return t
        t //= 2
    return n


def _cparams(sem=None, vmem=None):
    kw = {}
    if sem is not None:
        kw["dimension_semantics"] = sem
    if vmem is not None:
        kw["vmem_limit_bytes"] = vmem
    return pltpu.CompilerParams(**kw)


def _sigmoid(x):
    return jax.nn.sigmoid(x)


def _colsum8(x):
    return x.reshape(x.shape[0] // SUBLANE, SUBLANE, x.shape[1]).sum(axis=0)


def _mm(a, b, mode, out_dtype, name, acc=None, after=None):
    dims = {"nn": NN, "nt": NT, "tn": TN}[mode]
    has_acc = acc is not None
    parts = list(a) if isinstance(a, (list, tuple)) else [a]
    if mode == "tn":
        assert not has_acc
        K, N = b.shape
        widths = [t.shape[1] for t in parts]
        M = sum(widths)
        whole = M * N * 4 <= MM_OUT_BYTES
        assert whole or len(parts) == 1
        tmm = M if whole else M // 2
        ts = _pick(K, 2 * MM_ROWS)
        nk = K // ts

        def body_tn(*refs):
            b_ref, o_ref = refs[-2], refs[-1]
            k = pl.program_id(1)
            bv = b_ref[...]
            lo = 0
            for a_ref, w in zip(refs[:-2], widths if whole else [tmm]):
                part = lax.dot_general(a_ref[...], bv, dims, preferred_element_type=f32)
                rows = slice(lo, lo + w)
                lo += w

                @pl.when(k == 0)
                def _(part=part, rows=rows):
                    o_ref[rows, :] = part

                @pl.when(k > 0)
                def _(part=part, rows=rows):
                    o_ref[rows, :] += part

        return pl.pallas_call(
            body_tn,
            grid=(M // tmm, nk),
            in_specs=[pl.BlockSpec((ts, w if whole else tmm), lambda i, k: (k, i)) for w in widths]
            + [pl.BlockSpec((ts, N), lambda i, k: (k, 0))],
            out_specs=pl.BlockSpec((tmm, N), lambda i, k: (i, 0)),
            out_shape=SDS((M, N), out_dtype),
            compiler_params=_cparams(("parallel", "arbitrary"), VMEM_BIG),
            name=name,
        )(*parts, b)

    bs = list(b) if isinstance(b, (list, tuple)) else [b]
    widths = [t.shape[1] for t in parts]
    M = parts[0].shape[0]
    kdim = 0 if mode == "nn" else 1
    N = bs[0].shape[1 - kdim]
    tm = _pick(M, MM_ROWS)
    npart, nb = len(parts), len(bs)
    place, bi, lo = [], 0, 0
    for w in widths:
        place.append((bi, lo))
        lo += w
        if lo == bs[bi].shape[kdim]:
            bi, lo = bi + 1, 0
    assert bi == nb and lo == 0

    def body(*refs):
        a_refs, b_refs = refs[:npart], refs[npart:npart + nb]
        c_ref = refs[npart + nb] if has_acc else None
        o_ref = refs[-1]
        part = None
        for a_ref, w, (bi, lo) in zip(a_refs, widths, place):
            b_ref = b_refs[bi]
            if w == bs[bi].shape[kdim]:
                bk = b_ref[...]
            else:
                bk = b_ref[:, lo:lo + w] if mode == "nt" else b_ref[lo:lo + w, :]
            t = lax.dot_general(a_ref[...], bk, dims, preferred_element_type=f32)
            part = t if part is None else part + t
        if has_acc:
            part = part + c_ref[...]
        o_ref[...] = part.astype(out_dtype)

    specs = [pl.BlockSpec((tm, w), lambda i: (i, 0)) for w in widths] \
        + [pl.BlockSpec(t.shape, lambda i: (0, 0)) for t in bs]
    args = parts + bs
    aliases = {}
    if has_acc:
        specs.append(pl.BlockSpec((tm, N), lambda i: (i, 0)))
        args.append(acc)
        aliases = {npart + nb: 0}
    if after is not None:
        specs.append(pl.BlockSpec(memory_space=pl.ANY))
        args.append(after)
    return pl.pallas_call(
        body,
        grid=(M // tm,),
        in_specs=specs,
        out_specs=pl.BlockSpec((tm, N), lambda i: (i, 0)),
        out_shape=SDS((M, N), out_dtype),
        input_output_aliases=aliases,
        compiler_params=_cparams(("parallel",), VMEM_BIG),
        name=name,
    )(*args)


PERM_ROWS = 1024


def _perm_spec(d, cols=LANE):
    return pl.BlockSpec((d, PERM_ROWS // d, cols), lambda i, j: (0, i, j))


def _to_natural(src_ref, dst_ref, d):
    n = src_ref.shape[1]
    for r in range(d):
        dst_ref[pl.ds(r, n, stride=d), :] = src_ref[r]


def _prep(x, w, after=None):
    S, D = x.shape
    R = PERM_ROWS
    nc = D // LANE
    n_in = nc + 1 + (after is not None)

    def body(*refs):
        x_refs, w_ref = refs[:nc], refs[nc]
        h_ref, h4_ref, h16_ref, rs = refs[n_in:]
        ssq = None
        for xr in x_refs:
            v = xr[...]
            t = jnp.sum(v * v, axis=-1, keepdims=True)
            ssq = t if ssq is None else ssq + t
        rinv = lax.rsqrt(ssq * (1.0 / D) + EPS)
        rs[...] = jnp.broadcast_to(rinv, (R, LANE))
        for j, xr in enumerate(x_refs):
            cols = slice(j * LANE, (j + 1) * LANE)
            wj = w_ref[:, cols]
            h_ref[:, cols] = ((xr[...] * rinv) * wj).astype(bf16)
            for d, o_ref in ((4, h4_ref), (16, h16_ref)):
                n = R // d
                for r in range(d):
                    rows = pl.ds(r, n, stride=d)
                    o_ref[r, :, cols] = ((xr[rows, :] * rs[rows, :]) * wj).astype(bf16)

    col = lambda j: pl.BlockSpec((R, LANE), lambda i, j=j: (i, j))
    h, h4, h16 = pl.pallas_call(
        body,
        grid=(S // R,),
        in_specs=[col(j) for j in range(nc)] + [pl.BlockSpec((1, D), lambda i: (0, 0))]
        + ([] if after is None else [pl.BlockSpec(memory_space=pl.ANY)]),
        out_specs=[pl.BlockSpec((R, D), lambda i: (i, 0)), pl.BlockSpec((4, R // 4, D), lambda i: (0, i, 0)),
                   pl.BlockSpec((16, R // 16, D), lambda i: (0, i, 0))],
        out_shape=[SDS((S, D), bf16), SDS((4, S // 4, D), bf16), SDS((16, S // 16, D), bf16)],
        scratch_shapes=[pltpu.VMEM((R, LANE), f32)],
        compiler_params=_cparams(("parallel",), VMEM_BIG),
        name="prep_norm_perm",
    )(*([x] * nc), w, *([] if after is None else [after]))
    return [h, h4.reshape(S, D), h16.reshape(S, D)]


def _dh_sum(a, b, c):
    S, D = a.shape
    R = PERM_ROWS

    def body(a_ref, b_ref, c_ref, o_ref, sb, sc):
        _to_natural(b_ref, sb, 4)
        _to_natural(c_ref, sc, 16)
        o_ref[...] = (a_ref[...] + sb[...]) + sc[...]

    nat = pl.BlockSpec((R, LANE), lambda i, j: (i, j))
    return pl.pallas_call(
        body,
        grid=(S // R, D // LANE),
        in_specs=[nat, _perm_spec(4), _perm_spec(16)],
        out_specs=nat,
        out_shape=SDS((S, D), f32),
        scratch_shapes=[pltpu.VMEM((R, LANE), f32)] * 2,
        compiler_params=_cparams(("parallel", "parallel")),
        name="dh_sum",
    )(a, b.reshape(4, S // 4, D), c.reshape(16, S // 16, D))


def _rms_parts(xv):
    r = lax.rsqrt(jnp.mean(xv * xv, axis=-1, keepdims=True) + EPS)
    return r, xv * r


def _rms_bwd(xhat, r, w, dy):
    dyw = dy * w
    return r * (dyw - xhat * jnp.mean(dyw * xhat, axis=-1, keepdims=True))


def _mid_fwd(x, merged, w_out, w_pm, w_pf):
    S, D = x.shape
    tm = _pick(S, MM_ROWS)

    def body(x_ref, m_ref, wo_ref, wpm_ref, wpf_ref, mo_ref, x1_ref, h2_ref):
        mo = jnp.dot(m_ref[...], wo_ref[...], preferred_element_type=f32)
        mo_ref[...] = mo
        _, moh = _rms_parts(mo)
        x1 = x_ref[...] + moh * wpm_ref[...]
        x1_ref[...] = x1
        _, x1h = _rms_parts(x1)
        h2_ref[...] = (x1h * wpf_ref[...]).astype(bf16)

    row = pl.BlockSpec((tm, D), lambda i: (i, 0))
    vec = pl.BlockSpec((1, D), lambda i: (0, 0))
    return pl.pallas_call(
        body,
        grid=(S // tm,),
        in_specs=[row, pl.BlockSpec((tm, merged.shape[1]), lambda i: (i, 0)),
                  pl.BlockSpec(w_out.shape, lambda i: (0, 0)), vec, vec],
        out_specs=[row, row, row],
        out_shape=[SDS((S, D), f32), SDS((S, D), f32), SDS((S, D), bf16)],
        compiler_params=_cparams(("parallel",), VMEM_BIG),
        name="out_proj_mid_fwd",
    )(x, merged, w_out, w_pm, w_pf)


def _final(x1, act, w_down, tgt, w_pfn):
    S, D = x1.shape
    tm = _pick(S, MM_ROWS)
    nt = S // tm

    def body(x1_ref, a_ref, wd_ref, t_ref, w_ref, loss_ref, dy_ref, dfo_ref, gw_ref, lacc, gacc):
        i = pl.program_id(0)

        @pl.when(i == 0)
        def _():
            lacc[...] = jnp.zeros_like(lacc)
            gacc[...] = jnp.zeros_like(gacc)

        w = w_ref[...]
        r, foh = _rms_parts(jnp.dot(a_ref[...], wd_ref[...], preferred_element_type=f32))
        y = x1_ref[...] + foh * w
        err = y - t_ref[...]
        lacc[...] += _colsum8(err * err)
        dy = err * (1.0 / D)
        dy_ref[...] = dy
        gacc[...] += _colsum8(dy * foh)
        dfo_ref[...] = _rms_bwd(foh, r, w, dy).astype(bf16)

        @pl.when(i == nt - 1)
        def _():
            loss_ref[...] = jnp.full((SUBLANE, LANE), 0.5 / D, f32) * jnp.sum(lacc[...])
            gw_ref[...] = jnp.sum(gacc[...], axis=0, keepdims=True)

    row = pl.BlockSpec((tm, D), lambda i: (i, 0))
    vec = pl.BlockSpec((1, D), lambda i: (0, 0))
    return pl.pallas_call(
        body,
        grid=(nt,),
        in_specs=[row, pl.BlockSpec((tm, act.shape[1]), lambda i: (i, 0)),
                  pl.BlockSpec(w_down.shape, lambda i: (0, 0)), row, vec],
        out_specs=[pl.BlockSpec((SUBLANE, LANE), lambda i: (0, 0)), row, row, vec],
        out_shape=[SDS((SUBLANE, LANE), f32), SDS((S, D), f32), SDS((S, D), bf16), SDS((1, D), f32)],
        scratch_shapes=[pltpu.VMEM((SUBLANE, D), f32), pltpu.VMEM((SUBLANE, D), f32)],
        compiler_params=_cparams(("arbitrary",), VMEM_BIG),
        name="down_proj_final_loss",
    )(x1, act, w_down, tgt, w_pfn)


def _mid_bwd(dy, dh2, x1, mo, w_pf, w_pm):
    S, D = dy.shape
    tm = _pick(S, 512)
    nt = S // tm

    def body(dy_ref, dh2_ref, x1_ref, mo_ref, wpf_ref, wpm_ref, dx1_ref, dmo_ref, gpf_ref, gpm_ref, apf, apm):
        i = pl.program_id(0)

        @pl.when(i == 0)
        def _():
            apf[...] = jnp.zeros_like(apf)
            apm[...] = jnp.zeros_like(apm)

        r1, x1h = _rms_parts(x1_ref[...])
        dh2 = dh2_ref[...]
        apf[...] += _colsum8(dh2 * x1h)
        dx1 = dy_ref[...] + _rms_bwd(x1h, r1, wpf_ref[...], dh2)
        dx1_ref[...] = dx1
        rm, moh = _rms_parts(mo_ref[...])
        apm[...] += _colsum8(dx1 * moh)
        dmo_ref[...] = _rms_bwd(moh, rm, wpm_ref[...], dx1).astype(bf16)

        @pl.when(i == nt - 1)
        def _():
            gpf_ref[...] = jnp.sum(apf[...], axis=0, keepdims=True)
            gpm_ref[...] = jnp.sum(apm[...], axis=0, keepdims=True)

    row = pl.BlockSpec((tm, D), lambda i: (i, 0))
    vec = pl.BlockSpec((1, D), lambda i: (0, 0))
    return pl.pallas_call(
        body,
        grid=(nt,),
        in_specs=[row, row, row, row, vec, vec],
        out_specs=[row, row, vec, vec],
        out_shape=[SDS((S, D), f32), SDS((S, D), bf16), SDS((1, D), f32), SDS((1, D), f32)],
        scratch_shapes=[pltpu.VMEM((SUBLANE, D), f32), pltpu.VMEM((SUBLANE, D), f32)],
        compiler_params=_cparams(("arbitrary",)),
        name="mid_bwd",
    )(dy, dh2, x1, mo, w_pf, w_pm)


def _first_bwd(x, dx1, dh, w_pre):
    S, D = x.shape
    tm = _pick(S, 512)
    nt = S // tm

    def body(x_ref, dx1_ref, a_ref, w_ref, gx_ref, gw_ref, acc):
        i = pl.program_id(0)

        @pl.when(i == 0)
        def _():
            acc[...] = jnp.zeros_like(acc)

        r, xh = _rms_parts(x_ref[...])
        dh = a_ref[...]
        acc[...] += _colsum8(dh * xh)
        gx_ref[...] = dx1_ref[...] + _rms_bwd(xh, r, w_ref[...], dh)

        @pl.when(i == nt - 1)
        def _():
            gw_ref[...] = jnp.sum(acc[...], axis=0, keepdims=True)

    row = pl.BlockSpec((tm, D), lambda i: (i, 0))
    vec = pl.BlockSpec((1, D), lambda i: (0, 0))
    return pl.pallas_call(
        body,
        grid=(nt,),
        in_specs=[row, row, row, vec],
        out_specs=[row, vec],
        out_shape=[SDS((S, D), f32), SDS((1, D), f32)],
        scratch_shapes=[pltpu.VMEM((SUBLANE, D), f32)],
        compiler_params=_cparams(("arbitrary",)),
        name="first_bwd",
    )(x, dx1, dh, w_pre)


def _t5_bucket(dist):
    n = jnp.maximum(dist, 0)
    nf = jnp.maximum(n, 1).astype(f32)
    large = MAX_EXACT + (jnp.log(nf / MAX_EXACT) / math.log(MAX_DISTANCE / MAX_EXACT)
                         * (NUM_BUCKETS - MAX_EXACT)).astype(jnp.int32)
    large = jnp.minimum(large, NUM_BUCKETS - 1)
    return jnp.where(n < MAX_EXACT, n, large)


def _bias_consts(d):
    blk = ATTN_BLOCK
    rel = jnp.arange(blk)[:, None] + blk - jnp.arange(2 * blk)[None, :]
    in_win = (rel >= 0) & (rel <= blk)
    bucket = _t5_bucket(rel * d).reshape(1, -1)
    onehot = (bucket == jnp.arange(NUM_BUCKETS)[:, None]).astype(f32)
    return onehot, in_win.astype(f32).reshape(1, -1)


def _bias_build(tab_t, onehot, maskf, name, after):
    H = tab_t.shape[0]

    def body(t_ref, oh_ref, m_ref, after_ref, o_ref):
        b = jnp.dot(t_ref[...], oh_ref[...], precision=HIGHEST, preferred_element_type=f32)
        o_ref[...] = jnp.where(m_ref[...] > 0.5, b, NEG_INF)

    vm = pl.BlockSpec(memory_space=pltpu.VMEM)
    return pl.pallas_call(body, out_shape=SDS((H, onehot.shape[1]), f32), name=name,
                          in_specs=[vm, vm, vm, pl.BlockSpec(memory_space=pl.ANY)], out_specs=vm,
                          )(tab_t, onehot, maskf, after)


def _bias_grad(dbias_flat, onehot, name):
    H = dbias_flat.shape[0]

    def body(g_ref, oh_ref, o_ref):
        o_ref[...] = lax.dot_general(oh_ref[...], g_ref[...], NT, precision=HIGHEST, preferred_element_type=f32)

    return pl.pallas_call(body, out_shape=SDS((NUM_BUCKETS, H), f32), name=name)(dbias_flat, onehot)


ATTN_TILE = 512
ATTN_SUB = ATTN_TILE // ATTN_BLOCK
ATTN_HP = 4
ATTN_WIDE = ATTN_HP * LANE


def _qkv_specs(nt):
    tile = (ATTN_TILE, ATTN_WIDE)
    blk = (ATTN_BLOCK, ATTN_WIDE)
    sec = ATTN_OUT // ATTN_WIDE
    cur = lambda off: (lambda h, t: (jnp.minimum(t, nt - 1), off + h))
    prev = lambda off: (lambda h, t: (jnp.maximum(jnp.minimum(t, nt - 1) * ATTN_SUB - 1, 0), off + h))
    return [pl.BlockSpec(tile, cur(0)), pl.BlockSpec(blk, prev(sec)), pl.BlockSpec(tile, cur(sec)),
            pl.BlockSpec(blk, prev(2 * sec)), pl.BlockSpec(tile, cur(2 * sec))]


def _head_masks():
    lane = lax.broadcasted_iota(jnp.int32, (ATTN_BLOCK, LANE), 1)
    return lane < HEAD_DIM


def _stack_heads(x2, low):
    zero = jnp.zeros_like(x2)
    return jnp.concatenate([jnp.where(low, x2, zero), jnp.where(low, zero, x2)], axis=0)


def _attn_fwd(qkv, bias, bps, name, after=None):
    S = qkv.shape[0]
    nt = S // ATTN_TILE
    scale = HEAD_DIM ** -0.5

    def body(q_ref, kp_ref, kc_ref, vp_ref, vc_ref, b_ref, *rest):
        o_ref, l_ref = rest[-2:]
        t = pl.program_id(1)
        low = _head_masks()
        col = lax.broadcasted_iota(jnp.int32, (2 * ATTN_BLOCK, 2 * ATTN_BLOCK), 1)
        for hp in range(ATTN_HP):
            cols = slice(hp * LANE, (hp + 1) * LANE)
            kk = jnp.concatenate([kp_ref[:, cols], kc_ref[:, cols]], axis=0)
            vv = jnp.concatenate([vp_ref[:, cols], vc_ref[:, cols]], axis=0)
            bias2 = b_ref[2 * hp:2 * hp + 2].reshape(2 * ATTN_BLOCK, 2 * ATTN_BLOCK)
            for b in range(ATTN_SUB):
                lo = b * ATTN_BLOCK
                rows = slice(lo, lo + ATTN_BLOCK)
                keys = slice(lo, lo + 2 * ATTN_BLOCK)
                dead = jnp.logical_and((t * ATTN_SUB + b) % bps == 0, col < ATTN_BLOCK)
                q2 = _stack_heads(q_ref[rows, cols], low)
                kb, vb = kk[keys], vv[keys]
                s = lax.dot_general(q2, kb, NT, preferred_element_type=f32) * scale + bias2
                s = jnp.where(dead, NEG_INF, s)
                m = jnp.max(s, axis=-1, keepdims=True)
                p = jnp.exp(s - m)
                l = jnp.sum(p, axis=-1, keepdims=True)
                o2 = jnp.dot(p.astype(bf16), vb, preferred_element_type=f32) / l
                lse = m + jnp.log(l)
                o_ref[rows, cols] = jnp.where(low, o2[:ATTN_BLOCK], o2[ATTN_BLOCK:])
                l_ref[rows, cols] = jnp.where(low, lse[:ATTN_BLOCK], lse[ATTN_BLOCK:])

    tile = pl.BlockSpec((ATTN_TILE, ATTN_WIDE), lambda h, t: (t, h))
    return pl.pallas_call(
        body,
        grid=(4 // ATTN_HP, nt),
        in_specs=_qkv_specs(nt) + [pl.BlockSpec((2 * ATTN_HP, ATTN_BLOCK, 2 * ATTN_BLOCK), lambda h, t: (h, 0, 0))]
        + ([] if after is None else [pl.BlockSpec(memory_space=pl.ANY)]),
        out_specs=[tile, tile],
        out_shape=[SDS((S, ATTN_OUT), f32), SDS((S, ATTN_OUT), f32)],
        compiler_params=_cparams(("parallel", "parallel")),
        name=name,
    )(qkv, qkv, qkv, qkv, qkv, bias, *([] if after is None else [after]))


def _attn_bwd(qkv, bias, do, dvec, lse, bps, name):
    S = qkv.shape[0]
    nt = S // ATTN_TILE
    scale = HEAD_DIM ** -0.5

    def assemble(parts):
        rows = [parts[0][:ATTN_BLOCK]]
        for b in range(ATTN_SUB - 1):
            rows.append(parts[b][ATTN_BLOCK:] + parts[b + 1][:ATTN_BLOCK])
        rows.append(parts[-1][ATTN_BLOCK:])
        return rows

    def body(q_ref, kp_ref, kc_ref, vp_ref, vc_ref, b_ref, do_ref, dvec_ref, lse_ref,
             dq_ref, dk_ref, dv_ref, db_ref, ck, cv):
        t = pl.program_id(1)
        last = ATTN_TILE - ATTN_BLOCK

        @pl.when(t == 0)
        def _():
            ck[...] = jnp.zeros_like(ck)
            cv[...] = jnp.zeros_like(cv)
            db_ref[...] = jnp.zeros_like(db_ref)

        @pl.when(t < nt)
        def _():
            low = _head_masks()
            col = lax.broadcasted_iota(jnp.int32, (2 * ATTN_BLOCK, 2 * ATTN_BLOCK), 1)
            per_row = lambda t2: jnp.concatenate([t2[:, 0:1], t2[:, HEAD_DIM:HEAD_DIM + 1]], axis=0)
            for hp in range(ATTN_HP):
                cols = slice(hp * LANE, (hp + 1) * LANE)
                kk = jnp.concatenate([kp_ref[:, cols], kc_ref[:, cols]], axis=0)
                vv = jnp.concatenate([vp_ref[:, cols], vc_ref[:, cols]], axis=0)
                bias2 = b_ref[2 * hp:2 * hp + 2].reshape(2 * ATTN_BLOCK, 2 * ATTN_BLOCK)
                dk_parts, dv_parts = [], []
                dsum = None
                for b in range(ATTN_SUB):
                    lo = b * ATTN_BLOCK
                    rows = slice(lo, lo + ATTN_BLOCK)
                    keys = slice(lo, lo + 2 * ATTN_BLOCK)
                    dead = jnp.logical_and((t * ATTN_SUB + b) % bps == 0, col < ATTN_BLOCK)
                    q2 = _stack_heads(q_ref[rows, cols], low)
                    do2 = _stack_heads(do_ref[rows, cols].astype(bf16), low)
                    kb, vb = kk[keys], vv[keys]
                    s = lax.dot_general(q2, kb, NT, preferred_element_type=f32) * scale + bias2
                    s = jnp.where(dead, NEG_INF, s)
                    p = jnp.exp(s - per_row(lse_ref[rows, cols]))
                    dp = lax.dot_general(do2, vb, NT, preferred_element_type=f32)
                    ds = p * (dp - per_row(dvec_ref[rows, cols]))
                    dsum = ds if dsum is None else dsum + ds
                    dsb = ds.astype(bf16)
                    dq2 = jnp.dot(dsb, kb, preferred_element_type=f32) * scale
                    dq_ref[rows, cols] = jnp.where(low, dq2[:ATTN_BLOCK], dq2[ATTN_BLOCK:]).astype(bf16)
                    dk_parts.append(lax.dot_general(dsb, q2, TN, preferred_element_type=f32) * scale)
                    dv_parts.append(lax.dot_general(p.astype(bf16), do2, TN, preferred_element_type=f32))
                db_ref[2 * hp:2 * hp + 2] += dsum.reshape(2, ATTN_BLOCK, 2 * ATTN_BLOCK)
                for parts, carry, out_ref in ((dk_parts, ck, dk_ref), (dv_parts, cv, dv_ref)):
                    rws = assemble(parts)
                    out_ref[:last, cols] = carry[:last, cols].astype(bf16)
                    out_ref[last:, cols] = (carry[last:, cols] + rws[0]).astype(bf16)
                    for b in range(ATTN_SUB):
                        carry[b * ATTN_BLOCK:(b + 1) * ATTN_BLOCK, cols] = rws[b + 1]

        @pl.when(t == nt)
        def _():
            dk_ref[...] = ck[...].astype(bf16)
            dv_ref[...] = cv[...].astype(bf16)

    tile = (ATTN_TILE, ATTN_WIDE)
    cur = pl.BlockSpec(tile, lambda h, t: (jnp.minimum(t, nt - 1), h))
    lag = pl.BlockSpec(tile, lambda h, t: (jnp.maximum(t - 1, 0), h))
    bspec = pl.BlockSpec((2 * ATTN_HP, ATTN_BLOCK, 2 * ATTN_BLOCK), lambda h, t: (h, 0, 0))
    return pl.pallas_call(
        body,
        grid=(4 // ATTN_HP, nt + 1),
        in_specs=_qkv_specs(nt) + [bspec, cur, cur, cur],
        out_specs=[cur, lag, lag, bspec],
        out_shape=[SDS((S, ATTN_OUT), bf16), SDS((S, ATTN_OUT), bf16), SDS((S, ATTN_OUT), bf16),
                   SDS((8, ATTN_BLOCK, 2 * ATTN_BLOCK), f32)],
        scratch_shapes=[pltpu.VMEM(tile, f32), pltpu.VMEM(tile, f32)],
        compiler_params=_cparams(("parallel", "arbitrary")),
        name=name,
    )(qkv, qkv, qkv, qkv, qkv, bias, do, dvec, lse)


def _attn_merge(o0, o1, o2, l0, l1, l2):
    S, W = o0.shape
    R = PERM_ROWS

    def body(o0_ref, o1_ref, o2_ref, l0_ref, l1_ref, l2_ref, y_ref, yb_ref, w0_ref, w1_ref, w2_ref,
             so1, so2, sl1, sl2):
        _to_natural(o1_ref, so1, 4)
        _to_natural(l1_ref, sl1, 4)
        _to_natural(o2_ref, so2, 16)
        _to_natural(l2_ref, sl2, 16)
        a, b, c = l0_ref[...], sl1[...], sl2[...]
        m = jnp.maximum(jnp.maximum(a, b), c)
        ea, eb, ec = jnp.exp(a - m), jnp.exp(b - m), jnp.exp(c - m)
        den = (ea + eb) + ec
        w0, w1, w2 = ea / den, eb / den, ec / den
        y = (w0 * o0_ref[...] + w1 * so1[...]) + w2 * so2[...]
        y_ref[...] = y
        yb_ref[...] = y.astype(bf16)
        w0_ref[...] = w0
        w1_ref[...] = w1
        w2_ref[...] = w2

    nat = pl.BlockSpec((R, LANE), lambda i, j: (i, j))
    v4 = lambda t: t.reshape(4, S // 4, W)
    v16 = lambda t: t.reshape(16, S // 16, W)
    return pl.pallas_call(
        body,
        grid=(S // R, W // LANE),
        in_specs=[nat, _perm_spec(4), _perm_spec(16)] * 2,
        out_specs=[nat] * 5,
        out_shape=[SDS((S, W), f32), SDS((S, W), bf16)] + [SDS((S, W), f32)] * 3,
        scratch_shapes=[pltpu.VMEM((R, LANE), f32)] * 4,
        compiler_params=_cparams(("parallel", "parallel")),
        name="attn_merge",
    )(o0, v4(o1), v16(o2), l0, v4(l1), v16(l2))


def _attn_merge_bwd(dy, y, w0, w1, w2, after=None):
    S, W = dy.shape
    R = PERM_ROWS

    def body(dy_ref, y_ref, w0_ref, w1_ref, w2_ref, *rest):
        a0, a1, a2, b0, b1, b2, sa, sb = rest[-8:]
        dyv = dy_ref[...]
        r = lax.broadcasted_iota(jnp.int32, (LANE, LANE), 0) // HEAD_DIM
        c = lax.broadcasted_iota(jnp.int32, (LANE, LANE), 1) // HEAD_DIM
        seg = jnp.where(r == c, 1.0, 0.0).astype(f32)
        cbar = jnp.dot(dyv * y_ref[...], seg, precision=HIGHEST, preferred_element_type=f32)
        w = w0_ref[...]
        a0[...] = (w * dyv).astype(bf16)
        b0[...] = w * cbar
        for d, w_ref, a_ref, b_ref in ((4, w1_ref, a1, b1), (16, w2_ref, a2, b2)):
            w = w_ref[...]
            sa[...] = w * dyv
            sb[...] = w * cbar
            n = R // d
            for k in range(d):
                rows = pl.ds(k, n, stride=d)
                a_ref[k] = sa[rows, :].astype(bf16)
                b_ref[k] = sb[rows, :]

    nat = pl.BlockSpec((R, LANE), lambda i, j: (i, j))
    shapes = lambda dt: [SDS((S, W), dt), SDS((4, S // 4, W), dt), SDS((16, S // 16, W), dt)]
    outs = pl.pallas_call(
        body,
        grid=(S // R, W // LANE),
        in_specs=[nat] * 5 + ([] if after is None else [pl.BlockSpec(memory_space=pl.ANY)]),
        out_specs=[nat, _perm_spec(4), _perm_spec(16)] * 2,
        out_shape=shapes(bf16) + shapes(f32),
        scratch_shapes=[pltpu.VMEM((R, LANE), f32)] * 2,
        compiler_params=_cparams(("parallel", "parallel")),
        name="attn_merge_bwd",
    )(dy, y, w0, w1, w2, *([] if after is None else [after]))
    return [t.reshape(S, W) for t in outs]


HGRN_SB = 256
HGRN_PAIR = 4


def _chunk_masks():
    r = jnp.arange(HGRN_SB)[:, None]
    c = jnp.arange(HGRN_SB)[None, :]
    same = (r // HGRN_CHUNK) == (c // HGRN_CHUNK)
    return jnp.stack([same & (c <= r), same, same & (c >= r)]).astype(bf16)


def _mask_dot(mask, x):
    hi = x.astype(bf16)
    r1 = x - hi.astype(f32)
    mid = r1.astype(bf16)
    lo = (r1 - mid.astype(f32)).astype(bf16)
    p = jnp.dot(mask, jnp.concatenate([hi, mid, lo], axis=1), preferred_element_type=f32)
    n = x.shape[1]
    return (p[:, :n] + p[:, n:2 * n]) + p[:, 2 * n:]


def _hgrn_prep(q_raw, f_raw, lbv, tril, same):
    sq = _sigmoid(q_raw)
    qs = q_raw * sq
    sig = _sigmoid(f_raw)
    f = lbv + (1.0 - lbv) * sig
    g = jnp.log(f)
    k = 1.0 - f
    G = _mask_dot(tril, g)
    GL = _mask_dot(same, g)
    eG = jnp.exp(G)
    einv = jnp.exp(-G)
    edec = jnp.exp(GL - G)
    return dict(sq=sq, qs=qs, sig=sig, f=f, k=k, eG=eG, einv=einv, edec=edec, eGL=jnp.exp(GL),
                qt=qs * eG, kt=k * einv, kd=k * edec)


def _ride_split(ride, rest, n_out, n_scratch):
    if ride is None:
        return None, rest[:n_out], None, rest[n_out:], None
    return rest[0], rest[1:1 + n_out], rest[1 + n_out], rest[2 + n_out:2 + n_out + n_scratch], rest[2 + n_out + n_scratch:]


def _hgrn_fwd(hg, lb, normw, ride=None):
    S = hg.shape[0]
    sb = HGRN_SB
    nsb = S // sb
    nch = sb // HGRN_CHUNK

    def body(q_ref, f_ref, v_ref, og_ref, lb_ref, nw_ref, m_ref, *rest):
        src_ref, (y_ref, o_ref, ck_ref), got_ref, (st,), sems = _ride_split(ride, rest, 3, 1)
        j = pl.program_id(1)
        if ride is not None:
            @pl.when(jnp.logical_and(pl.program_id(0) == 0, j == 0))
            def _():
                _chip_start(src_ref, got_ref, sems[0], sems[1], ride[1])

        @pl.when(j == 0)
        def _():
            st[...] = jnp.zeros_like(st)

        tril_m = m_ref[0]
        tril = tril_m.astype(f32) > 0.5

        def one_head(hh):
            cols = slice(hh * LANE, (hh + 1) * LANE)
            ST = st[hh]
            ck_ref[hh, 0] = ST
            pr = _hgrn_prep(q_ref[:, cols], f_ref[:, cols], lb_ref[:, cols], tril_m, m_ref[1])
            qtb, ktb, kdb = pr["qt"].astype(bf16), pr["kt"].astype(bf16), pr["kd"].astype(bf16)
            eGL = pr["eGL"]
            vb = v_ref[:, cols].astype(bf16)
            A = jnp.where(tril, lax.dot_general(qtb, ktb, NT, preferred_element_type=f32), 0.0)
            o = jnp.dot(A.astype(bf16), vb, preferred_element_type=f32)
            outs = []
            for ci in range(nch):
                lo = ci * HGRN_CHUNK
                sl = slice(lo, lo + HGRN_CHUNK)
                outs.append(o[sl] + lax.dot_general(qtb[sl], ST.astype(bf16), NT, preferred_element_type=f32))
                ST = ST * eGL[lo:lo + 1, :] + lax.dot_general(vb[sl], kdb[sl], TN, preferred_element_type=f32)
            st[hh] = ST
            of = jnp.concatenate(outs, axis=0)
            o_ref[:, cols] = of
            rms = lax.rsqrt(jnp.mean(of * of, axis=-1, keepdims=True) + EPS)
            ogv = og_ref[:, cols]
            y_ref[:, cols] = ((of * rms * nw_ref[...]) * (ogv * _sigmoid(ogv))).astype(bf16)

        for hh in range(HGRN_PAIR):
            one_head(hh)

        if ride is not None:
            @pl.when(jnp.logical_and(pl.program_id(0) == ngrp - 1, j == nsb - 1))
            def _():
                _chip_finish(src_ref, got_ref, sems[0], sems[1], ride[1])

    wide = HGRN_PAIR * LANE
    ngrp = 4 // HGRN_PAIR
    col = lambda off: pl.BlockSpec((sb, wide), lambda h, j: (j, off // HGRN_PAIR + h))
    riding = ride is not None
    res = pl.pallas_call(
        body,
        grid=(ngrp, nsb),
        in_specs=[col(0), col(4), col(8), col(12), pl.BlockSpec((1, wide), lambda h, j: (0, h)),
                  pl.BlockSpec((1, LANE), lambda h, j: (0, 0)),
                  pl.BlockSpec((3, sb, sb), lambda h, j: (0, 0, 0))] + ([_ANY] if riding else []),
        out_specs=[col(0), col(0), pl.BlockSpec((HGRN_PAIR, 1, LANE, LANE), lambda h, j: (h, j, 0, 0))]
        + ([_ANY] if riding else []),
        out_shape=[SDS((S, HGRN_W), bf16), SDS((S, HGRN_W), f32), SDS((4, nsb, LANE, LANE), f32)]
        + ([_chip_out_shape(*ride)] if riding else []),
        scratch_shapes=[pltpu.VMEM((HGRN_PAIR, LANE, LANE), f32)] + (list(_CHIP_SEMS) if riding else []),
        compiler_params=_cparams(("arbitrary", "arbitrary") if riding else ("parallel", "arbitrary")),
        name="hgrn_fwd",
    )(hg, hg, hg, hg, lb, normw, _chunk_masks(), *([ride[0]] if riding else []))
    return tuple(res) if riding else (*res, None)


def _hgrn_bwd(hg, o_raw, dy, ck, lb, normw, ride=None):
    S = hg.shape[0]
    sb = HGRN_SB
    nsb = S // sb
    nch = sb // HGRN_CHUNK

    def body(q_ref, f_ref, v_ref, og_ref, o_ref, dy_ref, ck_ref, lb_ref, nw_ref, m_ref, *rest):
        src_ref, outs, got_ref, (dst, alb, anw), sems = _ride_split(ride, rest, 6, 3)
        dq_ref, df_ref, dv_ref, dog_ref, glb_ref, gnw_ref = outs
        j = pl.program_id(1)
        if ride is not None:
            @pl.when(jnp.logical_and(pl.program_id(0) == 0, j == 0))
            def _():
                _chip_start(src_ref, got_ref, sems[0], sems[1], ride[1])

        @pl.when(j == 0)
        def _():
            dst[...] = jnp.zeros_like(dst)
            alb[...] = jnp.zeros_like(alb)
            anw[...] = jnp.zeros_like(anw)

        tril_m = m_ref[0]
        tril = tril_m.astype(f32) > 0.5
        nw = nw_ref[...]

        def one_head(hh):
            cols = slice(hh * LANE, (hh + 1) * LANE)
            lbv = lb_ref[:, cols]
            q_raw = q_ref[:, cols]
            pr = _hgrn_prep(q_raw, f_ref[:, cols], lbv, tril_m, m_ref[1])
            qt, kt, kd, eGL = pr["qt"], pr["kt"], pr["kd"], pr["eGL"]
            qtb, ktb, kdb = qt.astype(bf16), kt.astype(bf16), kd.astype(bf16)
            vb = v_ref[:, cols].astype(bf16)

            o = o_ref[:, cols]
            ogv = og_ref[:, cols]
            sog = _sigmoid(ogv)
            rms = lax.rsqrt(jnp.mean(o * o, axis=-1, keepdims=True) + EPS)
            oh = o * rms
            dyv = dy_ref[:, cols]
            dog_ref[:, cols] = (dyv * (oh * nw) * (sog * (1.0 + ogv * (1.0 - sog)))).astype(bf16)
            dohw = dyv * (ogv * sog)
            anw[:, cols] += _colsum8(dohw * oh)
            doh = dohw * nw
            do = rms * (doh - oh * jnp.mean(doh * oh, axis=-1, keepdims=True))
            dob = do.astype(bf16)

            Ab = jnp.where(tril, lax.dot_general(qtb, ktb, NT, preferred_element_type=f32), 0.0).astype(bf16)
            dAb = jnp.where(tril, lax.dot_general(dob, vb, NT, preferred_element_type=f32), 0.0).astype(bf16)
            dv_acc = lax.dot_general(Ab, dob, TN, preferred_element_type=f32)
            dqt = jnp.dot(dAb, ktb, preferred_element_type=f32)
            dkt = lax.dot_general(dAb, qtb, TN, preferred_element_type=f32)

            ST = ck_ref[hh, 0]
            states = []
            for ci in range(nch):
                lo = ci * HGRN_CHUNK
                sl = slice(lo, lo + HGRN_CHUNK)
                states.append(ST)
                ST = ST * eGL[lo:lo + 1, :] + lax.dot_general(vb[sl], kdb[sl], TN, preferred_element_type=f32)

            dST = dst[hh]
            dqt_i, dkd_i, dv_i, deg_i = [None] * nch, [None] * nch, [None] * nch, [None] * nch
            for ci in reversed(range(nch)):
                lo = ci * HGRN_CHUNK
                sl = slice(lo, lo + HGRN_CHUNK)
                ST0 = states[ci]
                dSTb = dST.astype(bf16)
                dv_i[ci] = lax.dot_general(kdb[sl], dSTb, NT, preferred_element_type=f32)
                dqt_i[ci] = jnp.dot(dob[sl], ST0.astype(bf16), preferred_element_type=f32)
                dkd_i[ci] = jnp.dot(vb[sl], dSTb, preferred_element_type=f32)
                deg_i[ci] = jnp.broadcast_to(jnp.sum(dST * ST0, axis=0, keepdims=True), (HGRN_CHUNK, LANE))
                dST = dST * eGL[lo:lo + 1, :] + lax.dot_general(dob[sl], qtb[sl], TN, preferred_element_type=f32)
            dst[hh] = dST

            dqt = dqt + jnp.concatenate(dqt_i, axis=0)
            dkd = jnp.concatenate(dkd_i, axis=0)
            dv_ref[:, cols] = (dv_acc + jnp.concatenate(dv_i, axis=0)).astype(bf16)
            deg = jnp.concatenate(deg_i, axis=0)

            dqs = dqt * pr["eG"]
            dkdkd = dkd * kd
            dG = dqt * qt - dkt * kt - dkdkd
            dk = dkt * pr["einv"] + dkd * pr["edec"]
            dGL = _mask_dot(m_ref[1], dkdkd) + eGL * deg
            dg = _mask_dot(m_ref[2], dG) + dGL
            df = dg / pr["f"] - dk
            sig = pr["sig"]
            df_ref[:, cols] = (df * (1.0 - lbv) * (sig * (1.0 - sig))).astype(bf16)
            alb[:, cols] += _colsum8(df * (1.0 - sig))
            sq = pr["sq"]
            dq_ref[:, cols] = (dqs * (sq * (1.0 + q_raw * (1.0 - sq)))).astype(bf16)

        for hh in range(HGRN_PAIR):
            one_head(hh)

        @pl.when(j == nsb - 1)
        def _():
            glb_ref[...] = jnp.broadcast_to(jnp.sum(alb[...], axis=0, keepdims=True), (SUBLANE, wide))
            gnw_ref[...] = jnp.broadcast_to(jnp.sum(anw[...], axis=0, keepdims=True), (SUBLANE, wide))

        if ride is not None:
            @pl.when(jnp.logical_and(pl.program_id(0) == ngrp - 1, j == nsb - 1))
            def _():
                _chip_finish(src_ref, got_ref, sems[0], sems[1], ride[1])

    wide = HGRN_PAIR * LANE
    ngrp = 4 // HGRN_PAIR
    rev = lambda off: pl.BlockSpec((sb, wide), lambda h, j: (nsb - 1 - j, off // HGRN_PAIR + h))
    stat = pl.BlockSpec((SUBLANE, wide), lambda h, j: (0, h))
    riding = ride is not None
    res = pl.pallas_call(
        body,
        grid=(ngrp, nsb),
        in_specs=[rev(0), rev(4), rev(8), rev(12), rev(0), rev(0),
                  pl.BlockSpec((HGRN_PAIR, 1, LANE, LANE), lambda h, j: (h, nsb - 1 - j, 0, 0)),
                  pl.BlockSpec((1, wide), lambda h, j: (0, h)), pl.BlockSpec((1, LANE), lambda h, j: (0, 0)),
                  pl.BlockSpec((3, sb, sb), lambda h, j: (0, 0, 0))]
        + ([_ANY] if riding else []),
        out_specs=[rev(0), rev(0), rev(0), rev(0), stat, stat] + ([_ANY] if riding else []),
        out_shape=[SDS((S, HGRN_W), bf16)] * 4 + [SDS((SUBLANE, HGRN_W), f32)] * 2
        + ([_chip_out_shape(*ride)] if riding else []),
        scratch_shapes=[pltpu.VMEM((HGRN_PAIR, LANE, LANE), f32), pltpu.VMEM((SUBLANE, wide), f32),
                        pltpu.VMEM((SUBLANE, wide), f32)] + (list(_CHIP_SEMS) if riding else []),
        compiler_params=_cparams(("arbitrary", "arbitrary") if riding else ("parallel", "arbitrary")),
        name="hgrn_bwd",
    )(hg, hg, hg, hg, o_raw, dy, ck, lb, normw, _chunk_masks(), *([ride[0]] if riding else []))
    return tuple(res) if riding else (*res, None)


def _lb_fwd(raw):
    def body(r_ref, o_ref):
        r = r_ref[...]
        m = jnp.max(r, axis=0, keepdims=True)
        e = jnp.exp(r - m)
        o_ref[...] = (e / jnp.sum(e, axis=0, keepdims=True))[0:1]

    return pl.pallas_call(body, out_shape=SDS((1, raw.shape[1]), f32), name="lb_fwd")(raw)


def _lb_bwd(raw, dlb):
    def body(r_ref, d_ref, o_ref):
        r = r_ref[...]
        m = jnp.max(r, axis=0, keepdims=True)
        e = jnp.exp(r - m)
        s = e / jnp.sum(e, axis=0, keepdims=True)
        s0 = s[0:1]
        onehot0 = jnp.where(lax.broadcasted_iota(jnp.int32, r.shape, 0) == 0, 1.0, 0.0)
        o_ref[...] = d_ref[...] * s0 * (onehot0 - s)

    return pl.pallas_call(body, out_shape=SDS(raw.shape, f32), name="lb_bwd")(raw, dlb)


def _gate_fwd(a, b, gc):
    S, D = a.shape
    tm = _pick(S, 512)

    def body(a_ref, b_ref, g0_ref, g1_ref, o_ref):
        s0, s1 = _sigmoid(g0_ref[...].astype(f32)), _sigmoid(g1_ref[...].astype(f32))
        o_ref[...] = (s0 * a_ref[...].astype(f32) + s1 * b_ref[...].astype(f32)).astype(bf16)

    row = pl.BlockSpec((tm, D), lambda i: (i, 0))
    return pl.pallas_call(
        body,
        grid=(S // tm,),
        in_specs=[row, row, row, pl.BlockSpec((tm, D), lambda i: (i, 1))],
        out_specs=row,
        out_shape=SDS((S, D), bf16),
        compiler_params=_cparams(("parallel",)),
        name="gate_fwd",
    )(a, b, gc, gc)


def _gate_bwd(dm, a, b, gc):
    S, D = a.shape
    tm = _pick(S, 512)

    def body(dm_ref, a_ref, b_ref, g0_ref, g1_ref, da_ref, db_ref, dg_ref):
        dmv = dm_ref[...].astype(f32)
        s0, s1 = _sigmoid(g0_ref[...].astype(f32)), _sigmoid(g1_ref[...].astype(f32))
        da_ref[...] = (dmv * s0).astype(bf16)
        db_ref[...] = (dmv * s1).astype(bf16)
        dg_ref[:, :D] = (dmv * a_ref[...].astype(f32) * (s0 * (1.0 - s0))).astype(bf16)
        dg_ref[:, D:] = (dmv * b_ref[...].astype(f32) * (s1 * (1.0 - s1))).astype(bf16)

    row = pl.BlockSpec((tm, D), lambda i: (i, 0))
    wide = pl.BlockSpec((tm, 2 * D), lambda i: (i, 0))
    return pl.pallas_call(
        body,
        grid=(S // tm,),
        in_specs=[row, row, row, row, pl.BlockSpec((tm, D), lambda i: (i, 1))],
        out_specs=[row, row, wide],
        out_shape=[SDS((S, D), bf16), SDS((S, D), bf16), SDS((S, 2 * D), bf16)],
        compiler_params=_cparams(("parallel",)),
        name="gate_bwd",
    )(dm, a, b, gc, gc)


CONV_ROWS = 512
INV_SQRT2 = 0.7071067811865476
INV_SQRT_2PI = 0.3989422804014327


CONV_HALO = 16


def _tile8(a, rows):
    return jnp.tile(a, (rows // a.shape[0], 1))


def _conv_rows(u_ref, w, b, r0, first):
    R = CONV_ROWS
    cur = u_ref[pl.ds(r0, R), :].astype(f32)
    prev8 = u_ref[pl.ds(pl.multiple_of(jnp.maximum(r0 - CONV_HALO, 0), CONV_HALO), CONV_HALO), :].astype(f32)
    prev8 = jnp.where(first, 0.0, prev8)
    row = lax.broadcasted_iota(jnp.int32, (R, LANE), 0)
    x1 = jnp.where(row < 1, _tile8(pltpu.roll(prev8, 1, 0), R), pltpu.roll(cur, 1, 0))
    x2 = jnp.where(row < 2, _tile8(pltpu.roll(prev8, 2, 0), R), pltpu.roll(cur, 2, 0))
    c = ((b + w[0:1] * x2) + w[1:2] * x1) + w[2:3] * cur
    return c, x2, x1, cur


def _conv_fwd(ug, uv, wg, wv, bg, bv):
    S, F = ug.shape
    nchunk = S // CONV_ROWS

    def body(ug_ref, uv_ref, wg_ref, wv_ref, bg_ref, bv_ref, o_ref):
        wgv, wvv, bgv, bvv = wg_ref[...], wv_ref[...], bg_ref[...], bv_ref[...]

        def step(ci, carry):
            r0 = pl.multiple_of(ci * CONV_ROWS, CONV_ROWS)
            cg = _conv_rows(ug_ref, wgv, bgv, r0, ci == 0)[0]
            cv = _conv_rows(uv_ref, wvv, bvv, r0, ci == 0)[0]
            gelu = 0.5 * cg * (1.0 + lax.erf(cg * INV_SQRT2))
            o_ref[pl.ds(r0, CONV_ROWS), :] = (gelu * cv).astype(bf16)
            return carry

        lax.fori_loop(0, nchunk, step, 0)

    col = pl.BlockSpec((S, LANE), lambda j: (0, j))
    w3 = pl.BlockSpec((3, LANE), lambda j: (0, j))
    b1 = pl.BlockSpec((1, LANE), lambda j: (0, j))
    return pl.pallas_call(
        body,
        grid=(F // LANE,),
        in_specs=[col, col, w3, w3, b1, b1],
        out_specs=col,
        out_shape=SDS((S, F), bf16),
        compiler_params=_cparams(("parallel",), VMEM_BIG),
        name="conv_fwd",
    )(ug, uv, wg, wv, bg, bv)


def _conv_bwd(ug, uv, dact, wg, wv, bg, bv):
    S, F = ug.shape
    R = CONV_ROWS
    nchunk = S // R

    def body(ug_ref, uv_ref, da_ref, wg_ref, wv_ref, bg_ref, bv_ref, dug_ref, duv_ref, sg_ref, sv_ref, dcg, dcv):
        wgv, wvv, bgv, bvv = wg_ref[...], wv_ref[...], bg_ref[...], bv_ref[...]
        zero = jnp.zeros((SUBLANE, LANE), f32)

        def fwd_step(ci, acc):
            r0 = pl.multiple_of(ci * R, R)
            cg, g2, g1, g0 = _conv_rows(ug_ref, wgv, bgv, r0, ci == 0)
            cv, v2, v1, v0 = _conv_rows(uv_ref, wvv, bvv, r0, ci == 0)
            da = da_ref[pl.ds(r0, R), :].astype(f32)
            cdf = 0.5 * (1.0 + lax.erf(cg * INV_SQRT2))
            pdf = INV_SQRT_2PI * jnp.exp(-0.5 * cg * cg)
            dg = da * cv * (cdf + cg * pdf)
            dv = da * (cg * cdf)
            dcg[pl.ds(r0, R), :] = dg
            dcv[pl.ds(r0, R), :] = dv
            new = (acc[0] + _colsum8(dg * g2), acc[1] + _colsum8(dg * g1), acc[2] + _colsum8(dg * g0),
                   acc[3] + _colsum8(dg),
                   acc[4] + _colsum8(dv * v2), acc[5] + _colsum8(dv * v1), acc[6] + _colsum8(dv * v0),
                   acc[7] + _colsum8(dv))
            return new

        acc = lax.fori_loop(0, nchunk, fwd_step, (zero,) * 8)
        rows = lax.broadcasted_iota(jnp.int32, (SUBLANE, LANE), 0)

        def stats(parts):
            out = jnp.zeros((SUBLANE, LANE), f32)
            for k, pt in enumerate(parts):
                out = jnp.where(rows == k, jnp.sum(pt, axis=0, keepdims=True), out)
            return out

        sg_ref[...] = stats(acc[0:4])
        sv_ref[...] = stats(acc[4:8])

        def du_rows(dc, w, r0, last):
            cur = dc[pl.ds(r0, R), :]
            nxt = dc[pl.ds(pl.multiple_of(jnp.minimum(r0 + R, S - SUBLANE), SUBLANE), SUBLANE), :]
            nxt = jnp.where(last, 0.0, nxt)
            row = lax.broadcasted_iota(jnp.int32, (R, LANE), 0)
            y1 = jnp.where(row >= R - 1, _tile8(pltpu.roll(nxt, SUBLANE - 1, 0), R), pltpu.roll(cur, R - 1, 0))
            y2 = jnp.where(row >= R - 2, _tile8(pltpu.roll(nxt, SUBLANE - 2, 0), R), pltpu.roll(cur, R - 2, 0))
            return w[2:3] * cur + w[1:2] * y1 + w[0:1] * y2

        def bwd_step(ci, carry):
            r0 = pl.multiple_of(ci * R, R)
            last = ci == nchunk - 1
            dug_ref[pl.ds(r0, R), :] = du_rows(dcg, wgv, r0, last).astype(bf16)
            duv_ref[pl.ds(r0, R), :] = du_rows(dcv, wvv, r0, last).astype(bf16)
            return carry

        lax.fori_loop(0, nchunk, bwd_step, 0)

    col = pl.BlockSpec((S, LANE), lambda j: (0, j))
    w3 = pl.BlockSpec((3, LANE), lambda j: (0, j))
    b1 = pl.BlockSpec((1, LANE), lambda j: (0, j))
    st = pl.BlockSpec((SUBLANE, LANE), lambda j: (0, j))
    return pl.pallas_call(
        body,
        grid=(F // LANE,),
        in_specs=[col, col, col, w3, w3, b1, b1],
        out_specs=[col, col, st, st],
        out_shape=[SDS((S, F), bf16), SDS((S, F), bf16), SDS((SUBLANE, F), f32), SDS((SUBLANE, F), f32)],
        scratch_shapes=[pltpu.VMEM((S, LANE), f32), pltpu.VMEM((S, LANE), f32)],
        compiler_params=_cparams(("parallel",), VMEM_BIG),
        name="conv_bwd",
    )(ug, uv, dact, wg, wv, bg, bv)


def _adam_math(w, g, m, v):
    m = ADAM_B1 * m + (1.0 - ADAM_B1) * g
    v = ADAM_B2 * v + (1.0 - ADAM_B2) * (g * g)
    m_hat = m / (1.0 - ADAM_B1 ** ADAM_STEP)
    v_hat = v / (1.0 - ADAM_B2 ** ADAM_STEP)
    delta = -ADAM_LR * (m_hat / (jnp.sqrt(v_hat) + ADAM_EPS) + ADAM_WD * w)
    return delta, m, v


def _adamw(w, m, v, g, name):
    R, C = w.shape
    parts = g.ndim == 3
    tr = R
    for t in (256, 128, 64, 32, 16):
        if R % t == 0 and R > t:
            tr = t
            break

    def body(w_ref, m_ref, v_ref, g_ref, go_ref, d_ref, mo_ref, vo_ref):
        if parts:
            gv = ((g_ref[0].astype(f32) + g_ref[1].astype(f32)) + g_ref[2].astype(f32)) + g_ref[3].astype(f32)
        else:
            gv = g_ref[...]
        go_ref[...] = gv
        d, mn, vn = _adam_math(w_ref[...], gv, m_ref[...], v_ref[...])
        d_ref[...] = d
        mo_ref[...] = mn
        vo_ref[...] = vn

    row = pl.BlockSpec((tr, C), lambda i: (i, 0))
    gspec = pl.BlockSpec((4, tr, C), lambda i: (0, i, 0)) if parts else row
    return pl.pallas_call(
        body,
        grid=(R // tr,),
        in_specs=[row, row, row, gspec],
        out_specs=[row] * 4,
        out_shape=[SDS((R, C), f32)] * 4,
        compiler_params=_cparams(("parallel",)),
        name=name,
    )(w, m, v, g)


def _sum8(parts, name):
    _, _, R, C = parts.shape

    def body(p_ref, o_ref):
        acc = p_ref[0, 0]
        for c in range(2):
            for k in range(4):
                if c or k:
                    acc = acc + p_ref[c, k]
        o_ref[...] = acc

    return pl.pallas_call(body, out_shape=SDS((R, C), f32), name=name)(parts)


def _pair_add(by_core, b, name):
    _, K, R, C = by_core.shape
    tr = R // 2 if R % 32 == 0 else R

    def body(c_ref, a_ref, b_ref, o_ref):
        o_ref[...] = (a_ref[0].astype(f32) + b_ref[...].astype(f32)).astype(bf16)

    blk = pl.BlockSpec((1, tr, C), lambda k, i, c: (k, i, 0))
    return pl.pallas_call(
        body,
        grid_spec=pltpu.PrefetchScalarGridSpec(
            num_scalar_prefetch=1,
            grid=(K, R // tr),
            in_specs=[pl.BlockSpec((1, 1, tr, C), lambda k, i, c: (c[0], k, i, 0)), blk],
            out_specs=blk,
        ),
        out_shape=SDS((K, R, C), bf16),
        compiler_params=_cparams(("parallel", "parallel")),
        name=name,
    )(lax.axis_index("c").astype(jnp.int32).reshape(1), by_core, b)


_ANY = pl.BlockSpec(memory_space=pl.ANY)


def _chip_copies(src_ref, out_ref, send_sems, recv_sems, gather):
    x, y, c = lax.axis_index("x"), lax.axis_index("y"), lax.axis_index("c")
    mine = 2 * x + y

    def piece(k):
        return src_ref if gather else src_ref.at[k]

    sends, recvs = [], []
    for j, (px, py) in enumerate([(1 - x, y), (x, 1 - y), (1 - x, 1 - y)]):
        sends.append(pltpu.make_async_remote_copy(
            src_ref=piece(2 * px + py), dst_ref=out_ref.at[mine], send_sem=send_sems.at[j],
            recv_sem=recv_sems.at[j], device_id=(px, py, c), device_id_type=MESH))
        recvs.append(pltpu.make_async_remote_copy(
            src_ref=piece(mine), dst_ref=out_ref.at[2 * px + py], send_sem=send_sems.at[j],
            recv_sem=recv_sems.at[j], device_id=(px, py, c), device_id_type=MESH))
    return sends, recvs


def _chip_start(src_ref, out_ref, send_sems, recv_sems, gather):
    for cp in _chip_copies(src_ref, out_ref, send_sems, recv_sems, gather)[0]:
        cp.start()


def _chip_finish(src_ref, out_ref, send_sems, recv_sems, gather):
    sends, recvs = _chip_copies(src_ref, out_ref, send_sems, recv_sems, gather)
    for cp in recvs:
        cp.wait_recv()
    for cp in sends:
        cp.wait_send()


def _chip_out_shape(src, gather):
    return SDS((4,) + tuple(src.shape if gather else src.shape[1:]), src.dtype)


_CHIP_SEMS = [pltpu.SemaphoreType.DMA((3,)), pltpu.SemaphoreType.DMA((3,))]


def _fill_own(out, src, gather):
    mine = 2 * lax.axis_index("x") + lax.axis_index("y")
    own = src if gather else lax.dynamic_index_in_dim(src, mine, axis=0, keepdims=False)
    return lax.dynamic_update_index_in_dim(out, own, mine, axis=0)


def _chip_comm(src, gather, name):
    def body(src_ref, out_ref, send_sems, recv_sems):
        _chip_start(src_ref, out_ref, send_sems, recv_sems, gather)
        _chip_finish(src_ref, out_ref, send_sems, recv_sems, gather)

    out = pl.pallas_call(
        body,
        in_specs=[_ANY],
        out_specs=_ANY,
        out_shape=_chip_out_shape(src, gather),
        scratch_shapes=list(_CHIP_SEMS),
        name=name,
    )(src)
    return _fill_own(out, src, gather)


_HBM = pl.BlockSpec(memory_space=pltpu.HBM)
_SEM = pl.BlockSpec(memory_space=pltpu.SEMAPHORE)
_EFFECT = pltpu.SideEffectType.DATAFLOW_SIDE_EFFECTING
_SPLIT_PEERS = {"chip_gather": 3, "chip_xchg": 3, "core_gather": 1, "core_swap": 1}


def _split_land(src, kind):
    if kind == "core_gather":
        return SDS((2,) + tuple(src.shape), src.dtype)
    if kind == "core_swap":
        return SDS(tuple(src.shape[1:]), src.dtype)
    return _chip_out_shape(src, kind == "chip_gather")


def _split_copies(src_ref, land_ref, sems, kind):
    x, y, c = lax.axis_index("x"), lax.axis_index("y"), lax.axis_index("c")
    n = _SPLIT_PEERS[kind]
    if kind == "core_gather":
        routes = [((x, y, 1 - c), src_ref, land_ref.at[c], land_ref.at[1 - c])]
    elif kind == "core_swap":
        routes = [((x, y, 1 - c), src_ref.at[1 - c], land_ref, land_ref)]
    else:
        mine = 2 * x + y
        gather = kind == "chip_gather"
        routes = [((px, py, c), src_ref if gather else src_ref.at[2 * px + py], land_ref.at[mine],
                   land_ref.at[2 * px + py]) for px, py in [(1 - x, y), (x, 1 - y), (1 - x, 1 - y)]]
    sends, recvs = [], []
    for j, (peer, piece, there, here) in enumerate(routes):
        sends.append(pltpu.make_async_remote_copy(src_ref=piece, dst_ref=there, send_sem=sems[j],
                                                  recv_sem=sems[n + j], device_id=peer, device_id_type=MESH))
        recvs.append(pltpu.make_async_remote_copy(src_ref=piece, dst_ref=here, send_sem=sems[j],
                                                  recv_sem=sems[n + j], device_id=peer, device_id_type=MESH))
    return sends, recvs


def _split_start(src, kind, name, after=None):
    land = _split_land(src, kind)
    ns = 2 * _SPLIT_PEERS[kind]
    n_in = 2 if after is None else 3

    def body(*refs):
        src_ref, land_ref = refs[:2]
        outs = refs[n_in:]
        for cp in _split_copies(src_ref, land_ref, outs[:ns], kind)[0]:
            cp.start()
        token = outs[ns + 2]
        token[...] = jnp.zeros_like(token)

    res = pl.pallas_call(
        body,
        name=name,
        out_shape=(pltpu.SemaphoreType.DMA(()),) * ns
        + (pltpu.HBM(src.shape, src.dtype), pltpu.HBM(land.shape, land.dtype), SDS((SUBLANE, LANE), f32)),
        in_specs=(_HBM, _HBM) + (() if after is None else (_ANY,)),
        out_specs=(_SEM,) * ns + (_HBM, _HBM, pl.BlockSpec(memory_space=pltpu.VMEM)),
        input_output_aliases={0: ns, 1: ns + 1},
        compiler_params=pltpu.CompilerParams(has_side_effects=_EFFECT),
    )(pltpu.with_memory_space_constraint(src, pltpu.HBM),
      pltpu.with_memory_space_constraint(lax.empty(land.shape, land.dtype), pltpu.HBM),
      *(() if after is None else (after,)))
    return (res[:ns], res[ns], res[ns + 1]), res[ns + 2]


def _split_wait(state, after, kind, name):
    sems, src_thru, land_thru = state
    ns = 2 * _SPLIT_PEERS[kind]

    def body(src_ref, land_ref, *rest):
        sends, recvs = _split_copies(src_ref, land_ref, rest[:ns], kind)
        for cp in recvs:
            cp.wait_recv()
        for cp in sends:
            cp.wait_send()

    src_out, got = pl.pallas_call(
        body,
        name=name,
        out_shape=(pltpu.HBM(src_thru.shape, src_thru.dtype), pltpu.HBM(land_thru.shape, land_thru.dtype)),
        in_specs=(_HBM, _HBM) + (_SEM,) * ns + (_ANY,),
        out_specs=(_HBM, _HBM),
        input_output_aliases={0: 0, 1: 1},
        compiler_params=pltpu.CompilerParams(has_side_effects=_EFFECT),
    )(src_thru, land_thru, *sems, after)
    if kind == "core_swap":
        return got, src_out
    if kind == "core_gather":
        return lax.dynamic_update_index_in_dim(got, src_out, lax.axis_index("c"), axis=0)
    return _fill_own(got, src_out, kind == "chip_gather")


def _core_gather(src, name):
    def body(src_ref, out_ref, send_sem, recv_sem):
        x, y, c = lax.axis_index("x"), lax.axis_index("y"), lax.axis_index("c")
        cp = pltpu.make_async_remote_copy(src_ref=src_ref, dst_ref=out_ref.at[c], send_sem=send_sem,
                                          recv_sem=recv_sem, device_id=(x, y, 1 - c), device_id_type=MESH)
        cp.start()
        pltpu.make_async_remote_copy(src_ref=src_ref, dst_ref=out_ref.at[1 - c], send_sem=send_sem,
                                     recv_sem=recv_sem, device_id=(x, y, 1 - c), device_id_type=MESH).wait_recv()
        cp.wait_send()

    out = pl.pallas_call(
        body,
        in_specs=[_ANY],
        out_specs=_ANY,
        out_shape=SDS((2,) + tuple(src.shape), src.dtype),
        scratch_shapes=[pltpu.SemaphoreType.DMA, pltpu.SemaphoreType.DMA],
        name=name,
    )(src)
    return lax.dynamic_update_index_in_dim(out, src, lax.axis_index("c"), axis=0)


_PACK_A = (("w_in", (1088, 1024)),)
_PACK_B = (("w_ba", (512, 128)), ("w_bh", (512, 128)), ("w_out", (128, 1024)), ("w_up", (704, 1024)),
           ("w_down", (352, 1024)))
_PACK_SIZES = _PACK_A + _PACK_B
_TRANSPOSED = ("w_in", "w_up")


def _slab_rows(sizes):
    return sum(r * c for _, (r, c) in sizes) // D_MODEL


def _pack_rows(d, sizes):
    n = d[sizes[0][0]].shape[0]
    return jnp.concatenate([d[k].reshape(n, -1, D_MODEL) for k, _ in sizes], axis=1)


def _unpack_rows(slab, sizes):
    n = slab.shape[0]
    out, lo = {}, 0
    for key, (r, c) in sizes:
        rows = r * c // D_MODEL
        out[key] = slab[:, lo:lo + rows].reshape(n, r, c)
        lo += rows
    return out


def _by_core(gslab):
    return jnp.swapaxes(gslab.reshape((4, 2) + gslab.shape[1:]), 0, 1)


def _cols_to_full(t):
    return jnp.swapaxes(t, 0, 1).reshape(t.shape[1], -1)


def _full_to_cols(t):
    K = t.shape[0]
    return jnp.swapaxes(t.reshape(K, 8, -1), 0, 1)


_SMALL = (("pre_mix_norm", (1, 1024)), ("rel_bias", (32, 24)), ("hgrn_lb_raw", (2, 512)), ("hgrn_norm", (1, 128)),
          ("post_mix_norm", (1, 1024)), ("pre_ffn_norm", (1, 1024)), ("conv_b", (1, 5632)),
          ("post_ffn_norm", (1, 1024)))
_SMALL_ROWS = 96
_CONVW_ROWS = 136


_SMALL_USED = sum(r * c for _, (r, c) in _SMALL)


def _pack_small(d, extra=None):
    flat = jnp.concatenate([d[k].reshape(-1) for k, _ in _SMALL] + ([] if extra is None else [extra.reshape(-1)]))
    flat = jnp.pad(flat, (0, _SMALL_ROWS * LANE - flat.shape[0]))
    return flat.reshape(_SMALL_ROWS, LANE)


def _unpack_small(p):
    flat = p.reshape(-1)
    out, lo = {}, 0
    for k, shp in _SMALL:
        n = shp[0] * shp[1]
        out[k] = flat[lo:lo + n].reshape(shp)
        lo += n
    return out


def _local_step(x, tgt, P, plan):
    S = x.shape[0]
    P = dict(P)
    lb = _lb_fwd(P["hgrn_lb_raw"])
    hs = _prep(x, P["pre_mix_norm"], plan.start_token())
    h1 = hs[0]
    consts = [_bias_consts(d) for d in DILATIONS]
    biases, dep = [], h1
    for g in range(N_GROUPS):
        tab_t = P["rel_bias"][:, 8 * g:8 * g + 8].T
        dep = _bias_build(tab_t, consts[g][0], consts[g][1], f"bias_build{g}", dep)
        biases.append(dep.reshape(8, ATTN_BLOCK, 2 * ATTN_BLOCK))
    W = dict(plan.weights_a(dep))
    qkv = [_mm(hs[g], W["wt_qkv"][g], "nt", bf16, f"proj_qkv{g}") for g in range(N_GROUPS)]
    hg = _mm(h1, W["wt_hg"], "nt", f32, "proj_hg")
    gc = _mm(h1, W["wt_gate"], "nt", bf16, "proj_gate")
    obuf, lbuf, token = [], [], None
    for g, d in enumerate(DILATIONS):
        o_g, l_g = _attn_fwd(qkv[g], biases[g], (S // d) // ATTN_BLOCK, f"attn_fwd{g}", after=token)
        lbuf.append(l_g)
        obuf.append(o_g)
        if g == 0:
            token = plan.forward_b(o_g)
    y_attn, y_attn_b, w0, w1, w2 = _attn_merge(obuf[0], obuf[1], obuf[2], lbuf[0], lbuf[1], lbuf[2])
    y_hgrn, o_raw, ck, _ = _hgrn_fwd(hg, lb, P["hgrn_norm"])
    wb = plan.weights_b(y_hgrn)
    P["conv_w"] = wb.pop("conv_w")
    W.update(wb)
    a = _mm(y_attn_b, W["w_ba"], "nn", bf16, "branch_attn")
    b = _mm(y_hgrn, W["w_bh"], "nn", bf16, "branch_hgrn")
    merged = _gate_fwd(a, b, gc)
    mo, x1, h2 = _mid_fwd(x, merged, W["w_out"], P["post_mix_norm"], P["pre_ffn_norm"])
    ug = _mm(h2, W["wt_up_g"], "nt", bf16, "up_gate")
    uv = _mm(h2, W["wt_up_v"], "nt", bf16, "up_val")
    cw_g, cw_v = P["conv_w"][:, :D_FF], P["conv_w"][:, D_FF:]
    cb_g, cb_v = P["conv_b"][:, :D_FF], P["conv_b"][:, D_FF:]
    act = _conv_fwd(ug, uv, cw_g, cw_v, cb_g, cb_v)
    loss, dy, dfo, g_post_ffn = _final(x1, act, W["w_down"], tgt, P["post_ffn_norm"])
    dact = _mm(dfo, W["w_down"], "nt", bf16, "d_act")
    gW_down = _mm(act, dfo, "tn", f32, "gw_down")
    dug, duv, st_g, st_v = _conv_bwd(ug, uv, dact, cw_g, cw_v, cb_g, cb_v)
    dh2 = _mm([dug, duv], [W["wt_up_g"], W["wt_up_v"]], "nn", f32, "dh2")
    gW_up_g = _mm(dug, h2, "tn", f32, "gw_up_gate")
    gW_up_v = _mm(duv, h2, "tn", f32, "gw_up_val")
    dx1, dmo, g_pre_ffn, g_post_mix = _mid_bwd(dy, dh2, x1, mo, P["pre_ffn_norm"], P["post_mix_norm"])
    dmerged = _mm(dmo, W["w_out"], "nt", bf16, "d_merged")
    gW_out = _mm(merged, dmo, "tn", f32, "gw_out")
    da, db, dgc = _gate_bwd(dmerged, a, b, gc)
    dyattn = _mm(da, W["w_ba"], "nt", f32, "d_yattn")
    gW_ba = _mm(y_attn_b, da, "tn", f32, "gw_ba")
    dyhgrn = _mm(db, W["w_bh"], "nt", f32, "d_yhgrn")
    gW_bh = _mm(y_hgrn, db, "tn", f32, "gw_bh")
    big_b = dict(w_ba=gW_ba, w_bh=gW_bh, w_out=gW_out, w_up=[gW_up_g, gW_up_v], w_down=gW_down)
    dos = _attn_merge_bwd(dyattn, y_attn, w0, w1, w2, after=plan.grads_b_start(big_b))
    dq_h, df_h, dv_h, dog_h, glb8, gnw8, got_b = _hgrn_bwd(hg, o_raw, dyhgrn, ck, lb, P["hgrn_norm"],
                                                          plan.bwd_ride(dos[5]))
    dhg = [dq_h, df_h, dv_h, dog_h]
    g_lb_raw = _lb_bwd(P["hgrn_lb_raw"], glb8[0:1])
    gn = gnw8[0:1]
    g_hgrn_norm = (gn[:, 0:128] + gn[:, 128:256]) + (gn[:, 256:384] + gn[:, 384:512])
    dqkvs, gW_qkv, g_rel = [], [], []
    for g, d in enumerate(DILATIONS):
        dq, dk, dv, dbias = _attn_bwd(qkv[g], biases[g], dos[g], dos[3 + g], lbuf[g], (S // d) // ATTN_BLOCK,
                                      f"attn_bwd{g}")
        dqkvs.append([dq, dk, dv])
        gW_qkv.append(_mm(dqkvs[g], hs[g], "tn", f32, f"gw_qkv{g}"))
        g_rel.append(_bias_grad(dbias.reshape(8, -1), consts[g][0], f"bias_grad{g}"))
    gW_hg = _mm(dhg, h1, "tn", f32, "gw_hg")
    gW_gate = _mm(dgc, h1, "tn", f32, "gw_gate")
    gW_in = gW_qkv + [gW_hg, gW_gate]
    token = plan.grads_a_start(gW_in)
    dh_perm = [_mm(dqkvs[g], W["wt_qkv"][g], "nn", f32, f"dh1_qkv{g}", after=token) for g in (1, 2)]
    token = plan.grads_a_exchange(dh_perm[1])
    dh_main = _mm(dqkvs[0] + dhg + [dgc], [W["wt_qkv"][0], W["wt_hg"], W["wt_gate"]], "nn", f32, "dh1_main",
                  after=token)
    grad_x, g_pre_mix = _first_bwd(x, dx1, _dh_sum(dh_main, dh_perm[0], dh_perm[1]), P["pre_mix_norm"])

    g_conv_w = jnp.concatenate([st_g[0:3], st_v[0:3]], axis=1)
    g_conv_b = jnp.concatenate([st_g[3:4], st_v[3:4]], axis=1)
    small = dict(pre_mix_norm=g_pre_mix, rel_bias=jnp.concatenate(g_rel, axis=1), hgrn_lb_raw=g_lb_raw,
                 hgrn_norm=g_hgrn_norm, post_mix_norm=g_post_mix, pre_ffn_norm=g_pre_ffn, conv_b=g_conv_b,
                 post_ffn_norm=g_post_ffn, conv_w=g_conv_w)
    return loss, grad_x, gW_in, big_b, got_b, small


def _weights_a(both):
    wt = jnp.swapaxes(both, 0, 1).reshape(-1, D_MODEL)
    return dict(
        wt_qkv=[wt[g * QKV_G:(g + 1) * QKV_G] for g in range(N_GROUPS)],
        wt_hg=wt[3 * QKV_G:3 * QKV_G + 4 * HGRN_W],
        wt_gate=wt[3 * QKV_G + 4 * HGRN_W:],
    )


def _weights_b(slabs):
    sh = _unpack_rows(slabs, _PACK_B)
    wt_up = sh["w_up"].reshape(-1, D_MODEL)
    return dict(
        w_ba=_cols_to_full(sh["w_ba"]),
        w_bh=_cols_to_full(sh["w_bh"]),
        w_out=sh["w_out"].reshape(D_MODEL, D_MODEL),
        wt_up_g=wt_up[:D_FF],
        wt_up_v=wt_up[D_FF:],
        w_down=sh["w_down"].reshape(D_FF, D_MODEL),
    )


def _dest_rows(sections, height):
    out = []
    for j in range(8):
        lo, hi, off, pieces = j * height, (j + 1) * height, 0, []
        for s in sections:
            a, b = max(lo, off), min(hi, off + s.shape[0])
            if a < b:
                pieces.append(s[a - off:b - off])
            off += s.shape[0]
        out.append(pieces[0] if len(pieces) == 1 else jnp.concatenate(pieces, axis=0))
    return out


def _grad_blocks_a(sections):
    rows = _dest_rows(sections, 1088)
    return jnp.stack([jnp.stack([rows[2 * k + c].astype(bf16) for k in range(4)]) for c in range(2)])


def _grad_slab_b(g):
    shards = dict(w_ba=_full_to_cols(g["w_ba"]), w_bh=_full_to_cols(g["w_bh"]), w_out=g["w_out"].reshape(8, 128, D_MODEL),
                  w_up=jnp.stack(_dest_rows(g["w_up"], 704)), w_down=g["w_down"].reshape(8, 352, D_MODEL))
    return _pack_rows({k: v.astype(bf16) for k, v in shards.items()}, _PACK_B)


_CONVW_SLAB_ROWS = 16


class _Traffic:
    def __init__(self, slab_a, slab_b, conv_w):
        hi = conv_w.astype(bf16)
        r1 = conv_w - hi.astype(f32)
        mid = r1.astype(bf16)
        lo = (r1 - mid.astype(f32)).astype(bf16)
        bits = jnp.stack([hi, mid, lo]).reshape(-1)
        tail = jnp.pad(bits, (0, _CONVW_SLAB_ROWS * D_MODEL - bits.shape[0])).reshape(_CONVW_SLAB_ROWS, D_MODEL)
        self.slab_b = jnp.concatenate([slab_b, tail], axis=0)
        self.state_a, tok = _split_start(slab_a, "chip_gather", "ag_a_start")
        self.state_b, self.token = _split_start(self.slab_b, "chip_gather", "ag_b_start", after=tok)
        self.chip_sum = None
        self.state = None

    def start_token(self):
        return self.token

    def weights_a(self, after):
        by_chip = _split_wait(self.state_a, after, "chip_gather", "ag_a_wait")
        return _weights_a(_core_gather(by_chip, "ag_a_cores"))

    def forward_b(self, after):
        by_chip = _split_wait(self.state_b, after, "chip_gather", "ag_b_wait")
        self.state, token = _split_start(by_chip, "core_gather", "ag_b_cores_start")
        return token

    def weights_b(self, after):
        both = _split_wait(self.state, after, "core_gather", "ag_b_cores_wait")
        slabs = jnp.swapaxes(both, 0, 1).reshape((8,) + tuple(self.slab_b.shape))
        rows = _slab_rows(_PACK_B)
        out = _weights_b(slabs[:, :rows])
        pieces = slabs[:, rows:].reshape(8, -1)[:, :3 * 3 * 704].reshape(8, 3, 3, 704).astype(f32)
        out["conv_w"] = _cols_to_full((pieces[:, 0] + pieces[:, 1]) + pieces[:, 2])
        return out

    def grads_b_start(self, grads):
        self.state, token = _split_start(_by_core(_grad_slab_b(grads)), "core_swap", "rs_b_cores_start")
        return token

    def bwd_ride(self, after):
        from_sib, by_core = _split_wait(self.state, after, "core_swap", "rs_b_cores_wait")
        self.chip_sum = _pair_add(by_core, from_sib, "rs_b_pair_add")
        return (self.chip_sum, False)

    def grads_a_start(self, sections):
        self.state, token = _split_start(_grad_blocks_a(sections), "core_swap", "rs_a_cores_start")
        return token

    def grads_a_exchange(self, after):
        from_sib, by_core = _split_wait(self.state, after, "core_swap", "rs_a_cores_wait")
        self.state, token = _split_start(_pair_add(by_core, from_sib, "rs_a_pair_add"), "chip_xchg", "rs_a_start")
        return token

    def parts(self, got_b, after):
        parts = _unpack_rows(_fill_own(got_b, self.chip_sum, False), _PACK_B)
        parts["w_in"] = _split_wait(self.state, after, "chip_xchg", "rs_a_wait")
        return parts


def kernel(x, pre_mix_norm, w_in, rel_bias, hgrn_lb_raw, hgrn_norm, w_branch_attn, w_branch_hgrn, w_out, post_mix_norm, pre_ffn_norm, w_up, conv_w, conv_b, w_down, post_ffn_norm, loss_target, m_pre_mix_norm, m_w_in, m_rel_bias, m_hgrn_lb_raw, m_hgrn_norm, m_w_branch_attn, m_w_branch_hgrn, m_w_out, m_post_mix_norm, m_pre_ffn_norm, m_w_up, m_conv_w, m_conv_b, m_w_down, m_post_ffn_norm, v_pre_mix_norm, v_w_in, v_rel_bias, v_hgrn_lb_raw, v_hgrn_norm, v_w_branch_attn, v_w_branch_hgrn, v_w_out, v_post_mix_norm, v_pre_ffn_norm, v_w_up, v_conv_w, v_conv_b, v_w_down, v_post_ffn_norm):
    ci = lax.axis_index("c")
    dev = 4 * lax.axis_index("x") + 2 * lax.axis_index("y") + ci
    tr = lambda t: jnp.swapaxes(t[0], 0, 1)
    wts = dict(w_in=tr(w_in), w_ba=w_branch_attn[0], w_bh=w_branch_hgrn[0], w_out=w_out[0], w_up=tr(w_up),
               w_down=w_down[0])
    mom = dict(w_in=tr(m_w_in), w_ba=m_w_branch_attn[0], w_bh=m_w_branch_hgrn[0], w_out=m_w_out[0], w_up=tr(m_w_up),
               w_down=m_w_down[0])
    var = dict(w_in=tr(v_w_in), w_ba=v_w_branch_attn[0], w_bh=v_w_branch_hgrn[0], w_out=v_w_out[0], w_up=tr(v_w_up),
               w_down=v_w_down[0])
    small_w = dict(pre_mix_norm=pre_mix_norm, rel_bias=rel_bias, hgrn_lb_raw=hgrn_lb_raw, hgrn_norm=hgrn_norm,
                   post_mix_norm=post_mix_norm, pre_ffn_norm=pre_ffn_norm, conv_b=conv_b, post_ffn_norm=post_ffn_norm)
    small_m = dict(pre_mix_norm=m_pre_mix_norm, rel_bias=m_rel_bias, hgrn_lb_raw=m_hgrn_lb_raw, hgrn_norm=m_hgrn_norm,
                   post_mix_norm=m_post_mix_norm, pre_ffn_norm=m_pre_ffn_norm, conv_b=m_conv_b,
                   post_ffn_norm=m_post_ffn_norm)
    small_v = dict(pre_mix_norm=v_pre_mix_norm, rel_bias=v_rel_bias, hgrn_lb_raw=v_hgrn_lb_raw, hgrn_norm=v_hgrn_norm,
                   post_mix_norm=v_post_mix_norm, pre_ffn_norm=v_pre_ffn_norm, conv_b=v_conv_b,
                   post_ffn_norm=v_post_ffn_norm)

    plan = _Traffic(wts["w_in"].astype(bf16),
                    _pack_rows({k: wts[k].astype(bf16)[None] for k, _ in _PACK_B}, _PACK_B)[0], conv_w[0])

    loss8, grad_x, _, _, got_b, small = _local_step(x[0], loss_target[0], small_w, plan)
    parts = plan.parts(got_b, grad_x)
    outs_big = {}
    for k, _ in _PACK_SIZES:
        outs_big[k] = _adamw(wts[k], mom[k], var[k], parts[k], "adamw_" + k)

    spack = jnp.concatenate([_pack_small(small, loss8[0, 0:1]),
                             jnp.pad(small["conv_w"].reshape(-1, LANE), ((0, _CONVW_ROWS - 132), (0, 0)))], axis=0)
    allp = _core_gather(_chip_comm(spack, True, "ag_small_chips"), "ag_small_cores")
    ssum = _sum8(allp, "small_sum")
    gs = ssum[:_SMALL_ROWS]
    loss = ssum[_SMALL_USED // LANE, _SMALL_USED % LANE]
    res_small = _adamw(_pack_small(small_w), _pack_small(small_m), _pack_small(small_v), gs, "adamw_small")
    sm = [_unpack_small(t) for t in res_small]
    g_cw_full = ssum[_SMALL_ROWS:_SMALL_ROWS + 132].reshape(3, 2 * D_FF)
    g_cw = lax.dynamic_slice_in_dim(g_cw_full, dev * 704, 704, axis=1)
    res_cw = _adamw(conv_w[0], m_conv_w[0], v_conv_w[0], g_cw, "adamw_conv_w")

    def pick(i):
        def big_(k):
            t = outs_big[k][i]
            return (jnp.swapaxes(t, 0, 1) if k in _TRANSPOSED else t)[None]
        return [sm[i]["pre_mix_norm"], big_("w_in"), sm[i]["rel_bias"], sm[i]["hgrn_lb_raw"], sm[i]["hgrn_norm"],
                big_("w_ba"), big_("w_bh"), big_("w_out"), sm[i]["post_mix_norm"], sm[i]["pre_ffn_norm"],
                big_("w_up"), res_cw[i][None], sm[i]["conv_b"], big_("w_down"), sm[i]["post_ffn_norm"]]

    return (loss, grad_x[None], *pick(0), *pick(1), *pick(2), *pick(3))
```

```python
import functools
import math

import jax
import jax.numpy as jnp
from jax import lax
from jax.experimental import pallas as pl
from jax.experimental.pallas import tpu as pltpu

f32 = jnp.float32
bf16 = jnp.bfloat16
SDS = jax.ShapeDtypeStruct
HIGHEST = lax.Precision.HIGHEST
MESH = pl.DeviceIdType.MESH

NN = (((1,), (0,)), ((), ()))
NT = (((1,), (1,)), ((), ()))
TN = (((0,), (0,)), ((), ()))

D_MODEL = 1024
N_GROUPS = 3
DILATIONS = (1, 4, 16)
HEAD_DIM = 64
ATTN_BLOCK = 128
QKV_G = 1536
ATTN_OUT = 512
HGRN_W = 512
HGRN_CHUNK = 32
D_FF = 2816
NUM_BUCKETS = 32
MAX_EXACT = 16
MAX_DISTANCE = 2048
NEG_INF = -1e30
EPS = 1e-6
LANE = 128
SUBLANE = 8
VMEM_BIG = 48 * 1024 * 1024
MM_ROWS = 512
MM_OUT_BYTES = 8 * 1024 * 1024

ADAM_LR, ADAM_B1, ADAM_B2, ADAM_EPS, ADAM_WD, ADAM_STEP = 0.001, 0.9, 0.999, 1e-08, 0.01, 10


def _pick(n, pref):
    t = pref
    while t >= LANE:
        if n % t == 0:
            return t
        t //= 2
    return n


def _cparams(sem=None, vmem=None):
    kw = {}
    if sem is not None:
        kw["dimension_semantics"] = sem
    if vmem is not None:
        kw["vmem_limit_bytes"] = vmem
    return pltpu.CompilerParams(**kw)


def _sigmoid(x):
    return jax.nn.sigmoid(x)


def _colsum8(x):
    return x.reshape(x.shape[0] // SUBLANE, SUBLANE, x.shape[1]).sum(axis=0)


def _mm(a, b, mode, out_dtype, name, acc=None, after=None):
    dims = {"nn": NN, "nt": NT, "tn": TN}[mode]
    has_acc = acc is not None
    parts = list(a) if isinstance(a, (list, tuple)) else [a]
    if mode == "tn":
        assert not has_acc
        K, N = b.shape
        widths = [t.shape[1] for t in parts]
        M = sum(widths)
        whole = M * N * 4 <= MM_OUT_BYTES
        assert whole or len(parts) == 1
        tmm = M if whole else M // 2
        ts = _pick(K, 4 * MM_ROWS)
        nk = K // ts

        npart = len(parts)
        narrow = out_dtype != f32

        def body_tn(*refs):
            b_ref, o_ref = refs[npart], refs[npart + 1]
            acc_ref = refs[npart + 2] if narrow else o_ref
            k = pl.program_id(1)
            bv = b_ref[...]
            lo = 0
            for a_ref, w in zip(refs[:npart], widths if whole else [tmm]):
                part = lax.dot_general(a_ref[...], bv, dims, preferred_element_type=f32)
                rows = slice(lo, lo + w)
                lo += w

                @pl.when(k == 0)
                def _(part=part, rows=rows):
                    acc_ref[rows, :] = part

                @pl.when(k > 0)
                def _(part=part, rows=rows):
                    acc_ref[rows, :] += part

            if narrow:
                @pl.when(k == nk - 1)
                def _():
                    o_ref[...] = acc_ref[...].astype(out_dtype)

        return pl.pallas_call(
            body_tn,
            grid=(M // tmm, nk),
            in_specs=[pl.BlockSpec((ts, w if whole else tmm), lambda i, k: (k, i)) for w in widths]
            + [pl.BlockSpec((ts, N), lambda i, k: (k, 0))],
            out_specs=pl.BlockSpec((tmm, N), lambda i, k: (i, 0)),
            out_shape=SDS((M, N), out_dtype),
            scratch_shapes=[pltpu.VMEM((tmm, N), f32)] if narrow else [],
            compiler_params=_cparams(("parallel", "arbitrary"), VMEM_BIG),
            name=name,
        )(*parts, b)

    bs = list(b) if isinstance(b, (list, tuple)) else [b]
    widths = [t.shape[1] for t in parts]
    M = parts[0].shape[0]
    kdim = 0 if mode == "nn" else 1
    N = bs[0].shape[1 - kdim]
    tm = _pick(M, MM_ROWS)
    npart, nb = len(parts), len(bs)
    place, bi, lo = [], 0, 0
    for w in widths:
        place.append((bi, lo))
        lo += w
        if lo == bs[bi].shape[kdim]:
            bi, lo = bi + 1, 0
    assert bi == nb and lo == 0

    def body(*refs):
        a_refs, b_refs = refs[:npart], refs[npart:npart + nb]
        c_ref = refs[npart + nb] if has_acc else None
        o_ref = refs[-1]
        part = None
        for a_ref, w, (bi, lo) in zip(a_refs, widths, place):
            b_ref = b_refs[bi]
            if w == bs[bi].shape[kdim]:
                bk = b_ref[...]
            else:
                bk = b_ref[:, lo:lo + w] if mode == "nt" else b_ref[lo:lo + w, :]
            t = lax.dot_general(a_ref[...], bk, dims, preferred_element_type=f32)
            part = t if part is None else part + t
        if has_acc:
            part = part + c_ref[...]
        o_ref[...] = part.astype(out_dtype)

    specs = [pl.BlockSpec((tm, w), lambda i: (i, 0)) for w in widths] \
        + [pl.BlockSpec(t.shape, lambda i: (0, 0)) for t in bs]
    args = parts + bs
    aliases = {}
    if has_acc:
        specs.append(pl.BlockSpec((tm, N), lambda i: (i, 0)))
        args.append(acc)
        aliases = {npart + nb: 0}
    if after is not None:
        specs.append(pl.BlockSpec(memory_space=pl.ANY))
        args.append(after)
    return pl.pallas_call(
        body,
        grid=(M // tm,),
        in_specs=specs,
        out_specs=pl.BlockSpec((tm, N), lambda i: (i, 0)),
        out_shape=SDS((M, N), out_dtype),
        input_output_aliases=aliases,
        compiler_params=_cparams(("parallel",), VMEM_BIG),
        name=name,
    )(*args)


PERM_ROWS = 1024


def _perm_spec(d, cols=LANE):
    return pl.BlockSpec((d, PERM_ROWS // d, cols), lambda i, j: (0, i, j))


def _to_natural(src_ref, dst_ref, d):
    n = src_ref.shape[1]
    for r in range(d):
        dst_ref[pl.ds(r, n, stride=d), :] = src_ref[r]


def _prep(x, w, after=None):
    S, D = x.shape
    R = PERM_ROWS
    nc = D // LANE
    n_in = nc + 1 + (after is not None)

    def body(*refs):
        x_refs, w_ref = refs[:nc], refs[nc]
        h_ref, h4_ref, h16_ref, rs = refs[n_in:]
        ssq = None
        for xr in x_refs:
            v = xr[...]
            t = jnp.sum(v * v, axis=-1, keepdims=True)
            ssq = t if ssq is None else ssq + t
        rinv = lax.rsqrt(ssq * (1.0 / D) + EPS)
        rs[...] = jnp.broadcast_to(rinv, (R, LANE))
        for j, xr in enumerate(x_refs):
            cols = slice(j * LANE, (j + 1) * LANE)
            wj = w_ref[:, cols]
            h_ref[:, cols] = ((xr[...] * rinv) * wj).astype(bf16)
            for d, o_ref in ((4, h4_ref), (16, h16_ref)):
                n = R // d
                for r in range(d):
                    rows = pl.ds(r, n, stride=d)
                    o_ref[r, :, cols] = ((xr[rows, :] * rs[rows, :]) * wj).astype(bf16)

    col = lambda j: pl.BlockSpec((R, LANE), lambda i, j=j: (i, j))
    h, h4, h16 = pl.pallas_call(
        body,
        grid=(S // R,),
        in_specs=[col(j) for j in range(nc)] + [pl.BlockSpec((1, D), lambda i: (0, 0))]
        + ([] if after is None else [pl.BlockSpec(memory_space=pl.ANY)]),
        out_specs=[pl.BlockSpec((R, D), lambda i: (i, 0)), pl.BlockSpec((4, R // 4, D), lambda i: (0, i, 0)),
                   pl.BlockSpec((16, R // 16, D), lambda i: (0, i, 0))],
        out_shape=[SDS((S, D), bf16), SDS((4, S // 4, D), bf16), SDS((16, S // 16, D), bf16)],
        scratch_shapes=[pltpu.VMEM((R, LANE), f32)],
        compiler_params=_cparams(("parallel",), VMEM_BIG),
        name="prep_norm_perm",
    )(*([x] * nc), w, *([] if after is None else [after]))
    return [h, h4.reshape(S, D), h16.reshape(S, D)]


def _dh_sum(a, b, c):
    S, D = a.shape
    R = PERM_ROWS

    def body(a_ref, b_ref, c_ref, o_ref, sb, sc):
        _to_natural(b_ref, sb, 4)
        _to_natural(c_ref, sc, 16)
        o_ref[...] = (a_ref[...] + sb[...]) + sc[...]

    nat = pl.BlockSpec((R, LANE), lambda i, j: (i, j))
    return pl.pallas_call(
        body,
        grid=(S // R, D // LANE),
        in_specs=[nat, _perm_spec(4), _perm_spec(16)],
        out_specs=nat,
        out_shape=SDS((S, D), f32),
        scratch_shapes=[pltpu.VMEM((R, LANE), f32)] * 2,
        compiler_params=_cparams(("parallel", "parallel")),
        name="dh_sum",
    )(a, b.reshape(4, S // 4, D), c.reshape(16, S // 16, D))


def _rms_parts(xv):
    r = lax.rsqrt(jnp.mean(xv * xv, axis=-1, keepdims=True) + EPS)
    return r, xv * r


def _rms_bwd(xhat, r, w, dy):
    dyw = dy * w
    return r * (dyw - xhat * jnp.mean(dyw * xhat, axis=-1, keepdims=True))


def _mid_fwd(x, merged, w_out, w_pm, w_pf):
    S, D = x.shape
    tm = _pick(S, MM_ROWS)

    def body(x_ref, m_ref, wo_ref, wpm_ref, wpf_ref, mo_ref, x1_ref, h2_ref):
        mo = jnp.dot(m_ref[...], wo_ref[...], preferred_element_type=f32)
        mo_ref[...] = mo
        _, moh = _rms_parts(mo)
        x1 = x_ref[...] + moh * wpm_ref[...]
        x1_ref[...] = x1
        _, x1h = _rms_parts(x1)
        h2_ref[...] = (x1h * wpf_ref[...]).astype(bf16)

    row = pl.BlockSpec((tm, D), lambda i: (i, 0))
    vec = pl.BlockSpec((1, D), lambda i: (0, 0))
    return pl.pallas_call(
        body,
        grid=(S // tm,),
        in_specs=[row, pl.BlockSpec((tm, merged.shape[1]), lambda i: (i, 0)),
                  pl.BlockSpec(w_out.shape, lambda i: (0, 0)), vec, vec],
        out_specs=[row, row, row],
        out_shape=[SDS((S, D), f32), SDS((S, D), f32), SDS((S, D), bf16)],
        compiler_params=_cparams(("parallel",), VMEM_BIG),
        name="out_proj_mid_fwd",
    )(x, merged, w_out, w_pm, w_pf)


def _final(x1, act, w_down, tgt, w_pfn):
    S, D = x1.shape
    tm = _pick(S, MM_ROWS)
    nt = S // tm

    def body(x1_ref, a_ref, wd_ref, t_ref, w_ref, loss_ref, dy_ref, dfo_ref, gw_ref, lacc, gacc):
        i = pl.program_id(0)

        @pl.when(i == 0)
        def _():
            lacc[...] = jnp.zeros_like(lacc)
            gacc[...] = jnp.zeros_like(gacc)

        w = w_ref[...]
        r, foh = _rms_parts(jnp.dot(a_ref[...], wd_ref[...], preferred_element_type=f32))
        y = x1_ref[...] + foh * w
        err = y - t_ref[...]
        lacc[...] += _colsum8(err * err)
        dy = err * (1.0 / D)
        dy_ref[...] = dy
        gacc[...] += _colsum8(dy * foh)
        dfo_ref[...] = _rms_bwd(foh, r, w, dy).astype(bf16)

        @pl.when(i == nt - 1)
        def _():
            loss_ref[...] = jnp.full((SUBLANE, LANE), 0.5 / D, f32) * jnp.sum(lacc[...])
            gw_ref[...] = jnp.sum(gacc[...], axis=0, keepdims=True)

    row = pl.BlockSpec((tm, D), lambda i: (i, 0))
    vec = pl.BlockSpec((1, D), lambda i: (0, 0))
    return pl.pallas_call(
        body,
        grid=(nt,),
        in_specs=[row, pl.BlockSpec((tm, act.shape[1]), lambda i: (i, 0)),
                  pl.BlockSpec(w_down.shape, lambda i: (0, 0)), row, vec],
        out_specs=[pl.BlockSpec((SUBLANE, LANE), lambda i: (0, 0)), row, row, vec],
        out_shape=[SDS((SUBLANE, LANE), f32), SDS((S, D), f32), SDS((S, D), bf16), SDS((1, D), f32)],
        scratch_shapes=[pltpu.VMEM((SUBLANE, D), f32), pltpu.VMEM((SUBLANE, D), f32)],
        compiler_params=_cparams(("arbitrary",), VMEM_BIG),
        name="down_proj_final_loss",
    )(x1, act, w_down, tgt, w_pfn)


def _mid_bwd(dy, dh2, x1, mo, w_pf, w_pm):
    S, D = dy.shape
    tm = _pick(S, 512)
    nt = S // tm

    def body(dy_ref, dh2_ref, x1_ref, mo_ref, wpf_ref, wpm_ref, dx1_ref, dmo_ref, gpf_ref, gpm_ref, apf, apm):
        i = pl.program_id(0)

        @pl.when(i == 0)
        def _():
            apf[...] = jnp.zeros_like(apf)
            apm[...] = jnp.zeros_like(apm)

        r1, x1h = _rms_parts(x1_ref[...])
        dh2 = dh2_ref[...]
        apf[...] += _colsum8(dh2 * x1h)
        dx1 = dy_ref[...] + _rms_bwd(x1h, r1, wpf_ref[...], dh2)
        dx1_ref[...] = dx1
        rm, moh = _rms_parts(mo_ref[...])
        apm[...] += _colsum8(dx1 * moh)
        dmo_ref[...] = _rms_bwd(moh, rm, wpm_ref[...], dx1).astype(bf16)

        @pl.when(i == nt - 1)
        def _():
            gpf_ref[...] = jnp.sum(apf[...], axis=0, keepdims=True)
            gpm_ref[...] = jnp.sum(apm[...], axis=0, keepdims=True)

    row = pl.BlockSpec((tm, D), lambda i: (i, 0))
    vec = pl.BlockSpec((1, D), lambda i: (0, 0))
    return pl.pallas_call(
        body,
        grid=(nt,),
        in_specs=[row, row, row, row, vec, vec],
        out_specs=[row, row, vec, vec],
        out_shape=[SDS((S, D), f32), SDS((S, D), bf16), SDS((1, D), f32), SDS((1, D), f32)],
        scratch_shapes=[pltpu.VMEM((SUBLANE, D), f32), pltpu.VMEM((SUBLANE, D), f32)],
        compiler_params=_cparams(("arbitrary",)),
        name="mid_bwd",
    )(dy, dh2, x1, mo, w_pf, w_pm)


def _first_bwd(x, dx1, dh, w_pre):
    S, D = x.shape
    tm = _pick(S, 512)
    nt = S // tm

    def body(x_ref, dx1_ref, a_ref, w_ref, gx_ref, gw_ref, acc):
        i = pl.program_id(0)

        @pl.when(i == 0)
        def _():
            acc[...] = jnp.zeros_like(acc)

        r, xh = _rms_parts(x_ref[...])
        dh = a_ref[...]
        acc[...] += _colsum8(dh * xh)
        gx_ref[...] = dx1_ref[...] + _rms_bwd(xh, r, w_ref[...], dh)

        @pl.when(i == nt - 1)
        def _():
            gw_ref[...] = jnp.sum(acc[...], axis=0, keepdims=True)

    row = pl.BlockSpec((tm, D), lambda i: (i, 0))
    vec = pl.BlockSpec((1, D), lambda i: (0, 0))
    return pl.pallas_call(
        body,
        grid=(nt,),
        in_specs=[row, row, row, vec],
        out_specs=[row, vec],
        out_shape=[SDS((S, D), f32), SDS((1, D), f32)],
        scratch_shapes=[pltpu.VMEM((SUBLANE, D), f32)],
        compiler_params=_cparams(("arbitrary",)),
        name="first_bwd",
    )(x, dx1, dh, w_pre)


def _t5_bucket(dist):
    n = jnp.maximum(dist, 0)
    nf = jnp.maximum(n, 1).astype(f32)
    large = MAX_EXACT + (jnp.log(nf / MAX_EXACT) / math.log(MAX_DISTANCE / MAX_EXACT)
                         * (NUM_BUCKETS - MAX_EXACT)).astype(jnp.int32)
    large = jnp.minimum(large, NUM_BUCKETS - 1)
    return jnp.where(n < MAX_EXACT, n, large)


def _bias_consts(d):
    blk = ATTN_BLOCK
    rel = jnp.arange(blk)[:, None] + blk - jnp.arange(2 * blk)[None, :]
    in_win = (rel >= 0) & (rel <= blk)
    bucket = _t5_bucket(rel * d).reshape(1, -1)
    onehot = (bucket == jnp.arange(NUM_BUCKETS)[:, None]).astype(f32)
    return onehot, in_win.astype(f32).reshape(1, -1)


def _bias_build(tab_t, onehot, maskf, name, after):
    H = tab_t.shape[0]

    def body(t_ref, oh_ref, m_ref, after_ref, o_ref):
        b = jnp.dot(t_ref[...], oh_ref[...], precision=HIGHEST, preferred_element_type=f32)
        o_ref[...] = jnp.where(m_ref[...] > 0.5, b, NEG_INF)

    vm = pl.BlockSpec(memory_space=pltpu.VMEM)
    return pl.pallas_call(body, out_shape=SDS((H, onehot.shape[1]), f32), name=name,
                          in_specs=[vm, vm, vm, pl.BlockSpec(memory_space=pl.ANY)], out_specs=vm,
                          )(tab_t, onehot, maskf, after)


def _bias_grad(dbias_flat, onehot, name):
    H = dbias_flat.shape[0]

    def body(g_ref, oh_ref, o_ref):
        o_ref[...] = lax.dot_general(oh_ref[...], g_ref[...], NT, precision=HIGHEST, preferred_element_type=f32)

    return pl.pallas_call(body, out_shape=SDS((NUM_BUCKETS, H), f32), name=name)(dbias_flat, onehot)


ATTN_TILE = 512
ATTN_SUB = ATTN_TILE // ATTN_BLOCK
ATTN_HP = 4
ATTN_WIDE = ATTN_HP * LANE


def _qkv_specs(nt):
    tile = (ATTN_TILE, ATTN_WIDE)
    blk = (ATTN_BLOCK, ATTN_WIDE)
    sec = ATTN_OUT // ATTN_WIDE
    cur = lambda off: (lambda h, t: (jnp.minimum(t, nt - 1), off + h))
    prev = lambda off: (lambda h, t: (jnp.maximum(jnp.minimum(t, nt - 1) * ATTN_SUB - 1, 0), off + h))
    return [pl.BlockSpec(tile, cur(0)), pl.BlockSpec(blk, prev(sec)), pl.BlockSpec(tile, cur(sec)),
            pl.BlockSpec(blk, prev(2 * sec)), pl.BlockSpec(tile, cur(2 * sec))]


def _head_masks():
    lane = lax.broadcasted_iota(jnp.int32, (ATTN_BLOCK, LANE), 1)
    return lane < HEAD_DIM


def _stack_heads(x2, low):
    zero = jnp.zeros_like(x2)
    return jnp.concatenate([jnp.where(low, x2, zero), jnp.where(low, zero, x2)], axis=0)


def _attn_fwd(qkv, bias, bps, name, after=None):
    S = qkv.shape[0]
    nt = S // ATTN_TILE
    scale = HEAD_DIM ** -0.5

    def body(q_ref, kp_ref, kc_ref, vp_ref, vc_ref, b_ref, *rest):
        o_ref, l_ref = rest[-2:]
        t = pl.program_id(1)
        low = _head_masks()
        col = lax.broadcasted_iota(jnp.int32, (2 * ATTN_BLOCK, 2 * ATTN_BLOCK), 1)
        for hp in range(ATTN_HP):
            cols = slice(hp * LANE, (hp + 1) * LANE)
            kk = jnp.concatenate([kp_ref[:, cols], kc_ref[:, cols]], axis=0)
            vv = jnp.concatenate([vp_ref[:, cols], vc_ref[:, cols]], axis=0)
            bias2 = b_ref[2 * hp:2 * hp + 2].reshape(2 * ATTN_BLOCK, 2 * ATTN_BLOCK)
            for b in range(ATTN_SUB):
                lo = b * ATTN_BLOCK
                rows = slice(lo, lo + ATTN_BLOCK)
                keys = slice(lo, lo + 2 * ATTN_BLOCK)
                dead = jnp.logical_and((t * ATTN_SUB + b) % bps == 0, col < ATTN_BLOCK)
                q2 = _stack_heads(q_ref[rows, cols], low)
                kb, vb = kk[keys], vv[keys]
                s = lax.dot_general(q2, kb, NT, preferred_element_type=f32) * scale + bias2
                s = jnp.where(dead, NEG_INF, s)
                m = jnp.max(s, axis=-1, keepdims=True)
                p = jnp.exp(s - m)
                l = jnp.sum(p, axis=-1, keepdims=True)
                o2 = jnp.dot(p.astype(bf16), vb, preferred_element_type=f32) / l
                lse = m + jnp.log(l)
                o_ref[rows, cols] = jnp.where(low, o2[:ATTN_BLOCK], o2[ATTN_BLOCK:])
                l_ref[rows, cols] = jnp.where(low, lse[:ATTN_BLOCK], lse[ATTN_BLOCK:])

    tile = pl.BlockSpec((ATTN_TILE, ATTN_WIDE), lambda h, t: (t, h))
    return pl.pallas_call(
        body,
        grid=(4 // ATTN_HP, nt),
        in_specs=_qkv_specs(nt) + [pl.BlockSpec((2 * ATTN_HP, ATTN_BLOCK, 2 * ATTN_BLOCK), lambda h, t: (h, 0, 0))]
        + ([] if after is None else [pl.BlockSpec(memory_space=pl.ANY)]),
        out_specs=[tile, tile],
        out_shape=[SDS((S, ATTN_OUT), f32), SDS((S, ATTN_OUT), f32)],
        compiler_params=_cparams(("parallel", "parallel")),
        name=name,
    )(qkv, qkv, qkv, qkv, qkv, bias, *([] if after is None else [after]))


def _attn_bwd(qkv, bias, do, dvec, lse, bps, name):
    S = qkv.shape[0]
    nt = S // ATTN_TILE
    scale = HEAD_DIM ** -0.5

    def assemble(parts):
        rows = [parts[0][:ATTN_BLOCK]]
        for b in range(ATTN_SUB - 1):
            rows.append(parts[b][ATTN_BLOCK:] + parts[b + 1][:ATTN_BLOCK])
        rows.append(parts[-1][ATTN_BLOCK:])
        return rows

    def body(q_ref, kp_ref, kc_ref, vp_ref, vc_ref, b_ref, do_ref, dvec_ref, lse_ref,
             dq_ref, dk_ref, dv_ref, db_ref, ck, cv):
        t = pl.program_id(1)
        last = ATTN_TILE - ATTN_BLOCK

        @pl.when(t == 0)
        def _():
            ck[...] = jnp.zeros_like(ck)
            cv[...] = jnp.zeros_like(cv)
            db_ref[...] = jnp.zeros_like(db_ref)

        @pl.when(t < nt)
        def _():
            low = _head_masks()
            col = lax.broadcasted_iota(jnp.int32, (2 * ATTN_BLOCK, 2 * ATTN_BLOCK), 1)
            per_row = lambda t2: jnp.concatenate([t2[:, 0:1], t2[:, HEAD_DIM:HEAD_DIM + 1]], axis=0)
            for hp in range(ATTN_HP):
                cols = slice(hp * LANE, (hp + 1) * LANE)
                kk = jnp.concatenate([kp_ref[:, cols], kc_ref[:, cols]], axis=0)
                vv = jnp.concatenate([vp_ref[:, cols], vc_ref[:, cols]], axis=0)
                bias2 = b_ref[2 * hp:2 * hp + 2].reshape(2 * ATTN_BLOCK, 2 * ATTN_BLOCK)
                dk_parts, dv_parts = [], []
                dsum = None
                for b in range(ATTN_SUB):
                    lo = b * ATTN_BLOCK
                    rows = slice(lo, lo + ATTN_BLOCK)
                    keys = slice(lo, lo + 2 * ATTN_BLOCK)
                    dead = jnp.logical_and((t * ATTN_SUB + b) % bps == 0, col < ATTN_BLOCK)
                    q2 = _stack_heads(q_ref[rows, cols], low)
                    do2 = _stack_heads(do_ref[rows, cols].astype(bf16), low)
                    kb, vb = kk[keys], vv[keys]
                    s = lax.dot_general(q2, kb, NT, preferred_element_type=f32) * scale + bias2
                    s = jnp.where(dead, NEG_INF, s)
                    p = jnp.exp(s - per_row(lse_ref[rows, cols]))
                    dp = lax.dot_general(do2, vb, NT, preferred_element_type=f32)
                    ds = p * (dp - per_row(dvec_ref[rows, cols]))
                    dsum = ds if dsum is None else dsum + ds
                    dsb = ds.astype(bf16)
                    dq2 = jnp.dot(dsb, kb, preferred_element_type=f32) * scale
                    dq_ref[rows, cols] = jnp.where(low, dq2[:ATTN_BLOCK], dq2[ATTN_BLOCK:]).astype(bf16)
                    dk_parts.append(lax.dot_general(dsb, q2, TN, preferred_element_type=f32) * scale)
                    dv_parts.append(lax.dot_general(p.astype(bf16), do2, TN, preferred_element_type=f32))
                db_ref[2 * hp:2 * hp + 2] += dsum.reshape(2, ATTN_BLOCK, 2 * ATTN_BLOCK)
                for parts, carry, out_ref in ((dk_parts, ck, dk_ref), (dv_parts, cv, dv_ref)):
                    rws = assemble(parts)
                    out_ref[:last, cols] = carry[:last, cols].astype(bf16)
                    out_ref[last:, cols] = (carry[last:, cols] + rws[0]).astype(bf16)
                    for b in range(ATTN_SUB):
                        carry[b * ATTN_BLOCK:(b + 1) * ATTN_BLOCK, cols] = rws[b + 1]

        @pl.when(t == nt)
        def _():
            dk_ref[...] = ck[...].astype(bf16)
            dv_ref[...] = cv[...].astype(bf16)

    tile = (ATTN_TILE, ATTN_WIDE)
    cur = pl.BlockSpec(tile, lambda h, t: (jnp.minimum(t, nt - 1), h))
    lag = pl.BlockSpec(tile, lambda h, t: (jnp.maximum(t - 1, 0), h))
    bspec = pl.BlockSpec((2 * ATTN_HP, ATTN_BLOCK, 2 * ATTN_BLOCK), lambda h, t: (h, 0, 0))
    return pl.pallas_call(
        body,
        grid=(4 // ATTN_HP, nt + 1),
        in_specs=_qkv_specs(nt) + [bspec, cur, cur, cur],
        out_specs=[cur, lag, lag, bspec],
        out_shape=[SDS((S, ATTN_OUT), bf16), SDS((S, ATTN_OUT), bf16), SDS((S, ATTN_OUT), bf16),
                   SDS((8, ATTN_BLOCK, 2 * ATTN_BLOCK), f32)],
        scratch_shapes=[pltpu.VMEM(tile, f32), pltpu.VMEM(tile, f32)],
        compiler_params=_cparams(("parallel", "arbitrary")),
        name=name,
    )(qkv, qkv, qkv, qkv, qkv, bias, do, dvec, lse)


def _attn_merge(o0, o1, o2, l0, l1, l2):
    S, W = o0.shape
    R = PERM_ROWS

    def body(o0_ref, o1_ref, o2_ref, l0_ref, l1_ref, l2_ref, y_ref, yb_ref, w0_ref, w1_ref, w2_ref,
             so1, so2, sl1, sl2):
        _to_natural(o1_ref, so1, 4)
        _to_natural(l1_ref, sl1, 4)
        _to_natural(o2_ref, so2, 16)
        _to_natural(l2_ref, sl2, 16)
        a, b, c = l0_ref[...], sl1[...], sl2[...]
        m = jnp.maximum(jnp.maximum(a, b), c)
        ea, eb, ec = jnp.exp(a - m), jnp.exp(b - m), jnp.exp(c - m)
        den = (ea + eb) + ec
        w0, w1, w2 = ea / den, eb / den, ec / den
        y = (w0 * o0_ref[...] + w1 * so1[...]) + w2 * so2[...]
        y_ref[...] = y
        yb_ref[...] = y.astype(bf16)
        w0_ref[...] = w0
        w1_ref[...] = w1
        w2_ref[...] = w2

    nat = pl.BlockSpec((R, LANE), lambda i, j: (i, j))
    v4 = lambda t: t.reshape(4, S // 4, W)
    v16 = lambda t: t.reshape(16, S // 16, W)
    return pl.pallas_call(
        body,
        grid=(S // R, W // LANE),
        in_specs=[nat, _perm_spec(4), _perm_spec(16)] * 2,
        out_specs=[nat] * 5,
        out_shape=[SDS((S, W), f32), SDS((S, W), bf16)] + [SDS((S, W), f32)] * 3,
        scratch_shapes=[pltpu.VMEM((R, LANE), f32)] * 4,
        compiler_params=_cparams(("parallel", "parallel")),
        name="attn_merge",
    )(o0, v4(o1), v16(o2), l0, v4(l1), v16(l2))


def _attn_merge_bwd(dy, y, w0, w1, w2, after=None):
    S, W = dy.shape
    R = PERM_ROWS

    def body(dy_ref, y_ref, w0_ref, w1_ref, w2_ref, *rest):
        a0, a1, a2, b0, b1, b2, sa, sb = rest[-8:]
        dyv = dy_ref[...]
        r = lax.broadcasted_iota(jnp.int32, (LANE, LANE), 0) // HEAD_DIM
        c = lax.broadcasted_iota(jnp.int32, (LANE, LANE), 1) // HEAD_DIM
        seg = jnp.where(r == c, 1.0, 0.0).astype(f32)
        cbar = jnp.dot(dyv * y_ref[...], seg, precision=HIGHEST, preferred_element_type=f32)
        w = w0_ref[...]
        a0[...] = (w * dyv).astype(bf16)
        b0[...] = w * cbar
        for d, w_ref, a_ref, b_ref in ((4, w1_ref, a1, b1), (16, w2_ref, a2, b2)):
            w = w_ref[...]
            sa[...] = w * dyv
            sb[...] = w * cbar
            n = R // d
            for k in range(d):
                rows = pl.ds(k, n, stride=d)
                a_ref[k] = sa[rows, :].astype(bf16)
                b_ref[k] = sb[rows, :]

    nat = pl.BlockSpec((R, LANE), lambda i, j: (i, j))
    shapes = lambda dt: [SDS((S, W), dt), SDS((4, S // 4, W), dt), SDS((16, S // 16, W), dt)]
    outs = pl.pallas_call(
        body,
        grid=(S // R, W // LANE),
        in_specs=[nat] * 5 + ([] if after is None else [pl.BlockSpec(memory_space=pl.ANY)]),
        out_specs=[nat, _perm_spec(4), _perm_spec(16)] * 2,
        out_shape=shapes(bf16) + shapes(f32),
        scratch_shapes=[pltpu.VMEM((R, LANE), f32)] * 2,
        compiler_params=_cparams(("parallel", "parallel")),
        name="attn_merge_bwd",
    )(dy, y, w0, w1, w2, *([] if after is None else [after]))
    return [t.reshape(S, W) for t in outs]


HGRN_SB = 256
HGRN_PAIR = 4


def _chunk_masks():
    r = jnp.arange(HGRN_SB)[:, None]
    c = jnp.arange(HGRN_SB)[None, :]
    same = (r // HGRN_CHUNK) == (c // HGRN_CHUNK)
    return jnp.stack([same & (c <= r), same, same & (c >= r)]).astype(bf16)


def _mask_dot(mask, x):
    hi = x.astype(bf16)
    r1 = x - hi.astype(f32)
    mid = r1.astype(bf16)
    lo = (r1 - mid.astype(f32)).astype(bf16)
    p = jnp.dot(mask, jnp.concatenate([hi, mid, lo], axis=1), preferred_element_type=f32)
    n = x.shape[1]
    return (p[:, :n] + p[:, n:2 * n]) + p[:, 2 * n:]


def _hgrn_prep(q_raw, f_raw, lbv, tril, same):
    sq = _sigmoid(q_raw)
    qs = q_raw * sq
    sig = _sigmoid(f_raw)
    f = lbv + (1.0 - lbv) * sig
    g = jnp.log(f)
    k = 1.0 - f
    G = _mask_dot(tril, g)
    GL = _mask_dot(same, g)
    eG = jnp.exp(G)
    einv = jnp.exp(-G)
    edec = jnp.exp(GL - G)
    return dict(sq=sq, qs=qs, sig=sig, f=f, k=k, eG=eG, einv=einv, edec=edec, eGL=jnp.exp(GL),
                qt=qs * eG, kt=k * einv, kd=k * edec)


def _ride_split(ride, rest, n_out, n_scratch):
    if ride is None:
        return None, rest[:n_out], None, rest[n_out:], None
    return rest[0], rest[1:1 + n_out], rest[1 + n_out], rest[2 + n_out:2 + n_out + n_scratch], rest[2 + n_out + n_scratch:]


def _hgrn_fwd(hg, lb, normw, ride=None):
    S = hg.shape[0]
    sb = HGRN_SB
    nsb = S // sb
    nch = sb // HGRN_CHUNK

    def body(q_ref, f_ref, v_ref, og_ref, lb_ref, nw_ref, m_ref, *rest):
        src_ref, (y_ref, o_ref, ck_ref), got_ref, (st,), sems = _ride_split(ride, rest, 3, 1)
        j = pl.program_id(1)
        if ride is not None:
            @pl.when(jnp.logical_and(pl.program_id(0) == 0, j == 0))
            def _():
                _chip_start(src_ref, got_ref, sems[0], sems[1], ride[1])

        @pl.when(j == 0)
        def _():
            st[...] = jnp.zeros_like(st)

        tril_m = m_ref[0]
        tril = tril_m.astype(f32) > 0.5

        def one_head(hh):
            cols = slice(hh * LANE, (hh + 1) * LANE)
            ST = st[hh]
            ck_ref[hh, 0] = ST
            pr = _hgrn_prep(q_ref[:, cols], f_ref[:, cols], lb_ref[:, cols], tril_m, m_ref[1])
            qtb, ktb, kdb = pr["qt"].astype(bf16), pr["kt"].astype(bf16), pr["kd"].astype(bf16)
            eGL = pr["eGL"]
            vb = v_ref[:, cols].astype(bf16)
            A = jnp.where(tril, lax.dot_general(qtb, ktb, NT, preferred_element_type=f32), 0.0)
            o = jnp.dot(A.astype(bf16), vb, preferred_element_type=f32)
            outs = []
            for ci in range(nch):
                lo = ci * HGRN_CHUNK
                sl = slice(lo, lo + HGRN_CHUNK)
                outs.append(o[sl] + lax.dot_general(qtb[sl], ST.astype(bf16), NT, preferred_element_type=f32))
                ST = ST * eGL[lo:lo + 1, :] + lax.dot_general(vb[sl], kdb[sl], TN, preferred_element_type=f32)
            st[hh] = ST
            of = jnp.concatenate(outs, axis=0)
            o_ref[:, cols] = of
            rms = lax.rsqrt(jnp.mean(of * of, axis=-1, keepdims=True) + EPS)
            ogv = og_ref[:, cols]
            y_ref[:, cols] = ((of * rms * nw_ref[...]) * (ogv * _sigmoid(ogv))).astype(bf16)

        for hh in range(HGRN_PAIR):
            one_head(hh)

        if ride is not None:
            @pl.when(jnp.logical_and(pl.program_id(0) == ngrp - 1, j == nsb - 1))
            def _():
                _chip_finish(src_ref, got_ref, sems[0], sems[1], ride[1])

    wide = HGRN_PAIR * LANE
    ngrp = 4 // HGRN_PAIR
    col = lambda off: pl.BlockSpec((sb, wide), lambda h, j: (j, off // HGRN_PAIR + h))
    riding = ride is not None
    res = pl.pallas_call(
        body,
        grid=(ngrp, nsb),
        in_specs=[col(0), col(4), col(8), col(12), pl.BlockSpec((1, wide), lambda h, j: (0, h)),
                  pl.BlockSpec((1, LANE), lambda h, j: (0, 0)),
                  pl.BlockSpec((3, sb, sb), lambda h, j: (0, 0, 0))] + ([_ANY] if riding else []),
        out_specs=[col(0), col(0), pl.BlockSpec((HGRN_PAIR, 1, LANE, LANE), lambda h, j: (h, j, 0, 0))]
        + ([_ANY] if riding else []),
        out_shape=[SDS((S, HGRN_W), bf16), SDS((S, HGRN_W), f32), SDS((4, nsb, LANE, LANE), f32)]
        + ([_chip_out_shape(*ride)] if riding else []),
        scratch_shapes=[pltpu.VMEM((HGRN_PAIR, LANE, LANE), f32)] + (list(_CHIP_SEMS) if riding else []),
        compiler_params=_cparams(("arbitrary", "arbitrary") if riding else ("parallel", "arbitrary")),
        name="hgrn_fwd",
    )(hg, hg, hg, hg, lb, normw, _chunk_masks(), *([ride[0]] if riding else []))
    return tuple(res) if riding else (*res, None)


def _hgrn_bwd(hg, o_raw, dy, ck, lb, normw, ride=None):
    S = hg.shape[0]
    sb = HGRN_SB
    nsb = S // sb
    nch = sb // HGRN_CHUNK

    def body(q_ref, f_ref, v_ref, og_ref, o_ref, dy_ref, ck_ref, lb_ref, nw_ref, m_ref, *rest):
        src_ref, outs, got_ref, (dst, alb, anw), sems = _ride_split(ride, rest, 6, 3)
        dq_ref, df_ref, dv_ref, dog_ref, glb_ref, gnw_ref = outs
        j = pl.program_id(1)
        if ride is not None:
            @pl.when(jnp.logical_and(pl.program_id(0) == 0, j == 0))
            def _():
                _chip_start(src_ref, got_ref, sems[0], sems[1], ride[1])

        @pl.when(j == 0)
        def _():
            dst[...] = jnp.zeros_like(dst)
            alb[...] = jnp.zeros_like(alb)
            anw[...] = jnp.zeros_like(anw)

        tril_m = m_ref[0]
        tril = tril_m.astype(f32) > 0.5
        nw = nw_ref[...]

        def one_head(hh):
            cols = slice(hh * LANE, (hh + 1) * LANE)
            lbv = lb_ref[:, cols]
            q_raw = q_ref[:, cols]
            pr = _hgrn_prep(q_raw, f_ref[:, cols], lbv, tril_m, m_ref[1])
            qt, kt, kd, eGL = pr["qt"], pr["kt"], pr["kd"], pr["eGL"]
            qtb, ktb, kdb = qt.astype(bf16), kt.astype(bf16), kd.astype(bf16)
            vb = v_ref[:, cols].astype(bf16)

            o = o_ref[:, cols]
            ogv = og_ref[:, cols]
            sog = _sigmoid(ogv)
            rms = lax.rsqrt(jnp.mean(o * o, axis=-1, keepdims=True) + EPS)
            oh = o * rms
            dyv = dy_ref[:, cols]
            dog_ref[:, cols] = (dyv * (oh * nw) * (sog * (1.0 + ogv * (1.0 - sog)))).astype(bf16)
            dohw = dyv * (ogv * sog)
            anw[:, cols] += _colsum8(dohw * oh)
            doh = dohw * nw
            do = rms * (doh - oh * jnp.mean(doh * oh, axis=-1, keepdims=True))
            dob = do.astype(bf16)

            Ab = jnp.where(tril, lax.dot_general(qtb, ktb, NT, preferred_element_type=f32), 0.0).astype(bf16)
            dAb = jnp.where(tril, lax.dot_general(dob, vb, NT, preferred_element_type=f32), 0.0).astype(bf16)
            dv_acc = lax.dot_general(Ab, dob, TN, preferred_element_type=f32)
            dqt = jnp.dot(dAb, ktb, preferred_element_type=f32)
            dkt = lax.dot_general(dAb, qtb, TN, preferred_element_type=f32)

            ST = ck_ref[hh, 0]
            states = []
            for ci in range(nch):
                lo = ci * HGRN_CHUNK
                sl = slice(lo, lo + HGRN_CHUNK)
                states.append(ST)
                ST = ST * eGL[lo:lo + 1, :] + lax.dot_general(vb[sl], kdb[sl], TN, preferred_element_type=f32)

            dST = dst[hh]
            dqt_i, dkd_i, dv_i, deg_i = [None] * nch, [None] * nch, [None] * nch, [None] * nch
            for ci in reversed(range(nch)):
                lo = ci * HGRN_CHUNK
                sl = slice(lo, lo + HGRN_CHUNK)
                ST0 = states[ci]
                dSTb = dST.astype(bf16)
                dv_i[ci] = lax.dot_general(kdb[sl], dSTb, NT, preferred_element_type=f32)
                dqt_i[ci] = jnp.dot(dob[sl], ST0.astype(bf16), preferred_element_type=f32)
                dkd_i[ci] = jnp.dot(vb[sl], dSTb, preferred_element_type=f32)
                deg_i[ci] = jnp.broadcast_to(jnp.sum(dST * ST0, axis=0, keepdims=True), (HGRN_CHUNK, LANE))
                dST = dST * eGL[lo:lo + 1, :] + lax.dot_general(dob[sl], qtb[sl], TN, preferred_element_type=f32)
            dst[hh] = dST

            dqt = dqt + jnp.concatenate(dqt_i, axis=0)
            dkd = jnp.concatenate(dkd_i, axis=0)
            dv_ref[:, cols] = (dv_acc + jnp.concatenate(dv_i, axis=0)).astype(bf16)
            deg = jnp.concatenate(deg_i, axis=0)

            dqs = dqt * pr["eG"]
            dkdkd = dkd * kd
            dG = dqt * qt - dkt * kt - dkdkd
            dk = dkt * pr["einv"] + dkd * pr["edec"]
            dGL = _mask_dot(m_ref[1], dkdkd) + eGL * deg
            dg = _mask_dot(m_ref[2], dG) + dGL
            df = dg / pr["f"] - dk
            sig = pr["sig"]
            df_ref[:, cols] = (df * (1.0 - lbv) * (sig * (1.0 - sig))).astype(bf16)
            alb[:, cols] += _colsum8(df * (1.0 - sig))
            sq = pr["sq"]
            dq_ref[:, cols] = (dqs * (sq * (1.0 + q_raw * (1.0 - sq)))).astype(bf16)

        for hh in range(HGRN_PAIR):
            one_head(hh)

        @pl.when(j == nsb - 1)
        def _():
            glb_ref[...] = jnp.broadcast_to(jnp.sum(alb[...], axis=0, keepdims=True), (SUBLANE, wide))
            gnw_ref[...] = jnp.broadcast_to(jnp.sum(anw[...], axis=0, keepdims=True), (SUBLANE, wide))

        if ride is not None:
            @pl.when(jnp.logical_and(pl.program_id(0) == ngrp - 1, j == nsb - 1))
            def _():
                _chip_finish(src_ref, got_ref, sems[0], sems[1], ride[1])

    wide = HGRN_PAIR * LANE
    ngrp = 4 // HGRN_PAIR
    rev = lambda off: pl.BlockSpec((sb, wide), lambda h, j: (nsb - 1 - j, off // HGRN_PAIR + h))
    stat = pl.BlockSpec((SUBLANE, wide), lambda h, j: (0, h))
    riding = ride is not None
    res = pl.pallas_call(
        body,
        grid=(ngrp, nsb),
        in_specs=[rev(0), rev(4), rev(8), rev(12), rev(0), rev(0),
                  pl.BlockSpec((HGRN_PAIR, 1, LANE, LANE), lambda h, j: (h, nsb - 1 - j, 0, 0)),
                  pl.BlockSpec((1, wide), lambda h, j: (0, h)), pl.BlockSpec((1, LANE), lambda h, j: (0, 0)),
                  pl.BlockSpec((3, sb, sb), lambda h, j: (0, 0, 0))]
        + ([_ANY] if riding else []),
        out_specs=[rev(0), rev(0), rev(0), rev(0), stat, stat] + ([_ANY] if riding else []),
        out_shape=[SDS((S, HGRN_W), bf16)] * 4 + [SDS((SUBLANE, HGRN_W), f32)] * 2
        + ([_chip_out_shape(*ride)] if riding else []),
        scratch_shapes=[pltpu.VMEM((HGRN_PAIR, LANE, LANE), f32), pltpu.VMEM((SUBLANE, wide), f32),
                        pltpu.VMEM((SUBLANE, wide), f32)] + (list(_CHIP_SEMS) if riding else []),
        compiler_params=_cparams(("arbitrary", "arbitrary") if riding else ("parallel", "arbitrary")),
        name="hgrn_bwd",
    )(hg, hg, hg, hg, o_raw, dy, ck, lb, normw, _chunk_masks(), *([ride[0]] if riding else []))
    return tuple(res) if riding else (*res, None)


def _lb_fwd(raw):
    def body(r_ref, o_ref):
        r = r_ref[...]
        m = jnp.max(r, axis=0, keepdims=True)
        e = jnp.exp(r - m)
        o_ref[...] = (e / jnp.sum(e, axis=0, keepdims=True))[0:1]

    return pl.pallas_call(body, out_shape=SDS((1, raw.shape[1]), f32), name="lb_fwd")(raw)


def _lb_bwd(raw, dlb):
    def body(r_ref, d_ref, o_ref):
        r = r_ref[...]
        m = jnp.max(r, axis=0, keepdims=True)
        e = jnp.exp(r - m)
        s = e / jnp.sum(e, axis=0, keepdims=True)
        s0 = s[0:1]
        onehot0 = jnp.where(lax.broadcasted_iota(jnp.int32, r.shape, 0) == 0, 1.0, 0.0)
        o_ref[...] = d_ref[...] * s0 * (onehot0 - s)

    return pl.pallas_call(body, out_shape=SDS(raw.shape, f32), name="lb_bwd")(raw, dlb)


def _gate_fwd(a, b, gc):
    S, D = a.shape
    tm = _pick(S, 512)

    def body(a_ref, b_ref, g0_ref, g1_ref, o_ref):
        s0, s1 = _sigmoid(g0_ref[...].astype(f32)), _sigmoid(g1_ref[...].astype(f32))
        o_ref[...] = (s0 * a_ref[...].astype(f32) + s1 * b_ref[...].astype(f32)).astype(bf16)

    row = pl.BlockSpec((tm, D), lambda i: (i, 0))
    return pl.pallas_call(
        body,
        grid=(S // tm,),
        in_specs=[row, row, row, pl.BlockSpec((tm, D), lambda i: (i, 1))],
        out_specs=row,
        out_shape=SDS((S, D), bf16),
        compiler_params=_cparams(("parallel",)),
        name="gate_fwd",
    )(a, b, gc, gc)


def _gate_bwd(dm, a, b, gc):
    S, D = a.shape
    tm = _pick(S, 512)

    def body(dm_ref, a_ref, b_ref, g0_ref, g1_ref, da_ref, db_ref, dg_ref):
        dmv = dm_ref[...].astype(f32)
        s0, s1 = _sigmoid(g0_ref[...].astype(f32)), _sigmoid(g1_ref[...].astype(f32))
        da_ref[...] = (dmv * s0).astype(bf16)
        db_ref[...] = (dmv * s1).astype(bf16)
        dg_ref[:, :D] = (dmv * a_ref[...].astype(f32) * (s0 * (1.0 - s0))).astype(bf16)
        dg_ref[:, D:] = (dmv * b_ref[...].astype(f32) * (s1 * (1.0 - s1))).astype(bf16)

    row = pl.BlockSpec((tm, D), lambda i: (i, 0))
    wide = pl.BlockSpec((tm, 2 * D), lambda i: (i, 0))
    return pl.pallas_call(
        body,
        grid=(S // tm,),
        in_specs=[row, row, row, row, pl.BlockSpec((tm, D), lambda i: (i, 1))],
        out_specs=[row, row, wide],
        out_shape=[SDS((S, D), bf16), SDS((S, D), bf16), SDS((S, 2 * D), bf16)],
        compiler_params=_cparams(("parallel",)),
        name="gate_bwd",
    )(dm, a, b, gc, gc)


CONV_ROWS = 512
INV_SQRT2 = 0.7071067811865476
INV_SQRT_2PI = 0.3989422804014327


CONV_HALO = 16


def _tile8(a, rows):
    return jnp.tile(a, (rows // a.shape[0], 1))


def _conv_rows(u_ref, w, b, r0, first):
    R = CONV_ROWS
    cur = u_ref[pl.ds(r0, R), :].astype(f32)
    prev8 = u_ref[pl.ds(pl.multiple_of(jnp.maximum(r0 - CONV_HALO, 0), CONV_HALO), CONV_HALO), :].astype(f32)
    prev8 = jnp.where(first, 0.0, prev8)
    row = lax.broadcasted_iota(jnp.int32, (R, LANE), 0)
    x1 = jnp.where(row < 1, _tile8(pltpu.roll(prev8, 1, 0), R), pltpu.roll(cur, 1, 0))
    x2 = jnp.where(row < 2, _tile8(pltpu.roll(prev8, 2, 0), R), pltpu.roll(cur, 2, 0))
    c = ((b + w[0:1] * x2) + w[1:2] * x1) + w[2:3] * cur
    return c, x2, x1, cur


def _conv_fwd(ug, uv, wg, wv, bg, bv):
    S, F = ug.shape
    nchunk = S // CONV_ROWS

    def body(ug_ref, uv_ref, wg_ref, wv_ref, bg_ref, bv_ref, o_ref):
        wgv, wvv, bgv, bvv = wg_ref[...], wv_ref[...], bg_ref[...], bv_ref[...]

        def step(ci, carry):
            r0 = pl.multiple_of(ci * CONV_ROWS, CONV_ROWS)
            cg = _conv_rows(ug_ref, wgv, bgv, r0, ci == 0)[0]
            cv = _conv_rows(uv_ref, wvv, bvv, r0, ci == 0)[0]
            gelu = 0.5 * cg * (1.0 + lax.erf(cg * INV_SQRT2))
            o_ref[pl.ds(r0, CONV_ROWS), :] = (gelu * cv).astype(bf16)
            return carry

        lax.fori_loop(0, nchunk, step, 0)

    col = pl.BlockSpec((S, LANE), lambda j: (0, j))
    w3 = pl.BlockSpec((3, LANE), lambda j: (0, j))
    b1 = pl.BlockSpec((1, LANE), lambda j: (0, j))
    return pl.pallas_call(
        body,
        grid=(F // LANE,),
        in_specs=[col, col, w3, w3, b1, b1],
        out_specs=col,
        out_shape=SDS((S, F), bf16),
        compiler_params=_cparams(("parallel",), VMEM_BIG),
        name="conv_fwd",
    )(ug, uv, wg, wv, bg, bv)


def _conv_bwd(ug, uv, dact, wg, wv, bg, bv):
    S, F = ug.shape
    R = CONV_ROWS
    nchunk = S // R

    def body(ug_ref, uv_ref, da_ref, wg_ref, wv_ref, bg_ref, bv_ref, dug_ref, duv_ref, sg_ref, sv_ref, dcg, dcv):
        wgv, wvv, bgv, bvv = wg_ref[...], wv_ref[...], bg_ref[...], bv_ref[...]
        zero = jnp.zeros((SUBLANE, LANE), f32)

        def fwd_step(ci, acc):
            r0 = pl.multiple_of(ci * R, R)
            cg, g2, g1, g0 = _conv_rows(ug_ref, wgv, bgv, r0, ci == 0)
            cv, v2, v1, v0 = _conv_rows(uv_ref, wvv, bvv, r0, ci == 0)
            da = da_ref[pl.ds(r0, R), :].astype(f32)
            cdf = 0.5 * (1.0 + lax.erf(cg * INV_SQRT2))
            pdf = INV_SQRT_2PI * jnp.exp(-0.5 * cg * cg)
            dg = da * cv * (cdf + cg * pdf)
            dv = da * (cg * cdf)
            dcg[pl.ds(r0, R), :] = dg
            dcv[pl.ds(r0, R), :] = dv
            new = (acc[0] + _colsum8(dg * g2), acc[1] + _colsum8(dg * g1), acc[2] + _colsum8(dg * g0),
                   acc[3] + _colsum8(dg),
                   acc[4] + _colsum8(dv * v2), acc[5] + _colsum8(dv * v1), acc[6] + _colsum8(dv * v0),
                   acc[7] + _colsum8(dv))
            return new

        acc = lax.fori_loop(0, nchunk, fwd_step, (zero,) * 8)
        rows = lax.broadcasted_iota(jnp.int32, (SUBLANE, LANE), 0)

        def stats(parts):
            out = jnp.zeros((SUBLANE, LANE), f32)
            for k, pt in enumerate(parts):
                out = jnp.where(rows == k, jnp.sum(pt, axis=0, keepdims=True), out)
            return out

        sg_ref[...] = stats(acc[0:4])
        sv_ref[...] = stats(acc[4:8])

        def du_rows(dc, w, r0, last):
            cur = dc[pl.ds(r0, R), :]
            nxt = dc[pl.ds(pl.multiple_of(jnp.minimum(r0 + R, S - SUBLANE), SUBLANE), SUBLANE), :]
            nxt = jnp.where(last, 0.0, nxt)
            row = lax.broadcasted_iota(jnp.int32, (R, LANE), 0)
            y1 = jnp.where(row >= R - 1, _tile8(pltpu.roll(nxt, SUBLANE - 1, 0), R), pltpu.roll(cur, R - 1, 0))
            y2 = jnp.where(row >= R - 2, _tile8(pltpu.roll(nxt, SUBLANE - 2, 0), R), pltpu.roll(cur, R - 2, 0))
            return w[2:3] * cur + w[1:2] * y1 + w[0:1] * y2

        def bwd_step(ci, carry):
            r0 = pl.multiple_of(ci * R, R)
            last = ci == nchunk - 1
            dug_ref[pl.ds(r0, R), :] = du_rows(dcg, wgv, r0, last).astype(bf16)
            duv_ref[pl.ds(r0, R), :] = du_rows(dcv, wvv, r0, last).astype(bf16)
            return carry

        lax.fori_loop(0, nchunk, bwd_step, 0)

    col = pl.BlockSpec((S, LANE), lambda j: (0, j))
    w3 = pl.BlockSpec((3, LANE), lambda j: (0, j))
    b1 = pl.BlockSpec((1, LANE), lambda j: (0, j))
    st = pl.BlockSpec((SUBLANE, LANE), lambda j: (0, j))
    return pl.pallas_call(
        body,
        grid=(F // LANE,),
        in_specs=[col, col, col, w3, w3, b1, b1],
        out_specs=[col, col, st, st],
        out_shape=[SDS((S, F), bf16), SDS((S, F), bf16), SDS((SUBLANE, F), f32), SDS((SUBLANE, F), f32)],
        scratch_shapes=[pltpu.VMEM((S, LANE), f32), pltpu.VMEM((S, LANE), f32)],
        compiler_params=_cparams(("parallel",), VMEM_BIG),
        name="conv_bwd",
    )(ug, uv, dact, wg, wv, bg, bv)


def _adam_math(w, g, m, v):
    m = ADAM_B1 * m + (1.0 - ADAM_B1) * g
    v = ADAM_B2 * v + (1.0 - ADAM_B2) * (g * g)
    m_hat = m / (1.0 - ADAM_B1 ** ADAM_STEP)
    v_hat = v / (1.0 - ADAM_B2 ** ADAM_STEP)
    delta = -ADAM_LR * (m_hat / (jnp.sqrt(v_hat) + ADAM_EPS) + ADAM_WD * w)
    return delta, m, v


def _adamw(w, m, v, g, name):
    R, C = w.shape
    parts = g.ndim == 3
    tr = R
    for t in (256, 128, 64, 32, 16):
        if R % t == 0 and R > t:
            tr = t
            break

    def body(w_ref, m_ref, v_ref, g_ref, go_ref, d_ref, mo_ref, vo_ref):
        if parts:
            gv = ((g_ref[0].astype(f32) + g_ref[1].astype(f32)) + g_ref[2].astype(f32)) + g_ref[3].astype(f32)
        else:
            gv = g_ref[...]
        go_ref[...] = gv
        d, mn, vn = _adam_math(w_ref[...], gv, m_ref[...], v_ref[...])
        d_ref[...] = d
        mo_ref[...] = mn
        vo_ref[...] = vn

    row = pl.BlockSpec((tr, C), lambda i: (i, 0))
    gspec = pl.BlockSpec((4, tr, C), lambda i: (0, i, 0)) if parts else row
    return pl.pallas_call(
        body,
        grid=(R // tr,),
        in_specs=[row, row, row, gspec],
        out_specs=[row] * 4,
        out_shape=[SDS((R, C), f32)] * 4,
        compiler_params=_cparams(("parallel",)),
        name=name,
    )(w, m, v, g)


def _sum8(parts, name):
    _, _, R, C = parts.shape

    def body(p_ref, o_ref):
        acc = p_ref[0, 0]
        for c in range(2):
            for k in range(4):
                if c or k:
                    acc = acc + p_ref[c, k]
        o_ref[...] = acc

    return pl.pallas_call(body, out_shape=SDS((R, C), f32), name=name)(parts)


def _pair_add(by_core, b, name):
    _, K, R, C = by_core.shape
    tr = R // 2 if R % 32 == 0 else R

    def body(c_ref, a_ref, b_ref, o_ref):
        o_ref[...] = (a_ref[0].astype(f32) + b_ref[...].astype(f32)).astype(bf16)

    blk = pl.BlockSpec((1, tr, C), lambda k, i, c: (k, i, 0))
    return pl.pallas_call(
        body,
        grid_spec=pltpu.PrefetchScalarGridSpec(
            num_scalar_prefetch=1,
            grid=(K, R // tr),
            in_specs=[pl.BlockSpec((1, 1, tr, C), lambda k, i, c: (c[0], k, i, 0)), blk],
            out_specs=blk,
        ),
        out_shape=SDS((K, R, C), bf16),
        compiler_params=_cparams(("parallel", "parallel")),
        name=name,
    )(lax.axis_index("c").astype(jnp.int32).reshape(1), by_core, b)


_ANY = pl.BlockSpec(memory_space=pl.ANY)


def _chip_copies(src_ref, out_ref, send_sems, recv_sems, gather):
    x, y, c = lax.axis_index("x"), lax.axis_index("y"), lax.axis_index("c")
    mine = 2 * x + y

    def piece(k):
        return src_ref if gather else src_ref.at[k]

    sends, recvs = [], []
    for j, (px, py) in enumerate([(1 - x, y), (x, 1 - y), (1 - x, 1 - y)]):
        sends.append(pltpu.make_async_remote_copy(
            src_ref=piece(2 * px + py), dst_ref=out_ref.at[mine], send_sem=send_sems.at[j],
            recv_sem=recv_sems.at[j], device_id=(px, py, c), device_id_type=MESH))
        recvs.append(pltpu.make_async_remote_copy(
            src_ref=piece(mine), dst_ref=out_ref.at[2 * px + py], send_sem=send_sems.at[j],
            recv_sem=recv_sems.at[j], device_id=(px, py, c), device_id_type=MESH))
    return sends, recvs


def _chip_start(src_ref, out_ref, send_sems, recv_sems, gather):
    for cp in _chip_copies(src_ref, out_ref, send_sems, recv_sems, gather)[0]:
        cp.start()


def _chip_finish(src_ref, out_ref, send_sems, recv_sems, gather):
    sends, recvs = _chip_copies(src_ref, out_ref, send_sems, recv_sems, gather)
    for cp in recvs:
        cp.wait_recv()
    for cp in sends:
        cp.wait_send()


def _chip_out_shape(src, gather):
    return SDS((4,) + tuple(src.shape if gather else src.shape[1:]), src.dtype)


_CHIP_SEMS = [pltpu.SemaphoreType.DMA((3,)), pltpu.SemaphoreType.DMA((3,))]


def _fill_own(out, src, gather):
    mine = 2 * lax.axis_index("x") + lax.axis_index("y")
    own = src if gather else lax.dynamic_index_in_dim(src, mine, axis=0, keepdims=False)
    return lax.dynamic_update_index_in_dim(out, own, mine, axis=0)


def _chip_comm(src, gather, name):
    def body(src_ref, out_ref, send_sems, recv_sems):
        _chip_start(src_ref, out_ref, send_sems, recv_sems, gather)
        _chip_finish(src_ref, out_ref, send_sems, recv_sems, gather)

    out = pl.pallas_call(
        body,
        in_specs=[_ANY],
        out_specs=_ANY,
        out_shape=_chip_out_shape(src, gather),
        scratch_shapes=list(_CHIP_SEMS),
        name=name,
    )(src)
    return _fill_own(out, src, gather)


_HBM = pl.BlockSpec(memory_space=pltpu.HBM)
_SEM = pl.BlockSpec(memory_space=pltpu.SEMAPHORE)
_EFFECT = pltpu.SideEffectType.DATAFLOW_SIDE_EFFECTING
_SPLIT_PEERS = {"chip_gather": 3, "chip_xchg": 3, "core_gather": 1, "core_swap": 1}


def _split_land(src, kind):
    if kind == "core_gather":
        return SDS((2,) + tuple(src.shape), src.dtype)
    if kind == "core_swap":
        return SDS(tuple(src.shape[1:]), src.dtype)
    return _chip_out_shape(src, kind == "chip_gather")


def _split_copies(src_ref, land_ref, sems, kind):
    x, y, c = lax.axis_index("x"), lax.axis_index("y"), lax.axis_index("c")
    n = _SPLIT_PEERS[kind]
    if kind == "core_gather":
        routes = [((x, y, 1 - c), src_ref, land_ref.at[c], land_ref.at[1 - c])]
    elif kind == "core_swap":
        routes = [((x, y, 1 - c), src_ref.at[1 - c], land_ref, land_ref)]
    else:
        mine = 2 * x + y
        gather = kind == "chip_gather"
        routes = [((px, py, c), src_ref if gather else src_ref.at[2 * px + py], land_ref.at[mine],
                   land_ref.at[2 * px + py]) for px, py in [(1 - x, y), (x, 1 - y), (1 - x, 1 - y)]]
    sends, recvs = [], []
    for j, (peer, piece, there, here) in enumerate(routes):
        sends.append(pltpu.make_async_remote_copy(src_ref=piece, dst_ref=there, send_sem=sems[j],
                                                  recv_sem=sems[n + j], device_id=peer, device_id_type=MESH))
        recvs.append(pltpu.make_async_remote_copy(src_ref=piece, dst_ref=here, send_sem=sems[j],
                                                  recv_sem=sems[n + j], device_id=peer, device_id_type=MESH))
    return sends, recvs


def _split_start(src, kind, name, after=None):
    land = _split_land(src, kind)
    ns = 2 * _SPLIT_PEERS[kind]
    n_in = 2 if after is None else 3

    def body(*refs):
        src_ref, land_ref = refs[:2]
        outs = refs[n_in:]
        for cp in _split_copies(src_ref, land_ref, outs[:ns], kind)[0]:
            cp.start()
        token = outs[ns + 2]
        token[...] = jnp.zeros_like(token)

    res = pl.pallas_call(
        body,
        name=name,
        out_shape=(pltpu.SemaphoreType.DMA(()),) * ns
        + (pltpu.HBM(src.shape, src.dtype), pltpu.HBM(land.shape, land.dtype), SDS((SUBLANE, LANE), f32)),
        in_specs=(_HBM, _HBM) + (() if after is None else (_ANY,)),
        out_specs=(_SEM,) * ns + (_HBM, _HBM, pl.BlockSpec(memory_space=pltpu.VMEM)),
        input_output_aliases={0: ns, 1: ns + 1},
        compiler_params=pltpu.CompilerParams(has_side_effects=_EFFECT),
    )(pltpu.with_memory_space_constraint(src, pltpu.HBM),
      pltpu.with_memory_space_constraint(lax.empty(land.shape, land.dtype), pltpu.HBM),
      *(() if after is None else (after,)))
    return (res[:ns], res[ns], res[ns + 1]), res[ns + 2]


def _split_wait(state, after, kind, name):
    sems, src_thru, land_thru = state
    ns = 2 * _SPLIT_PEERS[kind]

    def body(src_ref, land_ref, *rest):
        sends, recvs = _split_copies(src_ref, land_ref, rest[:ns], kind)
        for cp in recvs:
            cp.wait_recv()
        for cp in sends:
            cp.wait_send()

    src_out, got = pl.pallas_call(
        body,
        name=name,
        out_shape=(pltpu.HBM(src_thru.shape, src_thru.dtype), pltpu.HBM(land_thru.shape, land_thru.dtype)),
        in_specs=(_HBM, _HBM) + (_SEM,) * ns + (_ANY,),
        out_specs=(_HBM, _HBM),
        input_output_aliases={0: 0, 1: 1},
        compiler_params=pltpu.CompilerParams(has_side_effects=_EFFECT),
    )(src_thru, land_thru, *sems, after)
    if kind == "core_swap":
        return got, src_out
    if kind == "core_gather":
        return lax.dynamic_update_index_in_dim(got, src_out, lax.axis_index("c"), axis=0)
    return _fill_own(got, src_out, kind == "chip_gather")


def _core_gather(src, name):
    def body(src_ref, out_ref, send_sem, recv_sem):
        x, y, c = lax.axis_index("x"), lax.axis_index("y"), lax.axis_index("c")
        cp = pltpu.make_async_remote_copy(src_ref=src_ref, dst_ref=out_ref.at[c], send_sem=send_sem,
                                          recv_sem=recv_sem, device_id=(x, y, 1 - c), device_id_type=MESH)
        cp.start()
        pltpu.make_async_remote_copy(src_ref=src_ref, dst_ref=out_ref.at[1 - c], send_sem=send_sem,
                                     recv_sem=recv_sem, device_id=(x, y, 1 - c), device_id_type=MESH).wait_recv()
        cp.wait_send()

    out = pl.pallas_call(
        body,
        in_specs=[_ANY],
        out_specs=_ANY,
        out_shape=SDS((2,) + tuple(src.shape), src.dtype),
        scratch_shapes=[pltpu.SemaphoreType.DMA, pltpu.SemaphoreType.DMA],
        name=name,
    )(src)
    return lax.dynamic_update_index_in_dim(out, src, lax.axis_index("c"), axis=0)


_PACK_A = (("w_in", (1088, 1024)),)
_PACK_B = (("w_ba", (512, 128)), ("w_bh", (512, 128)), ("w_out", (128, 1024)), ("w_up", (704, 1024)),
           ("w_down", (352, 1024)))
_PACK_SIZES = _PACK_A + _PACK_B
_TRANSPOSED = ("w_in", "w_up")


def _slab_rows(sizes):
    return sum(r * c for _, (r, c) in sizes) // D_MODEL


def _pack_rows(d, sizes):
    n = d[sizes[0][0]].shape[0]
    return jnp.concatenate([d[k].reshape(n, -1, D_MODEL) for k, _ in sizes], axis=1)


def _unpack_rows(slab, sizes):
    n = slab.shape[0]
    out, lo = {}, 0
    for key, (r, c) in sizes:
        rows = r * c // D_MODEL
        out[key] = slab[:, lo:lo + rows].reshape(n, r, c)
        lo += rows
    return out


def _by_core(gslab):
    return jnp.swapaxes(gslab.reshape((4, 2) + gslab.shape[1:]), 0, 1)


def _cols_to_full(t):
    return jnp.swapaxes(t, 0, 1).reshape(t.shape[1], -1)


def _full_to_cols(t):
    K = t.shape[0]
    return jnp.swapaxes(t.reshape(K, 8, -1), 0, 1)


_SMALL = (("pre_mix_norm", (1, 1024)), ("rel_bias", (32, 24)), ("hgrn_lb_raw", (2, 512)), ("hgrn_norm", (1, 128)),
          ("post_mix_norm", (1, 1024)), ("pre_ffn_norm", (1, 1024)), ("conv_b", (1, 5632)),
          ("post_ffn_norm", (1, 1024)))
_SMALL_ROWS = 96
_CONVW_ROWS = 136


_SMALL_USED = sum(r * c for _, (r, c) in _SMALL)


def _pack_small(d, extra=None):
    flat = jnp.concatenate([d[k].reshape(-1) for k, _ in _SMALL] + ([] if extra is None else [extra.reshape(-1)]))
    flat = jnp.pad(flat, (0, _SMALL_ROWS * LANE - flat.shape[0]))
    return flat.reshape(_SMALL_ROWS, LANE)


def _unpack_small(p):
    flat = p.reshape(-1)
    out, lo = {}, 0
    for k, shp in _SMALL:
        n = shp[0] * shp[1]
        out[k] = flat[lo:lo + n].reshape(shp)
        lo += n
    return out


def _local_step(x, tgt, P, plan):
    S = x.shape[0]
    P = dict(P)
    lb = _lb_fwd(P["hgrn_lb_raw"])
    hs = _prep(x, P["pre_mix_norm"], plan.start_token())
    h1 = hs[0]
    consts = [_bias_consts(d) for d in DILATIONS]
    biases, dep = [], h1
    for g in range(N_GROUPS):
        tab_t = P["rel_bias"][:, 8 * g:8 * g + 8].T
        dep = _bias_build(tab_t, consts[g][0], consts[g][1], f"bias_build{g}", dep)
        biases.append(dep.reshape(8, ATTN_BLOCK, 2 * ATTN_BLOCK))
    W = dict(plan.weights_a(dep))
    qkv = [_mm(hs[g], W["wt_qkv"][g], "nt", bf16, f"proj_qkv{g}") for g in range(N_GROUPS)]
    hg = _mm(h1, W["wt_hg"], "nt", f32, "proj_hg")
    gc = _mm(h1, W["wt_gate"], "nt", bf16, "proj_gate")
    obuf, lbuf, token = [], [], None
    for g, d in enumerate(DILATIONS):
        o_g, l_g = _attn_fwd(qkv[g], biases[g], (S // d) // ATTN_BLOCK, f"attn_fwd{g}", after=token)
        lbuf.append(l_g)
        obuf.append(o_g)
        if g == 0:
            token = plan.forward_b(o_g)
    y_attn, y_attn_b, w0, w1, w2 = _attn_merge(obuf[0], obuf[1], obuf[2], lbuf[0], lbuf[1], lbuf[2])
    y_hgrn, o_raw, ck, _ = _hgrn_fwd(hg, lb, P["hgrn_norm"])
    wb = plan.weights_b(y_hgrn)
    P["conv_w"] = wb.pop("conv_w")
    W.update(wb)
    a = _mm(y_attn_b, W["w_ba"], "nn", bf16, "branch_attn")
    b = _mm(y_hgrn, W["w_bh"], "nn", bf16, "branch_hgrn")
    merged = _gate_fwd(a, b, gc)
    mo, x1, h2 = _mid_fwd(x, merged, W["w_out"], P["post_mix_norm"], P["pre_ffn_norm"])
    ug = _mm(h2, W["wt_up_g"], "nt", bf16, "up_gate")
    uv = _mm(h2, W["wt_up_v"], "nt", bf16, "up_val")
    cw_g, cw_v = P["conv_w"][:, :D_FF], P["conv_w"][:, D_FF:]
    cb_g, cb_v = P["conv_b"][:, :D_FF], P["conv_b"][:, D_FF:]
    act = _conv_fwd(ug, uv, cw_g, cw_v, cb_g, cb_v)
    loss, dy, dfo, g_post_ffn = _final(x1, act, W["w_down"], tgt, P["post_ffn_norm"])
    dact = _mm(dfo, W["w_down"], "nt", bf16, "d_act")
    gW_down = _mm(act, dfo, "tn", bf16, "gw_down")
    dug, duv, st_g, st_v = _conv_bwd(ug, uv, dact, cw_g, cw_v, cb_g, cb_v)
    dh2 = _mm([dug, duv], [W["wt_up_g"], W["wt_up_v"]], "nn", f32, "dh2")
    gW_up_g = _mm(dug, h2, "tn", bf16, "gw_up_gate")
    gW_up_v = _mm(duv, h2, "tn", bf16, "gw_up_val")
    dx1, dmo, g_pre_ffn, g_post_mix = _mid_bwd(dy, dh2, x1, mo, P["pre_ffn_norm"], P["post_mix_norm"])
    dmerged = _mm(dmo, W["w_out"], "nt", bf16, "d_merged")
    gW_out = _mm(merged, dmo, "tn", bf16, "gw_out")
    da, db, dgc = _gate_bwd(dmerged, a, b, gc)
    dyattn = _mm(da, W["w_ba"], "nt", f32, "d_yattn")
    gW_ba = _mm(y_attn_b, da, "tn", bf16, "gw_ba")
    dyhgrn = _mm(db, W["w_bh"], "nt", f32, "d_yhgrn")
    gW_bh = _mm(y_hgrn, db, "tn", bf16, "gw_bh")
    big_b = dict(w_ba=gW_ba, w_bh=gW_bh, w_out=gW_out, w_up=[gW_up_g, gW_up_v], w_down=gW_down)
    dos = _attn_merge_bwd(dyattn, y_attn, w0, w1, w2, after=plan.grads_b_start(big_b))
    dq_h, df_h, dv_h, dog_h, glb8, gnw8, got_b = _hgrn_bwd(hg, o_raw, dyhgrn, ck, lb, P["hgrn_norm"],
                                                          plan.bwd_ride(dos[5]))
    dhg = [dq_h, df_h, dv_h, dog_h]
    g_lb_raw = _lb_bwd(P["hgrn_lb_raw"], glb8[0:1])
    gn = gnw8[0:1]
    g_hgrn_norm = (gn[:, 0:128] + gn[:, 128:256]) + (gn[:, 256:384] + gn[:, 384:512])
    dqkvs, gW_qkv, g_rel = [], [], []
    for g, d in enumerate(DILATIONS):
        dq, dk, dv, dbias = _attn_bwd(qkv[g], biases[g], dos[g], dos[3 + g], lbuf[g], (S // d) // ATTN_BLOCK,
                                      f"attn_bwd{g}")
        dqkvs.append([dq, dk, dv])
        gW_qkv.append(_mm(dqkvs[g], hs[g], "tn", bf16, f"gw_qkv{g}"))
        g_rel.append(_bias_grad(dbias.reshape(8, -1), consts[g][0], f"bias_grad{g}"))
    gW_hg = _mm(dhg, h1, "tn", bf16, "gw_hg")
    gW_gate = _mm(dgc, h1, "tn", bf16, "gw_gate")
    gW_in = gW_qkv + [gW_hg, gW_gate]
    token = plan.grads_a_start(gW_in)
    dh_perm = [_mm(dqkvs[g], W["wt_qkv"][g], "nn", f32, f"dh1_qkv{g}", after=token) for g in (1, 2)]
    token = plan.grads_a_exchange(dh_perm[1])
    dh_main = _mm(dqkvs[0] + dhg + [dgc], [W["wt_qkv"][0], W["wt_hg"], W["wt_gate"]], "nn", f32, "dh1_main",
                  after=token)
    grad_x, g_pre_mix = _first_bwd(x, dx1, _dh_sum(dh_main, dh_perm[0], dh_perm[1]), P["pre_mix_norm"])

    g_conv_w = jnp.concatenate([st_g[0:3], st_v[0:3]], axis=1)
    g_conv_b = jnp.concatenate([st_g[3:4], st_v[3:4]], axis=1)
    small = dict(pre_mix_norm=g_pre_mix, rel_bias=jnp.concatenate(g_rel, axis=1), hgrn_lb_raw=g_lb_raw,
                 hgrn_norm=g_hgrn_norm, post_mix_norm=g_post_mix, pre_ffn_norm=g_pre_ffn, conv_b=g_conv_b,
                 post_ffn_norm=g_post_ffn, conv_w=g_conv_w)
    return loss, grad_x, gW_in, big_b, got_b, small


def _weights_a(both):
    wt = jnp.swapaxes(both, 0, 1).reshape(-1, D_MODEL)
    return dict(
        wt_qkv=[wt[g * QKV_G:(g + 1) * QKV_G] for g in range(N_GROUPS)],
        wt_hg=wt[3 * QKV_G:3 * QKV_G + 4 * HGRN_W],
        wt_gate=wt[3 * QKV_G + 4 * HGRN_W:],
    )


def _weights_b(slabs):
    sh = _unpack_rows(slabs, _PACK_B)
    wt_up = sh["w_up"].reshape(-1, D_MODEL)
    return dict(
        w_ba=_cols_to_full(sh["w_ba"]),
        w_bh=_cols_to_full(sh["w_bh"]),
        w_out=sh["w_out"].reshape(D_MODEL, D_MODEL),
        wt_up_g=wt_up[:D_FF],
        wt_up_v=wt_up[D_FF:],
        w_down=sh["w_down"].reshape(D_FF, D_MODEL),
    )


def _dest_rows(sections, height):
    out = []
    for j in range(8):
        lo, hi, off, pieces = j * height, (j + 1) * height, 0, []
        for s in sections:
            a, b = max(lo, off), min(hi, off + s.shape[0])
            if a < b:
                pieces.append(s[a - off:b - off])
            off += s.shape[0]
        out.append(pieces[0] if len(pieces) == 1 else jnp.concatenate(pieces, axis=0))
    return out


def _grad_blocks_a(sections):
    rows = _dest_rows(sections, 1088)
    return jnp.stack([jnp.stack([rows[2 * k + c].astype(bf16) for k in range(4)]) for c in range(2)])


def _grad_slab_b(g):
    shards = dict(w_ba=_full_to_cols(g["w_ba"]), w_bh=_full_to_cols(g["w_bh"]), w_out=g["w_out"].reshape(8, 128, D_MODEL),
                  w_up=jnp.stack(_dest_rows(g["w_up"], 704)), w_down=g["w_down"].reshape(8, 352, D_MODEL))
    return _pack_rows({k: v.astype(bf16) for k, v in shards.items()}, _PACK_B)


_CONVW_SLAB_ROWS = 16


class _Traffic:
    def __init__(self, slab_a, slab_b, conv_w):
        hi = conv_w.astype(bf16)
        r1 = conv_w - hi.astype(f32)
        mid = r1.astype(bf16)
        lo = (r1 - mid.astype(f32)).astype(bf16)
        bits = jnp.stack([hi, mid, lo]).reshape(-1)
        tail = jnp.pad(bits, (0, _CONVW_SLAB_ROWS * D_MODEL - bits.shape[0])).reshape(_CONVW_SLAB_ROWS, D_MODEL)
        self.slab_b = jnp.concatenate([slab_b, tail], axis=0)
        self.state_a, tok = _split_start(slab_a, "chip_gather", "ag_a_start")
        self.state_b, self.token = _split_start(self.slab_b, "chip_gather", "ag_b_start", after=tok)
        self.chip_sum = None
        self.state = None

    def start_token(self):
        return self.token

    def weights_a(self, after):
        by_chip = _split_wait(self.state_a, after, "chip_gather", "ag_a_wait")
        return _weights_a(_core_gather(by_chip, "ag_a_cores"))

    def forward_b(self, after):
        by_chip = _split_wait(self.state_b, after, "chip_gather", "ag_b_wait")
        self.state, token = _split_start(by_chip, "core_gather", "ag_b_cores_start")
        return token

    def weights_b(self, after):
        both = _split_wait(self.state, after, "core_gather", "ag_b_cores_wait")
        slabs = jnp.swapaxes(both, 0, 1).reshape((8,) + tuple(self.slab_b.shape))
        rows = _slab_rows(_PACK_B)
        out = _weights_b(slabs[:, :rows])
        pieces = slabs[:, rows:].reshape(8, -1)[:, :3 * 3 * 704].reshape(8, 3, 3, 704).astype(f32)
        out["conv_w"] = _cols_to_full((pieces[:, 0] + pieces[:, 1]) + pieces[:, 2])
        return out

    def grads_b_start(self, grads):
        self.state, token = _split_start(_by_core(_grad_slab_b(grads)), "core_swap", "rs_b_cores_start")
        return token

    def bwd_ride(self, after):
        from_sib, by_core = _split_wait(self.state, after, "core_swap", "rs_b_cores_wait")
        self.chip_sum = _pair_add(by_core, from_sib, "rs_b_pair_add")
        return (self.chip_sum, False)

    def grads_a_start(self, sections):
        self.state, token = _split_start(_grad_blocks_a(sections), "core_swap", "rs_a_cores_start")
        return token

    def grads_a_exchange(self, after):
        from_sib, by_core = _split_wait(self.state, after, "core_swap", "rs_a_cores_wait")
        self.state, token = _split_start(_pair_add(by_core, from_sib, "rs_a_pair_add"), "chip_xchg", "rs_a_start")
        return token

    def parts(self, got_b, after):
        parts = _unpack_rows(_fill_own(got_b, self.chip_sum, False), _PACK_B)
        parts["w_in"] = _split_wait(self.state, after, "chip_xchg", "rs_a_wait")
        return parts


def kernel(x, pre_mix_norm, w_in, rel_bias, hgrn_lb_raw, hgrn_norm, w_branch_attn, w_branch_hgrn, w_out, post_mix_norm, pre_ffn_norm, w_up, conv_w, conv_b, w_down, post_ffn_norm, loss_target, m_pre_mix_norm, m_w_in, m_rel_bias, m_hgrn_lb_raw, m_hgrn_norm, m_w_branch_attn, m_w_branch_hgrn, m_w_out, m_post_mix_norm, m_pre_ffn_norm, m_w_up, m_conv_w, m_conv_b, m_w_down, m_post_ffn_norm, v_pre_mix_norm, v_w_in, v_rel_bias, v_hgrn_lb_raw, v_hgrn_norm, v_w_branch_attn, v_w_branch_hgrn, v_w_out, v_post_mix_norm, v_pre_ffn_norm, v_w_up, v_conv_w, v_conv_b, v_w_down, v_post_ffn_norm):
    ci = lax.axis_index("c")
    dev = 4 * lax.axis_index("x") + 2 * lax.axis_index("y") + ci
    tr = lambda t: jnp.swapaxes(t[0], 0, 1)
    wts = dict(w_in=tr(w_in), w_ba=w_branch_attn[0], w_bh=w_branch_hgrn[0], w_out=w_out[0], w_up=tr(w_up),
               w_down=w_down[0])
    mom = dict(w_in=tr(m_w_in), w_ba=m_w_branch_attn[0], w_bh=m_w_branch_hgrn[0], w_out=m_w_out[0], w_up=tr(m_w_up),
               w_down=m_w_down[0])
    var = dict(w_in=tr(v_w_in), w_ba=v_w_branch_attn[0], w_bh=v_w_branch_hgrn[0], w_out=v_w_out[0], w_up=tr(v_w_up),
               w_down=v_w_down[0])
    small_w = dict(pre_mix_norm=pre_mix_norm, rel_bias=rel_bias, hgrn_lb_raw=hgrn_lb_raw, hgrn_norm=hgrn_norm,
                   post_mix_norm=post_mix_norm, pre_ffn_norm=pre_ffn_norm, conv_b=conv_b, post_ffn_norm=post_ffn_norm)
    small_m = dict(pre_mix_norm=m_pre_mix_norm, rel_bias=m_rel_bias, hgrn_lb_raw=m_hgrn_lb_raw, hgrn_norm=m_hgrn_norm,
                   post_mix_norm=m_post_mix_norm, pre_ffn_norm=m_pre_ffn_norm, conv_b=m_conv_b,
                   post_ffn_norm=m_post_ffn_norm)
    small_v = dict(pre_mix_norm=v_pre_mix_norm, rel_bias=v_rel_bias, hgrn_lb_raw=v_hgrn_lb_raw, hgrn_norm=v_hgrn_norm,
                   post_mix_norm=v_post_mix_norm, pre_ffn_norm=v_pre_ffn_norm, conv_b=v_conv_b,
                   post_ffn_norm=v_post_ffn_norm)

    plan = _Traffic(wts["w_in"].astype(bf16),
                    _pack_rows({k: wts[k].astype(bf16)[None] for k, _ in _PACK_B}, _PACK_B)[0], conv_w[0])

    loss8, grad_x, _, _, got_b, small = _local_step(x[0], loss_target[0], small_w, plan)
    parts = plan.parts(got_b, grad_x)
    outs_big = {}
    for k, _ in _PACK_SIZES:
        outs_big[k] = _adamw(wts[k], mom[k], var[k], parts[k], "adamw_" + k)

    spack = jnp.concatenate([_pack_small(small, loss8[0, 0:1]),
                             jnp.pad(small["conv_w"].reshape(-1, LANE), ((0, _CONVW_ROWS - 132), (0, 0)))], axis=0)
    allp = _core_gather(_chip_comm(spack, True, "ag_small_chips"), "ag_small_cores")
    ssum = _sum8(allp, "small_sum")
    gs = ssum[:_SMALL_ROWS]
    loss = ssum[_SMALL_USED // LANE, _SMALL_USED % LANE]
    res_small = _adamw(_pack_small(small_w), _pack_small(small_m), _pack_small(small_v), gs, "adamw_small")
    sm = [_unpack_small(t) for t in res_small]
    g_cw_full = ssum[_SMALL_ROWS:_SMALL_ROWS + 132].reshape(3, 2 * D_FF)
    g_cw = lax.dynamic_slice_in_dim(g_cw_full, dev * 704, 704, axis=1)
    res_cw = _adamw(conv_w[0], m_conv_w[0], v_conv_w[0], g_cw, "adamw_conv_w")

    def pick(i):
        def big_(k):
            t = outs_big[k][i]
            return (jnp.swapaxes(t, 0, 1) if k in _TRANSPOSED else t)[None]
        return [sm[i]["pre_mix_norm"], big_("w_in"), sm[i]["rel_bias"], sm[i]["hgrn_lb_raw"], sm[i]["hgrn_norm"],
                big_("w_ba"), big_("w_bh"), big_("w_out"), sm[i]["post_mix_norm"], sm[i]["pre_ffn_norm"],
                big_("w_up"), res_cw[i][None], sm[i]["conv_b"], big_("w_down"), sm[i]["post_ffn_norm"]]

    return (loss, grad_x[None], *pick(0), *pick(1), *pick(2), *pick(3))
```

```python
import functools
import math

import jax
import jax.numpy as jnp
from jax import lax
from jax.experimental import pallas as pl
from jax.experimental.pallas import tpu as pltpu

f32 = jnp.float32
bf16 = jnp.bfloat16
SDS = jax.ShapeDtypeStruct
HIGHEST = lax.Precision.HIGHEST
MESH = pl.DeviceIdType.MESH

NN = (((1,), (0,)), ((), ()))
NT = (((1,), (1,)), ((), ()))
TN = (((0,), (0,)), ((), ()))

D_MODEL = 1024
N_GROUPS = 3
DILATIONS = (1, 4, 16)
HEAD_DIM = 64
ATTN_BLOCK = 128
QKV_G = 1536
ATTN_OUT = 512
HGRN_W = 512
HGRN_CHUNK = 32
D_FF = 2816
NUM_BUCKETS = 32
MAX_EXACT = 16
MAX_DISTANCE = 2048
NEG_INF = -1e30
EPS = 1e-6
LANE = 128
SUBLANE = 8
VMEM_BIG = 48 * 1024 * 1024
MM_ROWS = 512
MM_OUT_BYTES = 8 * 1024 * 1024
ADAM_BLOCK_BYTES = 2304 * 1024

ADAM_LR, ADAM_B1, ADAM_B2, ADAM_EPS, ADAM_WD, ADAM_STEP = 0.001, 0.9, 0.999, 1e-08, 0.01, 10


def _pick(n, pref):
    t = pref
    while t >= LANE:
        if n % t == 0:
            return t
        t //= 2
    return n


def _cparams(sem=None, vmem=None):
    kw = {}
    if sem is not None:
        kw["dimension_semantics"] = sem
    if vmem is not None:
        kw["vmem_limit_bytes"] = vmem
    return pltpu.CompilerParams(**kw)


def _sigmoid(x):
    return jax.nn.sigmoid(x)


def _colsum8(x):
    return x.reshape(x.shape[0] // SUBLANE, SUBLANE, x.shape[1]).sum(axis=0)


def _mm(a, b, mode, out_dtype, name, acc=None, after=None):
    dims = {"nn": NN, "nt": NT, "tn": TN}[mode]
    has_acc = acc is not None
    parts = list(a) if isinstance(a, (list, tuple)) else [a]
    if mode == "tn":
        assert not has_acc
        K, N = b.shape
        widths = [t.shape[1] for t in parts]
        M = sum(widths)
        whole = M * N * 4 <= MM_OUT_BYTES
        assert whole or len(parts) == 1
        tmm = M if whole else M // 2
        ts = _pick(K, 4 * MM_ROWS)
        nk = K // ts

        npart = len(parts)
        narrow = out_dtype != f32

        def body_tn(*refs):
            b_ref, o_ref = refs[npart], refs[npart + 1]
            acc_ref = refs[npart + 2] if narrow else o_ref
            k = pl.program_id(1)
            bv = b_ref[...]
            lo = 0
            for a_ref, w in zip(refs[:npart], widths if whole else [tmm]):
                part = lax.dot_general(a_ref[...], bv, dims, preferred_element_type=f32)
                rows = slice(lo, lo + w)
                lo += w

                @pl.when(k == 0)
                def _(part=part, rows=rows):
                    acc_ref[rows, :] = part

                @pl.when(k > 0)
                def _(part=part, rows=rows):
                    acc_ref[rows, :] += part

            if narrow:
                @pl.when(k == nk - 1)
                def _():
                    o_ref[...] = acc_ref[...].astype(out_dtype)

        return pl.pallas_call(
            body_tn,
            grid=(M // tmm, nk),
            in_specs=[pl.BlockSpec((ts, w if whole else tmm), lambda i, k: (k, i)) for w in widths]
            + [pl.BlockSpec((ts, N), lambda i, k: (k, 0))],
            out_specs=pl.BlockSpec((tmm, N), lambda i, k: (i, 0)),
            out_shape=SDS((M, N), out_dtype),
            scratch_shapes=[pltpu.VMEM((tmm, N), f32)] if narrow else [],
            compiler_params=_cparams(("parallel", "arbitrary"), VMEM_BIG),
            name=name,
        )(*parts, b)

    bs = list(b) if isinstance(b, (list, tuple)) else [b]
    widths = [t.shape[1] for t in parts]
    M = parts[0].shape[0]
    kdim = 0 if mode == "nn" else 1
    N = bs[0].shape[1 - kdim]
    tm = _pick(M, MM_ROWS)
    npart, nb = len(parts), len(bs)
    place, bi, lo = [], 0, 0
    for w in widths:
        place.append((bi, lo))
        lo += w
        if lo == bs[bi].shape[kdim]:
            bi, lo = bi + 1, 0
    assert bi == nb and lo == 0

    def body(*refs):
        a_refs, b_refs = refs[:npart], refs[npart:npart + nb]
        c_ref = refs[npart + nb] if has_acc else None
        o_ref = refs[-1]
        part = None
        for a_ref, w, (bi, lo) in zip(a_refs, widths, place):
            b_ref = b_refs[bi]
            if w == bs[bi].shape[kdim]:
                bk = b_ref[...]
            else:
                bk = b_ref[:, lo:lo + w] if mode == "nt" else b_ref[lo:lo + w, :]
            t = lax.dot_general(a_ref[...], bk, dims, preferred_element_type=f32)
            part = t if part is None else part + t
        if has_acc:
            part = part + c_ref[...]
        o_ref[...] = part.astype(out_dtype)

    specs = [pl.BlockSpec((tm, w), lambda i: (i, 0)) for w in widths] \
        + [pl.BlockSpec(t.shape, lambda i: (0, 0)) for t in bs]
    args = parts + bs
    aliases = {}
    if has_acc:
        specs.append(pl.BlockSpec((tm, N), lambda i: (i, 0)))
        args.append(acc)
        aliases = {npart + nb: 0}
    if after is not None:
        specs.append(pl.BlockSpec(memory_space=pl.ANY))
        args.append(after)
    return pl.pallas_call(
        body,
        grid=(M // tm,),
        in_specs=specs,
        out_specs=pl.BlockSpec((tm, N), lambda i: (i, 0)),
        out_shape=SDS((M, N), out_dtype),
        input_output_aliases=aliases,
        compiler_params=_cparams(("parallel",), VMEM_BIG),
        name=name,
    )(*args)


PERM_ROWS = 1024


def _perm_spec(d, cols=LANE):
    return pl.BlockSpec((d, PERM_ROWS // d, cols), lambda i, j: (0, i, j))


def _to_natural(src_ref, dst_ref, d):
    n = src_ref.shape[1]
    for r in range(d):
        dst_ref[pl.ds(r, n, stride=d), :] = src_ref[r]


def _prep(x, w, after=None):
    S, D = x.shape
    R = PERM_ROWS
    nc = D // LANE
    n_in = nc + 1 + (after is not None)

    def body(*refs):
        x_refs, w_ref = refs[:nc], refs[nc]
        h_ref, h4_ref, h16_ref, rs = refs[n_in:]
        ssq = None
        for xr in x_refs:
            v = xr[...]
            t = jnp.sum(v * v, axis=-1, keepdims=True)
            ssq = t if ssq is None else ssq + t
        rinv = lax.rsqrt(ssq * (1.0 / D) + EPS)
        rs[...] = jnp.broadcast_to(rinv, (R, LANE))
        for j, xr in enumerate(x_refs):
            cols = slice(j * LANE, (j + 1) * LANE)
            wj = w_ref[:, cols]
            h_ref[:, cols] = ((xr[...] * rinv) * wj).astype(bf16)
            for d, o_ref in ((4, h4_ref), (16, h16_ref)):
                n = R // d
                for r in range(d):
                    rows = pl.ds(r, n, stride=d)
                    o_ref[r, :, cols] = ((xr[rows, :] * rs[rows, :]) * wj).astype(bf16)

    col = lambda j: pl.BlockSpec((R, LANE), lambda i, j=j: (i, j))
    h, h4, h16 = pl.pallas_call(
        body,
        grid=(S // R,),
        in_specs=[col(j) for j in range(nc)] + [pl.BlockSpec((1, D), lambda i: (0, 0))]
        + ([] if after is None else [pl.BlockSpec(memory_space=pl.ANY)]),
        out_specs=[pl.BlockSpec((R, D), lambda i: (i, 0)), pl.BlockSpec((4, R // 4, D), lambda i: (0, i, 0)),
                   pl.BlockSpec((16, R // 16, D), lambda i: (0, i, 0))],
        out_shape=[SDS((S, D), bf16), SDS((4, S // 4, D), bf16), SDS((16, S // 16, D), bf16)],
        scratch_shapes=[pltpu.VMEM((R, LANE), f32)],
        compiler_params=_cparams(("parallel",), VMEM_BIG),
        name="prep_norm_perm",
    )(*([x] * nc), w, *([] if after is None else [after]))
    return [h, h4.reshape(S, D), h16.reshape(S, D)]


def _dh_sum(a, b, c):
    S, D = a.shape
    R = PERM_ROWS

    def body(a_ref, b_ref, c_ref, o_ref, sb, sc):
        _to_natural(b_ref, sb, 4)
        _to_natural(c_ref, sc, 16)
        o_ref[...] = (a_ref[...] + sb[...]) + sc[...]

    nat = pl.BlockSpec((R, LANE), lambda i, j: (i, j))
    return pl.pallas_call(
        body,
        grid=(S // R, D // LANE),
        in_specs=[nat, _perm_spec(4), _perm_spec(16)],
        out_specs=nat,
        out_shape=SDS((S, D), f32),
        scratch_shapes=[pltpu.VMEM((R, LANE), f32)] * 2,
        compiler_params=_cparams(("parallel", "parallel")),
        name="dh_sum",
    )(a, b.reshape(4, S // 4, D), c.reshape(16, S // 16, D))


def _rms_parts(xv):
    r = lax.rsqrt(jnp.mean(xv * xv, axis=-1, keepdims=True) + EPS)
    return r, xv * r


def _rms_bwd(xhat, r, w, dy):
    dyw = dy * w
    return r * (dyw - xhat * jnp.mean(dyw * xhat, axis=-1, keepdims=True))


def _mid_fwd(x, merged, w_out, w_pm, w_pf):
    S, D = x.shape
    tm = _pick(S, MM_ROWS)

    def body(x_ref, m_ref, wo_ref, wpm_ref, wpf_ref, mo_ref, x1_ref, h2_ref):
        mo = jnp.dot(m_ref[...], wo_ref[...], preferred_element_type=f32)
        mo_ref[...] = mo
        _, moh = _rms_parts(mo)
        x1 = x_ref[...] + moh * wpm_ref[...]
        x1_ref[...] = x1
        _, x1h = _rms_parts(x1)
        h2_ref[...] = (x1h * wpf_ref[...]).astype(bf16)

    row = pl.BlockSpec((tm, D), lambda i: (i, 0))
    vec = pl.BlockSpec((1, D), lambda i: (0, 0))
    return pl.pallas_call(
        body,
        grid=(S // tm,),
        in_specs=[row, pl.BlockSpec((tm, merged.shape[1]), lambda i: (i, 0)),
                  pl.BlockSpec(w_out.shape, lambda i: (0, 0)), vec, vec],
        out_specs=[row, row, row],
        out_shape=[SDS((S, D), f32), SDS((S, D), f32), SDS((S, D), bf16)],
        compiler_params=_cparams(("parallel",), VMEM_BIG),
        name="out_proj_mid_fwd",
    )(x, merged, w_out, w_pm, w_pf)


def _final(x1, act, w_down, tgt, w_pfn):
    S, D = x1.shape
    tm = _pick(S, MM_ROWS)
    nt = S // tm

    def body(x1_ref, a_ref, wd_ref, t_ref, w_ref, loss_ref, dy_ref, dfo_ref, gw_ref, lacc, gacc):
        i = pl.program_id(0)

        @pl.when(i == 0)
        def _():
            lacc[...] = jnp.zeros_like(lacc)
            gacc[...] = jnp.zeros_like(gacc)

        w = w_ref[...]
        r, foh = _rms_parts(jnp.dot(a_ref[...], wd_ref[...], preferred_element_type=f32))
        y = x1_ref[...] + foh * w
        err = y - t_ref[...]
        lacc[...] += _colsum8(err * err)
        dy = err * (1.0 / D)
        dy_ref[...] = dy
        gacc[...] += _colsum8(dy * foh)
        dfo_ref[...] = _rms_bwd(foh, r, w, dy).astype(bf16)

        @pl.when(i == nt - 1)
        def _():
            loss_ref[...] = jnp.full((SUBLANE, LANE), 0.5 / D, f32) * jnp.sum(lacc[...])
            gw_ref[...] = jnp.sum(gacc[...], axis=0, keepdims=True)

    row = pl.BlockSpec((tm, D), lambda i: (i, 0))
    vec = pl.BlockSpec((1, D), lambda i: (0, 0))
    return pl.pallas_call(
        body,
        grid=(nt,),
        in_specs=[row, pl.BlockSpec((tm, act.shape[1]), lambda i: (i, 0)),
                  pl.BlockSpec(w_down.shape, lambda i: (0, 0)), row, vec],
        out_specs=[pl.BlockSpec((SUBLANE, LANE), lambda i: (0, 0)), row, row, vec],
        out_shape=[SDS((SUBLANE, LANE), f32), SDS((S, D), f32), SDS((S, D), bf16), SDS((1, D), f32)],
        scratch_shapes=[pltpu.VMEM((SUBLANE, D), f32), pltpu.VMEM((SUBLANE, D), f32)],
        compiler_params=_cparams(("arbitrary",), VMEM_BIG),
        name="down_proj_final_loss",
    )(x1, act, w_down, tgt, w_pfn)


def _mid_bwd(dy, dh2, x1, mo, w_pf, w_pm):
    S, D = dy.shape
    tm = _pick(S, 512)
    nt = S // tm

    def body(dy_ref, dh2_ref, x1_ref, mo_ref, wpf_ref, wpm_ref, dx1_ref, dmo_ref, gpf_ref, gpm_ref, apf, apm):
        i = pl.program_id(0)

        @pl.when(i == 0)
        def _():
            apf[...] = jnp.zeros_like(apf)
            apm[...] = jnp.zeros_like(apm)

        r1, x1h = _rms_parts(x1_ref[...])
        dh2 = dh2_ref[...]
        apf[...] += _colsum8(dh2 * x1h)
        dx1 = dy_ref[...] + _rms_bwd(x1h, r1, wpf_ref[...], dh2)
        dx1_ref[...] = dx1
        rm, moh = _rms_parts(mo_ref[...])
        apm[...] += _colsum8(dx1 * moh)
        dmo_ref[...] = _rms_bwd(moh, rm, wpm_ref[...], dx1).astype(bf16)

        @pl.when(i == nt - 1)
        def _():
            gpf_ref[...] = jnp.sum(apf[...], axis=0, keepdims=True)
            gpm_ref[...] = jnp.sum(apm[...], axis=0, keepdims=True)

    row = pl.BlockSpec((tm, D), lambda i: (i, 0))
    vec = pl.BlockSpec((1, D), lambda i: (0, 0))
    return pl.pallas_call(
        body,
        grid=(nt,),
        in_specs=[row, row, row, row, vec, vec],
        out_specs=[row, row, vec, vec],
        out_shape=[SDS((S, D), f32), SDS((S, D), bf16), SDS((1, D), f32), SDS((1, D), f32)],
        scratch_shapes=[pltpu.VMEM((SUBLANE, D), f32), pltpu.VMEM((SUBLANE, D), f32)],
        compiler_params=_cparams(("arbitrary",)),
        name="mid_bwd",
    )(dy, dh2, x1, mo, w_pf, w_pm)


def _first_bwd(x, dx1, dh, w_pre):
    S, D = x.shape
    tm = _pick(S, 512)
    nt = S // tm

    def body(x_ref, dx1_ref, a_ref, w_ref, gx_ref, gw_ref, acc):
        i = pl.program_id(0)

        @pl.when(i == 0)
        def _():
            acc[...] = jnp.zeros_like(acc)

        r, xh = _rms_parts(x_ref[...])
        dh = a_ref[...]
        acc[...] += _colsum8(dh * xh)
        gx_ref[...] = dx1_ref[...] + _rms_bwd(xh, r, w_ref[...], dh)

        @pl.when(i == nt - 1)
        def _():
            gw_ref[...] = jnp.sum(acc[...], axis=0, keepdims=True)

    row = pl.BlockSpec((tm, D), lambda i: (i, 0))
    vec = pl.BlockSpec((1, D), lambda i: (0, 0))
    return pl.pallas_call(
        body,
        grid=(nt,),
        in_specs=[row, row, row, vec],
        out_specs=[row, vec],
        out_shape=[SDS((S, D), f32), SDS((1, D), f32)],
        scratch_shapes=[pltpu.VMEM((SUBLANE, D), f32)],
        compiler_params=_cparams(("arbitrary",)),
        name="first_bwd",
    )(x, dx1, dh, w_pre)


def _t5_bucket(dist):
    n = jnp.maximum(dist, 0)
    nf = jnp.maximum(n, 1).astype(f32)
    large = MAX_EXACT + (jnp.log(nf / MAX_EXACT) / math.log(MAX_DISTANCE / MAX_EXACT)
                         * (NUM_BUCKETS - MAX_EXACT)).astype(jnp.int32)
    large = jnp.minimum(large, NUM_BUCKETS - 1)
    return jnp.where(n < MAX_EXACT, n, large)


def _bias_consts(d):
    blk = ATTN_BLOCK
    rel = jnp.arange(blk)[:, None] + blk - jnp.arange(2 * blk)[None, :]
    in_win = (rel >= 0) & (rel <= blk)
    bucket = _t5_bucket(rel * d).reshape(1, -1)
    onehot = (bucket == jnp.arange(NUM_BUCKETS)[:, None]).astype(f32)
    return onehot, in_win.astype(f32).reshape(1, -1)


def _bias_build(tab_t, onehot, maskf, name, after):
    H = tab_t.shape[0]

    def body(t_ref, oh_ref, m_ref, after_ref, o_ref):
        b = jnp.dot(t_ref[...], oh_ref[...], precision=HIGHEST, preferred_element_type=f32)
        o_ref[...] = jnp.where(m_ref[...] > 0.5, b, NEG_INF)

    vm = pl.BlockSpec(memory_space=pltpu.VMEM)
    return pl.pallas_call(body, out_shape=SDS((H, onehot.shape[1]), f32), name=name,
                          in_specs=[vm, vm, vm, pl.BlockSpec(memory_space=pl.ANY)], out_specs=vm,
                          )(tab_t, onehot, maskf, after)


def _bias_grad(dbias_flat, onehot, name):
    H = dbias_flat.shape[0]

    def body(g_ref, oh_ref, o_ref):
        o_ref[...] = lax.dot_general(oh_ref[...], g_ref[...], NT, precision=HIGHEST, preferred_element_type=f32)

    return pl.pallas_call(body, out_shape=SDS((NUM_BUCKETS, H), f32), name=name)(dbias_flat, onehot)


ATTN_TILE = 512
ATTN_SUB = ATTN_TILE // ATTN_BLOCK
ATTN_HP = 4
ATTN_WIDE = ATTN_HP * LANE


def _qkv_specs(nt):
    tile = (ATTN_TILE, ATTN_WIDE)
    blk = (ATTN_BLOCK, ATTN_WIDE)
    sec = ATTN_OUT // ATTN_WIDE
    cur = lambda off: (lambda h, t: (jnp.minimum(t, nt - 1), off + h))
    prev = lambda off: (lambda h, t: (jnp.maximum(jnp.minimum(t, nt - 1) * ATTN_SUB - 1, 0), off + h))
    return [pl.BlockSpec(tile, cur(0)), pl.BlockSpec(blk, prev(sec)), pl.BlockSpec(tile, cur(sec)),
            pl.BlockSpec(blk, prev(2 * sec)), pl.BlockSpec(tile, cur(2 * sec))]


def _head_masks():
    lane = lax.broadcasted_iota(jnp.int32, (ATTN_BLOCK, LANE), 1)
    return lane < HEAD_DIM


def _stack_heads(x2, low):
    zero = jnp.zeros_like(x2)
    return jnp.concatenate([jnp.where(low, x2, zero), jnp.where(low, zero, x2)], axis=0)


def _attn_fwd(qkv, bias, bps, name, after=None):
    S = qkv.shape[0]
    nt = S // ATTN_TILE
    scale = HEAD_DIM ** -0.5

    def body(q_ref, kp_ref, kc_ref, vp_ref, vc_ref, b_ref, *rest):
        o_ref, l_ref = rest[-2:]
        t = pl.program_id(1)
        low = _head_masks()
        col = lax.broadcasted_iota(jnp.int32, (2 * ATTN_BLOCK, 2 * ATTN_BLOCK), 1)
        for hp in range(ATTN_HP):
            cols = slice(hp * LANE, (hp + 1) * LANE)
            kk = jnp.concatenate([kp_ref[:, cols], kc_ref[:, cols]], axis=0)
            vv = jnp.concatenate([vp_ref[:, cols], vc_ref[:, cols]], axis=0)
            bias2 = b_ref[2 * hp:2 * hp + 2].reshape(2 * ATTN_BLOCK, 2 * ATTN_BLOCK)
            for b in range(ATTN_SUB):
                lo = b * ATTN_BLOCK
                rows = slice(lo, lo + ATTN_BLOCK)
                keys = slice(lo, lo + 2 * ATTN_BLOCK)
                dead = jnp.logical_and((t * ATTN_SUB + b) % bps == 0, col < ATTN_BLOCK)
                q2 = _stack_heads(q_ref[rows, cols], low)
                kb, vb = kk[keys], vv[keys]
                s = lax.dot_general(q2, kb, NT, preferred_element_type=f32) * scale + bias2
                s = jnp.where(dead, NEG_INF, s)
                m = jnp.max(s, axis=-1, keepdims=True)
                p = jnp.exp(s - m)
                l = jnp.sum(p, axis=-1, keepdims=True)
                o2 = jnp.dot(p.astype(bf16), vb, preferred_element_type=f32) / l
                lse = m + jnp.log(l)
                o_ref[rows, cols] = jnp.where(low, o2[:ATTN_BLOCK], o2[ATTN_BLOCK:])
                l_ref[rows, cols] = jnp.where(low, lse[:ATTN_BLOCK], lse[ATTN_BLOCK:])

    tile = pl.BlockSpec((ATTN_TILE, ATTN_WIDE), lambda h, t: (t, h))
    return pl.pallas_call(
        body,
        grid=(4 // ATTN_HP, nt),
        in_specs=_qkv_specs(nt) + [pl.BlockSpec((2 * ATTN_HP, ATTN_BLOCK, 2 * ATTN_BLOCK), lambda h, t: (h, 0, 0))]
        + ([] if after is None else [pl.BlockSpec(memory_space=pl.ANY)]),
        out_specs=[tile, tile],
        out_shape=[SDS((S, ATTN_OUT), f32), SDS((S, ATTN_OUT), f32)],
        compiler_params=_cparams(("parallel", "parallel")),
        name=name,
    )(qkv, qkv, qkv, qkv, qkv, bias, *([] if after is None else [after]))


def _attn_bwd(qkv, bias, do, dvec, lse, bps, name):
    S = qkv.shape[0]
    nt = S // ATTN_TILE
    scale = HEAD_DIM ** -0.5

    def assemble(parts):
        rows = [parts[0][:ATTN_BLOCK]]
        for b in range(ATTN_SUB - 1):
            rows.append(parts[b][ATTN_BLOCK:] + parts[b + 1][:ATTN_BLOCK])
        rows.append(parts[-1][ATTN_BLOCK:])
        return rows

    def body(q_ref, kp_ref, kc_ref, vp_ref, vc_ref, b_ref, do_ref, dvec_ref, lse_ref,
             dq_ref, dk_ref, dv_ref, db_ref, ck, cv):
        t = pl.program_id(1)
        last = ATTN_TILE - ATTN_BLOCK

        @pl.when(t == 0)
        def _():
            ck[...] = jnp.zeros_like(ck)
            cv[...] = jnp.zeros_like(cv)
            db_ref[...] = jnp.zeros_like(db_ref)

        @pl.when(t < nt)
        def _():
            low = _head_masks()
            col = lax.broadcasted_iota(jnp.int32, (2 * ATTN_BLOCK, 2 * ATTN_BLOCK), 1)
            per_row = lambda t2: jnp.concatenate([t2[:, 0:1], t2[:, HEAD_DIM:HEAD_DIM + 1]], axis=0)
            for hp in range(ATTN_HP):
                cols = slice(hp * LANE, (hp + 1) * LANE)
                kk = jnp.concatenate([kp_ref[:, cols], kc_ref[:, cols]], axis=0)
                vv = jnp.concatenate([vp_ref[:, cols], vc_ref[:, cols]], axis=0)
                bias2 = b_ref[2 * hp:2 * hp + 2].reshape(2 * ATTN_BLOCK, 2 * ATTN_BLOCK)
                dk_parts, dv_parts = [], []
                dsum = None
                for b in range(ATTN_SUB):
                    lo = b * ATTN_BLOCK
                    rows = slice(lo, lo + ATTN_BLOCK)
                    keys = slice(lo, lo + 2 * ATTN_BLOCK)
                    dead = jnp.logical_and((t * ATTN_SUB + b) % bps == 0, col < ATTN_BLOCK)
                    q2 = _stack_heads(q_ref[rows, cols], low)
                    do2 = _stack_heads(do_ref[rows, cols].astype(bf16), low)
                    kb, vb = kk[keys], vv[keys]
                    s = lax.dot_general(q2, kb, NT, preferred_element_type=f32) * scale + bias2
                    s = jnp.where(dead, NEG_INF, s)
                    p = jnp.exp(s - per_row(lse_ref[rows, cols]))
                    dp = lax.dot_general(do2, vb, NT, preferred_element_type=f32)
                    ds = p * (dp - per_row(dvec_ref[rows, cols]))
                    dsum = ds if dsum is None else dsum + ds
                    dsb = ds.astype(bf16)
                    dq2 = jnp.dot(dsb, kb, preferred_element_type=f32) * scale
                    dq_ref[rows, cols] = jnp.where(low, dq2[:ATTN_BLOCK], dq2[ATTN_BLOCK:]).astype(bf16)
                    dk_parts.append(lax.dot_general(dsb, q2, TN, preferred_element_type=f32) * scale)
                    dv_parts.append(lax.dot_general(p.astype(bf16), do2, TN, preferred_element_type=f32))
                db_ref[2 * hp:2 * hp + 2] += dsum.reshape(2, ATTN_BLOCK, 2 * ATTN_BLOCK)
                for parts, carry, out_ref in ((dk_parts, ck, dk_ref), (dv_parts, cv, dv_ref)):
                    rws = assemble(parts)
                    out_ref[:last, cols] = carry[:last, cols].astype(bf16)
                    out_ref[last:, cols] = (carry[last:, cols] + rws[0]).astype(bf16)
                    for b in range(ATTN_SUB):
                        carry[b * ATTN_BLOCK:(b + 1) * ATTN_BLOCK, cols] = rws[b + 1]

        @pl.when(t == nt)
        def _():
            dk_ref[...] = ck[...].astype(bf16)
            dv_ref[...] = cv[...].astype(bf16)

    tile = (ATTN_TILE, ATTN_WIDE)
    cur = pl.BlockSpec(tile, lambda h, t: (jnp.minimum(t, nt - 1), h))
    lag = pl.BlockSpec(tile, lambda h, t: (jnp.maximum(t - 1, 0), h))
    bspec = pl.BlockSpec((2 * ATTN_HP, ATTN_BLOCK, 2 * ATTN_BLOCK), lambda h, t: (h, 0, 0))
    return pl.pallas_call(
        body,
        grid=(4 // ATTN_HP, nt + 1),
        in_specs=_qkv_specs(nt) + [bspec, cur, cur, cur],
        out_specs=[cur, lag, lag, bspec],
        out_shape=[SDS((S, ATTN_OUT), bf16), SDS((S, ATTN_OUT), bf16), SDS((S, ATTN_OUT), bf16),
                   SDS((8, ATTN_BLOCK, 2 * ATTN_BLOCK), f32)],
        scratch_shapes=[pltpu.VMEM(tile, f32), pltpu.VMEM(tile, f32)],
        compiler_params=_cparams(("parallel", "arbitrary")),
        name=name,
    )(qkv, qkv, qkv, qkv, qkv, bias, do, dvec, lse)


def _attn_merge(o0, o1, o2, l0, l1, l2):
    S, W = o0.shape
    R = PERM_ROWS

    def body(o0_ref, o1_ref, o2_ref, l0_ref, l1_ref, l2_ref, y_ref, yb_ref, w0_ref, w1_ref, w2_ref,
             so1, so2, sl1, sl2):
        _to_natural(o1_ref, so1, 4)
        _to_natural(l1_ref, sl1, 4)
        _to_natural(o2_ref, so2, 16)
        _to_natural(l2_ref, sl2, 16)
        a, b, c = l0_ref[...], sl1[...], sl2[...]
        m = jnp.maximum(jnp.maximum(a, b), c)
        ea, eb, ec = jnp.exp(a - m), jnp.exp(b - m), jnp.exp(c - m)
        den = (ea + eb) + ec
        w0, w1, w2 = ea / den, eb / den, ec / den
        y = (w0 * o0_ref[...] + w1 * so1[...]) + w2 * so2[...]
        y_ref[...] = y
        yb_ref[...] = y.astype(bf16)
        w0_ref[...] = w0
        w1_ref[...] = w1
        w2_ref[...] = w2

    nat = pl.BlockSpec((R, LANE), lambda i, j: (i, j))
    v4 = lambda t: t.reshape(4, S // 4, W)
    v16 = lambda t: t.reshape(16, S // 16, W)
    return pl.pallas_call(
        body,
        grid=(S // R, W // LANE),
        in_specs=[nat, _perm_spec(4), _perm_spec(16)] * 2,
        out_specs=[nat] * 5,
        out_shape=[SDS((S, W), f32), SDS((S, W), bf16)] + [SDS((S, W), f32)] * 3,
        scratch_shapes=[pltpu.VMEM((R, LANE), f32)] * 4,
        compiler_params=_cparams(("parallel", "parallel")),
        name="attn_merge",
    )(o0, v4(o1), v16(o2), l0, v4(l1), v16(l2))


def _attn_merge_bwd(dy, y, w0, w1, w2, after=None):
    S, W = dy.shape
    R = PERM_ROWS

    def body(dy_ref, y_ref, w0_ref, w1_ref, w2_ref, *rest):
        a0, a1, a2, b0, b1, b2, sa, sb = rest[-8:]
        dyv = dy_ref[...]
        r = lax.broadcasted_iota(jnp.int32, (LANE, LANE), 0) // HEAD_DIM
        c = lax.broadcasted_iota(jnp.int32, (LANE, LANE), 1) // HEAD_DIM
        seg = jnp.where(r == c, 1.0, 0.0).astype(f32)
        cbar = jnp.dot(dyv * y_ref[...], seg, precision=HIGHEST, preferred_element_type=f32)
        w = w0_ref[...]
        a0[...] = (w * dyv).astype(bf16)
        b0[...] = w * cbar
        for d, w_ref, a_ref, b_ref in ((4, w1_ref, a1, b1), (16, w2_ref, a2, b2)):
            w = w_ref[...]
            sa[...] = w * dyv
            sb[...] = w * cbar
            n = R // d
            for k in range(d):
                rows = pl.ds(k, n, stride=d)
                a_ref[k] = sa[rows, :].astype(bf16)
                b_ref[k] = sb[rows, :]

    nat = pl.BlockSpec((R, LANE), lambda i, j: (i, j))
    shapes = lambda dt: [SDS((S, W), dt), SDS((4, S // 4, W), dt), SDS((16, S // 16, W), dt)]
    outs = pl.pallas_call(
        body,
        grid=(S // R, W // LANE),
        in_specs=[nat] * 5 + ([] if after is None else [pl.BlockSpec(memory_space=pl.ANY)]),
        out_specs=[nat, _perm_spec(4), _perm_spec(16)] * 2,
        out_shape=shapes(bf16) + shapes(f32),
        scratch_shapes=[pltpu.VMEM((R, LANE), f32)] * 2,
        compiler_params=_cparams(("parallel", "parallel")),
        name="attn_merge_bwd",
    )(dy, y, w0, w1, w2, *([] if after is None else [after]))
    return [t.reshape(S, W) for t in outs]


HGRN_SB = 256
HGRN_PAIR = 4


def _chunk_masks():
    r = jnp.arange(HGRN_SB)[:, None]
    c = jnp.arange(HGRN_SB)[None, :]
    same = (r // HGRN_CHUNK) == (c // HGRN_CHUNK)
    return jnp.stack([same & (c <= r), same, same & (c >= r)]).astype(bf16)


def _mask_dot(mask, x):
    hi = x.astype(bf16)
    r1 = x - hi.astype(f32)
    mid = r1.astype(bf16)
    lo = (r1 - mid.astype(f32)).astype(bf16)
    p = jnp.dot(mask, jnp.concatenate([hi, mid, lo], axis=1), preferred_element_type=f32)
    n = x.shape[1]
    return (p[:, :n] + p[:, n:2 * n]) + p[:, 2 * n:]


def _hgrn_prep(q_raw, f_raw, lbv, tril, same):
    sq = _sigmoid(q_raw)
    qs = q_raw * sq
    sig = _sigmoid(f_raw)
    f = lbv + (1.0 - lbv) * sig
    g = jnp.log(f)
    k = 1.0 - f
    G = _mask_dot(tril, g)
    GL = _mask_dot(same, g)
    eG = jnp.exp(G)
    einv = jnp.exp(-G)
    edec = jnp.exp(GL - G)
    return dict(sq=sq, qs=qs, sig=sig, f=f, k=k, eG=eG, einv=einv, edec=edec, eGL=jnp.exp(GL),
                qt=qs * eG, kt=k * einv, kd=k * edec)


def _ride_split(ride, rest, n_out, n_scratch):
    if ride is None:
        return None, rest[:n_out], None, rest[n_out:], None
    return rest[0], rest[1:1 + n_out], rest[1 + n_out], rest[2 + n_out:2 + n_out + n_scratch], rest[2 + n_out + n_scratch:]


def _hgrn_fwd(hg, lb, normw, ride=None):
    S = hg.shape[0]
    sb = HGRN_SB
    nsb = S // sb
    nch = sb // HGRN_CHUNK

    def body(q_ref, f_ref, v_ref, og_ref, lb_ref, nw_ref, m_ref, *rest):
        src_ref, (y_ref, o_ref, ck_ref), got_ref, (st,), sems = _ride_split(ride, rest, 3, 1)
        j = pl.program_id(1)
        if ride is not None:
            @pl.when(jnp.logical_and(pl.program_id(0) == 0, j == 0))
            def _():
                _chip_start(src_ref, got_ref, sems[0], sems[1], ride[1])

        @pl.when(j == 0)
        def _():
            st[...] = jnp.zeros_like(st)

        tril_m = m_ref[0]
        tril = tril_m.astype(f32) > 0.5

        def one_head(hh):
            cols = slice(hh * LANE, (hh + 1) * LANE)
            ST = st[hh]
            ck_ref[hh, 0] = ST
            pr = _hgrn_prep(q_ref[:, cols], f_ref[:, cols], lb_ref[:, cols], tril_m, m_ref[1])
            qtb, ktb, kdb = pr["qt"].astype(bf16), pr["kt"].astype(bf16), pr["kd"].astype(bf16)
            eGL = pr["eGL"]
            vb = v_ref[:, cols].astype(bf16)
            A = jnp.where(tril, lax.dot_general(qtb, ktb, NT, preferred_element_type=f32), 0.0)
            o = jnp.dot(A.astype(bf16), vb, preferred_element_type=f32)
            outs = []
            for ci in range(nch):
                lo = ci * HGRN_CHUNK
                sl = slice(lo, lo + HGRN_CHUNK)
                outs.append(o[sl] + lax.dot_general(qtb[sl], ST.astype(bf16), NT, preferred_element_type=f32))
                ST = ST * eGL[lo:lo + 1, :] + lax.dot_general(vb[sl], kdb[sl], TN, preferred_element_type=f32)
            st[hh] = ST
            of = jnp.concatenate(outs, axis=0)
            o_ref[:, cols] = of
            rms = lax.rsqrt(jnp.mean(of * of, axis=-1, keepdims=True) + EPS)
            ogv = og_ref[:, cols]
            y_ref[:, cols] = ((of * rms * nw_ref[...]) * (ogv * _sigmoid(ogv))).astype(bf16)

        for hh in range(HGRN_PAIR):
            one_head(hh)

        if ride is not None:
            @pl.when(jnp.logical_and(pl.program_id(0) == ngrp - 1, j == nsb - 1))
            def _():
                _chip_finish(src_ref, got_ref, sems[0], sems[1], ride[1])

    wide = HGRN_PAIR * LANE
    ngrp = 4 // HGRN_PAIR
    col = lambda off: pl.BlockSpec((sb, wide), lambda h, j: (j, off // HGRN_PAIR + h))
    riding = ride is not None
    res = pl.pallas_call(
        body,
        grid=(ngrp, nsb),
        in_specs=[col(0), col(4), col(8), col(12), pl.BlockSpec((1, wide), lambda h, j: (0, h)),
                  pl.BlockSpec((1, LANE), lambda h, j: (0, 0)),
                  pl.BlockSpec((3, sb, sb), lambda h, j: (0, 0, 0))] + ([_ANY] if riding else []),
        out_specs=[col(0), col(0), pl.BlockSpec((HGRN_PAIR, 1, LANE, LANE), lambda h, j: (h, j, 0, 0))]
        + ([_ANY] if riding else []),
        out_shape=[SDS((S, HGRN_W), bf16), SDS((S, HGRN_W), f32), SDS((4, nsb, LANE, LANE), f32)]
        + ([_chip_out_shape(*ride)] if riding else []),
        scratch_shapes=[pltpu.VMEM((HGRN_PAIR, LANE, LANE), f32)] + (list(_CHIP_SEMS) if riding else []),
        compiler_params=_cparams(("arbitrary", "arbitrary") if riding else ("parallel", "arbitrary")),
        name="hgrn_fwd",
    )(hg, hg, hg, hg, lb, normw, _chunk_masks(), *([ride[0]] if riding else []))
    return tuple(res) if riding else (*res, None)


def _hgrn_bwd(hg, o_raw, dy, ck, lb, normw, ride=None):
    S = hg.shape[0]
    sb = HGRN_SB
    nsb = S // sb
    nch = sb // HGRN_CHUNK

    def body(q_ref, f_ref, v_ref, og_ref, o_ref, dy_ref, ck_ref, lb_ref, nw_ref, m_ref, *rest):
        src_ref, outs, got_ref, (dst, alb, anw), sems = _ride_split(ride, rest, 6, 3)
        dq_ref, df_ref, dv_ref, dog_ref, glb_ref, gnw_ref = outs
        j = pl.program_id(1)
        if ride is not None:
            @pl.when(jnp.logical_and(pl.program_id(0) == 0, j == 0))
            def _():
                _chip_start(src_ref, got_ref, sems[0], sems[1], ride[1])

        @pl.when(j == 0)
        def _():
            dst[...] = jnp.zeros_like(dst)
            alb[...] = jnp.zeros_like(alb)
            anw[...] = jnp.zeros_like(anw)

        tril_m = m_ref[0]
        tril = tril_m.astype(f32) > 0.5
        nw = nw_ref[...]

        def one_head(hh):
            cols = slice(hh * LANE, (hh + 1) * LANE)
            lbv = lb_ref[:, cols]
            q_raw = q_ref[:, cols]
            pr = _hgrn_prep(q_raw, f_ref[:, cols], lbv, tril_m, m_ref[1])
            qt, kt, kd, eGL = pr["qt"], pr["kt"], pr["kd"], pr["eGL"]
            qtb, ktb, kdb = qt.astype(bf16), kt.astype(bf16), kd.astype(bf16)
            vb = v_ref[:, cols].astype(bf16)

            o = o_ref[:, cols]
            ogv = og_ref[:, cols]
            sog = _sigmoid(ogv)
            rms = lax.rsqrt(jnp.mean(o * o, axis=-1, keepdims=True) + EPS)
            oh = o * rms
            dyv = dy_ref[:, cols]
            dog_ref[:, cols] = (dyv * (oh * nw) * (sog * (1.0 + ogv * (1.0 - sog)))).astype(bf16)
            dohw = dyv * (ogv * sog)
            anw[:, cols] += _colsum8(dohw * oh)
            doh = dohw * nw
            do = rms * (doh - oh * jnp.mean(doh * oh, axis=-1, keepdims=True))
            dob = do.astype(bf16)

            Ab = jnp.where(tril, lax.dot_general(qtb, ktb, NT, preferred_element_type=f32), 0.0).astype(bf16)
            dAb = jnp.where(tril, lax.dot_general(dob, vb, NT, preferred_element_type=f32), 0.0).astype(bf16)
            dv_acc = lax.dot_general(Ab, dob, TN, preferred_element_type=f32)
            dqt = jnp.dot(dAb, ktb, preferred_element_type=f32)
            dkt = lax.dot_general(dAb, qtb, TN, preferred_element_type=f32)

            ST = ck_ref[hh, 0]
            states = []
            for ci in range(nch):
                lo = ci * HGRN_CHUNK
                sl = slice(lo, lo + HGRN_CHUNK)
                states.append(ST)
                ST = ST * eGL[lo:lo + 1, :] + lax.dot_general(vb[sl], kdb[sl], TN, preferred_element_type=f32)

            dST = dst[hh]
            dqt_i, dkd_i, dv_i, deg_i = [None] * nch, [None] * nch, [None] * nch, [None] * nch
            for ci in reversed(range(nch)):
                lo = ci * HGRN_CHUNK
                sl = slice(lo, lo + HGRN_CHUNK)
                ST0 = states[ci]
                dSTb = dST.astype(bf16)
                dv_i[ci] = lax.dot_general(kdb[sl], dSTb, NT, preferred_element_type=f32)
                dqt_i[ci] = jnp.dot(dob[sl], ST0.astype(bf16), preferred_element_type=f32)
                dkd_i[ci] = jnp.dot(vb[sl], dSTb, preferred_element_type=f32)
                deg_i[ci] = jnp.broadcast_to(jnp.sum(dST * ST0, axis=0, keepdims=True), (HGRN_CHUNK, LANE))
                dST = dST * eGL[lo:lo + 1, :] + lax.dot_general(dob[sl], qtb[sl], TN, preferred_element_type=f32)
            dst[hh] = dST

            dqt = dqt + jnp.concatenate(dqt_i, axis=0)
            dkd = jnp.concatenate(dkd_i, axis=0)
            dv_ref[:, cols] = (dv_acc + jnp.concatenate(dv_i, axis=0)).astype(bf16)
            deg = jnp.concatenate(deg_i, axis=0)

            dqs = dqt * pr["eG"]
            dkdkd = dkd * kd
            dG = dqt * qt - dkt * kt - dkdkd
            dk = dkt * pr["einv"] + dkd * pr["edec"]
            dGL = _mask_dot(m_ref[1], dkdkd) + eGL * deg
            dg = _mask_dot(m_ref[2], dG) + dGL
            df = dg / pr["f"] - dk
            sig = pr["sig"]
            df_ref[:, cols] = (df * (1.0 - lbv) * (sig * (1.0 - sig))).astype(bf16)
            alb[:, cols] += _colsum8(df * (1.0 - sig))
            sq = pr["sq"]
            dq_ref[:, cols] = (dqs * (sq * (1.0 + q_raw * (1.0 - sq)))).astype(bf16)

        for hh in range(HGRN_PAIR):
            one_head(hh)

        @pl.when(j == nsb - 1)
        def _():
            glb_ref[...] = jnp.broadcast_to(jnp.sum(alb[...], axis=0, keepdims=True), (SUBLANE, wide))
            gnw_ref[...] = jnp.broadcast_to(jnp.sum(anw[...], axis=0, keepdims=True), (SUBLANE, wide))

        if ride is not None:
            @pl.when(jnp.logical_and(pl.program_id(0) == ngrp - 1, j == nsb - 1))
            def _():
                _chip_finish(src_ref, got_ref, sems[0], sems[1], ride[1])

    wide = HGRN_PAIR * LANE
    ngrp = 4 // HGRN_PAIR
    rev = lambda off: pl.BlockSpec((sb, wide), lambda h, j: (nsb - 1 - j, off // HGRN_PAIR + h))
    stat = pl.BlockSpec((SUBLANE, wide), lambda h, j: (0, h))
    riding = ride is not None
    res = pl.pallas_call(
        body,
        grid=(ngrp, nsb),
        in_specs=[rev(0), rev(4), rev(8), rev(12), rev(0), rev(0),
                  pl.BlockSpec((HGRN_PAIR, 1, LANE, LANE), lambda h, j: (h, nsb - 1 - j, 0, 0)),
                  pl.BlockSpec((1, wide), lambda h, j: (0, h)), pl.BlockSpec((1, LANE), lambda h, j: (0, 0)),
                  pl.BlockSpec((3, sb, sb), lambda h, j: (0, 0, 0))]
        + ([_ANY] if riding else []),
        out_specs=[rev(0), rev(0), rev(0), rev(0), stat, stat] + ([_ANY] if riding else []),
        out_shape=[SDS((S, HGRN_W), bf16)] * 4 + [SDS((SUBLANE, HGRN_W), f32)] * 2
        + ([_chip_out_shape(*ride)] if riding else []),
        scratch_shapes=[pltpu.VMEM((HGRN_PAIR, LANE, LANE), f32), pltpu.VMEM((SUBLANE, wide), f32),
                        pltpu.VMEM((SUBLANE, wide), f32)] + (list(_CHIP_SEMS) if riding else []),
        compiler_params=_cparams(("arbitrary", "arbitrary") if riding else ("parallel", "arbitrary")),
        name="hgrn_bwd",
    )(hg, hg, hg, hg, o_raw, dy, ck, lb, normw, _chunk_masks(), *([ride[0]] if riding else []))
    return tuple(res) if riding else (*res, None)


def _lb_fwd(raw):
    def body(r_ref, o_ref):
        r = r_ref[...]
        m = jnp.max(r, axis=0, keepdims=True)
        e = jnp.exp(r - m)
        o_ref[...] = (e / jnp.sum(e, axis=0, keepdims=True))[0:1]

    return pl.pallas_call(body, out_shape=SDS((1, raw.shape[1]), f32), name="lb_fwd")(raw)


def _lb_bwd(raw, dlb):
    def body(r_ref, d_ref, o_ref):
        r = r_ref[...]
        m = jnp.max(r, axis=0, keepdims=True)
        e = jnp.exp(r - m)
        s = e / jnp.sum(e, axis=0, keepdims=True)
        s0 = s[0:1]
        onehot0 = jnp.where(lax.broadcasted_iota(jnp.int32, r.shape, 0) == 0, 1.0, 0.0)
        o_ref[...] = d_ref[...] * s0 * (onehot0 - s)

    return pl.pallas_call(body, out_shape=SDS(raw.shape, f32), name="lb_bwd")(raw, dlb)


def _gate_fwd(a, b, gc):
    S, D = a.shape
    tm = _pick(S, 512)

    def body(a_ref, b_ref, g0_ref, g1_ref, o_ref):
        s0, s1 = _sigmoid(g0_ref[...].astype(f32)), _sigmoid(g1_ref[...].astype(f32))
        o_ref[...] = (s0 * a_ref[...].astype(f32) + s1 * b_ref[...].astype(f32)).astype(bf16)

    row = pl.BlockSpec((tm, D), lambda i: (i, 0))
    return pl.pallas_call(
        body,
        grid=(S // tm,),
        in_specs=[row, row, row, pl.BlockSpec((tm, D), lambda i: (i, 1))],
        out_specs=row,
        out_shape=SDS((S, D), bf16),
        compiler_params=_cparams(("parallel",)),
        name="gate_fwd",
    )(a, b, gc, gc)


def _gate_bwd(dm, a, b, gc):
    S, D = a.shape
    tm = _pick(S, 512)

    def body(dm_ref, a_ref, b_ref, g0_ref, g1_ref, da_ref, db_ref, dg_ref):
        dmv = dm_ref[...].astype(f32)
        s0, s1 = _sigmoid(g0_ref[...].astype(f32)), _sigmoid(g1_ref[...].astype(f32))
        da_ref[...] = (dmv * s0).astype(bf16)
        db_ref[...] = (dmv * s1).astype(bf16)
        dg_ref[:, :D] = (dmv * a_ref[...].astype(f32) * (s0 * (1.0 - s0))).astype(bf16)
        dg_ref[:, D:] = (dmv * b_ref[...].astype(f32) * (s1 * (1.0 - s1))).astype(bf16)

    row = pl.BlockSpec((tm, D), lambda i: (i, 0))
    wide = pl.BlockSpec((tm, 2 * D), lambda i: (i, 0))
    return pl.pallas_call(
        body,
        grid=(S // tm,),
        in_specs=[row, row, row, row, pl.BlockSpec((tm, D), lambda i: (i, 1))],
        out_specs=[row, row, wide],
        out_shape=[SDS((S, D), bf16), SDS((S, D), bf16), SDS((S, 2 * D), bf16)],
        compiler_params=_cparams(("parallel",)),
        name="gate_bwd",
    )(dm, a, b, gc, gc)


CONV_ROWS = 512
INV_SQRT2 = 0.7071067811865476
INV_SQRT_2PI = 0.3989422804014327


CONV_HALO = 16


def _shift_down(cur, prev, k):
    x = pltpu.roll(cur, k, 0)
    row = lax.broadcasted_iota(jnp.int32, (SUBLANE, LANE), 0)
    head = jnp.where(row < k, pltpu.roll(prev, k, 0)[:SUBLANE], x[:SUBLANE])
    return jnp.concatenate([head, x[SUBLANE:]], axis=0)


def _shift_up(cur, nxt, k):
    R = cur.shape[0]
    x = pltpu.roll(cur, R - k, 0)
    row = lax.broadcasted_iota(jnp.int32, (SUBLANE, LANE), 0)
    tail = jnp.where(row >= SUBLANE - k, pltpu.roll(nxt, SUBLANE - k, 0), x[R - SUBLANE:])
    return jnp.concatenate([x[:R - SUBLANE], tail], axis=0)


def _conv_rows(u_ref, w, b, r0, first):
    R = CONV_ROWS
    cur = u_ref[pl.ds(r0, R), :].astype(f32)
    prev = u_ref[pl.ds(pl.multiple_of(jnp.maximum(r0 - CONV_HALO, 0), CONV_HALO), CONV_HALO), :].astype(f32)
    prev = jnp.where(first, 0.0, prev)
    x1 = _shift_down(cur, prev, 1)
    x2 = _shift_down(cur, prev, 2)
    c = ((b + w[0:1] * x2) + w[1:2] * x1) + w[2:3] * cur
    return c, x2, x1, cur


def _conv_fwd(ug, uv, wg, wv, bg, bv):
    S, F = ug.shape
    nchunk = S // CONV_ROWS

    def body(ug_ref, uv_ref, wg_ref, wv_ref, bg_ref, bv_ref, o_ref):
        wgv, wvv, bgv, bvv = wg_ref[...], wv_ref[...], bg_ref[...], bv_ref[...]

        def step(ci, carry):
            r0 = pl.multiple_of(ci * CONV_ROWS, CONV_ROWS)
            cg = _conv_rows(ug_ref, wgv, bgv, r0, ci == 0)[0]
            cv = _conv_rows(uv_ref, wvv, bvv, r0, ci == 0)[0]
            gelu = 0.5 * cg * (1.0 + lax.erf(cg * INV_SQRT2))
            o_ref[pl.ds(r0, CONV_ROWS), :] = (gelu * cv).astype(bf16)
            return carry

        lax.fori_loop(0, nchunk, step, 0)

    col = pl.BlockSpec((S, LANE), lambda j: (0, j))
    w3 = pl.BlockSpec((3, LANE), lambda j: (0, j))
    b1 = pl.BlockSpec((1, LANE), lambda j: (0, j))
    return pl.pallas_call(
        body,
        grid=(F // LANE,),
        in_specs=[col, col, w3, w3, b1, b1],
        out_specs=col,
        out_shape=SDS((S, F), bf16),
        compiler_params=_cparams(("parallel",), VMEM_BIG),
        name="conv_fwd",
    )(ug, uv, wg, wv, bg, bv)


def _conv_bwd(ug, uv, dact, wg, wv, bg, bv):
    S, F = ug.shape
    R = CONV_ROWS
    nchunk = S // R

    def body(ug_ref, uv_ref, da_ref, wg_ref, wv_ref, bg_ref, bv_ref, dug_ref, duv_ref, sg_ref, sv_ref, dcg, dcv):
        wgv, wvv, bgv, bvv = wg_ref[...], wv_ref[...], bg_ref[...], bv_ref[...]
        zero = jnp.zeros((SUBLANE, LANE), f32)

        def fwd_step(ci, acc):
            r0 = pl.multiple_of(ci * R, R)
            cg, g2, g1, g0 = _conv_rows(ug_ref, wgv, bgv, r0, ci == 0)
            cv, v2, v1, v0 = _conv_rows(uv_ref, wvv, bvv, r0, ci == 0)
            da = da_ref[pl.ds(r0, R), :].astype(f32)
            cdf = 0.5 * (1.0 + lax.erf(cg * INV_SQRT2))
            pdf = INV_SQRT_2PI * jnp.exp(-0.5 * cg * cg)
            dg = da * cv * (cdf + cg * pdf)
            dv = da * (cg * cdf)
            dcg[pl.ds(r0, R), :] = dg
            dcv[pl.ds(r0, R), :] = dv
            new = (acc[0] + _colsum8(dg * g2), acc[1] + _colsum8(dg * g1), acc[2] + _colsum8(dg * g0),
                   acc[3] + _colsum8(dg),
                   acc[4] + _colsum8(dv * v2), acc[5] + _colsum8(dv * v1), acc[6] + _colsum8(dv * v0),
                   acc[7] + _colsum8(dv))
            return new

        acc = lax.fori_loop(0, nchunk, fwd_step, (zero,) * 8)
        rows = lax.broadcasted_iota(jnp.int32, (SUBLANE, LANE), 0)

        def stats(parts):
            out = jnp.zeros((SUBLANE, LANE), f32)
            for k, pt in enumerate(parts):
                out = jnp.where(rows == k, jnp.sum(pt, axis=0, keepdims=True), out)
            return out

        sg_ref[...] = stats(acc[0:4])
        sv_ref[...] = stats(acc[4:8])

        def du_rows(dc, w, r0, last):
            cur = dc[pl.ds(r0, R), :]
            nxt = dc[pl.ds(pl.multiple_of(jnp.minimum(r0 + R, S - SUBLANE), SUBLANE), SUBLANE), :]
            nxt = jnp.where(last, 0.0, nxt)
            return w[2:3] * cur + w[1:2] * _shift_up(cur, nxt, 1) + w[0:1] * _shift_up(cur, nxt, 2)

        def bwd_step(ci, carry):
            r0 = pl.multiple_of(ci * R, R)
            last = ci == nchunk - 1
            dug_ref[pl.ds(r0, R), :] = du_rows(dcg, wgv, r0, last).astype(bf16)
            duv_ref[pl.ds(r0, R), :] = du_rows(dcv, wvv, r0, last).astype(bf16)
            return carry

        lax.fori_loop(0, nchunk, bwd_step, 0)

    col = pl.BlockSpec((S, LANE), lambda j: (0, j))
    w3 = pl.BlockSpec((3, LANE), lambda j: (0, j))
    b1 = pl.BlockSpec((1, LANE), lambda j: (0, j))
    st = pl.BlockSpec((SUBLANE, LANE), lambda j: (0, j))
    return pl.pallas_call(
        body,
        grid=(F // LANE,),
        in_specs=[col, col, col, w3, w3, b1, b1],
        out_specs=[col, col, st, st],
        out_shape=[SDS((S, F), bf16), SDS((S, F), bf16), SDS((SUBLANE, F), f32), SDS((SUBLANE, F), f32)],
        scratch_shapes=[pltpu.VMEM((S, LANE), f32), pltpu.VMEM((S, LANE), f32)],
        compiler_params=_cparams(("parallel",), VMEM_BIG),
        name="conv_bwd",
    )(ug, uv, dact, wg, wv, bg, bv)


def _adam_math(w, g, m, v):
    m = ADAM_B1 * m + (1.0 - ADAM_B1) * g
    v = ADAM_B2 * v + (1.0 - ADAM_B2) * (g * g)
    m_hat = m / (1.0 - ADAM_B1 ** ADAM_STEP)
    v_hat = v / (1.0 - ADAM_B2 ** ADAM_STEP)
    delta = -ADAM_LR * (m_hat / (jnp.sqrt(v_hat) + ADAM_EPS) + ADAM_WD * w)
    return delta, m, v


def _adamw(w, m, v, g, name):
    R, C = w.shape
    parts = g.ndim == 3
    tr = R
    if R % 16 == 0:
        for t in range(R, 0, -16):
            if R % t == 0 and t * C * 4 <= ADAM_BLOCK_BYTES:
                tr = t
                break

    def body(w_ref, m_ref, v_ref, g_ref, go_ref, d_ref, mo_ref, vo_ref):
        if parts:
            gv = ((g_ref[0].astype(f32) + g_ref[1].astype(f32)) + g_ref[2].astype(f32)) + g_ref[3].astype(f32)
        else:
            gv = g_ref[...]
        go_ref[...] = gv
        d, mn, vn = _adam_math(w_ref[...], gv, m_ref[...], v_ref[...])
        d_ref[...] = d
        mo_ref[...] = mn
        vo_ref[...] = vn

    row = pl.BlockSpec((tr, C), lambda i: (i, 0))
    gspec = pl.BlockSpec((4, tr, C), lambda i: (0, i, 0)) if parts else row
    return pl.pallas_call(
        body,
        grid=(R // tr,),
        in_specs=[row, row, row, gspec],
        out_specs=[row] * 4,
        out_shape=[SDS((R, C), f32)] * 4,
        compiler_params=_cparams(("parallel",), VMEM_BIG),
        name=name,
    )(w, m, v, g)


def _sum8(parts, name):
    _, _, R, C = parts.shape

    def body(p_ref, o_ref):
        acc = p_ref[0, 0]
        for c in range(2):
            for k in range(4):
                if c or k:
                    acc = acc + p_ref[c, k]
        o_ref[...] = acc

    return pl.pallas_call(body, out_shape=SDS((R, C), f32), name=name)(parts)


def _pair_add(by_core, b, name):
    _, K, R, C = by_core.shape
    tr = R // 2 if R % 32 == 0 else R

    def body(c_ref, a_ref, b_ref, o_ref):
        o_ref[...] = (a_ref[0].astype(f32) + b_ref[...].astype(f32)).astype(bf16)

    blk = pl.BlockSpec((1, tr, C), lambda k, i, c: (k, i, 0))
    return pl.pallas_call(
        body,
        grid_spec=pltpu.PrefetchScalarGridSpec(
            num_scalar_prefetch=1,
            grid=(K, R // tr),
            in_specs=[pl.BlockSpec((1, 1, tr, C), lambda k, i, c: (c[0], k, i, 0)), blk],
            out_specs=blk,
        ),
        out_shape=SDS((K, R, C), bf16),
        compiler_params=_cparams(("parallel", "parallel")),
        name=name,
    )(lax.axis_index("c").astype(jnp.int32).reshape(1), by_core, b)


_ANY = pl.BlockSpec(memory_space=pl.ANY)


def _chip_copies(src_ref, out_ref, send_sems, recv_sems, gather):
    x, y, c = lax.axis_index("x"), lax.axis_index("y"), lax.axis_index("c")
    mine = 2 * x + y

    def piece(k):
        return src_ref if gather else src_ref.at[k]

    sends, recvs = [], []
    for j, (px, py) in enumerate([(1 - x, y), (x, 1 - y), (1 - x, 1 - y)]):
        sends.append(pltpu.make_async_remote_copy(
            src_ref=piece(2 * px + py), dst_ref=out_ref.at[mine], send_sem=send_sems.at[j],
            recv_sem=recv_sems.at[j], device_id=(px, py, c), device_id_type=MESH))
        recvs.append(pltpu.make_async_remote_copy(
            src_ref=piece(mine), dst_ref=out_ref.at[2 * px + py], send_sem=send_sems.at[j],
            recv_sem=recv_sems.at[j], device_id=(px, py, c), device_id_type=MESH))
    return sends, recvs


def _chip_start(src_ref, out_ref, send_sems, recv_sems, gather):
    for cp in _chip_copies(src_ref, out_ref, send_sems, recv_sems, gather)[0]:
        cp.start()


def _chip_finish(src_ref, out_ref, send_sems, recv_sems, gather):
    sends, recvs = _chip_copies(src_ref, out_ref, send_sems, recv_sems, gather)
    for cp in recvs:
        cp.wait_recv()
    for cp in sends:
        cp.wait_send()


def _chip_out_shape(src, gather):
    return SDS((4,) + tuple(src.shape if gather else src.shape[1:]), src.dtype)


_CHIP_SEMS = [pltpu.SemaphoreType.DMA((3,)), pltpu.SemaphoreType.DMA((3,))]


def _fill_own(out, src, gather):
    mine = 2 * lax.axis_index("x") + lax.axis_index("y")
    own = src if gather else lax.dynamic_index_in_dim(src, mine, axis=0, keepdims=False)
    return lax.dynamic_update_index_in_dim(out, own, mine, axis=0)


def _chip_comm(src, gather, name):
    def body(src_ref, out_ref, send_sems, recv_sems):
        _chip_start(src_ref, out_ref, send_sems, recv_sems, gather)
        _chip_finish(src_ref, out_ref, send_sems, recv_sems, gather)

    out = pl.pallas_call(
        body,
        in_specs=[_ANY],
        out_specs=_ANY,
        out_shape=_chip_out_shape(src, gather),
        scratch_shapes=list(_CHIP_SEMS),
        name=name,
    )(src)
    return _fill_own(out, src, gather)


_HBM = pl.BlockSpec(memory_space=pltpu.HBM)
_SEM = pl.BlockSpec(memory_space=pltpu.SEMAPHORE)
_EFFECT = pltpu.SideEffectType.DATAFLOW_SIDE_EFFECTING
_SPLIT_PEERS = {"chip_gather": 3, "chip_xchg": 3, "core_gather": 1, "core_swap": 1}


def _split_land(src, kind):
    if kind == "core_gather":
        return SDS((2,) + tuple(src.shape), src.dtype)
    if kind == "core_swap":
        return SDS(tuple(src.shape[1:]), src.dtype)
    return _chip_out_shape(src, kind == "chip_gather")


def _split_copies(src_ref, land_ref, sems, kind):
    x, y, c = lax.axis_index("x"), lax.axis_index("y"), lax.axis_index("c")
    n = _SPLIT_PEERS[kind]
    if kind == "core_gather":
        routes = [((x, y, 1 - c), src_ref, land_ref.at[c], land_ref.at[1 - c])]
    elif kind == "core_swap":
        routes = [((x, y, 1 - c), src_ref.at[1 - c], land_ref, land_ref)]
    else:
        mine = 2 * x + y
        gather = kind == "chip_gather"
        routes = [((px, py, c), src_ref if gather else src_ref.at[2 * px + py], land_ref.at[mine],
                   land_ref.at[2 * px + py]) for px, py in [(1 - x, y), (x, 1 - y), (1 - x, 1 - y)]]
    sends, recvs = [], []
    for j, (peer, piece, there, here) in enumerate(routes):
        sends.append(pltpu.make_async_remote_copy(src_ref=piece, dst_ref=there, send_sem=sems[j],
                                                  recv_sem=sems[n + j], device_id=peer, device_id_type=MESH))
        recvs.append(pltpu.make_async_remote_copy(src_ref=piece, dst_ref=here, send_sem=sems[j],
                                                  recv_sem=sems[n + j], device_id=peer, device_id_type=MESH))
    return sends, recvs


def _split_start(src, kind, name, after=None):
    land = _split_land(src, kind)
    ns = 2 * _SPLIT_PEERS[kind]
    n_in = 2 if after is None else 3

    def body(*refs):
        src_ref, land_ref = refs[:2]
        outs = refs[n_in:]
        for cp in _split_copies(src_ref, land_ref, outs[:ns], kind)[0]:
            cp.start()
        token = outs[ns + 2]
        token[...] = jnp.zeros_like(token)

    res = pl.pallas_call(
        body,
        name=name,
        out_shape=(pltpu.SemaphoreType.DMA(()),) * ns
        + (pltpu.HBM(src.shape, src.dtype), pltpu.HBM(land.shape, land.dtype), SDS((SUBLANE, LANE), f32)),
        in_specs=(_HBM, _HBM) + (() if after is None else (_ANY,)),
        out_specs=(_SEM,) * ns + (_HBM, _HBM, pl.BlockSpec(memory_space=pltpu.VMEM)),
        input_output_aliases={0: ns, 1: ns + 1},
        compiler_params=pltpu.CompilerParams(has_side_effects=_EFFECT),
    )(pltpu.with_memory_space_constraint(src, pltpu.HBM),
      pltpu.with_memory_space_constraint(lax.empty(land.shape, land.dtype), pltpu.HBM),
      *(() if after is None else (after,)))
    return (res[:ns], res[ns], res[ns + 1]), res[ns + 2]


def _split_wait(state, after, kind, name):
    sems, src_thru, land_thru = state
    ns = 2 * _SPLIT_PEERS[kind]

    def body(src_ref, land_ref, *rest):
        sends, recvs = _split_copies(src_ref, land_ref, rest[:ns], kind)
        for cp in recvs:
            cp.wait_recv()
        for cp in sends:
            cp.wait_send()

    src_out, got = pl.pallas_call(
        body,
        name=name,
        out_shape=(pltpu.HBM(src_thru.shape, src_thru.dtype), pltpu.HBM(land_thru.shape, land_thru.dtype)),
        in_specs=(_HBM, _HBM) + (_SEM,) * ns + (_ANY,),
        out_specs=(_HBM, _HBM),
        input_output_aliases={0: 0, 1: 1},
        compiler_params=pltpu.CompilerParams(has_side_effects=_EFFECT),
    )(src_thru, land_thru, *sems, after)
    if kind == "core_swap":
        return got, src_out
    if kind == "core_gather":
        return lax.dynamic_update_index_in_dim(got, src_out, lax.axis_index("c"), axis=0)
    return _fill_own(got, src_out, kind == "chip_gather")


def _core_gather(src, name):
    def body(src_ref, out_ref, send_sem, recv_sem):
        x, y, c = lax.axis_index("x"), lax.axis_index("y"), lax.axis_index("c")
        cp = pltpu.make_async_remote_copy(src_ref=src_ref, dst_ref=out_ref.at[c], send_sem=send_sem,
                                          recv_sem=recv_sem, device_id=(x, y, 1 - c), device_id_type=MESH)
        cp.start()
        pltpu.make_async_remote_copy(src_ref=src_ref, dst_ref=out_ref.at[1 - c], send_sem=send_sem,
                                     recv_sem=recv_sem, device_id=(x, y, 1 - c), device_id_type=MESH).wait_recv()
        cp.wait_send()

    out = pl.pallas_call(
        body,
        in_specs=[_ANY],
        out_specs=_ANY,
        out_shape=SDS((2,) + tuple(src.shape), src.dtype),
        scratch_shapes=[pltpu.SemaphoreType.DMA, pltpu.SemaphoreType.DMA],
        name=name,
    )(src)
    return lax.dynamic_update_index_in_dim(out, src, lax.axis_index("c"), axis=0)


_PACK_A = (("w_in", (1088, 1024)),)
_PACK_B = (("w_ba", (512, 128)), ("w_bh", (512, 128)), ("w_out", (128, 1024)), ("w_up", (704, 1024)),
           ("w_down", (352, 1024)))
_PACK_SIZES = _PACK_A + _PACK_B
_TRANSPOSED = ("w_in", "w_up")


def _slab_rows(sizes):
    return sum(r * c for _, (r, c) in sizes) // D_MODEL


def _pack_rows(d, sizes):
    n = d[sizes[0][0]].shape[0]
    return jnp.concatenate([d[k].reshape(n, -1, D_MODEL) for k, _ in sizes], axis=1)


def _unpack_rows(slab, sizes):
    n = slab.shape[0]
    out, lo = {}, 0
    for key, (r, c) in sizes:
        rows = r * c // D_MODEL
        out[key] = slab[:, lo:lo + rows].reshape(n, r, c)
        lo += rows
    return out


def _by_core(gslab):
    return jnp.swapaxes(gslab.reshape((4, 2) + gslab.shape[1:]), 0, 1)


def _cols_to_full(t):
    return jnp.swapaxes(t, 0, 1).reshape(t.shape[1], -1)


def _full_to_cols(t):
    K = t.shape[0]
    return jnp.swapaxes(t.reshape(K, 8, -1), 0, 1)


_SMALL = (("pre_mix_norm", (1, 1024)), ("rel_bias", (32, 24)), ("hgrn_lb_raw", (2, 512)), ("hgrn_norm", (1, 128)),
          ("post_mix_norm", (1, 1024)), ("pre_ffn_norm", (1, 1024)), ("conv_b", (1, 5632)),
          ("post_ffn_norm", (1, 1024)))
_SMALL_ROWS = 96
_CONVW_ROWS = 136


_SMALL_USED = sum(r * c for _, (r, c) in _SMALL)


def _pack_small(d, extra=None):
    flat = jnp.concatenate([d[k].reshape(-1) for k, _ in _SMALL] + ([] if extra is None else [extra.reshape(-1)]))
    flat = jnp.pad(flat, (0, _SMALL_ROWS * LANE - flat.shape[0]))
    return flat.reshape(_SMALL_ROWS, LANE)


def _unpack_small(p):
    flat = p.reshape(-1)
    out, lo = {}, 0
    for k, shp in _SMALL:
        n = shp[0] * shp[1]
        out[k] = flat[lo:lo + n].reshape(shp)
        lo += n
    return out


def _local_step(x, tgt, P, plan):
    S = x.shape[0]
    P = dict(P)
    lb = _lb_fwd(P["hgrn_lb_raw"])
    hs = _prep(x, P["pre_mix_norm"], plan.start_token())
    h1 = hs[0]
    consts = [_bias_consts(d) for d in DILATIONS]
    biases, dep = [], h1
    for g in range(N_GROUPS):
        tab_t = P["rel_bias"][:, 8 * g:8 * g + 8].T
        dep = _bias_build(tab_t, consts[g][0], consts[g][1], f"bias_build{g}", dep)
        biases.append(dep.reshape(8, ATTN_BLOCK, 2 * ATTN_BLOCK))
    W = dict(plan.weights_a(dep))
    qkv = [_mm(hs[g], W["wt_qkv"][g], "nt", bf16, f"proj_qkv{g}") for g in range(N_GROUPS)]
    hg = _mm(h1, W["wt_hg"], "nt", f32, "proj_hg")
    gc = _mm(h1, W["wt_gate"], "nt", bf16, "proj_gate")
    obuf, lbuf, token = [], [], None
    for g, d in enumerate(DILATIONS):
        o_g, l_g = _attn_fwd(qkv[g], biases[g], (S // d) // ATTN_BLOCK, f"attn_fwd{g}", after=token)
        lbuf.append(l_g)
        obuf.append(o_g)
        if g == 0:
            token = plan.forward_b(o_g)
    y_attn, y_attn_b, w0, w1, w2 = _attn_merge(obuf[0], obuf[1], obuf[2], lbuf[0], lbuf[1], lbuf[2])
    y_hgrn, o_raw, ck, _ = _hgrn_fwd(hg, lb, P["hgrn_norm"])
    wb = plan.weights_b(y_hgrn)
    P["conv_w"] = wb.pop("conv_w")
    W.update(wb)
    a = _mm(y_attn_b, W["w_ba"], "nn", bf16, "branch_attn")
    b = _mm(y_hgrn, W["w_bh"], "nn", bf16, "branch_hgrn")
    merged = _gate_fwd(a, b, gc)
    mo, x1, h2 = _mid_fwd(x, merged, W["w_out"], P["post_mix_norm"], P["pre_ffn_norm"])
    ug = _mm(h2, W["wt_up_g"], "nt", bf16, "up_gate")
    uv = _mm(h2, W["wt_up_v"], "nt", bf16, "up_val")
    cw_g, cw_v = P["conv_w"][:, :D_FF], P["conv_w"][:, D_FF:]
    cb_g, cb_v = P["conv_b"][:, :D_FF], P["conv_b"][:, D_FF:]
    act = _conv_fwd(ug, uv, cw_g, cw_v, cb_g, cb_v)
    loss, dy, dfo, g_post_ffn = _final(x1, act, W["w_down"], tgt, P["post_ffn_norm"])
    dact = _mm(dfo, W["w_down"], "nt", bf16, "d_act")
    gW_down = _mm(act, dfo, "tn", bf16, "gw_down")
    dug, duv, st_g, st_v = _conv_bwd(ug, uv, dact, cw_g, cw_v, cb_g, cb_v)
    dh2 = _mm([dug, duv], [W["wt_up_g"], W["wt_up_v"]], "nn", f32, "dh2")
    gW_up_g = _mm(dug, h2, "tn", bf16, "gw_up_gate")
    gW_up_v = _mm(duv, h2, "tn", bf16, "gw_up_val")
    dx1, dmo, g_pre_ffn, g_post_mix = _mid_bwd(dy, dh2, x1, mo, P["pre_ffn_norm"], P["post_mix_norm"])
    dmerged = _mm(dmo, W["w_out"], "nt", bf16, "d_merged")
    gW_out = _mm(merged, dmo, "tn", bf16, "gw_out")
    da, db, dgc = _gate_bwd(dmerged, a, b, gc)
    dyattn = _mm(da, W["w_ba"], "nt", f32, "d_yattn")
    gW_ba = _mm(y_attn_b, da, "tn", bf16, "gw_ba")
    dyhgrn = _mm(db, W["w_bh"], "nt", f32, "d_yhgrn")
    gW_bh = _mm(y_hgrn, db, "tn", bf16, "gw_bh")
    big_b = dict(w_ba=gW_ba, w_bh=gW_bh, w_out=gW_out, w_up=[gW_up_g, gW_up_v], w_down=gW_down)
    dos = _attn_merge_bwd(dyattn, y_attn, w0, w1, w2, after=plan.grads_b_start(big_b))
    dq_h, df_h, dv_h, dog_h, glb8, gnw8, got_b = _hgrn_bwd(hg, o_raw, dyhgrn, ck, lb, P["hgrn_norm"],
                                                          plan.bwd_ride(dos[5]))
    dhg = [dq_h, df_h, dv_h, dog_h]
    g_lb_raw = _lb_bwd(P["hgrn_lb_raw"], glb8[0:1])
    gn = gnw8[0:1]
    g_hgrn_norm = (gn[:, 0:128] + gn[:, 128:256]) + (gn[:, 256:384] + gn[:, 384:512])
    dqkvs, gW_qkv, g_rel = [], [], []
    for g, d in enumerate(DILATIONS):
        dq, dk, dv, dbias = _attn_bwd(qkv[g], biases[g], dos[g], dos[3 + g], lbuf[g], (S // d) // ATTN_BLOCK,
                                      f"attn_bwd{g}")
        dqkvs.append([dq, dk, dv])
        gW_qkv.append(_mm(dqkvs[g], hs[g], "tn", bf16, f"gw_qkv{g}"))
        g_rel.append(_bias_grad(dbias.reshape(8, -1), consts[g][0], f"bias_grad{g}"))
    gW_hg = _mm(dhg, h1, "tn", bf16, "gw_hg")
    gW_gate = _mm(dgc, h1, "tn", bf16, "gw_gate")
    gW_in = gW_qkv + [gW_hg, gW_gate]
    token = plan.grads_a_start(gW_in)
    dh_perm = [_mm(dqkvs[g], W["wt_qkv"][g], "nn", f32, f"dh1_qkv{g}", after=token) for g in (1, 2)]
    token = plan.grads_a_exchange(dh_perm[1])
    dh_main = _mm(dqkvs[0] + dhg + [dgc], [W["wt_qkv"][0], W["wt_hg"], W["wt_gate"]], "nn", f32, "dh1_main",
                  after=token)
    grad_x, g_pre_mix = _first_bwd(x, dx1, _dh_sum(dh_main, dh_perm[0], dh_perm[1]), P["pre_mix_norm"])

    g_conv_w = jnp.concatenate([st_g[0:3], st_v[0:3]], axis=1)
    g_conv_b = jnp.concatenate([st_g[3:4], st_v[3:4]], axis=1)
    small = dict(pre_mix_norm=g_pre_mix, rel_bias=jnp.concatenate(g_rel, axis=1), hgrn_lb_raw=g_lb_raw,
                 hgrn_norm=g_hgrn_norm, post_mix_norm=g_post_mix, pre_ffn_norm=g_pre_ffn, conv_b=g_conv_b,
                 post_ffn_norm=g_post_ffn, conv_w=g_conv_w)
    return loss, grad_x, gW_in, big_b, got_b, small


def _weights_a(both):
    wt = jnp.swapaxes(both, 0, 1).reshape(-1, D_MODEL)
    return dict(
        wt_qkv=[wt[g * QKV_G:(g + 1) * QKV_G] for g in range(N_GROUPS)],
        wt_hg=wt[3 * QKV_G:3 * QKV_G + 4 * HGRN_W],
        wt_gate=wt[3 * QKV_G + 4 * HGRN_W:],
    )


def _weights_b(slabs):
    sh = _unpack_rows(slabs, _PACK_B)
    wt_up = sh["w_up"].reshape(-1, D_MODEL)
    return dict(
        w_ba=_cols_to_full(sh["w_ba"]),
        w_bh=_cols_to_full(sh["w_bh"]),
        w_out=sh["w_out"].reshape(D_MODEL, D_MODEL),
        wt_up_g=wt_up[:D_FF],
        wt_up_v=wt_up[D_FF:],
        w_down=sh["w_down"].reshape(D_FF, D_MODEL),
    )


def _dest_rows(sections, height):
    out = []
    for j in range(8):
        lo, hi, off, pieces = j * height, (j + 1) * height, 0, []
        for s in sections:
            a, b = max(lo, off), min(hi, off + s.shape[0])
            if a < b:
                pieces.append(s[a - off:b - off])
            off += s.shape[0]
        out.append(pieces[0] if len(pieces) == 1 else jnp.concatenate(pieces, axis=0))
    return out


def _grad_blocks_a(sections):
    rows = _dest_rows(sections, 1088)
    return jnp.stack([jnp.stack([rows[2 * k + c].astype(bf16) for k in range(4)]) for c in range(2)])


def _grad_slab_b(g):
    shards = dict(w_ba=_full_to_cols(g["w_ba"]), w_bh=_full_to_cols(g["w_bh"]), w_out=g["w_out"].reshape(8, 128, D_MODEL),
                  w_up=jnp.stack(_dest_rows(g["w_up"], 704)), w_down=g["w_down"].reshape(8, 352, D_MODEL))
    return _pack_rows({k: v.astype(bf16) for k, v in shards.items()}, _PACK_B)


_CONVW_SLAB_ROWS = 16


class _Traffic:
    def __init__(self, slab_a, slab_b, conv_w):
        hi = conv_w.astype(bf16)
        r1 = conv_w - hi.astype(f32)
        mid = r1.astype(bf16)
        lo = (r1 - mid.astype(f32)).astype(bf16)
        bits = jnp.stack([hi, mid, lo]).reshape(-1)
        tail = jnp.pad(bits, (0, _CONVW_SLAB_ROWS * D_MODEL - bits.shape[0])).reshape(_CONVW_SLAB_ROWS, D_MODEL)
        self.slab_b = jnp.concatenate([slab_b, tail], axis=0)
        self.state_a, tok = _split_start(slab_a, "chip_gather", "ag_a_start")
        self.state_b, self.token = _split_start(self.slab_b, "chip_gather", "ag_b_start", after=tok)
        self.chip_sum = None
        self.state = None

    def start_token(self):
        return self.token

    def weights_a(self, after):
        by_chip = _split_wait(self.state_a, after, "chip_gather", "ag_a_wait")
        return _weights_a(_core_gather(by_chip, "ag_a_cores"))

    def forward_b(self, after):
        by_chip = _split_wait(self.state_b, after, "chip_gather", "ag_b_wait")
        self.state, token = _split_start(by_chip, "core_gather", "ag_b_cores_start")
        return token

    def weights_b(self, after):
        both = _split_wait(self.state, after, "core_gather", "ag_b_cores_wait")
        slabs = jnp.swapaxes(both, 0, 1).reshape((8,) + tuple(self.slab_b.shape))
        rows = _slab_rows(_PACK_B)
        out = _weights_b(slabs[:, :rows])
        pieces = slabs[:, rows:].reshape(8, -1)[:, :3 * 3 * 704].reshape(8, 3, 3, 704).astype(f32)
        out["conv_w"] = _cols_to_full((pieces[:, 0] + pieces[:, 1]) + pieces[:, 2])
        return out

    def grads_b_start(self, grads):
        self.state, token = _split_start(_by_core(_grad_slab_b(grads)), "core_swap", "rs_b_cores_start")
        return token

    def bwd_ride(self, after):
        from_sib, by_core = _split_wait(self.state, after, "core_swap", "rs_b_cores_wait")
        self.chip_sum = _pair_add(by_core, from_sib, "rs_b_pair_add")
        return (self.chip_sum, False)

    def grads_a_start(self, sections):
        self.state, token = _split_start(_grad_blocks_a(sections), "core_swap", "rs_a_cores_start")
        return token

    def grads_a_exchange(self, after):
        from_sib, by_core = _split_wait(self.state, after, "core_swap", "rs_a_cores_wait")
        self.state, token = _split_start(_pair_add(by_core, from_sib, "rs_a_pair_add"), "chip_xchg", "rs_a_start")
        return token

    def parts(self, got_b, after):
        parts = _unpack_rows(_fill_own(got_b, self.chip_sum, False), _PACK_B)
        parts["w_in"] = _split_wait(self.state, after, "chip_xchg", "rs_a_wait")
        return parts


def kernel(x, pre_mix_norm, w_in, rel_bias, hgrn_lb_raw, hgrn_norm, w_branch_attn, w_branch_hgrn, w_out, post_mix_norm, pre_ffn_norm, w_up, conv_w, conv_b, w_down, post_ffn_norm, loss_target, m_pre_mix_norm, m_w_in, m_rel_bias, m_hgrn_lb_raw, m_hgrn_norm, m_w_branch_attn, m_w_branch_hgrn, m_w_out, m_post_mix_norm, m_pre_ffn_norm, m_w_up, m_conv_w, m_conv_b, m_w_down, m_post_ffn_norm, v_pre_mix_norm, v_w_in, v_rel_bias, v_hgrn_lb_raw, v_hgrn_norm, v_w_branch_attn, v_w_branch_hgrn, v_w_out, v_post_mix_norm, v_pre_ffn_norm, v_w_up, v_conv_w, v_conv_b, v_w_down, v_post_ffn_norm):
    ci = lax.axis_index("c")
    dev = 4 * lax.axis_index("x") + 2 * lax.axis_index("y") + ci
    tr = lambda t: jnp.swapaxes(t[0], 0, 1)
    wts = dict(w_in=tr(w_in), w_ba=w_branch_attn[0], w_bh=w_branch_hgrn[0], w_out=w_out[0], w_up=tr(w_up),
               w_down=w_down[0])
    mom = dict(w_in=tr(m_w_in), w_ba=m_w_branch_attn[0], w_bh=m_w_branch_hgrn[0], w_out=m_w_out[0], w_up=tr(m_w_up),
               w_down=m_w_down[0])
    var = dict(w_in=tr(v_w_in), w_ba=v_w_branch_attn[0], w_bh=v_w_branch_hgrn[0], w_out=v_w_out[0], w_up=tr(v_w_up),
               w_down=v_w_down[0])
    small_w = dict(pre_mix_norm=pre_mix_norm, rel_bias=rel_bias, hgrn_lb_raw=hgrn_lb_raw, hgrn_norm=hgrn_norm,
                   post_mix_norm=post_mix_norm, pre_ffn_norm=pre_ffn_norm, conv_b=conv_b, post_ffn_norm=post_ffn_norm)
    small_m = dict(pre_mix_norm=m_pre_mix_norm, rel_bias=m_rel_bias, hgrn_lb_raw=m_hgrn_lb_raw, hgrn_norm=m_hgrn_norm,
                   post_mix_norm=m_post_mix_norm, pre_ffn_norm=m_pre_ffn_norm, conv_b=m_conv_b,
                   post_ffn_norm=m_post_ffn_norm)
    small_v = dict(pre_mix_norm=v_pre_mix_norm, rel_bias=v_rel_bias, hgrn_lb_raw=v_hgrn_lb_raw, hgrn_norm=v_hgrn_norm,
                   post_mix_norm=v_post_mix_norm, pre_ffn_norm=v_pre_ffn_norm, conv_b=v_conv_b,
                   post_ffn_norm=v_post_ffn_norm)

    plan = _Traffic(wts["w_in"].astype(bf16),
                    _pack_rows({k: wts[k].astype(bf16)[None] for k, _ in _PACK_B}, _PACK_B)[0], conv_w[0])

    loss8, grad_x, _, _, got_b, small = _local_step(x[0], loss_target[0], small_w, plan)
    parts = plan.parts(got_b, grad_x)
    outs_big = {}
    for k, _ in _PACK_SIZES:
        outs_big[k] = _adamw(wts[k], mom[k], var[k], parts[k], "adamw_" + k)

    spack = jnp.concatenate([_pack_small(small, loss8[0, 0:1]),
                             jnp.pad(small["conv_w"].reshape(-1, LANE), ((0, _CONVW_ROWS - 132), (0, 0)))], axis=0)
    allp = _core_gather(_chip_comm(spack, True, "ag_small_chips"), "ag_small_cores")
    ssum = _sum8(allp, "small_sum")
    gs = ssum[:_SMALL_ROWS]
    loss = ssum[_SMALL_USED // LANE, _SMALL_USED % LANE]
    res_small = _adamw(_pack_small(small_w), _pack_small(small_m), _pack_small(small_v), gs, "adamw_small")
    sm = [_unpack_small(t) for t in res_small]
    g_cw_full = ssum[_SMALL_ROWS:_SMALL_ROWS + 132].reshape(3, 2 * D_FF)
    g_cw = lax.dynamic_slice_in_dim(g_cw_full, dev * 704, 704, axis=1)
    res_cw = _adamw(conv_w[0], m_conv_w[0], v_conv_w[0], g_cw, "adamw_conv_w")

    def pick(i):
        def big_(k):
            t = outs_big[k][i]
            return (jnp.swapaxes(t, 0, 1) if k in _TRANSPOSED else t)[None]
        return [sm[i]["pre_mix_norm"], big_("w_in"), sm[i]["rel_bias"], sm[i]["hgrn_lb_raw"], sm[i]["hgrn_norm"],
                big_("w_ba"), big_("w_bh"), big_("w_out"), sm[i]["post_mix_norm"], sm[i]["pre_ffn_norm"],
                big_("w_up"), res_cw[i][None], sm[i]["conv_b"], big_("w_down"), sm[i]["post_ffn_norm"]]

    return (loss, grad_x[None], *pick(0), *pick(1), *pick(2), *pick(3))
```

```python
import functools
import math

import jax
import jax.numpy as jnp
from jax import lax
from jax.experimental import pallas as pl
from jax.experimental.pallas import tpu as pltpu

f32 = jnp.float32
bf16 = jnp.bfloat16
SDS = jax.ShapeDtypeStruct
HIGHEST = lax.Precision.HIGHEST
MESH = pl.DeviceIdType.MESH

NN = (((1,), (0,)), ((), ()))
NT = (((1,), (1,)), ((), ()))
TN = (((0,), (0,)), ((), ()))

D_MODEL = 1024
N_GROUPS = 3
DILATIONS = (1, 4, 16)
HEAD_DIM = 64
ATTN_BLOCK = 128
QKV_G = 1536
ATTN_OUT = 512
HGRN_W = 512
HGRN_CHUNK = 32
D_FF = 2816
NUM_BUCKETS = 32
MAX_EXACT = 16
MAX_DISTANCE = 2048
NEG_INF = -1e30
EPS = 1e-6
LANE = 128
SUBLANE = 8
VMEM_BIG = 48 * 1024 * 1024
MM_ROWS = 512
MM_OUT_BYTES = 8 * 1024 * 1024
ADAM_BLOCK_BYTES = 2304 * 1024

ADAM_LR, ADAM_B1, ADAM_B2, ADAM_EPS, ADAM_WD, ADAM_STEP = 0.001, 0.9, 0.999, 1e-08, 0.01, 10


def _pick(n, pref):
    t = pref
    while t >= LANE:
        if n % t == 0:
            return t
        t //= 2
    return n


def _cparams(sem=None, vmem=None):
    kw = {}
    if sem is not None:
        kw["dimension_semantics"] = sem
    if vmem is not None:
        kw["vmem_limit_bytes"] = vmem
    return pltpu.CompilerParams(**kw)


def _sigmoid(x):
    return jax.nn.sigmoid(x)


def _colsum8(x):
    return x.reshape(x.shape[0] // SUBLANE, SUBLANE, x.shape[1]).sum(axis=0)


def _mm(a, b, mode, out_dtype, name, acc=None, after=None):
    dims = {"nn": NN, "nt": NT, "tn": TN}[mode]
    has_acc = acc is not None
    parts = list(a) if isinstance(a, (list, tuple)) else [a]
    if mode == "tn":
        assert not has_acc
        K, N = b.shape
        widths = [t.shape[1] for t in parts]
        M = sum(widths)
        whole = M * N * 4 <= MM_OUT_BYTES
        assert whole or len(parts) == 1
        tmm = M if whole else M // 2
        ts = _pick(K, 4 * MM_ROWS)
        nk = K // ts

        npart = len(parts)
        narrow = out_dtype != f32

        def body_tn(*refs):
            b_ref, o_ref = refs[npart], refs[npart + 1]
            acc_ref = refs[npart + 2] if narrow else o_ref
            k = pl.program_id(1)
            bv = b_ref[...]
            lo = 0
            for a_ref, w in zip(refs[:npart], widths if whole else [tmm]):
                part = lax.dot_general(a_ref[...], bv, dims, preferred_element_type=f32)
                rows = slice(lo, lo + w)
                lo += w

                @pl.when(k == 0)
                def _(part=part, rows=rows):
                    acc_ref[rows, :] = part

                @pl.when(k > 0)
                def _(part=part, rows=rows):
                    acc_ref[rows, :] += part

            if narrow:
                @pl.when(k == nk - 1)
                def _():
                    o_ref[...] = acc_ref[...].astype(out_dtype)

        return pl.pallas_call(
            body_tn,
            grid=(M // tmm, nk),
            in_specs=[pl.BlockSpec((ts, w if whole else tmm), lambda i, k: (k, i)) for w in widths]
            + [pl.BlockSpec((ts, N), lambda i, k: (k, 0))],
            out_specs=pl.BlockSpec((tmm, N), lambda i, k: (i, 0)),
            out_shape=SDS((M, N), out_dtype),
            scratch_shapes=[pltpu.VMEM((tmm, N), f32)] if narrow else [],
            compiler_params=_cparams(("parallel", "arbitrary"), VMEM_BIG),
            name=name,
        )(*parts, b)

    bs = list(b) if isinstance(b, (list, tuple)) else [b]
    widths = [t.shape[1] for t in parts]
    M = parts[0].shape[0]
    kdim = 0 if mode == "nn" else 1
    N = bs[0].shape[1 - kdim]
    tm = _pick(M, MM_ROWS)
    npart, nb = len(parts), len(bs)
    place, bi, lo = [], 0, 0
    for w in widths:
        place.append((bi, lo))
        lo += w
        if lo == bs[bi].shape[kdim]:
            bi, lo = bi + 1, 0
    assert bi == nb and lo == 0

    def body(*refs):
        a_refs, b_refs = refs[:npart], refs[npart:npart + nb]
        c_ref = refs[npart + nb] if has_acc else None
        o_ref = refs[-1]
        part = None
        for a_ref, w, (bi, lo) in zip(a_refs, widths, place):
            b_ref = b_refs[bi]
            if w == bs[bi].shape[kdim]:
                bk = b_ref[...]
            else:
                bk = b_ref[:, lo:lo + w] if mode == "nt" else b_ref[lo:lo + w, :]
            t = lax.dot_general(a_ref[...], bk, dims, preferred_element_type=f32)
            part = t if part is None else part + t
        if has_acc:
            part = part + c_ref[...]
        o_ref[...] = part.astype(out_dtype)

    specs = [pl.BlockSpec((tm, w), lambda i: (i, 0)) for w in widths] \
        + [pl.BlockSpec(t.shape, lambda i: (0, 0)) for t in bs]
    args = parts + bs
    aliases = {}
    if has_acc:
        specs.append(pl.BlockSpec((tm, N), lambda i: (i, 0)))
        args.append(acc)
        aliases = {npart + nb: 0}
    if after is not None:
        specs.append(pl.BlockSpec(memory_space=pl.ANY))
        args.append(after)
    return pl.pallas_call(
        body,
        grid=(M // tm,),
        in_specs=specs,
        out_specs=pl.BlockSpec((tm, N), lambda i: (i, 0)),
        out_shape=SDS((M, N), out_dtype),
        input_output_aliases=aliases,
        compiler_params=_cparams(("parallel",), VMEM_BIG),
        name=name,
    )(*args)


PERM_ROWS = 1024


def _perm_spec(d, cols=LANE):
    return pl.BlockSpec((d, PERM_ROWS // d, cols), lambda i, j: (0, i, j))


def _to_natural(src_ref, dst_ref, d):
    n = src_ref.shape[1]
    for r in range(d):
        dst_ref[pl.ds(r, n, stride=d), :] = src_ref[r]


def _prep(x, w, after=None):
    S, D = x.shape
    R = PERM_ROWS
    nc = D // LANE
    n_in = nc + 1 + (after is not None)

    def body(*refs):
        x_refs, w_ref = refs[:nc], refs[nc]
        h_ref, h4_ref, h16_ref, rs = refs[n_in:]
        ssq = None
        for xr in x_refs:
            v = xr[...]
            t = jnp.sum(v * v, axis=-1, keepdims=True)
            ssq = t if ssq is None else ssq + t
        rinv = lax.rsqrt(ssq * (1.0 / D) + EPS)
        rs[...] = jnp.broadcast_to(rinv, (R, LANE))
        for j, xr in enumerate(x_refs):
            cols = slice(j * LANE, (j + 1) * LANE)
            wj = w_ref[:, cols]
            h_ref[:, cols] = ((xr[...] * rinv) * wj).astype(bf16)
            for d, o_ref in ((4, h4_ref), (16, h16_ref)):
                n = R // d
                for r in range(d):
                    rows = pl.ds(r, n, stride=d)
                    o_ref[r, :, cols] = ((xr[rows, :] * rs[rows, :]) * wj).astype(bf16)

    col = lambda j: pl.BlockSpec((R, LANE), lambda i, j=j: (i, j))
    h, h4, h16 = pl.pallas_call(
        body,
        grid=(S // R,),
        in_specs=[col(j) for j in range(nc)] + [pl.BlockSpec((1, D), lambda i: (0, 0))]
        + ([] if after is None else [pl.BlockSpec(memory_space=pl.ANY)]),
        out_specs=[pl.BlockSpec((R, D), lambda i: (i, 0)), pl.BlockSpec((4, R // 4, D), lambda i: (0, i, 0)),
                   pl.BlockSpec((16, R // 16, D), lambda i: (0, i, 0))],
        out_shape=[SDS((S, D), bf16), SDS((4, S // 4, D), bf16), SDS((16, S // 16, D), bf16)],
        scratch_shapes=[pltpu.VMEM((R, LANE), f32)],
        compiler_params=_cparams(("parallel",), VMEM_BIG),
        name="prep_norm_perm",
    )(*([x] * nc), w, *([] if after is None else [after]))
    return [h, h4.reshape(S, D), h16.reshape(S, D)]


def _dh_sum(a, b, c):
    S, D = a.shape
    R = PERM_ROWS

    def body(a_ref, b_ref, c_ref, o_ref, sb, sc):
        _to_natural(b_ref, sb, 4)
        _to_natural(c_ref, sc, 16)
        o_ref[...] = (a_ref[...] + sb[...]) + sc[...]

    nat = pl.BlockSpec((R, LANE), lambda i, j: (i, j))
    return pl.pallas_call(
        body,
        grid=(S // R, D // LANE),
        in_specs=[nat, _perm_spec(4), _perm_spec(16)],
        out_specs=nat,
        out_shape=SDS((S, D), f32),
        scratch_shapes=[pltpu.VMEM((R, LANE), f32)] * 2,
        compiler_params=_cparams(("parallel", "parallel")),
        name="dh_sum",
    )(a, b.reshape(4, S // 4, D), c.reshape(16, S // 16, D))


def _rms_parts(xv):
    r = lax.rsqrt(jnp.mean(xv * xv, axis=-1, keepdims=True) + EPS)
    return r, xv * r


def _rms_bwd(xhat, r, w, dy):
    dyw = dy * w
    return r * (dyw - xhat * jnp.mean(dyw * xhat, axis=-1, keepdims=True))


def _mid_fwd(x, merged, w_out, w_pm, w_pf):
    S, D = x.shape
    tm = _pick(S, MM_ROWS)

    def body(x_ref, m_ref, wo_ref, wpm_ref, wpf_ref, mo_ref, x1_ref, h2_ref):
        mo = jnp.dot(m_ref[...], wo_ref[...], preferred_element_type=f32)
        mo_ref[...] = mo
        _, moh = _rms_parts(mo)
        x1 = x_ref[...] + moh * wpm_ref[...]
        x1_ref[...] = x1
        _, x1h = _rms_parts(x1)
        h2_ref[...] = (x1h * wpf_ref[...]).astype(bf16)

    row = pl.BlockSpec((tm, D), lambda i: (i, 0))
    vec = pl.BlockSpec((1, D), lambda i: (0, 0))
    return pl.pallas_call(
        body,
        grid=(S // tm,),
        in_specs=[row, pl.BlockSpec((tm, merged.shape[1]), lambda i: (i, 0)),
                  pl.BlockSpec(w_out.shape, lambda i: (0, 0)), vec, vec],
        out_specs=[row, row, row],
        out_shape=[SDS((S, D), f32), SDS((S, D), f32), SDS((S, D), bf16)],
        compiler_params=_cparams(("parallel",), VMEM_BIG),
        name="out_proj_mid_fwd",
    )(x, merged, w_out, w_pm, w_pf)


def _final(x1, act, w_down, tgt, w_pfn):
    S, D = x1.shape
    tm = _pick(S, MM_ROWS)
    nt = S // tm

    def body(x1_ref, a_ref, wd_ref, t_ref, w_ref, loss_ref, dy_ref, dfo_ref, gw_ref, lacc, gacc):
        i = pl.program_id(0)

        @pl.when(i == 0)
        def _():
            lacc[...] = jnp.zeros_like(lacc)
            gacc[...] = jnp.zeros_like(gacc)

        w = w_ref[...]
        r, foh = _rms_parts(jnp.dot(a_ref[...], wd_ref[...], preferred_element_type=f32))
        y = x1_ref[...] + foh * w
        err = y - t_ref[...]
        lacc[...] += _colsum8(err * err)
        dy = err * (1.0 / D)
        dy_ref[...] = dy
        gacc[...] += _colsum8(dy * foh)
        dfo_ref[...] = _rms_bwd(foh, r, w, dy).astype(bf16)

        @pl.when(i == nt - 1)
        def _():
            loss_ref[...] = jnp.full((SUBLANE, LANE), 0.5 / D, f32) * jnp.sum(lacc[...])
            gw_ref[...] = jnp.sum(gacc[...], axis=0, keepdims=True)

    row = pl.BlockSpec((tm, D), lambda i: (i, 0))
    vec = pl.BlockSpec((1, D), lambda i: (0, 0))
    return pl.pallas_call(
        body,
        grid=(nt,),
        in_specs=[row, pl.BlockSpec((tm, act.shape[1]), lambda i: (i, 0)),
                  pl.BlockSpec(w_down.shape, lambda i: (0, 0)), row, vec],
        out_specs=[pl.BlockSpec((SUBLANE, LANE), lambda i: (0, 0)), row, row, vec],
        out_shape=[SDS((SUBLANE, LANE), f32), SDS((S, D), f32), SDS((S, D), bf16), SDS((1, D), f32)],
        scratch_shapes=[pltpu.VMEM((SUBLANE, D), f32), pltpu.VMEM((SUBLANE, D), f32)],
        compiler_params=_cparams(("arbitrary",), VMEM_BIG),
        name="down_proj_final_loss",
    )(x1, act, w_down, tgt, w_pfn)


def _mid_bwd(dy, dh2, x1, mo, w_pf, w_pm):
    S, D = dy.shape
    tm = _pick(S, 512)
    nt = S // tm

    def body(dy_ref, dh2_ref, x1_ref, mo_ref, wpf_ref, wpm_ref, dx1_ref, dmo_ref, gpf_ref, gpm_ref, apf, apm):
        i = pl.program_id(0)

        @pl.when(i == 0)
        def _():
            apf[...] = jnp.zeros_like(apf)
            apm[...] = jnp.zeros_like(apm)

        r1, x1h = _rms_parts(x1_ref[...])
        dh2 = dh2_ref[...]
        apf[...] += _colsum8(dh2 * x1h)
        dx1 = dy_ref[...] + _rms_bwd(x1h, r1, wpf_ref[...], dh2)
        dx1_ref[...] = dx1
        rm, moh = _rms_parts(mo_ref[...])
        apm[...] += _colsum8(dx1 * moh)
        dmo_ref[...] = _rms_bwd(moh, rm, wpm_ref[...], dx1).astype(bf16)

        @pl.when(i == nt - 1)
        def _():
            gpf_ref[...] = jnp.sum(apf[...], axis=0, keepdims=True)
            gpm_ref[...] = jnp.sum(apm[...], axis=0, keepdims=True)

    row = pl.BlockSpec((tm, D), lambda i: (i, 0))
    vec = pl.BlockSpec((1, D), lambda i: (0, 0))
    return pl.pallas_call(
        body,
        grid=(nt,),
        in_specs=[row, row, row, row, vec, vec],
        out_specs=[row, row, vec, vec],
        out_shape=[SDS((S, D), f32), SDS((S, D), bf16), SDS((1, D), f32), SDS((1, D), f32)],
        scratch_shapes=[pltpu.VMEM((SUBLANE, D), f32), pltpu.VMEM((SUBLANE, D), f32)],
        compiler_params=_cparams(("arbitrary",)),
        name="mid_bwd",
    )(dy, dh2, x1, mo, w_pf, w_pm)


def _first_bwd(x, dx1, dh, w_pre):
    S, D = x.shape
    tm = _pick(S, 512)
    nt = S // tm

    def body(x_ref, dx1_ref, a_ref, w_ref, gx_ref, gw_ref, acc):
        i = pl.program_id(0)

        @pl.when(i == 0)
        def _():
            acc[...] = jnp.zeros_like(acc)

        r, xh = _rms_parts(x_ref[...])
        dh = a_ref[...]
        acc[...] += _colsum8(dh * xh)
        gx_ref[...] = dx1_ref[...] + _rms_bwd(xh, r, w_ref[...], dh)

        @pl.when(i == nt - 1)
        def _():
            gw_ref[...] = jnp.sum(acc[...], axis=0, keepdims=True)

    row = pl.BlockSpec((tm, D), lambda i: (i, 0))
    vec = pl.BlockSpec((1, D), lambda i: (0, 0))
    return pl.pallas_call(
        body,
        grid=(nt,),
        in_specs=[row, row, row, vec],
        out_specs=[row, vec],
        out_shape=[SDS((S, D), f32), SDS((1, D), f32)],
        scratch_shapes=[pltpu.VMEM((SUBLANE, D), f32)],
        compiler_params=_cparams(("arbitrary",)),
        name="first_bwd",
    )(x, dx1, dh, w_pre)


def _t5_bucket(dist):
    n = jnp.maximum(dist, 0)
    nf = jnp.maximum(n, 1).astype(f32)
    large = MAX_EXACT + (jnp.log(nf / MAX_EXACT) / math.log(MAX_DISTANCE / MAX_EXACT)
                         * (NUM_BUCKETS - MAX_EXACT)).astype(jnp.int32)
    large = jnp.minimum(large, NUM_BUCKETS - 1)
    return jnp.where(n < MAX_EXACT, n, large)


def _bias_consts(d):
    blk = ATTN_BLOCK
    rel = jnp.arange(blk)[:, None] + blk - jnp.arange(2 * blk)[None, :]
    in_win = (rel >= 0) & (rel <= blk)
    bucket = _t5_bucket(rel * d).reshape(1, -1)
    onehot = (bucket == jnp.arange(NUM_BUCKETS)[:, None]).astype(f32)
    return onehot, in_win.astype(f32).reshape(1, -1)


def _bias_build(tab_t, onehot, maskf, name, after):
    H = tab_t.shape[0]

    def body(t_ref, oh_ref, m_ref, after_ref, o_ref):
        b = jnp.dot(t_ref[...], oh_ref[...], precision=HIGHEST, preferred_element_type=f32)
        o_ref[...] = jnp.where(m_ref[...] > 0.5, b, NEG_INF)

    vm = pl.BlockSpec(memory_space=pltpu.VMEM)
    return pl.pallas_call(body, out_shape=SDS((H, onehot.shape[1]), f32), name=name,
                          in_specs=[vm, vm, vm, pl.BlockSpec(memory_space=pl.ANY)], out_specs=vm,
                          )(tab_t, onehot, maskf, after)


def _bias_grad(dbias_flat, onehot, name):
    H = dbias_flat.shape[0]

    def body(g_ref, oh_ref, o_ref):
        o_ref[...] = lax.dot_general(oh_ref[...], g_ref[...], NT, precision=HIGHEST, preferred_element_type=f32)

    return pl.pallas_call(body, out_shape=SDS((NUM_BUCKETS, H), f32), name=name)(dbias_flat, onehot)


ATTN_TILE = 512
ATTN_SUB = ATTN_TILE // ATTN_BLOCK
ATTN_HP = 4
ATTN_WIDE = ATTN_HP * LANE


def _qkv_specs(nt):
    tile = (ATTN_TILE, ATTN_WIDE)
    blk = (ATTN_BLOCK, ATTN_WIDE)
    sec = ATTN_OUT // ATTN_WIDE
    cur = lambda off: (lambda h, t: (jnp.minimum(t, nt - 1), off + h))
    prev = lambda off: (lambda h, t: (jnp.maximum(jnp.minimum(t, nt - 1) * ATTN_SUB - 1, 0), off + h))
    return [pl.BlockSpec(tile, cur(0)), pl.BlockSpec(blk, prev(sec)), pl.BlockSpec(tile, cur(sec)),
            pl.BlockSpec(blk, prev(2 * sec)), pl.BlockSpec(tile, cur(2 * sec))]


def _head_masks():
    lane = lax.broadcasted_iota(jnp.int32, (ATTN_BLOCK, LANE), 1)
    return lane < HEAD_DIM


def _stack_heads(x2, low):
    zero = jnp.zeros_like(x2)
    return jnp.concatenate([jnp.where(low, x2, zero), jnp.where(low, zero, x2)], axis=0)


def _attn_fwd(qkv, bias, bps, name, after=None):
    S = qkv.shape[0]
    nt = S // ATTN_TILE
    scale = HEAD_DIM ** -0.5

    def body(q_ref, kp_ref, kc_ref, vp_ref, vc_ref, b_ref, *rest):
        o_ref, l_ref = rest[-2:]
        t = pl.program_id(1)
        low = _head_masks()
        col = lax.broadcasted_iota(jnp.int32, (2 * ATTN_BLOCK, 2 * ATTN_BLOCK), 1)
        for hp in range(ATTN_HP):
            cols = slice(hp * LANE, (hp + 1) * LANE)
            kk = jnp.concatenate([kp_ref[:, cols], kc_ref[:, cols]], axis=0)
            vv = jnp.concatenate([vp_ref[:, cols], vc_ref[:, cols]], axis=0)
            bias2 = b_ref[2 * hp:2 * hp + 2].reshape(2 * ATTN_BLOCK, 2 * ATTN_BLOCK)
            for b in range(ATTN_SUB):
                lo = b * ATTN_BLOCK
                rows = slice(lo, lo + ATTN_BLOCK)
                keys = slice(lo, lo + 2 * ATTN_BLOCK)
                dead = jnp.logical_and((t * ATTN_SUB + b) % bps == 0, col < ATTN_BLOCK)
                q2 = _stack_heads(q_ref[rows, cols], low)
                kb, vb = kk[keys], vv[keys]
                s = lax.dot_general(q2, kb, NT, preferred_element_type=f32) * scale + bias2
                s = jnp.where(dead, NEG_INF, s)
                m = jnp.max(s, axis=-1, keepdims=True)
                p = jnp.exp(s - m)
                l = jnp.sum(p, axis=-1, keepdims=True)
                o2 = jnp.dot(p.astype(bf16), vb, preferred_element_type=f32) / l
                lse = m + jnp.log(l)
                o_ref[rows, cols] = jnp.where(low, o2[:ATTN_BLOCK], o2[ATTN_BLOCK:])
                l_ref[rows, cols] = jnp.where(low, lse[:ATTN_BLOCK], lse[ATTN_BLOCK:])

    tile = pl.BlockSpec((ATTN_TILE, ATTN_WIDE), lambda h, t: (t, h))
    return pl.pallas_call(
        body,
        grid=(4 // ATTN_HP, nt),
        in_specs=_qkv_specs(nt) + [pl.BlockSpec((2 * ATTN_HP, ATTN_BLOCK, 2 * ATTN_BLOCK), lambda h, t: (h, 0, 0))]
        + ([] if after is None else [pl.BlockSpec(memory_space=pl.ANY)]),
        out_specs=[tile, tile],
        out_shape=[SDS((S, ATTN_OUT), f32), SDS((S, ATTN_OUT), f32)],
        compiler_params=_cparams(("parallel", "parallel")),
        name=name,
    )(qkv, qkv, qkv, qkv, qkv, bias, *([] if after is None else [after]))


def _attn_bwd(qkv, bias, do, dvec, lse, bps, name):
    S = qkv.shape[0]
    nt = S // ATTN_TILE
    scale = HEAD_DIM ** -0.5

    def assemble(parts):
        rows = [parts[0][:ATTN_BLOCK]]
        for b in range(ATTN_SUB - 1):
            rows.append(parts[b][ATTN_BLOCK:] + parts[b + 1][:ATTN_BLOCK])
        rows.append(parts[-1][ATTN_BLOCK:])
        return rows

    def body(q_ref, kp_ref, kc_ref, vp_ref, vc_ref, b_ref, do_ref, dvec_ref, lse_ref,
             dq_ref, dk_ref, dv_ref, db_ref, ck, cv):
        t = pl.program_id(1)
        last = ATTN_TILE - ATTN_BLOCK

        @pl.when(t == 0)
        def _():
            ck[...] = jnp.zeros_like(ck)
            cv[...] = jnp.zeros_like(cv)
            db_ref[...] = jnp.zeros_like(db_ref)

        @pl.when(t < nt)
        def _():
            low = _head_masks()
            col = lax.broadcasted_iota(jnp.int32, (2 * ATTN_BLOCK, 2 * ATTN_BLOCK), 1)
            per_row = lambda t2: jnp.concatenate([t2[:, 0:1], t2[:, HEAD_DIM:HEAD_DIM + 1]], axis=0)
            for hp in range(ATTN_HP):
                cols = slice(hp * LANE, (hp + 1) * LANE)
                kk = jnp.concatenate([kp_ref[:, cols], kc_ref[:, cols]], axis=0)
                vv = jnp.concatenate([vp_ref[:, cols], vc_ref[:, cols]], axis=0)
                bias2 = b_ref[2 * hp:2 * hp + 2].reshape(2 * ATTN_BLOCK, 2 * ATTN_BLOCK)
                dk_parts, dv_parts = [], []
                dsum = None
                for b in range(ATTN_SUB):
                    lo = b * ATTN_BLOCK
                    rows = slice(lo, lo + ATTN_BLOCK)
                    keys = slice(lo, lo + 2 * ATTN_BLOCK)
                    dead = jnp.logical_and((t * ATTN_SUB + b) % bps == 0, col < ATTN_BLOCK)
                    q2 = _stack_heads(q_ref[rows, cols], low)
                    do2 = _stack_heads(do_ref[rows, cols].astype(bf16), low)
                    kb, vb = kk[keys], vv[keys]
                    s = lax.dot_general(q2, kb, NT, preferred_element_type=f32) * scale + bias2
                    s = jnp.where(dead, NEG_INF, s)
                    p = jnp.exp(s - per_row(lse_ref[rows, cols]))
                    dp = lax.dot_general(do2, vb, NT, preferred_element_type=f32)
                    ds = p * (dp - per_row(dvec_ref[rows, cols]))
                    dsum = ds if dsum is None else dsum + ds
                    dsb = ds.astype(bf16)
                    dq2 = jnp.dot(dsb, kb, preferred_element_type=f32) * scale
                    dq_ref[rows, cols] = jnp.where(low, dq2[:ATTN_BLOCK], dq2[ATTN_BLOCK:]).astype(bf16)
                    dk_parts.append(lax.dot_general(dsb, q2, TN, preferred_element_type=f32) * scale)
                    dv_parts.append(lax.dot_general(p.astype(bf16), do2, TN, preferred_element_type=f32))
                db_ref[2 * hp:2 * hp + 2] += dsum.reshape(2, ATTN_BLOCK, 2 * ATTN_BLOCK)
                for parts, carry, out_ref in ((dk_parts, ck, dk_ref), (dv_parts, cv, dv_ref)):
                    rws = assemble(parts)
                    out_ref[:last, cols] = carry[:last, cols].astype(bf16)
                    out_ref[last:, cols] = (carry[last:, cols] + rws[0]).astype(bf16)
                    for b in range(ATTN_SUB):
                        carry[b * ATTN_BLOCK:(b + 1) * ATTN_BLOCK, cols] = rws[b + 1]

        @pl.when(t == nt)
        def _():
            dk_ref[...] = ck[...].astype(bf16)
            dv_ref[...] = cv[...].astype(bf16)

    tile = (ATTN_TILE, ATTN_WIDE)
    cur = pl.BlockSpec(tile, lambda h, t: (jnp.minimum(t, nt - 1), h))
    lag = pl.BlockSpec(tile, lambda h, t: (jnp.maximum(t - 1, 0), h))
    bspec = pl.BlockSpec((2 * ATTN_HP, ATTN_BLOCK, 2 * ATTN_BLOCK), lambda h, t: (h, 0, 0))
    return pl.pallas_call(
        body,
        grid=(4 // ATTN_HP, nt + 1),
        in_specs=_qkv_specs(nt) + [bspec, cur, cur, cur],
        out_specs=[cur, lag, lag, bspec],
        out_shape=[SDS((S, ATTN_OUT), bf16), SDS((S, ATTN_OUT), bf16), SDS((S, ATTN_OUT), bf16),
                   SDS((8, ATTN_BLOCK, 2 * ATTN_BLOCK), f32)],
        scratch_shapes=[pltpu.VMEM(tile, f32), pltpu.VMEM(tile, f32)],
        compiler_params=_cparams(("parallel", "arbitrary")),
        name=name,
    )(qkv, qkv, qkv, qkv, qkv, bias, do, dvec, lse)


def _attn_merge(o0, o1, o2, l0, l1, l2):
    S, W = o0.shape
    R = PERM_ROWS

    def body(o0_ref, o1_ref, o2_ref, l0_ref, l1_ref, l2_ref, y_ref, yb_ref, w0_ref, w1_ref, w2_ref,
             so1, so2, sl1, sl2):
        _to_natural(o1_ref, so1, 4)
        _to_natural(l1_ref, sl1, 4)
        _to_natural(o2_ref, so2, 16)
        _to_natural(l2_ref, sl2, 16)
        a, b, c = l0_ref[...], sl1[...], sl2[...]
        m = jnp.maximum(jnp.maximum(a, b), c)
        ea, eb, ec = jnp.exp(a - m), jnp.exp(b - m), jnp.exp(c - m)
        den = (ea + eb) + ec
        w0, w1, w2 = ea / den, eb / den, ec / den
        y = (w0 * o0_ref[...] + w1 * so1[...]) + w2 * so2[...]
        y_ref[...] = y
        yb_ref[...] = y.astype(bf16)
        w0_ref[...] = w0
        w1_ref[...] = w1
        w2_ref[...] = w2

    nat = pl.BlockSpec((R, LANE), lambda i, j: (i, j))
    v4 = lambda t: t.reshape(4, S // 4, W)
    v16 = lambda t: t.reshape(16, S // 16, W)
    return pl.pallas_call(
        body,
        grid=(S // R, W // LANE),
        in_specs=[nat, _perm_spec(4), _perm_spec(16)] * 2,
        out_specs=[nat] * 5,
        out_shape=[SDS((S, W), f32), SDS((S, W), bf16)] + [SDS((S, W), f32)] * 3,
        scratch_shapes=[pltpu.VMEM((R, LANE), f32)] * 4,
        compiler_params=_cparams(("parallel", "parallel")),
        name="attn_merge",
    )(o0, v4(o1), v16(o2), l0, v4(l1), v16(l2))


def _attn_merge_bwd(dy, y, w0, w1, w2, after=None):
    S, W = dy.shape
    R = PERM_ROWS

    def body(dy_ref, y_ref, w0_ref, w1_ref, w2_ref, *rest):
        a0, a1, a2, b0, b1, b2, sa, sb = rest[-8:]
        dyv = dy_ref[...]
        r = lax.broadcasted_iota(jnp.int32, (LANE, LANE), 0) // HEAD_DIM
        c = lax.broadcasted_iota(jnp.int32, (LANE, LANE), 1) // HEAD_DIM
        seg = jnp.where(r == c, 1.0, 0.0).astype(f32)
        cbar = jnp.dot(dyv * y_ref[...], seg, precision=HIGHEST, preferred_element_type=f32)
        w = w0_ref[...]
        a0[...] = (w * dyv).astype(bf16)
        b0[...] = w * cbar
        for d, w_ref, a_ref, b_ref in ((4, w1_ref, a1, b1), (16, w2_ref, a2, b2)):
            w = w_ref[...]
            sa[...] = w * dyv
            sb[...] = w * cbar
            n = R // d
            for k in range(d):
                rows = pl.ds(k, n, stride=d)
                a_ref[k] = sa[rows, :].astype(bf16)
                b_ref[k] = sb[rows, :]

    nat = pl.BlockSpec((R, LANE), lambda i, j: (i, j))
    shapes = lambda dt: [SDS((S, W), dt), SDS((4, S // 4, W), dt), SDS((16, S // 16, W), dt)]
    outs = pl.pallas_call(
        body,
        grid=(S // R, W // LANE),
        in_specs=[nat] * 5 + ([] if after is None else [pl.BlockSpec(memory_space=pl.ANY)]),
        out_specs=[nat, _perm_spec(4), _perm_spec(16)] * 2,
        out_shape=shapes(bf16) + shapes(f32),
        scratch_shapes=[pltpu.VMEM((R, LANE), f32)] * 2,
        compiler_params=_cparams(("parallel", "parallel")),
        name="attn_merge_bwd",
    )(dy, y, w0, w1, w2, *([] if after is None else [after]))
    return [t.reshape(S, W) for t in outs]


HGRN_SB = 256
HGRN_PAIR = 4


def _chunk_masks():
    r = jnp.arange(HGRN_SB)[:, None]
    c = jnp.arange(HGRN_SB)[None, :]
    same = (r // HGRN_CHUNK) == (c // HGRN_CHUNK)
    return jnp.stack([same & (c <= r), same, same & (c >= r)]).astype(bf16)


def _mask_dot(mask, x):
    hi = x.astype(bf16)
    r1 = x - hi.astype(f32)
    mid = r1.astype(bf16)
    lo = (r1 - mid.astype(f32)).astype(bf16)
    p = jnp.dot(mask, jnp.concatenate([hi, mid, lo], axis=1), preferred_element_type=f32)
    n = x.shape[1]
    return (p[:, :n] + p[:, n:2 * n]) + p[:, 2 * n:]


def _hgrn_prep(q_raw, f_raw, lbv, tril, same):
    sq = _sigmoid(q_raw)
    qs = q_raw * sq
    sig = _sigmoid(f_raw)
    f = lbv + (1.0 - lbv) * sig
    g = jnp.log(f)
    k = 1.0 - f
    G = _mask_dot(tril, g)
    GL = _mask_dot(same, g)
    eG = jnp.exp(G)
    einv = jnp.exp(-G)
    edec = jnp.exp(GL - G)
    return dict(sq=sq, qs=qs, sig=sig, f=f, k=k, eG=eG, einv=einv, edec=edec, eGL=jnp.exp(GL),
                qt=qs * eG, kt=k * einv, kd=k * edec)


def _ride_split(ride, rest, n_out, n_scratch):
    if ride is None:
        return None, rest[:n_out], None, rest[n_out:], None
    return rest[0], rest[1:1 + n_out], rest[1 + n_out], rest[2 + n_out:2 + n_out + n_scratch], rest[2 + n_out + n_scratch:]


def _hgrn_fwd(hg, lb, normw, ride=None):
    S = hg.shape[0]
    sb = HGRN_SB
    nsb = S // sb
    nch = sb // HGRN_CHUNK

    def body(q_ref, f_ref, v_ref, og_ref, lb_ref, nw_ref, m_ref, *rest):
        src_ref, (y_ref, o_ref, ck_ref), got_ref, (st,), sems = _ride_split(ride, rest, 3, 1)
        j = pl.program_id(1)
        if ride is not None:
            @pl.when(jnp.logical_and(pl.program_id(0) == 0, j == 0))
            def _():
                _chip_start(src_ref, got_ref, sems[0], sems[1], ride[1])

        @pl.when(j == 0)
        def _():
            st[...] = jnp.zeros_like(st)

        tril_m = m_ref[0]
        tril = tril_m.astype(f32) > 0.5

        def one_head(hh):
            cols = slice(hh * LANE, (hh + 1) * LANE)
            ST = st[hh]
            ck_ref[hh, 0] = ST
            pr = _hgrn_prep(q_ref[:, cols], f_ref[:, cols], lb_ref[:, cols], tril_m, m_ref[1])
            qtb, ktb, kdb = pr["qt"].astype(bf16), pr["kt"].astype(bf16), pr["kd"].astype(bf16)
            eGL = pr["eGL"]
            vb = v_ref[:, cols].astype(bf16)
            A = jnp.where(tril, lax.dot_general(qtb, ktb, NT, preferred_element_type=f32), 0.0)
            o = jnp.dot(A.astype(bf16), vb, preferred_element_type=f32)
            outs = []
            for ci in range(nch):
                lo = ci * HGRN_CHUNK
                sl = slice(lo, lo + HGRN_CHUNK)
                outs.append(o[sl] + lax.dot_general(qtb[sl], ST.astype(bf16), NT, preferred_element_type=f32))
                ST = ST * eGL[lo:lo + 1, :] + lax.dot_general(vb[sl], kdb[sl], TN, preferred_element_type=f32)
            st[hh] = ST
            of = jnp.concatenate(outs, axis=0)
            o_ref[:, cols] = of
            rms = lax.rsqrt(jnp.mean(of * of, axis=-1, keepdims=True) + EPS)
            ogv = og_ref[:, cols]
            y_ref[:, cols] = ((of * rms * nw_ref[...]) * (ogv * _sigmoid(ogv))).astype(bf16)

        for hh in range(HGRN_PAIR):
            one_head(hh)

        if ride is not None:
            @pl.when(jnp.logical_and(pl.program_id(0) == ngrp - 1, j == nsb - 1))
            def _():
                _chip_finish(src_ref, got_ref, sems[0], sems[1], ride[1])

    wide = HGRN_PAIR * LANE
    ngrp = 4 // HGRN_PAIR
    col = lambda off: pl.BlockSpec((sb, wide), lambda h, j: (j, off // HGRN_PAIR + h))
    riding = ride is not None
    res = pl.pallas_call(
        body,
        grid=(ngrp, nsb),
        in_specs=[col(0), col(4), col(8), col(12), pl.BlockSpec((1, wide), lambda h, j: (0, h)),
                  pl.BlockSpec((1, LANE), lambda h, j: (0, 0)),
                  pl.BlockSpec((3, sb, sb), lambda h, j: (0, 0, 0))] + ([_ANY] if riding else []),
        out_specs=[col(0), col(0), pl.BlockSpec((HGRN_PAIR, 1, LANE, LANE), lambda h, j: (h, j, 0, 0))]
        + ([_ANY] if riding else []),
        out_shape=[SDS((S, HGRN_W), bf16), SDS((S, HGRN_W), f32), SDS((4, nsb, LANE, LANE), f32)]
        + ([_chip_out_shape(*ride)] if riding else []),
        scratch_shapes=[pltpu.VMEM((HGRN_PAIR, LANE, LANE), f32)] + (list(_CHIP_SEMS) if riding else []),
        compiler_params=_cparams(("arbitrary", "arbitrary") if riding else ("parallel", "arbitrary")),
        name="hgrn_fwd",
    )(hg, hg, hg, hg, lb, normw, _chunk_masks(), *([ride[0]] if riding else []))
    return tuple(res) if riding else (*res, None)


def _hgrn_bwd(hg, o_raw, dy, ck, lb, normw, ride=None):
    S = hg.shape[0]
    sb = HGRN_SB
    nsb = S // sb
    nch = sb // HGRN_CHUNK

    def body(q_ref, f_ref, v_ref, og_ref, o_ref, dy_ref, ck_ref, lb_ref, nw_ref, m_ref, *rest):
        src_ref, outs, got_ref, (dst, alb, anw), sems = _ride_split(ride, rest, 6, 3)
        dq_ref, df_ref, dv_ref, dog_ref, glb_ref, gnw_ref = outs
        j = pl.program_id(1)
        if ride is not None:
            @pl.when(jnp.logical_and(pl.program_id(0) == 0, j == 0))
            def _():
                _chip_start(src_ref, got_ref, sems[0], sems[1], ride[1])

        @pl.when(j == 0)
        def _():
            dst[...] = jnp.zeros_like(dst)
            alb[...] = jnp.zeros_like(alb)
            anw[...] = jnp.zeros_like(anw)

        tril_m = m_ref[0]
        tril = tril_m.astype(f32) > 0.5
        nw = nw_ref[...]

        def one_head(hh):
            cols = slice(hh * LANE, (hh + 1) * LANE)
            lbv = lb_ref[:, cols]
            q_raw = q_ref[:, cols]
            pr = _hgrn_prep(q_raw, f_ref[:, cols], lbv, tril_m, m_ref[1])
            qt, kt, kd, eGL = pr["qt"], pr["kt"], pr["kd"], pr["eGL"]
            qtb, ktb, kdb = qt.astype(bf16), kt.astype(bf16), kd.astype(bf16)
            vb = v_ref[:, cols].astype(bf16)

            o = o_ref[:, cols]
            ogv = og_ref[:, cols]
            sog = _sigmoid(ogv)
            rms = lax.rsqrt(jnp.mean(o * o, axis=-1, keepdims=True) + EPS)
            oh = o * rms
            dyv = dy_ref[:, cols]
            dog_ref[:, cols] = (dyv * (oh * nw) * (sog * (1.0 + ogv * (1.0 - sog)))).astype(bf16)
            dohw = dyv * (ogv * sog)
            anw[:, cols] += _colsum8(dohw * oh)
            doh = dohw * nw
            do = rms * (doh - oh * jnp.mean(doh * oh, axis=-1, keepdims=True))
            dob = do.astype(bf16)

            Ab = jnp.where(tril, lax.dot_general(qtb, ktb, NT, preferred_element_type=f32), 0.0).astype(bf16)
            dAb = jnp.where(tril, lax.dot_general(dob, vb, NT, preferred_element_type=f32), 0.0).astype(bf16)
            dv_acc = lax.dot_general(Ab, dob, TN, preferred_element_type=f32)
            dqt = jnp.dot(dAb, ktb, preferred_element_type=f32)
            dkt = lax.dot_general(dAb, qtb, TN, preferred_element_type=f32)

            ST = ck_ref[hh, 0]
            states = []
            for ci in range(nch):
                lo = ci * HGRN_CHUNK
                sl = slice(lo, lo + HGRN_CHUNK)
                states.append(ST)
                ST = ST * eGL[lo:lo + 1, :] + lax.dot_general(vb[sl], kdb[sl], TN, preferred_element_type=f32)

            dST = dst[hh]
            dqt_i, dkd_i, dv_i, deg_i = [None] * nch, [None] * nch, [None] * nch, [None] * nch
            for ci in reversed(range(nch)):
                lo = ci * HGRN_CHUNK
                sl = slice(lo, lo + HGRN_CHUNK)
                ST0 = states[ci]
                dSTb = dST.astype(bf16)
                dv_i[ci] = lax.dot_general(kdb[sl], dSTb, NT, preferred_element_type=f32)
                dqt_i[ci] = jnp.dot(dob[sl], ST0.astype(bf16), preferred_element_type=f32)
                dkd_i[ci] = jnp.dot(vb[sl], dSTb, preferred_element_type=f32)
                deg_i[ci] = jnp.broadcast_to(jnp.sum(dST * ST0, axis=0, keepdims=True), (HGRN_CHUNK, LANE))
                dST = dST * eGL[lo:lo + 1, :] + lax.dot_general(dob[sl], qtb[sl], TN, preferred_element_type=f32)
            dst[hh] = dST

            dqt = dqt + jnp.concatenate(dqt_i, axis=0)
            dkd = jnp.concatenate(dkd_i, axis=0)
            dv_ref[:, cols] = (dv_acc + jnp.concatenate(dv_i, axis=0)).astype(bf16)
            deg = jnp.concatenate(deg_i, axis=0)

            dqs = dqt * pr["eG"]
            dkdkd = dkd * kd
            dG = dqt * qt - dkt * kt - dkdkd
            dk = dkt * pr["einv"] + dkd * pr["edec"]
            dGL = _mask_dot(m_ref[1], dkdkd) + eGL * deg
            dg = _mask_dot(m_ref[2], dG) + dGL
            df = dg / pr["f"] - dk
            sig = pr["sig"]
            df_ref[:, cols] = (df * (1.0 - lbv) * (sig * (1.0 - sig))).astype(bf16)
            alb[:, cols] += _colsum8(df * (1.0 - sig))
            sq = pr["sq"]
            dq_ref[:, cols] = (dqs * (sq * (1.0 + q_raw * (1.0 - sq)))).astype(bf16)

        for hh in range(HGRN_PAIR):
            one_head(hh)

        @pl.when(j == nsb - 1)
        def _():
            glb_ref[...] = jnp.broadcast_to(jnp.sum(alb[...], axis=0, keepdims=True), (SUBLANE, wide))
            gnw_ref[...] = jnp.broadcast_to(jnp.sum(anw[...], axis=0, keepdims=True), (SUBLANE, wide))

        if ride is not None:
            @pl.when(jnp.logical_and(pl.program_id(0) == ngrp - 1, j == nsb - 1))
            def _():
                _chip_finish(src_ref, got_ref, sems[0], sems[1], ride[1])

    wide = HGRN_PAIR * LANE
    ngrp = 4 // HGRN_PAIR
    rev = lambda off: pl.BlockSpec((sb, wide), lambda h, j: (nsb - 1 - j, off // HGRN_PAIR + h))
    stat = pl.BlockSpec((SUBLANE, wide), lambda h, j: (0, h))
    riding = ride is not None
    res = pl.pallas_call(
        body,
        grid=(ngrp, nsb),
        in_specs=[rev(0), rev(4), rev(8), rev(12), rev(0), rev(0),
                  pl.BlockSpec((HGRN_PAIR, 1, LANE, LANE), lambda h, j: (h, nsb - 1 - j, 0, 0)),
                  pl.BlockSpec((1, wide), lambda h, j: (0, h)), pl.BlockSpec((1, LANE), lambda h, j: (0, 0)),
                  pl.BlockSpec((3, sb, sb), lambda h, j: (0, 0, 0))]
        + ([_ANY] if riding else []),
        out_specs=[rev(0), rev(0), rev(0), rev(0), stat, stat] + ([_ANY] if riding else []),
        out_shape=[SDS((S, HGRN_W), bf16)] * 4 + [SDS((SUBLANE, HGRN_W), f32)] * 2
        + ([_chip_out_shape(*ride)] if riding else []),
        scratch_shapes=[pltpu.VMEM((HGRN_PAIR, LANE, LANE), f32), pltpu.VMEM((SUBLANE, wide), f32),
                        pltpu.VMEM((SUBLANE, wide), f32)] + (list(_CHIP_SEMS) if riding else []),
        compiler_params=_cparams(("arbitrary", "arbitrary") if riding else ("parallel", "arbitrary")),
        name="hgrn_bwd",
    )(hg, hg, hg, hg, o_raw, dy, ck, lb, normw, _chunk_masks(), *([ride[0]] if riding else []))
    return tuple(res) if riding else (*res, None)


def _lb_fwd(raw):
    def body(r_ref, o_ref):
        r = r_ref[...]
        m = jnp.max(r, axis=0, keepdims=True)
        e = jnp.exp(r - m)
        o_ref[...] = (e / jnp.sum(e, axis=0, keepdims=True))[0:1]

    return pl.pallas_call(body, out_shape=SDS((1, raw.shape[1]), f32), name="lb_fwd")(raw)


def _lb_bwd(raw, dlb):
    def body(r_ref, d_ref, o_ref):
        r = r_ref[...]
        m = jnp.max(r, axis=0, keepdims=True)
        e = jnp.exp(r - m)
        s = e / jnp.sum(e, axis=0, keepdims=True)
        s0 = s[0:1]
        onehot0 = jnp.where(lax.broadcasted_iota(jnp.int32, r.shape, 0) == 0, 1.0, 0.0)
        o_ref[...] = d_ref[...] * s0 * (onehot0 - s)

    return pl.pallas_call(body, out_shape=SDS(raw.shape, f32), name="lb_bwd")(raw, dlb)


def _gate_fwd(ya, yh, w_ba, w_bh, gc):
    S = ya.shape[0]
    D = w_ba.shape[1]
    tm = _pick(S, MM_ROWS)

    def body(ya_ref, yh_ref, wa_ref, wh_ref, g0_ref, g1_ref, a_ref, b_ref, o_ref):
        a = jnp.dot(ya_ref[...], wa_ref[...], preferred_element_type=f32).astype(bf16)
        b = jnp.dot(yh_ref[...], wh_ref[...], preferred_element_type=f32).astype(bf16)
        a_ref[...] = a
        b_ref[...] = b
        s0, s1 = _sigmoid(g0_ref[...].astype(f32)), _sigmoid(g1_ref[...].astype(f32))
        o_ref[...] = (s0 * a.astype(f32) + s1 * b.astype(f32)).astype(bf16)

    row = pl.BlockSpec((tm, D), lambda i: (i, 0))
    act = pl.BlockSpec((tm, ya.shape[1]), lambda i: (i, 0))
    wspec = pl.BlockSpec(w_ba.shape, lambda i: (0, 0))
    return pl.pallas_call(
        body,
        grid=(S // tm,),
        in_specs=[act, act, wspec, wspec, row, pl.BlockSpec((tm, D), lambda i: (i, 1))],
        out_specs=[row, row, row],
        out_shape=[SDS((S, D), bf16)] * 3,
        compiler_params=_cparams(("parallel",), VMEM_BIG),
        name="branch_gate_fwd",
    )(ya, yh, w_ba, w_bh, gc, gc)


def _gate_bwd(dmo, w_out, a, b, gc, w_ba, w_bh):
    S, D = a.shape
    W = w_ba.shape[0]
    tm = _pick(S, MM_ROWS)

    def body(dmo_ref, wo_ref, a_ref, b_ref, g0_ref, g1_ref, wa_ref, wh_ref,
             da_ref, db_ref, dg_ref, dya_ref, dyh_ref):
        dm = lax.dot_general(dmo_ref[...], wo_ref[...], NT, preferred_element_type=f32)
        dmv = dm.astype(bf16).astype(f32)
        s0, s1 = _sigmoid(g0_ref[...].astype(f32)), _sigmoid(g1_ref[...].astype(f32))
        da = (dmv * s0).astype(bf16)
        db = (dmv * s1).astype(bf16)
        da_ref[...] = da
        db_ref[...] = db
        dg_ref[:, :D] = (dmv * a_ref[...].astype(f32) * (s0 * (1.0 - s0))).astype(bf16)
        dg_ref[:, D:] = (dmv * b_ref[...].astype(f32) * (s1 * (1.0 - s1))).astype(bf16)
        dya_ref[...] = lax.dot_general(da, wa_ref[...], NT, preferred_element_type=f32)
        dyh_ref[...] = lax.dot_general(db, wh_ref[...], NT, preferred_element_type=f32)

    row = pl.BlockSpec((tm, D), lambda i: (i, 0))
    wide = pl.BlockSpec((tm, 2 * D), lambda i: (i, 0))
    narrow = pl.BlockSpec((tm, W), lambda i: (i, 0))
    whole = lambda t: pl.BlockSpec(t.shape, lambda i: (0, 0))
    return pl.pallas_call(
        body,
        grid=(S // tm,),
        in_specs=[row, whole(w_out), row, row, row, pl.BlockSpec((tm, D), lambda i: (i, 1)), whole(w_ba), whole(w_bh)],
        out_specs=[row, row, wide, narrow, narrow],
        out_shape=[SDS((S, D), bf16), SDS((S, D), bf16), SDS((S, 2 * D), bf16), SDS((S, W), f32), SDS((S, W), f32)],
        compiler_params=_cparams(("parallel",), VMEM_BIG),
        name="gate_bwd_fused",
    )(dmo, w_out, a, b, gc, gc, w_ba, w_bh)


CONV_ROWS = 512
INV_SQRT2 = 0.7071067811865476
INV_SQRT_2PI = 0.3989422804014327


CONV_HALO = 16


def _shift_down(cur, prev, k):
    x = pltpu.roll(cur, k, 0)
    row = lax.broadcasted_iota(jnp.int32, (SUBLANE, LANE), 0)
    head = jnp.where(row < k, pltpu.roll(prev, k, 0)[:SUBLANE], x[:SUBLANE])
    return jnp.concatenate([head, x[SUBLANE:]], axis=0)


def _shift_up(cur, nxt, k):
    R = cur.shape[0]
    x = pltpu.roll(cur, R - k, 0)
    row = lax.broadcasted_iota(jnp.int32, (SUBLANE, LANE), 0)
    tail = jnp.where(row >= SUBLANE - k, pltpu.roll(nxt, SUBLANE - k, 0), x[R - SUBLANE:])
    return jnp.concatenate([x[:R - SUBLANE], tail], axis=0)


def _conv_rows(u_ref, w, b, r0, first):
    R = CONV_ROWS
    cur = u_ref[pl.ds(r0, R), :].astype(f32)
    prev = u_ref[pl.ds(pl.multiple_of(jnp.maximum(r0 - CONV_HALO, 0), CONV_HALO), CONV_HALO), :].astype(f32)
    prev = jnp.where(first, 0.0, prev)
    x1 = _shift_down(cur, prev, 1)
    x2 = _shift_down(cur, prev, 2)
    c = ((b + w[0:1] * x2) + w[1:2] * x1) + w[2:3] * cur
    return c, x2, x1, cur


def _conv_fwd(ug, uv, wg, wv, bg, bv):
    S, F = ug.shape
    nchunk = S // CONV_ROWS

    def body(ug_ref, uv_ref, wg_ref, wv_ref, bg_ref, bv_ref, o_ref):
        wgv, wvv, bgv, bvv = wg_ref[...], wv_ref[...], bg_ref[...], bv_ref[...]

        def step(ci, carry):
            r0 = pl.multiple_of(ci * CONV_ROWS, CONV_ROWS)
            cg = _conv_rows(ug_ref, wgv, bgv, r0, ci == 0)[0]
            cv = _conv_rows(uv_ref, wvv, bvv, r0, ci == 0)[0]
            gelu = 0.5 * cg * (1.0 + lax.erf(cg * INV_SQRT2))
            o_ref[pl.ds(r0, CONV_ROWS), :] = (gelu * cv).astype(bf16)
            return carry

        lax.fori_loop(0, nchunk, step, 0)

    col = pl.BlockSpec((S, LANE), lambda j: (0, j))
    w3 = pl.BlockSpec((3, LANE), lambda j: (0, j))
    b1 = pl.BlockSpec((1, LANE), lambda j: (0, j))
    return pl.pallas_call(
        body,
        grid=(F // LANE,),
        in_specs=[col, col, w3, w3, b1, b1],
        out_specs=col,
        out_shape=SDS((S, F), bf16),
        compiler_params=_cparams(("parallel",), VMEM_BIG),
        name="conv_fwd",
    )(ug, uv, wg, wv, bg, bv)


def _conv_bwd(ug, uv, dact, wg, wv, bg, bv):
    S, F = ug.shape
    R = CONV_ROWS
    nchunk = S // R

    def body(ug_ref, uv_ref, da_ref, wg_ref, wv_ref, bg_ref, bv_ref, dug_ref, duv_ref, sg_ref, sv_ref, dcg, dcv):
        wgv, wvv, bgv, bvv = wg_ref[...], wv_ref[...], bg_ref[...], bv_ref[...]
        zero = jnp.zeros((SUBLANE, LANE), f32)

        def fwd_step(ci, acc):
            r0 = pl.multiple_of(ci * R, R)
            cg, g2, g1, g0 = _conv_rows(ug_ref, wgv, bgv, r0, ci == 0)
            cv, v2, v1, v0 = _conv_rows(uv_ref, wvv, bvv, r0, ci == 0)
            da = da_ref[pl.ds(r0, R), :].astype(f32)
            cdf = 0.5 * (1.0 + lax.erf(cg * INV_SQRT2))
            pdf = INV_SQRT_2PI * jnp.exp(-0.5 * cg * cg)
            dg = da * cv * (cdf + cg * pdf)
            dv = da * (cg * cdf)
            dcg[pl.ds(r0, R), :] = dg
            dcv[pl.ds(r0, R), :] = dv
            new = (acc[0] + _colsum8(dg * g2), acc[1] + _colsum8(dg * g1), acc[2] + _colsum8(dg * g0),
                   acc[3] + _colsum8(dg),
                   acc[4] + _colsum8(dv * v2), acc[5] + _colsum8(dv * v1), acc[6] + _colsum8(dv * v0),
                   acc[7] + _colsum8(dv))
            return new

        acc = lax.fori_loop(0, nchunk, fwd_step, (zero,) * 8)
        rows = lax.broadcasted_iota(jnp.int32, (SUBLANE, LANE), 0)

        def stats(parts):
            out = jnp.zeros((SUBLANE, LANE), f32)
            for k, pt in enumerate(parts):
                out = jnp.where(rows == k, jnp.sum(pt, axis=0, keepdims=True), out)
            return out

        sg_ref[...] = stats(acc[0:4])
        sv_ref[...] = stats(acc[4:8])

        def du_rows(dc, w, r0, last):
            cur = dc[pl.ds(r0, R), :]
            nxt = dc[pl.ds(pl.multiple_of(jnp.minimum(r0 + R, S - SUBLANE), SUBLANE), SUBLANE), :]
            nxt = jnp.where(last, 0.0, nxt)
            return w[2:3] * cur + w[1:2] * _shift_up(cur, nxt, 1) + w[0:1] * _shift_up(cur, nxt, 2)

        def bwd_step(ci, carry):
            r0 = pl.multiple_of(ci * R, R)
            last = ci == nchunk - 1
            dug_ref[pl.ds(r0, R), :] = du_rows(dcg, wgv, r0, last).astype(bf16)
            duv_ref[pl.ds(r0, R), :] = du_rows(dcv, wvv, r0, last).astype(bf16)
            return carry

        lax.fori_loop(0, nchunk, bwd_step, 0)

    col = pl.BlockSpec((S, LANE), lambda j: (0, j))
    w3 = pl.BlockSpec((3, LANE), lambda j: (0, j))
    b1 = pl.BlockSpec((1, LANE), lambda j: (0, j))
    st = pl.BlockSpec((SUBLANE, LANE), lambda j: (0, j))
    return pl.pallas_call(
        body,
        grid=(F // LANE,),
        in_specs=[col, col, col, w3, w3, b1, b1],
        out_specs=[col, col, st, st],
        out_shape=[SDS((S, F), bf16), SDS((S, F), bf16), SDS((SUBLANE, F), f32), SDS((SUBLANE, F), f32)],
        scratch_shapes=[pltpu.VMEM((S, LANE), f32), pltpu.VMEM((S, LANE), f32)],
        compiler_params=_cparams(("parallel",), VMEM_BIG),
        name="conv_bwd",
    )(ug, uv, dact, wg, wv, bg, bv)


def _adam_math(w, g, m, v):
    m = ADAM_B1 * m + (1.0 - ADAM_B1) * g
    v = ADAM_B2 * v + (1.0 - ADAM_B2) * (g * g)
    m_hat = m / (1.0 - ADAM_B1 ** ADAM_STEP)
    v_hat = v / (1.0 - ADAM_B2 ** ADAM_STEP)
    delta = -ADAM_LR * (m_hat / (jnp.sqrt(v_hat) + ADAM_EPS) + ADAM_WD * w)
    return delta, m, v


def _adamw(w, m, v, g, name):
    R, C = w.shape
    parts = g.ndim == 3
    tr = R
    if R % 16 == 0:
        for t in range(R, 0, -16):
            if R % t == 0 and t * C * 4 <= ADAM_BLOCK_BYTES:
                tr = t
                break

    def body(w_ref, m_ref, v_ref, g_ref, go_ref, d_ref, mo_ref, vo_ref):
        if parts:
            gv = ((g_ref[0].astype(f32) + g_ref[1].astype(f32)) + g_ref[2].astype(f32)) + g_ref[3].astype(f32)
        else:
            gv = g_ref[...]
        go_ref[...] = gv
        d, mn, vn = _adam_math(w_ref[...], gv, m_ref[...], v_ref[...])
        d_ref[...] = d
        mo_ref[...] = mn
        vo_ref[...] = vn

    row = pl.BlockSpec((tr, C), lambda i: (i, 0))
    gspec = pl.BlockSpec((4, tr, C), lambda i: (0, i, 0)) if parts else row
    return pl.pallas_call(
        body,
        grid=(R // tr,),
        in_specs=[row, row, row, gspec],
        out_specs=[row] * 4,
        out_shape=[SDS((R, C), f32)] * 4,
        compiler_params=_cparams(("parallel",), VMEM_BIG),
        name=name,
    )(w, m, v, g)


def _sum8(parts, name):
    _, _, R, C = parts.shape

    def body(p_ref, o_ref):
        acc = p_ref[0, 0]
        for c in range(2):
            for k in range(4):
                if c or k:
                    acc = acc + p_ref[c, k]
        o_ref[...] = acc

    return pl.pallas_call(body, out_shape=SDS((R, C), f32), name=name)(parts)


def _pair_add(by_core, b, name):
    _, K, R, C = by_core.shape
    tr = R // 2 if R % 32 == 0 else R

    def body(c_ref, a_ref, b_ref, o_ref):
        o_ref[...] = (a_ref[0].astype(f32) + b_ref[...].astype(f32)).astype(bf16)

    blk = pl.BlockSpec((1, tr, C), lambda k, i, c: (k, i, 0))
    return pl.pallas_call(
        body,
        grid_spec=pltpu.PrefetchScalarGridSpec(
            num_scalar_prefetch=1,
            grid=(K, R // tr),
            in_specs=[pl.BlockSpec((1, 1, tr, C), lambda k, i, c: (c[0], k, i, 0)), blk],
            out_specs=blk,
        ),
        out_shape=SDS((K, R, C), bf16),
        compiler_params=_cparams(("parallel", "parallel")),
        name=name,
    )(lax.axis_index("c").astype(jnp.int32).reshape(1), by_core, b)


_ANY = pl.BlockSpec(memory_space=pl.ANY)


def _chip_copies(src_ref, out_ref, send_sems, recv_sems, gather):
    x, y, c = lax.axis_index("x"), lax.axis_index("y"), lax.axis_index("c")
    mine = 2 * x + y

    def piece(k):
        return src_ref if gather else src_ref.at[k]

    sends, recvs = [], []
    for j, (px, py) in enumerate([(1 - x, y), (x, 1 - y), (1 - x, 1 - y)]):
        sends.append(pltpu.make_async_remote_copy(
            src_ref=piece(2 * px + py), dst_ref=out_ref.at[mine], send_sem=send_sems.at[j],
            recv_sem=recv_sems.at[j], device_id=(px, py, c), device_id_type=MESH))
        recvs.append(pltpu.make_async_remote_copy(
            src_ref=piece(mine), dst_ref=out_ref.at[2 * px + py], send_sem=send_sems.at[j],
            recv_sem=recv_sems.at[j], device_id=(px, py, c), device_id_type=MESH))
    return sends, recvs


def _chip_start(src_ref, out_ref, send_sems, recv_sems, gather):
    for cp in _chip_copies(src_ref, out_ref, send_sems, recv_sems, gather)[0]:
        cp.start()


def _chip_finish(src_ref, out_ref, send_sems, recv_sems, gather):
    sends, recvs = _chip_copies(src_ref, out_ref, send_sems, recv_sems, gather)
    for cp in recvs:
        cp.wait_recv()
    for cp in sends:
        cp.wait_send()


def _chip_out_shape(src, gather):
    return SDS((4,) + tuple(src.shape if gather else src.shape[1:]), src.dtype)


_CHIP_SEMS = [pltpu.SemaphoreType.DMA((3,)), pltpu.SemaphoreType.DMA((3,))]


def _fill_own(out, src, gather):
    mine = 2 * lax.axis_index("x") + lax.axis_index("y")
    own = src if gather else lax.dynamic_index_in_dim(src, mine, axis=0, keepdims=False)
    return lax.dynamic_update_index_in_dim(out, own, mine, axis=0)


def _chip_comm(src, gather, name):
    def body(src_ref, out_ref, send_sems, recv_sems):
        _chip_start(src_ref, out_ref, send_sems, recv_sems, gather)
        _chip_finish(src_ref, out_ref, send_sems, recv_sems, gather)

    out = pl.pallas_call(
        body,
        in_specs=[_ANY],
        out_specs=_ANY,
        out_shape=_chip_out_shape(src, gather),
        scratch_shapes=list(_CHIP_SEMS),
        name=name,
    )(src)
    return _fill_own(out, src, gather)


_HBM = pl.BlockSpec(memory_space=pltpu.HBM)
_SEM = pl.BlockSpec(memory_space=pltpu.SEMAPHORE)
_EFFECT = pltpu.SideEffectType.DATAFLOW_SIDE_EFFECTING
_SPLIT_PEERS = {"chip_gather": 3, "chip_xchg": 3, "core_gather": 1, "core_swap": 1}


def _split_land(src, kind):
    if kind == "core_gather":
        return SDS((2,) + tuple(src.shape), src.dtype)
    if kind == "core_swap":
        return SDS(tuple(src.shape[1:]), src.dtype)
    return _chip_out_shape(src, kind == "chip_gather")


def _split_copies(src_ref, land_ref, sems, kind):
    x, y, c = lax.axis_index("x"), lax.axis_index("y"), lax.axis_index("c")
    n = _SPLIT_PEERS[kind]
    if kind == "core_gather":
        routes = [((x, y, 1 - c), src_ref, land_ref.at[c], land_ref.at[1 - c])]
    elif kind == "core_swap":
        routes = [((x, y, 1 - c), src_ref.at[1 - c], land_ref, land_ref)]
    else:
        mine = 2 * x + y
        gather = kind == "chip_gather"
        routes = [((px, py, c), src_ref if gather else src_ref.at[2 * px + py], land_ref.at[mine],
                   land_ref.at[2 * px + py]) for px, py in [(1 - x, y), (x, 1 - y), (1 - x, 1 - y)]]
    sends, recvs = [], []
    for j, (peer, piece, there, here) in enumerate(routes):
        sends.append(pltpu.make_async_remote_copy(src_ref=piece, dst_ref=there, send_sem=sems[j],
                                                  recv_sem=sems[n + j], device_id=peer, device_id_type=MESH))
        recvs.append(pltpu.make_async_remote_copy(src_ref=piece, dst_ref=here, send_sem=sems[j],
                                                  recv_sem=sems[n + j], device_id=peer, device_id_type=MESH))
    return sends, recvs


def _split_start(src, kind, name, after=None):
    land = _split_land(src, kind)
    ns = 2 * _SPLIT_PEERS[kind]
    n_in = 2 if after is None else 3

    def body(*refs):
        src_ref, land_ref = refs[:2]
        outs = refs[n_in:]
        for cp in _split_copies(src_ref, land_ref, outs[:ns], kind)[0]:
            cp.start()
        token = outs[ns + 2]
        token[...] = jnp.zeros_like(token)

    res = pl.pallas_call(
        body,
        name=name,
        out_shape=(pltpu.SemaphoreType.DMA(()),) * ns
        + (pltpu.HBM(src.shape, src.dtype), pltpu.HBM(land.shape, land.dtype), SDS((SUBLANE, LANE), f32)),
        in_specs=(_HBM, _HBM) + (() if after is None else (_ANY,)),
        out_specs=(_SEM,) * ns + (_HBM, _HBM, pl.BlockSpec(memory_space=pltpu.VMEM)),
        input_output_aliases={0: ns, 1: ns + 1},
        compiler_params=pltpu.CompilerParams(has_side_effects=_EFFECT),
    )(pltpu.with_memory_space_constraint(src, pltpu.HBM),
      pltpu.with_memory_space_constraint(lax.empty(land.shape, land.dtype), pltpu.HBM),
      *(() if after is None else (after,)))
    return (res[:ns], res[ns], res[ns + 1]), res[ns + 2]


def _split_wait(state, after, kind, name):
    sems, src_thru, land_thru = state
    ns = 2 * _SPLIT_PEERS[kind]

    def body(src_ref, land_ref, *rest):
        sends, recvs = _split_copies(src_ref, land_ref, rest[:ns], kind)
        for cp in recvs:
            cp.wait_recv()
        for cp in sends:
            cp.wait_send()

    src_out, got = pl.pallas_call(
        body,
        name=name,
        out_shape=(pltpu.HBM(src_thru.shape, src_thru.dtype), pltpu.HBM(land_thru.shape, land_thru.dtype)),
        in_specs=(_HBM, _HBM) + (_SEM,) * ns + (_ANY,),
        out_specs=(_HBM, _HBM),
        input_output_aliases={0: 0, 1: 1},
        compiler_params=pltpu.CompilerParams(has_side_effects=_EFFECT),
    )(src_thru, land_thru, *sems, after)
    if kind == "core_swap":
        return got, src_out
    if kind == "core_gather":
        return lax.dynamic_update_index_in_dim(got, src_out, lax.axis_index("c"), axis=0)
    return _fill_own(got, src_out, kind == "chip_gather")


def _core_gather(src, name):
    def body(src_ref, out_ref, send_sem, recv_sem):
        x, y, c = lax.axis_index("x"), lax.axis_index("y"), lax.axis_index("c")
        cp = pltpu.make_async_remote_copy(src_ref=src_ref, dst_ref=out_ref.at[c], send_sem=send_sem,
                                          recv_sem=recv_sem, device_id=(x, y, 1 - c), device_id_type=MESH)
        cp.start()
        pltpu.make_async_remote_copy(src_ref=src_ref, dst_ref=out_ref.at[1 - c], send_sem=send_sem,
                                     recv_sem=recv_sem, device_id=(x, y, 1 - c), device_id_type=MESH).wait_recv()
        cp.wait_send()

    out = pl.pallas_call(
        body,
        in_specs=[_ANY],
        out_specs=_ANY,
        out_shape=SDS((2,) + tuple(src.shape), src.dtype),
        scratch_shapes=[pltpu.SemaphoreType.DMA, pltpu.SemaphoreType.DMA],
        name=name,
    )(src)
    return lax.dynamic_update_index_in_dim(out, src, lax.axis_index("c"), axis=0)


_PACK_A = (("w_in", (1088, 1024)),)
_PACK_B = (("w_ba", (512, 128)), ("w_bh", (512, 128)), ("w_out", (128, 1024)), ("w_up", (704, 1024)),
           ("w_down", (352, 1024)))
_PACK_SIZES = _PACK_A + _PACK_B
_TRANSPOSED = ("w_in", "w_up")


def _slab_rows(sizes):
    return sum(r * c for _, (r, c) in sizes) // D_MODEL


def _pack_rows(d, sizes):
    n = d[sizes[0][0]].shape[0]
    return jnp.concatenate([d[k].reshape(n, -1, D_MODEL) for k, _ in sizes], axis=1)


def _unpack_rows(slab, sizes):
    n = slab.shape[0]
    out, lo = {}, 0
    for key, (r, c) in sizes:
        rows = r * c // D_MODEL
        out[key] = slab[:, lo:lo + rows].reshape(n, r, c)
        lo += rows
    return out


def _by_core(gslab):
    return jnp.swapaxes(gslab.reshape((4, 2) + gslab.shape[1:]), 0, 1)


def _cols_to_full(t):
    return jnp.swapaxes(t, 0, 1).reshape(t.shape[1], -1)


def _full_to_cols(t):
    K = t.shape[0]
    return jnp.swapaxes(t.reshape(K, 8, -1), 0, 1)


_SMALL = (("pre_mix_norm", (1, 1024)), ("rel_bias", (32, 24)), ("hgrn_lb_raw", (2, 512)), ("hgrn_norm", (1, 128)),
          ("post_mix_norm", (1, 1024)), ("pre_ffn_norm", (1, 1024)), ("conv_b", (1, 5632)),
          ("post_ffn_norm", (1, 1024)))
_SMALL_ROWS = 96
_CONVW_ROWS = 136


_SMALL_USED = sum(r * c for _, (r, c) in _SMALL)


def _pack_small(d, extra=None):
    flat = jnp.concatenate([d[k].reshape(-1) for k, _ in _SMALL] + ([] if extra is None else [extra.reshape(-1)]))
    flat = jnp.pad(flat, (0, _SMALL_ROWS * LANE - flat.shape[0]))
    return flat.reshape(_SMALL_ROWS, LANE)


def _unpack_small(p):
    flat = p.reshape(-1)
    out, lo = {}, 0
    for k, shp in _SMALL:
        n = shp[0] * shp[1]
        out[k] = flat[lo:lo + n].reshape(shp)
        lo += n
    return out


def _local_step(x, tgt, P, plan):
    S = x.shape[0]
    P = dict(P)
    lb = _lb_fwd(P["hgrn_lb_raw"])
    hs = _prep(x, P["pre_mix_norm"], plan.start_token())
    h1 = hs[0]
    consts = [_bias_consts(d) for d in DILATIONS]
    biases, dep = [], h1
    for g in range(N_GROUPS):
        tab_t = P["rel_bias"][:, 8 * g:8 * g + 8].T
        dep = _bias_build(tab_t, consts[g][0], consts[g][1], f"bias_build{g}", dep)
        biases.append(dep.reshape(8, ATTN_BLOCK, 2 * ATTN_BLOCK))
    W = dict(plan.weights_a(dep))
    qkv = [_mm(hs[g], W["wt_qkv"][g], "nt", bf16, f"proj_qkv{g}") for g in range(N_GROUPS)]
    hg = _mm(h1, W["wt_hg"], "nt", f32, "proj_hg")
    gc = _mm(h1, W["wt_gate"], "nt", bf16, "proj_gate")
    obuf, lbuf, token = [], [], None
    for g, d in enumerate(DILATIONS):
        o_g, l_g = _attn_fwd(qkv[g], biases[g], (S // d) // ATTN_BLOCK, f"attn_fwd{g}", after=token)
        lbuf.append(l_g)
        obuf.append(o_g)
        if g == 0:
            token = plan.forward_b(o_g)
    y_attn, y_attn_b, w0, w1, w2 = _attn_merge(obuf[0], obuf[1], obuf[2], lbuf[0], lbuf[1], lbuf[2])
    y_hgrn, o_raw, ck, _ = _hgrn_fwd(hg, lb, P["hgrn_norm"])
    wb = plan.weights_b(y_hgrn)
    P["conv_w"] = wb.pop("conv_w")
    W.update(wb)
    a, b, merged = _gate_fwd(y_attn_b, y_hgrn, W["w_ba"], W["w_bh"], gc)
    mo, x1, h2 = _mid_fwd(x, merged, W["w_out"], P["post_mix_norm"], P["pre_ffn_norm"])
    ug = _mm(h2, W["wt_up_g"], "nt", bf16, "up_gate")
    uv = _mm(h2, W["wt_up_v"], "nt", bf16, "up_val")
    cw_g, cw_v = P["conv_w"][:, :D_FF], P["conv_w"][:, D_FF:]
    cb_g, cb_v = P["conv_b"][:, :D_FF], P["conv_b"][:, D_FF:]
    act = _conv_fwd(ug, uv, cw_g, cw_v, cb_g, cb_v)
    loss, dy, dfo, g_post_ffn = _final(x1, act, W["w_down"], tgt, P["post_ffn_norm"])
    dact = _mm(dfo, W["w_down"], "nt", bf16, "d_act")
    gW_down = _mm(act, dfo, "tn", bf16, "gw_down")
    dug, duv, st_g, st_v = _conv_bwd(ug, uv, dact, cw_g, cw_v, cb_g, cb_v)
    dh2 = _mm([dug, duv], [W["wt_up_g"], W["wt_up_v"]], "nn", f32, "dh2")
    gW_up_g = _mm(dug, h2, "tn", bf16, "gw_up_gate")
    gW_up_v = _mm(duv, h2, "tn", bf16, "gw_up_val")
    dx1, dmo, g_pre_ffn, g_post_mix = _mid_bwd(dy, dh2, x1, mo, P["pre_ffn_norm"], P["post_mix_norm"])
    gW_out = _mm(merged, dmo, "tn", bf16, "gw_out")
    da, db, dgc, dyattn, dyhgrn = _gate_bwd(dmo, W["w_out"], a, b, gc, W["w_ba"], W["w_bh"])
    gW_ba = _mm(y_attn_b, da, "tn", bf16, "gw_ba")
    gW_bh = _mm(y_hgrn, db, "tn", bf16, "gw_bh")
    big_b = dict(w_ba=gW_ba, w_bh=gW_bh, w_out=gW_out, w_up=[gW_up_g, gW_up_v], w_down=gW_down)
    dos = _attn_merge_bwd(dyattn, y_attn, w0, w1, w2, after=plan.grads_b_start(big_b))
    dq_h, df_h, dv_h, dog_h, glb8, gnw8, got_b = _hgrn_bwd(hg, o_raw, dyhgrn, ck, lb, P["hgrn_norm"],
                                                          plan.bwd_ride(dos[5]))
    dhg = [dq_h, df_h, dv_h, dog_h]
    g_lb_raw = _lb_bwd(P["hgrn_lb_raw"], glb8[0:1])
    gn = gnw8[0:1]
    g_hgrn_norm = (gn[:, 0:128] + gn[:, 128:256]) + (gn[:, 256:384] + gn[:, 384:512])
    dqkvs, gW_qkv, g_rel = [], [], []
    for g, d in enumerate(DILATIONS):
        dq, dk, dv, dbias = _attn_bwd(qkv[g], biases[g], dos[g], dos[3 + g], lbuf[g], (S // d) // ATTN_BLOCK,
                                      f"attn_bwd{g}")
        dqkvs.append([dq, dk, dv])
        gW_qkv.append(_mm(dqkvs[g], hs[g], "tn", bf16, f"gw_qkv{g}"))
        g_rel.append(_bias_grad(dbias.reshape(8, -1), consts[g][0], f"bias_grad{g}"))
    gW_hg = _mm(dhg, h1, "tn", bf16, "gw_hg")
    gW_gate = _mm(dgc, h1, "tn", bf16, "gw_gate")
    gW_in = gW_qkv + [gW_hg, gW_gate]
    token = plan.grads_a_start(gW_in)
    dh_perm = [_mm(dqkvs[g], W["wt_qkv"][g], "nn", f32, f"dh1_qkv{g}", after=token) for g in (1, 2)]
    token = plan.grads_a_exchange(dh_perm[1])
    dh_main = _mm(dqkvs[0] + dhg + [dgc], [W["wt_qkv"][0], W["wt_hg"], W["wt_gate"]], "nn", f32, "dh1_main",
                  after=token)
    grad_x, g_pre_mix = _first_bwd(x, dx1, _dh_sum(dh_main, dh_perm[0], dh_perm[1]), P["pre_mix_norm"])

    g_conv_w = jnp.concatenate([st_g[0:3], st_v[0:3]], axis=1)
    g_conv_b = jnp.concatenate([st_g[3:4], st_v[3:4]], axis=1)
    small = dict(pre_mix_norm=g_pre_mix, rel_bias=jnp.concatenate(g_rel, axis=1), hgrn_lb_raw=g_lb_raw,
                 hgrn_norm=g_hgrn_norm, post_mix_norm=g_post_mix, pre_ffn_norm=g_pre_ffn, conv_b=g_conv_b,
                 post_ffn_norm=g_post_ffn, conv_w=g_conv_w)
    return loss, grad_x, gW_in, big_b, got_b, small


def _weights_a(both):
    wt = jnp.swapaxes(both, 0, 1).reshape(-1, D_MODEL)
    return dict(
        wt_qkv=[wt[g * QKV_G:(g + 1) * QKV_G] for g in range(N_GROUPS)],
        wt_hg=wt[3 * QKV_G:3 * QKV_G + 4 * HGRN_W],
        wt_gate=wt[3 * QKV_G + 4 * HGRN_W:],
    )


def _weights_b(slabs):
    sh = _unpack_rows(slabs, _PACK_B)
    wt_up = sh["w_up"].reshape(-1, D_MODEL)
    return dict(
        w_ba=_cols_to_full(sh["w_ba"]),
        w_bh=_cols_to_full(sh["w_bh"]),
        w_out=sh["w_out"].reshape(D_MODEL, D_MODEL),
        wt_up_g=wt_up[:D_FF],
        wt_up_v=wt_up[D_FF:],
        w_down=sh["w_down"].reshape(D_FF, D_MODEL),
    )


def _dest_rows(sections, height):
    out = []
    for j in range(8):
        lo, hi, off, pieces = j * height, (j + 1) * height, 0, []
        for s in sections:
            a, b = max(lo, off), min(hi, off + s.shape[0])
            if a < b:
                pieces.append(s[a - off:b - off])
            off += s.shape[0]
        out.append(pieces[0] if len(pieces) == 1 else jnp.concatenate(pieces, axis=0))
    return out


def _grad_blocks_a(sections):
    rows = _dest_rows(sections, 1088)
    return jnp.stack([jnp.stack([rows[2 * k + c].astype(bf16) for k in range(4)]) for c in range(2)])


def _grad_slab_b(g):
    shards = dict(w_ba=_full_to_cols(g["w_ba"]), w_bh=_full_to_cols(g["w_bh"]), w_out=g["w_out"].reshape(8, 128, D_MODEL),
                  w_up=jnp.stack(_dest_rows(g["w_up"], 704)), w_down=g["w_down"].reshape(8, 352, D_MODEL))
    return _pack_rows({k: v.astype(bf16) for k, v in shards.items()}, _PACK_B)


_CONVW_SLAB_ROWS = 16


class _Traffic:
    def __init__(self, slab_a, slab_b, conv_w):
        hi = conv_w.astype(bf16)
        r1 = conv_w - hi.astype(f32)
        mid = r1.astype(bf16)
        lo = (r1 - mid.astype(f32)).astype(bf16)
        bits = jnp.stack([hi, mid, lo]).reshape(-1)
        tail = jnp.pad(bits, (0, _CONVW_SLAB_ROWS * D_MODEL - bits.shape[0])).reshape(_CONVW_SLAB_ROWS, D_MODEL)
        self.slab_b = jnp.concatenate([slab_b, tail], axis=0)
        self.state_a, tok = _split_start(slab_a, "chip_gather", "ag_a_start")
        self.state_b, self.token = _split_start(self.slab_b, "chip_gather", "ag_b_start", after=tok)
        self.chip_sum = None
        self.state = None

    def start_token(self):
        return self.token

    def weights_a(self, after):
        by_chip = _split_wait(self.state_a, after, "chip_gather", "ag_a_wait")
        return _weights_a(_core_gather(by_chip, "ag_a_cores"))

    def forward_b(self, after):
        by_chip = _split_wait(self.state_b, after, "chip_gather", "ag_b_wait")
        self.state, token = _split_start(by_chip, "core_gather", "ag_b_cores_start")
        return token

    def weights_b(self, after):
        both = _split_wait(self.state, after, "core_gather", "ag_b_cores_wait")
        slabs = jnp.swapaxes(both, 0, 1).reshape((8,) + tuple(self.slab_b.shape))
        rows = _slab_rows(_PACK_B)
        out = _weights_b(slabs[:, :rows])
        pieces = slabs[:, rows:].reshape(8, -1)[:, :3 * 3 * 704].reshape(8, 3, 3, 704).astype(f32)
        out["conv_w"] = _cols_to_full((pieces[:, 0] + pieces[:, 1]) + pieces[:, 2])
        return out

    def grads_b_start(self, grads):
        self.state, token = _split_start(_by_core(_grad_slab_b(grads)), "core_swap", "rs_b_cores_start")
        return token

    def bwd_ride(self, after):
        from_sib, by_core = _split_wait(self.state, after, "core_swap", "rs_b_cores_wait")
        self.chip_sum = _pair_add(by_core, from_sib, "rs_b_pair_add")
        return (self.chip_sum, False)

    def grads_a_start(self, sections):
        self.state, token = _split_start(_grad_blocks_a(sections), "core_swap", "rs_a_cores_start")
        return token

    def grads_a_exchange(self, after):
        from_sib, by_core = _split_wait(self.state, after, "core_swap", "rs_a_cores_wait")
        self.state, token = _split_start(_pair_add(by_core, from_sib, "rs_a_pair_add"), "chip_xchg", "rs_a_start")
        return token

    def parts(self, got_b, after):
        parts = _unpack_rows(_fill_own(got_b, self.chip_sum, False), _PACK_B)
        parts["w_in"] = _split_wait(self.state, after, "chip_xchg", "rs_a_wait")
        return parts


def kernel(x, pre_mix_norm, w_in, rel_bias, hgrn_lb_raw, hgrn_norm, w_branch_attn, w_branch_hgrn, w_out, post_mix_norm, pre_ffn_norm, w_up, conv_w, conv_b, w_down, post_ffn_norm, loss_target, m_pre_mix_norm, m_w_in, m_rel_bias, m_hgrn_lb_raw, m_hgrn_norm, m_w_branch_attn, m_w_branch_hgrn, m_w_out, m_post_mix_norm, m_pre_ffn_norm, m_w_up, m_conv_w, m_conv_b, m_w_down, m_post_ffn_norm, v_pre_mix_norm, v_w_in, v_rel_bias, v_hgrn_lb_raw, v_hgrn_norm, v_w_branch_attn, v_w_branch_hgrn, v_w_out, v_post_mix_norm, v_pre_ffn_norm, v_w_up, v_conv_w, v_conv_b, v_w_down, v_post_ffn_norm):
    ci = lax.axis_index("c")
    dev = 4 * lax.axis_index("x") + 2 * lax.axis_index("y") + ci
    tr = lambda t: jnp.swapaxes(t[0], 0, 1)
    wts = dict(w_in=tr(w_in), w_ba=w_branch_attn[0], w_bh=w_branch_hgrn[0], w_out=w_out[0], w_up=tr(w_up),
               w_down=w_down[0])
    mom = dict(w_in=tr(m_w_in), w_ba=m_w_branch_attn[0], w_bh=m_w_branch_hgrn[0], w_out=m_w_out[0], w_up=tr(m_w_up),
               w_down=m_w_down[0])
    var = dict(w_in=tr(v_w_in), w_ba=v_w_branch_attn[0], w_bh=v_w_branch_hgrn[0], w_out=v_w_out[0], w_up=tr(v_w_up),
               w_down=v_w_down[0])
    small_w = dict(pre_mix_norm=pre_mix_norm, rel_bias=rel_bias, hgrn_lb_raw=hgrn_lb_raw, hgrn_norm=hgrn_norm,
                   post_mix_norm=post_mix_norm, pre_ffn_norm=pre_ffn_norm, conv_b=conv_b, post_ffn_norm=post_ffn_norm)
    small_m = dict(pre_mix_norm=m_pre_mix_norm, rel_bias=m_rel_bias, hgrn_lb_raw=m_hgrn_lb_raw, hgrn_norm=m_hgrn_norm,
                   post_mix_norm=m_post_mix_norm, pre_ffn_norm=m_pre_ffn_norm, conv_b=m_conv_b,
                   post_ffn_norm=m_post_ffn_norm)
    small_v = dict(pre_mix_norm=v_pre_mix_norm, rel_bias=v_rel_bias, hgrn_lb_raw=v_hgrn_lb_raw, hgrn_norm=v_hgrn_norm,
                   post_mix_norm=v_post_mix_norm, pre_ffn_norm=v_pre_ffn_norm, conv_b=v_conv_b,
                   post_ffn_norm=v_post_ffn_norm)

    plan = _Traffic(wts["w_in"].astype(bf16),
                    _pack_rows({k: wts[k].astype(bf16)[None] for k, _ in _PACK_B}, _PACK_B)[0], conv_w[0])

    loss8, grad_x, _, _, got_b, small = _local_step(x[0], loss_target[0], small_w, plan)
    parts = plan.parts(got_b, grad_x)
    outs_big = {}
    for k, _ in _PACK_SIZES:
        outs_big[k] = _adamw(wts[k], mom[k], var[k], parts[k], "adamw_" + k)

    spack = jnp.concatenate([_pack_small(small, loss8[0, 0:1]),
                             jnp.pad(small["conv_w"].reshape(-1, LANE), ((0, _CONVW_ROWS - 132), (0, 0)))], axis=0)
    allp = _core_gather(_chip_comm(spack, True, "ag_small_chips"), "ag_small_cores")
    ssum = _sum8(allp, "small_sum")
    gs = ssum[:_SMALL_ROWS]
    loss = ssum[_SMALL_USED // LANE, _SMALL_USED % LANE]
    res_small = _adamw(_pack_small(small_w), _pack_small(small_m), _pack_small(small_v), gs, "adamw_small")
    sm = [_unpack_small(t) for t in res_small]
    g_cw_full = ssum[_SMALL_ROWS:_SMALL_ROWS + 132].reshape(3, 2 * D_FF)
    g_cw = lax.dynamic_slice_in_dim(g_cw_full, dev * 704, 704, axis=1)
    res_cw = _adamw(conv_w[0], m_conv_w[0], v_conv_w[0], g_cw, "adamw_conv_w")

    def pick(i):
        def big_(k):
            t = outs_big[k][i]
            return (jnp.swapaxes(t, 0, 1) if k in _TRANSPOSED else t)[None]
        return [sm[i]["pre_mix_norm"], big_("w_in"), sm[i]["rel_bias"], sm[i]["hgrn_lb_raw"], sm[i]["hgrn_norm"],
                big_("w_ba"), big_("w_bh"), big_("w_out"), sm[i]["post_mix_norm"], sm[i]["pre_ffn_norm"],
                big_("w_up"), res_cw[i][None], sm[i]["conv_b"], big_("w_down"), sm[i]["post_ffn_norm"]]

    return (loss, grad_x[None], *pick(0), *pick(1), *pick(2), *pick(3))
```

```python
import functools
import math

import jax
import jax.numpy as jnp
from jax import lax
from jax.experimental import pallas as pl
from jax.experimental.pallas import tpu as pltpu

f32 = jnp.float32
bf16 = jnp.bfloat16
SDS = jax.ShapeDtypeStruct
HIGHEST = lax.Precision.HIGHEST
MESH = pl.DeviceIdType.MESH

NN = (((1,), (0,)), ((), ()))
NT = (((1,), (1,)), ((), ()))
TN = (((0,), (0,)), ((), ()))

D_MODEL = 1024
N_GROUPS = 3
DILATIONS = (1, 4, 16)
HEAD_DIM = 64
ATTN_BLOCK = 128
QKV_G = 1536
ATTN_OUT = 512
HGRN_W = 512
HGRN_CHUNK = 32
D_FF = 2816
NUM_BUCKETS = 32
MAX_EXACT = 16
MAX_DISTANCE = 2048
NEG_INF = -1e30
EPS = 1e-6
LANE = 128
SUBLANE = 8
VMEM_BIG = 48 * 1024 * 1024
MM_ROWS = 512
MM_OUT_BYTES = 8 * 1024 * 1024
ADAM_BLOCK_BYTES = 2304 * 1024

ADAM_LR, ADAM_B1, ADAM_B2, ADAM_EPS, ADAM_WD, ADAM_STEP = 0.001, 0.9, 0.999, 1e-08, 0.01, 10


def _pick(n, pref):
    t = pref
    while t >= LANE:
        if n % t == 0:
            return t
        t //= 2
    return n


def _cparams(sem=None, vmem=None):
    kw = {}
    if sem is not None:
        kw["dimension_semantics"] = sem
    if vmem is not None:
        kw["vmem_limit_bytes"] = vmem
    return pltpu.CompilerParams(**kw)


def _sigmoid(x):
    return jax.nn.sigmoid(x)


def _colsum8(x):
    return x.reshape(x.shape[0] // SUBLANE, SUBLANE, x.shape[1]).sum(axis=0)


def _mm(a, b, mode, out_dtype, name, acc=None, after=None):
    dims = {"nn": NN, "nt": NT, "tn": TN}[mode]
    has_acc = acc is not None
    parts = list(a) if isinstance(a, (list, tuple)) else [a]
    if mode == "tn":
        assert not has_acc
        K, N = b.shape
        widths = [t.shape[1] for t in parts]
        M = sum(widths)
        whole = M * N * 4 <= MM_OUT_BYTES
        assert whole or len(parts) == 1
        tmm = M if whole else M // 2
        ts = _pick(K, 4 * MM_ROWS)
        nk = K // ts

        npart = len(parts)
        narrow = out_dtype != f32

        def body_tn(*refs):
            b_ref, o_ref = refs[npart], refs[npart + 1]
            acc_ref = refs[npart + 2] if narrow else o_ref
            k = pl.program_id(1)
            bv = b_ref[...]
            lo = 0
            for a_ref, w in zip(refs[:npart], widths if whole else [tmm]):
                part = lax.dot_general(a_ref[...], bv, dims, preferred_element_type=f32)
                rows = slice(lo, lo + w)
                lo += w

                @pl.when(k == 0)
                def _(part=part, rows=rows):
                    acc_ref[rows, :] = part

                @pl.when(k > 0)
                def _(part=part, rows=rows):
                    acc_ref[rows, :] += part

            if narrow:
                @pl.when(k == nk - 1)
                def _():
                    o_ref[...] = acc_ref[...].astype(out_dtype)

        return pl.pallas_call(
            body_tn,
            grid=(M // tmm, nk),
            in_specs=[pl.BlockSpec((ts, w if whole else tmm), lambda i, k: (k, i)) for w in widths]
            + [pl.BlockSpec((ts, N), lambda i, k: (k, 0))],
            out_specs=pl.BlockSpec((tmm, N), lambda i, k: (i, 0)),
            out_shape=SDS((M, N), out_dtype),
            scratch_shapes=[pltpu.VMEM((tmm, N), f32)] if narrow else [],
            compiler_params=_cparams(("parallel", "arbitrary"), VMEM_BIG),
            name=name,
        )(*parts, b)

    bs = list(b) if isinstance(b, (list, tuple)) else [b]
    widths = [t.shape[1] for t in parts]
    M = parts[0].shape[0]
    kdim = 0 if mode == "nn" else 1
    N = bs[0].shape[1 - kdim]
    tm = _pick(M, MM_ROWS)
    npart, nb = len(parts), len(bs)
    place, bi, lo = [], 0, 0
    for w in widths:
        place.append((bi, lo))
        lo += w
        if lo == bs[bi].shape[kdim]:
            bi, lo = bi + 1, 0
    assert bi == nb and lo == 0

    def body(*refs):
        a_refs, b_refs = refs[:npart], refs[npart:npart + nb]
        c_ref = refs[npart + nb] if has_acc else None
        o_ref = refs[-1]
        part = None
        for a_ref, w, (bi, lo) in zip(a_refs, widths, place):
            b_ref = b_refs[bi]
            if w == bs[bi].shape[kdim]:
                bk = b_ref[...]
            else:
                bk = b_ref[:, lo:lo + w] if mode == "nt" else b_ref[lo:lo + w, :]
            t = lax.dot_general(a_ref[...], bk, dims, preferred_element_type=f32)
            part = t if part is None else part + t
        if has_acc:
            part = part + c_ref[...]
        o_ref[...] = part.astype(out_dtype)

    specs = [pl.BlockSpec((tm, w), lambda i: (i, 0)) for w in widths] \
        + [pl.BlockSpec(t.shape, lambda i: (0, 0)) for t in bs]
    args = parts + bs
    aliases = {}
    if has_acc:
        specs.append(pl.BlockSpec((tm, N), lambda i: (i, 0)))
        args.append(acc)
        aliases = {npart + nb: 0}
    if after is not None:
        specs.append(pl.BlockSpec(memory_space=pl.ANY))
        args.append(after)
    return pl.pallas_call(
        body,
        grid=(M // tm,),
        in_specs=specs,
        out_specs=pl.BlockSpec((tm, N), lambda i: (i, 0)),
        out_shape=SDS((M, N), out_dtype),
        input_output_aliases=aliases,
        compiler_params=_cparams(("parallel",), VMEM_BIG),
        name=name,
    )(*args)


def _mm_fanout(a, bs, mode, out_dtypes, name):
    dims = {"nn": NN, "nt": NT}[mode]
    M, K = a.shape
    ns = [b.shape[1] if mode == "nn" else b.shape[0] for b in bs]
    tm = _pick(M, MM_ROWS)
    nb = len(bs)

    def body(a_ref, *refs):
        av = a_ref[...]
        for b_ref, o_ref, dt in zip(refs[:nb], refs[nb:], out_dtypes):
            o_ref[...] = lax.dot_general(av, b_ref[...], dims, preferred_element_type=f32).astype(dt)

    return pl.pallas_call(
        body,
        grid=(M // tm,),
        in_specs=[pl.BlockSpec((tm, K), lambda i: (i, 0))] + [pl.BlockSpec(b.shape, lambda i: (0, 0)) for b in bs],
        out_specs=[pl.BlockSpec((tm, n), lambda i: (i, 0)) for n in ns],
        out_shape=[SDS((M, n), dt) for n, dt in zip(ns, out_dtypes)],
        compiler_params=_cparams(("parallel",), VMEM_BIG),
        name=name,
    )(a, *bs)


PERM_ROWS = 1024


def _perm_spec(d, cols=LANE):
    return pl.BlockSpec((d, PERM_ROWS // d, cols), lambda i, j: (0, i, j))


def _to_natural(src_ref, dst_ref, d):
    n = src_ref.shape[1]
    for r in range(d):
        dst_ref[pl.ds(r, n, stride=d), :] = src_ref[r]


def _prep(x, w, after=None):
    S, D = x.shape
    R = PERM_ROWS
    nc = D // LANE
    n_in = nc + 1 + (after is not None)

    def body(*refs):
        x_refs, w_ref = refs[:nc], refs[nc]
        h_ref, h4_ref, h16_ref, rs = refs[n_in:]
        ssq = None
        for xr in x_refs:
            v = xr[...]
            t = jnp.sum(v * v, axis=-1, keepdims=True)
            ssq = t if ssq is None else ssq + t
        rinv = lax.rsqrt(ssq * (1.0 / D) + EPS)
        rs[...] = jnp.broadcast_to(rinv, (R, LANE))
        for j, xr in enumerate(x_refs):
            cols = slice(j * LANE, (j + 1) * LANE)
            wj = w_ref[:, cols]
            h_ref[:, cols] = ((xr[...] * rinv) * wj).astype(bf16)
            for d, o_ref in ((4, h4_ref), (16, h16_ref)):
                n = R // d
                for r in range(d):
                    rows = pl.ds(r, n, stride=d)
                    o_ref[r, :, cols] = ((xr[rows, :] * rs[rows, :]) * wj).astype(bf16)

    col = lambda j: pl.BlockSpec((R, LANE), lambda i, j=j: (i, j))
    h, h4, h16 = pl.pallas_call(
        body,
        grid=(S // R,),
        in_specs=[col(j) for j in range(nc)] + [pl.BlockSpec((1, D), lambda i: (0, 0))]
        + ([] if after is None else [pl.BlockSpec(memory_space=pl.ANY)]),
        out_specs=[pl.BlockSpec((R, D), lambda i: (i, 0)), pl.BlockSpec((4, R // 4, D), lambda i: (0, i, 0)),
                   pl.BlockSpec((16, R // 16, D), lambda i: (0, i, 0))],
        out_shape=[SDS((S, D), bf16), SDS((4, S // 4, D), bf16), SDS((16, S // 16, D), bf16)],
        scratch_shapes=[pltpu.VMEM((R, LANE), f32)],
        compiler_params=_cparams(("parallel",), VMEM_BIG),
        name="prep_norm_perm",
    )(*([x] * nc), w, *([] if after is None else [after]))
    return [h, h4.reshape(S, D), h16.reshape(S, D)]


def _dh_sum(a, b, c):
    S, D = a.shape
    R = PERM_ROWS

    def body(a_ref, b_ref, c_ref, o_ref, sb, sc):
        _to_natural(b_ref, sb, 4)
        _to_natural(c_ref, sc, 16)
        o_ref[...] = (a_ref[...] + sb[...]) + sc[...]

    nat = pl.BlockSpec((R, LANE), lambda i, j: (i, j))
    return pl.pallas_call(
        body,
        grid=(S // R, D // LANE),
        in_specs=[nat, _perm_spec(4), _perm_spec(16)],
        out_specs=nat,
        out_shape=SDS((S, D), f32),
        scratch_shapes=[pltpu.VMEM((R, LANE), f32)] * 2,
        compiler_params=_cparams(("parallel", "parallel")),
        name="dh_sum",
    )(a, b.reshape(4, S // 4, D), c.reshape(16, S // 16, D))


def _rms_parts(xv):
    r = lax.rsqrt(jnp.mean(xv * xv, axis=-1, keepdims=True) + EPS)
    return r, xv * r


def _rms_bwd(xhat, r, w, dy):
    dyw = dy * w
    return r * (dyw - xhat * jnp.mean(dyw * xhat, axis=-1, keepdims=True))


def _mid_fwd(x, merged, w_out, w_pm, w_pf):
    S, D = x.shape
    tm = _pick(S, MM_ROWS)

    def body(x_ref, m_ref, wo_ref, wpm_ref, wpf_ref, mo_ref, x1_ref, h2_ref):
        mo = jnp.dot(m_ref[...], wo_ref[...], preferred_element_type=f32)
        mo_ref[...] = mo
        _, moh = _rms_parts(mo)
        x1 = x_ref[...] + moh * wpm_ref[...]
        x1_ref[...] = x1
        _, x1h = _rms_parts(x1)
        h2_ref[...] = (x1h * wpf_ref[...]).astype(bf16)

    row = pl.BlockSpec((tm, D), lambda i: (i, 0))
    vec = pl.BlockSpec((1, D), lambda i: (0, 0))
    return pl.pallas_call(
        body,
        grid=(S // tm,),
        in_specs=[row, pl.BlockSpec((tm, merged.shape[1]), lambda i: (i, 0)),
                  pl.BlockSpec(w_out.shape, lambda i: (0, 0)), vec, vec],
        out_specs=[row, row, row],
        out_shape=[SDS((S, D), f32), SDS((S, D), f32), SDS((S, D), bf16)],
        compiler_params=_cparams(("parallel",), VMEM_BIG),
        name="out_proj_mid_fwd",
    )(x, merged, w_out, w_pm, w_pf)


def _final(x1, act, w_down, tgt, w_pfn):
    S, D = x1.shape
    tm = _pick(S, MM_ROWS)
    nt = S // tm

    def body(x1_ref, a_ref, wd_ref, t_ref, w_ref, loss_ref, dy_ref, dfo_ref, gw_ref, lacc, gacc):
        i = pl.program_id(0)

        @pl.when(i == 0)
        def _():
            lacc[...] = jnp.zeros_like(lacc)
            gacc[...] = jnp.zeros_like(gacc)

        w = w_ref[...]
        r, foh = _rms_parts(jnp.dot(a_ref[...], wd_ref[...], preferred_element_type=f32))
        y = x1_ref[...] + foh * w
        err = y - t_ref[...]
        lacc[...] += _colsum8(err * err)
        dy = err * (1.0 / D)
        dy_ref[...] = dy
        gacc[...] += _colsum8(dy * foh)
        dfo_ref[...] = _rms_bwd(foh, r, w, dy).astype(bf16)

        @pl.when(i == nt - 1)
        def _():
            loss_ref[...] = jnp.full((SUBLANE, LANE), 0.5 / D, f32) * jnp.sum(lacc[...])
            gw_ref[...] = jnp.sum(gacc[...], axis=0, keepdims=True)

    row = pl.BlockSpec((tm, D), lambda i: (i, 0))
    vec = pl.BlockSpec((1, D), lambda i: (0, 0))
    return pl.pallas_call(
        body,
        grid=(nt,),
        in_specs=[row, pl.BlockSpec((tm, act.shape[1]), lambda i: (i, 0)),
                  pl.BlockSpec(w_down.shape, lambda i: (0, 0)), row, vec],
        out_specs=[pl.BlockSpec((SUBLANE, LANE), lambda i: (0, 0)), row, row, vec],
        out_shape=[SDS((SUBLANE, LANE), f32), SDS((S, D), f32), SDS((S, D), bf16), SDS((1, D), f32)],
        scratch_shapes=[pltpu.VMEM((SUBLANE, D), f32), pltpu.VMEM((SUBLANE, D), f32)],
        compiler_params=_cparams(("arbitrary",), VMEM_BIG),
        name="down_proj_final_loss",
    )(x1, act, w_down, tgt, w_pfn)


MID_BWD_ROWS = 256


def _mid_bwd(dy, dug, duv, wt_g, wt_v, x1, mo, w_pf, w_pm):
    S, D = dy.shape
    tm = _pick(S, MID_BWD_ROWS)
    nt = S // tm

    def body(dy_ref, dug_ref, duv_ref, wg_ref, wv_ref, x1_ref, mo_ref, wpf_ref, wpm_ref,
             dx1_ref, dmo_ref, gpf_ref, gpm_ref, apf, apm):
        i = pl.program_id(0)

        @pl.when(i == 0)
        def _():
            apf[...] = jnp.zeros_like(apf)
            apm[...] = jnp.zeros_like(apm)

        r1, x1h = _rms_parts(x1_ref[...])
        dh2 = jnp.dot(dug_ref[...], wg_ref[...], preferred_element_type=f32) \
            + jnp.dot(duv_ref[...], wv_ref[...], preferred_element_type=f32)
        apf[...] += _colsum8(dh2 * x1h)
        dx1 = dy_ref[...] + _rms_bwd(x1h, r1, wpf_ref[...], dh2)
        dx1_ref[...] = dx1
        rm, moh = _rms_parts(mo_ref[...])
        apm[...] += _colsum8(dx1 * moh)
        dmo_ref[...] = _rms_bwd(moh, rm, wpm_ref[...], dx1).astype(bf16)

        @pl.when(i == nt - 1)
        def _():
            gpf_ref[...] = jnp.sum(apf[...], axis=0, keepdims=True)
            gpm_ref[...] = jnp.sum(apm[...], axis=0, keepdims=True)

    row = pl.BlockSpec((tm, D), lambda i: (i, 0))
    vec = pl.BlockSpec((1, D), lambda i: (0, 0))
    return pl.pallas_call(
        body,
        grid=(nt,),
        in_specs=[row, pl.BlockSpec((tm, dug.shape[1]), lambda i: (i, 0)), pl.BlockSpec((tm, duv.shape[1]), lambda i: (i, 0)),
                  pl.BlockSpec(wt_g.shape, lambda i: (0, 0)), pl.BlockSpec(wt_v.shape, lambda i: (0, 0)),
                  row, row, vec, vec],
        out_specs=[row, row, vec, vec],
        out_shape=[SDS((S, D), f32), SDS((S, D), bf16), SDS((1, D), f32), SDS((1, D), f32)],
        scratch_shapes=[pltpu.VMEM((SUBLANE, D), f32), pltpu.VMEM((SUBLANE, D), f32)],
        compiler_params=_cparams(("arbitrary",), VMEM_BIG),
        name="dh2_mid_bwd",
    )(dy, dug, duv, wt_g, wt_v, x1, mo, w_pf, w_pm)


def _first_bwd(x, dx1, dh, w_pre):
    S, D = x.shape
    tm = _pick(S, 512)
    nt = S // tm

    def body(x_ref, dx1_ref, a_ref, w_ref, gx_ref, gw_ref, acc):
        i = pl.program_id(0)

        @pl.when(i == 0)
        def _():
            acc[...] = jnp.zeros_like(acc)

        r, xh = _rms_parts(x_ref[...])
        dh = a_ref[...]
        acc[...] += _colsum8(dh * xh)
        gx_ref[...] = dx1_ref[...] + _rms_bwd(xh, r, w_ref[...], dh)

        @pl.when(i == nt - 1)
        def _():
            gw_ref[...] = jnp.sum(acc[...], axis=0, keepdims=True)

    row = pl.BlockSpec((tm, D), lambda i: (i, 0))
    vec = pl.BlockSpec((1, D), lambda i: (0, 0))
    return pl.pallas_call(
        body,
        grid=(nt,),
        in_specs=[row, row, row, vec],
        out_specs=[row, vec],
        out_shape=[SDS((S, D), f32), SDS((1, D), f32)],
        scratch_shapes=[pltpu.VMEM((SUBLANE, D), f32)],
        compiler_params=_cparams(("arbitrary",)),
        name="first_bwd",
    )(x, dx1, dh, w_pre)


def _t5_bucket(dist):
    n = jnp.maximum(dist, 0)
    nf = jnp.maximum(n, 1).astype(f32)
    large = MAX_EXACT + (jnp.log(nf / MAX_EXACT) / math.log(MAX_DISTANCE / MAX_EXACT)
                         * (NUM_BUCKETS - MAX_EXACT)).astype(jnp.int32)
    large = jnp.minimum(large, NUM_BUCKETS - 1)
    return jnp.where(n < MAX_EXACT, n, large)


def _bias_consts(d):
    blk = ATTN_BLOCK
    rel = jnp.arange(blk)[:, None] + blk - jnp.arange(2 * blk)[None, :]
    in_win = (rel >= 0) & (rel <= blk)
    bucket = _t5_bucket(rel * d).reshape(1, -1)
    onehot = (bucket == jnp.arange(NUM_BUCKETS)[:, None]).astype(f32)
    return onehot, in_win.astype(f32).reshape(1, -1)


def _bias_build(tab_t, onehot, maskf, name, after):
    H = tab_t.shape[0]

    def body(t_ref, oh_ref, m_ref, after_ref, o_ref):
        b = jnp.dot(t_ref[...], oh_ref[...], precision=HIGHEST, preferred_element_type=f32)
        o_ref[...] = jnp.where(m_ref[...] > 0.5, b, NEG_INF)

    vm = pl.BlockSpec(memory_space=pltpu.VMEM)
    return pl.pallas_call(body, out_shape=SDS((H, onehot.shape[1]), f32), name=name,
                          in_specs=[vm, vm, vm, pl.BlockSpec(memory_space=pl.ANY)], out_specs=vm,
                          )(tab_t, onehot, maskf, after)


def _bias_grad(dbias_flat, onehot, name):
    H = dbias_flat.shape[0]

    def body(g_ref, oh_ref, o_ref):
        o_ref[...] = lax.dot_general(oh_ref[...], g_ref[...], NT, precision=HIGHEST, preferred_element_type=f32)

    return pl.pallas_call(body, out_shape=SDS((NUM_BUCKETS, H), f32), name=name)(dbias_flat, onehot)


ATTN_TILE = 512
ATTN_SUB = ATTN_TILE // ATTN_BLOCK
ATTN_HP = 4
ATTN_WIDE = ATTN_HP * LANE


def _qkv_specs(nt):
    tile = (ATTN_TILE, ATTN_WIDE)
    blk = (ATTN_BLOCK, ATTN_WIDE)
    sec = ATTN_OUT // ATTN_WIDE
    cur = lambda off: (lambda h, t: (jnp.minimum(t, nt - 1), off + h))
    prev = lambda off: (lambda h, t: (jnp.maximum(jnp.minimum(t, nt - 1) * ATTN_SUB - 1, 0), off + h))
    return [pl.BlockSpec(tile, cur(0)), pl.BlockSpec(blk, prev(sec)), pl.BlockSpec(tile, cur(sec)),
            pl.BlockSpec(blk, prev(2 * sec)), pl.BlockSpec(tile, cur(2 * sec))]


def _head_masks():
    lane = lax.broadcasted_iota(jnp.int32, (ATTN_BLOCK, LANE), 1)
    return lane < HEAD_DIM


def _stack_heads(x2, low):
    zero = jnp.zeros_like(x2)
    return jnp.concatenate([jnp.where(low, x2, zero), jnp.where(low, zero, x2)], axis=0)


def _attn_fwd(qkv, bias, bps, name, after=None):
    S = qkv.shape[0]
    nt = S // ATTN_TILE
    scale = HEAD_DIM ** -0.5

    def body(q_ref, kp_ref, kc_ref, vp_ref, vc_ref, b_ref, *rest):
        o_ref, l_ref = rest[-2:]
        t = pl.program_id(1)
        low = _head_masks()
        col = lax.broadcasted_iota(jnp.int32, (2 * ATTN_BLOCK, 2 * ATTN_BLOCK), 1)
        for hp in range(ATTN_HP):
            cols = slice(hp * LANE, (hp + 1) * LANE)
            kk = jnp.concatenate([kp_ref[:, cols], kc_ref[:, cols]], axis=0)
            vv = jnp.concatenate([vp_ref[:, cols], vc_ref[:, cols]], axis=0)
            bias2 = b_ref[2 * hp:2 * hp + 2].reshape(2 * ATTN_BLOCK, 2 * ATTN_BLOCK)
            for b in range(ATTN_SUB):
                lo = b * ATTN_BLOCK
                rows = slice(lo, lo + ATTN_BLOCK)
                keys = slice(lo, lo + 2 * ATTN_BLOCK)
                dead = jnp.logical_and((t * ATTN_SUB + b) % bps == 0, col < ATTN_BLOCK)
                q2 = _stack_heads(q_ref[rows, cols], low)
                kb, vb = kk[keys], vv[keys]
                s = lax.dot_general(q2, kb, NT, preferred_element_type=f32) * scale + bias2
                s = jnp.where(dead, NEG_INF, s)
                m = jnp.max(s, axis=-1, keepdims=True)
                p = jnp.exp(s - m)
                l = jnp.sum(p, axis=-1, keepdims=True)
                o2 = jnp.dot(p.astype(bf16), vb, preferred_element_type=f32) / l
                lse = m + jnp.log(l)
                o_ref[rows, cols] = jnp.where(low, o2[:ATTN_BLOCK], o2[ATTN_BLOCK:])
                l_ref[rows, cols] = jnp.where(low, lse[:ATTN_BLOCK], lse[ATTN_BLOCK:])

    tile = pl.BlockSpec((ATTN_TILE, ATTN_WIDE), lambda h, t: (t, h))
    return pl.pallas_call(
        body,
        grid=(4 // ATTN_HP, nt),
        in_specs=_qkv_specs(nt) + [pl.BlockSpec((2 * ATTN_HP, ATTN_BLOCK, 2 * ATTN_BLOCK), lambda h, t: (h, 0, 0))]
        + ([] if after is None else [pl.BlockSpec(memory_space=pl.ANY)]),
        out_specs=[tile, tile],
        out_shape=[SDS((S, ATTN_OUT), f32), SDS((S, ATTN_OUT), f32)],
        compiler_params=_cparams(("parallel", "parallel")),
        name=name,
    )(qkv, qkv, qkv, qkv, qkv, bias, *([] if after is None else [after]))


def _attn_bwd(qkv, bias, do, dvec, lse, bps, name):
    S = qkv.shape[0]
    nt = S // ATTN_TILE
    scale = HEAD_DIM ** -0.5

    def assemble(parts):
        rows = [parts[0][:ATTN_BLOCK]]
        for b in range(ATTN_SUB - 1):
            rows.append(parts[b][ATTN_BLOCK:] + parts[b + 1][:ATTN_BLOCK])
        rows.append(parts[-1][ATTN_BLOCK:])
        return rows

    def body(q_ref, kp_ref, kc_ref, vp_ref, vc_ref, b_ref, do_ref, dvec_ref, lse_ref,
             dq_ref, dk_ref, dv_ref, db_ref, ck, cv):
        t = pl.program_id(1)
        last = ATTN_TILE - ATTN_BLOCK

        @pl.when(t == 0)
        def _():
            ck[...] = jnp.zeros_like(ck)
            cv[...] = jnp.zeros_like(cv)
            db_ref[...] = jnp.zeros_like(db_ref)

        @pl.when(t < nt)
        def _():
            low = _head_masks()
            col = lax.broadcasted_iota(jnp.int32, (2 * ATTN_BLOCK, 2 * ATTN_BLOCK), 1)
            per_row = lambda t2: jnp.concatenate([t2[:, 0:1], t2[:, HEAD_DIM:HEAD_DIM + 1]], axis=0)
            for hp in range(ATTN_HP):
                cols = slice(hp * LANE, (hp + 1) * LANE)
                kk = jnp.concatenate([kp_ref[:, cols], kc_ref[:, cols]], axis=0)
                vv = jnp.concatenate([vp_ref[:, cols], vc_ref[:, cols]], axis=0)
                bias2 = b_ref[2 * hp:2 * hp + 2].reshape(2 * ATTN_BLOCK, 2 * ATTN_BLOCK)
                dk_parts, dv_parts = [], []
                dsum = None
                for b in range(ATTN_SUB):
                    lo = b * ATTN_BLOCK
                    rows = slice(lo, lo + ATTN_BLOCK)
                    keys = slice(lo, lo + 2 * ATTN_BLOCK)
                    dead = jnp.logical_and((t * ATTN_SUB + b) % bps == 0, col < ATTN_BLOCK)
                    q2 = _stack_heads(q_ref[rows, cols], low)
                    do2 = _stack_heads(do_ref[rows, cols].astype(bf16), low)
                    kb, vb = kk[keys], vv[keys]
                    s = lax.dot_general(q2, kb, NT, preferred_element_type=f32) * scale + bias2
                    s = jnp.where(dead, NEG_INF, s)
                    p = jnp.exp(s - per_row(lse_ref[rows, cols]))
                    dp = lax.dot_general(do2, vb, NT, preferred_element_type=f32)
                    ds = p * (dp - per_row(dvec_ref[rows, cols]))
                    dsum = ds if dsum is None else dsum + ds
                    dsb = ds.astype(bf16)
                    dq2 = jnp.dot(dsb, kb, preferred_element_type=f32) * scale
                    dq_ref[rows, cols] = jnp.where(low, dq2[:ATTN_BLOCK], dq2[ATTN_BLOCK:]).astype(bf16)
                    dk_parts.append(lax.dot_general(dsb, q2, TN, preferred_element_type=f32) * scale)
                    dv_parts.append(lax.dot_general(p.astype(bf16), do2, TN, preferred_element_type=f32))
                db_ref[2 * hp:2 * hp + 2] += dsum.reshape(2, ATTN_BLOCK, 2 * ATTN_BLOCK)
                for parts, carry, out_ref in ((dk_parts, ck, dk_ref), (dv_parts, cv, dv_ref)):
                    rws = assemble(parts)
                    out_ref[:last, cols] = carry[:last, cols].astype(bf16)
                    out_ref[last:, cols] = (carry[last:, cols] + rws[0]).astype(bf16)
                    for b in range(ATTN_SUB):
                        carry[b * ATTN_BLOCK:(b + 1) * ATTN_BLOCK, cols] = rws[b + 1]

        @pl.when(t == nt)
        def _():
            dk_ref[...] = ck[...].astype(bf16)
            dv_ref[...] = cv[...].astype(bf16)

    tile = (ATTN_TILE, ATTN_WIDE)
    cur = pl.BlockSpec(tile, lambda h, t: (jnp.minimum(t, nt - 1), h))
    lag = pl.BlockSpec(tile, lambda h, t: (jnp.maximum(t - 1, 0), h))
    bspec = pl.BlockSpec((2 * ATTN_HP, ATTN_BLOCK, 2 * ATTN_BLOCK), lambda h, t: (h, 0, 0))
    return pl.pallas_call(
        body,
        grid=(4 // ATTN_HP, nt + 1),
        in_specs=_qkv_specs(nt) + [bspec, cur, cur, cur],
        out_specs=[cur, lag, lag, bspec],
        out_shape=[SDS((S, ATTN_OUT), bf16), SDS((S, ATTN_OUT), bf16), SDS((S, ATTN_OUT), bf16),
                   SDS((8, ATTN_BLOCK, 2 * ATTN_BLOCK), f32)],
        scratch_shapes=[pltpu.VMEM(tile, f32), pltpu.VMEM(tile, f32)],
        compiler_params=_cparams(("parallel", "arbitrary")),
        name=name,
    )(qkv, qkv, qkv, qkv, qkv, bias, do, dvec, lse)


def _attn_merge(o0, o1, o2, l0, l1, l2):
    S, W = o0.shape
    R = PERM_ROWS

    def body(o0_ref, o1_ref, o2_ref, l0_ref, l1_ref, l2_ref, y_ref, yb_ref, w0_ref, w1_ref, w2_ref,
             so1, so2, sl1, sl2):
        _to_natural(o1_ref, so1, 4)
        _to_natural(l1_ref, sl1, 4)
        _to_natural(o2_ref, so2, 16)
        _to_natural(l2_ref, sl2, 16)
        a, b, c = l0_ref[...], sl1[...], sl2[...]
        m = jnp.maximum(jnp.maximum(a, b), c)
        ea, eb, ec = jnp.exp(a - m), jnp.exp(b - m), jnp.exp(c - m)
        den = (ea + eb) + ec
        w0, w1, w2 = ea / den, eb / den, ec / den
        y = (w0 * o0_ref[...] + w1 * so1[...]) + w2 * so2[...]
        y_ref[...] = y
        yb_ref[...] = y.astype(bf16)
        w0_ref[...] = w0
        w1_ref[...] = w1
        w2_ref[...] = w2

    nat = pl.BlockSpec((R, LANE), lambda i, j: (i, j))
    v4 = lambda t: t.reshape(4, S // 4, W)
    v16 = lambda t: t.reshape(16, S // 16, W)
    return pl.pallas_call(
        body,
        grid=(S // R, W // LANE),
        in_specs=[nat, _perm_spec(4), _perm_spec(16)] * 2,
        out_specs=[nat] * 5,
        out_shape=[SDS((S, W), f32), SDS((S, W), bf16)] + [SDS((S, W), f32)] * 3,
        scratch_shapes=[pltpu.VMEM((R, LANE), f32)] * 4,
        compiler_params=_cparams(("parallel", "parallel")),
        name="attn_merge",
    )(o0, v4(o1), v16(o2), l0, v4(l1), v16(l2))


def _attn_merge_bwd(dy, y, w0, w1, w2, after=None):
    S, W = dy.shape
    R = PERM_ROWS

    def body(dy_ref, y_ref, w0_ref, w1_ref, w2_ref, *rest):
        a0, a1, a2, b0, b1, b2, sa, sb = rest[-8:]
        dyv = dy_ref[...]
        r = lax.broadcasted_iota(jnp.int32, (LANE, LANE), 0) // HEAD_DIM
        c = lax.broadcasted_iota(jnp.int32, (LANE, LANE), 1) // HEAD_DIM
        seg = jnp.where(r == c, 1.0, 0.0).astype(f32)
        cbar = jnp.dot(dyv * y_ref[...], seg, precision=HIGHEST, preferred_element_type=f32)
        w = w0_ref[...]
        a0[...] = (w * dyv).astype(bf16)
        b0[...] = w * cbar
        for d, w_ref, a_ref, b_ref in ((4, w1_ref, a1, b1), (16, w2_ref, a2, b2)):
            w = w_ref[...]
            sa[...] = w * dyv
            sb[...] = w * cbar
            n = R // d
            for k in range(d):
                rows = pl.ds(k, n, stride=d)
                a_ref[k] = sa[rows, :].astype(bf16)
                b_ref[k] = sb[rows, :]

    nat = pl.BlockSpec((R, LANE), lambda i, j: (i, j))
    shapes = lambda dt: [SDS((S, W), dt), SDS((4, S // 4, W), dt), SDS((16, S // 16, W), dt)]
    outs = pl.pallas_call(
        body,
        grid=(S // R, W // LANE),
        in_specs=[nat] * 5 + ([] if after is None else [pl.BlockSpec(memory_space=pl.ANY)]),
        out_specs=[nat, _perm_spec(4), _perm_spec(16)] * 2,
        out_shape=shapes(bf16) + shapes(f32),
        scratch_shapes=[pltpu.VMEM((R, LANE), f32)] * 2,
        compiler_params=_cparams(("parallel", "parallel")),
        name="attn_merge_bwd",
    )(dy, y, w0, w1, w2, *([] if after is None else [after]))
    return [t.reshape(S, W) for t in outs]


HGRN_SB = 256
HGRN_PAIR = 4


def _chunk_masks():
    r = jnp.arange(HGRN_SB)[:, None]
    c = jnp.arange(HGRN_SB)[None, :]
    same = (r // HGRN_CHUNK) == (c // HGRN_CHUNK)
    return jnp.stack([same & (c <= r), same, same & (c >= r)]).astype(bf16)


def _mask_dot(mask, x):
    hi = x.astype(bf16)
    r1 = x - hi.astype(f32)
    mid = r1.astype(bf16)
    lo = (r1 - mid.astype(f32)).astype(bf16)
    p = jnp.dot(mask, jnp.concatenate([hi, mid, lo], axis=1), preferred_element_type=f32)
    n = x.shape[1]
    return (p[:, :n] + p[:, n:2 * n]) + p[:, 2 * n:]


def _hgrn_prep(q_raw, f_raw, lbv, tril, same):
    sq = _sigmoid(q_raw)
    qs = q_raw * sq
    sig = _sigmoid(f_raw)
    f = lbv + (1.0 - lbv) * sig
    g = jnp.log(f)
    k = 1.0 - f
    G = _mask_dot(tril, g)
    GL = _mask_dot(same, g)
    eG = jnp.exp(G)
    einv = jnp.exp(-G)
    edec = jnp.exp(GL - G)
    return dict(sq=sq, qs=qs, sig=sig, f=f, k=k, eG=eG, einv=einv, edec=edec, eGL=jnp.exp(GL),
                qt=qs * eG, kt=k * einv, kd=k * edec)


def _ride_split(ride, rest, n_out, n_scratch):
    if ride is None:
        return None, rest[:n_out], None, rest[n_out:], None
    return rest[0], rest[1:1 + n_out], rest[1 + n_out], rest[2 + n_out:2 + n_out + n_scratch], rest[2 + n_out + n_scratch:]


def _hgrn_fwd(hg, lb, normw, ride=None):
    S = hg.shape[0]
    sb = HGRN_SB
    nsb = S // sb
    nch = sb // HGRN_CHUNK

    def body(q_ref, f_ref, v_ref, og_ref, lb_ref, nw_ref, m_ref, *rest):
        src_ref, (y_ref, o_ref, ck_ref), got_ref, (st,), sems = _ride_split(ride, rest, 3, 1)
        j = pl.program_id(1)
        if ride is not None:
            @pl.when(jnp.logical_and(pl.program_id(0) == 0, j == 0))
            def _():
                _chip_start(src_ref, got_ref, sems[0], sems[1], ride[1])

        @pl.when(j == 0)
        def _():
            st[...] = jnp.zeros_like(st)

        tril_m = m_ref[0]
        tril = tril_m.astype(f32) > 0.5

        def one_head(hh):
            cols = slice(hh * LANE, (hh + 1) * LANE)
            ST = st[hh]
            ck_ref[hh, 0] = ST
            pr = _hgrn_prep(q_ref[:, cols], f_ref[:, cols], lb_ref[:, cols], tril_m, m_ref[1])
            qtb, ktb, kdb = pr["qt"].astype(bf16), pr["kt"].astype(bf16), pr["kd"].astype(bf16)
            eGL = pr["eGL"]
            vb = v_ref[:, cols].astype(bf16)
            A = jnp.where(tril, lax.dot_general(qtb, ktb, NT, preferred_element_type=f32), 0.0)
            o = jnp.dot(A.astype(bf16), vb, preferred_element_type=f32)
            outs = []
            for ci in range(nch):
                lo = ci * HGRN_CHUNK
                sl = slice(lo, lo + HGRN_CHUNK)
                outs.append(o[sl] + lax.dot_general(qtb[sl], ST.astype(bf16), NT, preferred_element_type=f32))
                ST = ST * eGL[lo:lo + 1, :] + lax.dot_general(vb[sl], kdb[sl], TN, preferred_element_type=f32)
            st[hh] = ST
            of = jnp.concatenate(outs, axis=0)
            o_ref[:, cols] = of
            rms = lax.rsqrt(jnp.mean(of * of, axis=-1, keepdims=True) + EPS)
            ogv = og_ref[:, cols]
            y_ref[:, cols] = ((of * rms * nw_ref[...]) * (ogv * _sigmoid(ogv))).astype(bf16)

        for hh in range(HGRN_PAIR):
            one_head(hh)

        if ride is not None:
            @pl.when(jnp.logical_and(pl.program_id(0) == ngrp - 1, j == nsb - 1))
            def _():
                _chip_finish(src_ref, got_ref, sems[0], sems[1], ride[1])

    wide = HGRN_PAIR * LANE
    ngrp = 4 // HGRN_PAIR
    col = lambda off: pl.BlockSpec((sb, wide), lambda h, j: (j, off // HGRN_PAIR + h))
    riding = ride is not None
    res = pl.pallas_call(
        body,
        grid=(ngrp, nsb),
        in_specs=[col(0), col(4), col(8), col(12), pl.BlockSpec((1, wide), lambda h, j: (0, h)),
                  pl.BlockSpec((1, LANE), lambda h, j: (0, 0)),
                  pl.BlockSpec((3, sb, sb), lambda h, j: (0, 0, 0))] + ([_ANY] if riding else []),
        out_specs=[col(0), col(0), pl.BlockSpec((HGRN_PAIR, 1, LANE, LANE), lambda h, j: (h, j, 0, 0))]
        + ([_ANY] if riding else []),
        out_shape=[SDS((S, HGRN_W), bf16), SDS((S, HGRN_W), f32), SDS((4, nsb, LANE, LANE), f32)]
        + ([_chip_out_shape(*ride)] if riding else []),
        scratch_shapes=[pltpu.VMEM((HGRN_PAIR, LANE, LANE), f32)] + (list(_CHIP_SEMS) if riding else []),
        compiler_params=_cparams(("arbitrary", "arbitrary") if riding else ("parallel", "arbitrary")),
        name="hgrn_fwd",
    )(hg, hg, hg, hg, lb, normw, _chunk_masks(), *([ride[0]] if riding else []))
    return tuple(res) if riding else (*res, None)


def _hgrn_bwd(hg, o_raw, dy, ck, lb, normw, ride=None):
    S = hg.shape[0]
    sb = HGRN_SB
    nsb = S // sb
    nch = sb // HGRN_CHUNK

    def body(q_ref, f_ref, v_ref, og_ref, o_ref, dy_ref, ck_ref, lb_ref, nw_ref, m_ref, *rest):
        src_ref, outs, got_ref, (dst, alb, anw), sems = _ride_split(ride, rest, 6, 3)
        dq_ref, df_ref, dv_ref, dog_ref, glb_ref, gnw_ref = outs
        j = pl.program_id(1)
        if ride is not None:
            @pl.when(jnp.logical_and(pl.program_id(0) == 0, j == 0))
            def _():
                _chip_start(src_ref, got_ref, sems[0], sems[1], ride[1])

        @pl.when(j == 0)
        def _():
            dst[...] = jnp.zeros_like(dst)
            alb[...] = jnp.zeros_like(alb)
            anw[...] = jnp.zeros_like(anw)

        tril_m = m_ref[0]
        tril = tril_m.astype(f32) > 0.5
        nw = nw_ref[...]

        def one_head(hh):
            cols = slice(hh * LANE, (hh + 1) * LANE)
            lbv = lb_ref[:, cols]
            q_raw = q_ref[:, cols]
            pr = _hgrn_prep(q_raw, f_ref[:, cols], lbv, tril_m, m_ref[1])
            qt, kt, kd, eGL = pr["qt"], pr["kt"], pr["kd"], pr["eGL"]
            qtb, ktb, kdb = qt.astype(bf16), kt.astype(bf16), kd.astype(bf16)
            vb = v_ref[:, cols].astype(bf16)

            o = o_ref[:, cols]
            ogv = og_ref[:, cols]
            sog = _sigmoid(ogv)
            rms = lax.rsqrt(jnp.mean(o * o, axis=-1, keepdims=True) + EPS)
            oh = o * rms
            dyv = dy_ref[:, cols]
            dog_ref[:, cols] = (dyv * (oh * nw) * (sog * (1.0 + ogv * (1.0 - sog)))).astype(bf16)
            dohw = dyv * (ogv * sog)
            anw[:, cols] += _colsum8(dohw * oh)
            doh = dohw * nw
            do = rms * (doh - oh * jnp.mean(doh * oh, axis=-1, keepdims=True))
            dob = do.astype(bf16)

            Ab = jnp.where(tril, lax.dot_general(qtb, ktb, NT, preferred_element_type=f32), 0.0).astype(bf16)
            dAb = jnp.where(tril, lax.dot_general(dob, vb, NT, preferred_element_type=f32), 0.0).astype(bf16)
            dv_acc = lax.dot_general(Ab, dob, TN, preferred_element_type=f32)
            dqt = jnp.dot(dAb, ktb, preferred_element_type=f32)
            dkt = lax.dot_general(dAb, qtb, TN, preferred_element_type=f32)

            ST = ck_ref[hh, 0]
            states = []
            for ci in range(nch):
                lo = ci * HGRN_CHUNK
                sl = slice(lo, lo + HGRN_CHUNK)
                states.append(ST)
                ST = ST * eGL[lo:lo + 1, :] + lax.dot_general(vb[sl], kdb[sl], TN, preferred_element_type=f32)

            dST = dst[hh]
            dqt_i, dkd_i, dv_i, deg_i = [None] * nch, [None] * nch, [None] * nch, [None] * nch
            for ci in reversed(range(nch)):
                lo = ci * HGRN_CHUNK
                sl = slice(lo, lo + HGRN_CHUNK)
                ST0 = states[ci]
                dSTb = dST.astype(bf16)
                dv_i[ci] = lax.dot_general(kdb[sl], dSTb, NT, preferred_element_type=f32)
                dqt_i[ci] = jnp.dot(dob[sl], ST0.astype(bf16), preferred_element_type=f32)
                dkd_i[ci] = jnp.dot(vb[sl], dSTb, preferred_element_type=f32)
                deg_i[ci] = jnp.broadcast_to(jnp.sum(dST * ST0, axis=0, keepdims=True), (HGRN_CHUNK, LANE))
                dST = dST * eGL[lo:lo + 1, :] + lax.dot_general(dob[sl], qtb[sl], TN, preferred_element_type=f32)
            dst[hh] = dST

            dqt = dqt + jnp.concatenate(dqt_i, axis=0)
            dkd = jnp.concatenate(dkd_i, axis=0)
            dv_ref[:, cols] = (dv_acc + jnp.concatenate(dv_i, axis=0)).astype(bf16)
            deg = jnp.concatenate(deg_i, axis=0)

            dqs = dqt * pr["eG"]
            dkdkd = dkd * kd
            dG = dqt * qt - dkt * kt - dkdkd
            dk = dkt * pr["einv"] + dkd * pr["edec"]
            dGL = _mask_dot(m_ref[1], dkdkd) + eGL * deg
            dg = _mask_dot(m_ref[2], dG) + dGL
            df = dg / pr["f"] - dk
            sig = pr["sig"]
            df_ref[:, cols] = (df * (1.0 - lbv) * (sig * (1.0 - sig))).astype(bf16)
            alb[:, cols] += _colsum8(df * (1.0 - sig))
            sq = pr["sq"]
            dq_ref[:, cols] = (dqs * (sq * (1.0 + q_raw * (1.0 - sq)))).astype(bf16)

        for hh in range(HGRN_PAIR):
            one_head(hh)

        @pl.when(j == nsb - 1)
        def _():
            glb_ref[...] = jnp.broadcast_to(jnp.sum(alb[...], axis=0, keepdims=True), (SUBLANE, wide))
            gnw_ref[...] = jnp.broadcast_to(jnp.sum(anw[...], axis=0, keepdims=True), (SUBLANE, wide))

        if ride is not None:
            @pl.when(jnp.logical_and(pl.program_id(0) == ngrp - 1, j == nsb - 1))
            def _():
                _chip_finish(src_ref, got_ref, sems[0], sems[1], ride[1])

    wide = HGRN_PAIR * LANE
    ngrp = 4 // HGRN_PAIR
    rev = lambda off: pl.BlockSpec((sb, wide), lambda h, j: (nsb - 1 - j, off // HGRN_PAIR + h))
    stat = pl.BlockSpec((SUBLANE, wide), lambda h, j: (0, h))
    riding = ride is not None
    res = pl.pallas_call(
        body,
        grid=(ngrp, nsb),
        in_specs=[rev(0), rev(4), rev(8), rev(12), rev(0), rev(0),
                  pl.BlockSpec((HGRN_PAIR, 1, LANE, LANE), lambda h, j: (h, nsb - 1 - j, 0, 0)),
                  pl.BlockSpec((1, wide), lambda h, j: (0, h)), pl.BlockSpec((1, LANE), lambda h, j: (0, 0)),
                  pl.BlockSpec((3, sb, sb), lambda h, j: (0, 0, 0))]
        + ([_ANY] if riding else []),
        out_specs=[rev(0), rev(0), rev(0), rev(0), stat, stat] + ([_ANY] if riding else []),
        out_shape=[SDS((S, HGRN_W), bf16)] * 4 + [SDS((SUBLANE, HGRN_W), f32)] * 2
        + ([_chip_out_shape(*ride)] if riding else []),
        scratch_shapes=[pltpu.VMEM((HGRN_PAIR, LANE, LANE), f32), pltpu.VMEM((SUBLANE, wide), f32),
                        pltpu.VMEM((SUBLANE, wide), f32)] + (list(_CHIP_SEMS) if riding else []),
        compiler_params=_cparams(("arbitrary", "arbitrary") if riding else ("parallel", "arbitrary")),
        name="hgrn_bwd",
    )(hg, hg, hg, hg, o_raw, dy, ck, lb, normw, _chunk_masks(), *([ride[0]] if riding else []))
    return tuple(res) if riding else (*res, None)


def _lb_fwd(raw):
    def body(r_ref, o_ref):
        r = r_ref[...]
        m = jnp.max(r, axis=0, keepdims=True)
        e = jnp.exp(r - m)
        o_ref[...] = (e / jnp.sum(e, axis=0, keepdims=True))[0:1]

    return pl.pallas_call(body, out_shape=SDS((1, raw.shape[1]), f32), name="lb_fwd")(raw)


def _lb_bwd(raw, dlb):
    def body(r_ref, d_ref, o_ref):
        r = r_ref[...]
        m = jnp.max(r, axis=0, keepdims=True)
        e = jnp.exp(r - m)
        s = e / jnp.sum(e, axis=0, keepdims=True)
        s0 = s[0:1]
        onehot0 = jnp.where(lax.broadcasted_iota(jnp.int32, r.shape, 0) == 0, 1.0, 0.0)
        o_ref[...] = d_ref[...] * s0 * (onehot0 - s)

    return pl.pallas_call(body, out_shape=SDS(raw.shape, f32), name="lb_bwd")(raw, dlb)


def _gate_fwd(ya, yh, w_ba, w_bh, gc):
    S = ya.shape[0]
    D = w_ba.shape[1]
    tm = _pick(S, MM_ROWS)

    def body(ya_ref, yh_ref, wa_ref, wh_ref, g0_ref, g1_ref, a_ref, b_ref, o_ref):
        a = jnp.dot(ya_ref[...], wa_ref[...], preferred_element_type=f32).astype(bf16)
        b = jnp.dot(yh_ref[...], wh_ref[...], preferred_element_type=f32).astype(bf16)
        a_ref[...] = a
        b_ref[...] = b
        s0, s1 = _sigmoid(g0_ref[...].astype(f32)), _sigmoid(g1_ref[...].astype(f32))
        o_ref[...] = (s0 * a.astype(f32) + s1 * b.astype(f32)).astype(bf16)

    row = pl.BlockSpec((tm, D), lambda i: (i, 0))
    act = pl.BlockSpec((tm, ya.shape[1]), lambda i: (i, 0))
    wspec = pl.BlockSpec(w_ba.shape, lambda i: (0, 0))
    return pl.pallas_call(
        body,
        grid=(S // tm,),
        in_specs=[act, act, wspec, wspec, row, pl.BlockSpec((tm, D), lambda i: (i, 1))],
        out_specs=[row, row, row],
        out_shape=[SDS((S, D), bf16)] * 3,
        compiler_params=_cparams(("parallel",), VMEM_BIG),
        name="branch_gate_fwd",
    )(ya, yh, w_ba, w_bh, gc, gc)


def _gate_bwd(dmo, w_out, a, b, gc, w_ba, w_bh):
    S, D = a.shape
    W = w_ba.shape[0]
    tm = _pick(S, MM_ROWS)

    def body(dmo_ref, wo_ref, a_ref, b_ref, g0_ref, g1_ref, wa_ref, wh_ref,
             da_ref, db_ref, dg_ref, dya_ref, dyh_ref):
        dm = lax.dot_general(dmo_ref[...], wo_ref[...], NT, preferred_element_type=f32)
        dmv = dm.astype(bf16).astype(f32)
        s0, s1 = _sigmoid(g0_ref[...].astype(f32)), _sigmoid(g1_ref[...].astype(f32))
        da = (dmv * s0).astype(bf16)
        db = (dmv * s1).astype(bf16)
        da_ref[...] = da
        db_ref[...] = db
        dg_ref[:, :D] = (dmv * a_ref[...].astype(f32) * (s0 * (1.0 - s0))).astype(bf16)
        dg_ref[:, D:] = (dmv * b_ref[...].astype(f32) * (s1 * (1.0 - s1))).astype(bf16)
        dya_ref[...] = lax.dot_general(da, wa_ref[...], NT, preferred_element_type=f32)
        dyh_ref[...] = lax.dot_general(db, wh_ref[...], NT, preferred_element_type=f32)

    row = pl.BlockSpec((tm, D), lambda i: (i, 0))
    wide = pl.BlockSpec((tm, 2 * D), lambda i: (i, 0))
    narrow = pl.BlockSpec((tm, W), lambda i: (i, 0))
    whole = lambda t: pl.BlockSpec(t.shape, lambda i: (0, 0))
    return pl.pallas_call(
        body,
        grid=(S // tm,),
        in_specs=[row, whole(w_out), row, row, row, pl.BlockSpec((tm, D), lambda i: (i, 1)), whole(w_ba), whole(w_bh)],
        out_specs=[row, row, wide, narrow, narrow],
        out_shape=[SDS((S, D), bf16), SDS((S, D), bf16), SDS((S, 2 * D), bf16), SDS((S, W), f32), SDS((S, W), f32)],
        compiler_params=_cparams(("parallel",), VMEM_BIG),
        name="gate_bwd_fused",
    )(dmo, w_out, a, b, gc, gc, w_ba, w_bh)


CONV_ROWS = 512
INV_SQRT2 = 0.7071067811865476
INV_SQRT_2PI = 0.3989422804014327


CONV_HALO = 16


def _shift_down(cur, prev, k):
    x = pltpu.roll(cur, k, 0)
    row = lax.broadcasted_iota(jnp.int32, (SUBLANE, LANE), 0)
    head = jnp.where(row < k, pltpu.roll(prev, k, 0)[:SUBLANE], x[:SUBLANE])
    return jnp.concatenate([head, x[SUBLANE:]], axis=0)


def _shift_up(cur, nxt, k):
    R = cur.shape[0]
    x = pltpu.roll(cur, R - k, 0)
    row = lax.broadcasted_iota(jnp.int32, (SUBLANE, LANE), 0)
    tail = jnp.where(row >= SUBLANE - k, pltpu.roll(nxt, SUBLANE - k, 0), x[R - SUBLANE:])
    return jnp.concatenate([x[:R - SUBLANE], tail], axis=0)


def _conv_rows(u_ref, w, b, r0, first):
    R = CONV_ROWS
    cur = u_ref[pl.ds(r0, R), :].astype(f32)
    prev = u_ref[pl.ds(pl.multiple_of(jnp.maximum(r0 - CONV_HALO, 0), CONV_HALO), CONV_HALO), :].astype(f32)
    prev = jnp.where(first, 0.0, prev)
    x1 = _shift_down(cur, prev, 1)
    x2 = _shift_down(cur, prev, 2)
    c = ((b + w[0:1] * x2) + w[1:2] * x1) + w[2:3] * cur
    return c, x2, x1, cur


def _conv_fwd(ug, uv, wg, wv, bg, bv):
    S, F = ug.shape
    nchunk = S // CONV_ROWS

    def body(ug_ref, uv_ref, wg_ref, wv_ref, bg_ref, bv_ref, o_ref):
        wgv, wvv, bgv, bvv = wg_ref[...], wv_ref[...], bg_ref[...], bv_ref[...]

        def step(ci, carry):
            r0 = pl.multiple_of(ci * CONV_ROWS, CONV_ROWS)
            cg = _conv_rows(ug_ref, wgv, bgv, r0, ci == 0)[0]
            cv = _conv_rows(uv_ref, wvv, bvv, r0, ci == 0)[0]
            gelu = 0.5 * cg * (1.0 + lax.erf(cg * INV_SQRT2))
            o_ref[pl.ds(r0, CONV_ROWS), :] = (gelu * cv).astype(bf16)
            return carry

        lax.fori_loop(0, nchunk, step, 0)

    col = pl.BlockSpec((S, LANE), lambda j: (0, j))
    w3 = pl.BlockSpec((3, LANE), lambda j: (0, j))
    b1 = pl.BlockSpec((1, LANE), lambda j: (0, j))
    return pl.pallas_call(
        body,
        grid=(F // LANE,),
        in_specs=[col, col, w3, w3, b1, b1],
        out_specs=col,
        out_shape=SDS((S, F), bf16),
        compiler_params=_cparams(("parallel",), VMEM_BIG),
        name="conv_fwd",
    )(ug, uv, wg, wv, bg, bv)


def _conv_bwd(ug, uv, dact, wg, wv, bg, bv):
    S, F = ug.shape
    R = CONV_ROWS
    nchunk = S // R

    def body(ug_ref, uv_ref, da_ref, wg_ref, wv_ref, bg_ref, bv_ref, dug_ref, duv_ref, sg_ref, sv_ref, dcg, dcv):
        wgv, wvv, bgv, bvv = wg_ref[...], wv_ref[...], bg_ref[...], bv_ref[...]
        zero = jnp.zeros((SUBLANE, LANE), f32)

        def fwd_step(ci, acc):
            r0 = pl.multiple_of(ci * R, R)
            cg, g2, g1, g0 = _conv_rows(ug_ref, wgv, bgv, r0, ci == 0)
            cv, v2, v1, v0 = _conv_rows(uv_ref, wvv, bvv, r0, ci == 0)
            da = da_ref[pl.ds(r0, R), :].astype(f32)
            cdf = 0.5 * (1.0 + lax.erf(cg * INV_SQRT2))
            pdf = INV_SQRT_2PI * jnp.exp(-0.5 * cg * cg)
            dg = da * cv * (cdf + cg * pdf)
            dv = da * (cg * cdf)
            dcg[pl.ds(r0, R), :] = dg
            dcv[pl.ds(r0, R), :] = dv
            new = (acc[0] + _colsum8(dg * g2), acc[1] + _colsum8(dg * g1), acc[2] + _colsum8(dg * g0),
                   acc[3] + _colsum8(dg),
                   acc[4] + _colsum8(dv * v2), acc[5] + _colsum8(dv * v1), acc[6] + _colsum8(dv * v0),
                   acc[7] + _colsum8(dv))
            return new

        acc = lax.fori_loop(0, nchunk, fwd_step, (zero,) * 8)
        rows = lax.broadcasted_iota(jnp.int32, (SUBLANE, LANE), 0)

        def stats(parts):
            out = jnp.zeros((SUBLANE, LANE), f32)
            for k, pt in enumerate(parts):
                out = jnp.where(rows == k, jnp.sum(pt, axis=0, keepdims=True), out)
            return out

        sg_ref[...] = stats(acc[0:4])
        sv_ref[...] = stats(acc[4:8])

        def du_rows(dc, w, r0, last):
            cur = dc[pl.ds(r0, R), :]
            nxt = dc[pl.ds(pl.multiple_of(jnp.minimum(r0 + R, S - SUBLANE), SUBLANE), SUBLANE), :]
            nxt = jnp.where(last, 0.0, nxt)
            return w[2:3] * cur + w[1:2] * _shift_up(cur, nxt, 1) + w[0:1] * _shift_up(cur, nxt, 2)

        def bwd_step(ci, carry):
            r0 = pl.multiple_of(ci * R, R)
            last = ci == nchunk - 1
            dug_ref[pl.ds(r0, R), :] = du_rows(dcg, wgv, r0, last).astype(bf16)
            duv_ref[pl.ds(r0, R), :] = du_rows(dcv, wvv, r0, last).astype(bf16)
            return carry

        lax.fori_loop(0, nchunk, bwd_step, 0)

    col = pl.BlockSpec((S, LANE), lambda j: (0, j))
    w3 = pl.BlockSpec((3, LANE), lambda j: (0, j))
    b1 = pl.BlockSpec((1, LANE), lambda j: (0, j))
    st = pl.BlockSpec((SUBLANE, LANE), lambda j: (0, j))
    return pl.pallas_call(
        body,
        grid=(F // LANE,),
        in_specs=[col, col, col, w3, w3, b1, b1],
        out_specs=[col, col, st, st],
        out_shape=[SDS((S, F), bf16), SDS((S, F), bf16), SDS((SUBLANE, F), f32), SDS((SUBLANE, F), f32)],
        scratch_shapes=[pltpu.VMEM((S, LANE), f32), pltpu.VMEM((S, LANE), f32)],
        compiler_params=_cparams(("parallel",), VMEM_BIG),
        name="conv_bwd",
    )(ug, uv, dact, wg, wv, bg, bv)


def _adam_math(w, g, m, v):
    m = ADAM_B1 * m + (1.0 - ADAM_B1) * g
    v = ADAM_B2 * v + (1.0 - ADAM_B2) * (g * g)
    m_hat = m / (1.0 - ADAM_B1 ** ADAM_STEP)
    v_hat = v / (1.0 - ADAM_B2 ** ADAM_STEP)
    delta = -ADAM_LR * (m_hat / (jnp.sqrt(v_hat) + ADAM_EPS) + ADAM_WD * w)
    return delta, m, v


def _adamw(w, m, v, g, name):
    R, C = w.shape
    parts = g.ndim == 3
    tr = R
    if R % 16 == 0:
        for t in range(R, 0, -16):
            if R % t == 0 and t * C * 4 <= ADAM_BLOCK_BYTES:
                tr = t
                break

    def body(w_ref, m_ref, v_ref, g_ref, go_ref, d_ref, mo_ref, vo_ref):
        if parts:
            gv = ((g_ref[0].astype(f32) + g_ref[1].astype(f32)) + g_ref[2].astype(f32)) + g_ref[3].astype(f32)
        else:
            gv = g_ref[...]
        go_ref[...] = gv
        d, mn, vn = _adam_math(w_ref[...], gv, m_ref[...], v_ref[...])
        d_ref[...] = d
        mo_ref[...] = mn
        vo_ref[...] = vn

    row = pl.BlockSpec((tr, C), lambda i: (i, 0))
    gspec = pl.BlockSpec((4, tr, C), lambda i: (0, i, 0)) if parts else row
    return pl.pallas_call(
        body,
        grid=(R // tr,),
        in_specs=[row, row, row, gspec],
        out_specs=[row] * 4,
        out_shape=[SDS((R, C), f32)] * 4,
        compiler_params=_cparams(("parallel",), VMEM_BIG),
        name=name,
    )(w, m, v, g)


def _sum8(parts, name):
    _, _, R, C = parts.shape

    def body(p_ref, o_ref):
        acc = p_ref[0, 0]
        for c in range(2):
            for k in range(4):
                if c or k:
                    acc = acc + p_ref[c, k]
        o_ref[...] = acc

    return pl.pallas_call(body, out_shape=SDS((R, C), f32), name=name)(parts)


def _pair_add(by_core, b, name):
    _, K, R, C = by_core.shape
    tr = R // 2 if R % 32 == 0 else R

    def body(c_ref, a_ref, b_ref, o_ref):
        o_ref[...] = (a_ref[0].astype(f32) + b_ref[...].astype(f32)).astype(bf16)

    blk = pl.BlockSpec((1, tr, C), lambda k, i, c: (k, i, 0))
    return pl.pallas_call(
        body,
        grid_spec=pltpu.PrefetchScalarGridSpec(
            num_scalar_prefetch=1,
            grid=(K, R // tr),
            in_specs=[pl.BlockSpec((1, 1, tr, C), lambda k, i, c: (c[0], k, i, 0)), blk],
            out_specs=blk,
        ),
        out_shape=SDS((K, R, C), bf16),
        compiler_params=_cparams(("parallel", "parallel")),
        name=name,
    )(lax.axis_index("c").astype(jnp.int32).reshape(1), by_core, b)


_ANY = pl.BlockSpec(memory_space=pl.ANY)


def _chip_copies(src_ref, out_ref, send_sems, recv_sems, gather):
    x, y, c = lax.axis_index("x"), lax.axis_index("y"), lax.axis_index("c")
    mine = 2 * x + y

    def piece(k):
        return src_ref if gather else src_ref.at[k]

    sends, recvs = [], []
    for j, (px, py) in enumerate([(1 - x, y), (x, 1 - y), (1 - x, 1 - y)]):
        sends.append(pltpu.make_async_remote_copy(
            src_ref=piece(2 * px + py), dst_ref=out_ref.at[mine], send_sem=send_sems.at[j],
            recv_sem=recv_sems.at[j], device_id=(px, py, c), device_id_type=MESH))
        recvs.append(pltpu.make_async_remote_copy(
            src_ref=piece(mine), dst_ref=out_ref.at[2 * px + py], send_sem=send_sems.at[j],
            recv_sem=recv_sems.at[j], device_id=(px, py, c), device_id_type=MESH))
    return sends, recvs


def _chip_start(src_ref, out_ref, send_sems, recv_sems, gather):
    for cp in _chip_copies(src_ref, out_ref, send_sems, recv_sems, gather)[0]:
        cp.start()


def _chip_finish(src_ref, out_ref, send_sems, recv_sems, gather):
    sends, recvs = _chip_copies(src_ref, out_ref, send_sems, recv_sems, gather)
    for cp in recvs:
        cp.wait_recv()
    for cp in sends:
        cp.wait_send()


def _chip_out_shape(src, gather):
    return SDS((4,) + tuple(src.shape if gather else src.shape[1:]), src.dtype)


_CHIP_SEMS = [pltpu.SemaphoreType.DMA((3,)), pltpu.SemaphoreType.DMA((3,))]


def _fill_own(out, src, gather):
    mine = 2 * lax.axis_index("x") + lax.axis_index("y")
    own = src if gather else lax.dynamic_index_in_dim(src, mine, axis=0, keepdims=False)
    return lax.dynamic_update_index_in_dim(out, own, mine, axis=0)


def _chip_comm(src, gather, name):
    def body(src_ref, out_ref, send_sems, recv_sems):
        _chip_start(src_ref, out_ref, send_sems, recv_sems, gather)
        _chip_finish(src_ref, out_ref, send_sems, recv_sems, gather)

    out = pl.pallas_call(
        body,
        in_specs=[_ANY],
        out_specs=_ANY,
        out_shape=_chip_out_shape(src, gather),
        scratch_shapes=list(_CHIP_SEMS),
        name=name,
    )(src)
    return _fill_own(out, src, gather)


_HBM = pl.BlockSpec(memory_space=pltpu.HBM)
_SEM = pl.BlockSpec(memory_space=pltpu.SEMAPHORE)
_EFFECT = pltpu.SideEffectType.DATAFLOW_SIDE_EFFECTING
_SPLIT_PEERS = {"chip_gather": 3, "chip_xchg": 3, "core_gather": 1, "core_swap": 1}


def _split_land(src, kind):
    if kind == "core_gather":
        return SDS((2,) + tuple(src.shape), src.dtype)
    if kind == "core_swap":
        return SDS(tuple(src.shape[1:]), src.dtype)
    return _chip_out_shape(src, kind == "chip_gather")


def _split_copies(src_ref, land_ref, sems, kind):
    x, y, c = lax.axis_index("x"), lax.axis_index("y"), lax.axis_index("c")
    n = _SPLIT_PEERS[kind]
    if kind == "core_gather":
        routes = [((x, y, 1 - c), src_ref, land_ref.at[c], land_ref.at[1 - c])]
    elif kind == "core_swap":
        routes = [((x, y, 1 - c), src_ref.at[1 - c], land_ref, land_ref)]
    else:
        mine = 2 * x + y
        gather = kind == "chip_gather"
        routes = [((px, py, c), src_ref if gather else src_ref.at[2 * px + py], land_ref.at[mine],
                   land_ref.at[2 * px + py]) for px, py in [(1 - x, y), (x, 1 - y), (1 - x, 1 - y)]]
    sends, recvs = [], []
    for j, (peer, piece, there, here) in enumerate(routes):
        sends.append(pltpu.make_async_remote_copy(src_ref=piece, dst_ref=there, send_sem=sems[j],
                                                  recv_sem=sems[n + j], device_id=peer, device_id_type=MESH))
        recvs.append(pltpu.make_async_remote_copy(src_ref=piece, dst_ref=here, send_sem=sems[j],
                                                  recv_sem=sems[n + j], device_id=peer, device_id_type=MESH))
    return sends, recvs


def _split_start(src, kind, name, after=None):
    land = _split_land(src, kind)
    ns = 2 * _SPLIT_PEERS[kind]
    n_in = 2 if after is None else 3

    def body(*refs):
        src_ref, land_ref = refs[:2]
        outs = refs[n_in:]
        for cp in _split_copies(src_ref, land_ref, outs[:ns], kind)[0]:
            cp.start()
        token = outs[ns + 2]
        token[...] = jnp.zeros_like(token)

    res = pl.pallas_call(
        body,
        name=name,
        out_shape=(pltpu.SemaphoreType.DMA(()),) * ns
        + (pltpu.HBM(src.shape, src.dtype), pltpu.HBM(land.shape, land.dtype), SDS((SUBLANE, LANE), f32)),
        in_specs=(_HBM, _HBM) + (() if after is None else (_ANY,)),
        out_specs=(_SEM,) * ns + (_HBM, _HBM, pl.BlockSpec(memory_space=pltpu.VMEM)),
        input_output_aliases={0: ns, 1: ns + 1},
        compiler_params=pltpu.CompilerParams(has_side_effects=_EFFECT),
    )(pltpu.with_memory_space_constraint(src, pltpu.HBM),
      pltpu.with_memory_space_constraint(lax.empty(land.shape, land.dtype), pltpu.HBM),
      *(() if after is None else (after,)))
    return (res[:ns], res[ns], res[ns + 1]), res[ns + 2]


def _split_wait(state, after, kind, name):
    sems, src_thru, land_thru = state
    ns = 2 * _SPLIT_PEERS[kind]

    def body(src_ref, land_ref, *rest):
        sends, recvs = _split_copies(src_ref, land_ref, rest[:ns], kind)
        for cp in recvs:
            cp.wait_recv()
        for cp in sends:
            cp.wait_send()

    src_out, got = pl.pallas_call(
        body,
        name=name,
        out_shape=(pltpu.HBM(src_thru.shape, src_thru.dtype), pltpu.HBM(land_thru.shape, land_thru.dtype)),
        in_specs=(_HBM, _HBM) + (_SEM,) * ns + (_ANY,),
        out_specs=(_HBM, _HBM),
        input_output_aliases={0: 0, 1: 1},
        compiler_params=pltpu.CompilerParams(has_side_effects=_EFFECT),
    )(src_thru, land_thru, *sems, after)
    if kind == "core_swap":
        return got, src_out
    if kind == "core_gather":
        return lax.dynamic_update_index_in_dim(got, src_out, lax.axis_index("c"), axis=0)
    return _fill_own(got, src_out, kind == "chip_gather")


def _core_gather(src, name):
    def body(src_ref, out_ref, send_sem, recv_sem):
        x, y, c = lax.axis_index("x"), lax.axis_index("y"), lax.axis_index("c")
        cp = pltpu.make_async_remote_copy(src_ref=src_ref, dst_ref=out_ref.at[c], send_sem=send_sem,
                                          recv_sem=recv_sem, device_id=(x, y, 1 - c), device_id_type=MESH)
        cp.start()
        pltpu.make_async_remote_copy(src_ref=src_ref, dst_ref=out_ref.at[1 - c], send_sem=send_sem,
                                     recv_sem=recv_sem, device_id=(x, y, 1 - c), device_id_type=MESH).wait_recv()
        cp.wait_send()

    out = pl.pallas_call(
        body,
        in_specs=[_ANY],
        out_specs=_ANY,
        out_shape=SDS((2,) + tuple(src.shape), src.dtype),
        scratch_shapes=[pltpu.SemaphoreType.DMA, pltpu.SemaphoreType.DMA],
        name=name,
    )(src)
    return lax.dynamic_update_index_in_dim(out, src, lax.axis_index("c"), axis=0)


_PACK_A = (("w_in", (1088, 1024)),)
_PACK_B = (("w_ba", (512, 128)), ("w_bh", (512, 128)), ("w_out", (128, 1024)), ("w_up", (704, 1024)),
           ("w_down", (352, 1024)))
_PACK_SIZES = _PACK_A + _PACK_B
_TRANSPOSED = ("w_in", "w_up")


def _slab_rows(sizes):
    return sum(r * c for _, (r, c) in sizes) // D_MODEL


def _pack_rows(d, sizes):
    n = d[sizes[0][0]].shape[0]
    return jnp.concatenate([d[k].reshape(n, -1, D_MODEL) for k, _ in sizes], axis=1)


def _unpack_rows(slab, sizes):
    n = slab.shape[0]
    out, lo = {}, 0
    for key, (r, c) in sizes:
        rows = r * c // D_MODEL
        out[key] = slab[:, lo:lo + rows].reshape(n, r, c)
        lo += rows
    return out


def _by_core(gslab):
    return jnp.swapaxes(gslab.reshape((4, 2) + gslab.shape[1:]), 0, 1)


def _cols_to_full(t):
    return jnp.swapaxes(t, 0, 1).reshape(t.shape[1], -1)


def _full_to_cols(t):
    K = t.shape[0]
    return jnp.swapaxes(t.reshape(K, 8, -1), 0, 1)


_SMALL = (("pre_mix_norm", (1, 1024)), ("rel_bias", (32, 24)), ("hgrn_lb_raw", (2, 512)), ("hgrn_norm", (1, 128)),
          ("post_mix_norm", (1, 1024)), ("pre_ffn_norm", (1, 1024)), ("conv_b", (1, 5632)),
          ("post_ffn_norm", (1, 1024)))
_SMALL_ROWS = 96
_CONVW_ROWS = 136


_SMALL_USED = sum(r * c for _, (r, c) in _SMALL)


def _pack_small(d, extra=None):
    flat = jnp.concatenate([d[k].reshape(-1) for k, _ in _SMALL] + ([] if extra is None else [extra.reshape(-1)]))
    flat = jnp.pad(flat, (0, _SMALL_ROWS * LANE - flat.shape[0]))
    return flat.reshape(_SMALL_ROWS, LANE)


def _unpack_small(p):
    flat = p.reshape(-1)
    out, lo = {}, 0
    for k, shp in _SMALL:
        n = shp[0] * shp[1]
        out[k] = flat[lo:lo + n].reshape(shp)
        lo += n
    return out


def _local_step(x, tgt, P, plan):
    S = x.shape[0]
    P = dict(P)
    lb = _lb_fwd(P["hgrn_lb_raw"])
    hs = _prep(x, P["pre_mix_norm"], plan.start_token())
    h1 = hs[0]
    consts = [_bias_consts(d) for d in DILATIONS]
    biases, dep = [], h1
    for g in range(N_GROUPS):
        tab_t = P["rel_bias"][:, 8 * g:8 * g + 8].T
        dep = _bias_build(tab_t, consts[g][0], consts[g][1], f"bias_build{g}", dep)
        biases.append(dep.reshape(8, ATTN_BLOCK, 2 * ATTN_BLOCK))
    W = dict(plan.weights_a(dep))
    qkv0, hg, gc = _mm_fanout(h1, [W["wt_qkv"][0], W["wt_hg"], W["wt_gate"]], "nt", [bf16, f32, bf16], "proj_natural")
    qkv = [qkv0] + [_mm(hs[g], W["wt_qkv"][g], "nt", bf16, f"proj_qkv{g}") for g in (1, 2)]
    obuf, lbuf, token = [], [], None
    for g, d in enumerate(DILATIONS):
        o_g, l_g = _attn_fwd(qkv[g], biases[g], (S // d) // ATTN_BLOCK, f"attn_fwd{g}", after=token)
        lbuf.append(l_g)
        obuf.append(o_g)
        if g == 0:
            token = plan.forward_b(o_g)
    y_attn, y_attn_b, w0, w1, w2 = _attn_merge(obuf[0], obuf[1], obuf[2], lbuf[0], lbuf[1], lbuf[2])
    y_hgrn, o_raw, ck, _ = _hgrn_fwd(hg, lb, P["hgrn_norm"])
    wb = plan.weights_b(y_hgrn)
    P["conv_w"] = wb.pop("conv_w")
    W.update(wb)
    a, b, merged = _gate_fwd(y_attn_b, y_hgrn, W["w_ba"], W["w_bh"], gc)
    mo, x1, h2 = _mid_fwd(x, merged, W["w_out"], P["post_mix_norm"], P["pre_ffn_norm"])
    ug, uv = _mm_fanout(h2, [W["wt_up_g"], W["wt_up_v"]], "nt", [bf16, bf16], "up_proj")
    cw_g, cw_v = P["conv_w"][:, :D_FF], P["conv_w"][:, D_FF:]
    cb_g, cb_v = P["conv_b"][:, :D_FF], P["conv_b"][:, D_FF:]
    act = _conv_fwd(ug, uv, cw_g, cw_v, cb_g, cb_v)
    loss, dy, dfo, g_post_ffn = _final(x1, act, W["w_down"], tgt, P["post_ffn_norm"])
    dact = _mm(dfo, W["w_down"], "nt", bf16, "d_act")
    gW_down = _mm(act, dfo, "tn", bf16, "gw_down")
    dug, duv, st_g, st_v = _conv_bwd(ug, uv, dact, cw_g, cw_v, cb_g, cb_v)
    gW_up_g = _mm(dug, h2, "tn", bf16, "gw_up_gate")
    gW_up_v = _mm(duv, h2, "tn", bf16, "gw_up_val")
    dx1, dmo, g_pre_ffn, g_post_mix = _mid_bwd(dy, dug, duv, W["wt_up_g"], W["wt_up_v"], x1, mo, P["pre_ffn_norm"],
                                               P["post_mix_norm"])
    gW_out = _mm(merged, dmo, "tn", bf16, "gw_out")
    da, db, dgc, dyattn, dyhgrn = _gate_bwd(dmo, W["w_out"], a, b, gc, W["w_ba"], W["w_bh"])
    gW_ba = _mm(y_attn_b, da, "tn", bf16, "gw_ba")
    gW_bh = _mm(y_hgrn, db, "tn", bf16, "gw_bh")
    big_b = dict(w_ba=gW_ba, w_bh=gW_bh, w_out=gW_out, w_up=[gW_up_g, gW_up_v], w_down=gW_down)
    dos = _attn_merge_bwd(dyattn, y_attn, w0, w1, w2, after=plan.grads_b_start(big_b))
    dq_h, df_h, dv_h, dog_h, glb8, gnw8, got_b = _hgrn_bwd(hg, o_raw, dyhgrn, ck, lb, P["hgrn_norm"],
                                                          plan.bwd_ride(dos[5]))
    dhg = [dq_h, df_h, dv_h, dog_h]
    g_lb_raw = _lb_bwd(P["hgrn_lb_raw"], glb8[0:1])
    gn = gnw8[0:1]
    g_hgrn_norm = (gn[:, 0:128] + gn[:, 128:256]) + (gn[:, 256:384] + gn[:, 384:512])
    dqkvs, gW_qkv, g_rel = [], [], []
    for g, d in enumerate(DILATIONS):
        dq, dk, dv, dbias = _attn_bwd(qkv[g], biases[g], dos[g], dos[3 + g], lbuf[g], (S // d) // ATTN_BLOCK,
                                      f"attn_bwd{g}")
        dqkvs.append([dq, dk, dv])
        gW_qkv.append(_mm(dqkvs[g], hs[g], "tn", bf16, f"gw_qkv{g}"))
        g_rel.append(_bias_grad(dbias.reshape(8, -1), consts[g][0], f"bias_grad{g}"))
    gW_hg = _mm(dhg, h1, "tn", bf16, "gw_hg")
    gW_gate = _mm(dgc, h1, "tn", bf16, "gw_gate")
    gW_in = gW_qkv + [gW_hg, gW_gate]
    token = plan.grads_a_start(gW_in)
    dh_perm = [_mm(dqkvs[g], W["wt_qkv"][g], "nn", f32, f"dh1_qkv{g}", after=token) for g in (1, 2)]
    token = plan.grads_a_exchange(dh_perm[1])
    dh_main = _mm(dqkvs[0] + dhg + [dgc], [W["wt_qkv"][0], W["wt_hg"], W["wt_gate"]], "nn", f32, "dh1_main",
                  after=token)
    grad_x, g_pre_mix = _first_bwd(x, dx1, _dh_sum(dh_main, dh_perm[0], dh_perm[1]), P["pre_mix_norm"])

    g_conv_w = jnp.concatenate([st_g[0:3], st_v[0:3]], axis=1)
    g_conv_b = jnp.concatenate([st_g[3:4], st_v[3:4]], axis=1)
    small = dict(pre_mix_norm=g_pre_mix, rel_bias=jnp.concatenate(g_rel, axis=1), hgrn_lb_raw=g_lb_raw,
                 hgrn_norm=g_hgrn_norm, post_mix_norm=g_post_mix, pre_ffn_norm=g_pre_ffn, conv_b=g_conv_b,
                 post_ffn_norm=g_post_ffn, conv_w=g_conv_w)
    return loss, grad_x, gW_in, big_b, got_b, small


def _weights_a(both):
    wt = jnp.swapaxes(both, 0, 1).reshape(-1, D_MODEL)
    return dict(
        wt_qkv=[wt[g * QKV_G:(g + 1) * QKV_G] for g in range(N_GROUPS)],
        wt_hg=wt[3 * QKV_G:3 * QKV_G + 4 * HGRN_W],
        wt_gate=wt[3 * QKV_G + 4 * HGRN_W:],
    )


def _weights_b(slabs):
    sh = _unpack_rows(slabs, _PACK_B)
    wt_up = sh["w_up"].reshape(-1, D_MODEL)
    return dict(
        w_ba=_cols_to_full(sh["w_ba"]),
        w_bh=_cols_to_full(sh["w_bh"]),
        w_out=sh["w_out"].reshape(D_MODEL, D_MODEL),
        wt_up_g=wt_up[:D_FF],
        wt_up_v=wt_up[D_FF:],
        w_down=sh["w_down"].reshape(D_FF, D_MODEL),
    )


def _dest_rows(sections, height):
    out = []
    for j in range(8):
        lo, hi, off, pieces = j * height, (j + 1) * height, 0, []
        for s in sections:
            a, b = max(lo, off), min(hi, off + s.shape[0])
            if a < b:
                pieces.append(s[a - off:b - off])
            off += s.shape[0]
        out.append(pieces[0] if len(pieces) == 1 else jnp.concatenate(pieces, axis=0))
    return out


def _grad_blocks_a(sections):
    rows = _dest_rows(sections, 1088)
    return jnp.stack([jnp.stack([rows[2 * k + c].astype(bf16) for k in range(4)]) for c in range(2)])


def _grad_slab_b(g):
    shards = dict(w_ba=_full_to_cols(g["w_ba"]), w_bh=_full_to_cols(g["w_bh"]), w_out=g["w_out"].reshape(8, 128, D_MODEL),
                  w_up=jnp.stack(_dest_rows(g["w_up"], 704)), w_down=g["w_down"].reshape(8, 352, D_MODEL))
    return _pack_rows({k: v.astype(bf16) for k, v in shards.items()}, _PACK_B)


_CONVW_SLAB_ROWS = 16


class _Traffic:
    def __init__(self, slab_a, slab_b, conv_w):
        hi = conv_w.astype(bf16)
        r1 = conv_w - hi.astype(f32)
        mid = r1.astype(bf16)
        lo = (r1 - mid.astype(f32)).astype(bf16)
        bits = jnp.stack([hi, mid, lo]).reshape(-1)
        tail = jnp.pad(bits, (0, _CONVW_SLAB_ROWS * D_MODEL - bits.shape[0])).reshape(_CONVW_SLAB_ROWS, D_MODEL)
        self.slab_b = jnp.concatenate([slab_b, tail], axis=0)
        self.state_a, tok = _split_start(slab_a, "chip_gather", "ag_a_start")
        self.state_b, self.token = _split_start(self.slab_b, "chip_gather", "ag_b_start", after=tok)
        self.chip_sum = None
        self.state = None

    def start_token(self):
        return self.token

    def weights_a(self, after):
        by_chip = _split_wait(self.state_a, after, "chip_gather", "ag_a_wait")
        return _weights_a(_core_gather(by_chip, "ag_a_cores"))

    def forward_b(self, after):
        by_chip = _split_wait(self.state_b, after, "chip_gather", "ag_b_wait")
        self.state, token = _split_start(by_chip, "core_gather", "ag_b_cores_start")
        return token

    def weights_b(self, after):
        both = _split_wait(self.state, after, "core_gather", "ag_b_cores_wait")
        slabs = jnp.swapaxes(both, 0, 1).reshape((8,) + tuple(self.slab_b.shape))
        rows = _slab_rows(_PACK_B)
        out = _weights_b(slabs[:, :rows])
        pieces = slabs[:, rows:].reshape(8, -1)[:, :3 * 3 * 704].reshape(8, 3, 3, 704).astype(f32)
        out["conv_w"] = _cols_to_full((pieces[:, 0] + pieces[:, 1]) + pieces[:, 2])
        return out

    def grads_b_start(self, grads):
        self.state, token = _split_start(_by_core(_grad_slab_b(grads)), "core_swap", "rs_b_cores_start")
        return token

    def bwd_ride(self, after):
        from_sib, by_core = _split_wait(self.state, after, "core_swap", "rs_b_cores_wait")
        self.chip_sum = _pair_add(by_core, from_sib, "rs_b_pair_add")
        return (self.chip_sum, False)

    def grads_a_start(self, sections):
        self.state, token = _split_start(_grad_blocks_a(sections), "core_swap", "rs_a_cores_start")
        return token

    def grads_a_exchange(self, after):
        from_sib, by_core = _split_wait(self.state, after, "core_swap", "rs_a_cores_wait")
        self.state, token = _split_start(_pair_add(by_core, from_sib, "rs_a_pair_add"), "chip_xchg", "rs_a_start")
        return token

    def parts(self, got_b, after):
        parts = _unpack_rows(_fill_own(got_b, self.chip_sum, False), _PACK_B)
        parts["w_in"] = _split_wait(self.state, after, "chip_xchg", "rs_a_wait")
        return parts


def kernel(x, pre_mix_norm, w_in, rel_bias, hgrn_lb_raw, hgrn_norm, w_branch_attn, w_branch_hgrn, w_out, post_mix_norm, pre_ffn_norm, w_up, conv_w, conv_b, w_down, post_ffn_norm, loss_target, m_pre_mix_norm, m_w_in, m_rel_bias, m_hgrn_lb_raw, m_hgrn_norm, m_w_branch_attn, m_w_branch_hgrn, m_w_out, m_post_mix_norm, m_pre_ffn_norm, m_w_up, m_conv_w, m_conv_b, m_w_down, m_post_ffn_norm, v_pre_mix_norm, v_w_in, v_rel_bias, v_hgrn_lb_raw, v_hgrn_norm, v_w_branch_attn, v_w_branch_hgrn, v_w_out, v_post_mix_norm, v_pre_ffn_norm, v_w_up, v_conv_w, v_conv_b, v_w_down, v_post_ffn_norm):
    ci = lax.axis_index("c")
    dev = 4 * lax.axis_index("x") + 2 * lax.axis_index("y") + ci
    tr = lambda t: jnp.swapaxes(t[0], 0, 1)
    wts = dict(w_in=tr(w_in), w_ba=w_branch_attn[0], w_bh=w_branch_hgrn[0], w_out=w_out[0], w_up=tr(w_up),
               w_down=w_down[0])
    mom = dict(w_in=tr(m_w_in), w_ba=m_w_branch_attn[0], w_bh=m_w_branch_hgrn[0], w_out=m_w_out[0], w_up=tr(m_w_up),
               w_down=m_w_down[0])
    var = dict(w_in=tr(v_w_in), w_ba=v_w_branch_attn[0], w_bh=v_w_branch_hgrn[0], w_out=v_w_out[0], w_up=tr(v_w_up),
               w_down=v_w_down[0])
    small_w = dict(pre_mix_norm=pre_mix_norm, rel_bias=rel_bias, hgrn_lb_raw=hgrn_lb_raw, hgrn_norm=hgrn_norm,
                   post_mix_norm=post_mix_norm, pre_ffn_norm=pre_ffn_norm, conv_b=conv_b, post_ffn_norm=post_ffn_norm)
    small_m = dict(pre_mix_norm=m_pre_mix_norm, rel_bias=m_rel_bias, hgrn_lb_raw=m_hgrn_lb_raw, hgrn_norm=m_hgrn_norm,
                   post_mix_norm=m_post_mix_norm, pre_ffn_norm=m_pre_ffn_norm, conv_b=m_conv_b,
                   post_ffn_norm=m_post_ffn_norm)
    small_v = dict(pre_mix_norm=v_pre_mix_norm, rel_bias=v_rel_bias, hgrn_lb_raw=v_hgrn_lb_raw, hgrn_norm=v_hgrn_norm,
                   post_mix_norm=v_post_mix_norm, pre_ffn_norm=v_pre_ffn_norm, conv_b=v_conv_b,
                   post_ffn_norm=v_post_ffn_norm)

    plan = _Traffic(wts["w_in"].astype(bf16),
                    _pack_rows({k: wts[k].astype(bf16)[None] for k, _ in _PACK_B}, _PACK_B)[0], conv_w[0])

    loss8, grad_x, _, _, got_b, small = _local_step(x[0], loss_target[0], small_w, plan)
    parts = plan.parts(got_b, grad_x)
    outs_big = {}
    for k, _ in _PACK_SIZES:
        outs_big[k] = _adamw(wts[k], mom[k], var[k], parts[k], "adamw_" + k)

    spack = jnp.concatenate([_pack_small(small, loss8[0, 0:1]),
                             jnp.pad(small["conv_w"].reshape(-1, LANE), ((0, _CONVW_ROWS - 132), (0, 0)))], axis=0)
    allp = _core_gather(_chip_comm(spack, True, "ag_small_chips"), "ag_small_cores")
    ssum = _sum8(allp, "small_sum")
    gs = ssum[:_SMALL_ROWS]
    loss = ssum[_SMALL_USED // LANE, _SMALL_USED % LANE]
    res_small = _adamw(_pack_small(small_w), _pack_small(small_m), _pack_small(small_v), gs, "adamw_small")
    sm = [_unpack_small(t) for t in res_small]
    g_cw_full = ssum[_SMALL_ROWS:_SMALL_ROWS + 132].reshape(3, 2 * D_FF)
    g_cw = lax.dynamic_slice_in_dim(g_cw_full, dev * 704, 704, axis=1)
    res_cw = _adamw(conv_w[0], m_conv_w[0], v_conv_w[0], g_cw, "adamw_conv_w")

    def pick(i):
        def big_(k):
            t = outs_big[k][i]
            return (jnp.swapaxes(t, 0, 1) if k in _TRANSPOSED else t)[None]
        return [sm[i]["pre_mix_norm"], big_("w_in"), sm[i]["rel_bias"], sm[i]["hgrn_lb_raw"], sm[i]["hgrn_norm"],
                big_("w_ba"), big_("w_bh"), big_("w_out"), sm[i]["post_mix_norm"], sm[i]["pre_ffn_norm"],
                big_("w_up"), res_cw[i][None], sm[i]["conv_b"], big_("w_down"), sm[i]["post_ffn_norm"]]

    return (loss, grad_x[None], *pick(0), *pick(1), *pick(2), *pick(3))
```

```python
import functools
import math

import jax
import jax.numpy as jnp
from jax import lax
from jax.experimental import pallas as pl
from jax.experimental.pallas import tpu as pltpu

f32 = jnp.float32
bf16 = jnp.bfloat16
SDS = jax.ShapeDtypeStruct
HIGHEST = lax.Precision.HIGHEST
MESH = pl.DeviceIdType.MESH

NN = (((1,), (0,)), ((), ()))
NT = (((1,), (1,)), ((), ()))
TN = (((0,), (0,)), ((), ()))

D_MODEL = 1024
N_GROUPS = 3
DILATIONS = (1, 4, 16)
HEAD_DIM = 64
ATTN_BLOCK = 128
QKV_G = 1536
ATTN_OUT = 512
HGRN_W = 512
HGRN_CHUNK = 32
D_FF = 2816
NUM_BUCKETS = 32
MAX_EXACT = 16
MAX_DISTANCE = 2048
NEG_INF = -1e30
EPS = 1e-6
LANE = 128
SUBLANE = 8
VMEM_BIG = 48 * 1024 * 1024
MM_ROWS = 512
MM_OUT_BYTES = 8 * 1024 * 1024
ADAM_BLOCK_BYTES = 2304 * 1024

ADAM_LR, ADAM_B1, ADAM_B2, ADAM_EPS, ADAM_WD, ADAM_STEP = 0.001, 0.9, 0.999, 1e-08, 0.01, 10


def _pick(n, pref):
    t = pref
    while t >= LANE:
        if n % t == 0:
            return t
        t //= 2
    return n


def _cparams(sem=None, vmem=None):
    kw = {}
    if sem is not None:
        kw["dimension_semantics"] = sem
    if vmem is not None:
        kw["vmem_limit_bytes"] = vmem
    return pltpu.CompilerParams(**kw)


def _sigmoid(x):
    return jax.nn.sigmoid(x)


def _colsum8(x):
    return x.reshape(x.shape[0] // SUBLANE, SUBLANE, x.shape[1]).sum(axis=0)


def _mm(a, b, mode, out_dtype, name, acc=None, after=None):
    dims = {"nn": NN, "nt": NT, "tn": TN}[mode]
    has_acc = acc is not None
    parts = list(a) if isinstance(a, (list, tuple)) else [a]
    if mode == "tn":
        assert not has_acc
        K, N = b.shape
        widths = [t.shape[1] for t in parts]
        M = sum(widths)
        whole = M * N * 4 <= MM_OUT_BYTES
        assert whole or len(parts) == 1
        tmm = M if whole else M // 2
        ts = _pick(K, 4 * MM_ROWS)
        nk = K // ts

        npart = len(parts)
        narrow = out_dtype != f32

        def body_tn(*refs):
            b_ref, o_ref = refs[npart], refs[npart + 1]
            acc_ref = refs[npart + 2] if narrow else o_ref
            k = pl.program_id(1)
            bv = b_ref[...]
            lo = 0
            for a_ref, w in zip(refs[:npart], widths if whole else [tmm]):
                part = lax.dot_general(a_ref[...], bv, dims, preferred_element_type=f32)
                rows = slice(lo, lo + w)
                lo += w

                @pl.when(k == 0)
                def _(part=part, rows=rows):
                    acc_ref[rows, :] = part

                @pl.when(k > 0)
                def _(part=part, rows=rows):
                    acc_ref[rows, :] += part

            if narrow:
                @pl.when(k == nk - 1)
                def _():
                    o_ref[...] = acc_ref[...].astype(out_dtype)

        return pl.pallas_call(
            body_tn,
            grid=(M // tmm, nk),
            in_specs=[pl.BlockSpec((ts, w if whole else tmm), lambda i, k: (k, i)) for w in widths]
            + [pl.BlockSpec((ts, N), lambda i, k: (k, 0))],
            out_specs=pl.BlockSpec((tmm, N), lambda i, k: (i, 0)),
            out_shape=SDS((M, N), out_dtype),
            scratch_shapes=[pltpu.VMEM((tmm, N), f32)] if narrow else [],
            compiler_params=_cparams(("parallel", "arbitrary"), VMEM_BIG),
            name=name,
        )(*parts, b)

    bs = list(b) if isinstance(b, (list, tuple)) else [b]
    widths = [t.shape[1] for t in parts]
    M = parts[0].shape[0]
    kdim = 0 if mode == "nn" else 1
    N = bs[0].shape[1 - kdim]
    tm = _pick(M, MM_ROWS)
    npart, nb = len(parts), len(bs)
    place, bi, lo = [], 0, 0
    for w in widths:
        place.append((bi, lo))
        lo += w
        if lo == bs[bi].shape[kdim]:
            bi, lo = bi + 1, 0
    assert bi == nb and lo == 0

    def body(*refs):
        a_refs, b_refs = refs[:npart], refs[npart:npart + nb]
        c_ref = refs[npart + nb] if has_acc else None
        o_ref = refs[-1]
        part = None
        for a_ref, w, (bi, lo) in zip(a_refs, widths, place):
            b_ref = b_refs[bi]
            if w == bs[bi].shape[kdim]:
                bk = b_ref[...]
            else:
                bk = b_ref[:, lo:lo + w] if mode == "nt" else b_ref[lo:lo + w, :]
            t = lax.dot_general(a_ref[...], bk, dims, preferred_element_type=f32)
            part = t if part is None else part + t
        if has_acc:
            part = part + c_ref[...]
        o_ref[...] = part.astype(out_dtype)

    specs = [pl.BlockSpec((tm, w), lambda i: (i, 0)) for w in widths] \
        + [pl.BlockSpec(t.shape, lambda i: (0, 0)) for t in bs]
    args = parts + bs
    aliases = {}
    if has_acc:
        specs.append(pl.BlockSpec((tm, N), lambda i: (i, 0)))
        args.append(acc)
        aliases = {npart + nb: 0}
    if after is not None:
        specs.append(pl.BlockSpec(memory_space=pl.ANY))
        args.append(after)
    return pl.pallas_call(
        body,
        grid=(M // tm,),
        in_specs=specs,
        out_specs=pl.BlockSpec((tm, N), lambda i: (i, 0)),
        out_shape=SDS((M, N), out_dtype),
        input_output_aliases=aliases,
        compiler_params=_cparams(("parallel",), VMEM_BIG),
        name=name,
    )(*args)


def _mm_fanout(a, bs, mode, out_dtypes, name):
    dims = {"nn": NN, "nt": NT}[mode]
    M, K = a.shape
    ns = [b.shape[1] if mode == "nn" else b.shape[0] for b in bs]
    tm = _pick(M, MM_ROWS)
    nb = len(bs)

    def body(a_ref, *refs):
        av = a_ref[...]
        for b_ref, o_ref, dt in zip(refs[:nb], refs[nb:], out_dtypes):
            o_ref[...] = lax.dot_general(av, b_ref[...], dims, preferred_element_type=f32).astype(dt)

    return pl.pallas_call(
        body,
        grid=(M // tm,),
        in_specs=[pl.BlockSpec((tm, K), lambda i: (i, 0))] + [pl.BlockSpec(b.shape, lambda i: (0, 0)) for b in bs],
        out_specs=[pl.BlockSpec((tm, n), lambda i: (i, 0)) for n in ns],
        out_shape=[SDS((M, n), dt) for n, dt in zip(ns, out_dtypes)],
        compiler_params=_cparams(("parallel",), VMEM_BIG),
        name=name,
    )(a, *bs)


PERM_ROWS = 2048


def _perm_spec(d, cols=LANE):
    return pl.BlockSpec((d, PERM_ROWS // d, cols), lambda i, j: (0, i, j))


def _to_natural(src_ref, dst_ref, d):
    n = src_ref.shape[1]
    for r in range(d):
        dst_ref[pl.ds(r, n, stride=d), :] = src_ref[r]


def _prep(x, w, after=None):
    S, D = x.shape
    R = PERM_ROWS
    nc = D // LANE
    n_in = nc + 1 + (after is not None)

    def body(*refs):
        x_refs, w_ref = refs[:nc], refs[nc]
        h_ref, h4_ref, h16_ref, rs = refs[n_in:]
        ssq = None
        for xr in x_refs:
            v = xr[...]
            t = jnp.sum(v * v, axis=-1, keepdims=True)
            ssq = t if ssq is None else ssq + t
        rinv = lax.rsqrt(ssq * (1.0 / D) + EPS)
        rs[...] = jnp.broadcast_to(rinv, (R, LANE))
        for j, xr in enumerate(x_refs):
            cols = slice(j * LANE, (j + 1) * LANE)
            wj = w_ref[:, cols]
            h_ref[:, cols] = ((xr[...] * rinv) * wj).astype(bf16)
            for d, o_ref in ((4, h4_ref), (16, h16_ref)):
                n = R // d
                for r in range(d):
                    rows = pl.ds(r, n, stride=d)
                    o_ref[r, :, cols] = ((xr[rows, :] * rs[rows, :]) * wj).astype(bf16)

    col = lambda j: pl.BlockSpec((R, LANE), lambda i, j=j: (i, j))
    h, h4, h16 = pl.pallas_call(
        body,
        grid=(S // R,),
        in_specs=[col(j) for j in range(nc)] + [pl.BlockSpec((1, D), lambda i: (0, 0))]
        + ([] if after is None else [pl.BlockSpec(memory_space=pl.ANY)]),
        out_specs=[pl.BlockSpec((R, D), lambda i: (i, 0)), pl.BlockSpec((4, R // 4, D), lambda i: (0, i, 0)),
                   pl.BlockSpec((16, R // 16, D), lambda i: (0, i, 0))],
        out_shape=[SDS((S, D), bf16), SDS((4, S // 4, D), bf16), SDS((16, S // 16, D), bf16)],
        scratch_shapes=[pltpu.VMEM((R, LANE), f32)],
        compiler_params=_cparams(("parallel",), VMEM_BIG),
        name="prep_norm_perm",
    )(*([x] * nc), w, *([] if after is None else [after]))
    return [h, h4.reshape(S, D), h16.reshape(S, D)]


def _dh_sum(a, b, c):
    S, D = a.shape
    R = PERM_ROWS

    def body(a_ref, b_ref, c_ref, o_ref, sb, sc):
        _to_natural(b_ref, sb, 4)
        _to_natural(c_ref, sc, 16)
        o_ref[...] = (a_ref[...] + sb[...]) + sc[...]

    nat = pl.BlockSpec((R, LANE), lambda i, j: (i, j))
    return pl.pallas_call(
        body,
        grid=(S // R, D // LANE),
        in_specs=[nat, _perm_spec(4), _perm_spec(16)],
        out_specs=nat,
        out_shape=SDS((S, D), f32),
        scratch_shapes=[pltpu.VMEM((R, LANE), f32)] * 2,
        compiler_params=_cparams(("parallel", "parallel"), VMEM_BIG),
        name="dh_sum",
    )(a, b.reshape(4, S // 4, D), c.reshape(16, S // 16, D))


def _rms_parts(xv):
    r = lax.rsqrt(jnp.mean(xv * xv, axis=-1, keepdims=True) + EPS)
    return r, xv * r


def _rms_bwd(xhat, r, w, dy):
    dyw = dy * w
    return r * (dyw - xhat * jnp.mean(dyw * xhat, axis=-1, keepdims=True))


def _mid_fwd(x, merged, w_out, w_pm, w_pf):
    S, D = x.shape
    tm = _pick(S, MM_ROWS)

    def body(x_ref, m_ref, wo_ref, wpm_ref, wpf_ref, mo_ref, x1_ref, h2_ref):
        mo = jnp.dot(m_ref[...], wo_ref[...], preferred_element_type=f32)
        mo_ref[...] = mo
        _, moh = _rms_parts(mo)
        x1 = x_ref[...] + moh * wpm_ref[...]
        x1_ref[...] = x1
        _, x1h = _rms_parts(x1)
        h2_ref[...] = (x1h * wpf_ref[...]).astype(bf16)

    row = pl.BlockSpec((tm, D), lambda i: (i, 0))
    vec = pl.BlockSpec((1, D), lambda i: (0, 0))
    return pl.pallas_call(
        body,
        grid=(S // tm,),
        in_specs=[row, pl.BlockSpec((tm, merged.shape[1]), lambda i: (i, 0)),
                  pl.BlockSpec(w_out.shape, lambda i: (0, 0)), vec, vec],
        out_specs=[row, row, row],
        out_shape=[SDS((S, D), f32), SDS((S, D), f32), SDS((S, D), bf16)],
        compiler_params=_cparams(("parallel",), VMEM_BIG),
        name="out_proj_mid_fwd",
    )(x, merged, w_out, w_pm, w_pf)


def _final(x1, act, w_down, tgt, w_pfn):
    S, D = x1.shape
    tm = _pick(S, MM_ROWS)
    nt = S // tm

    def body(x1_ref, a_ref, wd_ref, t_ref, w_ref, loss_ref, dy_ref, dfo_ref, gw_ref, lacc, gacc):
        i = pl.program_id(0)

        @pl.when(i == 0)
        def _():
            lacc[...] = jnp.zeros_like(lacc)
            gacc[...] = jnp.zeros_like(gacc)

        w = w_ref[...]
        r, foh = _rms_parts(jnp.dot(a_ref[...], wd_ref[...], preferred_element_type=f32))
        y = x1_ref[...] + foh * w
        err = y - t_ref[...]
        lacc[...] += _colsum8(err * err)
        dy = err * (1.0 / D)
        dy_ref[...] = dy
        gacc[...] += _colsum8(dy * foh)
        dfo_ref[...] = _rms_bwd(foh, r, w, dy).astype(bf16)

        @pl.when(i == nt - 1)
        def _():
            loss_ref[...] = jnp.full((SUBLANE, LANE), 0.5 / D, f32) * jnp.sum(lacc[...])
            gw_ref[...] = jnp.sum(gacc[...], axis=0, keepdims=True)

    row = pl.BlockSpec((tm, D), lambda i: (i, 0))
    vec = pl.BlockSpec((1, D), lambda i: (0, 0))
    return pl.pallas_call(
        body,
        grid=(nt,),
        in_specs=[row, pl.BlockSpec((tm, act.shape[1]), lambda i: (i, 0)),
                  pl.BlockSpec(w_down.shape, lambda i: (0, 0)), row, vec],
        out_specs=[pl.BlockSpec((SUBLANE, LANE), lambda i: (0, 0)), row, row, vec],
        out_shape=[SDS((SUBLANE, LANE), f32), SDS((S, D), f32), SDS((S, D), bf16), SDS((1, D), f32)],
        scratch_shapes=[pltpu.VMEM((SUBLANE, D), f32), pltpu.VMEM((SUBLANE, D), f32)],
        compiler_params=_cparams(("arbitrary",), VMEM_BIG),
        name="down_proj_final_loss",
    )(x1, act, w_down, tgt, w_pfn)


MID_BWD_ROWS = 256


def _mid_bwd(dy, dug, duv, wt_g, wt_v, x1, mo, w_pf, w_pm):
    S, D = dy.shape
    tm = _pick(S, MID_BWD_ROWS)
    nt = S // tm

    def body(dy_ref, dug_ref, duv_ref, wg_ref, wv_ref, x1_ref, mo_ref, wpf_ref, wpm_ref,
             dx1_ref, dmo_ref, gpf_ref, gpm_ref, apf, apm):
        i = pl.program_id(0)

        @pl.when(i == 0)
        def _():
            apf[...] = jnp.zeros_like(apf)
            apm[...] = jnp.zeros_like(apm)

        r1, x1h = _rms_parts(x1_ref[...])
        dh2 = jnp.dot(dug_ref[...], wg_ref[...], preferred_element_type=f32) \
            + jnp.dot(duv_ref[...], wv_ref[...], preferred_element_type=f32)
        apf[...] += _colsum8(dh2 * x1h)
        dx1 = dy_ref[...] + _rms_bwd(x1h, r1, wpf_ref[...], dh2)
        dx1_ref[...] = dx1
        rm, moh = _rms_parts(mo_ref[...])
        apm[...] += _colsum8(dx1 * moh)
        dmo_ref[...] = _rms_bwd(moh, rm, wpm_ref[...], dx1).astype(bf16)

        @pl.when(i == nt - 1)
        def _():
            gpf_ref[...] = jnp.sum(apf[...], axis=0, keepdims=True)
            gpm_ref[...] = jnp.sum(apm[...], axis=0, keepdims=True)

    row = pl.BlockSpec((tm, D), lambda i: (i, 0))
    vec = pl.BlockSpec((1, D), lambda i: (0, 0))
    return pl.pallas_call(
        body,
        grid=(nt,),
        in_specs=[row, pl.BlockSpec((tm, dug.shape[1]), lambda i: (i, 0)), pl.BlockSpec((tm, duv.shape[1]), lambda i: (i, 0)),
                  pl.BlockSpec(wt_g.shape, lambda i: (0, 0)), pl.BlockSpec(wt_v.shape, lambda i: (0, 0)),
                  row, row, vec, vec],
        out_specs=[row, row, vec, vec],
        out_shape=[SDS((S, D), f32), SDS((S, D), bf16), SDS((1, D), f32), SDS((1, D), f32)],
        scratch_shapes=[pltpu.VMEM((SUBLANE, D), f32), pltpu.VMEM((SUBLANE, D), f32)],
        compiler_params=_cparams(("arbitrary",), VMEM_BIG),
        name="dh2_mid_bwd",
    )(dy, dug, duv, wt_g, wt_v, x1, mo, w_pf, w_pm)


def _first_bwd(x, dx1, dh, w_pre):
    S, D = x.shape
    tm = _pick(S, 512)
    nt = S // tm

    def body(x_ref, dx1_ref, a_ref, w_ref, gx_ref, gw_ref, acc):
        i = pl.program_id(0)

        @pl.when(i == 0)
        def _():
            acc[...] = jnp.zeros_like(acc)

        r, xh = _rms_parts(x_ref[...])
        dh = a_ref[...]
        acc[...] += _colsum8(dh * xh)
        gx_ref[...] = dx1_ref[...] + _rms_bwd(xh, r, w_ref[...], dh)

        @pl.when(i == nt - 1)
        def _():
            gw_ref[...] = jnp.sum(acc[...], axis=0, keepdims=True)

    row = pl.BlockSpec((tm, D), lambda i: (i, 0))
    vec = pl.BlockSpec((1, D), lambda i: (0, 0))
    return pl.pallas_call(
        body,
        grid=(nt,),
        in_specs=[row, row, row, vec],
        out_specs=[row, vec],
        out_shape=[SDS((S, D), f32), SDS((1, D), f32)],
        scratch_shapes=[pltpu.VMEM((SUBLANE, D), f32)],
        compiler_params=_cparams(("arbitrary",)),
        name="first_bwd",
    )(x, dx1, dh, w_pre)


def _t5_bucket(dist):
    n = jnp.maximum(dist, 0)
    nf = jnp.maximum(n, 1).astype(f32)
    large = MAX_EXACT + (jnp.log(nf / MAX_EXACT) / math.log(MAX_DISTANCE / MAX_EXACT)
                         * (NUM_BUCKETS - MAX_EXACT)).astype(jnp.int32)
    large = jnp.minimum(large, NUM_BUCKETS - 1)
    return jnp.where(n < MAX_EXACT, n, large)


def _bias_consts(d):
    blk = ATTN_BLOCK
    rel = jnp.arange(blk)[:, None] + blk - jnp.arange(2 * blk)[None, :]
    in_win = (rel >= 0) & (rel <= blk)
    bucket = _t5_bucket(rel * d).reshape(1, -1)
    onehot = (bucket == jnp.arange(NUM_BUCKETS)[:, None]).astype(f32)
    return onehot, in_win.astype(f32).reshape(1, -1)


def _bias_build(tab_t, onehot, maskf, name, after):
    H = tab_t.shape[0]

    def body(t_ref, oh_ref, m_ref, after_ref, o_ref):
        b = jnp.dot(t_ref[...], oh_ref[...], precision=HIGHEST, preferred_element_type=f32)
        o_ref[...] = jnp.where(m_ref[...] > 0.5, b, NEG_INF)

    vm = pl.BlockSpec(memory_space=pltpu.VMEM)
    return pl.pallas_call(body, out_shape=SDS((H, onehot.shape[1]), f32), name=name,
                          in_specs=[vm, vm, vm, pl.BlockSpec(memory_space=pl.ANY)], out_specs=vm,
                          )(tab_t, onehot, maskf, after)


def _bias_grad(dbias_flat, onehot, name):
    H = dbias_flat.shape[0]

    def body(g_ref, oh_ref, o_ref):
        o_ref[...] = lax.dot_general(oh_ref[...], g_ref[...], NT, precision=HIGHEST, preferred_element_type=f32)

    return pl.pallas_call(body, out_shape=SDS((NUM_BUCKETS, H), f32), name=name)(dbias_flat, onehot)


ATTN_TILE = 512
ATTN_SUB = ATTN_TILE // ATTN_BLOCK
ATTN_HP = 4
ATTN_WIDE = ATTN_HP * LANE


def _qkv_specs(nt):
    tile = (ATTN_TILE, ATTN_WIDE)
    blk = (ATTN_BLOCK, ATTN_WIDE)
    sec = ATTN_OUT // ATTN_WIDE
    cur = lambda off: (lambda h, t: (jnp.minimum(t, nt - 1), off + h))
    prev = lambda off: (lambda h, t: (jnp.maximum(jnp.minimum(t, nt - 1) * ATTN_SUB - 1, 0), off + h))
    return [pl.BlockSpec(tile, cur(0)), pl.BlockSpec(blk, prev(sec)), pl.BlockSpec(tile, cur(sec)),
            pl.BlockSpec(blk, prev(2 * sec)), pl.BlockSpec(tile, cur(2 * sec))]


def _head_masks():
    lane = lax.broadcasted_iota(jnp.int32, (ATTN_BLOCK, LANE), 1)
    return lane < HEAD_DIM


def _stack_heads(x2, low):
    zero = jnp.zeros_like(x2)
    return jnp.concatenate([jnp.where(low, x2, zero), jnp.where(low, zero, x2)], axis=0)


def _attn_fwd(qkv, bias, bps, name, after=None):
    S = qkv.shape[0]
    nt = S // ATTN_TILE
    scale = HEAD_DIM ** -0.5

    def body(q_ref, kp_ref, kc_ref, vp_ref, vc_ref, b_ref, *rest):
        o_ref, l_ref = rest[-2:]
        t = pl.program_id(1)
        low = _head_masks()
        col = lax.broadcasted_iota(jnp.int32, (2 * ATTN_BLOCK, 2 * ATTN_BLOCK), 1)
        for hp in range(ATTN_HP):
            cols = slice(hp * LANE, (hp + 1) * LANE)
            kk = jnp.concatenate([kp_ref[:, cols], kc_ref[:, cols]], axis=0)
            vv = jnp.concatenate([vp_ref[:, cols], vc_ref[:, cols]], axis=0)
            bias2 = b_ref[2 * hp:2 * hp + 2].reshape(2 * ATTN_BLOCK, 2 * ATTN_BLOCK)
            for b in range(ATTN_SUB):
                lo = b * ATTN_BLOCK
                rows = slice(lo, lo + ATTN_BLOCK)
                keys = slice(lo, lo + 2 * ATTN_BLOCK)
                dead = jnp.logical_and((t * ATTN_SUB + b) % bps == 0, col < ATTN_BLOCK)
                q2 = _stack_heads(q_ref[rows, cols], low)
                kb, vb = kk[keys], vv[keys]
                s = lax.dot_general(q2, kb, NT, preferred_element_type=f32) * scale + bias2
                s = jnp.where(dead, NEG_INF, s)
                m = jnp.max(s, axis=-1, keepdims=True)
                p = jnp.exp(s - m)
                l = jnp.sum(p, axis=-1, keepdims=True)
                o2 = jnp.dot(p.astype(bf16), vb, preferred_element_type=f32) / l
                lse = m + jnp.log(l)
                o_ref[rows, cols] = jnp.where(low, o2[:ATTN_BLOCK], o2[ATTN_BLOCK:])
                l_ref[rows, cols] = jnp.where(low, lse[:ATTN_BLOCK], lse[ATTN_BLOCK:])

    tile = pl.BlockSpec((ATTN_TILE, ATTN_WIDE), lambda h, t: (t, h))
    return pl.pallas_call(
        body,
        grid=(4 // ATTN_HP, nt),
        in_specs=_qkv_specs(nt) + [pl.BlockSpec((2 * ATTN_HP, ATTN_BLOCK, 2 * ATTN_BLOCK), lambda h, t: (h, 0, 0))]
        + ([] if after is None else [pl.BlockSpec(memory_space=pl.ANY)]),
        out_specs=[tile, tile],
        out_shape=[SDS((S, ATTN_OUT), f32), SDS((S, ATTN_OUT), f32)],
        compiler_params=_cparams(("parallel", "parallel")),
        name=name,
    )(qkv, qkv, qkv, qkv, qkv, bias, *([] if after is None else [after]))


def _attn_bwd(qkv, bias, do, dvec, lse, bps, name):
    S = qkv.shape[0]
    nt = S // ATTN_TILE
    scale = HEAD_DIM ** -0.5

    def assemble(parts):
        rows = [parts[0][:ATTN_BLOCK]]
        for b in range(ATTN_SUB - 1):
            rows.append(parts[b][ATTN_BLOCK:] + parts[b + 1][:ATTN_BLOCK])
        rows.append(parts[-1][ATTN_BLOCK:])
        return rows

    def body(q_ref, kp_ref, kc_ref, vp_ref, vc_ref, b_ref, do_ref, dvec_ref, lse_ref,
             dq_ref, dk_ref, dv_ref, db_ref, ck, cv):
        t = pl.program_id(1)
        last = ATTN_TILE - ATTN_BLOCK

        @pl.when(t == 0)
        def _():
            ck[...] = jnp.zeros_like(ck)
            cv[...] = jnp.zeros_like(cv)
            db_ref[...] = jnp.zeros_like(db_ref)

        @pl.when(t < nt)
        def _():
            low = _head_masks()
            col = lax.broadcasted_iota(jnp.int32, (2 * ATTN_BLOCK, 2 * ATTN_BLOCK), 1)
            per_row = lambda t2: jnp.concatenate([t2[:, 0:1], t2[:, HEAD_DIM:HEAD_DIM + 1]], axis=0)
            for hp in range(ATTN_HP):
                cols = slice(hp * LANE, (hp + 1) * LANE)
                kk = jnp.concatenate([kp_ref[:, cols], kc_ref[:, cols]], axis=0)
                vv = jnp.concatenate([vp_ref[:, cols], vc_ref[:, cols]], axis=0)
                bias2 = b_ref[2 * hp:2 * hp + 2].reshape(2 * ATTN_BLOCK, 2 * ATTN_BLOCK)
                dk_parts, dv_parts = [], []
                dsum = None
                for b in range(ATTN_SUB):
                    lo = b * ATTN_BLOCK
                    rows = slice(lo, lo + ATTN_BLOCK)
                    keys = slice(lo, lo + 2 * ATTN_BLOCK)
                    dead = jnp.logical_and((t * ATTN_SUB + b) % bps == 0, col < ATTN_BLOCK)
                    q2 = _stack_heads(q_ref[rows, cols], low)
                    do2 = _stack_heads(do_ref[rows, cols].astype(bf16), low)
                    kb, vb = kk[keys], vv[keys]
                    s = lax.dot_general(q2, kb, NT, preferred_element_type=f32) * scale + bias2
                    s = jnp.where(dead, NEG_INF, s)
                    p = jnp.exp(s - per_row(lse_ref[rows, cols]))
                    dp = lax.dot_general(do2, vb, NT, preferred_element_type=f32)
                    ds = p * (dp - per_row(dvec_ref[rows, cols]))
                    dsum = ds if dsum is None else dsum + ds
                    dsb = ds.astype(bf16)
                    dq2 = jnp.dot(dsb, kb, preferred_element_type=f32) * scale
                    dq_ref[rows, cols] = jnp.where(low, dq2[:ATTN_BLOCK], dq2[ATTN_BLOCK:]).astype(bf16)
                    dk_parts.append(lax.dot_general(dsb, q2, TN, preferred_element_type=f32) * scale)
                    dv_parts.append(lax.dot_general(p.astype(bf16), do2, TN, preferred_element_type=f32))
                db_ref[2 * hp:2 * hp + 2] += dsum.reshape(2, ATTN_BLOCK, 2 * ATTN_BLOCK)
                for parts, carry, out_ref in ((dk_parts, ck, dk_ref), (dv_parts, cv, dv_ref)):
                    rws = assemble(parts)
                    out_ref[:last, cols] = carry[:last, cols].astype(bf16)
                    out_ref[last:, cols] = (carry[last:, cols] + rws[0]).astype(bf16)
                    for b in range(ATTN_SUB):
                        carry[b * ATTN_BLOCK:(b + 1) * ATTN_BLOCK, cols] = rws[b + 1]

        @pl.when(t == nt)
        def _():
            dk_ref[...] = ck[...].astype(bf16)
            dv_ref[...] = cv[...].astype(bf16)

    tile = (ATTN_TILE, ATTN_WIDE)
    cur = pl.BlockSpec(tile, lambda h, t: (jnp.minimum(t, nt - 1), h))
    lag = pl.BlockSpec(tile, lambda h, t: (jnp.maximum(t - 1, 0), h))
    bspec = pl.BlockSpec((2 * ATTN_HP, ATTN_BLOCK, 2 * ATTN_BLOCK), lambda h, t: (h, 0, 0))
    return pl.pallas_call(
        body,
        grid=(4 // ATTN_HP, nt + 1),
        in_specs=_qkv_specs(nt) + [bspec, cur, cur, cur],
        out_specs=[cur, lag, lag, bspec],
        out_shape=[SDS((S, ATTN_OUT), bf16), SDS((S, ATTN_OUT), bf16), SDS((S, ATTN_OUT), bf16),
                   SDS((8, ATTN_BLOCK, 2 * ATTN_BLOCK), f32)],
        scratch_shapes=[pltpu.VMEM(tile, f32), pltpu.VMEM(tile, f32)],
        compiler_params=_cparams(("parallel", "arbitrary")),
        name=name,
    )(qkv, qkv, qkv, qkv, qkv, bias, do, dvec, lse)


def _attn_merge(o0, o1, o2, l0, l1, l2):
    S, W = o0.shape
    R = PERM_ROWS

    def body(o0_ref, o1_ref, o2_ref, l0_ref, l1_ref, l2_ref, y_ref, yb_ref, w0_ref, w1_ref, w2_ref,
             so1, so2, sl1, sl2):
        _to_natural(o1_ref, so1, 4)
        _to_natural(l1_ref, sl1, 4)
        _to_natural(o2_ref, so2, 16)
        _to_natural(l2_ref, sl2, 16)
        a, b, c = l0_ref[...], sl1[...], sl2[...]
        m = jnp.maximum(jnp.maximum(a, b), c)
        ea, eb, ec = jnp.exp(a - m), jnp.exp(b - m), jnp.exp(c - m)
        den = (ea + eb) + ec
        w0, w1, w2 = ea / den, eb / den, ec / den
        y = (w0 * o0_ref[...] + w1 * so1[...]) + w2 * so2[...]
        y_ref[...] = y
        yb_ref[...] = y.astype(bf16)
        w0_ref[...] = w0
        w1_ref[...] = w1
        w2_ref[...] = w2

    nat = pl.BlockSpec((R, LANE), lambda i, j: (i, j))
    v4 = lambda t: t.reshape(4, S // 4, W)
    v16 = lambda t: t.reshape(16, S // 16, W)
    return pl.pallas_call(
        body,
        grid=(S // R, W // LANE),
        in_specs=[nat, _perm_spec(4), _perm_spec(16)] * 2,
        out_specs=[nat] * 5,
        out_shape=[SDS((S, W), f32), SDS((S, W), bf16)] + [SDS((S, W), f32)] * 3,
        scratch_shapes=[pltpu.VMEM((R, LANE), f32)] * 4,
        compiler_params=_cparams(("parallel", "parallel"), VMEM_BIG),
        name="attn_merge",
    )(o0, v4(o1), v16(o2), l0, v4(l1), v16(l2))


def _attn_merge_bwd(dy, y, w0, w1, w2, after=None):
    S, W = dy.shape
    R = PERM_ROWS

    def body(dy_ref, y_ref, w0_ref, w1_ref, w2_ref, *rest):
        a0, a1, a2, b0, b1, b2, sa, sb = rest[-8:]
        dyv = dy_ref[...]
        r = lax.broadcasted_iota(jnp.int32, (LANE, LANE), 0) // HEAD_DIM
        c = lax.broadcasted_iota(jnp.int32, (LANE, LANE), 1) // HEAD_DIM
        seg = jnp.where(r == c, 1.0, 0.0).astype(f32)
        cbar = jnp.dot(dyv * y_ref[...], seg, precision=HIGHEST, preferred_element_type=f32)
        w = w0_ref[...]
        a0[...] = (w * dyv).astype(bf16)
        b0[...] = w * cbar
        for d, w_ref, a_ref, b_ref in ((4, w1_ref, a1, b1), (16, w2_ref, a2, b2)):
            w = w_ref[...]
            sa[...] = w * dyv
            sb[...] = w * cbar
            n = R // d
            for k in range(d):
                rows = pl.ds(k, n, stride=d)
                a_ref[k] = sa[rows, :].astype(bf16)
                b_ref[k] = sb[rows, :]

    nat = pl.BlockSpec((R, LANE), lambda i, j: (i, j))
    shapes = lambda dt: [SDS((S, W), dt), SDS((4, S // 4, W), dt), SDS((16, S // 16, W), dt)]
    outs = pl.pallas_call(
        body,
        grid=(S // R, W // LANE),
        in_specs=[nat] * 5 + ([] if after is None else [pl.BlockSpec(memory_space=pl.ANY)]),
        out_specs=[nat, _perm_spec(4), _perm_spec(16)] * 2,
        out_shape=shapes(bf16) + shapes(f32),
        scratch_shapes=[pltpu.VMEM((R, LANE), f32)] * 2,
        compiler_params=_cparams(("parallel", "parallel"), VMEM_BIG),
        name="attn_merge_bwd",
    )(dy, y, w0, w1, w2, *([] if after is None else [after]))
    return [t.reshape(S, W) for t in outs]


HGRN_SB = 256
HGRN_PAIR = 4


def _chunk_masks():
    r = jnp.arange(HGRN_SB)[:, None]
    c = jnp.arange(HGRN_SB)[None, :]
    same = (r // HGRN_CHUNK) == (c // HGRN_CHUNK)
    return jnp.stack([same & (c <= r), same, same & (c >= r)]).astype(bf16)


def _mask_dot(mask, x):
    hi = x.astype(bf16)
    r1 = x - hi.astype(f32)
    mid = r1.astype(bf16)
    lo = (r1 - mid.astype(f32)).astype(bf16)
    p = jnp.dot(mask, jnp.concatenate([hi, mid, lo], axis=1), preferred_element_type=f32)
    n = x.shape[1]
    return (p[:, :n] + p[:, n:2 * n]) + p[:, 2 * n:]


def _hgrn_prep(q_raw, f_raw, lbv, tril, same):
    sq = _sigmoid(q_raw)
    qs = q_raw * sq
    sig = _sigmoid(f_raw)
    f = lbv + (1.0 - lbv) * sig
    g = jnp.log(f)
    k = 1.0 - f
    G = _mask_dot(tril, g)
    GL = _mask_dot(same, g)
    eG = jnp.exp(G)
    einv = jnp.exp(-G)
    edec = jnp.exp(GL - G)
    return dict(sq=sq, qs=qs, sig=sig, f=f, k=k, eG=eG, einv=einv, edec=edec, eGL=jnp.exp(GL),
                qt=qs * eG, kt=k * einv, kd=k * edec)


def _ride_split(ride, rest, n_out, n_scratch):
    if ride is None:
        return None, rest[:n_out], None, rest[n_out:], None
    return rest[0], rest[1:1 + n_out], rest[1 + n_out], rest[2 + n_out:2 + n_out + n_scratch], rest[2 + n_out + n_scratch:]


def _hgrn_fwd(hg, lb, normw, ride=None):
    S = hg.shape[0]
    sb = HGRN_SB
    nsb = S // sb
    nch = sb // HGRN_CHUNK

    def body(q_ref, f_ref, v_ref, og_ref, lb_ref, nw_ref, m_ref, *rest):
        src_ref, (y_ref, o_ref, ck_ref), got_ref, (st,), sems = _ride_split(ride, rest, 3, 1)
        j = pl.program_id(1)
        if ride is not None:
            @pl.when(jnp.logical_and(pl.program_id(0) == 0, j == 0))
            def _():
                _chip_start(src_ref, got_ref, sems[0], sems[1], ride[1])

        @pl.when(j == 0)
        def _():
            st[...] = jnp.zeros_like(st)

        tril_m = m_ref[0]
        tril = tril_m.astype(f32) > 0.5

        def one_head(hh):
            cols = slice(hh * LANE, (hh + 1) * LANE)
            ST = st[hh]
            ck_ref[hh, 0] = ST
            pr = _hgrn_prep(q_ref[:, cols], f_ref[:, cols], lb_ref[:, cols], tril_m, m_ref[1])
            qtb, ktb, kdb = pr["qt"].astype(bf16), pr["kt"].astype(bf16), pr["kd"].astype(bf16)
            eGL = pr["eGL"]
            vb = v_ref[:, cols].astype(bf16)
            A = jnp.where(tril, lax.dot_general(qtb, ktb, NT, preferred_element_type=f32), 0.0)
            o = jnp.dot(A.astype(bf16), vb, preferred_element_type=f32)
            outs = []
            for ci in range(nch):
                lo = ci * HGRN_CHUNK
                sl = slice(lo, lo + HGRN_CHUNK)
                outs.append(o[sl] + lax.dot_general(qtb[sl], ST.astype(bf16), NT, preferred_element_type=f32))
                ST = ST * eGL[lo:lo + 1, :] + lax.dot_general(vb[sl], kdb[sl], TN, preferred_element_type=f32)
            st[hh] = ST
            of = jnp.concatenate(outs, axis=0)
            o_ref[:, cols] = of
            rms = lax.rsqrt(jnp.mean(of * of, axis=-1, keepdims=True) + EPS)
            ogv = og_ref[:, cols]
            y_ref[:, cols] = ((of * rms * nw_ref[...]) * (ogv * _sigmoid(ogv))).astype(bf16)

        for hh in range(HGRN_PAIR):
            one_head(hh)

        if ride is not None:
            @pl.when(jnp.logical_and(pl.program_id(0) == ngrp - 1, j == nsb - 1))
            def _():
                _chip_finish(src_ref, got_ref, sems[0], sems[1], ride[1])

    wide = HGRN_PAIR * LANE
    ngrp = 4 // HGRN_PAIR
    col = lambda off: pl.BlockSpec((sb, wide), lambda h, j: (j, off // HGRN_PAIR + h))
    riding = ride is not None
    res = pl.pallas_call(
        body,
        grid=(ngrp, nsb),
        in_specs=[col(0), col(4), col(8), col(12), pl.BlockSpec((1, wide), lambda h, j: (0, h)),
                  pl.BlockSpec((1, LANE), lambda h, j: (0, 0)),
                  pl.BlockSpec((3, sb, sb), lambda h, j: (0, 0, 0))] + ([_ANY] if riding else []),
        out_specs=[col(0), col(0), pl.BlockSpec((HGRN_PAIR, 1, LANE, LANE), lambda h, j: (h, j, 0, 0))]
        + ([_ANY] if riding else []),
        out_shape=[SDS((S, HGRN_W), bf16), SDS((S, HGRN_W), f32), SDS((4, nsb, LANE, LANE), f32)]
        + ([_chip_out_shape(*ride)] if riding else []),
        scratch_shapes=[pltpu.VMEM((HGRN_PAIR, LANE, LANE), f32)] + (list(_CHIP_SEMS) if riding else []),
        compiler_params=_cparams(("arbitrary", "arbitrary") if riding else ("parallel", "arbitrary")),
        name="hgrn_fwd",
    )(hg, hg, hg, hg, lb, normw, _chunk_masks(), *([ride[0]] if riding else []))
    return tuple(res) if riding else (*res, None)


def _hgrn_bwd(hg, o_raw, dy, ck, lb, normw, ride=None):
    S = hg.shape[0]
    sb = HGRN_SB
    nsb = S // sb
    nch = sb // HGRN_CHUNK

    def body(q_ref, f_ref, v_ref, og_ref, o_ref, dy_ref, ck_ref, lb_ref, nw_ref, m_ref, *rest):
        src_ref, outs, got_ref, (dst, alb, anw), sems = _ride_split(ride, rest, 6, 3)
        dq_ref, df_ref, dv_ref, dog_ref, glb_ref, gnw_ref = outs
        j = pl.program_id(1)
        if ride is not None:
            @pl.when(jnp.logical_and(pl.program_id(0) == 0, j == 0))
            def _():
                _chip_start(src_ref, got_ref, sems[0], sems[1], ride[1])

        @pl.when(j == 0)
        def _():
            dst[...] = jnp.zeros_like(dst)
            alb[...] = jnp.zeros_like(alb)
            anw[...] = jnp.zeros_like(anw)

        tril_m = m_ref[0]
        tril = tril_m.astype(f32) > 0.5
        nw = nw_ref[...]

        def one_head(hh):
            cols = slice(hh * LANE, (hh + 1) * LANE)
            lbv = lb_ref[:, cols]
            q_raw = q_ref[:, cols]
            pr = _hgrn_prep(q_raw, f_ref[:, cols], lbv, tril_m, m_ref[1])
            qt, kt, kd, eGL = pr["qt"], pr["kt"], pr["kd"], pr["eGL"]
            qtb, ktb, kdb = qt.astype(bf16), kt.astype(bf16), kd.astype(bf16)
            vb = v_ref[:, cols].astype(bf16)

            o = o_ref[:, cols]
            ogv = og_ref[:, cols]
            sog = _sigmoid(ogv)
            rms = lax.rsqrt(jnp.mean(o * o, axis=-1, keepdims=True) + EPS)
            oh = o * rms
            dyv = dy_ref[:, cols]
            dog_ref[:, cols] = (dyv * (oh * nw) * (sog * (1.0 + ogv * (1.0 - sog)))).astype(bf16)
            dohw = dyv * (ogv * sog)
            anw[:, cols] += _colsum8(dohw * oh)
            doh = dohw * nw
            do = rms * (doh - oh * jnp.mean(doh * oh, axis=-1, keepdims=True))
            dob = do.astype(bf16)

            Ab = jnp.where(tril, lax.dot_general(qtb, ktb, NT, preferred_element_type=f32), 0.0).astype(bf16)
            dAb = jnp.where(tril, lax.dot_general(dob, vb, NT, preferred_element_type=f32), 0.0).astype(bf16)
            dv_acc = lax.dot_general(Ab, dob, TN, preferred_element_type=f32)
            dqt = jnp.dot(dAb, ktb, preferred_element_type=f32)
            dkt = lax.dot_general(dAb, qtb, TN, preferred_element_type=f32)

            ST = ck_ref[hh, 0]
            states = []
            for ci in range(nch):
                lo = ci * HGRN_CHUNK
                sl = slice(lo, lo + HGRN_CHUNK)
                states.append(ST)
                ST = ST * eGL[lo:lo + 1, :] + lax.dot_general(vb[sl], kdb[sl], TN, preferred_element_type=f32)

            dST = dst[hh]
            dqt_i, dkd_i, dv_i, deg_i = [None] * nch, [None] * nch, [None] * nch, [None] * nch
            for ci in reversed(range(nch)):
                lo = ci * HGRN_CHUNK
                sl = slice(lo, lo + HGRN_CHUNK)
                ST0 = states[ci]
                dSTb = dST.astype(bf16)
                dv_i[ci] = lax.dot_general(kdb[sl], dSTb, NT, preferred_element_type=f32)
                dqt_i[ci] = jnp.dot(dob[sl], ST0.astype(bf16), preferred_element_type=f32)
                dkd_i[ci] = jnp.dot(vb[sl], dSTb, preferred_element_type=f32)
                deg_i[ci] = jnp.broadcast_to(jnp.sum(dST * ST0, axis=0, keepdims=True), (HGRN_CHUNK, LANE))
                dST = dST * eGL[lo:lo + 1, :] + lax.dot_general(dob[sl], qtb[sl], TN, preferred_element_type=f32)
            dst[hh] = dST

            dqt = dqt + jnp.concatenate(dqt_i, axis=0)
            dkd = jnp.concatenate(dkd_i, axis=0)
            dv_ref[:, cols] = (dv_acc + jnp.concatenate(dv_i, axis=0)).astype(bf16)
            deg = jnp.concatenate(deg_i, axis=0)

            dqs = dqt * pr["eG"]
            dkdkd = dkd * kd
            dG = dqt * qt - dkt * kt - dkdkd
            dk = dkt * pr["einv"] + dkd * pr["edec"]
            dGL = _mask_dot(m_ref[1], dkdkd) + eGL * deg
            dg = _mask_dot(m_ref[2], dG) + dGL
            df = dg / pr["f"] - dk
            sig = pr["sig"]
            df_ref[:, cols] = (df * (1.0 - lbv) * (sig * (1.0 - sig))).astype(bf16)
            alb[:, cols] += _colsum8(df * (1.0 - sig))
            sq = pr["sq"]
            dq_ref[:, cols] = (dqs * (sq * (1.0 + q_raw * (1.0 - sq)))).astype(bf16)

        for hh in range(HGRN_PAIR):
            one_head(hh)

        @pl.when(j == nsb - 1)
        def _():
            glb_ref[...] = jnp.broadcast_to(jnp.sum(alb[...], axis=0, keepdims=True), (SUBLANE, wide))
            gnw_ref[...] = jnp.broadcast_to(jnp.sum(anw[...], axis=0, keepdims=True), (SUBLANE, wide))

        if ride is not None:
            @pl.when(jnp.logical_and(pl.program_id(0) == ngrp - 1, j == nsb - 1))
            def _():
                _chip_finish(src_ref, got_ref, sems[0], sems[1], ride[1])

    wide = HGRN_PAIR * LANE
    ngrp = 4 // HGRN_PAIR
    rev = lambda off: pl.BlockSpec((sb, wide), lambda h, j: (nsb - 1 - j, off // HGRN_PAIR + h))
    stat = pl.BlockSpec((SUBLANE, wide), lambda h, j: (0, h))
    riding = ride is not None
    res = pl.pallas_call(
        body,
        grid=(ngrp, nsb),
        in_specs=[rev(0), rev(4), rev(8), rev(12), rev(0), rev(0),
                  pl.BlockSpec((HGRN_PAIR, 1, LANE, LANE), lambda h, j: (h, nsb - 1 - j, 0, 0)),
                  pl.BlockSpec((1, wide), lambda h, j: (0, h)), pl.BlockSpec((1, LANE), lambda h, j: (0, 0)),
                  pl.BlockSpec((3, sb, sb), lambda h, j: (0, 0, 0))]
        + ([_ANY] if riding else []),
        out_specs=[rev(0), rev(0), rev(0), rev(0), stat, stat] + ([_ANY] if riding else []),
        out_shape=[SDS((S, HGRN_W), bf16)] * 4 + [SDS((SUBLANE, HGRN_W), f32)] * 2
        + ([_chip_out_shape(*ride)] if riding else []),
        scratch_shapes=[pltpu.VMEM((HGRN_PAIR, LANE, LANE), f32), pltpu.VMEM((SUBLANE, wide), f32),
                        pltpu.VMEM((SUBLANE, wide), f32)] + (list(_CHIP_SEMS) if riding else []),
        compiler_params=_cparams(("arbitrary", "arbitrary") if riding else ("parallel", "arbitrary")),
        name="hgrn_bwd",
    )(hg, hg, hg, hg, o_raw, dy, ck, lb, normw, _chunk_masks(), *([ride[0]] if riding else []))
    return tuple(res) if riding else (*res, None)


def _lb_fwd(raw):
    def body(r_ref, o_ref):
        r = r_ref[...]
        m = jnp.max(r, axis=0, keepdims=True)
        e = jnp.exp(r - m)
        o_ref[...] = (e / jnp.sum(e, axis=0, keepdims=True))[0:1]

    return pl.pallas_call(body, out_shape=SDS((1, raw.shape[1]), f32), name="lb_fwd")(raw)


def _lb_bwd(raw, dlb):
    def body(r_ref, d_ref, o_ref):
        r = r_ref[...]
        m = jnp.max(r, axis=0, keepdims=True)
        e = jnp.exp(r - m)
        s = e / jnp.sum(e, axis=0, keepdims=True)
        s0 = s[0:1]
        onehot0 = jnp.where(lax.broadcasted_iota(jnp.int32, r.shape, 0) == 0, 1.0, 0.0)
        o_ref[...] = d_ref[...] * s0 * (onehot0 - s)

    return pl.pallas_call(body, out_shape=SDS(raw.shape, f32), name="lb_bwd")(raw, dlb)


def _gate_fwd(ya, yh, w_ba, w_bh, gc):
    S = ya.shape[0]
    D = w_ba.shape[1]
    tm = _pick(S, MM_ROWS)

    def body(ya_ref, yh_ref, wa_ref, wh_ref, g0_ref, g1_ref, a_ref, b_ref, o_ref):
        a = jnp.dot(ya_ref[...], wa_ref[...], preferred_element_type=f32).astype(bf16)
        b = jnp.dot(yh_ref[...], wh_ref[...], preferred_element_type=f32).astype(bf16)
        a_ref[...] = a
        b_ref[...] = b
        s0, s1 = _sigmoid(g0_ref[...].astype(f32)), _sigmoid(g1_ref[...].astype(f32))
        o_ref[...] = (s0 * a.astype(f32) + s1 * b.astype(f32)).astype(bf16)

    row = pl.BlockSpec((tm, D), lambda i: (i, 0))
    act = pl.BlockSpec((tm, ya.shape[1]), lambda i: (i, 0))
    wspec = pl.BlockSpec(w_ba.shape, lambda i: (0, 0))
    return pl.pallas_call(
        body,
        grid=(S // tm,),
        in_specs=[act, act, wspec, wspec, row, pl.BlockSpec((tm, D), lambda i: (i, 1))],
        out_specs=[row, row, row],
        out_shape=[SDS((S, D), bf16)] * 3,
        compiler_params=_cparams(("parallel",), VMEM_BIG),
        name="branch_gate_fwd",
    )(ya, yh, w_ba, w_bh, gc, gc)


def _gate_bwd(dmo, w_out, a, b, gc, w_ba, w_bh):
    S, D = a.shape
    W = w_ba.shape[0]
    tm = _pick(S, MM_ROWS)

    def body(dmo_ref, wo_ref, a_ref, b_ref, g0_ref, g1_ref, wa_ref, wh_ref,
             da_ref, db_ref, dg_ref, dya_ref, dyh_ref):
        dm = lax.dot_general(dmo_ref[...], wo_ref[...], NT, preferred_element_type=f32)
        dmv = dm.astype(bf16).astype(f32)
        s0, s1 = _sigmoid(g0_ref[...].astype(f32)), _sigmoid(g1_ref[...].astype(f32))
        da = (dmv * s0).astype(bf16)
        db = (dmv * s1).astype(bf16)
        da_ref[...] = da
        db_ref[...] = db
        dg_ref[:, :D] = (dmv * a_ref[...].astype(f32) * (s0 * (1.0 - s0))).astype(bf16)
        dg_ref[:, D:] = (dmv * b_ref[...].astype(f32) * (s1 * (1.0 - s1))).astype(bf16)
        dya_ref[...] = lax.dot_general(da, wa_ref[...], NT, preferred_element_type=f32)
        dyh_ref[...] = lax.dot_general(db, wh_ref[...], NT, preferred_element_type=f32)

    row = pl.BlockSpec((tm, D), lambda i: (i, 0))
    wide = pl.BlockSpec((tm, 2 * D), lambda i: (i, 0))
    narrow = pl.BlockSpec((tm, W), lambda i: (i, 0))
    whole = lambda t: pl.BlockSpec(t.shape, lambda i: (0, 0))
    return pl.pallas_call(
        body,
        grid=(S // tm,),
        in_specs=[row, whole(w_out), row, row, row, pl.BlockSpec((tm, D), lambda i: (i, 1)), whole(w_ba), whole(w_bh)],
        out_specs=[row, row, wide, narrow, narrow],
        out_shape=[SDS((S, D), bf16), SDS((S, D), bf16), SDS((S, 2 * D), bf16), SDS((S, W), f32), SDS((S, W), f32)],
        compiler_params=_cparams(("parallel",), VMEM_BIG),
        name="gate_bwd_fused",
    )(dmo, w_out, a, b, gc, gc, w_ba, w_bh)


CONV_ROWS = 512
INV_SQRT2 = 0.7071067811865476
INV_SQRT_2PI = 0.3989422804014327


CONV_HALO = 16


def _shift_down(cur, prev, k):
    x = pltpu.roll(cur, k, 0)
    row = lax.broadcasted_iota(jnp.int32, (SUBLANE, LANE), 0)
    head = jnp.where(row < k, pltpu.roll(prev, k, 0)[:SUBLANE], x[:SUBLANE])
    return jnp.concatenate([head, x[SUBLANE:]], axis=0)


def _shift_up(cur, nxt, k):
    R = cur.shape[0]
    x = pltpu.roll(cur, R - k, 0)
    row = lax.broadcasted_iota(jnp.int32, (SUBLANE, LANE), 0)
    tail = jnp.where(row >= SUBLANE - k, pltpu.roll(nxt, SUBLANE - k, 0), x[R - SUBLANE:])
    return jnp.concatenate([x[:R - SUBLANE], tail], axis=0)


def _conv_rows(u_ref, w, b, r0, first):
    R = CONV_ROWS
    cur = u_ref[pl.ds(r0, R), :].astype(f32)
    prev = u_ref[pl.ds(pl.multiple_of(jnp.maximum(r0 - CONV_HALO, 0), CONV_HALO), CONV_HALO), :].astype(f32)
    prev = jnp.where(first, 0.0, prev)
    x1 = _shift_down(cur, prev, 1)
    x2 = _shift_down(cur, prev, 2)
    c = ((b + w[0:1] * x2) + w[1:2] * x1) + w[2:3] * cur
    return c, x2, x1, cur


def _conv_fwd(ug, uv, wg, wv, bg, bv):
    S, F = ug.shape
    nchunk = S // CONV_ROWS

    def body(ug_ref, uv_ref, wg_ref, wv_ref, bg_ref, bv_ref, o_ref):
        wgv, wvv, bgv, bvv = wg_ref[...], wv_ref[...], bg_ref[...], bv_ref[...]

        def step(ci, carry):
            r0 = pl.multiple_of(ci * CONV_ROWS, CONV_ROWS)
            cg = _conv_rows(ug_ref, wgv, bgv, r0, ci == 0)[0]
            cv = _conv_rows(uv_ref, wvv, bvv, r0, ci == 0)[0]
            gelu = 0.5 * cg * (1.0 + lax.erf(cg * INV_SQRT2))
            o_ref[pl.ds(r0, CONV_ROWS), :] = (gelu * cv).astype(bf16)
            return carry

        lax.fori_loop(0, nchunk, step, 0)

    col = pl.BlockSpec((S, LANE), lambda j: (0, j))
    w3 = pl.BlockSpec((3, LANE), lambda j: (0, j))
    b1 = pl.BlockSpec((1, LANE), lambda j: (0, j))
    return pl.pallas_call(
        body,
        grid=(F // LANE,),
        in_specs=[col, col, w3, w3, b1, b1],
        out_specs=col,
        out_shape=SDS((S, F), bf16),
        compiler_params=_cparams(("parallel",), VMEM_BIG),
        name="conv_fwd",
    )(ug, uv, wg, wv, bg, bv)


def _conv_bwd(ug, uv, dact, wg, wv, bg, bv):
    S, F = ug.shape
    R = CONV_ROWS
    nchunk = S // R

    def body(ug_ref, uv_ref, da_ref, wg_ref, wv_ref, bg_ref, bv_ref, dug_ref, duv_ref, sg_ref, sv_ref, dcg, dcv):
        wgv, wvv, bgv, bvv = wg_ref[...], wv_ref[...], bg_ref[...], bv_ref[...]
        zero = jnp.zeros((SUBLANE, LANE), f32)

        def fwd_step(ci, acc):
            r0 = pl.multiple_of(ci * R, R)
            cg, g2, g1, g0 = _conv_rows(ug_ref, wgv, bgv, r0, ci == 0)
            cv, v2, v1, v0 = _conv_rows(uv_ref, wvv, bvv, r0, ci == 0)
            da = da_ref[pl.ds(r0, R), :].astype(f32)
            cdf = 0.5 * (1.0 + lax.erf(cg * INV_SQRT2))
            pdf = INV_SQRT_2PI * jnp.exp(-0.5 * cg * cg)
            dg = da * cv * (cdf + cg * pdf)
            dv = da * (cg * cdf)
            dcg[pl.ds(r0, R), :] = dg
            dcv[pl.ds(r0, R), :] = dv
            new = (acc[0] + _colsum8(dg * g2), acc[1] + _colsum8(dg * g1), acc[2] + _colsum8(dg * g0),
                   acc[3] + _colsum8(dg),
                   acc[4] + _colsum8(dv * v2), acc[5] + _colsum8(dv * v1), acc[6] + _colsum8(dv * v0),
                   acc[7] + _colsum8(dv))
            return new

        acc = lax.fori_loop(0, nchunk, fwd_step, (zero,) * 8)
        rows = lax.broadcasted_iota(jnp.int32, (SUBLANE, LANE), 0)

        def stats(parts):
            out = jnp.zeros((SUBLANE, LANE), f32)
            for k, pt in enumerate(parts):
                out = jnp.where(rows == k, jnp.sum(pt, axis=0, keepdims=True), out)
            return out

        sg_ref[...] = stats(acc[0:4])
        sv_ref[...] = stats(acc[4:8])

        def du_rows(dc, w, r0, last):
            cur = dc[pl.ds(r0, R), :]
            nxt = dc[pl.ds(pl.multiple_of(jnp.minimum(r0 + R, S - SUBLANE), SUBLANE), SUBLANE), :]
            nxt = jnp.where(last, 0.0, nxt)
            return w[2:3] * cur + w[1:2] * _shift_up(cur, nxt, 1) + w[0:1] * _shift_up(cur, nxt, 2)

        def bwd_step(ci, carry):
            r0 = pl.multiple_of(ci * R, R)
            last = ci == nchunk - 1
            dug_ref[pl.ds(r0, R), :] = du_rows(dcg, wgv, r0, last).astype(bf16)
            duv_ref[pl.ds(r0, R), :] = du_rows(dcv, wvv, r0, last).astype(bf16)
            return carry

        lax.fori_loop(0, nchunk, bwd_step, 0)

    col = pl.BlockSpec((S, LANE), lambda j: (0, j))
    w3 = pl.BlockSpec((3, LANE), lambda j: (0, j))
    b1 = pl.BlockSpec((1, LANE), lambda j: (0, j))
    st = pl.BlockSpec((SUBLANE, LANE), lambda j: (0, j))
    return pl.pallas_call(
        body,
        grid=(F // LANE,),
        in_specs=[col, col, col, w3, w3, b1, b1],
        out_specs=[col, col, st, st],
        out_shape=[SDS((S, F), bf16), SDS((S, F), bf16), SDS((SUBLANE, F), f32), SDS((SUBLANE, F), f32)],
        scratch_shapes=[pltpu.VMEM((S, LANE), f32), pltpu.VMEM((S, LANE), f32)],
        compiler_params=_cparams(("parallel",), VMEM_BIG),
        name="conv_bwd",
    )(ug, uv, dact, wg, wv, bg, bv)


def _adam_math(w, g, m, v):
    m = ADAM_B1 * m + (1.0 - ADAM_B1) * g
    v = ADAM_B2 * v + (1.0 - ADAM_B2) * (g * g)
    m_hat = m / (1.0 - ADAM_B1 ** ADAM_STEP)
    v_hat = v / (1.0 - ADAM_B2 ** ADAM_STEP)
    delta = -ADAM_LR * (m_hat / (jnp.sqrt(v_hat) + ADAM_EPS) + ADAM_WD * w)
    return delta, m, v


def _adamw(w, m, v, g, name):
    R, C = w.shape
    parts = g.ndim == 3
    tr = R
    if R % 16 == 0:
        for t in range(R, 0, -16):
            if R % t == 0 and t * C * 4 <= ADAM_BLOCK_BYTES:
                tr = t
                break

    def body(w_ref, m_ref, v_ref, g_ref, go_ref, d_ref, mo_ref, vo_ref):
        if parts:
            gv = ((g_ref[0].astype(f32) + g_ref[1].astype(f32)) + g_ref[2].astype(f32)) + g_ref[3].astype(f32)
        else:
            gv = g_ref[...]
        go_ref[...] = gv
        d, mn, vn = _adam_math(w_ref[...], gv, m_ref[...], v_ref[...])
        d_ref[...] = d
        mo_ref[...] = mn
        vo_ref[...] = vn

    row = pl.BlockSpec((tr, C), lambda i: (i, 0))
    gspec = pl.BlockSpec((4, tr, C), lambda i: (0, i, 0)) if parts else row
    return pl.pallas_call(
        body,
        grid=(R // tr,),
        in_specs=[row, row, row, gspec],
        out_specs=[row] * 4,
        out_shape=[SDS((R, C), f32)] * 4,
        compiler_params=_cparams(("parallel",), VMEM_BIG),
        name=name,
    )(w, m, v, g)


def _sum8(parts, name):
    _, _, R, C = parts.shape

    def body(p_ref, o_ref):
        acc = p_ref[0, 0]
        for c in range(2):
            for k in range(4):
                if c or k:
                    acc = acc + p_ref[c, k]
        o_ref[...] = acc

    return pl.pallas_call(body, out_shape=SDS((R, C), f32), name=name)(parts)


def _pair_add(by_core, b, name):
    _, K, R, C = by_core.shape
    tr = R // 2 if R % 32 == 0 else R

    def body(c_ref, a_ref, b_ref, o_ref):
        o_ref[...] = (a_ref[0].astype(f32) + b_ref[...].astype(f32)).astype(bf16)

    blk = pl.BlockSpec((1, tr, C), lambda k, i, c: (k, i, 0))
    return pl.pallas_call(
        body,
        grid_spec=pltpu.PrefetchScalarGridSpec(
            num_scalar_prefetch=1,
            grid=(K, R // tr),
            in_specs=[pl.BlockSpec((1, 1, tr, C), lambda k, i, c: (c[0], k, i, 0)), blk],
            out_specs=blk,
        ),
        out_shape=SDS((K, R, C), bf16),
        compiler_params=_cparams(("parallel", "parallel")),
        name=name,
    )(lax.axis_index("c").astype(jnp.int32).reshape(1), by_core, b)


_ANY = pl.BlockSpec(memory_space=pl.ANY)


def _chip_copies(src_ref, out_ref, send_sems, recv_sems, gather):
    x, y, c = lax.axis_index("x"), lax.axis_index("y"), lax.axis_index("c")
    mine = 2 * x + y

    def piece(k):
        return src_ref if gather else src_ref.at[k]

    sends, recvs = [], []
    for j, (px, py) in enumerate([(1 - x, y), (x, 1 - y), (1 - x, 1 - y)]):
        sends.append(pltpu.make_async_remote_copy(
            src_ref=piece(2 * px + py), dst_ref=out_ref.at[mine], send_sem=send_sems.at[j],
            recv_sem=recv_sems.at[j], device_id=(px, py, c), device_id_type=MESH))
        recvs.append(pltpu.make_async_remote_copy(
            src_ref=piece(mine), dst_ref=out_ref.at[2 * px + py], send_sem=send_sems.at[j],
            recv_sem=recv_sems.at[j], device_id=(px, py, c), device_id_type=MESH))
    return sends, recvs


def _chip_start(src_ref, out_ref, send_sems, recv_sems, gather):
    for cp in _chip_copies(src_ref, out_ref, send_sems, recv_sems, gather)[0]:
        cp.start()


def _chip_finish(src_ref, out_ref, send_sems, recv_sems, gather):
    sends, recvs = _chip_copies(src_ref, out_ref, send_sems, recv_sems, gather)
    for cp in recvs:
        cp.wait_recv()
    for cp in sends:
        cp.wait_send()


def _chip_out_shape(src, gather):
    return SDS((4,) + tuple(src.shape if gather else src.shape[1:]), src.dtype)


_CHIP_SEMS = [pltpu.SemaphoreType.DMA((3,)), pltpu.SemaphoreType.DMA((3,))]


def _fill_own(out, src, gather):
    mine = 2 * lax.axis_index("x") + lax.axis_index("y")
    own = src if gather else lax.dynamic_index_in_dim(src, mine, axis=0, keepdims=False)
    return lax.dynamic_update_index_in_dim(out, own, mine, axis=0)


_HBM = pl.BlockSpec(memory_space=pltpu.HBM)
_SEM = pl.BlockSpec(memory_space=pltpu.SEMAPHORE)
_EFFECT = pltpu.SideEffectType.DATAFLOW_SIDE_EFFECTING
_SPLIT_PEERS = {"chip_gather": 3, "chip_xchg": 3, "core_gather": 1, "core_swap": 1}


def _split_land(src, kind):
    if kind == "core_gather":
        return SDS((2,) + tuple(src.shape), src.dtype)
    if kind == "core_swap":
        return SDS(tuple(src.shape[1:]), src.dtype)
    return _chip_out_shape(src, kind == "chip_gather")


def _split_copies(src_ref, land_ref, sems, kind):
    x, y, c = lax.axis_index("x"), lax.axis_index("y"), lax.axis_index("c")
    n = _SPLIT_PEERS[kind]
    if kind == "core_gather":
        routes = [((x, y, 1 - c), src_ref, land_ref.at[c], land_ref.at[1 - c])]
    elif kind == "core_swap":
        routes = [((x, y, 1 - c), src_ref.at[1 - c], land_ref, land_ref)]
    else:
        mine = 2 * x + y
        gather = kind == "chip_gather"
        routes = [((px, py, c), src_ref if gather else src_ref.at[2 * px + py], land_ref.at[mine],
                   land_ref.at[2 * px + py]) for px, py in [(1 - x, y), (x, 1 - y), (1 - x, 1 - y)]]
    sends, recvs = [], []
    for j, (peer, piece, there, here) in enumerate(routes):
        sends.append(pltpu.make_async_remote_copy(src_ref=piece, dst_ref=there, send_sem=sems[j],
                                                  recv_sem=sems[n + j], device_id=peer, device_id_type=MESH))
        recvs.append(pltpu.make_async_remote_copy(src_ref=piece, dst_ref=here, send_sem=sems[j],
                                                  recv_sem=sems[n + j], device_id=peer, device_id_type=MESH))
    return sends, recvs


def _split_start(src, kind, name, after=None):
    land = _split_land(src, kind)
    ns = 2 * _SPLIT_PEERS[kind]
    n_in = 2 if after is None else 3

    def body(*refs):
        src_ref, land_ref = refs[:2]
        outs = refs[n_in:]
        for cp in _split_copies(src_ref, land_ref, outs[:ns], kind)[0]:
            cp.start()
        token = outs[ns + 2]
        token[...] = jnp.zeros_like(token)

    res = pl.pallas_call(
        body,
        name=name,
        out_shape=(pltpu.SemaphoreType.DMA(()),) * ns
        + (pltpu.HBM(src.shape, src.dtype), pltpu.HBM(land.shape, land.dtype), SDS((SUBLANE, LANE), f32)),
        in_specs=(_HBM, _HBM) + (() if after is None else (_ANY,)),
        out_specs=(_SEM,) * ns + (_HBM, _HBM, pl.BlockSpec(memory_space=pltpu.VMEM)),
        input_output_aliases={0: ns, 1: ns + 1},
        compiler_params=pltpu.CompilerParams(has_side_effects=_EFFECT),
    )(pltpu.with_memory_space_constraint(src, pltpu.HBM),
      pltpu.with_memory_space_constraint(lax.empty(land.shape, land.dtype), pltpu.HBM),
      *(() if after is None else (after,)))
    return (res[:ns], res[ns], res[ns + 1]), res[ns + 2]


def _split_wait(state, after, kind, name):
    sems, src_thru, land_thru = state
    ns = 2 * _SPLIT_PEERS[kind]

    def body(src_ref, land_ref, *rest):
        sends, recvs = _split_copies(src_ref, land_ref, rest[:ns], kind)
        for cp in recvs:
            cp.wait_recv()
        for cp in sends:
            cp.wait_send()

    src_out, got = pl.pallas_call(
        body,
        name=name,
        out_shape=(pltpu.HBM(src_thru.shape, src_thru.dtype), pltpu.HBM(land_thru.shape, land_thru.dtype)),
        in_specs=(_HBM, _HBM) + (_SEM,) * ns + (_ANY,),
        out_specs=(_HBM, _HBM),
        input_output_aliases={0: 0, 1: 1},
        compiler_params=pltpu.CompilerParams(has_side_effects=_EFFECT),
    )(src_thru, land_thru, *sems, after)
    if kind == "core_swap":
        return got, src_out
    if kind == "core_gather":
        return lax.dynamic_update_index_in_dim(got, src_out, lax.axis_index("c"), axis=0)
    return _fill_own(got, src_out, kind == "chip_gather")


def _core_gather(src, name):
    def body(src_ref, out_ref, send_sem, recv_sem):
        x, y, c = lax.axis_index("x"), lax.axis_index("y"), lax.axis_index("c")
        cp = pltpu.make_async_remote_copy(src_ref=src_ref, dst_ref=out_ref.at[c], send_sem=send_sem,
                                          recv_sem=recv_sem, device_id=(x, y, 1 - c), device_id_type=MESH)
        cp.start()
        pltpu.make_async_remote_copy(src_ref=src_ref, dst_ref=out_ref.at[1 - c], send_sem=send_sem,
                                     recv_sem=recv_sem, device_id=(x, y, 1 - c), device_id_type=MESH).wait_recv()
        cp.wait_send()

    out = pl.pallas_call(
        body,
        in_specs=[_ANY],
        out_specs=_ANY,
        out_shape=SDS((2,) + tuple(src.shape), src.dtype),
        scratch_shapes=[pltpu.SemaphoreType.DMA, pltpu.SemaphoreType.DMA],
        name=name,
    )(src)
    return lax.dynamic_update_index_in_dim(out, src, lax.axis_index("c"), axis=0)


_PACK_A = (("w_in", (1088, 1024)),)
_PACK_B = (("w_ba", (512, 128)), ("w_bh", (512, 128)), ("w_out", (128, 1024)), ("w_up", (704, 1024)),
           ("w_down", (352, 1024)))
_PACK_SIZES = _PACK_A + _PACK_B
_TRANSPOSED = ("w_in", "w_up")


def _slab_rows(sizes):
    return sum(r * c for _, (r, c) in sizes) // D_MODEL


def _pack_rows(d, sizes):
    n = d[sizes[0][0]].shape[0]
    return jnp.concatenate([d[k].reshape(n, -1, D_MODEL) for k, _ in sizes], axis=1)


def _unpack_rows(slab, sizes):
    n = slab.shape[0]
    out, lo = {}, 0
    for key, (r, c) in sizes:
        rows = r * c // D_MODEL
        out[key] = slab[:, lo:lo + rows].reshape(n, r, c)
        lo += rows
    return out


def _by_core(gslab):
    return jnp.swapaxes(gslab.reshape((4, 2) + gslab.shape[1:]), 0, 1)


def _cols_to_full(t):
    return jnp.swapaxes(t, 0, 1).reshape(t.shape[1], -1)


def _full_to_cols(t):
    K = t.shape[0]
    return jnp.swapaxes(t.reshape(K, 8, -1), 0, 1)


_SMALL = (("pre_mix_norm", (1, 1024)), ("rel_bias", (32, 24)), ("hgrn_lb_raw", (2, 512)), ("hgrn_norm", (1, 128)),
          ("post_mix_norm", (1, 1024)), ("pre_ffn_norm", (1, 1024)), ("conv_b", (1, 5632)),
          ("post_ffn_norm", (1, 1024)))
_SMALL_ROWS = 96
_CONVW_ROWS = 136


_SMALL_USED = sum(r * c for _, (r, c) in _SMALL)


def _pack_small(d, extra=None):
    flat = jnp.concatenate([d[k].reshape(-1) for k, _ in _SMALL] + ([] if extra is None else [extra.reshape(-1)]))
    flat = jnp.pad(flat, (0, _SMALL_ROWS * LANE - flat.shape[0]))
    return flat.reshape(_SMALL_ROWS, LANE)


def _unpack_small(p):
    flat = p.reshape(-1)
    out, lo = {}, 0
    for k, shp in _SMALL:
        n = shp[0] * shp[1]
        out[k] = flat[lo:lo + n].reshape(shp)
        lo += n
    return out


def _local_step(x, tgt, P, plan):
    S = x.shape[0]
    P = dict(P)
    lb = _lb_fwd(P["hgrn_lb_raw"])
    hs = _prep(x, P["pre_mix_norm"], plan.start_token())
    h1 = hs[0]
    consts = [_bias_consts(d) for d in DILATIONS]
    biases, dep = [], h1
    for g in range(N_GROUPS):
        tab_t = P["rel_bias"][:, 8 * g:8 * g + 8].T
        dep = _bias_build(tab_t, consts[g][0], consts[g][1], f"bias_build{g}", dep)
        biases.append(dep.reshape(8, ATTN_BLOCK, 2 * ATTN_BLOCK))
    W = dict(plan.weights_a(dep))
    qkv0, hg, gc = _mm_fanout(h1, [W["wt_qkv"][0], W["wt_hg"], W["wt_gate"]], "nt", [bf16, f32, bf16], "proj_natural")
    qkv = [qkv0] + [_mm(hs[g], W["wt_qkv"][g], "nt", bf16, f"proj_qkv{g}") for g in (1, 2)]
    obuf, lbuf, token = [], [], None
    for g, d in enumerate(DILATIONS):
        o_g, l_g = _attn_fwd(qkv[g], biases[g], (S // d) // ATTN_BLOCK, f"attn_fwd{g}", after=token)
        lbuf.append(l_g)
        obuf.append(o_g)
        if g == 0:
            token = plan.forward_b(o_g)
    y_attn, y_attn_b, w0, w1, w2 = _attn_merge(obuf[0], obuf[1], obuf[2], lbuf[0], lbuf[1], lbuf[2])
    y_hgrn, o_raw, ck, _ = _hgrn_fwd(hg, lb, P["hgrn_norm"])
    wb = plan.weights_b(y_hgrn)
    P["conv_w"] = wb.pop("conv_w")
    W.update(wb)
    a, b, merged = _gate_fwd(y_attn_b, y_hgrn, W["w_ba"], W["w_bh"], gc)
    mo, x1, h2 = _mid_fwd(x, merged, W["w_out"], P["post_mix_norm"], P["pre_ffn_norm"])
    ug, uv = _mm_fanout(h2, [W["wt_up_g"], W["wt_up_v"]], "nt", [bf16, bf16], "up_proj")
    cw_g, cw_v = P["conv_w"][:, :D_FF], P["conv_w"][:, D_FF:]
    cb_g, cb_v = P["conv_b"][:, :D_FF], P["conv_b"][:, D_FF:]
    act = _conv_fwd(ug, uv, cw_g, cw_v, cb_g, cb_v)
    loss, dy, dfo, g_post_ffn = _final(x1, act, W["w_down"], tgt, P["post_ffn_norm"])
    gW_down = _mm(act, dfo, "tn", bf16, "gw_down")
    dact = _mm(dfo, W["w_down"], "nt", bf16, "d_act")
    dug, duv, st_g, st_v = _conv_bwd(ug, uv, dact, cw_g, cw_v, cb_g, cb_v)
    gW_up_g = _mm(dug, h2, "tn", bf16, "gw_up_gate")
    gW_up_v = _mm(duv, h2, "tn", bf16, "gw_up_val")
    dx1, dmo, g_pre_ffn, g_post_mix = _mid_bwd(dy, dug, duv, W["wt_up_g"], W["wt_up_v"], x1, mo, P["pre_ffn_norm"],
                                               P["post_mix_norm"])
    gW_out = _mm(merged, dmo, "tn", bf16, "gw_out")
    da, db, dgc, dyattn, dyhgrn = _gate_bwd(dmo, W["w_out"], a, b, gc, W["w_ba"], W["w_bh"])
    gW_ba = _mm(y_attn_b, da, "tn", bf16, "gw_ba")
    gW_bh = _mm(y_hgrn, db, "tn", bf16, "gw_bh")
    big_b = dict(w_ba=gW_ba, w_bh=gW_bh, w_out=gW_out, w_up=[gW_up_g, gW_up_v], w_down=gW_down)
    dos = _attn_merge_bwd(dyattn, y_attn, w0, w1, w2, after=plan.grads_b_start(big_b))
    dq_h, df_h, dv_h, dog_h, glb8, gnw8, got_b = _hgrn_bwd(hg, o_raw, dyhgrn, ck, lb, P["hgrn_norm"],
                                                          plan.bwd_ride(dos[5]))
    dhg = [dq_h, df_h, dv_h, dog_h]
    g_lb_raw = _lb_bwd(P["hgrn_lb_raw"], glb8[0:1])
    gn = gnw8[0:1]
    g_hgrn_norm = (gn[:, 0:128] + gn[:, 128:256]) + (gn[:, 256:384] + gn[:, 384:512])
    dqkvs, gW_qkv, g_rel = [], [], []
    for g, d in enumerate(DILATIONS):
        dq, dk, dv, dbias = _attn_bwd(qkv[g], biases[g], dos[g], dos[3 + g], lbuf[g], (S // d) // ATTN_BLOCK,
                                      f"attn_bwd{g}")
        dqkvs.append([dq, dk, dv])
        gW_qkv.append(_mm(dqkvs[g], hs[g], "tn", bf16, f"gw_qkv{g}"))
        g_rel.append(_bias_grad(dbias.reshape(8, -1), consts[g][0], f"bias_grad{g}"))
    gW_hg = _mm(dhg, h1, "tn", bf16, "gw_hg")
    gW_gate = _mm(dgc, h1, "tn", bf16, "gw_gate")
    gW_in = gW_qkv + [gW_hg, gW_gate]
    token = plan.grads_a_start(gW_in)
    dh_perm = [_mm(dqkvs[g], W["wt_qkv"][g], "nn", f32, f"dh1_qkv{g}", after=token) for g in (1, 2)]
    token = plan.grads_a_exchange(dh_perm[1])
    dh_main = _mm(dqkvs[0] + dhg + [dgc], [W["wt_qkv"][0], W["wt_hg"], W["wt_gate"]], "nn", f32, "dh1_main",
                  after=token)
    grad_x, g_pre_mix = _first_bwd(x, dx1, _dh_sum(dh_main, dh_perm[0], dh_perm[1]), P["pre_mix_norm"])

    g_conv_w = jnp.concatenate([st_g[0:3], st_v[0:3]], axis=1)
    g_conv_b = jnp.concatenate([st_g[3:4], st_v[3:4]], axis=1)
    small = dict(pre_mix_norm=g_pre_mix, rel_bias=jnp.concatenate(g_rel, axis=1), hgrn_lb_raw=g_lb_raw,
                 hgrn_norm=g_hgrn_norm, post_mix_norm=g_post_mix, pre_ffn_norm=g_pre_ffn, conv_b=g_conv_b,
                 post_ffn_norm=g_post_ffn, conv_w=g_conv_w)
    return loss, grad_x, gW_in, big_b, got_b, small


def _weights_a(both):
    wt = jnp.swapaxes(both, 0, 1).reshape(-1, D_MODEL)
    return dict(
        wt_qkv=[wt[g * QKV_G:(g + 1) * QKV_G] for g in range(N_GROUPS)],
        wt_hg=wt[3 * QKV_G:3 * QKV_G + 4 * HGRN_W],
        wt_gate=wt[3 * QKV_G + 4 * HGRN_W:],
    )


def _weights_b(slabs):
    sh = _unpack_rows(slabs, _PACK_B)
    wt_up = sh["w_up"].reshape(-1, D_MODEL)
    return dict(
        w_ba=_cols_to_full(sh["w_ba"]),
        w_bh=_cols_to_full(sh["w_bh"]),
        w_out=sh["w_out"].reshape(D_MODEL, D_MODEL),
        wt_up_g=wt_up[:D_FF],
        wt_up_v=wt_up[D_FF:],
        w_down=sh["w_down"].reshape(D_FF, D_MODEL),
    )


def _dest_rows(sections, height):
    out = []
    for j in range(8):
        lo, hi, off, pieces = j * height, (j + 1) * height, 0, []
        for s in sections:
            a, b = max(lo, off), min(hi, off + s.shape[0])
            if a < b:
                pieces.append(s[a - off:b - off])
            off += s.shape[0]
        out.append(pieces[0] if len(pieces) == 1 else jnp.concatenate(pieces, axis=0))
    return out


def _grad_blocks_a(sections):
    rows = _dest_rows(sections, 1088)
    return jnp.stack([jnp.stack([rows[2 * k + c].astype(bf16) for k in range(4)]) for c in range(2)])


def _grad_slab_b(g):
    shards = dict(w_ba=_full_to_cols(g["w_ba"]), w_bh=_full_to_cols(g["w_bh"]), w_out=g["w_out"].reshape(8, 128, D_MODEL),
                  w_up=jnp.stack(_dest_rows(g["w_up"], 704)), w_down=g["w_down"].reshape(8, 352, D_MODEL))
    return _pack_rows({k: v.astype(bf16) for k, v in shards.items()}, _PACK_B)


_CONVW_SLAB_ROWS = 16


class _Traffic:
    def __init__(self, slab_a, slab_b, conv_w):
        hi = conv_w.astype(bf16)
        r1 = conv_w - hi.astype(f32)
        mid = r1.astype(bf16)
        lo = (r1 - mid.astype(f32)).astype(bf16)
        bits = jnp.stack([hi, mid, lo]).reshape(-1)
        tail = jnp.pad(bits, (0, _CONVW_SLAB_ROWS * D_MODEL - bits.shape[0])).reshape(_CONVW_SLAB_ROWS, D_MODEL)
        self.slab_b = jnp.concatenate([slab_b, tail], axis=0)
        self.state_a, tok = _split_start(slab_a, "chip_gather", "ag_a_start")
        self.state_b, self.token = _split_start(self.slab_b, "chip_gather", "ag_b_start", after=tok)
        self.chip_sum = None
        self.state = None

    def start_token(self):
        return self.token

    def weights_a(self, after):
        by_chip = _split_wait(self.state_a, after, "chip_gather", "ag_a_wait")
        return _weights_a(_core_gather(by_chip, "ag_a_cores"))

    def forward_b(self, after):
        by_chip = _split_wait(self.state_b, after, "chip_gather", "ag_b_wait")
        self.state, token = _split_start(by_chip, "core_gather", "ag_b_cores_start")
        return token

    def weights_b(self, after):
        both = _split_wait(self.state, after, "core_gather", "ag_b_cores_wait")
        slabs = jnp.swapaxes(both, 0, 1).reshape((8,) + tuple(self.slab_b.shape))
        rows = _slab_rows(_PACK_B)
        out = _weights_b(slabs[:, :rows])
        pieces = slabs[:, rows:].reshape(8, -1)[:, :3 * 3 * 704].reshape(8, 3, 3, 704).astype(f32)
        out["conv_w"] = _cols_to_full((pieces[:, 0] + pieces[:, 1]) + pieces[:, 2])
        return out

    def grads_b_start(self, grads):
        self.state, token = _split_start(_by_core(_grad_slab_b(grads)), "core_swap", "rs_b_cores_start")
        return token

    def bwd_ride(self, after):
        from_sib, by_core = _split_wait(self.state, after, "core_swap", "rs_b_cores_wait")
        self.chip_sum = _pair_add(by_core, from_sib, "rs_b_pair_add")
        return (self.chip_sum, False)

    def grads_a_start(self, sections):
        self.state, token = _split_start(_grad_blocks_a(sections), "core_swap", "rs_a_cores_start")
        return token

    def grads_a_exchange(self, after):
        from_sib, by_core = _split_wait(self.state, after, "core_swap", "rs_a_cores_wait")
        self.state, token = _split_start(_pair_add(by_core, from_sib, "rs_a_pair_add"), "chip_xchg", "rs_a_start")
        return token

    def parts(self, got_b, after):
        parts = _unpack_rows(_fill_own(got_b, self.chip_sum, False), _PACK_B)
        parts["w_in"] = _split_wait(self.state, after, "chip_xchg", "rs_a_wait")
        return parts


def kernel(x, pre_mix_norm, w_in, rel_bias, hgrn_lb_raw, hgrn_norm, w_branch_attn, w_branch_hgrn, w_out, post_mix_norm, pre_ffn_norm, w_up, conv_w, conv_b, w_down, post_ffn_norm, loss_target, m_pre_mix_norm, m_w_in, m_rel_bias, m_hgrn_lb_raw, m_hgrn_norm, m_w_branch_attn, m_w_branch_hgrn, m_w_out, m_post_mix_norm, m_pre_ffn_norm, m_w_up, m_conv_w, m_conv_b, m_w_down, m_post_ffn_norm, v_pre_mix_norm, v_w_in, v_rel_bias, v_hgrn_lb_raw, v_hgrn_norm, v_w_branch_attn, v_w_branch_hgrn, v_w_out, v_post_mix_norm, v_pre_ffn_norm, v_w_up, v_conv_w, v_conv_b, v_w_down, v_post_ffn_norm):
    ci = lax.axis_index("c")
    dev = 4 * lax.axis_index("x") + 2 * lax.axis_index("y") + ci
    tr = lambda t: jnp.swapaxes(t[0], 0, 1)
    wts = dict(w_in=tr(w_in), w_ba=w_branch_attn[0], w_bh=w_branch_hgrn[0], w_out=w_out[0], w_up=tr(w_up),
               w_down=w_down[0])
    mom = dict(w_in=tr(m_w_in), w_ba=m_w_branch_attn[0], w_bh=m_w_branch_hgrn[0], w_out=m_w_out[0], w_up=tr(m_w_up),
               w_down=m_w_down[0])
    var = dict(w_in=tr(v_w_in), w_ba=v_w_branch_attn[0], w_bh=v_w_branch_hgrn[0], w_out=v_w_out[0], w_up=tr(v_w_up),
               w_down=v_w_down[0])
    small_w = dict(pre_mix_norm=pre_mix_norm, rel_bias=rel_bias, hgrn_lb_raw=hgrn_lb_raw, hgrn_norm=hgrn_norm,
                   post_mix_norm=post_mix_norm, pre_ffn_norm=pre_ffn_norm, conv_b=conv_b, post_ffn_norm=post_ffn_norm)
    small_m = dict(pre_mix_norm=m_pre_mix_norm, rel_bias=m_rel_bias, hgrn_lb_raw=m_hgrn_lb_raw, hgrn_norm=m_hgrn_norm,
                   post_mix_norm=m_post_mix_norm, pre_ffn_norm=m_pre_ffn_norm, conv_b=m_conv_b,
                   post_ffn_norm=m_post_ffn_norm)
    small_v = dict(pre_mix_norm=v_pre_mix_norm, rel_bias=v_rel_bias, hgrn_lb_raw=v_hgrn_lb_raw, hgrn_norm=v_hgrn_norm,
                   post_mix_norm=v_post_mix_norm, pre_ffn_norm=v_pre_ffn_norm, conv_b=v_conv_b,
                   post_ffn_norm=v_post_ffn_norm)

    plan = _Traffic(wts["w_in"].astype(bf16),
                    _pack_rows({k: wts[k].astype(bf16)[None] for k, _ in _PACK_B}, _PACK_B)[0], conv_w[0])

    loss8, grad_x, _, _, got_b, small = _local_step(x[0], loss_target[0], small_w, plan)
    spack = jnp.concatenate([_pack_small(small, loss8[0, 0:1]),
                             jnp.pad(small["conv_w"].reshape(-1, LANE), ((0, _CONVW_ROWS - 132), (0, 0)))], axis=0)
    small_state, token = _split_start(spack, "chip_gather", "ag_small_start")

    parts = plan.parts(got_b, token)
    outs_big = {}
    for k, _ in _PACK_SIZES:
        outs_big[k] = _adamw(wts[k], mom[k], var[k], parts[k], "adamw_" + k)

    by_chip = _split_wait(small_state, outs_big["w_in"][1], "chip_gather", "ag_small_wait")
    allp = _core_gather(by_chip, "ag_small_cores")
    ssum = _sum8(allp, "small_sum")
    gs = ssum[:_SMALL_ROWS]
    loss = ssum[_SMALL_USED // LANE, _SMALL_USED % LANE]
    res_small = _adamw(_pack_small(small_w), _pack_small(small_m), _pack_small(small_v), gs, "adamw_small")
    sm = [_unpack_small(t) for t in res_small]
    g_cw_full = ssum[_SMALL_ROWS:_SMALL_ROWS + 132].reshape(3, 2 * D_FF)
    g_cw = lax.dynamic_slice_in_dim(g_cw_full, dev * 704, 704, axis=1)
    res_cw = _adamw(conv_w[0], m_conv_w[0], v_conv_w[0], g_cw, "adamw_conv_w")

    def pick(i):
        def big_(k):
            t = outs_big[k][i]
            return (jnp.swapaxes(t, 0, 1) if k in _TRANSPOSED else t)[None]
        return [sm[i]["pre_mix_norm"], big_("w_in"), sm[i]["rel_bias"], sm[i]["hgrn_lb_raw"], sm[i]["hgrn_norm"],
                big_("w_ba"), big_("w_bh"), big_("w_out"), sm[i]["post_mix_norm"], sm[i]["pre_ffn_norm"],
                big_("w_up"), res_cw[i][None], sm[i]["conv_b"], big_("w_down"), sm[i]["post_ffn_norm"]]

    return (loss, grad_x[None], *pick(0), *pick(1), *pick(2), *pick(3))
```

```python
import functools
import math

import jax
import jax.numpy as jnp
from jax import lax
from jax.experimental import pallas as pl
from jax.experimental.pallas import tpu as pltpu

f32 = jnp.float32
bf16 = jnp.bfloat16
SDS = jax.ShapeDtypeStruct
HIGHEST = lax.Precision.HIGHEST
MESH = pl.DeviceIdType.MESH

NN = (((1,), (0,)), ((), ()))
NT = (((1,), (1,)), ((), ()))
TN = (((0,), (0,)), ((), ()))

D_MODEL = 1024
N_GROUPS = 3
DILATIONS = (1, 4, 16)
HEAD_DIM = 64
ATTN_BLOCK = 128
QKV_G = 1536
ATTN_OUT = 512
HGRN_W = 512
HGRN_CHUNK = 32
D_FF = 2816
NUM_BUCKETS = 32
MAX_EXACT = 16
MAX_DISTANCE = 2048
NEG_INF = -1e30
EPS = 1e-6
LANE = 128
SUBLANE = 8
VMEM_BIG = 48 * 1024 * 1024
MM_ROWS = 512
MM_OUT_BYTES = 8 * 1024 * 1024
ADAM_BLOCK_BYTES = 2304 * 1024

ADAM_LR, ADAM_B1, ADAM_B2, ADAM_EPS, ADAM_WD, ADAM_STEP = 0.001, 0.9, 0.999, 1e-08, 0.01, 10


def _pick(n, pref):
    t = pref
    while t >= LANE:
        if n % t == 0:
            return t
        t //= 2
    return n


def _cparams(sem=None, vmem=None):
    kw = {}
    if sem is not None:
        kw["dimension_semantics"] = sem
    if vmem is not None:
        kw["vmem_limit_bytes"] = vmem
    return pltpu.CompilerParams(**kw)


def _sigmoid(x):
    return jax.nn.sigmoid(x)


def _colsum8(x):
    return x.reshape(x.shape[0] // SUBLANE, SUBLANE, x.shape[1]).sum(axis=0)


def _mm(a, b, mode, out_dtype, name, acc=None, after=None):
    dims = {"nn": NN, "nt": NT, "tn": TN}[mode]
    has_acc = acc is not None
    parts = list(a) if isinstance(a, (list, tuple)) else [a]
    if mode == "tn":
        assert not has_acc
        K, N = b.shape
        widths = [t.shape[1] for t in parts]
        M = sum(widths)
        whole = M * N * 4 <= MM_OUT_BYTES
        assert whole or len(parts) == 1
        tmm = M if whole else M // 2
        ts = _pick(K, 4 * MM_ROWS)
        nk = K // ts

        npart = len(parts)
        narrow = out_dtype != f32

        def body_tn(*refs):
            b_ref, o_ref = refs[npart], refs[npart + 1]
            acc_ref = refs[npart + 2] if narrow else o_ref
            k = pl.program_id(1)
            bv = b_ref[...]
            lo = 0
            for a_ref, w in zip(refs[:npart], widths if whole else [tmm]):
                part = lax.dot_general(a_ref[...], bv, dims, preferred_element_type=f32)
                rows = slice(lo, lo + w)
                lo += w

                @pl.when(k == 0)
                def _(part=part, rows=rows):
                    acc_ref[rows, :] = part

                @pl.when(k > 0)
                def _(part=part, rows=rows):
                    acc_ref[rows, :] += part

            if narrow:
                @pl.when(k == nk - 1)
                def _():
                    o_ref[...] = acc_ref[...].astype(out_dtype)

        return pl.pallas_call(
            body_tn,
            grid=(M // tmm, nk),
            in_specs=[pl.BlockSpec((ts, w if whole else tmm), lambda i, k: (k, i)) for w in widths]
            + [pl.BlockSpec((ts, N), lambda i, k: (k, 0))],
            out_specs=pl.BlockSpec((tmm, N), lambda i, k: (i, 0)),
            out_shape=SDS((M, N), out_dtype),
            scratch_shapes=[pltpu.VMEM((tmm, N), f32)] if narrow else [],
            compiler_params=_cparams(("parallel", "arbitrary"), VMEM_BIG),
            name=name,
        )(*parts, b)

    bs = list(b) if isinstance(b, (list, tuple)) else [b]
    widths = [t.shape[1] for t in parts]
    M = parts[0].shape[0]
    kdim = 0 if mode == "nn" else 1
    N = bs[0].shape[1 - kdim]
    tm = _pick(M, MM_ROWS)
    npart, nb = len(parts), len(bs)
    place, bi, lo = [], 0, 0
    for w in widths:
        place.append((bi, lo))
        lo += w
        if lo == bs[bi].shape[kdim]:
            bi, lo = bi + 1, 0
    assert bi == nb and lo == 0

    def body(*refs):
        a_refs, b_refs = refs[:npart], refs[npart:npart + nb]
        c_ref = refs[npart + nb] if has_acc else None
        o_ref = refs[-1]
        part = None
        for a_ref, w, (bi, lo) in zip(a_refs, widths, place):
            b_ref = b_refs[bi]
            if w == bs[bi].shape[kdim]:
                bk = b_ref[...]
            else:
                bk = b_ref[:, lo:lo + w] if mode == "nt" else b_ref[lo:lo + w, :]
            t = lax.dot_general(a_ref[...], bk, dims, preferred_element_type=f32)
            part = t if part is None else part + t
        if has_acc:
            part = part + c_ref[...]
        o_ref[...] = part.astype(out_dtype)

    specs = [pl.BlockSpec((tm, w), lambda i: (i, 0)) for w in widths] \
        + [pl.BlockSpec(t.shape, lambda i: (0, 0)) for t in bs]
    args = parts + bs
    aliases = {}
    if has_acc:
        specs.append(pl.BlockSpec((tm, N), lambda i: (i, 0)))
        args.append(acc)
        aliases = {npart + nb: 0}
    if after is not None:
        specs.append(pl.BlockSpec(memory_space=pl.ANY))
        args.append(after)
    return pl.pallas_call(
        body,
        grid=(M // tm,),
        in_specs=specs,
        out_specs=pl.BlockSpec((tm, N), lambda i: (i, 0)),
        out_shape=SDS((M, N), out_dtype),
        input_output_aliases=aliases,
        compiler_params=_cparams(("parallel",), VMEM_BIG),
        name=name,
    )(*args)


def _mm_fanout(a, bs, mode, out_dtypes, name):
    dims = {"nn": NN, "nt": NT}[mode]
    M, K = a.shape
    ns = [b.shape[1] if mode == "nn" else b.shape[0] for b in bs]
    tm = _pick(M, MM_ROWS)
    nb = len(bs)

    def body(a_ref, *refs):
        av = a_ref[...]
        for b_ref, o_ref, dt in zip(refs[:nb], refs[nb:], out_dtypes):
            o_ref[...] = lax.dot_general(av, b_ref[...], dims, preferred_element_type=f32).astype(dt)

    return pl.pallas_call(
        body,
        grid=(M // tm,),
        in_specs=[pl.BlockSpec((tm, K), lambda i: (i, 0))] + [pl.BlockSpec(b.shape, lambda i: (0, 0)) for b in bs],
        out_specs=[pl.BlockSpec((tm, n), lambda i: (i, 0)) for n in ns],
        out_shape=[SDS((M, n), dt) for n, dt in zip(ns, out_dtypes)],
        compiler_params=_cparams(("parallel",), VMEM_BIG),
        name=name,
    )(a, *bs)


PERM_ROWS = 2048


def _perm_spec(d, cols=LANE):
    return pl.BlockSpec((d, PERM_ROWS // d, cols), lambda i, j: (0, i, j))


def _to_natural(src_ref, dst_ref, d):
    n = src_ref.shape[1]
    for r in range(d):
        dst_ref[pl.ds(r, n, stride=d), :] = src_ref[r]


def _prep(x, w, after=None):
    S, D = x.shape
    R = PERM_ROWS
    nc = D // LANE
    n_in = nc + 1 + (after is not None)

    def body(*refs):
        x_refs, w_ref = refs[:nc], refs[nc]
        h_ref, h4_ref, h16_ref, rs = refs[n_in:]
        ssq = None
        for xr in x_refs:
            v = xr[...]
            t = jnp.sum(v * v, axis=-1, keepdims=True)
            ssq = t if ssq is None else ssq + t
        rinv = lax.rsqrt(ssq * (1.0 / D) + EPS)
        rs[...] = jnp.broadcast_to(rinv, (R, LANE))
        for j, xr in enumerate(x_refs):
            cols = slice(j * LANE, (j + 1) * LANE)
            wj = w_ref[:, cols]
            h_ref[:, cols] = ((xr[...] * rinv) * wj).astype(bf16)
            for d, o_ref in ((4, h4_ref), (16, h16_ref)):
                n = R // d
                for r in range(d):
                    rows = pl.ds(r, n, stride=d)
                    o_ref[r, :, cols] = ((xr[rows, :] * rs[rows, :]) * wj).astype(bf16)

    col = lambda j: pl.BlockSpec((R, LANE), lambda i, j=j: (i, j))
    h, h4, h16 = pl.pallas_call(
        body,
        grid=(S // R,),
        in_specs=[col(j) for j in range(nc)] + [pl.BlockSpec((1, D), lambda i: (0, 0))]
        + ([] if after is None else [pl.BlockSpec(memory_space=pl.ANY)]),
        out_specs=[pl.BlockSpec((R, D), lambda i: (i, 0)), pl.BlockSpec((4, R // 4, D), lambda i: (0, i, 0)),
                   pl.BlockSpec((16, R // 16, D), lambda i: (0, i, 0))],
        out_shape=[SDS((S, D), bf16), SDS((4, S // 4, D), bf16), SDS((16, S // 16, D), bf16)],
        scratch_shapes=[pltpu.VMEM((R, LANE), f32)],
        compiler_params=_cparams(("parallel",), VMEM_BIG),
        name="prep_norm_perm",
    )(*([x] * nc), w, *([] if after is None else [after]))
    return [h, h4.reshape(S, D), h16.reshape(S, D)]


def _dh_sum(a, b, c):
    S, D = a.shape
    R = PERM_ROWS

    def body(a_ref, b_ref, c_ref, o_ref, sb, sc):
        _to_natural(b_ref, sb, 4)
        _to_natural(c_ref, sc, 16)
        o_ref[...] = (a_ref[...] + sb[...]) + sc[...]

    nat = pl.BlockSpec((R, LANE), lambda i, j: (i, j))
    return pl.pallas_call(
        body,
        grid=(S // R, D // LANE),
        in_specs=[nat, _perm_spec(4), _perm_spec(16)],
        out_specs=nat,
        out_shape=SDS((S, D), f32),
        scratch_shapes=[pltpu.VMEM((R, LANE), f32)] * 2,
        compiler_params=_cparams(("parallel", "parallel"), VMEM_BIG),
        name="dh_sum",
    )(a, b.reshape(4, S // 4, D), c.reshape(16, S // 16, D))


def _rms_parts(xv):
    r = lax.rsqrt(jnp.mean(xv * xv, axis=-1, keepdims=True) + EPS)
    return r, xv * r


def _rms_bwd(xhat, r, w, dy):
    dyw = dy * w
    return r * (dyw - xhat * jnp.mean(dyw * xhat, axis=-1, keepdims=True))


def _mid_fwd(x, merged, w_out, w_pm, w_pf):
    S, D = x.shape
    tm = _pick(S, MM_ROWS)

    def body(x_ref, m_ref, wo_ref, wpm_ref, wpf_ref, mo_ref, x1_ref, h2_ref):
        mo = jnp.dot(m_ref[...], wo_ref[...], preferred_element_type=f32)
        mo_ref[...] = mo
        _, moh = _rms_parts(mo)
        x1 = x_ref[...] + moh * wpm_ref[...]
        x1_ref[...] = x1
        _, x1h = _rms_parts(x1)
        h2_ref[...] = (x1h * wpf_ref[...]).astype(bf16)

    row = pl.BlockSpec((tm, D), lambda i: (i, 0))
    vec = pl.BlockSpec((1, D), lambda i: (0, 0))
    return pl.pallas_call(
        body,
        grid=(S // tm,),
        in_specs=[row, pl.BlockSpec((tm, merged.shape[1]), lambda i: (i, 0)),
                  pl.BlockSpec(w_out.shape, lambda i: (0, 0)), vec, vec],
        out_specs=[row, row, row],
        out_shape=[SDS((S, D), f32), SDS((S, D), f32), SDS((S, D), bf16)],
        compiler_params=_cparams(("parallel",), VMEM_BIG),
        name="out_proj_mid_fwd",
    )(x, merged, w_out, w_pm, w_pf)


def _final(x1, act, w_down, tgt, w_pfn):
    S, D = x1.shape
    tm = _pick(S, MM_ROWS)
    nt = S // tm

    def body(x1_ref, a_ref, wd_ref, t_ref, w_ref, loss_ref, dy_ref, dfo_ref, gw_ref, lacc, gacc):
        i = pl.program_id(0)

        @pl.when(i == 0)
        def _():
            lacc[...] = jnp.zeros_like(lacc)
            gacc[...] = jnp.zeros_like(gacc)

        w = w_ref[...]
        r, foh = _rms_parts(jnp.dot(a_ref[...], wd_ref[...], preferred_element_type=f32))
        y = x1_ref[...] + foh * w
        err = y - t_ref[...]
        lacc[...] += _colsum8(err * err)
        dy = err * (1.0 / D)
        dy_ref[...] = dy
        gacc[...] += _colsum8(dy * foh)
        dfo_ref[...] = _rms_bwd(foh, r, w, dy).astype(bf16)

        @pl.when(i == nt - 1)
        def _():
            loss_ref[...] = jnp.full((SUBLANE, LANE), 0.5 / D, f32) * jnp.sum(lacc[...])
            gw_ref[...] = jnp.sum(gacc[...], axis=0, keepdims=True)

    row = pl.BlockSpec((tm, D), lambda i: (i, 0))
    vec = pl.BlockSpec((1, D), lambda i: (0, 0))
    return pl.pallas_call(
        body,
        grid=(nt,),
        in_specs=[row, pl.BlockSpec((tm, act.shape[1]), lambda i: (i, 0)),
                  pl.BlockSpec(w_down.shape, lambda i: (0, 0)), row, vec],
        out_specs=[pl.BlockSpec((SUBLANE, LANE), lambda i: (0, 0)), row, row, vec],
        out_shape=[SDS((SUBLANE, LANE), f32), SDS((S, D), f32), SDS((S, D), bf16), SDS((1, D), f32)],
        scratch_shapes=[pltpu.VMEM((SUBLANE, D), f32), pltpu.VMEM((SUBLANE, D), f32)],
        compiler_params=_cparams(("arbitrary",), VMEM_BIG),
        name="down_proj_final_loss",
    )(x1, act, w_down, tgt, w_pfn)


MID_BWD_ROWS = 256


def _mid_bwd(dy, dug, duv, wt_g, wt_v, x1, mo, w_pf, w_pm):
    S, D = dy.shape
    tm = _pick(S, MID_BWD_ROWS)
    nt = S // tm

    def body(dy_ref, dug_ref, duv_ref, wg_ref, wv_ref, x1_ref, mo_ref, wpf_ref, wpm_ref,
             dx1_ref, dmo_ref, gpf_ref, gpm_ref, apf, apm):
        i = pl.program_id(0)

        @pl.when(i == 0)
        def _():
            apf[...] = jnp.zeros_like(apf)
            apm[...] = jnp.zeros_like(apm)

        r1, x1h = _rms_parts(x1_ref[...])
        dh2 = jnp.dot(dug_ref[...], wg_ref[...], preferred_element_type=f32) \
            + jnp.dot(duv_ref[...], wv_ref[...], preferred_element_type=f32)
        apf[...] += _colsum8(dh2 * x1h)
        dx1 = dy_ref[...] + _rms_bwd(x1h, r1, wpf_ref[...], dh2)
        dx1_ref[...] = dx1
        rm, moh = _rms_parts(mo_ref[...])
        apm[...] += _colsum8(dx1 * moh)
        dmo_ref[...] = _rms_bwd(moh, rm, wpm_ref[...], dx1).astype(bf16)

        @pl.when(i == nt - 1)
        def _():
            gpf_ref[...] = jnp.sum(apf[...], axis=0, keepdims=True)
            gpm_ref[...] = jnp.sum(apm[...], axis=0, keepdims=True)

    row = pl.BlockSpec((tm, D), lambda i: (i, 0))
    vec = pl.BlockSpec((1, D), lambda i: (0, 0))
    return pl.pallas_call(
        body,
        grid=(nt,),
        in_specs=[row, pl.BlockSpec((tm, dug.shape[1]), lambda i: (i, 0)), pl.BlockSpec((tm, duv.shape[1]), lambda i: (i, 0)),
                  pl.BlockSpec(wt_g.shape, lambda i: (0, 0)), pl.BlockSpec(wt_v.shape, lambda i: (0, 0)),
                  row, row, vec, vec],
        out_specs=[row, row, vec, vec],
        out_shape=[SDS((S, D), f32), SDS((S, D), bf16), SDS((1, D), f32), SDS((1, D), f32)],
        scratch_shapes=[pltpu.VMEM((SUBLANE, D), f32), pltpu.VMEM((SUBLANE, D), f32)],
        compiler_params=_cparams(("arbitrary",), VMEM_BIG),
        name="dh2_mid_bwd",
    )(dy, dug, duv, wt_g, wt_v, x1, mo, w_pf, w_pm)


def _first_bwd(x, dx1, dh, w_pre):
    S, D = x.shape
    tm = _pick(S, 512)
    nt = S // tm

    def body(x_ref, dx1_ref, a_ref, w_ref, gx_ref, gw_ref, acc):
        i = pl.program_id(0)

        @pl.when(i == 0)
        def _():
            acc[...] = jnp.zeros_like(acc)

        r, xh = _rms_parts(x_ref[...])
        dh = a_ref[...]
        acc[...] += _colsum8(dh * xh)
        gx_ref[...] = dx1_ref[...] + _rms_bwd(xh, r, w_ref[...], dh)

        @pl.when(i == nt - 1)
        def _():
            gw_ref[...] = jnp.sum(acc[...], axis=0, keepdims=True)

    row = pl.BlockSpec((tm, D), lambda i: (i, 0))
    vec = pl.BlockSpec((1, D), lambda i: (0, 0))
    return pl.pallas_call(
        body,
        grid=(nt,),
        in_specs=[row, row, row, vec],
        out_specs=[row, vec],
        out_shape=[SDS((S, D), f32), SDS((1, D), f32)],
        scratch_shapes=[pltpu.VMEM((SUBLANE, D), f32)],
        compiler_params=_cparams(("arbitrary",)),
        name="first_bwd",
    )(x, dx1, dh, w_pre)


def _t5_bucket(dist):
    n = jnp.maximum(dist, 0)
    nf = jnp.maximum(n, 1).astype(f32)
    large = MAX_EXACT + (jnp.log(nf / MAX_EXACT) / math.log(MAX_DISTANCE / MAX_EXACT)
                         * (NUM_BUCKETS - MAX_EXACT)).astype(jnp.int32)
    large = jnp.minimum(large, NUM_BUCKETS - 1)
    return jnp.where(n < MAX_EXACT, n, large)


def _bias_consts(d):
    blk = ATTN_BLOCK
    rel = jnp.arange(blk)[:, None] + blk - jnp.arange(2 * blk)[None, :]
    in_win = (rel >= 0) & (rel <= blk)
    bucket = _t5_bucket(rel * d).reshape(1, -1)
    onehot = (bucket == jnp.arange(NUM_BUCKETS)[:, None]).astype(f32)
    return onehot, in_win.astype(f32).reshape(1, -1)


def _bias_build(tab_t, onehot, maskf, name, after):
    H = tab_t.shape[0]

    def body(t_ref, oh_ref, m_ref, after_ref, o_ref):
        b = jnp.dot(t_ref[...], oh_ref[...], precision=HIGHEST, preferred_element_type=f32)
        o_ref[...] = jnp.where(m_ref[...] > 0.5, b, NEG_INF)

    vm = pl.BlockSpec(memory_space=pltpu.VMEM)
    return pl.pallas_call(body, out_shape=SDS((H, onehot.shape[1]), f32), name=name,
                          in_specs=[vm, vm, vm, pl.BlockSpec(memory_space=pl.ANY)], out_specs=vm,
                          )(tab_t, onehot, maskf, after)


def _bias_grad(dbias_flat, onehot, name):
    H = dbias_flat.shape[0]

    def body(g_ref, oh_ref, o_ref):
        o_ref[...] = lax.dot_general(oh_ref[...], g_ref[...], NT, precision=HIGHEST, preferred_element_type=f32)

    return pl.pallas_call(body, out_shape=SDS((NUM_BUCKETS, H), f32), name=name)(dbias_flat, onehot)


ATTN_TILE = 512
ATTN_SUB = ATTN_TILE // ATTN_BLOCK
ATTN_HP = 4
ATTN_WIDE = ATTN_HP * LANE


def _qkv_specs(nt):
    tile = (ATTN_TILE, ATTN_WIDE)
    blk = (ATTN_BLOCK, ATTN_WIDE)
    sec = ATTN_OUT // ATTN_WIDE
    cur = lambda off: (lambda h, t: (jnp.minimum(t, nt - 1), off + h))
    prev = lambda off: (lambda h, t: (jnp.maximum(jnp.minimum(t, nt - 1) * ATTN_SUB - 1, 0), off + h))
    return [pl.BlockSpec(tile, cur(0)), pl.BlockSpec(blk, prev(sec)), pl.BlockSpec(tile, cur(sec)),
            pl.BlockSpec(blk, prev(2 * sec)), pl.BlockSpec(tile, cur(2 * sec))]


def _head_masks():
    lane = lax.broadcasted_iota(jnp.int32, (ATTN_BLOCK, LANE), 1)
    return lane < HEAD_DIM


def _stack_heads(x2, low):
    zero = jnp.zeros_like(x2)
    return jnp.concatenate([jnp.where(low, x2, zero), jnp.where(low, zero, x2)], axis=0)


def _attn_fwd(qkv, bias, bps, name, after=None):
    S = qkv.shape[0]
    nt = S // ATTN_TILE
    scale = HEAD_DIM ** -0.5

    def body(q_ref, kp_ref, kc_ref, vp_ref, vc_ref, b_ref, *rest):
        o_ref, l_ref = rest[-2:]
        t = pl.program_id(1)
        low = _head_masks()
        col = lax.broadcasted_iota(jnp.int32, (2 * ATTN_BLOCK, 2 * ATTN_BLOCK), 1)
        for hp in range(ATTN_HP):
            cols = slice(hp * LANE, (hp + 1) * LANE)
            kk = jnp.concatenate([kp_ref[:, cols], kc_ref[:, cols]], axis=0)
            vv = jnp.concatenate([vp_ref[:, cols], vc_ref[:, cols]], axis=0)
            bias2 = b_ref[2 * hp:2 * hp + 2].reshape(2 * ATTN_BLOCK, 2 * ATTN_BLOCK)
            for b in range(ATTN_SUB):
                lo = b * ATTN_BLOCK
                rows = slice(lo, lo + ATTN_BLOCK)
                keys = slice(lo, lo + 2 * ATTN_BLOCK)
                dead = jnp.logical_and((t * ATTN_SUB + b) % bps == 0, col < ATTN_BLOCK)
                q2 = _stack_heads(q_ref[rows, cols], low)
                kb, vb = kk[keys], vv[keys]
                s = lax.dot_general(q2, kb, NT, preferred_element_type=f32) * scale + bias2
                s = jnp.where(dead, NEG_INF, s)
                m = jnp.max(s, axis=-1, keepdims=True)
                p = jnp.exp(s - m)
                l = jnp.sum(p, axis=-1, keepdims=True)
                o2 = jnp.dot(p.astype(bf16), vb, preferred_element_type=f32) / l
                lse = m + jnp.log(l)
                o_ref[rows, cols] = jnp.where(low, o2[:ATTN_BLOCK], o2[ATTN_BLOCK:])
                l_ref[rows, cols] = jnp.where(low, lse[:ATTN_BLOCK], lse[ATTN_BLOCK:])

    tile = pl.BlockSpec((ATTN_TILE, ATTN_WIDE), lambda h, t: (t, h))
    return pl.pallas_call(
        body,
        grid=(4 // ATTN_HP, nt),
        in_specs=_qkv_specs(nt) + [pl.BlockSpec((2 * ATTN_HP, ATTN_BLOCK, 2 * ATTN_BLOCK), lambda h, t: (h, 0, 0))]
        + ([] if after is None else [pl.BlockSpec(memory_space=pl.ANY)]),
        out_specs=[tile, tile],
        out_shape=[SDS((S, ATTN_OUT), f32), SDS((S, ATTN_OUT), f32)],
        compiler_params=_cparams(("parallel", "parallel")),
        name=name,
    )(qkv, qkv, qkv, qkv, qkv, bias, *([] if after is None else [after]))


def _attn_bwd(qkv, bias, do, dvec, lse, bps, name):
    S = qkv.shape[0]
    nt = S // ATTN_TILE
    scale = HEAD_DIM ** -0.5

    def assemble(parts):
        rows = [parts[0][:ATTN_BLOCK]]
        for b in range(ATTN_SUB - 1):
            rows.append(parts[b][ATTN_BLOCK:] + parts[b + 1][:ATTN_BLOCK])
        rows.append(parts[-1][ATTN_BLOCK:])
        return rows

    def body(q_ref, kp_ref, kc_ref, vp_ref, vc_ref, b_ref, do_ref, dvec_ref, lse_ref,
             dq_ref, dk_ref, dv_ref, db_ref, ck, cv):
        t = pl.program_id(1)
        last = ATTN_TILE - ATTN_BLOCK

        @pl.when(t == 0)
        def _():
            ck[...] = jnp.zeros_like(ck)
            cv[...] = jnp.zeros_like(cv)
            db_ref[...] = jnp.zeros_like(db_ref)

        @pl.when(t < nt)
        def _():
            low = _head_masks()
            col = lax.broadcasted_iota(jnp.int32, (2 * ATTN_BLOCK, 2 * ATTN_BLOCK), 1)
            per_row = lambda t2: jnp.concatenate([t2[:, 0:1], t2[:, HEAD_DIM:HEAD_DIM + 1]], axis=0)
            for hp in range(ATTN_HP):
                cols = slice(hp * LANE, (hp + 1) * LANE)
                kk = jnp.concatenate([kp_ref[:, cols], kc_ref[:, cols]], axis=0)
                vv = jnp.concatenate([vp_ref[:, cols], vc_ref[:, cols]], axis=0)
                bias2 = b_ref[2 * hp:2 * hp + 2].reshape(2 * ATTN_BLOCK, 2 * ATTN_BLOCK)
                dk_parts, dv_parts = [], []
                dsum = None
                for b in range(ATTN_SUB):
                    lo = b * ATTN_BLOCK
                    rows = slice(lo, lo + ATTN_BLOCK)
                    keys = slice(lo, lo + 2 * ATTN_BLOCK)
                    dead = jnp.logical_and((t * ATTN_SUB + b) % bps == 0, col < ATTN_BLOCK)
                    q2 = _stack_heads(q_ref[rows, cols], low)
                    do2 = _stack_heads(do_ref[rows, cols].astype(bf16), low)
                    kb, vb = kk[keys], vv[keys]
                    s = lax.dot_general(q2, kb, NT, preferred_element_type=f32) * scale + bias2
                    s = jnp.where(dead, NEG_INF, s)
                    p = jnp.exp(s - per_row(lse_ref[rows, cols]))
                    dp = lax.dot_general(do2, vb, NT, preferred_element_type=f32)
                    ds = p * (dp - per_row(dvec_ref[rows, cols]))
                    dsum = ds if dsum is None else dsum + ds
                    dsb = ds.astype(bf16)
                    dq2 = jnp.dot(dsb, kb, preferred_element_type=f32) * scale
                    dq_ref[rows, cols] = jnp.where(low, dq2[:ATTN_BLOCK], dq2[ATTN_BLOCK:]).astype(bf16)
                    dk_parts.append(lax.dot_general(dsb, q2, TN, preferred_element_type=f32) * scale)
                    dv_parts.append(lax.dot_general(p.astype(bf16), do2, TN, preferred_element_type=f32))
                db_ref[2 * hp:2 * hp + 2] += dsum.reshape(2, ATTN_BLOCK, 2 * ATTN_BLOCK)
                for parts, carry, out_ref in ((dk_parts, ck, dk_ref), (dv_parts, cv, dv_ref)):
                    rws = assemble(parts)
                    out_ref[:last, cols] = carry[:last, cols].astype(bf16)
                    out_ref[last:, cols] = (carry[last:, cols] + rws[0]).astype(bf16)
                    for b in range(ATTN_SUB):
                        carry[b * ATTN_BLOCK:(b + 1) * ATTN_BLOCK, cols] = rws[b + 1]

        @pl.when(t == nt)
        def _():
            dk_ref[...] = ck[...].astype(bf16)
            dv_ref[...] = cv[...].astype(bf16)

    tile = (ATTN_TILE, ATTN_WIDE)
    cur = pl.BlockSpec(tile, lambda h, t: (jnp.minimum(t, nt - 1), h))
    lag = pl.BlockSpec(tile, lambda h, t: (jnp.maximum(t - 1, 0), h))
    bspec = pl.BlockSpec((2 * ATTN_HP, ATTN_BLOCK, 2 * ATTN_BLOCK), lambda h, t: (h, 0, 0))
    return pl.pallas_call(
        body,
        grid=(4 // ATTN_HP, nt + 1),
        in_specs=_qkv_specs(nt) + [bspec, cur, cur, cur],
        out_specs=[cur, lag, lag, bspec],
        out_shape=[SDS((S, ATTN_OUT), bf16), SDS((S, ATTN_OUT), bf16), SDS((S, ATTN_OUT), bf16),
                   SDS((8, ATTN_BLOCK, 2 * ATTN_BLOCK), f32)],
        scratch_shapes=[pltpu.VMEM(tile, f32), pltpu.VMEM(tile, f32)],
        compiler_params=_cparams(("parallel", "arbitrary")),
        name=name,
    )(qkv, qkv, qkv, qkv, qkv, bias, do, dvec, lse)


def _attn_merge(o0, o1, o2, l0, l1, l2):
    S, W = o0.shape
    R = PERM_ROWS

    def body(o0_ref, o1_ref, o2_ref, l0_ref, l1_ref, l2_ref, y_ref, yb_ref, w0_ref, w1_ref, w2_ref,
             so1, so2, sl1, sl2):
        _to_natural(o1_ref, so1, 4)
        _to_natural(l1_ref, sl1, 4)
        _to_natural(o2_ref, so2, 16)
        _to_natural(l2_ref, sl2, 16)
        a, b, c = l0_ref[...], sl1[...], sl2[...]
        m = jnp.maximum(jnp.maximum(a, b), c)
        ea, eb, ec = jnp.exp(a - m), jnp.exp(b - m), jnp.exp(c - m)
        den = (ea + eb) + ec
        w0, w1, w2 = ea / den, eb / den, ec / den
        y = (w0 * o0_ref[...] + w1 * so1[...]) + w2 * so2[...]
        y_ref[...] = y
        yb_ref[...] = y.astype(bf16)
        w0_ref[...] = w0
        w1_ref[...] = w1
        w2_ref[...] = w2

    nat = pl.BlockSpec((R, LANE), lambda i, j: (i, j))
    v4 = lambda t: t.reshape(4, S // 4, W)
    v16 = lambda t: t.reshape(16, S // 16, W)
    return pl.pallas_call(
        body,
        grid=(S // R, W // LANE),
        in_specs=[nat, _perm_spec(4), _perm_spec(16)] * 2,
        out_specs=[nat] * 5,
        out_shape=[SDS((S, W), f32), SDS((S, W), bf16)] + [SDS((S, W), f32)] * 3,
        scratch_shapes=[pltpu.VMEM((R, LANE), f32)] * 4,
        compiler_params=_cparams(("parallel", "parallel"), VMEM_BIG),
        name="attn_merge",
    )(o0, v4(o1), v16(o2), l0, v4(l1), v16(l2))


def _attn_merge_bwd(dy, y, w0, w1, w2, after=None):
    S, W = dy.shape
    R = PERM_ROWS

    def body(dy_ref, y_ref, w0_ref, w1_ref, w2_ref, *rest):
        a0, a1, a2, b0, b1, b2, sa, sb = rest[-8:]
        dyv = dy_ref[...]
        r = lax.broadcasted_iota(jnp.int32, (LANE, LANE), 0) // HEAD_DIM
        c = lax.broadcasted_iota(jnp.int32, (LANE, LANE), 1) // HEAD_DIM
        seg = jnp.where(r == c, 1.0, 0.0).astype(f32)
        cbar = jnp.dot(dyv * y_ref[...], seg, precision=HIGHEST, preferred_element_type=f32)
        w = w0_ref[...]
        a0[...] = (w * dyv).astype(bf16)
        b0[...] = w * cbar
        for d, w_ref, a_ref, b_ref in ((4, w1_ref, a1, b1), (16, w2_ref, a2, b2)):
            w = w_ref[...]
            sa[...] = w * dyv
            sb[...] = w * cbar
            n = R // d
            for k in range(d):
                rows = pl.ds(k, n, stride=d)
                a_ref[k] = sa[rows, :].astype(bf16)
                b_ref[k] = sb[rows, :]

    nat = pl.BlockSpec((R, LANE), lambda i, j: (i, j))
    shapes = lambda dt: [SDS((S, W), dt), SDS((4, S // 4, W), dt), SDS((16, S // 16, W), dt)]
    outs = pl.pallas_call(
        body,
        grid=(S // R, W // LANE),
        in_specs=[nat] * 5 + ([] if after is None else [pl.BlockSpec(memory_space=pl.ANY)]),
        out_specs=[nat, _perm_spec(4), _perm_spec(16)] * 2,
        out_shape=shapes(bf16) + shapes(f32),
        scratch_shapes=[pltpu.VMEM((R, LANE), f32)] * 2,
        compiler_params=_cparams(("parallel", "parallel"), VMEM_BIG),
        name="attn_merge_bwd",
    )(dy, y, w0, w1, w2, *([] if after is None else [after]))
    return [t.reshape(S, W) for t in outs]


HGRN_SB = 256
HGRN_PAIR = 4


def _chunk_masks():
    r = jnp.arange(HGRN_SB)[:, None]
    c = jnp.arange(HGRN_SB)[None, :]
    same = (r // HGRN_CHUNK) == (c // HGRN_CHUNK)
    return jnp.stack([same & (c <= r), same, same & (c >= r)]).astype(bf16)


def _mask_dot(mask, x):
    hi = x.astype(bf16)
    r1 = x - hi.astype(f32)
    mid = r1.astype(bf16)
    lo = (r1 - mid.astype(f32)).astype(bf16)
    p = jnp.dot(mask, jnp.concatenate([hi, mid, lo], axis=1), preferred_element_type=f32)
    n = x.shape[1]
    return (p[:, :n] + p[:, n:2 * n]) + p[:, 2 * n:]


def _hgrn_prep(q_raw, f_raw, lbv, tril, same):
    sq = _sigmoid(q_raw)
    qs = q_raw * sq
    sig = _sigmoid(f_raw)
    f = lbv + (1.0 - lbv) * sig
    g = jnp.log(f)
    k = 1.0 - f
    G = _mask_dot(tril, g)
    GL = _mask_dot(same, g)
    eG = jnp.exp(G)
    einv = jnp.exp(-G)
    edec = jnp.exp(GL - G)
    return dict(sq=sq, qs=qs, sig=sig, f=f, k=k, eG=eG, einv=einv, edec=edec, eGL=jnp.exp(GL),
                qt=qs * eG, kt=k * einv, kd=k * edec)


def _hgrn_fwd(hg, lb, normw):
    S = hg.shape[0]
    sb = HGRN_SB
    nsb = S // sb
    nch = sb // HGRN_CHUNK

    def body(q_ref, f_ref, v_ref, og_ref, lb_ref, nw_ref, m_ref, y_ref, o_ref, ck_ref, st):
        j = pl.program_id(1)

        @pl.when(j == 0)
        def _():
            st[...] = jnp.zeros_like(st)

        tril_m = m_ref[0]
        tril = tril_m.astype(f32) > 0.5

        def one_head(hh):
            cols = slice(hh * LANE, (hh + 1) * LANE)
            ST = st[hh]
            ck_ref[hh, 0] = ST
            pr = _hgrn_prep(q_ref[:, cols], f_ref[:, cols], lb_ref[:, cols], tril_m, m_ref[1])
            qtb, ktb, kdb = pr["qt"].astype(bf16), pr["kt"].astype(bf16), pr["kd"].astype(bf16)
            eGL = pr["eGL"]
            vb = v_ref[:, cols].astype(bf16)
            A = jnp.where(tril, lax.dot_general(qtb, ktb, NT, preferred_element_type=f32), 0.0)
            o = jnp.dot(A.astype(bf16), vb, preferred_element_type=f32)
            outs = []
            for ci in range(nch):
                lo = ci * HGRN_CHUNK
                sl = slice(lo, lo + HGRN_CHUNK)
                outs.append(o[sl] + lax.dot_general(qtb[sl], ST.astype(bf16), NT, preferred_element_type=f32))
                ST = ST * eGL[lo:lo + 1, :] + lax.dot_general(vb[sl], kdb[sl], TN, preferred_element_type=f32)
            st[hh] = ST
            of = jnp.concatenate(outs, axis=0)
            o_ref[:, cols] = of
            rms = lax.rsqrt(jnp.mean(of * of, axis=-1, keepdims=True) + EPS)
            ogv = og_ref[:, cols]
            y_ref[:, cols] = ((of * rms * nw_ref[...]) * (ogv * _sigmoid(ogv))).astype(bf16)

        for hh in range(HGRN_PAIR):
            one_head(hh)

    wide = HGRN_PAIR * LANE
    col = lambda off: pl.BlockSpec((sb, wide), lambda h, j: (j, off // HGRN_PAIR + h))
    return pl.pallas_call(
        body,
        grid=(4 // HGRN_PAIR, nsb),
        in_specs=[col(0), col(4), col(8), col(12), pl.BlockSpec((1, wide), lambda h, j: (0, h)),
                  pl.BlockSpec((1, LANE), lambda h, j: (0, 0)),
                  pl.BlockSpec((3, sb, sb), lambda h, j: (0, 0, 0))],
        out_specs=[col(0), col(0), pl.BlockSpec((HGRN_PAIR, 1, LANE, LANE), lambda h, j: (h, j, 0, 0))],
        out_shape=[SDS((S, HGRN_W), bf16), SDS((S, HGRN_W), f32), SDS((4, nsb, LANE, LANE), f32)],
        scratch_shapes=[pltpu.VMEM((HGRN_PAIR, LANE, LANE), f32)],
        compiler_params=_cparams(("parallel", "arbitrary")),
        name="hgrn_fwd",
    )(hg, hg, hg, hg, lb, normw, _chunk_masks())


def _hgrn_bwd(hg, o_raw, dy, ck, lb, normw, after=None):
    S = hg.shape[0]
    sb = HGRN_SB
    nsb = S // sb
    nch = sb // HGRN_CHUNK

    def body(q_ref, f_ref, v_ref, og_ref, o_ref, dy_ref, ck_ref, lb_ref, nw_ref, m_ref, *rest):
        dq_ref, df_ref, dv_ref, dog_ref, glb_ref, gnw_ref, dst, alb, anw = rest[-9:]
        j = pl.program_id(1)

        @pl.when(j == 0)
        def _():
            dst[...] = jnp.zeros_like(dst)
            alb[...] = jnp.zeros_like(alb)
            anw[...] = jnp.zeros_like(anw)

        tril_m = m_ref[0]
        tril = tril_m.astype(f32) > 0.5
        nw = nw_ref[...]

        def one_head(hh):
            cols = slice(hh * LANE, (hh + 1) * LANE)
            lbv = lb_ref[:, cols]
            q_raw = q_ref[:, cols]
            pr = _hgrn_prep(q_raw, f_ref[:, cols], lbv, tril_m, m_ref[1])
            qt, kt, kd, eGL = pr["qt"], pr["kt"], pr["kd"], pr["eGL"]
            qtb, ktb, kdb = qt.astype(bf16), kt.astype(bf16), kd.astype(bf16)
            vb = v_ref[:, cols].astype(bf16)

            o = o_ref[:, cols]
            ogv = og_ref[:, cols]
            sog = _sigmoid(ogv)
            rms = lax.rsqrt(jnp.mean(o * o, axis=-1, keepdims=True) + EPS)
            oh = o * rms
            dyv = dy_ref[:, cols]
            dog_ref[:, cols] = (dyv * (oh * nw) * (sog * (1.0 + ogv * (1.0 - sog)))).astype(bf16)
            dohw = dyv * (ogv * sog)
            anw[:, cols] += _colsum8(dohw * oh)
            doh = dohw * nw
            do = rms * (doh - oh * jnp.mean(doh * oh, axis=-1, keepdims=True))
            dob = do.astype(bf16)

            Ab = jnp.where(tril, lax.dot_general(qtb, ktb, NT, preferred_element_type=f32), 0.0).astype(bf16)
            dAb = jnp.where(tril, lax.dot_general(dob, vb, NT, preferred_element_type=f32), 0.0).astype(bf16)
            dv_acc = lax.dot_general(Ab, dob, TN, preferred_element_type=f32)
            dqt = jnp.dot(dAb, ktb, preferred_element_type=f32)
            dkt = lax.dot_general(dAb, qtb, TN, preferred_element_type=f32)

            ST = ck_ref[hh, 0]
            states = []
            for ci in range(nch):
                lo = ci * HGRN_CHUNK
                sl = slice(lo, lo + HGRN_CHUNK)
                states.append(ST)
                ST = ST * eGL[lo:lo + 1, :] + lax.dot_general(vb[sl], kdb[sl], TN, preferred_element_type=f32)

            dST = dst[hh]
            dqt_i, dkd_i, dv_i, deg_i = [None] * nch, [None] * nch, [None] * nch, [None] * nch
            for ci in reversed(range(nch)):
                lo = ci * HGRN_CHUNK
                sl = slice(lo, lo + HGRN_CHUNK)
                ST0 = states[ci]
                dSTb = dST.astype(bf16)
                dv_i[ci] = lax.dot_general(kdb[sl], dSTb, NT, preferred_element_type=f32)
                dqt_i[ci] = jnp.dot(dob[sl], ST0.astype(bf16), preferred_element_type=f32)
                dkd_i[ci] = jnp.dot(vb[sl], dSTb, preferred_element_type=f32)
                deg_i[ci] = jnp.broadcast_to(jnp.sum(dST * ST0, axis=0, keepdims=True), (HGRN_CHUNK, LANE))
                dST = dST * eGL[lo:lo + 1, :] + lax.dot_general(dob[sl], qtb[sl], TN, preferred_element_type=f32)
            dst[hh] = dST

            dqt = dqt + jnp.concatenate(dqt_i, axis=0)
            dkd = jnp.concatenate(dkd_i, axis=0)
            dv_ref[:, cols] = (dv_acc + jnp.concatenate(dv_i, axis=0)).astype(bf16)
            deg = jnp.concatenate(deg_i, axis=0)

            dqs = dqt * pr["eG"]
            dkdkd = dkd * kd
            dG = dqt * qt - dkt * kt - dkdkd
            dk = dkt * pr["einv"] + dkd * pr["edec"]
            dGL = _mask_dot(m_ref[1], dkdkd) + eGL * deg
            dg = _mask_dot(m_ref[2], dG) + dGL
            df = dg / pr["f"] - dk
            sig = pr["sig"]
            df_ref[:, cols] = (df * (1.0 - lbv) * (sig * (1.0 - sig))).astype(bf16)
            alb[:, cols] += _colsum8(df * (1.0 - sig))
            sq = pr["sq"]
            dq_ref[:, cols] = (dqs * (sq * (1.0 + q_raw * (1.0 - sq)))).astype(bf16)

        for hh in range(HGRN_PAIR):
            one_head(hh)

        @pl.when(j == nsb - 1)
        def _():
            glb_ref[...] = jnp.broadcast_to(jnp.sum(alb[...], axis=0, keepdims=True), (SUBLANE, wide))
            gnw_ref[...] = jnp.broadcast_to(jnp.sum(anw[...], axis=0, keepdims=True), (SUBLANE, wide))

    wide = HGRN_PAIR * LANE
    rev = lambda off: pl.BlockSpec((sb, wide), lambda h, j: (nsb - 1 - j, off // HGRN_PAIR + h))
    stat = pl.BlockSpec((SUBLANE, wide), lambda h, j: (0, h))
    return pl.pallas_call(
        body,
        grid=(4 // HGRN_PAIR, nsb),
        in_specs=[rev(0), rev(4), rev(8), rev(12), rev(0), rev(0),
                  pl.BlockSpec((HGRN_PAIR, 1, LANE, LANE), lambda h, j: (h, nsb - 1 - j, 0, 0)),
                  pl.BlockSpec((1, wide), lambda h, j: (0, h)), pl.BlockSpec((1, LANE), lambda h, j: (0, 0)),
                  pl.BlockSpec((3, sb, sb), lambda h, j: (0, 0, 0))]
        + ([] if after is None else [pl.BlockSpec(memory_space=pl.ANY)]),
        out_specs=[rev(0), rev(0), rev(0), rev(0), stat, stat],
        out_shape=[SDS((S, HGRN_W), bf16)] * 4 + [SDS((SUBLANE, HGRN_W), f32)] * 2,
        scratch_shapes=[pltpu.VMEM((HGRN_PAIR, LANE, LANE), f32), pltpu.VMEM((SUBLANE, wide), f32),
                        pltpu.VMEM((SUBLANE, wide), f32)],
        compiler_params=_cparams(("parallel", "arbitrary")),
        name="hgrn_bwd",
    )(hg, hg, hg, hg, o_raw, dy, ck, lb, normw, _chunk_masks(), *([] if after is None else [after]))


def _lb_fwd(raw):
    def body(r_ref, o_ref):
        r = r_ref[...]
        m = jnp.max(r, axis=0, keepdims=True)
        e = jnp.exp(r - m)
        o_ref[...] = (e / jnp.sum(e, axis=0, keepdims=True))[0:1]

    return pl.pallas_call(body, out_shape=SDS((1, raw.shape[1]), f32), name="lb_fwd")(raw)


def _lb_bwd(raw, dlb):
    def body(r_ref, d_ref, o_ref):
        r = r_ref[...]
        m = jnp.max(r, axis=0, keepdims=True)
        e = jnp.exp(r - m)
        s = e / jnp.sum(e, axis=0, keepdims=True)
        s0 = s[0:1]
        onehot0 = jnp.where(lax.broadcasted_iota(jnp.int32, r.shape, 0) == 0, 1.0, 0.0)
        o_ref[...] = d_ref[...] * s0 * (onehot0 - s)

    return pl.pallas_call(body, out_shape=SDS(raw.shape, f32), name="lb_bwd")(raw, dlb)


def _gate_fwd(ya, yh, w_ba, w_bh, gc):
    S = ya.shape[0]
    D = w_ba.shape[1]
    tm = _pick(S, MM_ROWS)

    def body(ya_ref, yh_ref, wa_ref, wh_ref, g0_ref, g1_ref, a_ref, b_ref, o_ref):
        a = jnp.dot(ya_ref[...], wa_ref[...], preferred_element_type=f32).astype(bf16)
        b = jnp.dot(yh_ref[...], wh_ref[...], preferred_element_type=f32).astype(bf16)
        a_ref[...] = a
        b_ref[...] = b
        s0, s1 = _sigmoid(g0_ref[...].astype(f32)), _sigmoid(g1_ref[...].astype(f32))
        o_ref[...] = (s0 * a.astype(f32) + s1 * b.astype(f32)).astype(bf16)

    row = pl.BlockSpec((tm, D), lambda i: (i, 0))
    act = pl.BlockSpec((tm, ya.shape[1]), lambda i: (i, 0))
    wspec = pl.BlockSpec(w_ba.shape, lambda i: (0, 0))
    return pl.pallas_call(
        body,
        grid=(S // tm,),
        in_specs=[act, act, wspec, wspec, row, pl.BlockSpec((tm, D), lambda i: (i, 1))],
        out_specs=[row, row, row],
        out_shape=[SDS((S, D), bf16)] * 3,
        compiler_params=_cparams(("parallel",), VMEM_BIG),
        name="branch_gate_fwd",
    )(ya, yh, w_ba, w_bh, gc, gc)


def _gate_bwd(dmo, w_out, a, b, gc, w_ba, w_bh):
    S, D = a.shape
    W = w_ba.shape[0]
    tm = _pick(S, MM_ROWS)

    def body(dmo_ref, wo_ref, a_ref, b_ref, g0_ref, g1_ref, wa_ref, wh_ref,
             da_ref, db_ref, dg_ref, dya_ref, dyh_ref):
        dm = lax.dot_general(dmo_ref[...], wo_ref[...], NT, preferred_element_type=f32)
        dmv = dm.astype(bf16).astype(f32)
        s0, s1 = _sigmoid(g0_ref[...].astype(f32)), _sigmoid(g1_ref[...].astype(f32))
        da = (dmv * s0).astype(bf16)
        db = (dmv * s1).astype(bf16)
        da_ref[...] = da
        db_ref[...] = db
        dg_ref[:, :D] = (dmv * a_ref[...].astype(f32) * (s0 * (1.0 - s0))).astype(bf16)
        dg_ref[:, D:] = (dmv * b_ref[...].astype(f32) * (s1 * (1.0 - s1))).astype(bf16)
        dya_ref[...] = lax.dot_general(da, wa_ref[...], NT, preferred_element_type=f32)
        dyh_ref[...] = lax.dot_general(db, wh_ref[...], NT, preferred_element_type=f32)

    row = pl.BlockSpec((tm, D), lambda i: (i, 0))
    wide = pl.BlockSpec((tm, 2 * D), lambda i: (i, 0))
    narrow = pl.BlockSpec((tm, W), lambda i: (i, 0))
    whole = lambda t: pl.BlockSpec(t.shape, lambda i: (0, 0))
    return pl.pallas_call(
        body,
        grid=(S // tm,),
        in_specs=[row, whole(w_out), row, row, row, pl.BlockSpec((tm, D), lambda i: (i, 1)), whole(w_ba), whole(w_bh)],
        out_specs=[row, row, wide, narrow, narrow],
        out_shape=[SDS((S, D), bf16), SDS((S, D), bf16), SDS((S, 2 * D), bf16), SDS((S, W), f32), SDS((S, W), f32)],
        compiler_params=_cparams(("parallel",), VMEM_BIG),
        name="gate_bwd_fused",
    )(dmo, w_out, a, b, gc, gc, w_ba, w_bh)


CONV_ROWS = 512
INV_SQRT2 = 0.7071067811865476
INV_SQRT_2PI = 0.3989422804014327


CONV_HALO = 16


def _shift_down(cur, prev, k):
    x = pltpu.roll(cur, k, 0)
    row = lax.broadcasted_iota(jnp.int32, (SUBLANE, LANE), 0)
    head = jnp.where(row < k, pltpu.roll(prev, k, 0)[:SUBLANE], x[:SUBLANE])
    return jnp.concatenate([head, x[SUBLANE:]], axis=0)


def _shift_up(cur, nxt, k):
    R = cur.shape[0]
    x = pltpu.roll(cur, R - k, 0)
    row = lax.broadcasted_iota(jnp.int32, (SUBLANE, LANE), 0)
    tail = jnp.where(row >= SUBLANE - k, pltpu.roll(nxt, SUBLANE - k, 0), x[R - SUBLANE:])
    return jnp.concatenate([x[:R - SUBLANE], tail], axis=0)


def _conv_rows(u_ref, w, b, r0, first):
    R = CONV_ROWS
    cur = u_ref[pl.ds(r0, R), :].astype(f32)
    prev = u_ref[pl.ds(pl.multiple_of(jnp.maximum(r0 - CONV_HALO, 0), CONV_HALO), CONV_HALO), :].astype(f32)
    prev = jnp.where(first, 0.0, prev)
    x1 = _shift_down(cur, prev, 1)
    x2 = _shift_down(cur, prev, 2)
    c = ((b + w[0:1] * x2) + w[1:2] * x1) + w[2:3] * cur
    return c, x2, x1, cur


def _conv_fwd(ug, uv, wg, wv, bg, bv):
    S, F = ug.shape
    nchunk = S // CONV_ROWS

    def body(ug_ref, uv_ref, wg_ref, wv_ref, bg_ref, bv_ref, o_ref):
        wgv, wvv, bgv, bvv = wg_ref[...], wv_ref[...], bg_ref[...], bv_ref[...]

        def step(ci, carry):
            r0 = pl.multiple_of(ci * CONV_ROWS, CONV_ROWS)
            cg = _conv_rows(ug_ref, wgv, bgv, r0, ci == 0)[0]
            cv = _conv_rows(uv_ref, wvv, bvv, r0, ci == 0)[0]
            gelu = 0.5 * cg * (1.0 + lax.erf(cg * INV_SQRT2))
            o_ref[pl.ds(r0, CONV_ROWS), :] = (gelu * cv).astype(bf16)
            return carry

        lax.fori_loop(0, nchunk, step, 0)

    col = pl.BlockSpec((S, LANE), lambda j: (0, j))
    w3 = pl.BlockSpec((3, LANE), lambda j: (0, j))
    b1 = pl.BlockSpec((1, LANE), lambda j: (0, j))
    return pl.pallas_call(
        body,
        grid=(F // LANE,),
        in_specs=[col, col, w3, w3, b1, b1],
        out_specs=col,
        out_shape=SDS((S, F), bf16),
        compiler_params=_cparams(("parallel",), VMEM_BIG),
        name="conv_fwd",
    )(ug, uv, wg, wv, bg, bv)


def _conv_bwd(ug, uv, dact, wg, wv, bg, bv):
    S, F = ug.shape
    R = CONV_ROWS
    nchunk = S // R

    def body(ug_ref, uv_ref, da_ref, wg_ref, wv_ref, bg_ref, bv_ref, dug_ref, duv_ref, sg_ref, sv_ref, dcg, dcv):
        wgv, wvv, bgv, bvv = wg_ref[...], wv_ref[...], bg_ref[...], bv_ref[...]
        zero = jnp.zeros((SUBLANE, LANE), f32)

        def fwd_step(ci, acc):
            r0 = pl.multiple_of(ci * R, R)
            cg, g2, g1, g0 = _conv_rows(ug_ref, wgv, bgv, r0, ci == 0)
            cv, v2, v1, v0 = _conv_rows(uv_ref, wvv, bvv, r0, ci == 0)
            da = da_ref[pl.ds(r0, R), :].astype(f32)
            cdf = 0.5 * (1.0 + lax.erf(cg * INV_SQRT2))
            pdf = INV_SQRT_2PI * jnp.exp(-0.5 * cg * cg)
            dg = da * cv * (cdf + cg * pdf)
            dv = da * (cg * cdf)
            dcg[pl.ds(r0, R), :] = dg
            dcv[pl.ds(r0, R), :] = dv
            new = (acc[0] + _colsum8(dg * g2), acc[1] + _colsum8(dg * g1), acc[2] + _colsum8(dg * g0),
                   acc[3] + _colsum8(dg),
                   acc[4] + _colsum8(dv * v2), acc[5] + _colsum8(dv * v1), acc[6] + _colsum8(dv * v0),
                   acc[7] + _colsum8(dv))
            return new

        acc = lax.fori_loop(0, nchunk, fwd_step, (zero,) * 8)
        rows = lax.broadcasted_iota(jnp.int32, (SUBLANE, LANE), 0)

        def stats(parts):
            out = jnp.zeros((SUBLANE, LANE), f32)
            for k, pt in enumerate(parts):
                out = jnp.where(rows == k, jnp.sum(pt, axis=0, keepdims=True), out)
            return out

        sg_ref[...] = stats(acc[0:4])
        sv_ref[...] = stats(acc[4:8])

        def du_rows(dc, w, r0, last):
            cur = dc[pl.ds(r0, R), :]
            nxt = dc[pl.ds(pl.multiple_of(jnp.minimum(r0 + R, S - SUBLANE), SUBLANE), SUBLANE), :]
            nxt = jnp.where(last, 0.0, nxt)
            return w[2:3] * cur + w[1:2] * _shift_up(cur, nxt, 1) + w[0:1] * _shift_up(cur, nxt, 2)

        def bwd_step(ci, carry):
            r0 = pl.multiple_of(ci * R, R)
            last = ci == nchunk - 1
            dug_ref[pl.ds(r0, R), :] = du_rows(dcg, wgv, r0, last).astype(bf16)
            duv_ref[pl.ds(r0, R), :] = du_rows(dcv, wvv, r0, last).astype(bf16)
            return carry

        lax.fori_loop(0, nchunk, bwd_step, 0)

    col = pl.BlockSpec((S, LANE), lambda j: (0, j))
    w3 = pl.BlockSpec((3, LANE), lambda j: (0, j))
    b1 = pl.BlockSpec((1, LANE), lambda j: (0, j))
    st = pl.BlockSpec((SUBLANE, LANE), lambda j: (0, j))
    return pl.pallas_call(
        body,
        grid=(F // LANE,),
        in_specs=[col, col, col, w3, w3, b1, b1],
        out_specs=[col, col, st, st],
        out_shape=[SDS((S, F), bf16), SDS((S, F), bf16), SDS((SUBLANE, F), f32), SDS((SUBLANE, F), f32)],
        scratch_shapes=[pltpu.VMEM((S, LANE), f32), pltpu.VMEM((S, LANE), f32)],
        compiler_params=_cparams(("parallel",), VMEM_BIG),
        name="conv_bwd",
    )(ug, uv, dact, wg, wv, bg, bv)


def _adam_math(w, g, m, v):
    m = ADAM_B1 * m + (1.0 - ADAM_B1) * g
    v = ADAM_B2 * v + (1.0 - ADAM_B2) * (g * g)
    m_hat = m / (1.0 - ADAM_B1 ** ADAM_STEP)
    v_hat = v / (1.0 - ADAM_B2 ** ADAM_STEP)
    delta = -ADAM_LR * (m_hat / (jnp.sqrt(v_hat) + ADAM_EPS) + ADAM_WD * w)
    return delta, m, v


def _adamw(w, m, v, g, name):
    R, C = w.shape
    parts = g.ndim == 3
    tr = R
    if R % 16 == 0:
        for t in range(R, 0, -16):
            if R % t == 0 and t * C * 4 <= ADAM_BLOCK_BYTES:
                tr = t
                break

    def body(w_ref, m_ref, v_ref, g_ref, go_ref, d_ref, mo_ref, vo_ref):
        if parts:
            gv = ((g_ref[0].astype(f32) + g_ref[1].astype(f32)) + g_ref[2].astype(f32)) + g_ref[3].astype(f32)
        else:
            gv = g_ref[...]
        go_ref[...] = gv
        d, mn, vn = _adam_math(w_ref[...], gv, m_ref[...], v_ref[...])
        d_ref[...] = d
        mo_ref[...] = mn
        vo_ref[...] = vn

    row = pl.BlockSpec((tr, C), lambda i: (i, 0))
    gspec = pl.BlockSpec((4, tr, C), lambda i: (0, i, 0)) if parts else row
    return pl.pallas_call(
        body,
        grid=(R // tr,),
        in_specs=[row, row, row, gspec],
        out_specs=[row] * 4,
        out_shape=[SDS((R, C), f32)] * 4,
        compiler_params=_cparams(("parallel",), VMEM_BIG),
        name=name,
    )(w, m, v, g)


def _sum8(parts, name):
    _, _, R, C = parts.shape

    def body(p_ref, o_ref):
        acc = p_ref[0, 0]
        for c in range(2):
            for k in range(4):
                if c or k:
                    acc = acc + p_ref[c, k]
        o_ref[...] = acc

    return pl.pallas_call(body, out_shape=SDS((R, C), f32), name=name)(parts)


def _pair_add(by_core, b, name):
    _, K, R, C = by_core.shape
    tr = R // 2 if R % 32 == 0 else R

    def body(c_ref, a_ref, b_ref, o_ref):
        o_ref[...] = (a_ref[0].astype(f32) + b_ref[...].astype(f32)).astype(bf16)

    blk = pl.BlockSpec((1, tr, C), lambda k, i, c: (k, i, 0))
    return pl.pallas_call(
        body,
        grid_spec=pltpu.PrefetchScalarGridSpec(
            num_scalar_prefetch=1,
            grid=(K, R // tr),
            in_specs=[pl.BlockSpec((1, 1, tr, C), lambda k, i, c: (c[0], k, i, 0)), blk],
            out_specs=blk,
        ),
        out_shape=SDS((K, R, C), bf16),
        compiler_params=_cparams(("parallel", "parallel")),
        name=name,
    )(lax.axis_index("c").astype(jnp.int32).reshape(1), by_core, b)


_ANY = pl.BlockSpec(memory_space=pl.ANY)


def _chip_out_shape(src, gather):
    return SDS((4,) + tuple(src.shape if gather else src.shape[1:]), src.dtype)


def _fill_own(out, src, gather):
    mine = 2 * lax.axis_index("x") + lax.axis_index("y")
    own = src if gather else lax.dynamic_index_in_dim(src, mine, axis=0, keepdims=False)
    return lax.dynamic_update_index_in_dim(out, own, mine, axis=0)


_HBM = pl.BlockSpec(memory_space=pltpu.HBM)
_SEM = pl.BlockSpec(memory_space=pltpu.SEMAPHORE)
_EFFECT = pltpu.SideEffectType.DATAFLOW_SIDE_EFFECTING
_SPLIT_PEERS = {"chip_gather": 3, "chip_xchg": 3, "core_gather": 1, "core_swap": 1}


def _split_land(src, kind):
    if kind == "core_gather":
        return SDS((2,) + tuple(src.shape), src.dtype)
    if kind == "core_swap":
        return SDS(tuple(src.shape[1:]), src.dtype)
    return _chip_out_shape(src, kind == "chip_gather")


def _split_copies(src_ref, land_ref, sems, kind):
    x, y, c = lax.axis_index("x"), lax.axis_index("y"), lax.axis_index("c")
    n = _SPLIT_PEERS[kind]
    if kind == "core_gather":
        routes = [((x, y, 1 - c), src_ref, land_ref.at[c], land_ref.at[1 - c])]
    elif kind == "core_swap":
        routes = [((x, y, 1 - c), src_ref.at[1 - c], land_ref, land_ref)]
    else:
        mine = 2 * x + y
        gather = kind == "chip_gather"
        routes = [((px, py, c), src_ref if gather else src_ref.at[2 * px + py], land_ref.at[mine],
                   land_ref.at[2 * px + py]) for px, py in [(1 - x, y), (x, 1 - y), (1 - x, 1 - y)]]
    sends, recvs = [], []
    for j, (peer, piece, there, here) in enumerate(routes):
        sends.append(pltpu.make_async_remote_copy(src_ref=piece, dst_ref=there, send_sem=sems[j],
                                                  recv_sem=sems[n + j], device_id=peer, device_id_type=MESH))
        recvs.append(pltpu.make_async_remote_copy(src_ref=piece, dst_ref=here, send_sem=sems[j],
                                                  recv_sem=sems[n + j], device_id=peer, device_id_type=MESH))
    return sends, recvs


def _split_start(src, kind, name, after=None):
    land = _split_land(src, kind)
    ns = 2 * _SPLIT_PEERS[kind]
    n_in = 2 if after is None else 3

    def body(*refs):
        src_ref, land_ref = refs[:2]
        outs = refs[n_in:]
        for cp in _split_copies(src_ref, land_ref, outs[:ns], kind)[0]:
            cp.start()
        token = outs[ns + 2]
        token[...] = jnp.zeros_like(token)

    res = pl.pallas_call(
        body,
        name=name,
        out_shape=(pltpu.SemaphoreType.DMA(()),) * ns
        + (pltpu.HBM(src.shape, src.dtype), pltpu.HBM(land.shape, land.dtype), SDS((SUBLANE, LANE), f32)),
        in_specs=(_HBM, _HBM) + (() if after is None else (_ANY,)),
        out_specs=(_SEM,) * ns + (_HBM, _HBM, pl.BlockSpec(memory_space=pltpu.VMEM)),
        input_output_aliases={0: ns, 1: ns + 1},
        compiler_params=pltpu.CompilerParams(has_side_effects=_EFFECT),
    )(pltpu.with_memory_space_constraint(src, pltpu.HBM),
      pltpu.with_memory_space_constraint(lax.empty(land.shape, land.dtype), pltpu.HBM),
      *(() if after is None else (after,)))
    return (res[:ns], res[ns], res[ns + 1]), res[ns + 2]


def _split_wait(state, after, kind, name):
    sems, src_thru, land_thru = state
    ns = 2 * _SPLIT_PEERS[kind]

    def body(src_ref, land_ref, *rest):
        sends, recvs = _split_copies(src_ref, land_ref, rest[:ns], kind)
        for cp in recvs:
            cp.wait_recv()
        for cp in sends:
            cp.wait_send()

    src_out, got = pl.pallas_call(
        body,
        name=name,
        out_shape=(pltpu.HBM(src_thru.shape, src_thru.dtype), pltpu.HBM(land_thru.shape, land_thru.dtype)),
        in_specs=(_HBM, _HBM) + (_SEM,) * ns + (_ANY,),
        out_specs=(_HBM, _HBM),
        input_output_aliases={0: 0, 1: 1},
        compiler_params=pltpu.CompilerParams(has_side_effects=_EFFECT),
    )(src_thru, land_thru, *sems, after)
    if kind == "core_swap":
        return got, src_out
    if kind == "core_gather":
        return lax.dynamic_update_index_in_dim(got, src_out, lax.axis_index("c"), axis=0)
    return _fill_own(got, src_out, kind == "chip_gather")


def _core_gather(src, name):
    def body(src_ref, out_ref, send_sem, recv_sem):
        x, y, c = lax.axis_index("x"), lax.axis_index("y"), lax.axis_index("c")
        cp = pltpu.make_async_remote_copy(src_ref=src_ref, dst_ref=out_ref.at[c], send_sem=send_sem,
                                          recv_sem=recv_sem, device_id=(x, y, 1 - c), device_id_type=MESH)
        cp.start()
        pltpu.make_async_remote_copy(src_ref=src_ref, dst_ref=out_ref.at[1 - c], send_sem=send_sem,
                                     recv_sem=recv_sem, device_id=(x, y, 1 - c), device_id_type=MESH).wait_recv()
        cp.wait_send()

    out = pl.pallas_call(
        body,
        in_specs=[_ANY],
        out_specs=_ANY,
        out_shape=SDS((2,) + tuple(src.shape), src.dtype),
        scratch_shapes=[pltpu.SemaphoreType.DMA, pltpu.SemaphoreType.DMA],
        name=name,
    )(src)
    return lax.dynamic_update_index_in_dim(out, src, lax.axis_index("c"), axis=0)


_PACK_A = (("w_in", (1088, 1024)),)
_PACK_B = (("w_ba", (512, 128)), ("w_bh", (512, 128)), ("w_out", (128, 1024)), ("w_up", (704, 1024)),
           ("w_down", (352, 1024)))
_PACK_SIZES = _PACK_A + _PACK_B
_TRANSPOSED = ("w_in", "w_up")


def _slab_rows(sizes):
    return sum(r * c for _, (r, c) in sizes) // D_MODEL


def _pack_rows(d, sizes):
    n = d[sizes[0][0]].shape[0]
    return jnp.concatenate([d[k].reshape(n, -1, D_MODEL) for k, _ in sizes], axis=1)


def _unpack_rows(slab, sizes):
    n = slab.shape[0]
    out, lo = {}, 0
    for key, (r, c) in sizes:
        rows = r * c // D_MODEL
        out[key] = slab[:, lo:lo + rows].reshape(n, r, c)
        lo += rows
    return out


def _by_core(gslab):
    return jnp.swapaxes(gslab.reshape((4, 2) + gslab.shape[1:]), 0, 1)


def _cols_to_full(t):
    return jnp.swapaxes(t, 0, 1).reshape(t.shape[1], -1)


def _full_to_cols(t):
    K = t.shape[0]
    return jnp.swapaxes(t.reshape(K, 8, -1), 0, 1)


_SMALL = (("pre_mix_norm", (1, 1024)), ("rel_bias", (32, 24)), ("hgrn_lb_raw", (2, 512)), ("hgrn_norm", (1, 128)),
          ("post_mix_norm", (1, 1024)), ("pre_ffn_norm", (1, 1024)), ("conv_b", (1, 5632)),
          ("post_ffn_norm", (1, 1024)))
_SMALL_ROWS = 96
_CONVW_ROWS = 136


_SMALL_USED = sum(r * c for _, (r, c) in _SMALL)


def _pack_small(d, extra=None):
    flat = jnp.concatenate([d[k].reshape(-1) for k, _ in _SMALL] + ([] if extra is None else [extra.reshape(-1)]))
    flat = jnp.pad(flat, (0, _SMALL_ROWS * LANE - flat.shape[0]))
    return flat.reshape(_SMALL_ROWS, LANE)


def _unpack_small(p):
    flat = p.reshape(-1)
    out, lo = {}, 0
    for k, shp in _SMALL:
        n = shp[0] * shp[1]
        out[k] = flat[lo:lo + n].reshape(shp)
        lo += n
    return out


def _local_step(x, tgt, P, plan):
    S = x.shape[0]
    P = dict(P)
    lb = _lb_fwd(P["hgrn_lb_raw"])
    hs = _prep(x, P["pre_mix_norm"], plan.start_token())
    h1 = hs[0]
    consts = [_bias_consts(d) for d in DILATIONS]
    biases, dep = [], h1
    for g in range(N_GROUPS):
        tab_t = P["rel_bias"][:, 8 * g:8 * g + 8].T
        dep = _bias_build(tab_t, consts[g][0], consts[g][1], f"bias_build{g}", dep)
        biases.append(dep.reshape(8, ATTN_BLOCK, 2 * ATTN_BLOCK))
    W = dict(plan.weights_a(dep))
    qkv0, hg, gc = _mm_fanout(h1, [W["wt_qkv"][0], W["wt_hg"], W["wt_gate"]], "nt", [bf16, f32, bf16], "proj_natural")
    qkv = [qkv0] + [_mm(hs[g], W["wt_qkv"][g], "nt", bf16, f"proj_qkv{g}") for g in (1, 2)]
    obuf, lbuf, token = [], [], None
    for g, d in enumerate(DILATIONS):
        o_g, l_g = _attn_fwd(qkv[g], biases[g], (S // d) // ATTN_BLOCK, f"attn_fwd{g}", after=token)
        lbuf.append(l_g)
        obuf.append(o_g)
        if g == 0:
            token = plan.forward_b(o_g)
    y_attn, y_attn_b, w0, w1, w2 = _attn_merge(obuf[0], obuf[1], obuf[2], lbuf[0], lbuf[1], lbuf[2])
    y_hgrn, o_raw, ck = _hgrn_fwd(hg, lb, P["hgrn_norm"])
    wb = plan.weights_b(y_hgrn)
    P["conv_w"] = wb.pop("conv_w")
    W.update(wb)
    a, b, merged = _gate_fwd(y_attn_b, y_hgrn, W["w_ba"], W["w_bh"], gc)
    mo, x1, h2 = _mid_fwd(x, merged, W["w_out"], P["post_mix_norm"], P["pre_ffn_norm"])
    ug, uv = _mm_fanout(h2, [W["wt_up_g"], W["wt_up_v"]], "nt", [bf16, bf16], "up_proj")
    cw_g, cw_v = P["conv_w"][:, :D_FF], P["conv_w"][:, D_FF:]
    cb_g, cb_v = P["conv_b"][:, :D_FF], P["conv_b"][:, D_FF:]
    act = _conv_fwd(ug, uv, cw_g, cw_v, cb_g, cb_v)
    loss, dy, dfo, g_post_ffn = _final(x1, act, W["w_down"], tgt, P["post_ffn_norm"])
    gW_down = _mm(act, dfo, "tn", bf16, "gw_down")
    dact = _mm(dfo, W["w_down"], "nt", bf16, "d_act")
    dug, duv, st_g, st_v = _conv_bwd(ug, uv, dact, cw_g, cw_v, cb_g, cb_v)
    gW_up_g = _mm(dug, h2, "tn", bf16, "gw_up_gate")
    gW_up_v = _mm(duv, h2, "tn", bf16, "gw_up_val")
    dx1, dmo, g_pre_ffn, g_post_mix = _mid_bwd(dy, dug, duv, W["wt_up_g"], W["wt_up_v"], x1, mo, P["pre_ffn_norm"],
                                               P["post_mix_norm"])
    gW_out = _mm(merged, dmo, "tn", bf16, "gw_out")
    da, db, dgc, dyattn, dyhgrn = _gate_bwd(dmo, W["w_out"], a, b, gc, W["w_ba"], W["w_bh"])
    gW_ba = _mm(y_attn_b, da, "tn", bf16, "gw_ba")
    gW_bh = _mm(y_hgrn, db, "tn", bf16, "gw_bh")
    big_b = dict(w_ba=gW_ba, w_bh=gW_bh, w_out=gW_out, w_up=[gW_up_g, gW_up_v], w_down=gW_down)
    dos = _attn_merge_bwd(dyattn, y_attn, w0, w1, w2, after=plan.grads_b_start(big_b))
    dq_h, df_h, dv_h, dog_h, glb8, gnw8 = _hgrn_bwd(hg, o_raw, dyhgrn, ck, lb, P["hgrn_norm"],
                                                   after=plan.grads_b_exchange(dos[5]))
    dhg = [dq_h, df_h, dv_h, dog_h]
    g_lb_raw = _lb_bwd(P["hgrn_lb_raw"], glb8[0:1])
    gn = gnw8[0:1]
    g_hgrn_norm = (gn[:, 0:128] + gn[:, 128:256]) + (gn[:, 256:384] + gn[:, 384:512])
    dqkvs, gW_qkv, g_rel = [], [], []
    for g, d in enumerate(DILATIONS):
        dq, dk, dv, dbias = _attn_bwd(qkv[g], biases[g], dos[g], dos[3 + g], lbuf[g], (S // d) // ATTN_BLOCK,
                                      f"attn_bwd{g}")
        dqkvs.append([dq, dk, dv])
        gW_qkv.append(_mm(dqkvs[g], hs[g], "tn", bf16, f"gw_qkv{g}"))
        g_rel.append(_bias_grad(dbias.reshape(8, -1), consts[g][0], f"bias_grad{g}"))
    gW_hg = _mm(dhg, h1, "tn", bf16, "gw_hg")
    gW_gate = _mm(dgc, h1, "tn", bf16, "gw_gate")
    gW_in = gW_qkv + [gW_hg, gW_gate]
    token = plan.grads_a_start(gW_in)
    dh_perm = [_mm(dqkvs[g], W["wt_qkv"][g], "nn", f32, f"dh1_qkv{g}", after=token) for g in (1, 2)]
    token = plan.grads_a_exchange(dh_perm[1])
    dh_main = _mm(dqkvs[0] + dhg + [dgc], [W["wt_qkv"][0], W["wt_hg"], W["wt_gate"]], "nn", f32, "dh1_main",
                  after=token)
    grad_x, g_pre_mix = _first_bwd(x, dx1, _dh_sum(dh_main, dh_perm[0], dh_perm[1]), P["pre_mix_norm"])

    g_conv_w = jnp.concatenate([st_g[0:3], st_v[0:3]], axis=1)
    g_conv_b = jnp.concatenate([st_g[3:4], st_v[3:4]], axis=1)
    small = dict(pre_mix_norm=g_pre_mix, rel_bias=jnp.concatenate(g_rel, axis=1), hgrn_lb_raw=g_lb_raw,
                 hgrn_norm=g_hgrn_norm, post_mix_norm=g_post_mix, pre_ffn_norm=g_pre_ffn, conv_b=g_conv_b,
                 post_ffn_norm=g_post_ffn, conv_w=g_conv_w)
    return loss, grad_x, gW_in, big_b, small


def _weights_a(both):
    wt = jnp.swapaxes(both, 0, 1).reshape(-1, D_MODEL)
    return dict(
        wt_qkv=[wt[g * QKV_G:(g + 1) * QKV_G] for g in range(N_GROUPS)],
        wt_hg=wt[3 * QKV_G:3 * QKV_G + 4 * HGRN_W],
        wt_gate=wt[3 * QKV_G + 4 * HGRN_W:],
    )


def _weights_b(slabs):
    sh = _unpack_rows(slabs, _PACK_B)
    wt_up = sh["w_up"].reshape(-1, D_MODEL)
    return dict(
        w_ba=_cols_to_full(sh["w_ba"]),
        w_bh=_cols_to_full(sh["w_bh"]),
        w_out=sh["w_out"].reshape(D_MODEL, D_MODEL),
        wt_up_g=wt_up[:D_FF],
        wt_up_v=wt_up[D_FF:],
        w_down=sh["w_down"].reshape(D_FF, D_MODEL),
    )


def _dest_rows(sections, height):
    out = []
    for j in range(8):
        lo, hi, off, pieces = j * height, (j + 1) * height, 0, []
        for s in sections:
            a, b = max(lo, off), min(hi, off + s.shape[0])
            if a < b:
                pieces.append(s[a - off:b - off])
            off += s.shape[0]
        out.append(pieces[0] if len(pieces) == 1 else jnp.concatenate(pieces, axis=0))
    return out


def _grad_blocks_a(sections):
    rows = _dest_rows(sections, 1088)
    return jnp.stack([jnp.stack([rows[2 * k + c].astype(bf16) for k in range(4)]) for c in range(2)])


def _grad_slab_b(g):
    shards = dict(w_ba=_full_to_cols(g["w_ba"]), w_bh=_full_to_cols(g["w_bh"]), w_out=g["w_out"].reshape(8, 128, D_MODEL),
                  w_up=jnp.stack(_dest_rows(g["w_up"], 704)), w_down=g["w_down"].reshape(8, 352, D_MODEL))
    return _pack_rows({k: v.astype(bf16) for k, v in shards.items()}, _PACK_B)


_CONVW_SLAB_ROWS = 16


class _Traffic:
    def __init__(self, slab_a, slab_b, conv_w):
        hi = conv_w.astype(bf16)
        r1 = conv_w - hi.astype(f32)
        mid = r1.astype(bf16)
        lo = (r1 - mid.astype(f32)).astype(bf16)
        bits = jnp.stack([hi, mid, lo]).reshape(-1)
        tail = jnp.pad(bits, (0, _CONVW_SLAB_ROWS * D_MODEL - bits.shape[0])).reshape(_CONVW_SLAB_ROWS, D_MODEL)
        self.slab_b = jnp.concatenate([slab_b, tail], axis=0)
        self.state_a, tok = _split_start(slab_a, "chip_gather", "ag_a_start")
        self.state_b, self.token = _split_start(self.slab_b, "chip_gather", "ag_b_start", after=tok)
        self.state = None
        self.state_gb = None

    def start_token(self):
        return self.token

    def weights_a(self, after):
        by_chip = _split_wait(self.state_a, after, "chip_gather", "ag_a_wait")
        return _weights_a(_core_gather(by_chip, "ag_a_cores"))

    def forward_b(self, after):
        by_chip = _split_wait(self.state_b, after, "chip_gather", "ag_b_wait")
        self.state, token = _split_start(by_chip, "core_gather", "ag_b_cores_start")
        return token

    def weights_b(self, after):
        both = _split_wait(self.state, after, "core_gather", "ag_b_cores_wait")
        slabs = jnp.swapaxes(both, 0, 1).reshape((8,) + tuple(self.slab_b.shape))
        rows = _slab_rows(_PACK_B)
        out = _weights_b(slabs[:, :rows])
        pieces = slabs[:, rows:].reshape(8, -1)[:, :3 * 3 * 704].reshape(8, 3, 3, 704).astype(f32)
        out["conv_w"] = _cols_to_full((pieces[:, 0] + pieces[:, 1]) + pieces[:, 2])
        return out

    def grads_b_start(self, grads):
        self.state, token = _split_start(_by_core(_grad_slab_b(grads)), "core_swap", "rs_b_cores_start")
        return token

    def grads_b_exchange(self, after):
        from_sib, by_core = _split_wait(self.state, after, "core_swap", "rs_b_cores_wait")
        self.state_gb, token = _split_start(_pair_add(by_core, from_sib, "rs_b_pair_add"), "chip_xchg", "rs_b_start")
        return token

    def grads_a_start(self, sections):
        self.state, token = _split_start(_grad_blocks_a(sections), "core_swap", "rs_a_cores_start")
        return token

    def grads_a_exchange(self, after):
        from_sib, by_core = _split_wait(self.state, after, "core_swap", "rs_a_cores_wait")
        self.state, token = _split_start(_pair_add(by_core, from_sib, "rs_a_pair_add"), "chip_xchg", "rs_a_start")
        return token

    def parts(self, after):
        parts = _unpack_rows(_split_wait(self.state_gb, after, "chip_xchg", "rs_b_wait"), _PACK_B)
        parts["w_in"] = _split_wait(self.state, after, "chip_xchg", "rs_a_wait")
        return parts


def kernel(x, pre_mix_norm, w_in, rel_bias, hgrn_lb_raw, hgrn_norm, w_branch_attn, w_branch_hgrn, w_out, post_mix_norm, pre_ffn_norm, w_up, conv_w, conv_b, w_down, post_ffn_norm, loss_target, m_pre_mix_norm, m_w_in, m_rel_bias, m_hgrn_lb_raw, m_hgrn_norm, m_w_branch_attn, m_w_branch_hgrn, m_w_out, m_post_mix_norm, m_pre_ffn_norm, m_w_up, m_conv_w, m_conv_b, m_w_down, m_post_ffn_norm, v_pre_mix_norm, v_w_in, v_rel_bias, v_hgrn_lb_raw, v_hgrn_norm, v_w_branch_attn, v_w_branch_hgrn, v_w_out, v_post_mix_norm, v_pre_ffn_norm, v_w_up, v_conv_w, v_conv_b, v_w_down, v_post_ffn_norm):
    ci = lax.axis_index("c")
    dev = 4 * lax.axis_index("x") + 2 * lax.axis_index("y") + ci
    tr = lambda t: jnp.swapaxes(t[0], 0, 1)
    wts = dict(w_in=tr(w_in), w_ba=w_branch_attn[0], w_bh=w_branch_hgrn[0], w_out=w_out[0], w_up=tr(w_up),
               w_down=w_down[0])
    mom = dict(w_in=tr(m_w_in), w_ba=m_w_branch_attn[0], w_bh=m_w_branch_hgrn[0], w_out=m_w_out[0], w_up=tr(m_w_up),
               w_down=m_w_down[0])
    var = dict(w_in=tr(v_w_in), w_ba=v_w_branch_attn[0], w_bh=v_w_branch_hgrn[0], w_out=v_w_out[0], w_up=tr(v_w_up),
               w_down=v_w_down[0])
    small_w = dict(pre_mix_norm=pre_mix_norm, rel_bias=rel_bias, hgrn_lb_raw=hgrn_lb_raw, hgrn_norm=hgrn_norm,
                   post_mix_norm=post_mix_norm, pre_ffn_norm=pre_ffn_norm, conv_b=conv_b, post_ffn_norm=post_ffn_norm)
    small_m = dict(pre_mix_norm=m_pre_mix_norm, rel_bias=m_rel_bias, hgrn_lb_raw=m_hgrn_lb_raw, hgrn_norm=m_hgrn_norm,
                   post_mix_norm=m_post_mix_norm, pre_ffn_norm=m_pre_ffn_norm, conv_b=m_conv_b,
                   post_ffn_norm=m_post_ffn_norm)
    small_v = dict(pre_mix_norm=v_pre_mix_norm, rel_bias=v_rel_bias, hgrn_lb_raw=v_hgrn_lb_raw, hgrn_norm=v_hgrn_norm,
                   post_mix_norm=v_post_mix_norm, pre_ffn_norm=v_pre_ffn_norm, conv_b=v_conv_b,
                   post_ffn_norm=v_post_ffn_norm)

    plan = _Traffic(wts["w_in"].astype(bf16),
                    _pack_rows({k: wts[k].astype(bf16)[None] for k, _ in _PACK_B}, _PACK_B)[0], conv_w[0])

    loss8, grad_x, _, _, small = _local_step(x[0], loss_target[0], small_w, plan)
    spack = jnp.concatenate([_pack_small(small, loss8[0, 0:1]),
                             jnp.pad(small["conv_w"].reshape(-1, LANE), ((0, _CONVW_ROWS - 132), (0, 0)))], axis=0)
    small_state, token = _split_start(spack, "chip_gather", "ag_small_start")

    parts = plan.parts(token)
    outs_big = {}
    for k, _ in _PACK_SIZES:
        outs_big[k] = _adamw(wts[k], mom[k], var[k], parts[k], "adamw_" + k)

    by_chip = _split_wait(small_state, outs_big["w_in"][1], "chip_gather", "ag_small_wait")
    allp = _core_gather(by_chip, "ag_small_cores")
    ssum = _sum8(allp, "small_sum")
    gs = ssum[:_SMALL_ROWS]
    loss = ssum[_SMALL_USED // LANE, _SMALL_USED % LANE]
    res_small = _adamw(_pack_small(small_w), _pack_small(small_m), _pack_small(small_v), gs, "adamw_small")
    sm = [_unpack_small(t) for t in res_small]
    g_cw_full = ssum[_SMALL_ROWS:_SMALL_ROWS + 132].reshape(3, 2 * D_FF)
    g_cw = lax.dynamic_slice_in_dim(g_cw_full, dev * 704, 704, axis=1)
    res_cw = _adamw(conv_w[0], m_conv_w[0], v_conv_w[0], g_cw, "adamw_conv_w")

    def pick(i):
        def big_(k):
            t = outs_big[k][i]
            return (jnp.swapaxes(t, 0, 1) if k in _TRANSPOSED else t)[None]
        return [sm[i]["pre_mix_norm"], big_("w_in"), sm[i]["rel_bias"], sm[i]["hgrn_lb_raw"], sm[i]["hgrn_norm"],
                big_("w_ba"), big_("w_bh"), big_("w_out"), sm[i]["post_mix_norm"], sm[i]["pre_ffn_norm"],
                big_("w_up"), res_cw[i][None], sm[i]["conv_b"], big_("w_down"), sm[i]["post_ffn_norm"]]

    return (loss, grad_x[None], *pick(0), *pick(1), *pick(2), *pick(3))
```

```python
import functools
import math

import jax
import jax.numpy as jnp
from jax import lax
from jax.experimental import pallas as pl
from jax.experimental.pallas import tpu as pltpu

f32 = jnp.float32
bf16 = jnp.bfloat16
SDS = jax.ShapeDtypeStruct
HIGHEST = lax.Precision.HIGHEST
MESH = pl.DeviceIdType.MESH

NN = (((1,), (0,)), ((), ()))
NT = (((1,), (1,)), ((), ()))
TN = (((0,), (0,)), ((), ()))

D_MODEL = 1024
N_GROUPS = 3
DILATIONS = (1, 4, 16)
HEAD_DIM = 64
ATTN_BLOCK = 128
QKV_G = 1536
ATTN_OUT = 512
HGRN_W = 512
HGRN_CHUNK = 32
D_FF = 2816
NUM_BUCKETS = 32
MAX_EXACT = 16
MAX_DISTANCE = 2048
NEG_INF = -1e30
EPS = 1e-6
LANE = 128
SUBLANE = 8
VMEM_BIG = 48 * 1024 * 1024
MM_ROWS = 512
MM_OUT_BYTES = 8 * 1024 * 1024
ADAM_BLOCK_BYTES = 2304 * 1024

ADAM_LR, ADAM_B1, ADAM_B2, ADAM_EPS, ADAM_WD, ADAM_STEP = 0.001, 0.9, 0.999, 1e-08, 0.01, 10


def _pick(n, pref):
    t = pref
    while t >= LANE:
        if n % t == 0:
            return t
        t //= 2
    return n


def _cparams(sem=None, vmem=None):
    kw = {}
    if sem is not None:
        kw["dimension_semantics"] = sem
    if vmem is not None:
        kw["vmem_limit_bytes"] = vmem
    return pltpu.CompilerParams(**kw)


def _sigmoid(x):
    return jax.nn.sigmoid(x)


def _colsum8(x):
    return x.reshape(x.shape[0] // SUBLANE, SUBLANE, x.shape[1]).sum(axis=0)


def _mm(a, b, mode, out_dtype, name, acc=None, after=None):
    dims = {"nn": NN, "nt": NT, "tn": TN}[mode]
    has_acc = acc is not None
    parts = list(a) if isinstance(a, (list, tuple)) else [a]
    if mode == "tn":
        assert not has_acc
        K, N = b.shape
        widths = [t.shape[1] for t in parts]
        M = sum(widths)
        whole = M * N * 4 <= MM_OUT_BYTES
        assert whole or len(parts) == 1
        tmm = M if whole else M // 2
        ts = _pick(K, 4 * MM_ROWS)
        nk = K // ts

        npart = len(parts)
        narrow = out_dtype != f32

        def body_tn(*refs):
            b_ref, o_ref = refs[npart], refs[npart + 1]
            acc_ref = refs[npart + 2] if narrow else o_ref
            k = pl.program_id(1)
            bv = b_ref[...]
            lo = 0
            for a_ref, w in zip(refs[:npart], widths if whole else [tmm]):
                part = lax.dot_general(a_ref[...], bv, dims, preferred_element_type=f32)
                rows = slice(lo, lo + w)
                lo += w

                @pl.when(k == 0)
                def _(part=part, rows=rows):
                    acc_ref[rows, :] = part

                @pl.when(k > 0)
                def _(part=part, rows=rows):
                    acc_ref[rows, :] += part

            if narrow:
                @pl.when(k == nk - 1)
                def _():
                    o_ref[...] = acc_ref[...].astype(out_dtype)

        return pl.pallas_call(
            body_tn,
            grid=(M // tmm, nk),
            in_specs=[pl.BlockSpec((ts, w if whole else tmm), lambda i, k: (k, i)) for w in widths]
            + [pl.BlockSpec((ts, N), lambda i, k: (k, 0))],
            out_specs=pl.BlockSpec((tmm, N), lambda i, k: (i, 0)),
            out_shape=SDS((M, N), out_dtype),
            scratch_shapes=[pltpu.VMEM((tmm, N), f32)] if narrow else [],
            compiler_params=_cparams(("parallel", "arbitrary"), VMEM_BIG),
            name=name,
        )(*parts, b)

    bs = list(b) if isinstance(b, (list, tuple)) else [b]
    widths = [t.shape[1] for t in parts]
    M = parts[0].shape[0]
    kdim = 0 if mode == "nn" else 1
    N = bs[0].shape[1 - kdim]
    tm = _pick(M, MM_ROWS)
    npart, nb = len(parts), len(bs)
    place, bi, lo = [], 0, 0
    for w in widths:
        place.append((bi, lo))
        lo += w
        if lo == bs[bi].shape[kdim]:
            bi, lo = bi + 1, 0
    assert bi == nb and lo == 0

    def body(*refs):
        a_refs, b_refs = refs[:npart], refs[npart:npart + nb]
        c_ref = refs[npart + nb] if has_acc else None
        o_ref = refs[-1]
        part = None
        for a_ref, w, (bi, lo) in zip(a_refs, widths, place):
            b_ref = b_refs[bi]
            if w == bs[bi].shape[kdim]:
                bk = b_ref[...]
            else:
                bk = b_ref[:, lo:lo + w] if mode == "nt" else b_ref[lo:lo + w, :]
            t = lax.dot_general(a_ref[...], bk, dims, preferred_element_type=f32)
            part = t if part is None else part + t
        if has_acc:
            part = part + c_ref[...]
        o_ref[...] = part.astype(out_dtype)

    specs = [pl.BlockSpec((tm, w), lambda i: (i, 0)) for w in widths] \
        + [pl.BlockSpec(t.shape, lambda i: (0, 0)) for t in bs]
    args = parts + bs
    aliases = {}
    if has_acc:
        specs.append(pl.BlockSpec((tm, N), lambda i: (i, 0)))
        args.append(acc)
        aliases = {npart + nb: 0}
    if after is not None:
        specs.append(pl.BlockSpec(memory_space=pl.ANY))
        args.append(after)
    return pl.pallas_call(
        body,
        grid=(M // tm,),
        in_specs=specs,
        out_specs=pl.BlockSpec((tm, N), lambda i: (i, 0)),
        out_shape=SDS((M, N), out_dtype),
        input_output_aliases=aliases,
        compiler_params=_cparams(("parallel",), VMEM_BIG),
        name=name,
    )(*args)


def _mm_fanout(a, bs, mode, out_dtypes, name):
    dims = {"nn": NN, "nt": NT}[mode]
    M, K = a.shape
    ns = [b.shape[1] if mode == "nn" else b.shape[0] for b in bs]
    tm = _pick(M, MM_ROWS)
    nb = len(bs)

    def body(a_ref, *refs):
        av = a_ref[...]
        for b_ref, o_ref, dt in zip(refs[:nb], refs[nb:], out_dtypes):
            o_ref[...] = lax.dot_general(av, b_ref[...], dims, preferred_element_type=f32).astype(dt)

    return pl.pallas_call(
        body,
        grid=(M // tm,),
        in_specs=[pl.BlockSpec((tm, K), lambda i: (i, 0))] + [pl.BlockSpec(b.shape, lambda i: (0, 0)) for b in bs],
        out_specs=[pl.BlockSpec((tm, n), lambda i: (i, 0)) for n in ns],
        out_shape=[SDS((M, n), dt) for n, dt in zip(ns, out_dtypes)],
        compiler_params=_cparams(("parallel",), VMEM_BIG),
        name=name,
    )(a, *bs)


PERM_ROWS = 2048


def _perm_spec(d, cols=LANE):
    return pl.BlockSpec((d, PERM_ROWS // d, cols), lambda i, j: (0, i, j))


def _to_natural(src_ref, dst_ref, d):
    n = src_ref.shape[1]
    for r in range(d):
        dst_ref[pl.ds(r, n, stride=d), :] = src_ref[r]


def _prep(x, w, after=None):
    S, D = x.shape
    R = PERM_ROWS
    nc = D // LANE
    n_in = nc + 1 + (after is not None)

    def body(*refs):
        x_refs, w_ref = refs[:nc], refs[nc]
        h_ref, h4_ref, h16_ref, rs = refs[n_in:]
        ssq = None
        for xr in x_refs:
            v = xr[...]
            t = jnp.sum(v * v, axis=-1, keepdims=True)
            ssq = t if ssq is None else ssq + t
        rinv = lax.rsqrt(ssq * (1.0 / D) + EPS)
        rs[...] = jnp.broadcast_to(rinv, (R, LANE))
        for j, xr in enumerate(x_refs):
            cols = slice(j * LANE, (j + 1) * LANE)
            wj = w_ref[:, cols]
            h_ref[:, cols] = ((xr[...] * rinv) * wj).astype(bf16)
            for d, o_ref in ((4, h4_ref), (16, h16_ref)):
                n = R // d
                for r in range(d):
                    rows = pl.ds(r, n, stride=d)
                    o_ref[r, :, cols] = ((xr[rows, :] * rs[rows, :]) * wj).astype(bf16)

    col = lambda j: pl.BlockSpec((R, LANE), lambda i, j=j: (i, j))
    h, h4, h16 = pl.pallas_call(
        body,
        grid=(S // R,),
        in_specs=[col(j) for j in range(nc)] + [pl.BlockSpec((1, D), lambda i: (0, 0))]
        + ([] if after is None else [pl.BlockSpec(memory_space=pl.ANY)]),
        out_specs=[pl.BlockSpec((R, D), lambda i: (i, 0)), pl.BlockSpec((4, R // 4, D), lambda i: (0, i, 0)),
                   pl.BlockSpec((16, R // 16, D), lambda i: (0, i, 0))],
        out_shape=[SDS((S, D), bf16), SDS((4, S // 4, D), bf16), SDS((16, S // 16, D), bf16)],
        scratch_shapes=[pltpu.VMEM((R, LANE), f32)],
        compiler_params=_cparams(("parallel",), VMEM_BIG),
        name="prep_norm_perm",
    )(*([x] * nc), w, *([] if after is None else [after]))
    return [h, h4.reshape(S, D), h16.reshape(S, D)]


def _rms_parts(xv):
    r = lax.rsqrt(jnp.mean(xv * xv, axis=-1, keepdims=True) + EPS)
    return r, xv * r


def _rms_bwd(xhat, r, w, dy):
    dyw = dy * w
    return r * (dyw - xhat * jnp.mean(dyw * xhat, axis=-1, keepdims=True))


def _mid_fwd(x, merged, w_out, w_pm, w_pf):
    S, D = x.shape
    tm = _pick(S, MM_ROWS)

    def body(x_ref, m_ref, wo_ref, wpm_ref, wpf_ref, mo_ref, x1_ref, h2_ref):
        mo = jnp.dot(m_ref[...], wo_ref[...], preferred_element_type=f32)
        mo_ref[...] = mo
        _, moh = _rms_parts(mo)
        x1 = x_ref[...] + moh * wpm_ref[...]
        x1_ref[...] = x1
        _, x1h = _rms_parts(x1)
        h2_ref[...] = (x1h * wpf_ref[...]).astype(bf16)

    row = pl.BlockSpec((tm, D), lambda i: (i, 0))
    vec = pl.BlockSpec((1, D), lambda i: (0, 0))
    return pl.pallas_call(
        body,
        grid=(S // tm,),
        in_specs=[row, pl.BlockSpec((tm, merged.shape[1]), lambda i: (i, 0)),
                  pl.BlockSpec(w_out.shape, lambda i: (0, 0)), vec, vec],
        out_specs=[row, row, row],
        out_shape=[SDS((S, D), f32), SDS((S, D), f32), SDS((S, D), bf16)],
        compiler_params=_cparams(("parallel",), VMEM_BIG),
        name="out_proj_mid_fwd",
    )(x, merged, w_out, w_pm, w_pf)


def _final(x1, act, w_down, tgt, w_pfn):
    S, D = x1.shape
    tm = _pick(S, MM_ROWS)
    nt = S // tm

    def body(x1_ref, a_ref, wd_ref, t_ref, w_ref, loss_ref, dy_ref, dfo_ref, gw_ref, lacc, gacc):
        i = pl.program_id(0)

        @pl.when(i == 0)
        def _():
            lacc[...] = jnp.zeros_like(lacc)
            gacc[...] = jnp.zeros_like(gacc)

        w = w_ref[...]
        r, foh = _rms_parts(jnp.dot(a_ref[...], wd_ref[...], preferred_element_type=f32))
        y = x1_ref[...] + foh * w
        err = y - t_ref[...]
        lacc[...] += _colsum8(err * err)
        dy = err * (1.0 / D)
        dy_ref[...] = dy
        gacc[...] += _colsum8(dy * foh)
        dfo_ref[...] = _rms_bwd(foh, r, w, dy).astype(bf16)

        @pl.when(i == nt - 1)
        def _():
            loss_ref[...] = jnp.full((SUBLANE, LANE), 0.5 / D, f32) * jnp.sum(lacc[...])
            gw_ref[...] = jnp.sum(gacc[...], axis=0, keepdims=True)

    row = pl.BlockSpec((tm, D), lambda i: (i, 0))
    vec = pl.BlockSpec((1, D), lambda i: (0, 0))
    return pl.pallas_call(
        body,
        grid=(nt,),
        in_specs=[row, pl.BlockSpec((tm, act.shape[1]), lambda i: (i, 0)),
                  pl.BlockSpec(w_down.shape, lambda i: (0, 0)), row, vec],
        out_specs=[pl.BlockSpec((SUBLANE, LANE), lambda i: (0, 0)), row, row, vec],
        out_shape=[SDS((SUBLANE, LANE), f32), SDS((S, D), f32), SDS((S, D), bf16), SDS((1, D), f32)],
        scratch_shapes=[pltpu.VMEM((SUBLANE, D), f32), pltpu.VMEM((SUBLANE, D), f32)],
        compiler_params=_cparams(("arbitrary",), VMEM_BIG),
        name="down_proj_final_loss",
    )(x1, act, w_down, tgt, w_pfn)


MID_BWD_ROWS = 256


def _mid_bwd(dy, dug, duv, wt_g, wt_v, x1, mo, w_pf, w_pm):
    S, D = dy.shape
    tm = _pick(S, MID_BWD_ROWS)
    nt = S // tm

    def body(dy_ref, dug_ref, duv_ref, wg_ref, wv_ref, x1_ref, mo_ref, wpf_ref, wpm_ref,
             dx1_ref, dmo_ref, gpf_ref, gpm_ref, apf, apm):
        i = pl.program_id(0)

        @pl.when(i == 0)
        def _():
            apf[...] = jnp.zeros_like(apf)
            apm[...] = jnp.zeros_like(apm)

        r1, x1h = _rms_parts(x1_ref[...])
        dh2 = jnp.dot(dug_ref[...], wg_ref[...], preferred_element_type=f32) \
            + jnp.dot(duv_ref[...], wv_ref[...], preferred_element_type=f32)
        apf[...] += _colsum8(dh2 * x1h)
        dx1 = dy_ref[...] + _rms_bwd(x1h, r1, wpf_ref[...], dh2)
        dx1_ref[...] = dx1
        rm, moh = _rms_parts(mo_ref[...])
        apm[...] += _colsum8(dx1 * moh)
        dmo_ref[...] = _rms_bwd(moh, rm, wpm_ref[...], dx1).astype(bf16)

        @pl.when(i == nt - 1)
        def _():
            gpf_ref[...] = jnp.sum(apf[...], axis=0, keepdims=True)
            gpm_ref[...] = jnp.sum(apm[...], axis=0, keepdims=True)

    row = pl.BlockSpec((tm, D), lambda i: (i, 0))
    vec = pl.BlockSpec((1, D), lambda i: (0, 0))
    return pl.pallas_call(
        body,
        grid=(nt,),
        in_specs=[row, pl.BlockSpec((tm, dug.shape[1]), lambda i: (i, 0)), pl.BlockSpec((tm, duv.shape[1]), lambda i: (i, 0)),
                  pl.BlockSpec(wt_g.shape, lambda i: (0, 0)), pl.BlockSpec(wt_v.shape, lambda i: (0, 0)),
                  row, row, vec, vec],
        out_specs=[row, row, vec, vec],
        out_shape=[SDS((S, D), f32), SDS((S, D), bf16), SDS((1, D), f32), SDS((1, D), f32)],
        scratch_shapes=[pltpu.VMEM((SUBLANE, D), f32), pltpu.VMEM((SUBLANE, D), f32)],
        compiler_params=_cparams(("arbitrary",), VMEM_BIG),
        name="dh2_mid_bwd",
    )(dy, dug, duv, wt_g, wt_v, x1, mo, w_pf, w_pm)


def _first_bwd(x, dx1, dh_a, dh_b, dh_c, w_pre):
    S, D = x.shape
    tm = _pick(S, 512)
    nt = S // tm
    nc = D // LANE

    def body(*refs):
        x_ref, dx1_ref, a_ref = refs[:3]
        b_refs, c_refs, w_ref = refs[3:3 + nc], refs[3 + nc:3 + 2 * nc], refs[3 + 2 * nc]
        gx_ref, gw_ref, acc, dh_s, sb, sc = refs[4 + 2 * nc:]
        i = pl.program_id(0)

        @pl.when(i == 0)
        def _():
            acc[...] = jnp.zeros_like(acc)

        for j in range(nc):
            cols = slice(j * LANE, (j + 1) * LANE)
            _to_natural(b_refs[j], sb, 4)
            _to_natural(c_refs[j], sc, 16)
            dh_s[:, cols] = (a_ref[:, cols] + sb[...]) + sc[...]
        r, xh = _rms_parts(x_ref[...])
        dh = dh_s[...]
        acc[...] += _colsum8(dh * xh)
        gx_ref[...] = dx1_ref[...] + _rms_bwd(xh, r, w_ref[...], dh)

        @pl.when(i == nt - 1)
        def _():
            gw_ref[...] = jnp.sum(acc[...], axis=0, keepdims=True)

    row = pl.BlockSpec((tm, D), lambda i: (i, 0))
    vec = pl.BlockSpec((1, D), lambda i: (0, 0))
    perm = lambda d: [pl.BlockSpec((d, tm // d, LANE), lambda i, j=j: (0, i, j)) for j in range(nc)]
    return pl.pallas_call(
        body,
        grid=(nt,),
        in_specs=[row, row, row] + perm(4) + perm(16) + [vec],
        out_specs=[row, vec],
        out_shape=[SDS((S, D), f32), SDS((1, D), f32)],
        scratch_shapes=[pltpu.VMEM((SUBLANE, D), f32), pltpu.VMEM((tm, D), f32), pltpu.VMEM((tm, LANE), f32),
                        pltpu.VMEM((tm, LANE), f32)],
        compiler_params=_cparams(("arbitrary",), VMEM_BIG),
        name="first_bwd",
    )(x, dx1, dh_a, *([dh_b.reshape(4, S // 4, D)] * nc), *([dh_c.reshape(16, S // 16, D)] * nc), w_pre)


def _t5_bucket(dist):
    n = jnp.maximum(dist, 0)
    nf = jnp.maximum(n, 1).astype(f32)
    large = MAX_EXACT + (jnp.log(nf / MAX_EXACT) / math.log(MAX_DISTANCE / MAX_EXACT)
                         * (NUM_BUCKETS - MAX_EXACT)).astype(jnp.int32)
    large = jnp.minimum(large, NUM_BUCKETS - 1)
    return jnp.where(n < MAX_EXACT, n, large)


def _bias_consts(d):
    blk = ATTN_BLOCK
    rel = jnp.arange(blk)[:, None] + blk - jnp.arange(2 * blk)[None, :]
    in_win = (rel >= 0) & (rel <= blk)
    bucket = _t5_bucket(rel * d).reshape(1, -1)
    onehot = (bucket == jnp.arange(NUM_BUCKETS)[:, None]).astype(f32)
    return onehot, in_win.astype(f32).reshape(1, -1)


def _bias_build(tab_t, onehot, maskf, name, after):
    H = tab_t.shape[0]

    def body(t_ref, oh_ref, m_ref, after_ref, o_ref):
        b = jnp.dot(t_ref[...], oh_ref[...], precision=HIGHEST, preferred_element_type=f32)
        o_ref[...] = jnp.where(m_ref[...] > 0.5, b, NEG_INF)

    vm = pl.BlockSpec(memory_space=pltpu.VMEM)
    return pl.pallas_call(body, out_shape=SDS((H, onehot.shape[1]), f32), name=name,
                          in_specs=[vm, vm, vm, pl.BlockSpec(memory_space=pl.ANY)], out_specs=vm,
                          )(tab_t, onehot, maskf, after)


def _bias_grad(dbias_flat, onehot, name):
    H = dbias_flat.shape[0]

    def body(g_ref, oh_ref, o_ref):
        o_ref[...] = lax.dot_general(oh_ref[...], g_ref[...], NT, precision=HIGHEST, preferred_element_type=f32)

    return pl.pallas_call(body, out_shape=SDS((NUM_BUCKETS, H), f32), name=name)(dbias_flat, onehot)


ATTN_TILE = 512
ATTN_SUB = ATTN_TILE // ATTN_BLOCK
ATTN_HP = 4
ATTN_WIDE = ATTN_HP * LANE


def _qkv_specs(nt):
    tile = (ATTN_TILE, ATTN_WIDE)
    blk = (ATTN_BLOCK, ATTN_WIDE)
    sec = ATTN_OUT // ATTN_WIDE
    cur = lambda off: (lambda h, t: (jnp.minimum(t, nt - 1), off + h))
    prev = lambda off: (lambda h, t: (jnp.maximum(jnp.minimum(t, nt - 1) * ATTN_SUB - 1, 0), off + h))
    return [pl.BlockSpec(tile, cur(0)), pl.BlockSpec(blk, prev(sec)), pl.BlockSpec(tile, cur(sec)),
            pl.BlockSpec(blk, prev(2 * sec)), pl.BlockSpec(tile, cur(2 * sec))]


def _head_masks():
    lane = lax.broadcasted_iota(jnp.int32, (ATTN_BLOCK, LANE), 1)
    return lane < HEAD_DIM


def _stack_heads(x2, low):
    zero = jnp.zeros_like(x2)
    return jnp.concatenate([jnp.where(low, x2, zero), jnp.where(low, zero, x2)], axis=0)


def _attn_fwd(qkv, bias, bps, name, after=None):
    S = qkv.shape[0]
    nt = S // ATTN_TILE
    scale = HEAD_DIM ** -0.5

    def body(q_ref, kp_ref, kc_ref, vp_ref, vc_ref, b_ref, *rest):
        o_ref, l_ref = rest[-2:]
        t = pl.program_id(1)
        low = _head_masks()
        col = lax.broadcasted_iota(jnp.int32, (2 * ATTN_BLOCK, 2 * ATTN_BLOCK), 1)
        for hp in range(ATTN_HP):
            cols = slice(hp * LANE, (hp + 1) * LANE)
            kk = jnp.concatenate([kp_ref[:, cols], kc_ref[:, cols]], axis=0)
            vv = jnp.concatenate([vp_ref[:, cols], vc_ref[:, cols]], axis=0)
            bias2 = b_ref[2 * hp:2 * hp + 2].reshape(2 * ATTN_BLOCK, 2 * ATTN_BLOCK)
            for b in range(ATTN_SUB):
                lo = b * ATTN_BLOCK
                rows = slice(lo, lo + ATTN_BLOCK)
                keys = slice(lo, lo + 2 * ATTN_BLOCK)
                dead = jnp.logical_and((t * ATTN_SUB + b) % bps == 0, col < ATTN_BLOCK)
                q2 = _stack_heads(q_ref[rows, cols], low)
                kb, vb = kk[keys], vv[keys]
                s = lax.dot_general(q2, kb, NT, preferred_element_type=f32) * scale + bias2
                s = jnp.where(dead, NEG_INF, s)
                m = jnp.max(s, axis=-1, keepdims=True)
                p = jnp.exp(s - m)
                l = jnp.sum(p, axis=-1, keepdims=True)
                o2 = jnp.dot(p.astype(bf16), vb, preferred_element_type=f32) / l
                lse = m + jnp.log(l)
                o_ref[rows, cols] = jnp.where(low, o2[:ATTN_BLOCK], o2[ATTN_BLOCK:])
                l_ref[rows, cols] = jnp.where(low, lse[:ATTN_BLOCK], lse[ATTN_BLOCK:])

    tile = pl.BlockSpec((ATTN_TILE, ATTN_WIDE), lambda h, t: (t, h))
    return pl.pallas_call(
        body,
        grid=(4 // ATTN_HP, nt),
        in_specs=_qkv_specs(nt) + [pl.BlockSpec((2 * ATTN_HP, ATTN_BLOCK, 2 * ATTN_BLOCK), lambda h, t: (h, 0, 0))]
        + ([] if after is None else [pl.BlockSpec(memory_space=pl.ANY)]),
        out_specs=[tile, tile],
        out_shape=[SDS((S, ATTN_OUT), f32), SDS((S, ATTN_OUT), f32)],
        compiler_params=_cparams(("parallel", "parallel")),
        name=name,
    )(qkv, qkv, qkv, qkv, qkv, bias, *([] if after is None else [after]))


def _attn_bwd(qkv, bias, do, dvec, lse, bps, name):
    S = qkv.shape[0]
    nt = S // ATTN_TILE
    scale = HEAD_DIM ** -0.5

    def assemble(parts):
        rows = [parts[0][:ATTN_BLOCK]]
        for b in range(ATTN_SUB - 1):
            rows.append(parts[b][ATTN_BLOCK:] + parts[b + 1][:ATTN_BLOCK])
        rows.append(parts[-1][ATTN_BLOCK:])
        return rows

    def body(q_ref, kp_ref, kc_ref, vp_ref, vc_ref, b_ref, do_ref, dvec_ref, lse_ref,
             dq_ref, dk_ref, dv_ref, db_ref, ck, cv):
        t = pl.program_id(1)
        last = ATTN_TILE - ATTN_BLOCK

        @pl.when(t == 0)
        def _():
            ck[...] = jnp.zeros_like(ck)
            cv[...] = jnp.zeros_like(cv)
            db_ref[...] = jnp.zeros_like(db_ref)

        @pl.when(t < nt)
        def _():
            low = _head_masks()
            col = lax.broadcasted_iota(jnp.int32, (2 * ATTN_BLOCK, 2 * ATTN_BLOCK), 1)
            per_row = lambda t2: jnp.concatenate([t2[:, 0:1], t2[:, HEAD_DIM:HEAD_DIM + 1]], axis=0)
            for hp in range(ATTN_HP):
                cols = slice(hp * LANE, (hp + 1) * LANE)
                kk = jnp.concatenate([kp_ref[:, cols], kc_ref[:, cols]], axis=0)
                vv = jnp.concatenate([vp_ref[:, cols], vc_ref[:, cols]], axis=0)
                bias2 = b_ref[2 * hp:2 * hp + 2].reshape(2 * ATTN_BLOCK, 2 * ATTN_BLOCK)
                dk_parts, dv_parts = [], []
                dsum = None
                for b in range(ATTN_SUB):
                    lo = b * ATTN_BLOCK
                    rows = slice(lo, lo + ATTN_BLOCK)
                    keys = slice(lo, lo + 2 * ATTN_BLOCK)
                    dead = jnp.logical_and((t * ATTN_SUB + b) % bps == 0, col < ATTN_BLOCK)
                    q2 = _stack_heads(q_ref[rows, cols], low)
                    do2 = _stack_heads(do_ref[rows, cols].astype(bf16), low)
                    kb, vb = kk[keys], vv[keys]
                    s = lax.dot_general(q2, kb, NT, preferred_element_type=f32) * scale + bias2
                    s = jnp.where(dead, NEG_INF, s)
                    p = jnp.exp(s - per_row(lse_ref[rows, cols]))
                    dp = lax.dot_general(do2, vb, NT, preferred_element_type=f32)
                    ds = p * (dp - per_row(dvec_ref[rows, cols]))
                    dsum = ds if dsum is None else dsum + ds
                    dsb = ds.astype(bf16)
                    dq2 = jnp.dot(dsb, kb, preferred_element_type=f32) * scale
                    dq_ref[rows, cols] = jnp.where(low, dq2[:ATTN_BLOCK], dq2[ATTN_BLOCK:]).astype(bf16)
                    dk_parts.append(lax.dot_general(dsb, q2, TN, preferred_element_type=f32) * scale)
                    dv_parts.append(lax.dot_general(p.astype(bf16), do2, TN, preferred_element_type=f32))
                db_ref[2 * hp:2 * hp + 2] += dsum.reshape(2, ATTN_BLOCK, 2 * ATTN_BLOCK)
                for parts, carry, out_ref in ((dk_parts, ck, dk_ref), (dv_parts, cv, dv_ref)):
                    rws = assemble(parts)
                    out_ref[:last, cols] = carry[:last, cols].astype(bf16)
                    out_ref[last:, cols] = (carry[last:, cols] + rws[0]).astype(bf16)
                    for b in range(ATTN_SUB):
                        carry[b * ATTN_BLOCK:(b + 1) * ATTN_BLOCK, cols] = rws[b + 1]

        @pl.when(t == nt)
        def _():
            dk_ref[...] = ck[...].astype(bf16)
            dv_ref[...] = cv[...].astype(bf16)

    tile = (ATTN_TILE, ATTN_WIDE)
    cur = pl.BlockSpec(tile, lambda h, t: (jnp.minimum(t, nt - 1), h))
    lag = pl.BlockSpec(tile, lambda h, t: (jnp.maximum(t - 1, 0), h))
    bspec = pl.BlockSpec((2 * ATTN_HP, ATTN_BLOCK, 2 * ATTN_BLOCK), lambda h, t: (h, 0, 0))
    return pl.pallas_call(
        body,
        grid=(4 // ATTN_HP, nt + 1),
        in_specs=_qkv_specs(nt) + [bspec, cur, cur, cur],
        out_specs=[cur, lag, lag, bspec],
        out_shape=[SDS((S, ATTN_OUT), bf16), SDS((S, ATTN_OUT), bf16), SDS((S, ATTN_OUT), bf16),
                   SDS((8, ATTN_BLOCK, 2 * ATTN_BLOCK), f32)],
        scratch_shapes=[pltpu.VMEM(tile, f32), pltpu.VMEM(tile, f32)],
        compiler_params=_cparams(("parallel", "arbitrary")),
        name=name,
    )(qkv, qkv, qkv, qkv, qkv, bias, do, dvec, lse)


def _attn_merge(o0, o1, o2, l0, l1, l2):
    S, W = o0.shape
    R = PERM_ROWS

    def body(o0_ref, o1_ref, o2_ref, l0_ref, l1_ref, l2_ref, y_ref, yb_ref, w0_ref, w1_ref, w2_ref,
             so1, so2, sl1, sl2):
        _to_natural(o1_ref, so1, 4)
        _to_natural(l1_ref, sl1, 4)
        _to_natural(o2_ref, so2, 16)
        _to_natural(l2_ref, sl2, 16)
        a, b, c = l0_ref[...], sl1[...], sl2[...]
        m = jnp.maximum(jnp.maximum(a, b), c)
        ea, eb, ec = jnp.exp(a - m), jnp.exp(b - m), jnp.exp(c - m)
        den = (ea + eb) + ec
        w0, w1, w2 = ea / den, eb / den, ec / den
        y = (w0 * o0_ref[...] + w1 * so1[...]) + w2 * so2[...]
        y_ref[...] = y
        yb_ref[...] = y.astype(bf16)
        w0_ref[...] = w0
        w1_ref[...] = w1
        w2_ref[...] = w2

    nat = pl.BlockSpec((R, LANE), lambda i, j: (i, j))
    v4 = lambda t: t.reshape(4, S // 4, W)
    v16 = lambda t: t.reshape(16, S // 16, W)
    return pl.pallas_call(
        body,
        grid=(S // R, W // LANE),
        in_specs=[nat, _perm_spec(4), _perm_spec(16)] * 2,
        out_specs=[nat] * 5,
        out_shape=[SDS((S, W), f32), SDS((S, W), bf16)] + [SDS((S, W), f32)] * 3,
        scratch_shapes=[pltpu.VMEM((R, LANE), f32)] * 4,
        compiler_params=_cparams(("parallel", "parallel"), VMEM_BIG),
        name="attn_merge",
    )(o0, v4(o1), v16(o2), l0, v4(l1), v16(l2))


def _attn_merge_bwd(dy, y, w0, w1, w2, after=None):
    S, W = dy.shape
    R = PERM_ROWS

    def body(dy_ref, y_ref, w0_ref, w1_ref, w2_ref, *rest):
        a0, a1, a2, b0, b1, b2, sa, sb = rest[-8:]
        dyv = dy_ref[...]
        r = lax.broadcasted_iota(jnp.int32, (LANE, LANE), 0) // HEAD_DIM
        c = lax.broadcasted_iota(jnp.int32, (LANE, LANE), 1) // HEAD_DIM
        seg = jnp.where(r == c, 1.0, 0.0).astype(f32)
        cbar = jnp.dot(dyv * y_ref[...], seg, precision=HIGHEST, preferred_element_type=f32)
        w = w0_ref[...]
        a0[...] = (w * dyv).astype(bf16)
        b0[...] = w * cbar
        for d, w_ref, a_ref, b_ref in ((4, w1_ref, a1, b1), (16, w2_ref, a2, b2)):
            w = w_ref[...]
            sa[...] = w * dyv
            sb[...] = w * cbar
            n = R // d
            for k in range(d):
                rows = pl.ds(k, n, stride=d)
                a_ref[k] = sa[rows, :].astype(bf16)
                b_ref[k] = sb[rows, :]

    nat = pl.BlockSpec((R, LANE), lambda i, j: (i, j))
    shapes = lambda dt: [SDS((S, W), dt), SDS((4, S // 4, W), dt), SDS((16, S // 16, W), dt)]
    outs = pl.pallas_call(
        body,
        grid=(S // R, W // LANE),
        in_specs=[nat] * 5 + ([] if after is None else [pl.BlockSpec(memory_space=pl.ANY)]),
        out_specs=[nat, _perm_spec(4), _perm_spec(16)] * 2,
        out_shape=shapes(bf16) + shapes(f32),
        scratch_shapes=[pltpu.VMEM((R, LANE), f32)] * 2,
        compiler_params=_cparams(("parallel", "parallel"), VMEM_BIG),
        name="attn_merge_bwd",
    )(dy, y, w0, w1, w2, *([] if after is None else [after]))
    return [t.reshape(S, W) for t in outs]


HGRN_SB = 256
HGRN_PAIR = 4


def _chunk_masks():
    r = jnp.arange(HGRN_SB)[:, None]
    c = jnp.arange(HGRN_SB)[None, :]
    same = (r // HGRN_CHUNK) == (c // HGRN_CHUNK)
    return jnp.stack([same & (c <= r), same, same & (c >= r)]).astype(bf16)


def _mask_dot(mask, x):
    hi = x.astype(bf16)
    r1 = x - hi.astype(f32)
    mid = r1.astype(bf16)
    lo = (r1 - mid.astype(f32)).astype(bf16)
    p = jnp.dot(mask, jnp.concatenate([hi, mid, lo], axis=1), preferred_element_type=f32)
    n = x.shape[1]
    return (p[:, :n] + p[:, n:2 * n]) + p[:, 2 * n:]


def _hgrn_prep(q_raw, f_raw, lbv, tril, same):
    sq = _sigmoid(q_raw)
    qs = q_raw * sq
    sig = _sigmoid(f_raw)
    f = lbv + (1.0 - lbv) * sig
    g = jnp.log(f)
    k = 1.0 - f
    G = _mask_dot(tril, g)
    GL = _mask_dot(same, g)
    eG = jnp.exp(G)
    einv = jnp.exp(-G)
    edec = jnp.exp(GL - G)
    return dict(sq=sq, qs=qs, sig=sig, f=f, k=k, eG=eG, einv=einv, edec=edec, eGL=jnp.exp(GL),
                qt=qs * eG, kt=k * einv, kd=k * edec)


def _hgrn_fwd(hg, lb, normw):
    S = hg.shape[0]
    sb = HGRN_SB
    nsb = S // sb
    nch = sb // HGRN_CHUNK

    def body(q_ref, f_ref, v_ref, og_ref, lb_ref, nw_ref, m_ref, y_ref, o_ref, ck_ref, st):
        j = pl.program_id(1)

        @pl.when(j == 0)
        def _():
            st[...] = jnp.zeros_like(st)

        tril_m = m_ref[0]
        tril = tril_m.astype(f32) > 0.5

        def one_head(hh):
            cols = slice(hh * LANE, (hh + 1) * LANE)
            ST = st[hh]
            ck_ref[hh, 0] = ST
            pr = _hgrn_prep(q_ref[:, cols], f_ref[:, cols], lb_ref[:, cols], tril_m, m_ref[1])
            qtb, ktb, kdb = pr["qt"].astype(bf16), pr["kt"].astype(bf16), pr["kd"].astype(bf16)
            eGL = pr["eGL"]
            vb = v_ref[:, cols].astype(bf16)
            A = jnp.where(tril, lax.dot_general(qtb, ktb, NT, preferred_element_type=f32), 0.0)
            o = jnp.dot(A.astype(bf16), vb, preferred_element_type=f32)
            outs = []
            for ci in range(nch):
                lo = ci * HGRN_CHUNK
                sl = slice(lo, lo + HGRN_CHUNK)
                outs.append(o[sl] + lax.dot_general(qtb[sl], ST.astype(bf16), NT, preferred_element_type=f32))
                ST = ST * eGL[lo:lo + 1, :] + lax.dot_general(vb[sl], kdb[sl], TN, preferred_element_type=f32)
            st[hh] = ST
            of = jnp.concatenate(outs, axis=0)
            o_ref[:, cols] = of
            rms = lax.rsqrt(jnp.mean(of * of, axis=-1, keepdims=True) + EPS)
            ogv = og_ref[:, cols]
            y_ref[:, cols] = ((of * rms * nw_ref[...]) * (ogv * _sigmoid(ogv))).astype(bf16)

        for hh in range(HGRN_PAIR):
            one_head(hh)

    wide = HGRN_PAIR * LANE
    col = lambda off: pl.BlockSpec((sb, wide), lambda h, j: (j, off // HGRN_PAIR + h))
    return pl.pallas_call(
        body,
        grid=(4 // HGRN_PAIR, nsb),
        in_specs=[col(0), col(4), col(8), col(12), pl.BlockSpec((1, wide), lambda h, j: (0, h)),
                  pl.BlockSpec((1, LANE), lambda h, j: (0, 0)),
                  pl.BlockSpec((3, sb, sb), lambda h, j: (0, 0, 0))],
        out_specs=[col(0), col(0), pl.BlockSpec((HGRN_PAIR, 1, LANE, LANE), lambda h, j: (h, j, 0, 0))],
        out_shape=[SDS((S, HGRN_W), bf16), SDS((S, HGRN_W), f32), SDS((4, nsb, LANE, LANE), f32)],
        scratch_shapes=[pltpu.VMEM((HGRN_PAIR, LANE, LANE), f32)],
        compiler_params=_cparams(("parallel", "arbitrary")),
        name="hgrn_fwd",
    )(hg, hg, hg, hg, lb, normw, _chunk_masks())


def _hgrn_bwd(hg, o_raw, dy, ck, lb, normw, after=None):
    S = hg.shape[0]
    sb = HGRN_SB
    nsb = S // sb
    nch = sb // HGRN_CHUNK

    def body(q_ref, f_ref, v_ref, og_ref, o_ref, dy_ref, ck_ref, lb_ref, nw_ref, m_ref, *rest):
        dq_ref, df_ref, dv_ref, dog_ref, glb_ref, gnw_ref, dst, alb, anw = rest[-9:]
        j = pl.program_id(1)

        @pl.when(j == 0)
        def _():
            dst[...] = jnp.zeros_like(dst)
            alb[...] = jnp.zeros_like(alb)
            anw[...] = jnp.zeros_like(anw)

        tril_m = m_ref[0]
        tril = tril_m.astype(f32) > 0.5
        nw = nw_ref[...]

        def one_head(hh):
            cols = slice(hh * LANE, (hh + 1) * LANE)
            lbv = lb_ref[:, cols]
            q_raw = q_ref[:, cols]
            pr = _hgrn_prep(q_raw, f_ref[:, cols], lbv, tril_m, m_ref[1])
            qt, kt, kd, eGL = pr["qt"], pr["kt"], pr["kd"], pr["eGL"]
            qtb, ktb, kdb = qt.astype(bf16), kt.astype(bf16), kd.astype(bf16)
            vb = v_ref[:, cols].astype(bf16)

            o = o_ref[:, cols]
            ogv = og_ref[:, cols]
            sog = _sigmoid(ogv)
            rms = lax.rsqrt(jnp.mean(o * o, axis=-1, keepdims=True) + EPS)
            oh = o * rms
            dyv = dy_ref[:, cols]
            dog_ref[:, cols] = (dyv * (oh * nw) * (sog * (1.0 + ogv * (1.0 - sog)))).astype(bf16)
            dohw = dyv * (ogv * sog)
            anw[:, cols] += _colsum8(dohw * oh)
            doh = dohw * nw
            do = rms * (doh - oh * jnp.mean(doh * oh, axis=-1, keepdims=True))
            dob = do.astype(bf16)

            Ab = jnp.where(tril, lax.dot_general(qtb, ktb, NT, preferred_element_type=f32), 0.0).astype(bf16)
            dAb = jnp.where(tril, lax.dot_general(dob, vb, NT, preferred_element_type=f32), 0.0).astype(bf16)
            dv_acc = lax.dot_general(Ab, dob, TN, preferred_element_type=f32)
            dqt = jnp.dot(dAb, ktb, preferred_element_type=f32)
            dkt = lax.dot_general(dAb, qtb, TN, preferred_element_type=f32)

            ST = ck_ref[hh, 0]
            states = []
            for ci in range(nch):
                lo = ci * HGRN_CHUNK
                sl = slice(lo, lo + HGRN_CHUNK)
                states.append(ST)
                ST = ST * eGL[lo:lo + 1, :] + lax.dot_general(vb[sl], kdb[sl], TN, preferred_element_type=f32)

            dST = dst[hh]
            dqt_i, dkd_i, dv_i, deg_i = [None] * nch, [None] * nch, [None] * nch, [None] * nch
            for ci in reversed(range(nch)):
                lo = ci * HGRN_CHUNK
                sl = slice(lo, lo + HGRN_CHUNK)
                ST0 = states[ci]
                dSTb = dST.astype(bf16)
                dv_i[ci] = lax.dot_general(kdb[sl], dSTb, NT, preferred_element_type=f32)
                dqt_i[ci] = jnp.dot(dob[sl], ST0.astype(bf16), preferred_element_type=f32)
                dkd_i[ci] = jnp.dot(vb[sl], dSTb, preferred_element_type=f32)
                deg_i[ci] = jnp.broadcast_to(jnp.sum(dST * ST0, axis=0, keepdims=True), (HGRN_CHUNK, LANE))
                dST = dST * eGL[lo:lo + 1, :] + lax.dot_general(dob[sl], qtb[sl], TN, preferred_element_type=f32)
            dst[hh] = dST

            dqt = dqt + jnp.concatenate(dqt_i, axis=0)
            dkd = jnp.concatenate(dkd_i, axis=0)
            dv_ref[:, cols] = (dv_acc + jnp.concatenate(dv_i, axis=0)).astype(bf16)
            deg = jnp.concatenate(deg_i, axis=0)

            dqs = dqt * pr["eG"]
            dkdkd = dkd * kd
            dG = dqt * qt - dkt * kt - dkdkd
            dk = dkt * pr["einv"] + dkd * pr["edec"]
            dGL = _mask_dot(m_ref[1], dkdkd) + eGL * deg
            dg = _mask_dot(m_ref[2], dG) + dGL
            df = dg / pr["f"] - dk
            sig = pr["sig"]
            df_ref[:, cols] = (df * (1.0 - lbv) * (sig * (1.0 - sig))).astype(bf16)
            alb[:, cols] += _colsum8(df * (1.0 - sig))
            sq = pr["sq"]
            dq_ref[:, cols] = (dqs * (sq * (1.0 + q_raw * (1.0 - sq)))).astype(bf16)

        for hh in range(HGRN_PAIR):
            one_head(hh)

        @pl.when(j == nsb - 1)
        def _():
            glb_ref[...] = jnp.broadcast_to(jnp.sum(alb[...], axis=0, keepdims=True), (SUBLANE, wide))
            gnw_ref[...] = jnp.broadcast_to(jnp.sum(anw[...], axis=0, keepdims=True), (SUBLANE, wide))

    wide = HGRN_PAIR * LANE
    rev = lambda off: pl.BlockSpec((sb, wide), lambda h, j: (nsb - 1 - j, off // HGRN_PAIR + h))
    stat = pl.BlockSpec((SUBLANE, wide), lambda h, j: (0, h))
    return pl.pallas_call(
        body,
        grid=(4 // HGRN_PAIR, nsb),
        in_specs=[rev(0), rev(4), rev(8), rev(12), rev(0), rev(0),
                  pl.BlockSpec((HGRN_PAIR, 1, LANE, LANE), lambda h, j: (h, nsb - 1 - j, 0, 0)),
                  pl.BlockSpec((1, wide), lambda h, j: (0, h)), pl.BlockSpec((1, LANE), lambda h, j: (0, 0)),
                  pl.BlockSpec((3, sb, sb), lambda h, j: (0, 0, 0))]
        + ([] if after is None else [pl.BlockSpec(memory_space=pl.ANY)]),
        out_specs=[rev(0), rev(0), rev(0), rev(0), stat, stat],
        out_shape=[SDS((S, HGRN_W), bf16)] * 4 + [SDS((SUBLANE, HGRN_W), f32)] * 2,
        scratch_shapes=[pltpu.VMEM((HGRN_PAIR, LANE, LANE), f32), pltpu.VMEM((SUBLANE, wide), f32),
                        pltpu.VMEM((SUBLANE, wide), f32)],
        compiler_params=_cparams(("parallel", "arbitrary")),
        name="hgrn_bwd",
    )(hg, hg, hg, hg, o_raw, dy, ck, lb, normw, _chunk_masks(), *([] if after is None else [after]))


def _lb_fwd(raw):
    def body(r_ref, o_ref):
        r = r_ref[...]
        m = jnp.max(r, axis=0, keepdims=True)
        e = jnp.exp(r - m)
        o_ref[...] = (e / jnp.sum(e, axis=0, keepdims=True))[0:1]

    return pl.pallas_call(body, out_shape=SDS((1, raw.shape[1]), f32), name="lb_fwd")(raw)


def _lb_bwd(raw, dlb):
    def body(r_ref, d_ref, o_ref):
        r = r_ref[...]
        m = jnp.max(r, axis=0, keepdims=True)
        e = jnp.exp(r - m)
        s = e / jnp.sum(e, axis=0, keepdims=True)
        s0 = s[0:1]
        onehot0 = jnp.where(lax.broadcasted_iota(jnp.int32, r.shape, 0) == 0, 1.0, 0.0)
        o_ref[...] = d_ref[...] * s0 * (onehot0 - s)

    return pl.pallas_call(body, out_shape=SDS(raw.shape, f32), name="lb_bwd")(raw, dlb)


def _gate_fwd(ya, yh, w_ba, w_bh, gc):
    S = ya.shape[0]
    D = w_ba.shape[1]
    tm = _pick(S, MM_ROWS)

    def body(ya_ref, yh_ref, wa_ref, wh_ref, g0_ref, g1_ref, a_ref, b_ref, o_ref):
        a = jnp.dot(ya_ref[...], wa_ref[...], preferred_element_type=f32).astype(bf16)
        b = jnp.dot(yh_ref[...], wh_ref[...], preferred_element_type=f32).astype(bf16)
        a_ref[...] = a
        b_ref[...] = b
        s0, s1 = _sigmoid(g0_ref[...].astype(f32)), _sigmoid(g1_ref[...].astype(f32))
        o_ref[...] = (s0 * a.astype(f32) + s1 * b.astype(f32)).astype(bf16)

    row = pl.BlockSpec((tm, D), lambda i: (i, 0))
    act = pl.BlockSpec((tm, ya.shape[1]), lambda i: (i, 0))
    wspec = pl.BlockSpec(w_ba.shape, lambda i: (0, 0))
    return pl.pallas_call(
        body,
        grid=(S // tm,),
        in_specs=[act, act, wspec, wspec, row, pl.BlockSpec((tm, D), lambda i: (i, 1))],
        out_specs=[row, row, row],
        out_shape=[SDS((S, D), bf16)] * 3,
        compiler_params=_cparams(("parallel",), VMEM_BIG),
        name="branch_gate_fwd",
    )(ya, yh, w_ba, w_bh, gc, gc)


def _gate_bwd(dmo, w_out, a, b, gc, w_ba, w_bh):
    S, D = a.shape
    W = w_ba.shape[0]
    tm = _pick(S, MM_ROWS)

    def body(dmo_ref, wo_ref, a_ref, b_ref, g0_ref, g1_ref, wa_ref, wh_ref,
             da_ref, db_ref, dg_ref, dya_ref, dyh_ref):
        dm = lax.dot_general(dmo_ref[...], wo_ref[...], NT, preferred_element_type=f32)
        dmv = dm.astype(bf16).astype(f32)
        s0, s1 = _sigmoid(g0_ref[...].astype(f32)), _sigmoid(g1_ref[...].astype(f32))
        da = (dmv * s0).astype(bf16)
        db = (dmv * s1).astype(bf16)
        da_ref[...] = da
        db_ref[...] = db
        dg_ref[:, :D] = (dmv * a_ref[...].astype(f32) * (s0 * (1.0 - s0))).astype(bf16)
        dg_ref[:, D:] = (dmv * b_ref[...].astype(f32) * (s1 * (1.0 - s1))).astype(bf16)
        dya_ref[...] = lax.dot_general(da, wa_ref[...], NT, preferred_element_type=f32)
        dyh_ref[...] = lax.dot_general(db, wh_ref[...], NT, preferred_element_type=f32)

    row = pl.BlockSpec((tm, D), lambda i: (i, 0))
    wide = pl.BlockSpec((tm, 2 * D), lambda i: (i, 0))
    narrow = pl.BlockSpec((tm, W), lambda i: (i, 0))
    whole = lambda t: pl.BlockSpec(t.shape, lambda i: (0, 0))
    return pl.pallas_call(
        body,
        grid=(S // tm,),
        in_specs=[row, whole(w_out), row, row, row, pl.BlockSpec((tm, D), lambda i: (i, 1)), whole(w_ba), whole(w_bh)],
        out_specs=[row, row, wide, narrow, narrow],
        out_shape=[SDS((S, D), bf16), SDS((S, D), bf16), SDS((S, 2 * D), bf16), SDS((S, W), f32), SDS((S, W), f32)],
        compiler_params=_cparams(("parallel",), VMEM_BIG),
        name="gate_bwd_fused",
    )(dmo, w_out, a, b, gc, gc, w_ba, w_bh)


CONV_ROWS = 512
INV_SQRT2 = 0.7071067811865476
INV_SQRT_2PI = 0.3989422804014327


CONV_HALO = 16


def _shift_down(cur, prev, k):
    x = pltpu.roll(cur, k, 0)
    row = lax.broadcasted_iota(jnp.int32, (SUBLANE, LANE), 0)
    head = jnp.where(row < k, pltpu.roll(prev, k, 0)[:SUBLANE], x[:SUBLANE])
    return jnp.concatenate([head, x[SUBLANE:]], axis=0)


def _shift_up(cur, nxt, k):
    R = cur.shape[0]
    x = pltpu.roll(cur, R - k, 0)
    row = lax.broadcasted_iota(jnp.int32, (SUBLANE, LANE), 0)
    tail = jnp.where(row >= SUBLANE - k, pltpu.roll(nxt, SUBLANE - k, 0), x[R - SUBLANE:])
    return jnp.concatenate([x[:R - SUBLANE], tail], axis=0)


def _conv_rows(u_ref, w, b, r0, first):
    R = CONV_ROWS
    cur = u_ref[pl.ds(r0, R), :].astype(f32)
    prev = u_ref[pl.ds(pl.multiple_of(jnp.maximum(r0 - CONV_HALO, 0), CONV_HALO), CONV_HALO), :].astype(f32)
    prev = jnp.where(first, 0.0, prev)
    x1 = _shift_down(cur, prev, 1)
    x2 = _shift_down(cur, prev, 2)
    c = ((b + w[0:1] * x2) + w[1:2] * x1) + w[2:3] * cur
    return c, x2, x1, cur


def _conv_fwd(ug, uv, wg, wv, bg, bv):
    S, F = ug.shape
    nchunk = S // CONV_ROWS

    def body(ug_ref, uv_ref, wg_ref, wv_ref, bg_ref, bv_ref, o_ref):
        wgv, wvv, bgv, bvv = wg_ref[...], wv_ref[...], bg_ref[...], bv_ref[...]

        def step(ci, carry):
            r0 = pl.multiple_of(ci * CONV_ROWS, CONV_ROWS)
            cg = _conv_rows(ug_ref, wgv, bgv, r0, ci == 0)[0]
            cv = _conv_rows(uv_ref, wvv, bvv, r0, ci == 0)[0]
            gelu = 0.5 * cg * (1.0 + lax.erf(cg * INV_SQRT2))
            o_ref[pl.ds(r0, CONV_ROWS), :] = (gelu * cv).astype(bf16)
            return carry

        lax.fori_loop(0, nchunk, step, 0)

    col = pl.BlockSpec((S, LANE), lambda j: (0, j))
    w3 = pl.BlockSpec((3, LANE), lambda j: (0, j))
    b1 = pl.BlockSpec((1, LANE), lambda j: (0, j))
    return pl.pallas_call(
        body,
        grid=(F // LANE,),
        in_specs=[col, col, w3, w3, b1, b1],
        out_specs=col,
        out_shape=SDS((S, F), bf16),
        compiler_params=_cparams(("parallel",), VMEM_BIG),
        name="conv_fwd",
    )(ug, uv, wg, wv, bg, bv)


def _conv_bwd(ug, uv, dact, wg, wv, bg, bv):
    S, F = ug.shape
    R = CONV_ROWS
    nchunk = S // R

    def body(ug_ref, uv_ref, da_ref, wg_ref, wv_ref, bg_ref, bv_ref, dug_ref, duv_ref, sg_ref, sv_ref, dcg, dcv):
        wgv, wvv, bgv, bvv = wg_ref[...], wv_ref[...], bg_ref[...], bv_ref[...]
        zero = jnp.zeros((SUBLANE, LANE), f32)

        def fwd_step(ci, acc):
            r0 = pl.multiple_of(ci * R, R)
            cg, g2, g1, g0 = _conv_rows(ug_ref, wgv, bgv, r0, ci == 0)
            cv, v2, v1, v0 = _conv_rows(uv_ref, wvv, bvv, r0, ci == 0)
            da = da_ref[pl.ds(r0, R), :].astype(f32)
            cdf = 0.5 * (1.0 + lax.erf(cg * INV_SQRT2))
            pdf = INV_SQRT_2PI * jnp.exp(-0.5 * cg * cg)
            dg = da * cv * (cdf + cg * pdf)
            dv = da * (cg * cdf)
            dcg[pl.ds(r0, R), :] = dg
            dcv[pl.ds(r0, R), :] = dv
            new = (acc[0] + _colsum8(dg * g2), acc[1] + _colsum8(dg * g1), acc[2] + _colsum8(dg * g0),
                   acc[3] + _colsum8(dg),
                   acc[4] + _colsum8(dv * v2), acc[5] + _colsum8(dv * v1), acc[6] + _colsum8(dv * v0),
                   acc[7] + _colsum8(dv))
            return new

        acc = lax.fori_loop(0, nchunk, fwd_step, (zero,) * 8)
        rows = lax.broadcasted_iota(jnp.int32, (SUBLANE, LANE), 0)

        def stats(parts):
            out = jnp.zeros((SUBLANE, LANE), f32)
            for k, pt in enumerate(parts):
                out = jnp.where(rows == k, jnp.sum(pt, axis=0, keepdims=True), out)
            return out

        sg_ref[...] = stats(acc[0:4])
        sv_ref[...] = stats(acc[4:8])

        def du_rows(dc, w, r0, last):
            cur = dc[pl.ds(r0, R), :]
            nxt = dc[pl.ds(pl.multiple_of(jnp.minimum(r0 + R, S - SUBLANE), SUBLANE), SUBLANE), :]
            nxt = jnp.where(last, 0.0, nxt)
            return w[2:3] * cur + w[1:2] * _shift_up(cur, nxt, 1) + w[0:1] * _shift_up(cur, nxt, 2)

        def bwd_step(ci, carry):
            r0 = pl.multiple_of(ci * R, R)
            last = ci == nchunk - 1
            dug_ref[pl.ds(r0, R), :] = du_rows(dcg, wgv, r0, last).astype(bf16)
            duv_ref[pl.ds(r0, R), :] = du_rows(dcv, wvv, r0, last).astype(bf16)
            return carry

        lax.fori_loop(0, nchunk, bwd_step, 0)

    col = pl.BlockSpec((S, LANE), lambda j: (0, j))
    w3 = pl.BlockSpec((3, LANE), lambda j: (0, j))
    b1 = pl.BlockSpec((1, LANE), lambda j: (0, j))
    st = pl.BlockSpec((SUBLANE, LANE), lambda j: (0, j))
    return pl.pallas_call(
        body,
        grid=(F // LANE,),
        in_specs=[col, col, col, w3, w3, b1, b1],
        out_specs=[col, col, st, st],
        out_shape=[SDS((S, F), bf16), SDS((S, F), bf16), SDS((SUBLANE, F), f32), SDS((SUBLANE, F), f32)],
        scratch_shapes=[pltpu.VMEM((S, LANE), f32), pltpu.VMEM((S, LANE), f32)],
        compiler_params=_cparams(("parallel",), VMEM_BIG),
        name="conv_bwd",
    )(ug, uv, dact, wg, wv, bg, bv)


def _adam_math(w, g, m, v):
    m = ADAM_B1 * m + (1.0 - ADAM_B1) * g
    v = ADAM_B2 * v + (1.0 - ADAM_B2) * (g * g)
    m_hat = m / (1.0 - ADAM_B1 ** ADAM_STEP)
    v_hat = v / (1.0 - ADAM_B2 ** ADAM_STEP)
    delta = -ADAM_LR * (m_hat / (jnp.sqrt(v_hat) + ADAM_EPS) + ADAM_WD * w)
    return delta, m, v


def _adamw(w, m, v, g, name):
    R, C = w.shape
    parts = g.ndim == 3
    tr = R
    if R % 16 == 0:
        for t in range(R, 0, -16):
            if R % t == 0 and t * C * 4 <= ADAM_BLOCK_BYTES:
                tr = t
                break

    def body(w_ref, m_ref, v_ref, g_ref, go_ref, d_ref, mo_ref, vo_ref):
        if parts:
            gv = ((g_ref[0].astype(f32) + g_ref[1].astype(f32)) + g_ref[2].astype(f32)) + g_ref[3].astype(f32)
        else:
            gv = g_ref[...]
        go_ref[...] = gv
        d, mn, vn = _adam_math(w_ref[...], gv, m_ref[...], v_ref[...])
        d_ref[...] = d
        mo_ref[...] = mn
        vo_ref[...] = vn

    row = pl.BlockSpec((tr, C), lambda i: (i, 0))
    gspec = pl.BlockSpec((4, tr, C), lambda i: (0, i, 0)) if parts else row
    return pl.pallas_call(
        body,
        grid=(R // tr,),
        in_specs=[row, row, row, gspec],
        out_specs=[row] * 4,
        out_shape=[SDS((R, C), f32)] * 4,
        compiler_params=_cparams(("parallel",), VMEM_BIG),
        name=name,
    )(w, m, v, g)


def _sum8(parts, name):
    _, _, R, C = parts.shape

    def body(p_ref, o_ref):
        acc = p_ref[0, 0]
        for c in range(2):
            for k in range(4):
                if c or k:
                    acc = acc + p_ref[c, k]
        o_ref[...] = acc

    return pl.pallas_call(body, out_shape=SDS((R, C), f32), name=name)(parts)


def _pair_add(by_core, b, name):
    _, K, R, C = by_core.shape
    tr = R // 2 if R % 32 == 0 else R

    def body(c_ref, a_ref, b_ref, o_ref):
        o_ref[...] = (a_ref[0].astype(f32) + b_ref[...].astype(f32)).astype(bf16)

    blk = pl.BlockSpec((1, tr, C), lambda k, i, c: (k, i, 0))
    return pl.pallas_call(
        body,
        grid_spec=pltpu.PrefetchScalarGridSpec(
            num_scalar_prefetch=1,
            grid=(K, R // tr),
            in_specs=[pl.BlockSpec((1, 1, tr, C), lambda k, i, c: (c[0], k, i, 0)), blk],
            out_specs=blk,
        ),
        out_shape=SDS((K, R, C), bf16),
        compiler_params=_cparams(("parallel", "parallel")),
        name=name,
    )(lax.axis_index("c").astype(jnp.int32).reshape(1), by_core, b)


_ANY = pl.BlockSpec(memory_space=pl.ANY)


def _chip_out_shape(src, gather):
    return SDS((4,) + tuple(src.shape if gather else src.shape[1:]), src.dtype)


def _fill_own(out, src, gather):
    mine = 2 * lax.axis_index("x") + lax.axis_index("y")
    own = src if gather else lax.dynamic_index_in_dim(src, mine, axis=0, keepdims=False)
    return lax.dynamic_update_index_in_dim(out, own, mine, axis=0)


_HBM = pl.BlockSpec(memory_space=pltpu.HBM)
_SEM = pl.BlockSpec(memory_space=pltpu.SEMAPHORE)
_EFFECT = pltpu.SideEffectType.DATAFLOW_SIDE_EFFECTING
_SPLIT_PEERS = {"chip_gather": 3, "chip_xchg": 3, "core_gather": 1, "core_swap": 1}


def _split_land(src, kind):
    if kind == "core_gather":
        return SDS((2,) + tuple(src.shape), src.dtype)
    if kind == "core_swap":
        return SDS(tuple(src.shape[1:]), src.dtype)
    return _chip_out_shape(src, kind == "chip_gather")


def _split_copies(src_ref, land_ref, sems, kind):
    x, y, c = lax.axis_index("x"), lax.axis_index("y"), lax.axis_index("c")
    n = _SPLIT_PEERS[kind]
    if kind == "core_gather":
        routes = [((x, y, 1 - c), src_ref, land_ref.at[c], land_ref.at[1 - c])]
    elif kind == "core_swap":
        routes = [((x, y, 1 - c), src_ref.at[1 - c], land_ref, land_ref)]
    else:
        mine = 2 * x + y
        gather = kind == "chip_gather"
        routes = [((px, py, c), src_ref if gather else src_ref.at[2 * px + py], land_ref.at[mine],
                   land_ref.at[2 * px + py]) for px, py in [(1 - x, y), (x, 1 - y), (1 - x, 1 - y)]]
    sends, recvs = [], []
    for j, (peer, piece, there, here) in enumerate(routes):
        sends.append(pltpu.make_async_remote_copy(src_ref=piece, dst_ref=there, send_sem=sems[j],
                                                  recv_sem=sems[n + j], device_id=peer, device_id_type=MESH))
        recvs.append(pltpu.make_async_remote_copy(src_ref=piece, dst_ref=here, send_sem=sems[j],
                                                  recv_sem=sems[n + j], device_id=peer, device_id_type=MESH))
    return sends, recvs


def _split_start(src, kind, name, after=None):
    land = _split_land(src, kind)
    ns = 2 * _SPLIT_PEERS[kind]
    n_in = 2 if after is None else 3

    def body(*refs):
        src_ref, land_ref = refs[:2]
        outs = refs[n_in:]
        for cp in _split_copies(src_ref, land_ref, outs[:ns], kind)[0]:
            cp.start()
        token = outs[ns + 2]
        token[...] = jnp.zeros_like(token)

    res = pl.pallas_call(
        body,
        name=name,
        out_shape=(pltpu.SemaphoreType.DMA(()),) * ns
        + (pltpu.HBM(src.shape, src.dtype), pltpu.HBM(land.shape, land.dtype), SDS((SUBLANE, LANE), f32)),
        in_specs=(_HBM, _HBM) + (() if after is None else (_ANY,)),
        out_specs=(_SEM,) * ns + (_HBM, _HBM, pl.BlockSpec(memory_space=pltpu.VMEM)),
        input_output_aliases={0: ns, 1: ns + 1},
        compiler_params=pltpu.CompilerParams(has_side_effects=_EFFECT),
    )(pltpu.with_memory_space_constraint(src, pltpu.HBM),
      pltpu.with_memory_space_constraint(lax.empty(land.shape, land.dtype), pltpu.HBM),
      *(() if after is None else (after,)))
    return (res[:ns], res[ns], res[ns + 1]), res[ns + 2]


def _split_wait(state, after, kind, name):
    sems, src_thru, land_thru = state
    ns = 2 * _SPLIT_PEERS[kind]

    def body(src_ref, land_ref, *rest):
        sends, recvs = _split_copies(src_ref, land_ref, rest[:ns], kind)
        for cp in recvs:
            cp.wait_recv()
        for cp in sends:
            cp.wait_send()

    src_out, got = pl.pallas_call(
        body,
        name=name,
        out_shape=(pltpu.HBM(src_thru.shape, src_thru.dtype), pltpu.HBM(land_thru.shape, land_thru.dtype)),
        in_specs=(_HBM, _HBM) + (_SEM,) * ns + (_ANY,),
        out_specs=(_HBM, _HBM),
        input_output_aliases={0: 0, 1: 1},
        compiler_params=pltpu.CompilerParams(has_side_effects=_EFFECT),
    )(src_thru, land_thru, *sems, after)
    if kind == "core_swap":
        return got, src_out
    if kind == "core_gather":
        return lax.dynamic_update_index_in_dim(got, src_out, lax.axis_index("c"), axis=0)
    return _fill_own(got, src_out, kind == "chip_gather")


def _core_gather(src, name):
    def body(src_ref, out_ref, send_sem, recv_sem):
        x, y, c = lax.axis_index("x"), lax.axis_index("y"), lax.axis_index("c")
        cp = pltpu.make_async_remote_copy(src_ref=src_ref, dst_ref=out_ref.at[c], send_sem=send_sem,
                                          recv_sem=recv_sem, device_id=(x, y, 1 - c), device_id_type=MESH)
        cp.start()
        pltpu.make_async_remote_copy(src_ref=src_ref, dst_ref=out_ref.at[1 - c], send_sem=send_sem,
                                     recv_sem=recv_sem, device_id=(x, y, 1 - c), device_id_type=MESH).wait_recv()
        cp.wait_send()

    out = pl.pallas_call(
        body,
        in_specs=[_ANY],
        out_specs=_ANY,
        out_shape=SDS((2,) + tuple(src.shape), src.dtype),
        scratch_shapes=[pltpu.SemaphoreType.DMA, pltpu.SemaphoreType.DMA],
        name=name,
    )(src)
    return lax.dynamic_update_index_in_dim(out, src, lax.axis_index("c"), axis=0)


_PACK_A = (("w_in", (1088, 1024)),)
_PACK_B = (("w_ba", (512, 128)), ("w_bh", (512, 128)), ("w_out", (128, 1024)), ("w_up", (704, 1024)),
           ("w_down", (352, 1024)))
_PACK_SIZES = _PACK_A + _PACK_B
_TRANSPOSED = ("w_in", "w_up")


def _slab_rows(sizes):
    return sum(r * c for _, (r, c) in sizes) // D_MODEL


def _pack_rows(d, sizes):
    n = d[sizes[0][0]].shape[0]
    return jnp.concatenate([d[k].reshape(n, -1, D_MODEL) for k, _ in sizes], axis=1)


def _unpack_rows(slab, sizes):
    n = slab.shape[0]
    out, lo = {}, 0
    for key, (r, c) in sizes:
        rows = r * c // D_MODEL
        out[key] = slab[:, lo:lo + rows].reshape(n, r, c)
        lo += rows
    return out


def _by_core(gslab):
    return jnp.swapaxes(gslab.reshape((4, 2) + gslab.shape[1:]), 0, 1)


def _cols_to_full(t):
    return jnp.swapaxes(t, 0, 1).reshape(t.shape[1], -1)


def _full_to_cols(t):
    K = t.shape[0]
    return jnp.swapaxes(t.reshape(K, 8, -1), 0, 1)


_SMALL = (("pre_mix_norm", (1, 1024)), ("rel_bias", (32, 24)), ("hgrn_lb_raw", (2, 512)), ("hgrn_norm", (1, 128)),
          ("post_mix_norm", (1, 1024)), ("pre_ffn_norm", (1, 1024)), ("conv_b", (1, 5632)),
          ("post_ffn_norm", (1, 1024)))
_SMALL_ROWS = 96
_CONVW_ROWS = 136


_SMALL_USED = sum(r * c for _, (r, c) in _SMALL)


def _pack_small(d, extra=None):
    flat = jnp.concatenate([d[k].reshape(-1) for k, _ in _SMALL] + ([] if extra is None else [extra.reshape(-1)]))
    flat = jnp.pad(flat, (0, _SMALL_ROWS * LANE - flat.shape[0]))
    return flat.reshape(_SMALL_ROWS, LANE)


def _unpack_small(p):
    flat = p.reshape(-1)
    out, lo = {}, 0
    for k, shp in _SMALL:
        n = shp[0] * shp[1]
        out[k] = flat[lo:lo + n].reshape(shp)
        lo += n
    return out


def _local_step(x, tgt, P, plan):
    S = x.shape[0]
    P = dict(P)
    lb = _lb_fwd(P["hgrn_lb_raw"])
    hs = _prep(x, P["pre_mix_norm"], plan.start_token())
    h1 = hs[0]
    consts = [_bias_consts(d) for d in DILATIONS]
    biases, dep = [], h1
    for g in range(N_GROUPS):
        tab_t = P["rel_bias"][:, 8 * g:8 * g + 8].T
        dep = _bias_build(tab_t, consts[g][0], consts[g][1], f"bias_build{g}", dep)
        biases.append(dep.reshape(8, ATTN_BLOCK, 2 * ATTN_BLOCK))
    W = dict(plan.weights_a(dep))
    qkv0, hg, gc = _mm_fanout(h1, [W["wt_qkv"][0], W["wt_hg"], W["wt_gate"]], "nt", [bf16, f32, bf16], "proj_natural")
    qkv = [qkv0] + [_mm(hs[g], W["wt_qkv"][g], "nt", bf16, f"proj_qkv{g}") for g in (1, 2)]
    obuf, lbuf, token = [], [], None
    for g, d in enumerate(DILATIONS):
        o_g, l_g = _attn_fwd(qkv[g], biases[g], (S // d) // ATTN_BLOCK, f"attn_fwd{g}", after=token)
        lbuf.append(l_g)
        obuf.append(o_g)
        if g == 0:
            token = plan.forward_b(o_g)
    y_attn, y_attn_b, w0, w1, w2 = _attn_merge(obuf[0], obuf[1], obuf[2], lbuf[0], lbuf[1], lbuf[2])
    y_hgrn, o_raw, ck = _hgrn_fwd(hg, lb, P["hgrn_norm"])
    wb = plan.weights_b(y_hgrn)
    P["conv_w"] = wb.pop("conv_w")
    W.update(wb)
    a, b, merged = _gate_fwd(y_attn_b, y_hgrn, W["w_ba"], W["w_bh"], gc)
    mo, x1, h2 = _mid_fwd(x, merged, W["w_out"], P["post_mix_norm"], P["pre_ffn_norm"])
    ug, uv = _mm_fanout(h2, [W["wt_up_g"], W["wt_up_v"]], "nt", [bf16, bf16], "up_proj")
    cw_g, cw_v = P["conv_w"][:, :D_FF], P["conv_w"][:, D_FF:]
    cb_g, cb_v = P["conv_b"][:, :D_FF], P["conv_b"][:, D_FF:]
    act = _conv_fwd(ug, uv, cw_g, cw_v, cb_g, cb_v)
    loss, dy, dfo, g_post_ffn = _final(x1, act, W["w_down"], tgt, P["post_ffn_norm"])
    gW_down = _mm(act, dfo, "tn", bf16, "gw_down")
    dact = _mm(dfo, W["w_down"], "nt", bf16, "d_act")
    dug, duv, st_g, st_v = _conv_bwd(ug, uv, dact, cw_g, cw_v, cb_g, cb_v)
    gW_up_g = _mm(dug, h2, "tn", bf16, "gw_up_gate")
    gW_up_v = _mm(duv, h2, "tn", bf16, "gw_up_val")
    dx1, dmo, g_pre_ffn, g_post_mix = _mid_bwd(dy, dug, duv, W["wt_up_g"], W["wt_up_v"], x1, mo, P["pre_ffn_norm"],
                                               P["post_mix_norm"])
    gW_out = _mm(merged, dmo, "tn", bf16, "gw_out")
    da, db, dgc, dyattn, dyhgrn = _gate_bwd(dmo, W["w_out"], a, b, gc, W["w_ba"], W["w_bh"])
    gW_ba = _mm(y_attn_b, da, "tn", bf16, "gw_ba")
    gW_bh = _mm(y_hgrn, db, "tn", bf16, "gw_bh")
    big_b = dict(w_ba=gW_ba, w_bh=gW_bh, w_out=gW_out, w_up=[gW_up_g, gW_up_v], w_down=gW_down)
    dos = _attn_merge_bwd(dyattn, y_attn, w0, w1, w2, after=plan.grads_b_start(big_b))
    dq_h, df_h, dv_h, dog_h, glb8, gnw8 = _hgrn_bwd(hg, o_raw, dyhgrn, ck, lb, P["hgrn_norm"],
                                                   after=plan.grads_b_exchange(dos[5]))
    dhg = [dq_h, df_h, dv_h, dog_h]
    g_lb_raw = _lb_bwd(P["hgrn_lb_raw"], glb8[0:1])
    gn = gnw8[0:1]
    g_hgrn_norm = (gn[:, 0:128] + gn[:, 128:256]) + (gn[:, 256:384] + gn[:, 384:512])
    dqkvs, gW_qkv, g_rel = [], [], []
    for g, d in enumerate(DILATIONS):
        dq, dk, dv, dbias = _attn_bwd(qkv[g], biases[g], dos[g], dos[3 + g], lbuf[g], (S // d) // ATTN_BLOCK,
                                      f"attn_bwd{g}")
        dqkvs.append([dq, dk, dv])
        gW_qkv.append(_mm(dqkvs[g], hs[g], "tn", bf16, f"gw_qkv{g}"))
        g_rel.append(_bias_grad(dbias.reshape(8, -1), consts[g][0], f"bias_grad{g}"))
    gW_hg = _mm(dhg, h1, "tn", bf16, "gw_hg")
    gW_gate = _mm(dgc, h1, "tn", bf16, "gw_gate")
    gW_in = gW_qkv + [gW_hg, gW_gate]
    token = plan.grads_a_start(gW_in)
    dh_perm = [_mm(dqkvs[g], W["wt_qkv"][g], "nn", f32, f"dh1_qkv{g}", after=token) for g in (1, 2)]
    token = plan.grads_a_exchange(dh_perm[1])
    dh_main = _mm(dqkvs[0] + dhg + [dgc], [W["wt_qkv"][0], W["wt_hg"], W["wt_gate"]], "nn", f32, "dh1_main",
                  after=token)
    grad_x, g_pre_mix = _first_bwd(x, dx1, dh_main, dh_perm[0], dh_perm[1], P["pre_mix_norm"])

    g_conv_w = jnp.concatenate([st_g[0:3], st_v[0:3]], axis=1)
    g_conv_b = jnp.concatenate([st_g[3:4], st_v[3:4]], axis=1)
    small = dict(pre_mix_norm=g_pre_mix, rel_bias=jnp.concatenate(g_rel, axis=1), hgrn_lb_raw=g_lb_raw,
                 hgrn_norm=g_hgrn_norm, post_mix_norm=g_post_mix, pre_ffn_norm=g_pre_ffn, conv_b=g_conv_b,
                 post_ffn_norm=g_post_ffn, conv_w=g_conv_w)
    return loss, grad_x, gW_in, big_b, small


def _weights_a(both):
    wt = jnp.swapaxes(both, 0, 1).reshape(-1, D_MODEL)
    return dict(
        wt_qkv=[wt[g * QKV_G:(g + 1) * QKV_G] for g in range(N_GROUPS)],
        wt_hg=wt[3 * QKV_G:3 * QKV_G + 4 * HGRN_W],
        wt_gate=wt[3 * QKV_G + 4 * HGRN_W:],
    )


def _weights_b(slabs):
    sh = _unpack_rows(slabs, _PACK_B)
    wt_up = sh["w_up"].reshape(-1, D_MODEL)
    return dict(
        w_ba=_cols_to_full(sh["w_ba"]),
        w_bh=_cols_to_full(sh["w_bh"]),
        w_out=sh["w_out"].reshape(D_MODEL, D_MODEL),
        wt_up_g=wt_up[:D_FF],
        wt_up_v=wt_up[D_FF:],
        w_down=sh["w_down"].reshape(D_FF, D_MODEL),
    )


def _dest_rows(sections, height):
    out = []
    for j in range(8):
        lo, hi, off, pieces = j * height, (j + 1) * height, 0, []
        for s in sections:
            a, b = max(lo, off), min(hi, off + s.shape[0])
            if a < b:
                pieces.append(s[a - off:b - off])
            off += s.shape[0]
        out.append(pieces[0] if len(pieces) == 1 else jnp.concatenate(pieces, axis=0))
    return out


def _grad_blocks_a(sections):
    rows = _dest_rows(sections, 1088)
    return jnp.stack([jnp.stack([rows[2 * k + c].astype(bf16) for k in range(4)]) for c in range(2)])


def _grad_slab_b(g):
    shards = dict(w_ba=_full_to_cols(g["w_ba"]), w_bh=_full_to_cols(g["w_bh"]), w_out=g["w_out"].reshape(8, 128, D_MODEL),
                  w_up=jnp.stack(_dest_rows(g["w_up"], 704)), w_down=g["w_down"].reshape(8, 352, D_MODEL))
    return _pack_rows({k: v.astype(bf16) for k, v in shards.items()}, _PACK_B)


_CONVW_SLAB_ROWS = 16


class _Traffic:
    def __init__(self, slab_a, slab_b, conv_w):
        hi = conv_w.astype(bf16)
        r1 = conv_w - hi.astype(f32)
        mid = r1.astype(bf16)
        lo = (r1 - mid.astype(f32)).astype(bf16)
        bits = jnp.stack([hi, mid, lo]).reshape(-1)
        tail = jnp.pad(bits, (0, _CONVW_SLAB_ROWS * D_MODEL - bits.shape[0])).reshape(_CONVW_SLAB_ROWS, D_MODEL)
        self.slab_b = jnp.concatenate([slab_b, tail], axis=0)
        self.state_a, tok = _split_start(slab_a, "chip_gather", "ag_a_start")
        self.state_b, self.token = _split_start(self.slab_b, "chip_gather", "ag_b_start", after=tok)
        self.state = None
        self.state_gb = None

    def start_token(self):
        return self.token

    def weights_a(self, after):
        by_chip = _split_wait(self.state_a, after, "chip_gather", "ag_a_wait")
        return _weights_a(_core_gather(by_chip, "ag_a_cores"))

    def forward_b(self, after):
        by_chip = _split_wait(self.state_b, after, "chip_gather", "ag_b_wait")
        self.state, token = _split_start(by_chip, "core_gather", "ag_b_cores_start")
        return token

    def weights_b(self, after):
        both = _split_wait(self.state, after, "core_gather", "ag_b_cores_wait")
        slabs = jnp.swapaxes(both, 0, 1).reshape((8,) + tuple(self.slab_b.shape))
        rows = _slab_rows(_PACK_B)
        out = _weights_b(slabs[:, :rows])
        pieces = slabs[:, rows:].reshape(8, -1)[:, :3 * 3 * 704].reshape(8, 3, 3, 704).astype(f32)
        out["conv_w"] = _cols_to_full((pieces[:, 0] + pieces[:, 1]) + pieces[:, 2])
        return out

    def grads_b_start(self, grads):
        self.state, token = _split_start(_by_core(_grad_slab_b(grads)), "core_swap", "rs_b_cores_start")
        return token

    def grads_b_exchange(self, after):
        from_sib, by_core = _split_wait(self.state, after, "core_swap", "rs_b_cores_wait")
        self.state_gb, token = _split_start(_pair_add(by_core, from_sib, "rs_b_pair_add"), "chip_xchg", "rs_b_start")
        return token

    def grads_a_start(self, sections):
        self.state, token = _split_start(_grad_blocks_a(sections), "core_swap", "rs_a_cores_start")
        return token

    def grads_a_exchange(self, after):
        from_sib, by_core = _split_wait(self.state, after, "core_swap", "rs_a_cores_wait")
        self.state, token = _split_start(_pair_add(by_core, from_sib, "rs_a_pair_add"), "chip_xchg", "rs_a_start")
        return token

    def parts(self, after):
        parts = _unpack_rows(_split_wait(self.state_gb, after, "chip_xchg", "rs_b_wait"), _PACK_B)
        parts["w_in"] = _split_wait(self.state, after, "chip_xchg", "rs_a_wait")
        return parts


def kernel(x, pre_mix_norm, w_in, rel_bias, hgrn_lb_raw, hgrn_norm, w_branch_attn, w_branch_hgrn, w_out, post_mix_norm, pre_ffn_norm, w_up, conv_w, conv_b, w_down, post_ffn_norm, loss_target, m_pre_mix_norm, m_w_in, m_rel_bias, m_hgrn_lb_raw, m_hgrn_norm, m_w_branch_attn, m_w_branch_hgrn, m_w_out, m_post_mix_norm, m_pre_ffn_norm, m_w_up, m_conv_w, m_conv_b, m_w_down, m_post_ffn_norm, v_pre_mix_norm, v_w_in, v_rel_bias, v_hgrn_lb_raw, v_hgrn_norm, v_w_branch_attn, v_w_branch_hgrn, v_w_out, v_post_mix_norm, v_pre_ffn_norm, v_w_up, v_conv_w, v_conv_b, v_w_down, v_post_ffn_norm):
    ci = lax.axis_index("c")
    dev = 4 * lax.axis_index("x") + 2 * lax.axis_index("y") + ci
    tr = lambda t: jnp.swapaxes(t[0], 0, 1)
    wts = dict(w_in=tr(w_in), w_ba=w_branch_attn[0], w_bh=w_branch_hgrn[0], w_out=w_out[0], w_up=tr(w_up),
               w_down=w_down[0])
    mom = dict(w_in=tr(m_w_in), w_ba=m_w_branch_attn[0], w_bh=m_w_branch_hgrn[0], w_out=m_w_out[0], w_up=tr(m_w_up),
               w_down=m_w_down[0])
    var = dict(w_in=tr(v_w_in), w_ba=v_w_branch_attn[0], w_bh=v_w_branch_hgrn[0], w_out=v_w_out[0], w_up=tr(v_w_up),
               w_down=v_w_down[0])
    small_w = dict(pre_mix_norm=pre_mix_norm, rel_bias=rel_bias, hgrn_lb_raw=hgrn_lb_raw, hgrn_norm=hgrn_norm,
                   post_mix_norm=post_mix_norm, pre_ffn_norm=pre_ffn_norm, conv_b=conv_b, post_ffn_norm=post_ffn_norm)
    small_m = dict(pre_mix_norm=m_pre_mix_norm, rel_bias=m_rel_bias, hgrn_lb_raw=m_hgrn_lb_raw, hgrn_norm=m_hgrn_norm,
                   post_mix_norm=m_post_mix_norm, pre_ffn_norm=m_pre_ffn_norm, conv_b=m_conv_b,
                   post_ffn_norm=m_post_ffn_norm)
    small_v = dict(pre_mix_norm=v_pre_mix_norm, rel_bias=v_rel_bias, hgrn_lb_raw=v_hgrn_lb_raw, hgrn_norm=v_hgrn_norm,
                   post_mix_norm=v_post_mix_norm, pre_ffn_norm=v_pre_ffn_norm, conv_b=v_conv_b,
                   post_ffn_norm=v_post_ffn_norm)

    plan = _Traffic(wts["w_in"].astype(bf16),
                    _pack_rows({k: wts[k].astype(bf16)[None] for k, _ in _PACK_B}, _PACK_B)[0], conv_w[0])

    loss8, grad_x, _, _, small = _local_step(x[0], loss_target[0], small_w, plan)
    spack = jnp.concatenate([_pack_small(small, loss8[0, 0:1]),
                             jnp.pad(small["conv_w"].reshape(-1, LANE), ((0, _CONVW_ROWS - 132), (0, 0)))], axis=0)
    small_state, token = _split_start(spack, "chip_gather", "ag_small_start")

    parts = plan.parts(token)
    outs_big = {}
    for k, _ in _PACK_SIZES:
        outs_big[k] = _adamw(wts[k], mom[k], var[k], parts[k], "adamw_" + k)

    by_chip = _split_wait(small_state, outs_big["w_in"][1], "chip_gather", "ag_small_wait")
    allp = _core_gather(by_chip, "ag_small_cores")
    ssum = _sum8(allp, "small_sum")
    gs = ssum[:_SMALL_ROWS]
    loss = ssum[_SMALL_USED // LANE, _SMALL_USED % LANE]
    res_small = _adamw(_pack_small(small_w), _pack_small(small_m), _pack_small(small_v), gs, "adamw_small")
    sm = [_unpack_small(t) for t in res_small]
    g_cw_full = ssum[_SMALL_ROWS:_SMALL_ROWS + 132].reshape(3, 2 * D_FF)
    g_cw = lax.dynamic_slice_in_dim(g_cw_full, dev * 704, 704, axis=1)
    res_cw = _adamw(conv_w[0], m_conv_w[0], v_conv_w[0], g_cw, "adamw_conv_w")

    def pick(i):
        def big_(k):
            t = outs_big[k][i]
            return (jnp.swapaxes(t, 0, 1) if k in _TRANSPOSED else t)[None]
        return [sm[i]["pre_mix_norm"], big_("w_in"), sm[i]["rel_bias"], sm[i]["hgrn_lb_raw"], sm[i]["hgrn_norm"],
                big_("w_ba"), big_("w_bh"), big_("w_out"), sm[i]["post_mix_norm"], sm[i]["pre_ffn_norm"],
                big_("w_up"), res_cw[i][None], sm[i]["conv_b"], big_("w_down"), sm[i]["post_ffn_norm"]]

    return (loss, grad_x[None], *pick(0), *pick(1), *pick(2), *pick(3))
```

```python
import functools
import math

import jax
import jax.numpy as jnp
from jax import lax
from jax.experimental import pallas as pl
from jax.experimental.pallas import tpu as pltpu

f32 = jnp.float32
bf16 = jnp.bfloat16
SDS = jax.ShapeDtypeStruct
HIGHEST = lax.Precision.HIGHEST
MESH = pl.DeviceIdType.MESH

NN = (((1,), (0,)), ((), ()))
NT = (((1,), (1,)), ((), ()))
TN = (((0,), (0,)), ((), ()))

D_MODEL = 1024
N_GROUPS = 3
DILATIONS = (1, 4, 16)
HEAD_DIM = 64
ATTN_BLOCK = 128
QKV_G = 1536
ATTN_OUT = 512
HGRN_W = 512
HGRN_CHUNK = 32
D_FF = 2816
NUM_BUCKETS = 32
MAX_EXACT = 16
MAX_DISTANCE = 2048
NEG_INF = -1e30
EPS = 1e-6
LANE = 128
SUBLANE = 8
VMEM_BIG = 48 * 1024 * 1024
MM_ROWS = 512
MM_OUT_BYTES = 8 * 1024 * 1024
ADAM_BLOCK_BYTES = 2304 * 1024

ADAM_LR, ADAM_B1, ADAM_B2, ADAM_EPS, ADAM_WD, ADAM_STEP = 0.001, 0.9, 0.999, 1e-08, 0.01, 10


def _pick(n, pref):
    t = pref
    while t >= LANE:
        if n % t == 0:
            return t
        t //= 2
    return n


def _cparams(sem=None, vmem=None):
    kw = {}
    if sem is not None:
        kw["dimension_semantics"] = sem
    if vmem is not None:
        kw["vmem_limit_bytes"] = vmem
    return pltpu.CompilerParams(**kw)


def _sigmoid(x):
    return jax.nn.sigmoid(x)


def _colsum8(x):
    return x.reshape(x.shape[0] // SUBLANE, SUBLANE, x.shape[1]).sum(axis=0)


def _mm(a, b, mode, out_dtype, name, acc=None, after=None):
    dims = {"nn": NN, "nt": NT, "tn": TN}[mode]
    has_acc = acc is not None
    parts = list(a) if isinstance(a, (list, tuple)) else [a]
    if mode == "tn":
        assert not has_acc
        K, N = b.shape
        widths = [t.shape[1] for t in parts]
        M = sum(widths)
        whole = M * N * 4 <= MM_OUT_BYTES
        assert whole or len(parts) == 1
        tmm = M if whole else M // 2
        ts = _pick(K, 4 * MM_ROWS)
        nk = K // ts

        npart = len(parts)
        narrow = out_dtype != f32

        def body_tn(*refs):
            b_ref, o_ref = refs[npart], refs[npart + 1]
            acc_ref = refs[npart + 2] if narrow else o_ref
            k = pl.program_id(1)
            bv = b_ref[...]
            lo = 0
            for a_ref, w in zip(refs[:npart], widths if whole else [tmm]):
                part = lax.dot_general(a_ref[...], bv, dims, preferred_element_type=f32)
                rows = slice(lo, lo + w)
                lo += w

                @pl.when(k == 0)
                def _(part=part, rows=rows):
                    acc_ref[rows, :] = part

                @pl.when(k > 0)
                def _(part=part, rows=rows):
                    acc_ref[rows, :] += part

            if narrow:
                @pl.when(k == nk - 1)
                def _():
                    o_ref[...] = acc_ref[...].astype(out_dtype)

        return pl.pallas_call(
            body_tn,
            grid=(M // tmm, nk),
            in_specs=[pl.BlockSpec((ts, w if whole else tmm), lambda i, k: (k, i)) for w in widths]
            + [pl.BlockSpec((ts, N), lambda i, k: (k, 0))],
            out_specs=pl.BlockSpec((tmm, N), lambda i, k: (i, 0)),
            out_shape=SDS((M, N), out_dtype),
            scratch_shapes=[pltpu.VMEM((tmm, N), f32)] if narrow else [],
            compiler_params=_cparams(("parallel", "arbitrary"), VMEM_BIG),
            name=name,
        )(*parts, b)

    bs = list(b) if isinstance(b, (list, tuple)) else [b]
    widths = [t.shape[1] for t in parts]
    M = parts[0].shape[0]
    kdim = 0 if mode == "nn" else 1
    N = bs[0].shape[1 - kdim]
    tm = _pick(M, MM_ROWS)
    npart, nb = len(parts), len(bs)
    place, bi, lo = [], 0, 0
    for w in widths:
        place.append((bi, lo))
        lo += w
        if lo == bs[bi].shape[kdim]:
            bi, lo = bi + 1, 0
    assert bi == nb and lo == 0

    def body(*refs):
        a_refs, b_refs = refs[:npart], refs[npart:npart + nb]
        c_ref = refs[npart + nb] if has_acc else None
        o_ref = refs[-1]
        part = None
        for a_ref, w, (bi, lo) in zip(a_refs, widths, place):
            b_ref = b_refs[bi]
            if w == bs[bi].shape[kdim]:
                bk = b_ref[...]
            else:
                bk = b_ref[:, lo:lo + w] if mode == "nt" else b_ref[lo:lo + w, :]
            t = lax.dot_general(a_ref[...], bk, dims, preferred_element_type=f32)
            part = t if part is None else part + t
        if has_acc:
            part = part + c_ref[...]
        o_ref[...] = part.astype(out_dtype)

    specs = [pl.BlockSpec((tm, w), lambda i: (i, 0)) for w in widths] \
        + [pl.BlockSpec(t.shape, lambda i: (0, 0)) for t in bs]
    args = parts + bs
    aliases = {}
    if has_acc:
        specs.append(pl.BlockSpec((tm, N), lambda i: (i, 0)))
        args.append(acc)
        aliases = {npart + nb: 0}
    if after is not None:
        specs.append(pl.BlockSpec(memory_space=pl.ANY))
        args.append(after)
    return pl.pallas_call(
        body,
        grid=(M // tm,),
        in_specs=specs,
        out_specs=pl.BlockSpec((tm, N), lambda i: (i, 0)),
        out_shape=SDS((M, N), out_dtype),
        input_output_aliases=aliases,
        compiler_params=_cparams(("parallel",), VMEM_BIG),
        name=name,
    )(*args)


def _mm_fanout(a, bs, mode, out_dtypes, name):
    dims = {"nn": NN, "nt": NT}[mode]
    M, K = a.shape
    ns = [b.shape[1] if mode == "nn" else b.shape[0] for b in bs]
    tm = _pick(M, MM_ROWS)
    nb = len(bs)

    def body(a_ref, *refs):
        av = a_ref[...]
        for b_ref, o_ref, dt in zip(refs[:nb], refs[nb:], out_dtypes):
            o_ref[...] = lax.dot_general(av, b_ref[...], dims, preferred_element_type=f32).astype(dt)

    return pl.pallas_call(
        body,
        grid=(M // tm,),
        in_specs=[pl.BlockSpec((tm, K), lambda i: (i, 0))] + [pl.BlockSpec(b.shape, lambda i: (0, 0)) for b in bs],
        out_specs=[pl.BlockSpec((tm, n), lambda i: (i, 0)) for n in ns],
        out_shape=[SDS((M, n), dt) for n, dt in zip(ns, out_dtypes)],
        compiler_params=_cparams(("parallel",), VMEM_BIG),
        name=name,
    )(a, *bs)


PERM_ROWS = 2048


def _perm_spec(d, cols=LANE):
    return pl.BlockSpec((d, PERM_ROWS // d, cols), lambda i, j: (0, i, j))


def _to_natural(src_ref, dst_ref, d):
    n = src_ref.shape[1]
    for r in range(d):
        dst_ref[pl.ds(r, n, stride=d), :] = src_ref[r]


def _prep(x, w, after=None):
    S, D = x.shape
    R = PERM_ROWS
    nc = D // LANE
    n_in = nc + 1 + (after is not None)

    def body(*refs):
        x_refs, w_ref = refs[:nc], refs[nc]
        h_ref, h4_ref, h16_ref, rs = refs[n_in:]
        ssq = None
        for xr in x_refs:
            v = xr[...]
            t = jnp.sum(v * v, axis=-1, keepdims=True)
            ssq = t if ssq is None else ssq + t
        rinv = lax.rsqrt(ssq * (1.0 / D) + EPS)
        rs[...] = jnp.broadcast_to(rinv, (R, LANE))
        for j, xr in enumerate(x_refs):
            cols = slice(j * LANE, (j + 1) * LANE)
            wj = w_ref[:, cols]
            h_ref[:, cols] = ((xr[...] * rinv) * wj).astype(bf16)
            for d, o_ref in ((4, h4_ref), (16, h16_ref)):
                n = R // d
                for r in range(d):
                    rows = pl.ds(r, n, stride=d)
                    o_ref[r, :, cols] = ((xr[rows, :] * rs[rows, :]) * wj).astype(bf16)

    col = lambda j: pl.BlockSpec((R, LANE), lambda i, j=j: (i, j))
    h, h4, h16 = pl.pallas_call(
        body,
        grid=(S // R,),
        in_specs=[col(j) for j in range(nc)] + [pl.BlockSpec((1, D), lambda i: (0, 0))]
        + ([] if after is None else [pl.BlockSpec(memory_space=pl.ANY)]),
        out_specs=[pl.BlockSpec((R, D), lambda i: (i, 0)), pl.BlockSpec((4, R // 4, D), lambda i: (0, i, 0)),
                   pl.BlockSpec((16, R // 16, D), lambda i: (0, i, 0))],
        out_shape=[SDS((S, D), bf16), SDS((4, S // 4, D), bf16), SDS((16, S // 16, D), bf16)],
        scratch_shapes=[pltpu.VMEM((R, LANE), f32)],
        compiler_params=_cparams(("parallel",), VMEM_BIG),
        name="prep_norm_perm",
    )(*([x] * nc), w, *([] if after is None else [after]))
    return [h, h4.reshape(S, D), h16.reshape(S, D)]


def _rms_parts(xv):
    r = lax.rsqrt(jnp.mean(xv * xv, axis=-1, keepdims=True) + EPS)
    return r, xv * r


def _rms_bwd(xhat, r, w, dy):
    dyw = dy * w
    return r * (dyw - xhat * jnp.mean(dyw * xhat, axis=-1, keepdims=True))


def _mid_fwd(x, merged, w_out, w_pm, w_pf):
    S, D = x.shape
    tm = _pick(S, MM_ROWS)

    def body(x_ref, m_ref, wo_ref, wpm_ref, wpf_ref, mo_ref, x1_ref, h2_ref):
        mo = jnp.dot(m_ref[...], wo_ref[...], preferred_element_type=f32)
        mo_ref[...] = mo
        _, moh = _rms_parts(mo)
        x1 = x_ref[...] + moh * wpm_ref[...]
        x1_ref[...] = x1
        _, x1h = _rms_parts(x1)
        h2_ref[...] = (x1h * wpf_ref[...]).astype(bf16)

    row = pl.BlockSpec((tm, D), lambda i: (i, 0))
    vec = pl.BlockSpec((1, D), lambda i: (0, 0))
    return pl.pallas_call(
        body,
        grid=(S // tm,),
        in_specs=[row, pl.BlockSpec((tm, merged.shape[1]), lambda i: (i, 0)),
                  pl.BlockSpec(w_out.shape, lambda i: (0, 0)), vec, vec],
        out_specs=[row, row, row],
        out_shape=[SDS((S, D), f32), SDS((S, D), f32), SDS((S, D), bf16)],
        compiler_params=_cparams(("parallel",), VMEM_BIG),
        name="out_proj_mid_fwd",
    )(x, merged, w_out, w_pm, w_pf)


def _final(x1, act, w_down, tgt, w_pfn):
    S, D = x1.shape
    tm = _pick(S, MM_ROWS)
    nt = S // tm

    def body(x1_ref, a_ref, wd_ref, t_ref, w_ref, loss_ref, dy_ref, dfo_ref, gw_ref, lacc, gacc):
        i = pl.program_id(0)

        @pl.when(i == 0)
        def _():
            lacc[...] = jnp.zeros_like(lacc)
            gacc[...] = jnp.zeros_like(gacc)

        w = w_ref[...]
        r, foh = _rms_parts(jnp.dot(a_ref[...], wd_ref[...], preferred_element_type=f32))
        y = x1_ref[...] + foh * w
        err = y - t_ref[...]
        lacc[...] += _colsum8(err * err)
        dy = err * (1.0 / D)
        dy_ref[...] = dy
        gacc[...] += _colsum8(dy * foh)
        dfo_ref[...] = _rms_bwd(foh, r, w, dy).astype(bf16)

        @pl.when(i == nt - 1)
        def _():
            loss_ref[...] = jnp.full((SUBLANE, LANE), 0.5 / D, f32) * jnp.sum(lacc[...])
            gw_ref[...] = jnp.sum(gacc[...], axis=0, keepdims=True)

    row = pl.BlockSpec((tm, D), lambda i: (i, 0))
    vec = pl.BlockSpec((1, D), lambda i: (0, 0))
    return pl.pallas_call(
        body,
        grid=(nt,),
        in_specs=[row, pl.BlockSpec((tm, act.shape[1]), lambda i: (i, 0)),
                  pl.BlockSpec(w_down.shape, lambda i: (0, 0)), row, vec],
        out_specs=[pl.BlockSpec((SUBLANE, LANE), lambda i: (0, 0)), row, row, vec],
        out_shape=[SDS((SUBLANE, LANE), f32), SDS((S, D), f32), SDS((S, D), bf16), SDS((1, D), f32)],
        scratch_shapes=[pltpu.VMEM((SUBLANE, D), f32), pltpu.VMEM((SUBLANE, D), f32)],
        compiler_params=_cparams(("arbitrary",), VMEM_BIG),
        name="down_proj_final_loss",
    )(x1, act, w_down, tgt, w_pfn)


MID_BWD_ROWS = 256


def _mid_bwd(dy, dug, duv, wt_g, wt_v, x1, mo, w_pf, w_pm):
    S, D = dy.shape
    tm = _pick(S, MID_BWD_ROWS)
    nt = S // tm

    def body(dy_ref, dug_ref, duv_ref, wg_ref, wv_ref, x1_ref, mo_ref, wpf_ref, wpm_ref,
             dx1_ref, dmo_ref, gpf_ref, gpm_ref, apf, apm):
        i = pl.program_id(0)

        @pl.when(i == 0)
        def _():
            apf[...] = jnp.zeros_like(apf)
            apm[...] = jnp.zeros_like(apm)

        r1, x1h = _rms_parts(x1_ref[...])
        dh2 = jnp.dot(dug_ref[...], wg_ref[...], preferred_element_type=f32) \
            + jnp.dot(duv_ref[...], wv_ref[...], preferred_element_type=f32)
        apf[...] += _colsum8(dh2 * x1h)
        dx1 = dy_ref[...] + _rms_bwd(x1h, r1, wpf_ref[...], dh2)
        dx1_ref[...] = dx1
        rm, moh = _rms_parts(mo_ref[...])
        apm[...] += _colsum8(dx1 * moh)
        dmo_ref[...] = _rms_bwd(moh, rm, wpm_ref[...], dx1).astype(bf16)

        @pl.when(i == nt - 1)
        def _():
            gpf_ref[...] = jnp.sum(apf[...], axis=0, keepdims=True)
            gpm_ref[...] = jnp.sum(apm[...], axis=0, keepdims=True)

    row = pl.BlockSpec((tm, D), lambda i: (i, 0))
    vec = pl.BlockSpec((1, D), lambda i: (0, 0))
    return pl.pallas_call(
        body,
        grid=(nt,),
        in_specs=[row, pl.BlockSpec((tm, dug.shape[1]), lambda i: (i, 0)), pl.BlockSpec((tm, duv.shape[1]), lambda i: (i, 0)),
                  pl.BlockSpec(wt_g.shape, lambda i: (0, 0)), pl.BlockSpec(wt_v.shape, lambda i: (0, 0)),
                  row, row, vec, vec],
        out_specs=[row, row, vec, vec],
        out_shape=[SDS((S, D), f32), SDS((S, D), bf16), SDS((1, D), f32), SDS((1, D), f32)],
        scratch_shapes=[pltpu.VMEM((SUBLANE, D), f32), pltpu.VMEM((SUBLANE, D), f32)],
        compiler_params=_cparams(("arbitrary",), VMEM_BIG),
        name="dh2_mid_bwd",
    )(dy, dug, duv, wt_g, wt_v, x1, mo, w_pf, w_pm)


def _first_bwd(x, dx1, dh_a, dh_b, dh_c, w_pre):
    S, D = x.shape
    tm = _pick(S, 512)
    nt = S // tm
    nc = D // LANE

    def body(*refs):
        x_ref, dx1_ref, a_ref = refs[:3]
        b_refs, c_refs, w_ref = refs[3:3 + nc], refs[3 + nc:3 + 2 * nc], refs[3 + 2 * nc]
        gx_ref, gw_ref, acc, dh_s, sb, sc = refs[4 + 2 * nc:]
        i = pl.program_id(0)

        @pl.when(i == 0)
        def _():
            acc[...] = jnp.zeros_like(acc)

        for j in range(nc):
            cols = slice(j * LANE, (j + 1) * LANE)
            _to_natural(b_refs[j], sb, 4)
            _to_natural(c_refs[j], sc, 16)
            dh_s[:, cols] = (a_ref[:, cols] + sb[...]) + sc[...]
        r, xh = _rms_parts(x_ref[...])
        dh = dh_s[...]
        acc[...] += _colsum8(dh * xh)
        gx_ref[...] = dx1_ref[...] + _rms_bwd(xh, r, w_ref[...], dh)

        @pl.when(i == nt - 1)
        def _():
            gw_ref[...] = jnp.sum(acc[...], axis=0, keepdims=True)

    row = pl.BlockSpec((tm, D), lambda i: (i, 0))
    vec = pl.BlockSpec((1, D), lambda i: (0, 0))
    perm = lambda d: [pl.BlockSpec((d, tm // d, LANE), lambda i, j=j: (0, i, j)) for j in range(nc)]
    return pl.pallas_call(
        body,
        grid=(nt,),
        in_specs=[row, row, row] + perm(4) + perm(16) + [vec],
        out_specs=[row, vec],
        out_shape=[SDS((S, D), f32), SDS((1, D), f32)],
        scratch_shapes=[pltpu.VMEM((SUBLANE, D), f32), pltpu.VMEM((tm, D), f32), pltpu.VMEM((tm, LANE), f32),
                        pltpu.VMEM((tm, LANE), f32)],
        compiler_params=_cparams(("arbitrary",), VMEM_BIG),
        name="first_bwd",
    )(x, dx1, dh_a, *([dh_b.reshape(4, S // 4, D)] * nc), *([dh_c.reshape(16, S // 16, D)] * nc), w_pre)


def _t5_bucket(dist):
    n = jnp.maximum(dist, 0)
    nf = jnp.maximum(n, 1).astype(f32)
    large = MAX_EXACT + (jnp.log(nf / MAX_EXACT) / math.log(MAX_DISTANCE / MAX_EXACT)
                         * (NUM_BUCKETS - MAX_EXACT)).astype(jnp.int32)
    large = jnp.minimum(large, NUM_BUCKETS - 1)
    return jnp.where(n < MAX_EXACT, n, large)


def _bias_consts(d):
    blk = ATTN_BLOCK
    rel = jnp.arange(blk)[:, None] + blk - jnp.arange(2 * blk)[None, :]
    in_win = (rel >= 0) & (rel <= blk)
    bucket = _t5_bucket(rel * d).reshape(1, -1)
    onehot = (bucket == jnp.arange(NUM_BUCKETS)[:, None]).astype(f32)
    return onehot, in_win.astype(f32).reshape(1, -1)


def _bias_build(tab_t, onehot, maskf, name, after):
    H = tab_t.shape[0]

    def body(t_ref, oh_ref, m_ref, after_ref, o_ref):
        b = jnp.dot(t_ref[...], oh_ref[...], precision=HIGHEST, preferred_element_type=f32)
        o_ref[...] = jnp.where(m_ref[...] > 0.5, b, NEG_INF)

    vm = pl.BlockSpec(memory_space=pltpu.VMEM)
    return pl.pallas_call(body, out_shape=SDS((H, onehot.shape[1]), f32), name=name,
                          in_specs=[vm, vm, vm, pl.BlockSpec(memory_space=pl.ANY)], out_specs=vm,
                          )(tab_t, onehot, maskf, after)


def _bias_grad(dbias_flat, onehot, name):
    H = dbias_flat.shape[0]

    def body(g_ref, oh_ref, o_ref):
        o_ref[...] = lax.dot_general(oh_ref[...], g_ref[...], NT, precision=HIGHEST, preferred_element_type=f32)

    return pl.pallas_call(body, out_shape=SDS((NUM_BUCKETS, H), f32), name=name)(dbias_flat, onehot)


ATTN_TILE = 512
ATTN_SUB = ATTN_TILE // ATTN_BLOCK
ATTN_HP = 4
ATTN_WIDE = ATTN_HP * LANE


def _qkv_specs(nt):
    tile = (ATTN_TILE, ATTN_WIDE)
    blk = (ATTN_BLOCK, ATTN_WIDE)
    sec = ATTN_OUT // ATTN_WIDE
    cur = lambda off: (lambda h, t: (jnp.minimum(t, nt - 1), off + h))
    prev = lambda off: (lambda h, t: (jnp.maximum(jnp.minimum(t, nt - 1) * ATTN_SUB - 1, 0), off + h))
    return [pl.BlockSpec(tile, cur(0)), pl.BlockSpec(blk, prev(sec)), pl.BlockSpec(tile, cur(sec)),
            pl.BlockSpec(blk, prev(2 * sec)), pl.BlockSpec(tile, cur(2 * sec))]


def _head_masks():
    lane = lax.broadcasted_iota(jnp.int32, (ATTN_BLOCK, LANE), 1)
    return lane < HEAD_DIM


def _stack_heads(x2, low):
    zero = jnp.zeros_like(x2)
    return jnp.concatenate([jnp.where(low, x2, zero), jnp.where(low, zero, x2)], axis=0)


def _attn_fwd(qkv, bias, bps, name, after=None):
    S = qkv.shape[0]
    nt = S // ATTN_TILE
    scale = HEAD_DIM ** -0.5

    def body(q_ref, kp_ref, kc_ref, vp_ref, vc_ref, b_ref, *rest):
        o_ref, l_ref = rest[-2:]
        t = pl.program_id(1)
        low = _head_masks()
        col = lax.broadcasted_iota(jnp.int32, (2 * ATTN_BLOCK, 2 * ATTN_BLOCK), 1)
        for hp in range(ATTN_HP):
            cols = slice(hp * LANE, (hp + 1) * LANE)
            kk = jnp.concatenate([kp_ref[:, cols], kc_ref[:, cols]], axis=0)
            vv = jnp.concatenate([vp_ref[:, cols], vc_ref[:, cols]], axis=0)
            bias2 = b_ref[2 * hp:2 * hp + 2].reshape(2 * ATTN_BLOCK, 2 * ATTN_BLOCK)
            for b in range(ATTN_SUB):
                lo = b * ATTN_BLOCK
                rows = slice(lo, lo + ATTN_BLOCK)
                keys = slice(lo, lo + 2 * ATTN_BLOCK)
                dead = jnp.logical_and((t * ATTN_SUB + b) % bps == 0, col < ATTN_BLOCK)
                q2 = _stack_heads(q_ref[rows, cols], low)
                kb, vb = kk[keys], vv[keys]
                s = lax.dot_general(q2, kb, NT, preferred_element_type=f32) * scale + bias2
                s = jnp.where(dead, NEG_INF, s)
                m = jnp.max(s, axis=-1, keepdims=True)
                p = jnp.exp(s - m)
                l = jnp.sum(p, axis=-1, keepdims=True)
                o2 = jnp.dot(p.astype(bf16), vb, preferred_element_type=f32) / l
                lse = m + jnp.log(l)
                o_ref[rows, cols] = jnp.where(low, o2[:ATTN_BLOCK], o2[ATTN_BLOCK:])
                l_ref[rows, cols] = jnp.where(low, lse[:ATTN_BLOCK], lse[ATTN_BLOCK:])

    tile = pl.BlockSpec((ATTN_TILE, ATTN_WIDE), lambda h, t: (t, h))
    return pl.pallas_call(
        body,
        grid=(4 // ATTN_HP, nt),
        in_specs=_qkv_specs(nt) + [pl.BlockSpec((2 * ATTN_HP, ATTN_BLOCK, 2 * ATTN_BLOCK), lambda h, t: (h, 0, 0))]
        + ([] if after is None else [pl.BlockSpec(memory_space=pl.ANY)]),
        out_specs=[tile, tile],
        out_shape=[SDS((S, ATTN_OUT), f32), SDS((S, ATTN_OUT), f32)],
        compiler_params=_cparams(("parallel", "parallel")),
        name=name,
    )(qkv, qkv, qkv, qkv, qkv, bias, *([] if after is None else [after]))


def _attn_bwd(qkv, bias, do, dvec, lse, bps, name):
    S = qkv.shape[0]
    nt = S // ATTN_TILE
    scale = HEAD_DIM ** -0.5

    def assemble(parts):
        rows = [parts[0][:ATTN_BLOCK]]
        for b in range(ATTN_SUB - 1):
            rows.append(parts[b][ATTN_BLOCK:] + parts[b + 1][:ATTN_BLOCK])
        rows.append(parts[-1][ATTN_BLOCK:])
        return rows

    def body(q_ref, kp_ref, kc_ref, vp_ref, vc_ref, b_ref, do_ref, dvec_ref, lse_ref,
             dq_ref, dk_ref, dv_ref, db_ref, ck, cv):
        t = pl.program_id(1)
        last = ATTN_TILE - ATTN_BLOCK

        @pl.when(t == 0)
        def _():
            ck[...] = jnp.zeros_like(ck)
            cv[...] = jnp.zeros_like(cv)
            db_ref[...] = jnp.zeros_like(db_ref)

        @pl.when(t < nt)
        def _():
            low = _head_masks()
            col = lax.broadcasted_iota(jnp.int32, (2 * ATTN_BLOCK, 2 * ATTN_BLOCK), 1)
            per_row = lambda t2: jnp.concatenate([t2[:, 0:1], t2[:, HEAD_DIM:HEAD_DIM + 1]], axis=0)
            for hp in range(ATTN_HP):
                cols = slice(hp * LANE, (hp + 1) * LANE)
                kk = jnp.concatenate([kp_ref[:, cols], kc_ref[:, cols]], axis=0)
                vv = jnp.concatenate([vp_ref[:, cols], vc_ref[:, cols]], axis=0)
                bias2 = b_ref[2 * hp:2 * hp + 2].reshape(2 * ATTN_BLOCK, 2 * ATTN_BLOCK)
                dk_parts, dv_parts = [], []
                dsum = None
                for b in range(ATTN_SUB):
                    lo = b * ATTN_BLOCK
                    rows = slice(lo, lo + ATTN_BLOCK)
                    keys = slice(lo, lo + 2 * ATTN_BLOCK)
                    dead = jnp.logical_and((t * ATTN_SUB + b) % bps == 0, col < ATTN_BLOCK)
                    q2 = _stack_heads(q_ref[rows, cols], low)
                    do2 = _stack_heads(do_ref[rows, cols].astype(bf16), low)
                    kb, vb = kk[keys], vv[keys]
                    s = lax.dot_general(q2, kb, NT, preferred_element_type=f32) * scale + bias2
                    s = jnp.where(dead, NEG_INF, s)
                    p = jnp.exp(s - per_row(lse_ref[rows, cols]))
                    dp = lax.dot_general(do2, vb, NT, preferred_element_type=f32)
                    ds = p * (dp - per_row(dvec_ref[rows, cols]))
                    dsum = ds if dsum is None else dsum + ds
                    dsb = ds.astype(bf16)
                    dq2 = jnp.dot(dsb, kb, preferred_element_type=f32) * scale
                    dq_ref[rows, cols] = jnp.where(low, dq2[:ATTN_BLOCK], dq2[ATTN_BLOCK:]).astype(bf16)
                    dk_parts.append(lax.dot_general(dsb, q2, TN, preferred_element_type=f32) * scale)
                    dv_parts.append(lax.dot_general(p.astype(bf16), do2, TN, preferred_element_type=f32))
                db_ref[2 * hp:2 * hp + 2] += dsum.reshape(2, ATTN_BLOCK, 2 * ATTN_BLOCK)
                for parts, carry, out_ref in ((dk_parts, ck, dk_ref), (dv_parts, cv, dv_ref)):
                    rws = assemble(parts)
                    out_ref[:last, cols] = carry[:last, cols].astype(bf16)
                    out_ref[last:, cols] = (carry[last:, cols] + rws[0]).astype(bf16)
                    for b in range(ATTN_SUB):
                        carry[b * ATTN_BLOCK:(b + 1) * ATTN_BLOCK, cols] = rws[b + 1]

        @pl.when(t == nt)
        def _():
            dk_ref[...] = ck[...].astype(bf16)
            dv_ref[...] = cv[...].astype(bf16)

    tile = (ATTN_TILE, ATTN_WIDE)
    cur = pl.BlockSpec(tile, lambda h, t: (jnp.minimum(t, nt - 1), h))
    lag = pl.BlockSpec(tile, lambda h, t: (jnp.maximum(t - 1, 0), h))
    bspec = pl.BlockSpec((2 * ATTN_HP, ATTN_BLOCK, 2 * ATTN_BLOCK), lambda h, t: (h, 0, 0))
    return pl.pallas_call(
        body,
        grid=(4 // ATTN_HP, nt + 1),
        in_specs=_qkv_specs(nt) + [bspec, cur, cur, cur],
        out_specs=[cur, lag, lag, bspec],
        out_shape=[SDS((S, ATTN_OUT), bf16), SDS((S, ATTN_OUT), bf16), SDS((S, ATTN_OUT), bf16),
                   SDS((8, ATTN_BLOCK, 2 * ATTN_BLOCK), f32)],
        scratch_shapes=[pltpu.VMEM(tile, f32), pltpu.VMEM(tile, f32)],
        compiler_params=_cparams(("parallel", "arbitrary")),
        name=name,
    )(qkv, qkv, qkv, qkv, qkv, bias, do, dvec, lse)


def _attn_merge(o0, o1, o2, l0, l1, l2):
    S, W = o0.shape
    R = PERM_ROWS

    def body(o0_ref, o1_ref, o2_ref, l0_ref, l1_ref, l2_ref, y_ref, yb_ref, w0_ref, w1_ref, w2_ref,
             so1, so2, sl1, sl2):
        _to_natural(o1_ref, so1, 4)
        _to_natural(l1_ref, sl1, 4)
        _to_natural(o2_ref, so2, 16)
        _to_natural(l2_ref, sl2, 16)
        a, b, c = l0_ref[...], sl1[...], sl2[...]
        m = jnp.maximum(jnp.maximum(a, b), c)
        ea, eb, ec = jnp.exp(a - m), jnp.exp(b - m), jnp.exp(c - m)
        den = (ea + eb) + ec
        w0, w1, w2 = ea / den, eb / den, ec / den
        y = (w0 * o0_ref[...] + w1 * so1[...]) + w2 * so2[...]
        y_ref[...] = y
        yb_ref[...] = y.astype(bf16)
        w0_ref[...] = w0
        w1_ref[...] = w1
        w2_ref[...] = w2

    nat = pl.BlockSpec((R, LANE), lambda i, j: (i, j))
    v4 = lambda t: t.reshape(4, S // 4, W)
    v16 = lambda t: t.reshape(16, S // 16, W)
    return pl.pallas_call(
        body,
        grid=(S // R, W // LANE),
        in_specs=[nat, _perm_spec(4), _perm_spec(16)] * 2,
        out_specs=[nat] * 5,
        out_shape=[SDS((S, W), f32), SDS((S, W), bf16)] + [SDS((S, W), f32)] * 3,
        scratch_shapes=[pltpu.VMEM((R, LANE), f32)] * 4,
        compiler_params=_cparams(("parallel", "parallel"), VMEM_BIG),
        name="attn_merge",
    )(o0, v4(o1), v16(o2), l0, v4(l1), v16(l2))


def _attn_merge_bwd(dy, y, w0, w1, w2, after=None):
    S, W = dy.shape
    R = PERM_ROWS

    def body(dy_ref, y_ref, w0_ref, w1_ref, w2_ref, *rest):
        a0, a1, a2, b0, b1, b2, sa, sb = rest[-8:]
        dyv = dy_ref[...]
        r = lax.broadcasted_iota(jnp.int32, (LANE, LANE), 0) // HEAD_DIM
        c = lax.broadcasted_iota(jnp.int32, (LANE, LANE), 1) // HEAD_DIM
        seg = jnp.where(r == c, 1.0, 0.0).astype(f32)
        cbar = jnp.dot(dyv * y_ref[...], seg, precision=HIGHEST, preferred_element_type=f32)
        w = w0_ref[...]
        a0[...] = (w * dyv).astype(bf16)
        b0[...] = w * cbar
        for d, w_ref, a_ref, b_ref in ((4, w1_ref, a1, b1), (16, w2_ref, a2, b2)):
            w = w_ref[...]
            sa[...] = w * dyv
            sb[...] = w * cbar
            n = R // d
            for k in range(d):
                rows = pl.ds(k, n, stride=d)
                a_ref[k] = sa[rows, :].astype(bf16)
                b_ref[k] = sb[rows, :]

    nat = pl.BlockSpec((R, LANE), lambda i, j: (i, j))
    shapes = lambda dt: [SDS((S, W), dt), SDS((4, S // 4, W), dt), SDS((16, S // 16, W), dt)]
    outs = pl.pallas_call(
        body,
        grid=(S // R, W // LANE),
        in_specs=[nat] * 5 + ([] if after is None else [pl.BlockSpec(memory_space=pl.ANY)]),
        out_specs=[nat, _perm_spec(4), _perm_spec(16)] * 2,
        out_shape=shapes(bf16) + shapes(f32),
        scratch_shapes=[pltpu.VMEM((R, LANE), f32)] * 2,
        compiler_params=_cparams(("parallel", "parallel"), VMEM_BIG),
        name="attn_merge_bwd",
    )(dy, y, w0, w1, w2, *([] if after is None else [after]))
    return [t.reshape(S, W) for t in outs]


HGRN_SB = 256
HGRN_PAIR = 4


def _chunk_masks():
    r = jnp.arange(HGRN_SB)[:, None]
    c = jnp.arange(HGRN_SB)[None, :]
    same = (r // HGRN_CHUNK) == (c // HGRN_CHUNK)
    return jnp.stack([same & (c <= r), same, same & (c >= r)]).astype(bf16)


def _mask_dot(mask, x):
    hi = x.astype(bf16)
    r1 = x - hi.astype(f32)
    mid = r1.astype(bf16)
    lo = (r1 - mid.astype(f32)).astype(bf16)
    p = jnp.dot(mask, jnp.concatenate([hi, mid, lo], axis=1), preferred_element_type=f32)
    n = x.shape[1]
    return (p[:, :n] + p[:, n:2 * n]) + p[:, 2 * n:]


def _hgrn_prep(q_raw, f_raw, lbv, tril, same):
    sq = _sigmoid(q_raw)
    qs = q_raw * sq
    sig = _sigmoid(f_raw)
    f = lbv + (1.0 - lbv) * sig
    g = jnp.log(f)
    k = 1.0 - f
    G = _mask_dot(tril, g)
    GL = _mask_dot(same, g)
    eG = jnp.exp(G)
    einv = jnp.exp(-G)
    edec = jnp.exp(GL - G)
    return dict(sq=sq, qs=qs, sig=sig, f=f, k=k, eG=eG, einv=einv, edec=edec, eGL=jnp.exp(GL),
                qt=qs * eG, kt=k * einv, kd=k * edec)


def _hgrn_fwd(hg, lb, normw):
    S = hg.shape[0]
    sb = HGRN_SB
    nsb = S // sb
    nch = sb // HGRN_CHUNK

    def body(q_ref, f_ref, v_ref, og_ref, lb_ref, nw_ref, m_ref, y_ref, o_ref, ck_ref, st):
        j = pl.program_id(1)

        @pl.when(j == 0)
        def _():
            st[...] = jnp.zeros_like(st)

        tril_m = m_ref[0]
        tril = tril_m.astype(f32) > 0.5

        def one_head(hh):
            cols = slice(hh * LANE, (hh + 1) * LANE)
            ST = st[hh]
            ck_ref[hh, 0] = ST
            pr = _hgrn_prep(q_ref[:, cols], f_ref[:, cols], lb_ref[:, cols], tril_m, m_ref[1])
            qtb, ktb, kdb = pr["qt"].astype(bf16), pr["kt"].astype(bf16), pr["kd"].astype(bf16)
            eGL = pr["eGL"]
            vb = v_ref[:, cols].astype(bf16)
            A = jnp.where(tril, lax.dot_general(qtb, ktb, NT, preferred_element_type=f32), 0.0)
            o = jnp.dot(A.astype(bf16), vb, preferred_element_type=f32)
            outs = []
            for ci in range(nch):
                lo = ci * HGRN_CHUNK
                sl = slice(lo, lo + HGRN_CHUNK)
                outs.append(o[sl] + lax.dot_general(qtb[sl], ST.astype(bf16), NT, preferred_element_type=f32))
                ST = ST * eGL[lo:lo + 1, :] + lax.dot_general(vb[sl], kdb[sl], TN, preferred_element_type=f32)
            st[hh] = ST
            of = jnp.concatenate(outs, axis=0)
            o_ref[:, cols] = of
            rms = lax.rsqrt(jnp.mean(of * of, axis=-1, keepdims=True) + EPS)
            ogv = og_ref[:, cols]
            y_ref[:, cols] = ((of * rms * nw_ref[...]) * (ogv * _sigmoid(ogv))).astype(bf16)

        for hh in range(HGRN_PAIR):
            one_head(hh)

    wide = HGRN_PAIR * LANE
    col = lambda off: pl.BlockSpec((sb, wide), lambda h, j: (j, off // HGRN_PAIR + h))
    return pl.pallas_call(
        body,
        grid=(4 // HGRN_PAIR, nsb),
        in_specs=[col(0), col(4), col(8), col(12), pl.BlockSpec((1, wide), lambda h, j: (0, h)),
                  pl.BlockSpec((1, LANE), lambda h, j: (0, 0)),
                  pl.BlockSpec((3, sb, sb), lambda h, j: (0, 0, 0))],
        out_specs=[col(0), col(0), pl.BlockSpec((HGRN_PAIR, 1, LANE, LANE), lambda h, j: (h, j, 0, 0))],
        out_shape=[SDS((S, HGRN_W), bf16), SDS((S, HGRN_W), f32), SDS((4, nsb, LANE, LANE), f32)],
        scratch_shapes=[pltpu.VMEM((HGRN_PAIR, LANE, LANE), f32)],
        compiler_params=_cparams(("parallel", "arbitrary")),
        name="hgrn_fwd",
    )(hg, hg, hg, hg, lb, normw, _chunk_masks())


def _hgrn_bwd(hg, o_raw, dy, ck, lb, normw, after=None):
    S = hg.shape[0]
    sb = HGRN_SB
    nsb = S // sb
    nch = sb // HGRN_CHUNK

    def body(q_ref, f_ref, v_ref, og_ref, o_ref, dy_ref, ck_ref, lb_ref, nw_ref, m_ref, *rest):
        dq_ref, df_ref, dv_ref, dog_ref, glb_ref, gnw_ref, dst, alb, anw = rest[-9:]
        j = pl.program_id(1)

        @pl.when(j == 0)
        def _():
            dst[...] = jnp.zeros_like(dst)
            alb[...] = jnp.zeros_like(alb)
            anw[...] = jnp.zeros_like(anw)

        tril_m = m_ref[0]
        tril = tril_m.astype(f32) > 0.5
        nw = nw_ref[...]

        def one_head(hh):
            cols = slice(hh * LANE, (hh + 1) * LANE)
            lbv = lb_ref[:, cols]
            q_raw = q_ref[:, cols]
            pr = _hgrn_prep(q_raw, f_ref[:, cols], lbv, tril_m, m_ref[1])
            qt, kt, kd, eGL = pr["qt"], pr["kt"], pr["kd"], pr["eGL"]
            qtb, ktb, kdb = qt.astype(bf16), kt.astype(bf16), kd.astype(bf16)
            vb = v_ref[:, cols].astype(bf16)

            o = o_ref[:, cols]
            ogv = og_ref[:, cols]
            sog = _sigmoid(ogv)
            rms = lax.rsqrt(jnp.mean(o * o, axis=-1, keepdims=True) + EPS)
            oh = o * rms
            dyv = dy_ref[:, cols]
            dog_ref[:, cols] = (dyv * (oh * nw) * (sog * (1.0 + ogv * (1.0 - sog)))).astype(bf16)
            dohw = dyv * (ogv * sog)
            anw[:, cols] += _colsum8(dohw * oh)
            doh = dohw * nw
            do = rms * (doh - oh * jnp.mean(doh * oh, axis=-1, keepdims=True))
            dob = do.astype(bf16)

            Ab = jnp.where(tril, lax.dot_general(qtb, ktb, NT, preferred_element_type=f32), 0.0).astype(bf16)
            dAb = jnp.where(tril, lax.dot_general(dob, vb, NT, preferred_element_type=f32), 0.0).astype(bf16)
            dv_acc = lax.dot_general(Ab, dob, TN, preferred_element_type=f32)
            dqt = jnp.dot(dAb, ktb, preferred_element_type=f32)
            dkt = lax.dot_general(dAb, qtb, TN, preferred_element_type=f32)

            ST = ck_ref[hh, 0]
            states = []
            for ci in range(nch):
                lo = ci * HGRN_CHUNK
                sl = slice(lo, lo + HGRN_CHUNK)
                states.append(ST)
                ST = ST * eGL[lo:lo + 1, :] + lax.dot_general(vb[sl], kdb[sl], TN, preferred_element_type=f32)

            dST = dst[hh]
            dqt_i, dkd_i, dv_i, deg_i = [None] * nch, [None] * nch, [None] * nch, [None] * nch
            for ci in reversed(range(nch)):
                lo = ci * HGRN_CHUNK
                sl = slice(lo, lo + HGRN_CHUNK)
                ST0 = states[ci]
                dSTb = dST.astype(bf16)
                dv_i[ci] = lax.dot_general(kdb[sl], dSTb, NT, preferred_element_type=f32)
                dqt_i[ci] = jnp.dot(dob[sl], ST0.astype(bf16), preferred_element_type=f32)
                dkd_i[ci] = jnp.dot(vb[sl], dSTb, preferred_element_type=f32)
                deg_i[ci] = jnp.broadcast_to(jnp.sum(dST * ST0, axis=0, keepdims=True), (HGRN_CHUNK, LANE))
                dST = dST * eGL[lo:lo + 1, :] + lax.dot_general(dob[sl], qtb[sl], TN, preferred_element_type=f32)
            dst[hh] = dST

            dqt = dqt + jnp.concatenate(dqt_i, axis=0)
            dkd = jnp.concatenate(dkd_i, axis=0)
            dv_ref[:, cols] = (dv_acc + jnp.concatenate(dv_i, axis=0)).astype(bf16)
            deg = jnp.concatenate(deg_i, axis=0)

            dqs = dqt * pr["eG"]
            dkdkd = dkd * kd
            dG = dqt * qt - dkt * kt - dkdkd
            dk = dkt * pr["einv"] + dkd * pr["edec"]
            dGL = _mask_dot(m_ref[1], dkdkd) + eGL * deg
            dg = _mask_dot(m_ref[2], dG) + dGL
            df = dg / pr["f"] - dk
            sig = pr["sig"]
            df_ref[:, cols] = (df * (1.0 - lbv) * (sig * (1.0 - sig))).astype(bf16)
            alb[:, cols] += _colsum8(df * (1.0 - sig))
            sq = pr["sq"]
            dq_ref[:, cols] = (dqs * (sq * (1.0 + q_raw * (1.0 - sq)))).astype(bf16)

        for hh in range(HGRN_PAIR):
            one_head(hh)

        @pl.when(j == nsb - 1)
        def _():
            glb_ref[...] = jnp.broadcast_to(jnp.sum(alb[...], axis=0, keepdims=True), (SUBLANE, wide))
            gnw_ref[...] = jnp.broadcast_to(jnp.sum(anw[...], axis=0, keepdims=True), (SUBLANE, wide))

    wide = HGRN_PAIR * LANE
    rev = lambda off: pl.BlockSpec((sb, wide), lambda h, j: (nsb - 1 - j, off // HGRN_PAIR + h))
    stat = pl.BlockSpec((SUBLANE, wide), lambda h, j: (0, h))
    return pl.pallas_call(
        body,
        grid=(4 // HGRN_PAIR, nsb),
        in_specs=[rev(0), rev(4), rev(8), rev(12), rev(0), rev(0),
                  pl.BlockSpec((HGRN_PAIR, 1, LANE, LANE), lambda h, j: (h, nsb - 1 - j, 0, 0)),
                  pl.BlockSpec((1, wide), lambda h, j: (0, h)), pl.BlockSpec((1, LANE), lambda h, j: (0, 0)),
                  pl.BlockSpec((3, sb, sb), lambda h, j: (0, 0, 0))]
        + ([] if after is None else [pl.BlockSpec(memory_space=pl.ANY)]),
        out_specs=[rev(0), rev(0), rev(0), rev(0), stat, stat],
        out_shape=[SDS((S, HGRN_W), bf16)] * 4 + [SDS((SUBLANE, HGRN_W), f32)] * 2,
        scratch_shapes=[pltpu.VMEM((HGRN_PAIR, LANE, LANE), f32), pltpu.VMEM((SUBLANE, wide), f32),
                        pltpu.VMEM((SUBLANE, wide), f32)],
        compiler_params=_cparams(("parallel", "arbitrary")),
        name="hgrn_bwd",
    )(hg, hg, hg, hg, o_raw, dy, ck, lb, normw, _chunk_masks(), *([] if after is None else [after]))


def _lb_fwd(raw):
    def body(r_ref, o_ref):
        r = r_ref[...]
        m = jnp.max(r, axis=0, keepdims=True)
        e = jnp.exp(r - m)
        o_ref[...] = (e / jnp.sum(e, axis=0, keepdims=True))[0:1]

    return pl.pallas_call(body, out_shape=SDS((1, raw.shape[1]), f32), name="lb_fwd")(raw)


def _lb_bwd(raw, dlb):
    def body(r_ref, d_ref, o_ref):
        r = r_ref[...]
        m = jnp.max(r, axis=0, keepdims=True)
        e = jnp.exp(r - m)
        s = e / jnp.sum(e, axis=0, keepdims=True)
        s0 = s[0:1]
        onehot0 = jnp.where(lax.broadcasted_iota(jnp.int32, r.shape, 0) == 0, 1.0, 0.0)
        o_ref[...] = d_ref[...] * s0 * (onehot0 - s)

    return pl.pallas_call(body, out_shape=SDS(raw.shape, f32), name="lb_bwd")(raw, dlb)


def _gate_fwd(ya, yh, w_ba, w_bh, gc):
    S = ya.shape[0]
    D = w_ba.shape[1]
    tm = _pick(S, MM_ROWS)

    def body(ya_ref, yh_ref, wa_ref, wh_ref, g0_ref, g1_ref, a_ref, b_ref, o_ref):
        a = jnp.dot(ya_ref[...], wa_ref[...], preferred_element_type=f32).astype(bf16)
        b = jnp.dot(yh_ref[...], wh_ref[...], preferred_element_type=f32).astype(bf16)
        a_ref[...] = a
        b_ref[...] = b
        s0, s1 = _sigmoid(g0_ref[...].astype(f32)), _sigmoid(g1_ref[...].astype(f32))
        o_ref[...] = (s0 * a.astype(f32) + s1 * b.astype(f32)).astype(bf16)

    row = pl.BlockSpec((tm, D), lambda i: (i, 0))
    act = pl.BlockSpec((tm, ya.shape[1]), lambda i: (i, 0))
    wspec = pl.BlockSpec(w_ba.shape, lambda i: (0, 0))
    return pl.pallas_call(
        body,
        grid=(S // tm,),
        in_specs=[act, act, wspec, wspec, row, pl.BlockSpec((tm, D), lambda i: (i, 1))],
        out_specs=[row, row, row],
        out_shape=[SDS((S, D), bf16)] * 3,
        compiler_params=_cparams(("parallel",), VMEM_BIG),
        name="branch_gate_fwd",
    )(ya, yh, w_ba, w_bh, gc, gc)


def _gate_bwd(dmo, w_out, a, b, gc, w_ba, w_bh):
    S, D = a.shape
    W = w_ba.shape[0]
    tm = _pick(S, MM_ROWS)

    def body(dmo_ref, wo_ref, a_ref, b_ref, g0_ref, g1_ref, wa_ref, wh_ref,
             da_ref, db_ref, dg_ref, dya_ref, dyh_ref):
        dm = lax.dot_general(dmo_ref[...], wo_ref[...], NT, preferred_element_type=f32)
        dmv = dm.astype(bf16).astype(f32)
        s0, s1 = _sigmoid(g0_ref[...].astype(f32)), _sigmoid(g1_ref[...].astype(f32))
        da = (dmv * s0).astype(bf16)
        db = (dmv * s1).astype(bf16)
        da_ref[...] = da
        db_ref[...] = db
        dg_ref[:, :D] = (dmv * a_ref[...].astype(f32) * (s0 * (1.0 - s0))).astype(bf16)
        dg_ref[:, D:] = (dmv * b_ref[...].astype(f32) * (s1 * (1.0 - s1))).astype(bf16)
        dya_ref[...] = lax.dot_general(da, wa_ref[...], NT, preferred_element_type=f32)
        dyh_ref[...] = lax.dot_general(db, wh_ref[...], NT, preferred_element_type=f32)

    row = pl.BlockSpec((tm, D), lambda i: (i, 0))
    wide = pl.BlockSpec((tm, 2 * D), lambda i: (i, 0))
    narrow = pl.BlockSpec((tm, W), lambda i: (i, 0))
    whole = lambda t: pl.BlockSpec(t.shape, lambda i: (0, 0))
    return pl.pallas_call(
        body,
        grid=(S // tm,),
        in_specs=[row, whole(w_out), row, row, row, pl.BlockSpec((tm, D), lambda i: (i, 1)), whole(w_ba), whole(w_bh)],
        out_specs=[row, row, wide, narrow, narrow],
        out_shape=[SDS((S, D), bf16), SDS((S, D), bf16), SDS((S, 2 * D), bf16), SDS((S, W), f32), SDS((S, W), f32)],
        compiler_params=_cparams(("parallel",), VMEM_BIG),
        name="gate_bwd_fused",
    )(dmo, w_out, a, b, gc, gc, w_ba, w_bh)


CONV_ROWS = 512
INV_SQRT2 = 0.7071067811865476
INV_SQRT_2PI = 0.3989422804014327


CONV_HALO = 16


def _shift_down(cur, prev, k):
    x = pltpu.roll(cur, k, 0)
    row = lax.broadcasted_iota(jnp.int32, (SUBLANE, LANE), 0)
    head = jnp.where(row < k, pltpu.roll(prev, k, 0)[:SUBLANE], x[:SUBLANE])
    return jnp.concatenate([head, x[SUBLANE:]], axis=0)


def _shift_up(cur, nxt, k):
    R = cur.shape[0]
    x = pltpu.roll(cur, R - k, 0)
    row = lax.broadcasted_iota(jnp.int32, (SUBLANE, LANE), 0)
    tail = jnp.where(row >= SUBLANE - k, pltpu.roll(nxt, SUBLANE - k, 0), x[R - SUBLANE:])
    return jnp.concatenate([x[:R - SUBLANE], tail], axis=0)


def _conv_rows(u_ref, w, b, r0, first):
    R = CONV_ROWS
    cur = u_ref[pl.ds(r0, R), :].astype(f32)
    prev = u_ref[pl.ds(pl.multiple_of(jnp.maximum(r0 - CONV_HALO, 0), CONV_HALO), CONV_HALO), :].astype(f32)
    prev = jnp.where(first, 0.0, prev)
    x1 = _shift_down(cur, prev, 1)
    x2 = _shift_down(cur, prev, 2)
    c = ((b + w[0:1] * x2) + w[1:2] * x1) + w[2:3] * cur
    return c, x2, x1, cur


def _conv_fwd(ug, uv, wg, wv, bg, bv):
    S, F = ug.shape
    nchunk = S // CONV_ROWS

    def body(ug_ref, uv_ref, wg_ref, wv_ref, bg_ref, bv_ref, o_ref):
        wgv, wvv, bgv, bvv = wg_ref[...], wv_ref[...], bg_ref[...], bv_ref[...]

        def step(ci, carry):
            r0 = pl.multiple_of(ci * CONV_ROWS, CONV_ROWS)
            cg = _conv_rows(ug_ref, wgv, bgv, r0, ci == 0)[0]
            cv = _conv_rows(uv_ref, wvv, bvv, r0, ci == 0)[0]
            gelu = 0.5 * cg * (1.0 + lax.erf(cg * INV_SQRT2))
            o_ref[pl.ds(r0, CONV_ROWS), :] = (gelu * cv).astype(bf16)
            return carry

        lax.fori_loop(0, nchunk, step, 0)

    col = pl.BlockSpec((S, LANE), lambda j: (0, j))
    w3 = pl.BlockSpec((3, LANE), lambda j: (0, j))
    b1 = pl.BlockSpec((1, LANE), lambda j: (0, j))
    return pl.pallas_call(
        body,
        grid=(F // LANE,),
        in_specs=[col, col, w3, w3, b1, b1],
        out_specs=col,
        out_shape=SDS((S, F), bf16),
        compiler_params=_cparams(("parallel",), VMEM_BIG),
        name="conv_fwd",
    )(ug, uv, wg, wv, bg, bv)


def _conv_bwd(ug, uv, dact, wg, wv, bg, bv):
    S, F = ug.shape
    R = CONV_ROWS
    nchunk = S // R

    def body(ug_ref, uv_ref, da_ref, wg_ref, wv_ref, bg_ref, bv_ref, dug_ref, duv_ref, sg_ref, sv_ref, dcg, dcv):
        wgv, wvv, bgv, bvv = wg_ref[...], wv_ref[...], bg_ref[...], bv_ref[...]
        zero = jnp.zeros((SUBLANE, LANE), f32)

        def fwd_step(ci, acc):
            r0 = pl.multiple_of(ci * R, R)
            cg, g2, g1, g0 = _conv_rows(ug_ref, wgv, bgv, r0, ci == 0)
            cv, v2, v1, v0 = _conv_rows(uv_ref, wvv, bvv, r0, ci == 0)
            da = da_ref[pl.ds(r0, R), :].astype(f32)
            cdf = 0.5 * (1.0 + lax.erf(cg * INV_SQRT2))
            pdf = INV_SQRT_2PI * jnp.exp(-0.5 * cg * cg)
            dg = da * cv * (cdf + cg * pdf)
            dv = da * (cg * cdf)
            dcg[pl.ds(r0, R), :] = dg
            dcv[pl.ds(r0, R), :] = dv
            new = (acc[0] + _colsum8(dg * g2), acc[1] + _colsum8(dg * g1), acc[2] + _colsum8(dg * g0),
                   acc[3] + _colsum8(dg),
                   acc[4] + _colsum8(dv * v2), acc[5] + _colsum8(dv * v1), acc[6] + _colsum8(dv * v0),
                   acc[7] + _colsum8(dv))
            return new

        acc = lax.fori_loop(0, nchunk, fwd_step, (zero,) * 8)
        rows = lax.broadcasted_iota(jnp.int32, (SUBLANE, LANE), 0)

        def stats(parts):
            out = jnp.zeros((SUBLANE, LANE), f32)
            for k, pt in enumerate(parts):
                out = jnp.where(rows == k, jnp.sum(pt, axis=0, keepdims=True), out)
            return out

        sg_ref[...] = stats(acc[0:4])
        sv_ref[...] = stats(acc[4:8])

        def du_rows(dc, w, r0, last):
            cur = dc[pl.ds(r0, R), :]
            nxt = dc[pl.ds(pl.multiple_of(jnp.minimum(r0 + R, S - SUBLANE), SUBLANE), SUBLANE), :]
            nxt = jnp.where(last, 0.0, nxt)
            return w[2:3] * cur + w[1:2] * _shift_up(cur, nxt, 1) + w[0:1] * _shift_up(cur, nxt, 2)

        def bwd_step(ci, carry):
            r0 = pl.multiple_of(ci * R, R)
            last = ci == nchunk - 1
            dug_ref[pl.ds(r0, R), :] = du_rows(dcg, wgv, r0, last).astype(bf16)
            duv_ref[pl.ds(r0, R), :] = du_rows(dcv, wvv, r0, last).astype(bf16)
            return carry

        lax.fori_loop(0, nchunk, bwd_step, 0)

    col = pl.BlockSpec((S, LANE), lambda j: (0, j))
    w3 = pl.BlockSpec((3, LANE), lambda j: (0, j))
    b1 = pl.BlockSpec((1, LANE), lambda j: (0, j))
    st = pl.BlockSpec((SUBLANE, LANE), lambda j: (0, j))
    return pl.pallas_call(
        body,
        grid=(F // LANE,),
        in_specs=[col, col, col, w3, w3, b1, b1],
        out_specs=[col, col, st, st],
        out_shape=[SDS((S, F), bf16), SDS((S, F), bf16), SDS((SUBLANE, F), f32), SDS((SUBLANE, F), f32)],
        scratch_shapes=[pltpu.VMEM((S, LANE), f32), pltpu.VMEM((S, LANE), f32)],
        compiler_params=_cparams(("parallel",), VMEM_BIG),
        name="conv_bwd",
    )(ug, uv, dact, wg, wv, bg, bv)


def _adam_math(w, g, m, v):
    m = ADAM_B1 * m + (1.0 - ADAM_B1) * g
    v = ADAM_B2 * v + (1.0 - ADAM_B2) * (g * g)
    m_hat = m / (1.0 - ADAM_B1 ** ADAM_STEP)
    v_hat = v / (1.0 - ADAM_B2 ** ADAM_STEP)
    delta = -ADAM_LR * (m_hat / (jnp.sqrt(v_hat) + ADAM_EPS) + ADAM_WD * w)
    return delta, m, v


def _adamw(w, m, v, g, name):
    R, C = w.shape
    parts = g.ndim == 3
    tr = R
    if R % 16 == 0:
        for t in range(R, 0, -16):
            if R % t == 0 and t * C * 4 <= ADAM_BLOCK_BYTES:
                tr = t
                break

    def body(w_ref, m_ref, v_ref, g_ref, go_ref, d_ref, mo_ref, vo_ref):
        if parts:
            gv = ((g_ref[0].astype(f32) + g_ref[1].astype(f32)) + g_ref[2].astype(f32)) + g_ref[3].astype(f32)
        else:
            gv = g_ref[...]
        go_ref[...] = gv
        d, mn, vn = _adam_math(w_ref[...], gv, m_ref[...], v_ref[...])
        d_ref[...] = d
        mo_ref[...] = mn
        vo_ref[...] = vn

    row = pl.BlockSpec((tr, C), lambda i: (i, 0))
    gspec = pl.BlockSpec((4, tr, C), lambda i: (0, i, 0)) if parts else row
    return pl.pallas_call(
        body,
        grid=(R // tr,),
        in_specs=[row, row, row, gspec],
        out_specs=[row] * 4,
        out_shape=[SDS((R, C), f32)] * 4,
        compiler_params=_cparams(("parallel",), VMEM_BIG),
        name=name,
    )(w, m, v, g)


def _sum8(parts, name):
    _, R, C = parts.shape

    def body(p_ref, o_ref):
        acc = p_ref[0]
        for k in range(1, 8):
            acc = acc + p_ref[k]
        o_ref[...] = acc

    return pl.pallas_call(body, out_shape=SDS((R, C), f32), name=name)(parts)


def _pair_add(by_core, b, name):
    _, K, R, C = by_core.shape
    tr = R // 2 if R % 32 == 0 else R

    def body(c_ref, a_ref, b_ref, o_ref):
        o_ref[...] = (a_ref[0].astype(f32) + b_ref[...].astype(f32)).astype(bf16)

    blk = pl.BlockSpec((1, tr, C), lambda k, i, c: (k, i, 0))
    return pl.pallas_call(
        body,
        grid_spec=pltpu.PrefetchScalarGridSpec(
            num_scalar_prefetch=1,
            grid=(K, R // tr),
            in_specs=[pl.BlockSpec((1, 1, tr, C), lambda k, i, c: (c[0], k, i, 0)), blk],
            out_specs=blk,
        ),
        out_shape=SDS((K, R, C), bf16),
        compiler_params=_cparams(("parallel", "parallel")),
        name=name,
    )(lax.axis_index("c").astype(jnp.int32).reshape(1), by_core, b)


_ANY = pl.BlockSpec(memory_space=pl.ANY)


def _chip_out_shape(src, gather):
    return SDS((4,) + tuple(src.shape if gather else src.shape[1:]), src.dtype)


def _fill_own(out, src, gather):
    mine = 2 * lax.axis_index("x") + lax.axis_index("y")
    own = src if gather else lax.dynamic_index_in_dim(src, mine, axis=0, keepdims=False)
    return lax.dynamic_update_index_in_dim(out, own, mine, axis=0)


_HBM = pl.BlockSpec(memory_space=pltpu.HBM)
_SEM = pl.BlockSpec(memory_space=pltpu.SEMAPHORE)
_EFFECT = pltpu.SideEffectType.DATAFLOW_SIDE_EFFECTING
_SPLIT_PEERS = {"chip_gather": 3, "chip_xchg": 3, "core_gather": 1, "core_swap": 1}


def _split_land(src, kind):
    if kind == "core_gather":
        return SDS((2,) + tuple(src.shape), src.dtype)
    if kind == "core_swap":
        return SDS(tuple(src.shape[1:]), src.dtype)
    return _chip_out_shape(src, kind == "chip_gather")


def _split_copies(src_ref, land_ref, sems, kind):
    x, y, c = lax.axis_index("x"), lax.axis_index("y"), lax.axis_index("c")
    n = _SPLIT_PEERS[kind]
    if kind == "core_gather":
        routes = [((x, y, 1 - c), src_ref, land_ref.at[c], land_ref.at[1 - c])]
    elif kind == "core_swap":
        routes = [((x, y, 1 - c), src_ref.at[1 - c], land_ref, land_ref)]
    else:
        mine = 2 * x + y
        gather = kind == "chip_gather"
        routes = [((px, py, c), src_ref if gather else src_ref.at[2 * px + py], land_ref.at[mine],
                   land_ref.at[2 * px + py]) for px, py in [(1 - x, y), (x, 1 - y), (1 - x, 1 - y)]]
    sends, recvs = [], []
    for j, (peer, piece, there, here) in enumerate(routes):
        sends.append(pltpu.make_async_remote_copy(src_ref=piece, dst_ref=there, send_sem=sems[j],
                                                  recv_sem=sems[n + j], device_id=peer, device_id_type=MESH))
        recvs.append(pltpu.make_async_remote_copy(src_ref=piece, dst_ref=here, send_sem=sems[j],
                                                  recv_sem=sems[n + j], device_id=peer, device_id_type=MESH))
    return sends, recvs


def _split_start(src, kind, name, after=None):
    land = _split_land(src, kind)
    ns = 2 * _SPLIT_PEERS[kind]
    n_in = 2 if after is None else 3

    def body(*refs):
        src_ref, land_ref = refs[:2]
        outs = refs[n_in:]
        for cp in _split_copies(src_ref, land_ref, outs[:ns], kind)[0]:
            cp.start()
        token = outs[ns + 2]
        token[...] = jnp.zeros_like(token)

    res = pl.pallas_call(
        body,
        name=name,
        out_shape=(pltpu.SemaphoreType.DMA(()),) * ns
        + (pltpu.HBM(src.shape, src.dtype), pltpu.HBM(land.shape, land.dtype), SDS((SUBLANE, LANE), f32)),
        in_specs=(_HBM, _HBM) + (() if after is None else (_ANY,)),
        out_specs=(_SEM,) * ns + (_HBM, _HBM, pl.BlockSpec(memory_space=pltpu.VMEM)),
        input_output_aliases={0: ns, 1: ns + 1},
        compiler_params=pltpu.CompilerParams(has_side_effects=_EFFECT),
    )(pltpu.with_memory_space_constraint(src, pltpu.HBM),
      pltpu.with_memory_space_constraint(lax.empty(land.shape, land.dtype), pltpu.HBM),
      *(() if after is None else (after,)))
    return (res[:ns], res[ns], res[ns + 1]), res[ns + 2]


def _split_wait(state, after, kind, name):
    sems, src_thru, land_thru = state
    ns = 2 * _SPLIT_PEERS[kind]

    def body(src_ref, land_ref, *rest):
        sends, recvs = _split_copies(src_ref, land_ref, rest[:ns], kind)
        for cp in recvs:
            cp.wait_recv()
        for cp in sends:
            cp.wait_send()

    src_out, got = pl.pallas_call(
        body,
        name=name,
        out_shape=(pltpu.HBM(src_thru.shape, src_thru.dtype), pltpu.HBM(land_thru.shape, land_thru.dtype)),
        in_specs=(_HBM, _HBM) + (_SEM,) * ns + (_ANY,),
        out_specs=(_HBM, _HBM),
        input_output_aliases={0: 0, 1: 1},
        compiler_params=pltpu.CompilerParams(has_side_effects=_EFFECT),
    )(src_thru, land_thru, *sems, after)
    if kind == "core_swap":
        return got, src_out
    if kind == "core_gather":
        return got, src_out
    return _fill_own(got, src_out, kind == "chip_gather")


def _by_device(got, mine):
    own = (jnp.arange(2) == lax.axis_index("c"))[None, :, None, None]
    full = jnp.where(own, mine[:, None], jnp.swapaxes(got, 0, 1))
    return full.reshape((8,) + tuple(mine.shape[1:]))


def _core_gather(src, name):
    def body(src_ref, out_ref, send_sem, recv_sem):
        x, y, c = lax.axis_index("x"), lax.axis_index("y"), lax.axis_index("c")
        cp = pltpu.make_async_remote_copy(src_ref=src_ref, dst_ref=out_ref.at[c], send_sem=send_sem,
                                          recv_sem=recv_sem, device_id=(x, y, 1 - c), device_id_type=MESH)
        cp.start()
        pltpu.make_async_remote_copy(src_ref=src_ref, dst_ref=out_ref.at[1 - c], send_sem=send_sem,
                                     recv_sem=recv_sem, device_id=(x, y, 1 - c), device_id_type=MESH).wait_recv()
        cp.wait_send()

    out = pl.pallas_call(
        body,
        in_specs=[_ANY],
        out_specs=_ANY,
        out_shape=SDS((2,) + tuple(src.shape), src.dtype),
        scratch_shapes=[pltpu.SemaphoreType.DMA, pltpu.SemaphoreType.DMA],
        name=name,
    )(src)
    return out, src


_PACK_A = (("w_in", (1088, 1024)),)
_PACK_B = (("w_ba", (512, 128)), ("w_bh", (512, 128)), ("w_out", (128, 1024)), ("w_up", (704, 1024)),
           ("w_down", (352, 1024)))
_PACK_SIZES = _PACK_A + _PACK_B
_TRANSPOSED = ("w_in", "w_up")


def _slab_rows(sizes):
    return sum(r * c for _, (r, c) in sizes) // D_MODEL


def _pack_rows(d, sizes):
    n = d[sizes[0][0]].shape[0]
    return jnp.concatenate([d[k].reshape(n, -1, D_MODEL) for k, _ in sizes], axis=1)


def _unpack_rows(slab, sizes):
    n = slab.shape[0]
    out, lo = {}, 0
    for key, (r, c) in sizes:
        rows = r * c // D_MODEL
        out[key] = slab[:, lo:lo + rows].reshape(n, r, c)
        lo += rows
    return out


def _by_core(gslab):
    return jnp.swapaxes(gslab.reshape((4, 2) + gslab.shape[1:]), 0, 1)


def _cols_to_full(t):
    return jnp.swapaxes(t, 0, 1).reshape(t.shape[1], -1)


def _full_to_cols(t):
    K = t.shape[0]
    return jnp.swapaxes(t.reshape(K, 8, -1), 0, 1)


_SMALL = (("pre_mix_norm", (1, 1024)), ("rel_bias", (32, 24)), ("hgrn_lb_raw", (2, 512)), ("hgrn_norm", (1, 128)),
          ("post_mix_norm", (1, 1024)), ("pre_ffn_norm", (1, 1024)), ("conv_b", (1, 5632)),
          ("post_ffn_norm", (1, 1024)))
_SMALL_ROWS = 96
_CONVW_ROWS = 136


_SMALL_USED = sum(r * c for _, (r, c) in _SMALL)


def _pack_small(d, extra=None):
    flat = jnp.concatenate([d[k].reshape(-1) for k, _ in _SMALL] + ([] if extra is None else [extra.reshape(-1)]))
    flat = jnp.pad(flat, (0, _SMALL_ROWS * LANE - flat.shape[0]))
    return flat.reshape(_SMALL_ROWS, LANE)


def _unpack_small(p):
    flat = p.reshape(-1)
    out, lo = {}, 0
    for k, shp in _SMALL:
        n = shp[0] * shp[1]
        out[k] = flat[lo:lo + n].reshape(shp)
        lo += n
    return out


def _local_step(x, tgt, P, plan):
    S = x.shape[0]
    P = dict(P)
    lb = _lb_fwd(P["hgrn_lb_raw"])
    hs = _prep(x, P["pre_mix_norm"], plan.start_token())
    h1 = hs[0]
    consts = [_bias_consts(d) for d in DILATIONS]
    biases, dep = [], h1
    for g in range(N_GROUPS):
        tab_t = P["rel_bias"][:, 8 * g:8 * g + 8].T
        dep = _bias_build(tab_t, consts[g][0], consts[g][1], f"bias_build{g}", dep)
        biases.append(dep.reshape(8, ATTN_BLOCK, 2 * ATTN_BLOCK))
    W = dict(plan.weights_a(dep))
    qkv0, hg, gc = _mm_fanout(h1, [W["wt_qkv"][0], W["wt_hg"], W["wt_gate"]], "nt", [bf16, f32, bf16], "proj_natural")
    qkv = [qkv0] + [_mm(hs[g], W["wt_qkv"][g], "nt", bf16, f"proj_qkv{g}") for g in (1, 2)]
    obuf, lbuf, token = [], [], None
    for g, d in enumerate(DILATIONS):
        o_g, l_g = _attn_fwd(qkv[g], biases[g], (S // d) // ATTN_BLOCK, f"attn_fwd{g}", after=token)
        lbuf.append(l_g)
        obuf.append(o_g)
        if g == 0:
            token = plan.forward_b(o_g)
    y_attn, y_attn_b, w0, w1, w2 = _attn_merge(obuf[0], obuf[1], obuf[2], lbuf[0], lbuf[1], lbuf[2])
    y_hgrn, o_raw, ck = _hgrn_fwd(hg, lb, P["hgrn_norm"])
    wb = plan.weights_b(y_hgrn)
    P["conv_w"] = wb.pop("conv_w")
    W.update(wb)
    a, b, merged = _gate_fwd(y_attn_b, y_hgrn, W["w_ba"], W["w_bh"], gc)
    mo, x1, h2 = _mid_fwd(x, merged, W["w_out"], P["post_mix_norm"], P["pre_ffn_norm"])
    ug, uv = _mm_fanout(h2, [W["wt_up_g"], W["wt_up_v"]], "nt", [bf16, bf16], "up_proj")
    cw_g, cw_v = P["conv_w"][:, :D_FF], P["conv_w"][:, D_FF:]
    cb_g, cb_v = P["conv_b"][:, :D_FF], P["conv_b"][:, D_FF:]
    act = _conv_fwd(ug, uv, cw_g, cw_v, cb_g, cb_v)
    loss, dy, dfo, g_post_ffn = _final(x1, act, W["w_down"], tgt, P["post_ffn_norm"])
    gW_down = _mm(act, dfo, "tn", bf16, "gw_down")
    dact = _mm(dfo, W["w_down"], "nt", bf16, "d_act")
    dug, duv, st_g, st_v = _conv_bwd(ug, uv, dact, cw_g, cw_v, cb_g, cb_v)
    gW_up_g = _mm(dug, h2, "tn", bf16, "gw_up_gate")
    gW_up_v = _mm(duv, h2, "tn", bf16, "gw_up_val")
    dx1, dmo, g_pre_ffn, g_post_mix = _mid_bwd(dy, dug, duv, W["wt_up_g"], W["wt_up_v"], x1, mo, P["pre_ffn_norm"],
                                               P["post_mix_norm"])
    gW_out = _mm(merged, dmo, "tn", bf16, "gw_out")
    da, db, dgc, dyattn, dyhgrn = _gate_bwd(dmo, W["w_out"], a, b, gc, W["w_ba"], W["w_bh"])
    gW_ba = _mm(y_attn_b, da, "tn", bf16, "gw_ba")
    gW_bh = _mm(y_hgrn, db, "tn", bf16, "gw_bh")
    big_b = dict(w_ba=gW_ba, w_bh=gW_bh, w_out=gW_out, w_up=[gW_up_g, gW_up_v], w_down=gW_down)
    dos = _attn_merge_bwd(dyattn, y_attn, w0, w1, w2, after=plan.grads_b_start(big_b))
    dq_h, df_h, dv_h, dog_h, glb8, gnw8 = _hgrn_bwd(hg, o_raw, dyhgrn, ck, lb, P["hgrn_norm"],
                                                   after=plan.grads_b_exchange(dos[5]))
    dhg = [dq_h, df_h, dv_h, dog_h]
    g_lb_raw = _lb_bwd(P["hgrn_lb_raw"], glb8[0:1])
    gn = gnw8[0:1]
    g_hgrn_norm = (gn[:, 0:128] + gn[:, 128:256]) + (gn[:, 256:384] + gn[:, 384:512])
    dqkvs, gW_qkv, g_rel = [], [], []
    for g, d in enumerate(DILATIONS):
        dq, dk, dv, dbias = _attn_bwd(qkv[g], biases[g], dos[g], dos[3 + g], lbuf[g], (S // d) // ATTN_BLOCK,
                                      f"attn_bwd{g}")
        dqkvs.append([dq, dk, dv])
        gW_qkv.append(_mm(dqkvs[g], hs[g], "tn", bf16, f"gw_qkv{g}"))
        g_rel.append(_bias_grad(dbias.reshape(8, -1), consts[g][0], f"bias_grad{g}"))
    gW_hg = _mm(dhg, h1, "tn", bf16, "gw_hg")
    gW_gate = _mm(dgc, h1, "tn", bf16, "gw_gate")
    gW_in = gW_qkv + [gW_hg, gW_gate]
    token = plan.grads_a_start(gW_in)
    dh_perm = [_mm(dqkvs[g], W["wt_qkv"][g], "nn", f32, f"dh1_qkv{g}", after=token) for g in (1, 2)]
    token = plan.grads_a_exchange(dh_perm[1])
    dh_main = _mm(dqkvs[0] + dhg + [dgc], [W["wt_qkv"][0], W["wt_hg"], W["wt_gate"]], "nn", f32, "dh1_main",
                  after=token)
    grad_x, g_pre_mix = _first_bwd(x, dx1, dh_main, dh_perm[0], dh_perm[1], P["pre_mix_norm"])

    g_conv_w = jnp.concatenate([st_g[0:3], st_v[0:3]], axis=1)
    g_conv_b = jnp.concatenate([st_g[3:4], st_v[3:4]], axis=1)
    small = dict(pre_mix_norm=g_pre_mix, rel_bias=jnp.concatenate(g_rel, axis=1), hgrn_lb_raw=g_lb_raw,
                 hgrn_norm=g_hgrn_norm, post_mix_norm=g_post_mix, pre_ffn_norm=g_pre_ffn, conv_b=g_conv_b,
                 post_ffn_norm=g_post_ffn, conv_w=g_conv_w)
    return loss, grad_x, gW_in, big_b, small


def _weights_a(shards):
    wt = shards.reshape(-1, D_MODEL)
    return dict(
        wt_qkv=[wt[g * QKV_G:(g + 1) * QKV_G] for g in range(N_GROUPS)],
        wt_hg=wt[3 * QKV_G:3 * QKV_G + 4 * HGRN_W],
        wt_gate=wt[3 * QKV_G + 4 * HGRN_W:],
    )


def _weights_b(slabs):
    sh = _unpack_rows(slabs, _PACK_B)
    wt_up = sh["w_up"].reshape(-1, D_MODEL)
    return dict(
        w_ba=_cols_to_full(sh["w_ba"]),
        w_bh=_cols_to_full(sh["w_bh"]),
        w_out=sh["w_out"].reshape(D_MODEL, D_MODEL),
        wt_up_g=wt_up[:D_FF],
        wt_up_v=wt_up[D_FF:],
        w_down=sh["w_down"].reshape(D_FF, D_MODEL),
    )


def _dest_rows(sections, height):
    out = []
    for j in range(8):
        lo, hi, off, pieces = j * height, (j + 1) * height, 0, []
        for s in sections:
            a, b = max(lo, off), min(hi, off + s.shape[0])
            if a < b:
                pieces.append(s[a - off:b - off])
            off += s.shape[0]
        out.append(pieces[0] if len(pieces) == 1 else jnp.concatenate(pieces, axis=0))
    return out


def _grad_blocks_a(sections):
    rows = _dest_rows(sections, 1088)
    return jnp.stack([jnp.stack([rows[2 * k + c].astype(bf16) for k in range(4)]) for c in range(2)])


def _grad_slab_b(g):
    shards = dict(w_ba=_full_to_cols(g["w_ba"]), w_bh=_full_to_cols(g["w_bh"]), w_out=g["w_out"].reshape(8, 128, D_MODEL),
                  w_up=jnp.stack(_dest_rows(g["w_up"], 704)), w_down=g["w_down"].reshape(8, 352, D_MODEL))
    return _pack_rows({k: v.astype(bf16) for k, v in shards.items()}, _PACK_B)


_CONVW_SLAB_ROWS = 16


class _Traffic:
    def __init__(self, slab_a, slab_b, conv_w):
        hi = conv_w.astype(bf16)
        r1 = conv_w - hi.astype(f32)
        mid = r1.astype(bf16)
        lo = (r1 - mid.astype(f32)).astype(bf16)
        bits = jnp.stack([hi, mid, lo]).reshape(-1)
        tail = jnp.pad(bits, (0, _CONVW_SLAB_ROWS * D_MODEL - bits.shape[0])).reshape(_CONVW_SLAB_ROWS, D_MODEL)
        self.slab_b = jnp.concatenate([slab_b, tail], axis=0)
        self.state_a, tok = _split_start(slab_a, "chip_gather", "ag_a_start")
        self.state_b, self.token = _split_start(self.slab_b, "chip_gather", "ag_b_start", after=tok)
        self.state = None
        self.state_gb = None

    def start_token(self):
        return self.token

    def weights_a(self, after):
        by_chip = _split_wait(self.state_a, after, "chip_gather", "ag_a_wait")
        return _weights_a(_by_device(*_core_gather(by_chip, "ag_a_cores")))

    def forward_b(self, after):
        by_chip = _split_wait(self.state_b, after, "chip_gather", "ag_b_wait")
        self.state, token = _split_start(by_chip, "core_gather", "ag_b_cores_start")
        return token

    def weights_b(self, after):
        slabs = _by_device(*_split_wait(self.state, after, "core_gather", "ag_b_cores_wait"))
        rows = _slab_rows(_PACK_B)
        out = _weights_b(slabs[:, :rows])
        pieces = slabs[:, rows:].reshape(8, -1)[:, :3 * 3 * 704].reshape(8, 3, 3, 704).astype(f32)
        out["conv_w"] = _cols_to_full((pieces[:, 0] + pieces[:, 1]) + pieces[:, 2])
        return out

    def grads_b_start(self, grads):
        self.state, token = _split_start(_by_core(_grad_slab_b(grads)), "core_swap", "rs_b_cores_start")
        return token

    def grads_b_exchange(self, after):
        from_sib, by_core = _split_wait(self.state, after, "core_swap", "rs_b_cores_wait")
        self.state_gb, token = _split_start(_pair_add(by_core, from_sib, "rs_b_pair_add"), "chip_xchg", "rs_b_start")
        return token

    def grads_a_start(self, sections):
        self.state, token = _split_start(_grad_blocks_a(sections), "core_swap", "rs_a_cores_start")
        return token

    def grads_a_exchange(self, after):
        from_sib, by_core = _split_wait(self.state, after, "core_swap", "rs_a_cores_wait")
        self.state, token = _split_start(_pair_add(by_core, from_sib, "rs_a_pair_add"), "chip_xchg", "rs_a_start")
        return token

    def parts(self, after):
        parts = _unpack_rows(_split_wait(self.state_gb, after, "chip_xchg", "rs_b_wait"), _PACK_B)
        parts["w_in"] = _split_wait(self.state, after, "chip_xchg", "rs_a_wait")
        return parts


def kernel(x, pre_mix_norm, w_in, rel_bias, hgrn_lb_raw, hgrn_norm, w_branch_attn, w_branch_hgrn, w_out, post_mix_norm, pre_ffn_norm, w_up, conv_w, conv_b, w_down, post_ffn_norm, loss_target, m_pre_mix_norm, m_w_in, m_rel_bias, m_hgrn_lb_raw, m_hgrn_norm, m_w_branch_attn, m_w_branch_hgrn, m_w_out, m_post_mix_norm, m_pre_ffn_norm, m_w_up, m_conv_w, m_conv_b, m_w_down, m_post_ffn_norm, v_pre_mix_norm, v_w_in, v_rel_bias, v_hgrn_lb_raw, v_hgrn_norm, v_w_branch_attn, v_w_branch_hgrn, v_w_out, v_post_mix_norm, v_pre_ffn_norm, v_w_up, v_conv_w, v_conv_b, v_w_down, v_post_ffn_norm):
    ci = lax.axis_index("c")
    dev = 4 * lax.axis_index("x") + 2 * lax.axis_index("y") + ci
    tr = lambda t: jnp.swapaxes(t[0], 0, 1)
    wts = dict(w_in=tr(w_in), w_ba=w_branch_attn[0], w_bh=w_branch_hgrn[0], w_out=w_out[0], w_up=tr(w_up),
               w_down=w_down[0])
    mom = dict(w_in=tr(m_w_in), w_ba=m_w_branch_attn[0], w_bh=m_w_branch_hgrn[0], w_out=m_w_out[0], w_up=tr(m_w_up),
               w_down=m_w_down[0])
    var = dict(w_in=tr(v_w_in), w_ba=v_w_branch_attn[0], w_bh=v_w_branch_hgrn[0], w_out=v_w_out[0], w_up=tr(v_w_up),
               w_down=v_w_down[0])
    small_w = dict(pre_mix_norm=pre_mix_norm, rel_bias=rel_bias, hgrn_lb_raw=hgrn_lb_raw, hgrn_norm=hgrn_norm,
                   post_mix_norm=post_mix_norm, pre_ffn_norm=pre_ffn_norm, conv_b=conv_b, post_ffn_norm=post_ffn_norm)
    small_m = dict(pre_mix_norm=m_pre_mix_norm, rel_bias=m_rel_bias, hgrn_lb_raw=m_hgrn_lb_raw, hgrn_norm=m_hgrn_norm,
                   post_mix_norm=m_post_mix_norm, pre_ffn_norm=m_pre_ffn_norm, conv_b=m_conv_b,
                   post_ffn_norm=m_post_ffn_norm)
    small_v = dict(pre_mix_norm=v_pre_mix_norm, rel_bias=v_rel_bias, hgrn_lb_raw=v_hgrn_lb_raw, hgrn_norm=v_hgrn_norm,
                   post_mix_norm=v_post_mix_norm, pre_ffn_norm=v_pre_ffn_norm, conv_b=v_conv_b,
                   post_ffn_norm=v_post_ffn_norm)

    plan = _Traffic(wts["w_in"].astype(bf16),
                    _pack_rows({k: wts[k].astype(bf16)[None] for k, _ in _PACK_B}, _PACK_B)[0], conv_w[0])

    loss8, grad_x, _, _, small = _local_step(x[0], loss_target[0], small_w, plan)
    spack = jnp.concatenate([_pack_small(small, loss8[0, 0:1]),
                             jnp.pad(small["conv_w"].reshape(-1, LANE), ((0, _CONVW_ROWS - 132), (0, 0)))], axis=0)
    small_state, token = _split_start(spack, "chip_gather", "ag_small_start")

    parts = plan.parts(token)
    outs_big = {}
    for k, _ in _PACK_SIZES:
        outs_big[k] = _adamw(wts[k], mom[k], var[k], parts[k], "adamw_" + k)

    by_chip = _split_wait(small_state, outs_big["w_in"][1], "chip_gather", "ag_small_wait")
    ssum = _sum8(_by_device(*_core_gather(by_chip, "ag_small_cores")), "small_sum")
    gs = ssum[:_SMALL_ROWS]
    loss = ssum[_SMALL_USED // LANE, _SMALL_USED % LANE]
    res_small = _adamw(_pack_small(small_w), _pack_small(small_m), _pack_small(small_v), gs, "adamw_small")
    sm = [_unpack_small(t) for t in res_small]
    g_cw_full = ssum[_SMALL_ROWS:_SMALL_ROWS + 132].reshape(3, 2 * D_FF)
    g_cw = lax.dynamic_slice_in_dim(g_cw_full, dev * 704, 704, axis=1)
    res_cw = _adamw(conv_w[0], m_conv_w[0], v_conv_w[0], g_cw, "adamw_conv_w")

    def pick(i):
        def big_(k):
            t = outs_big[k][i]
            return (jnp.swapaxes(t, 0, 1) if k in _TRANSPOSED else t)[None]
        return [sm[i]["pre_mix_norm"], big_("w_in"), sm[i]["rel_bias"], sm[i]["hgrn_lb_raw"], sm[i]["hgrn_norm"],
                big_("w_ba"), big_("w_bh"), big_("w_out"), sm[i]["post_mix_norm"], sm[i]["pre_ffn_norm"],
                big_("w_up"), res_cw[i][None], sm[i]["conv_b"], big_("w_down"), sm[i]["post_ffn_norm"]]

    return (loss, grad_x[None], *pick(0), *pick(1), *pick(2), *pick(3))
```

```python
import functools
import math

import jax
import jax.numpy as jnp
from jax import lax
from jax.experimental import pallas as pl
from jax.experimental.pallas import tpu as pltpu

f32 = jnp.float32
bf16 = jnp.bfloat16
SDS = jax.ShapeDtypeStruct
HIGHEST = lax.Precision.HIGHEST
MESH = pl.DeviceIdType.MESH

NN = (((1,), (0,)), ((), ()))
NT = (((1,), (1,)), ((), ()))
TN = (((0,), (0,)), ((), ()))

D_MODEL = 1024
N_GROUPS = 3
DILATIONS = (1, 4, 16)
HEAD_DIM = 64
ATTN_BLOCK = 128
QKV_G = 1536
ATTN_OUT = 512
HGRN_W = 512
HGRN_CHUNK = 32
D_FF = 2816
NUM_BUCKETS = 32
MAX_EXACT = 16
MAX_DISTANCE = 2048
NEG_INF = -1e30
EPS = 1e-6
LANE = 128
SUBLANE = 8
VMEM_BIG = 48 * 1024 * 1024
MM_ROWS = 512
MM_OUT_BYTES = 8 * 1024 * 1024
ADAM_BLOCK_BYTES = 2304 * 1024

ADAM_LR, ADAM_B1, ADAM_B2, ADAM_EPS, ADAM_WD, ADAM_STEP = 0.001, 0.9, 0.999, 1e-08, 0.01, 10


def _pick(n, pref):
    t = pref
    while t >= LANE:
        if n % t == 0:
            return t
        t //= 2
    return n


def _cparams(sem=None, vmem=None):
    kw = {}
    if sem is not None:
        kw["dimension_semantics"] = sem
    if vmem is not None:
        kw["vmem_limit_bytes"] = vmem
    return pltpu.CompilerParams(**kw)


def _sigmoid(x):
    return jax.nn.sigmoid(x)


def _colsum8(x):
    return x.reshape(x.shape[0] // SUBLANE, SUBLANE, x.shape[1]).sum(axis=0)


def _mm(a, b, mode, out_dtype, name, acc=None, after=None):
    dims = {"nn": NN, "nt": NT, "tn": TN}[mode]
    has_acc = acc is not None
    parts = list(a) if isinstance(a, (list, tuple)) else [a]
    if mode == "tn":
        assert not has_acc
        K, N = b.shape
        widths = [t.shape[1] for t in parts]
        M = sum(widths)
        whole = M * N * 4 <= MM_OUT_BYTES
        assert whole or len(parts) == 1
        tmm = M if whole else M // 2
        ts = _pick(K, 4 * MM_ROWS)
        nk = K // ts

        npart = len(parts)
        narrow = out_dtype != f32

        def body_tn(*refs):
            b_ref, o_ref = refs[npart], refs[npart + 1]
            acc_ref = refs[npart + 2] if narrow else o_ref
            k = pl.program_id(1)
            bv = b_ref[...]
            lo = 0
            for a_ref, w in zip(refs[:npart], widths if whole else [tmm]):
                part = lax.dot_general(a_ref[...], bv, dims, preferred_element_type=f32)
                rows = slice(lo, lo + w)
                lo += w

                @pl.when(k == 0)
                def _(part=part, rows=rows):
                    acc_ref[rows, :] = part

                @pl.when(k > 0)
                def _(part=part, rows=rows):
                    acc_ref[rows, :] += part

            if narrow:
                @pl.when(k == nk - 1)
                def _():
                    o_ref[...] = acc_ref[...].astype(out_dtype)

        return pl.pallas_call(
            body_tn,
            grid=(M // tmm, nk),
            in_specs=[pl.BlockSpec((ts, w if whole else tmm), lambda i, k: (k, i)) for w in widths]
            + [pl.BlockSpec((ts, N), lambda i, k: (k, 0))],
            out_specs=pl.BlockSpec((tmm, N), lambda i, k: (i, 0)),
            out_shape=SDS((M, N), out_dtype),
            scratch_shapes=[pltpu.VMEM((tmm, N), f32)] if narrow else [],
            compiler_params=_cparams(("parallel", "arbitrary"), VMEM_BIG),
            name=name,
        )(*parts, b)

    bs = list(b) if isinstance(b, (list, tuple)) else [b]
    widths = [t.shape[1] for t in parts]
    M = parts[0].shape[0]
    kdim = 0 if mode == "nn" else 1
    N = bs[0].shape[1 - kdim]
    tm = _pick(M, MM_ROWS)
    npart, nb = len(parts), len(bs)
    place, bi, lo = [], 0, 0
    for w in widths:
        place.append((bi, lo))
        lo += w
        if lo == bs[bi].shape[kdim]:
            bi, lo = bi + 1, 0
    assert bi == nb and lo == 0

    def body(*refs):
        a_refs, b_refs = refs[:npart], refs[npart:npart + nb]
        c_ref = refs[npart + nb] if has_acc else None
        o_ref = refs[-1]
        part = None
        for a_ref, w, (bi, lo) in zip(a_refs, widths, place):
            b_ref = b_refs[bi]
            if w == bs[bi].shape[kdim]:
                bk = b_ref[...]
            else:
                bk = b_ref[:, lo:lo + w] if mode == "nt" else b_ref[lo:lo + w, :]
            t = lax.dot_general(a_ref[...], bk, dims, preferred_element_type=f32)
            part = t if part is None else part + t
        if has_acc:
            part = part + c_ref[...]
        o_ref[...] = part.astype(out_dtype)

    specs = [pl.BlockSpec((tm, w), lambda i: (i, 0)) for w in widths] \
        + [pl.BlockSpec(t.shape, lambda i: (0, 0)) for t in bs]
    args = parts + bs
    aliases = {}
    if has_acc:
        specs.append(pl.BlockSpec((tm, N), lambda i: (i, 0)))
        args.append(acc)
        aliases = {npart + nb: 0}
    if after is not None:
        specs.append(pl.BlockSpec(memory_space=pl.ANY))
        args.append(after)
    return pl.pallas_call(
        body,
        grid=(M // tm,),
        in_specs=specs,
        out_specs=pl.BlockSpec((tm, N), lambda i: (i, 0)),
        out_shape=SDS((M, N), out_dtype),
        input_output_aliases=aliases,
        compiler_params=_cparams(("parallel",), VMEM_BIG),
        name=name,
    )(*args)


def _mm_fanout(a, bs, mode, out_dtypes, name):
    dims = {"nn": NN, "nt": NT}[mode]
    M, K = a.shape
    ns = [b.shape[1] if mode == "nn" else b.shape[0] for b in bs]
    tm = _pick(M, MM_ROWS)
    nb = len(bs)

    def body(a_ref, *refs):
        av = a_ref[...]
        for b_ref, o_ref, dt in zip(refs[:nb], refs[nb:], out_dtypes):
            o_ref[...] = lax.dot_general(av, b_ref[...], dims, preferred_element_type=f32).astype(dt)

    return pl.pallas_call(
        body,
        grid=(M // tm,),
        in_specs=[pl.BlockSpec((tm, K), lambda i: (i, 0))] + [pl.BlockSpec(b.shape, lambda i: (0, 0)) for b in bs],
        out_specs=[pl.BlockSpec((tm, n), lambda i: (i, 0)) for n in ns],
        out_shape=[SDS((M, n), dt) for n, dt in zip(ns, out_dtypes)],
        compiler_params=_cparams(("parallel",), VMEM_BIG),
        name=name,
    )(a, *bs)


PERM_ROWS = 2048


def _perm_spec(d, cols=LANE):
    return pl.BlockSpec((d, PERM_ROWS // d, cols), lambda i, j: (0, i, j))


def _to_natural(src_ref, dst_ref, d):
    n = src_ref.shape[1]
    for r in range(d):
        dst_ref[pl.ds(r, n, stride=d), :] = src_ref[r]


def _prep(x, w, after=None):
    S, D = x.shape
    R = PERM_ROWS
    nc = D // LANE
    n_in = nc + 1 + (after is not None)

    def body(*refs):
        x_refs, w_ref = refs[:nc], refs[nc]
        h_ref, h4_ref, h16_ref, rs = refs[n_in:]
        ssq = None
        for xr in x_refs:
            v = xr[...]
            t = jnp.sum(v * v, axis=-1, keepdims=True)
            ssq = t if ssq is None else ssq + t
        rinv = lax.rsqrt(ssq * (1.0 / D) + EPS)
        rs[...] = jnp.broadcast_to(rinv, (R, LANE))
        for j, xr in enumerate(x_refs):
            cols = slice(j * LANE, (j + 1) * LANE)
            wj = w_ref[:, cols]
            h_ref[:, cols] = ((xr[...] * rinv) * wj).astype(bf16)
            for d, o_ref in ((4, h4_ref), (16, h16_ref)):
                n = R // d
                for r in range(d):
                    rows = pl.ds(r, n, stride=d)
                    o_ref[r, :, cols] = ((xr[rows, :] * rs[rows, :]) * wj).astype(bf16)

    col = lambda j: pl.BlockSpec((R, LANE), lambda i, j=j: (i, j))
    h, h4, h16 = pl.pallas_call(
        body,
        grid=(S // R,),
        in_specs=[col(j) for j in range(nc)] + [pl.BlockSpec((1, D), lambda i: (0, 0))]
        + ([] if after is None else [pl.BlockSpec(memory_space=pl.ANY)]),
        out_specs=[pl.BlockSpec((R, D), lambda i: (i, 0)), pl.BlockSpec((4, R // 4, D), lambda i: (0, i, 0)),
                   pl.BlockSpec((16, R // 16, D), lambda i: (0, i, 0))],
        out_shape=[SDS((S, D), bf16), SDS((4, S // 4, D), bf16), SDS((16, S // 16, D), bf16)],
        scratch_shapes=[pltpu.VMEM((R, LANE), f32)],
        compiler_params=_cparams(("parallel",), VMEM_BIG),
        name="prep_norm_perm",
    )(*([x] * nc), w, *([] if after is None else [after]))
    return [h, h4.reshape(S, D), h16.reshape(S, D)]


def _rms_parts(xv):
    r = lax.rsqrt(jnp.mean(xv * xv, axis=-1, keepdims=True) + EPS)
    return r, xv * r


def _rms_bwd(xhat, r, w, dy):
    dyw = dy * w
    return r * (dyw - xhat * jnp.mean(dyw * xhat, axis=-1, keepdims=True))


def _mid_fwd(x, merged, w_out, w_pm, w_pf):
    S, D = x.shape
    tm = _pick(S, MM_ROWS)

    def body(x_ref, m_ref, wo_ref, wpm_ref, wpf_ref, mo_ref, x1_ref, h2_ref):
        mo = jnp.dot(m_ref[...], wo_ref[...], preferred_element_type=f32)
        mo_ref[...] = mo
        _, moh = _rms_parts(mo)
        x1 = x_ref[...] + moh * wpm_ref[...]
        x1_ref[...] = x1
        _, x1h = _rms_parts(x1)
        h2_ref[...] = (x1h * wpf_ref[...]).astype(bf16)

    row = pl.BlockSpec((tm, D), lambda i: (i, 0))
    vec = pl.BlockSpec((1, D), lambda i: (0, 0))
    return pl.pallas_call(
        body,
        grid=(S // tm,),
        in_specs=[row, pl.BlockSpec((tm, merged.shape[1]), lambda i: (i, 0)),
                  pl.BlockSpec(w_out.shape, lambda i: (0, 0)), vec, vec],
        out_specs=[row, row, row],
        out_shape=[SDS((S, D), f32), SDS((S, D), f32), SDS((S, D), bf16)],
        compiler_params=_cparams(("parallel",), VMEM_BIG),
        name="out_proj_mid_fwd",
    )(x, merged, w_out, w_pm, w_pf)


def _final(x1, act, w_down, tgt, w_pfn):
    S, D = x1.shape
    tm = _pick(S, MM_ROWS)
    nt = S // tm

    def body(x1_ref, a_ref, wd_ref, t_ref, w_ref, loss_ref, dy_ref, dfo_ref, gw_ref, da_ref, lacc, gacc):
        i = pl.program_id(0)

        @pl.when(i == 0)
        def _():
            lacc[...] = jnp.zeros_like(lacc)
            gacc[...] = jnp.zeros_like(gacc)

        w = w_ref[...]
        r, foh = _rms_parts(jnp.dot(a_ref[...], wd_ref[...], preferred_element_type=f32))
        y = x1_ref[...] + foh * w
        err = y - t_ref[...]
        lacc[...] += _colsum8(err * err)
        dy = err * (1.0 / D)
        dy_ref[...] = dy
        gacc[...] += _colsum8(dy * foh)
        dfo = _rms_bwd(foh, r, w, dy).astype(bf16)
        dfo_ref[...] = dfo
        da_ref[...] = lax.dot_general(dfo, wd_ref[...], NT, preferred_element_type=f32).astype(bf16)

        @pl.when(i == nt - 1)
        def _():
            loss_ref[...] = jnp.full((SUBLANE, LANE), 0.5 / D, f32) * jnp.sum(lacc[...])
            gw_ref[...] = jnp.sum(gacc[...], axis=0, keepdims=True)

    row = pl.BlockSpec((tm, D), lambda i: (i, 0))
    vec = pl.BlockSpec((1, D), lambda i: (0, 0))
    wide = pl.BlockSpec((tm, act.shape[1]), lambda i: (i, 0))
    return pl.pallas_call(
        body,
        grid=(nt,),
        in_specs=[row, wide, pl.BlockSpec(w_down.shape, lambda i: (0, 0)), row, vec],
        out_specs=[pl.BlockSpec((SUBLANE, LANE), lambda i: (0, 0)), row, row, vec, wide],
        out_shape=[SDS((SUBLANE, LANE), f32), SDS((S, D), f32), SDS((S, D), bf16), SDS((1, D), f32),
                   SDS(act.shape, bf16)],
        scratch_shapes=[pltpu.VMEM((SUBLANE, D), f32), pltpu.VMEM((SUBLANE, D), f32)],
        compiler_params=_cparams(("arbitrary",), VMEM_BIG),
        name="down_proj_final_loss",
    )(x1, act, w_down, tgt, w_pfn)


MID_BWD_ROWS = 256


def _mid_bwd(dy, dug, duv, wt_g, wt_v, x1, mo, w_pf, w_pm):
    S, D = dy.shape
    tm = _pick(S, MID_BWD_ROWS)
    nt = S // tm

    def body(dy_ref, dug_ref, duv_ref, wg_ref, wv_ref, x1_ref, mo_ref, wpf_ref, wpm_ref,
             dx1_ref, dmo_ref, gpf_ref, gpm_ref, apf, apm):
        i = pl.program_id(0)

        @pl.when(i == 0)
        def _():
            apf[...] = jnp.zeros_like(apf)
            apm[...] = jnp.zeros_like(apm)

        r1, x1h = _rms_parts(x1_ref[...])
        dh2 = jnp.dot(dug_ref[...], wg_ref[...], preferred_element_type=f32) \
            + jnp.dot(duv_ref[...], wv_ref[...], preferred_element_type=f32)
        apf[...] += _colsum8(dh2 * x1h)
        dx1 = dy_ref[...] + _rms_bwd(x1h, r1, wpf_ref[...], dh2)
        dx1_ref[...] = dx1
        rm, moh = _rms_parts(mo_ref[...])
        apm[...] += _colsum8(dx1 * moh)
        dmo_ref[...] = _rms_bwd(moh, rm, wpm_ref[...], dx1).astype(bf16)

        @pl.when(i == nt - 1)
        def _():
            gpf_ref[...] = jnp.sum(apf[...], axis=0, keepdims=True)
            gpm_ref[...] = jnp.sum(apm[...], axis=0, keepdims=True)

    row = pl.BlockSpec((tm, D), lambda i: (i, 0))
    vec = pl.BlockSpec((1, D), lambda i: (0, 0))
    return pl.pallas_call(
        body,
        grid=(nt,),
        in_specs=[row, pl.BlockSpec((tm, dug.shape[1]), lambda i: (i, 0)), pl.BlockSpec((tm, duv.shape[1]), lambda i: (i, 0)),
                  pl.BlockSpec(wt_g.shape, lambda i: (0, 0)), pl.BlockSpec(wt_v.shape, lambda i: (0, 0)),
                  row, row, vec, vec],
        out_specs=[row, row, vec, vec],
        out_shape=[SDS((S, D), f32), SDS((S, D), bf16), SDS((1, D), f32), SDS((1, D), f32)],
        scratch_shapes=[pltpu.VMEM((SUBLANE, D), f32), pltpu.VMEM((SUBLANE, D), f32)],
        compiler_params=_cparams(("arbitrary",), VMEM_BIG),
        name="dh2_mid_bwd",
    )(dy, dug, duv, wt_g, wt_v, x1, mo, w_pf, w_pm)


def _first_bwd(x, dx1, dh_a, dh_b, dh_c, w_pre):
    S, D = x.shape
    tm = _pick(S, 512)
    nt = S // tm
    nc = D // LANE

    def body(*refs):
        x_ref, dx1_ref, a_ref = refs[:3]
        b_refs, c_refs, w_ref = refs[3:3 + nc], refs[3 + nc:3 + 2 * nc], refs[3 + 2 * nc]
        gx_ref, gw_ref, acc, dh_s, sb, sc = refs[4 + 2 * nc:]
        i = pl.program_id(0)

        @pl.when(i == 0)
        def _():
            acc[...] = jnp.zeros_like(acc)

        for j in range(nc):
            cols = slice(j * LANE, (j + 1) * LANE)
            _to_natural(b_refs[j], sb, 4)
            _to_natural(c_refs[j], sc, 16)
            dh_s[:, cols] = (a_ref[:, cols] + sb[...]) + sc[...]
        r, xh = _rms_parts(x_ref[...])
        dh = dh_s[...]
        acc[...] += _colsum8(dh * xh)
        gx_ref[...] = dx1_ref[...] + _rms_bwd(xh, r, w_ref[...], dh)

        @pl.when(i == nt - 1)
        def _():
            gw_ref[...] = jnp.sum(acc[...], axis=0, keepdims=True)

    row = pl.BlockSpec((tm, D), lambda i: (i, 0))
    vec = pl.BlockSpec((1, D), lambda i: (0, 0))
    perm = lambda d: [pl.BlockSpec((d, tm // d, LANE), lambda i, j=j: (0, i, j)) for j in range(nc)]
    return pl.pallas_call(
        body,
        grid=(nt,),
        in_specs=[row, row, row] + perm(4) + perm(16) + [vec],
        out_specs=[row, vec],
        out_shape=[SDS((S, D), f32), SDS((1, D), f32)],
        scratch_shapes=[pltpu.VMEM((SUBLANE, D), f32), pltpu.VMEM((tm, D), f32), pltpu.VMEM((tm, LANE), f32),
                        pltpu.VMEM((tm, LANE), f32)],
        compiler_params=_cparams(("arbitrary",), VMEM_BIG),
        name="first_bwd",
    )(x, dx1, dh_a, *([dh_b.reshape(4, S // 4, D)] * nc), *([dh_c.reshape(16, S // 16, D)] * nc), w_pre)


def _t5_bucket(dist):
    n = jnp.maximum(dist, 0)
    nf = jnp.maximum(n, 1).astype(f32)
    large = MAX_EXACT + (jnp.log(nf / MAX_EXACT) / math.log(MAX_DISTANCE / MAX_EXACT)
                         * (NUM_BUCKETS - MAX_EXACT)).astype(jnp.int32)
    large = jnp.minimum(large, NUM_BUCKETS - 1)
    return jnp.where(n < MAX_EXACT, n, large)


def _bias_consts(d):
    blk = ATTN_BLOCK
    rel = jnp.arange(blk)[:, None] + blk - jnp.arange(2 * blk)[None, :]
    in_win = (rel >= 0) & (rel <= blk)
    bucket = _t5_bucket(rel * d).reshape(1, -1)
    onehot = (bucket == jnp.arange(NUM_BUCKETS)[:, None]).astype(f32)
    return onehot, in_win.astype(f32).reshape(1, -1)


def _bias_build(tab_t, onehot, maskf, name, after):
    H = tab_t.shape[0]

    def body(t_ref, oh_ref, m_ref, after_ref, o_ref):
        b = jnp.dot(t_ref[...], oh_ref[...], precision=HIGHEST, preferred_element_type=f32)
        o_ref[...] = jnp.where(m_ref[...] > 0.5, b, NEG_INF)

    vm = pl.BlockSpec(memory_space=pltpu.VMEM)
    return pl.pallas_call(body, out_shape=SDS((H, onehot.shape[1]), f32), name=name,
                          in_specs=[vm, vm, vm, pl.BlockSpec(memory_space=pl.ANY)], out_specs=vm,
                          )(tab_t, onehot, maskf, after)


def _bias_grad(dbias_flat, onehot, name):
    H = dbias_flat.shape[0]

    def body(g_ref, oh_ref, o_ref):
        o_ref[...] = lax.dot_general(oh_ref[...], g_ref[...], NT, precision=HIGHEST, preferred_element_type=f32)

    return pl.pallas_call(body, out_shape=SDS((NUM_BUCKETS, H), f32), name=name)(dbias_flat, onehot)


ATTN_TILE = 512
ATTN_SUB = ATTN_TILE // ATTN_BLOCK
ATTN_HP = 4
ATTN_WIDE = ATTN_HP * LANE


def _qkv_specs(nt):
    tile = (ATTN_TILE, ATTN_WIDE)
    blk = (ATTN_BLOCK, ATTN_WIDE)
    sec = ATTN_OUT // ATTN_WIDE
    cur = lambda off: (lambda h, t: (jnp.minimum(t, nt - 1), off + h))
    prev = lambda off: (lambda h, t: (jnp.maximum(jnp.minimum(t, nt - 1) * ATTN_SUB - 1, 0), off + h))
    return [pl.BlockSpec(tile, cur(0)), pl.BlockSpec(blk, prev(sec)), pl.BlockSpec(tile, cur(sec)),
            pl.BlockSpec(blk, prev(2 * sec)), pl.BlockSpec(tile, cur(2 * sec))]


def _head_masks():
    lane = lax.broadcasted_iota(jnp.int32, (ATTN_BLOCK, LANE), 1)
    return lane < HEAD_DIM


def _stack_heads(x2, low):
    zero = jnp.zeros_like(x2)
    return jnp.concatenate([jnp.where(low, x2, zero), jnp.where(low, zero, x2)], axis=0)


def _attn_fwd(qkv, bias, bps, name, after=None):
    S = qkv.shape[0]
    nt = S // ATTN_TILE
    scale = HEAD_DIM ** -0.5

    def body(q_ref, kp_ref, kc_ref, vp_ref, vc_ref, b_ref, *rest):
        o_ref, l_ref = rest[-2:]
        t = pl.program_id(1)
        low = _head_masks()
        col = lax.broadcasted_iota(jnp.int32, (2 * ATTN_BLOCK, 2 * ATTN_BLOCK), 1)
        for hp in range(ATTN_HP):
            cols = slice(hp * LANE, (hp + 1) * LANE)
            kk = jnp.concatenate([kp_ref[:, cols], kc_ref[:, cols]], axis=0)
            vv = jnp.concatenate([vp_ref[:, cols], vc_ref[:, cols]], axis=0)
            bias2 = b_ref[2 * hp:2 * hp + 2].reshape(2 * ATTN_BLOCK, 2 * ATTN_BLOCK)
            for b in range(ATTN_SUB):
                lo = b * ATTN_BLOCK
                rows = slice(lo, lo + ATTN_BLOCK)
                keys = slice(lo, lo + 2 * ATTN_BLOCK)
                dead = jnp.logical_and((t * ATTN_SUB + b) % bps == 0, col < ATTN_BLOCK)
                q2 = _stack_heads(q_ref[rows, cols], low)
                kb, vb = kk[keys], vv[keys]
                s = lax.dot_general(q2, kb, NT, preferred_element_type=f32) * scale + bias2
                s = jnp.where(dead, NEG_INF, s)
                m = jnp.max(s, axis=-1, keepdims=True)
                p = jnp.exp(s - m)
                l = jnp.sum(p, axis=-1, keepdims=True)
                o2 = jnp.dot(p.astype(bf16), vb, preferred_element_type=f32) / l
                lse = m + jnp.log(l)
                o_ref[rows, cols] = jnp.where(low, o2[:ATTN_BLOCK], o2[ATTN_BLOCK:])
                l_ref[rows, cols] = jnp.where(low, lse[:ATTN_BLOCK], lse[ATTN_BLOCK:])

    tile = pl.BlockSpec((ATTN_TILE, ATTN_WIDE), lambda h, t: (t, h))
    return pl.pallas_call(
        body,
        grid=(4 // ATTN_HP, nt),
        in_specs=_qkv_specs(nt) + [pl.BlockSpec((2 * ATTN_HP, ATTN_BLOCK, 2 * ATTN_BLOCK), lambda h, t: (h, 0, 0))]
        + ([] if after is None else [pl.BlockSpec(memory_space=pl.ANY)]),
        out_specs=[tile, tile],
        out_shape=[SDS((S, ATTN_OUT), f32), SDS((S, ATTN_OUT), f32)],
        compiler_params=_cparams(("parallel", "parallel")),
        name=name,
    )(qkv, qkv, qkv, qkv, qkv, bias, *([] if after is None else [after]))


def _attn_bwd(qkv, bias, do, dvec, lse, bps, name):
    S = qkv.shape[0]
    nt = S // ATTN_TILE
    scale = HEAD_DIM ** -0.5

    def assemble(parts):
        rows = [parts[0][:ATTN_BLOCK]]
        for b in range(ATTN_SUB - 1):
            rows.append(parts[b][ATTN_BLOCK:] + parts[b + 1][:ATTN_BLOCK])
        rows.append(parts[-1][ATTN_BLOCK:])
        return rows

    def body(q_ref, kp_ref, kc_ref, vp_ref, vc_ref, b_ref, do_ref, dvec_ref, lse_ref,
             dq_ref, dk_ref, dv_ref, db_ref, ck, cv):
        t = pl.program_id(1)
        last = ATTN_TILE - ATTN_BLOCK

        @pl.when(t == 0)
        def _():
            ck[...] = jnp.zeros_like(ck)
            cv[...] = jnp.zeros_like(cv)
            db_ref[...] = jnp.zeros_like(db_ref)

        @pl.when(t < nt)
        def _():
            low = _head_masks()
            col = lax.broadcasted_iota(jnp.int32, (2 * ATTN_BLOCK, 2 * ATTN_BLOCK), 1)
            per_row = lambda t2: jnp.concatenate([t2[:, 0:1], t2[:, HEAD_DIM:HEAD_DIM + 1]], axis=0)
            for hp in range(ATTN_HP):
                cols = slice(hp * LANE, (hp + 1) * LANE)
                kk = jnp.concatenate([kp_ref[:, cols], kc_ref[:, cols]], axis=0)
                vv = jnp.concatenate([vp_ref[:, cols], vc_ref[:, cols]], axis=0)
                bias2 = b_ref[2 * hp:2 * hp + 2].reshape(2 * ATTN_BLOCK, 2 * ATTN_BLOCK)
                dk_parts, dv_parts = [], []
                dsum = None
                for b in range(ATTN_SUB):
                    lo = b * ATTN_BLOCK
                    rows = slice(lo, lo + ATTN_BLOCK)
                    keys = slice(lo, lo + 2 * ATTN_BLOCK)
                    dead = jnp.logical_and((t * ATTN_SUB + b) % bps == 0, col < ATTN_BLOCK)
                    q2 = _stack_heads(q_ref[rows, cols], low)
                    do2 = _stack_heads(do_ref[rows, cols].astype(bf16), low)
                    kb, vb = kk[keys], vv[keys]
                    s = lax.dot_general(q2, kb, NT, preferred_element_type=f32) * scale + bias2
                    s = jnp.where(dead, NEG_INF, s)
                    p = jnp.exp(s - per_row(lse_ref[rows, cols]))
                    dp = lax.dot_general(do2, vb, NT, preferred_element_type=f32)
                    ds = p * (dp - per_row(dvec_ref[rows, cols]))
                    dsum = ds if dsum is None else dsum + ds
                    dsb = ds.astype(bf16)
                    dq2 = jnp.dot(dsb, kb, preferred_element_type=f32) * scale
                    dq_ref[rows, cols] = jnp.where(low, dq2[:ATTN_BLOCK], dq2[ATTN_BLOCK:]).astype(bf16)
                    dk_parts.append(lax.dot_general(dsb, q2, TN, preferred_element_type=f32) * scale)
                    dv_parts.append(lax.dot_general(p.astype(bf16), do2, TN, preferred_element_type=f32))
                db_ref[2 * hp:2 * hp + 2] += dsum.reshape(2, ATTN_BLOCK, 2 * ATTN_BLOCK)
                for parts, carry, out_ref in ((dk_parts, ck, dk_ref), (dv_parts, cv, dv_ref)):
                    rws = assemble(parts)
                    out_ref[:last, cols] = carry[:last, cols].astype(bf16)
                    out_ref[last:, cols] = (carry[last:, cols] + rws[0]).astype(bf16)
                    for b in range(ATTN_SUB):
                        carry[b * ATTN_BLOCK:(b + 1) * ATTN_BLOCK, cols] = rws[b + 1]

        @pl.when(t == nt)
        def _():
            dk_ref[...] = ck[...].astype(bf16)
            dv_ref[...] = cv[...].astype(bf16)

    tile = (ATTN_TILE, ATTN_WIDE)
    cur = pl.BlockSpec(tile, lambda h, t: (jnp.minimum(t, nt - 1), h))
    lag = pl.BlockSpec(tile, lambda h, t: (jnp.maximum(t - 1, 0), h))
    bspec = pl.BlockSpec((2 * ATTN_HP, ATTN_BLOCK, 2 * ATTN_BLOCK), lambda h, t: (h, 0, 0))
    return pl.pallas_call(
        body,
        grid=(4 // ATTN_HP, nt + 1),
        in_specs=_qkv_specs(nt) + [bspec, cur, cur, cur],
        out_specs=[cur, lag, lag, bspec],
        out_shape=[SDS((S, ATTN_OUT), bf16), SDS((S, ATTN_OUT), bf16), SDS((S, ATTN_OUT), bf16),
                   SDS((8, ATTN_BLOCK, 2 * ATTN_BLOCK), f32)],
        scratch_shapes=[pltpu.VMEM(tile, f32), pltpu.VMEM(tile, f32)],
        compiler_params=_cparams(("parallel", "arbitrary")),
        name=name,
    )(qkv, qkv, qkv, qkv, qkv, bias, do, dvec, lse)


def _attn_merge(o0, o1, o2, l0, l1, l2):
    S, W = o0.shape
    R = PERM_ROWS

    def body(o0_ref, o1_ref, o2_ref, l0_ref, l1_ref, l2_ref, y_ref, yb_ref, w0_ref, w1_ref, w2_ref,
             so1, so2, sl1, sl2):
        _to_natural(o1_ref, so1, 4)
        _to_natural(l1_ref, sl1, 4)
        _to_natural(o2_ref, so2, 16)
        _to_natural(l2_ref, sl2, 16)
        a, b, c = l0_ref[...], sl1[...], sl2[...]
        m = jnp.maximum(jnp.maximum(a, b), c)
        ea, eb, ec = jnp.exp(a - m), jnp.exp(b - m), jnp.exp(c - m)
        den = (ea + eb) + ec
        w0, w1, w2 = ea / den, eb / den, ec / den
        y = (w0 * o0_ref[...] + w1 * so1[...]) + w2 * so2[...]
        y_ref[...] = y
        yb_ref[...] = y.astype(bf16)
        w0_ref[...] = w0
        w1_ref[...] = w1
        w2_ref[...] = w2

    nat = pl.BlockSpec((R, LANE), lambda i, j: (i, j))
    v4 = lambda t: t.reshape(4, S // 4, W)
    v16 = lambda t: t.reshape(16, S // 16, W)
    return pl.pallas_call(
        body,
        grid=(S // R, W // LANE),
        in_specs=[nat, _perm_spec(4), _perm_spec(16)] * 2,
        out_specs=[nat] * 5,
        out_shape=[SDS((S, W), f32), SDS((S, W), bf16)] + [SDS((S, W), f32)] * 3,
        scratch_shapes=[pltpu.VMEM((R, LANE), f32)] * 4,
        compiler_params=_cparams(("parallel", "parallel"), VMEM_BIG),
        name="attn_merge",
    )(o0, v4(o1), v16(o2), l0, v4(l1), v16(l2))


def _attn_merge_bwd(dy, y, w0, w1, w2, after=None):
    S, W = dy.shape
    R = PERM_ROWS

    def body(dy_ref, y_ref, w0_ref, w1_ref, w2_ref, *rest):
        a0, a1, a2, b0, b1, b2, sa, sb = rest[-8:]
        dyv = dy_ref[...]
        r = lax.broadcasted_iota(jnp.int32, (LANE, LANE), 0) // HEAD_DIM
        c = lax.broadcasted_iota(jnp.int32, (LANE, LANE), 1) // HEAD_DIM
        seg = jnp.where(r == c, 1.0, 0.0).astype(f32)
        cbar = jnp.dot(dyv * y_ref[...], seg, precision=HIGHEST, preferred_element_type=f32)
        w = w0_ref[...]
        a0[...] = (w * dyv).astype(bf16)
        b0[...] = w * cbar
        for d, w_ref, a_ref, b_ref in ((4, w1_ref, a1, b1), (16, w2_ref, a2, b2)):
            w = w_ref[...]
            sa[...] = w * dyv
            sb[...] = w * cbar
            n = R // d
            for k in range(d):
                rows = pl.ds(k, n, stride=d)
                a_ref[k] = sa[rows, :].astype(bf16)
                b_ref[k] = sb[rows, :]

    nat = pl.BlockSpec((R, LANE), lambda i, j: (i, j))
    shapes = lambda dt: [SDS((S, W), dt), SDS((4, S // 4, W), dt), SDS((16, S // 16, W), dt)]
    outs = pl.pallas_call(
        body,
        grid=(S // R, W // LANE),
        in_specs=[nat] * 5 + ([] if after is None else [pl.BlockSpec(memory_space=pl.ANY)]),
        out_specs=[nat, _perm_spec(4), _perm_spec(16)] * 2,
        out_shape=shapes(bf16) + shapes(f32),
        scratch_shapes=[pltpu.VMEM((R, LANE), f32)] * 2,
        compiler_params=_cparams(("parallel", "parallel"), VMEM_BIG),
        name="attn_merge_bwd",
    )(dy, y, w0, w1, w2, *([] if after is None else [after]))
    return [t.reshape(S, W) for t in outs]


HGRN_SB = 256
HGRN_PAIR = 4


def _chunk_masks():
    r = jnp.arange(HGRN_SB)[:, None]
    c = jnp.arange(HGRN_SB)[None, :]
    same = (r // HGRN_CHUNK) == (c // HGRN_CHUNK)
    return jnp.stack([same & (c <= r), same, same & (c >= r)]).astype(bf16)


def _mask_dot(mask, x):
    hi = x.astype(bf16)
    r1 = x - hi.astype(f32)
    mid = r1.astype(bf16)
    lo = (r1 - mid.astype(f32)).astype(bf16)
    p = jnp.dot(mask, jnp.concatenate([hi, mid, lo], axis=1), preferred_element_type=f32)
    n = x.shape[1]
    return (p[:, :n] + p[:, n:2 * n]) + p[:, 2 * n:]


def _hgrn_prep(q_raw, f_raw, lbv, tril, same):
    sq = _sigmoid(q_raw)
    qs = q_raw * sq
    sig = _sigmoid(f_raw)
    f = lbv + (1.0 - lbv) * sig
    g = jnp.log(f)
    k = 1.0 - f
    G = _mask_dot(tril, g)
    GL = _mask_dot(same, g)
    eG = jnp.exp(G)
    einv = jnp.exp(-G)
    edec = jnp.exp(GL - G)
    return dict(sq=sq, qs=qs, sig=sig, f=f, k=k, eG=eG, einv=einv, edec=edec, eGL=jnp.exp(GL),
                qt=qs * eG, kt=k * einv, kd=k * edec)


def _hgrn_fwd(hg, lb, normw):
    S = hg.shape[0]
    sb = HGRN_SB
    nsb = S // sb
    nch = sb // HGRN_CHUNK

    def body(q_ref, f_ref, v_ref, og_ref, lb_ref, nw_ref, m_ref, y_ref, o_ref, ck_ref, st):
        j = pl.program_id(1)

        @pl.when(j == 0)
        def _():
            st[...] = jnp.zeros_like(st)

        tril_m = m_ref[0]
        tril = tril_m.astype(f32) > 0.5

        def one_head(hh):
            cols = slice(hh * LANE, (hh + 1) * LANE)
            ST = st[hh]
            ck_ref[hh, 0] = ST
            pr = _hgrn_prep(q_ref[:, cols], f_ref[:, cols], lb_ref[:, cols], tril_m, m_ref[1])
            qtb, ktb, kdb = pr["qt"].astype(bf16), pr["kt"].astype(bf16), pr["kd"].astype(bf16)
            eGL = pr["eGL"]
            vb = v_ref[:, cols].astype(bf16)
            A = jnp.where(tril, lax.dot_general(qtb, ktb, NT, preferred_element_type=f32), 0.0)
            o = jnp.dot(A.astype(bf16), vb, preferred_element_type=f32)
            outs = []
            for ci in range(nch):
                lo = ci * HGRN_CHUNK
                sl = slice(lo, lo + HGRN_CHUNK)
                outs.append(o[sl] + lax.dot_general(qtb[sl], ST.astype(bf16), NT, preferred_element_type=f32))
                ST = ST * eGL[lo:lo + 1, :] + lax.dot_general(vb[sl], kdb[sl], TN, preferred_element_type=f32)
            st[hh] = ST
            of = jnp.concatenate(outs, axis=0)
            o_ref[:, cols] = of
            rms = lax.rsqrt(jnp.mean(of * of, axis=-1, keepdims=True) + EPS)
            ogv = og_ref[:, cols]
            y_ref[:, cols] = ((of * rms * nw_ref[...]) * (ogv * _sigmoid(ogv))).astype(bf16)

        for hh in range(HGRN_PAIR):
            one_head(hh)

    wide = HGRN_PAIR * LANE
    col = lambda off: pl.BlockSpec((sb, wide), lambda h, j: (j, off // HGRN_PAIR + h))
    return pl.pallas_call(
        body,
        grid=(4 // HGRN_PAIR, nsb),
        in_specs=[col(0), col(4), col(8), col(12), pl.BlockSpec((1, wide), lambda h, j: (0, h)),
                  pl.BlockSpec((1, LANE), lambda h, j: (0, 0)),
                  pl.BlockSpec((3, sb, sb), lambda h, j: (0, 0, 0))],
        out_specs=[col(0), col(0), pl.BlockSpec((HGRN_PAIR, 1, LANE, LANE), lambda h, j: (h, j, 0, 0))],
        out_shape=[SDS((S, HGRN_W), bf16), SDS((S, HGRN_W), f32), SDS((4, nsb, LANE, LANE), f32)],
        scratch_shapes=[pltpu.VMEM((HGRN_PAIR, LANE, LANE), f32)],
        compiler_params=_cparams(("parallel", "arbitrary")),
        name="hgrn_fwd",
    )(hg, hg, hg, hg, lb, normw, _chunk_masks())


def _hgrn_bwd(hg, o_raw, dy, ck, lb, normw, after=None):
    S = hg.shape[0]
    sb = HGRN_SB
    nsb = S // sb
    nch = sb // HGRN_CHUNK

    def body(q_ref, f_ref, v_ref, og_ref, o_ref, dy_ref, ck_ref, lb_ref, nw_ref, m_ref, *rest):
        dq_ref, df_ref, dv_ref, dog_ref, glb_ref, gnw_ref, dst, alb, anw = rest[-9:]
        j = pl.program_id(1)

        @pl.when(j == 0)
        def _():
            dst[...] = jnp.zeros_like(dst)
            alb[...] = jnp.zeros_like(alb)
            anw[...] = jnp.zeros_like(anw)

        tril_m = m_ref[0]
        tril = tril_m.astype(f32) > 0.5
        nw = nw_ref[...]

        def one_head(hh):
            cols = slice(hh * LANE, (hh + 1) * LANE)
            lbv = lb_ref[:, cols]
            q_raw = q_ref[:, cols]
            pr = _hgrn_prep(q_raw, f_ref[:, cols], lbv, tril_m, m_ref[1])
            qt, kt, kd, eGL = pr["qt"], pr["kt"], pr["kd"], pr["eGL"]
            qtb, ktb, kdb = qt.astype(bf16), kt.astype(bf16), kd.astype(bf16)
            vb = v_ref[:, cols].astype(bf16)

            o = o_ref[:, cols]
            ogv = og_ref[:, cols]
            sog = _sigmoid(ogv)
            rms = lax.rsqrt(jnp.mean(o * o, axis=-1, keepdims=True) + EPS)
            oh = o * rms
            dyv = dy_ref[:, cols]
            dog_ref[:, cols] = (dyv * (oh * nw) * (sog * (1.0 + ogv * (1.0 - sog)))).astype(bf16)
            dohw = dyv * (ogv * sog)
            anw[:, cols] += _colsum8(dohw * oh)
            doh = dohw * nw
            do = rms * (doh - oh * jnp.mean(doh * oh, axis=-1, keepdims=True))
            dob = do.astype(bf16)

            Ab = jnp.where(tril, lax.dot_general(qtb, ktb, NT, preferred_element_type=f32), 0.0).astype(bf16)
            dAb = jnp.where(tril, lax.dot_general(dob, vb, NT, preferred_element_type=f32), 0.0).astype(bf16)
            dv_acc = lax.dot_general(Ab, dob, TN, preferred_element_type=f32)
            dqt = jnp.dot(dAb, ktb, preferred_element_type=f32)
            dkt = lax.dot_general(dAb, qtb, TN, preferred_element_type=f32)

            ST = ck_ref[hh, 0]
            states = []
            for ci in range(nch):
                lo = ci * HGRN_CHUNK
                sl = slice(lo, lo + HGRN_CHUNK)
                states.append(ST)
                ST = ST * eGL[lo:lo + 1, :] + lax.dot_general(vb[sl], kdb[sl], TN, preferred_element_type=f32)

            dST = dst[hh]
            dqt_i, dkd_i, dv_i, deg_i = [None] * nch, [None] * nch, [None] * nch, [None] * nch
            for ci in reversed(range(nch)):
                lo = ci * HGRN_CHUNK
                sl = slice(lo, lo + HGRN_CHUNK)
                ST0 = states[ci]
                dSTb = dST.astype(bf16)
                dv_i[ci] = lax.dot_general(kdb[sl], dSTb, NT, preferred_element_type=f32)
                dqt_i[ci] = jnp.dot(dob[sl], ST0.astype(bf16), preferred_element_type=f32)
                dkd_i[ci] = jnp.dot(vb[sl], dSTb, preferred_element_type=f32)
                deg_i[ci] = jnp.broadcast_to(jnp.sum(dST * ST0, axis=0, keepdims=True), (HGRN_CHUNK, LANE))
                dST = dST * eGL[lo:lo + 1, :] + lax.dot_general(dob[sl], qtb[sl], TN, preferred_element_type=f32)
            dst[hh] = dST

            dqt = dqt + jnp.concatenate(dqt_i, axis=0)
            dkd = jnp.concatenate(dkd_i, axis=0)
            dv_ref[:, cols] = (dv_acc + jnp.concatenate(dv_i, axis=0)).astype(bf16)
            deg = jnp.concatenate(deg_i, axis=0)

            dqs = dqt * pr["eG"]
            dkdkd = dkd * kd
            dG = dqt * qt - dkt * kt - dkdkd
            dk = dkt * pr["einv"] + dkd * pr["edec"]
            dGL = _mask_dot(m_ref[1], dkdkd) + eGL * deg
            dg = _mask_dot(m_ref[2], dG) + dGL
            df = dg / pr["f"] - dk
            sig = pr["sig"]
            df_ref[:, cols] = (df * (1.0 - lbv) * (sig * (1.0 - sig))).astype(bf16)
            alb[:, cols] += _colsum8(df * (1.0 - sig))
            sq = pr["sq"]
            dq_ref[:, cols] = (dqs * (sq * (1.0 + q_raw * (1.0 - sq)))).astype(bf16)

        for hh in range(HGRN_PAIR):
            one_head(hh)

        @pl.when(j == nsb - 1)
        def _():
            glb_ref[...] = jnp.broadcast_to(jnp.sum(alb[...], axis=0, keepdims=True), (SUBLANE, wide))
            gnw_ref[...] = jnp.broadcast_to(jnp.sum(anw[...], axis=0, keepdims=True), (SUBLANE, wide))

    wide = HGRN_PAIR * LANE
    rev = lambda off: pl.BlockSpec((sb, wide), lambda h, j: (nsb - 1 - j, off // HGRN_PAIR + h))
    stat = pl.BlockSpec((SUBLANE, wide), lambda h, j: (0, h))
    return pl.pallas_call(
        body,
        grid=(4 // HGRN_PAIR, nsb),
        in_specs=[rev(0), rev(4), rev(8), rev(12), rev(0), rev(0),
                  pl.BlockSpec((HGRN_PAIR, 1, LANE, LANE), lambda h, j: (h, nsb - 1 - j, 0, 0)),
                  pl.BlockSpec((1, wide), lambda h, j: (0, h)), pl.BlockSpec((1, LANE), lambda h, j: (0, 0)),
                  pl.BlockSpec((3, sb, sb), lambda h, j: (0, 0, 0))]
        + ([] if after is None else [pl.BlockSpec(memory_space=pl.ANY)]),
        out_specs=[rev(0), rev(0), rev(0), rev(0), stat, stat],
        out_shape=[SDS((S, HGRN_W), bf16)] * 4 + [SDS((SUBLANE, HGRN_W), f32)] * 2,
        scratch_shapes=[pltpu.VMEM((HGRN_PAIR, LANE, LANE), f32), pltpu.VMEM((SUBLANE, wide), f32),
                        pltpu.VMEM((SUBLANE, wide), f32)],
        compiler_params=_cparams(("parallel", "arbitrary")),
        name="hgrn_bwd",
    )(hg, hg, hg, hg, o_raw, dy, ck, lb, normw, _chunk_masks(), *([] if after is None else [after]))


def _lb_fwd(raw):
    def body(r_ref, o_ref):
        r = r_ref[...]
        m = jnp.max(r, axis=0, keepdims=True)
        e = jnp.exp(r - m)
        o_ref[...] = (e / jnp.sum(e, axis=0, keepdims=True))[0:1]

    return pl.pallas_call(body, out_shape=SDS((1, raw.shape[1]), f32), name="lb_fwd")(raw)


def _lb_bwd(raw, dlb):
    def body(r_ref, d_ref, o_ref):
        r = r_ref[...]
        m = jnp.max(r, axis=0, keepdims=True)
        e = jnp.exp(r - m)
        s = e / jnp.sum(e, axis=0, keepdims=True)
        s0 = s[0:1]
        onehot0 = jnp.where(lax.broadcasted_iota(jnp.int32, r.shape, 0) == 0, 1.0, 0.0)
        o_ref[...] = d_ref[...] * s0 * (onehot0 - s)

    return pl.pallas_call(body, out_shape=SDS(raw.shape, f32), name="lb_bwd")(raw, dlb)


def _gate_fwd(ya, yh, w_ba, w_bh, gc):
    S = ya.shape[0]
    D = w_ba.shape[1]
    tm = _pick(S, MM_ROWS)

    def body(ya_ref, yh_ref, wa_ref, wh_ref, g0_ref, g1_ref, a_ref, b_ref, o_ref):
        a = jnp.dot(ya_ref[...], wa_ref[...], preferred_element_type=f32).astype(bf16)
        b = jnp.dot(yh_ref[...], wh_ref[...], preferred_element_type=f32).astype(bf16)
        a_ref[...] = a
        b_ref[...] = b
        s0, s1 = _sigmoid(g0_ref[...].astype(f32)), _sigmoid(g1_ref[...].astype(f32))
        o_ref[...] = (s0 * a.astype(f32) + s1 * b.astype(f32)).astype(bf16)

    row = pl.BlockSpec((tm, D), lambda i: (i, 0))
    act = pl.BlockSpec((tm, ya.shape[1]), lambda i: (i, 0))
    wspec = pl.BlockSpec(w_ba.shape, lambda i: (0, 0))
    return pl.pallas_call(
        body,
        grid=(S // tm,),
        in_specs=[act, act, wspec, wspec, row, pl.BlockSpec((tm, D), lambda i: (i, 1))],
        out_specs=[row, row, row],
        out_shape=[SDS((S, D), bf16)] * 3,
        compiler_params=_cparams(("parallel",), VMEM_BIG),
        name="branch_gate_fwd",
    )(ya, yh, w_ba, w_bh, gc, gc)


def _gate_bwd(dmo, w_out, a, b, gc, w_ba, w_bh):
    S, D = a.shape
    W = w_ba.shape[0]
    tm = _pick(S, MM_ROWS)

    def body(dmo_ref, wo_ref, a_ref, b_ref, g0_ref, g1_ref, wa_ref, wh_ref,
             da_ref, db_ref, dg_ref, dya_ref, dyh_ref):
        dm = lax.dot_general(dmo_ref[...], wo_ref[...], NT, preferred_element_type=f32)
        dmv = dm.astype(bf16).astype(f32)
        s0, s1 = _sigmoid(g0_ref[...].astype(f32)), _sigmoid(g1_ref[...].astype(f32))
        da = (dmv * s0).astype(bf16)
        db = (dmv * s1).astype(bf16)
        da_ref[...] = da
        db_ref[...] = db
        dg_ref[:, :D] = (dmv * a_ref[...].astype(f32) * (s0 * (1.0 - s0))).astype(bf16)
        dg_ref[:, D:] = (dmv * b_ref[...].astype(f32) * (s1 * (1.0 - s1))).astype(bf16)
        dya_ref[...] = lax.dot_general(da, wa_ref[...], NT, preferred_element_type=f32)
        dyh_ref[...] = lax.dot_general(db, wh_ref[...], NT, preferred_element_type=f32)

    row = pl.BlockSpec((tm, D), lambda i: (i, 0))
    wide = pl.BlockSpec((tm, 2 * D), lambda i: (i, 0))
    narrow = pl.BlockSpec((tm, W), lambda i: (i, 0))
    whole = lambda t: pl.BlockSpec(t.shape, lambda i: (0, 0))
    return pl.pallas_call(
        body,
        grid=(S // tm,),
        in_specs=[row, whole(w_out), row, row, row, pl.BlockSpec((tm, D), lambda i: (i, 1)), whole(w_ba), whole(w_bh)],
        out_specs=[row, row, wide, narrow, narrow],
        out_shape=[SDS((S, D), bf16), SDS((S, D), bf16), SDS((S, 2 * D), bf16), SDS((S, W), f32), SDS((S, W), f32)],
        compiler_params=_cparams(("parallel",), VMEM_BIG),
        name="gate_bwd_fused",
    )(dmo, w_out, a, b, gc, gc, w_ba, w_bh)


CONV_ROWS = 512
INV_SQRT2 = 0.7071067811865476
INV_SQRT_2PI = 0.3989422804014327


CONV_HALO = 16


def _shift_down(cur, prev, k):
    x = pltpu.roll(cur, k, 0)
    row = lax.broadcasted_iota(jnp.int32, (SUBLANE, LANE), 0)
    head = jnp.where(row < k, pltpu.roll(prev, k, 0)[:SUBLANE], x[:SUBLANE])
    return jnp.concatenate([head, x[SUBLANE:]], axis=0)


def _shift_up(cur, nxt, k):
    R = cur.shape[0]
    x = pltpu.roll(cur, R - k, 0)
    row = lax.broadcasted_iota(jnp.int32, (SUBLANE, LANE), 0)
    tail = jnp.where(row >= SUBLANE - k, pltpu.roll(nxt, SUBLANE - k, 0), x[R - SUBLANE:])
    return jnp.concatenate([x[:R - SUBLANE], tail], axis=0)


def _conv_rows(u_ref, w, b, r0, first):
    R = CONV_ROWS
    cur = u_ref[pl.ds(r0, R), :].astype(f32)
    prev = u_ref[pl.ds(pl.multiple_of(jnp.maximum(r0 - CONV_HALO, 0), CONV_HALO), CONV_HALO), :].astype(f32)
    prev = jnp.where(first, 0.0, prev)
    x1 = _shift_down(cur, prev, 1)
    x2 = _shift_down(cur, prev, 2)
    c = ((b + w[0:1] * x2) + w[1:2] * x1) + w[2:3] * cur
    return c, x2, x1, cur


def _conv_fwd(ug, uv, wg, wv, bg, bv):
    S, F = ug.shape
    nchunk = S // CONV_ROWS

    def body(ug_ref, uv_ref, wg_ref, wv_ref, bg_ref, bv_ref, o_ref):
        wgv, wvv, bgv, bvv = wg_ref[...], wv_ref[...], bg_ref[...], bv_ref[...]

        def step(ci, carry):
            r0 = pl.multiple_of(ci * CONV_ROWS, CONV_ROWS)
            cg = _conv_rows(ug_ref, wgv, bgv, r0, ci == 0)[0]
            cv = _conv_rows(uv_ref, wvv, bvv, r0, ci == 0)[0]
            gelu = 0.5 * cg * (1.0 + lax.erf(cg * INV_SQRT2))
            o_ref[pl.ds(r0, CONV_ROWS), :] = (gelu * cv).astype(bf16)
            return carry

        lax.fori_loop(0, nchunk, step, 0)

    col = pl.BlockSpec((S, LANE), lambda j: (0, j))
    w3 = pl.BlockSpec((3, LANE), lambda j: (0, j))
    b1 = pl.BlockSpec((1, LANE), lambda j: (0, j))
    return pl.pallas_call(
        body,
        grid=(F // LANE,),
        in_specs=[col, col, w3, w3, b1, b1],
        out_specs=col,
        out_shape=SDS((S, F), bf16),
        compiler_params=_cparams(("parallel",), VMEM_BIG),
        name="conv_fwd",
    )(ug, uv, wg, wv, bg, bv)


def _conv_bwd(ug, uv, dact, wg, wv, bg, bv):
    S, F = ug.shape
    R = CONV_ROWS
    nchunk = S // R

    def body(ug_ref, uv_ref, da_ref, wg_ref, wv_ref, bg_ref, bv_ref, dug_ref, duv_ref, sg_ref, sv_ref, dcg, dcv):
        wgv, wvv, bgv, bvv = wg_ref[...], wv_ref[...], bg_ref[...], bv_ref[...]
        zero = jnp.zeros((SUBLANE, LANE), f32)

        def fwd_step(ci, acc):
            r0 = pl.multiple_of(ci * R, R)
            cg, g2, g1, g0 = _conv_rows(ug_ref, wgv, bgv, r0, ci == 0)
            cv, v2, v1, v0 = _conv_rows(uv_ref, wvv, bvv, r0, ci == 0)
            da = da_ref[pl.ds(r0, R), :].astype(f32)
            cdf = 0.5 * (1.0 + lax.erf(cg * INV_SQRT2))
            pdf = INV_SQRT_2PI * jnp.exp(-0.5 * cg * cg)
            dg = da * cv * (cdf + cg * pdf)
            dv = da * (cg * cdf)
            dcg[pl.ds(r0, R), :] = dg
            dcv[pl.ds(r0, R), :] = dv
            new = (acc[0] + _colsum8(dg * g2), acc[1] + _colsum8(dg * g1), acc[2] + _colsum8(dg * g0),
                   acc[3] + _colsum8(dg),
                   acc[4] + _colsum8(dv * v2), acc[5] + _colsum8(dv * v1), acc[6] + _colsum8(dv * v0),
                   acc[7] + _colsum8(dv))
            return new

        acc = lax.fori_loop(0, nchunk, fwd_step, (zero,) * 8)
        rows = lax.broadcasted_iota(jnp.int32, (SUBLANE, LANE), 0)

        def stats(parts):
            out = jnp.zeros((SUBLANE, LANE), f32)
            for k, pt in enumerate(parts):
                out = jnp.where(rows == k, jnp.sum(pt, axis=0, keepdims=True), out)
            return out

        sg_ref[...] = stats(acc[0:4])
        sv_ref[...] = stats(acc[4:8])

        def du_rows(dc, w, r0, last):
            cur = dc[pl.ds(r0, R), :]
            nxt = dc[pl.ds(pl.multiple_of(jnp.minimum(r0 + R, S - SUBLANE), SUBLANE), SUBLANE), :]
            nxt = jnp.where(last, 0.0, nxt)
            return w[2:3] * cur + w[1:2] * _shift_up(cur, nxt, 1) + w[0:1] * _shift_up(cur, nxt, 2)

        def bwd_step(ci, carry):
            r0 = pl.multiple_of(ci * R, R)
            last = ci == nchunk - 1
            dug_ref[pl.ds(r0, R), :] = du_rows(dcg, wgv, r0, last).astype(bf16)
            duv_ref[pl.ds(r0, R), :] = du_rows(dcv, wvv, r0, last).astype(bf16)
            return carry

        lax.fori_loop(0, nchunk, bwd_step, 0)

    col = pl.BlockSpec((S, LANE), lambda j: (0, j))
    w3 = pl.BlockSpec((3, LANE), lambda j: (0, j))
    b1 = pl.BlockSpec((1, LANE), lambda j: (0, j))
    st = pl.BlockSpec((SUBLANE, LANE), lambda j: (0, j))
    return pl.pallas_call(
        body,
        grid=(F // LANE,),
        in_specs=[col, col, col, w3, w3, b1, b1],
        out_specs=[col, col, st, st],
        out_shape=[SDS((S, F), bf16), SDS((S, F), bf16), SDS((SUBLANE, F), f32), SDS((SUBLANE, F), f32)],
        scratch_shapes=[pltpu.VMEM((S, LANE), f32), pltpu.VMEM((S, LANE), f32)],
        compiler_params=_cparams(("parallel",), VMEM_BIG),
        name="conv_bwd",
    )(ug, uv, dact, wg, wv, bg, bv)


def _adam_math(w, g, m, v):
    m = ADAM_B1 * m + (1.0 - ADAM_B1) * g
    v = ADAM_B2 * v + (1.0 - ADAM_B2) * (g * g)
    m_hat = m / (1.0 - ADAM_B1 ** ADAM_STEP)
    v_hat = v / (1.0 - ADAM_B2 ** ADAM_STEP)
    delta = -ADAM_LR * (m_hat / (jnp.sqrt(v_hat) + ADAM_EPS) + ADAM_WD * w)
    return delta, m, v


def _adamw(w, m, v, g, name):
    R, C = w.shape
    parts = g.ndim == 3
    tr = R
    if R % 16 == 0:
        for t in range(R, 0, -16):
            if R % t == 0 and t * C * 4 <= ADAM_BLOCK_BYTES:
                tr = t
                break

    def body(w_ref, m_ref, v_ref, g_ref, go_ref, d_ref, mo_ref, vo_ref):
        if parts:
            gv = ((g_ref[0].astype(f32) + g_ref[1].astype(f32)) + g_ref[2].astype(f32)) + g_ref[3].astype(f32)
        else:
            gv = g_ref[...]
        go_ref[...] = gv
        d, mn, vn = _adam_math(w_ref[...], gv, m_ref[...], v_ref[...])
        d_ref[...] = d
        mo_ref[...] = mn
        vo_ref[...] = vn

    row = pl.BlockSpec((tr, C), lambda i: (i, 0))
    gspec = pl.BlockSpec((4, tr, C), lambda i: (0, i, 0)) if parts else row
    return pl.pallas_call(
        body,
        grid=(R // tr,),
        in_specs=[row, row, row, gspec],
        out_specs=[row] * 4,
        out_shape=[SDS((R, C), f32)] * 4,
        compiler_params=_cparams(("parallel",), VMEM_BIG),
        name=name,
    )(w, m, v, g)


def _sum8(parts, name):
    _, _, R, C = parts.shape

    def body(p_ref, o_ref):
        acc = p_ref[0, 0]
        for c in range(2):
            for k in range(4):
                if c or k:
                    acc = acc + p_ref[c, k]
        o_ref[...] = acc

    return pl.pallas_call(body, out_shape=SDS((R, C), f32), name=name)(parts)


def _pair_add(by_core, b, name):
    _, K, R, C = by_core.shape
    tr = R // 2 if R % 32 == 0 else R

    def body(c_ref, a_ref, b_ref, o_ref):
        o_ref[...] = (a_ref[0].astype(f32) + b_ref[...].astype(f32)).astype(bf16)

    blk = pl.BlockSpec((1, tr, C), lambda k, i, c: (k, i, 0))
    return pl.pallas_call(
        body,
        grid_spec=pltpu.PrefetchScalarGridSpec(
            num_scalar_prefetch=1,
            grid=(K, R // tr),
            in_specs=[pl.BlockSpec((1, 1, tr, C), lambda k, i, c: (c[0], k, i, 0)), blk],
            out_specs=blk,
        ),
        out_shape=SDS((K, R, C), bf16),
        compiler_params=_cparams(("parallel", "parallel")),
        name=name,
    )(lax.axis_index("c").astype(jnp.int32).reshape(1), by_core, b)


_ANY = pl.BlockSpec(memory_space=pl.ANY)


def _chip_out_shape(src, gather):
    return SDS((4,) + tuple(src.shape if gather else src.shape[1:]), src.dtype)


def _fill_own(out, src, gather):
    mine = 2 * lax.axis_index("x") + lax.axis_index("y")
    own = src if gather else lax.dynamic_index_in_dim(src, mine, axis=0, keepdims=False)
    return lax.dynamic_update_index_in_dim(out, own, mine, axis=0)


_HBM = pl.BlockSpec(memory_space=pltpu.HBM)
_SEM = pl.BlockSpec(memory_space=pltpu.SEMAPHORE)
_EFFECT = pltpu.SideEffectType.DATAFLOW_SIDE_EFFECTING
_SPLIT_PEERS = {"chip_gather": 3, "chip_xchg": 3, "core_gather": 1, "core_swap": 1}


def _split_land(src, kind):
    if kind == "core_gather":
        return SDS((2,) + tuple(src.shape), src.dtype)
    if kind == "core_swap":
        return SDS(tuple(src.shape[1:]), src.dtype)
    return _chip_out_shape(src, kind == "chip_gather")


def _split_copies(src_ref, land_ref, sems, kind):
    x, y, c = lax.axis_index("x"), lax.axis_index("y"), lax.axis_index("c")
    n = _SPLIT_PEERS[kind]
    if kind == "core_gather":
        routes = [((x, y, 1 - c), src_ref, land_ref.at[c], land_ref.at[1 - c])]
    elif kind == "core_swap":
        routes = [((x, y, 1 - c), src_ref.at[1 - c], land_ref, land_ref)]
    else:
        mine = 2 * x + y
        gather = kind == "chip_gather"
        routes = [((px, py, c), src_ref if gather else src_ref.at[2 * px + py], land_ref.at[mine],
                   land_ref.at[2 * px + py]) for px, py in [(1 - x, y), (x, 1 - y), (1 - x, 1 - y)]]
    sends, recvs = [], []
    for j, (peer, piece, there, here) in enumerate(routes):
        sends.append(pltpu.make_async_remote_copy(src_ref=piece, dst_ref=there, send_sem=sems[j],
                                                  recv_sem=sems[n + j], device_id=peer, device_id_type=MESH))
        recvs.append(pltpu.make_async_remote_copy(src_ref=piece, dst_ref=here, send_sem=sems[j],
                                                  recv_sem=sems[n + j], device_id=peer, device_id_type=MESH))
    return sends, recvs


def _split_start(src, kind, name, after=None):
    land = _split_land(src, kind)
    ns = 2 * _SPLIT_PEERS[kind]
    n_in = 2 if after is None else 3

    def body(*refs):
        src_ref, land_ref = refs[:2]
        outs = refs[n_in:]
        for cp in _split_copies(src_ref, land_ref, outs[:ns], kind)[0]:
            cp.start()
        token = outs[ns + 2]
        token[...] = jnp.zeros_like(token)

    res = pl.pallas_call(
        body,
        name=name,
        out_shape=(pltpu.SemaphoreType.DMA(()),) * ns
        + (pltpu.HBM(src.shape, src.dtype), pltpu.HBM(land.shape, land.dtype), SDS((SUBLANE, LANE), f32)),
        in_specs=(_HBM, _HBM) + (() if after is None else (_ANY,)),
        out_specs=(_SEM,) * ns + (_HBM, _HBM, pl.BlockSpec(memory_space=pltpu.VMEM)),
        input_output_aliases={0: ns, 1: ns + 1},
        compiler_params=pltpu.CompilerParams(has_side_effects=_EFFECT),
    )(pltpu.with_memory_space_constraint(src, pltpu.HBM),
      pltpu.with_memory_space_constraint(lax.empty(land.shape, land.dtype), pltpu.HBM),
      *(() if after is None else (after,)))
    return (res[:ns], res[ns], res[ns + 1]), res[ns + 2]


def _split_wait(state, after, kind, name):
    sems, src_thru, land_thru = state
    ns = 2 * _SPLIT_PEERS[kind]

    def body(src_ref, land_ref, *rest):
        sends, recvs = _split_copies(src_ref, land_ref, rest[:ns], kind)
        for cp in recvs:
            cp.wait_recv()
        for cp in sends:
            cp.wait_send()

    src_out, got = pl.pallas_call(
        body,
        name=name,
        out_shape=(pltpu.HBM(src_thru.shape, src_thru.dtype), pltpu.HBM(land_thru.shape, land_thru.dtype)),
        in_specs=(_HBM, _HBM) + (_SEM,) * ns + (_ANY,),
        out_specs=(_HBM, _HBM),
        input_output_aliases={0: 0, 1: 1},
        compiler_params=pltpu.CompilerParams(has_side_effects=_EFFECT),
    )(src_thru, land_thru, *sems, after)
    if kind == "core_swap":
        return got, src_out
    if kind == "core_gather":
        return lax.dynamic_update_index_in_dim(got, src_out, lax.axis_index("c"), axis=0)
    return _fill_own(got, src_out, kind == "chip_gather")


def _core_gather(src, name):
    def body(src_ref, out_ref, send_sem, recv_sem):
        x, y, c = lax.axis_index("x"), lax.axis_index("y"), lax.axis_index("c")
        cp = pltpu.make_async_remote_copy(src_ref=src_ref, dst_ref=out_ref.at[c], send_sem=send_sem,
                                          recv_sem=recv_sem, device_id=(x, y, 1 - c), device_id_type=MESH)
        cp.start()
        pltpu.make_async_remote_copy(src_ref=src_ref, dst_ref=out_ref.at[1 - c], send_sem=send_sem,
                                     recv_sem=recv_sem, device_id=(x, y, 1 - c), device_id_type=MESH).wait_recv()
        cp.wait_send()

    out = pl.pallas_call(
        body,
        in_specs=[_ANY],
        out_specs=_ANY,
        out_shape=SDS((2,) + tuple(src.shape), src.dtype),
        scratch_shapes=[pltpu.SemaphoreType.DMA, pltpu.SemaphoreType.DMA],
        name=name,
    )(src)
    return lax.dynamic_update_index_in_dim(out, src, lax.axis_index("c"), axis=0)


_PACK_A = (("w_in", (1088, 1024)),)
_PACK_B = (("w_ba", (512, 128)), ("w_bh", (512, 128)), ("w_out", (128, 1024)), ("w_up", (704, 1024)),
           ("w_down", (352, 1024)))
_PACK_SIZES = _PACK_A + _PACK_B
_TRANSPOSED = ("w_in", "w_up")


def _slab_rows(sizes):
    return sum(r * c for _, (r, c) in sizes) // D_MODEL


def _pack_rows(d, sizes):
    n = d[sizes[0][0]].shape[0]
    return jnp.concatenate([d[k].reshape(n, -1, D_MODEL) for k, _ in sizes], axis=1)


def _unpack_rows(slab, sizes):
    n = slab.shape[0]
    out, lo = {}, 0
    for key, (r, c) in sizes:
        rows = r * c // D_MODEL
        out[key] = slab[:, lo:lo + rows].reshape(n, r, c)
        lo += rows
    return out


def _by_core(gslab):
    return jnp.swapaxes(gslab.reshape((4, 2) + gslab.shape[1:]), 0, 1)


def _cols_to_full(t):
    return jnp.swapaxes(t, 0, 1).reshape(t.shape[1], -1)


def _full_to_cols(t):
    K = t.shape[0]
    return jnp.swapaxes(t.reshape(K, 8, -1), 0, 1)


_SMALL = (("pre_mix_norm", (1, 1024)), ("rel_bias", (32, 24)), ("hgrn_lb_raw", (2, 512)), ("hgrn_norm", (1, 128)),
          ("post_mix_norm", (1, 1024)), ("pre_ffn_norm", (1, 1024)), ("conv_b", (1, 5632)),
          ("post_ffn_norm", (1, 1024)))
_SMALL_ROWS = 96
_CONVW_ROWS = 136


_SMALL_USED = sum(r * c for _, (r, c) in _SMALL)


def _pack_small(d, extra=None):
    flat = jnp.concatenate([d[k].reshape(-1) for k, _ in _SMALL] + ([] if extra is None else [extra.reshape(-1)]))
    flat = jnp.pad(flat, (0, _SMALL_ROWS * LANE - flat.shape[0]))
    return flat.reshape(_SMALL_ROWS, LANE)


def _unpack_small(p):
    flat = p.reshape(-1)
    out, lo = {}, 0
    for k, shp in _SMALL:
        n = shp[0] * shp[1]
        out[k] = flat[lo:lo + n].reshape(shp)
        lo += n
    return out


def _local_step(x, tgt, P, plan):
    S = x.shape[0]
    P = dict(P)
    lb = _lb_fwd(P["hgrn_lb_raw"])
    hs = _prep(x, P["pre_mix_norm"], plan.start_token())
    h1 = hs[0]
    consts = [_bias_consts(d) for d in DILATIONS]
    biases, dep = [], h1
    for g in range(N_GROUPS):
        tab_t = P["rel_bias"][:, 8 * g:8 * g + 8].T
        dep = _bias_build(tab_t, consts[g][0], consts[g][1], f"bias_build{g}", dep)
        biases.append(dep.reshape(8, ATTN_BLOCK, 2 * ATTN_BLOCK))
    W = dict(plan.weights_a(dep))
    qkv0, hg, gc = _mm_fanout(h1, [W["wt_qkv"][0], W["wt_hg"], W["wt_gate"]], "nt", [bf16, f32, bf16], "proj_natural")
    qkv = [qkv0] + [_mm(hs[g], W["wt_qkv"][g], "nt", bf16, f"proj_qkv{g}") for g in (1, 2)]
    obuf, lbuf, token = [], [], None
    for g, d in enumerate(DILATIONS):
        o_g, l_g = _attn_fwd(qkv[g], biases[g], (S // d) // ATTN_BLOCK, f"attn_fwd{g}", after=token)
        lbuf.append(l_g)
        obuf.append(o_g)
        if g == 0:
            token = plan.forward_b(o_g)
    y_attn, y_attn_b, w0, w1, w2 = _attn_merge(obuf[0], obuf[1], obuf[2], lbuf[0], lbuf[1], lbuf[2])
    y_hgrn, o_raw, ck = _hgrn_fwd(hg, lb, P["hgrn_norm"])
    wb = plan.weights_b(y_hgrn)
    P["conv_w"] = wb.pop("conv_w")
    W.update(wb)
    a, b, merged = _gate_fwd(y_attn_b, y_hgrn, W["w_ba"], W["w_bh"], gc)
    mo, x1, h2 = _mid_fwd(x, merged, W["w_out"], P["post_mix_norm"], P["pre_ffn_norm"])
    ug, uv = _mm_fanout(h2, [W["wt_up_g"], W["wt_up_v"]], "nt", [bf16, bf16], "up_proj")
    cw_g, cw_v = P["conv_w"][:, :D_FF], P["conv_w"][:, D_FF:]
    cb_g, cb_v = P["conv_b"][:, :D_FF], P["conv_b"][:, D_FF:]
    act = _conv_fwd(ug, uv, cw_g, cw_v, cb_g, cb_v)
    loss, dy, dfo, g_post_ffn, dact = _final(x1, act, W["w_down"], tgt, P["post_ffn_norm"])
    gW_down = _mm(act, dfo, "tn", bf16, "gw_down")
    dug, duv, st_g, st_v = _conv_bwd(ug, uv, dact, cw_g, cw_v, cb_g, cb_v)
    gW_up_g = _mm(dug, h2, "tn", bf16, "gw_up_gate")
    gW_up_v = _mm(duv, h2, "tn", bf16, "gw_up_val")
    dx1, dmo, g_pre_ffn, g_post_mix = _mid_bwd(dy, dug, duv, W["wt_up_g"], W["wt_up_v"], x1, mo, P["pre_ffn_norm"],
                                               P["post_mix_norm"])
    gW_out = _mm(merged, dmo, "tn", bf16, "gw_out")
    da, db, dgc, dyattn, dyhgrn = _gate_bwd(dmo, W["w_out"], a, b, gc, W["w_ba"], W["w_bh"])
    gW_ba = _mm(y_attn_b, da, "tn", bf16, "gw_ba")
    gW_bh = _mm(y_hgrn, db, "tn", bf16, "gw_bh")
    big_b = dict(w_ba=gW_ba, w_bh=gW_bh, w_out=gW_out, w_up=[gW_up_g, gW_up_v], w_down=gW_down)
    dos = _attn_merge_bwd(dyattn, y_attn, w0, w1, w2, after=plan.grads_b_start(big_b))
    dq_h, df_h, dv_h, dog_h, glb8, gnw8 = _hgrn_bwd(hg, o_raw, dyhgrn, ck, lb, P["hgrn_norm"],
                                                   after=plan.grads_b_exchange(dos[5]))
    dhg = [dq_h, df_h, dv_h, dog_h]
    g_lb_raw = _lb_bwd(P["hgrn_lb_raw"], glb8[0:1])
    gn = gnw8[0:1]
    g_hgrn_norm = (gn[:, 0:128] + gn[:, 128:256]) + (gn[:, 256:384] + gn[:, 384:512])
    dqkvs, gW_qkv, g_rel = [], [], []
    for g, d in enumerate(DILATIONS):
        dq, dk, dv, dbias = _attn_bwd(qkv[g], biases[g], dos[g], dos[3 + g], lbuf[g], (S // d) // ATTN_BLOCK,
                                      f"attn_bwd{g}")
        dqkvs.append([dq, dk, dv])
        gW_qkv.append(_mm(dqkvs[g], hs[g], "tn", bf16, f"gw_qkv{g}"))
        g_rel.append(_bias_grad(dbias.reshape(8, -1), consts[g][0], f"bias_grad{g}"))
    gW_hg = _mm(dhg, h1, "tn", bf16, "gw_hg")
    gW_gate = _mm(dgc, h1, "tn", bf16, "gw_gate")
    gW_in = gW_qkv + [gW_hg, gW_gate]
    token = plan.grads_a_start(gW_in)
    dh_perm = [_mm(dqkvs[g], W["wt_qkv"][g], "nn", f32, f"dh1_qkv{g}", after=token) for g in (1, 2)]
    token = plan.grads_a_exchange(dh_perm[1])
    dh_main = _mm(dqkvs[0] + dhg + [dgc], [W["wt_qkv"][0], W["wt_hg"], W["wt_gate"]], "nn", f32, "dh1_main",
                  after=token)
    grad_x, g_pre_mix = _first_bwd(x, dx1, dh_main, dh_perm[0], dh_perm[1], P["pre_mix_norm"])

    g_conv_w = jnp.concatenate([st_g[0:3], st_v[0:3]], axis=1)
    g_conv_b = jnp.concatenate([st_g[3:4], st_v[3:4]], axis=1)
    small = dict(pre_mix_norm=g_pre_mix, rel_bias=jnp.concatenate(g_rel, axis=1), hgrn_lb_raw=g_lb_raw,
                 hgrn_norm=g_hgrn_norm, post_mix_norm=g_post_mix, pre_ffn_norm=g_pre_ffn, conv_b=g_conv_b,
                 post_ffn_norm=g_post_ffn, conv_w=g_conv_w)
    return loss, grad_x, gW_in, big_b, small


def _weights_a(both):
    wt = jnp.swapaxes(both, 0, 1).reshape(-1, D_MODEL)
    return dict(
        wt_qkv=[wt[g * QKV_G:(g + 1) * QKV_G] for g in range(N_GROUPS)],
        wt_hg=wt[3 * QKV_G:3 * QKV_G + 4 * HGRN_W],
        wt_gate=wt[3 * QKV_G + 4 * HGRN_W:],
    )


def _weights_b(slabs):
    sh = _unpack_rows(slabs, _PACK_B)
    wt_up = sh["w_up"].reshape(-1, D_MODEL)
    return dict(
        w_ba=_cols_to_full(sh["w_ba"]),
        w_bh=_cols_to_full(sh["w_bh"]),
        w_out=sh["w_out"].reshape(D_MODEL, D_MODEL),
        wt_up_g=wt_up[:D_FF],
        wt_up_v=wt_up[D_FF:],
        w_down=sh["w_down"].reshape(D_FF, D_MODEL),
    )


def _dest_rows(sections, height):
    out = []
    for j in range(8):
        lo, hi, off, pieces = j * height, (j + 1) * height, 0, []
        for s in sections:
            a, b = max(lo, off), min(hi, off + s.shape[0])
            if a < b:
                pieces.append(s[a - off:b - off])
            off += s.shape[0]
        out.append(pieces[0] if len(pieces) == 1 else jnp.concatenate(pieces, axis=0))
    return out


def _grad_blocks_a(sections):
    rows = _dest_rows(sections, 1088)
    return jnp.stack([jnp.stack([rows[2 * k + c].astype(bf16) for k in range(4)]) for c in range(2)])


def _grad_slab_b(g):
    shards = dict(w_ba=_full_to_cols(g["w_ba"]), w_bh=_full_to_cols(g["w_bh"]), w_out=g["w_out"].reshape(8, 128, D_MODEL),
                  w_up=jnp.stack(_dest_rows(g["w_up"], 704)), w_down=g["w_down"].reshape(8, 352, D_MODEL))
    return _pack_rows({k: v.astype(bf16) for k, v in shards.items()}, _PACK_B)


_CONVW_SLAB_ROWS = 16


class _Traffic:
    def __init__(self, slab_a, slab_b, conv_w):
        hi = conv_w.astype(bf16)
        r1 = conv_w - hi.astype(f32)
        mid = r1.astype(bf16)
        lo = (r1 - mid.astype(f32)).astype(bf16)
        bits = jnp.stack([hi, mid, lo]).reshape(-1)
        tail = jnp.pad(bits, (0, _CONVW_SLAB_ROWS * D_MODEL - bits.shape[0])).reshape(_CONVW_SLAB_ROWS, D_MODEL)
        self.slab_b = jnp.concatenate([slab_b, tail], axis=0)
        self.state_a, tok = _split_start(slab_a, "chip_gather", "ag_a_start")
        self.state_b, self.token = _split_start(self.slab_b, "chip_gather", "ag_b_start", after=tok)
        self.state = None
        self.state_gb = None

    def start_token(self):
        return self.token

    def weights_a(self, after):
        by_chip = _split_wait(self.state_a, after, "chip_gather", "ag_a_wait")
        return _weights_a(_core_gather(by_chip, "ag_a_cores"))

    def forward_b(self, after):
        by_chip = _split_wait(self.state_b, after, "chip_gather", "ag_b_wait")
        self.state, token = _split_start(by_chip, "core_gather", "ag_b_cores_start")
        return token

    def weights_b(self, after):
        both = _split_wait(self.state, after, "core_gather", "ag_b_cores_wait")
        slabs = jnp.swapaxes(both, 0, 1).reshape((8,) + tuple(self.slab_b.shape))
        rows = _slab_rows(_PACK_B)
        out = _weights_b(slabs[:, :rows])
        pieces = slabs[:, rows:].reshape(8, -1)[:, :3 * 3 * 704].reshape(8, 3, 3, 704).astype(f32)
        out["conv_w"] = _cols_to_full((pieces[:, 0] + pieces[:, 1]) + pieces[:, 2])
        return out

    def grads_b_start(self, grads):
        self.state, token = _split_start(_by_core(_grad_slab_b(grads)), "core_swap", "rs_b_cores_start")
        return token

    def grads_b_exchange(self, after):
        from_sib, by_core = _split_wait(self.state, after, "core_swap", "rs_b_cores_wait")
        self.state_gb, token = _split_start(_pair_add(by_core, from_sib, "rs_b_pair_add"), "chip_xchg", "rs_b_start")
        return token

    def grads_a_start(self, sections):
        self.state, token = _split_start(_grad_blocks_a(sections), "core_swap", "rs_a_cores_start")
        return token

    def grads_a_exchange(self, after):
        from_sib, by_core = _split_wait(self.state, after, "core_swap", "rs_a_cores_wait")
        self.state, token = _split_start(_pair_add(by_core, from_sib, "rs_a_pair_add"), "chip_xchg", "rs_a_start")
        return token

    def parts(self, after):
        parts = _unpack_rows(_split_wait(self.state_gb, after, "chip_xchg", "rs_b_wait"), _PACK_B)
        parts["w_in"] = _split_wait(self.state, after, "chip_xchg", "rs_a_wait")
        return parts


def kernel(x, pre_mix_norm, w_in, rel_bias, hgrn_lb_raw, hgrn_norm, w_branch_attn, w_branch_hgrn, w_out, post_mix_norm, pre_ffn_norm, w_up, conv_w, conv_b, w_down, post_ffn_norm, loss_target, m_pre_mix_norm, m_w_in, m_rel_bias, m_hgrn_lb_raw, m_hgrn_norm, m_w_branch_attn, m_w_branch_hgrn, m_w_out, m_post_mix_norm, m_pre_ffn_norm, m_w_up, m_conv_w, m_conv_b, m_w_down, m_post_ffn_norm, v_pre_mix_norm, v_w_in, v_rel_bias, v_hgrn_lb_raw, v_hgrn_norm, v_w_branch_attn, v_w_branch_hgrn, v_w_out, v_post_mix_norm, v_pre_ffn_norm, v_w_up, v_conv_w, v_conv_b, v_w_down, v_post_ffn_norm):
    ci = lax.axis_index("c")
    dev = 4 * lax.axis_index("x") + 2 * lax.axis_index("y") + ci
    tr = lambda t: jnp.swapaxes(t[0], 0, 1)
    wts = dict(w_in=tr(w_in), w_ba=w_branch_attn[0], w_bh=w_branch_hgrn[0], w_out=w_out[0], w_up=tr(w_up),
               w_down=w_down[0])
    mom = dict(w_in=tr(m_w_in), w_ba=m_w_branch_attn[0], w_bh=m_w_branch_hgrn[0], w_out=m_w_out[0], w_up=tr(m_w_up),
               w_down=m_w_down[0])
    var = dict(w_in=tr(v_w_in), w_ba=v_w_branch_attn[0], w_bh=v_w_branch_hgrn[0], w_out=v_w_out[0], w_up=tr(v_w_up),
               w_down=v_w_down[0])
    small_w = dict(pre_mix_norm=pre_mix_norm, rel_bias=rel_bias, hgrn_lb_raw=hgrn_lb_raw, hgrn_norm=hgrn_norm,
                   post_mix_norm=post_mix_norm, pre_ffn_norm=pre_ffn_norm, conv_b=conv_b, post_ffn_norm=post_ffn_norm)
    small_m = dict(pre_mix_norm=m_pre_mix_norm, rel_bias=m_rel_bias, hgrn_lb_raw=m_hgrn_lb_raw, hgrn_norm=m_hgrn_norm,
                   post_mix_norm=m_post_mix_norm, pre_ffn_norm=m_pre_ffn_norm, conv_b=m_conv_b,
                   post_ffn_norm=m_post_ffn_norm)
    small_v = dict(pre_mix_norm=v_pre_mix_norm, rel_bias=v_rel_bias, hgrn_lb_raw=v_hgrn_lb_raw, hgrn_norm=v_hgrn_norm,
                   post_mix_norm=v_post_mix_norm, pre_ffn_norm=v_pre_ffn_norm, conv_b=v_conv_b,
                   post_ffn_norm=v_post_ffn_norm)

    plan = _Traffic(wts["w_in"].astype(bf16),
                    _pack_rows({k: wts[k].astype(bf16)[None] for k, _ in _PACK_B}, _PACK_B)[0], conv_w[0])

    loss8, grad_x, _, _, small = _local_step(x[0], loss_target[0], small_w, plan)
    spack = jnp.concatenate([_pack_small(small, loss8[0, 0:1]),
                             jnp.pad(small["conv_w"].reshape(-1, LANE), ((0, _CONVW_ROWS - 132), (0, 0)))], axis=0)
    small_state, token = _split_start(spack, "chip_gather", "ag_small_start")

    parts = plan.parts(token)
    outs_big = {}
    for k, _ in _PACK_SIZES:
        outs_big[k] = _adamw(wts[k], mom[k], var[k], parts[k], "adamw_" + k)

    by_chip = _split_wait(small_state, outs_big["w_in"][1], "chip_gather", "ag_small_wait")
    allp = _core_gather(by_chip, "ag_small_cores")
    ssum = _sum8(allp, "small_sum")
    gs = ssum[:_SMALL_ROWS]
    loss = ssum[_SMALL_USED // LANE, _SMALL_USED % LANE]
    res_small = _adamw(_pack_small(small_w), _pack_small(small_m), _pack_small(small_v), gs, "adamw_small")
    sm = [_unpack_small(t) for t in res_small]
    g_cw_full = ssum[_SMALL_ROWS:_SMALL_ROWS + 132].reshape(3, 2 * D_FF)
    g_cw = lax.dynamic_slice_in_dim(g_cw_full, dev * 704, 704, axis=1)
    res_cw = _adamw(conv_w[0], m_conv_w[0], v_conv_w[0], g_cw, "adamw_conv_w")

    def pick(i):
        def big_(k):
            t = outs_big[k][i]
            return (jnp.swapaxes(t, 0, 1) if k in _TRANSPOSED else t)[None]
        return [sm[i]["pre_mix_norm"], big_("w_in"), sm[i]["rel_bias"], sm[i]["hgrn_lb_raw"], sm[i]["hgrn_norm"],
                big_("w_ba"), big_("w_bh"), big_("w_out"), sm[i]["post_mix_norm"], sm[i]["pre_ffn_norm"],
                big_("w_up"), res_cw[i][None], sm[i]["conv_b"], big_("w_down"), sm[i]["post_ffn_norm"]]

    return (loss, grad_x[None], *pick(0), *pick(1), *pick(2), *pick(3))
```

```python
import functools
import math

import jax
import jax.numpy as jnp
from jax import lax
from jax.experimental import pallas as pl
from jax.experimental.pallas import tpu as pltpu

f32 = jnp.float32
bf16 = jnp.bfloat16
SDS = jax.ShapeDtypeStruct
HIGHEST = lax.Precision.HIGHEST
MESH = pl.DeviceIdType.MESH

NN = (((1,), (0,)), ((), ()))
NT = (((1,), (1,)), ((), ()))
TN = (((0,), (0,)), ((), ()))

D_MODEL = 1024
N_GROUPS = 3
DILATIONS = (1, 4, 16)
HEAD_DIM = 64
ATTN_BLOCK = 128
QKV_G = 1536
ATTN_OUT = 512
HGRN_W = 512
HGRN_CHUNK = 32
D_FF = 2816
NUM_BUCKETS = 32
MAX_EXACT = 16
MAX_DISTANCE = 2048
NEG_INF = -1e30
EPS = 1e-6
LANE = 128
SUBLANE = 8
VMEM_BIG = 48 * 1024 * 1024
MM_ROWS = 512
MM_OUT_BYTES = 8 * 1024 * 1024
ADAM_BLOCK_BYTES = 2304 * 1024

ADAM_LR, ADAM_B1, ADAM_B2, ADAM_EPS, ADAM_WD, ADAM_STEP = 0.001, 0.9, 0.999, 1e-08, 0.01, 10


def _pick(n, pref):
    t = pref
    while t >= LANE:
        if n % t == 0:
            return t
        t //= 2
    return n


def _cparams(sem=None, vmem=None):
    kw = {}
    if sem is not None:
        kw["dimension_semantics"] = sem
    if vmem is not None:
        kw["vmem_limit_bytes"] = vmem
    return pltpu.CompilerParams(**kw)


def _sigmoid(x):
    return jax.nn.sigmoid(x)


def _colsum8(x):
    return x.reshape(x.shape[0] // SUBLANE, SUBLANE, x.shape[1]).sum(axis=0)


def _mm(a, b, mode, out_dtype, name, acc=None, after=None):
    dims = {"nn": NN, "nt": NT, "tn": TN}[mode]
    has_acc = acc is not None
    parts = list(a) if isinstance(a, (list, tuple)) else [a]
    if mode == "tn":
        assert not has_acc
        K, N = b.shape
        widths = [t.shape[1] for t in parts]
        M = sum(widths)
        whole = M * N * 4 <= MM_OUT_BYTES
        assert whole or len(parts) == 1
        tmm = M if whole else M // 2
        ts = _pick(K, 4 * MM_ROWS)
        nk = K // ts

        npart = len(parts)
        narrow = out_dtype != f32

        def body_tn(*refs):
            b_ref, o_ref = refs[npart], refs[npart + 1]
            acc_ref = refs[npart + 2] if narrow else o_ref
            k = pl.program_id(1)
            bv = b_ref[...]
            lo = 0
            for a_ref, w in zip(refs[:npart], widths if whole else [tmm]):
                part = lax.dot_general(a_ref[...], bv, dims, preferred_element_type=f32)
                rows = slice(lo, lo + w)
                lo += w

                @pl.when(k == 0)
                def _(part=part, rows=rows):
                    acc_ref[rows, :] = part

                @pl.when(k > 0)
                def _(part=part, rows=rows):
                    acc_ref[rows, :] += part

            if narrow:
                @pl.when(k == nk - 1)
                def _():
                    o_ref[...] = acc_ref[...].astype(out_dtype)

        return pl.pallas_call(
            body_tn,
            grid=(M // tmm, nk),
            in_specs=[pl.BlockSpec((ts, w if whole else tmm), lambda i, k: (k, i)) for w in widths]
            + [pl.BlockSpec((ts, N), lambda i, k: (k, 0))],
            out_specs=pl.BlockSpec((tmm, N), lambda i, k: (i, 0)),
            out_shape=SDS((M, N), out_dtype),
            scratch_shapes=[pltpu.VMEM((tmm, N), f32)] if narrow else [],
            compiler_params=_cparams(("parallel", "arbitrary"), VMEM_BIG),
            name=name,
        )(*parts, b)

    bs = list(b) if isinstance(b, (list, tuple)) else [b]
    widths = [t.shape[1] for t in parts]
    M = parts[0].shape[0]
    kdim = 0 if mode == "nn" else 1
    N = bs[0].shape[1 - kdim]
    tm = _pick(M, MM_ROWS)
    npart, nb = len(parts), len(bs)
    place, bi, lo = [], 0, 0
    for w in widths:
        place.append((bi, lo))
        lo += w
        if lo == bs[bi].shape[kdim]:
            bi, lo = bi + 1, 0
    assert bi == nb and lo == 0

    def body(*refs):
        a_refs, b_refs = refs[:npart], refs[npart:npart + nb]
        c_ref = refs[npart + nb] if has_acc else None
        o_ref = refs[-1]
        part = None
        for a_ref, w, (bi, lo) in zip(a_refs, widths, place):
            b_ref = b_refs[bi]
            if w == bs[bi].shape[kdim]:
                bk = b_ref[...]
            else:
                bk = b_ref[:, lo:lo + w] if mode == "nt" else b_ref[lo:lo + w, :]
            t = lax.dot_general(a_ref[...], bk, dims, preferred_element_type=f32)
            part = t if part is None else part + t
        if has_acc:
            part = part + c_ref[...]
        o_ref[...] = part.astype(out_dtype)

    specs = [pl.BlockSpec((tm, w), lambda i: (i, 0)) for w in widths] \
        + [pl.BlockSpec(t.shape, lambda i: (0, 0)) for t in bs]
    args = parts + bs
    aliases = {}
    if has_acc:
        specs.append(pl.BlockSpec((tm, N), lambda i: (i, 0)))
        args.append(acc)
        aliases = {npart + nb: 0}
    if after is not None:
        specs.append(pl.BlockSpec(memory_space=pl.ANY))
        args.append(after)
    return pl.pallas_call(
        body,
        grid=(M // tm,),
        in_specs=specs,
        out_specs=pl.BlockSpec((tm, N), lambda i: (i, 0)),
        out_shape=SDS((M, N), out_dtype),
        input_output_aliases=aliases,
        compiler_params=_cparams(("parallel",), VMEM_BIG),
        name=name,
    )(*args)


def _mm_fanout(a, bs, mode, out_dtypes, name):
    dims = {"nn": NN, "nt": NT}[mode]
    M, K = a.shape
    ns = [b.shape[1] if mode == "nn" else b.shape[0] for b in bs]
    tm = _pick(M, MM_ROWS)
    nb = len(bs)

    def body(a_ref, *refs):
        av = a_ref[...]
        for b_ref, o_ref, dt in zip(refs[:nb], refs[nb:], out_dtypes):
            o_ref[...] = lax.dot_general(av, b_ref[...], dims, preferred_element_type=f32).astype(dt)

    return pl.pallas_call(
        body,
        grid=(M // tm,),
        in_specs=[pl.BlockSpec((tm, K), lambda i: (i, 0))] + [pl.BlockSpec(b.shape, lambda i: (0, 0)) for b in bs],
        out_specs=[pl.BlockSpec((tm, n), lambda i: (i, 0)) for n in ns],
        out_shape=[SDS((M, n), dt) for n, dt in zip(ns, out_dtypes)],
        compiler_params=_cparams(("parallel",), VMEM_BIG),
        name=name,
    )(a, *bs)


PERM_ROWS = 2048


def _perm_spec(d, cols=LANE):
    return pl.BlockSpec((d, PERM_ROWS // d, cols), lambda i, j: (0, i, j))


def _to_natural(src_ref, dst_ref, d):
    n = src_ref.shape[1]
    for r in range(d):
        dst_ref[pl.ds(r, n, stride=d), :] = src_ref[r]


def _prep(x, w, after=None):
    S, D = x.shape
    R = PERM_ROWS
    nc = D // LANE
    n_in = nc + 1 + (after is not None)

    def body(*refs):
        x_refs, w_ref = refs[:nc], refs[nc]
        h_ref, h4_ref, h16_ref, rs = refs[n_in:]
        ssq = None
        for xr in x_refs:
            v = xr[...]
            t = jnp.sum(v * v, axis=-1, keepdims=True)
            ssq = t if ssq is None else ssq + t
        rinv = lax.rsqrt(ssq * (1.0 / D) + EPS)
        rs[...] = jnp.broadcast_to(rinv, (R, LANE))
        for j, xr in enumerate(x_refs):
            cols = slice(j * LANE, (j + 1) * LANE)
            wj = w_ref[:, cols]
            h_ref[:, cols] = ((xr[...] * rinv) * wj).astype(bf16)
            for d, o_ref in ((4, h4_ref), (16, h16_ref)):
                n = R // d
                for r in range(d):
                    rows = pl.ds(r, n, stride=d)
                    o_ref[r, :, cols] = ((xr[rows, :] * rs[rows, :]) * wj).astype(bf16)

    col = lambda j: pl.BlockSpec((R, LANE), lambda i, j=j: (i, j))
    h, h4, h16 = pl.pallas_call(
        body,
        grid=(S // R,),
        in_specs=[col(j) for j in range(nc)] + [pl.BlockSpec((1, D), lambda i: (0, 0))]
        + ([] if after is None else [pl.BlockSpec(memory_space=pl.ANY)]),
        out_specs=[pl.BlockSpec((R, D), lambda i: (i, 0)), pl.BlockSpec((4, R // 4, D), lambda i: (0, i, 0)),
                   pl.BlockSpec((16, R // 16, D), lambda i: (0, i, 0))],
        out_shape=[SDS((S, D), bf16), SDS((4, S // 4, D), bf16), SDS((16, S // 16, D), bf16)],
        scratch_shapes=[pltpu.VMEM((R, LANE), f32)],
        compiler_params=_cparams(("parallel",), VMEM_BIG),
        name="prep_norm_perm",
    )(*([x] * nc), w, *([] if after is None else [after]))
    return [h, h4.reshape(S, D), h16.reshape(S, D)]


def _rms_parts(xv):
    r = lax.rsqrt(jnp.mean(xv * xv, axis=-1, keepdims=True) + EPS)
    return r, xv * r


def _rms_bwd(xhat, r, w, dy):
    dyw = dy * w
    return r * (dyw - xhat * jnp.mean(dyw * xhat, axis=-1, keepdims=True))


def _mid_fwd(x, merged, w_out, w_pm, w_pf):
    S, D = x.shape
    tm = _pick(S, MM_ROWS)

    def body(x_ref, m_ref, wo_ref, wpm_ref, wpf_ref, mo_ref, x1_ref, h2_ref):
        mo = jnp.dot(m_ref[...], wo_ref[...], preferred_element_type=f32)
        mo_ref[...] = mo
        _, moh = _rms_parts(mo)
        x1 = x_ref[...] + moh * wpm_ref[...]
        x1_ref[...] = x1
        _, x1h = _rms_parts(x1)
        h2_ref[...] = (x1h * wpf_ref[...]).astype(bf16)

    row = pl.BlockSpec((tm, D), lambda i: (i, 0))
    vec = pl.BlockSpec((1, D), lambda i: (0, 0))
    return pl.pallas_call(
        body,
        grid=(S // tm,),
        in_specs=[row, pl.BlockSpec((tm, merged.shape[1]), lambda i: (i, 0)),
                  pl.BlockSpec(w_out.shape, lambda i: (0, 0)), vec, vec],
        out_specs=[row, row, row],
        out_shape=[SDS((S, D), f32), SDS((S, D), f32), SDS((S, D), bf16)],
        compiler_params=_cparams(("parallel",), VMEM_BIG),
        name="out_proj_mid_fwd",
    )(x, merged, w_out, w_pm, w_pf)


def _final(x1, act, w_down, tgt, w_pfn):
    S, D = x1.shape
    tm = _pick(S, MM_ROWS)
    nt = S // tm

    def body(x1_ref, a_ref, wd_ref, t_ref, w_ref, loss_ref, dy_ref, dfo_ref, gw_ref, lacc, gacc):
        i = pl.program_id(0)

        @pl.when(i == 0)
        def _():
            lacc[...] = jnp.zeros_like(lacc)
            gacc[...] = jnp.zeros_like(gacc)

        w = w_ref[...]
        r, foh = _rms_parts(jnp.dot(a_ref[...], wd_ref[...], preferred_element_type=f32))
        y = x1_ref[...] + foh * w
        err = y - t_ref[...]
        lacc[...] += _colsum8(err * err)
        dy = err * (1.0 / D)
        dy_ref[...] = dy
        gacc[...] += _colsum8(dy * foh)
        dfo_ref[...] = _rms_bwd(foh, r, w, dy).astype(bf16)

        @pl.when(i == nt - 1)
        def _():
            loss_ref[...] = jnp.full((SUBLANE, LANE), 0.5 / D, f32) * jnp.sum(lacc[...])
            gw_ref[...] = jnp.sum(gacc[...], axis=0, keepdims=True)

    row = pl.BlockSpec((tm, D), lambda i: (i, 0))
    vec = pl.BlockSpec((1, D), lambda i: (0, 0))
    return pl.pallas_call(
        body,
        grid=(nt,),
        in_specs=[row, pl.BlockSpec((tm, act.shape[1]), lambda i: (i, 0)),
                  pl.BlockSpec(w_down.shape, lambda i: (0, 0)), row, vec],
        out_specs=[pl.BlockSpec((SUBLANE, LANE), lambda i: (0, 0)), row, row, vec],
        out_shape=[SDS((SUBLANE, LANE), f32), SDS((S, D), f32), SDS((S, D), bf16), SDS((1, D), f32)],
        scratch_shapes=[pltpu.VMEM((SUBLANE, D), f32), pltpu.VMEM((SUBLANE, D), f32)],
        compiler_params=_cparams(("arbitrary",), VMEM_BIG),
        name="down_proj_final_loss",
    )(x1, act, w_down, tgt, w_pfn)


MID_BWD_ROWS = 256


def _mid_bwd(dy, dug, duv, wt_g, wt_v, x1, mo, w_pf, w_pm):
    S, D = dy.shape
    tm = _pick(S, MID_BWD_ROWS)
    nt = S // tm

    def body(dy_ref, dug_ref, duv_ref, wg_ref, wv_ref, x1_ref, mo_ref, wpf_ref, wpm_ref,
             dx1_ref, dmo_ref, gpf_ref, gpm_ref, apf, apm):
        i = pl.program_id(0)

        @pl.when(i == 0)
        def _():
            apf[...] = jnp.zeros_like(apf)
            apm[...] = jnp.zeros_like(apm)

        r1, x1h = _rms_parts(x1_ref[...])
        dh2 = jnp.dot(dug_ref[...], wg_ref[...], preferred_element_type=f32) \
            + jnp.dot(duv_ref[...], wv_ref[...], preferred_element_type=f32)
        apf[...] += _colsum8(dh2 * x1h)
        dx1 = dy_ref[...] + _rms_bwd(x1h, r1, wpf_ref[...], dh2)
        dx1_ref[...] = dx1
        rm, moh = _rms_parts(mo_ref[...])
        apm[...] += _colsum8(dx1 * moh)
        dmo_ref[...] = _rms_bwd(moh, rm, wpm_ref[...], dx1).astype(bf16)

        @pl.when(i == nt - 1)
        def _():
            gpf_ref[...] = jnp.sum(apf[...], axis=0, keepdims=True)
            gpm_ref[...] = jnp.sum(apm[...], axis=0, keepdims=True)

    row = pl.BlockSpec((tm, D), lambda i: (i, 0))
    vec = pl.BlockSpec((1, D), lambda i: (0, 0))
    return pl.pallas_call(
        body,
        grid=(nt,),
        in_specs=[row, pl.BlockSpec((tm, dug.shape[1]), lambda i: (i, 0)), pl.BlockSpec((tm, duv.shape[1]), lambda i: (i, 0)),
                  pl.BlockSpec(wt_g.shape, lambda i: (0, 0)), pl.BlockSpec(wt_v.shape, lambda i: (0, 0)),
                  row, row, vec, vec],
        out_specs=[row, row, vec, vec],
        out_shape=[SDS((S, D), f32), SDS((S, D), bf16), SDS((1, D), f32), SDS((1, D), f32)],
        scratch_shapes=[pltpu.VMEM((SUBLANE, D), f32), pltpu.VMEM((SUBLANE, D), f32)],
        compiler_params=_cparams(("arbitrary",), VMEM_BIG),
        name="dh2_mid_bwd",
    )(dy, dug, duv, wt_g, wt_v, x1, mo, w_pf, w_pm)


def _first_bwd(x, dx1, dh_a, dh_b, dh_c, w_pre):
    S, D = x.shape
    tm = _pick(S, 512)
    nt = S // tm
    nc = D // LANE

    def body(*refs):
        x_ref, dx1_ref, a_ref = refs[:3]
        b_refs, c_refs, w_ref = refs[3:3 + nc], refs[3 + nc:3 + 2 * nc], refs[3 + 2 * nc]
        gx_ref, gw_ref, acc, dh_s, sb, sc = refs[4 + 2 * nc:]
        i = pl.program_id(0)

        @pl.when(i == 0)
        def _():
            acc[...] = jnp.zeros_like(acc)

        for j in range(nc):
            cols = slice(j * LANE, (j + 1) * LANE)
            _to_natural(b_refs[j], sb, 4)
            _to_natural(c_refs[j], sc, 16)
            dh_s[:, cols] = (a_ref[:, cols] + sb[...]) + sc[...]
        r, xh = _rms_parts(x_ref[...])
        dh = dh_s[...]
        acc[...] += _colsum8(dh * xh)
        gx_ref[...] = dx1_ref[...] + _rms_bwd(xh, r, w_ref[...], dh)

        @pl.when(i == nt - 1)
        def _():
            gw_ref[...] = jnp.sum(acc[...], axis=0, keepdims=True)

    row = pl.BlockSpec((tm, D), lambda i: (i, 0))
    vec = pl.BlockSpec((1, D), lambda i: (0, 0))
    perm = lambda d: [pl.BlockSpec((d, tm // d, LANE), lambda i, j=j: (0, i, j)) for j in range(nc)]
    return pl.pallas_call(
        body,
        grid=(nt,),
        in_specs=[row, row, row] + perm(4) + perm(16) + [vec],
        out_specs=[row, vec],
        out_shape=[SDS((S, D), f32), SDS((1, D), f32)],
        scratch_shapes=[pltpu.VMEM((SUBLANE, D), f32), pltpu.VMEM((tm, D), f32), pltpu.VMEM((tm, LANE), f32),
                        pltpu.VMEM((tm, LANE), f32)],
        compiler_params=_cparams(("arbitrary",), VMEM_BIG),
        name="first_bwd",
    )(x, dx1, dh_a, *([dh_b.reshape(4, S // 4, D)] * nc), *([dh_c.reshape(16, S // 16, D)] * nc), w_pre)


def _t5_bucket(dist):
    n = jnp.maximum(dist, 0)
    nf = jnp.maximum(n, 1).astype(f32)
    large = MAX_EXACT + (jnp.log(nf / MAX_EXACT) / math.log(MAX_DISTANCE / MAX_EXACT)
                         * (NUM_BUCKETS - MAX_EXACT)).astype(jnp.int32)
    large = jnp.minimum(large, NUM_BUCKETS - 1)
    return jnp.where(n < MAX_EXACT, n, large)


def _bias_consts(d, after=None):
    if after is not None:
        d, _ = lax.optimization_barrier((jnp.int32(d), after))
    blk = ATTN_BLOCK
    rel = jnp.arange(blk)[:, None] + blk - jnp.arange(2 * blk)[None, :]
    in_win = (rel >= 0) & (rel <= blk)
    bucket = _t5_bucket(rel * d).reshape(1, -1)
    onehot = (bucket == jnp.arange(NUM_BUCKETS)[:, None]).astype(f32)
    return onehot, in_win.astype(f32).reshape(1, -1)


def _bias_build(tab_t, onehot, maskf, name, after):
    H = tab_t.shape[0]

    def body(t_ref, oh_ref, m_ref, after_ref, o_ref):
        b = jnp.dot(t_ref[...], oh_ref[...], precision=HIGHEST, preferred_element_type=f32)
        o_ref[...] = jnp.where(m_ref[...] > 0.5, b, NEG_INF)

    vm = pl.BlockSpec(memory_space=pltpu.VMEM)
    return pl.pallas_call(body, out_shape=SDS((H, onehot.shape[1]), f32), name=name,
                          in_specs=[vm, vm, vm, pl.BlockSpec(memory_space=pl.ANY)], out_specs=vm,
                          )(tab_t, onehot, maskf, after)


def _bias_grad(dbias_flat, onehot, name):
    H = dbias_flat.shape[0]

    def body(g_ref, oh_ref, o_ref):
        o_ref[...] = lax.dot_general(oh_ref[...], g_ref[...], NT, precision=HIGHEST, preferred_element_type=f32)

    return pl.pallas_call(body, out_shape=SDS((NUM_BUCKETS, H), f32), name=name)(dbias_flat, onehot)


ATTN_TILE = 512
ATTN_SUB = ATTN_TILE // ATTN_BLOCK
ATTN_HP = 4
ATTN_WIDE = ATTN_HP * LANE


def _qkv_specs(nt):
    tile = (ATTN_TILE, ATTN_WIDE)
    blk = (ATTN_BLOCK, ATTN_WIDE)
    sec = ATTN_OUT // ATTN_WIDE
    cur = lambda off: (lambda h, t: (jnp.minimum(t, nt - 1), off + h))
    prev = lambda off: (lambda h, t: (jnp.maximum(jnp.minimum(t, nt - 1) * ATTN_SUB - 1, 0), off + h))
    return [pl.BlockSpec(tile, cur(0)), pl.BlockSpec(blk, prev(sec)), pl.BlockSpec(tile, cur(sec)),
            pl.BlockSpec(blk, prev(2 * sec)), pl.BlockSpec(tile, cur(2 * sec))]


def _head_masks():
    lane = lax.broadcasted_iota(jnp.int32, (ATTN_BLOCK, LANE), 1)
    return lane < HEAD_DIM


def _stack_heads(x2, low):
    zero = jnp.zeros_like(x2)
    return jnp.concatenate([jnp.where(low, x2, zero), jnp.where(low, zero, x2)], axis=0)


def _attn_fwd(qkv, bias, bps, name, after=None):
    S = qkv.shape[0]
    nt = S // ATTN_TILE
    scale = HEAD_DIM ** -0.5

    def body(q_ref, kp_ref, kc_ref, vp_ref, vc_ref, b_ref, *rest):
        o_ref, l_ref = rest[-2:]
        t = pl.program_id(1)
        low = _head_masks()
        col = lax.broadcasted_iota(jnp.int32, (2 * ATTN_BLOCK, 2 * ATTN_BLOCK), 1)
        for hp in range(ATTN_HP):
            cols = slice(hp * LANE, (hp + 1) * LANE)
            kk = jnp.concatenate([kp_ref[:, cols], kc_ref[:, cols]], axis=0)
            vv = jnp.concatenate([vp_ref[:, cols], vc_ref[:, cols]], axis=0)
            bias2 = b_ref[2 * hp:2 * hp + 2].reshape(2 * ATTN_BLOCK, 2 * ATTN_BLOCK)
            for b in range(ATTN_SUB):
                lo = b * ATTN_BLOCK
                rows = slice(lo, lo + ATTN_BLOCK)
                keys = slice(lo, lo + 2 * ATTN_BLOCK)
                dead = jnp.logical_and((t * ATTN_SUB + b) % bps == 0, col < ATTN_BLOCK)
                q2 = _stack_heads(q_ref[rows, cols], low)
                kb, vb = kk[keys], vv[keys]
                s = lax.dot_general(q2, kb, NT, preferred_element_type=f32) * scale + bias2
                s = jnp.where(dead, NEG_INF, s)
                m = jnp.max(s, axis=-1, keepdims=True)
                p = jnp.exp(s - m)
                l = jnp.sum(p, axis=-1, keepdims=True)
                o2 = jnp.dot(p.astype(bf16), vb, preferred_element_type=f32) / l
                lse = m + jnp.log(l)
                o_ref[rows, cols] = jnp.where(low, o2[:ATTN_BLOCK], o2[ATTN_BLOCK:])
                l_ref[rows, cols] = jnp.where(low, lse[:ATTN_BLOCK], lse[ATTN_BLOCK:])

    tile = pl.BlockSpec((ATTN_TILE, ATTN_WIDE), lambda h, t: (t, h))
    return pl.pallas_call(
        body,
        grid=(4 // ATTN_HP, nt),
        in_specs=_qkv_specs(nt) + [pl.BlockSpec((2 * ATTN_HP, ATTN_BLOCK, 2 * ATTN_BLOCK), lambda h, t: (h, 0, 0))]
        + ([] if after is None else [pl.BlockSpec(memory_space=pl.ANY)]),
        out_specs=[tile, tile],
        out_shape=[SDS((S, ATTN_OUT), f32), SDS((S, ATTN_OUT), f32)],
        compiler_params=_cparams(("parallel", "parallel")),
        name=name,
    )(qkv, qkv, qkv, qkv, qkv, bias, *([] if after is None else [after]))


def _attn_bwd(qkv, bias, do, dvec, lse, bps, name):
    S = qkv.shape[0]
    nt = S // ATTN_TILE
    scale = HEAD_DIM ** -0.5

    def assemble(parts):
        rows = [parts[0][:ATTN_BLOCK]]
        for b in range(ATTN_SUB - 1):
            rows.append(parts[b][ATTN_BLOCK:] + parts[b + 1][:ATTN_BLOCK])
        rows.append(parts[-1][ATTN_BLOCK:])
        return rows

    def body(q_ref, kp_ref, kc_ref, vp_ref, vc_ref, b_ref, do_ref, dvec_ref, lse_ref,
             dq_ref, dk_ref, dv_ref, db_ref, ck, cv):
        t = pl.program_id(1)
        last = ATTN_TILE - ATTN_BLOCK

        @pl.when(t == 0)
        def _():
            ck[...] = jnp.zeros_like(ck)
            cv[...] = jnp.zeros_like(cv)
            db_ref[...] = jnp.zeros_like(db_ref)

        @pl.when(t < nt)
        def _():
            low = _head_masks()
            col = lax.broadcasted_iota(jnp.int32, (2 * ATTN_BLOCK, 2 * ATTN_BLOCK), 1)
            per_row = lambda t2: jnp.concatenate([t2[:, 0:1], t2[:, HEAD_DIM:HEAD_DIM + 1]], axis=0)
            for hp in range(ATTN_HP):
                cols = slice(hp * LANE, (hp + 1) * LANE)
                kk = jnp.concatenate([kp_ref[:, cols], kc_ref[:, cols]], axis=0)
                vv = jnp.concatenate([vp_ref[:, cols], vc_ref[:, cols]], axis=0)
                bias2 = b_ref[2 * hp:2 * hp + 2].reshape(2 * ATTN_BLOCK, 2 * ATTN_BLOCK)
                dk_parts, dv_parts = [], []
                dsum = None
                for b in range(ATTN_SUB):
                    lo = b * ATTN_BLOCK
                    rows = slice(lo, lo + ATTN_BLOCK)
                    keys = slice(lo, lo + 2 * ATTN_BLOCK)
                    dead = jnp.logical_and((t * ATTN_SUB + b) % bps == 0, col < ATTN_BLOCK)
                    q2 = _stack_heads(q_ref[rows, cols], low)
                    do2 = _stack_heads(do_ref[rows, cols].astype(bf16), low)
                    kb, vb = kk[keys], vv[keys]
                    s = lax.dot_general(q2, kb, NT, preferred_element_type=f32) * scale + bias2
                    s = jnp.where(dead, NEG_INF, s)
                    p = jnp.exp(s - per_row(lse_ref[rows, cols]))
                    dp = lax.dot_general(do2, vb, NT, preferred_element_type=f32)
                    ds = p * (dp - per_row(dvec_ref[rows, cols]))
                    dsum = ds if dsum is None else dsum + ds
                    dsb = ds.astype(bf16)
                    dq2 = jnp.dot(dsb, kb, preferred_element_type=f32) * scale
                    dq_ref[rows, cols] = jnp.where(low, dq2[:ATTN_BLOCK], dq2[ATTN_BLOCK:]).astype(bf16)
                    dk_parts.append(lax.dot_general(dsb, q2, TN, preferred_element_type=f32) * scale)
                    dv_parts.append(lax.dot_general(p.astype(bf16), do2, TN, preferred_element_type=f32))
                db_ref[2 * hp:2 * hp + 2] += dsum.reshape(2, ATTN_BLOCK, 2 * ATTN_BLOCK)
                for parts, carry, out_ref in ((dk_parts, ck, dk_ref), (dv_parts, cv, dv_ref)):
                    rws = assemble(parts)
                    out_ref[:last, cols] = carry[:last, cols].astype(bf16)
                    out_ref[last:, cols] = (carry[last:, cols] + rws[0]).astype(bf16)
                    for b in range(ATTN_SUB):
                        carry[b * ATTN_BLOCK:(b + 1) * ATTN_BLOCK, cols] = rws[b + 1]

        @pl.when(t == nt)
        def _():
            dk_ref[...] = ck[...].astype(bf16)
            dv_ref[...] = cv[...].astype(bf16)

    tile = (ATTN_TILE, ATTN_WIDE)
    cur = pl.BlockSpec(tile, lambda h, t: (jnp.minimum(t, nt - 1), h))
    lag = pl.BlockSpec(tile, lambda h, t: (jnp.maximum(t - 1, 0), h))
    bspec = pl.BlockSpec((2 * ATTN_HP, ATTN_BLOCK, 2 * ATTN_BLOCK), lambda h, t: (h, 0, 0))
    return pl.pallas_call(
        body,
        grid=(4 // ATTN_HP, nt + 1),
        in_specs=_qkv_specs(nt) + [bspec, cur, cur, cur],
        out_specs=[cur, lag, lag, bspec],
        out_shape=[SDS((S, ATTN_OUT), bf16), SDS((S, ATTN_OUT), bf16), SDS((S, ATTN_OUT), bf16),
                   SDS((8, ATTN_BLOCK, 2 * ATTN_BLOCK), f32)],
        scratch_shapes=[pltpu.VMEM(tile, f32), pltpu.VMEM(tile, f32)],
        compiler_params=_cparams(("parallel", "arbitrary")),
        name=name,
    )(qkv, qkv, qkv, qkv, qkv, bias, do, dvec, lse)


def _attn_merge(o0, o1, o2, l0, l1, l2):
    S, W = o0.shape
    R = PERM_ROWS

    def body(o0_ref, o1_ref, o2_ref, l0_ref, l1_ref, l2_ref, y_ref, yb_ref, w0_ref, w1_ref, w2_ref,
             so1, so2, sl1, sl2):
        _to_natural(o1_ref, so1, 4)
        _to_natural(l1_ref, sl1, 4)
        _to_natural(o2_ref, so2, 16)
        _to_natural(l2_ref, sl2, 16)
        a, b, c = l0_ref[...], sl1[...], sl2[...]
        m = jnp.maximum(jnp.maximum(a, b), c)
        ea, eb, ec = jnp.exp(a - m), jnp.exp(b - m), jnp.exp(c - m)
        den = (ea + eb) + ec
        w0, w1, w2 = ea / den, eb / den, ec / den
        y = (w0 * o0_ref[...] + w1 * so1[...]) + w2 * so2[...]
        y_ref[...] = y
        yb_ref[...] = y.astype(bf16)
        w0_ref[...] = w0
        w1_ref[...] = w1
        w2_ref[...] = w2

    nat = pl.BlockSpec((R, LANE), lambda i, j: (i, j))
    v4 = lambda t: t.reshape(4, S // 4, W)
    v16 = lambda t: t.reshape(16, S // 16, W)
    return pl.pallas_call(
        body,
        grid=(S // R, W // LANE),
        in_specs=[nat, _perm_spec(4), _perm_spec(16)] * 2,
        out_specs=[nat] * 5,
        out_shape=[SDS((S, W), f32), SDS((S, W), bf16)] + [SDS((S, W), f32)] * 3,
        scratch_shapes=[pltpu.VMEM((R, LANE), f32)] * 4,
        compiler_params=_cparams(("parallel", "parallel"), VMEM_BIG),
        name="attn_merge",
    )(o0, v4(o1), v16(o2), l0, v4(l1), v16(l2))


def _attn_merge_bwd(dy, y, w0, w1, w2, after=None):
    S, W = dy.shape
    R = PERM_ROWS

    def body(dy_ref, y_ref, w0_ref, w1_ref, w2_ref, *rest):
        a0, a1, a2, b0, b1, b2, sa, sb = rest[-8:]
        dyv = dy_ref[...]
        r = lax.broadcasted_iota(jnp.int32, (LANE, LANE), 0) // HEAD_DIM
        c = lax.broadcasted_iota(jnp.int32, (LANE, LANE), 1) // HEAD_DIM
        seg = jnp.where(r == c, 1.0, 0.0).astype(f32)
        cbar = jnp.dot(dyv * y_ref[...], seg, precision=HIGHEST, preferred_element_type=f32)
        w = w0_ref[...]
        a0[...] = (w * dyv).astype(bf16)
        b0[...] = w * cbar
        for d, w_ref, a_ref, b_ref in ((4, w1_ref, a1, b1), (16, w2_ref, a2, b2)):
            w = w_ref[...]
            sa[...] = w * dyv
            sb[...] = w * cbar
            n = R // d
            for k in range(d):
                rows = pl.ds(k, n, stride=d)
                a_ref[k] = sa[rows, :].astype(bf16)
                b_ref[k] = sb[rows, :]

    nat = pl.BlockSpec((R, LANE), lambda i, j: (i, j))
    shapes = lambda dt: [SDS((S, W), dt), SDS((4, S // 4, W), dt), SDS((16, S // 16, W), dt)]
    outs = pl.pallas_call(
        body,
        grid=(S // R, W // LANE),
        in_specs=[nat] * 5 + ([] if after is None else [pl.BlockSpec(memory_space=pl.ANY)]),
        out_specs=[nat, _perm_spec(4), _perm_spec(16)] * 2,
        out_shape=shapes(bf16) + shapes(f32),
        scratch_shapes=[pltpu.VMEM((R, LANE), f32)] * 2,
        compiler_params=_cparams(("parallel", "parallel"), VMEM_BIG),
        name="attn_merge_bwd",
    )(dy, y, w0, w1, w2, *([] if after is None else [after]))
    return [t.reshape(S, W) for t in outs]


HGRN_SB = 256
HGRN_PAIR = 4


def _chunk_masks():
    r = jnp.arange(HGRN_SB)[:, None]
    c = jnp.arange(HGRN_SB)[None, :]
    same = (r // HGRN_CHUNK) == (c // HGRN_CHUNK)
    return jnp.stack([same & (c <= r), same, same & (c >= r)]).astype(bf16)


def _mask_dot(mask, x):
    hi = x.astype(bf16)
    r1 = x - hi.astype(f32)
    mid = r1.astype(bf16)
    lo = (r1 - mid.astype(f32)).astype(bf16)
    p = jnp.dot(mask, jnp.concatenate([hi, mid, lo], axis=1), preferred_element_type=f32)
    n = x.shape[1]
    return (p[:, :n] + p[:, n:2 * n]) + p[:, 2 * n:]


def _hgrn_prep(q_raw, f_raw, lbv, tril, same):
    sq = _sigmoid(q_raw)
    qs = q_raw * sq
    sig = _sigmoid(f_raw)
    f = lbv + (1.0 - lbv) * sig
    g = jnp.log(f)
    k = 1.0 - f
    G = _mask_dot(tril, g)
    GL = _mask_dot(same, g)
    eG = jnp.exp(G)
    einv = jnp.exp(-G)
    edec = jnp.exp(GL - G)
    return dict(sq=sq, qs=qs, sig=sig, f=f, k=k, eG=eG, einv=einv, edec=edec, eGL=jnp.exp(GL),
                qt=qs * eG, kt=k * einv, kd=k * edec)


def _hgrn_fwd(hg, lb, normw):
    S = hg.shape[0]
    sb = HGRN_SB
    nsb = S // sb
    nch = sb // HGRN_CHUNK

    def body(q_ref, f_ref, v_ref, og_ref, lb_ref, nw_ref, m_ref, y_ref, o_ref, ck_ref, st):
        j = pl.program_id(1)

        @pl.when(j == 0)
        def _():
            st[...] = jnp.zeros_like(st)

        tril_m = m_ref[0]
        tril = tril_m.astype(f32) > 0.5

        def one_head(hh):
            cols = slice(hh * LANE, (hh + 1) * LANE)
            ST = st[hh]
            ck_ref[hh, 0] = ST
            pr = _hgrn_prep(q_ref[:, cols], f_ref[:, cols], lb_ref[:, cols], tril_m, m_ref[1])
            qtb, ktb, kdb = pr["qt"].astype(bf16), pr["kt"].astype(bf16), pr["kd"].astype(bf16)
            eGL = pr["eGL"]
            vb = v_ref[:, cols].astype(bf16)
            A = jnp.where(tril, lax.dot_general(qtb, ktb, NT, preferred_element_type=f32), 0.0)
            o = jnp.dot(A.astype(bf16), vb, preferred_element_type=f32)
            outs = []
            for ci in range(nch):
                lo = ci * HGRN_CHUNK
                sl = slice(lo, lo + HGRN_CHUNK)
                outs.append(o[sl] + lax.dot_general(qtb[sl], ST.astype(bf16), NT, preferred_element_type=f32))
                ST = ST * eGL[lo:lo + 1, :] + lax.dot_general(vb[sl], kdb[sl], TN, preferred_element_type=f32)
            st[hh] = ST
            of = jnp.concatenate(outs, axis=0)
            o_ref[:, cols] = of
            rms = lax.rsqrt(jnp.mean(of * of, axis=-1, keepdims=True) + EPS)
            ogv = og_ref[:, cols]
            y_ref[:, cols] = ((of * rms * nw_ref[...]) * (ogv * _sigmoid(ogv))).astype(bf16)

        for hh in range(HGRN_PAIR):
            one_head(hh)

    wide = HGRN_PAIR * LANE
    col = lambda off: pl.BlockSpec((sb, wide), lambda h, j: (j, off // HGRN_PAIR + h))
    return pl.pallas_call(
        body,
        grid=(4 // HGRN_PAIR, nsb),
        in_specs=[col(0), col(4), col(8), col(12), pl.BlockSpec((1, wide), lambda h, j: (0, h)),
                  pl.BlockSpec((1, LANE), lambda h, j: (0, 0)),
                  pl.BlockSpec((3, sb, sb), lambda h, j: (0, 0, 0))],
        out_specs=[col(0), col(0), pl.BlockSpec((HGRN_PAIR, 1, LANE, LANE), lambda h, j: (h, j, 0, 0))],
        out_shape=[SDS((S, HGRN_W), bf16), SDS((S, HGRN_W), f32), SDS((4, nsb, LANE, LANE), f32)],
        scratch_shapes=[pltpu.VMEM((HGRN_PAIR, LANE, LANE), f32)],
        compiler_params=_cparams(("parallel", "arbitrary")),
        name="hgrn_fwd",
    )(hg, hg, hg, hg, lb, normw, _chunk_masks())


def _hgrn_bwd(hg, o_raw, dy, ck, lb, normw, after=None):
    S = hg.shape[0]
    sb = HGRN_SB
    nsb = S // sb
    nch = sb // HGRN_CHUNK

    def body(q_ref, f_ref, v_ref, og_ref, o_ref, dy_ref, ck_ref, lb_ref, nw_ref, m_ref, *rest):
        dq_ref, df_ref, dv_ref, dog_ref, glb_ref, gnw_ref, dst, alb, anw = rest[-9:]
        j = pl.program_id(1)

        @pl.when(j == 0)
        def _():
            dst[...] = jnp.zeros_like(dst)
            alb[...] = jnp.zeros_like(alb)
            anw[...] = jnp.zeros_like(anw)

        tril_m = m_ref[0]
        tril = tril_m.astype(f32) > 0.5
        nw = nw_ref[...]

        def one_head(hh):
            cols = slice(hh * LANE, (hh + 1) * LANE)
            lbv = lb_ref[:, cols]
            q_raw = q_ref[:, cols]
            pr = _hgrn_prep(q_raw, f_ref[:, cols], lbv, tril_m, m_ref[1])
            qt, kt, kd, eGL = pr["qt"], pr["kt"], pr["kd"], pr["eGL"]
            qtb, ktb, kdb = qt.astype(bf16), kt.astype(bf16), kd.astype(bf16)
            vb = v_ref[:, cols].astype(bf16)

            o = o_ref[:, cols]
            ogv = og_ref[:, cols]
            sog = _sigmoid(ogv)
            rms = lax.rsqrt(jnp.mean(o * o, axis=-1, keepdims=True) + EPS)
            oh = o * rms
            dyv = dy_ref[:, cols]
            dog_ref[:, cols] = (dyv * (oh * nw) * (sog * (1.0 + ogv * (1.0 - sog)))).astype(bf16)
            dohw = dyv * (ogv * sog)
            anw[:, cols] += _colsum8(dohw * oh)
            doh = dohw * nw
            do = rms * (doh - oh * jnp.mean(doh * oh, axis=-1, keepdims=True))
            dob = do.astype(bf16)

            Ab = jnp.where(tril, lax.dot_general(qtb, ktb, NT, preferred_element_type=f32), 0.0).astype(bf16)
            dAb = jnp.where(tril, lax.dot_general(dob, vb, NT, preferred_element_type=f32), 0.0).astype(bf16)
            dv_acc = lax.dot_general(Ab, dob, TN, preferred_element_type=f32)
            dqt = jnp.dot(dAb, ktb, preferred_element_type=f32)
            dkt = lax.dot_general(dAb, qtb, TN, preferred_element_type=f32)

            ST = ck_ref[hh, 0]
            states = []
            for ci in range(nch):
                lo = ci * HGRN_CHUNK
                sl = slice(lo, lo + HGRN_CHUNK)
                states.append(ST)
                ST = ST * eGL[lo:lo + 1, :] + lax.dot_general(vb[sl], kdb[sl], TN, preferred_element_type=f32)

            dST = dst[hh]
            dqt_i, dkd_i, dv_i, deg_i = [None] * nch, [None] * nch, [None] * nch, [None] * nch
            for ci in reversed(range(nch)):
                lo = ci * HGRN_CHUNK
                sl = slice(lo, lo + HGRN_CHUNK)
                ST0 = states[ci]
                dSTb = dST.astype(bf16)
                dv_i[ci] = lax.dot_general(kdb[sl], dSTb, NT, preferred_element_type=f32)
                dqt_i[ci] = jnp.dot(dob[sl], ST0.astype(bf16), preferred_element_type=f32)
                dkd_i[ci] = jnp.dot(vb[sl], dSTb, preferred_element_type=f32)
                deg_i[ci] = jnp.broadcast_to(jnp.sum(dST * ST0, axis=0, keepdims=True), (HGRN_CHUNK, LANE))
                dST = dST * eGL[lo:lo + 1, :] + lax.dot_general(dob[sl], qtb[sl], TN, preferred_element_type=f32)
            dst[hh] = dST

            dqt = dqt + jnp.concatenate(dqt_i, axis=0)
            dkd = jnp.concatenate(dkd_i, axis=0)
            dv_ref[:, cols] = (dv_acc + jnp.concatenate(dv_i, axis=0)).astype(bf16)
            deg = jnp.concatenate(deg_i, axis=0)

            dqs = dqt * pr["eG"]
            dkdkd = dkd * kd
            dG = dqt * qt - dkt * kt - dkdkd
            dk = dkt * pr["einv"] + dkd * pr["edec"]
            dGL = _mask_dot(m_ref[1], dkdkd) + eGL * deg
            dg = _mask_dot(m_ref[2], dG) + dGL
            df = dg / pr["f"] - dk
            sig = pr["sig"]
            df_ref[:, cols] = (df * (1.0 - lbv) * (sig * (1.0 - sig))).astype(bf16)
            alb[:, cols] += _colsum8(df * (1.0 - sig))
            sq = pr["sq"]
            dq_ref[:, cols] = (dqs * (sq * (1.0 + q_raw * (1.0 - sq)))).astype(bf16)

        for hh in range(HGRN_PAIR):
            one_head(hh)

        @pl.when(j == nsb - 1)
        def _():
            glb_ref[...] = jnp.broadcast_to(jnp.sum(alb[...], axis=0, keepdims=True), (SUBLANE, wide))
            gnw_ref[...] = jnp.broadcast_to(jnp.sum(anw[...], axis=0, keepdims=True), (SUBLANE, wide))

    wide = HGRN_PAIR * LANE
    rev = lambda off: pl.BlockSpec((sb, wide), lambda h, j: (nsb - 1 - j, off // HGRN_PAIR + h))
    stat = pl.BlockSpec((SUBLANE, wide), lambda h, j: (0, h))
    return pl.pallas_call(
        body,
        grid=(4 // HGRN_PAIR, nsb),
        in_specs=[rev(0), rev(4), rev(8), rev(12), rev(0), rev(0),
                  pl.BlockSpec((HGRN_PAIR, 1, LANE, LANE), lambda h, j: (h, nsb - 1 - j, 0, 0)),
                  pl.BlockSpec((1, wide), lambda h, j: (0, h)), pl.BlockSpec((1, LANE), lambda h, j: (0, 0)),
                  pl.BlockSpec((3, sb, sb), lambda h, j: (0, 0, 0))]
        + ([] if after is None else [pl.BlockSpec(memory_space=pl.ANY)]),
        out_specs=[rev(0), rev(0), rev(0), rev(0), stat, stat],
        out_shape=[SDS((S, HGRN_W), bf16)] * 4 + [SDS((SUBLANE, HGRN_W), f32)] * 2,
        scratch_shapes=[pltpu.VMEM((HGRN_PAIR, LANE, LANE), f32), pltpu.VMEM((SUBLANE, wide), f32),
                        pltpu.VMEM((SUBLANE, wide), f32)],
        compiler_params=_cparams(("parallel", "arbitrary")),
        name="hgrn_bwd",
    )(hg, hg, hg, hg, o_raw, dy, ck, lb, normw, _chunk_masks(), *([] if after is None else [after]))


def _lb_fwd(raw):
    def body(r_ref, o_ref):
        r = r_ref[...]
        m = jnp.max(r, axis=0, keepdims=True)
        e = jnp.exp(r - m)
        o_ref[...] = (e / jnp.sum(e, axis=0, keepdims=True))[0:1]

    return pl.pallas_call(body, out_shape=SDS((1, raw.shape[1]), f32), name="lb_fwd")(raw)


def _lb_bwd(raw, dlb):
    def body(r_ref, d_ref, o_ref):
        r = r_ref[...]
        m = jnp.max(r, axis=0, keepdims=True)
        e = jnp.exp(r - m)
        s = e / jnp.sum(e, axis=0, keepdims=True)
        s0 = s[0:1]
        onehot0 = jnp.where(lax.broadcasted_iota(jnp.int32, r.shape, 0) == 0, 1.0, 0.0)
        o_ref[...] = d_ref[...] * s0 * (onehot0 - s)

    return pl.pallas_call(body, out_shape=SDS(raw.shape, f32), name="lb_bwd")(raw, dlb)


def _gate_fwd(ya, yh, w_ba, w_bh, gc):
    S = ya.shape[0]
    D = w_ba.shape[1]
    tm = _pick(S, MM_ROWS)

    def body(ya_ref, yh_ref, wa_ref, wh_ref, g0_ref, g1_ref, a_ref, b_ref, o_ref):
        a = jnp.dot(ya_ref[...], wa_ref[...], preferred_element_type=f32).astype(bf16)
        b = jnp.dot(yh_ref[...], wh_ref[...], preferred_element_type=f32).astype(bf16)
        a_ref[...] = a
        b_ref[...] = b
        s0, s1 = _sigmoid(g0_ref[...].astype(f32)), _sigmoid(g1_ref[...].astype(f32))
        o_ref[...] = (s0 * a.astype(f32) + s1 * b.astype(f32)).astype(bf16)

    row = pl.BlockSpec((tm, D), lambda i: (i, 0))
    act = pl.BlockSpec((tm, ya.shape[1]), lambda i: (i, 0))
    wspec = pl.BlockSpec(w_ba.shape, lambda i: (0, 0))
    return pl.pallas_call(
        body,
        grid=(S // tm,),
        in_specs=[act, act, wspec, wspec, row, pl.BlockSpec((tm, D), lambda i: (i, 1))],
        out_specs=[row, row, row],
        out_shape=[SDS((S, D), bf16)] * 3,
        compiler_params=_cparams(("parallel",), VMEM_BIG),
        name="branch_gate_fwd",
    )(ya, yh, w_ba, w_bh, gc, gc)


def _gate_bwd(dmo, w_out, a, b, gc, w_ba, w_bh):
    S, D = a.shape
    W = w_ba.shape[0]
    tm = _pick(S, MM_ROWS)

    def body(dmo_ref, wo_ref, a_ref, b_ref, g0_ref, g1_ref, wa_ref, wh_ref,
             da_ref, db_ref, dg_ref, dya_ref, dyh_ref):
        dm = lax.dot_general(dmo_ref[...], wo_ref[...], NT, preferred_element_type=f32)
        dmv = dm.astype(bf16).astype(f32)
        s0, s1 = _sigmoid(g0_ref[...].astype(f32)), _sigmoid(g1_ref[...].astype(f32))
        da = (dmv * s0).astype(bf16)
        db = (dmv * s1).astype(bf16)
        da_ref[...] = da
        db_ref[...] = db
        dg_ref[:, :D] = (dmv * a_ref[...].astype(f32) * (s0 * (1.0 - s0))).astype(bf16)
        dg_ref[:, D:] = (dmv * b_ref[...].astype(f32) * (s1 * (1.0 - s1))).astype(bf16)
        dya_ref[...] = lax.dot_general(da, wa_ref[...], NT, preferred_element_type=f32)
        dyh_ref[...] = lax.dot_general(db, wh_ref[...], NT, preferred_element_type=f32)

    row = pl.BlockSpec((tm, D), lambda i: (i, 0))
    wide = pl.BlockSpec((tm, 2 * D), lambda i: (i, 0))
    narrow = pl.BlockSpec((tm, W), lambda i: (i, 0))
    whole = lambda t: pl.BlockSpec(t.shape, lambda i: (0, 0))
    return pl.pallas_call(
        body,
        grid=(S // tm,),
        in_specs=[row, whole(w_out), row, row, row, pl.BlockSpec((tm, D), lambda i: (i, 1)), whole(w_ba), whole(w_bh)],
        out_specs=[row, row, wide, narrow, narrow],
        out_shape=[SDS((S, D), bf16), SDS((S, D), bf16), SDS((S, 2 * D), bf16), SDS((S, W), f32), SDS((S, W), f32)],
        compiler_params=_cparams(("parallel",), VMEM_BIG),
        name="gate_bwd_fused",
    )(dmo, w_out, a, b, gc, gc, w_ba, w_bh)


CONV_ROWS = 512
INV_SQRT2 = 0.7071067811865476
INV_SQRT_2PI = 0.3989422804014327


CONV_HALO = 16


def _shift_down(cur, prev, k):
    x = pltpu.roll(cur, k, 0)
    row = lax.broadcasted_iota(jnp.int32, (SUBLANE, LANE), 0)
    head = jnp.where(row < k, pltpu.roll(prev, k, 0)[:SUBLANE], x[:SUBLANE])
    return jnp.concatenate([head, x[SUBLANE:]], axis=0)


def _shift_up(cur, nxt, k):
    R = cur.shape[0]
    x = pltpu.roll(cur, R - k, 0)
    row = lax.broadcasted_iota(jnp.int32, (SUBLANE, LANE), 0)
    tail = jnp.where(row >= SUBLANE - k, pltpu.roll(nxt, SUBLANE - k, 0), x[R - SUBLANE:])
    return jnp.concatenate([x[:R - SUBLANE], tail], axis=0)


def _conv_rows(u_ref, w, b, r0, first):
    R = CONV_ROWS
    cur = u_ref[pl.ds(r0, R), :].astype(f32)
    prev = u_ref[pl.ds(pl.multiple_of(jnp.maximum(r0 - CONV_HALO, 0), CONV_HALO), CONV_HALO), :].astype(f32)
    prev = jnp.where(first, 0.0, prev)
    x1 = _shift_down(cur, prev, 1)
    x2 = _shift_down(cur, prev, 2)
    c = ((b + w[0:1] * x2) + w[1:2] * x1) + w[2:3] * cur
    return c, x2, x1, cur


def _conv_fwd(ug, uv, wg, wv, bg, bv):
    S, F = ug.shape
    nchunk = S // CONV_ROWS

    def body(ug_ref, uv_ref, wg_ref, wv_ref, bg_ref, bv_ref, o_ref):
        wgv, wvv, bgv, bvv = wg_ref[...], wv_ref[...], bg_ref[...], bv_ref[...]

        def step(ci, carry):
            r0 = pl.multiple_of(ci * CONV_ROWS, CONV_ROWS)
            cg = _conv_rows(ug_ref, wgv, bgv, r0, ci == 0)[0]
            cv = _conv_rows(uv_ref, wvv, bvv, r0, ci == 0)[0]
            gelu = 0.5 * cg * (1.0 + lax.erf(cg * INV_SQRT2))
            o_ref[pl.ds(r0, CONV_ROWS), :] = (gelu * cv).astype(bf16)
            return carry

        lax.fori_loop(0, nchunk, step, 0)

    col = pl.BlockSpec((S, LANE), lambda j: (0, j))
    w3 = pl.BlockSpec((3, LANE), lambda j: (0, j))
    b1 = pl.BlockSpec((1, LANE), lambda j: (0, j))
    return pl.pallas_call(
        body,
        grid=(F // LANE,),
        in_specs=[col, col, w3, w3, b1, b1],
        out_specs=col,
        out_shape=SDS((S, F), bf16),
        compiler_params=_cparams(("parallel",), VMEM_BIG),
        name="conv_fwd",
    )(ug, uv, wg, wv, bg, bv)


def _conv_bwd(ug, uv, dact, wg, wv, bg, bv):
    S, F = ug.shape
    R = CONV_ROWS
    nchunk = S // R

    def body(ug_ref, uv_ref, da_ref, wg_ref, wv_ref, bg_ref, bv_ref, dug_ref, duv_ref, sg_ref, sv_ref, dcg, dcv):
        wgv, wvv, bgv, bvv = wg_ref[...], wv_ref[...], bg_ref[...], bv_ref[...]
        zero = jnp.zeros((SUBLANE, LANE), f32)

        def fwd_step(ci, acc):
            r0 = pl.multiple_of(ci * R, R)
            cg, g2, g1, g0 = _conv_rows(ug_ref, wgv, bgv, r0, ci == 0)
            cv, v2, v1, v0 = _conv_rows(uv_ref, wvv, bvv, r0, ci == 0)
            da = da_ref[pl.ds(r0, R), :].astype(f32)
            cdf = 0.5 * (1.0 + lax.erf(cg * INV_SQRT2))
            pdf = INV_SQRT_2PI * jnp.exp(-0.5 * cg * cg)
            dg = da * cv * (cdf + cg * pdf)
            dv = da * (cg * cdf)
            dcg[pl.ds(r0, R), :] = dg
            dcv[pl.ds(r0, R), :] = dv
            new = (acc[0] + _colsum8(dg * g2), acc[1] + _colsum8(dg * g1), acc[2] + _colsum8(dg * g0),
                   acc[3] + _colsum8(dg),
                   acc[4] + _colsum8(dv * v2), acc[5] + _colsum8(dv * v1), acc[6] + _colsum8(dv * v0),
                   acc[7] + _colsum8(dv))
            return new

        acc = lax.fori_loop(0, nchunk, fwd_step, (zero,) * 8)
        rows = lax.broadcasted_iota(jnp.int32, (SUBLANE, LANE), 0)

        def stats(parts):
            out = jnp.zeros((SUBLANE, LANE), f32)
            for k, pt in enumerate(parts):
                out = jnp.where(rows == k, jnp.sum(pt, axis=0, keepdims=True), out)
            return out

        sg_ref[...] = stats(acc[0:4])
        sv_ref[...] = stats(acc[4:8])

        def du_rows(dc, w, r0, last):
            cur = dc[pl.ds(r0, R), :]
            nxt = dc[pl.ds(pl.multiple_of(jnp.minimum(r0 + R, S - SUBLANE), SUBLANE), SUBLANE), :]
            nxt = jnp.where(last, 0.0, nxt)
            return w[2:3] * cur + w[1:2] * _shift_up(cur, nxt, 1) + w[0:1] * _shift_up(cur, nxt, 2)

        def bwd_step(ci, carry):
            r0 = pl.multiple_of(ci * R, R)
            last = ci == nchunk - 1
            dug_ref[pl.ds(r0, R), :] = du_rows(dcg, wgv, r0, last).astype(bf16)
            duv_ref[pl.ds(r0, R), :] = du_rows(dcv, wvv, r0, last).astype(bf16)
            return carry

        lax.fori_loop(0, nchunk, bwd_step, 0)

    col = pl.BlockSpec((S, LANE), lambda j: (0, j))
    w3 = pl.BlockSpec((3, LANE), lambda j: (0, j))
    b1 = pl.BlockSpec((1, LANE), lambda j: (0, j))
    st = pl.BlockSpec((SUBLANE, LANE), lambda j: (0, j))
    return pl.pallas_call(
        body,
        grid=(F // LANE,),
        in_specs=[col, col, col, w3, w3, b1, b1],
        out_specs=[col, col, st, st],
        out_shape=[SDS((S, F), bf16), SDS((S, F), bf16), SDS((SUBLANE, F), f32), SDS((SUBLANE, F), f32)],
        scratch_shapes=[pltpu.VMEM((S, LANE), f32), pltpu.VMEM((S, LANE), f32)],
        compiler_params=_cparams(("parallel",), VMEM_BIG),
        name="conv_bwd",
    )(ug, uv, dact, wg, wv, bg, bv)


def _adam_math(w, g, m, v):
    m = ADAM_B1 * m + (1.0 - ADAM_B1) * g
    v = ADAM_B2 * v + (1.0 - ADAM_B2) * (g * g)
    m_hat = m / (1.0 - ADAM_B1 ** ADAM_STEP)
    v_hat = v / (1.0 - ADAM_B2 ** ADAM_STEP)
    delta = -ADAM_LR * (m_hat / (jnp.sqrt(v_hat) + ADAM_EPS) + ADAM_WD * w)
    return delta, m, v


def _adamw(w, m, v, g, name):
    R, C = w.shape
    parts = g.ndim == 3
    tr = R
    if R % 16 == 0:
        for t in range(R, 0, -16):
            if R % t == 0 and t * C * 4 <= ADAM_BLOCK_BYTES:
                tr = t
                break

    def body(w_ref, m_ref, v_ref, g_ref, go_ref, d_ref, mo_ref, vo_ref):
        if parts:
            gv = ((g_ref[0].astype(f32) + g_ref[1].astype(f32)) + g_ref[2].astype(f32)) + g_ref[3].astype(f32)
        else:
            gv = g_ref[...]
        go_ref[...] = gv
        d, mn, vn = _adam_math(w_ref[...], gv, m_ref[...], v_ref[...])
        d_ref[...] = d
        mo_ref[...] = mn
        vo_ref[...] = vn

    row = pl.BlockSpec((tr, C), lambda i: (i, 0))
    gspec = pl.BlockSpec((4, tr, C), lambda i: (0, i, 0)) if parts else row
    return pl.pallas_call(
        body,
        grid=(R // tr,),
        in_specs=[row, row, row, gspec],
        out_specs=[row] * 4,
        out_shape=[SDS((R, C), f32)] * 4,
        compiler_params=_cparams(("parallel",), VMEM_BIG),
        name=name,
    )(w, m, v, g)


def _sum8(parts, name):
    _, _, R, C = parts.shape

    def body(p_ref, o_ref):
        acc = p_ref[0, 0]
        for c in range(2):
            for k in range(4):
                if c or k:
                    acc = acc + p_ref[c, k]
        o_ref[...] = acc

    return pl.pallas_call(body, out_shape=SDS((R, C), f32), name=name)(parts)


def _pair_add(by_core, b, name):
    _, K, R, C = by_core.shape
    tr = R // 2 if R % 32 == 0 else R

    def body(c_ref, a_ref, b_ref, o_ref):
        o_ref[...] = (a_ref[0].astype(f32) + b_ref[...].astype(f32)).astype(bf16)

    blk = pl.BlockSpec((1, tr, C), lambda k, i, c: (k, i, 0))
    return pl.pallas_call(
        body,
        grid_spec=pltpu.PrefetchScalarGridSpec(
            num_scalar_prefetch=1,
            grid=(K, R // tr),
            in_specs=[pl.BlockSpec((1, 1, tr, C), lambda k, i, c: (c[0], k, i, 0)), blk],
            out_specs=blk,
        ),
        out_shape=SDS((K, R, C), bf16),
        compiler_params=_cparams(("parallel", "parallel")),
        name=name,
    )(lax.axis_index("c").astype(jnp.int32).reshape(1), by_core, b)


_ANY = pl.BlockSpec(memory_space=pl.ANY)


def _chip_out_shape(src, gather):
    return SDS((4,) + tuple(src.shape if gather else src.shape[1:]), src.dtype)


def _fill_own(out, src, gather):
    mine = 2 * lax.axis_index("x") + lax.axis_index("y")
    own = src if gather else lax.dynamic_index_in_dim(src, mine, axis=0, keepdims=False)
    return lax.dynamic_update_index_in_dim(out, own, mine, axis=0)


_HBM = pl.BlockSpec(memory_space=pltpu.HBM)
_SEM = pl.BlockSpec(memory_space=pltpu.SEMAPHORE)
_EFFECT = pltpu.SideEffectType.DATAFLOW_SIDE_EFFECTING
_SPLIT_PEERS = {"chip_gather": 3, "chip_gather_wide": 3, "chip_xchg": 3, "core_gather": 1, "core_swap": 1}


def _split_land(src, kind):
    if kind == "chip_gather_wide":
        return SDS((4, 2) + tuple(src.shape), src.dtype)
    if kind == "core_gather":
        return SDS((2,) + tuple(src.shape), src.dtype)
    if kind == "core_swap":
        return SDS(tuple(src.shape[1:]), src.dtype)
    return _chip_out_shape(src, kind == "chip_gather")


def _split_copies(src_ref, land_ref, sems, kind):
    x, y, c = lax.axis_index("x"), lax.axis_index("y"), lax.axis_index("c")
    n = _SPLIT_PEERS[kind]
    if kind == "core_gather":
        routes = [((x, y, 1 - c), src_ref, land_ref.at[c], land_ref.at[1 - c])]
    elif kind == "core_swap":
        routes = [((x, y, 1 - c), src_ref.at[1 - c], land_ref, land_ref)]
    else:
        mine = 2 * x + y
        gather = kind != "chip_xchg"
        slot = (lambda k: land_ref.at[k, c]) if kind == "chip_gather_wide" else (lambda k: land_ref.at[k])
        routes = [((px, py, c), src_ref if gather else src_ref.at[2 * px + py], slot(mine), slot(2 * px + py))
                  for px, py in [(1 - x, y), (x, 1 - y), (1 - x, 1 - y)]]
    sends, recvs = [], []
    for j, (peer, piece, there, here) in enumerate(routes):
        sends.append(pltpu.make_async_remote_copy(src_ref=piece, dst_ref=there, send_sem=sems[j],
                                                  recv_sem=sems[n + j], device_id=peer, device_id_type=MESH))
        recvs.append(pltpu.make_async_remote_copy(src_ref=piece, dst_ref=here, send_sem=sems[j],
                                                  recv_sem=sems[n + j], device_id=peer, device_id_type=MESH))
    return sends, recvs


def _split_start(src, kind, name, after=None):
    land = _split_land(src, kind)
    ns = 2 * _SPLIT_PEERS[kind]
    n_in = 2 if after is None else 3

    def body(*refs):
        src_ref, land_ref = refs[:2]
        outs = refs[n_in:]
        for cp in _split_copies(src_ref, land_ref, outs[:ns], kind)[0]:
            cp.start()
        token = outs[ns + 2]
        token[...] = jnp.zeros_like(token)

    res = pl.pallas_call(
        body,
        name=name,
        out_shape=(pltpu.SemaphoreType.DMA(()),) * ns
        + (pltpu.HBM(src.shape, src.dtype), pltpu.HBM(land.shape, land.dtype), SDS((SUBLANE, LANE), f32)),
        in_specs=(_HBM, _HBM) + (() if after is None else (_ANY,)),
        out_specs=(_SEM,) * ns + (_HBM, _HBM, pl.BlockSpec(memory_space=pltpu.VMEM)),
        input_output_aliases={0: ns, 1: ns + 1},
        compiler_params=pltpu.CompilerParams(has_side_effects=_EFFECT),
    )(pltpu.with_memory_space_constraint(src, pltpu.HBM),
      pltpu.with_memory_space_constraint(lax.empty(land.shape, land.dtype), pltpu.HBM),
      *(() if after is None else (after,)))
    return (res[:ns], res[ns], res[ns + 1]), res[ns + 2]


def _split_wait(state, after, kind, name):
    sems, src_thru, land_thru = state
    ns = 2 * _SPLIT_PEERS[kind]

    def body(src_ref, land_ref, *rest):
        sends, recvs = _split_copies(src_ref, land_ref, rest[:ns], kind)
        for cp in recvs:
            cp.wait_recv()
        for cp in sends:
            cp.wait_send()

    src_out, got = pl.pallas_call(
        body,
        name=name,
        out_shape=(pltpu.HBM(src_thru.shape, src_thru.dtype), pltpu.HBM(land_thru.shape, land_thru.dtype)),
        in_specs=(_HBM, _HBM) + (_SEM,) * ns + (_ANY,),
        out_specs=(_HBM, _HBM),
        input_output_aliases={0: 0, 1: 1},
        compiler_params=pltpu.CompilerParams(has_side_effects=_EFFECT),
    )(src_thru, land_thru, *sems, after)
    if kind == "core_swap":
        return got, src_out
    if kind == "core_gather":
        return lax.dynamic_update_index_in_dim(got, src_out, lax.axis_index("c"), axis=0)
    if kind == "chip_gather_wide":
        mine = 2 * lax.axis_index("x") + lax.axis_index("y")
        zero = jnp.zeros((), mine.dtype)
        return lax.dynamic_update_slice(got, src_out[None, None], (mine, lax.axis_index("c").astype(mine.dtype))
                                        + (zero,) * src_out.ndim)
    return _fill_own(got, src_out, kind == "chip_gather")


def _core_fill(both, name):
    n = both.shape[0]

    def body(in_ref, out_ref, send_sems, recv_sems):
        x, y, c = lax.axis_index("x"), lax.axis_index("y"), lax.axis_index("c")
        sends = [pltpu.make_async_remote_copy(src_ref=out_ref.at[k, c], dst_ref=out_ref.at[k, c],
                                              send_sem=send_sems.at[k], recv_sem=recv_sems.at[k],
                                              device_id=(x, y, 1 - c), device_id_type=MESH) for k in range(n)]
        recvs = [pltpu.make_async_remote_copy(src_ref=out_ref.at[k, c], dst_ref=out_ref.at[k, 1 - c],
                                              send_sem=send_sems.at[k], recv_sem=recv_sems.at[k],
                                              device_id=(x, y, 1 - c), device_id_type=MESH) for k in range(n)]
        for cp in sends:
            cp.start()
        for cp in recvs:
            cp.wait_recv()
        for cp in sends:
            cp.wait_send()

    return pl.pallas_call(
        body,
        in_specs=[_ANY],
        out_specs=_ANY,
        out_shape=SDS(both.shape, both.dtype),
        scratch_shapes=[pltpu.SemaphoreType.DMA((n,)), pltpu.SemaphoreType.DMA((n,))],
        input_output_aliases={0: 0},
        name=name,
    )(both)


def _core_gather(src, name):
    def body(src_ref, out_ref, send_sem, recv_sem):
        x, y, c = lax.axis_index("x"), lax.axis_index("y"), lax.axis_index("c")
        cp = pltpu.make_async_remote_copy(src_ref=src_ref, dst_ref=out_ref.at[c], send_sem=send_sem,
                                          recv_sem=recv_sem, device_id=(x, y, 1 - c), device_id_type=MESH)
        cp.start()
        pltpu.make_async_remote_copy(src_ref=src_ref, dst_ref=out_ref.at[1 - c], send_sem=send_sem,
                                     recv_sem=recv_sem, device_id=(x, y, 1 - c), device_id_type=MESH).wait_recv()
        cp.wait_send()

    out = pl.pallas_call(
        body,
        in_specs=[_ANY],
        out_specs=_ANY,
        out_shape=SDS((2,) + tuple(src.shape), src.dtype),
        scratch_shapes=[pltpu.SemaphoreType.DMA, pltpu.SemaphoreType.DMA],
        name=name,
    )(src)
    return lax.dynamic_update_index_in_dim(out, src, lax.axis_index("c"), axis=0)


_PACK_A = (("w_in", (1088, 1024)),)
_PACK_B = (("w_ba", (512, 128)), ("w_bh", (512, 128)), ("w_out", (128, 1024)), ("w_up", (704, 1024)),
           ("w_down", (352, 1024)))
_PACK_SIZES = _PACK_A + _PACK_B
_TRANSPOSED = ("w_in", "w_up")


def _slab_rows(sizes):
    return sum(r * c for _, (r, c) in sizes) // D_MODEL


def _pack_rows(d, sizes):
    n = d[sizes[0][0]].shape[0]
    return jnp.concatenate([d[k].reshape(n, -1, D_MODEL) for k, _ in sizes], axis=1)


def _unpack_rows(slab, sizes):
    n = slab.shape[0]
    out, lo = {}, 0
    for key, (r, c) in sizes:
        rows = r * c // D_MODEL
        out[key] = slab[:, lo:lo + rows].reshape(n, r, c)
        lo += rows
    return out


def _by_core(gslab):
    return jnp.swapaxes(gslab.reshape((4, 2) + gslab.shape[1:]), 0, 1)


def _cols_to_full(t):
    return jnp.swapaxes(t, 0, 1).reshape(t.shape[1], -1)


def _full_to_cols(t):
    K = t.shape[0]
    return jnp.swapaxes(t.reshape(K, 8, -1), 0, 1)


_SMALL = (("pre_mix_norm", (1, 1024)), ("rel_bias", (32, 24)), ("hgrn_lb_raw", (2, 512)), ("hgrn_norm", (1, 128)),
          ("post_mix_norm", (1, 1024)), ("pre_ffn_norm", (1, 1024)), ("conv_b", (1, 5632)),
          ("post_ffn_norm", (1, 1024)))
_SMALL_ROWS = 96
_CONVW_ROWS = 136


_SMALL_USED = sum(r * c for _, (r, c) in _SMALL)


def _pack_small(d, extra=None):
    flat = jnp.concatenate([d[k].reshape(-1) for k, _ in _SMALL] + ([] if extra is None else [extra.reshape(-1)]))
    flat = jnp.pad(flat, (0, _SMALL_ROWS * LANE - flat.shape[0]))
    return flat.reshape(_SMALL_ROWS, LANE)


def _unpack_small(p):
    flat = p.reshape(-1)
    out, lo = {}, 0
    for k, shp in _SMALL:
        n = shp[0] * shp[1]
        out[k] = flat[lo:lo + n].reshape(shp)
        lo += n
    return out


def _local_step(x, tgt, P, plan):
    S = x.shape[0]
    P = dict(P)
    lb = _lb_fwd(P["hgrn_lb_raw"])
    hs = _prep(x, P["pre_mix_norm"], plan.start_token())
    h1 = hs[0]
    consts = [_bias_consts(d, plan.start_token()) for d in DILATIONS]
    biases, dep = [], h1
    for g in range(N_GROUPS):
        tab_t = P["rel_bias"][:, 8 * g:8 * g + 8].T
        dep = _bias_build(tab_t, consts[g][0], consts[g][1], f"bias_build{g}", dep)
        biases.append(dep.reshape(8, ATTN_BLOCK, 2 * ATTN_BLOCK))
    W = dict(plan.weights_a(dep))
    qkv0, hg, gc = _mm_fanout(h1, [W["wt_qkv"][0], W["wt_hg"], W["wt_gate"]], "nt", [bf16, f32, bf16], "proj_natural")
    qkv = [qkv0] + [_mm(hs[g], W["wt_qkv"][g], "nt", bf16, f"proj_qkv{g}") for g in (1, 2)]
    obuf, lbuf, token = [], [], None
    for g, d in enumerate(DILATIONS):
        o_g, l_g = _attn_fwd(qkv[g], biases[g], (S // d) // ATTN_BLOCK, f"attn_fwd{g}", after=token)
        lbuf.append(l_g)
        obuf.append(o_g)
        if g == 0:
            token = plan.forward_b(o_g)
    y_attn, y_attn_b, w0, w1, w2 = _attn_merge(obuf[0], obuf[1], obuf[2], lbuf[0], lbuf[1], lbuf[2])
    y_hgrn, o_raw, ck = _hgrn_fwd(hg, lb, P["hgrn_norm"])
    wb = plan.weights_b(y_hgrn)
    P["conv_w"] = wb.pop("conv_w")
    W.update(wb)
    a, b, merged = _gate_fwd(y_attn_b, y_hgrn, W["w_ba"], W["w_bh"], gc)
    mo, x1, h2 = _mid_fwd(x, merged, W["w_out"], P["post_mix_norm"], P["pre_ffn_norm"])
    ug, uv = _mm_fanout(h2, [W["wt_up_g"], W["wt_up_v"]], "nt", [bf16, bf16], "up_proj")
    cw_g, cw_v = P["conv_w"][:, :D_FF], P["conv_w"][:, D_FF:]
    cb_g, cb_v = P["conv_b"][:, :D_FF], P["conv_b"][:, D_FF:]
    act = _conv_fwd(ug, uv, cw_g, cw_v, cb_g, cb_v)
    loss, dy, dfo, g_post_ffn = _final(x1, act, W["w_down"], tgt, P["post_ffn_norm"])
    gW_down = _mm(act, dfo, "tn", bf16, "gw_down")
    dact = _mm(dfo, W["w_down"], "nt", bf16, "d_act")
    dug, duv, st_g, st_v = _conv_bwd(ug, uv, dact, cw_g, cw_v, cb_g, cb_v)
    gW_up_g = _mm(dug, h2, "tn", bf16, "gw_up_gate")
    gW_up_v = _mm(duv, h2, "tn", bf16, "gw_up_val")
    dx1, dmo, g_pre_ffn, g_post_mix = _mid_bwd(dy, dug, duv, W["wt_up_g"], W["wt_up_v"], x1, mo, P["pre_ffn_norm"],
                                               P["post_mix_norm"])
    gW_out = _mm(merged, dmo, "tn", bf16, "gw_out")
    da, db, dgc, dyattn, dyhgrn = _gate_bwd(dmo, W["w_out"], a, b, gc, W["w_ba"], W["w_bh"])
    gW_ba = _mm(y_attn_b, da, "tn", bf16, "gw_ba")
    gW_bh = _mm(y_hgrn, db, "tn", bf16, "gw_bh")
    big_b = dict(w_ba=gW_ba, w_bh=gW_bh, w_out=gW_out, w_up=[gW_up_g, gW_up_v], w_down=gW_down)
    dos = _attn_merge_bwd(dyattn, y_attn, w0, w1, w2, after=plan.grads_b_start(big_b))
    dq_h, df_h, dv_h, dog_h, glb8, gnw8 = _hgrn_bwd(hg, o_raw, dyhgrn, ck, lb, P["hgrn_norm"],
                                                   after=plan.grads_b_exchange(dos[5]))
    dhg = [dq_h, df_h, dv_h, dog_h]
    g_lb_raw = _lb_bwd(P["hgrn_lb_raw"], glb8[0:1])
    gn = gnw8[0:1]
    g_hgrn_norm = (gn[:, 0:128] + gn[:, 128:256]) + (gn[:, 256:384] + gn[:, 384:512])
    dqkvs, gW_qkv, g_rel = [], [], []
    for g, d in enumerate(DILATIONS):
        dq, dk, dv, dbias = _attn_bwd(qkv[g], biases[g], dos[g], dos[3 + g], lbuf[g], (S // d) // ATTN_BLOCK,
                                      f"attn_bwd{g}")
        dqkvs.append([dq, dk, dv])
        gW_qkv.append(_mm(dqkvs[g], hs[g], "tn", bf16, f"gw_qkv{g}"))
        g_rel.append(_bias_grad(dbias.reshape(8, -1), consts[g][0], f"bias_grad{g}"))
    gW_hg = _mm(dhg, h1, "tn", bf16, "gw_hg")
    gW_gate = _mm(dgc, h1, "tn", bf16, "gw_gate")
    gW_in = gW_qkv + [gW_hg, gW_gate]
    token = plan.grads_a_start(gW_in)
    dh_perm = [_mm(dqkvs[g], W["wt_qkv"][g], "nn", f32, f"dh1_qkv{g}", after=token) for g in (1, 2)]
    token = plan.grads_a_exchange(dh_perm[1])
    dh_main = _mm(dqkvs[0] + dhg + [dgc], [W["wt_qkv"][0], W["wt_hg"], W["wt_gate"]], "nn", f32, "dh1_main",
                  after=token)
    grad_x, g_pre_mix = _first_bwd(x, dx1, dh_main, dh_perm[0], dh_perm[1], P["pre_mix_norm"])

    g_conv_w = jnp.concatenate([st_g[0:3], st_v[0:3]], axis=1)
    g_conv_b = jnp.concatenate([st_g[3:4], st_v[3:4]], axis=1)
    small = dict(pre_mix_norm=g_pre_mix, rel_bias=jnp.concatenate(g_rel, axis=1), hgrn_lb_raw=g_lb_raw,
                 hgrn_norm=g_hgrn_norm, post_mix_norm=g_post_mix, pre_ffn_norm=g_pre_ffn, conv_b=g_conv_b,
                 post_ffn_norm=g_post_ffn, conv_w=g_conv_w)
    return loss, grad_x, gW_in, big_b, small


def _weights_a(both):
    wt = both.reshape(-1, D_MODEL)
    return dict(
        wt_qkv=[wt[g * QKV_G:(g + 1) * QKV_G] for g in range(N_GROUPS)],
        wt_hg=wt[3 * QKV_G:3 * QKV_G + 4 * HGRN_W],
        wt_gate=wt[3 * QKV_G + 4 * HGRN_W:],
    )


def _weights_b(slabs):
    sh = _unpack_rows(slabs, _PACK_B)
    wt_up = sh["w_up"].reshape(-1, D_MODEL)
    return dict(
        w_ba=_cols_to_full(sh["w_ba"]),
        w_bh=_cols_to_full(sh["w_bh"]),
        w_out=sh["w_out"].reshape(D_MODEL, D_MODEL),
        wt_up_g=wt_up[:D_FF],
        wt_up_v=wt_up[D_FF:],
        w_down=sh["w_down"].reshape(D_FF, D_MODEL),
    )


def _dest_rows(sections, height):
    out = []
    for j in range(8):
        lo, hi, off, pieces = j * height, (j + 1) * height, 0, []
        for s in sections:
            a, b = max(lo, off), min(hi, off + s.shape[0])
            if a < b:
                pieces.append(s[a - off:b - off])
            off += s.shape[0]
        out.append(pieces[0] if len(pieces) == 1 else jnp.concatenate(pieces, axis=0))
    return out


def _grad_blocks_a(sections):
    rows = _dest_rows(sections, 1088)
    return jnp.stack([jnp.stack([rows[2 * k + c].astype(bf16) for k in range(4)]) for c in range(2)])


def _grad_slab_b(g):
    shards = dict(w_ba=_full_to_cols(g["w_ba"]), w_bh=_full_to_cols(g["w_bh"]), w_out=g["w_out"].reshape(8, 128, D_MODEL),
                  w_up=jnp.stack(_dest_rows(g["w_up"], 704)), w_down=g["w_down"].reshape(8, 352, D_MODEL))
    return _pack_rows({k: v.astype(bf16) for k, v in shards.items()}, _PACK_B)


_CONVW_SLAB_ROWS = 16


class _Traffic:
    def __init__(self, slab_a, slab_b, conv_w):
        hi = conv_w.astype(bf16)
        r1 = conv_w - hi.astype(f32)
        mid = r1.astype(bf16)
        lo = (r1 - mid.astype(f32)).astype(bf16)
        bits = jnp.stack([hi, mid, lo]).reshape(-1)
        tail = jnp.pad(bits, (0, _CONVW_SLAB_ROWS * D_MODEL - bits.shape[0])).reshape(_CONVW_SLAB_ROWS, D_MODEL)
        self.slab_b = jnp.concatenate([slab_b, tail], axis=0)
        self.state_a, tok = _split_start(slab_a, "chip_gather_wide", "ag_a_start")
        self.state_b, self.token = _split_start(self.slab_b, "chip_gather", "ag_b_start", after=tok)
        self.state = None
        self.state_gb = None

    def start_token(self):
        return self.token

    def weights_a(self, after):
        half = _split_wait(self.state_a, after, "chip_gather_wide", "ag_a_wait")
        return _weights_a(_core_fill(half, "ag_a_cores"))

    def forward_b(self, after):
        by_chip = _split_wait(self.state_b, after, "chip_gather", "ag_b_wait")
        self.state, token = _split_start(by_chip, "core_gather", "ag_b_cores_start")
        return token

    def weights_b(self, after):
        both = _split_wait(self.state, after, "core_gather", "ag_b_cores_wait")
        slabs = jnp.swapaxes(both, 0, 1).reshape((8,) + tuple(self.slab_b.shape))
        rows = _slab_rows(_PACK_B)
        out = _weights_b(slabs[:, :rows])
        pieces = slabs[:, rows:].reshape(8, -1)[:, :3 * 3 * 704].reshape(8, 3, 3, 704).astype(f32)
        out["conv_w"] = _cols_to_full((pieces[:, 0] + pieces[:, 1]) + pieces[:, 2])
        return out

    def grads_b_start(self, grads):
        self.state, token = _split_start(_by_core(_grad_slab_b(grads)), "core_swap", "rs_b_cores_start")
        return token

    def grads_b_exchange(self, after):
        from_sib, by_core = _split_wait(self.state, after, "core_swap", "rs_b_cores_wait")
        self.state_gb, token = _split_start(_pair_add(by_core, from_sib, "rs_b_pair_add"), "chip_xchg", "rs_b_start")
        return token

    def grads_a_start(self, sections):
        self.state, token = _split_start(_grad_blocks_a(sections), "core_swap", "rs_a_cores_start")
        return token

    def grads_a_exchange(self, after):
        from_sib, by_core = _split_wait(self.state, after, "core_swap", "rs_a_cores_wait")
        self.state, token = _split_start(_pair_add(by_core, from_sib, "rs_a_pair_add"), "chip_xchg", "rs_a_start")
        return token

    def parts(self, after):
        parts = _unpack_rows(_split_wait(self.state_gb, after, "chip_xchg", "rs_b_wait"), _PACK_B)
        parts["w_in"] = _split_wait(self.state, after, "chip_xchg", "rs_a_wait")
        return parts


def kernel(x, pre_mix_norm, w_in, rel_bias, hgrn_lb_raw, hgrn_norm, w_branch_attn, w_branch_hgrn, w_out, post_mix_norm, pre_ffn_norm, w_up, conv_w, conv_b, w_down, post_ffn_norm, loss_target, m_pre_mix_norm, m_w_in, m_rel_bias, m_hgrn_lb_raw, m_hgrn_norm, m_w_branch_attn, m_w_branch_hgrn, m_w_out, m_post_mix_norm, m_pre_ffn_norm, m_w_up, m_conv_w, m_conv_b, m_w_down, m_post_ffn_norm, v_pre_mix_norm, v_w_in, v_rel_bias, v_hgrn_lb_raw, v_hgrn_norm, v_w_branch_attn, v_w_branch_hgrn, v_w_out, v_post_mix_norm, v_pre_ffn_norm, v_w_up, v_conv_w, v_conv_b, v_w_down, v_post_ffn_norm):
    ci = lax.axis_index("c")
    dev = 4 * lax.axis_index("x") + 2 * lax.axis_index("y") + ci
    tr = lambda t: jnp.swapaxes(t[0], 0, 1)
    wts = dict(w_in=tr(w_in), w_ba=w_branch_attn[0], w_bh=w_branch_hgrn[0], w_out=w_out[0], w_up=tr(w_up),
               w_down=w_down[0])
    mom = dict(w_in=tr(m_w_in), w_ba=m_w_branch_attn[0], w_bh=m_w_branch_hgrn[0], w_out=m_w_out[0], w_up=tr(m_w_up),
               w_down=m_w_down[0])
    var = dict(w_in=tr(v_w_in), w_ba=v_w_branch_attn[0], w_bh=v_w_branch_hgrn[0], w_out=v_w_out[0], w_up=tr(v_w_up),
               w_down=v_w_down[0])
    small_w = dict(pre_mix_norm=pre_mix_norm, rel_bias=rel_bias, hgrn_lb_raw=hgrn_lb_raw, hgrn_norm=hgrn_norm,
                   post_mix_norm=post_mix_norm, pre_ffn_norm=pre_ffn_norm, conv_b=conv_b, post_ffn_norm=post_ffn_norm)
    small_m = dict(pre_mix_norm=m_pre_mix_norm, rel_bias=m_rel_bias, hgrn_lb_raw=m_hgrn_lb_raw, hgrn_norm=m_hgrn_norm,
                   post_mix_norm=m_post_mix_norm, pre_ffn_norm=m_pre_ffn_norm, conv_b=m_conv_b,
                   post_ffn_norm=m_post_ffn_norm)
    small_v = dict(pre_mix_norm=v_pre_mix_norm, rel_bias=v_rel_bias, hgrn_lb_raw=v_hgrn_lb_raw, hgrn_norm=v_hgrn_norm,
                   post_mix_norm=v_post_mix_norm, pre_ffn_norm=v_pre_ffn_norm, conv_b=v_conv_b,
                   post_ffn_norm=v_post_ffn_norm)

    plan = _Traffic(wts["w_in"].astype(bf16),
                    _pack_rows({k: wts[k].astype(bf16)[None] for k, _ in _PACK_B}, _PACK_B)[0], conv_w[0])

    loss8, grad_x, _, _, small = _local_step(x[0], loss_target[0], small_w, plan)
    spack = jnp.concatenate([_pack_small(small, loss8[0, 0:1]),
                             jnp.pad(small["conv_w"].reshape(-1, LANE), ((0, _CONVW_ROWS - 132), (0, 0)))], axis=0)
    small_state, token = _split_start(spack, "chip_gather", "ag_small_start")

    parts = plan.parts(token)
    outs_big = {}
    for k, _ in _PACK_SIZES:
        outs_big[k] = _adamw(wts[k], mom[k], var[k], parts[k], "adamw_" + k)

    by_chip = _split_wait(small_state, outs_big["w_in"][1], "chip_gather", "ag_small_wait")
    allp = _core_gather(by_chip, "ag_small_cores")
    ssum = _sum8(allp, "small_sum")
    gs = ssum[:_SMALL_ROWS]
    loss = ssum[_SMALL_USED // LANE, _SMALL_USED % LANE]
    res_small = _adamw(_pack_small(small_w), _pack_small(small_m), _pack_small(small_v), gs, "adamw_small")
    sm = [_unpack_small(t) for t in res_small]
    g_cw_full = ssum[_SMALL_ROWS:_SMALL_ROWS + 132].reshape(3, 2 * D_FF)
    g_cw = lax.dynamic_slice_in_dim(g_cw_full, dev * 704, 704, axis=1)
    res_cw = _adamw(conv_w[0], m_conv_w[0], v_conv_w[0], g_cw, "adamw_conv_w")

    def pick(i):
        def big_(k):
            t = outs_big[k][i]
            return (jnp.swapaxes(t, 0, 1) if k in _TRANSPOSED else t)[None]
        return [sm[i]["pre_mix_norm"], big_("w_in"), sm[i]["rel_bias"], sm[i]["hgrn_lb_raw"], sm[i]["hgrn_norm"],
                big_("w_ba"), big_("w_bh"), big_("w_out"), sm[i]["post_mix_norm"], sm[i]["pre_ffn_norm"],
                big_("w_up"), res_cw[i][None], sm[i]["conv_b"], big_("w_down"), sm[i]["post_ffn_norm"]]

    return (loss, grad_x[None], *pick(0), *pick(1), *pick(2), *pick(3))
```

```python
import functools
import math

import jax
import jax.numpy as jnp
from jax import lax
from jax.experimental import pallas as pl
from jax.experimental.pallas import tpu as pltpu

f32 = jnp.float32
bf16 = jnp.bfloat16
SDS = jax.ShapeDtypeStruct
HIGHEST = lax.Precision.HIGHEST
MESH = pl.DeviceIdType.MESH

NN = (((1,), (0,)), ((), ()))
NT = (((1,), (1,)), ((), ()))
TN = (((0,), (0,)), ((), ()))

D_MODEL = 1024
N_GROUPS = 3
DILATIONS = (1, 4, 16)
HEAD_DIM = 64
ATTN_BLOCK = 128
QKV_G = 1536
ATTN_OUT = 512
HGRN_W = 512
HGRN_CHUNK = 32
D_FF = 2816
NUM_BUCKETS = 32
MAX_EXACT = 16
MAX_DISTANCE = 2048
NEG_INF = -1e30
EPS = 1e-6
LANE = 128
SUBLANE = 8
VMEM_BIG = 48 * 1024 * 1024
MM_ROWS = 512
MM_OUT_BYTES = 8 * 1024 * 1024
ADAM_BLOCK_BYTES = 2304 * 1024

ADAM_LR, ADAM_B1, ADAM_B2, ADAM_EPS, ADAM_WD, ADAM_STEP = 0.001, 0.9, 0.999, 1e-08, 0.01, 10


def _pick(n, pref):
    t = pref
    while t >= LANE:
        if n % t == 0:
            return t
        t //= 2
    return n


def _cparams(sem=None, vmem=None):
    kw = {}
    if sem is not None:
        kw["dimension_semantics"] = sem
    if vmem is not None:
        kw["vmem_limit_bytes"] = vmem
    return pltpu.CompilerParams(**kw)


def _sigmoid(x):
    return jax.nn.sigmoid(x)


def _colsum8(x):
    return x.reshape(x.shape[0] // SUBLANE, SUBLANE, x.shape[1]).sum(axis=0)


def _mm(a, b, mode, out_dtype, name, acc=None, after=None):
    dims = {"nn": NN, "nt": NT, "tn": TN}[mode]
    has_acc = acc is not None
    parts = list(a) if isinstance(a, (list, tuple)) else [a]
    if mode == "tn":
        assert not has_acc
        K, N = b.shape
        widths = [t.shape[1] for t in parts]
        M = sum(widths)
        whole = M * N * 4 <= MM_OUT_BYTES
        assert whole or len(parts) == 1
        tmm = M if whole else M // 2
        ts = _pick(K, 4 * MM_ROWS)
        nk = K // ts

        npart = len(parts)
        narrow = out_dtype != f32

        def body_tn(*refs):
            b_ref, o_ref = refs[npart], refs[npart + 1]
            acc_ref = refs[npart + 2] if narrow else o_ref
            k = pl.program_id(1)
            bv = b_ref[...]
            lo = 0
            for a_ref, w in zip(refs[:npart], widths if whole else [tmm]):
                part = lax.dot_general(a_ref[...], bv, dims, preferred_element_type=f32)
                rows = slice(lo, lo + w)
                lo += w

                @pl.when(k == 0)
                def _(part=part, rows=rows):
                    acc_ref[rows, :] = part

                @pl.when(k > 0)
                def _(part=part, rows=rows):
                    acc_ref[rows, :] += part

            if narrow:
                @pl.when(k == nk - 1)
                def _():
                    o_ref[...] = acc_ref[...].astype(out_dtype)

        return pl.pallas_call(
            body_tn,
            grid=(M // tmm, nk),
            in_specs=[pl.BlockSpec((ts, w if whole else tmm), lambda i, k: (k, i)) for w in widths]
            + [pl.BlockSpec((ts, N), lambda i, k: (k, 0))],
            out_specs=pl.BlockSpec((tmm, N), lambda i, k: (i, 0)),
            out_shape=SDS((M, N), out_dtype),
            scratch_shapes=[pltpu.VMEM((tmm, N), f32)] if narrow else [],
            compiler_params=_cparams(("parallel", "arbitrary"), VMEM_BIG),
            name=name,
        )(*parts, b)

    bs = list(b) if isinstance(b, (list, tuple)) else [b]
    widths = [t.shape[1] for t in parts]
    M = parts[0].shape[0]
    kdim = 0 if mode == "nn" else 1
    N = bs[0].shape[1 - kdim]
    tm = _pick(M, MM_ROWS)
    npart, nb = len(parts), len(bs)
    place, bi, lo = [], 0, 0
    for w in widths:
        place.append((bi, lo))
        lo += w
        if lo == bs[bi].shape[kdim]:
            bi, lo = bi + 1, 0
    assert bi == nb and lo == 0

    def body(*refs):
        a_refs, b_refs = refs[:npart], refs[npart:npart + nb]
        c_ref = refs[npart + nb] if has_acc else None
        o_ref = refs[-1]
        part = None
        for a_ref, w, (bi, lo) in zip(a_refs, widths, place):
            b_ref = b_refs[bi]
            if w == bs[bi].shape[kdim]:
                bk = b_ref[...]
            else:
                bk = b_ref[:, lo:lo + w] if mode == "nt" else b_ref[lo:lo + w, :]
            t = lax.dot_general(a_ref[...], bk, dims, preferred_element_type=f32)
            part = t if part is None else part + t
        if has_acc:
            part = part + c_ref[...]
        o_ref[...] = part.astype(out_dtype)

    specs = [pl.BlockSpec((tm, w), lambda i: (i, 0)) for w in widths] \
        + [pl.BlockSpec(t.shape, lambda i: (0, 0)) for t in bs]
    args = parts + bs
    aliases = {}
    if has_acc:
        specs.append(pl.BlockSpec((tm, N), lambda i: (i, 0)))
        args.append(acc)
        aliases = {npart + nb: 0}
    if after is not None:
        specs.append(pl.BlockSpec(memory_space=pl.ANY))
        args.append(after)
    return pl.pallas_call(
        body,
        grid=(M // tm,),
        in_specs=specs,
        out_specs=pl.BlockSpec((tm, N), lambda i: (i, 0)),
        out_shape=SDS((M, N), out_dtype),
        input_output_aliases=aliases,
        compiler_params=_cparams(("parallel",), VMEM_BIG),
        name=name,
    )(*args)


def _mm_fanout(a, bs, mode, out_dtypes, name):
    dims = {"nn": NN, "nt": NT}[mode]
    M, K = a.shape
    ns = [b.shape[1] if mode == "nn" else b.shape[0] for b in bs]
    tm = _pick(M, MM_ROWS)
    nb = len(bs)

    def body(a_ref, *refs):
        av = a_ref[...]
        for b_ref, o_ref, dt in zip(refs[:nb], refs[nb:], out_dtypes):
            o_ref[...] = lax.dot_general(av, b_ref[...], dims, preferred_element_type=f32).astype(dt)

    return pl.pallas_call(
        body,
        grid=(M // tm,),
        in_specs=[pl.BlockSpec((tm, K), lambda i: (i, 0))] + [pl.BlockSpec(b.shape, lambda i: (0, 0)) for b in bs],
        out_specs=[pl.BlockSpec((tm, n), lambda i: (i, 0)) for n in ns],
        out_shape=[SDS((M, n), dt) for n, dt in zip(ns, out_dtypes)],
        compiler_params=_cparams(("parallel",), VMEM_BIG),
        name=name,
    )(a, *bs)


PERM_ROWS = 2048


def _perm_spec(d, cols=LANE):
    return pl.BlockSpec((d, PERM_ROWS // d, cols), lambda i, j: (0, i, j))


def _to_natural(src_ref, dst_ref, d):
    n = src_ref.shape[1]
    for r in range(d):
        dst_ref[pl.ds(r, n, stride=d), :] = src_ref[r]


def _prep(x, w, after=None):
    S, D = x.shape
    R = PERM_ROWS
    nc = D // LANE
    n_in = nc + 1 + (after is not None)

    def body(*refs):
        x_refs, w_ref = refs[:nc], refs[nc]
        h_ref, h4_ref, h16_ref, rs = refs[n_in:]
        ssq = None
        for xr in x_refs:
            v = xr[...]
            t = jnp.sum(v * v, axis=-1, keepdims=True)
            ssq = t if ssq is None else ssq + t
        rinv = lax.rsqrt(ssq * (1.0 / D) + EPS)
        rs[...] = jnp.broadcast_to(rinv, (R, LANE))
        for j, xr in enumerate(x_refs):
            cols = slice(j * LANE, (j + 1) * LANE)
            wj = w_ref[:, cols]
            h_ref[:, cols] = ((xr[...] * rinv) * wj).astype(bf16)
            for d, o_ref in ((4, h4_ref), (16, h16_ref)):
                n = R // d
                for r in range(d):
                    rows = pl.ds(r, n, stride=d)
                    o_ref[r, :, cols] = ((xr[rows, :] * rs[rows, :]) * wj).astype(bf16)

    col = lambda j: pl.BlockSpec((R, LANE), lambda i, j=j: (i, j))
    h, h4, h16 = pl.pallas_call(
        body,
        grid=(S // R,),
        in_specs=[col(j) for j in range(nc)] + [pl.BlockSpec((1, D), lambda i: (0, 0))]
        + ([] if after is None else [pl.BlockSpec(memory_space=pl.ANY)]),
        out_specs=[pl.BlockSpec((R, D), lambda i: (i, 0)), pl.BlockSpec((4, R // 4, D), lambda i: (0, i, 0)),
                   pl.BlockSpec((16, R // 16, D), lambda i: (0, i, 0))],
        out_shape=[SDS((S, D), bf16), SDS((4, S // 4, D), bf16), SDS((16, S // 16, D), bf16)],
        scratch_shapes=[pltpu.VMEM((R, LANE), f32)],
        compiler_params=_cparams(("parallel",), VMEM_BIG),
        name="prep_norm_perm",
    )(*([x] * nc), w, *([] if after is None else [after]))
    return [h, h4.reshape(S, D), h16.reshape(S, D)]


def _rms_parts(xv):
    r = lax.rsqrt(jnp.mean(xv * xv, axis=-1, keepdims=True) + EPS)
    return r, xv * r


def _rms_bwd(xhat, r, w, dy):
    dyw = dy * w
    return r * (dyw - xhat * jnp.mean(dyw * xhat, axis=-1, keepdims=True))


def _mid_fwd(x, merged, w_out, w_pm, w_pf):
    S, D = x.shape
    tm = _pick(S, MM_ROWS)

    def body(x_ref, m_ref, wo_ref, wpm_ref, wpf_ref, mo_ref, x1_ref, h2_ref):
        mo = jnp.dot(m_ref[...], wo_ref[...], preferred_element_type=f32)
        mo_ref[...] = mo
        _, moh = _rms_parts(mo)
        x1 = x_ref[...] + moh * wpm_ref[...]
        x1_ref[...] = x1
        _, x1h = _rms_parts(x1)
        h2_ref[...] = (x1h * wpf_ref[...]).astype(bf16)

    row = pl.BlockSpec((tm, D), lambda i: (i, 0))
    vec = pl.BlockSpec((1, D), lambda i: (0, 0))
    return pl.pallas_call(
        body,
        grid=(S // tm,),
        in_specs=[row, pl.BlockSpec((tm, merged.shape[1]), lambda i: (i, 0)),
                  pl.BlockSpec(w_out.shape, lambda i: (0, 0)), vec, vec],
        out_specs=[row, row, row],
        out_shape=[SDS((S, D), f32), SDS((S, D), f32), SDS((S, D), bf16)],
        compiler_params=_cparams(("parallel",), VMEM_BIG),
        name="out_proj_mid_fwd",
    )(x, merged, w_out, w_pm, w_pf)


def _final(x1, act, w_down, tgt, w_pfn):
    S, D = x1.shape
    tm = _pick(S, MM_ROWS)
    nt = S // tm

    def body(x1_ref, a_ref, wd_ref, t_ref, w_ref, loss_ref, dy_ref, dfo_ref, gw_ref, lacc, gacc):
        i = pl.program_id(0)

        @pl.when(i == 0)
        def _():
            lacc[...] = jnp.zeros_like(lacc)
            gacc[...] = jnp.zeros_like(gacc)

        w = w_ref[...]
        r, foh = _rms_parts(jnp.dot(a_ref[...], wd_ref[...], preferred_element_type=f32))
        y = x1_ref[...] + foh * w
        err = y - t_ref[...]
        lacc[...] += _colsum8(err * err)
        dy = err * (1.0 / D)
        dy_ref[...] = dy
        gacc[...] += _colsum8(dy * foh)
        dfo_ref[...] = _rms_bwd(foh, r, w, dy).astype(bf16)

        @pl.when(i == nt - 1)
        def _():
            loss_ref[...] = jnp.full((SUBLANE, LANE), 0.5 / D, f32) * jnp.sum(lacc[...])
            gw_ref[...] = jnp.sum(gacc[...], axis=0, keepdims=True)

    row = pl.BlockSpec((tm, D), lambda i: (i, 0))
    vec = pl.BlockSpec((1, D), lambda i: (0, 0))
    return pl.pallas_call(
        body,
        grid=(nt,),
        in_specs=[row, pl.BlockSpec((tm, act.shape[1]), lambda i: (i, 0)),
                  pl.BlockSpec(w_down.shape, lambda i: (0, 0)), row, vec],
        out_specs=[pl.BlockSpec((SUBLANE, LANE), lambda i: (0, 0)), row, row, vec],
        out_shape=[SDS((SUBLANE, LANE), f32), SDS((S, D), f32), SDS((S, D), bf16), SDS((1, D), f32)],
        scratch_shapes=[pltpu.VMEM((SUBLANE, D), f32), pltpu.VMEM((SUBLANE, D), f32)],
        compiler_params=_cparams(("arbitrary",), VMEM_BIG),
        name="down_proj_final_loss",
    )(x1, act, w_down, tgt, w_pfn)


MID_BWD_ROWS = 256


def _mid_bwd(dy, dug, duv, wt_g, wt_v, x1, mo, w_pf, w_pm):
    S, D = dy.shape
    tm = _pick(S, MID_BWD_ROWS)
    nt = S // tm

    def body(dy_ref, dug_ref, duv_ref, wg_ref, wv_ref, x1_ref, mo_ref, wpf_ref, wpm_ref,
             dx1_ref, dmo_ref, gpf_ref, gpm_ref, apf, apm):
        i = pl.program_id(0)

        @pl.when(i == 0)
        def _():
            apf[...] = jnp.zeros_like(apf)
            apm[...] = jnp.zeros_like(apm)

        r1, x1h = _rms_parts(x1_ref[...])
        dh2 = jnp.dot(dug_ref[...], wg_ref[...], preferred_element_type=f32) \
            + jnp.dot(duv_ref[...], wv_ref[...], preferred_element_type=f32)
        apf[...] += _colsum8(dh2 * x1h)
        dx1 = dy_ref[...] + _rms_bwd(x1h, r1, wpf_ref[...], dh2)
        dx1_ref[...] = dx1
        rm, moh = _rms_parts(mo_ref[...])
        apm[...] += _colsum8(dx1 * moh)
        dmo_ref[...] = _rms_bwd(moh, rm, wpm_ref[...], dx1).astype(bf16)

        @pl.when(i == nt - 1)
        def _():
            gpf_ref[...] = jnp.sum(apf[...], axis=0, keepdims=True)
            gpm_ref[...] = jnp.sum(apm[...], axis=0, keepdims=True)

    row = pl.BlockSpec((tm, D), lambda i: (i, 0))
    vec = pl.BlockSpec((1, D), lambda i: (0, 0))
    return pl.pallas_call(
        body,
        grid=(nt,),
        in_specs=[row, pl.BlockSpec((tm, dug.shape[1]), lambda i: (i, 0)), pl.BlockSpec((tm, duv.shape[1]), lambda i: (i, 0)),
                  pl.BlockSpec(wt_g.shape, lambda i: (0, 0)), pl.BlockSpec(wt_v.shape, lambda i: (0, 0)),
                  row, row, vec, vec],
        out_specs=[row, row, vec, vec],
        out_shape=[SDS((S, D), f32), SDS((S, D), bf16), SDS((1, D), f32), SDS((1, D), f32)],
        scratch_shapes=[pltpu.VMEM((SUBLANE, D), f32), pltpu.VMEM((SUBLANE, D), f32)],
        compiler_params=_cparams(("arbitrary",), VMEM_BIG),
        name="dh2_mid_bwd",
    )(dy, dug, duv, wt_g, wt_v, x1, mo, w_pf, w_pm)


def _first_bwd(x, dx1, dh_a, dh_b, dh_c, w_pre):
    S, D = x.shape
    tm = _pick(S, 512)
    nt = S // tm
    nc = D // LANE

    def body(*refs):
        x_ref, dx1_ref, a_ref = refs[:3]
        b_refs, c_refs, w_ref = refs[3:3 + nc], refs[3 + nc:3 + 2 * nc], refs[3 + 2 * nc]
        gx_ref, gw_ref, acc, dh_s, sb, sc = refs[4 + 2 * nc:]
        i = pl.program_id(0)

        @pl.when(i == 0)
        def _():
            acc[...] = jnp.zeros_like(acc)

        for j in range(nc):
            cols = slice(j * LANE, (j + 1) * LANE)
            _to_natural(b_refs[j], sb, 4)
            _to_natural(c_refs[j], sc, 16)
            dh_s[:, cols] = (a_ref[:, cols] + sb[...]) + sc[...]
        r, xh = _rms_parts(x_ref[...])
        dh = dh_s[...]
        acc[...] += _colsum8(dh * xh)
        gx_ref[...] = dx1_ref[...] + _rms_bwd(xh, r, w_ref[...], dh)

        @pl.when(i == nt - 1)
        def _():
            gw_ref[...] = jnp.sum(acc[...], axis=0, keepdims=True)

    row = pl.BlockSpec((tm, D), lambda i: (i, 0))
    vec = pl.BlockSpec((1, D), lambda i: (0, 0))
    perm = lambda d: [pl.BlockSpec((d, tm // d, LANE), lambda i, j=j: (0, i, j)) for j in range(nc)]
    return pl.pallas_call(
        body,
        grid=(nt,),
        in_specs=[row, row, row] + perm(4) + perm(16) + [vec],
        out_specs=[row, vec],
        out_shape=[SDS((S, D), f32), SDS((1, D), f32)],
        scratch_shapes=[pltpu.VMEM((SUBLANE, D), f32), pltpu.VMEM((tm, D), f32), pltpu.VMEM((tm, LANE), f32),
                        pltpu.VMEM((tm, LANE), f32)],
        compiler_params=_cparams(("arbitrary",), VMEM_BIG),
        name="first_bwd",
    )(x, dx1, dh_a, *([dh_b.reshape(4, S // 4, D)] * nc), *([dh_c.reshape(16, S // 16, D)] * nc), w_pre)


def _t5_bucket(dist):
    n = jnp.maximum(dist, 0)
    nf = jnp.maximum(n, 1).astype(f32)
    large = MAX_EXACT + (jnp.log(nf / MAX_EXACT) / math.log(MAX_DISTANCE / MAX_EXACT)
                         * (NUM_BUCKETS - MAX_EXACT)).astype(jnp.int32)
    large = jnp.minimum(large, NUM_BUCKETS - 1)
    return jnp.where(n < MAX_EXACT, n, large)


def _bias_consts(d, after=None):
    if after is not None:
        d, _ = lax.optimization_barrier((jnp.int32(d), after))
    blk = ATTN_BLOCK
    rel = jnp.arange(blk)[:, None] + blk - jnp.arange(2 * blk)[None, :]
    in_win = (rel >= 0) & (rel <= blk)
    bucket = _t5_bucket(rel * d).reshape(1, -1)
    onehot = (bucket == jnp.arange(NUM_BUCKETS)[:, None]).astype(f32)
    return onehot, in_win.astype(f32).reshape(1, -1)


def _bias_build(tab_t, onehot, maskf, name, after):
    H = tab_t.shape[0]

    def body(t_ref, oh_ref, m_ref, after_ref, o_ref):
        b = jnp.dot(t_ref[...], oh_ref[...], precision=HIGHEST, preferred_element_type=f32)
        o_ref[...] = jnp.where(m_ref[...] > 0.5, b, NEG_INF)

    vm = pl.BlockSpec(memory_space=pltpu.VMEM)
    return pl.pallas_call(body, out_shape=SDS((H, onehot.shape[1]), f32), name=name,
                          in_specs=[vm, vm, vm, pl.BlockSpec(memory_space=pl.ANY)], out_specs=vm,
                          )(tab_t, onehot, maskf, after)


def _bias_grad(dbias_flat, onehot, name):
    H = dbias_flat.shape[0]

    def body(g_ref, oh_ref, o_ref):
        o_ref[...] = lax.dot_general(oh_ref[...], g_ref[...], NT, precision=HIGHEST, preferred_element_type=f32)

    return pl.pallas_call(body, out_shape=SDS((NUM_BUCKETS, H), f32), name=name)(dbias_flat, onehot)


ATTN_TILE = 512
ATTN_SUB = ATTN_TILE // ATTN_BLOCK
ATTN_HP = 4
ATTN_WIDE = ATTN_HP * LANE


def _qkv_specs(nt):
    tile = (ATTN_TILE, ATTN_WIDE)
    blk = (ATTN_BLOCK, ATTN_WIDE)
    sec = ATTN_OUT // ATTN_WIDE
    cur = lambda off: (lambda h, t: (jnp.minimum(t, nt - 1), off + h))
    prev = lambda off: (lambda h, t: (jnp.maximum(jnp.minimum(t, nt - 1) * ATTN_SUB - 1, 0), off + h))
    return [pl.BlockSpec(tile, cur(0)), pl.BlockSpec(blk, prev(sec)), pl.BlockSpec(tile, cur(sec)),
            pl.BlockSpec(blk, prev(2 * sec)), pl.BlockSpec(tile, cur(2 * sec))]


def _head_masks():
    lane = lax.broadcasted_iota(jnp.int32, (ATTN_BLOCK, LANE), 1)
    return lane < HEAD_DIM


def _stack_heads(x2, low):
    zero = jnp.zeros_like(x2)
    return jnp.concatenate([jnp.where(low, x2, zero), jnp.where(low, zero, x2)], axis=0)


def _attn_fwd(qkv, bias, bps, name, after=None):
    S = qkv.shape[0]
    nt = S // ATTN_TILE
    scale = HEAD_DIM ** -0.5

    def body(q_ref, kp_ref, kc_ref, vp_ref, vc_ref, b_ref, *rest):
        o_ref, l_ref = rest[-2:]
        t = pl.program_id(1)
        low = _head_masks()
        col = lax.broadcasted_iota(jnp.int32, (2 * ATTN_BLOCK, 2 * ATTN_BLOCK), 1)
        for hp in range(ATTN_HP):
            cols = slice(hp * LANE, (hp + 1) * LANE)
            kk = jnp.concatenate([kp_ref[:, cols], kc_ref[:, cols]], axis=0)
            vv = jnp.concatenate([vp_ref[:, cols], vc_ref[:, cols]], axis=0)
            bias2 = b_ref[2 * hp:2 * hp + 2].reshape(2 * ATTN_BLOCK, 2 * ATTN_BLOCK)
            for b in range(ATTN_SUB):
                lo = b * ATTN_BLOCK
                rows = slice(lo, lo + ATTN_BLOCK)
                keys = slice(lo, lo + 2 * ATTN_BLOCK)
                dead = jnp.logical_and((t * ATTN_SUB + b) % bps == 0, col < ATTN_BLOCK)
                q2 = _stack_heads(q_ref[rows, cols], low)
                kb, vb = kk[keys], vv[keys]
                s = lax.dot_general(q2, kb, NT, preferred_element_type=f32) * scale + bias2
                s = jnp.where(dead, NEG_INF, s)
                m = jnp.max(s, axis=-1, keepdims=True)
                p = jnp.exp(s - m)
                l = jnp.sum(p, axis=-1, keepdims=True)
                o2 = jnp.dot(p.astype(bf16), vb, preferred_element_type=f32) / l
                lse = m + jnp.log(l)
                o_ref[rows, cols] = jnp.where(low, o2[:ATTN_BLOCK], o2[ATTN_BLOCK:])
                l_ref[rows, cols] = jnp.where(low, lse[:ATTN_BLOCK], lse[ATTN_BLOCK:])

    tile = pl.BlockSpec((ATTN_TILE, ATTN_WIDE), lambda h, t: (t, h))
    return pl.pallas_call(
        body,
        grid=(4 // ATTN_HP, nt),
        in_specs=_qkv_specs(nt) + [pl.BlockSpec((2 * ATTN_HP, ATTN_BLOCK, 2 * ATTN_BLOCK), lambda h, t: (h, 0, 0))]
        + ([] if after is None else [pl.BlockSpec(memory_space=pl.ANY)]),
        out_specs=[tile, tile],
        out_shape=[SDS((S, ATTN_OUT), f32), SDS((S, ATTN_OUT), f32)],
        compiler_params=_cparams(("parallel", "parallel")),
        name=name,
    )(qkv, qkv, qkv, qkv, qkv, bias, *([] if after is None else [after]))


def _attn_bwd(qkv, bias, do, dvec, lse, bps, name):
    S = qkv.shape[0]
    nt = S // ATTN_TILE
    scale = HEAD_DIM ** -0.5

    def assemble(parts):
        rows = [parts[0][:ATTN_BLOCK]]
        for b in range(ATTN_SUB - 1):
            rows.append(parts[b][ATTN_BLOCK:] + parts[b + 1][:ATTN_BLOCK])
        rows.append(parts[-1][ATTN_BLOCK:])
        return rows

    def body(q_ref, kp_ref, kc_ref, vp_ref, vc_ref, b_ref, do_ref, dvec_ref, lse_ref,
             dq_ref, dk_ref, dv_ref, db_ref, ck, cv):
        t = pl.program_id(1)
        last = ATTN_TILE - ATTN_BLOCK

        @pl.when(t == 0)
        def _():
            ck[...] = jnp.zeros_like(ck)
            cv[...] = jnp.zeros_like(cv)
            db_ref[...] = jnp.zeros_like(db_ref)

        @pl.when(t < nt)
        def _():
            low = _head_masks()
            col = lax.broadcasted_iota(jnp.int32, (2 * ATTN_BLOCK, 2 * ATTN_BLOCK), 1)
            per_row = lambda t2: jnp.concatenate([t2[:, 0:1], t2[:, HEAD_DIM:HEAD_DIM + 1]], axis=0)
            for hp in range(ATTN_HP):
                cols = slice(hp * LANE, (hp + 1) * LANE)
                kk = jnp.concatenate([kp_ref[:, cols], kc_ref[:, cols]], axis=0)
                vv = jnp.concatenate([vp_ref[:, cols], vc_ref[:, cols]], axis=0)
                bias2 = b_ref[2 * hp:2 * hp + 2].reshape(2 * ATTN_BLOCK, 2 * ATTN_BLOCK)
                dk_parts, dv_parts = [], []
                dsum = None
                for b in range(ATTN_SUB):
                    lo = b * ATTN_BLOCK
                    rows = slice(lo, lo + ATTN_BLOCK)
                    keys = slice(lo, lo + 2 * ATTN_BLOCK)
                    dead = jnp.logical_and((t * ATTN_SUB + b) % bps == 0, col < ATTN_BLOCK)
                    q2 = _stack_heads(q_ref[rows, cols], low)
                    do2 = _stack_heads(do_ref[rows, cols].astype(bf16), low)
                    kb, vb = kk[keys], vv[keys]
                    s = lax.dot_general(q2, kb, NT, preferred_element_type=f32) * scale + bias2
                    s = jnp.where(dead, NEG_INF, s)
                    p = jnp.exp(s - per_row(lse_ref[rows, cols]))
                    dp = lax.dot_general(do2, vb, NT, preferred_element_type=f32)
                    ds = p * (dp - per_row(dvec_ref[rows, cols]))
                    dsum = ds if dsum is None else dsum + ds
                    dsb = ds.astype(bf16)
                    dq2 = jnp.dot(dsb, kb, preferred_element_type=f32) * scale
                    dq_ref[rows, cols] = jnp.where(low, dq2[:ATTN_BLOCK], dq2[ATTN_BLOCK:]).astype(bf16)
                    dk_parts.append(lax.dot_general(dsb, q2, TN, preferred_element_type=f32) * scale)
                    dv_parts.append(lax.dot_general(p.astype(bf16), do2, TN, preferred_element_type=f32))
                db_ref[2 * hp:2 * hp + 2] += dsum.reshape(2, ATTN_BLOCK, 2 * ATTN_BLOCK)
                for parts, carry, out_ref in ((dk_parts, ck, dk_ref), (dv_parts, cv, dv_ref)):
                    rws = assemble(parts)
                    out_ref[:last, cols] = carry[:last, cols].astype(bf16)
                    out_ref[last:, cols] = (carry[last:, cols] + rws[0]).astype(bf16)
                    for b in range(ATTN_SUB):
                        carry[b * ATTN_BLOCK:(b + 1) * ATTN_BLOCK, cols] = rws[b + 1]

        @pl.when(t == nt)
        def _():
            dk_ref[...] = ck[...].astype(bf16)
            dv_ref[...] = cv[...].astype(bf16)

    tile = (ATTN_TILE, ATTN_WIDE)
    cur = pl.BlockSpec(tile, lambda h, t: (jnp.minimum(t, nt - 1), h))
    lag = pl.BlockSpec(tile, lambda h, t: (jnp.maximum(t - 1, 0), h))
    bspec = pl.BlockSpec((2 * ATTN_HP, ATTN_BLOCK, 2 * ATTN_BLOCK), lambda h, t: (h, 0, 0))
    return pl.pallas_call(
        body,
        grid=(4 // ATTN_HP, nt + 1),
        in_specs=_qkv_specs(nt) + [bspec, cur, cur, cur],
        out_specs=[cur, lag, lag, bspec],
        out_shape=[SDS((S, ATTN_OUT), bf16), SDS((S, ATTN_OUT), bf16), SDS((S, ATTN_OUT), bf16),
                   SDS((8, ATTN_BLOCK, 2 * ATTN_BLOCK), f32)],
        scratch_shapes=[pltpu.VMEM(tile, f32), pltpu.VMEM(tile, f32)],
        compiler_params=_cparams(("parallel", "arbitrary")),
        name=name,
    )(qkv, qkv, qkv, qkv, qkv, bias, do, dvec, lse)


def _attn_merge(o0, o1, o2, l0, l1, l2):
    S, W = o0.shape
    R = PERM_ROWS

    def body(o0_ref, o1_ref, o2_ref, l0_ref, l1_ref, l2_ref, y_ref, yb_ref, w0_ref, w1_ref, w2_ref,
             so1, so2, sl1, sl2):
        _to_natural(o1_ref, so1, 4)
        _to_natural(l1_ref, sl1, 4)
        _to_natural(o2_ref, so2, 16)
        _to_natural(l2_ref, sl2, 16)
        a, b, c = l0_ref[...], sl1[...], sl2[...]
        m = jnp.maximum(jnp.maximum(a, b), c)
        ea, eb, ec = jnp.exp(a - m), jnp.exp(b - m), jnp.exp(c - m)
        den = (ea + eb) + ec
        w0, w1, w2 = ea / den, eb / den, ec / den
        y = (w0 * o0_ref[...] + w1 * so1[...]) + w2 * so2[...]
        y_ref[...] = y
        yb_ref[...] = y.astype(bf16)
        w0_ref[...] = w0
        w1_ref[...] = w1
        w2_ref[...] = w2

    nat = pl.BlockSpec((R, LANE), lambda i, j: (i, j))
    v4 = lambda t: t.reshape(4, S // 4, W)
    v16 = lambda t: t.reshape(16, S // 16, W)
    return pl.pallas_call(
        body,
        grid=(S // R, W // LANE),
        in_specs=[nat, _perm_spec(4), _perm_spec(16)] * 2,
        out_specs=[nat] * 5,
        out_shape=[SDS((S, W), f32), SDS((S, W), bf16)] + [SDS((S, W), f32)] * 3,
        scratch_shapes=[pltpu.VMEM((R, LANE), f32)] * 4,
        compiler_params=_cparams(("parallel", "parallel"), VMEM_BIG),
        name="attn_merge",
    )(o0, v4(o1), v16(o2), l0, v4(l1), v16(l2))


def _attn_merge_bwd(dy, y, w0, w1, w2, after=None):
    S, W = dy.shape
    R = PERM_ROWS

    def body(dy_ref, y_ref, w0_ref, w1_ref, w2_ref, *rest):
        a0, a1, a2, b0, b1, b2, sa, sb = rest[-8:]
        dyv = dy_ref[...]
        r = lax.broadcasted_iota(jnp.int32, (LANE, LANE), 0) // HEAD_DIM
        c = lax.broadcasted_iota(jnp.int32, (LANE, LANE), 1) // HEAD_DIM
        seg = jnp.where(r == c, 1.0, 0.0).astype(f32)
        cbar = jnp.dot(dyv * y_ref[...], seg, precision=HIGHEST, preferred_element_type=f32)
        w = w0_ref[...]
        a0[...] = (w * dyv).astype(bf16)
        b0[...] = w * cbar
        for d, w_ref, a_ref, b_ref in ((4, w1_ref, a1, b1), (16, w2_ref, a2, b2)):
            w = w_ref[...]
            sa[...] = w * dyv
            sb[...] = w * cbar
            n = R // d
            for k in range(d):
                rows = pl.ds(k, n, stride=d)
                a_ref[k] = sa[rows, :].astype(bf16)
                b_ref[k] = sb[rows, :]

    nat = pl.BlockSpec((R, LANE), lambda i, j: (i, j))
    shapes = lambda dt: [SDS((S, W), dt), SDS((4, S // 4, W), dt), SDS((16, S // 16, W), dt)]
    outs = pl.pallas_call(
        body,
        grid=(S // R, W // LANE),
        in_specs=[nat] * 5 + ([] if after is None else [pl.BlockSpec(memory_space=pl.ANY)]),
        out_specs=[nat, _perm_spec(4), _perm_spec(16)] * 2,
        out_shape=shapes(bf16) + shapes(f32),
        scratch_shapes=[pltpu.VMEM((R, LANE), f32)] * 2,
        compiler_params=_cparams(("parallel", "parallel"), VMEM_BIG),
        name="attn_merge_bwd",
    )(dy, y, w0, w1, w2, *([] if after is None else [after]))
    return [t.reshape(S, W) for t in outs]


HGRN_SB = 256
HGRN_PAIR = 4


def _chunk_masks():
    r = jnp.arange(HGRN_SB)[:, None]
    c = jnp.arange(HGRN_SB)[None, :]
    same = (r // HGRN_CHUNK) == (c // HGRN_CHUNK)
    return jnp.stack([same & (c <= r), same, same & (c >= r)]).astype(bf16)


def _mask_dot(mask, x):
    hi = x.astype(bf16)
    r1 = x - hi.astype(f32)
    mid = r1.astype(bf16)
    lo = (r1 - mid.astype(f32)).astype(bf16)
    p = jnp.dot(mask, jnp.concatenate([hi, mid, lo], axis=1), preferred_element_type=f32)
    n = x.shape[1]
    return (p[:, :n] + p[:, n:2 * n]) + p[:, 2 * n:]


def _hgrn_prep(q_raw, f_raw, lbv, tril, same):
    sq = _sigmoid(q_raw)
    qs = q_raw * sq
    sig = _sigmoid(f_raw)
    f = lbv + (1.0 - lbv) * sig
    g = jnp.log(f)
    k = 1.0 - f
    G = _mask_dot(tril, g)
    GL = _mask_dot(same, g)
    eG = jnp.exp(G)
    einv = jnp.exp(-G)
    edec = jnp.exp(GL - G)
    return dict(sq=sq, qs=qs, sig=sig, f=f, k=k, eG=eG, einv=einv, edec=edec, eGL=jnp.exp(GL),
                qt=qs * eG, kt=k * einv, kd=k * edec)


def _hgrn_fwd(hg, lb, normw):
    S = hg.shape[0]
    sb = HGRN_SB
    nsb = S // sb
    nch = sb // HGRN_CHUNK

    def body(q_ref, f_ref, v_ref, og_ref, lb_ref, nw_ref, m_ref, y_ref, o_ref, ck_ref, st):
        j = pl.program_id(1)

        @pl.when(j == 0)
        def _():
            st[...] = jnp.zeros_like(st)

        tril_m = m_ref[0]
        tril = tril_m.astype(f32) > 0.5

        def one_head(hh):
            cols = slice(hh * LANE, (hh + 1) * LANE)
            ST = st[hh]
            ck_ref[hh, 0] = ST
            pr = _hgrn_prep(q_ref[:, cols], f_ref[:, cols], lb_ref[:, cols], tril_m, m_ref[1])
            qtb, ktb, kdb = pr["qt"].astype(bf16), pr["kt"].astype(bf16), pr["kd"].astype(bf16)
            eGL = pr["eGL"]
            vb = v_ref[:, cols].astype(bf16)
            A = jnp.where(tril, lax.dot_general(qtb, ktb, NT, preferred_element_type=f32), 0.0)
            o = jnp.dot(A.astype(bf16), vb, preferred_element_type=f32)
            outs = []
            for ci in range(nch):
                lo = ci * HGRN_CHUNK
                sl = slice(lo, lo + HGRN_CHUNK)
                outs.append(o[sl] + lax.dot_general(qtb[sl], ST.astype(bf16), NT, preferred_element_type=f32))
                ST = ST * eGL[lo:lo + 1, :] + lax.dot_general(vb[sl], kdb[sl], TN, preferred_element_type=f32)
            st[hh] = ST
            of = jnp.concatenate(outs, axis=0)
            o_ref[:, cols] = of
            rms = lax.rsqrt(jnp.mean(of * of, axis=-1, keepdims=True) + EPS)
            ogv = og_ref[:, cols]
            y_ref[:, cols] = ((of * rms * nw_ref[...]) * (ogv * _sigmoid(ogv))).astype(bf16)

        for hh in range(HGRN_PAIR):
            one_head(hh)

    wide = HGRN_PAIR * LANE
    col = lambda off: pl.BlockSpec((sb, wide), lambda h, j: (j, off // HGRN_PAIR + h))
    return pl.pallas_call(
        body,
        grid=(4 // HGRN_PAIR, nsb),
        in_specs=[col(0), col(4), col(8), col(12), pl.BlockSpec((1, wide), lambda h, j: (0, h)),
                  pl.BlockSpec((1, LANE), lambda h, j: (0, 0)),
                  pl.BlockSpec((3, sb, sb), lambda h, j: (0, 0, 0))],
        out_specs=[col(0), col(0), pl.BlockSpec((HGRN_PAIR, 1, LANE, LANE), lambda h, j: (h, j, 0, 0))],
        out_shape=[SDS((S, HGRN_W), bf16), SDS((S, HGRN_W), f32), SDS((4, nsb, LANE, LANE), f32)],
        scratch_shapes=[pltpu.VMEM((HGRN_PAIR, LANE, LANE), f32)],
        compiler_params=_cparams(("parallel", "arbitrary")),
        name="hgrn_fwd",
    )(hg, hg, hg, hg, lb, normw, _chunk_masks())


def _hgrn_bwd(hg, o_raw, dy, ck, lb, normw, after=None):
    S = hg.shape[0]
    sb = HGRN_SB
    nsb = S // sb
    nch = sb // HGRN_CHUNK

    def body(q_ref, f_ref, v_ref, og_ref, o_ref, dy_ref, ck_ref, lb_ref, nw_ref, m_ref, *rest):
        dq_ref, df_ref, dv_ref, dog_ref, glb_ref, gnw_ref, dst, alb, anw = rest[-9:]
        j = pl.program_id(1)

        @pl.when(j == 0)
        def _():
            dst[...] = jnp.zeros_like(dst)
            alb[...] = jnp.zeros_like(alb)
            anw[...] = jnp.zeros_like(anw)

        tril_m = m_ref[0]
        tril = tril_m.astype(f32) > 0.5
        nw = nw_ref[...]

        def one_head(hh):
            cols = slice(hh * LANE, (hh + 1) * LANE)
            lbv = lb_ref[:, cols]
            q_raw = q_ref[:, cols]
            pr = _hgrn_prep(q_raw, f_ref[:, cols], lbv, tril_m, m_ref[1])
            qt, kt, kd, eGL = pr["qt"], pr["kt"], pr["kd"], pr["eGL"]
            qtb, ktb, kdb = qt.astype(bf16), kt.astype(bf16), kd.astype(bf16)
            vb = v_ref[:, cols].astype(bf16)

            o = o_ref[:, cols]
            ogv = og_ref[:, cols]
            sog = _sigmoid(ogv)
            rms = lax.rsqrt(jnp.mean(o * o, axis=-1, keepdims=True) + EPS)
            oh = o * rms
            dyv = dy_ref[:, cols]
            dog_ref[:, cols] = (dyv * (oh * nw) * (sog * (1.0 + ogv * (1.0 - sog)))).astype(bf16)
            dohw = dyv * (ogv * sog)
            anw[:, cols] += _colsum8(dohw * oh)
            doh = dohw * nw
            do = rms * (doh - oh * jnp.mean(doh * oh, axis=-1, keepdims=True))
            dob = do.astype(bf16)

            Ab = jnp.where(tril, lax.dot_general(qtb, ktb, NT, preferred_element_type=f32), 0.0).astype(bf16)
            dAb = jnp.where(tril, lax.dot_general(dob, vb, NT, preferred_element_type=f32), 0.0).astype(bf16)
            dv_acc = lax.dot_general(Ab, dob, TN, preferred_element_type=f32)
            dqt = jnp.dot(dAb, ktb, preferred_element_type=f32)
            dkt = lax.dot_general(dAb, qtb, TN, preferred_element_type=f32)

            ST = ck_ref[hh, 0]
            states = []
            for ci in range(nch):
                lo = ci * HGRN_CHUNK
                sl = slice(lo, lo + HGRN_CHUNK)
                states.append(ST)
                ST = ST * eGL[lo:lo + 1, :] + lax.dot_general(vb[sl], kdb[sl], TN, preferred_element_type=f32)

            dST = dst[hh]
            dqt_i, dkd_i, dv_i, deg_i = [None] * nch, [None] * nch, [None] * nch, [None] * nch
            for ci in reversed(range(nch)):
                lo = ci * HGRN_CHUNK
                sl = slice(lo, lo + HGRN_CHUNK)
                ST0 = states[ci]
                dSTb = dST.astype(bf16)
                dv_i[ci] = lax.dot_general(kdb[sl], dSTb, NT, preferred_element_type=f32)
                dqt_i[ci] = jnp.dot(dob[sl], ST0.astype(bf16), preferred_element_type=f32)
                dkd_i[ci] = jnp.dot(vb[sl], dSTb, preferred_element_type=f32)
                deg_i[ci] = jnp.broadcast_to(jnp.sum(dST * ST0, axis=0, keepdims=True), (HGRN_CHUNK, LANE))
                dST = dST * eGL[lo:lo + 1, :] + lax.dot_general(dob[sl], qtb[sl], TN, preferred_element_type=f32)
            dst[hh] = dST

            dqt = dqt + jnp.concatenate(dqt_i, axis=0)
            dkd = jnp.concatenate(dkd_i, axis=0)
            dv_ref[:, cols] = (dv_acc + jnp.concatenate(dv_i, axis=0)).astype(bf16)
            deg = jnp.concatenate(deg_i, axis=0)

            dqs = dqt * pr["eG"]
            dkdkd = dkd * kd
            dG = dqt * qt - dkt * kt - dkdkd
            dk = dkt * pr["einv"] + dkd * pr["edec"]
            dGL = _mask_dot(m_ref[1], dkdkd) + eGL * deg
            dg = _mask_dot(m_ref[2], dG) + dGL
            df = dg / pr["f"] - dk
            sig = pr["sig"]
            df_ref[:, cols] = (df * (1.0 - lbv) * (sig * (1.0 - sig))).astype(bf16)
            alb[:, cols] += _colsum8(df * (1.0 - sig))
            sq = pr["sq"]
            dq_ref[:, cols] = (dqs * (sq * (1.0 + q_raw * (1.0 - sq)))).astype(bf16)

        for hh in range(HGRN_PAIR):
            one_head(hh)

        @pl.when(j == nsb - 1)
        def _():
            glb_ref[...] = jnp.broadcast_to(jnp.sum(alb[...], axis=0, keepdims=True), (SUBLANE, wide))
            gnw_ref[...] = jnp.broadcast_to(jnp.sum(anw[...], axis=0, keepdims=True), (SUBLANE, wide))

    wide = HGRN_PAIR * LANE
    rev = lambda off: pl.BlockSpec((sb, wide), lambda h, j: (nsb - 1 - j, off // HGRN_PAIR + h))
    stat = pl.BlockSpec((SUBLANE, wide), lambda h, j: (0, h))
    return pl.pallas_call(
        body,
        grid=(4 // HGRN_PAIR, nsb),
        in_specs=[rev(0), rev(4), rev(8), rev(12), rev(0), rev(0),
                  pl.BlockSpec((HGRN_PAIR, 1, LANE, LANE), lambda h, j: (h, nsb - 1 - j, 0, 0)),
                  pl.BlockSpec((1, wide), lambda h, j: (0, h)), pl.BlockSpec((1, LANE), lambda h, j: (0, 0)),
                  pl.BlockSpec((3, sb, sb), lambda h, j: (0, 0, 0))]
        + ([] if after is None else [pl.BlockSpec(memory_space=pl.ANY)]),
        out_specs=[rev(0), rev(0), rev(0), rev(0), stat, stat],
        out_shape=[SDS((S, HGRN_W), bf16)] * 4 + [SDS((SUBLANE, HGRN_W), f32)] * 2,
        scratch_shapes=[pltpu.VMEM((HGRN_PAIR, LANE, LANE), f32), pltpu.VMEM((SUBLANE, wide), f32),
                        pltpu.VMEM((SUBLANE, wide), f32)],
        compiler_params=_cparams(("parallel", "arbitrary")),
        name="hgrn_bwd",
    )(hg, hg, hg, hg, o_raw, dy, ck, lb, normw, _chunk_masks(), *([] if after is None else [after]))


def _lb_fwd(raw):
    def body(r_ref, o_ref):
        r = r_ref[...]
        m = jnp.max(r, axis=0, keepdims=True)
        e = jnp.exp(r - m)
        o_ref[...] = (e / jnp.sum(e, axis=0, keepdims=True))[0:1]

    return pl.pallas_call(body, out_shape=SDS((1, raw.shape[1]), f32), name="lb_fwd")(raw)


def _lb_bwd(raw, dlb):
    def body(r_ref, d_ref, o_ref):
        r = r_ref[...]
        m = jnp.max(r, axis=0, keepdims=True)
        e = jnp.exp(r - m)
        s = e / jnp.sum(e, axis=0, keepdims=True)
        s0 = s[0:1]
        onehot0 = jnp.where(lax.broadcasted_iota(jnp.int32, r.shape, 0) == 0, 1.0, 0.0)
        o_ref[...] = d_ref[...] * s0 * (onehot0 - s)

    return pl.pallas_call(body, out_shape=SDS(raw.shape, f32), name="lb_bwd")(raw, dlb)


def _gate_fwd(ya, yh, w_ba, w_bh, gc):
    S = ya.shape[0]
    D = w_ba.shape[1]
    tm = _pick(S, MM_ROWS)

    def body(ya_ref, yh_ref, wa_ref, wh_ref, g0_ref, g1_ref, a_ref, b_ref, o_ref):
        a = jnp.dot(ya_ref[...], wa_ref[...], preferred_element_type=f32).astype(bf16)
        b = jnp.dot(yh_ref[...], wh_ref[...], preferred_element_type=f32).astype(bf16)
        a_ref[...] = a
        b_ref[...] = b
        s0, s1 = _sigmoid(g0_ref[...].astype(f32)), _sigmoid(g1_ref[...].astype(f32))
        o_ref[...] = (s0 * a.astype(f32) + s1 * b.astype(f32)).astype(bf16)

    row = pl.BlockSpec((tm, D), lambda i: (i, 0))
    act = pl.BlockSpec((tm, ya.shape[1]), lambda i: (i, 0))
    wspec = pl.BlockSpec(w_ba.shape, lambda i: (0, 0))
    return pl.pallas_call(
        body,
        grid=(S // tm,),
        in_specs=[act, act, wspec, wspec, row, pl.BlockSpec((tm, D), lambda i: (i, 1))],
        out_specs=[row, row, row],
        out_shape=[SDS((S, D), bf16)] * 3,
        compiler_params=_cparams(("parallel",), VMEM_BIG),
        name="branch_gate_fwd",
    )(ya, yh, w_ba, w_bh, gc, gc)


def _gate_bwd(dmo, w_out, a, b, gc, w_ba, w_bh):
    S, D = a.shape
    W = w_ba.shape[0]
    tm = _pick(S, MM_ROWS)

    def body(dmo_ref, wo_ref, a_ref, b_ref, g0_ref, g1_ref, wa_ref, wh_ref,
             da_ref, db_ref, dg_ref, dya_ref, dyh_ref):
        dm = lax.dot_general(dmo_ref[...], wo_ref[...], NT, preferred_element_type=f32)
        dmv = dm.astype(bf16).astype(f32)
        s0, s1 = _sigmoid(g0_ref[...].astype(f32)), _sigmoid(g1_ref[...].astype(f32))
        da = (dmv * s0).astype(bf16)
        db = (dmv * s1).astype(bf16)
        da_ref[...] = da
        db_ref[...] = db
        dg_ref[:, :D] = (dmv * a_ref[...].astype(f32) * (s0 * (1.0 - s0))).astype(bf16)
        dg_ref[:, D:] = (dmv * b_ref[...].astype(f32) * (s1 * (1.0 - s1))).astype(bf16)
        dya_ref[...] = lax.dot_general(da, wa_ref[...], NT, preferred_element_type=f32)
        dyh_ref[...] = lax.dot_general(db, wh_ref[...], NT, preferred_element_type=f32)

    row = pl.BlockSpec((tm, D), lambda i: (i, 0))
    wide = pl.BlockSpec((tm, 2 * D), lambda i: (i, 0))
    narrow = pl.BlockSpec((tm, W), lambda i: (i, 0))
    whole = lambda t: pl.BlockSpec(t.shape, lambda i: (0, 0))
    return pl.pallas_call(
        body,
        grid=(S // tm,),
        in_specs=[row, whole(w_out), row, row, row, pl.BlockSpec((tm, D), lambda i: (i, 1)), whole(w_ba), whole(w_bh)],
        out_specs=[row, row, wide, narrow, narrow],
        out_shape=[SDS((S, D), bf16), SDS((S, D), bf16), SDS((S, 2 * D), bf16), SDS((S, W), f32), SDS((S, W), f32)],
        compiler_params=_cparams(("parallel",), VMEM_BIG),
        name="gate_bwd_fused",
    )(dmo, w_out, a, b, gc, gc, w_ba, w_bh)


CONV_ROWS = 512
INV_SQRT2 = 0.7071067811865476
INV_SQRT_2PI = 0.3989422804014327


CONV_HALO = 16


def _shift_down(cur, prev, k):
    x = pltpu.roll(cur, k, 0)
    row = lax.broadcasted_iota(jnp.int32, (SUBLANE, LANE), 0)
    head = jnp.where(row < k, pltpu.roll(prev, k, 0)[:SUBLANE], x[:SUBLANE])
    return jnp.concatenate([head, x[SUBLANE:]], axis=0)


def _shift_up(cur, nxt, k):
    R = cur.shape[0]
    x = pltpu.roll(cur, R - k, 0)
    row = lax.broadcasted_iota(jnp.int32, (SUBLANE, LANE), 0)
    tail = jnp.where(row >= SUBLANE - k, pltpu.roll(nxt, SUBLANE - k, 0), x[R - SUBLANE:])
    return jnp.concatenate([x[:R - SUBLANE], tail], axis=0)


def _conv_rows(u_ref, w, b, r0, first):
    R = CONV_ROWS
    cur = u_ref[pl.ds(r0, R), :].astype(f32)
    prev = u_ref[pl.ds(pl.multiple_of(jnp.maximum(r0 - CONV_HALO, 0), CONV_HALO), CONV_HALO), :].astype(f32)
    prev = jnp.where(first, 0.0, prev)
    x1 = _shift_down(cur, prev, 1)
    x2 = _shift_down(cur, prev, 2)
    c = ((b + w[0:1] * x2) + w[1:2] * x1) + w[2:3] * cur
    return c, x2, x1, cur


def _conv_fwd(ug, uv, wg, wv, bg, bv):
    S, F = ug.shape
    nchunk = S // CONV_ROWS

    def body(ug_ref, uv_ref, wg_ref, wv_ref, bg_ref, bv_ref, o_ref):
        wgv, wvv, bgv, bvv = wg_ref[...], wv_ref[...], bg_ref[...], bv_ref[...]

        def step(ci, carry):
            r0 = pl.multiple_of(ci * CONV_ROWS, CONV_ROWS)
            cg = _conv_rows(ug_ref, wgv, bgv, r0, ci == 0)[0]
            cv = _conv_rows(uv_ref, wvv, bvv, r0, ci == 0)[0]
            gelu = 0.5 * cg * (1.0 + lax.erf(cg * INV_SQRT2))
            o_ref[pl.ds(r0, CONV_ROWS), :] = (gelu * cv).astype(bf16)
            return carry

        lax.fori_loop(0, nchunk, step, 0)

    col = pl.BlockSpec((S, LANE), lambda j: (0, j))
    w3 = pl.BlockSpec((3, LANE), lambda j: (0, j))
    b1 = pl.BlockSpec((1, LANE), lambda j: (0, j))
    return pl.pallas_call(
        body,
        grid=(F // LANE,),
        in_specs=[col, col, w3, w3, b1, b1],
        out_specs=col,
        out_shape=SDS((S, F), bf16),
        compiler_params=_cparams(("parallel",), VMEM_BIG),
        name="conv_fwd",
    )(ug, uv, wg, wv, bg, bv)


def _conv_bwd(ug, uv, dact, wg, wv, bg, bv):
    S, F = ug.shape
    R = CONV_ROWS
    nchunk = S // R

    def body(ug_ref, uv_ref, da_ref, wg_ref, wv_ref, bg_ref, bv_ref, dug_ref, duv_ref, sg_ref, sv_ref, dcg, dcv):
        wgv, wvv, bgv, bvv = wg_ref[...], wv_ref[...], bg_ref[...], bv_ref[...]
        zero = jnp.zeros((SUBLANE, LANE), f32)

        def fwd_step(ci, acc):
            r0 = pl.multiple_of(ci * R, R)
            cg, g2, g1, g0 = _conv_rows(ug_ref, wgv, bgv, r0, ci == 0)
            cv, v2, v1, v0 = _conv_rows(uv_ref, wvv, bvv, r0, ci == 0)
            da = da_ref[pl.ds(r0, R), :].astype(f32)
            cdf = 0.5 * (1.0 + lax.erf(cg * INV_SQRT2))
            pdf = INV_SQRT_2PI * jnp.exp(-0.5 * cg * cg)
            dg = da * cv * (cdf + cg * pdf)
            dv = da * (cg * cdf)
            dcg[pl.ds(r0, R), :] = dg
            dcv[pl.ds(r0, R), :] = dv
            new = (acc[0] + _colsum8(dg * g2), acc[1] + _colsum8(dg * g1), acc[2] + _colsum8(dg * g0),
                   acc[3] + _colsum8(dg),
                   acc[4] + _colsum8(dv * v2), acc[5] + _colsum8(dv * v1), acc[6] + _colsum8(dv * v0),
                   acc[7] + _colsum8(dv))
            return new

        acc = lax.fori_loop(0, nchunk, fwd_step, (zero,) * 8)
        rows = lax.broadcasted_iota(jnp.int32, (SUBLANE, LANE), 0)

        def stats(parts):
            out = jnp.zeros((SUBLANE, LANE), f32)
            for k, pt in enumerate(parts):
                out = jnp.where(rows == k, jnp.sum(pt, axis=0, keepdims=True), out)
            return out

        sg_ref[...] = stats(acc[0:4])
        sv_ref[...] = stats(acc[4:8])

        def du_rows(dc, w, r0, last):
            cur = dc[pl.ds(r0, R), :]
            nxt = dc[pl.ds(pl.multiple_of(jnp.minimum(r0 + R, S - SUBLANE), SUBLANE), SUBLANE), :]
            nxt = jnp.where(last, 0.0, nxt)
            return w[2:3] * cur + w[1:2] * _shift_up(cur, nxt, 1) + w[0:1] * _shift_up(cur, nxt, 2)

        def bwd_step(ci, carry):
            r0 = pl.multiple_of(ci * R, R)
            last = ci == nchunk - 1
            dug_ref[pl.ds(r0, R), :] = du_rows(dcg, wgv, r0, last).astype(bf16)
            duv_ref[pl.ds(r0, R), :] = du_rows(dcv, wvv, r0, last).astype(bf16)
            return carry

        lax.fori_loop(0, nchunk, bwd_step, 0)

    col = pl.BlockSpec((S, LANE), lambda j: (0, j))
    w3 = pl.BlockSpec((3, LANE), lambda j: (0, j))
    b1 = pl.BlockSpec((1, LANE), lambda j: (0, j))
    st = pl.BlockSpec((SUBLANE, LANE), lambda j: (0, j))
    return pl.pallas_call(
        body,
        grid=(F // LANE,),
        in_specs=[col, col, col, w3, w3, b1, b1],
        out_specs=[col, col, st, st],
        out_shape=[SDS((S, F), bf16), SDS((S, F), bf16), SDS((SUBLANE, F), f32), SDS((SUBLANE, F), f32)],
        scratch_shapes=[pltpu.VMEM((S, LANE), f32), pltpu.VMEM((S, LANE), f32)],
        compiler_params=_cparams(("parallel",), VMEM_BIG),
        name="conv_bwd",
    )(ug, uv, dact, wg, wv, bg, bv)


def _adam_math(w, g, m, v):
    m = ADAM_B1 * m + (1.0 - ADAM_B1) * g
    v = ADAM_B2 * v + (1.0 - ADAM_B2) * (g * g)
    m_hat = m / (1.0 - ADAM_B1 ** ADAM_STEP)
    v_hat = v / (1.0 - ADAM_B2 ** ADAM_STEP)
    delta = -ADAM_LR * (m_hat / (jnp.sqrt(v_hat) + ADAM_EPS) + ADAM_WD * w)
    return delta, m, v


def _adamw(w, m, v, g, name):
    R, C = w.shape
    parts = g.ndim == 3
    tr = R
    if R % 16 == 0:
        for t in range(R, 0, -16):
            if R % t == 0 and t * C * 4 <= ADAM_BLOCK_BYTES:
                tr = t
                break

    def body(w_ref, m_ref, v_ref, g_ref, go_ref, d_ref, mo_ref, vo_ref):
        if parts:
            gv = ((g_ref[0].astype(f32) + g_ref[1].astype(f32)) + g_ref[2].astype(f32)) + g_ref[3].astype(f32)
        else:
            gv = g_ref[...]
        go_ref[...] = gv
        d, mn, vn = _adam_math(w_ref[...], gv, m_ref[...], v_ref[...])
        d_ref[...] = d
        mo_ref[...] = mn
        vo_ref[...] = vn

    row = pl.BlockSpec((tr, C), lambda i: (i, 0))
    gspec = pl.BlockSpec((4, tr, C), lambda i: (0, i, 0)) if parts else row
    return pl.pallas_call(
        body,
        grid=(R // tr,),
        in_specs=[row, row, row, gspec],
        out_specs=[row] * 4,
        out_shape=[SDS((R, C), f32)] * 4,
        compiler_params=_cparams(("parallel",), VMEM_BIG),
        name=name,
    )(w, m, v, g)


def _sum8(parts, name):
    _, _, R, C = parts.shape

    def body(p_ref, o_ref):
        acc = p_ref[0, 0]
        for c in range(2):
            for k in range(4):
                if c or k:
                    acc = acc + p_ref[c, k]
        o_ref[...] = acc

    return pl.pallas_call(body, out_shape=SDS((R, C), f32), name=name)(parts)


def _pair_add(by_core, b, name):
    _, K, R, C = by_core.shape
    tr = R // 2 if R % 32 == 0 else R

    def body(c_ref, a_ref, b_ref, o_ref):
        o_ref[...] = (a_ref[0].astype(f32) + b_ref[...].astype(f32)).astype(bf16)

    blk = pl.BlockSpec((1, tr, C), lambda k, i, c: (k, i, 0))
    return pl.pallas_call(
        body,
        grid_spec=pltpu.PrefetchScalarGridSpec(
            num_scalar_prefetch=1,
            grid=(K, R // tr),
            in_specs=[pl.BlockSpec((1, 1, tr, C), lambda k, i, c: (c[0], k, i, 0)), blk],
            out_specs=blk,
        ),
        out_shape=SDS((K, R, C), bf16),
        compiler_params=_cparams(("parallel", "parallel")),
        name=name,
    )(lax.axis_index("c").astype(jnp.int32).reshape(1), by_core, b)


_ANY = pl.BlockSpec(memory_space=pl.ANY)


def _chip_out_shape(src, gather):
    return SDS((4,) + tuple(src.shape if gather else src.shape[1:]), src.dtype)


def _fill_own(out, src, gather):
    mine = 2 * lax.axis_index("x") + lax.axis_index("y")
    own = src if gather else lax.dynamic_index_in_dim(src, mine, axis=0, keepdims=False)
    return lax.dynamic_update_index_in_dim(out, own, mine, axis=0)


_HBM = pl.BlockSpec(memory_space=pltpu.HBM)
_SEM = pl.BlockSpec(memory_space=pltpu.SEMAPHORE)
_EFFECT = pltpu.SideEffectType.DATAFLOW_SIDE_EFFECTING
_SPLIT_PEERS = {"chip_gather": 3, "chip_gather_wide": 3, "chip_xchg": 3, "core_fill": 4, "core_swap": 1}


def _split_land(src, kind):
    if kind == "chip_gather_wide":
        return SDS((4, 2) + tuple(src.shape), src.dtype)
    if kind == "core_fill":
        return SDS((SUBLANE, LANE), src.dtype)
    if kind == "core_swap":
        return SDS(tuple(src.shape[1:]), src.dtype)
    return _chip_out_shape(src, kind == "chip_gather")


def _split_copies(src_ref, land_ref, sems, kind):
    x, y, c = lax.axis_index("x"), lax.axis_index("y"), lax.axis_index("c")
    n = _SPLIT_PEERS[kind]
    if kind == "core_fill":
        routes = [((x, y, 1 - c), src_ref.at[k, c], src_ref.at[k, c], src_ref.at[k, 1 - c]) for k in range(n)]
    elif kind == "core_swap":
        routes = [((x, y, 1 - c), src_ref.at[1 - c], land_ref, land_ref)]
    else:
        mine = 2 * x + y
        gather = kind != "chip_xchg"
        slot = (lambda k: land_ref.at[k, c]) if kind == "chip_gather_wide" else (lambda k: land_ref.at[k])
        routes = [((px, py, c), src_ref if gather else src_ref.at[2 * px + py], slot(mine), slot(2 * px + py))
                  for px, py in [(1 - x, y), (x, 1 - y), (1 - x, 1 - y)]]
    sends, recvs = [], []
    for j, (peer, piece, there, here) in enumerate(routes):
        sends.append(pltpu.make_async_remote_copy(src_ref=piece, dst_ref=there, send_sem=sems[j],
                                                  recv_sem=sems[n + j], device_id=peer, device_id_type=MESH))
        recvs.append(pltpu.make_async_remote_copy(src_ref=piece, dst_ref=here, send_sem=sems[j],
                                                  recv_sem=sems[n + j], device_id=peer, device_id_type=MESH))
    return sends, recvs


def _split_start(src, kind, name, after=None):
    land = _split_land(src, kind)
    ns = 2 * _SPLIT_PEERS[kind]
    n_in = 2 if after is None else 3

    def body(*refs):
        src_ref, land_ref = refs[:2]
        outs = refs[n_in:]
        for cp in _split_copies(src_ref, land_ref, outs[:ns], kind)[0]:
            cp.start()
        token = outs[ns + 2]
        token[...] = jnp.zeros_like(token)

    res = pl.pallas_call(
        body,
        name=name,
        out_shape=(pltpu.SemaphoreType.DMA(()),) * ns
        + (pltpu.HBM(src.shape, src.dtype), pltpu.HBM(land.shape, land.dtype), SDS((SUBLANE, LANE), f32)),
        in_specs=(_HBM, _HBM) + (() if after is None else (_ANY,)),
        out_specs=(_SEM,) * ns + (_HBM, _HBM, pl.BlockSpec(memory_space=pltpu.VMEM)),
        input_output_aliases={0: ns, 1: ns + 1},
        compiler_params=pltpu.CompilerParams(has_side_effects=_EFFECT),
    )(pltpu.with_memory_space_constraint(src, pltpu.HBM),
      pltpu.with_memory_space_constraint(lax.empty(land.shape, land.dtype), pltpu.HBM),
      *(() if after is None else (after,)))
    return (res[:ns], res[ns], res[ns + 1]), res[ns + 2]


def _split_wait(state, after, kind, name):
    sems, src_thru, land_thru = state
    ns = 2 * _SPLIT_PEERS[kind]

    def body(src_ref, land_ref, *rest):
        sends, recvs = _split_copies(src_ref, land_ref, rest[:ns], kind)
        for cp in recvs:
            cp.wait_recv()
        for cp in sends:
            cp.wait_send()

    src_out, got = pl.pallas_call(
        body,
        name=name,
        out_shape=(pltpu.HBM(src_thru.shape, src_thru.dtype), pltpu.HBM(land_thru.shape, land_thru.dtype)),
        in_specs=(_HBM, _HBM) + (_SEM,) * ns + (_ANY,),
        out_specs=(_HBM, _HBM),
        input_output_aliases={0: 0, 1: 1},
        compiler_params=pltpu.CompilerParams(has_side_effects=_EFFECT),
    )(src_thru, land_thru, *sems, after)
    if kind == "core_swap":
        return got, src_out
    if kind == "core_fill":
        return src_out
    if kind == "chip_gather_wide":
        mine = 2 * lax.axis_index("x") + lax.axis_index("y")
        zero = jnp.zeros((), mine.dtype)
        return lax.dynamic_update_slice(got, src_out[None, None], (mine, lax.axis_index("c").astype(mine.dtype))
                                        + (zero,) * src_out.ndim)
    return _fill_own(got, src_out, kind == "chip_gather")


def _core_fill(both, name):
    n = both.shape[0]

    def body(in_ref, out_ref, send_sems, recv_sems):
        x, y, c = lax.axis_index("x"), lax.axis_index("y"), lax.axis_index("c")
        sends = [pltpu.make_async_remote_copy(src_ref=out_ref.at[k, c], dst_ref=out_ref.at[k, c],
                                              send_sem=send_sems.at[k], recv_sem=recv_sems.at[k],
                                              device_id=(x, y, 1 - c), device_id_type=MESH) for k in range(n)]
        recvs = [pltpu.make_async_remote_copy(src_ref=out_ref.at[k, c], dst_ref=out_ref.at[k, 1 - c],
                                              send_sem=send_sems.at[k], recv_sem=recv_sems.at[k],
                                              device_id=(x, y, 1 - c), device_id_type=MESH) for k in range(n)]
        for cp in sends:
            cp.start()
        for cp in recvs:
            cp.wait_recv()
        for cp in sends:
            cp.wait_send()

    return pl.pallas_call(
        body,
        in_specs=[_ANY],
        out_specs=_ANY,
        out_shape=SDS(both.shape, both.dtype),
        scratch_shapes=[pltpu.SemaphoreType.DMA((n,)), pltpu.SemaphoreType.DMA((n,))],
        input_output_aliases={0: 0},
        name=name,
    )(both)


def _core_gather(src, name):
    def body(src_ref, out_ref, send_sem, recv_sem):
        x, y, c = lax.axis_index("x"), lax.axis_index("y"), lax.axis_index("c")
        cp = pltpu.make_async_remote_copy(src_ref=src_ref, dst_ref=out_ref.at[c], send_sem=send_sem,
                                          recv_sem=recv_sem, device_id=(x, y, 1 - c), device_id_type=MESH)
        cp.start()
        pltpu.make_async_remote_copy(src_ref=src_ref, dst_ref=out_ref.at[1 - c], send_sem=send_sem,
                                     recv_sem=recv_sem, device_id=(x, y, 1 - c), device_id_type=MESH).wait_recv()
        cp.wait_send()

    out = pl.pallas_call(
        body,
        in_specs=[_ANY],
        out_specs=_ANY,
        out_shape=SDS((2,) + tuple(src.shape), src.dtype),
        scratch_shapes=[pltpu.SemaphoreType.DMA, pltpu.SemaphoreType.DMA],
        name=name,
    )(src)
    return lax.dynamic_update_index_in_dim(out, src, lax.axis_index("c"), axis=0)


_PACK_A = (("w_in", (1088, 1024)),)
_PACK_B = (("w_ba", (512, 128)), ("w_bh", (512, 128)), ("w_out", (128, 1024)), ("w_up", (704, 1024)),
           ("w_down", (352, 1024)))
_PACK_SIZES = _PACK_A + _PACK_B
_TRANSPOSED = ("w_in", "w_up")


def _slab_rows(sizes):
    return sum(r * c for _, (r, c) in sizes) // D_MODEL


def _pack_rows(d, sizes):
    n = d[sizes[0][0]].shape[0]
    return jnp.concatenate([d[k].reshape(n, -1, D_MODEL) for k, _ in sizes], axis=1)


def _unpack_rows(slab, sizes):
    n = slab.shape[0]
    out, lo = {}, 0
    for key, (r, c) in sizes:
        rows = r * c // D_MODEL
        out[key] = slab[:, lo:lo + rows].reshape(n, r, c)
        lo += rows
    return out


def _by_core(gslab):
    return jnp.swapaxes(gslab.reshape((4, 2) + gslab.shape[1:]), 0, 1)


def _cols_to_full(t):
    return jnp.swapaxes(t, 0, 1).reshape(t.shape[1], -1)


def _full_to_cols(t):
    K = t.shape[0]
    return jnp.swapaxes(t.reshape(K, 8, -1), 0, 1)


_SMALL = (("pre_mix_norm", (1, 1024)), ("rel_bias", (32, 24)), ("hgrn_lb_raw", (2, 512)), ("hgrn_norm", (1, 128)),
          ("post_mix_norm", (1, 1024)), ("pre_ffn_norm", (1, 1024)), ("conv_b", (1, 5632)),
          ("post_ffn_norm", (1, 1024)))
_SMALL_ROWS = 96
_CONVW_ROWS = 136


_SMALL_USED = sum(r * c for _, (r, c) in _SMALL)


def _pack_small(d, extra=None):
    flat = jnp.concatenate([d[k].reshape(-1) for k, _ in _SMALL] + ([] if extra is None else [extra.reshape(-1)]))
    flat = jnp.pad(flat, (0, _SMALL_ROWS * LANE - flat.shape[0]))
    return flat.reshape(_SMALL_ROWS, LANE)


def _unpack_small(p):
    flat = p.reshape(-1)
    out, lo = {}, 0
    for k, shp in _SMALL:
        n = shp[0] * shp[1]
        out[k] = flat[lo:lo + n].reshape(shp)
        lo += n
    return out


def _local_step(x, tgt, P, plan):
    S = x.shape[0]
    P = dict(P)
    lb = _lb_fwd(P["hgrn_lb_raw"])
    hs = _prep(x, P["pre_mix_norm"], plan.start_token())
    h1 = hs[0]
    consts = [_bias_consts(d, plan.start_token()) for d in DILATIONS]
    biases, dep = [], h1
    for g in range(N_GROUPS):
        tab_t = P["rel_bias"][:, 8 * g:8 * g + 8].T
        dep = _bias_build(tab_t, consts[g][0], consts[g][1], f"bias_build{g}", dep)
        biases.append(dep.reshape(8, ATTN_BLOCK, 2 * ATTN_BLOCK))
    W = dict(plan.weights_a(dep))
    qkv0, hg, gc = _mm_fanout(h1, [W["wt_qkv"][0], W["wt_hg"], W["wt_gate"]], "nt", [bf16, f32, bf16], "proj_natural")
    qkv = [qkv0] + [_mm(hs[g], W["wt_qkv"][g], "nt", bf16, f"proj_qkv{g}") for g in (1, 2)]
    obuf, lbuf, token = [], [], None
    for g, d in enumerate(DILATIONS):
        o_g, l_g = _attn_fwd(qkv[g], biases[g], (S // d) // ATTN_BLOCK, f"attn_fwd{g}", after=token)
        lbuf.append(l_g)
        obuf.append(o_g)
        if g == 0:
            token = plan.forward_b(o_g)
    y_attn, y_attn_b, w0, w1, w2 = _attn_merge(obuf[0], obuf[1], obuf[2], lbuf[0], lbuf[1], lbuf[2])
    y_hgrn, o_raw, ck = _hgrn_fwd(hg, lb, P["hgrn_norm"])
    wb = plan.weights_b(y_hgrn)
    P["conv_w"] = wb.pop("conv_w")
    W.update(wb)
    a, b, merged = _gate_fwd(y_attn_b, y_hgrn, W["w_ba"], W["w_bh"], gc)
    mo, x1, h2 = _mid_fwd(x, merged, W["w_out"], P["post_mix_norm"], P["pre_ffn_norm"])
    ug, uv = _mm_fanout(h2, [W["wt_up_g"], W["wt_up_v"]], "nt", [bf16, bf16], "up_proj")
    cw_g, cw_v = P["conv_w"][:, :D_FF], P["conv_w"][:, D_FF:]
    cb_g, cb_v = P["conv_b"][:, :D_FF], P["conv_b"][:, D_FF:]
    act = _conv_fwd(ug, uv, cw_g, cw_v, cb_g, cb_v)
    loss, dy, dfo, g_post_ffn = _final(x1, act, W["w_down"], tgt, P["post_ffn_norm"])
    gW_down = _mm(act, dfo, "tn", bf16, "gw_down")
    dact = _mm(dfo, W["w_down"], "nt", bf16, "d_act")
    dug, duv, st_g, st_v = _conv_bwd(ug, uv, dact, cw_g, cw_v, cb_g, cb_v)
    gW_up_g = _mm(dug, h2, "tn", bf16, "gw_up_gate")
    gW_up_v = _mm(duv, h2, "tn", bf16, "gw_up_val")
    dx1, dmo, g_pre_ffn, g_post_mix = _mid_bwd(dy, dug, duv, W["wt_up_g"], W["wt_up_v"], x1, mo, P["pre_ffn_norm"],
                                               P["post_mix_norm"])
    gW_out = _mm(merged, dmo, "tn", bf16, "gw_out")
    da, db, dgc, dyattn, dyhgrn = _gate_bwd(dmo, W["w_out"], a, b, gc, W["w_ba"], W["w_bh"])
    gW_ba = _mm(y_attn_b, da, "tn", bf16, "gw_ba")
    gW_bh = _mm(y_hgrn, db, "tn", bf16, "gw_bh")
    big_b = dict(w_ba=gW_ba, w_bh=gW_bh, w_out=gW_out, w_up=[gW_up_g, gW_up_v], w_down=gW_down)
    dos = _attn_merge_bwd(dyattn, y_attn, w0, w1, w2, after=plan.grads_b_start(big_b))
    dq_h, df_h, dv_h, dog_h, glb8, gnw8 = _hgrn_bwd(hg, o_raw, dyhgrn, ck, lb, P["hgrn_norm"],
                                                   after=plan.grads_b_exchange(dos[5]))
    dhg = [dq_h, df_h, dv_h, dog_h]
    g_lb_raw = _lb_bwd(P["hgrn_lb_raw"], glb8[0:1])
    gn = gnw8[0:1]
    g_hgrn_norm = (gn[:, 0:128] + gn[:, 128:256]) + (gn[:, 256:384] + gn[:, 384:512])
    dqkvs, gW_qkv, g_rel = [], [], []
    for g, d in enumerate(DILATIONS):
        dq, dk, dv, dbias = _attn_bwd(qkv[g], biases[g], dos[g], dos[3 + g], lbuf[g], (S // d) // ATTN_BLOCK,
                                      f"attn_bwd{g}")
        dqkvs.append([dq, dk, dv])
        gW_qkv.append(_mm(dqkvs[g], hs[g], "tn", bf16, f"gw_qkv{g}"))
        g_rel.append(_bias_grad(dbias.reshape(8, -1), consts[g][0], f"bias_grad{g}"))
    gW_hg = _mm(dhg, h1, "tn", bf16, "gw_hg")
    gW_gate = _mm(dgc, h1, "tn", bf16, "gw_gate")
    gW_in = gW_qkv + [gW_hg, gW_gate]
    token = plan.grads_a_start(gW_in)
    dh_perm = [_mm(dqkvs[g], W["wt_qkv"][g], "nn", f32, f"dh1_qkv{g}", after=token) for g in (1, 2)]
    token = plan.grads_a_exchange(dh_perm[1])
    dh_main = _mm(dqkvs[0] + dhg + [dgc], [W["wt_qkv"][0], W["wt_hg"], W["wt_gate"]], "nn", f32, "dh1_main",
                  after=token)
    grad_x, g_pre_mix = _first_bwd(x, dx1, dh_main, dh_perm[0], dh_perm[1], P["pre_mix_norm"])

    g_conv_w = jnp.concatenate([st_g[0:3], st_v[0:3]], axis=1)
    g_conv_b = jnp.concatenate([st_g[3:4], st_v[3:4]], axis=1)
    small = dict(pre_mix_norm=g_pre_mix, rel_bias=jnp.concatenate(g_rel, axis=1), hgrn_lb_raw=g_lb_raw,
                 hgrn_norm=g_hgrn_norm, post_mix_norm=g_post_mix, pre_ffn_norm=g_pre_ffn, conv_b=g_conv_b,
                 post_ffn_norm=g_post_ffn, conv_w=g_conv_w)
    return loss, grad_x, gW_in, big_b, small


def _weights_a(both):
    wt = both.reshape(-1, D_MODEL)
    return dict(
        wt_qkv=[wt[g * QKV_G:(g + 1) * QKV_G] for g in range(N_GROUPS)],
        wt_hg=wt[3 * QKV_G:3 * QKV_G + 4 * HGRN_W],
        wt_gate=wt[3 * QKV_G + 4 * HGRN_W:],
    )


def _weights_b(slabs):
    sh = _unpack_rows(slabs, _PACK_B)
    wt_up = sh["w_up"].reshape(-1, D_MODEL)
    return dict(
        w_ba=_cols_to_full(sh["w_ba"]),
        w_bh=_cols_to_full(sh["w_bh"]),
        w_out=sh["w_out"].reshape(D_MODEL, D_MODEL),
        wt_up_g=wt_up[:D_FF],
        wt_up_v=wt_up[D_FF:],
        w_down=sh["w_down"].reshape(D_FF, D_MODEL),
    )


def _dest_rows(sections, height):
    out = []
    for j in range(8):
        lo, hi, off, pieces = j * height, (j + 1) * height, 0, []
        for s in sections:
            a, b = max(lo, off), min(hi, off + s.shape[0])
            if a < b:
                pieces.append(s[a - off:b - off])
            off += s.shape[0]
        out.append(pieces[0] if len(pieces) == 1 else jnp.concatenate(pieces, axis=0))
    return out


def _grad_blocks_a(sections):
    rows = _dest_rows(sections, 1088)
    return jnp.stack([jnp.stack([rows[2 * k + c].astype(bf16) for k in range(4)]) for c in range(2)])


def _grad_slab_b(g):
    shards = dict(w_ba=_full_to_cols(g["w_ba"]), w_bh=_full_to_cols(g["w_bh"]), w_out=g["w_out"].reshape(8, 128, D_MODEL),
                  w_up=jnp.stack(_dest_rows(g["w_up"], 704)), w_down=g["w_down"].reshape(8, 352, D_MODEL))
    return _pack_rows({k: v.astype(bf16) for k, v in shards.items()}, _PACK_B)


_CONVW_SLAB_ROWS = 16


class _Traffic:
    def __init__(self, slab_a, slab_b, conv_w):
        hi = conv_w.astype(bf16)
        r1 = conv_w - hi.astype(f32)
        mid = r1.astype(bf16)
        lo = (r1 - mid.astype(f32)).astype(bf16)
        bits = jnp.stack([hi, mid, lo]).reshape(-1)
        tail = jnp.pad(bits, (0, _CONVW_SLAB_ROWS * D_MODEL - bits.shape[0])).reshape(_CONVW_SLAB_ROWS, D_MODEL)
        self.slab_b = jnp.concatenate([slab_b, tail], axis=0)
        self.state_a, tok = _split_start(slab_a, "chip_gather_wide", "ag_a_start")
        self.state_b, self.token = _split_start(self.slab_b, "chip_gather_wide", "ag_b_start", after=tok)
        self.state = None
        self.state_gb = None

    def start_token(self):
        return self.token

    def weights_a(self, after):
        half = _split_wait(self.state_a, after, "chip_gather_wide", "ag_a_wait")
        return _weights_a(_core_fill(half, "ag_a_cores"))

    def forward_b(self, after):
        half = _split_wait(self.state_b, after, "chip_gather_wide", "ag_b_wait")
        self.state, token = _split_start(half, "core_fill", "ag_b_cores_start")
        return token

    def weights_b(self, after):
        both = _split_wait(self.state, after, "core_fill", "ag_b_cores_wait")
        slabs = both.reshape((8,) + tuple(self.slab_b.shape))
        rows = _slab_rows(_PACK_B)
        out = _weights_b(slabs[:, :rows])
        pieces = slabs[:, rows:].reshape(8, -1)[:, :3 * 3 * 704].reshape(8, 3, 3, 704).astype(f32)
        out["conv_w"] = _cols_to_full((pieces[:, 0] + pieces[:, 1]) + pieces[:, 2])
        return out

    def grads_b_start(self, grads):
        self.state, token = _split_start(_by_core(_grad_slab_b(grads)), "core_swap", "rs_b_cores_start")
        return token

    def grads_b_exchange(self, after):
        from_sib, by_core = _split_wait(self.state, after, "core_swap", "rs_b_cores_wait")
        self.state_gb, token = _split_start(_pair_add(by_core, from_sib, "rs_b_pair_add"), "chip_xchg", "rs_b_start")
        return token

    def grads_a_start(self, sections):
        self.state, token = _split_start(_grad_blocks_a(sections), "core_swap", "rs_a_cores_start")
        return token

    def grads_a_exchange(self, after):
        from_sib, by_core = _split_wait(self.state, after, "core_swap", "rs_a_cores_wait")
        self.state, token = _split_start(_pair_add(by_core, from_sib, "rs_a_pair_add"), "chip_xchg", "rs_a_start")
        return token

    def parts(self, after):
        parts = _unpack_rows(_split_wait(self.state_gb, after, "chip_xchg", "rs_b_wait"), _PACK_B)
        parts["w_in"] = _split_wait(self.state, after, "chip_xchg", "rs_a_wait")
        return parts


def kernel(x, pre_mix_norm, w_in, rel_bias, hgrn_lb_raw, hgrn_norm, w_branch_attn, w_branch_hgrn, w_out, post_mix_norm, pre_ffn_norm, w_up, conv_w, conv_b, w_down, post_ffn_norm, loss_target, m_pre_mix_norm, m_w_in, m_rel_bias, m_hgrn_lb_raw, m_hgrn_norm, m_w_branch_attn, m_w_branch_hgrn, m_w_out, m_post_mix_norm, m_pre_ffn_norm, m_w_up, m_conv_w, m_conv_b, m_w_down, m_post_ffn_norm, v_pre_mix_norm, v_w_in, v_rel_bias, v_hgrn_lb_raw, v_hgrn_norm, v_w_branch_attn, v_w_branch_hgrn, v_w_out, v_post_mix_norm, v_pre_ffn_norm, v_w_up, v_conv_w, v_conv_b, v_w_down, v_post_ffn_norm):
    ci = lax.axis_index("c")
    dev = 4 * lax.axis_index("x") + 2 * lax.axis_index("y") + ci
    tr = lambda t: jnp.swapaxes(t[0], 0, 1)
    wts = dict(w_in=tr(w_in), w_ba=w_branch_attn[0], w_bh=w_branch_hgrn[0], w_out=w_out[0], w_up=tr(w_up),
               w_down=w_down[0])
    mom = dict(w_in=tr(m_w_in), w_ba=m_w_branch_attn[0], w_bh=m_w_branch_hgrn[0], w_out=m_w_out[0], w_up=tr(m_w_up),
               w_down=m_w_down[0])
    var = dict(w_in=tr(v_w_in), w_ba=v_w_branch_attn[0], w_bh=v_w_branch_hgrn[0], w_out=v_w_out[0], w_up=tr(v_w_up),
               w_down=v_w_down[0])
    small_w = dict(pre_mix_norm=pre_mix_norm, rel_bias=rel_bias, hgrn_lb_raw=hgrn_lb_raw, hgrn_norm=hgrn_norm,
                   post_mix_norm=post_mix_norm, pre_ffn_norm=pre_ffn_norm, conv_b=conv_b, post_ffn_norm=post_ffn_norm)
    small_m = dict(pre_mix_norm=m_pre_mix_norm, rel_bias=m_rel_bias, hgrn_lb_raw=m_hgrn_lb_raw, hgrn_norm=m_hgrn_norm,
                   post_mix_norm=m_post_mix_norm, pre_ffn_norm=m_pre_ffn_norm, conv_b=m_conv_b,
                   post_ffn_norm=m_post_ffn_norm)
    small_v = dict(pre_mix_norm=v_pre_mix_norm, rel_bias=v_rel_bias, hgrn_lb_raw=v_hgrn_lb_raw, hgrn_norm=v_hgrn_norm,
                   post_mix_norm=v_post_mix_norm, pre_ffn_norm=v_pre_ffn_norm, conv_b=v_conv_b,
                   post_ffn_norm=v_post_ffn_norm)

    plan = _Traffic(wts["w_in"].astype(bf16),
                    _pack_rows({k: wts[k].astype(bf16)[None] for k, _ in _PACK_B}, _PACK_B)[0], conv_w[0])

    loss8, grad_x, _, _, small = _local_step(x[0], loss_target[0], small_w, plan)
    spack = jnp.concatenate([_pack_small(small, loss8[0, 0:1]),
                             jnp.pad(small["conv_w"].reshape(-1, LANE), ((0, _CONVW_ROWS - 132), (0, 0)))], axis=0)
    small_state, token = _split_start(spack, "chip_gather", "ag_small_start")

    parts = plan.parts(token)
    outs_big = {}
    for k, _ in _PACK_SIZES:
        outs_big[k] = _adamw(wts[k], mom[k], var[k], parts[k], "adamw_" + k)

    by_chip = _split_wait(small_state, outs_big["w_in"][1], "chip_gather", "ag_small_wait")
    allp = _core_gather(by_chip, "ag_small_cores")
    ssum = _sum8(allp, "small_sum")
    gs = ssum[:_SMALL_ROWS]
    loss = ssum[_SMALL_USED // LANE, _SMALL_USED % LANE]
    res_small = _adamw(_pack_small(small_w), _pack_small(small_m), _pack_small(small_v), gs, "adamw_small")
    sm = [_unpack_small(t) for t in res_small]
    g_cw_full = ssum[_SMALL_ROWS:_SMALL_ROWS + 132].reshape(3, 2 * D_FF)
    g_cw = lax.dynamic_slice_in_dim(g_cw_full, dev * 704, 704, axis=1)
    res_cw = _adamw(conv_w[0], m_conv_w[0], v_conv_w[0], g_cw, "adamw_conv_w")

    def pick(i):
        def big_(k):
            t = outs_big[k][i]
            return (jnp.swapaxes(t, 0, 1) if k in _TRANSPOSED else t)[None]
        return [sm[i]["pre_mix_norm"], big_("w_in"), sm[i]["rel_bias"], sm[i]["hgrn_lb_raw"], sm[i]["hgrn_norm"],
                big_("w_ba"), big_("w_bh"), big_("w_out"), sm[i]["post_mix_norm"], sm[i]["pre_ffn_norm"],
                big_("w_up"), res_cw[i][None], sm[i]["conv_b"], big_("w_down"), sm[i]["post_ffn_norm"]]

    return (loss, grad_x[None], *pick(0), *pick(1), *pick(2), *pick(3))
```

```python
import functools
import math

import jax
import jax.numpy as jnp
from jax import lax
from jax.experimental import pallas as pl
from jax.experimental.pallas import tpu as pltpu

f32 = jnp.float32
bf16 = jnp.bfloat16
SDS = jax.ShapeDtypeStruct
HIGHEST = lax.Precision.HIGHEST
MESH = pl.DeviceIdType.MESH

NN = (((1,), (0,)), ((), ()))
NT = (((1,), (1,)), ((), ()))
TN = (((0,), (0,)), ((), ()))

D_MODEL = 1024
N_GROUPS = 3
DILATIONS = (1, 4, 16)
HEAD_DIM = 64
ATTN_BLOCK = 128
QKV_G = 1536
ATTN_OUT = 512
HGRN_W = 512
HGRN_CHUNK = 32
D_FF = 2816
NUM_BUCKETS = 32
MAX_EXACT = 16
MAX_DISTANCE = 2048
NEG_INF = -1e30
EPS = 1e-6
LANE = 128
SUBLANE = 8
VMEM_BIG = 48 * 1024 * 1024
MM_ROWS = 512
MM_OUT_BYTES = 8 * 1024 * 1024
ADAM_BLOCK_BYTES = 2304 * 1024

ADAM_LR, ADAM_B1, ADAM_B2, ADAM_EPS, ADAM_WD, ADAM_STEP = 0.001, 0.9, 0.999, 1e-08, 0.01, 10


def _pick(n, pref):
    t = pref
    while t >= LANE:
        if n % t == 0:
            return t
        t //= 2
    return n


def _cparams(sem=None, vmem=None):
    kw = {}
    if sem is not None:
        kw["dimension_semantics"] = sem
    if vmem is not None:
        kw["vmem_limit_bytes"] = vmem
    return pltpu.CompilerParams(**kw)


def _sigmoid(x):
    return jax.nn.sigmoid(x)


def _colsum8(x):
    return x.reshape(x.shape[0] // SUBLANE, SUBLANE, x.shape[1]).sum(axis=0)


class _Rows:
    def __init__(self, full, lo, rows):
        self.full, self.lo, self.shape = full, lo, (rows, full.shape[1])


def _resident(t):
    if isinstance(t, _Rows):
        return pl.BlockSpec((pl.Element(t.shape[0]), pl.Element(t.shape[1])), lambda i: (t.lo, 0)), t.full
    return pl.BlockSpec(t.shape, lambda i: (0, 0)), t


def _mm(a, b, mode, out_dtype, name, acc=None, after=None):
    dims = {"nn": NN, "nt": NT, "tn": TN}[mode]
    has_acc = acc is not None
    parts = list(a) if isinstance(a, (list, tuple)) else [a]
    if mode == "tn":
        assert not has_acc
        K, N = b.shape
        widths = [t.shape[1] for t in parts]
        M = sum(widths)
        whole = M * N * 4 <= MM_OUT_BYTES
        assert whole or len(parts) == 1
        tmm = M if whole else M // 2
        ts = _pick(K, 4 * MM_ROWS)
        nk = K // ts

        npart = len(parts)
        narrow = out_dtype != f32

        def body_tn(*refs):
            b_ref, o_ref = refs[npart], refs[npart + 1]
            acc_ref = refs[npart + 2] if narrow else o_ref
            k = pl.program_id(1)
            bv = b_ref[...]
            lo = 0
            for a_ref, w in zip(refs[:npart], widths if whole else [tmm]):
                part = lax.dot_general(a_ref[...], bv, dims, preferred_element_type=f32)
                rows = slice(lo, lo + w)
                lo += w

                @pl.when(k == 0)
                def _(part=part, rows=rows):
                    acc_ref[rows, :] = part

                @pl.when(k > 0)
                def _(part=part, rows=rows):
                    acc_ref[rows, :] += part

            if narrow:
                @pl.when(k == nk - 1)
                def _():
                    o_ref[...] = acc_ref[...].astype(out_dtype)

        return pl.pallas_call(
            body_tn,
            grid=(M // tmm, nk),
            in_specs=[pl.BlockSpec((ts, w if whole else tmm), lambda i, k: (k, i)) for w in widths]
            + [pl.BlockSpec((ts, N), lambda i, k: (k, 0))],
            out_specs=pl.BlockSpec((tmm, N), lambda i, k: (i, 0)),
            out_shape=SDS((M, N), out_dtype),
            scratch_shapes=[pltpu.VMEM((tmm, N), f32)] if narrow else [],
            compiler_params=_cparams(("parallel", "arbitrary"), VMEM_BIG),
            name=name,
        )(*parts, b)

    bs = list(b) if isinstance(b, (list, tuple)) else [b]
    widths = [t.shape[1] for t in parts]
    M = parts[0].shape[0]
    kdim = 0 if mode == "nn" else 1
    N = bs[0].shape[1 - kdim]
    tm = _pick(M, MM_ROWS)
    npart, nb = len(parts), len(bs)
    place, bi, lo = [], 0, 0
    for w in widths:
        place.append((bi, lo))
        lo += w
        if lo == bs[bi].shape[kdim]:
            bi, lo = bi + 1, 0
    assert bi == nb and lo == 0

    def body(*refs):
        a_refs, b_refs = refs[:npart], refs[npart:npart + nb]
        c_ref = refs[npart + nb] if has_acc else None
        o_ref = refs[-1]
        part = None
        for a_ref, w, (bi, lo) in zip(a_refs, widths, place):
            b_ref = b_refs[bi]
            if w == bs[bi].shape[kdim]:
                bk = b_ref[...]
            else:
                bk = b_ref[:, lo:lo + w] if mode == "nt" else b_ref[lo:lo + w, :]
            t = lax.dot_general(a_ref[...], bk, dims, preferred_element_type=f32)
            part = t if part is None else part + t
        if has_acc:
            part = part + c_ref[...]
        o_ref[...] = part.astype(out_dtype)

    specs = [pl.BlockSpec((tm, w), lambda i: (i, 0)) for w in widths] + [_resident(t)[0] for t in bs]
    args = parts + [_resident(t)[1] for t in bs]
    aliases = {}
    if has_acc:
        specs.append(pl.BlockSpec((tm, N), lambda i: (i, 0)))
        args.append(acc)
        aliases = {npart + nb: 0}
    if after is not None:
        specs.append(pl.BlockSpec(memory_space=pl.ANY))
        args.append(after)
    return pl.pallas_call(
        body,
        grid=(M // tm,),
        in_specs=specs,
        out_specs=pl.BlockSpec((tm, N), lambda i: (i, 0)),
        out_shape=SDS((M, N), out_dtype),
        input_output_aliases=aliases,
        compiler_params=_cparams(("parallel",), VMEM_BIG),
        name=name,
    )(*args)


def _mm_fanout(a, bs, mode, out_dtypes, name):
    dims = {"nn": NN, "nt": NT}[mode]
    M, K = a.shape
    ns = [b.shape[1] if mode == "nn" else b.shape[0] for b in bs]
    tm = _pick(M, MM_ROWS)
    nb = len(bs)

    def body(a_ref, *refs):
        av = a_ref[...]
        for b_ref, o_ref, dt in zip(refs[:nb], refs[nb:], out_dtypes):
            o_ref[...] = lax.dot_general(av, b_ref[...], dims, preferred_element_type=f32).astype(dt)

    return pl.pallas_call(
        body,
        grid=(M // tm,),
        in_specs=[pl.BlockSpec((tm, K), lambda i: (i, 0))] + [_resident(b)[0] for b in bs],
        out_specs=[pl.BlockSpec((tm, n), lambda i: (i, 0)) for n in ns],
        out_shape=[SDS((M, n), dt) for n, dt in zip(ns, out_dtypes)],
        compiler_params=_cparams(("parallel",), VMEM_BIG),
        name=name,
    )(a, *[_resident(b)[1] for b in bs])


PERM_ROWS = 2048


def _perm_spec(d, cols=LANE):
    return pl.BlockSpec((d, PERM_ROWS // d, cols), lambda i, j: (0, i, j))


def _to_natural(src_ref, dst_ref, d):
    n = src_ref.shape[1]
    for r in range(d):
        dst_ref[pl.ds(r, n, stride=d), :] = src_ref[r]


def _prep(x, w, after=None):
    S, D = x.shape
    R = PERM_ROWS
    nc = D // LANE
    n_in = nc + 1 + (after is not None)

    def body(*refs):
        x_refs, w_ref = refs[:nc], refs[nc]
        h_ref, h4_ref, h16_ref, rs = refs[n_in:]
        ssq = None
        for xr in x_refs:
            v = xr[...]
            t = jnp.sum(v * v, axis=-1, keepdims=True)
            ssq = t if ssq is None else ssq + t
        rinv = lax.rsqrt(ssq * (1.0 / D) + EPS)
        rs[...] = jnp.broadcast_to(rinv, (R, LANE))
        for j, xr in enumerate(x_refs):
            cols = slice(j * LANE, (j + 1) * LANE)
            wj = w_ref[:, cols]
            h_ref[:, cols] = ((xr[...] * rinv) * wj).astype(bf16)
            for d, o_ref in ((4, h4_ref), (16, h16_ref)):
                n = R // d
                for r in range(d):
                    rows = pl.ds(r, n, stride=d)
                    o_ref[r, :, cols] = ((xr[rows, :] * rs[rows, :]) * wj).astype(bf16)

    col = lambda j: pl.BlockSpec((R, LANE), lambda i, j=j: (i, j))
    h, h4, h16 = pl.pallas_call(
        body,
        grid=(S // R,),
        in_specs=[col(j) for j in range(nc)] + [pl.BlockSpec((1, D), lambda i: (0, 0))]
        + ([] if after is None else [pl.BlockSpec(memory_space=pl.ANY)]),
        out_specs=[pl.BlockSpec((R, D), lambda i: (i, 0)), pl.BlockSpec((4, R // 4, D), lambda i: (0, i, 0)),
                   pl.BlockSpec((16, R // 16, D), lambda i: (0, i, 0))],
        out_shape=[SDS((S, D), bf16), SDS((4, S // 4, D), bf16), SDS((16, S // 16, D), bf16)],
        scratch_shapes=[pltpu.VMEM((R, LANE), f32)],
        compiler_params=_cparams(("parallel",), VMEM_BIG),
        name="prep_norm_perm",
    )(*([x] * nc), w, *([] if after is None else [after]))
    return [h, h4.reshape(S, D), h16.reshape(S, D)]


def _rms_parts(xv):
    r = lax.rsqrt(jnp.mean(xv * xv, axis=-1, keepdims=True) + EPS)
    return r, xv * r


def _rms_bwd(xhat, r, w, dy):
    dyw = dy * w
    return r * (dyw - xhat * jnp.mean(dyw * xhat, axis=-1, keepdims=True))


def _mid_fwd(x, merged, w_out, w_pm, w_pf):
    S, D = x.shape
    tm = _pick(S, MM_ROWS)

    def body(x_ref, m_ref, wo_ref, wpm_ref, wpf_ref, mo_ref, x1_ref, h2_ref):
        mo = jnp.dot(m_ref[...], wo_ref[...], preferred_element_type=f32)
        mo_ref[...] = mo
        _, moh = _rms_parts(mo)
        x1 = x_ref[...] + moh * wpm_ref[...]
        x1_ref[...] = x1
        _, x1h = _rms_parts(x1)
        h2_ref[...] = (x1h * wpf_ref[...]).astype(bf16)

    row = pl.BlockSpec((tm, D), lambda i: (i, 0))
    vec = pl.BlockSpec((1, D), lambda i: (0, 0))
    return pl.pallas_call(
        body,
        grid=(S // tm,),
        in_specs=[row, pl.BlockSpec((tm, merged.shape[1]), lambda i: (i, 0)),
                  pl.BlockSpec(w_out.shape, lambda i: (0, 0)), vec, vec],
        out_specs=[row, row, row],
        out_shape=[SDS((S, D), f32), SDS((S, D), f32), SDS((S, D), bf16)],
        compiler_params=_cparams(("parallel",), VMEM_BIG),
        name="out_proj_mid_fwd",
    )(x, merged, w_out, w_pm, w_pf)


def _final(x1, act, w_down, tgt, w_pfn):
    S, D = x1.shape
    tm = _pick(S, MM_ROWS)
    nt = S // tm

    def body(x1_ref, a_ref, wd_ref, t_ref, w_ref, loss_ref, dy_ref, dfo_ref, gw_ref, lacc, gacc):
        i = pl.program_id(0)

        @pl.when(i == 0)
        def _():
            lacc[...] = jnp.zeros_like(lacc)
            gacc[...] = jnp.zeros_like(gacc)

        w = w_ref[...]
        r, foh = _rms_parts(jnp.dot(a_ref[...], wd_ref[...], preferred_element_type=f32))
        y = x1_ref[...] + foh * w
        err = y - t_ref[...]
        lacc[...] += _colsum8(err * err)
        dy = err * (1.0 / D)
        dy_ref[...] = dy
        gacc[...] += _colsum8(dy * foh)
        dfo_ref[...] = _rms_bwd(foh, r, w, dy).astype(bf16)

        @pl.when(i == nt - 1)
        def _():
            loss_ref[...] = jnp.full((SUBLANE, LANE), 0.5 / D, f32) * jnp.sum(lacc[...])
            gw_ref[...] = jnp.sum(gacc[...], axis=0, keepdims=True)

    row = pl.BlockSpec((tm, D), lambda i: (i, 0))
    vec = pl.BlockSpec((1, D), lambda i: (0, 0))
    return pl.pallas_call(
        body,
        grid=(nt,),
        in_specs=[row, pl.BlockSpec((tm, act.shape[1]), lambda i: (i, 0)),
                  pl.BlockSpec(w_down.shape, lambda i: (0, 0)), row, vec],
        out_specs=[pl.BlockSpec((SUBLANE, LANE), lambda i: (0, 0)), row, row, vec],
        out_shape=[SDS((SUBLANE, LANE), f32), SDS((S, D), f32), SDS((S, D), bf16), SDS((1, D), f32)],
        scratch_shapes=[pltpu.VMEM((SUBLANE, D), f32), pltpu.VMEM((SUBLANE, D), f32)],
        compiler_params=_cparams(("arbitrary",), VMEM_BIG),
        name="down_proj_final_loss",
    )(x1, act, w_down, tgt, w_pfn)


MID_BWD_ROWS = 256


def _mid_bwd(dy, dug, duv, wt_g, wt_v, x1, mo, w_pf, w_pm):
    S, D = dy.shape
    tm = _pick(S, MID_BWD_ROWS)
    nt = S // tm

    def body(dy_ref, dug_ref, duv_ref, wg_ref, wv_ref, x1_ref, mo_ref, wpf_ref, wpm_ref,
             dx1_ref, dmo_ref, gpf_ref, gpm_ref, apf, apm):
        i = pl.program_id(0)

        @pl.when(i == 0)
        def _():
            apf[...] = jnp.zeros_like(apf)
            apm[...] = jnp.zeros_like(apm)

        r1, x1h = _rms_parts(x1_ref[...])
        dh2 = jnp.dot(dug_ref[...], wg_ref[...], preferred_element_type=f32) \
            + jnp.dot(duv_ref[...], wv_ref[...], preferred_element_type=f32)
        apf[...] += _colsum8(dh2 * x1h)
        dx1 = dy_ref[...] + _rms_bwd(x1h, r1, wpf_ref[...], dh2)
        dx1_ref[...] = dx1
        rm, moh = _rms_parts(mo_ref[...])
        apm[...] += _colsum8(dx1 * moh)
        dmo_ref[...] = _rms_bwd(moh, rm, wpm_ref[...], dx1).astype(bf16)

        @pl.when(i == nt - 1)
        def _():
            gpf_ref[...] = jnp.sum(apf[...], axis=0, keepdims=True)
            gpm_ref[...] = jnp.sum(apm[...], axis=0, keepdims=True)

    row = pl.BlockSpec((tm, D), lambda i: (i, 0))
    vec = pl.BlockSpec((1, D), lambda i: (0, 0))
    return pl.pallas_call(
        body,
        grid=(nt,),
        in_specs=[row, pl.BlockSpec((tm, dug.shape[1]), lambda i: (i, 0)), pl.BlockSpec((tm, duv.shape[1]), lambda i: (i, 0)),
                  pl.BlockSpec(wt_g.shape, lambda i: (0, 0)), pl.BlockSpec(wt_v.shape, lambda i: (0, 0)),
                  row, row, vec, vec],
        out_specs=[row, row, vec, vec],
        out_shape=[SDS((S, D), f32), SDS((S, D), bf16), SDS((1, D), f32), SDS((1, D), f32)],
        scratch_shapes=[pltpu.VMEM((SUBLANE, D), f32), pltpu.VMEM((SUBLANE, D), f32)],
        compiler_params=_cparams(("arbitrary",), VMEM_BIG),
        name="dh2_mid_bwd",
    )(dy, dug, duv, wt_g, wt_v, x1, mo, w_pf, w_pm)


def _first_bwd(x, dx1, dh_a, dh_b, dh_c, w_pre):
    S, D = x.shape
    tm = _pick(S, 512)
    nt = S // tm
    nc = D // LANE

    def body(*refs):
        x_ref, dx1_ref, a_ref = refs[:3]
        b_refs, c_refs, w_ref = refs[3:3 + nc], refs[3 + nc:3 + 2 * nc], refs[3 + 2 * nc]
        gx_ref, gw_ref, acc, dh_s, sb, sc = refs[4 + 2 * nc:]
        i = pl.program_id(0)

        @pl.when(i == 0)
        def _():
            acc[...] = jnp.zeros_like(acc)

        for j in range(nc):
            cols = slice(j * LANE, (j + 1) * LANE)
            _to_natural(b_refs[j], sb, 4)
            _to_natural(c_refs[j], sc, 16)
            dh_s[:, cols] = (a_ref[:, cols] + sb[...]) + sc[...]
        r, xh = _rms_parts(x_ref[...])
        dh = dh_s[...]
        acc[...] += _colsum8(dh * xh)
        gx_ref[...] = dx1_ref[...] + _rms_bwd(xh, r, w_ref[...], dh)

        @pl.when(i == nt - 1)
        def _():
            gw_ref[...] = jnp.sum(acc[...], axis=0, keepdims=True)

    row = pl.BlockSpec((tm, D), lambda i: (i, 0))
    vec = pl.BlockSpec((1, D), lambda i: (0, 0))
    perm = lambda d: [pl.BlockSpec((d, tm // d, LANE), lambda i, j=j: (0, i, j)) for j in range(nc)]
    return pl.pallas_call(
        body,
        grid=(nt,),
        in_specs=[row, row, row] + perm(4) + perm(16) + [vec],
        out_specs=[row, vec],
        out_shape=[SDS((S, D), f32), SDS((1, D), f32)],
        scratch_shapes=[pltpu.VMEM((SUBLANE, D), f32), pltpu.VMEM((tm, D), f32), pltpu.VMEM((tm, LANE), f32),
                        pltpu.VMEM((tm, LANE), f32)],
        compiler_params=_cparams(("arbitrary",), VMEM_BIG),
        name="first_bwd",
    )(x, dx1, dh_a, *([dh_b.reshape(4, S // 4, D)] * nc), *([dh_c.reshape(16, S // 16, D)] * nc), w_pre)


def _t5_bucket(dist):
    n = jnp.maximum(dist, 0)
    nf = jnp.maximum(n, 1).astype(f32)
    large = MAX_EXACT + (jnp.log(nf / MAX_EXACT) / math.log(MAX_DISTANCE / MAX_EXACT)
                         * (NUM_BUCKETS - MAX_EXACT)).astype(jnp.int32)
    large = jnp.minimum(large, NUM_BUCKETS - 1)
    return jnp.where(n < MAX_EXACT, n, large)


def _bias_consts(d, after=None):
    if after is not None:
        d, _ = lax.optimization_barrier((jnp.int32(d), after))
    blk = ATTN_BLOCK
    rel = jnp.arange(blk)[:, None] + blk - jnp.arange(2 * blk)[None, :]
    in_win = (rel >= 0) & (rel <= blk)
    bucket = _t5_bucket(rel * d).reshape(1, -1)
    onehot = (bucket == jnp.arange(NUM_BUCKETS)[:, None]).astype(f32)
    return onehot, in_win.astype(f32).reshape(1, -1)


def _bias_build(tab_t, onehot, maskf, name, after):
    H = tab_t.shape[0]

    def body(t_ref, oh_ref, m_ref, after_ref, o_ref):
        b = jnp.dot(t_ref[...], oh_ref[...], precision=HIGHEST, preferred_element_type=f32)
        o_ref[...] = jnp.where(m_ref[...] > 0.5, b, NEG_INF)

    vm = pl.BlockSpec(memory_space=pltpu.VMEM)
    return pl.pallas_call(body, out_shape=SDS((H, onehot.shape[1]), f32), name=name,
                          in_specs=[vm, vm, vm, pl.BlockSpec(memory_space=pl.ANY)], out_specs=vm,
                          )(tab_t, onehot, maskf, after)


def _bias_grad(dbias_flat, onehot, name):
    H = dbias_flat.shape[0]

    def body(g_ref, oh_ref, o_ref):
        o_ref[...] = lax.dot_general(oh_ref[...], g_ref[...], NT, precision=HIGHEST, preferred_element_type=f32)

    return pl.pallas_call(body, out_shape=SDS((NUM_BUCKETS, H), f32), name=name)(dbias_flat, onehot)


ATTN_TILE = 512
ATTN_SUB = ATTN_TILE // ATTN_BLOCK
ATTN_HP = 4
ATTN_WIDE = ATTN_HP * LANE


def _qkv_specs(nt):
    tile = (ATTN_TILE, ATTN_WIDE)
    blk = (ATTN_BLOCK, ATTN_WIDE)
    sec = ATTN_OUT // ATTN_WIDE
    cur = lambda off: (lambda h, t: (jnp.minimum(t, nt - 1), off + h))
    prev = lambda off: (lambda h, t: (jnp.maximum(jnp.minimum(t, nt - 1) * ATTN_SUB - 1, 0), off + h))
    return [pl.BlockSpec(tile, cur(0)), pl.BlockSpec(blk, prev(sec)), pl.BlockSpec(tile, cur(sec)),
            pl.BlockSpec(blk, prev(2 * sec)), pl.BlockSpec(tile, cur(2 * sec))]


def _head_masks():
    lane = lax.broadcasted_iota(jnp.int32, (ATTN_BLOCK, LANE), 1)
    return lane < HEAD_DIM


def _stack_heads(x2, low):
    zero = jnp.zeros_like(x2)
    return jnp.concatenate([jnp.where(low, x2, zero), jnp.where(low, zero, x2)], axis=0)


def _attn_fwd(qkv, bias, bps, name, after=None):
    S = qkv.shape[0]
    nt = S // ATTN_TILE
    scale = HEAD_DIM ** -0.5

    def body(q_ref, kp_ref, kc_ref, vp_ref, vc_ref, b_ref, *rest):
        o_ref, l_ref = rest[-2:]
        t = pl.program_id(1)
        low = _head_masks()
        col = lax.broadcasted_iota(jnp.int32, (2 * ATTN_BLOCK, 2 * ATTN_BLOCK), 1)
        for hp in range(ATTN_HP):
            cols = slice(hp * LANE, (hp + 1) * LANE)
            kk = jnp.concatenate([kp_ref[:, cols], kc_ref[:, cols]], axis=0)
            vv = jnp.concatenate([vp_ref[:, cols], vc_ref[:, cols]], axis=0)
            bias2 = b_ref[2 * hp:2 * hp + 2].reshape(2 * ATTN_BLOCK, 2 * ATTN_BLOCK)
            for b in range(ATTN_SUB):
                lo = b * ATTN_BLOCK
                rows = slice(lo, lo + ATTN_BLOCK)
                keys = slice(lo, lo + 2 * ATTN_BLOCK)
                dead = jnp.logical_and((t * ATTN_SUB + b) % bps == 0, col < ATTN_BLOCK)
                q2 = _stack_heads(q_ref[rows, cols], low)
                kb, vb = kk[keys], vv[keys]
                s = lax.dot_general(q2, kb, NT, preferred_element_type=f32) * scale + bias2
                s = jnp.where(dead, NEG_INF, s)
                m = jnp.max(s, axis=-1, keepdims=True)
                p = jnp.exp(s - m)
                l = jnp.sum(p, axis=-1, keepdims=True)
                o2 = jnp.dot(p.astype(bf16), vb, preferred_element_type=f32) / l
                lse = m + jnp.log(l)
                o_ref[rows, cols] = jnp.where(low, o2[:ATTN_BLOCK], o2[ATTN_BLOCK:])
                l_ref[rows, cols] = jnp.where(low, lse[:ATTN_BLOCK], lse[ATTN_BLOCK:])

    tile = pl.BlockSpec((ATTN_TILE, ATTN_WIDE), lambda h, t: (t, h))
    return pl.pallas_call(
        body,
        grid=(4 // ATTN_HP, nt),
        in_specs=_qkv_specs(nt) + [pl.BlockSpec((2 * ATTN_HP, ATTN_BLOCK, 2 * ATTN_BLOCK), lambda h, t: (h, 0, 0))]
        + ([] if after is None else [pl.BlockSpec(memory_space=pl.ANY)]),
        out_specs=[tile, tile],
        out_shape=[SDS((S, ATTN_OUT), f32), SDS((S, ATTN_OUT), f32)],
        compiler_params=_cparams(("parallel", "parallel")),
        name=name,
    )(qkv, qkv, qkv, qkv, qkv, bias, *([] if after is None else [after]))


def _attn_bwd(qkv, bias, do, dvec, lse, bps, name):
    S = qkv.shape[0]
    nt = S // ATTN_TILE
    scale = HEAD_DIM ** -0.5

    def assemble(parts):
        rows = [parts[0][:ATTN_BLOCK]]
        for b in range(ATTN_SUB - 1):
            rows.append(parts[b][ATTN_BLOCK:] + parts[b + 1][:ATTN_BLOCK])
        rows.append(parts[-1][ATTN_BLOCK:])
        return rows

    def body(q_ref, kp_ref, kc_ref, vp_ref, vc_ref, b_ref, do_ref, dvec_ref, lse_ref,
             dq_ref, dk_ref, dv_ref, db_ref, ck, cv):
        t = pl.program_id(1)
        last = ATTN_TILE - ATTN_BLOCK

        @pl.when(t == 0)
        def _():
            ck[...] = jnp.zeros_like(ck)
            cv[...] = jnp.zeros_like(cv)
            db_ref[...] = jnp.zeros_like(db_ref)

        @pl.when(t < nt)
        def _():
            low = _head_masks()
            col = lax.broadcasted_iota(jnp.int32, (2 * ATTN_BLOCK, 2 * ATTN_BLOCK), 1)
            per_row = lambda t2: jnp.concatenate([t2[:, 0:1], t2[:, HEAD_DIM:HEAD_DIM + 1]], axis=0)
            for hp in range(ATTN_HP):
                cols = slice(hp * LANE, (hp + 1) * LANE)
                kk = jnp.concatenate([kp_ref[:, cols], kc_ref[:, cols]], axis=0)
                vv = jnp.concatenate([vp_ref[:, cols], vc_ref[:, cols]], axis=0)
                bias2 = b_ref[2 * hp:2 * hp + 2].reshape(2 * ATTN_BLOCK, 2 * ATTN_BLOCK)
                dk_parts, dv_parts = [], []
                dsum = None
                for b in range(ATTN_SUB):
                    lo = b * ATTN_BLOCK
                    rows = slice(lo, lo + ATTN_BLOCK)
                    keys = slice(lo, lo + 2 * ATTN_BLOCK)
                    dead = jnp.logical_and((t * ATTN_SUB + b) % bps == 0, col < ATTN_BLOCK)
                    q2 = _stack_heads(q_ref[rows, cols], low)
                    do2 = _stack_heads(do_ref[rows, cols].astype(bf16), low)
                    kb, vb = kk[keys], vv[keys]
                    s = lax.dot_general(q2, kb, NT, preferred_element_type=f32) * scale + bias2
                    s = jnp.where(dead, NEG_INF, s)
                    p = jnp.exp(s - per_row(lse_ref[rows, cols]))
                    dp = lax.dot_general(do2, vb, NT, preferred_element_type=f32)
                    ds = p * (dp - per_row(dvec_ref[rows, cols]))
                    dsum = ds if dsum is None else dsum + ds
                    dsb = ds.astype(bf16)
                    dq2 = jnp.dot(dsb, kb, preferred_element_type=f32) * scale
                    dq_ref[rows, cols] = jnp.where(low, dq2[:ATTN_BLOCK], dq2[ATTN_BLOCK:]).astype(bf16)
                    dk_parts.append(lax.dot_general(dsb, q2, TN, preferred_element_type=f32) * scale)
                    dv_parts.append(lax.dot_general(p.astype(bf16), do2, TN, preferred_element_type=f32))
                db_ref[2 * hp:2 * hp + 2] += dsum.reshape(2, ATTN_BLOCK, 2 * ATTN_BLOCK)
                for parts, carry, out_ref in ((dk_parts, ck, dk_ref), (dv_parts, cv, dv_ref)):
                    rws = assemble(parts)
                    out_ref[:last, cols] = carry[:last, cols].astype(bf16)
                    out_ref[last:, cols] = (carry[last:, cols] + rws[0]).astype(bf16)
                    for b in range(ATTN_SUB):
                        carry[b * ATTN_BLOCK:(b + 1) * ATTN_BLOCK, cols] = rws[b + 1]

        @pl.when(t == nt)
        def _():
            dk_ref[...] = ck[...].astype(bf16)
            dv_ref[...] = cv[...].astype(bf16)

    tile = (ATTN_TILE, ATTN_WIDE)
    cur = pl.BlockSpec(tile, lambda h, t: (jnp.minimum(t, nt - 1), h))
    lag = pl.BlockSpec(tile, lambda h, t: (jnp.maximum(t - 1, 0), h))
    bspec = pl.BlockSpec((2 * ATTN_HP, ATTN_BLOCK, 2 * ATTN_BLOCK), lambda h, t: (h, 0, 0))
    return pl.pallas_call(
        body,
        grid=(4 // ATTN_HP, nt + 1),
        in_specs=_qkv_specs(nt) + [bspec, cur, cur, cur],
        out_specs=[cur, lag, lag, bspec],
        out_shape=[SDS((S, ATTN_OUT), bf16), SDS((S, ATTN_OUT), bf16), SDS((S, ATTN_OUT), bf16),
                   SDS((8, ATTN_BLOCK, 2 * ATTN_BLOCK), f32)],
        scratch_shapes=[pltpu.VMEM(tile, f32), pltpu.VMEM(tile, f32)],
        compiler_params=_cparams(("parallel", "arbitrary")),
        name=name,
    )(qkv, qkv, qkv, qkv, qkv, bias, do, dvec, lse)


def _attn_merge(o0, o1, o2, l0, l1, l2):
    S, W = o0.shape
    R = PERM_ROWS

    def body(o0_ref, o1_ref, o2_ref, l0_ref, l1_ref, l2_ref, y_ref, yb_ref, w0_ref, w1_ref, w2_ref,
             so1, so2, sl1, sl2):
        _to_natural(o1_ref, so1, 4)
        _to_natural(l1_ref, sl1, 4)
        _to_natural(o2_ref, so2, 16)
        _to_natural(l2_ref, sl2, 16)
        a, b, c = l0_ref[...], sl1[...], sl2[...]
        m = jnp.maximum(jnp.maximum(a, b), c)
        ea, eb, ec = jnp.exp(a - m), jnp.exp(b - m), jnp.exp(c - m)
        den = (ea + eb) + ec
        w0, w1, w2 = ea / den, eb / den, ec / den
        y = (w0 * o0_ref[...] + w1 * so1[...]) + w2 * so2[...]
        y_ref[...] = y
        yb_ref[...] = y.astype(bf16)
        w0_ref[...] = w0
        w1_ref[...] = w1
        w2_ref[...] = w2

    nat = pl.BlockSpec((R, LANE), lambda i, j: (i, j))
    v4 = lambda t: t.reshape(4, S // 4, W)
    v16 = lambda t: t.reshape(16, S // 16, W)
    return pl.pallas_call(
        body,
        grid=(S // R, W // LANE),
        in_specs=[nat, _perm_spec(4), _perm_spec(16)] * 2,
        out_specs=[nat] * 5,
        out_shape=[SDS((S, W), f32), SDS((S, W), bf16)] + [SDS((S, W), f32)] * 3,
        scratch_shapes=[pltpu.VMEM((R, LANE), f32)] * 4,
        compiler_params=_cparams(("parallel", "parallel"), VMEM_BIG),
        name="attn_merge",
    )(o0, v4(o1), v16(o2), l0, v4(l1), v16(l2))


def _attn_merge_bwd(dy, y, w0, w1, w2, after=None):
    S, W = dy.shape
    R = PERM_ROWS

    def body(dy_ref, y_ref, w0_ref, w1_ref, w2_ref, *rest):
        a0, a1, a2, b0, b1, b2, sa, sb = rest[-8:]
        dyv = dy_ref[...]
        r = lax.broadcasted_iota(jnp.int32, (LANE, LANE), 0) // HEAD_DIM
        c = lax.broadcasted_iota(jnp.int32, (LANE, LANE), 1) // HEAD_DIM
        seg = jnp.where(r == c, 1.0, 0.0).astype(f32)
        cbar = jnp.dot(dyv * y_ref[...], seg, precision=HIGHEST, preferred_element_type=f32)
        w = w0_ref[...]
        a0[...] = (w * dyv).astype(bf16)
        b0[...] = w * cbar
        for d, w_ref, a_ref, b_ref in ((4, w1_ref, a1, b1), (16, w2_ref, a2, b2)):
            w = w_ref[...]
            sa[...] = w * dyv
            sb[...] = w * cbar
            n = R // d
            for k in range(d):
                rows = pl.ds(k, n, stride=d)
                a_ref[k] = sa[rows, :].astype(bf16)
                b_ref[k] = sb[rows, :]

    nat = pl.BlockSpec((R, LANE), lambda i, j: (i, j))
    shapes = lambda dt: [SDS((S, W), dt), SDS((4, S // 4, W), dt), SDS((16, S // 16, W), dt)]
    outs = pl.pallas_call(
        body,
        grid=(S // R, W // LANE),
        in_specs=[nat] * 5 + ([] if after is None else [pl.BlockSpec(memory_space=pl.ANY)]),
        out_specs=[nat, _perm_spec(4), _perm_spec(16)] * 2,
        out_shape=shapes(bf16) + shapes(f32),
        scratch_shapes=[pltpu.VMEM((R, LANE), f32)] * 2,
        compiler_params=_cparams(("parallel", "parallel"), VMEM_BIG),
        name="attn_merge_bwd",
    )(dy, y, w0, w1, w2, *([] if after is None else [after]))
    return [t.reshape(S, W) for t in outs]


HGRN_SB = 256
HGRN_PAIR = 4


def _chunk_masks():
    r = jnp.arange(HGRN_SB)[:, None]
    c = jnp.arange(HGRN_SB)[None, :]
    same = (r // HGRN_CHUNK) == (c // HGRN_CHUNK)
    return jnp.stack([same & (c <= r), same, same & (c >= r)]).astype(bf16)


def _mask_dot(mask, x):
    hi = x.astype(bf16)
    r1 = x - hi.astype(f32)
    mid = r1.astype(bf16)
    lo = (r1 - mid.astype(f32)).astype(bf16)
    p = jnp.dot(mask, jnp.concatenate([hi, mid, lo], axis=1), preferred_element_type=f32)
    n = x.shape[1]
    return (p[:, :n] + p[:, n:2 * n]) + p[:, 2 * n:]


def _hgrn_prep(q_raw, f_raw, lbv, tril, same):
    sq = _sigmoid(q_raw)
    qs = q_raw * sq
    sig = _sigmoid(f_raw)
    f = lbv + (1.0 - lbv) * sig
    g = jnp.log(f)
    k = 1.0 - f
    G = _mask_dot(tril, g)
    GL = _mask_dot(same, g)
    eG = jnp.exp(G)
    einv = jnp.exp(-G)
    edec = jnp.exp(GL - G)
    return dict(sq=sq, qs=qs, sig=sig, f=f, k=k, eG=eG, einv=einv, edec=edec, eGL=jnp.exp(GL),
                qt=qs * eG, kt=k * einv, kd=k * edec)


def _hgrn_fwd(hg, lb, normw):
    S = hg.shape[0]
    sb = HGRN_SB
    nsb = S // sb
    nch = sb // HGRN_CHUNK

    def body(q_ref, f_ref, v_ref, og_ref, lb_ref, nw_ref, m_ref, y_ref, o_ref, ck_ref, st):
        j = pl.program_id(1)

        @pl.when(j == 0)
        def _():
            st[...] = jnp.zeros_like(st)

        tril_m = m_ref[0]
        tril = tril_m.astype(f32) > 0.5

        def one_head(hh):
            cols = slice(hh * LANE, (hh + 1) * LANE)
            ST = st[hh]
            ck_ref[hh, 0] = ST
            pr = _hgrn_prep(q_ref[:, cols], f_ref[:, cols], lb_ref[:, cols], tril_m, m_ref[1])
            qtb, ktb, kdb = pr["qt"].astype(bf16), pr["kt"].astype(bf16), pr["kd"].astype(bf16)
            eGL = pr["eGL"]
            vb = v_ref[:, cols].astype(bf16)
            A = jnp.where(tril, lax.dot_general(qtb, ktb, NT, preferred_element_type=f32), 0.0)
            o = jnp.dot(A.astype(bf16), vb, preferred_element_type=f32)
            outs = []
            for ci in range(nch):
                lo = ci * HGRN_CHUNK
                sl = slice(lo, lo + HGRN_CHUNK)
                outs.append(o[sl] + lax.dot_general(qtb[sl], ST.astype(bf16), NT, preferred_element_type=f32))
                ST = ST * eGL[lo:lo + 1, :] + lax.dot_general(vb[sl], kdb[sl], TN, preferred_element_type=f32)
            st[hh] = ST
            of = jnp.concatenate(outs, axis=0)
            o_ref[:, cols] = of
            rms = lax.rsqrt(jnp.mean(of * of, axis=-1, keepdims=True) + EPS)
            ogv = og_ref[:, cols]
            y_ref[:, cols] = ((of * rms * nw_ref[...]) * (ogv * _sigmoid(ogv))).astype(bf16)

        for hh in range(HGRN_PAIR):
            one_head(hh)

    wide = HGRN_PAIR * LANE
    col = lambda off: pl.BlockSpec((sb, wide), lambda h, j: (j, off // HGRN_PAIR + h))
    return pl.pallas_call(
        body,
        grid=(4 // HGRN_PAIR, nsb),
        in_specs=[col(0), col(4), col(8), col(12), pl.BlockSpec((1, wide), lambda h, j: (0, h)),
                  pl.BlockSpec((1, LANE), lambda h, j: (0, 0)),
                  pl.BlockSpec((3, sb, sb), lambda h, j: (0, 0, 0))],
        out_specs=[col(0), col(0), pl.BlockSpec((HGRN_PAIR, 1, LANE, LANE), lambda h, j: (h, j, 0, 0))],
        out_shape=[SDS((S, HGRN_W), bf16), SDS((S, HGRN_W), f32), SDS((4, nsb, LANE, LANE), f32)],
        scratch_shapes=[pltpu.VMEM((HGRN_PAIR, LANE, LANE), f32)],
        compiler_params=_cparams(("parallel", "arbitrary")),
        name="hgrn_fwd",
    )(hg, hg, hg, hg, lb, normw, _chunk_masks())


def _hgrn_bwd(hg, o_raw, dy, ck, lb, normw, after=None):
    S = hg.shape[0]
    sb = HGRN_SB
    nsb = S // sb
    nch = sb // HGRN_CHUNK

    def body(q_ref, f_ref, v_ref, og_ref, o_ref, dy_ref, ck_ref, lb_ref, nw_ref, m_ref, *rest):
        dq_ref, df_ref, dv_ref, dog_ref, glb_ref, gnw_ref, dst, alb, anw = rest[-9:]
        j = pl.program_id(1)

        @pl.when(j == 0)
        def _():
            dst[...] = jnp.zeros_like(dst)
            alb[...] = jnp.zeros_like(alb)
            anw[...] = jnp.zeros_like(anw)

        tril_m = m_ref[0]
        tril = tril_m.astype(f32) > 0.5
        nw = nw_ref[...]

        def one_head(hh):
            cols = slice(hh * LANE, (hh + 1) * LANE)
            lbv = lb_ref[:, cols]
            q_raw = q_ref[:, cols]
            pr = _hgrn_prep(q_raw, f_ref[:, cols], lbv, tril_m, m_ref[1])
            qt, kt, kd, eGL = pr["qt"], pr["kt"], pr["kd"], pr["eGL"]
            qtb, ktb, kdb = qt.astype(bf16), kt.astype(bf16), kd.astype(bf16)
            vb = v_ref[:, cols].astype(bf16)

            o = o_ref[:, cols]
            ogv = og_ref[:, cols]
            sog = _sigmoid(ogv)
            rms = lax.rsqrt(jnp.mean(o * o, axis=-1, keepdims=True) + EPS)
            oh = o * rms
            dyv = dy_ref[:, cols]
            dog_ref[:, cols] = (dyv * (oh * nw) * (sog * (1.0 + ogv * (1.0 - sog)))).astype(bf16)
            dohw = dyv * (ogv * sog)
            anw[:, cols] += _colsum8(dohw * oh)
            doh = dohw * nw
            do = rms * (doh - oh * jnp.mean(doh * oh, axis=-1, keepdims=True))
            dob = do.astype(bf16)

            Ab = jnp.where(tril, lax.dot_general(qtb, ktb, NT, preferred_element_type=f32), 0.0).astype(bf16)
            dAb = jnp.where(tril, lax.dot_general(dob, vb, NT, preferred_element_type=f32), 0.0).astype(bf16)
            dv_acc = lax.dot_general(Ab, dob, TN, preferred_element_type=f32)
            dqt = jnp.dot(dAb, ktb, preferred_element_type=f32)
            dkt = lax.dot_general(dAb, qtb, TN, preferred_element_type=f32)

            ST = ck_ref[hh, 0]
            states = []
            for ci in range(nch):
                lo = ci * HGRN_CHUNK
                sl = slice(lo, lo + HGRN_CHUNK)
                states.append(ST)
                ST = ST * eGL[lo:lo + 1, :] + lax.dot_general(vb[sl], kdb[sl], TN, preferred_element_type=f32)

            dST = dst[hh]
            dqt_i, dkd_i, dv_i, deg_i = [None] * nch, [None] * nch, [None] * nch, [None] * nch
            for ci in reversed(range(nch)):
                lo = ci * HGRN_CHUNK
                sl = slice(lo, lo + HGRN_CHUNK)
                ST0 = states[ci]
                dSTb = dST.astype(bf16)
                dv_i[ci] = lax.dot_general(kdb[sl], dSTb, NT, preferred_element_type=f32)
                dqt_i[ci] = jnp.dot(dob[sl], ST0.astype(bf16), preferred_element_type=f32)
                dkd_i[ci] = jnp.dot(vb[sl], dSTb, preferred_element_type=f32)
                deg_i[ci] = jnp.broadcast_to(jnp.sum(dST * ST0, axis=0, keepdims=True), (HGRN_CHUNK, LANE))
                dST = dST * eGL[lo:lo + 1, :] + lax.dot_general(dob[sl], qtb[sl], TN, preferred_element_type=f32)
            dst[hh] = dST

            dqt = dqt + jnp.concatenate(dqt_i, axis=0)
            dkd = jnp.concatenate(dkd_i, axis=0)
            dv_ref[:, cols] = (dv_acc + jnp.concatenate(dv_i, axis=0)).astype(bf16)
            deg = jnp.concatenate(deg_i, axis=0)

            dqs = dqt * pr["eG"]
            dkdkd = dkd * kd
            dG = dqt * qt - dkt * kt - dkdkd
            dk = dkt * pr["einv"] + dkd * pr["edec"]
            dGL = _mask_dot(m_ref[1], dkdkd) + eGL * deg
            dg = _mask_dot(m_ref[2], dG) + dGL
            df = dg / pr["f"] - dk
            sig = pr["sig"]
            df_ref[:, cols] = (df * (1.0 - lbv) * (sig * (1.0 - sig))).astype(bf16)
            alb[:, cols] += _colsum8(df * (1.0 - sig))
            sq = pr["sq"]
            dq_ref[:, cols] = (dqs * (sq * (1.0 + q_raw * (1.0 - sq)))).astype(bf16)

        for hh in range(HGRN_PAIR):
            one_head(hh)

        @pl.when(j == nsb - 1)
        def _():
            glb_ref[...] = jnp.broadcast_to(jnp.sum(alb[...], axis=0, keepdims=True), (SUBLANE, wide))
            gnw_ref[...] = jnp.broadcast_to(jnp.sum(anw[...], axis=0, keepdims=True), (SUBLANE, wide))

    wide = HGRN_PAIR * LANE
    rev = lambda off: pl.BlockSpec((sb, wide), lambda h, j: (nsb - 1 - j, off // HGRN_PAIR + h))
    stat = pl.BlockSpec((SUBLANE, wide), lambda h, j: (0, h))
    return pl.pallas_call(
        body,
        grid=(4 // HGRN_PAIR, nsb),
        in_specs=[rev(0), rev(4), rev(8), rev(12), rev(0), rev(0),
                  pl.BlockSpec((HGRN_PAIR, 1, LANE, LANE), lambda h, j: (h, nsb - 1 - j, 0, 0)),
                  pl.BlockSpec((1, wide), lambda h, j: (0, h)), pl.BlockSpec((1, LANE), lambda h, j: (0, 0)),
                  pl.BlockSpec((3, sb, sb), lambda h, j: (0, 0, 0))]
        + ([] if after is None else [pl.BlockSpec(memory_space=pl.ANY)]),
        out_specs=[rev(0), rev(0), rev(0), rev(0), stat, stat],
        out_shape=[SDS((S, HGRN_W), bf16)] * 4 + [SDS((SUBLANE, HGRN_W), f32)] * 2,
        scratch_shapes=[pltpu.VMEM((HGRN_PAIR, LANE, LANE), f32), pltpu.VMEM((SUBLANE, wide), f32),
                        pltpu.VMEM((SUBLANE, wide), f32)],
        compiler_params=_cparams(("parallel", "arbitrary")),
        name="hgrn_bwd",
    )(hg, hg, hg, hg, o_raw, dy, ck, lb, normw, _chunk_masks(), *([] if after is None else [after]))


def _lb_fwd(raw):
    def body(r_ref, o_ref):
        r = r_ref[...]
        m = jnp.max(r, axis=0, keepdims=True)
        e = jnp.exp(r - m)
        o_ref[...] = (e / jnp.sum(e, axis=0, keepdims=True))[0:1]

    return pl.pallas_call(body, out_shape=SDS((1, raw.shape[1]), f32), name="lb_fwd")(raw)


def _lb_bwd(raw, dlb):
    def body(r_ref, d_ref, o_ref):
        r = r_ref[...]
        m = jnp.max(r, axis=0, keepdims=True)
        e = jnp.exp(r - m)
        s = e / jnp.sum(e, axis=0, keepdims=True)
        s0 = s[0:1]
        onehot0 = jnp.where(lax.broadcasted_iota(jnp.int32, r.shape, 0) == 0, 1.0, 0.0)
        o_ref[...] = d_ref[...] * s0 * (onehot0 - s)

    return pl.pallas_call(body, out_shape=SDS(raw.shape, f32), name="lb_bwd")(raw, dlb)


def _gate_fwd(ya, yh, w_ba, w_bh, gc):
    S = ya.shape[0]
    D = w_ba.shape[1]
    tm = _pick(S, MM_ROWS)

    def body(ya_ref, yh_ref, wa_ref, wh_ref, g0_ref, g1_ref, a_ref, b_ref, o_ref):
        a = jnp.dot(ya_ref[...], wa_ref[...], preferred_element_type=f32).astype(bf16)
        b = jnp.dot(yh_ref[...], wh_ref[...], preferred_element_type=f32).astype(bf16)
        a_ref[...] = a
        b_ref[...] = b
        s0, s1 = _sigmoid(g0_ref[...].astype(f32)), _sigmoid(g1_ref[...].astype(f32))
        o_ref[...] = (s0 * a.astype(f32) + s1 * b.astype(f32)).astype(bf16)

    row = pl.BlockSpec((tm, D), lambda i: (i, 0))
    act = pl.BlockSpec((tm, ya.shape[1]), lambda i: (i, 0))
    wspec = pl.BlockSpec(w_ba.shape, lambda i: (0, 0))
    return pl.pallas_call(
        body,
        grid=(S // tm,),
        in_specs=[act, act, wspec, wspec, row, pl.BlockSpec((tm, D), lambda i: (i, 1))],
        out_specs=[row, row, row],
        out_shape=[SDS((S, D), bf16)] * 3,
        compiler_params=_cparams(("parallel",), VMEM_BIG),
        name="branch_gate_fwd",
    )(ya, yh, w_ba, w_bh, gc, gc)


def _gate_bwd(dmo, w_out, a, b, gc, w_ba, w_bh):
    S, D = a.shape
    W = w_ba.shape[0]
    tm = _pick(S, MM_ROWS)

    def body(dmo_ref, wo_ref, a_ref, b_ref, g0_ref, g1_ref, wa_ref, wh_ref,
             da_ref, db_ref, dg_ref, dya_ref, dyh_ref):
        dm = lax.dot_general(dmo_ref[...], wo_ref[...], NT, preferred_element_type=f32)
        dmv = dm.astype(bf16).astype(f32)
        s0, s1 = _sigmoid(g0_ref[...].astype(f32)), _sigmoid(g1_ref[...].astype(f32))
        da = (dmv * s0).astype(bf16)
        db = (dmv * s1).astype(bf16)
        da_ref[...] = da
        db_ref[...] = db
        dg_ref[:, :D] = (dmv * a_ref[...].astype(f32) * (s0 * (1.0 - s0))).astype(bf16)
        dg_ref[:, D:] = (dmv * b_ref[...].astype(f32) * (s1 * (1.0 - s1))).astype(bf16)
        dya_ref[...] = lax.dot_general(da, wa_ref[...], NT, preferred_element_type=f32)
        dyh_ref[...] = lax.dot_general(db, wh_ref[...], NT, preferred_element_type=f32)

    row = pl.BlockSpec((tm, D), lambda i: (i, 0))
    wide = pl.BlockSpec((tm, 2 * D), lambda i: (i, 0))
    narrow = pl.BlockSpec((tm, W), lambda i: (i, 0))
    whole = lambda t: pl.BlockSpec(t.shape, lambda i: (0, 0))
    return pl.pallas_call(
        body,
        grid=(S // tm,),
        in_specs=[row, whole(w_out), row, row, row, pl.BlockSpec((tm, D), lambda i: (i, 1)), whole(w_ba), whole(w_bh)],
        out_specs=[row, row, wide, narrow, narrow],
        out_shape=[SDS((S, D), bf16), SDS((S, D), bf16), SDS((S, 2 * D), bf16), SDS((S, W), f32), SDS((S, W), f32)],
        compiler_params=_cparams(("parallel",), VMEM_BIG),
        name="gate_bwd_fused",
    )(dmo, w_out, a, b, gc, gc, w_ba, w_bh)


CONV_ROWS = 512
INV_SQRT2 = 0.7071067811865476
INV_SQRT_2PI = 0.3989422804014327


CONV_HALO = 16


def _shift_down(cur, prev, k):
    x = pltpu.roll(cur, k, 0)
    row = lax.broadcasted_iota(jnp.int32, (SUBLANE, LANE), 0)
    head = jnp.where(row < k, pltpu.roll(prev, k, 0)[:SUBLANE], x[:SUBLANE])
    return jnp.concatenate([head, x[SUBLANE:]], axis=0)


def _shift_up(cur, nxt, k):
    R = cur.shape[0]
    x = pltpu.roll(cur, R - k, 0)
    row = lax.broadcasted_iota(jnp.int32, (SUBLANE, LANE), 0)
    tail = jnp.where(row >= SUBLANE - k, pltpu.roll(nxt, SUBLANE - k, 0), x[R - SUBLANE:])
    return jnp.concatenate([x[:R - SUBLANE], tail], axis=0)


def _conv_rows(u_ref, w, b, r0, first):
    R = CONV_ROWS
    cur = u_ref[pl.ds(r0, R), :].astype(f32)
    prev = u_ref[pl.ds(pl.multiple_of(jnp.maximum(r0 - CONV_HALO, 0), CONV_HALO), CONV_HALO), :].astype(f32)
    prev = jnp.where(first, 0.0, prev)
    x1 = _shift_down(cur, prev, 1)
    x2 = _shift_down(cur, prev, 2)
    c = ((b + w[0:1] * x2) + w[1:2] * x1) + w[2:3] * cur
    return c, x2, x1, cur


def _conv_fwd(ug, uv, wg, wv, bg, bv):
    S, F = ug.shape
    nchunk = S // CONV_ROWS

    def body(ug_ref, uv_ref, wg_ref, wv_ref, bg_ref, bv_ref, o_ref):
        wgv, wvv, bgv, bvv = wg_ref[...], wv_ref[...], bg_ref[...], bv_ref[...]

        def step(ci, carry):
            r0 = pl.multiple_of(ci * CONV_ROWS, CONV_ROWS)
            cg = _conv_rows(ug_ref, wgv, bgv, r0, ci == 0)[0]
            cv = _conv_rows(uv_ref, wvv, bvv, r0, ci == 0)[0]
            gelu = 0.5 * cg * (1.0 + lax.erf(cg * INV_SQRT2))
            o_ref[pl.ds(r0, CONV_ROWS), :] = (gelu * cv).astype(bf16)
            return carry

        lax.fori_loop(0, nchunk, step, 0)

    col = pl.BlockSpec((S, LANE), lambda j: (0, j))
    w3 = pl.BlockSpec((3, LANE), lambda j: (0, j))
    b1 = pl.BlockSpec((1, LANE), lambda j: (0, j))
    return pl.pallas_call(
        body,
        grid=(F // LANE,),
        in_specs=[col, col, w3, w3, b1, b1],
        out_specs=col,
        out_shape=SDS((S, F), bf16),
        compiler_params=_cparams(("parallel",), VMEM_BIG),
        name="conv_fwd",
    )(ug, uv, wg, wv, bg, bv)


def _conv_bwd(ug, uv, dact, wg, wv, bg, bv):
    S, F = ug.shape
    R = CONV_ROWS
    nchunk = S // R

    def body(ug_ref, uv_ref, da_ref, wg_ref, wv_ref, bg_ref, bv_ref, dug_ref, duv_ref, sg_ref, sv_ref, dcg, dcv):
        wgv, wvv, bgv, bvv = wg_ref[...], wv_ref[...], bg_ref[...], bv_ref[...]
        zero = jnp.zeros((SUBLANE, LANE), f32)

        def fwd_step(ci, acc):
            r0 = pl.multiple_of(ci * R, R)
            cg, g2, g1, g0 = _conv_rows(ug_ref, wgv, bgv, r0, ci == 0)
            cv, v2, v1, v0 = _conv_rows(uv_ref, wvv, bvv, r0, ci == 0)
            da = da_ref[pl.ds(r0, R), :].astype(f32)
            cdf = 0.5 * (1.0 + lax.erf(cg * INV_SQRT2))
            pdf = INV_SQRT_2PI * jnp.exp(-0.5 * cg * cg)
            dg = da * cv * (cdf + cg * pdf)
            dv = da * (cg * cdf)
            dcg[pl.ds(r0, R), :] = dg
            dcv[pl.ds(r0, R), :] = dv
            new = (acc[0] + _colsum8(dg * g2), acc[1] + _colsum8(dg * g1), acc[2] + _colsum8(dg * g0),
                   acc[3] + _colsum8(dg),
                   acc[4] + _colsum8(dv * v2), acc[5] + _colsum8(dv * v1), acc[6] + _colsum8(dv * v0),
                   acc[7] + _colsum8(dv))
            return new

        acc = lax.fori_loop(0, nchunk, fwd_step, (zero,) * 8)
        rows = lax.broadcasted_iota(jnp.int32, (SUBLANE, LANE), 0)

        def stats(parts):
            out = jnp.zeros((SUBLANE, LANE), f32)
            for k, pt in enumerate(parts):
                out = jnp.where(rows == k, jnp.sum(pt, axis=0, keepdims=True), out)
            return out

        sg_ref[...] = stats(acc[0:4])
        sv_ref[...] = stats(acc[4:8])

        def du_rows(dc, w, r0, last):
            cur = dc[pl.ds(r0, R), :]
            nxt = dc[pl.ds(pl.multiple_of(jnp.minimum(r0 + R, S - SUBLANE), SUBLANE), SUBLANE), :]
            nxt = jnp.where(last, 0.0, nxt)
            return w[2:3] * cur + w[1:2] * _shift_up(cur, nxt, 1) + w[0:1] * _shift_up(cur, nxt, 2)

        def bwd_step(ci, carry):
            r0 = pl.multiple_of(ci * R, R)
            last = ci == nchunk - 1
            dug_ref[pl.ds(r0, R), :] = du_rows(dcg, wgv, r0, last).astype(bf16)
            duv_ref[pl.ds(r0, R), :] = du_rows(dcv, wvv, r0, last).astype(bf16)
            return carry

        lax.fori_loop(0, nchunk, bwd_step, 0)

    col = pl.BlockSpec((S, LANE), lambda j: (0, j))
    w3 = pl.BlockSpec((3, LANE), lambda j: (0, j))
    b1 = pl.BlockSpec((1, LANE), lambda j: (0, j))
    st = pl.BlockSpec((SUBLANE, LANE), lambda j: (0, j))
    return pl.pallas_call(
        body,
        grid=(F // LANE,),
        in_specs=[col, col, col, w3, w3, b1, b1],
        out_specs=[col, col, st, st],
        out_shape=[SDS((S, F), bf16), SDS((S, F), bf16), SDS((SUBLANE, F), f32), SDS((SUBLANE, F), f32)],
        scratch_shapes=[pltpu.VMEM((S, LANE), f32), pltpu.VMEM((S, LANE), f32)],
        compiler_params=_cparams(("parallel",), VMEM_BIG),
        name="conv_bwd",
    )(ug, uv, dact, wg, wv, bg, bv)


def _adam_math(w, g, m, v):
    m = ADAM_B1 * m + (1.0 - ADAM_B1) * g
    v = ADAM_B2 * v + (1.0 - ADAM_B2) * (g * g)
    m_hat = m / (1.0 - ADAM_B1 ** ADAM_STEP)
    v_hat = v / (1.0 - ADAM_B2 ** ADAM_STEP)
    delta = -ADAM_LR * (m_hat / (jnp.sqrt(v_hat) + ADAM_EPS) + ADAM_WD * w)
    return delta, m, v


def _adamw(w, m, v, g, name):
    R, C = w.shape
    parts = g.ndim == 3
    tr = R
    if R % 16 == 0:
        for t in range(R, 0, -16):
            if R % t == 0 and t * C * 4 <= ADAM_BLOCK_BYTES:
                tr = t
                break

    def body(w_ref, m_ref, v_ref, g_ref, go_ref, d_ref, mo_ref, vo_ref):
        if parts:
            gv = ((g_ref[0].astype(f32) + g_ref[1].astype(f32)) + g_ref[2].astype(f32)) + g_ref[3].astype(f32)
        else:
            gv = g_ref[...]
        go_ref[...] = gv
        d, mn, vn = _adam_math(w_ref[...], gv, m_ref[...], v_ref[...])
        d_ref[...] = d
        mo_ref[...] = mn
        vo_ref[...] = vn

    row = pl.BlockSpec((tr, C), lambda i: (i, 0))
    gspec = pl.BlockSpec((4, tr, C), lambda i: (0, i, 0)) if parts else row
    return pl.pallas_call(
        body,
        grid=(R // tr,),
        in_specs=[row, row, row, gspec],
        out_specs=[row] * 4,
        out_shape=[SDS((R, C), f32)] * 4,
        compiler_params=_cparams(("parallel",), VMEM_BIG),
        name=name,
    )(w, m, v, g)


def _sum8(parts, name):
    _, _, R, C = parts.shape

    def body(p_ref, o_ref):
        acc = p_ref[0, 0]
        for c in range(2):
            for k in range(4):
                if c or k:
                    acc = acc + p_ref[c, k]
        o_ref[...] = acc

    return pl.pallas_call(body, out_shape=SDS((R, C), f32), name=name)(parts)


def _pair_add(by_core, b, name):
    _, K, R, C = by_core.shape
    tr = R // 2 if R % 32 == 0 else R

    def body(c_ref, a_ref, b_ref, o_ref):
        o_ref[...] = (a_ref[0].astype(f32) + b_ref[...].astype(f32)).astype(bf16)

    blk = pl.BlockSpec((1, tr, C), lambda k, i, c: (k, i, 0))
    return pl.pallas_call(
        body,
        grid_spec=pltpu.PrefetchScalarGridSpec(
            num_scalar_prefetch=1,
            grid=(K, R // tr),
            in_specs=[pl.BlockSpec((1, 1, tr, C), lambda k, i, c: (c[0], k, i, 0)), blk],
            out_specs=blk,
        ),
        out_shape=SDS((K, R, C), bf16),
        compiler_params=_cparams(("parallel", "parallel")),
        name=name,
    )(lax.axis_index("c").astype(jnp.int32).reshape(1), by_core, b)


_ANY = pl.BlockSpec(memory_space=pl.ANY)


def _chip_out_shape(src, gather):
    return SDS((4,) + tuple(src.shape if gather else src.shape[1:]), src.dtype)


def _fill_own(out, src, gather):
    mine = 2 * lax.axis_index("x") + lax.axis_index("y")
    own = src if gather else lax.dynamic_index_in_dim(src, mine, axis=0, keepdims=False)
    return lax.dynamic_update_index_in_dim(out, own, mine, axis=0)


_HBM = pl.BlockSpec(memory_space=pltpu.HBM)
_SEM = pl.BlockSpec(memory_space=pltpu.SEMAPHORE)
_EFFECT = pltpu.SideEffectType.DATAFLOW_SIDE_EFFECTING
_SPLIT_PEERS = {"chip_gather": 3, "chip_gather_wide": 3, "chip_xchg": 3, "core_fill": 4, "core_swap": 1}


def _split_land(src, kind):
    if kind == "chip_gather_wide":
        return SDS((4, 2) + tuple(src.shape), src.dtype)
    if kind == "core_fill":
        return SDS((SUBLANE, LANE), src.dtype)
    if kind == "core_swap":
        return SDS(tuple(src.shape[1:]), src.dtype)
    return _chip_out_shape(src, kind == "chip_gather")


def _split_copies(src_ref, land_ref, sems, kind):
    x, y, c = lax.axis_index("x"), lax.axis_index("y"), lax.axis_index("c")
    n = _SPLIT_PEERS[kind]
    if kind == "core_fill":
        routes = [((x, y, 1 - c), src_ref.at[k, c], src_ref.at[k, c], src_ref.at[k, 1 - c]) for k in range(n)]
    elif kind == "core_swap":
        routes = [((x, y, 1 - c), src_ref.at[1 - c], land_ref, land_ref)]
    else:
        mine = 2 * x + y
        gather = kind != "chip_xchg"
        slot = (lambda k: land_ref.at[k, c]) if kind == "chip_gather_wide" else (lambda k: land_ref.at[k])
        routes = [((px, py, c), src_ref if gather else src_ref.at[2 * px + py], slot(mine), slot(2 * px + py))
                  for px, py in [(1 - x, y), (x, 1 - y), (1 - x, 1 - y)]]
    sends, recvs = [], []
    for j, (peer, piece, there, here) in enumerate(routes):
        sends.append(pltpu.make_async_remote_copy(src_ref=piece, dst_ref=there, send_sem=sems[j],
                                                  recv_sem=sems[n + j], device_id=peer, device_id_type=MESH))
        recvs.append(pltpu.make_async_remote_copy(src_ref=piece, dst_ref=here, send_sem=sems[j],
                                                  recv_sem=sems[n + j], device_id=peer, device_id_type=MESH))
    return sends, recvs


def _split_start(src, kind, name, after=None):
    land = _split_land(src, kind)
    ns = 2 * _SPLIT_PEERS[kind]
    n_in = 2 if after is None else 3

    def body(*refs):
        src_ref, land_ref = refs[:2]
        outs = refs[n_in:]
        for cp in _split_copies(src_ref, land_ref, outs[:ns], kind)[0]:
            cp.start()
        token = outs[ns + 2]
        token[...] = jnp.zeros_like(token)

    res = pl.pallas_call(
        body,
        name=name,
        out_shape=(pltpu.SemaphoreType.DMA(()),) * ns
        + (pltpu.HBM(src.shape, src.dtype), pltpu.HBM(land.shape, land.dtype), SDS((SUBLANE, LANE), f32)),
        in_specs=(_HBM, _HBM) + (() if after is None else (_ANY,)),
        out_specs=(_SEM,) * ns + (_HBM, _HBM, pl.BlockSpec(memory_space=pltpu.VMEM)),
        input_output_aliases={0: ns, 1: ns + 1},
        compiler_params=pltpu.CompilerParams(has_side_effects=_EFFECT),
    )(pltpu.with_memory_space_constraint(src, pltpu.HBM),
      pltpu.with_memory_space_constraint(lax.empty(land.shape, land.dtype), pltpu.HBM),
      *(() if after is None else (after,)))
    return (res[:ns], res[ns], res[ns + 1]), res[ns + 2]


def _split_wait(state, after, kind, name):
    sems, src_thru, land_thru = state
    ns = 2 * _SPLIT_PEERS[kind]

    def body(src_ref, land_ref, *rest):
        sends, recvs = _split_copies(src_ref, land_ref, rest[:ns], kind)
        for cp in recvs:
            cp.wait_recv()
        for cp in sends:
            cp.wait_send()

    src_out, got = pl.pallas_call(
        body,
        name=name,
        out_shape=(pltpu.HBM(src_thru.shape, src_thru.dtype), pltpu.HBM(land_thru.shape, land_thru.dtype)),
        in_specs=(_HBM, _HBM) + (_SEM,) * ns + (_ANY,),
        out_specs=(_HBM, _HBM),
        input_output_aliases={0: 0, 1: 1},
        compiler_params=pltpu.CompilerParams(has_side_effects=_EFFECT),
    )(src_thru, land_thru, *sems, after)
    if kind == "core_swap":
        return got, src_out
    if kind == "core_fill":
        return src_out
    if kind == "chip_gather_wide":
        mine = 2 * lax.axis_index("x") + lax.axis_index("y")
        zero = jnp.zeros((), mine.dtype)
        return lax.dynamic_update_slice(got, src_out[None, None], (mine, lax.axis_index("c").astype(mine.dtype))
                                        + (zero,) * src_out.ndim)
    return _fill_own(got, src_out, kind == "chip_gather")


def _core_fill(both, name):
    n = both.shape[0]

    def body(in_ref, out_ref, send_sems, recv_sems):
        x, y, c = lax.axis_index("x"), lax.axis_index("y"), lax.axis_index("c")
        sends = [pltpu.make_async_remote_copy(src_ref=out_ref.at[k, c], dst_ref=out_ref.at[k, c],
                                              send_sem=send_sems.at[k], recv_sem=recv_sems.at[k],
                                              device_id=(x, y, 1 - c), device_id_type=MESH) for k in range(n)]
        recvs = [pltpu.make_async_remote_copy(src_ref=out_ref.at[k, c], dst_ref=out_ref.at[k, 1 - c],
                                              send_sem=send_sems.at[k], recv_sem=recv_sems.at[k],
                                              device_id=(x, y, 1 - c), device_id_type=MESH) for k in range(n)]
        for cp in sends:
            cp.start()
        for cp in recvs:
            cp.wait_recv()
        for cp in sends:
            cp.wait_send()

    return pl.pallas_call(
        body,
        in_specs=[_ANY],
        out_specs=_ANY,
        out_shape=SDS(both.shape, both.dtype),
        scratch_shapes=[pltpu.SemaphoreType.DMA((n,)), pltpu.SemaphoreType.DMA((n,))],
        input_output_aliases={0: 0},
        name=name,
    )(both)


def _core_gather(src, name):
    def body(src_ref, out_ref, send_sem, recv_sem):
        x, y, c = lax.axis_index("x"), lax.axis_index("y"), lax.axis_index("c")
        cp = pltpu.make_async_remote_copy(src_ref=src_ref, dst_ref=out_ref.at[c], send_sem=send_sem,
                                          recv_sem=recv_sem, device_id=(x, y, 1 - c), device_id_type=MESH)
        cp.start()
        pltpu.make_async_remote_copy(src_ref=src_ref, dst_ref=out_ref.at[1 - c], send_sem=send_sem,
                                     recv_sem=recv_sem, device_id=(x, y, 1 - c), device_id_type=MESH).wait_recv()
        cp.wait_send()

    out = pl.pallas_call(
        body,
        in_specs=[_ANY],
        out_specs=_ANY,
        out_shape=SDS((2,) + tuple(src.shape), src.dtype),
        scratch_shapes=[pltpu.SemaphoreType.DMA, pltpu.SemaphoreType.DMA],
        name=name,
    )(src)
    return lax.dynamic_update_index_in_dim(out, src, lax.axis_index("c"), axis=0)


_PACK_A = (("w_in", (1088, 1024)),)
_PACK_B = (("w_ba", (512, 128)), ("w_bh", (512, 128)), ("w_out", (128, 1024)), ("w_up", (704, 1024)),
           ("w_down", (352, 1024)))
_PACK_SIZES = _PACK_A + _PACK_B
_TRANSPOSED = ("w_in", "w_up")


def _slab_rows(sizes):
    return sum(r * c for _, (r, c) in sizes) // D_MODEL


def _pack_rows(d, sizes):
    n = d[sizes[0][0]].shape[0]
    return jnp.concatenate([d[k].reshape(n, -1, D_MODEL) for k, _ in sizes], axis=1)


def _unpack_rows(slab, sizes):
    n = slab.shape[0]
    out, lo = {}, 0
    for key, (r, c) in sizes:
        rows = r * c // D_MODEL
        out[key] = slab[:, lo:lo + rows].reshape(n, r, c)
        lo += rows
    return out


def _by_core(gslab):
    return jnp.swapaxes(gslab.reshape((4, 2) + gslab.shape[1:]), 0, 1)


def _cols_to_full(t):
    return jnp.swapaxes(t, 0, 1).reshape(t.shape[1], -1)


def _full_to_cols(t):
    K = t.shape[0]
    return jnp.swapaxes(t.reshape(K, 8, -1), 0, 1)


_SMALL = (("pre_mix_norm", (1, 1024)), ("rel_bias", (32, 24)), ("hgrn_lb_raw", (2, 512)), ("hgrn_norm", (1, 128)),
          ("post_mix_norm", (1, 1024)), ("pre_ffn_norm", (1, 1024)), ("conv_b", (1, 5632)),
          ("post_ffn_norm", (1, 1024)))
_SMALL_ROWS = 96
_CONVW_ROWS = 136


_SMALL_USED = sum(r * c for _, (r, c) in _SMALL)


def _pack_small(d, extra=None):
    flat = jnp.concatenate([d[k].reshape(-1) for k, _ in _SMALL] + ([] if extra is None else [extra.reshape(-1)]))
    flat = jnp.pad(flat, (0, _SMALL_ROWS * LANE - flat.shape[0]))
    return flat.reshape(_SMALL_ROWS, LANE)


def _unpack_small(p):
    flat = p.reshape(-1)
    out, lo = {}, 0
    for k, shp in _SMALL:
        n = shp[0] * shp[1]
        out[k] = flat[lo:lo + n].reshape(shp)
        lo += n
    return out


def _local_step(x, tgt, P, plan):
    S = x.shape[0]
    P = dict(P)
    lb = _lb_fwd(P["hgrn_lb_raw"])
    hs = _prep(x, P["pre_mix_norm"], plan.start_token())
    h1 = hs[0]
    consts = [_bias_consts(d, plan.start_token()) for d in DILATIONS]
    biases, dep = [], h1
    for g in range(N_GROUPS):
        tab_t = P["rel_bias"][:, 8 * g:8 * g + 8].T
        dep = _bias_build(tab_t, consts[g][0], consts[g][1], f"bias_build{g}", dep)
        biases.append(dep.reshape(8, ATTN_BLOCK, 2 * ATTN_BLOCK))
    W = dict(plan.weights_a(dep))
    qkv0, hg, gc = _mm_fanout(h1, [W["wt_qkv"][0], W["wt_hg"], W["wt_gate"]], "nt", [bf16, f32, bf16], "proj_natural")
    qkv = [qkv0] + [_mm(hs[g], W["wt_qkv"][g], "nt", bf16, f"proj_qkv{g}") for g in (1, 2)]
    obuf, lbuf, token = [], [], None
    for g, d in enumerate(DILATIONS):
        o_g, l_g = _attn_fwd(qkv[g], biases[g], (S // d) // ATTN_BLOCK, f"attn_fwd{g}", after=token)
        lbuf.append(l_g)
        obuf.append(o_g)
        if g == 0:
            token = plan.forward_b(o_g)
    y_attn, y_attn_b, w0, w1, w2 = _attn_merge(obuf[0], obuf[1], obuf[2], lbuf[0], lbuf[1], lbuf[2])
    y_hgrn, o_raw, ck = _hgrn_fwd(hg, lb, P["hgrn_norm"])
    wb = plan.weights_b(y_hgrn)
    P["conv_w"] = wb.pop("conv_w")
    W.update(wb)
    a, b, merged = _gate_fwd(y_attn_b, y_hgrn, W["w_ba"], W["w_bh"], gc)
    mo, x1, h2 = _mid_fwd(x, merged, W["w_out"], P["post_mix_norm"], P["pre_ffn_norm"])
    ug, uv = _mm_fanout(h2, [W["wt_up_g"], W["wt_up_v"]], "nt", [bf16, bf16], "up_proj")
    cw_g, cw_v = P["conv_w"][:, :D_FF], P["conv_w"][:, D_FF:]
    cb_g, cb_v = P["conv_b"][:, :D_FF], P["conv_b"][:, D_FF:]
    act = _conv_fwd(ug, uv, cw_g, cw_v, cb_g, cb_v)
    loss, dy, dfo, g_post_ffn = _final(x1, act, W["w_down"], tgt, P["post_ffn_norm"])
    gW_down = _mm(act, dfo, "tn", bf16, "gw_down")
    dact = _mm(dfo, W["w_down"], "nt", bf16, "d_act")
    dug, duv, st_g, st_v = _conv_bwd(ug, uv, dact, cw_g, cw_v, cb_g, cb_v)
    gW_up_g = _mm(dug, h2, "tn", bf16, "gw_up_gate")
    gW_up_v = _mm(duv, h2, "tn", bf16, "gw_up_val")
    dx1, dmo, g_pre_ffn, g_post_mix = _mid_bwd(dy, dug, duv, W["wt_up_g"], W["wt_up_v"], x1, mo, P["pre_ffn_norm"],
                                               P["post_mix_norm"])
    gW_out = _mm(merged, dmo, "tn", bf16, "gw_out")
    da, db, dgc, dyattn, dyhgrn = _gate_bwd(dmo, W["w_out"], a, b, gc, W["w_ba"], W["w_bh"])
    gW_ba = _mm(y_attn_b, da, "tn", bf16, "gw_ba")
    gW_bh = _mm(y_hgrn, db, "tn", bf16, "gw_bh")
    big_b = dict(w_ba=gW_ba, w_bh=gW_bh, w_out=gW_out, w_up=[gW_up_g, gW_up_v], w_down=gW_down)
    dos = _attn_merge_bwd(dyattn, y_attn, w0, w1, w2, after=plan.grads_b_start(big_b))
    dq_h, df_h, dv_h, dog_h, glb8, gnw8 = _hgrn_bwd(hg, o_raw, dyhgrn, ck, lb, P["hgrn_norm"],
                                                   after=plan.grads_b_exchange(dos[5]))
    dhg = [dq_h, df_h, dv_h, dog_h]
    g_lb_raw = _lb_bwd(P["hgrn_lb_raw"], glb8[0:1])
    gn = gnw8[0:1]
    g_hgrn_norm = (gn[:, 0:128] + gn[:, 128:256]) + (gn[:, 256:384] + gn[:, 384:512])
    dqkvs, gW_qkv, g_rel = [], [], []
    for g, d in enumerate(DILATIONS):
        dq, dk, dv, dbias = _attn_bwd(qkv[g], biases[g], dos[g], dos[3 + g], lbuf[g], (S // d) // ATTN_BLOCK,
                                      f"attn_bwd{g}")
        dqkvs.append([dq, dk, dv])
        gW_qkv.append(_mm(dqkvs[g], hs[g], "tn", bf16, f"gw_qkv{g}"))
        g_rel.append(_bias_grad(dbias.reshape(8, -1), consts[g][0], f"bias_grad{g}"))
    gW_hg = _mm(dhg, h1, "tn", bf16, "gw_hg")
    gW_gate = _mm(dgc, h1, "tn", bf16, "gw_gate")
    gW_in = gW_qkv + [gW_hg, gW_gate]
    token = plan.grads_a_start(gW_in)
    dh_perm = [_mm(dqkvs[g], W["wt_qkv"][g], "nn", f32, f"dh1_qkv{g}", after=token) for g in (1, 2)]
    token = plan.grads_a_exchange(dh_perm[1])
    dh_main = _mm(dqkvs[0] + dhg + [dgc], [W["wt_qkv"][0], W["wt_hg"], W["wt_gate"]], "nn", f32, "dh1_main",
                  after=token)
    grad_x, g_pre_mix = _first_bwd(x, dx1, dh_main, dh_perm[0], dh_perm[1], P["pre_mix_norm"])

    g_conv_w = jnp.concatenate([st_g[0:3], st_v[0:3]], axis=1)
    g_conv_b = jnp.concatenate([st_g[3:4], st_v[3:4]], axis=1)
    small = dict(pre_mix_norm=g_pre_mix, rel_bias=jnp.concatenate(g_rel, axis=1), hgrn_lb_raw=g_lb_raw,
                 hgrn_norm=g_hgrn_norm, post_mix_norm=g_post_mix, pre_ffn_norm=g_pre_ffn, conv_b=g_conv_b,
                 post_ffn_norm=g_post_ffn, conv_w=g_conv_w)
    return loss, grad_x, gW_in, big_b, small


def _weights_a(both):
    wt = both.reshape(-1, D_MODEL)
    return dict(
        wt_qkv=[_Rows(wt, g * QKV_G, QKV_G) for g in range(N_GROUPS)],
        wt_hg=_Rows(wt, 3 * QKV_G, 4 * HGRN_W),
        wt_gate=_Rows(wt, 3 * QKV_G + 4 * HGRN_W, wt.shape[0] - 3 * QKV_G - 4 * HGRN_W),
    )


def _weights_b(slabs):
    sh = _unpack_rows(slabs, _PACK_B)
    wt_up = sh["w_up"].reshape(-1, D_MODEL)
    return dict(
        w_ba=_cols_to_full(sh["w_ba"]),
        w_bh=_cols_to_full(sh["w_bh"]),
        w_out=sh["w_out"].reshape(D_MODEL, D_MODEL),
        wt_up_g=wt_up[:D_FF],
        wt_up_v=wt_up[D_FF:],
        w_down=sh["w_down"].reshape(D_FF, D_MODEL),
    )


def _dest_rows(sections, height):
    out = []
    for j in range(8):
        lo, hi, off, pieces = j * height, (j + 1) * height, 0, []
        for s in sections:
            a, b = max(lo, off), min(hi, off + s.shape[0])
            if a < b:
                pieces.append(s[a - off:b - off])
            off += s.shape[0]
        out.append(pieces[0] if len(pieces) == 1 else jnp.concatenate(pieces, axis=0))
    return out


def _grad_blocks_a(sections):
    rows = _dest_rows(sections, 1088)
    return jnp.stack([jnp.stack([rows[2 * k + c].astype(bf16) for k in range(4)]) for c in range(2)])


def _grad_slab_b(g):
    shards = dict(w_ba=_full_to_cols(g["w_ba"]), w_bh=_full_to_cols(g["w_bh"]), w_out=g["w_out"].reshape(8, 128, D_MODEL),
                  w_up=jnp.stack(_dest_rows(g["w_up"], 704)), w_down=g["w_down"].reshape(8, 352, D_MODEL))
    return _pack_rows({k: v.astype(bf16) for k, v in shards.items()}, _PACK_B)


_CONVW_SLAB_ROWS = 16


class _Traffic:
    def __init__(self, slab_a, slab_b, conv_w):
        hi = conv_w.astype(bf16)
        r1 = conv_w - hi.astype(f32)
        mid = r1.astype(bf16)
        lo = (r1 - mid.astype(f32)).astype(bf16)
        bits = jnp.stack([hi, mid, lo]).reshape(-1)
        tail = jnp.pad(bits, (0, _CONVW_SLAB_ROWS * D_MODEL - bits.shape[0])).reshape(_CONVW_SLAB_ROWS, D_MODEL)
        self.slab_b = jnp.concatenate([slab_b, tail], axis=0)
        self.state_a, tok = _split_start(slab_a, "chip_gather_wide", "ag_a_start")
        self.state_b, self.token = _split_start(self.slab_b, "chip_gather_wide", "ag_b_start", after=tok)
        self.state = None
        self.state_gb = None

    def start_token(self):
        return self.token

    def weights_a(self, after):
        half = _split_wait(self.state_a, after, "chip_gather_wide", "ag_a_wait")
        return _weights_a(_core_fill(half, "ag_a_cores"))

    def forward_b(self, after):
        half = _split_wait(self.state_b, after, "chip_gather_wide", "ag_b_wait")
        self.state, token = _split_start(half, "core_fill", "ag_b_cores_start")
        return token

    def weights_b(self, after):
        both = _split_wait(self.state, after, "core_fill", "ag_b_cores_wait")
        slabs = both.reshape((8,) + tuple(self.slab_b.shape))
        rows = _slab_rows(_PACK_B)
        out = _weights_b(slabs[:, :rows])
        pieces = slabs[:, rows:].reshape(8, -1)[:, :3 * 3 * 704].reshape(8, 3, 3, 704).astype(f32)
        out["conv_w"] = _cols_to_full((pieces[:, 0] + pieces[:, 1]) + pieces[:, 2])
        return out

    def grads_b_start(self, grads):
        self.state, token = _split_start(_by_core(_grad_slab_b(grads)), "core_swap", "rs_b_cores_start")
        return token

    def grads_b_exchange(self, after):
        from_sib, by_core = _split_wait(self.state, after, "core_swap", "rs_b_cores_wait")
        self.state_gb, token = _split_start(_pair_add(by_core, from_sib, "rs_b_pair_add"), "chip_xchg", "rs_b_start")
        return token

    def grads_a_start(self, sections):
        self.state, token = _split_start(_grad_blocks_a(sections), "core_swap", "rs_a_cores_start")
        return token

    def grads_a_exchange(self, after):
        from_sib, by_core = _split_wait(self.state, after, "core_swap", "rs_a_cores_wait")
        self.state, token = _split_start(_pair_add(by_core, from_sib, "rs_a_pair_add"), "chip_xchg", "rs_a_start")
        return token

    def parts(self, after):
        parts = _unpack_rows(_split_wait(self.state_gb, after, "chip_xchg", "rs_b_wait"), _PACK_B)
        parts["w_in"] = _split_wait(self.state, after, "chip_xchg", "rs_a_wait")
        return parts


def kernel(x, pre_mix_norm, w_in, rel_bias, hgrn_lb_raw, hgrn_norm, w_branch_attn, w_branch_hgrn, w_out, post_mix_norm, pre_ffn_norm, w_up, conv_w, conv_b, w_down, post_ffn_norm, loss_target, m_pre_mix_norm, m_w_in, m_rel_bias, m_hgrn_lb_raw, m_hgrn_norm, m_w_branch_attn, m_w_branch_hgrn, m_w_out, m_post_mix_norm, m_pre_ffn_norm, m_w_up, m_conv_w, m_conv_b, m_w_down, m_post_ffn_norm, v_pre_mix_norm, v_w_in, v_rel_bias, v_hgrn_lb_raw, v_hgrn_norm, v_w_branch_attn, v_w_branch_hgrn, v_w_out, v_post_mix_norm, v_pre_ffn_norm, v_w_up, v_conv_w, v_conv_b, v_w_down, v_post_ffn_norm):
    ci = lax.axis_index("c")
    dev = 4 * lax.axis_index("x") + 2 * lax.axis_index("y") + ci
    tr = lambda t: jnp.swapaxes(t[0], 0, 1)
    wts = dict(w_in=tr(w_in), w_ba=w_branch_attn[0], w_bh=w_branch_hgrn[0], w_out=w_out[0], w_up=tr(w_up),
               w_down=w_down[0])
    mom = dict(w_in=tr(m_w_in), w_ba=m_w_branch_attn[0], w_bh=m_w_branch_hgrn[0], w_out=m_w_out[0], w_up=tr(m_w_up),
               w_down=m_w_down[0])
    var = dict(w_in=tr(v_w_in), w_ba=v_w_branch_attn[0], w_bh=v_w_branch_hgrn[0], w_out=v_w_out[0], w_up=tr(v_w_up),
               w_down=v_w_down[0])
    small_w = dict(pre_mix_norm=pre_mix_norm, rel_bias=rel_bias, hgrn_lb_raw=hgrn_lb_raw, hgrn_norm=hgrn_norm,
                   post_mix_norm=post_mix_norm, pre_ffn_norm=pre_ffn_norm, conv_b=conv_b, post_ffn_norm=post_ffn_norm)
    small_m = dict(pre_mix_norm=m_pre_mix_norm, rel_bias=m_rel_bias, hgrn_lb_raw=m_hgrn_lb_raw, hgrn_norm=m_hgrn_norm,
                   post_mix_norm=m_post_mix_norm, pre_ffn_norm=m_pre_ffn_norm, conv_b=m_conv_b,
                   post_ffn_norm=m_post_ffn_norm)
    small_v = dict(pre_mix_norm=v_pre_mix_norm, rel_bias=v_rel_bias, hgrn_lb_raw=v_hgrn_lb_raw, hgrn_norm=v_hgrn_norm,
                   post_mix_norm=v_post_mix_norm, pre_ffn_norm=v_pre_ffn_norm, conv_b=v_conv_b,
                   post_ffn_norm=v_post_ffn_norm)

    plan = _Traffic(wts["w_in"].astype(bf16),
                    _pack_rows({k: wts[k].astype(bf16)[None] for k, _ in _PACK_B}, _PACK_B)[0], conv_w[0])

    loss8, grad_x, _, _, small = _local_step(x[0], loss_target[0], small_w, plan)
    spack = jnp.concatenate([_pack_small(small, loss8[0, 0:1]),
                             jnp.pad(small["conv_w"].reshape(-1, LANE), ((0, _CONVW_ROWS - 132), (0, 0)))], axis=0)
    small_state, token = _split_start(spack, "chip_gather", "ag_small_start")

    parts = plan.parts(token)
    outs_big = {}
    for k, _ in _PACK_SIZES:
        outs_big[k] = _adamw(wts[k], mom[k], var[k], parts[k], "adamw_" + k)

    by_chip = _split_wait(small_state, outs_big["w_in"][1], "chip_gather", "ag_small_wait")
    allp = _core_gather(by_chip, "ag_small_cores")
    ssum = _sum8(allp, "small_sum")
    gs = ssum[:_SMALL_ROWS]
    loss = ssum[_SMALL_USED // LANE, _SMALL_USED % LANE]
    res_small = _adamw(_pack_small(small_w), _pack_small(small_m), _pack_small(small_v), gs, "adamw_small")
    sm = [_unpack_small(t) for t in res_small]
    g_cw_full = ssum[_SMALL_ROWS:_SMALL_ROWS + 132].reshape(3, 2 * D_FF)
    g_cw = lax.dynamic_slice_in_dim(g_cw_full, dev * 704, 704, axis=1)
    res_cw = _adamw(conv_w[0], m_conv_w[0], v_conv_w[0], g_cw, "adamw_conv_w")

    def pick(i):
        def big_(k):
            t = outs_big[k][i]
            return (jnp.swapaxes(t, 0, 1) if k in _TRANSPOSED else t)[None]
        return [sm[i]["pre_mix_norm"], big_("w_in"), sm[i]["rel_bias"], sm[i]["hgrn_lb_raw"], sm[i]["hgrn_norm"],
                big_("w_ba"), big_("w_bh"), big_("w_out"), sm[i]["post_mix_norm"], sm[i]["pre_ffn_norm"],
                big_("w_up"), res_cw[i][None], sm[i]["conv_b"], big_("w_down"), sm[i]["post_ffn_norm"]]

    return (loss, grad_x[None], *pick(0), *pick(1), *pick(2), *pick(3))
```

```python
import functools
import math

import jax
import jax.numpy as jnp
from jax import lax
from jax.experimental import pallas as pl
from jax.experimental.pallas import tpu as pltpu

f32 = jnp.float32
bf16 = jnp.bfloat16
SDS = jax.ShapeDtypeStruct
HIGHEST = lax.Precision.HIGHEST
MESH = pl.DeviceIdType.MESH

NN = (((1,), (0,)), ((), ()))
NT = (((1,), (1,)), ((), ()))
TN = (((0,), (0,)), ((), ()))

D_MODEL = 1024
N_GROUPS = 3
DILATIONS = (1, 4, 16)
HEAD_DIM = 64
ATTN_BLOCK = 128
QKV_G = 1536
ATTN_OUT = 512
HGRN_W = 512
HGRN_CHUNK = 32
D_FF = 2816
NUM_BUCKETS = 32
MAX_EXACT = 16
MAX_DISTANCE = 2048
NEG_INF = -1e30
EPS = 1e-6
LANE = 128
SUBLANE = 8
VMEM_BIG = 48 * 1024 * 1024
MM_ROWS = 512
MM_OUT_BYTES = 8 * 1024 * 1024
ADAM_BLOCK_BYTES = 2304 * 1024

ADAM_LR, ADAM_B1, ADAM_B2, ADAM_EPS, ADAM_WD, ADAM_STEP = 0.001, 0.9, 0.999, 1e-08, 0.01, 10


def _pick(n, pref):
    t = pref
    while t >= LANE:
        if n % t == 0:
            return t
        t //= 2
    return n


def _cparams(sem=None, vmem=None):
    kw = {}
    if sem is not None:
        kw["dimension_semantics"] = sem
    if vmem is not None:
        kw["vmem_limit_bytes"] = vmem
    return pltpu.CompilerParams(**kw)


def _sigmoid(x):
    return jax.nn.sigmoid(x)


def _colsum8(x):
    return x.reshape(x.shape[0] // SUBLANE, SUBLANE, x.shape[1]).sum(axis=0)


class _Rows:
    def __init__(self, full, lo, rows):
        self.full, self.lo, self.shape = full, lo, tuple(full.shape[:-2]) + (rows, full.shape[-1])


def _resident(t):
    if isinstance(t, _Rows):
        return pl.BlockSpec((pl.Element(t.shape[0]), pl.Element(t.shape[1])), lambda i: (t.lo, 0)), t.full
    return pl.BlockSpec(t.shape, lambda i: (0, 0)), t


def _mm(a, b, mode, out_dtype, name, acc=None, after=None):
    dims = {"nn": NN, "nt": NT, "tn": TN}[mode]
    has_acc = acc is not None
    parts = list(a) if isinstance(a, (list, tuple)) else [a]
    if mode == "tn":
        assert not has_acc
        K, N = b.shape
        widths = [t.shape[1] for t in parts]
        M = sum(widths)
        whole = M * N * 4 <= MM_OUT_BYTES
        assert whole or len(parts) == 1
        tmm = M if whole else M // 2
        ts = _pick(K, 4 * MM_ROWS)
        nk = K // ts

        npart = len(parts)
        narrow = out_dtype != f32

        def body_tn(*refs):
            b_ref, o_ref = refs[npart], refs[npart + 1]
            acc_ref = refs[npart + 2] if narrow else o_ref
            k = pl.program_id(1)
            bv = b_ref[...]
            lo = 0
            for a_ref, w in zip(refs[:npart], widths if whole else [tmm]):
                part = lax.dot_general(a_ref[...], bv, dims, preferred_element_type=f32)
                rows = slice(lo, lo + w)
                lo += w

                @pl.when(k == 0)
                def _(part=part, rows=rows):
                    acc_ref[rows, :] = part

                @pl.when(k > 0)
                def _(part=part, rows=rows):
                    acc_ref[rows, :] += part

            if narrow:
                @pl.when(k == nk - 1)
                def _():
                    o_ref[...] = acc_ref[...].astype(out_dtype)

        return pl.pallas_call(
            body_tn,
            grid=(M // tmm, nk),
            in_specs=[pl.BlockSpec((ts, w if whole else tmm), lambda i, k: (k, i)) for w in widths]
            + [pl.BlockSpec((ts, N), lambda i, k: (k, 0))],
            out_specs=pl.BlockSpec((tmm, N), lambda i, k: (i, 0)),
            out_shape=SDS((M, N), out_dtype),
            scratch_shapes=[pltpu.VMEM((tmm, N), f32)] if narrow else [],
            compiler_params=_cparams(("parallel", "arbitrary"), VMEM_BIG),
            name=name,
        )(*parts, b)

    bs = list(b) if isinstance(b, (list, tuple)) else [b]
    widths = [t.shape[1] for t in parts]
    M = parts[0].shape[0]
    kdim = 0 if mode == "nn" else 1
    N = bs[0].shape[1 - kdim]
    tm = _pick(M, MM_ROWS)
    npart, nb = len(parts), len(bs)
    place, bi, lo = [], 0, 0
    for w in widths:
        place.append((bi, lo))
        lo += w
        if lo == bs[bi].shape[kdim]:
            bi, lo = bi + 1, 0
    assert bi == nb and lo == 0

    def body(*refs):
        a_refs, b_refs = refs[:npart], refs[npart:npart + nb]
        c_ref = refs[npart + nb] if has_acc else None
        o_ref = refs[-1]
        part = None
        for a_ref, w, (bi, lo) in zip(a_refs, widths, place):
            b_ref = b_refs[bi]
            if w == bs[bi].shape[kdim]:
                bk = b_ref[...]
            else:
                bk = b_ref[:, lo:lo + w] if mode == "nt" else b_ref[lo:lo + w, :]
            t = lax.dot_general(a_ref[...], bk, dims, preferred_element_type=f32)
            part = t if part is None else part + t
        if has_acc:
            part = part + c_ref[...]
        o_ref[...] = part.astype(out_dtype)

    specs = [pl.BlockSpec((tm, w), lambda i: (i, 0)) for w in widths] + [_resident(t)[0] for t in bs]
    args = parts + [_resident(t)[1] for t in bs]
    aliases = {}
    if has_acc:
        specs.append(pl.BlockSpec((tm, N), lambda i: (i, 0)))
        args.append(acc)
        aliases = {npart + nb: 0}
    if after is not None:
        specs.append(pl.BlockSpec(memory_space=pl.ANY))
        args.append(after)
    return pl.pallas_call(
        body,
        grid=(M // tm,),
        in_specs=specs,
        out_specs=pl.BlockSpec((tm, N), lambda i: (i, 0)),
        out_shape=SDS((M, N), out_dtype),
        input_output_aliases=aliases,
        compiler_params=_cparams(("parallel",), VMEM_BIG),
        name=name,
    )(*args)


def _mm_fanout(a, bs, mode, out_dtypes, name):
    dims = {"nn": NN, "nt": NT}[mode]
    M, K = a.shape
    ns = [b.shape[1] if mode == "nn" else b.shape[0] for b in bs]
    tm = _pick(M, MM_ROWS)
    nb = len(bs)

    def body(a_ref, *refs):
        av = a_ref[...]
        for b_ref, o_ref, dt in zip(refs[:nb], refs[nb:], out_dtypes):
            o_ref[...] = lax.dot_general(av, b_ref[...], dims, preferred_element_type=f32).astype(dt)

    return pl.pallas_call(
        body,
        grid=(M // tm,),
        in_specs=[pl.BlockSpec((tm, K), lambda i: (i, 0))] + [_resident(b)[0] for b in bs],
        out_specs=[pl.BlockSpec((tm, n), lambda i: (i, 0)) for n in ns],
        out_shape=[SDS((M, n), dt) for n, dt in zip(ns, out_dtypes)],
        compiler_params=_cparams(("parallel",), VMEM_BIG),
        name=name,
    )(a, *[_resident(b)[1] for b in bs])


PERM_ROWS = 2048


def _perm_spec(d, cols=LANE):
    return pl.BlockSpec((d, PERM_ROWS // d, cols), lambda i, j: (0, i, j))


def _to_natural(src_ref, dst_ref, d):
    n = src_ref.shape[1]
    for r in range(d):
        dst_ref[pl.ds(r, n, stride=d), :] = src_ref[r]


def _prep(x, w, after=None):
    S, D = x.shape
    R = PERM_ROWS
    nc = D // LANE
    n_in = nc + 1 + (after is not None)

    def body(*refs):
        x_refs, w_ref = refs[:nc], refs[nc]
        h_ref, h4_ref, h16_ref, rs = refs[n_in:]
        ssq = None
        for xr in x_refs:
            v = xr[...]
            t = jnp.sum(v * v, axis=-1, keepdims=True)
            ssq = t if ssq is None else ssq + t
        rinv = lax.rsqrt(ssq * (1.0 / D) + EPS)
        rs[...] = jnp.broadcast_to(rinv, (R, LANE))
        for j, xr in enumerate(x_refs):
            cols = slice(j * LANE, (j + 1) * LANE)
            wj = w_ref[:, cols]
            h_ref[:, cols] = ((xr[...] * rinv) * wj).astype(bf16)
            for d, o_ref in ((4, h4_ref), (16, h16_ref)):
                n = R // d
                for r in range(d):
                    rows = pl.ds(r, n, stride=d)
                    o_ref[r, :, cols] = ((xr[rows, :] * rs[rows, :]) * wj).astype(bf16)

    col = lambda j: pl.BlockSpec((R, LANE), lambda i, j=j: (i, j))
    h, h4, h16 = pl.pallas_call(
        body,
        grid=(S // R,),
        in_specs=[col(j) for j in range(nc)] + [pl.BlockSpec((1, D), lambda i: (0, 0))]
        + ([] if after is None else [pl.BlockSpec(memory_space=pl.ANY)]),
        out_specs=[pl.BlockSpec((R, D), lambda i: (i, 0)), pl.BlockSpec((4, R // 4, D), lambda i: (0, i, 0)),
                   pl.BlockSpec((16, R // 16, D), lambda i: (0, i, 0))],
        out_shape=[SDS((S, D), bf16), SDS((4, S // 4, D), bf16), SDS((16, S // 16, D), bf16)],
        scratch_shapes=[pltpu.VMEM((R, LANE), f32)],
        compiler_params=_cparams(("parallel",), VMEM_BIG),
        name="prep_norm_perm",
    )(*([x] * nc), w, *([] if after is None else [after]))
    return [h, h4.reshape(S, D), h16.reshape(S, D)]


def _rms_parts(xv):
    r = lax.rsqrt(jnp.mean(xv * xv, axis=-1, keepdims=True) + EPS)
    return r, xv * r


def _rms_bwd(xhat, r, w, dy):
    dyw = dy * w
    return r * (dyw - xhat * jnp.mean(dyw * xhat, axis=-1, keepdims=True))


def _mid_fwd(x, merged, w_out, w_pm, w_pf):
    S, D = x.shape
    tm = _pick(S, MM_ROWS)

    def body(x_ref, m_ref, wo_ref, wpm_ref, wpf_ref, mo_ref, x1_ref, h2_ref):
        mo = jnp.dot(m_ref[...], wo_ref[...], preferred_element_type=f32)
        mo_ref[...] = mo
        _, moh = _rms_parts(mo)
        x1 = x_ref[...] + moh * wpm_ref[...]
        x1_ref[...] = x1
        _, x1h = _rms_parts(x1)
        h2_ref[...] = (x1h * wpf_ref[...]).astype(bf16)

    row = pl.BlockSpec((tm, D), lambda i: (i, 0))
    vec = pl.BlockSpec((1, D), lambda i: (0, 0))
    return pl.pallas_call(
        body,
        grid=(S // tm,),
        in_specs=[row, pl.BlockSpec((tm, merged.shape[1]), lambda i: (i, 0)),
                  pl.BlockSpec(w_out.shape, lambda i: (0, 0)), vec, vec],
        out_specs=[row, row, row],
        out_shape=[SDS((S, D), f32), SDS((S, D), f32), SDS((S, D), bf16)],
        compiler_params=_cparams(("parallel",), VMEM_BIG),
        name="out_proj_mid_fwd",
    )(x, merged, w_out, w_pm, w_pf)


def _final(x1, act, w_down, tgt, w_pfn):
    S, D = x1.shape
    tm = _pick(S, MM_ROWS)
    nt = S // tm

    def body(x1_ref, a_ref, wd_ref, t_ref, w_ref, loss_ref, dy_ref, dfo_ref, gw_ref, lacc, gacc):
        i = pl.program_id(0)

        @pl.when(i == 0)
        def _():
            lacc[...] = jnp.zeros_like(lacc)
            gacc[...] = jnp.zeros_like(gacc)

        w = w_ref[...]
        r, foh = _rms_parts(jnp.dot(a_ref[...], wd_ref[...], preferred_element_type=f32))
        y = x1_ref[...] + foh * w
        err = y - t_ref[...]
        lacc[...] += _colsum8(err * err)
        dy = err * (1.0 / D)
        dy_ref[...] = dy
        gacc[...] += _colsum8(dy * foh)
        dfo_ref[...] = _rms_bwd(foh, r, w, dy).astype(bf16)

        @pl.when(i == nt - 1)
        def _():
            loss_ref[...] = jnp.full((SUBLANE, LANE), 0.5 / D, f32) * jnp.sum(lacc[...])
            gw_ref[...] = jnp.sum(gacc[...], axis=0, keepdims=True)

    row = pl.BlockSpec((tm, D), lambda i: (i, 0))
    vec = pl.BlockSpec((1, D), lambda i: (0, 0))
    return pl.pallas_call(
        body,
        grid=(nt,),
        in_specs=[row, pl.BlockSpec((tm, act.shape[1]), lambda i: (i, 0)),
                  pl.BlockSpec(w_down.shape, lambda i: (0, 0)), row, vec],
        out_specs=[pl.BlockSpec((SUBLANE, LANE), lambda i: (0, 0)), row, row, vec],
        out_shape=[SDS((SUBLANE, LANE), f32), SDS((S, D), f32), SDS((S, D), bf16), SDS((1, D), f32)],
        scratch_shapes=[pltpu.VMEM((SUBLANE, D), f32), pltpu.VMEM((SUBLANE, D), f32)],
        compiler_params=_cparams(("arbitrary",), VMEM_BIG),
        name="down_proj_final_loss",
    )(x1, act, w_down, tgt, w_pfn)


MID_BWD_ROWS = 256


def _mid_bwd(dy, dug, duv, wt_g, wt_v, x1, mo, w_pf, w_pm):
    S, D = dy.shape
    tm = _pick(S, MID_BWD_ROWS)
    nt = S // tm

    def body(dy_ref, dug_ref, duv_ref, wg_ref, wv_ref, x1_ref, mo_ref, wpf_ref, wpm_ref,
             dx1_ref, dmo_ref, gpf_ref, gpm_ref, apf, apm):
        i = pl.program_id(0)

        @pl.when(i == 0)
        def _():
            apf[...] = jnp.zeros_like(apf)
            apm[...] = jnp.zeros_like(apm)

        r1, x1h = _rms_parts(x1_ref[...])
        dh2 = jnp.dot(dug_ref[...], wg_ref[...], preferred_element_type=f32) \
            + jnp.dot(duv_ref[...], wv_ref[...], preferred_element_type=f32)
        apf[...] += _colsum8(dh2 * x1h)
        dx1 = dy_ref[...] + _rms_bwd(x1h, r1, wpf_ref[...], dh2)
        dx1_ref[...] = dx1
        rm, moh = _rms_parts(mo_ref[...])
        apm[...] += _colsum8(dx1 * moh)
        dmo_ref[...] = _rms_bwd(moh, rm, wpm_ref[...], dx1).astype(bf16)

        @pl.when(i == nt - 1)
        def _():
            gpf_ref[...] = jnp.sum(apf[...], axis=0, keepdims=True)
            gpm_ref[...] = jnp.sum(apm[...], axis=0, keepdims=True)

    row = pl.BlockSpec((tm, D), lambda i: (i, 0))
    vec = pl.BlockSpec((1, D), lambda i: (0, 0))
    return pl.pallas_call(
        body,
        grid=(nt,),
        in_specs=[row, pl.BlockSpec((tm, dug.shape[1]), lambda i: (i, 0)), pl.BlockSpec((tm, duv.shape[1]), lambda i: (i, 0)),
                  pl.BlockSpec(wt_g.shape, lambda i: (0, 0)), pl.BlockSpec(wt_v.shape, lambda i: (0, 0)),
                  row, row, vec, vec],
        out_specs=[row, row, vec, vec],
        out_shape=[SDS((S, D), f32), SDS((S, D), bf16), SDS((1, D), f32), SDS((1, D), f32)],
        scratch_shapes=[pltpu.VMEM((SUBLANE, D), f32), pltpu.VMEM((SUBLANE, D), f32)],
        compiler_params=_cparams(("arbitrary",), VMEM_BIG),
        name="dh2_mid_bwd",
    )(dy, dug, duv, wt_g, wt_v, x1, mo, w_pf, w_pm)


def _first_bwd(x, dx1, dh_a, dh_b, dh_c, w_pre):
    S, D = x.shape
    tm = _pick(S, 512)
    nt = S // tm
    nc = D // LANE

    def body(*refs):
        x_ref, dx1_ref, a_ref = refs[:3]
        b_refs, c_refs, w_ref = refs[3:3 + nc], refs[3 + nc:3 + 2 * nc], refs[3 + 2 * nc]
        gx_ref, gw_ref, acc, dh_s, sb, sc = refs[4 + 2 * nc:]
        i = pl.program_id(0)

        @pl.when(i == 0)
        def _():
            acc[...] = jnp.zeros_like(acc)

        for j in range(nc):
            cols = slice(j * LANE, (j + 1) * LANE)
            _to_natural(b_refs[j], sb, 4)
            _to_natural(c_refs[j], sc, 16)
            dh_s[:, cols] = (a_ref[:, cols] + sb[...]) + sc[...]
        r, xh = _rms_parts(x_ref[...])
        dh = dh_s[...]
        acc[...] += _colsum8(dh * xh)
        gx_ref[...] = dx1_ref[...] + _rms_bwd(xh, r, w_ref[...], dh)

        @pl.when(i == nt - 1)
        def _():
            gw_ref[...] = jnp.sum(acc[...], axis=0, keepdims=True)

    row = pl.BlockSpec((tm, D), lambda i: (i, 0))
    vec = pl.BlockSpec((1, D), lambda i: (0, 0))
    perm = lambda d: [pl.BlockSpec((d, tm // d, LANE), lambda i, j=j: (0, i, j)) for j in range(nc)]
    return pl.pallas_call(
        body,
        grid=(nt,),
        in_specs=[row, row, row] + perm(4) + perm(16) + [vec],
        out_specs=[row, vec],
        out_shape=[SDS((S, D), f32), SDS((1, D), f32)],
        scratch_shapes=[pltpu.VMEM((SUBLANE, D), f32), pltpu.VMEM((tm, D), f32), pltpu.VMEM((tm, LANE), f32),
                        pltpu.VMEM((tm, LANE), f32)],
        compiler_params=_cparams(("arbitrary",), VMEM_BIG),
        name="first_bwd",
    )(x, dx1, dh_a, *([dh_b.reshape(4, S // 4, D)] * nc), *([dh_c.reshape(16, S // 16, D)] * nc), w_pre)


def _t5_bucket(dist):
    n = jnp.maximum(dist, 0)
    nf = jnp.maximum(n, 1).astype(f32)
    large = MAX_EXACT + (jnp.log(nf / MAX_EXACT) / math.log(MAX_DISTANCE / MAX_EXACT)
                         * (NUM_BUCKETS - MAX_EXACT)).astype(jnp.int32)
    large = jnp.minimum(large, NUM_BUCKETS - 1)
    return jnp.where(n < MAX_EXACT, n, large)


def _bias_consts(d, after=None):
    if after is not None:
        d, _ = lax.optimization_barrier((jnp.int32(d), after))
    blk = ATTN_BLOCK
    rel = jnp.arange(blk)[:, None] + blk - jnp.arange(2 * blk)[None, :]
    in_win = (rel >= 0) & (rel <= blk)
    bucket = _t5_bucket(rel * d).reshape(1, -1)
    onehot = (bucket == jnp.arange(NUM_BUCKETS)[:, None]).astype(f32)
    return onehot, in_win.astype(f32).reshape(1, -1)


def _bias_build(tab_t, onehot, maskf, name, after):
    H = tab_t.shape[0]

    def body(t_ref, oh_ref, m_ref, after_ref, o_ref):
        b = jnp.dot(t_ref[...], oh_ref[...], precision=HIGHEST, preferred_element_type=f32)
        o_ref[...] = jnp.where(m_ref[...] > 0.5, b, NEG_INF)

    vm = pl.BlockSpec(memory_space=pltpu.VMEM)
    return pl.pallas_call(body, out_shape=SDS((H, onehot.shape[1]), f32), name=name,
                          in_specs=[vm, vm, vm, pl.BlockSpec(memory_space=pl.ANY)], out_specs=vm,
                          )(tab_t, onehot, maskf, after)


def _bias_grad(dbias_flat, onehot, name):
    H = dbias_flat.shape[0]

    def body(g_ref, oh_ref, o_ref):
        o_ref[...] = lax.dot_general(oh_ref[...], g_ref[...], NT, precision=HIGHEST, preferred_element_type=f32)

    return pl.pallas_call(body, out_shape=SDS((NUM_BUCKETS, H), f32), name=name)(dbias_flat, onehot)


ATTN_TILE = 512
ATTN_SUB = ATTN_TILE // ATTN_BLOCK
ATTN_HP = 4
ATTN_WIDE = ATTN_HP * LANE


def _qkv_specs(nt):
    tile = (ATTN_TILE, ATTN_WIDE)
    blk = (ATTN_BLOCK, ATTN_WIDE)
    sec = ATTN_OUT // ATTN_WIDE
    cur = lambda off: (lambda h, t: (jnp.minimum(t, nt - 1), off + h))
    prev = lambda off: (lambda h, t: (jnp.maximum(jnp.minimum(t, nt - 1) * ATTN_SUB - 1, 0), off + h))
    return [pl.BlockSpec(tile, cur(0)), pl.BlockSpec(blk, prev(sec)), pl.BlockSpec(tile, cur(sec)),
            pl.BlockSpec(blk, prev(2 * sec)), pl.BlockSpec(tile, cur(2 * sec))]


def _head_masks():
    lane = lax.broadcasted_iota(jnp.int32, (ATTN_BLOCK, LANE), 1)
    return lane < HEAD_DIM


def _stack_heads(x2, low):
    zero = jnp.zeros_like(x2)
    return jnp.concatenate([jnp.where(low, x2, zero), jnp.where(low, zero, x2)], axis=0)


def _attn_fwd(qkv, bias, bps, name, after=None):
    S = qkv.shape[0]
    nt = S // ATTN_TILE
    scale = HEAD_DIM ** -0.5

    def body(q_ref, kp_ref, kc_ref, vp_ref, vc_ref, b_ref, *rest):
        o_ref, l_ref = rest[-2:]
        t = pl.program_id(1)
        low = _head_masks()
        col = lax.broadcasted_iota(jnp.int32, (2 * ATTN_BLOCK, 2 * ATTN_BLOCK), 1)
        for hp in range(ATTN_HP):
            cols = slice(hp * LANE, (hp + 1) * LANE)
            kk = jnp.concatenate([kp_ref[:, cols], kc_ref[:, cols]], axis=0)
            vv = jnp.concatenate([vp_ref[:, cols], vc_ref[:, cols]], axis=0)
            bias2 = b_ref[2 * hp:2 * hp + 2].reshape(2 * ATTN_BLOCK, 2 * ATTN_BLOCK)
            for b in range(ATTN_SUB):
                lo = b * ATTN_BLOCK
                rows = slice(lo, lo + ATTN_BLOCK)
                keys = slice(lo, lo + 2 * ATTN_BLOCK)
                dead = jnp.logical_and((t * ATTN_SUB + b) % bps == 0, col < ATTN_BLOCK)
                q2 = _stack_heads(q_ref[rows, cols], low)
                kb, vb = kk[keys], vv[keys]
                s = lax.dot_general(q2, kb, NT, preferred_element_type=f32) * scale + bias2
                s = jnp.where(dead, NEG_INF, s)
                m = jnp.max(s, axis=-1, keepdims=True)
                p = jnp.exp(s - m)
                l = jnp.sum(p, axis=-1, keepdims=True)
                o2 = jnp.dot(p.astype(bf16), vb, preferred_element_type=f32) / l
                lse = m + jnp.log(l)
                o_ref[rows, cols] = jnp.where(low, o2[:ATTN_BLOCK], o2[ATTN_BLOCK:])
                l_ref[rows, cols] = jnp.where(low, lse[:ATTN_BLOCK], lse[ATTN_BLOCK:])

    tile = pl.BlockSpec((ATTN_TILE, ATTN_WIDE), lambda h, t: (t, h))
    return pl.pallas_call(
        body,
        grid=(4 // ATTN_HP, nt),
        in_specs=_qkv_specs(nt) + [pl.BlockSpec((2 * ATTN_HP, ATTN_BLOCK, 2 * ATTN_BLOCK), lambda h, t: (h, 0, 0))]
        + ([] if after is None else [pl.BlockSpec(memory_space=pl.ANY)]),
        out_specs=[tile, tile],
        out_shape=[SDS((S, ATTN_OUT), f32), SDS((S, ATTN_OUT), f32)],
        compiler_params=_cparams(("parallel", "parallel")),
        name=name,
    )(qkv, qkv, qkv, qkv, qkv, bias, *([] if after is None else [after]))


def _attn_bwd(qkv, bias, do, dvec, lse, bps, name):
    S = qkv.shape[0]
    nt = S // ATTN_TILE
    scale = HEAD_DIM ** -0.5

    def assemble(parts):
        rows = [parts[0][:ATTN_BLOCK]]
        for b in range(ATTN_SUB - 1):
            rows.append(parts[b][ATTN_BLOCK:] + parts[b + 1][:ATTN_BLOCK])
        rows.append(parts[-1][ATTN_BLOCK:])
        return rows

    def body(q_ref, kp_ref, kc_ref, vp_ref, vc_ref, b_ref, do_ref, dvec_ref, lse_ref,
             dq_ref, dk_ref, dv_ref, db_ref, ck, cv):
        t = pl.program_id(1)
        last = ATTN_TILE - ATTN_BLOCK

        @pl.when(t == 0)
        def _():
            ck[...] = jnp.zeros_like(ck)
            cv[...] = jnp.zeros_like(cv)
            db_ref[...] = jnp.zeros_like(db_ref)

        @pl.when(t < nt)
        def _():
            low = _head_masks()
            col = lax.broadcasted_iota(jnp.int32, (2 * ATTN_BLOCK, 2 * ATTN_BLOCK), 1)
            per_row = lambda t2: jnp.concatenate([t2[:, 0:1], t2[:, HEAD_DIM:HEAD_DIM + 1]], axis=0)
            for hp in range(ATTN_HP):
                cols = slice(hp * LANE, (hp + 1) * LANE)
                kk = jnp.concatenate([kp_ref[:, cols], kc_ref[:, cols]], axis=0)
                vv = jnp.concatenate([vp_ref[:, cols], vc_ref[:, cols]], axis=0)
                bias2 = b_ref[2 * hp:2 * hp + 2].reshape(2 * ATTN_BLOCK, 2 * ATTN_BLOCK)
                dk_parts, dv_parts = [], []
                dsum = None
                for b in range(ATTN_SUB):
                    lo = b * ATTN_BLOCK
                    rows = slice(lo, lo + ATTN_BLOCK)
                    keys = slice(lo, lo + 2 * ATTN_BLOCK)
                    dead = jnp.logical_and((t * ATTN_SUB + b) % bps == 0, col < ATTN_BLOCK)
                    q2 = _stack_heads(q_ref[rows, cols], low)
                    do2 = _stack_heads(do_ref[rows, cols].astype(bf16), low)
                    kb, vb = kk[keys], vv[keys]
                    s = lax.dot_general(q2, kb, NT, preferred_element_type=f32) * scale + bias2
                    s = jnp.where(dead, NEG_INF, s)
                    p = jnp.exp(s - per_row(lse_ref[rows, cols]))
                    dp = lax.dot_general(do2, vb, NT, preferred_element_type=f32)
                    ds = p * (dp - per_row(dvec_ref[rows, cols]))
                    dsum = ds if dsum is None else dsum + ds
                    dsb = ds.astype(bf16)
                    dq2 = jnp.dot(dsb, kb, preferred_element_type=f32) * scale
                    dq_ref[rows, cols] = jnp.where(low, dq2[:ATTN_BLOCK], dq2[ATTN_BLOCK:]).astype(bf16)
                    dk_parts.append(lax.dot_general(dsb, q2, TN, preferred_element_type=f32) * scale)
                    dv_parts.append(lax.dot_general(p.astype(bf16), do2, TN, preferred_element_type=f32))
                db_ref[2 * hp:2 * hp + 2] += dsum.reshape(2, ATTN_BLOCK, 2 * ATTN_BLOCK)
                for parts, carry, out_ref in ((dk_parts, ck, dk_ref), (dv_parts, cv, dv_ref)):
                    rws = assemble(parts)
                    out_ref[:last, cols] = carry[:last, cols].astype(bf16)
                    out_ref[last:, cols] = (carry[last:, cols] + rws[0]).astype(bf16)
                    for b in range(ATTN_SUB):
                        carry[b * ATTN_BLOCK:(b + 1) * ATTN_BLOCK, cols] = rws[b + 1]

        @pl.when(t == nt)
        def _():
            dk_ref[...] = ck[...].astype(bf16)
            dv_ref[...] = cv[...].astype(bf16)

    tile = (ATTN_TILE, ATTN_WIDE)
    cur = pl.BlockSpec(tile, lambda h, t: (jnp.minimum(t, nt - 1), h))
    lag = pl.BlockSpec(tile, lambda h, t: (jnp.maximum(t - 1, 0), h))
    bspec = pl.BlockSpec((2 * ATTN_HP, ATTN_BLOCK, 2 * ATTN_BLOCK), lambda h, t: (h, 0, 0))
    return pl.pallas_call(
        body,
        grid=(4 // ATTN_HP, nt + 1),
        in_specs=_qkv_specs(nt) + [bspec, cur, cur, cur],
        out_specs=[cur, lag, lag, bspec],
        out_shape=[SDS((S, ATTN_OUT), bf16), SDS((S, ATTN_OUT), bf16), SDS((S, ATTN_OUT), bf16),
                   SDS((8, ATTN_BLOCK, 2 * ATTN_BLOCK), f32)],
        scratch_shapes=[pltpu.VMEM(tile, f32), pltpu.VMEM(tile, f32)],
        compiler_params=_cparams(("parallel", "arbitrary")),
        name=name,
    )(qkv, qkv, qkv, qkv, qkv, bias, do, dvec, lse)


def _attn_merge(o0, o1, o2, l0, l1, l2):
    S, W = o0.shape
    R = PERM_ROWS

    def body(o0_ref, o1_ref, o2_ref, l0_ref, l1_ref, l2_ref, y_ref, yb_ref, w0_ref, w1_ref, w2_ref,
             so1, so2, sl1, sl2):
        _to_natural(o1_ref, so1, 4)
        _to_natural(l1_ref, sl1, 4)
        _to_natural(o2_ref, so2, 16)
        _to_natural(l2_ref, sl2, 16)
        a, b, c = l0_ref[...], sl1[...], sl2[...]
        m = jnp.maximum(jnp.maximum(a, b), c)
        ea, eb, ec = jnp.exp(a - m), jnp.exp(b - m), jnp.exp(c - m)
        den = (ea + eb) + ec
        w0, w1, w2 = ea / den, eb / den, ec / den
        y = (w0 * o0_ref[...] + w1 * so1[...]) + w2 * so2[...]
        y_ref[...] = y
        yb_ref[...] = y.astype(bf16)
        w0_ref[...] = w0
        w1_ref[...] = w1
        w2_ref[...] = w2

    nat = pl.BlockSpec((R, LANE), lambda i, j: (i, j))
    v4 = lambda t: t.reshape(4, S // 4, W)
    v16 = lambda t: t.reshape(16, S // 16, W)
    return pl.pallas_call(
        body,
        grid=(S // R, W // LANE),
        in_specs=[nat, _perm_spec(4), _perm_spec(16)] * 2,
        out_specs=[nat] * 5,
        out_shape=[SDS((S, W), f32), SDS((S, W), bf16)] + [SDS((S, W), f32)] * 3,
        scratch_shapes=[pltpu.VMEM((R, LANE), f32)] * 4,
        compiler_params=_cparams(("parallel", "parallel"), VMEM_BIG),
        name="attn_merge",
    )(o0, v4(o1), v16(o2), l0, v4(l1), v16(l2))


def _attn_merge_bwd(dy, y, w0, w1, w2, after=None):
    S, W = dy.shape
    R = PERM_ROWS

    def body(dy_ref, y_ref, w0_ref, w1_ref, w2_ref, *rest):
        a0, a1, a2, b0, b1, b2, sa, sb = rest[-8:]
        dyv = dy_ref[...]
        r = lax.broadcasted_iota(jnp.int32, (LANE, LANE), 0) // HEAD_DIM
        c = lax.broadcasted_iota(jnp.int32, (LANE, LANE), 1) // HEAD_DIM
        seg = jnp.where(r == c, 1.0, 0.0).astype(f32)
        cbar = jnp.dot(dyv * y_ref[...], seg, precision=HIGHEST, preferred_element_type=f32)
        w = w0_ref[...]
        a0[...] = (w * dyv).astype(bf16)
        b0[...] = w * cbar
        for d, w_ref, a_ref, b_ref in ((4, w1_ref, a1, b1), (16, w2_ref, a2, b2)):
            w = w_ref[...]
            sa[...] = w * dyv
            sb[...] = w * cbar
            n = R // d
            for k in range(d):
                rows = pl.ds(k, n, stride=d)
                a_ref[k] = sa[rows, :].astype(bf16)
                b_ref[k] = sb[rows, :]

    nat = pl.BlockSpec((R, LANE), lambda i, j: (i, j))
    shapes = lambda dt: [SDS((S, W), dt), SDS((4, S // 4, W), dt), SDS((16, S // 16, W), dt)]
    outs = pl.pallas_call(
        body,
        grid=(S // R, W // LANE),
        in_specs=[nat] * 5 + ([] if after is None else [pl.BlockSpec(memory_space=pl.ANY)]),
        out_specs=[nat, _perm_spec(4), _perm_spec(16)] * 2,
        out_shape=shapes(bf16) + shapes(f32),
        scratch_shapes=[pltpu.VMEM((R, LANE), f32)] * 2,
        compiler_params=_cparams(("parallel", "parallel"), VMEM_BIG),
        name="attn_merge_bwd",
    )(dy, y, w0, w1, w2, *([] if after is None else [after]))
    return [t.reshape(S, W) for t in outs]


HGRN_SB = 256
HGRN_PAIR = 4


def _chunk_masks():
    r = jnp.arange(HGRN_SB)[:, None]
    c = jnp.arange(HGRN_SB)[None, :]
    same = (r // HGRN_CHUNK) == (c // HGRN_CHUNK)
    return jnp.stack([same & (c <= r), same, same & (c >= r)]).astype(bf16)


def _mask_dot(mask, x):
    hi = x.astype(bf16)
    r1 = x - hi.astype(f32)
    mid = r1.astype(bf16)
    lo = (r1 - mid.astype(f32)).astype(bf16)
    p = jnp.dot(mask, jnp.concatenate([hi, mid, lo], axis=1), preferred_element_type=f32)
    n = x.shape[1]
    return (p[:, :n] + p[:, n:2 * n]) + p[:, 2 * n:]


def _hgrn_prep(q_raw, f_raw, lbv, tril, same):
    sq = _sigmoid(q_raw)
    qs = q_raw * sq
    sig = _sigmoid(f_raw)
    f = lbv + (1.0 - lbv) * sig
    g = jnp.log(f)
    k = 1.0 - f
    G = _mask_dot(tril, g)
    GL = _mask_dot(same, g)
    eG = jnp.exp(G)
    einv = jnp.exp(-G)
    edec = jnp.exp(GL - G)
    return dict(sq=sq, qs=qs, sig=sig, f=f, k=k, eG=eG, einv=einv, edec=edec, eGL=jnp.exp(GL),
                qt=qs * eG, kt=k * einv, kd=k * edec)


def _hgrn_fwd(hg, lb, normw):
    S = hg.shape[0]
    sb = HGRN_SB
    nsb = S // sb
    nch = sb // HGRN_CHUNK

    def body(q_ref, f_ref, v_ref, og_ref, lb_ref, nw_ref, m_ref, y_ref, o_ref, ck_ref, st):
        j = pl.program_id(1)

        @pl.when(j == 0)
        def _():
            st[...] = jnp.zeros_like(st)

        tril_m = m_ref[0]
        tril = tril_m.astype(f32) > 0.5

        def one_head(hh):
            cols = slice(hh * LANE, (hh + 1) * LANE)
            ST = st[hh]
            ck_ref[hh, 0] = ST
            pr = _hgrn_prep(q_ref[:, cols], f_ref[:, cols], lb_ref[:, cols], tril_m, m_ref[1])
            qtb, ktb, kdb = pr["qt"].astype(bf16), pr["kt"].astype(bf16), pr["kd"].astype(bf16)
            eGL = pr["eGL"]
            vb = v_ref[:, cols].astype(bf16)
            A = jnp.where(tril, lax.dot_general(qtb, ktb, NT, preferred_element_type=f32), 0.0)
            o = jnp.dot(A.astype(bf16), vb, preferred_element_type=f32)
            outs = []
            for ci in range(nch):
                lo = ci * HGRN_CHUNK
                sl = slice(lo, lo + HGRN_CHUNK)
                outs.append(o[sl] + lax.dot_general(qtb[sl], ST.astype(bf16), NT, preferred_element_type=f32))
                ST = ST * eGL[lo:lo + 1, :] + lax.dot_general(vb[sl], kdb[sl], TN, preferred_element_type=f32)
            st[hh] = ST
            of = jnp.concatenate(outs, axis=0)
            o_ref[:, cols] = of
            rms = lax.rsqrt(jnp.mean(of * of, axis=-1, keepdims=True) + EPS)
            ogv = og_ref[:, cols]
            y_ref[:, cols] = ((of * rms * nw_ref[...]) * (ogv * _sigmoid(ogv))).astype(bf16)

        for hh in range(HGRN_PAIR):
            one_head(hh)

    wide = HGRN_PAIR * LANE
    col = lambda off: pl.BlockSpec((sb, wide), lambda h, j: (j, off // HGRN_PAIR + h))
    return pl.pallas_call(
        body,
        grid=(4 // HGRN_PAIR, nsb),
        in_specs=[col(0), col(4), col(8), col(12), pl.BlockSpec((1, wide), lambda h, j: (0, h)),
                  pl.BlockSpec((1, LANE), lambda h, j: (0, 0)),
                  pl.BlockSpec((3, sb, sb), lambda h, j: (0, 0, 0))],
        out_specs=[col(0), col(0), pl.BlockSpec((HGRN_PAIR, 1, LANE, LANE), lambda h, j: (h, j, 0, 0))],
        out_shape=[SDS((S, HGRN_W), bf16), SDS((S, HGRN_W), f32), SDS((4, nsb, LANE, LANE), f32)],
        scratch_shapes=[pltpu.VMEM((HGRN_PAIR, LANE, LANE), f32)],
        compiler_params=_cparams(("parallel", "arbitrary")),
        name="hgrn_fwd",
    )(hg, hg, hg, hg, lb, normw, _chunk_masks())


def _hgrn_bwd(hg, o_raw, dy, ck, lb, normw, after=None):
    S = hg.shape[0]
    sb = HGRN_SB
    nsb = S // sb
    nch = sb // HGRN_CHUNK

    def body(q_ref, f_ref, v_ref, og_ref, o_ref, dy_ref, ck_ref, lb_ref, nw_ref, m_ref, *rest):
        dq_ref, df_ref, dv_ref, dog_ref, glb_ref, gnw_ref, dst, alb, anw = rest[-9:]
        j = pl.program_id(1)

        @pl.when(j == 0)
        def _():
            dst[...] = jnp.zeros_like(dst)
            alb[...] = jnp.zeros_like(alb)
            anw[...] = jnp.zeros_like(anw)

        tril_m = m_ref[0]
        tril = tril_m.astype(f32) > 0.5
        nw = nw_ref[...]

        def one_head(hh):
            cols = slice(hh * LANE, (hh + 1) * LANE)
            lbv = lb_ref[:, cols]
            q_raw = q_ref[:, cols]
            pr = _hgrn_prep(q_raw, f_ref[:, cols], lbv, tril_m, m_ref[1])
            qt, kt, kd, eGL = pr["qt"], pr["kt"], pr["kd"], pr["eGL"]
            qtb, ktb, kdb = qt.astype(bf16), kt.astype(bf16), kd.astype(bf16)
            vb = v_ref[:, cols].astype(bf16)

            o = o_ref[:, cols]
            ogv = og_ref[:, cols]
            sog = _sigmoid(ogv)
            rms = lax.rsqrt(jnp.mean(o * o, axis=-1, keepdims=True) + EPS)
            oh = o * rms
            dyv = dy_ref[:, cols]
            dog_ref[:, cols] = (dyv * (oh * nw) * (sog * (1.0 + ogv * (1.0 - sog)))).astype(bf16)
            dohw = dyv * (ogv * sog)
            anw[:, cols] += _colsum8(dohw * oh)
            doh = dohw * nw
            do = rms * (doh - oh * jnp.mean(doh * oh, axis=-1, keepdims=True))
            dob = do.astype(bf16)

            Ab = jnp.where(tril, lax.dot_general(qtb, ktb, NT, preferred_element_type=f32), 0.0).astype(bf16)
            dAb = jnp.where(tril, lax.dot_general(dob, vb, NT, preferred_element_type=f32), 0.0).astype(bf16)
            dv_acc = lax.dot_general(Ab, dob, TN, preferred_element_type=f32)
            dqt = jnp.dot(dAb, ktb, preferred_element_type=f32)
            dkt = lax.dot_general(dAb, qtb, TN, preferred_element_type=f32)

            ST = ck_ref[hh, 0]
            states = []
            for ci in range(nch):
                lo = ci * HGRN_CHUNK
                sl = slice(lo, lo + HGRN_CHUNK)
                states.append(ST)
                ST = ST * eGL[lo:lo + 1, :] + lax.dot_general(vb[sl], kdb[sl], TN, preferred_element_type=f32)

            dST = dst[hh]
            dqt_i, dkd_i, dv_i, deg_i = [None] * nch, [None] * nch, [None] * nch, [None] * nch
            for ci in reversed(range(nch)):
                lo = ci * HGRN_CHUNK
                sl = slice(lo, lo + HGRN_CHUNK)
                ST0 = states[ci]
                dSTb = dST.astype(bf16)
                dv_i[ci] = lax.dot_general(kdb[sl], dSTb, NT, preferred_element_type=f32)
                dqt_i[ci] = jnp.dot(dob[sl], ST0.astype(bf16), preferred_element_type=f32)
                dkd_i[ci] = jnp.dot(vb[sl], dSTb, preferred_element_type=f32)
                deg_i[ci] = jnp.broadcast_to(jnp.sum(dST * ST0, axis=0, keepdims=True), (HGRN_CHUNK, LANE))
                dST = dST * eGL[lo:lo + 1, :] + lax.dot_general(dob[sl], qtb[sl], TN, preferred_element_type=f32)
            dst[hh] = dST

            dqt = dqt + jnp.concatenate(dqt_i, axis=0)
            dkd = jnp.concatenate(dkd_i, axis=0)
            dv_ref[:, cols] = (dv_acc + jnp.concatenate(dv_i, axis=0)).astype(bf16)
            deg = jnp.concatenate(deg_i, axis=0)

            dqs = dqt * pr["eG"]
            dkdkd = dkd * kd
            dG = dqt * qt - dkt * kt - dkdkd
            dk = dkt * pr["einv"] + dkd * pr["edec"]
            dGL = _mask_dot(m_ref[1], dkdkd) + eGL * deg
            dg = _mask_dot(m_ref[2], dG) + dGL
            df = dg / pr["f"] - dk
            sig = pr["sig"]
            df_ref[:, cols] = (df * (1.0 - lbv) * (sig * (1.0 - sig))).astype(bf16)
            alb[:, cols] += _colsum8(df * (1.0 - sig))
            sq = pr["sq"]
            dq_ref[:, cols] = (dqs * (sq * (1.0 + q_raw * (1.0 - sq)))).astype(bf16)

        for hh in range(HGRN_PAIR):
            one_head(hh)

        @pl.when(j == nsb - 1)
        def _():
            glb_ref[...] = jnp.broadcast_to(jnp.sum(alb[...], axis=0, keepdims=True), (SUBLANE, wide))
            gnw_ref[...] = jnp.broadcast_to(jnp.sum(anw[...], axis=0, keepdims=True), (SUBLANE, wide))

    wide = HGRN_PAIR * LANE
    rev = lambda off: pl.BlockSpec((sb, wide), lambda h, j: (nsb - 1 - j, off // HGRN_PAIR + h))
    stat = pl.BlockSpec((SUBLANE, wide), lambda h, j: (0, h))
    return pl.pallas_call(
        body,
        grid=(4 // HGRN_PAIR, nsb),
        in_specs=[rev(0), rev(4), rev(8), rev(12), rev(0), rev(0),
                  pl.BlockSpec((HGRN_PAIR, 1, LANE, LANE), lambda h, j: (h, nsb - 1 - j, 0, 0)),
                  pl.BlockSpec((1, wide), lambda h, j: (0, h)), pl.BlockSpec((1, LANE), lambda h, j: (0, 0)),
                  pl.BlockSpec((3, sb, sb), lambda h, j: (0, 0, 0))]
        + ([] if after is None else [pl.BlockSpec(memory_space=pl.ANY)]),
        out_specs=[rev(0), rev(0), rev(0), rev(0), stat, stat],
        out_shape=[SDS((S, HGRN_W), bf16)] * 4 + [SDS((SUBLANE, HGRN_W), f32)] * 2,
        scratch_shapes=[pltpu.VMEM((HGRN_PAIR, LANE, LANE), f32), pltpu.VMEM((SUBLANE, wide), f32),
                        pltpu.VMEM((SUBLANE, wide), f32)],
        compiler_params=_cparams(("parallel", "arbitrary")),
        name="hgrn_bwd",
    )(hg, hg, hg, hg, o_raw, dy, ck, lb, normw, _chunk_masks(), *([] if after is None else [after]))


def _lb_fwd(raw):
    def body(r_ref, o_ref):
        r = r_ref[...]
        m = jnp.max(r, axis=0, keepdims=True)
        e = jnp.exp(r - m)
        o_ref[...] = (e / jnp.sum(e, axis=0, keepdims=True))[0:1]

    return pl.pallas_call(body, out_shape=SDS((1, raw.shape[1]), f32), name="lb_fwd")(raw)


def _lb_bwd(raw, dlb):
    def body(r_ref, d_ref, o_ref):
        r = r_ref[...]
        m = jnp.max(r, axis=0, keepdims=True)
        e = jnp.exp(r - m)
        s = e / jnp.sum(e, axis=0, keepdims=True)
        s0 = s[0:1]
        onehot0 = jnp.where(lax.broadcasted_iota(jnp.int32, r.shape, 0) == 0, 1.0, 0.0)
        o_ref[...] = d_ref[...] * s0 * (onehot0 - s)

    return pl.pallas_call(body, out_shape=SDS(raw.shape, f32), name="lb_bwd")(raw, dlb)


def _gate_fwd(ya, yh, w_ba, w_bh, gc):
    S = ya.shape[0]
    D = w_ba.shape[1]
    tm = _pick(S, MM_ROWS)

    def body(ya_ref, yh_ref, wa_ref, wh_ref, g0_ref, g1_ref, a_ref, b_ref, o_ref):
        a = jnp.dot(ya_ref[...], wa_ref[...], preferred_element_type=f32).astype(bf16)
        b = jnp.dot(yh_ref[...], wh_ref[...], preferred_element_type=f32).astype(bf16)
        a_ref[...] = a
        b_ref[...] = b
        s0, s1 = _sigmoid(g0_ref[...].astype(f32)), _sigmoid(g1_ref[...].astype(f32))
        o_ref[...] = (s0 * a.astype(f32) + s1 * b.astype(f32)).astype(bf16)

    row = pl.BlockSpec((tm, D), lambda i: (i, 0))
    act = pl.BlockSpec((tm, ya.shape[1]), lambda i: (i, 0))
    wspec = pl.BlockSpec(w_ba.shape, lambda i: (0, 0))
    return pl.pallas_call(
        body,
        grid=(S // tm,),
        in_specs=[act, act, wspec, wspec, row, pl.BlockSpec((tm, D), lambda i: (i, 1))],
        out_specs=[row, row, row],
        out_shape=[SDS((S, D), bf16)] * 3,
        compiler_params=_cparams(("parallel",), VMEM_BIG),
        name="branch_gate_fwd",
    )(ya, yh, w_ba, w_bh, gc, gc)


def _gate_bwd(dmo, w_out, a, b, gc, w_ba, w_bh):
    S, D = a.shape
    W = w_ba.shape[0]
    tm = _pick(S, MM_ROWS)

    def body(dmo_ref, wo_ref, a_ref, b_ref, g0_ref, g1_ref, wa_ref, wh_ref,
             da_ref, db_ref, dg_ref, dya_ref, dyh_ref):
        dm = lax.dot_general(dmo_ref[...], wo_ref[...], NT, preferred_element_type=f32)
        dmv = dm.astype(bf16).astype(f32)
        s0, s1 = _sigmoid(g0_ref[...].astype(f32)), _sigmoid(g1_ref[...].astype(f32))
        da = (dmv * s0).astype(bf16)
        db = (dmv * s1).astype(bf16)
        da_ref[...] = da
        db_ref[...] = db
        dg_ref[:, :D] = (dmv * a_ref[...].astype(f32) * (s0 * (1.0 - s0))).astype(bf16)
        dg_ref[:, D:] = (dmv * b_ref[...].astype(f32) * (s1 * (1.0 - s1))).astype(bf16)
        dya_ref[...] = lax.dot_general(da, wa_ref[...], NT, preferred_element_type=f32)
        dyh_ref[...] = lax.dot_general(db, wh_ref[...], NT, preferred_element_type=f32)

    row = pl.BlockSpec((tm, D), lambda i: (i, 0))
    wide = pl.BlockSpec((tm, 2 * D), lambda i: (i, 0))
    narrow = pl.BlockSpec((tm, W), lambda i: (i, 0))
    whole = lambda t: pl.BlockSpec(t.shape, lambda i: (0, 0))
    return pl.pallas_call(
        body,
        grid=(S // tm,),
        in_specs=[row, whole(w_out), row, row, row, pl.BlockSpec((tm, D), lambda i: (i, 1)), whole(w_ba), whole(w_bh)],
        out_specs=[row, row, wide, narrow, narrow],
        out_shape=[SDS((S, D), bf16), SDS((S, D), bf16), SDS((S, 2 * D), bf16), SDS((S, W), f32), SDS((S, W), f32)],
        compiler_params=_cparams(("parallel",), VMEM_BIG),
        name="gate_bwd_fused",
    )(dmo, w_out, a, b, gc, gc, w_ba, w_bh)


CONV_ROWS = 512
INV_SQRT2 = 0.7071067811865476
INV_SQRT_2PI = 0.3989422804014327


CONV_HALO = 16


def _shift_down(cur, prev, k):
    x = pltpu.roll(cur, k, 0)
    row = lax.broadcasted_iota(jnp.int32, (SUBLANE, LANE), 0)
    head = jnp.where(row < k, pltpu.roll(prev, k, 0)[:SUBLANE], x[:SUBLANE])
    return jnp.concatenate([head, x[SUBLANE:]], axis=0)


def _shift_up(cur, nxt, k):
    R = cur.shape[0]
    x = pltpu.roll(cur, R - k, 0)
    row = lax.broadcasted_iota(jnp.int32, (SUBLANE, LANE), 0)
    tail = jnp.where(row >= SUBLANE - k, pltpu.roll(nxt, SUBLANE - k, 0), x[R - SUBLANE:])
    return jnp.concatenate([x[:R - SUBLANE], tail], axis=0)


def _conv_rows(u_ref, w, b, r0, first):
    R = CONV_ROWS
    cur = u_ref[pl.ds(r0, R), :].astype(f32)
    prev = u_ref[pl.ds(pl.multiple_of(jnp.maximum(r0 - CONV_HALO, 0), CONV_HALO), CONV_HALO), :].astype(f32)
    prev = jnp.where(first, 0.0, prev)
    x1 = _shift_down(cur, prev, 1)
    x2 = _shift_down(cur, prev, 2)
    c = ((b + w[0:1] * x2) + w[1:2] * x1) + w[2:3] * cur
    return c, x2, x1, cur


def _conv_fwd(ug, uv, wg, wv, bg, bv):
    S, F = ug.shape
    nchunk = S // CONV_ROWS

    def body(ug_ref, uv_ref, wg_ref, wv_ref, bg_ref, bv_ref, o_ref):
        wgv, wvv, bgv, bvv = wg_ref[...], wv_ref[...], bg_ref[...], bv_ref[...]

        def step(ci, carry):
            r0 = pl.multiple_of(ci * CONV_ROWS, CONV_ROWS)
            cg = _conv_rows(ug_ref, wgv, bgv, r0, ci == 0)[0]
            cv = _conv_rows(uv_ref, wvv, bvv, r0, ci == 0)[0]
            gelu = 0.5 * cg * (1.0 + lax.erf(cg * INV_SQRT2))
            o_ref[pl.ds(r0, CONV_ROWS), :] = (gelu * cv).astype(bf16)
            return carry

        lax.fori_loop(0, nchunk, step, 0)

    col = pl.BlockSpec((S, LANE), lambda j: (0, j))
    w3 = pl.BlockSpec((3, LANE), lambda j: (0, j))
    b1 = pl.BlockSpec((1, LANE), lambda j: (0, j))
    return pl.pallas_call(
        body,
        grid=(F // LANE,),
        in_specs=[col, col, w3, w3, b1, b1],
        out_specs=col,
        out_shape=SDS((S, F), bf16),
        compiler_params=_cparams(("parallel",), VMEM_BIG),
        name="conv_fwd",
    )(ug, uv, wg, wv, bg, bv)


def _conv_bwd(ug, uv, dact, wg, wv, bg, bv):
    S, F = ug.shape
    R = CONV_ROWS
    nchunk = S // R

    def body(ug_ref, uv_ref, da_ref, wg_ref, wv_ref, bg_ref, bv_ref, dug_ref, duv_ref, sg_ref, sv_ref, dcg, dcv):
        wgv, wvv, bgv, bvv = wg_ref[...], wv_ref[...], bg_ref[...], bv_ref[...]
        zero = jnp.zeros((SUBLANE, LANE), f32)

        def fwd_step(ci, acc):
            r0 = pl.multiple_of(ci * R, R)
            cg, g2, g1, g0 = _conv_rows(ug_ref, wgv, bgv, r0, ci == 0)
            cv, v2, v1, v0 = _conv_rows(uv_ref, wvv, bvv, r0, ci == 0)
            da = da_ref[pl.ds(r0, R), :].astype(f32)
            cdf = 0.5 * (1.0 + lax.erf(cg * INV_SQRT2))
            pdf = INV_SQRT_2PI * jnp.exp(-0.5 * cg * cg)
            dg = da * cv * (cdf + cg * pdf)
            dv = da * (cg * cdf)
            dcg[pl.ds(r0, R), :] = dg
            dcv[pl.ds(r0, R), :] = dv
            new = (acc[0] + _colsum8(dg * g2), acc[1] + _colsum8(dg * g1), acc[2] + _colsum8(dg * g0),
                   acc[3] + _colsum8(dg),
                   acc[4] + _colsum8(dv * v2), acc[5] + _colsum8(dv * v1), acc[6] + _colsum8(dv * v0),
                   acc[7] + _colsum8(dv))
            return new

        acc = lax.fori_loop(0, nchunk, fwd_step, (zero,) * 8)
        rows = lax.broadcasted_iota(jnp.int32, (SUBLANE, LANE), 0)

        def stats(parts):
            out = jnp.zeros((SUBLANE, LANE), f32)
            for k, pt in enumerate(parts):
                out = jnp.where(rows == k, jnp.sum(pt, axis=0, keepdims=True), out)
            return out

        sg_ref[...] = stats(acc[0:4])
        sv_ref[...] = stats(acc[4:8])

        def du_rows(dc, w, r0, last):
            cur = dc[pl.ds(r0, R), :]
            nxt = dc[pl.ds(pl.multiple_of(jnp.minimum(r0 + R, S - SUBLANE), SUBLANE), SUBLANE), :]
            nxt = jnp.where(last, 0.0, nxt)
            return w[2:3] * cur + w[1:2] * _shift_up(cur, nxt, 1) + w[0:1] * _shift_up(cur, nxt, 2)

        def bwd_step(ci, carry):
            r0 = pl.multiple_of(ci * R, R)
            last = ci == nchunk - 1
            dug_ref[pl.ds(r0, R), :] = du_rows(dcg, wgv, r0, last).astype(bf16)
            duv_ref[pl.ds(r0, R), :] = du_rows(dcv, wvv, r0, last).astype(bf16)
            return carry

        lax.fori_loop(0, nchunk, bwd_step, 0)

    col = pl.BlockSpec((S, LANE), lambda j: (0, j))
    w3 = pl.BlockSpec((3, LANE), lambda j: (0, j))
    b1 = pl.BlockSpec((1, LANE), lambda j: (0, j))
    st = pl.BlockSpec((SUBLANE, LANE), lambda j: (0, j))
    return pl.pallas_call(
        body,
        grid=(F // LANE,),
        in_specs=[col, col, col, w3, w3, b1, b1],
        out_specs=[col, col, st, st],
        out_shape=[SDS((S, F), bf16), SDS((S, F), bf16), SDS((SUBLANE, F), f32), SDS((SUBLANE, F), f32)],
        scratch_shapes=[pltpu.VMEM((S, LANE), f32), pltpu.VMEM((S, LANE), f32)],
        compiler_params=_cparams(("parallel",), VMEM_BIG),
        name="conv_bwd",
    )(ug, uv, dact, wg, wv, bg, bv)


def _adam_math(w, g, m, v):
    m = ADAM_B1 * m + (1.0 - ADAM_B1) * g
    v = ADAM_B2 * v + (1.0 - ADAM_B2) * (g * g)
    m_hat = m / (1.0 - ADAM_B1 ** ADAM_STEP)
    v_hat = v / (1.0 - ADAM_B2 ** ADAM_STEP)
    delta = -ADAM_LR * (m_hat / (jnp.sqrt(v_hat) + ADAM_EPS) + ADAM_WD * w)
    return delta, m, v


def _adamw(w, m, v, g, name):
    R, C = w.shape
    parts = len(g.shape) == 3
    tr = R
    if R % 16 == 0:
        for t in range(R, 0, -16):
            if R % t == 0 and t * C * 4 <= ADAM_BLOCK_BYTES:
                tr = t
                break

    def body(w_ref, m_ref, v_ref, g_ref, go_ref, d_ref, mo_ref, vo_ref):
        if parts:
            gv = ((g_ref[0].astype(f32) + g_ref[1].astype(f32)) + g_ref[2].astype(f32)) + g_ref[3].astype(f32)
        else:
            gv = g_ref[...]
        go_ref[...] = gv
        d, mn, vn = _adam_math(w_ref[...], gv, m_ref[...], v_ref[...])
        d_ref[...] = d
        mo_ref[...] = mn
        vo_ref[...] = vn

    row = pl.BlockSpec((tr, C), lambda i: (i, 0))
    gspec = pl.BlockSpec((4, tr, C), lambda i: (0, i, 0)) if parts else row
    if isinstance(g, _Rows):
        lo, g = g.lo, g.full
        assert lo % (2 * SUBLANE) == 0 and tr % (2 * SUBLANE) == 0
        gspec = pl.BlockSpec((pl.Element(4), pl.Element(tr), pl.Element(C)),
                             lambda i: (0, pl.multiple_of(lo + i * tr, 2 * SUBLANE), 0))
    return pl.pallas_call(
        body,
        grid=(R // tr,),
        in_specs=[row, row, row, gspec],
        out_specs=[row] * 4,
        out_shape=[SDS((R, C), f32)] * 4,
        compiler_params=_cparams(("parallel",), VMEM_BIG),
        name=name,
    )(w, m, v, g)


def _sum8(parts, name):
    _, _, R, C = parts.shape

    def body(p_ref, o_ref):
        acc = p_ref[0, 0]
        for c in range(2):
            for k in range(4):
                if c or k:
                    acc = acc + p_ref[c, k]
        o_ref[...] = acc

    return pl.pallas_call(body, out_shape=SDS((R, C), f32), name=name)(parts)


def _pair_add(by_core, b, name):
    _, K, R, C = by_core.shape
    tr = R // 2 if R % 32 == 0 else R

    def body(c_ref, a_ref, b_ref, o_ref):
        o_ref[...] = (a_ref[0].astype(f32) + b_ref[...].astype(f32)).astype(bf16)

    blk = pl.BlockSpec((1, tr, C), lambda k, i, c: (k, i, 0))
    return pl.pallas_call(
        body,
        grid_spec=pltpu.PrefetchScalarGridSpec(
            num_scalar_prefetch=1,
            grid=(K, R // tr),
            in_specs=[pl.BlockSpec((1, 1, tr, C), lambda k, i, c: (c[0], k, i, 0)), blk],
            out_specs=blk,
        ),
        out_shape=SDS((K, R, C), bf16),
        compiler_params=_cparams(("parallel", "parallel")),
        name=name,
    )(lax.axis_index("c").astype(jnp.int32).reshape(1), by_core, b)


_ANY = pl.BlockSpec(memory_space=pl.ANY)


def _chip_out_shape(src, gather):
    return SDS((4,) + tuple(src.shape if gather else src.shape[1:]), src.dtype)


def _fill_own(out, src, gather):
    mine = 2 * lax.axis_index("x") + lax.axis_index("y")
    own = src if gather else lax.dynamic_index_in_dim(src, mine, axis=0, keepdims=False)
    return lax.dynamic_update_index_in_dim(out, own, mine, axis=0)


_HBM = pl.BlockSpec(memory_space=pltpu.HBM)
_SEM = pl.BlockSpec(memory_space=pltpu.SEMAPHORE)
_EFFECT = pltpu.SideEffectType.DATAFLOW_SIDE_EFFECTING
_SPLIT_PEERS = {"chip_gather": 3, "chip_gather_wide": 3, "chip_xchg": 3, "core_fill": 4, "core_swap": 1}


def _split_land(src, kind):
    if kind == "chip_gather_wide":
        return SDS((4, 2) + tuple(src.shape), src.dtype)
    if kind == "core_fill":
        return SDS((SUBLANE, LANE), src.dtype)
    if kind == "core_swap":
        return SDS(tuple(src.shape[1:]), src.dtype)
    return _chip_out_shape(src, kind == "chip_gather")


def _split_copies(src_ref, land_ref, sems, kind):
    x, y, c = lax.axis_index("x"), lax.axis_index("y"), lax.axis_index("c")
    n = _SPLIT_PEERS[kind]
    if kind == "core_fill":
        routes = [((x, y, 1 - c), src_ref.at[k, c], src_ref.at[k, c], src_ref.at[k, 1 - c]) for k in range(n)]
    elif kind == "core_swap":
        routes = [((x, y, 1 - c), src_ref.at[1 - c], land_ref, land_ref)]
    else:
        mine = 2 * x + y
        gather = kind != "chip_xchg"
        slot = (lambda k: land_ref.at[k, c]) if kind == "chip_gather_wide" else (lambda k: land_ref.at[k])
        routes = [((px, py, c), src_ref if gather else src_ref.at[2 * px + py], slot(mine), slot(2 * px + py))
                  for px, py in [(1 - x, y), (x, 1 - y), (1 - x, 1 - y)]]
    sends, recvs = [], []
    for j, (peer, piece, there, here) in enumerate(routes):
        sends.append(pltpu.make_async_remote_copy(src_ref=piece, dst_ref=there, send_sem=sems[j],
                                                  recv_sem=sems[n + j], device_id=peer, device_id_type=MESH))
        recvs.append(pltpu.make_async_remote_copy(src_ref=piece, dst_ref=here, send_sem=sems[j],
                                                  recv_sem=sems[n + j], device_id=peer, device_id_type=MESH))
    return sends, recvs


def _split_start(src, kind, name, after=None):
    land = _split_land(src, kind)
    ns = 2 * _SPLIT_PEERS[kind]
    n_in = 2 if after is None else 3

    def body(*refs):
        src_ref, land_ref = refs[:2]
        outs = refs[n_in:]
        for cp in _split_copies(src_ref, land_ref, outs[:ns], kind)[0]:
            cp.start()
        token = outs[ns + 2]
        token[...] = jnp.zeros_like(token)

    res = pl.pallas_call(
        body,
        name=name,
        out_shape=(pltpu.SemaphoreType.DMA(()),) * ns
        + (pltpu.HBM(src.shape, src.dtype), pltpu.HBM(land.shape, land.dtype), SDS((SUBLANE, LANE), f32)),
        in_specs=(_HBM, _HBM) + (() if after is None else (_ANY,)),
        out_specs=(_SEM,) * ns + (_HBM, _HBM, pl.BlockSpec(memory_space=pltpu.VMEM)),
        input_output_aliases={0: ns, 1: ns + 1},
        compiler_params=pltpu.CompilerParams(has_side_effects=_EFFECT),
    )(pltpu.with_memory_space_constraint(src, pltpu.HBM),
      pltpu.with_memory_space_constraint(lax.empty(land.shape, land.dtype), pltpu.HBM),
      *(() if after is None else (after,)))
    return (res[:ns], res[ns], res[ns + 1]), res[ns + 2]


def _split_wait(state, after, kind, name):
    sems, src_thru, land_thru = state
    ns = 2 * _SPLIT_PEERS[kind]

    def body(src_ref, land_ref, *rest):
        sends, recvs = _split_copies(src_ref, land_ref, rest[:ns], kind)
        for cp in recvs:
            cp.wait_recv()
        for cp in sends:
            cp.wait_send()

    src_out, got = pl.pallas_call(
        body,
        name=name,
        out_shape=(pltpu.HBM(src_thru.shape, src_thru.dtype), pltpu.HBM(land_thru.shape, land_thru.dtype)),
        in_specs=(_HBM, _HBM) + (_SEM,) * ns + (_ANY,),
        out_specs=(_HBM, _HBM),
        input_output_aliases={0: 0, 1: 1},
        compiler_params=pltpu.CompilerParams(has_side_effects=_EFFECT),
    )(src_thru, land_thru, *sems, after)
    if kind == "core_swap":
        return got, src_out
    if kind == "core_fill":
        return src_out
    if kind == "chip_gather_wide":
        mine = 2 * lax.axis_index("x") + lax.axis_index("y")
        zero = jnp.zeros((), mine.dtype)
        return lax.dynamic_update_slice(got, src_out[None, None], (mine, lax.axis_index("c").astype(mine.dtype))
                                        + (zero,) * src_out.ndim)
    return _fill_own(got, src_out, kind == "chip_gather")


def _core_fill(both, name):
    n = both.shape[0]

    def body(in_ref, out_ref, send_sems, recv_sems):
        x, y, c = lax.axis_index("x"), lax.axis_index("y"), lax.axis_index("c")
        sends = [pltpu.make_async_remote_copy(src_ref=out_ref.at[k, c], dst_ref=out_ref.at[k, c],
                                              send_sem=send_sems.at[k], recv_sem=recv_sems.at[k],
                                              device_id=(x, y, 1 - c), device_id_type=MESH) for k in range(n)]
        recvs = [pltpu.make_async_remote_copy(src_ref=out_ref.at[k, c], dst_ref=out_ref.at[k, 1 - c],
                                              send_sem=send_sems.at[k], recv_sem=recv_sems.at[k],
                                              device_id=(x, y, 1 - c), device_id_type=MESH) for k in range(n)]
        for cp in sends:
            cp.start()
        for cp in recvs:
            cp.wait_recv()
        for cp in sends:
            cp.wait_send()

    return pl.pallas_call(
        body,
        in_specs=[_ANY],
        out_specs=_ANY,
        out_shape=SDS(both.shape, both.dtype),
        scratch_shapes=[pltpu.SemaphoreType.DMA((n,)), pltpu.SemaphoreType.DMA((n,))],
        input_output_aliases={0: 0},
        name=name,
    )(both)


def _core_gather(src, name):
    def body(src_ref, out_ref, send_sem, recv_sem):
        x, y, c = lax.axis_index("x"), lax.axis_index("y"), lax.axis_index("c")
        cp = pltpu.make_async_remote_copy(src_ref=src_ref, dst_ref=out_ref.at[c], send_sem=send_sem,
                                          recv_sem=recv_sem, device_id=(x, y, 1 - c), device_id_type=MESH)
        cp.start()
        pltpu.make_async_remote_copy(src_ref=src_ref, dst_ref=out_ref.at[1 - c], send_sem=send_sem,
                                     recv_sem=recv_sem, device_id=(x, y, 1 - c), device_id_type=MESH).wait_recv()
        cp.wait_send()

    out = pl.pallas_call(
        body,
        in_specs=[_ANY],
        out_specs=_ANY,
        out_shape=SDS((2,) + tuple(src.shape), src.dtype),
        scratch_shapes=[pltpu.SemaphoreType.DMA, pltpu.SemaphoreType.DMA],
        name=name,
    )(src)
    return lax.dynamic_update_index_in_dim(out, src, lax.axis_index("c"), axis=0)


_PACK_A = (("w_in", (1088, 1024)),)
_PACK_B = (("w_ba", (512, 128)), ("w_bh", (512, 128)), ("w_out", (128, 1024)), ("w_up", (704, 1024)),
           ("w_down", (352, 1024)))
_PACK_SIZES = _PACK_A + _PACK_B
_TRANSPOSED = ("w_in", "w_up")


def _slab_rows(sizes):
    return sum(r * c for _, (r, c) in sizes) // D_MODEL


def _pack_rows(d, sizes):
    n = d[sizes[0][0]].shape[0]
    return jnp.concatenate([d[k].reshape(n, -1, D_MODEL) for k, _ in sizes], axis=1)


def _unpack_rows(slab, sizes):
    n = slab.shape[0]
    out, lo = {}, 0
    for key, (r, c) in sizes:
        rows = r * c // D_MODEL
        out[key] = slab[:, lo:lo + rows].reshape(n, r, c)
        lo += rows
    return out


def _by_core(gslab):
    return jnp.swapaxes(gslab.reshape((4, 2) + gslab.shape[1:]), 0, 1)


def _cols_to_full(t):
    return jnp.swapaxes(t, 0, 1).reshape(t.shape[1], -1)


def _full_to_cols(t):
    K = t.shape[0]
    return jnp.swapaxes(t.reshape(K, 8, -1), 0, 1)


_SMALL = (("pre_mix_norm", (1, 1024)), ("rel_bias", (32, 24)), ("hgrn_lb_raw", (2, 512)), ("hgrn_norm", (1, 128)),
          ("post_mix_norm", (1, 1024)), ("pre_ffn_norm", (1, 1024)), ("conv_b", (1, 5632)),
          ("post_ffn_norm", (1, 1024)))
_SMALL_ROWS = 96
_CONVW_ROWS = 136


_SMALL_USED = sum(r * c for _, (r, c) in _SMALL)


def _pack_small(d, extra=None):
    flat = jnp.concatenate([d[k].reshape(-1) for k, _ in _SMALL] + ([] if extra is None else [extra.reshape(-1)]))
    flat = jnp.pad(flat, (0, _SMALL_ROWS * LANE - flat.shape[0]))
    return flat.reshape(_SMALL_ROWS, LANE)


def _unpack_small(p):
    flat = p.reshape(-1)
    out, lo = {}, 0
    for k, shp in _SMALL:
        n = shp[0] * shp[1]
        out[k] = flat[lo:lo + n].reshape(shp)
        lo += n
    return out


def _local_step(x, tgt, P, plan):
    S = x.shape[0]
    P = dict(P)
    lb = _lb_fwd(P["hgrn_lb_raw"])
    hs = _prep(x, P["pre_mix_norm"], plan.start_token())
    h1 = hs[0]
    consts = [_bias_consts(d, plan.start_token()) for d in DILATIONS]
    biases, dep = [], h1
    for g in range(N_GROUPS):
        tab_t = P["rel_bias"][:, 8 * g:8 * g + 8].T
        dep = _bias_build(tab_t, consts[g][0], consts[g][1], f"bias_build{g}", dep)
        biases.append(dep.reshape(8, ATTN_BLOCK, 2 * ATTN_BLOCK))
    W = dict(plan.weights_a(dep))
    qkv0, hg, gc = _mm_fanout(h1, [W["wt_qkv"][0], W["wt_hg"], W["wt_gate"]], "nt", [bf16, f32, bf16], "proj_natural")
    qkv = [qkv0] + [_mm(hs[g], W["wt_qkv"][g], "nt", bf16, f"proj_qkv{g}") for g in (1, 2)]
    obuf, lbuf, token = [], [], None
    for g, d in enumerate(DILATIONS):
        o_g, l_g = _attn_fwd(qkv[g], biases[g], (S // d) // ATTN_BLOCK, f"attn_fwd{g}", after=token)
        lbuf.append(l_g)
        obuf.append(o_g)
        if g == 0:
            token = plan.forward_b(o_g)
    y_attn, y_attn_b, w0, w1, w2 = _attn_merge(obuf[0], obuf[1], obuf[2], lbuf[0], lbuf[1], lbuf[2])
    y_hgrn, o_raw, ck = _hgrn_fwd(hg, lb, P["hgrn_norm"])
    wb = plan.weights_b(y_hgrn)
    P["conv_w"] = wb.pop("conv_w")
    W.update(wb)
    a, b, merged = _gate_fwd(y_attn_b, y_hgrn, W["w_ba"], W["w_bh"], gc)
    mo, x1, h2 = _mid_fwd(x, merged, W["w_out"], P["post_mix_norm"], P["pre_ffn_norm"])
    ug, uv = _mm_fanout(h2, [W["wt_up_g"], W["wt_up_v"]], "nt", [bf16, bf16], "up_proj")
    cw_g, cw_v = P["conv_w"][:, :D_FF], P["conv_w"][:, D_FF:]
    cb_g, cb_v = P["conv_b"][:, :D_FF], P["conv_b"][:, D_FF:]
    act = _conv_fwd(ug, uv, cw_g, cw_v, cb_g, cb_v)
    loss, dy, dfo, g_post_ffn = _final(x1, act, W["w_down"], tgt, P["post_ffn_norm"])
    gW_down = _mm(act, dfo, "tn", bf16, "gw_down")
    dact = _mm(dfo, W["w_down"], "nt", bf16, "d_act")
    dug, duv, st_g, st_v = _conv_bwd(ug, uv, dact, cw_g, cw_v, cb_g, cb_v)
    gW_up_g = _mm(dug, h2, "tn", bf16, "gw_up_gate")
    gW_up_v = _mm(duv, h2, "tn", bf16, "gw_up_val")
    dx1, dmo, g_pre_ffn, g_post_mix = _mid_bwd(dy, dug, duv, W["wt_up_g"], W["wt_up_v"], x1, mo, P["pre_ffn_norm"],
                                               P["post_mix_norm"])
    gW_out = _mm(merged, dmo, "tn", bf16, "gw_out")
    da, db, dgc, dyattn, dyhgrn = _gate_bwd(dmo, W["w_out"], a, b, gc, W["w_ba"], W["w_bh"])
    gW_ba = _mm(y_attn_b, da, "tn", bf16, "gw_ba")
    gW_bh = _mm(y_hgrn, db, "tn", bf16, "gw_bh")
    big_b = dict(w_ba=gW_ba, w_bh=gW_bh, w_out=gW_out, w_up=[gW_up_g, gW_up_v], w_down=gW_down)
    dos = _attn_merge_bwd(dyattn, y_attn, w0, w1, w2, after=plan.grads_b_start(big_b))
    dq_h, df_h, dv_h, dog_h, glb8, gnw8 = _hgrn_bwd(hg, o_raw, dyhgrn, ck, lb, P["hgrn_norm"],
                                                   after=plan.grads_b_exchange(dos[5]))
    dhg = [dq_h, df_h, dv_h, dog_h]
    g_lb_raw = _lb_bwd(P["hgrn_lb_raw"], glb8[0:1])
    gn = gnw8[0:1]
    g_hgrn_norm = (gn[:, 0:128] + gn[:, 128:256]) + (gn[:, 256:384] + gn[:, 384:512])
    dqkvs, gW_qkv, g_rel = [], [], []
    for g, d in enumerate(DILATIONS):
        dq, dk, dv, dbias = _attn_bwd(qkv[g], biases[g], dos[g], dos[3 + g], lbuf[g], (S // d) // ATTN_BLOCK,
                                      f"attn_bwd{g}")
        dqkvs.append([dq, dk, dv])
        gW_qkv.append(_mm(dqkvs[g], hs[g], "tn", bf16, f"gw_qkv{g}"))
        g_rel.append(_bias_grad(dbias.reshape(8, -1), consts[g][0], f"bias_grad{g}"))
    gW_hg = _mm(dhg, h1, "tn", bf16, "gw_hg")
    gW_gate = _mm(dgc, h1, "tn", bf16, "gw_gate")
    gW_in = gW_qkv + [gW_hg, gW_gate]
    token = plan.grads_a_start(gW_in)
    dh_perm = [_mm(dqkvs[g], W["wt_qkv"][g], "nn", f32, f"dh1_qkv{g}", after=token) for g in (1, 2)]
    token = plan.grads_a_exchange(dh_perm[1])
    dh_main = _mm(dqkvs[0] + dhg + [dgc], [W["wt_qkv"][0], W["wt_hg"], W["wt_gate"]], "nn", f32, "dh1_main",
                  after=token)
    grad_x, g_pre_mix = _first_bwd(x, dx1, dh_main, dh_perm[0], dh_perm[1], P["pre_mix_norm"])

    g_conv_w = jnp.concatenate([st_g[0:3], st_v[0:3]], axis=1)
    g_conv_b = jnp.concatenate([st_g[3:4], st_v[3:4]], axis=1)
    small = dict(pre_mix_norm=g_pre_mix, rel_bias=jnp.concatenate(g_rel, axis=1), hgrn_lb_raw=g_lb_raw,
                 hgrn_norm=g_hgrn_norm, post_mix_norm=g_post_mix, pre_ffn_norm=g_pre_ffn, conv_b=g_conv_b,
                 post_ffn_norm=g_post_ffn, conv_w=g_conv_w)
    return loss, grad_x, gW_in, big_b, small


def _weights_a(both):
    wt = both.reshape(-1, D_MODEL)
    return dict(
        wt_qkv=[_Rows(wt, g * QKV_G, QKV_G) for g in range(N_GROUPS)],
        wt_hg=_Rows(wt, 3 * QKV_G, 4 * HGRN_W),
        wt_gate=_Rows(wt, 3 * QKV_G + 4 * HGRN_W, wt.shape[0] - 3 * QKV_G - 4 * HGRN_W),
    )


def _weights_b(slabs):
    sh = _unpack_rows(slabs, _PACK_B)
    wt_up = sh["w_up"].reshape(-1, D_MODEL)
    return dict(
        w_ba=_cols_to_full(sh["w_ba"]),
        w_bh=_cols_to_full(sh["w_bh"]),
        w_out=sh["w_out"].reshape(D_MODEL, D_MODEL),
        wt_up_g=wt_up[:D_FF],
        wt_up_v=wt_up[D_FF:],
        w_down=sh["w_down"].reshape(D_FF, D_MODEL),
    )


def _dest_rows(sections, height):
    out = []
    for j in range(8):
        lo, hi, off, pieces = j * height, (j + 1) * height, 0, []
        for s in sections:
            a, b = max(lo, off), min(hi, off + s.shape[0])
            if a < b:
                pieces.append(s[a - off:b - off])
            off += s.shape[0]
        out.append(pieces[0] if len(pieces) == 1 else jnp.concatenate(pieces, axis=0))
    return out


def _grad_blocks_a(sections):
    rows = _dest_rows(sections, 1088)
    return jnp.stack([jnp.stack([rows[2 * k + c].astype(bf16) for k in range(4)]) for c in range(2)])


def _grad_slab_b(g):
    shards = dict(w_ba=_full_to_cols(g["w_ba"]), w_bh=_full_to_cols(g["w_bh"]), w_out=g["w_out"].reshape(8, 128, D_MODEL),
                  w_up=jnp.stack(_dest_rows(g["w_up"], 704)), w_down=g["w_down"].reshape(8, 352, D_MODEL))
    return _pack_rows({k: v.astype(bf16) for k, v in shards.items()}, _PACK_B)


_CONVW_SLAB_ROWS = 16


class _Traffic:
    def __init__(self, slab_a, slab_b, conv_w):
        hi = conv_w.astype(bf16)
        r1 = conv_w - hi.astype(f32)
        mid = r1.astype(bf16)
        lo = (r1 - mid.astype(f32)).astype(bf16)
        bits = jnp.stack([hi, mid, lo]).reshape(-1)
        tail = jnp.pad(bits, (0, _CONVW_SLAB_ROWS * D_MODEL - bits.shape[0])).reshape(_CONVW_SLAB_ROWS, D_MODEL)
        self.slab_b = jnp.concatenate([slab_b, tail], axis=0)
        self.state_a, tok = _split_start(slab_a, "chip_gather_wide", "ag_a_start")
        self.state_b, self.token = _split_start(self.slab_b, "chip_gather_wide", "ag_b_start", after=tok)
        self.state = None
        self.state_gb = None

    def start_token(self):
        return self.token

    def weights_a(self, after):
        half = _split_wait(self.state_a, after, "chip_gather_wide", "ag_a_wait")
        return _weights_a(_core_fill(half, "ag_a_cores"))

    def forward_b(self, after):
        half = _split_wait(self.state_b, after, "chip_gather_wide", "ag_b_wait")
        self.state, token = _split_start(half, "core_fill", "ag_b_cores_start")
        return token

    def weights_b(self, after):
        both = _split_wait(self.state, after, "core_fill", "ag_b_cores_wait")
        slabs = both.reshape((8,) + tuple(self.slab_b.shape))
        rows = _slab_rows(_PACK_B)
        out = _weights_b(slabs[:, :rows])
        pieces = slabs[:, rows:].reshape(8, -1)[:, :3 * 3 * 704].reshape(8, 3, 3, 704).astype(f32)
        out["conv_w"] = _cols_to_full((pieces[:, 0] + pieces[:, 1]) + pieces[:, 2])
        return out

    def grads_b_start(self, grads):
        self.state, token = _split_start(_by_core(_grad_slab_b(grads)), "core_swap", "rs_b_cores_start")
        return token

    def grads_b_exchange(self, after):
        from_sib, by_core = _split_wait(self.state, after, "core_swap", "rs_b_cores_wait")
        self.state_gb, token = _split_start(_pair_add(by_core, from_sib, "rs_b_pair_add"), "chip_xchg", "rs_b_start")
        return token

    def grads_a_start(self, sections):
        self.state, token = _split_start(_grad_blocks_a(sections), "core_swap", "rs_a_cores_start")
        return token

    def grads_a_exchange(self, after):
        from_sib, by_core = _split_wait(self.state, after, "core_swap", "rs_a_cores_wait")
        self.state, token = _split_start(_pair_add(by_core, from_sib, "rs_a_pair_add"), "chip_xchg", "rs_a_start")
        return token

    def parts(self, after):
        slab = _split_wait(self.state_gb, after, "chip_xchg", "rs_b_wait")
        parts, lo = _unpack_rows(slab, _PACK_B), 0
        for key, (r, c) in _PACK_B:
            if c == D_MODEL:
                parts[key] = _Rows(slab, lo, r)
            lo += r * c // D_MODEL
        parts["w_in"] =_split_wait(self.state, after, "chip_xchg", "rs_a_wait")
        return parts


def kernel(x, pre_mix_norm, w_in, rel_bias, hgrn_lb_raw, hgrn_norm, w_branch_attn, w_branch_hgrn, w_out, post_mix_norm, pre_ffn_norm, w_up, conv_w, conv_b, w_down, post_ffn_norm, loss_target, m_pre_mix_norm, m_w_in, m_rel_bias, m_hgrn_lb_raw, m_hgrn_norm, m_w_branch_attn, m_w_branch_hgrn, m_w_out, m_post_mix_norm, m_pre_ffn_norm, m_w_up, m_conv_w, m_conv_b, m_w_down, m_post_ffn_norm, v_pre_mix_norm, v_w_in, v_rel_bias, v_hgrn_lb_raw, v_hgrn_norm, v_w_branch_attn, v_w_branch_hgrn, v_w_out, v_post_mix_norm, v_pre_ffn_norm, v_w_up, v_conv_w, v_conv_b, v_w_down, v_post_ffn_norm):
    ci = lax.axis_index("c")
    dev = 4 * lax.axis_index("x") + 2 * lax.axis_index("y") + ci
    tr = lambda t: jnp.swapaxes(t[0], 0, 1)
    wts = dict(w_in=tr(w_in), w_ba=w_branch_attn[0], w_bh=w_branch_hgrn[0], w_out=w_out[0], w_up=tr(w_up),
               w_down=w_down[0])
    mom = dict(w_in=tr(m_w_in), w_ba=m_w_branch_attn[0], w_bh=m_w_branch_hgrn[0], w_out=m_w_out[0], w_up=tr(m_w_up),
               w_down=m_w_down[0])
    var = dict(w_in=tr(v_w_in), w_ba=v_w_branch_attn[0], w_bh=v_w_branch_hgrn[0], w_out=v_w_out[0], w_up=tr(v_w_up),
               w_down=v_w_down[0])
    small_w = dict(pre_mix_norm=pre_mix_norm, rel_bias=rel_bias, hgrn_lb_raw=hgrn_lb_raw, hgrn_norm=hgrn_norm,
                   post_mix_norm=post_mix_norm, pre_ffn_norm=pre_ffn_norm, conv_b=conv_b, post_ffn_norm=post_ffn_norm)
    small_m = dict(pre_mix_norm=m_pre_mix_norm, rel_bias=m_rel_bias, hgrn_lb_raw=m_hgrn_lb_raw, hgrn_norm=m_hgrn_norm,
                   post_mix_norm=m_post_mix_norm, pre_ffn_norm=m_pre_ffn_norm, conv_b=m_conv_b,
                   post_ffn_norm=m_post_ffn_norm)
    small_v = dict(pre_mix_norm=v_pre_mix_norm, rel_bias=v_rel_bias, hgrn_lb_raw=v_hgrn_lb_raw, hgrn_norm=v_hgrn_norm,
                   post_mix_norm=v_post_mix_norm, pre_ffn_norm=v_pre_ffn_norm, conv_b=v_conv_b,
                   post_ffn_norm=v_post_ffn_norm)

    plan = _Traffic(wts["w_in"].astype(bf16),
                    _pack_rows({k: wts[k].astype(bf16)[None] for k, _ in _PACK_B}, _PACK_B)[0], conv_w[0])

    loss8, grad_x, _, _, small = _local_step(x[0], loss_target[0], small_w, plan)
    spack = jnp.concatenate([_pack_small(small, loss8[0, 0:1]),
                             jnp.pad(small["conv_w"].reshape(-1, LANE), ((0, _CONVW_ROWS - 132), (0, 0)))], axis=0)
    small_state, token = _split_start(spack, "chip_gather", "ag_small_start")

    parts = plan.parts(token)
    outs_big = {}
    for k, _ in _PACK_SIZES:
        outs_big[k] = _adamw(wts[k], mom[k], var[k], parts[k], "adamw_" + k)

    by_chip = _split_wait(small_state, outs_big["w_in"][1], "chip_gather", "ag_small_wait")
    allp = _core_gather(by_chip, "ag_small_cores")
    ssum = _sum8(allp, "small_sum")
    gs = ssum[:_SMALL_ROWS]
    loss = ssum[_SMALL_USED // LANE, _SMALL_USED % LANE]
    res_small = _adamw(_pack_small(small_w), _pack_small(small_m), _pack_small(small_v), gs, "adamw_small")
    sm = [_unpack_small(t) for t in res_small]
    g_cw_full = ssum[_SMALL_ROWS:_SMALL_ROWS + 132].reshape(3, 2 * D_FF)
    g_cw = lax.dynamic_slice_in_dim(g_cw_full, dev * 704, 704, axis=1)
    res_cw = _adamw(conv_w[0], m_conv_w[0], v_conv_w[0], g_cw, "adamw_conv_w")

    def pick(i):
        def big_(k):
            t = outs_big[k][i]
            return (jnp.swapaxes(t, 0, 1) if k in _TRANSPOSED else t)[None]
        return [sm[i]["pre_mix_norm"], big_("w_in"), sm[i]["rel_bias"], sm[i]["hgrn_lb_raw"], sm[i]["hgrn_norm"],
                big_("w_ba"), big_("w_bh"), big_("w_out"), sm[i]["post_mix_norm"], sm[i]["pre_ffn_norm"],
                big_("w_up"), res_cw[i][None], sm[i]["conv_b"], big_("w_down"), sm[i]["post_ffn_norm"]]

    return (loss, grad_x[None], *pick(0), *pick(1), *pick(2), *pick(3))
```

```python
import functools
import math

import jax
import jax.numpy as jnp
from jax import lax
from jax.experimental import pallas as pl
from jax.experimental.pallas import tpu as pltpu

f32 = jnp.float32
bf16 = jnp.bfloat16
SDS = jax.ShapeDtypeStruct
HIGHEST = lax.Precision.HIGHEST
MESH = pl.DeviceIdType.MESH

NN = (((1,), (0,)), ((), ()))
NT = (((1,), (1,)), ((), ()))
TN = (((0,), (0,)), ((), ()))

D_MODEL = 1024
N_GROUPS = 3
DILATIONS = (1, 4, 16)
HEAD_DIM = 64
ATTN_BLOCK = 128
QKV_G = 1536
ATTN_OUT = 512
HGRN_W = 512
HGRN_CHUNK = 32
D_FF = 2816
NUM_BUCKETS = 32
MAX_EXACT = 16
MAX_DISTANCE = 2048
NEG_INF = -1e30
EPS = 1e-6
LANE = 128
SUBLANE = 8
VMEM_BIG = 48 * 1024 * 1024
MM_ROWS = 512
MM_OUT_BYTES = 8 * 1024 * 1024
ADAM_BLOCK_BYTES = 2304 * 1024

ADAM_LR, ADAM_B1, ADAM_B2, ADAM_EPS, ADAM_WD, ADAM_STEP = 0.001, 0.9, 0.999, 1e-08, 0.01, 10


def _pick(n, pref):
    t = pref
    while t >= LANE:
        if n % t == 0:
            return t
        t //= 2
    return n


def _cparams(sem=None, vmem=None):
    kw = {}
    if sem is not None:
        kw["dimension_semantics"] = sem
    if vmem is not None:
        kw["vmem_limit_bytes"] = vmem
    return pltpu.CompilerParams(**kw)


def _sigmoid(x):
    return jax.nn.sigmoid(x)


def _colsum8(x):
    return x.reshape(x.shape[0] // SUBLANE, SUBLANE, x.shape[1]).sum(axis=0)


class _Rows:
    def __init__(self, full, lo, rows):
        self.full, self.lo, self.shape = full, lo, tuple(full.shape[:-2]) + (rows, full.shape[-1])


def _resident(t):
    if isinstance(t, _Rows):
        return pl.BlockSpec((pl.Element(t.shape[0]), pl.Element(t.shape[1])), lambda i: (t.lo, 0)), t.full
    return pl.BlockSpec(t.shape, lambda i: (0, 0)), t


def _mm(a, b, mode, out_dtype, name, acc=None, after=None, into=None):
    dims = {"nn": NN, "nt": NT, "tn": TN}[mode]
    has_acc = acc is not None
    parts = list(a) if isinstance(a, (list, tuple)) else [a]
    if mode == "tn":
        assert not has_acc
        K, N = b.shape
        widths = [t.shape[1] for t in parts]
        M = sum(widths)
        whole = M * N * 4 <= MM_OUT_BYTES
        assert whole or len(parts) == 1
        tmm = M if whole else M // 2
        ts = _pick(K, 4 * MM_ROWS)
        nk = K // ts

        npart = len(parts)
        narrow = out_dtype != f32
        n_in = npart + (1 if into is None else 2)
        out_spec, out_shape, aliases, extra = pl.BlockSpec((tmm, N), lambda i, k: (i, 0)), SDS((M, N), out_dtype), {}, []
        if into is not None:
            slab, first, lo, shard_rows = into
            assert narrow and slab.dtype == out_dtype and tmm % shard_rows == 0 and slab.shape[2] == N
            per = tmm // shard_rows
            out_spec = pl.BlockSpec((pl.Element(per), pl.Element(shard_rows), pl.Element(N)),
                                    lambda i, k: (first + i * per, lo, 0))
            out_shape, aliases, extra = SDS(slab.shape, out_dtype), {npart + 1: 0}, [slab]

        def body_tn(*refs):
            b_ref, o_ref = refs[npart], refs[n_in]
            acc_ref = refs[n_in + 1] if narrow else o_ref
            k = pl.program_id(1)
            bv = b_ref[...]
            lo = 0
            for a_ref, w in zip(refs[:npart], widths if whole else [tmm]):
                part = lax.dot_general(a_ref[...], bv, dims, preferred_element_type=f32)
                rows = slice(lo, lo + w)
                lo += w

                @pl.when(k == 0)
                def _(part=part, rows=rows):
                    acc_ref[rows, :] = part

                @pl.when(k > 0)
                def _(part=part, rows=rows):
                    acc_ref[rows, :] += part

            if narrow:
                @pl.when(k == nk - 1)
                def _():
                    o_ref[...] = acc_ref[...].astype(out_dtype).reshape(o_ref.shape)

        return pl.pallas_call(
            body_tn,
            grid=(M // tmm, nk),
            in_specs=[pl.BlockSpec((ts, w if whole else tmm), lambda i, k: (k, i)) for w in widths]
            + [pl.BlockSpec((ts, N), lambda i, k: (k, 0))] + [pl.BlockSpec(memory_space=pl.ANY)] * len(extra),
            out_specs=out_spec,
            out_shape=out_shape,
            input_output_aliases=aliases,
            scratch_shapes=[pltpu.VMEM((tmm, N), f32)] if narrow else [],
            compiler_params=_cparams(("parallel", "arbitrary"), VMEM_BIG),
            name=name,
        )(*parts, b, *extra)

    bs = list(b) if isinstance(b, (list, tuple)) else [b]
    widths = [t.shape[1] for t in parts]
    M = parts[0].shape[0]
    kdim = 0 if mode == "nn" else 1
    N = bs[0].shape[1 - kdim]
    tm = _pick(M, MM_ROWS)
    npart, nb = len(parts), len(bs)
    place, bi, lo = [], 0, 0
    for w in widths:
        place.append((bi, lo))
        lo += w
        if lo == bs[bi].shape[kdim]:
            bi, lo = bi + 1, 0
    assert bi == nb and lo == 0

    def body(*refs):
        a_refs, b_refs = refs[:npart], refs[npart:npart + nb]
        c_ref = refs[npart + nb] if has_acc else None
        o_ref = refs[-1]
        part = None
        for a_ref, w, (bi, lo) in zip(a_refs, widths, place):
            b_ref = b_refs[bi]
            if w == bs[bi].shape[kdim]:
                bk = b_ref[...]
            else:
                bk = b_ref[:, lo:lo + w] if mode == "nt" else b_ref[lo:lo + w, :]
            t = lax.dot_general(a_ref[...], bk, dims, preferred_element_type=f32)
            part = t if part is None else part + t
        if has_acc:
            part = part + c_ref[...]
        o_ref[...] = part.astype(out_dtype)

    specs = [pl.BlockSpec((tm, w), lambda i: (i, 0)) for w in widths] + [_resident(t)[0] for t in bs]
    args = parts + [_resident(t)[1] for t in bs]
    aliases = {}
    if has_acc:
        specs.append(pl.BlockSpec((tm, N), lambda i: (i, 0)))
        args.append(acc)
        aliases = {npart + nb: 0}
    if after is not None:
        specs.append(pl.BlockSpec(memory_space=pl.ANY))
        args.append(after)
    return pl.pallas_call(
        body,
        grid=(M // tm,),
        in_specs=specs,
        out_specs=pl.BlockSpec((tm, N), lambda i: (i, 0)),
        out_shape=SDS((M, N), out_dtype),
        input_output_aliases=aliases,
        compiler_params=_cparams(("parallel",), VMEM_BIG),
        name=name,
    )(*args)


def _mm_fanout(a, bs, mode, out_dtypes, name):
    dims = {"nn": NN, "nt": NT}[mode]
    M, K = a.shape
    ns = [b.shape[1] if mode == "nn" else b.shape[0] for b in bs]
    tm = _pick(M, MM_ROWS)
    nb = len(bs)

    def body(a_ref, *refs):
        av = a_ref[...]
        for b_ref, o_ref, dt in zip(refs[:nb], refs[nb:], out_dtypes):
            o_ref[...] = lax.dot_general(av, b_ref[...], dims, preferred_element_type=f32).astype(dt)

    return pl.pallas_call(
        body,
        grid=(M // tm,),
        in_specs=[pl.BlockSpec((tm, K), lambda i: (i, 0))] + [_resident(b)[0] for b in bs],
        out_specs=[pl.BlockSpec((tm, n), lambda i: (i, 0)) for n in ns],
        out_shape=[SDS((M, n), dt) for n, dt in zip(ns, out_dtypes)],
        compiler_params=_cparams(("parallel",), VMEM_BIG),
        name=name,
    )(a, *[_resident(b)[1] for b in bs])


PERM_ROWS = 2048


def _perm_spec(d, cols=LANE):
    return pl.BlockSpec((d, PERM_ROWS // d, cols), lambda i, j: (0, i, j))


def _to_natural(src_ref, dst_ref, d):
    n = src_ref.shape[1]
    for r in range(d):
        dst_ref[pl.ds(r, n, stride=d), :] = src_ref[r]


def _prep(x, w, after=None):
    S, D = x.shape
    R = PERM_ROWS
    nc = D // LANE
    n_in = nc + 1 + (after is not None)

    def body(*refs):
        x_refs, w_ref = refs[:nc], refs[nc]
        h_ref, h4_ref, h16_ref, rs = refs[n_in:]
        ssq = None
        for xr in x_refs:
            v = xr[...]
            t = jnp.sum(v * v, axis=-1, keepdims=True)
            ssq = t if ssq is None else ssq + t
        rinv = lax.rsqrt(ssq * (1.0 / D) + EPS)
        rs[...] = jnp.broadcast_to(rinv, (R, LANE))
        for j, xr in enumerate(x_refs):
            cols = slice(j * LANE, (j + 1) * LANE)
            wj = w_ref[:, cols]
            h_ref[:, cols] = ((xr[...] * rinv) * wj).astype(bf16)
            for d, o_ref in ((4, h4_ref), (16, h16_ref)):
                n = R // d
                for r in range(d):
                    rows = pl.ds(r, n, stride=d)
                    o_ref[r, :, cols] = ((xr[rows, :] * rs[rows, :]) * wj).astype(bf16)

    col = lambda j: pl.BlockSpec((R, LANE), lambda i, j=j: (i, j))
    h, h4, h16 = pl.pallas_call(
        body,
        grid=(S // R,),
        in_specs=[col(j) for j in range(nc)] + [pl.BlockSpec((1, D), lambda i: (0, 0))]
        + ([] if after is None else [pl.BlockSpec(memory_space=pl.ANY)]),
        out_specs=[pl.BlockSpec((R, D), lambda i: (i, 0)), pl.BlockSpec((4, R // 4, D), lambda i: (0, i, 0)),
                   pl.BlockSpec((16, R // 16, D), lambda i: (0, i, 0))],
        out_shape=[SDS((S, D), bf16), SDS((4, S // 4, D), bf16), SDS((16, S // 16, D), bf16)],
        scratch_shapes=[pltpu.VMEM((R, LANE), f32)],
        compiler_params=_cparams(("parallel",), VMEM_BIG),
        name="prep_norm_perm",
    )(*([x] * nc), w, *([] if after is None else [after]))
    return [h, h4.reshape(S, D), h16.reshape(S, D)]


def _rms_parts(xv):
    r = lax.rsqrt(jnp.mean(xv * xv, axis=-1, keepdims=True) + EPS)
    return r, xv * r


def _rms_bwd(xhat, r, w, dy):
    dyw = dy * w
    return r * (dyw - xhat * jnp.mean(dyw * xhat, axis=-1, keepdims=True))


def _mid_fwd(x, merged, w_out, w_pm, w_pf):
    S, D = x.shape
    tm = _pick(S, MM_ROWS)

    def body(x_ref, m_ref, wo_ref, wpm_ref, wpf_ref, mo_ref, x1_ref, h2_ref):
        mo = jnp.dot(m_ref[...], wo_ref[...], preferred_element_type=f32)
        mo_ref[...] = mo
        _, moh = _rms_parts(mo)
        x1 = x_ref[...] + moh * wpm_ref[...]
        x1_ref[...] = x1
        _, x1h = _rms_parts(x1)
        h2_ref[...] = (x1h * wpf_ref[...]).astype(bf16)

    row = pl.BlockSpec((tm, D), lambda i: (i, 0))
    vec = pl.BlockSpec((1, D), lambda i: (0, 0))
    return pl.pallas_call(
        body,
        grid=(S // tm,),
        in_specs=[row, pl.BlockSpec((tm, merged.shape[1]), lambda i: (i, 0)),
                  pl.BlockSpec(w_out.shape, lambda i: (0, 0)), vec, vec],
        out_specs=[row, row, row],
        out_shape=[SDS((S, D), f32), SDS((S, D), f32), SDS((S, D), bf16)],
        compiler_params=_cparams(("parallel",), VMEM_BIG),
        name="out_proj_mid_fwd",
    )(x, merged, w_out, w_pm, w_pf)


def _final(x1, act, w_down, tgt, w_pfn):
    S, D = x1.shape
    tm = _pick(S, MM_ROWS)
    nt = S // tm

    def body(x1_ref, a_ref, wd_ref, t_ref, w_ref, loss_ref, dy_ref, dfo_ref, gw_ref, lacc, gacc):
        i = pl.program_id(0)

        @pl.when(i == 0)
        def _():
            lacc[...] = jnp.zeros_like(lacc)
            gacc[...] = jnp.zeros_like(gacc)

        w = w_ref[...]
        r, foh = _rms_parts(jnp.dot(a_ref[...], wd_ref[...], preferred_element_type=f32))
        y = x1_ref[...] + foh * w
        err = y - t_ref[...]
        lacc[...] += _colsum8(err * err)
        dy = err * (1.0 / D)
        dy_ref[...] = dy
        gacc[...] += _colsum8(dy * foh)
        dfo_ref[...] = _rms_bwd(foh, r, w, dy).astype(bf16)

        @pl.when(i == nt - 1)
        def _():
            loss_ref[...] = jnp.full((SUBLANE, LANE), 0.5 / D, f32) * jnp.sum(lacc[...])
            gw_ref[...] = jnp.sum(gacc[...], axis=0, keepdims=True)

    row = pl.BlockSpec((tm, D), lambda i: (i, 0))
    vec = pl.BlockSpec((1, D), lambda i: (0, 0))
    return pl.pallas_call(
        body,
        grid=(nt,),
        in_specs=[row, pl.BlockSpec((tm, act.shape[1]), lambda i: (i, 0)),
                  pl.BlockSpec(w_down.shape, lambda i: (0, 0)), row, vec],
        out_specs=[pl.BlockSpec((SUBLANE, LANE), lambda i: (0, 0)), row, row, vec],
        out_shape=[SDS((SUBLANE, LANE), f32), SDS((S, D), f32), SDS((S, D), bf16), SDS((1, D), f32)],
        scratch_shapes=[pltpu.VMEM((SUBLANE, D), f32), pltpu.VMEM((SUBLANE, D), f32)],
        compiler_params=_cparams(("arbitrary",), VMEM_BIG),
        name="down_proj_final_loss",
    )(x1, act, w_down, tgt, w_pfn)


MID_BWD_ROWS = 256


def _mid_bwd(dy, dug, duv, wt_g, wt_v, x1, mo, w_pf, w_pm):
    S, D = dy.shape
    tm = _pick(S, MID_BWD_ROWS)
    nt = S // tm

    def body(dy_ref, dug_ref, duv_ref, wg_ref, wv_ref, x1_ref, mo_ref, wpf_ref, wpm_ref,
             dx1_ref, dmo_ref, gpf_ref, gpm_ref, apf, apm):
        i = pl.program_id(0)

        @pl.when(i == 0)
        def _():
            apf[...] = jnp.zeros_like(apf)
            apm[...] = jnp.zeros_like(apm)

        r1, x1h = _rms_parts(x1_ref[...])
        dh2 = jnp.dot(dug_ref[...], wg_ref[...], preferred_element_type=f32) \
            + jnp.dot(duv_ref[...], wv_ref[...], preferred_element_type=f32)
        apf[...] += _colsum8(dh2 * x1h)
        dx1 = dy_ref[...] + _rms_bwd(x1h, r1, wpf_ref[...], dh2)
        dx1_ref[...] = dx1
        rm, moh = _rms_parts(mo_ref[...])
        apm[...] += _colsum8(dx1 * moh)
        dmo_ref[...] = _rms_bwd(moh, rm, wpm_ref[...], dx1).astype(bf16)

        @pl.when(i == nt - 1)
        def _():
            gpf_ref[...] = jnp.sum(apf[...], axis=0, keepdims=True)
            gpm_ref[...] = jnp.sum(apm[...], axis=0, keepdims=True)

    row = pl.BlockSpec((tm, D), lambda i: (i, 0))
    vec = pl.BlockSpec((1, D), lambda i: (0, 0))
    return pl.pallas_call(
        body,
        grid=(nt,),
        in_specs=[row, pl.BlockSpec((tm, dug.shape[1]), lambda i: (i, 0)), pl.BlockSpec((tm, duv.shape[1]), lambda i: (i, 0)),
                  pl.BlockSpec(wt_g.shape, lambda i: (0, 0)), pl.BlockSpec(wt_v.shape, lambda i: (0, 0)),
                  row, row, vec, vec],
        out_specs=[row, row, vec, vec],
        out_shape=[SDS((S, D), f32), SDS((S, D), bf16), SDS((1, D), f32), SDS((1, D), f32)],
        scratch_shapes=[pltpu.VMEM((SUBLANE, D), f32), pltpu.VMEM((SUBLANE, D), f32)],
        compiler_params=_cparams(("arbitrary",), VMEM_BIG),
        name="dh2_mid_bwd",
    )(dy, dug, duv, wt_g, wt_v, x1, mo, w_pf, w_pm)


def _first_bwd(x, dx1, dh_a, dh_b, dh_c, w_pre):
    S, D = x.shape
    tm = _pick(S, 512)
    nt = S // tm
    nc = D // LANE

    def body(*refs):
        x_ref, dx1_ref, a_ref = refs[:3]
        b_refs, c_refs, w_ref = refs[3:3 + nc], refs[3 + nc:3 + 2 * nc], refs[3 + 2 * nc]
        gx_ref, gw_ref, acc, dh_s, sb, sc = refs[4 + 2 * nc:]
        i = pl.program_id(0)

        @pl.when(i == 0)
        def _():
            acc[...] = jnp.zeros_like(acc)

        for j in range(nc):
            cols = slice(j * LANE, (j + 1) * LANE)
            _to_natural(b_refs[j], sb, 4)
            _to_natural(c_refs[j], sc, 16)
            dh_s[:, cols] = (a_ref[:, cols] + sb[...]) + sc[...]
        r, xh = _rms_parts(x_ref[...])
        dh = dh_s[...]
        acc[...] += _colsum8(dh * xh)
        gx_ref[...] = dx1_ref[...] + _rms_bwd(xh, r, w_ref[...], dh)

        @pl.when(i == nt - 1)
        def _():
            gw_ref[...] = jnp.sum(acc[...], axis=0, keepdims=True)

    row = pl.BlockSpec((tm, D), lambda i: (i, 0))
    vec = pl.BlockSpec((1, D), lambda i: (0, 0))
    perm = lambda d: [pl.BlockSpec((d, tm // d, LANE), lambda i, j=j: (0, i, j)) for j in range(nc)]
    return pl.pallas_call(
        body,
        grid=(nt,),
        in_specs=[row, row, row] + perm(4) + perm(16) + [vec],
        out_specs=[row, vec],
        out_shape=[SDS((S, D), f32), SDS((1, D), f32)],
        scratch_shapes=[pltpu.VMEM((SUBLANE, D), f32), pltpu.VMEM((tm, D), f32), pltpu.VMEM((tm, LANE), f32),
                        pltpu.VMEM((tm, LANE), f32)],
        compiler_params=_cparams(("arbitrary",), VMEM_BIG),
        name="first_bwd",
    )(x, dx1, dh_a, *([dh_b.reshape(4, S // 4, D)] * nc), *([dh_c.reshape(16, S // 16, D)] * nc), w_pre)


def _t5_bucket(dist):
    n = jnp.maximum(dist, 0)
    nf = jnp.maximum(n, 1).astype(f32)
    large = MAX_EXACT + (jnp.log(nf / MAX_EXACT) / math.log(MAX_DISTANCE / MAX_EXACT)
                         * (NUM_BUCKETS - MAX_EXACT)).astype(jnp.int32)
    large = jnp.minimum(large, NUM_BUCKETS - 1)
    return jnp.where(n < MAX_EXACT, n, large)


def _bias_consts(d, after=None):
    if after is not None:
        d, _ = lax.optimization_barrier((jnp.int32(d), after))
    blk = ATTN_BLOCK
    rel = jnp.arange(blk)[:, None] + blk - jnp.arange(2 * blk)[None, :]
    in_win = (rel >= 0) & (rel <= blk)
    bucket = _t5_bucket(rel * d).reshape(1, -1)
    onehot = (bucket == jnp.arange(NUM_BUCKETS)[:, None]).astype(f32)
    return onehot, in_win.astype(f32).reshape(1, -1)


def _bias_build(tab_t, onehot, maskf, name, after):
    H = tab_t.shape[0]

    def body(t_ref, oh_ref, m_ref, after_ref, o_ref):
        b = jnp.dot(t_ref[...], oh_ref[...], precision=HIGHEST, preferred_element_type=f32)
        o_ref[...] = jnp.where(m_ref[...] > 0.5, b, NEG_INF)

    vm = pl.BlockSpec(memory_space=pltpu.VMEM)
    return pl.pallas_call(body, out_shape=SDS((H, onehot.shape[1]), f32), name=name,
                          in_specs=[vm, vm, vm, pl.BlockSpec(memory_space=pl.ANY)], out_specs=vm,
                          )(tab_t, onehot, maskf, after)


def _bias_grad(dbias_flat, onehot, name):
    H = dbias_flat.shape[0]

    def body(g_ref, oh_ref, o_ref):
        o_ref[...] = lax.dot_general(oh_ref[...], g_ref[...], NT, precision=HIGHEST, preferred_element_type=f32)

    return pl.pallas_call(body, out_shape=SDS((NUM_BUCKETS, H), f32), name=name)(dbias_flat, onehot)


ATTN_TILE = 512
ATTN_SUB = ATTN_TILE // ATTN_BLOCK
ATTN_HP = 4
ATTN_WIDE = ATTN_HP * LANE


def _qkv_specs(nt):
    tile = (ATTN_TILE, ATTN_WIDE)
    blk = (ATTN_BLOCK, ATTN_WIDE)
    sec = ATTN_OUT // ATTN_WIDE
    cur = lambda off: (lambda h, t: (jnp.minimum(t, nt - 1), off + h))
    prev = lambda off: (lambda h, t: (jnp.maximum(jnp.minimum(t, nt - 1) * ATTN_SUB - 1, 0), off + h))
    return [pl.BlockSpec(tile, cur(0)), pl.BlockSpec(blk, prev(sec)), pl.BlockSpec(tile, cur(sec)),
            pl.BlockSpec(blk, prev(2 * sec)), pl.BlockSpec(tile, cur(2 * sec))]


def _head_masks():
    lane = lax.broadcasted_iota(jnp.int32, (ATTN_BLOCK, LANE), 1)
    return lane < HEAD_DIM


def _stack_heads(x2, low):
    zero = jnp.zeros_like(x2)
    return jnp.concatenate([jnp.where(low, x2, zero), jnp.where(low, zero, x2)], axis=0)


def _attn_fwd(qkv, bias, bps, name, after=None):
    S = qkv.shape[0]
    nt = S // ATTN_TILE
    scale = HEAD_DIM ** -0.5

    def body(q_ref, kp_ref, kc_ref, vp_ref, vc_ref, b_ref, *rest):
        o_ref, l_ref = rest[-2:]
        t = pl.program_id(1)
        low = _head_masks()
        col = lax.broadcasted_iota(jnp.int32, (2 * ATTN_BLOCK, 2 * ATTN_BLOCK), 1)
        for hp in range(ATTN_HP):
            cols = slice(hp * LANE, (hp + 1) * LANE)
            kk = jnp.concatenate([kp_ref[:, cols], kc_ref[:, cols]], axis=0)
            vv = jnp.concatenate([vp_ref[:, cols], vc_ref[:, cols]], axis=0)
            bias2 = b_ref[2 * hp:2 * hp + 2].reshape(2 * ATTN_BLOCK, 2 * ATTN_BLOCK)
            for b in range(ATTN_SUB):
                lo = b * ATTN_BLOCK
                rows = slice(lo, lo + ATTN_BLOCK)
                keys = slice(lo, lo + 2 * ATTN_BLOCK)
                dead = jnp.logical_and((t * ATTN_SUB + b) % bps == 0, col < ATTN_BLOCK)
                q2 = _stack_heads(q_ref[rows, cols], low)
                kb, vb = kk[keys], vv[keys]
                s = lax.dot_general(q2, kb, NT, preferred_element_type=f32) * scale + bias2
                s = jnp.where(dead, NEG_INF, s)
                m = jnp.max(s, axis=-1, keepdims=True)
                p = jnp.exp(s - m)
                l = jnp.sum(p, axis=-1, keepdims=True)
                o2 = jnp.dot(p.astype(bf16), vb, preferred_element_type=f32) / l
                lse = m + jnp.log(l)
                o_ref[rows, cols] = jnp.where(low, o2[:ATTN_BLOCK], o2[ATTN_BLOCK:])
                l_ref[rows, cols] = jnp.where(low, lse[:ATTN_BLOCK], lse[ATTN_BLOCK:])

    tile = pl.BlockSpec((ATTN_TILE, ATTN_WIDE), lambda h, t: (t, h))
    return pl.pallas_call(
        body,
        grid=(4 // ATTN_HP, nt),
        in_specs=_qkv_specs(nt) + [pl.BlockSpec((2 * ATTN_HP, ATTN_BLOCK, 2 * ATTN_BLOCK), lambda h, t: (h, 0, 0))]
        + ([] if after is None else [pl.BlockSpec(memory_space=pl.ANY)]),
        out_specs=[tile, tile],
        out_shape=[SDS((S, ATTN_OUT), f32), SDS((S, ATTN_OUT), f32)],
        compiler_params=_cparams(("parallel", "parallel")),
        name=name,
    )(qkv, qkv, qkv, qkv, qkv, bias, *([] if after is None else [after]))


def _attn_bwd(qkv, bias, do, dvec, lse, bps, name):
    S = qkv.shape[0]
    nt = S // ATTN_TILE
    scale = HEAD_DIM ** -0.5

    def assemble(parts):
        rows = [parts[0][:ATTN_BLOCK]]
        for b in range(ATTN_SUB - 1):
            rows.append(parts[b][ATTN_BLOCK:] + parts[b + 1][:ATTN_BLOCK])
        rows.append(parts[-1][ATTN_BLOCK:])
        return rows

    def body(q_ref, kp_ref, kc_ref, vp_ref, vc_ref, b_ref, do_ref, dvec_ref, lse_ref,
             dq_ref, dk_ref, dv_ref, db_ref, ck, cv):
        t = pl.program_id(1)
        last = ATTN_TILE - ATTN_BLOCK

        @pl.when(t == 0)
        def _():
            ck[...] = jnp.zeros_like(ck)
            cv[...] = jnp.zeros_like(cv)
            db_ref[...] = jnp.zeros_like(db_ref)

        @pl.when(t < nt)
        def _():
            low = _head_masks()
            col = lax.broadcasted_iota(jnp.int32, (2 * ATTN_BLOCK, 2 * ATTN_BLOCK), 1)
            per_row = lambda t2: jnp.concatenate([t2[:, 0:1], t2[:, HEAD_DIM:HEAD_DIM + 1]], axis=0)
            for hp in range(ATTN_HP):
                cols = slice(hp * LANE, (hp + 1) * LANE)
                kk = jnp.concatenate([kp_ref[:, cols], kc_ref[:, cols]], axis=0)
                vv = jnp.concatenate([vp_ref[:, cols], vc_ref[:, cols]], axis=0)
                bias2 = b_ref[2 * hp:2 * hp + 2].reshape(2 * ATTN_BLOCK, 2 * ATTN_BLOCK)
                dk_parts, dv_parts = [], []
                dsum = None
                for b in range(ATTN_SUB):
                    lo = b * ATTN_BLOCK
                    rows = slice(lo, lo + ATTN_BLOCK)
                    keys = slice(lo, lo + 2 * ATTN_BLOCK)
                    dead = jnp.logical_and((t * ATTN_SUB + b) % bps == 0, col < ATTN_BLOCK)
                    q2 = _stack_heads(q_ref[rows, cols], low)
                    do2 = _stack_heads(do_ref[rows, cols].astype(bf16), low)
                    kb, vb = kk[keys], vv[keys]
                    s = lax.dot_general(q2, kb, NT, preferred_element_type=f32) * scale + bias2
                    s = jnp.where(dead, NEG_INF, s)
                    p = jnp.exp(s - per_row(lse_ref[rows, cols]))
                    dp = lax.dot_general(do2, vb, NT, preferred_element_type=f32)
                    ds = p * (dp - per_row(dvec_ref[rows, cols]))
                    dsum = ds if dsum is None else dsum + ds
                    dsb = ds.astype(bf16)
                    dq2 = jnp.dot(dsb, kb, preferred_element_type=f32) * scale
                    dq_ref[rows, cols] = jnp.where(low, dq2[:ATTN_BLOCK], dq2[ATTN_BLOCK:]).astype(bf16)
                    dk_parts.append(lax.dot_general(dsb, q2, TN, preferred_element_type=f32) * scale)
                    dv_parts.append(lax.dot_general(p.astype(bf16), do2, TN, preferred_element_type=f32))
                db_ref[2 * hp:2 * hp + 2] += dsum.reshape(2, ATTN_BLOCK, 2 * ATTN_BLOCK)
                for parts, carry, out_ref in ((dk_parts, ck, dk_ref), (dv_parts, cv, dv_ref)):
                    rws = assemble(parts)
                    out_ref[:last, cols] = carry[:last, cols].astype(bf16)
                    out_ref[last:, cols] = (carry[last:, cols] + rws[0]).astype(bf16)
                    for b in range(ATTN_SUB):
                        carry[b * ATTN_BLOCK:(b + 1) * ATTN_BLOCK, cols] = rws[b + 1]

        @pl.when(t == nt)
        def _():
            dk_ref[...] = ck[...].astype(bf16)
            dv_ref[...] = cv[...].astype(bf16)

    tile = (ATTN_TILE, ATTN_WIDE)
    cur = pl.BlockSpec(tile, lambda h, t: (jnp.minimum(t, nt - 1), h))
    lag = pl.BlockSpec(tile, lambda h, t: (jnp.maximum(t - 1, 0), h))
    bspec = pl.BlockSpec((2 * ATTN_HP, ATTN_BLOCK, 2 * ATTN_BLOCK), lambda h, t: (h, 0, 0))
    return pl.pallas_call(
        body,
        grid=(4 // ATTN_HP, nt + 1),
        in_specs=_qkv_specs(nt) + [bspec, cur, cur, cur],
        out_specs=[cur, lag, lag, bspec],
        out_shape=[SDS((S, ATTN_OUT), bf16), SDS((S, ATTN_OUT), bf16), SDS((S, ATTN_OUT), bf16),
                   SDS((8, ATTN_BLOCK, 2 * ATTN_BLOCK), f32)],
        scratch_shapes=[pltpu.VMEM(tile, f32), pltpu.VMEM(tile, f32)],
        compiler_params=_cparams(("parallel", "arbitrary")),
        name=name,
    )(qkv, qkv, qkv, qkv, qkv, bias, do, dvec, lse)


def _attn_merge(o0, o1, o2, l0, l1, l2):
    S, W = o0.shape
    R = PERM_ROWS

    def body(o0_ref, o1_ref, o2_ref, l0_ref, l1_ref, l2_ref, y_ref, yb_ref, w0_ref, w1_ref, w2_ref,
             so1, so2, sl1, sl2):
        _to_natural(o1_ref, so1, 4)
        _to_natural(l1_ref, sl1, 4)
        _to_natural(o2_ref, so2, 16)
        _to_natural(l2_ref, sl2, 16)
        a, b, c = l0_ref[...], sl1[...], sl2[...]
        m = jnp.maximum(jnp.maximum(a, b), c)
        ea, eb, ec = jnp.exp(a - m), jnp.exp(b - m), jnp.exp(c - m)
        den = (ea + eb) + ec
        w0, w1, w2 = ea / den, eb / den, ec / den
        y = (w0 * o0_ref[...] + w1 * so1[...]) + w2 * so2[...]
        y_ref[...] = y
        yb_ref[...] = y.astype(bf16)
        w0_ref[...] = w0
        w1_ref[...] = w1
        w2_ref[...] = w2

    nat = pl.BlockSpec((R, LANE), lambda i, j: (i, j))
    v4 = lambda t: t.reshape(4, S // 4, W)
    v16 = lambda t: t.reshape(16, S // 16, W)
    return pl.pallas_call(
        body,
        grid=(S // R, W // LANE),
        in_specs=[nat, _perm_spec(4), _perm_spec(16)] * 2,
        out_specs=[nat] * 5,
        out_shape=[SDS((S, W), f32), SDS((S, W), bf16)] + [SDS((S, W), f32)] * 3,
        scratch_shapes=[pltpu.VMEM((R, LANE), f32)] * 4,
        compiler_params=_cparams(("parallel", "parallel"), VMEM_BIG),
        name="attn_merge",
    )(o0, v4(o1), v16(o2), l0, v4(l1), v16(l2))


def _attn_merge_bwd(dy, y, w0, w1, w2, after=None):
    S, W = dy.shape
    R = PERM_ROWS

    def body(dy_ref, y_ref, w0_ref, w1_ref, w2_ref, *rest):
        a0, a1, a2, b0, b1, b2, sa, sb = rest[-8:]
        dyv = dy_ref[...]
        r = lax.broadcasted_iota(jnp.int32, (LANE, LANE), 0) // HEAD_DIM
        c = lax.broadcasted_iota(jnp.int32, (LANE, LANE), 1) // HEAD_DIM
        seg = jnp.where(r == c, 1.0, 0.0).astype(f32)
        cbar = jnp.dot(dyv * y_ref[...], seg, precision=HIGHEST, preferred_element_type=f32)
        w = w0_ref[...]
        a0[...] = (w * dyv).astype(bf16)
        b0[...] = w * cbar
        for d, w_ref, a_ref, b_ref in ((4, w1_ref, a1, b1), (16, w2_ref, a2, b2)):
            w = w_ref[...]
            sa[...] = w * dyv
            sb[...] = w * cbar
            n = R // d
            for k in range(d):
                rows = pl.ds(k, n, stride=d)
                a_ref[k] = sa[rows, :].astype(bf16)
                b_ref[k] = sb[rows, :]

    nat = pl.BlockSpec((R, LANE), lambda i, j: (i, j))
    shapes = lambda dt: [SDS((S, W), dt), SDS((4, S // 4, W), dt), SDS((16, S // 16, W), dt)]
    outs = pl.pallas_call(
        body,
        grid=(S // R, W // LANE),
        in_specs=[nat] * 5 + ([] if after is None else [pl.BlockSpec(memory_space=pl.ANY)]),
        out_specs=[nat, _perm_spec(4), _perm_spec(16)] * 2,
        out_shape=shapes(bf16) + shapes(f32),
        scratch_shapes=[pltpu.VMEM((R, LANE), f32)] * 2,
        compiler_params=_cparams(("parallel", "parallel"), VMEM_BIG),
        name="attn_merge_bwd",
    )(dy, y, w0, w1, w2, *([] if after is None else [after]))
    return [t.reshape(S, W) for t in outs]


HGRN_SB = 256
HGRN_PAIR = 4


def _chunk_masks():
    r = jnp.arange(HGRN_SB)[:, None]
    c = jnp.arange(HGRN_SB)[None, :]
    same = (r // HGRN_CHUNK) == (c // HGRN_CHUNK)
    return jnp.stack([same & (c <= r), same, same & (c >= r)]).astype(bf16)


def _mask_dot(mask, x):
    hi = x.astype(bf16)
    r1 = x - hi.astype(f32)
    mid = r1.astype(bf16)
    lo = (r1 - mid.astype(f32)).astype(bf16)
    p = jnp.dot(mask, jnp.concatenate([hi, mid, lo], axis=1), preferred_element_type=f32)
    n = x.shape[1]
    return (p[:, :n] + p[:, n:2 * n]) + p[:, 2 * n:]


def _hgrn_prep(q_raw, f_raw, lbv, tril, same):
    sq = _sigmoid(q_raw)
    qs = q_raw * sq
    sig = _sigmoid(f_raw)
    f = lbv + (1.0 - lbv) * sig
    g = jnp.log(f)
    k = 1.0 - f
    G = _mask_dot(tril, g)
    GL = _mask_dot(same, g)
    eG = jnp.exp(G)
    einv = jnp.exp(-G)
    edec = jnp.exp(GL - G)
    return dict(sq=sq, qs=qs, sig=sig, f=f, k=k, eG=eG, einv=einv, edec=edec, eGL=jnp.exp(GL),
                qt=qs * eG, kt=k * einv, kd=k * edec)


def _hgrn_fwd(hg, lb, normw):
    S = hg.shape[0]
    sb = HGRN_SB
    nsb = S // sb
    nch = sb // HGRN_CHUNK

    def body(q_ref, f_ref, v_ref, og_ref, lb_ref, nw_ref, m_ref, y_ref, o_ref, ck_ref, st):
        j = pl.program_id(1)

        @pl.when(j == 0)
        def _():
            st[...] = jnp.zeros_like(st)

        tril_m = m_ref[0]
        tril = tril_m.astype(f32) > 0.5

        def one_head(hh):
            cols = slice(hh * LANE, (hh + 1) * LANE)
            ST = st[hh]
            ck_ref[hh, 0] = ST
            pr = _hgrn_prep(q_ref[:, cols], f_ref[:, cols], lb_ref[:, cols], tril_m, m_ref[1])
            qtb, ktb, kdb = pr["qt"].astype(bf16), pr["kt"].astype(bf16), pr["kd"].astype(bf16)
            eGL = pr["eGL"]
            vb = v_ref[:, cols].astype(bf16)
            A = jnp.where(tril, lax.dot_general(qtb, ktb, NT, preferred_element_type=f32), 0.0)
            o = jnp.dot(A.astype(bf16), vb, preferred_element_type=f32)
            outs = []
            for ci in range(nch):
                lo = ci * HGRN_CHUNK
                sl = slice(lo, lo + HGRN_CHUNK)
                outs.append(o[sl] + lax.dot_general(qtb[sl], ST.astype(bf16), NT, preferred_element_type=f32))
                ST = ST * eGL[lo:lo + 1, :] + lax.dot_general(vb[sl], kdb[sl], TN, preferred_element_type=f32)
            st[hh] = ST
            of = jnp.concatenate(outs, axis=0)
            o_ref[:, cols] = of
            rms = lax.rsqrt(jnp.mean(of * of, axis=-1, keepdims=True) + EPS)
            ogv = og_ref[:, cols]
            y_ref[:, cols] = ((of * rms * nw_ref[...]) * (ogv * _sigmoid(ogv))).astype(bf16)

        for hh in range(HGRN_PAIR):
            one_head(hh)

    wide = HGRN_PAIR * LANE
    col = lambda off: pl.BlockSpec((sb, wide), lambda h, j: (j, off // HGRN_PAIR + h))
    return pl.pallas_call(
        body,
        grid=(4 // HGRN_PAIR, nsb),
        in_specs=[col(0), col(4), col(8), col(12), pl.BlockSpec((1, wide), lambda h, j: (0, h)),
                  pl.BlockSpec((1, LANE), lambda h, j: (0, 0)),
                  pl.BlockSpec((3, sb, sb), lambda h, j: (0, 0, 0))],
        out_specs=[col(0), col(0), pl.BlockSpec((HGRN_PAIR, 1, LANE, LANE), lambda h, j: (h, j, 0, 0))],
        out_shape=[SDS((S, HGRN_W), bf16), SDS((S, HGRN_W), f32), SDS((4, nsb, LANE, LANE), f32)],
        scratch_shapes=[pltpu.VMEM((HGRN_PAIR, LANE, LANE), f32)],
        compiler_params=_cparams(("parallel", "arbitrary")),
        name="hgrn_fwd",
    )(hg, hg, hg, hg, lb, normw, _chunk_masks())


def _hgrn_bwd(hg, o_raw, dy, ck, lb, normw, after=None):
    S = hg.shape[0]
    sb = HGRN_SB
    nsb = S // sb
    nch = sb // HGRN_CHUNK

    def body(q_ref, f_ref, v_ref, og_ref, o_ref, dy_ref, ck_ref, lb_ref, nw_ref, m_ref, *rest):
        dq_ref, df_ref, dv_ref, dog_ref, glb_ref, gnw_ref, dst, alb, anw = rest[-9:]
        j = pl.program_id(1)

        @pl.when(j == 0)
        def _():
            dst[...] = jnp.zeros_like(dst)
            alb[...] = jnp.zeros_like(alb)
            anw[...] = jnp.zeros_like(anw)

        tril_m = m_ref[0]
        tril = tril_m.astype(f32) > 0.5
        nw = nw_ref[...]

        def one_head(hh):
            cols = slice(hh * LANE, (hh + 1) * LANE)
            lbv = lb_ref[:, cols]
            q_raw = q_ref[:, cols]
            pr = _hgrn_prep(q_raw, f_ref[:, cols], lbv, tril_m, m_ref[1])
            qt, kt, kd, eGL = pr["qt"], pr["kt"], pr["kd"], pr["eGL"]
            qtb, ktb, kdb = qt.astype(bf16), kt.astype(bf16), kd.astype(bf16)
            vb = v_ref[:, cols].astype(bf16)

            o = o_ref[:, cols]
            ogv = og_ref[:, cols]
            sog = _sigmoid(ogv)
            rms = lax.rsqrt(jnp.mean(o * o, axis=-1, keepdims=True) + EPS)
            oh = o * rms
            dyv = dy_ref[:, cols]
            dog_ref[:, cols] = (dyv * (oh * nw) * (sog * (1.0 + ogv * (1.0 - sog)))).astype(bf16)
            dohw = dyv * (ogv * sog)
            anw[:, cols] += _colsum8(dohw * oh)
            doh = dohw * nw
            do = rms * (doh - oh * jnp.mean(doh * oh, axis=-1, keepdims=True))
            dob = do.astype(bf16)

            Ab = jnp.where(tril, lax.dot_general(qtb, ktb, NT, preferred_element_type=f32), 0.0).astype(bf16)
            dAb = jnp.where(tril, lax.dot_general(dob, vb, NT, preferred_element_type=f32), 0.0).astype(bf16)
            dv_acc = lax.dot_general(Ab, dob, TN, preferred_element_type=f32)
            dqt = jnp.dot(dAb, ktb, preferred_element_type=f32)
            dkt = lax.dot_general(dAb, qtb, TN, preferred_element_type=f32)

            ST = ck_ref[hh, 0]
            states = []
            for ci in range(nch):
                lo = ci * HGRN_CHUNK
                sl = slice(lo, lo + HGRN_CHUNK)
                states.append(ST)
                ST = ST * eGL[lo:lo + 1, :] + lax.dot_general(vb[sl], kdb[sl], TN, preferred_element_type=f32)

            dST = dst[hh]
            dqt_i, dkd_i, dv_i, deg_i = [None] * nch, [None] * nch, [None] * nch, [None] * nch
            for ci in reversed(range(nch)):
                lo = ci * HGRN_CHUNK
                sl = slice(lo, lo + HGRN_CHUNK)
                ST0 = states[ci]
                dSTb = dST.astype(bf16)
                dv_i[ci] = lax.dot_general(kdb[sl], dSTb, NT, preferred_element_type=f32)
                dqt_i[ci] = jnp.dot(dob[sl], ST0.astype(bf16), preferred_element_type=f32)
                dkd_i[ci] = jnp.dot(vb[sl], dSTb, preferred_element_type=f32)
                deg_i[ci] = jnp.broadcast_to(jnp.sum(dST * ST0, axis=0, keepdims=True), (HGRN_CHUNK, LANE))
                dST = dST * eGL[lo:lo + 1, :] + lax.dot_general(dob[sl], qtb[sl], TN, preferred_element_type=f32)
            dst[hh] = dST

            dqt = dqt + jnp.concatenate(dqt_i, axis=0)
            dkd = jnp.concatenate(dkd_i, axis=0)
            dv_ref[:, cols] = (dv_acc + jnp.concatenate(dv_i, axis=0)).astype(bf16)
            deg = jnp.concatenate(deg_i, axis=0)

            dqs = dqt * pr["eG"]
            dkdkd = dkd * kd
            dG = dqt * qt - dkt * kt - dkdkd
            dk = dkt * pr["einv"] + dkd * pr["edec"]
            dGL = _mask_dot(m_ref[1], dkdkd) + eGL * deg
            dg = _mask_dot(m_ref[2], dG) + dGL
            df = dg / pr["f"] - dk
            sig = pr["sig"]
            df_ref[:, cols] = (df * (1.0 - lbv) * (sig * (1.0 - sig))).astype(bf16)
            alb[:, cols] += _colsum8(df * (1.0 - sig))
            sq = pr["sq"]
            dq_ref[:, cols] = (dqs * (sq * (1.0 + q_raw * (1.0 - sq)))).astype(bf16)

        for hh in range(HGRN_PAIR):
            one_head(hh)

        @pl.when(j == nsb - 1)
        def _():
            glb_ref[...] = jnp.broadcast_to(jnp.sum(alb[...], axis=0, keepdims=True), (SUBLANE, wide))
            gnw_ref[...] = jnp.broadcast_to(jnp.sum(anw[...], axis=0, keepdims=True), (SUBLANE, wide))

    wide = HGRN_PAIR * LANE
    rev = lambda off: pl.BlockSpec((sb, wide), lambda h, j: (nsb - 1 - j, off // HGRN_PAIR + h))
    stat = pl.BlockSpec((SUBLANE, wide), lambda h, j: (0, h))
    return pl.pallas_call(
        body,
        grid=(4 // HGRN_PAIR, nsb),
        in_specs=[rev(0), rev(4), rev(8), rev(12), rev(0), rev(0),
                  pl.BlockSpec((HGRN_PAIR, 1, LANE, LANE), lambda h, j: (h, nsb - 1 - j, 0, 0)),
                  pl.BlockSpec((1, wide), lambda h, j: (0, h)), pl.BlockSpec((1, LANE), lambda h, j: (0, 0)),
                  pl.BlockSpec((3, sb, sb), lambda h, j: (0, 0, 0))]
        + ([] if after is None else [pl.BlockSpec(memory_space=pl.ANY)]),
        out_specs=[rev(0), rev(0), rev(0), rev(0), stat, stat],
        out_shape=[SDS((S, HGRN_W), bf16)] * 4 + [SDS((SUBLANE, HGRN_W), f32)] * 2,
        scratch_shapes=[pltpu.VMEM((HGRN_PAIR, LANE, LANE), f32), pltpu.VMEM((SUBLANE, wide), f32),
                        pltpu.VMEM((SUBLANE, wide), f32)],
        compiler_params=_cparams(("parallel", "arbitrary")),
        name="hgrn_bwd",
    )(hg, hg, hg, hg, o_raw, dy, ck, lb, normw, _chunk_masks(), *([] if after is None else [after]))


def _lb_fwd(raw):
    def body(r_ref, o_ref):
        r = r_ref[...]
        m = jnp.max(r, axis=0, keepdims=True)
        e = jnp.exp(r - m)
        o_ref[...] = (e / jnp.sum(e, axis=0, keepdims=True))[0:1]

    return pl.pallas_call(body, out_shape=SDS((1, raw.shape[1]), f32), name="lb_fwd")(raw)


def _lb_bwd(raw, dlb):
    def body(r_ref, d_ref, o_ref):
        r = r_ref[...]
        m = jnp.max(r, axis=0, keepdims=True)
        e = jnp.exp(r - m)
        s = e / jnp.sum(e, axis=0, keepdims=True)
        s0 = s[0:1]
        onehot0 = jnp.where(lax.broadcasted_iota(jnp.int32, r.shape, 0) == 0, 1.0, 0.0)
        o_ref[...] = d_ref[...] * s0 * (onehot0 - s)

    return pl.pallas_call(body, out_shape=SDS(raw.shape, f32), name="lb_bwd")(raw, dlb)


def _gate_fwd(ya, yh, w_ba, w_bh, gc):
    S = ya.shape[0]
    D = w_ba.shape[1]
    tm = _pick(S, MM_ROWS)

    def body(ya_ref, yh_ref, wa_ref, wh_ref, g0_ref, g1_ref, a_ref, b_ref, o_ref):
        a = jnp.dot(ya_ref[...], wa_ref[...], preferred_element_type=f32).astype(bf16)
        b = jnp.dot(yh_ref[...], wh_ref[...], preferred_element_type=f32).astype(bf16)
        a_ref[...] = a
        b_ref[...] = b
        s0, s1 = _sigmoid(g0_ref[...].astype(f32)), _sigmoid(g1_ref[...].astype(f32))
        o_ref[...] = (s0 * a.astype(f32) + s1 * b.astype(f32)).astype(bf16)

    row = pl.BlockSpec((tm, D), lambda i: (i, 0))
    act = pl.BlockSpec((tm, ya.shape[1]), lambda i: (i, 0))
    wspec = pl.BlockSpec(w_ba.shape, lambda i: (0, 0))
    return pl.pallas_call(
        body,
        grid=(S // tm,),
        in_specs=[act, act, wspec, wspec, row, pl.BlockSpec((tm, D), lambda i: (i, 1))],
        out_specs=[row, row, row],
        out_shape=[SDS((S, D), bf16)] * 3,
        compiler_params=_cparams(("parallel",), VMEM_BIG),
        name="branch_gate_fwd",
    )(ya, yh, w_ba, w_bh, gc, gc)


def _gate_bwd(dmo, w_out, a, b, gc, w_ba, w_bh):
    S, D = a.shape
    W = w_ba.shape[0]
    tm = _pick(S, MM_ROWS)

    def body(dmo_ref, wo_ref, a_ref, b_ref, g0_ref, g1_ref, wa_ref, wh_ref,
             da_ref, db_ref, dg_ref, dya_ref, dyh_ref):
        dm = lax.dot_general(dmo_ref[...], wo_ref[...], NT, preferred_element_type=f32)
        dmv = dm.astype(bf16).astype(f32)
        s0, s1 = _sigmoid(g0_ref[...].astype(f32)), _sigmoid(g1_ref[...].astype(f32))
        da = (dmv * s0).astype(bf16)
        db = (dmv * s1).astype(bf16)
        da_ref[...] = da
        db_ref[...] = db
        dg_ref[:, :D] = (dmv * a_ref[...].astype(f32) * (s0 * (1.0 - s0))).astype(bf16)
        dg_ref[:, D:] = (dmv * b_ref[...].astype(f32) * (s1 * (1.0 - s1))).astype(bf16)
        dya_ref[...] = lax.dot_general(da, wa_ref[...], NT, preferred_element_type=f32)
        dyh_ref[...] = lax.dot_general(db, wh_ref[...], NT, preferred_element_type=f32)

    row = pl.BlockSpec((tm, D), lambda i: (i, 0))
    wide = pl.BlockSpec((tm, 2 * D), lambda i: (i, 0))
    narrow = pl.BlockSpec((tm, W), lambda i: (i, 0))
    whole = lambda t: pl.BlockSpec(t.shape, lambda i: (0, 0))
    return pl.pallas_call(
        body,
        grid=(S // tm,),
        in_specs=[row, whole(w_out), row, row, row, pl.BlockSpec((tm, D), lambda i: (i, 1)), whole(w_ba), whole(w_bh)],
        out_specs=[row, row, wide, narrow, narrow],
        out_shape=[SDS((S, D), bf16), SDS((S, D), bf16), SDS((S, 2 * D), bf16), SDS((S, W), f32), SDS((S, W), f32)],
        compiler_params=_cparams(("parallel",), VMEM_BIG),
        name="gate_bwd_fused",
    )(dmo, w_out, a, b, gc, gc, w_ba, w_bh)


CONV_ROWS = 512
INV_SQRT2 = 0.7071067811865476
INV_SQRT_2PI = 0.3989422804014327


CONV_HALO = 16


def _shift_down(cur, prev, k):
    x = pltpu.roll(cur, k, 0)
    row = lax.broadcasted_iota(jnp.int32, (SUBLANE, LANE), 0)
    head = jnp.where(row < k, pltpu.roll(prev, k, 0)[:SUBLANE], x[:SUBLANE])
    return jnp.concatenate([head, x[SUBLANE:]], axis=0)


def _shift_up(cur, nxt, k):
    R = cur.shape[0]
    x = pltpu.roll(cur, R - k, 0)
    row = lax.broadcasted_iota(jnp.int32, (SUBLANE, LANE), 0)
    tail = jnp.where(row >= SUBLANE - k, pltpu.roll(nxt, SUBLANE - k, 0), x[R - SUBLANE:])
    return jnp.concatenate([x[:R - SUBLANE], tail], axis=0)


def _conv_rows(u_ref, w, b, r0, first):
    R = CONV_ROWS
    cur = u_ref[pl.ds(r0, R), :].astype(f32)
    prev = u_ref[pl.ds(pl.multiple_of(jnp.maximum(r0 - CONV_HALO, 0), CONV_HALO), CONV_HALO), :].astype(f32)
    prev = jnp.where(first, 0.0, prev)
    x1 = _shift_down(cur, prev, 1)
    x2 = _shift_down(cur, prev, 2)
    c = ((b + w[0:1] * x2) + w[1:2] * x1) + w[2:3] * cur
    return c, x2, x1, cur


def _conv_fwd(ug, uv, wg, wv, bg, bv):
    S, F = ug.shape
    nchunk = S // CONV_ROWS

    def body(ug_ref, uv_ref, wg_ref, wv_ref, bg_ref, bv_ref, o_ref):
        wgv, wvv, bgv, bvv = wg_ref[...], wv_ref[...], bg_ref[...], bv_ref[...]

        def step(ci, carry):
            r0 = pl.multiple_of(ci * CONV_ROWS, CONV_ROWS)
            cg = _conv_rows(ug_ref, wgv, bgv, r0, ci == 0)[0]
            cv = _conv_rows(uv_ref, wvv, bvv, r0, ci == 0)[0]
            gelu = 0.5 * cg * (1.0 + lax.erf(cg * INV_SQRT2))
            o_ref[pl.ds(r0, CONV_ROWS), :] = (gelu * cv).astype(bf16)
            return carry

        lax.fori_loop(0, nchunk, step, 0)

    col = pl.BlockSpec((S, LANE), lambda j: (0, j))
    w3 = pl.BlockSpec((3, LANE), lambda j: (0, j))
    b1 = pl.BlockSpec((1, LANE), lambda j: (0, j))
    return pl.pallas_call(
        body,
        grid=(F // LANE,),
        in_specs=[col, col, w3, w3, b1, b1],
        out_specs=col,
        out_shape=SDS((S, F), bf16),
        compiler_params=_cparams(("parallel",), VMEM_BIG),
        name="conv_fwd",
    )(ug, uv, wg, wv, bg, bv)


def _conv_bwd(ug, uv, dact, wg, wv, bg, bv):
    S, F = ug.shape
    R = CONV_ROWS
    nchunk = S // R

    def body(ug_ref, uv_ref, da_ref, wg_ref, wv_ref, bg_ref, bv_ref, dug_ref, duv_ref, sg_ref, sv_ref, dcg, dcv):
        wgv, wvv, bgv, bvv = wg_ref[...], wv_ref[...], bg_ref[...], bv_ref[...]
        zero = jnp.zeros((SUBLANE, LANE), f32)

        def fwd_step(ci, acc):
            r0 = pl.multiple_of(ci * R, R)
            cg, g2, g1, g0 = _conv_rows(ug_ref, wgv, bgv, r0, ci == 0)
            cv, v2, v1, v0 = _conv_rows(uv_ref, wvv, bvv, r0, ci == 0)
            da = da_ref[pl.ds(r0, R), :].astype(f32)
            cdf = 0.5 * (1.0 + lax.erf(cg * INV_SQRT2))
            pdf = INV_SQRT_2PI * jnp.exp(-0.5 * cg * cg)
            dg = da * cv * (cdf + cg * pdf)
            dv = da * (cg * cdf)
            dcg[pl.ds(r0, R), :] = dg
            dcv[pl.ds(r0, R), :] = dv
            new = (acc[0] + _colsum8(dg * g2), acc[1] + _colsum8(dg * g1), acc[2] + _colsum8(dg * g0),
                   acc[3] + _colsum8(dg),
                   acc[4] + _colsum8(dv * v2), acc[5] + _colsum8(dv * v1), acc[6] + _colsum8(dv * v0),
                   acc[7] + _colsum8(dv))
            return new

        acc = lax.fori_loop(0, nchunk, fwd_step, (zero,) * 8)
        rows = lax.broadcasted_iota(jnp.int32, (SUBLANE, LANE), 0)

        def stats(parts):
            out = jnp.zeros((SUBLANE, LANE), f32)
            for k, pt in enumerate(parts):
                out = jnp.where(rows == k, jnp.sum(pt, axis=0, keepdims=True), out)
            return out

        sg_ref[...] = stats(acc[0:4])
        sv_ref[...] = stats(acc[4:8])

        def du_rows(dc, w, r0, last):
            cur = dc[pl.ds(r0, R), :]
            nxt = dc[pl.ds(pl.multiple_of(jnp.minimum(r0 + R, S - SUBLANE), SUBLANE), SUBLANE), :]
            nxt = jnp.where(last, 0.0, nxt)
            return w[2:3] * cur + w[1:2] * _shift_up(cur, nxt, 1) + w[0:1] * _shift_up(cur, nxt, 2)

        def bwd_step(ci, carry):
            r0 = pl.multiple_of(ci * R, R)
            last = ci == nchunk - 1
            dug_ref[pl.ds(r0, R), :] = du_rows(dcg, wgv, r0, last).astype(bf16)
            duv_ref[pl.ds(r0, R), :] = du_rows(dcv, wvv, r0, last).astype(bf16)
            return carry

        lax.fori_loop(0, nchunk, bwd_step, 0)

    col = pl.BlockSpec((S, LANE), lambda j: (0, j))
    w3 = pl.BlockSpec((3, LANE), lambda j: (0, j))
    b1 = pl.BlockSpec((1, LANE), lambda j: (0, j))
    st = pl.BlockSpec((SUBLANE, LANE), lambda j: (0, j))
    return pl.pallas_call(
        body,
        grid=(F // LANE,),
        in_specs=[col, col, col, w3, w3, b1, b1],
        out_specs=[col, col, st, st],
        out_shape=[SDS((S, F), bf16), SDS((S, F), bf16), SDS((SUBLANE, F), f32), SDS((SUBLANE, F), f32)],
        scratch_shapes=[pltpu.VMEM((S, LANE), f32), pltpu.VMEM((S, LANE), f32)],
        compiler_params=_cparams(("parallel",), VMEM_BIG),
        name="conv_bwd",
    )(ug, uv, dact, wg, wv, bg, bv)


def _adam_math(w, g, m, v):
    m = ADAM_B1 * m + (1.0 - ADAM_B1) * g
    v = ADAM_B2 * v + (1.0 - ADAM_B2) * (g * g)
    m_hat = m / (1.0 - ADAM_B1 ** ADAM_STEP)
    v_hat = v / (1.0 - ADAM_B2 ** ADAM_STEP)
    delta = -ADAM_LR * (m_hat / (jnp.sqrt(v_hat) + ADAM_EPS) + ADAM_WD * w)
    return delta, m, v


def _adamw(w, m, v, g, name):
    R, C = w.shape
    parts = len(g.shape) == 3
    tr = R
    if R % 16 == 0:
        for t in range(R, 0, -16):
            if R % t == 0 and t * C * 4 <= ADAM_BLOCK_BYTES:
                tr = t
                break

    def body(w_ref, m_ref, v_ref, g_ref, go_ref, d_ref, mo_ref, vo_ref):
        if parts:
            gv = ((g_ref[0].astype(f32) + g_ref[1].astype(f32)) + g_ref[2].astype(f32)) + g_ref[3].astype(f32)
        else:
            gv = g_ref[...]
        go_ref[...] = gv
        d, mn, vn = _adam_math(w_ref[...], gv, m_ref[...], v_ref[...])
        d_ref[...] = d
        mo_ref[...] = mn
        vo_ref[...] = vn

    row = pl.BlockSpec((tr, C), lambda i: (i, 0))
    gspec = pl.BlockSpec((4, tr, C), lambda i: (0, i, 0)) if parts else row
    if isinstance(g, _Rows):
        lo, g = g.lo, g.full
        assert lo % (2 * SUBLANE) == 0 and tr % (2 * SUBLANE) == 0
        gspec = pl.BlockSpec((pl.Element(4), pl.Element(tr), pl.Element(C)),
                             lambda i: (0, pl.multiple_of(lo + i * tr, 2 * SUBLANE), 0))
    return pl.pallas_call(
        body,
        grid=(R // tr,),
        in_specs=[row, row, row, gspec],
        out_specs=[row] * 4,
        out_shape=[SDS((R, C), f32)] * 4,
        compiler_params=_cparams(("parallel",), VMEM_BIG),
        name=name,
    )(w, m, v, g)


def _sum8(parts, name):
    _, _, R, C = parts.shape

    def body(p_ref, o_ref):
        acc = p_ref[0, 0]
        for c in range(2):
            for k in range(4):
                if c or k:
                    acc = acc + p_ref[c, k]
        o_ref[...] = acc

    return pl.pallas_call(body, out_shape=SDS((R, C), f32), name=name)(parts)


def _pair_add(by_core, b, name):
    _, K, R, C = by_core.shape
    tr = R // 2 if R % 32 == 0 else R

    def body(c_ref, a_ref, b_ref, o_ref):
        o_ref[...] = (a_ref[0].astype(f32) + b_ref[...].astype(f32)).astype(bf16)

    blk = pl.BlockSpec((1, tr, C), lambda k, i, c: (k, i, 0))
    return pl.pallas_call(
        body,
        grid_spec=pltpu.PrefetchScalarGridSpec(
            num_scalar_prefetch=1,
            grid=(K, R // tr),
            in_specs=[pl.BlockSpec((1, 1, tr, C), lambda k, i, c: (c[0], k, i, 0)), blk],
            out_specs=blk,
        ),
        out_shape=SDS((K, R, C), bf16),
        compiler_params=_cparams(("parallel", "parallel")),
        name=name,
    )(lax.axis_index("c").astype(jnp.int32).reshape(1), by_core, b)


_ANY = pl.BlockSpec(memory_space=pl.ANY)


def _chip_out_shape(src, gather):
    return SDS((4,) + tuple(src.shape if gather else src.shape[1:]), src.dtype)


def _fill_own(out, src, gather):
    mine = 2 * lax.axis_index("x") + lax.axis_index("y")
    own = src if gather else lax.dynamic_index_in_dim(src, mine, axis=0, keepdims=False)
    return lax.dynamic_update_index_in_dim(out, own, mine, axis=0)


_HBM = pl.BlockSpec(memory_space=pltpu.HBM)
_SEM = pl.BlockSpec(memory_space=pltpu.SEMAPHORE)
_EFFECT = pltpu.SideEffectType.DATAFLOW_SIDE_EFFECTING
_SPLIT_PEERS = {"chip_gather": 3, "chip_gather_wide": 3, "chip_xchg": 3, "core_fill": 4, "core_swap": 1}


def _split_land(src, kind):
    if kind == "chip_gather_wide":
        return SDS((4, 2) + tuple(src.shape), src.dtype)
    if kind == "core_fill":
        return SDS((SUBLANE, LANE), src.dtype)
    if kind == "core_swap":
        return SDS(tuple(src.shape[1:]), src.dtype)
    return _chip_out_shape(src, kind == "chip_gather")


def _split_copies(src_ref, land_ref, sems, kind):
    x, y, c = lax.axis_index("x"), lax.axis_index("y"), lax.axis_index("c")
    n = _SPLIT_PEERS[kind]
    if kind == "core_fill":
        routes = [((x, y, 1 - c), src_ref.at[k, c], src_ref.at[k, c], src_ref.at[k, 1 - c]) for k in range(n)]
    elif kind == "core_swap":
        routes = [((x, y, 1 - c), src_ref.at[1 - c], land_ref, land_ref)]
    else:
        mine = 2 * x + y
        gather = kind != "chip_xchg"
        slot = (lambda k: land_ref.at[k, c]) if kind == "chip_gather_wide" else (lambda k: land_ref.at[k])
        routes = [((px, py, c), src_ref if gather else src_ref.at[2 * px + py], slot(mine), slot(2 * px + py))
                  for px, py in [(1 - x, y), (x, 1 - y), (1 - x, 1 - y)]]
    sends, recvs = [], []
    for j, (peer, piece, there, here) in enumerate(routes):
        sends.append(pltpu.make_async_remote_copy(src_ref=piece, dst_ref=there, send_sem=sems[j],
                                                  recv_sem=sems[n + j], device_id=peer, device_id_type=MESH))
        recvs.append(pltpu.make_async_remote_copy(src_ref=piece, dst_ref=here, send_sem=sems[j],
                                                  recv_sem=sems[n + j], device_id=peer, device_id_type=MESH))
    return sends, recvs


def _split_start(src, kind, name, after=None):
    land = _split_land(src, kind)
    ns = 2 * _SPLIT_PEERS[kind]
    n_in = 2 if after is None else 3

    def body(*refs):
        src_ref, land_ref = refs[:2]
        outs = refs[n_in:]
        for cp in _split_copies(src_ref, land_ref, outs[:ns], kind)[0]:
            cp.start()
        token = outs[ns + 2]
        token[...] = jnp.zeros_like(token)

    res = pl.pallas_call(
        body,
        name=name,
        out_shape=(pltpu.SemaphoreType.DMA(()),) * ns
        + (pltpu.HBM(src.shape, src.dtype), pltpu.HBM(land.shape, land.dtype), SDS((SUBLANE, LANE), f32)),
        in_specs=(_HBM, _HBM) + (() if after is None else (_ANY,)),
        out_specs=(_SEM,) * ns + (_HBM, _HBM, pl.BlockSpec(memory_space=pltpu.VMEM)),
        input_output_aliases={0: ns, 1: ns + 1},
        compiler_params=pltpu.CompilerParams(has_side_effects=_EFFECT),
    )(pltpu.with_memory_space_constraint(src, pltpu.HBM),
      pltpu.with_memory_space_constraint(lax.empty(land.shape, land.dtype), pltpu.HBM),
      *(() if after is None else (after,)))
    return (res[:ns], res[ns], res[ns + 1]), res[ns + 2]


def _split_wait(state, after, kind, name):
    sems, src_thru, land_thru = state
    ns = 2 * _SPLIT_PEERS[kind]

    def body(src_ref, land_ref, *rest):
        sends, recvs = _split_copies(src_ref, land_ref, rest[:ns], kind)
        for cp in recvs:
            cp.wait_recv()
        for cp in sends:
            cp.wait_send()

    src_out, got = pl.pallas_call(
        body,
        name=name,
        out_shape=(pltpu.HBM(src_thru.shape, src_thru.dtype), pltpu.HBM(land_thru.shape, land_thru.dtype)),
        in_specs=(_HBM, _HBM) + (_SEM,) * ns + (_ANY,),
        out_specs=(_HBM, _HBM),
        input_output_aliases={0: 0, 1: 1},
        compiler_params=pltpu.CompilerParams(has_side_effects=_EFFECT),
    )(src_thru, land_thru, *sems, after)
    if kind == "core_swap":
        return got, src_out
    if kind == "core_fill":
        return src_out
    if kind == "chip_gather_wide":
        mine = 2 * lax.axis_index("x") + lax.axis_index("y")
        zero = jnp.zeros((), mine.dtype)
        return lax.dynamic_update_slice(got, src_out[None, None], (mine, lax.axis_index("c").astype(mine.dtype))
                                        + (zero,) * src_out.ndim)
    return _fill_own(got, src_out, kind == "chip_gather")


def _core_fill(both, name):
    n = both.shape[0]

    def body(in_ref, out_ref, send_sems, recv_sems):
        x, y, c = lax.axis_index("x"), lax.axis_index("y"), lax.axis_index("c")
        sends = [pltpu.make_async_remote_copy(src_ref=out_ref.at[k, c], dst_ref=out_ref.at[k, c],
                                              send_sem=send_sems.at[k], recv_sem=recv_sems.at[k],
                                              device_id=(x, y, 1 - c), device_id_type=MESH) for k in range(n)]
        recvs = [pltpu.make_async_remote_copy(src_ref=out_ref.at[k, c], dst_ref=out_ref.at[k, 1 - c],
                                              send_sem=send_sems.at[k], recv_sem=recv_sems.at[k],
                                              device_id=(x, y, 1 - c), device_id_type=MESH) for k in range(n)]
        for cp in sends:
            cp.start()
        for cp in recvs:
            cp.wait_recv()
        for cp in sends:
            cp.wait_send()

    return pl.pallas_call(
        body,
        in_specs=[_ANY],
        out_specs=_ANY,
        out_shape=SDS(both.shape, both.dtype),
        scratch_shapes=[pltpu.SemaphoreType.DMA((n,)), pltpu.SemaphoreType.DMA((n,))],
        input_output_aliases={0: 0},
        name=name,
    )(both)


def _core_gather(src, name):
    def body(src_ref, out_ref, send_sem, recv_sem):
        x, y, c = lax.axis_index("x"), lax.axis_index("y"), lax.axis_index("c")
        cp = pltpu.make_async_remote_copy(src_ref=src_ref, dst_ref=out_ref.at[c], send_sem=send_sem,
                                          recv_sem=recv_sem, device_id=(x, y, 1 - c), device_id_type=MESH)
        cp.start()
        pltpu.make_async_remote_copy(src_ref=src_ref, dst_ref=out_ref.at[1 - c], send_sem=send_sem,
                                     recv_sem=recv_sem, device_id=(x, y, 1 - c), device_id_type=MESH).wait_recv()
        cp.wait_send()

    out = pl.pallas_call(
        body,
        in_specs=[_ANY],
        out_specs=_ANY,
        out_shape=SDS((2,) + tuple(src.shape), src.dtype),
        scratch_shapes=[pltpu.SemaphoreType.DMA, pltpu.SemaphoreType.DMA],
        name=name,
    )(src)
    return lax.dynamic_update_index_in_dim(out, src, lax.axis_index("c"), axis=0)


_PACK_A = (("w_in", (1088, 1024)),)
_PACK_B = (("w_ba", (512, 128)), ("w_bh", (512, 128)), ("w_out", (128, 1024)), ("w_up", (704, 1024)),
           ("w_down", (352, 1024)))
_PACK_SIZES = _PACK_A + _PACK_B
_TRANSPOSED = ("w_in", "w_up")


def _slab_rows(sizes):
    return sum(r * c for _, (r, c) in sizes) // D_MODEL


def _pack_lo(key):
    keys = [k for k, _ in _PACK_B]
    return _slab_rows(_PACK_B[:keys.index(key)])


def _pack_rows(d, sizes):
    n = d[sizes[0][0]].shape[0]
    return jnp.concatenate([d[k].reshape(n, -1, D_MODEL) for k, _ in sizes], axis=1)


def _unpack_rows(slab, sizes):
    n = slab.shape[0]
    out, lo = {}, 0
    for key, (r, c) in sizes:
        rows = r * c // D_MODEL
        out[key] = slab[:, lo:lo + rows].reshape(n, r, c)
        lo += rows
    return out


def _by_core(gslab):
    return jnp.swapaxes(gslab.reshape((4, 2) + gslab.shape[1:]), 0, 1)


def _cols_to_full(t):
    return jnp.swapaxes(t, 0, 1).reshape(t.shape[1], -1)


def _full_to_cols(t):
    K = t.shape[0]
    return jnp.swapaxes(t.reshape(K, 8, -1), 0, 1)


_SMALL = (("pre_mix_norm", (1, 1024)), ("rel_bias", (32, 24)), ("hgrn_lb_raw", (2, 512)), ("hgrn_norm", (1, 128)),
          ("post_mix_norm", (1, 1024)), ("pre_ffn_norm", (1, 1024)), ("conv_b", (1, 5632)),
          ("post_ffn_norm", (1, 1024)))
_SMALL_ROWS = 96
_CONVW_ROWS = 136


_SMALL_USED = sum(r * c for _, (r, c) in _SMALL)


def _pack_small(d, extra=None):
    flat = jnp.concatenate([d[k].reshape(-1) for k, _ in _SMALL] + ([] if extra is None else [extra.reshape(-1)]))
    flat = jnp.pad(flat, (0, _SMALL_ROWS * LANE - flat.shape[0]))
    return flat.reshape(_SMALL_ROWS, LANE)


def _unpack_small(p):
    flat = p.reshape(-1)
    out, lo = {}, 0
    for k, shp in _SMALL:
        n = shp[0] * shp[1]
        out[k] = flat[lo:lo + n].reshape(shp)
        lo += n
    return out


def _local_step(x, tgt, P, plan):
    S = x.shape[0]
    P = dict(P)
    lb = _lb_fwd(P["hgrn_lb_raw"])
    hs = _prep(x, P["pre_mix_norm"], plan.start_token())
    h1 = hs[0]
    consts = [_bias_consts(d, plan.start_token()) for d in DILATIONS]
    biases, dep = [], h1
    for g in range(N_GROUPS):
        tab_t = P["rel_bias"][:, 8 * g:8 * g + 8].T
        dep = _bias_build(tab_t, consts[g][0], consts[g][1], f"bias_build{g}", dep)
        biases.append(dep.reshape(8, ATTN_BLOCK, 2 * ATTN_BLOCK))
    W = dict(plan.weights_a(dep))
    qkv0, hg, gc = _mm_fanout(h1, [W["wt_qkv"][0], W["wt_hg"], W["wt_gate"]], "nt", [bf16, f32, bf16], "proj_natural")
    qkv = [qkv0] + [_mm(hs[g], W["wt_qkv"][g], "nt", bf16, f"proj_qkv{g}") for g in (1, 2)]
    obuf, lbuf, token = [], [], None
    for g, d in enumerate(DILATIONS):
        o_g, l_g = _attn_fwd(qkv[g], biases[g], (S // d) // ATTN_BLOCK, f"attn_fwd{g}", after=token)
        lbuf.append(l_g)
        obuf.append(o_g)
        if g == 0:
            token = plan.forward_b(o_g)
    y_attn, y_attn_b, w0, w1, w2 = _attn_merge(obuf[0], obuf[1], obuf[2], lbuf[0], lbuf[1], lbuf[2])
    y_hgrn, o_raw, ck = _hgrn_fwd(hg, lb, P["hgrn_norm"])
    wb = plan.weights_b(y_hgrn)
    P["conv_w"] = wb.pop("conv_w")
    W.update(wb)
    a, b, merged = _gate_fwd(y_attn_b, y_hgrn, W["w_ba"], W["w_bh"], gc)
    mo, x1, h2 = _mid_fwd(x, merged, W["w_out"], P["post_mix_norm"], P["pre_ffn_norm"])
    ug, uv = _mm_fanout(h2, [W["wt_up_g"], W["wt_up_v"]], "nt", [bf16, bf16], "up_proj")
    cw_g, cw_v = P["conv_w"][:, :D_FF], P["conv_w"][:, D_FF:]
    cb_g, cb_v = P["conv_b"][:, :D_FF], P["conv_b"][:, D_FF:]
    act = _conv_fwd(ug, uv, cw_g, cw_v, cb_g, cb_v)
    loss, dy, dfo, g_post_ffn = _final(x1, act, W["w_down"], tgt, P["post_ffn_norm"])
    gslab = lax.empty((8, _slab_rows(_PACK_B), D_MODEL), bf16)
    rows_b = {k: r for k, (r, _) in _PACK_B}
    gslab = _mm(act, dfo, "tn", bf16, "gw_down", into=(gslab, 0, _pack_lo("w_down"), rows_b["w_down"]))
    dact = _mm(dfo, W["w_down"], "nt", bf16, "d_act")
    dug, duv, st_g, st_v = _conv_bwd(ug, uv, dact, cw_g, cw_v, cb_g, cb_v)
    gslab = _mm(dug, h2, "tn", bf16, "gw_up_gate", into=(gslab, 0, _pack_lo("w_up"), rows_b["w_up"]))
    gslab = _mm(duv, h2, "tn", bf16, "gw_up_val", into=(gslab, 4, _pack_lo("w_up"), rows_b["w_up"]))
    dx1, dmo, g_pre_ffn, g_post_mix = _mid_bwd(dy, dug, duv, W["wt_up_g"], W["wt_up_v"], x1, mo, P["pre_ffn_norm"],
                                               P["post_mix_norm"])
    gW_out = _mm(merged, dmo, "tn", bf16, "gw_out")
    da, db, dgc, dyattn, dyhgrn = _gate_bwd(dmo, W["w_out"], a, b, gc, W["w_ba"], W["w_bh"])
    gW_ba = _mm(y_attn_b, da, "tn", bf16, "gw_ba")
    gW_bh = _mm(y_hgrn, db, "tn", bf16, "gw_bh")
    big_b = dict(w_ba=gW_ba, w_bh=gW_bh, w_out=gW_out, slab=gslab)
    dos = _attn_merge_bwd(dyattn, y_attn, w0, w1, w2, after=plan.grads_b_start(big_b))
    dq_h, df_h, dv_h, dog_h, glb8, gnw8 = _hgrn_bwd(hg, o_raw, dyhgrn, ck, lb, P["hgrn_norm"],
                                                   after=plan.grads_b_exchange(dos[5]))
    dhg = [dq_h, df_h, dv_h, dog_h]
    g_lb_raw = _lb_bwd(P["hgrn_lb_raw"], glb8[0:1])
    gn = gnw8[0:1]
    g_hgrn_norm = (gn[:, 0:128] + gn[:, 128:256]) + (gn[:, 256:384] + gn[:, 384:512])
    dqkvs, gW_qkv, g_rel = [], [], []
    for g, d in enumerate(DILATIONS):
        dq, dk, dv, dbias = _attn_bwd(qkv[g], biases[g], dos[g], dos[3 + g], lbuf[g], (S // d) // ATTN_BLOCK,
                                      f"attn_bwd{g}")
        dqkvs.append([dq, dk, dv])
        gW_qkv.append(_mm(dqkvs[g], hs[g], "tn", bf16, f"gw_qkv{g}"))
        g_rel.append(_bias_grad(dbias.reshape(8, -1), consts[g][0], f"bias_grad{g}"))
    gW_hg = _mm(dhg, h1, "tn", bf16, "gw_hg")
    gW_gate = _mm(dgc, h1, "tn", bf16, "gw_gate")
    gW_in = gW_qkv + [gW_hg, gW_gate]
    token = plan.grads_a_start(gW_in)
    dh_perm = [_mm(dqkvs[g], W["wt_qkv"][g], "nn", f32, f"dh1_qkv{g}", after=token) for g in (1, 2)]
    token = plan.grads_a_exchange(dh_perm[1])
    dh_main = _mm(dqkvs[0] + dhg + [dgc], [W["wt_qkv"][0], W["wt_hg"], W["wt_gate"]], "nn", f32, "dh1_main",
                  after=token)
    grad_x, g_pre_mix = _first_bwd(x, dx1, dh_main, dh_perm[0], dh_perm[1], P["pre_mix_norm"])

    g_conv_w = jnp.concatenate([st_g[0:3], st_v[0:3]], axis=1)
    g_conv_b = jnp.concatenate([st_g[3:4], st_v[3:4]], axis=1)
    small = dict(pre_mix_norm=g_pre_mix, rel_bias=jnp.concatenate(g_rel, axis=1), hgrn_lb_raw=g_lb_raw,
                 hgrn_norm=g_hgrn_norm, post_mix_norm=g_post_mix, pre_ffn_norm=g_pre_ffn, conv_b=g_conv_b,
                 post_ffn_norm=g_post_ffn, conv_w=g_conv_w)
    return loss, grad_x, gW_in, big_b, small


def _weights_a(both):
    wt = both.reshape(-1, D_MODEL)
    return dict(
        wt_qkv=[_Rows(wt, g * QKV_G, QKV_G) for g in range(N_GROUPS)],
        wt_hg=_Rows(wt, 3 * QKV_G, 4 * HGRN_W),
        wt_gate=_Rows(wt, 3 * QKV_G + 4 * HGRN_W, wt.shape[0] - 3 * QKV_G - 4 * HGRN_W),
    )


def _weights_b(slabs):
    sh = _unpack_rows(slabs, _PACK_B)
    wt_up = sh["w_up"].reshape(-1, D_MODEL)
    return dict(
        w_ba=_cols_to_full(sh["w_ba"]),
        w_bh=_cols_to_full(sh["w_bh"]),
        w_out=sh["w_out"].reshape(D_MODEL, D_MODEL),
        wt_up_g=wt_up[:D_FF],
        wt_up_v=wt_up[D_FF:],
        w_down=sh["w_down"].reshape(D_FF, D_MODEL),
    )


def _dest_rows(sections, height):
    out = []
    for j in range(8):
        lo, hi, off, pieces = j * height, (j + 1) * height, 0, []
        for s in sections:
            a, b = max(lo, off), min(hi, off + s.shape[0])
            if a < b:
                pieces.append(s[a - off:b - off])
            off += s.shape[0]
        out.append(pieces[0] if len(pieces) == 1 else jnp.concatenate(pieces, axis=0))
    return out


def _grad_blocks_a(sections):
    rows = _dest_rows(sections, 1088)
    return jnp.stack([jnp.stack([rows[2 * k + c].astype(bf16) for k in range(4)]) for c in range(2)])


def _grad_slab_b(g):
    shards = dict(w_ba=_full_to_cols(g["w_ba"]), w_bh=_full_to_cols(g["w_bh"]), w_out=g["w_out"].reshape(8, 128, D_MODEL))
    head = _pack_rows({k: v.astype(bf16) for k, v in shards.items()}, _PACK_B[:3])
    assert head.shape[1] == _pack_lo("w_up")
    return lax.dynamic_update_slice(g["slab"], head, (0, 0, 0))


_CONVW_SLAB_ROWS = 16


class _Traffic:
    def __init__(self, slab_a, slab_b, conv_w):
        hi = conv_w.astype(bf16)
        r1 = conv_w - hi.astype(f32)
        mid = r1.astype(bf16)
        lo = (r1 - mid.astype(f32)).astype(bf16)
        bits = jnp.stack([hi, mid, lo]).reshape(-1)
        tail = jnp.pad(bits, (0, _CONVW_SLAB_ROWS * D_MODEL - bits.shape[0])).reshape(_CONVW_SLAB_ROWS, D_MODEL)
        self.slab_b = jnp.concatenate([slab_b, tail], axis=0)
        self.state_a, tok = _split_start(slab_a, "chip_gather_wide", "ag_a_start")
        self.state_b, self.token = _split_start(self.slab_b, "chip_gather_wide", "ag_b_start", after=tok)
        self.state = None
        self.state_gb = None

    def start_token(self):
        return self.token

    def weights_a(self, after):
        half = _split_wait(self.state_a, after, "chip_gather_wide", "ag_a_wait")
        return _weights_a(_core_fill(half, "ag_a_cores"))

    def forward_b(self, after):
        half = _split_wait(self.state_b, after, "chip_gather_wide", "ag_b_wait")
        self.state, token = _split_start(half, "core_fill", "ag_b_cores_start")
        return token

    def weights_b(self, after):
        both = _split_wait(self.state, after, "core_fill", "ag_b_cores_wait")
        slabs = both.reshape((8,) + tuple(self.slab_b.shape))
        rows = _slab_rows(_PACK_B)
        out = _weights_b(slabs[:, :rows])
        pieces = slabs[:, rows:].reshape(8, -1)[:, :3 * 3 * 704].reshape(8, 3, 3, 704).astype(f32)
        out["conv_w"] = _cols_to_full((pieces[:, 0] + pieces[:, 1]) + pieces[:, 2])
        return out

    def grads_b_start(self, grads):
        self.state, token = _split_start(_by_core(_grad_slab_b(grads)), "core_swap", "rs_b_cores_start")
        return token

    def grads_b_exchange(self, after):
        from_sib, by_core = _split_wait(self.state, after, "core_swap", "rs_b_cores_wait")
        self.state_gb, token = _split_start(_pair_add(by_core, from_sib, "rs_b_pair_add"), "chip_xchg", "rs_b_start")
        return token

    def grads_a_start(self, sections):
        self.state, token = _split_start(_grad_blocks_a(sections), "core_swap", "rs_a_cores_start")
        return token

    def grads_a_exchange(self, after):
        from_sib, by_core = _split_wait(self.state, after, "core_swap", "rs_a_cores_wait")
        self.state, token = _split_start(_pair_add(by_core, from_sib, "rs_a_pair_add"), "chip_xchg", "rs_a_start")
        return token

    def parts(self, after):
        slab = _split_wait(self.state_gb, after, "chip_xchg", "rs_b_wait")
        parts, lo = _unpack_rows(slab, _PACK_B), 0
        for key, (r, c) in _PACK_B:
            if c == D_MODEL:
                parts[key] = _Rows(slab, lo, r)
            lo += r * c // D_MODEL
        parts["w_in"] =_split_wait(self.state, after, "chip_xchg", "rs_a_wait")
        return parts


def kernel(x, pre_mix_norm, w_in, rel_bias, hgrn_lb_raw, hgrn_norm, w_branch_attn, w_branch_hgrn, w_out, post_mix_norm, pre_ffn_norm, w_up, conv_w, conv_b, w_down, post_ffn_norm, loss_target, m_pre_mix_norm, m_w_in, m_rel_bias, m_hgrn_lb_raw, m_hgrn_norm, m_w_branch_attn, m_w_branch_hgrn, m_w_out, m_post_mix_norm, m_pre_ffn_norm, m_w_up, m_conv_w, m_conv_b, m_w_down, m_post_ffn_norm, v_pre_mix_norm, v_w_in, v_rel_bias, v_hgrn_lb_raw, v_hgrn_norm, v_w_branch_attn, v_w_branch_hgrn, v_w_out, v_post_mix_norm, v_pre_ffn_norm, v_w_up, v_conv_w, v_conv_b, v_w_down, v_post_ffn_norm):
    ci = lax.axis_index("c")
    dev = 4 * lax.axis_index("x") + 2 * lax.axis_index("y") + ci
    tr = lambda t: jnp.swapaxes(t[0], 0, 1)
    wts = dict(w_in=tr(w_in), w_ba=w_branch_attn[0], w_bh=w_branch_hgrn[0], w_out=w_out[0], w_up=tr(w_up),
               w_down=w_down[0])
    mom = dict(w_in=tr(m_w_in), w_ba=m_w_branch_attn[0], w_bh=m_w_branch_hgrn[0], w_out=m_w_out[0], w_up=tr(m_w_up),
               w_down=m_w_down[0])
    var = dict(w_in=tr(v_w_in), w_ba=v_w_branch_attn[0], w_bh=v_w_branch_hgrn[0], w_out=v_w_out[0], w_up=tr(v_w_up),
               w_down=v_w_down[0])
    small_w = dict(pre_mix_norm=pre_mix_norm, rel_bias=rel_bias, hgrn_lb_raw=hgrn_lb_raw, hgrn_norm=hgrn_norm,
                   post_mix_norm=post_mix_norm, pre_ffn_norm=pre_ffn_norm, conv_b=conv_b, post_ffn_norm=post_ffn_norm)
    small_m = dict(pre_mix_norm=m_pre_mix_norm, rel_bias=m_rel_bias, hgrn_lb_raw=m_hgrn_lb_raw, hgrn_norm=m_hgrn_norm,
                   post_mix_norm=m_post_mix_norm, pre_ffn_norm=m_pre_ffn_norm, conv_b=m_conv_b,
                   post_ffn_norm=m_post_ffn_norm)
    small_v = dict(pre_mix_norm=v_pre_mix_norm, rel_bias=v_rel_bias, hgrn_lb_raw=v_hgrn_lb_raw, hgrn_norm=v_hgrn_norm,
                   post_mix_norm=v_post_mix_norm, pre_ffn_norm=v_pre_ffn_norm, conv_b=v_conv_b,
                   post_ffn_norm=v_post_ffn_norm)

    plan = _Traffic(wts["w_in"].astype(bf16),
                    _pack_rows({k: wts[k].astype(bf16)[None] for k, _ in _PACK_B}, _PACK_B)[0], conv_w[0])

    loss8, grad_x, _, _, small = _local_step(x[0], loss_target[0], small_w, plan)
    spack = jnp.concatenate([_pack_small(small, loss8[0, 0:1]),
                             jnp.pad(small["conv_w"].reshape(-1, LANE), ((0, _CONVW_ROWS - 132), (0, 0)))], axis=0)
    small_state, token = _split_start(spack, "chip_gather", "ag_small_start")

    parts = plan.parts(token)
    outs_big = {}
    for k, _ in _PACK_SIZES:
        outs_big[k] = _adamw(wts[k], mom[k], var[k], parts[k], "adamw_" + k)

    by_chip = _split_wait(small_state, outs_big["w_in"][1], "chip_gather", "ag_small_wait")
    allp = _core_gather(by_chip, "ag_small_cores")
    ssum = _sum8(allp, "small_sum")
    gs = ssum[:_SMALL_ROWS]
    loss = ssum[_SMALL_USED // LANE, _SMALL_USED % LANE]
    res_small = _adamw(_pack_small(small_w), _pack_small(small_m), _pack_small(small_v), gs, "adamw_small")
    sm = [_unpack_small(t) for t in res_small]
    g_cw_full = ssum[_SMALL_ROWS:_SMALL_ROWS + 132].reshape(3, 2 * D_FF)
    g_cw = lax.dynamic_slice_in_dim(g_cw_full, dev * 704, 704, axis=1)
    res_cw = _adamw(conv_w[0], m_conv_w[0], v_conv_w[0], g_cw, "adamw_conv_w")

    def pick(i):
        def big_(k):
            t = outs_big[k][i]
            return (jnp.swapaxes(t, 0, 1) if k in _TRANSPOSED else t)[None]
        return [sm[i]["pre_mix_norm"], big_("w_in"), sm[i]["rel_bias"], sm[i]["hgrn_lb_raw"], sm[i]["hgrn_norm"],
                big_("w_ba"), big_("w_bh"), big_("w_out"), sm[i]["post_mix_norm"], sm[i]["pre_ffn_norm"],
                big_("w_up"), res_cw[i][None], sm[i]["conv_b"], big_("w_down"), sm[i]["post_ffn_norm"]]

    return (loss, grad_x[None], *pick(0), *pick(1), *pick(2), *pick(3))
```

```python
import functools
import math

import jax
import jax.numpy as jnp
from jax import lax
from jax.experimental import pallas as pl
from jax.experimental.pallas import tpu as pltpu

f32 = jnp.float32
bf16 = jnp.bfloat16
SDS = jax.ShapeDtypeStruct
HIGHEST = lax.Precision.HIGHEST
MESH = pl.DeviceIdType.MESH

NN = (((1,), (0,)), ((), ()))
NT = (((1,), (1,)), ((), ()))
TN = (((0,), (0,)), ((), ()))

D_MODEL = 1024
N_GROUPS = 3
DILATIONS = (1, 4, 16)
HEAD_DIM = 64
ATTN_BLOCK = 128
QKV_G = 1536
ATTN_OUT = 512
HGRN_W = 512
HGRN_CHUNK = 32
D_FF = 2816
NUM_BUCKETS = 32
MAX_EXACT = 16
MAX_DISTANCE = 2048
NEG_INF = -1e30
EPS = 1e-6
LANE = 128
SUBLANE = 8
VMEM_BIG = 48 * 1024 * 1024
MM_ROWS = 512
MM_OUT_BYTES = 8 * 1024 * 1024
ADAM_BLOCK_BYTES = 2304 * 1024

ADAM_LR, ADAM_B1, ADAM_B2, ADAM_EPS, ADAM_WD, ADAM_STEP = 0.001, 0.9, 0.999, 1e-08, 0.01, 10


def _pick(n, pref):
    t = pref
    while t >= LANE:
        if n % t == 0:
            return t
        t //= 2
    return n


def _cparams(sem=None, vmem=None):
    kw = {}
    if sem is not None:
        kw["dimension_semantics"] = sem
    if vmem is not None:
        kw["vmem_limit_bytes"] = vmem
    return pltpu.CompilerParams(**kw)


def _sigmoid(x):
    return jax.nn.sigmoid(x)


def _colsum8(x):
    return x.reshape(x.shape[0] // SUBLANE, SUBLANE, x.shape[1]).sum(axis=0)


class _Rows:
    def __init__(self, full, lo, rows):
        self.full, self.lo, self.shape = full, lo, tuple(full.shape[:-2]) + (rows, full.shape[-1])


def _resident(t):
    if isinstance(t, _Rows):
        return pl.BlockSpec((pl.Element(t.shape[0]), pl.Element(t.shape[1])), lambda i: (t.lo, 0)), t.full
    return pl.BlockSpec(t.shape, lambda i: (0, 0)), t


def _mm(a, b, mode, out_dtype, name, acc=None, after=None, into=None):
    dims = {"nn": NN, "nt": NT, "tn": TN}[mode]
    has_acc = acc is not None
    parts = list(a) if isinstance(a, (list, tuple)) else [a]
    if mode == "tn":
        assert not has_acc
        K, N = b.shape
        widths = [t.shape[1] for t in parts]
        M = sum(widths)
        whole = M * N * 4 <= MM_OUT_BYTES
        assert whole or len(parts) == 1
        tmm = M if whole else M // 2
        ts = _pick(K, 4 * MM_ROWS)
        nk = K // ts

        npart = len(parts)
        narrow = out_dtype != f32
        n_in = npart + (1 if into is None else 2)
        out_spec, out_shape, aliases, extra = pl.BlockSpec((tmm, N), lambda i, k: (i, 0)), SDS((M, N), out_dtype), {}, []
        if into is not None:
            slab, first, lo, shard_rows = into
            assert narrow and slab.dtype == out_dtype and tmm % shard_rows == 0 and slab.shape[2] == N
            per = tmm // shard_rows
            out_spec = pl.BlockSpec((pl.Element(per), pl.Element(shard_rows), pl.Element(N)),
                                    lambda i, k: (first + i * per, lo, 0))
            out_shape, aliases, extra = SDS(slab.shape, out_dtype), {npart + 1: 0}, [slab]

        def body_tn(*refs):
            b_ref, o_ref = refs[npart], refs[n_in]
            acc_ref = refs[n_in + 1] if narrow else o_ref
            k = pl.program_id(1)
            bv = b_ref[...]
            lo = 0
            for a_ref, w in zip(refs[:npart], widths if whole else [tmm]):
                part = lax.dot_general(a_ref[...], bv, dims, preferred_element_type=f32)
                rows = slice(lo, lo + w)
                lo += w

                @pl.when(k == 0)
                def _(part=part, rows=rows):
                    acc_ref[rows, :] = part

                @pl.when(k > 0)
                def _(part=part, rows=rows):
                    acc_ref[rows, :] += part

            if narrow:
                @pl.when(k == nk - 1)
                def _():
                    o_ref[...] = acc_ref[...].astype(out_dtype).reshape(o_ref.shape)

        return pl.pallas_call(
            body_tn,
            grid=(M // tmm, nk),
            in_specs=[pl.BlockSpec((ts, w if whole else tmm), lambda i, k: (k, i)) for w in widths]
            + [pl.BlockSpec((ts, N), lambda i, k: (k, 0))] + [pl.BlockSpec(memory_space=pl.ANY)] * len(extra),
            out_specs=out_spec,
            out_shape=out_shape,
            input_output_aliases=aliases,
            scratch_shapes=[pltpu.VMEM((tmm, N), f32)] if narrow else [],
            compiler_params=_cparams(("parallel", "arbitrary"), VMEM_BIG),
            name=name,
        )(*parts, b, *extra)

    bs = list(b) if isinstance(b, (list, tuple)) else [b]
    widths = [t.shape[1] for t in parts]
    M = parts[0].shape[0]
    kdim = 0 if mode == "nn" else 1
    N = bs[0].shape[1 - kdim]
    tm = _pick(M, MM_ROWS)
    npart, nb = len(parts), len(bs)
    place, bi, lo = [], 0, 0
    for w in widths:
        place.append((bi, lo))
        lo += w
        if lo == bs[bi].shape[kdim]:
            bi, lo = bi + 1, 0
    assert bi == nb and lo == 0

    def body(*refs):
        a_refs, b_refs = refs[:npart], refs[npart:npart + nb]
        c_ref = refs[npart + nb] if has_acc else None
        o_ref = refs[-1]
        part = None
        for a_ref, w, (bi, lo) in zip(a_refs, widths, place):
            b_ref = b_refs[bi]
            if w == bs[bi].shape[kdim]:
                bk = b_ref[...]
            else:
                bk = b_ref[:, lo:lo + w] if mode == "nt" else b_ref[lo:lo + w, :]
            t = lax.dot_general(a_ref[...], bk, dims, preferred_element_type=f32)
            part = t if part is None else part + t
        if has_acc:
            part = part + c_ref[...]
        o_ref[...] = part.astype(out_dtype)

    specs = [pl.BlockSpec((tm, w), lambda i: (i, 0)) for w in widths] + [_resident(t)[0] for t in bs]
    args = parts + [_resident(t)[1] for t in bs]
    aliases = {}
    if has_acc:
        specs.append(pl.BlockSpec((tm, N), lambda i: (i, 0)))
        args.append(acc)
        aliases = {npart + nb: 0}
    if after is not None:
        specs.append(pl.BlockSpec(memory_space=pl.ANY))
        args.append(after)
    return pl.pallas_call(
        body,
        grid=(M // tm,),
        in_specs=specs,
        out_specs=pl.BlockSpec((tm, N), lambda i: (i, 0)),
        out_shape=SDS((M, N), out_dtype),
        input_output_aliases=aliases,
        compiler_params=_cparams(("parallel",), VMEM_BIG),
        name=name,
    )(*args)


def _mm_fanout(a, bs, mode, out_dtypes, name):
    dims = {"nn": NN, "nt": NT}[mode]
    M, K = a.shape
    ns = [b.shape[1] if mode == "nn" else b.shape[0] for b in bs]
    tm = _pick(M, MM_ROWS)
    nb = len(bs)

    def body(a_ref, *refs):
        av = a_ref[...]
        for b_ref, o_ref, dt in zip(refs[:nb], refs[nb:], out_dtypes):
            o_ref[...] = lax.dot_general(av, b_ref[...], dims, preferred_element_type=f32).astype(dt)

    return pl.pallas_call(
        body,
        grid=(M // tm,),
        in_specs=[pl.BlockSpec((tm, K), lambda i: (i, 0))] + [_resident(b)[0] for b in bs],
        out_specs=[pl.BlockSpec((tm, n), lambda i: (i, 0)) for n in ns],
        out_shape=[SDS((M, n), dt) for n, dt in zip(ns, out_dtypes)],
        compiler_params=_cparams(("parallel",), VMEM_BIG),
        name=name,
    )(a, *[_resident(b)[1] for b in bs])


PERM_ROWS = 2048


def _perm_spec(d, cols=LANE):
    return pl.BlockSpec((d, PERM_ROWS // d, cols), lambda i, j: (0, i, j))


def _to_natural(src_ref, dst_ref, d):
    n = src_ref.shape[1]
    for r in range(d):
        dst_ref[pl.ds(r, n, stride=d), :] = src_ref[r]


def _prep(x, w, after=None):
    S, D = x.shape
    R = PERM_ROWS
    nc = D // LANE
    n_in = nc + 1 + (after is not None)

    def body(*refs):
        x_refs, w_ref = refs[:nc], refs[nc]
        h_ref, h4_ref, h16_ref, rs = refs[n_in:]
        ssq = None
        for xr in x_refs:
            v = xr[...]
            t = jnp.sum(v * v, axis=-1, keepdims=True)
            ssq = t if ssq is None else ssq + t
        rinv = lax.rsqrt(ssq * (1.0 / D) + EPS)
        rs[...] = jnp.broadcast_to(rinv, (R, LANE))
        for j, xr in enumerate(x_refs):
            cols = slice(j * LANE, (j + 1) * LANE)
            wj = w_ref[:, cols]
            h_ref[:, cols] = ((xr[...] * rinv) * wj).astype(bf16)
            for d, o_ref in ((4, h4_ref), (16, h16_ref)):
                n = R // d
                for r in range(d):
                    rows = pl.ds(r, n, stride=d)
                    o_ref[r, :, cols] = ((xr[rows, :] * rs[rows, :]) * wj).astype(bf16)

    col = lambda j: pl.BlockSpec((R, LANE), lambda i, j=j: (i, j))
    h, h4, h16 = pl.pallas_call(
        body,
        grid=(S // R,),
        in_specs=[col(j) for j in range(nc)] + [pl.BlockSpec((1, D), lambda i: (0, 0))]
        + ([] if after is None else [pl.BlockSpec(memory_space=pl.ANY)]),
        out_specs=[pl.BlockSpec((R, D), lambda i: (i, 0)), pl.BlockSpec((4, R // 4, D), lambda i: (0, i, 0)),
                   pl.BlockSpec((16, R // 16, D), lambda i: (0, i, 0))],
        out_shape=[SDS((S, D), bf16), SDS((4, S // 4, D), bf16), SDS((16, S // 16, D), bf16)],
        scratch_shapes=[pltpu.VMEM((R, LANE), f32)],
        compiler_params=_cparams(("parallel",), VMEM_BIG),
        name="prep_norm_perm",
    )(*([x] * nc), w, *([] if after is None else [after]))
    return [h, h4.reshape(S, D), h16.reshape(S, D)]


def _rms_parts(xv):
    r = lax.rsqrt(jnp.mean(xv * xv, axis=-1, keepdims=True) + EPS)
    return r, xv * r


def _rms_bwd(xhat, r, w, dy):
    dyw = dy * w
    return r * (dyw - xhat * jnp.mean(dyw * xhat, axis=-1, keepdims=True))


def _mid_fwd(x, merged, w_out, w_pm, w_pf):
    S, D = x.shape
    tm = _pick(S, MM_ROWS)

    def body(x_ref, m_ref, wo_ref, wpm_ref, wpf_ref, mo_ref, x1_ref, h2_ref):
        mo = jnp.dot(m_ref[...], wo_ref[...], preferred_element_type=f32)
        mo_ref[...] = mo
        _, moh = _rms_parts(mo)
        x1 = x_ref[...] + moh * wpm_ref[...]
        x1_ref[...] = x1
        _, x1h = _rms_parts(x1)
        h2_ref[...] = (x1h * wpf_ref[...]).astype(bf16)

    row = pl.BlockSpec((tm, D), lambda i: (i, 0))
    vec = pl.BlockSpec((1, D), lambda i: (0, 0))
    return pl.pallas_call(
        body,
        grid=(S // tm,),
        in_specs=[row, pl.BlockSpec((tm, merged.shape[1]), lambda i: (i, 0)),
                  pl.BlockSpec(w_out.shape, lambda i: (0, 0)), vec, vec],
        out_specs=[row, row, row],
        out_shape=[SDS((S, D), f32), SDS((S, D), f32), SDS((S, D), bf16)],
        compiler_params=_cparams(("parallel",), VMEM_BIG),
        name="out_proj_mid_fwd",
    )(x, merged, w_out, w_pm, w_pf)


def _final(x1, act, w_down, tgt, w_pfn):
    S, D = x1.shape
    tm = _pick(S, MM_ROWS)
    nt = S // tm

    def body(x1_ref, a_ref, wd_ref, t_ref, w_ref, loss_ref, dy_ref, dfo_ref, gw_ref, lacc, gacc):
        i = pl.program_id(0)

        @pl.when(i == 0)
        def _():
            lacc[...] = jnp.zeros_like(lacc)
            gacc[...] = jnp.zeros_like(gacc)

        w = w_ref[...]
        r, foh = _rms_parts(jnp.dot(a_ref[...], wd_ref[...], preferred_element_type=f32))
        y = x1_ref[...] + foh * w
        err = y - t_ref[...]
        lacc[...] += _colsum8(err * err)
        dy = err * (1.0 / D)
        dy_ref[...] = dy
        gacc[...] += _colsum8(dy * foh)
        dfo_ref[...] = _rms_bwd(foh, r, w, dy).astype(bf16)

        @pl.when(i == nt - 1)
        def _():
            loss_ref[...] = jnp.full((SUBLANE, LANE), 0.5 / D, f32) * jnp.sum(lacc[...])
            gw_ref[...] = jnp.sum(gacc[...], axis=0, keepdims=True)

    row = pl.BlockSpec((tm, D), lambda i: (i, 0))
    vec = pl.BlockSpec((1, D), lambda i: (0, 0))
    return pl.pallas_call(
        body,
        grid=(nt,),
        in_specs=[row, pl.BlockSpec((tm, act.shape[1]), lambda i: (i, 0)),
                  pl.BlockSpec(w_down.shape, lambda i: (0, 0)), row, vec],
        out_specs=[pl.BlockSpec((SUBLANE, LANE), lambda i: (0, 0)), row, row, vec],
        out_shape=[SDS((SUBLANE, LANE), f32), SDS((S, D), f32), SDS((S, D), bf16), SDS((1, D), f32)],
        scratch_shapes=[pltpu.VMEM((SUBLANE, D), f32), pltpu.VMEM((SUBLANE, D), f32)],
        compiler_params=_cparams(("arbitrary",), VMEM_BIG),
        name="down_proj_final_loss",
    )(x1, act, w_down, tgt, w_pfn)


MID_BWD_ROWS = 256


def _mid_bwd(dy, dug, duv, wt_g, wt_v, x1, mo, w_pf, w_pm):
    S, D = dy.shape
    tm = _pick(S, MID_BWD_ROWS)
    nt = S // tm

    def body(dy_ref, dug_ref, duv_ref, wg_ref, wv_ref, x1_ref, mo_ref, wpf_ref, wpm_ref,
             dx1_ref, dmo_ref, gpf_ref, gpm_ref, apf, apm):
        i = pl.program_id(0)

        @pl.when(i == 0)
        def _():
            apf[...] = jnp.zeros_like(apf)
            apm[...] = jnp.zeros_like(apm)

        r1, x1h = _rms_parts(x1_ref[...])
        dh2 = jnp.dot(dug_ref[...], wg_ref[...], preferred_element_type=f32) \
            + jnp.dot(duv_ref[...], wv_ref[...], preferred_element_type=f32)
        apf[...] += _colsum8(dh2 * x1h)
        dx1 = dy_ref[...] + _rms_bwd(x1h, r1, wpf_ref[...], dh2)
        dx1_ref[...] = dx1
        rm, moh = _rms_parts(mo_ref[...])
        apm[...] += _colsum8(dx1 * moh)
        dmo_ref[...] = _rms_bwd(moh, rm, wpm_ref[...], dx1).astype(bf16)

        @pl.when(i == nt - 1)
        def _():
            gpf_ref[...] = jnp.sum(apf[...], axis=0, keepdims=True)
            gpm_ref[...] = jnp.sum(apm[...], axis=0, keepdims=True)

    row = pl.BlockSpec((tm, D), lambda i: (i, 0))
    vec = pl.BlockSpec((1, D), lambda i: (0, 0))
    return pl.pallas_call(
        body,
        grid=(nt,),
        in_specs=[row, pl.BlockSpec((tm, dug.shape[1]), lambda i: (i, 0)), pl.BlockSpec((tm, duv.shape[1]), lambda i: (i, 0)),
                  pl.BlockSpec(wt_g.shape, lambda i: (0, 0)), pl.BlockSpec(wt_v.shape, lambda i: (0, 0)),
                  row, row, vec, vec],
        out_specs=[row, row, vec, vec],
        out_shape=[SDS((S, D), f32), SDS((S, D), bf16), SDS((1, D), f32), SDS((1, D), f32)],
        scratch_shapes=[pltpu.VMEM((SUBLANE, D), f32), pltpu.VMEM((SUBLANE, D), f32)],
        compiler_params=_cparams(("arbitrary",), VMEM_BIG),
        name="dh2_mid_bwd",
    )(dy, dug, duv, wt_g, wt_v, x1, mo, w_pf, w_pm)


def _first_bwd(x, dx1, dh_a, dh_b, dh_c, w_pre):
    S, D = x.shape
    tm = _pick(S, 512)
    nt = S // tm
    nc = D // LANE

    def body(*refs):
        x_ref, dx1_ref, a_ref = refs[:3]
        b_refs, c_refs, w_ref = refs[3:3 + nc], refs[3 + nc:3 + 2 * nc], refs[3 + 2 * nc]
        gx_ref, gw_ref, acc, dh_s, sb, sc = refs[4 + 2 * nc:]
        i = pl.program_id(0)

        @pl.when(i == 0)
        def _():
            acc[...] = jnp.zeros_like(acc)

        for j in range(nc):
            cols = slice(j * LANE, (j + 1) * LANE)
            _to_natural(b_refs[j], sb, 4)
            _to_natural(c_refs[j], sc, 16)
            dh_s[:, cols] = (a_ref[:, cols] + sb[...]) + sc[...]
        r, xh = _rms_parts(x_ref[...])
        dh = dh_s[...]
        acc[...] += _colsum8(dh * xh)
        gx_ref[...] = dx1_ref[...] + _rms_bwd(xh, r, w_ref[...], dh)

        @pl.when(i == nt - 1)
        def _():
            gw_ref[...] = jnp.sum(acc[...], axis=0, keepdims=True)

    row = pl.BlockSpec((tm, D), lambda i: (i, 0))
    vec = pl.BlockSpec((1, D), lambda i: (0, 0))
    perm = lambda d: [pl.BlockSpec((d, tm // d, LANE), lambda i, j=j: (0, i, j)) for j in range(nc)]
    return pl.pallas_call(
        body,
        grid=(nt,),
        in_specs=[row, row, row] + perm(4) + perm(16) + [vec],
        out_specs=[row, vec],
        out_shape=[SDS((S, D), f32), SDS((1, D), f32)],
        scratch_shapes=[pltpu.VMEM((SUBLANE, D), f32), pltpu.VMEM((tm, D), f32), pltpu.VMEM((tm, LANE), f32),
                        pltpu.VMEM((tm, LANE), f32)],
        compiler_params=_cparams(("arbitrary",), VMEM_BIG),
        name="first_bwd",
    )(x, dx1, dh_a, *([dh_b.reshape(4, S // 4, D)] * nc), *([dh_c.reshape(16, S // 16, D)] * nc), w_pre)


def _t5_bucket(dist):
    n = jnp.maximum(dist, 0)
    nf = jnp.maximum(n, 1).astype(f32)
    large = MAX_EXACT + (jnp.log(nf / MAX_EXACT) / math.log(MAX_DISTANCE / MAX_EXACT)
                         * (NUM_BUCKETS - MAX_EXACT)).astype(jnp.int32)
    large = jnp.minimum(large, NUM_BUCKETS - 1)
    return jnp.where(n < MAX_EXACT, n, large)


def _bias_consts(d, after=None):
    if after is not None:
        d, _ = lax.optimization_barrier((jnp.int32(d), after))
    blk = ATTN_BLOCK
    rel = jnp.arange(blk)[:, None] + blk - jnp.arange(2 * blk)[None, :]
    in_win = (rel >= 0) & (rel <= blk)
    bucket = _t5_bucket(rel * d).reshape(1, -1)
    onehot = (bucket == jnp.arange(NUM_BUCKETS)[:, None]).astype(f32)
    return onehot, in_win.astype(f32).reshape(1, -1)


def _bias_build(tab_t, onehot, maskf, name, after):
    H = tab_t.shape[0]

    def body(t_ref, oh_ref, m_ref, after_ref, o_ref):
        b = jnp.dot(t_ref[...], oh_ref[...], precision=HIGHEST, preferred_element_type=f32)
        o_ref[...] = jnp.where(m_ref[...] > 0.5, b, NEG_INF)

    vm = pl.BlockSpec(memory_space=pltpu.VMEM)
    return pl.pallas_call(body, out_shape=SDS((H, onehot.shape[1]), f32), name=name,
                          in_specs=[vm, vm, vm, pl.BlockSpec(memory_space=pl.ANY)], out_specs=vm,
                          )(tab_t, onehot, maskf, after)


def _bias_grad(dbias_flat, onehot, name):
    H = dbias_flat.shape[0]

    def body(g_ref, oh_ref, o_ref):
        o_ref[...] = lax.dot_general(oh_ref[...], g_ref[...], NT, precision=HIGHEST, preferred_element_type=f32)

    return pl.pallas_call(body, out_shape=SDS((NUM_BUCKETS, H), f32), name=name)(dbias_flat, onehot)


ATTN_TILE = 512
ATTN_SUB = ATTN_TILE // ATTN_BLOCK
ATTN_HP = 4
ATTN_WIDE = ATTN_HP * LANE


def _qkv_specs(nt):
    tile = (ATTN_TILE, ATTN_WIDE)
    blk = (ATTN_BLOCK, ATTN_WIDE)
    sec = ATTN_OUT // ATTN_WIDE
    cur = lambda off: (lambda h, t: (jnp.minimum(t, nt - 1), off + h))
    prev = lambda off: (lambda h, t: (jnp.maximum(jnp.minimum(t, nt - 1) * ATTN_SUB - 1, 0), off + h))
    return [pl.BlockSpec(tile, cur(0)), pl.BlockSpec(blk, prev(sec)), pl.BlockSpec(tile, cur(sec)),
            pl.BlockSpec(blk, prev(2 * sec)), pl.BlockSpec(tile, cur(2 * sec))]


def _head_masks():
    lane = lax.broadcasted_iota(jnp.int32, (ATTN_BLOCK, LANE), 1)
    return lane < HEAD_DIM


def _stack_heads(x2, low):
    zero = jnp.zeros_like(x2)
    return jnp.concatenate([jnp.where(low, x2, zero), jnp.where(low, zero, x2)], axis=0)


def _attn_fwd(qkv, bias, bps, name, after=None):
    S = qkv.shape[0]
    nt = S // ATTN_TILE
    scale = HEAD_DIM ** -0.5

    def body(q_ref, kp_ref, kc_ref, vp_ref, vc_ref, b_ref, *rest):
        o_ref, l_ref = rest[-2:]
        t = pl.program_id(1)
        low = _head_masks()
        col = lax.broadcasted_iota(jnp.int32, (2 * ATTN_BLOCK, 2 * ATTN_BLOCK), 1)
        for hp in range(ATTN_HP):
            cols = slice(hp * LANE, (hp + 1) * LANE)
            kk = jnp.concatenate([kp_ref[:, cols], kc_ref[:, cols]], axis=0)
            vv = jnp.concatenate([vp_ref[:, cols], vc_ref[:, cols]], axis=0)
            bias2 = b_ref[2 * hp:2 * hp + 2].reshape(2 * ATTN_BLOCK, 2 * ATTN_BLOCK)
            for b in range(ATTN_SUB):
                lo = b * ATTN_BLOCK
                rows = slice(lo, lo + ATTN_BLOCK)
                keys = slice(lo, lo + 2 * ATTN_BLOCK)
                dead = jnp.logical_and((t * ATTN_SUB + b) % bps == 0, col < ATTN_BLOCK)
                q2 = _stack_heads(q_ref[rows, cols], low)
                kb, vb = kk[keys], vv[keys]
                s = lax.dot_general(q2, kb, NT, preferred_element_type=f32) * scale + bias2
                s = jnp.where(dead, NEG_INF, s)
                m = jnp.max(s, axis=-1, keepdims=True)
                p = jnp.exp(s - m)
                l = jnp.sum(p, axis=-1, keepdims=True)
                o2 = jnp.dot(p.astype(bf16), vb, preferred_element_type=f32) / l
                lse = m + jnp.log(l)
                o_ref[rows, cols] = jnp.where(low, o2[:ATTN_BLOCK], o2[ATTN_BLOCK:])
                l_ref[rows, cols] = jnp.where(low, lse[:ATTN_BLOCK], lse[ATTN_BLOCK:])

    tile = pl.BlockSpec((ATTN_TILE, ATTN_WIDE), lambda h, t: (t, h))
    return pl.pallas_call(
        body,
        grid=(4 // ATTN_HP, nt),
        in_specs=_qkv_specs(nt) + [pl.BlockSpec((2 * ATTN_HP, ATTN_BLOCK, 2 * ATTN_BLOCK), lambda h, t: (h, 0, 0))]
        + ([] if after is None else [pl.BlockSpec(memory_space=pl.ANY)]),
        out_specs=[tile, tile],
        out_shape=[SDS((S, ATTN_OUT), f32), SDS((S, ATTN_OUT), f32)],
        compiler_params=_cparams(("parallel", "parallel")),
        name=name,
    )(qkv, qkv, qkv, qkv, qkv, bias, *([] if after is None else [after]))


def _attn_bwd(qkv, bias, do, dvec, lse, bps, name):
    S = qkv.shape[0]
    nt = S // ATTN_TILE
    scale = HEAD_DIM ** -0.5

    def assemble(parts):
        rows = [parts[0][:ATTN_BLOCK]]
        for b in range(ATTN_SUB - 1):
            rows.append(parts[b][ATTN_BLOCK:] + parts[b + 1][:ATTN_BLOCK])
        rows.append(parts[-1][ATTN_BLOCK:])
        return rows

    def body(q_ref, kp_ref, kc_ref, vp_ref, vc_ref, b_ref, do_ref, dvec_ref, lse_ref,
             dq_ref, dk_ref, dv_ref, db_ref, ck, cv):
        t = pl.program_id(1)
        last = ATTN_TILE - ATTN_BLOCK

        @pl.when(t == 0)
        def _():
            ck[...] = jnp.zeros_like(ck)
            cv[...] = jnp.zeros_like(cv)
            db_ref[...] = jnp.zeros_like(db_ref)

        @pl.when(t < nt)
        def _():
            low = _head_masks()
            col = lax.broadcasted_iota(jnp.int32, (2 * ATTN_BLOCK, 2 * ATTN_BLOCK), 1)
            per_row = lambda t2: jnp.concatenate([t2[:, 0:1], t2[:, HEAD_DIM:HEAD_DIM + 1]], axis=0)
            for hp in range(ATTN_HP):
                cols = slice(hp * LANE, (hp + 1) * LANE)
                kk = jnp.concatenate([kp_ref[:, cols], kc_ref[:, cols]], axis=0)
                vv = jnp.concatenate([vp_ref[:, cols], vc_ref[:, cols]], axis=0)
                bias2 = b_ref[2 * hp:2 * hp + 2].reshape(2 * ATTN_BLOCK, 2 * ATTN_BLOCK)
                dk_parts, dv_parts = [], []
                dsum = None
                for b in range(ATTN_SUB):
                    lo = b * ATTN_BLOCK
                    rows = slice(lo, lo + ATTN_BLOCK)
                    keys = slice(lo, lo + 2 * ATTN_BLOCK)
                    dead = jnp.logical_and((t * ATTN_SUB + b) % bps == 0, col < ATTN_BLOCK)
                    q2 = _stack_heads(q_ref[rows, cols], low)
                    do2 = _stack_heads(do_ref[rows, cols].astype(bf16), low)
                    kb, vb = kk[keys], vv[keys]
                    s = lax.dot_general(q2, kb, NT, preferred_element_type=f32) * scale + bias2
                    s = jnp.where(dead, NEG_INF, s)
                    p = jnp.exp(s - per_row(lse_ref[rows, cols]))
                    dp = lax.dot_general(do2, vb, NT, preferred_element_type=f32)
                    ds = p * (dp - per_row(dvec_ref[rows, cols]))
                    dsum = ds if dsum is None else dsum + ds
                    dsb = ds.astype(bf16)
                    dq2 = jnp.dot(dsb, kb, preferred_element_type=f32) * scale
                    dq_ref[rows, cols] = jnp.where(low, dq2[:ATTN_BLOCK], dq2[ATTN_BLOCK:]).astype(bf16)
                    dk_parts.append(lax.dot_general(dsb, q2, TN, preferred_element_type=f32) * scale)
                    dv_parts.append(lax.dot_general(p.astype(bf16), do2, TN, preferred_element_type=f32))
                db_ref[2 * hp:2 * hp + 2] += dsum.reshape(2, ATTN_BLOCK, 2 * ATTN_BLOCK)
                for parts, carry, out_ref in ((dk_parts, ck, dk_ref), (dv_parts, cv, dv_ref)):
                    rws = assemble(parts)
                    out_ref[:last, cols] = carry[:last, cols].astype(bf16)
                    out_ref[last:, cols] = (carry[last:, cols] + rws[0]).astype(bf16)
                    for b in range(ATTN_SUB):
                        carry[b * ATTN_BLOCK:(b + 1) * ATTN_BLOCK, cols] = rws[b + 1]

        @pl.when(t == nt)
        def _():
            dk_ref[...] = ck[...].astype(bf16)
            dv_ref[...] = cv[...].astype(bf16)

    tile = (ATTN_TILE, ATTN_WIDE)
    cur = pl.BlockSpec(tile, lambda h, t: (jnp.minimum(t, nt - 1), h))
    lag = pl.BlockSpec(tile, lambda h, t: (jnp.maximum(t - 1, 0), h))
    bspec = pl.BlockSpec((2 * ATTN_HP, ATTN_BLOCK, 2 * ATTN_BLOCK), lambda h, t: (h, 0, 0))
    return pl.pallas_call(
        body,
        grid=(4 // ATTN_HP, nt + 1),
        in_specs=_qkv_specs(nt) + [bspec, cur, cur, cur],
        out_specs=[cur, lag, lag, bspec],
        out_shape=[SDS((S, ATTN_OUT), bf16), SDS((S, ATTN_OUT), bf16), SDS((S, ATTN_OUT), bf16),
                   SDS((8, ATTN_BLOCK, 2 * ATTN_BLOCK), f32)],
        scratch_shapes=[pltpu.VMEM(tile, f32), pltpu.VMEM(tile, f32)],
        compiler_params=_cparams(("parallel", "arbitrary")),
        name=name,
    )(qkv, qkv, qkv, qkv, qkv, bias, do, dvec, lse)


def _attn_merge(o0, o1, o2, l0, l1, l2):
    S, W = o0.shape
    R = PERM_ROWS

    def body(o0_ref, o1_ref, o2_ref, l0_ref, l1_ref, l2_ref, y_ref, yb_ref, w0_ref, w1_ref, w2_ref,
             so1, so2, sl1, sl2):
        _to_natural(o1_ref, so1, 4)
        _to_natural(l1_ref, sl1, 4)
        _to_natural(o2_ref, so2, 16)
        _to_natural(l2_ref, sl2, 16)
        a, b, c = l0_ref[...], sl1[...], sl2[...]
        m = jnp.maximum(jnp.maximum(a, b), c)
        ea, eb, ec = jnp.exp(a - m), jnp.exp(b - m), jnp.exp(c - m)
        den = (ea + eb) + ec
        w0, w1, w2 = ea / den, eb / den, ec / den
        y = (w0 * o0_ref[...] + w1 * so1[...]) + w2 * so2[...]
        y_ref[...] = y
        yb_ref[...] = y.astype(bf16)
        w0_ref[...] = w0
        w1_ref[...] = w1
        w2_ref[...] = w2

    nat = pl.BlockSpec((R, LANE), lambda i, j: (i, j))
    v4 = lambda t: t.reshape(4, S // 4, W)
    v16 = lambda t: t.reshape(16, S // 16, W)
    return pl.pallas_call(
        body,
        grid=(S // R, W // LANE),
        in_specs=[nat, _perm_spec(4), _perm_spec(16)] * 2,
        out_specs=[nat] * 5,
        out_shape=[SDS((S, W), f32), SDS((S, W), bf16)] + [SDS((S, W), f32)] * 3,
        scratch_shapes=[pltpu.VMEM((R, LANE), f32)] * 4,
        compiler_params=_cparams(("parallel", "parallel"), VMEM_BIG),
        name="attn_merge",
    )(o0, v4(o1), v16(o2), l0, v4(l1), v16(l2))


def _attn_merge_bwd(dy, y, w0, w1, w2, after=None):
    S, W = dy.shape
    R = PERM_ROWS

    def body(dy_ref, y_ref, w0_ref, w1_ref, w2_ref, *rest):
        a0, a1, a2, b0, b1, b2, sa, sb = rest[-8:]
        dyv = dy_ref[...]
        r = lax.broadcasted_iota(jnp.int32, (LANE, LANE), 0) // HEAD_DIM
        c = lax.broadcasted_iota(jnp.int32, (LANE, LANE), 1) // HEAD_DIM
        seg = jnp.where(r == c, 1.0, 0.0).astype(f32)
        cbar = jnp.dot(dyv * y_ref[...], seg, precision=HIGHEST, preferred_element_type=f32)
        w = w0_ref[...]
        a0[...] = (w * dyv).astype(bf16)
        b0[...] = w * cbar
        for d, w_ref, a_ref, b_ref in ((4, w1_ref, a1, b1), (16, w2_ref, a2, b2)):
            w = w_ref[...]
            sa[...] = w * dyv
            sb[...] = w * cbar
            n = R // d
            for k in range(d):
                rows = pl.ds(k, n, stride=d)
                a_ref[k] = sa[rows, :].astype(bf16)
                b_ref[k] = sb[rows, :]

    nat = pl.BlockSpec((R, LANE), lambda i, j: (i, j))
    shapes = lambda dt: [SDS((S, W), dt), SDS((4, S // 4, W), dt), SDS((16, S // 16, W), dt)]
    outs = pl.pallas_call(
        body,
        grid=(S // R, W // LANE),
        in_specs=[nat] * 5 + ([] if after is None else [pl.BlockSpec(memory_space=pl.ANY)]),
        out_specs=[nat, _perm_spec(4), _perm_spec(16)] * 2,
        out_shape=shapes(bf16) + shapes(f32),
        scratch_shapes=[pltpu.VMEM((R, LANE), f32)] * 2,
        compiler_params=_cparams(("parallel", "parallel"), VMEM_BIG),
        name="attn_merge_bwd",
    )(dy, y, w0, w1, w2, *([] if after is None else [after]))
    return [t.reshape(S, W) for t in outs]


HGRN_SB = 256
HGRN_PAIR = 4


def _chunk_masks():
    r = jnp.arange(HGRN_SB)[:, None]
    c = jnp.arange(HGRN_SB)[None, :]
    same = (r // HGRN_CHUNK) == (c // HGRN_CHUNK)
    return jnp.stack([same & (c <= r), same, same & (c >= r)]).astype(bf16)


def _mask_dot(mask, x):
    hi = x.astype(bf16)
    r1 = x - hi.astype(f32)
    mid = r1.astype(bf16)
    lo = (r1 - mid.astype(f32)).astype(bf16)
    p = jnp.dot(mask, jnp.concatenate([hi, mid, lo], axis=1), preferred_element_type=f32)
    n = x.shape[1]
    return (p[:, :n] + p[:, n:2 * n]) + p[:, 2 * n:]


def _hgrn_prep(q_raw, f_raw, lbv, tril, same):
    sq = _sigmoid(q_raw)
    qs = q_raw * sq
    sig = _sigmoid(f_raw)
    f = lbv + (1.0 - lbv) * sig
    g = jnp.log(f)
    k = 1.0 - f
    G = _mask_dot(tril, g)
    GL = _mask_dot(same, g)
    eG = jnp.exp(G)
    einv = jnp.exp(-G)
    edec = jnp.exp(GL - G)
    return dict(sq=sq, qs=qs, sig=sig, f=f, k=k, eG=eG, einv=einv, edec=edec, eGL=jnp.exp(GL),
                qt=qs * eG, kt=k * einv, kd=k * edec)


def _hgrn_fwd(hg, lb, normw):
    S = hg.shape[0]
    sb = HGRN_SB
    nsb = S // sb
    nch = sb // HGRN_CHUNK

    def body(q_ref, f_ref, v_ref, og_ref, lb_ref, nw_ref, m_ref, y_ref, o_ref, ck_ref, st):
        j = pl.program_id(1)

        @pl.when(j == 0)
        def _():
            st[...] = jnp.zeros_like(st)

        tril_m = m_ref[0]
        tril = tril_m.astype(f32) > 0.5

        def one_head(hh):
            cols = slice(hh * LANE, (hh + 1) * LANE)
            ST = st[hh]
            ck_ref[hh, 0] = ST
            pr = _hgrn_prep(q_ref[:, cols], f_ref[:, cols], lb_ref[:, cols], tril_m, m_ref[1])
            qtb, ktb, kdb = pr["qt"].astype(bf16), pr["kt"].astype(bf16), pr["kd"].astype(bf16)
            eGL = pr["eGL"]
            vb = v_ref[:, cols].astype(bf16)
            A = jnp.where(tril, lax.dot_general(qtb, ktb, NT, preferred_element_type=f32), 0.0)
            o = jnp.dot(A.astype(bf16), vb, preferred_element_type=f32)
            outs = []
            for ci in range(nch):
                lo = ci * HGRN_CHUNK
                sl = slice(lo, lo + HGRN_CHUNK)
                outs.append(o[sl] + lax.dot_general(qtb[sl], ST.astype(bf16), NT, preferred_element_type=f32))
                ST = ST * eGL[lo:lo + 1, :] + lax.dot_general(vb[sl], kdb[sl], TN, preferred_element_type=f32)
            st[hh] = ST
            of = jnp.concatenate(outs, axis=0)
            o_ref[:, cols] = of
            rms = lax.rsqrt(jnp.mean(of * of, axis=-1, keepdims=True) + EPS)
            ogv = og_ref[:, cols]
            y_ref[:, cols] = ((of * rms * nw_ref[...]) * (ogv * _sigmoid(ogv))).astype(bf16)

        for hh in range(HGRN_PAIR):
            one_head(hh)

    wide = HGRN_PAIR * LANE
    col = lambda off: pl.BlockSpec((sb, wide), lambda h, j: (j, off // HGRN_PAIR + h))
    return pl.pallas_call(
        body,
        grid=(4 // HGRN_PAIR, nsb),
        in_specs=[col(0), col(4), col(8), col(12), pl.BlockSpec((1, wide), lambda h, j: (0, h)),
                  pl.BlockSpec((1, LANE), lambda h, j: (0, 0)),
                  pl.BlockSpec((3, sb, sb), lambda h, j: (0, 0, 0))],
        out_specs=[col(0), col(0), pl.BlockSpec((HGRN_PAIR, 1, LANE, LANE), lambda h, j: (h, j, 0, 0))],
        out_shape=[SDS((S, HGRN_W), bf16), SDS((S, HGRN_W), f32), SDS((4, nsb, LANE, LANE), f32)],
        scratch_shapes=[pltpu.VMEM((HGRN_PAIR, LANE, LANE), f32)],
        compiler_params=_cparams(("parallel", "arbitrary")),
        name="hgrn_fwd",
    )(hg, hg, hg, hg, lb, normw, _chunk_masks())


def _hgrn_bwd(hg, o_raw, dy, ck, lb, normw, after=None):
    S = hg.shape[0]
    sb = HGRN_SB
    nsb = S // sb
    nch = sb // HGRN_CHUNK

    def body(q_ref, f_ref, v_ref, og_ref, o_ref, dy_ref, ck_ref, lb_ref, nw_ref, m_ref, *rest):
        dq_ref, df_ref, dv_ref, dog_ref, glb_ref, gnw_ref, dst, alb, anw = rest[-9:]
        j = pl.program_id(1)

        @pl.when(j == 0)
        def _():
            dst[...] = jnp.zeros_like(dst)
            alb[...] = jnp.zeros_like(alb)
            anw[...] = jnp.zeros_like(anw)

        tril_m = m_ref[0]
        tril = tril_m.astype(f32) > 0.5
        nw = nw_ref[...]

        def one_head(hh):
            cols = slice(hh * LANE, (hh + 1) * LANE)
            lbv = lb_ref[:, cols]
            q_raw = q_ref[:, cols]
            pr = _hgrn_prep(q_raw, f_ref[:, cols], lbv, tril_m, m_ref[1])
            qt, kt, kd, eGL = pr["qt"], pr["kt"], pr["kd"], pr["eGL"]
            qtb, ktb, kdb = qt.astype(bf16), kt.astype(bf16), kd.astype(bf16)
            vb = v_ref[:, cols].astype(bf16)

            o = o_ref[:, cols]
            ogv = og_ref[:, cols]
            sog = _sigmoid(ogv)
            rms = lax.rsqrt(jnp.mean(o * o, axis=-1, keepdims=True) + EPS)
            oh = o * rms
            dyv = dy_ref[:, cols]
            dog_ref[:, cols] = (dyv * (oh * nw) * (sog * (1.0 + ogv * (1.0 - sog)))).astype(bf16)
            dohw = dyv * (ogv * sog)
            anw[:, cols] += _colsum8(dohw * oh)
            doh = dohw * nw
            do = rms * (doh - oh * jnp.mean(doh * oh, axis=-1, keepdims=True))
            dob = do.astype(bf16)

            Ab = jnp.where(tril, lax.dot_general(qtb, ktb, NT, preferred_element_type=f32), 0.0).astype(bf16)
            dAb = jnp.where(tril, lax.dot_general(dob, vb, NT, preferred_element_type=f32), 0.0).astype(bf16)
            dv_acc = lax.dot_general(Ab, dob, TN, preferred_element_type=f32)
            dqt = jnp.dot(dAb, ktb, preferred_element_type=f32)
            dkt = lax.dot_general(dAb, qtb, TN, preferred_element_type=f32)

            ST = ck_ref[hh, 0]
            states = []
            for ci in range(nch):
                lo = ci * HGRN_CHUNK
                sl = slice(lo, lo + HGRN_CHUNK)
                states.append(ST)
                ST = ST * eGL[lo:lo + 1, :] + lax.dot_general(vb[sl], kdb[sl], TN, preferred_element_type=f32)

            dST = dst[hh]
            dqt_i, dkd_i, dv_i, deg_i = [None] * nch, [None] * nch, [None] * nch, [None] * nch
            for ci in reversed(range(nch)):
                lo = ci * HGRN_CHUNK
                sl = slice(lo, lo + HGRN_CHUNK)
                ST0 = states[ci]
                dSTb = dST.astype(bf16)
                dv_i[ci] = lax.dot_general(kdb[sl], dSTb, NT, preferred_element_type=f32)
                dqt_i[ci] = jnp.dot(dob[sl], ST0.astype(bf16), preferred_element_type=f32)
                dkd_i[ci] = jnp.dot(vb[sl], dSTb, preferred_element_type=f32)
                deg_i[ci] = jnp.broadcast_to(jnp.sum(dST * ST0, axis=0, keepdims=True), (HGRN_CHUNK, LANE))
                dST = dST * eGL[lo:lo + 1, :] + lax.dot_general(dob[sl], qtb[sl], TN, preferred_element_type=f32)
            dst[hh] = dST

            dqt = dqt + jnp.concatenate(dqt_i, axis=0)
            dkd = jnp.concatenate(dkd_i, axis=0)
            dv_ref[:, cols] = (dv_acc + jnp.concatenate(dv_i, axis=0)).astype(bf16)
            deg = jnp.concatenate(deg_i, axis=0)

            dqs = dqt * pr["eG"]
            dkdkd = dkd * kd
            dG = dqt * qt - dkt * kt - dkdkd
            dk = dkt * pr["einv"] + dkd * pr["edec"]
            dGL = _mask_dot(m_ref[1], dkdkd) + eGL * deg
            dg = _mask_dot(m_ref[2], dG) + dGL
            df = dg / pr["f"] - dk
            sig = pr["sig"]
            df_ref[:, cols] = (df * (1.0 - lbv) * (sig * (1.0 - sig))).astype(bf16)
            alb[:, cols] += _colsum8(df * (1.0 - sig))
            sq = pr["sq"]
            dq_ref[:, cols] = (dqs * (sq * (1.0 + q_raw * (1.0 - sq)))).astype(bf16)

        for hh in range(HGRN_PAIR):
            one_head(hh)

        @pl.when(j == nsb - 1)
        def _():
            glb_ref[...] = jnp.broadcast_to(jnp.sum(alb[...], axis=0, keepdims=True), (SUBLANE, wide))
            gnw_ref[...] = jnp.broadcast_to(jnp.sum(anw[...], axis=0, keepdims=True), (SUBLANE, wide))

    wide = HGRN_PAIR * LANE
    rev = lambda off: pl.BlockSpec((sb, wide), lambda h, j: (nsb - 1 - j, off // HGRN_PAIR + h))
    stat = pl.BlockSpec((SUBLANE, wide), lambda h, j: (0, h))
    return pl.pallas_call(
        body,
        grid=(4 // HGRN_PAIR, nsb),
        in_specs=[rev(0), rev(4), rev(8), rev(12), rev(0), rev(0),
                  pl.BlockSpec((HGRN_PAIR, 1, LANE, LANE), lambda h, j: (h, nsb - 1 - j, 0, 0)),
                  pl.BlockSpec((1, wide), lambda h, j: (0, h)), pl.BlockSpec((1, LANE), lambda h, j: (0, 0)),
                  pl.BlockSpec((3, sb, sb), lambda h, j: (0, 0, 0))]
        + ([] if after is None else [pl.BlockSpec(memory_space=pl.ANY)]),
        out_specs=[rev(0), rev(0), rev(0), rev(0), stat, stat],
        out_shape=[SDS((S, HGRN_W), bf16)] * 4 + [SDS((SUBLANE, HGRN_W), f32)] * 2,
        scratch_shapes=[pltpu.VMEM((HGRN_PAIR, LANE, LANE), f32), pltpu.VMEM((SUBLANE, wide), f32),
                        pltpu.VMEM((SUBLANE, wide), f32)],
        compiler_params=_cparams(("parallel", "arbitrary")),
        name="hgrn_bwd",
    )(hg, hg, hg, hg, o_raw, dy, ck, lb, normw, _chunk_masks(), *([] if after is None else [after]))


def _lb_fwd(raw):
    def body(r_ref, o_ref):
        r = r_ref[...]
        m = jnp.max(r, axis=0, keepdims=True)
        e = jnp.exp(r - m)
        o_ref[...] = (e / jnp.sum(e, axis=0, keepdims=True))[0:1]

    return pl.pallas_call(body, out_shape=SDS((1, raw.shape[1]), f32), name="lb_fwd")(raw)


def _lb_bwd(raw, dlb):
    def body(r_ref, d_ref, o_ref):
        r = r_ref[...]
        m = jnp.max(r, axis=0, keepdims=True)
        e = jnp.exp(r - m)
        s = e / jnp.sum(e, axis=0, keepdims=True)
        s0 = s[0:1]
        onehot0 = jnp.where(lax.broadcasted_iota(jnp.int32, r.shape, 0) == 0, 1.0, 0.0)
        o_ref[...] = d_ref[...] * s0 * (onehot0 - s)

    return pl.pallas_call(body, out_shape=SDS(raw.shape, f32), name="lb_bwd")(raw, dlb)


def _gate_fwd(ya, yh, w_ba, w_bh, gc):
    S = ya.shape[0]
    D = w_ba.shape[1]
    tm = _pick(S, MM_ROWS)

    def body(ya_ref, yh_ref, wa_ref, wh_ref, g0_ref, g1_ref, a_ref, b_ref, o_ref):
        a = jnp.dot(ya_ref[...], wa_ref[...], preferred_element_type=f32).astype(bf16)
        b = jnp.dot(yh_ref[...], wh_ref[...], preferred_element_type=f32).astype(bf16)
        a_ref[...] = a
        b_ref[...] = b
        s0, s1 = _sigmoid(g0_ref[...].astype(f32)), _sigmoid(g1_ref[...].astype(f32))
        o_ref[...] = (s0 * a.astype(f32) + s1 * b.astype(f32)).astype(bf16)

    row = pl.BlockSpec((tm, D), lambda i: (i, 0))
    act = pl.BlockSpec((tm, ya.shape[1]), lambda i: (i, 0))
    wspec = pl.BlockSpec(w_ba.shape, lambda i: (0, 0))
    return pl.pallas_call(
        body,
        grid=(S // tm,),
        in_specs=[act, act, wspec, wspec, row, pl.BlockSpec((tm, D), lambda i: (i, 1))],
        out_specs=[row, row, row],
        out_shape=[SDS((S, D), bf16)] * 3,
        compiler_params=_cparams(("parallel",), VMEM_BIG),
        name="branch_gate_fwd",
    )(ya, yh, w_ba, w_bh, gc, gc)


def _gate_bwd(dmo, w_out, a, b, gc, w_ba, w_bh):
    S, D = a.shape
    W = w_ba.shape[0]
    tm = _pick(S, MM_ROWS)

    def body(dmo_ref, wo_ref, a_ref, b_ref, g0_ref, g1_ref, wa_ref, wh_ref,
             da_ref, db_ref, dg_ref, dya_ref, dyh_ref):
        dm = lax.dot_general(dmo_ref[...], wo_ref[...], NT, preferred_element_type=f32)
        dmv = dm.astype(bf16).astype(f32)
        s0, s1 = _sigmoid(g0_ref[...].astype(f32)), _sigmoid(g1_ref[...].astype(f32))
        da = (dmv * s0).astype(bf16)
        db = (dmv * s1).astype(bf16)
        da_ref[...] = da
        db_ref[...] = db
        dg_ref[:, :D] = (dmv * a_ref[...].astype(f32) * (s0 * (1.0 - s0))).astype(bf16)
        dg_ref[:, D:] = (dmv * b_ref[...].astype(f32) * (s1 * (1.0 - s1))).astype(bf16)
        dya_ref[...] = lax.dot_general(da, wa_ref[...], NT, preferred_element_type=f32)
        dyh_ref[...] = lax.dot_general(db, wh_ref[...], NT, preferred_element_type=f32)

    row = pl.BlockSpec((tm, D), lambda i: (i, 0))
    wide = pl.BlockSpec((tm, 2 * D), lambda i: (i, 0))
    narrow = pl.BlockSpec((tm, W), lambda i: (i, 0))
    whole = lambda t: pl.BlockSpec(t.shape, lambda i: (0, 0))
    return pl.pallas_call(
        body,
        grid=(S // tm,),
        in_specs=[row, whole(w_out), row, row, row, pl.BlockSpec((tm, D), lambda i: (i, 1)), whole(w_ba), whole(w_bh)],
        out_specs=[row, row, wide, narrow, narrow],
        out_shape=[SDS((S, D), bf16), SDS((S, D), bf16), SDS((S, 2 * D), bf16), SDS((S, W), f32), SDS((S, W), f32)],
        compiler_params=_cparams(("parallel",), VMEM_BIG),
        name="gate_bwd_fused",
    )(dmo, w_out, a, b, gc, gc, w_ba, w_bh)


CONV_ROWS = 512
INV_SQRT2 = 0.7071067811865476
INV_SQRT_2PI = 0.3989422804014327


CONV_HALO = 16


def _shift_down(cur, prev, k):
    x = pltpu.roll(cur, k, 0)
    row = lax.broadcasted_iota(jnp.int32, (SUBLANE, LANE), 0)
    head = jnp.where(row < k, pltpu.roll(prev, k, 0)[:SUBLANE], x[:SUBLANE])
    return jnp.concatenate([head, x[SUBLANE:]], axis=0)


def _shift_up(cur, nxt, k):
    R = cur.shape[0]
    x = pltpu.roll(cur, R - k, 0)
    row = lax.broadcasted_iota(jnp.int32, (SUBLANE, LANE), 0)
    tail = jnp.where(row >= SUBLANE - k, pltpu.roll(nxt, SUBLANE - k, 0), x[R - SUBLANE:])
    return jnp.concatenate([x[:R - SUBLANE], tail], axis=0)


def _conv_rows(u_ref, w, b, r0, first):
    R = CONV_ROWS
    cur = u_ref[pl.ds(r0, R), :].astype(f32)
    prev = u_ref[pl.ds(pl.multiple_of(jnp.maximum(r0 - CONV_HALO, 0), CONV_HALO), CONV_HALO), :].astype(f32)
    prev = jnp.where(first, 0.0, prev)
    x1 = _shift_down(cur, prev, 1)
    x2 = _shift_down(cur, prev, 2)
    c = ((b + w[0:1] * x2) + w[1:2] * x1) + w[2:3] * cur
    return c, x2, x1, cur


def _conv_fwd(ug, uv, wg, wv, bg, bv):
    S, F = ug.shape
    nchunk = S // CONV_ROWS

    def body(ug_ref, uv_ref, wg_ref, wv_ref, bg_ref, bv_ref, o_ref):
        wgv, wvv, bgv, bvv = wg_ref[...], wv_ref[...], bg_ref[...], bv_ref[...]

        def step(ci, carry):
            r0 = pl.multiple_of(ci * CONV_ROWS, CONV_ROWS)
            cg = _conv_rows(ug_ref, wgv, bgv, r0, ci == 0)[0]
            cv = _conv_rows(uv_ref, wvv, bvv, r0, ci == 0)[0]
            gelu = 0.5 * cg * (1.0 + lax.erf(cg * INV_SQRT2))
            o_ref[pl.ds(r0, CONV_ROWS), :] = (gelu * cv).astype(bf16)
            return carry

        lax.fori_loop(0, nchunk, step, 0)

    col = pl.BlockSpec((S, LANE), lambda j: (0, j))
    w3 = pl.BlockSpec((3, LANE), lambda j: (0, j))
    b1 = pl.BlockSpec((1, LANE), lambda j: (0, j))
    return pl.pallas_call(
        body,
        grid=(F // LANE,),
        in_specs=[col, col, w3, w3, b1, b1],
        out_specs=col,
        out_shape=SDS((S, F), bf16),
        compiler_params=_cparams(("parallel",), VMEM_BIG),
        name="conv_fwd",
    )(ug, uv, wg, wv, bg, bv)


def _conv_bwd(ug, uv, dact, wg, wv, bg, bv):
    S, F = ug.shape
    R = CONV_ROWS
    nchunk = S // R

    def body(ug_ref, uv_ref, da_ref, wg_ref, wv_ref, bg_ref, bv_ref, dug_ref, duv_ref, sg_ref, sv_ref, dcg, dcv):
        wgv, wvv, bgv, bvv = wg_ref[...], wv_ref[...], bg_ref[...], bv_ref[...]
        zero = jnp.zeros((SUBLANE, LANE), f32)

        def fwd_step(ci, acc):
            r0 = pl.multiple_of(ci * R, R)
            cg, g2, g1, g0 = _conv_rows(ug_ref, wgv, bgv, r0, ci == 0)
            cv, v2, v1, v0 = _conv_rows(uv_ref, wvv, bvv, r0, ci == 0)
            da = da_ref[pl.ds(r0, R), :].astype(f32)
            cdf = 0.5 * (1.0 + lax.erf(cg * INV_SQRT2))
            pdf = INV_SQRT_2PI * jnp.exp(-0.5 * cg * cg)
            dg = da * cv * (cdf + cg * pdf)
            dv = da * (cg * cdf)
            dcg[pl.ds(r0, R), :] = dg
            dcv[pl.ds(r0, R), :] = dv
            new = (acc[0] + _colsum8(dg * g2), acc[1] + _colsum8(dg * g1), acc[2] + _colsum8(dg * g0),
                   acc[3] + _colsum8(dg),
                   acc[4] + _colsum8(dv * v2), acc[5] + _colsum8(dv * v1), acc[6] + _colsum8(dv * v0),
                   acc[7] + _colsum8(dv))
            return new

        acc = lax.fori_loop(0, nchunk, fwd_step, (zero,) * 8)
        rows = lax.broadcasted_iota(jnp.int32, (SUBLANE, LANE), 0)

        def stats(parts):
            out = jnp.zeros((SUBLANE, LANE), f32)
            for k, pt in enumerate(parts):
                out = jnp.where(rows == k, jnp.sum(pt, axis=0, keepdims=True), out)
            return out

        sg_ref[...] = stats(acc[0:4])
        sv_ref[...] = stats(acc[4:8])

        def du_rows(dc, w, r0, last):
            cur = dc[pl.ds(r0, R), :]
            nxt = dc[pl.ds(pl.multiple_of(jnp.minimum(r0 + R, S - SUBLANE), SUBLANE), SUBLANE), :]
            nxt = jnp.where(last, 0.0, nxt)
            return w[2:3] * cur + w[1:2] * _shift_up(cur, nxt, 1) + w[0:1] * _shift_up(cur, nxt, 2)

        def bwd_step(ci, carry):
            r0 = pl.multiple_of(ci * R, R)
            last = ci == nchunk - 1
            dug_ref[pl.ds(r0, R), :] = du_rows(dcg, wgv, r0, last).astype(bf16)
            duv_ref[pl.ds(r0, R), :] = du_rows(dcv, wvv, r0, last).astype(bf16)
            return carry

        lax.fori_loop(0, nchunk, bwd_step, 0)

    col = pl.BlockSpec((S, LANE), lambda j: (0, j))
    w3 = pl.BlockSpec((3, LANE), lambda j: (0, j))
    b1 = pl.BlockSpec((1, LANE), lambda j: (0, j))
    st = pl.BlockSpec((SUBLANE, LANE), lambda j: (0, j))
    return pl.pallas_call(
        body,
        grid=(F // LANE,),
        in_specs=[col, col, col, w3, w3, b1, b1],
        out_specs=[col, col, st, st],
        out_shape=[SDS((S, F), bf16), SDS((S, F), bf16), SDS((SUBLANE, F), f32), SDS((SUBLANE, F), f32)],
        scratch_shapes=[pltpu.VMEM((S, LANE), f32), pltpu.VMEM((S, LANE), f32)],
        compiler_params=_cparams(("parallel",), VMEM_BIG),
        name="conv_bwd",
    )(ug, uv, dact, wg, wv, bg, bv)


def _adam_math(w, g, m, v):
    m = ADAM_B1 * m + (1.0 - ADAM_B1) * g
    v = ADAM_B2 * v + (1.0 - ADAM_B2) * (g * g)
    m_hat = m / (1.0 - ADAM_B1 ** ADAM_STEP)
    v_hat = v / (1.0 - ADAM_B2 ** ADAM_STEP)
    delta = -ADAM_LR * (m_hat / (jnp.sqrt(v_hat) + ADAM_EPS) + ADAM_WD * w)
    return delta, m, v


def _adamw(w, m, v, g, name):
    R, C = w.shape
    parts = len(g.shape) == 3
    tr = R
    if R % 16 == 0:
        for t in range(R, 0, -16):
            if R % t == 0 and t * C * 4 <= ADAM_BLOCK_BYTES:
                tr = t
                break

    def body(w_ref, m_ref, v_ref, g_ref, go_ref, d_ref, mo_ref, vo_ref):
        if parts:
            gv = ((g_ref[0].astype(f32) + g_ref[1].astype(f32)) + g_ref[2].astype(f32)) + g_ref[3].astype(f32)
        else:
            gv = g_ref[...]
        go_ref[...] = gv
        d, mn, vn = _adam_math(w_ref[...], gv, m_ref[...], v_ref[...])
        d_ref[...] = d
        mo_ref[...] = mn
        vo_ref[...] = vn

    row = pl.BlockSpec((tr, C), lambda i: (i, 0))
    gspec = pl.BlockSpec((4, tr, C), lambda i: (0, i, 0)) if parts else row
    if isinstance(g, _Rows):
        lo, g = g.lo, g.full
        assert lo % (2 * SUBLANE) == 0 and tr % (2 * SUBLANE) == 0
        gspec = pl.BlockSpec((pl.Element(4), pl.Element(tr), pl.Element(C)),
                             lambda i: (0, pl.multiple_of(lo + i * tr, 2 * SUBLANE), 0))
    return pl.pallas_call(
        body,
        grid=(R // tr,),
        in_specs=[row, row, row, gspec],
        out_specs=[row] * 4,
        out_shape=[SDS((R, C), f32)] * 4,
        compiler_params=_cparams(("parallel",), VMEM_BIG),
        name=name,
    )(w, m, v, g)


def _sum8(parts, name):
    _, _, R, C = parts.shape

    def body(p_ref, o_ref):
        acc = p_ref[0, 0]
        for c in range(2):
            for k in range(4):
                if c or k:
                    acc = acc + p_ref[c, k]
        o_ref[...] = acc

    return pl.pallas_call(body, out_shape=SDS((R, C), f32), name=name)(parts)


def _pair_add(by_core, b, name):
    _, K, R, C = by_core.shape
    tr = R // 2 if R % 32 == 0 else R

    def body(c_ref, a_ref, b_ref, o_ref):
        o_ref[...] = (a_ref[0].astype(f32) + b_ref[...].astype(f32)).astype(bf16)

    blk = pl.BlockSpec((1, tr, C), lambda k, i, c: (k, i, 0))
    return pl.pallas_call(
        body,
        grid_spec=pltpu.PrefetchScalarGridSpec(
            num_scalar_prefetch=1,
            grid=(K, R // tr),
            in_specs=[pl.BlockSpec((1, 1, tr, C), lambda k, i, c: (c[0], k, i, 0)), blk],
            out_specs=blk,
        ),
        out_shape=SDS((K, R, C), bf16),
        compiler_params=_cparams(("parallel", "parallel")),
        name=name,
    )(lax.axis_index("c").astype(jnp.int32).reshape(1), by_core, b)


_ANY = pl.BlockSpec(memory_space=pl.ANY)


def _chip_out_shape(src, gather):
    return SDS((4,) + tuple(src.shape if gather else src.shape[1:]), src.dtype)


def _fill_own(out, src, gather):
    mine = 2 * lax.axis_index("x") + lax.axis_index("y")
    own = src if gather else lax.dynamic_index_in_dim(src, mine, axis=0, keepdims=False)
    return lax.dynamic_update_index_in_dim(out, own, mine, axis=0)


_HBM = pl.BlockSpec(memory_space=pltpu.HBM)
_SEM = pl.BlockSpec(memory_space=pltpu.SEMAPHORE)
_EFFECT = pltpu.SideEffectType.DATAFLOW_SIDE_EFFECTING
_SPLIT_PEERS = {"chip_gather": 3, "chip_gather_wide": 3, "chip_xchg": 3, "core_fill": 4, "core_swap": 1}


def _split_land(src, kind):
    if kind == "chip_gather_wide":
        return SDS((4, 2) + tuple(src.shape), src.dtype)
    if kind == "core_fill":
        return SDS((SUBLANE, LANE), src.dtype)
    if kind == "core_swap":
        return SDS(tuple(src.shape[1:]), src.dtype)
    return _chip_out_shape(src, kind == "chip_gather")


def _split_copies(src_ref, land_ref, sems, kind):
    x, y, c = lax.axis_index("x"), lax.axis_index("y"), lax.axis_index("c")
    n = _SPLIT_PEERS[kind]
    if kind == "core_fill":
        routes = [((x, y, 1 - c), src_ref.at[k, c], src_ref.at[k, c], src_ref.at[k, 1 - c]) for k in range(n)]
    elif kind == "core_swap":
        routes = [((x, y, 1 - c), src_ref.at[1 - c], land_ref, land_ref)]
    else:
        mine = 2 * x + y
        gather = kind != "chip_xchg"
        slot = (lambda k: land_ref.at[k, c]) if kind == "chip_gather_wide" else (lambda k: land_ref.at[k])
        routes = [((px, py, c), src_ref if gather else src_ref.at[2 * px + py], slot(mine), slot(2 * px + py))
                  for px, py in [(1 - x, y), (x, 1 - y), (1 - x, 1 - y)]]
    sends, recvs = [], []
    for j, (peer, piece, there, here) in enumerate(routes):
        sends.append(pltpu.make_async_remote_copy(src_ref=piece, dst_ref=there, send_sem=sems[j],
                                                  recv_sem=sems[n + j], device_id=peer, device_id_type=MESH))
        recvs.append(pltpu.make_async_remote_copy(src_ref=piece, dst_ref=here, send_sem=sems[j],
                                                  recv_sem=sems[n + j], device_id=peer, device_id_type=MESH))
    return sends, recvs


def _split_start(src, kind, name, after=None):
    land = _split_land(src, kind)
    ns = 2 * _SPLIT_PEERS[kind]
    n_in = 2 if after is None else 3

    def body(*refs):
        src_ref, land_ref = refs[:2]
        outs = refs[n_in:]
        for cp in _split_copies(src_ref, land_ref, outs[:ns], kind)[0]:
            cp.start()
        token = outs[ns + 2]
        token[...] = jnp.zeros_like(token)

    res = pl.pallas_call(
        body,
        name=name,
        out_shape=(pltpu.SemaphoreType.DMA(()),) * ns
        + (pltpu.HBM(src.shape, src.dtype), pltpu.HBM(land.shape, land.dtype), SDS((SUBLANE, LANE), f32)),
        in_specs=(_HBM, _HBM) + (() if after is None else (_ANY,)),
        out_specs=(_SEM,) * ns + (_HBM, _HBM, pl.BlockSpec(memory_space=pltpu.VMEM)),
        input_output_aliases={0: ns, 1: ns + 1},
        compiler_params=pltpu.CompilerParams(has_side_effects=_EFFECT),
    )(pltpu.with_memory_space_constraint(src, pltpu.HBM),
      pltpu.with_memory_space_constraint(lax.empty(land.shape, land.dtype), pltpu.HBM),
      *(() if after is None else (after,)))
    return (res[:ns], res[ns], res[ns + 1]), res[ns + 2]


def _split_wait(state, after, kind, name):
    sems, src_thru, land_thru = state
    ns = 2 * _SPLIT_PEERS[kind]

    def body(src_ref, land_ref, *rest):
        sends, recvs = _split_copies(src_ref, land_ref, rest[:ns], kind)
        for cp in recvs:
            cp.wait_recv()
        for cp in sends:
            cp.wait_send()

    src_out, got = pl.pallas_call(
        body,
        name=name,
        out_shape=(pltpu.HBM(src_thru.shape, src_thru.dtype), pltpu.HBM(land_thru.shape, land_thru.dtype)),
        in_specs=(_HBM, _HBM) + (_SEM,) * ns + (_ANY,),
        out_specs=(_HBM, _HBM),
        input_output_aliases={0: 0, 1: 1},
        compiler_params=pltpu.CompilerParams(has_side_effects=_EFFECT),
    )(src_thru, land_thru, *sems, after)
    if kind == "core_swap":
        return got, src_out
    if kind == "core_fill":
        return src_out
    if kind == "chip_gather_wide":
        mine = 2 * lax.axis_index("x") + lax.axis_index("y")
        zero = jnp.zeros((), mine.dtype)
        return lax.dynamic_update_slice(got, src_out[None, None], (mine, lax.axis_index("c").astype(mine.dtype))
                                        + (zero,) * src_out.ndim)
    return _fill_own(got, src_out, kind == "chip_gather")


def _core_fill(both, name):
    n = both.shape[0]

    def body(in_ref, out_ref, send_sems, recv_sems):
        x, y, c = lax.axis_index("x"), lax.axis_index("y"), lax.axis_index("c")
        sends = [pltpu.make_async_remote_copy(src_ref=out_ref.at[k, c], dst_ref=out_ref.at[k, c],
                                              send_sem=send_sems.at[k], recv_sem=recv_sems.at[k],
                                              device_id=(x, y, 1 - c), device_id_type=MESH) for k in range(n)]
        recvs = [pltpu.make_async_remote_copy(src_ref=out_ref.at[k, c], dst_ref=out_ref.at[k, 1 - c],
                                              send_sem=send_sems.at[k], recv_sem=recv_sems.at[k],
                                              device_id=(x, y, 1 - c), device_id_type=MESH) for k in range(n)]
        for cp in sends:
            cp.start()
        for cp in recvs:
            cp.wait_recv()
        for cp in sends:
            cp.wait_send()

    return pl.pallas_call(
        body,
        in_specs=[_ANY],
        out_specs=_ANY,
        out_shape=SDS(both.shape, both.dtype),
        scratch_shapes=[pltpu.SemaphoreType.DMA((n,)), pltpu.SemaphoreType.DMA((n,))],
        input_output_aliases={0: 0},
        name=name,
    )(both)


def _core_gather(src, name):
    def body(src_ref, out_ref, send_sem, recv_sem):
        x, y, c = lax.axis_index("x"), lax.axis_index("y"), lax.axis_index("c")
        cp = pltpu.make_async_remote_copy(src_ref=src_ref, dst_ref=out_ref.at[c], send_sem=send_sem,
                                          recv_sem=recv_sem, device_id=(x, y, 1 - c), device_id_type=MESH)
        cp.start()
        pltpu.make_async_remote_copy(src_ref=src_ref, dst_ref=out_ref.at[1 - c], send_sem=send_sem,
                                     recv_sem=recv_sem, device_id=(x, y, 1 - c), device_id_type=MESH).wait_recv()
        cp.wait_send()

    out = pl.pallas_call(
        body,
        in_specs=[_ANY],
        out_specs=_ANY,
        out_shape=SDS((2,) + tuple(src.shape), src.dtype),
        scratch_shapes=[pltpu.SemaphoreType.DMA, pltpu.SemaphoreType.DMA],
        name=name,
    )(src)
    return lax.dynamic_update_index_in_dim(out, src, lax.axis_index("c"), axis=0)


_PACK_A = (("w_in", (1088, 1024)),)
_PACK_B = (("w_ba", (512, 128)), ("w_bh", (512, 128)), ("w_out", (128, 1024)), ("w_up", (704, 1024)),
           ("w_down", (352, 1024)))
_PACK_SIZES = _PACK_A + _PACK_B
_TRANSPOSED = ("w_in", "w_up")


def _slab_rows(sizes):
    return sum(r * c for _, (r, c) in sizes) // D_MODEL


def _pack_lo(key):
    keys = [k for k, _ in _PACK_B]
    return _slab_rows(_PACK_B[:keys.index(key)])


def _pack_rows(d, sizes):
    n = d[sizes[0][0]].shape[0]
    return jnp.concatenate([d[k].reshape(n, -1, D_MODEL) for k, _ in sizes], axis=1)


def _unpack_rows(slab, sizes):
    n = slab.shape[0]
    out, lo = {}, 0
    for key, (r, c) in sizes:
        rows = r * c // D_MODEL
        out[key] = slab[:, lo:lo + rows].reshape(n, r, c)
        lo += rows
    return out


def _by_core(gslab):
    return jnp.swapaxes(gslab.reshape((4, 2) + gslab.shape[1:]), 0, 1)


def _cols_to_full(t):
    return jnp.swapaxes(t, 0, 1).reshape(t.shape[1], -1)


def _full_to_cols(t):
    K = t.shape[0]
    return jnp.swapaxes(t.reshape(K, 8, -1), 0, 1)


_SMALL = (("pre_mix_norm", (1, 1024)), ("rel_bias", (32, 24)), ("hgrn_lb_raw", (2, 512)), ("hgrn_norm", (1, 128)),
          ("post_mix_norm", (1, 1024)), ("pre_ffn_norm", (1, 1024)), ("conv_b", (1, 5632)),
          ("post_ffn_norm", (1, 1024)))
_SMALL_ROWS = 96
_CONVW_ROWS = 136


_SMALL_USED = sum(r * c for _, (r, c) in _SMALL)


def _pack_small(d, extra=None):
    flat = jnp.concatenate([d[k].reshape(-1) for k, _ in _SMALL] + ([] if extra is None else [extra.reshape(-1)]))
    flat = jnp.pad(flat, (0, _SMALL_ROWS * LANE - flat.shape[0]))
    return flat.reshape(_SMALL_ROWS, LANE)


def _unpack_small(p):
    flat = p.reshape(-1)
    out, lo = {}, 0
    for k, shp in _SMALL:
        n = shp[0] * shp[1]
        out[k] = flat[lo:lo + n].reshape(shp)
        lo += n
    return out


def _local_step(x, tgt, P, plan):
    S = x.shape[0]
    P = dict(P)
    lb = _lb_fwd(P["hgrn_lb_raw"])
    hs = _prep(x, P["pre_mix_norm"], plan.start_token())
    h1 = hs[0]
    consts = [_bias_consts(d, plan.start_token()) for d in DILATIONS]
    biases, dep = [], h1
    for g in range(N_GROUPS):
        tab_t = P["rel_bias"][:, 8 * g:8 * g + 8].T
        dep = _bias_build(tab_t, consts[g][0], consts[g][1], f"bias_build{g}", dep)
        biases.append(dep.reshape(8, ATTN_BLOCK, 2 * ATTN_BLOCK))
    W = dict(plan.weights_a(dep))
    qkv0, hg, gc = _mm_fanout(h1, [W["wt_qkv"][0], W["wt_hg"], W["wt_gate"]], "nt", [bf16, f32, bf16], "proj_natural")
    qkv = [qkv0] + [_mm(hs[g], W["wt_qkv"][g], "nt", bf16, f"proj_qkv{g}") for g in (1, 2)]
    obuf, lbuf, token = [], [], None
    for g, d in enumerate(DILATIONS):
        o_g, l_g = _attn_fwd(qkv[g], biases[g], (S // d) // ATTN_BLOCK, f"attn_fwd{g}", after=token)
        lbuf.append(l_g)
        obuf.append(o_g)
        if g == 0:
            token = plan.forward_b(o_g)
    y_attn, y_attn_b, w0, w1, w2 = _attn_merge(obuf[0], obuf[1], obuf[2], lbuf[0], lbuf[1], lbuf[2])
    y_hgrn, o_raw, ck = _hgrn_fwd(hg, lb, P["hgrn_norm"])
    wb = plan.weights_b(y_hgrn)
    P["conv_w"] = wb.pop("conv_w")
    W.update(wb)
    a, b, merged = _gate_fwd(y_attn_b, y_hgrn, W["w_ba"], W["w_bh"], gc)
    mo, x1, h2 = _mid_fwd(x, merged, W["w_out"], P["post_mix_norm"], P["pre_ffn_norm"])
    ug, uv = _mm_fanout(h2, [W["wt_up_g"], W["wt_up_v"]], "nt", [bf16, bf16], "up_proj")
    cw_g, cw_v = P["conv_w"][:, :D_FF], P["conv_w"][:, D_FF:]
    cb_g, cb_v = P["conv_b"][:, :D_FF], P["conv_b"][:, D_FF:]
    act = _conv_fwd(ug, uv, cw_g, cw_v, cb_g, cb_v)
    loss, dy, dfo, g_post_ffn = _final(x1, act, W["w_down"], tgt, P["post_ffn_norm"])
    gslab = lax.empty((8, _slab_rows(_PACK_B), D_MODEL), bf16)
    rows_b = {k: r for k, (r, _) in _PACK_B}
    gslab = _mm(act, dfo, "tn", bf16, "gw_down", into=(gslab, 0, _pack_lo("w_down"), rows_b["w_down"]))
    dact = _mm(dfo, W["w_down"], "nt", bf16, "d_act")
    dug, duv, st_g, st_v = _conv_bwd(ug, uv, dact, cw_g, cw_v, cb_g, cb_v)
    gslab = _mm(dug, h2, "tn", bf16, "gw_up_gate", into=(gslab, 0, _pack_lo("w_up"), rows_b["w_up"]))
    gslab = _mm(duv, h2, "tn", bf16, "gw_up_val", into=(gslab, 4, _pack_lo("w_up"), rows_b["w_up"]))
    dx1, dmo, g_pre_ffn, g_post_mix = _mid_bwd(dy, dug, duv, W["wt_up_g"], W["wt_up_v"], x1, mo, P["pre_ffn_norm"],
                                               P["post_mix_norm"])
    gslab = _mm(merged, dmo, "tn", bf16, "gw_out", into=(gslab, 0, _pack_lo("w_out"), rows_b["w_out"]))
    da, db, dgc, dyattn, dyhgrn = _gate_bwd(dmo, W["w_out"], a, b, gc, W["w_ba"], W["w_bh"])
    gW_ba = _mm(y_attn_b, da, "tn", bf16, "gw_ba")
    gW_bh = _mm(y_hgrn, db, "tn", bf16, "gw_bh")
    big_b = dict(w_ba=gW_ba, w_bh=gW_bh, slab=gslab)
    dos = _attn_merge_bwd(dyattn, y_attn, w0, w1, w2, after=plan.grads_b_start(big_b))
    dq_h, df_h, dv_h, dog_h, glb8, gnw8 = _hgrn_bwd(hg, o_raw, dyhgrn, ck, lb, P["hgrn_norm"],
                                                   after=plan.grads_b_exchange(dos[5]))
    dhg = [dq_h, df_h, dv_h, dog_h]
    g_lb_raw = _lb_bwd(P["hgrn_lb_raw"], glb8[0:1])
    gn = gnw8[0:1]
    g_hgrn_norm = (gn[:, 0:128] + gn[:, 128:256]) + (gn[:, 256:384] + gn[:, 384:512])
    dqkvs, gW_qkv, g_rel = [], [], []
    for g, d in enumerate(DILATIONS):
        dq, dk, dv, dbias = _attn_bwd(qkv[g], biases[g], dos[g], dos[3 + g], lbuf[g], (S // d) // ATTN_BLOCK,
                                      f"attn_bwd{g}")
        dqkvs.append([dq, dk, dv])
        gW_qkv.append(_mm(dqkvs[g], hs[g], "tn", bf16, f"gw_qkv{g}"))
        g_rel.append(_bias_grad(dbias.reshape(8, -1), consts[g][0], f"bias_grad{g}"))
    gW_hg = _mm(dhg, h1, "tn", bf16, "gw_hg")
    gW_gate = _mm(dgc, h1, "tn", bf16, "gw_gate")
    gW_in = gW_qkv + [gW_hg, gW_gate]
    token = plan.grads_a_start(gW_in)
    dh_perm = [_mm(dqkvs[g], W["wt_qkv"][g], "nn", f32, f"dh1_qkv{g}", after=token) for g in (1, 2)]
    token = plan.grads_a_exchange(dh_perm[1])
    dh_main = _mm(dqkvs[0] + dhg + [dgc], [W["wt_qkv"][0], W["wt_hg"], W["wt_gate"]], "nn", f32, "dh1_main",
                  after=token)
    grad_x, g_pre_mix = _first_bwd(x, dx1, dh_main, dh_perm[0], dh_perm[1], P["pre_mix_norm"])

    g_conv_w = jnp.concatenate([st_g[0:3], st_v[0:3]], axis=1)
    g_conv_b = jnp.concatenate([st_g[3:4], st_v[3:4]], axis=1)
    small = dict(pre_mix_norm=g_pre_mix, rel_bias=jnp.concatenate(g_rel, axis=1), hgrn_lb_raw=g_lb_raw,
                 hgrn_norm=g_hgrn_norm, post_mix_norm=g_post_mix, pre_ffn_norm=g_pre_ffn, conv_b=g_conv_b,
                 post_ffn_norm=g_post_ffn, conv_w=g_conv_w)
    return loss, grad_x, gW_in, big_b, small


def _weights_a(both):
    wt = both.reshape(-1, D_MODEL)
    return dict(
        wt_qkv=[_Rows(wt, g * QKV_G, QKV_G) for g in range(N_GROUPS)],
        wt_hg=_Rows(wt, 3 * QKV_G, 4 * HGRN_W),
        wt_gate=_Rows(wt, 3 * QKV_G + 4 * HGRN_W, wt.shape[0] - 3 * QKV_G - 4 * HGRN_W),
    )


def _weights_b(slabs):
    sh = _unpack_rows(slabs, _PACK_B)
    wt_up = sh["w_up"].reshape(-1, D_MODEL)
    return dict(
        w_ba=_cols_to_full(sh["w_ba"]),
        w_bh=_cols_to_full(sh["w_bh"]),
        w_out=sh["w_out"].reshape(D_MODEL, D_MODEL),
        wt_up_g=wt_up[:D_FF],
        wt_up_v=wt_up[D_FF:],
        w_down=sh["w_down"].reshape(D_FF, D_MODEL),
    )


def _dest_rows(sections, height):
    out = []
    for j in range(8):
        lo, hi, off, pieces = j * height, (j + 1) * height, 0, []
        for s in sections:
            a, b = max(lo, off), min(hi, off + s.shape[0])
            if a < b:
                pieces.append(s[a - off:b - off])
            off += s.shape[0]
        out.append(pieces[0] if len(pieces) == 1 else jnp.concatenate(pieces, axis=0))
    return out


def _grad_blocks_a(sections):
    rows = _dest_rows(sections, 1088)
    return jnp.stack([jnp.stack([rows[2 * k + c].astype(bf16) for k in range(4)]) for c in range(2)])


def _grad_slab_b(g):
    shards = dict(w_ba=_full_to_cols(g["w_ba"]), w_bh=_full_to_cols(g["w_bh"]))
    head = _pack_rows({k: v.astype(bf16) for k, v in shards.items()}, _PACK_B[:2])
    assert head.shape[1] == _pack_lo("w_out")
    return lax.dynamic_update_slice(g["slab"], head, (0, 0, 0))


_CONVW_SLAB_ROWS = 16


class _Traffic:
    def __init__(self, slab_a, slab_b, conv_w):
        hi = conv_w.astype(bf16)
        r1 = conv_w - hi.astype(f32)
        mid = r1.astype(bf16)
        lo = (r1 - mid.astype(f32)).astype(bf16)
        bits = jnp.stack([hi, mid, lo]).reshape(-1)
        tail = jnp.pad(bits, (0, _CONVW_SLAB_ROWS * D_MODEL - bits.shape[0])).reshape(_CONVW_SLAB_ROWS, D_MODEL)
        self.slab_b = jnp.concatenate([slab_b, tail], axis=0)
        self.state_a, tok = _split_start(slab_a, "chip_gather_wide", "ag_a_start")
        self.state_b, self.token = _split_start(self.slab_b, "chip_gather_wide", "ag_b_start", after=tok)
        self.state = None
        self.state_gb = None

    def start_token(self):
        return self.token

    def weights_a(self, after):
        half = _split_wait(self.state_a, after, "chip_gather_wide", "ag_a_wait")
        return _weights_a(_core_fill(half, "ag_a_cores"))

    def forward_b(self, after):
        half = _split_wait(self.state_b, after, "chip_gather_wide", "ag_b_wait")
        self.state, token = _split_start(half, "core_fill", "ag_b_cores_start")
        return token

    def weights_b(self, after):
        both = _split_wait(self.state, after, "core_fill", "ag_b_cores_wait")
        slabs = both.reshape((8,) + tuple(self.slab_b.shape))
        rows = _slab_rows(_PACK_B)
        out = _weights_b(slabs[:, :rows])
        pieces = slabs[:, rows:].reshape(8, -1)[:, :3 * 3 * 704].reshape(8, 3, 3, 704).astype(f32)
        out["conv_w"] = _cols_to_full((pieces[:, 0] + pieces[:, 1]) + pieces[:, 2])
        return out

    def grads_b_start(self, grads):
        self.state, token = _split_start(_by_core(_grad_slab_b(grads)), "core_swap", "rs_b_cores_start")
        return token

    def grads_b_exchange(self, after):
        from_sib, by_core = _split_wait(self.state, after, "core_swap", "rs_b_cores_wait")
        self.state_gb, token = _split_start(_pair_add(by_core, from_sib, "rs_b_pair_add"), "chip_xchg", "rs_b_start")
        return token

    def grads_a_start(self, sections):
        self.state, token = _split_start(_grad_blocks_a(sections), "core_swap", "rs_a_cores_start")
        return token

    def grads_a_exchange(self, after):
        from_sib, by_core = _split_wait(self.state, after, "core_swap", "rs_a_cores_wait")
        self.state, token = _split_start(_pair_add(by_core, from_sib, "rs_a_pair_add"), "chip_xchg", "rs_a_start")
        return token

    def parts(self, after):
        slab = _split_wait(self.state_gb, after, "chip_xchg", "rs_b_wait")
        parts, lo = _unpack_rows(slab, _PACK_B), 0
        for key, (r, c) in _PACK_B:
            if c == D_MODEL:
                parts[key] = _Rows(slab, lo, r)
            lo += r * c // D_MODEL
        parts["w_in"] =_split_wait(self.state, after, "chip_xchg", "rs_a_wait")
        return parts


def kernel(x, pre_mix_norm, w_in, rel_bias, hgrn_lb_raw, hgrn_norm, w_branch_attn, w_branch_hgrn, w_out, post_mix_norm, pre_ffn_norm, w_up, conv_w, conv_b, w_down, post_ffn_norm, loss_target, m_pre_mix_norm, m_w_in, m_rel_bias, m_hgrn_lb_raw, m_hgrn_norm, m_w_branch_attn, m_w_branch_hgrn, m_w_out, m_post_mix_norm, m_pre_ffn_norm, m_w_up, m_conv_w, m_conv_b, m_w_down, m_post_ffn_norm, v_pre_mix_norm, v_w_in, v_rel_bias, v_hgrn_lb_raw, v_hgrn_norm, v_w_branch_attn, v_w_branch_hgrn, v_w_out, v_post_mix_norm, v_pre_ffn_norm, v_w_up, v_conv_w, v_conv_b, v_w_down, v_post_ffn_norm):
    ci = lax.axis_index("c")
    dev = 4 * lax.axis_index("x") + 2 * lax.axis_index("y") + ci
    tr = lambda t: jnp.swapaxes(t[0], 0, 1)
    wts = dict(w_in=tr(w_in), w_ba=w_branch_attn[0], w_bh=w_branch_hgrn[0], w_out=w_out[0], w_up=tr(w_up),
               w_down=w_down[0])
    mom = dict(w_in=tr(m_w_in), w_ba=m_w_branch_attn[0], w_bh=m_w_branch_hgrn[0], w_out=m_w_out[0], w_up=tr(m_w_up),
               w_down=m_w_down[0])
    var = dict(w_in=tr(v_w_in), w_ba=v_w_branch_attn[0], w_bh=v_w_branch_hgrn[0], w_out=v_w_out[0], w_up=tr(v_w_up),
               w_down=v_w_down[0])
    small_w = dict(pre_mix_norm=pre_mix_norm, rel_bias=rel_bias, hgrn_lb_raw=hgrn_lb_raw, hgrn_norm=hgrn_norm,
                   post_mix_norm=post_mix_norm, pre_ffn_norm=pre_ffn_norm, conv_b=conv_b, post_ffn_norm=post_ffn_norm)
    small_m = dict(pre_mix_norm=m_pre_mix_norm, rel_bias=m_rel_bias, hgrn_lb_raw=m_hgrn_lb_raw, hgrn_norm=m_hgrn_norm,
                   post_mix_norm=m_post_mix_norm, pre_ffn_norm=m_pre_ffn_norm, conv_b=m_conv_b,
                   post_ffn_norm=m_post_ffn_norm)
    small_v = dict(pre_mix_norm=v_pre_mix_norm, rel_bias=v_rel_bias, hgrn_lb_raw=v_hgrn_lb_raw, hgrn_norm=v_hgrn_norm,
                   post_mix_norm=v_post_mix_norm, pre_ffn_norm=v_pre_ffn_norm, conv_b=v_conv_b,
                   post_ffn_norm=v_post_ffn_norm)

    plan = _Traffic(wts["w_in"].astype(bf16),
                    _pack_rows({k: wts[k].astype(bf16)[None] for k, _ in _PACK_B}, _PACK_B)[0], conv_w[0])

    loss8, grad_x, _, _, small = _local_step(x[0], loss_target[0], small_w, plan)
    spack = jnp.concatenate([_pack_small(small, loss8[0, 0:1]),
                             jnp.pad(small["conv_w"].reshape(-1, LANE), ((0, _CONVW_ROWS - 132), (0, 0)))], axis=0)
    small_state, token = _split_start(spack, "chip_gather", "ag_small_start")

    parts = plan.parts(token)
    outs_big = {}
    for k, _ in _PACK_SIZES:
        outs_big[k] = _adamw(wts[k], mom[k], var[k], parts[k], "adamw_" + k)

    by_chip = _split_wait(small_state, outs_big["w_in"][1], "chip_gather", "ag_small_wait")
    allp = _core_gather(by_chip, "ag_small_cores")
    ssum = _sum8(allp, "small_sum")
    gs = ssum[:_SMALL_ROWS]
    loss = ssum[_SMALL_USED // LANE, _SMALL_USED % LANE]
    res_small = _adamw(_pack_small(small_w), _pack_small(small_m), _pack_small(small_v), gs, "adamw_small")
    sm = [_unpack_small(t) for t in res_small]
    g_cw_full = ssum[_SMALL_ROWS:_SMALL_ROWS + 132].reshape(3, 2 * D_FF)
    g_cw = lax.dynamic_slice_in_dim(g_cw_full, dev * 704, 704, axis=1)
    res_cw = _adamw(conv_w[0], m_conv_w[0], v_conv_w[0], g_cw, "adamw_conv_w")

    def pick(i):
        def big_(k):
            t = outs_big[k][i]
            return (jnp.swapaxes(t, 0, 1) if k in _TRANSPOSED else t)[None]
        return [sm[i]["pre_mix_norm"], big_("w_in"), sm[i]["rel_bias"], sm[i]["hgrn_lb_raw"], sm[i]["hgrn_norm"],
                big_("w_ba"), big_("w_bh"), big_("w_out"), sm[i]["post_mix_norm"], sm[i]["pre_ffn_norm"],
                big_("w_up"), res_cw[i][None], sm[i]["conv_b"], big_("w_down"), sm[i]["post_ffn_norm"]]

    return (loss, grad_x[None], *pick(0), *pick(1), *pick(2), *pick(3))
```

```python
import functools
import math

import jax
import jax.numpy as jnp
from jax import lax
from jax.experimental import pallas as pl
from jax.experimental.pallas import tpu as pltpu

f32 = jnp.float32
bf16 = jnp.bfloat16
SDS = jax.ShapeDtypeStruct
HIGHEST = lax.Precision.HIGHEST
MESH = pl.DeviceIdType.MESH

NN = (((1,), (0,)), ((), ()))
NT = (((1,), (1,)), ((), ()))
TN = (((0,), (0,)), ((), ()))

D_MODEL = 1024
N_GROUPS = 3
DILATIONS = (1, 4, 16)
HEAD_DIM = 64
ATTN_BLOCK = 128
QKV_G = 1536
ATTN_OUT = 512
HGRN_W = 512
HGRN_CHUNK = 32
D_FF = 2816
NUM_BUCKETS = 32
MAX_EXACT = 16
MAX_DISTANCE = 2048
NEG_INF = -1e30
EPS = 1e-6
LANE = 128
SUBLANE = 8
VMEM_BIG = 48 * 1024 * 1024
MM_ROWS = 512
MM_OUT_BYTES = 8 * 1024 * 1024
ADAM_BLOCK_BYTES = 2304 * 1024

ADAM_LR, ADAM_B1, ADAM_B2, ADAM_EPS, ADAM_WD, ADAM_STEP = 0.001, 0.9, 0.999, 1e-08, 0.01, 10


def _pick(n, pref):
    t = pref
    while t >= LANE:
        if n % t == 0:
            return t
        t //= 2
    return n


def _cparams(sem=None, vmem=None):
    kw = {}
    if sem is not None:
        kw["dimension_semantics"] = sem
    if vmem is not None:
        kw["vmem_limit_bytes"] = vmem
    return pltpu.CompilerParams(**kw)


def _sigmoid(x):
    return jax.nn.sigmoid(x)


def _colsum8(x):
    return x.reshape(x.shape[0] // SUBLANE, SUBLANE, x.shape[1]).sum(axis=0)


class _Rows:
    def __init__(self, full, lo, rows):
        self.full, self.lo, self.shape = full, lo, tuple(full.shape[:-2]) + (rows, full.shape[-1])


def _resident(t):
    if isinstance(t, _Rows):
        return pl.BlockSpec((pl.Element(t.shape[0]), pl.Element(t.shape[1])), lambda i: (t.lo, 0)), t.full
    return pl.BlockSpec(t.shape, lambda i: (0, 0)), t


def _mm(a, b, mode, out_dtype, name, acc=None, after=None, into=None):
    dims = {"nn": NN, "nt": NT, "tn": TN}[mode]
    has_acc = acc is not None
    parts = list(a) if isinstance(a, (list, tuple)) else [a]
    if mode == "tn":
        assert not has_acc
        K, N = b.shape
        widths = [t.shape[1] for t in parts]
        M = sum(widths)
        whole = M * N * 4 <= MM_OUT_BYTES
        assert whole or len(parts) == 1
        tmm = M if whole else M // 2
        ts = _pick(K, 4 * MM_ROWS)
        nk = K // ts

        npart = len(parts)
        narrow = out_dtype != f32
        n_in = npart + (1 if into is None else 2)
        out_spec, out_shape, aliases, extra = pl.BlockSpec((tmm, N), lambda i, k: (i, 0)), SDS((M, N), out_dtype), {}, []
        if into is not None:
            slab, first, lo, shard_rows = into
            assert narrow and slab.dtype == out_dtype and tmm % shard_rows == 0 and slab.shape[2] == N
            per = tmm // shard_rows
            out_spec = pl.BlockSpec((pl.Element(per), pl.Element(shard_rows), pl.Element(N)),
                                    lambda i, k: (first + i * per, lo, 0))
            out_shape, aliases, extra = SDS(slab.shape, out_dtype), {npart + 1: 0}, [slab]

        def body_tn(*refs):
            b_ref, o_ref = refs[npart], refs[n_in]
            acc_ref = refs[n_in + 1] if narrow else o_ref
            k = pl.program_id(1)
            bv = b_ref[...]
            lo = 0
            for a_ref, w in zip(refs[:npart], widths if whole else [tmm]):
                part = lax.dot_general(a_ref[...], bv, dims, preferred_element_type=f32)
                rows = slice(lo, lo + w)
                lo += w

                @pl.when(k == 0)
                def _(part=part, rows=rows):
                    acc_ref[rows, :] = part

                @pl.when(k > 0)
                def _(part=part, rows=rows):
                    acc_ref[rows, :] += part

            if narrow:
                @pl.when(k == nk - 1)
                def _():
                    o_ref[...] = acc_ref[...].astype(out_dtype).reshape(o_ref.shape)

        return pl.pallas_call(
            body_tn,
            grid=(M // tmm, nk),
            in_specs=[pl.BlockSpec((ts, w if whole else tmm), lambda i, k: (k, i)) for w in widths]
            + [pl.BlockSpec((ts, N), lambda i, k: (k, 0))] + [pl.BlockSpec(memory_space=pl.ANY)] * len(extra),
            out_specs=out_spec,
            out_shape=out_shape,
            input_output_aliases=aliases,
            scratch_shapes=[pltpu.VMEM((tmm, N), f32)] if narrow else [],
            compiler_params=_cparams(("parallel", "arbitrary"), VMEM_BIG),
            name=name,
        )(*parts, b, *extra)

    bs = list(b) if isinstance(b, (list, tuple)) else [b]
    widths = [t.shape[1] for t in parts]
    M = parts[0].shape[0]
    kdim = 0 if mode == "nn" else 1
    N = bs[0].shape[1 - kdim]
    tm = _pick(M, MM_ROWS)
    npart, nb = len(parts), len(bs)
    place, bi, lo = [], 0, 0
    for w in widths:
        place.append((bi, lo))
        lo += w
        if lo == bs[bi].shape[kdim]:
            bi, lo = bi + 1, 0
    assert bi == nb and lo == 0

    def body(*refs):
        a_refs, b_refs = refs[:npart], refs[npart:npart + nb]
        c_ref = refs[npart + nb] if has_acc else None
        o_ref = refs[-1]
        part = None
        for a_ref, w, (bi, lo) in zip(a_refs, widths, place):
            b_ref = b_refs[bi]
            if w == bs[bi].shape[kdim]:
                bk = b_ref[...]
            else:
                bk = b_ref[:, lo:lo + w] if mode == "nt" else b_ref[lo:lo + w, :]
            t = lax.dot_general(a_ref[...], bk, dims, preferred_element_type=f32)
            part = t if part is None else part + t
        if has_acc:
            part = part + c_ref[...]
        o_ref[...] = part.astype(out_dtype)

    specs = [pl.BlockSpec((tm, w), lambda i: (i, 0)) for w in widths] + [_resident(t)[0] for t in bs]
    args = parts + [_resident(t)[1] for t in bs]
    aliases = {}
    if has_acc:
        specs.append(pl.BlockSpec((tm, N), lambda i: (i, 0)))
        args.append(acc)
        aliases = {npart + nb: 0}
    if after is not None:
        specs.append(pl.BlockSpec(memory_space=pl.ANY))
        args.append(after)
    return pl.pallas_call(
        body,
        grid=(M // tm,),
        in_specs=specs,
        out_specs=pl.BlockSpec((tm, N), lambda i: (i, 0)),
        out_shape=SDS((M, N), out_dtype),
        input_output_aliases=aliases,
        compiler_params=_cparams(("parallel",), VMEM_BIG),
        name=name,
    )(*args)


def _mm_fanout(a, bs, mode, out_dtypes, name):
    dims = {"nn": NN, "nt": NT}[mode]
    M, K = a.shape
    ns = [b.shape[1] if mode == "nn" else b.shape[0] for b in bs]
    tm = _pick(M, MM_ROWS)
    nb = len(bs)

    def body(a_ref, *refs):
        av = a_ref[...]
        for b_ref, o_ref, dt in zip(refs[:nb], refs[nb:], out_dtypes):
            o_ref[...] = lax.dot_general(av, b_ref[...], dims, preferred_element_type=f32).astype(dt)

    return pl.pallas_call(
        body,
        grid=(M // tm,),
        in_specs=[pl.BlockSpec((tm, K), lambda i: (i, 0))] + [_resident(b)[0] for b in bs],
        out_specs=[pl.BlockSpec((tm, n), lambda i: (i, 0)) for n in ns],
        out_shape=[SDS((M, n), dt) for n, dt in zip(ns, out_dtypes)],
        compiler_params=_cparams(("parallel",), VMEM_BIG),
        name=name,
    )(a, *[_resident(b)[1] for b in bs])


PERM_ROWS = 2048


def _perm_spec(d, cols=LANE):
    return pl.BlockSpec((d, PERM_ROWS // d, cols), lambda i, j: (0, i, j))


def _to_natural(src_ref, dst_ref, d):
    n = src_ref.shape[1]
    for r in range(d):
        dst_ref[pl.ds(r, n, stride=d), :] = src_ref[r]


def _prep(x, w, after=None):
    S, D = x.shape
    R = PERM_ROWS
    nc = D // LANE
    n_in = nc + 1 + (after is not None)

    def body(*refs):
        x_refs, w_ref = refs[:nc], refs[nc]
        h_ref, h4_ref, h16_ref, rs = refs[n_in:]
        ssq = None
        for xr in x_refs:
            v = xr[...]
            t = jnp.sum(v * v, axis=-1, keepdims=True)
            ssq = t if ssq is None else ssq + t
        rinv = lax.rsqrt(ssq * (1.0 / D) + EPS)
        rs[...] = jnp.broadcast_to(rinv, (R, LANE))
        for j, xr in enumerate(x_refs):
            cols = slice(j * LANE, (j + 1) * LANE)
            wj = w_ref[:, cols]
            h_ref[:, cols] = ((xr[...] * rinv) * wj).astype(bf16)
            for d, o_ref in ((4, h4_ref), (16, h16_ref)):
                n = R // d
                for r in range(d):
                    rows = pl.ds(r, n, stride=d)
                    o_ref[r, :, cols] = ((xr[rows, :] * rs[rows, :]) * wj).astype(bf16)

    col = lambda j: pl.BlockSpec((R, LANE), lambda i, j=j: (i, j))
    h, h4, h16 = pl.pallas_call(
        body,
        grid=(S // R,),
        in_specs=[col(j) for j in range(nc)] + [pl.BlockSpec((1, D), lambda i: (0, 0))]
        + ([] if after is None else [pl.BlockSpec(memory_space=pl.ANY)]),
        out_specs=[pl.BlockSpec((R, D), lambda i: (i, 0)), pl.BlockSpec((4, R // 4, D), lambda i: (0, i, 0)),
                   pl.BlockSpec((16, R // 16, D), lambda i: (0, i, 0))],
        out_shape=[SDS((S, D), bf16), SDS((4, S // 4, D), bf16), SDS((16, S // 16, D), bf16)],
        scratch_shapes=[pltpu.VMEM((R, LANE), f32)],
        compiler_params=_cparams(("parallel",), VMEM_BIG),
        name="prep_norm_perm",
    )(*([x] * nc), w, *([] if after is None else [after]))
    return [h, h4.reshape(S, D), h16.reshape(S, D)]


def _rms_parts(xv):
    r = lax.rsqrt(jnp.mean(xv * xv, axis=-1, keepdims=True) + EPS)
    return r, xv * r


def _rms_bwd(xhat, r, w, dy):
    dyw = dy * w
    return r * (dyw - xhat * jnp.mean(dyw * xhat, axis=-1, keepdims=True))


def _mid_fwd(x, merged, w_out, w_pm, w_pf):
    S, D = x.shape
    tm = _pick(S, MM_ROWS)

    def body(x_ref, m_ref, wo_ref, wpm_ref, wpf_ref, mo_ref, x1_ref, h2_ref):
        mo = jnp.dot(m_ref[...], wo_ref[...], preferred_element_type=f32)
        mo_ref[...] = mo
        _, moh = _rms_parts(mo)
        x1 = x_ref[...] + moh * wpm_ref[...]
        x1_ref[...] = x1
        _, x1h = _rms_parts(x1)
        h2_ref[...] = (x1h * wpf_ref[...]).astype(bf16)

    row = pl.BlockSpec((tm, D), lambda i: (i, 0))
    vec = pl.BlockSpec((1, D), lambda i: (0, 0))
    return pl.pallas_call(
        body,
        grid=(S // tm,),
        in_specs=[row, pl.BlockSpec((tm, merged.shape[1]), lambda i: (i, 0)),
                  pl.BlockSpec(w_out.shape, lambda i: (0, 0)), vec, vec],
        out_specs=[row, row, row],
        out_shape=[SDS((S, D), f32), SDS((S, D), f32), SDS((S, D), bf16)],
        compiler_params=_cparams(("parallel",), VMEM_BIG),
        name="out_proj_mid_fwd",
    )(x, merged, w_out, w_pm, w_pf)


def _final(x1, act, w_down, tgt, w_pfn):
    S, D = x1.shape
    tm = _pick(S, MM_ROWS)
    nt = S // tm

    def body(x1_ref, a_ref, wd_ref, t_ref, w_ref, loss_ref, dy_ref, dfo_ref, gw_ref, lacc, gacc):
        i = pl.program_id(0)

        @pl.when(i == 0)
        def _():
            lacc[...] = jnp.zeros_like(lacc)
            gacc[...] = jnp.zeros_like(gacc)

        w = w_ref[...]
        r, foh = _rms_parts(jnp.dot(a_ref[...], wd_ref[...], preferred_element_type=f32))
        y = x1_ref[...] + foh * w
        err = y - t_ref[...]
        lacc[...] += _colsum8(err * err)
        dy = err * (1.0 / D)
        dy_ref[...] = dy
        gacc[...] += _colsum8(dy * foh)
        dfo_ref[...] = _rms_bwd(foh, r, w, dy).astype(bf16)

        @pl.when(i == nt - 1)
        def _():
            loss_ref[...] = jnp.full((SUBLANE, LANE), 0.5 / D, f32) * jnp.sum(lacc[...])
            gw_ref[...] = jnp.sum(gacc[...], axis=0, keepdims=True)

    row = pl.BlockSpec((tm, D), lambda i: (i, 0))
    vec = pl.BlockSpec((1, D), lambda i: (0, 0))
    return pl.pallas_call(
        body,
        grid=(nt,),
        in_specs=[row, pl.BlockSpec((tm, act.shape[1]), lambda i: (i, 0)),
                  pl.BlockSpec(w_down.shape, lambda i: (0, 0)), row, vec],
        out_specs=[pl.BlockSpec((SUBLANE, LANE), lambda i: (0, 0)), row, row, vec],
        out_shape=[SDS((SUBLANE, LANE), f32), SDS((S, D), f32), SDS((S, D), bf16), SDS((1, D), f32)],
        scratch_shapes=[pltpu.VMEM((SUBLANE, D), f32), pltpu.VMEM((SUBLANE, D), f32)],
        compiler_params=_cparams(("arbitrary",), VMEM_BIG),
        name="down_proj_final_loss",
    )(x1, act, w_down, tgt, w_pfn)


MID_BWD_ROWS = 256


def _mid_bwd(dy, dug, duv, wt_g, wt_v, x1, mo, w_pf, w_pm):
    S, D = dy.shape
    tm = _pick(S, MID_BWD_ROWS)
    nt = S // tm

    def body(dy_ref, dug_ref, duv_ref, wg_ref, wv_ref, x1_ref, mo_ref, wpf_ref, wpm_ref,
             dx1_ref, dmo_ref, gpf_ref, gpm_ref, apf, apm):
        i = pl.program_id(0)

        @pl.when(i == 0)
        def _():
            apf[...] = jnp.zeros_like(apf)
            apm[...] = jnp.zeros_like(apm)

        r1, x1h = _rms_parts(x1_ref[...])
        dh2 = jnp.dot(dug_ref[...], wg_ref[...], preferred_element_type=f32) \
            + jnp.dot(duv_ref[...], wv_ref[...], preferred_element_type=f32)
        apf[...] += _colsum8(dh2 * x1h)
        dx1 = dy_ref[...] + _rms_bwd(x1h, r1, wpf_ref[...], dh2)
        dx1_ref[...] = dx1
        rm, moh = _rms_parts(mo_ref[...])
        apm[...] += _colsum8(dx1 * moh)
        dmo_ref[...] = _rms_bwd(moh, rm, wpm_ref[...], dx1).astype(bf16)

        @pl.when(i == nt - 1)
        def _():
            gpf_ref[...] = jnp.sum(apf[...], axis=0, keepdims=True)
            gpm_ref[...] = jnp.sum(apm[...], axis=0, keepdims=True)

    row = pl.BlockSpec((tm, D), lambda i: (i, 0))
    vec = pl.BlockSpec((1, D), lambda i: (0, 0))
    return pl.pallas_call(
        body,
        grid=(nt,),
        in_specs=[row, pl.BlockSpec((tm, dug.shape[1]), lambda i: (i, 0)), pl.BlockSpec((tm, duv.shape[1]), lambda i: (i, 0)),
                  pl.BlockSpec(wt_g.shape, lambda i: (0, 0)), pl.BlockSpec(wt_v.shape, lambda i: (0, 0)),
                  row, row, vec, vec],
        out_specs=[row, row, vec, vec],
        out_shape=[SDS((S, D), f32), SDS((S, D), bf16), SDS((1, D), f32), SDS((1, D), f32)],
        scratch_shapes=[pltpu.VMEM((SUBLANE, D), f32), pltpu.VMEM((SUBLANE, D), f32)],
        compiler_params=_cparams(("arbitrary",), VMEM_BIG),
        name="dh2_mid_bwd",
    )(dy, dug, duv, wt_g, wt_v, x1, mo, w_pf, w_pm)


def _first_bwd(x, dx1, dh_a, dh_b, dh_c, w_pre):
    S, D = x.shape
    tm = _pick(S, 512)
    nt = S // tm
    nc = D // LANE

    def body(*refs):
        x_ref, dx1_ref, a_ref = refs[:3]
        b_refs, c_refs, w_ref = refs[3:3 + nc], refs[3 + nc:3 + 2 * nc], refs[3 + 2 * nc]
        gx_ref, gw_ref, acc, dh_s, sb, sc = refs[4 + 2 * nc:]
        i = pl.program_id(0)

        @pl.when(i == 0)
        def _():
            acc[...] = jnp.zeros_like(acc)

        for j in range(nc):
            cols = slice(j * LANE, (j + 1) * LANE)
            _to_natural(b_refs[j], sb, 4)
            _to_natural(c_refs[j], sc, 16)
            dh_s[:, cols] = (a_ref[:, cols] + sb[...]) + sc[...]
        r, xh = _rms_parts(x_ref[...])
        dh = dh_s[...]
        acc[...] += _colsum8(dh * xh)
        gx_ref[...] = dx1_ref[...] + _rms_bwd(xh, r, w_ref[...], dh)

        @pl.when(i == nt - 1)
        def _():
            gw_ref[...] = jnp.sum(acc[...], axis=0, keepdims=True)

    row = pl.BlockSpec((tm, D), lambda i: (i, 0))
    vec = pl.BlockSpec((1, D), lambda i: (0, 0))
    perm = lambda d: [pl.BlockSpec((d, tm // d, LANE), lambda i, j=j: (0, i, j)) for j in range(nc)]
    return pl.pallas_call(
        body,
        grid=(nt,),
        in_specs=[row, row, row] + perm(4) + perm(16) + [vec],
        out_specs=[row, vec],
        out_shape=[SDS((S, D), f32), SDS((1, D), f32)],
        scratch_shapes=[pltpu.VMEM((SUBLANE, D), f32), pltpu.VMEM((tm, D), f32), pltpu.VMEM((tm, LANE), f32),
                        pltpu.VMEM((tm, LANE), f32)],
        compiler_params=_cparams(("arbitrary",), VMEM_BIG),
        name="first_bwd",
    )(x, dx1, dh_a, *([dh_b.reshape(4, S // 4, D)] * nc), *([dh_c.reshape(16, S // 16, D)] * nc), w_pre)


def _t5_bucket(dist):
    n = jnp.maximum(dist, 0)
    nf = jnp.maximum(n, 1).astype(f32)
    large = MAX_EXACT + (jnp.log(nf / MAX_EXACT) / math.log(MAX_DISTANCE / MAX_EXACT)
                         * (NUM_BUCKETS - MAX_EXACT)).astype(jnp.int32)
    large = jnp.minimum(large, NUM_BUCKETS - 1)
    return jnp.where(n < MAX_EXACT, n, large)


def _bias_consts(d, after=None):
    if after is not None:
        d, _ = lax.optimization_barrier((jnp.int32(d), after))
    blk = ATTN_BLOCK
    rel = jnp.arange(blk)[:, None] + blk - jnp.arange(2 * blk)[None, :]
    in_win = (rel >= 0) & (rel <= blk)
    bucket = _t5_bucket(rel * d).reshape(1, -1)
    onehot = (bucket == jnp.arange(NUM_BUCKETS)[:, None]).astype(f32)
    return onehot, in_win.astype(f32).reshape(1, -1)


def _bias_build(tab_t, onehot, maskf, name, after):
    H = tab_t.shape[0]

    def body(t_ref, oh_ref, m_ref, after_ref, o_ref):
        b = jnp.dot(t_ref[...], oh_ref[...], precision=HIGHEST, preferred_element_type=f32)
        o_ref[...] = jnp.where(m_ref[...] > 0.5, b, NEG_INF)

    vm = pl.BlockSpec(memory_space=pltpu.VMEM)
    return pl.pallas_call(body, out_shape=SDS((H, onehot.shape[1]), f32), name=name,
                          in_specs=[vm, vm, vm, pl.BlockSpec(memory_space=pl.ANY)], out_specs=vm,
                          )(tab_t, onehot, maskf, after)


def _bias_grad(dbias_flat, onehot, name):
    H = dbias_flat.shape[0]

    def body(g_ref, oh_ref, o_ref):
        o_ref[...] = lax.dot_general(oh_ref[...], g_ref[...], NT, precision=HIGHEST, preferred_element_type=f32)

    return pl.pallas_call(body, out_shape=SDS((NUM_BUCKETS, H), f32), name=name)(dbias_flat, onehot)


ATTN_TILE = 512
ATTN_SUB = ATTN_TILE // ATTN_BLOCK
ATTN_HP = 4
ATTN_WIDE = ATTN_HP * LANE


def _qkv_specs(nt):
    tile = (ATTN_TILE, ATTN_WIDE)
    blk = (ATTN_BLOCK, ATTN_WIDE)
    sec = ATTN_OUT // ATTN_WIDE
    cur = lambda off: (lambda h, t: (jnp.minimum(t, nt - 1), off + h))
    prev = lambda off: (lambda h, t: (jnp.maximum(jnp.minimum(t, nt - 1) * ATTN_SUB - 1, 0), off + h))
    return [pl.BlockSpec(tile, cur(0)), pl.BlockSpec(blk, prev(sec)), pl.BlockSpec(tile, cur(sec)),
            pl.BlockSpec(blk, prev(2 * sec)), pl.BlockSpec(tile, cur(2 * sec))]


def _head_masks():
    lane = lax.broadcasted_iota(jnp.int32, (ATTN_BLOCK, LANE), 1)
    return lane < HEAD_DIM


def _stack_heads(x2, low):
    zero = jnp.zeros_like(x2)
    return jnp.concatenate([jnp.where(low, x2, zero), jnp.where(low, zero, x2)], axis=0)


def _attn_fwd(qkv, bias, bps, name, after=None):
    S = qkv.shape[0]
    nt = S // ATTN_TILE
    scale = HEAD_DIM ** -0.5

    def body(q_ref, kp_ref, kc_ref, vp_ref, vc_ref, b_ref, *rest):
        o_ref, l_ref = rest[-2:]
        t = pl.program_id(1)
        low = _head_masks()
        col = lax.broadcasted_iota(jnp.int32, (2 * ATTN_BLOCK, 2 * ATTN_BLOCK), 1)
        for hp in range(ATTN_HP):
            cols = slice(hp * LANE, (hp + 1) * LANE)
            kk = jnp.concatenate([kp_ref[:, cols], kc_ref[:, cols]], axis=0)
            vv = jnp.concatenate([vp_ref[:, cols], vc_ref[:, cols]], axis=0)
            bias2 = b_ref[2 * hp:2 * hp + 2].reshape(2 * ATTN_BLOCK, 2 * ATTN_BLOCK)
            for b in range(ATTN_SUB):
                lo = b * ATTN_BLOCK
                rows = slice(lo, lo + ATTN_BLOCK)
                keys = slice(lo, lo + 2 * ATTN_BLOCK)
                dead = jnp.logical_and((t * ATTN_SUB + b) % bps == 0, col < ATTN_BLOCK)
                q2 = _stack_heads(q_ref[rows, cols], low)
                kb, vb = kk[keys], vv[keys]
                s = lax.dot_general(q2, kb, NT, preferred_element_type=f32) * scale + bias2
                s = jnp.where(dead, NEG_INF, s)
                m = jnp.max(s, axis=-1, keepdims=True)
                p = jnp.exp(s - m)
                l = jnp.sum(p, axis=-1, keepdims=True)
                o2 = jnp.dot(p.astype(bf16), vb, preferred_element_type=f32) / l
                lse = m + jnp.log(l)
                o_ref[rows, cols] = jnp.where(low, o2[:ATTN_BLOCK], o2[ATTN_BLOCK:])
                l_ref[rows, cols] = jnp.where(low, lse[:ATTN_BLOCK], lse[ATTN_BLOCK:])

    tile = pl.BlockSpec((ATTN_TILE, ATTN_WIDE), lambda h, t: (t, h))
    return pl.pallas_call(
        body,
        grid=(4 // ATTN_HP, nt),
        in_specs=_qkv_specs(nt) + [pl.BlockSpec((2 * ATTN_HP, ATTN_BLOCK, 2 * ATTN_BLOCK), lambda h, t: (h, 0, 0))]
        + ([] if after is None else [pl.BlockSpec(memory_space=pl.ANY)]),
        out_specs=[tile, tile],
        out_shape=[SDS((S, ATTN_OUT), f32), SDS((S, ATTN_OUT), f32)],
        compiler_params=_cparams(("parallel", "parallel")),
        name=name,
    )(qkv, qkv, qkv, qkv, qkv, bias, *([] if after is None else [after]))


def _attn_bwd(qkv, bias, do, dvec, lse, bps, name):
    S = qkv.shape[0]
    nt = S // ATTN_TILE
    scale = HEAD_DIM ** -0.5

    def assemble(parts):
        rows = [parts[0][:ATTN_BLOCK]]
        for b in range(ATTN_SUB - 1):
            rows.append(parts[b][ATTN_BLOCK:] + parts[b + 1][:ATTN_BLOCK])
        rows.append(parts[-1][ATTN_BLOCK:])
        return rows

    def body(q_ref, kp_ref, kc_ref, vp_ref, vc_ref, b_ref, do_ref, dvec_ref, lse_ref,
             dq_ref, dk_ref, dv_ref, db_ref, ck, cv):
        t = pl.program_id(1)
        last = ATTN_TILE - ATTN_BLOCK

        @pl.when(t == 0)
        def _():
            ck[...] = jnp.zeros_like(ck)
            cv[...] = jnp.zeros_like(cv)
            db_ref[...] = jnp.zeros_like(db_ref)

        @pl.when(t < nt)
        def _():
            low = _head_masks()
            col = lax.broadcasted_iota(jnp.int32, (2 * ATTN_BLOCK, 2 * ATTN_BLOCK), 1)
            per_row = lambda t2: jnp.concatenate([t2[:, 0:1], t2[:, HEAD_DIM:HEAD_DIM + 1]], axis=0)
            for hp in range(ATTN_HP):
                cols = slice(hp * LANE, (hp + 1) * LANE)
                kk = jnp.concatenate([kp_ref[:, cols], kc_ref[:, cols]], axis=0)
                vv = jnp.concatenate([vp_ref[:, cols], vc_ref[:, cols]], axis=0)
                bias2 = b_ref[2 * hp:2 * hp + 2].reshape(2 * ATTN_BLOCK, 2 * ATTN_BLOCK)
                dk_parts, dv_parts = [], []
                dsum = None
                for b in range(ATTN_SUB):
                    lo = b * ATTN_BLOCK
                    rows = slice(lo, lo + ATTN_BLOCK)
                    keys = slice(lo, lo + 2 * ATTN_BLOCK)
                    dead = jnp.logical_and((t * ATTN_SUB + b) % bps == 0, col < ATTN_BLOCK)
                    q2 = _stack_heads(q_ref[rows, cols], low)
                    do2 = _stack_heads(do_ref[rows, cols].astype(bf16), low)
                    kb, vb = kk[keys], vv[keys]
                    s = lax.dot_general(q2, kb, NT, preferred_element_type=f32) * scale + bias2
                    s = jnp.where(dead, NEG_INF, s)
                    p = jnp.exp(s - per_row(lse_ref[rows, cols]))
                    dp = lax.dot_general(do2, vb, NT, preferred_element_type=f32)
                    ds = p * (dp - per_row(dvec_ref[rows, cols]))
                    dsum = ds if dsum is None else dsum + ds
                    dsb = ds.astype(bf16)
                    dq2 = jnp.dot(dsb, kb, preferred_element_type=f32) * scale
                    dq_ref[rows, cols] = jnp.where(low, dq2[:ATTN_BLOCK], dq2[ATTN_BLOCK:]).astype(bf16)
                    dk_parts.append(lax.dot_general(dsb, q2, TN, preferred_element_type=f32) * scale)
                    dv_parts.append(lax.dot_general(p.astype(bf16), do2, TN, preferred_element_type=f32))
                db_ref[2 * hp:2 * hp + 2] += dsum.reshape(2, ATTN_BLOCK, 2 * ATTN_BLOCK)
                for parts, carry, out_ref in ((dk_parts, ck, dk_ref), (dv_parts, cv, dv_ref)):
                    rws = assemble(parts)
                    out_ref[:last, cols] = carry[:last, cols].astype(bf16)
                    out_ref[last:, cols] = (carry[last:, cols] + rws[0]).astype(bf16)
                    for b in range(ATTN_SUB):
                        carry[b * ATTN_BLOCK:(b + 1) * ATTN_BLOCK, cols] = rws[b + 1]

        @pl.when(t == nt)
        def _():
            dk_ref[...] = ck[...].astype(bf16)
            dv_ref[...] = cv[...].astype(bf16)

    tile = (ATTN_TILE, ATTN_WIDE)
    cur = pl.BlockSpec(tile, lambda h, t: (jnp.minimum(t, nt - 1), h))
    lag = pl.BlockSpec(tile, lambda h, t: (jnp.maximum(t - 1, 0), h))
    bspec = pl.BlockSpec((2 * ATTN_HP, ATTN_BLOCK, 2 * ATTN_BLOCK), lambda h, t: (h, 0, 0))
    return pl.pallas_call(
        body,
        grid=(4 // ATTN_HP, nt + 1),
        in_specs=_qkv_specs(nt) + [bspec, cur, cur, cur],
        out_specs=[cur, lag, lag, bspec],
        out_shape=[SDS((S, ATTN_OUT), bf16), SDS((S, ATTN_OUT), bf16), SDS((S, ATTN_OUT), bf16),
                   SDS((8, ATTN_BLOCK, 2 * ATTN_BLOCK), f32)],
        scratch_shapes=[pltpu.VMEM(tile, f32), pltpu.VMEM(tile, f32)],
        compiler_params=_cparams(("parallel", "arbitrary")),
        name=name,
    )(qkv, qkv, qkv, qkv, qkv, bias, do, dvec, lse)


def _attn_merge(o0, o1, o2, l0, l1, l2):
    S, W = o0.shape
    R = PERM_ROWS

    def body(o0_ref, o1_ref, o2_ref, l0_ref, l1_ref, l2_ref, y_ref, yb_ref, w0_ref, w1_ref, w2_ref,
             so1, so2, sl1, sl2):
        _to_natural(o1_ref, so1, 4)
        _to_natural(l1_ref, sl1, 4)
        _to_natural(o2_ref, so2, 16)
        _to_natural(l2_ref, sl2, 16)
        a, b, c = l0_ref[...], sl1[...], sl2[...]
        m = jnp.maximum(jnp.maximum(a, b), c)
        ea, eb, ec = jnp.exp(a - m), jnp.exp(b - m), jnp.exp(c - m)
        den = (ea + eb) + ec
        w0, w1, w2 = ea / den, eb / den, ec / den
        y = (w0 * o0_ref[...] + w1 * so1[...]) + w2 * so2[...]
        y_ref[...] = y
        yb_ref[...] = y.astype(bf16)
        w0_ref[...] = w0
        w1_ref[...] = w1
        w2_ref[...] = w2

    nat = pl.BlockSpec((R, LANE), lambda i, j: (i, j))
    v4 = lambda t: t.reshape(4, S // 4, W)
    v16 = lambda t: t.reshape(16, S // 16, W)
    return pl.pallas_call(
        body,
        grid=(S // R, W // LANE),
        in_specs=[nat, _perm_spec(4), _perm_spec(16)] * 2,
        out_specs=[nat] * 5,
        out_shape=[SDS((S, W), f32), SDS((S, W), bf16)] + [SDS((S, W), f32)] * 3,
        scratch_shapes=[pltpu.VMEM((R, LANE), f32)] * 4,
        compiler_params=_cparams(("parallel", "parallel"), VMEM_BIG),
        name="attn_merge",
    )(o0, v4(o1), v16(o2), l0, v4(l1), v16(l2))


def _attn_merge_bwd(dy, y, w0, w1, w2, after=None):
    S, W = dy.shape
    R = PERM_ROWS

    def body(dy_ref, y_ref, w0_ref, w1_ref, w2_ref, *rest):
        a0, a1, a2, b0, b1, b2, sa, sb = rest[-8:]
        dyv = dy_ref[...]
        r = lax.broadcasted_iota(jnp.int32, (LANE, LANE), 0) // HEAD_DIM
        c = lax.broadcasted_iota(jnp.int32, (LANE, LANE), 1) // HEAD_DIM
        seg = jnp.where(r == c, 1.0, 0.0).astype(f32)
        cbar = jnp.dot(dyv * y_ref[...], seg, precision=HIGHEST, preferred_element_type=f32)
        w = w0_ref[...]
        a0[...] = (w * dyv).astype(bf16)
        b0[...] = w * cbar
        for d, w_ref, a_ref, b_ref in ((4, w1_ref, a1, b1), (16, w2_ref, a2, b2)):
            w = w_ref[...]
            sa[...] = w * dyv
            sb[...] = w * cbar
            n = R // d
            for k in range(d):
                rows = pl.ds(k, n, stride=d)
                a_ref[k] = sa[rows, :].astype(bf16)
                b_ref[k] = sb[rows, :]

    nat = pl.BlockSpec((R, LANE), lambda i, j: (i, j))
    shapes = lambda dt: [SDS((S, W), dt), SDS((4, S // 4, W), dt), SDS((16, S // 16, W), dt)]
    outs = pl.pallas_call(
        body,
        grid=(S // R, W // LANE),
        in_specs=[nat] * 5 + ([] if after is None else [pl.BlockSpec(memory_space=pl.ANY)]),
        out_specs=[nat, _perm_spec(4), _perm_spec(16)] * 2,
        out_shape=shapes(bf16) + shapes(f32),
        scratch_shapes=[pltpu.VMEM((R, LANE), f32)] * 2,
        compiler_params=_cparams(("parallel", "parallel"), VMEM_BIG),
        name="attn_merge_bwd",
    )(dy, y, w0, w1, w2, *([] if after is None else [after]))
    return [t.reshape(S, W) for t in outs]


HGRN_SB = 256
HGRN_PAIR = 4


def _chunk_masks():
    r = jnp.arange(HGRN_SB)[:, None]
    c = jnp.arange(HGRN_SB)[None, :]
    same = (r // HGRN_CHUNK) == (c // HGRN_CHUNK)
    return jnp.stack([same & (c <= r), same, same & (c >= r)]).astype(bf16)


def _mask_dot(mask, x):
    hi = x.astype(bf16)
    r1 = x - hi.astype(f32)
    mid = r1.astype(bf16)
    lo = (r1 - mid.astype(f32)).astype(bf16)
    p = jnp.dot(mask, jnp.concatenate([hi, mid, lo], axis=1), preferred_element_type=f32)
    n = x.shape[1]
    return (p[:, :n] + p[:, n:2 * n]) + p[:, 2 * n:]


def _hgrn_prep(q_raw, f_raw, lbv, tril, same):
    sq = _sigmoid(q_raw)
    qs = q_raw * sq
    sig = _sigmoid(f_raw)
    f = lbv + (1.0 - lbv) * sig
    g = jnp.log(f)
    k = 1.0 - f
    G = _mask_dot(tril, g)
    GL = _mask_dot(same, g)
    eG = jnp.exp(G)
    einv = jnp.exp(-G)
    edec = jnp.exp(GL - G)
    return dict(sq=sq, qs=qs, sig=sig, f=f, k=k, eG=eG, einv=einv, edec=edec, eGL=jnp.exp(GL),
                qt=qs * eG, kt=k * einv, kd=k * edec)


def _hgrn_fwd(hg, lb, normw):
    S = hg.shape[0]
    sb = HGRN_SB
    nsb = S // sb
    nch = sb // HGRN_CHUNK

    def body(q_ref, f_ref, v_ref, og_ref, lb_ref, nw_ref, m_ref, y_ref, o_ref, ck_ref, st):
        j = pl.program_id(1)

        @pl.when(j == 0)
        def _():
            st[...] = jnp.zeros_like(st)

        tril_m = m_ref[0]
        tril = tril_m.astype(f32) > 0.5

        def one_head(hh):
            cols = slice(hh * LANE, (hh + 1) * LANE)
            ST = st[hh]
            ck_ref[hh, 0] = ST
            pr = _hgrn_prep(q_ref[:, cols], f_ref[:, cols], lb_ref[:, cols], tril_m, m_ref[1])
            qtb, ktb, kdb = pr["qt"].astype(bf16), pr["kt"].astype(bf16), pr["kd"].astype(bf16)
            eGL = pr["eGL"]
            vb = v_ref[:, cols].astype(bf16)
            A = jnp.where(tril, lax.dot_general(qtb, ktb, NT, preferred_element_type=f32), 0.0)
            o = jnp.dot(A.astype(bf16), vb, preferred_element_type=f32)
            outs = []
            for ci in range(nch):
                lo = ci * HGRN_CHUNK
                sl = slice(lo, lo + HGRN_CHUNK)
                outs.append(o[sl] + lax.dot_general(qtb[sl], ST.astype(bf16), NT, preferred_element_type=f32))
                ST = ST * eGL[lo:lo + 1, :] + lax.dot_general(vb[sl], kdb[sl], TN, preferred_element_type=f32)
            st[hh] = ST
            of = jnp.concatenate(outs, axis=0)
            o_ref[:, cols] = of
            rms = lax.rsqrt(jnp.mean(of * of, axis=-1, keepdims=True) + EPS)
            ogv = og_ref[:, cols]
            y_ref[:, cols] = ((of * rms * nw_ref[...]) * (ogv * _sigmoid(ogv))).astype(bf16)

        for hh in range(HGRN_PAIR):
            one_head(hh)

    wide = HGRN_PAIR * LANE
    col = lambda off: pl.BlockSpec((sb, wide), lambda h, j: (j, off // HGRN_PAIR + h))
    return pl.pallas_call(
        body,
        grid=(4 // HGRN_PAIR, nsb),
        in_specs=[col(0), col(4), col(8), col(12), pl.BlockSpec((1, wide), lambda h, j: (0, h)),
                  pl.BlockSpec((1, LANE), lambda h, j: (0, 0)),
                  pl.BlockSpec((3, sb, sb), lambda h, j: (0, 0, 0))],
        out_specs=[col(0), col(0), pl.BlockSpec((HGRN_PAIR, 1, LANE, LANE), lambda h, j: (h, j, 0, 0))],
        out_shape=[SDS((S, HGRN_W), bf16), SDS((S, HGRN_W), f32), SDS((4, nsb, LANE, LANE), f32)],
        scratch_shapes=[pltpu.VMEM((HGRN_PAIR, LANE, LANE), f32)],
        compiler_params=_cparams(("parallel", "arbitrary")),
        name="hgrn_fwd",
    )(hg, hg, hg, hg, lb, normw, _chunk_masks())


def _hgrn_bwd(hg, o_raw, dy, ck, lb, normw, after=None):
    S = hg.shape[0]
    sb = HGRN_SB
    nsb = S // sb
    nch = sb // HGRN_CHUNK

    def body(q_ref, f_ref, v_ref, og_ref, o_ref, dy_ref, ck_ref, lb_ref, nw_ref, m_ref, *rest):
        dq_ref, df_ref, dv_ref, dog_ref, glb_ref, gnw_ref, dst, alb, anw = rest[-9:]
        j = pl.program_id(1)

        @pl.when(j == 0)
        def _():
            dst[...] = jnp.zeros_like(dst)
            alb[...] = jnp.zeros_like(alb)
            anw[...] = jnp.zeros_like(anw)

        tril_m = m_ref[0]
        tril = tril_m.astype(f32) > 0.5
        nw = nw_ref[...]

        def one_head(hh):
            cols = slice(hh * LANE, (hh + 1) * LANE)
            lbv = lb_ref[:, cols]
            q_raw = q_ref[:, cols]
            pr = _hgrn_prep(q_raw, f_ref[:, cols], lbv, tril_m, m_ref[1])
            qt, kt, kd, eGL = pr["qt"], pr["kt"], pr["kd"], pr["eGL"]
            qtb, ktb, kdb = qt.astype(bf16), kt.astype(bf16), kd.astype(bf16)
            vb = v_ref[:, cols].astype(bf16)

            o = o_ref[:, cols]
            ogv = og_ref[:, cols]
            sog = _sigmoid(ogv)
            rms = lax.rsqrt(jnp.mean(o * o, axis=-1, keepdims=True) + EPS)
            oh = o * rms
            dyv = dy_ref[:, cols]
            dog_ref[:, cols] = (dyv * (oh * nw) * (sog * (1.0 + ogv * (1.0 - sog)))).astype(bf16)
            dohw = dyv * (ogv * sog)
            anw[:, cols] += _colsum8(dohw * oh)
            doh = dohw * nw
            do = rms * (doh - oh * jnp.mean(doh * oh, axis=-1, keepdims=True))
            dob = do.astype(bf16)

            Ab = jnp.where(tril, lax.dot_general(qtb, ktb, NT, preferred_element_type=f32), 0.0).astype(bf16)
            dAb = jnp.where(tril, lax.dot_general(dob, vb, NT, preferred_element_type=f32), 0.0).astype(bf16)
            dv_acc = lax.dot_general(Ab, dob, TN, preferred_element_type=f32)
            dqt = jnp.dot(dAb, ktb, preferred_element_type=f32)
            dkt = lax.dot_general(dAb, qtb, TN, preferred_element_type=f32)

            ST = ck_ref[hh, 0]
            states = []
            for ci in range(nch):
                lo = ci * HGRN_CHUNK
                sl = slice(lo, lo + HGRN_CHUNK)
                states.append(ST)
                ST = ST * eGL[lo:lo + 1, :] + lax.dot_general(vb[sl], kdb[sl], TN, preferred_element_type=f32)

            dST = dst[hh]
            dqt_i, dkd_i, dv_i, deg_i = [None] * nch, [None] * nch, [None] * nch, [None] * nch
            for ci in reversed(range(nch)):
                lo = ci * HGRN_CHUNK
                sl = slice(lo, lo + HGRN_CHUNK)
                ST0 = states[ci]
                dSTb = dST.astype(bf16)
                dv_i[ci] = lax.dot_general(kdb[sl], dSTb, NT, preferred_element_type=f32)
                dqt_i[ci] = jnp.dot(dob[sl], ST0.astype(bf16), preferred_element_type=f32)
                dkd_i[ci] = jnp.dot(vb[sl], dSTb, preferred_element_type=f32)
                deg_i[ci] = jnp.broadcast_to(jnp.sum(dST * ST0, axis=0, keepdims=True), (HGRN_CHUNK, LANE))
                dST = dST * eGL[lo:lo + 1, :] + lax.dot_general(dob[sl], qtb[sl], TN, preferred_element_type=f32)
            dst[hh] = dST

            dqt = dqt + jnp.concatenate(dqt_i, axis=0)
            dkd = jnp.concatenate(dkd_i, axis=0)
            dv_ref[:, cols] = (dv_acc + jnp.concatenate(dv_i, axis=0)).astype(bf16)
            deg = jnp.concatenate(deg_i, axis=0)

            dqs = dqt * pr["eG"]
            dkdkd = dkd * kd
            dG = dqt * qt - dkt * kt - dkdkd
            dk = dkt * pr["einv"] + dkd * pr["edec"]
            dGL = _mask_dot(m_ref[1], dkdkd) + eGL * deg
            dg = _mask_dot(m_ref[2], dG) + dGL
            df = dg / pr["f"] - dk
            sig = pr["sig"]
            df_ref[:, cols] = (df * (1.0 - lbv) * (sig * (1.0 - sig))).astype(bf16)
            alb[:, cols] += _colsum8(df * (1.0 - sig))
            sq = pr["sq"]
            dq_ref[:, cols] = (dqs * (sq * (1.0 + q_raw * (1.0 - sq)))).astype(bf16)

        for hh in range(HGRN_PAIR):
            one_head(hh)

        @pl.when(j == nsb - 1)
        def _():
            glb_ref[...] = jnp.broadcast_to(jnp.sum(alb[...], axis=0, keepdims=True), (SUBLANE, wide))
            gnw_ref[...] = jnp.broadcast_to(jnp.sum(anw[...], axis=0, keepdims=True), (SUBLANE, wide))

    wide = HGRN_PAIR * LANE
    rev = lambda off: pl.BlockSpec((sb, wide), lambda h, j: (nsb - 1 - j, off // HGRN_PAIR + h))
    stat = pl.BlockSpec((SUBLANE, wide), lambda h, j: (0, h))
    return pl.pallas_call(
        body,
        grid=(4 // HGRN_PAIR, nsb),
        in_specs=[rev(0), rev(4), rev(8), rev(12), rev(0), rev(0),
                  pl.BlockSpec((HGRN_PAIR, 1, LANE, LANE), lambda h, j: (h, nsb - 1 - j, 0, 0)),
                  pl.BlockSpec((1, wide), lambda h, j: (0, h)), pl.BlockSpec((1, LANE), lambda h, j: (0, 0)),
                  pl.BlockSpec((3, sb, sb), lambda h, j: (0, 0, 0))]
        + ([] if after is None else [pl.BlockSpec(memory_space=pl.ANY)]),
        out_specs=[rev(0), rev(0), rev(0), rev(0), stat, stat],
        out_shape=[SDS((S, HGRN_W), bf16)] * 4 + [SDS((SUBLANE, HGRN_W), f32)] * 2,
        scratch_shapes=[pltpu.VMEM((HGRN_PAIR, LANE, LANE), f32), pltpu.VMEM((SUBLANE, wide), f32),
                        pltpu.VMEM((SUBLANE, wide), f32)],
        compiler_params=_cparams(("parallel", "arbitrary")),
        name="hgrn_bwd",
    )(hg, hg, hg, hg, o_raw, dy, ck, lb, normw, _chunk_masks(), *([] if after is None else [after]))


def _lb_fwd(raw):
    def body(r_ref, o_ref):
        r = r_ref[...]
        m = jnp.max(r, axis=0, keepdims=True)
        e = jnp.exp(r - m)
        o_ref[...] = (e / jnp.sum(e, axis=0, keepdims=True))[0:1]

    return pl.pallas_call(body, out_shape=SDS((1, raw.shape[1]), f32), name="lb_fwd")(raw)


def _lb_bwd(raw, dlb):
    def body(r_ref, d_ref, o_ref):
        r = r_ref[...]
        m = jnp.max(r, axis=0, keepdims=True)
        e = jnp.exp(r - m)
        s = e / jnp.sum(e, axis=0, keepdims=True)
        s0 = s[0:1]
        onehot0 = jnp.where(lax.broadcasted_iota(jnp.int32, r.shape, 0) == 0, 1.0, 0.0)
        o_ref[...] = d_ref[...] * s0 * (onehot0 - s)

    return pl.pallas_call(body, out_shape=SDS(raw.shape, f32), name="lb_bwd")(raw, dlb)


def _gate_fwd(ya, yh, w_ba, w_bh, gc):
    S = ya.shape[0]
    D = w_ba.shape[1]
    tm = _pick(S, MM_ROWS)

    def body(ya_ref, yh_ref, wa_ref, wh_ref, g0_ref, g1_ref, a_ref, b_ref, o_ref):
        a = jnp.dot(ya_ref[...], wa_ref[...], preferred_element_type=f32).astype(bf16)
        b = jnp.dot(yh_ref[...], wh_ref[...], preferred_element_type=f32).astype(bf16)
        a_ref[...] = a
        b_ref[...] = b
        s0, s1 = _sigmoid(g0_ref[...].astype(f32)), _sigmoid(g1_ref[...].astype(f32))
        o_ref[...] = (s0 * a.astype(f32) + s1 * b.astype(f32)).astype(bf16)

    row = pl.BlockSpec((tm, D), lambda i: (i, 0))
    act = pl.BlockSpec((tm, ya.shape[1]), lambda i: (i, 0))
    wspec = pl.BlockSpec(w_ba.shape, lambda i: (0, 0))
    return pl.pallas_call(
        body,
        grid=(S // tm,),
        in_specs=[act, act, wspec, wspec, row, pl.BlockSpec((tm, D), lambda i: (i, 1))],
        out_specs=[row, row, row],
        out_shape=[SDS((S, D), bf16)] * 3,
        compiler_params=_cparams(("parallel",), VMEM_BIG),
        name="branch_gate_fwd",
    )(ya, yh, w_ba, w_bh, gc, gc)


def _gate_bwd(dmo, w_out, a, b, gc, w_ba, w_bh):
    S, D = a.shape
    W = w_ba.shape[0]
    tm = _pick(S, MM_ROWS)

    def body(dmo_ref, wo_ref, a_ref, b_ref, g0_ref, g1_ref, wa_ref, wh_ref,
             da_ref, db_ref, dg_ref, dya_ref, dyh_ref):
        dm = lax.dot_general(dmo_ref[...], wo_ref[...], NT, preferred_element_type=f32)
        dmv = dm.astype(bf16).astype(f32)
        s0, s1 = _sigmoid(g0_ref[...].astype(f32)), _sigmoid(g1_ref[...].astype(f32))
        da = (dmv * s0).astype(bf16)
        db = (dmv * s1).astype(bf16)
        da_ref[...] = da
        db_ref[...] = db
        dg_ref[:, :D] = (dmv * a_ref[...].astype(f32) * (s0 * (1.0 - s0))).astype(bf16)
        dg_ref[:, D:] = (dmv * b_ref[...].astype(f32) * (s1 * (1.0 - s1))).astype(bf16)
        dya_ref[...] = lax.dot_general(da, wa_ref[...], NT, preferred_element_type=f32)
        dyh_ref[...] = lax.dot_general(db, wh_ref[...], NT, preferred_element_type=f32)

    row = pl.BlockSpec((tm, D), lambda i: (i, 0))
    wide = pl.BlockSpec((tm, 2 * D), lambda i: (i, 0))
    narrow = pl.BlockSpec((tm, W), lambda i: (i, 0))
    whole = lambda t: pl.BlockSpec(t.shape, lambda i: (0, 0))
    return pl.pallas_call(
        body,
        grid=(S // tm,),
        in_specs=[row, whole(w_out), row, row, row, pl.BlockSpec((tm, D), lambda i: (i, 1)), whole(w_ba), whole(w_bh)],
        out_specs=[row, row, wide, narrow, narrow],
        out_shape=[SDS((S, D), bf16), SDS((S, D), bf16), SDS((S, 2 * D), bf16), SDS((S, W), f32), SDS((S, W), f32)],
        compiler_params=_cparams(("parallel",), VMEM_BIG),
        name="gate_bwd_fused",
    )(dmo, w_out, a, b, gc, gc, w_ba, w_bh)


CONV_ROWS = 512
INV_SQRT2 = 0.7071067811865476
INV_SQRT_2PI = 0.3989422804014327


CONV_HALO = 16


def _shift_down(cur, prev, k):
    x = pltpu.roll(cur, k, 0)
    row = lax.broadcasted_iota(jnp.int32, (SUBLANE, LANE), 0)
    head = jnp.where(row < k, pltpu.roll(prev, k, 0)[:SUBLANE], x[:SUBLANE])
    return jnp.concatenate([head, x[SUBLANE:]], axis=0)


def _shift_up(cur, nxt, k):
    R = cur.shape[0]
    x = pltpu.roll(cur, R - k, 0)
    row = lax.broadcasted_iota(jnp.int32, (SUBLANE, LANE), 0)
    tail = jnp.where(row >= SUBLANE - k, pltpu.roll(nxt, SUBLANE - k, 0), x[R - SUBLANE:])
    return jnp.concatenate([x[:R - SUBLANE], tail], axis=0)


def _conv_rows(u_ref, w, b, r0, first):
    R = CONV_ROWS
    cur = u_ref[pl.ds(r0, R), :].astype(f32)
    prev = u_ref[pl.ds(pl.multiple_of(jnp.maximum(r0 - CONV_HALO, 0), CONV_HALO), CONV_HALO), :].astype(f32)
    prev = jnp.where(first, 0.0, prev)
    x1 = _shift_down(cur, prev, 1)
    x2 = _shift_down(cur, prev, 2)
    c = ((b + w[0:1] * x2) + w[1:2] * x1) + w[2:3] * cur
    return c, x2, x1, cur


def _conv_fwd(ug, uv, wg, wv, bg, bv):
    S, F = ug.shape
    nchunk = S // CONV_ROWS

    def body(ug_ref, uv_ref, wg_ref, wv_ref, bg_ref, bv_ref, o_ref):
        wgv, wvv, bgv, bvv = wg_ref[...], wv_ref[...], bg_ref[...], bv_ref[...]

        def step(ci, carry):
            r0 = pl.multiple_of(ci * CONV_ROWS, CONV_ROWS)
            cg = _conv_rows(ug_ref, wgv, bgv, r0, ci == 0)[0]
            cv = _conv_rows(uv_ref, wvv, bvv, r0, ci == 0)[0]
            gelu = 0.5 * cg * (1.0 + lax.erf(cg * INV_SQRT2))
            o_ref[pl.ds(r0, CONV_ROWS), :] = (gelu * cv).astype(bf16)
            return carry

        lax.fori_loop(0, nchunk, step, 0)

    col = pl.BlockSpec((S, LANE), lambda j: (0, j))
    w3 = pl.BlockSpec((3, LANE), lambda j: (0, j))
    b1 = pl.BlockSpec((1, LANE), lambda j: (0, j))
    return pl.pallas_call(
        body,
        grid=(F // LANE,),
        in_specs=[col, col, w3, w3, b1, b1],
        out_specs=col,
        out_shape=SDS((S, F), bf16),
        compiler_params=_cparams(("parallel",), VMEM_BIG),
        name="conv_fwd",
    )(ug, uv, wg, wv, bg, bv)


def _conv_bwd(ug, uv, dact, wg, wv, bg, bv):
    S, F = ug.shape
    R = CONV_ROWS
    nchunk = S // R

    def body(ug_ref, uv_ref, da_ref, wg_ref, wv_ref, bg_ref, bv_ref, dug_ref, duv_ref, sg_ref, sv_ref, dcg, dcv):
        wgv, wvv, bgv, bvv = wg_ref[...], wv_ref[...], bg_ref[...], bv_ref[...]
        zero = jnp.zeros((SUBLANE, LANE), f32)

        def fwd_step(ci, acc):
            r0 = pl.multiple_of(ci * R, R)
            cg, g2, g1, g0 = _conv_rows(ug_ref, wgv, bgv, r0, ci == 0)
            cv, v2, v1, v0 = _conv_rows(uv_ref, wvv, bvv, r0, ci == 0)
            da = da_ref[pl.ds(r0, R), :].astype(f32)
            cdf = 0.5 * (1.0 + lax.erf(cg * INV_SQRT2))
            pdf = INV_SQRT_2PI * jnp.exp(-0.5 * cg * cg)
            dg = da * cv * (cdf + cg * pdf)
            dv = da * (cg * cdf)
            dcg[pl.ds(r0, R), :] = dg
            dcv[pl.ds(r0, R), :] = dv
            new = (acc[0] + _colsum8(dg * g2), acc[1] + _colsum8(dg * g1), acc[2] + _colsum8(dg * g0),
                   acc[3] + _colsum8(dg),
                   acc[4] + _colsum8(dv * v2), acc[5] + _colsum8(dv * v1), acc[6] + _colsum8(dv * v0),
                   acc[7] + _colsum8(dv))
            return new

        acc = lax.fori_loop(0, nchunk, fwd_step, (zero,) * 8)
        rows = lax.broadcasted_iota(jnp.int32, (SUBLANE, LANE), 0)

        def stats(parts):
            out = jnp.zeros((SUBLANE, LANE), f32)
            for k, pt in enumerate(parts):
                out = jnp.where(rows == k, jnp.sum(pt, axis=0, keepdims=True), out)
            return out

        sg_ref[...] = stats(acc[0:4])
        sv_ref[...] = stats(acc[4:8])

        def du_rows(dc, w, r0, last):
            cur = dc[pl.ds(r0, R), :]
            nxt = dc[pl.ds(pl.multiple_of(jnp.minimum(r0 + R, S - SUBLANE), SUBLANE), SUBLANE), :]
            nxt = jnp.where(last, 0.0, nxt)
            return w[2:3] * cur + w[1:2] * _shift_up(cur, nxt, 1) + w[0:1] * _shift_up(cur, nxt, 2)

        def bwd_step(ci, carry):
            r0 = pl.multiple_of(ci * R, R)
            last = ci == nchunk - 1
            dug_ref[pl.ds(r0, R), :] = du_rows(dcg, wgv, r0, last).astype(bf16)
            duv_ref[pl.ds(r0, R), :] = du_rows(dcv, wvv, r0, last).astype(bf16)
            return carry

        lax.fori_loop(0, nchunk, bwd_step, 0)

    col = pl.BlockSpec((S, LANE), lambda j: (0, j))
    w3 = pl.BlockSpec((3, LANE), lambda j: (0, j))
    b1 = pl.BlockSpec((1, LANE), lambda j: (0, j))
    st = pl.BlockSpec((SUBLANE, LANE), lambda j: (0, j))
    return pl.pallas_call(
        body,
        grid=(F // LANE,),
        in_specs=[col, col, col, w3, w3, b1, b1],
        out_specs=[col, col, st, st],
        out_shape=[SDS((S, F), bf16), SDS((S, F), bf16), SDS((SUBLANE, F), f32), SDS((SUBLANE, F), f32)],
        scratch_shapes=[pltpu.VMEM((S, LANE), f32), pltpu.VMEM((S, LANE), f32)],
        compiler_params=_cparams(("parallel",), VMEM_BIG),
        name="conv_bwd",
    )(ug, uv, dact, wg, wv, bg, bv)


def _adam_math(w, g, m, v):
    m = ADAM_B1 * m + (1.0 - ADAM_B1) * g
    v = ADAM_B2 * v + (1.0 - ADAM_B2) * (g * g)
    m_hat = m / (1.0 - ADAM_B1 ** ADAM_STEP)
    v_hat = v / (1.0 - ADAM_B2 ** ADAM_STEP)
    delta = -ADAM_LR * (m_hat / (jnp.sqrt(v_hat) + ADAM_EPS) + ADAM_WD * w)
    return delta, m, v


def _adamw(w, m, v, g, name):
    R, C = w.shape
    parts = len(g.shape) == 3
    tr = R
    if R % 16 == 0:
        for t in range(R, 0, -16):
            if R % t == 0 and t * C * 4 <= ADAM_BLOCK_BYTES:
                tr = t
                break

    def body(w_ref, m_ref, v_ref, g_ref, go_ref, d_ref, mo_ref, vo_ref):
        if parts:
            gv = ((g_ref[0].astype(f32) + g_ref[1].astype(f32)) + g_ref[2].astype(f32)) + g_ref[3].astype(f32)
        else:
            gv = g_ref[...]
        go_ref[...] = gv
        d, mn, vn = _adam_math(w_ref[...], gv, m_ref[...], v_ref[...])
        d_ref[...] = d
        mo_ref[...] = mn
        vo_ref[...] = vn

    row = pl.BlockSpec((tr, C), lambda i: (i, 0))
    gspec = pl.BlockSpec((4, tr, C), lambda i: (0, i, 0)) if parts else row
    if isinstance(g, _Rows):
        lo, g = g.lo, g.full
        assert lo % (2 * SUBLANE) == 0 and tr % (2 * SUBLANE) == 0
        gspec = pl.BlockSpec((pl.Element(4), pl.Element(tr), pl.Element(C)),
                             lambda i: (0, pl.multiple_of(lo + i * tr, 2 * SUBLANE), 0))
    return pl.pallas_call(
        body,
        grid=(R // tr,),
        in_specs=[row, row, row, gspec],
        out_specs=[row] * 4,
        out_shape=[SDS((R, C), f32)] * 4,
        compiler_params=_cparams(("parallel",), VMEM_BIG),
        name=name,
    )(w, m, v, g)


def _sum8(parts, name):
    _, _, R, C = parts.shape

    def body(p_ref, o_ref):
        acc = p_ref[0, 0]
        for c in range(2):
            for k in range(4):
                if c or k:
                    acc = acc + p_ref[c, k]
        o_ref[...] = acc

    return pl.pallas_call(body, out_shape=SDS((R, C), f32), name=name)(parts)


def _pair_add(by_core, b, name):
    _, K, R, C = by_core.shape
    tr = R // 2 if R % 32 == 0 else R

    def body(c_ref, a_ref, b_ref, o_ref):
        o_ref[...] = (a_ref[0].astype(f32) + b_ref[...].astype(f32)).astype(bf16)

    blk = pl.BlockSpec((1, tr, C), lambda k, i, c: (k, i, 0))
    return pl.pallas_call(
        body,
        grid_spec=pltpu.PrefetchScalarGridSpec(
            num_scalar_prefetch=1,
            grid=(K, R // tr),
            in_specs=[pl.BlockSpec((1, 1, tr, C), lambda k, i, c: (c[0], k, i, 0)), blk],
            out_specs=blk,
        ),
        out_shape=SDS((K, R, C), bf16),
        compiler_params=_cparams(("parallel", "parallel")),
        name=name,
    )(lax.axis_index("c").astype(jnp.int32).reshape(1), by_core, b)


_ANY = pl.BlockSpec(memory_space=pl.ANY)


def _chip_out_shape(src, gather):
    return SDS((4,) + tuple(src.shape if gather else src.shape[1:]), src.dtype)


def _fill_own(out, src, gather):
    mine = 2 * lax.axis_index("x") + lax.axis_index("y")
    own = src if gather else lax.dynamic_index_in_dim(src, mine, axis=0, keepdims=False)
    return lax.dynamic_update_index_in_dim(out, own, mine, axis=0)


_HBM = pl.BlockSpec(memory_space=pltpu.HBM)
_SEM = pl.BlockSpec(memory_space=pltpu.SEMAPHORE)
_EFFECT = pltpu.SideEffectType.DATAFLOW_SIDE_EFFECTING
_SPLIT_PEERS = {"chip_gather": 3, "chip_gather_wide": 3, "chip_xchg": 3, "core_fill": 4, "core_swap": 1}


def _split_land(src, kind):
    if kind == "chip_gather_wide":
        return SDS((4, 2) + tuple(src.shape), src.dtype)
    if kind == "core_fill":
        return SDS((SUBLANE, LANE), src.dtype)
    if kind == "core_swap":
        return SDS(tuple(src.shape[1:]), src.dtype)
    return _chip_out_shape(src, kind == "chip_gather")


def _split_copies(src_ref, land_ref, sems, kind):
    x, y, c = lax.axis_index("x"), lax.axis_index("y"), lax.axis_index("c")
    n = _SPLIT_PEERS[kind]
    if kind == "core_fill":
        routes = [((x, y, 1 - c), src_ref.at[k, c], src_ref.at[k, c], src_ref.at[k, 1 - c]) for k in range(n)]
    elif kind == "core_swap":
        routes = [((x, y, 1 - c), src_ref.at[1 - c], land_ref, land_ref)]
    else:
        mine = 2 * x + y
        gather = kind != "chip_xchg"
        slot = (lambda k: land_ref.at[k, c]) if kind == "chip_gather_wide" else (lambda k: land_ref.at[k])
        routes = [((px, py, c), src_ref if gather else src_ref.at[2 * px + py], slot(mine), slot(2 * px + py))
                  for px, py in [(1 - x, y), (x, 1 - y), (1 - x, 1 - y)]]
    sends, recvs = [], []
    for j, (peer, piece, there, here) in enumerate(routes):
        sends.append(pltpu.make_async_remote_copy(src_ref=piece, dst_ref=there, send_sem=sems[j],
                                                  recv_sem=sems[n + j], device_id=peer, device_id_type=MESH))
        recvs.append(pltpu.make_async_remote_copy(src_ref=piece, dst_ref=here, send_sem=sems[j],
                                                  recv_sem=sems[n + j], device_id=peer, device_id_type=MESH))
    return sends, recvs


def _split_start(src, kind, name, after=None):
    land = _split_land(src, kind)
    ns = 2 * _SPLIT_PEERS[kind]
    n_in = 2 if after is None else 3

    def body(*refs):
        src_ref, land_ref = refs[:2]
        outs = refs[n_in:]
        for cp in _split_copies(src_ref, land_ref, outs[:ns], kind)[0]:
            cp.start()
        token = outs[ns + 2]
        token[...] = jnp.zeros_like(token)

    res = pl.pallas_call(
        body,
        name=name,
        out_shape=(pltpu.SemaphoreType.DMA(()),) * ns
        + (pltpu.HBM(src.shape, src.dtype), pltpu.HBM(land.shape, land.dtype), SDS((SUBLANE, LANE), f32)),
        in_specs=(_HBM, _HBM) + (() if after is None else (_ANY,)),
        out_specs=(_SEM,) * ns + (_HBM, _HBM, pl.BlockSpec(memory_space=pltpu.VMEM)),
        input_output_aliases={0: ns, 1: ns + 1},
        compiler_params=pltpu.CompilerParams(has_side_effects=_EFFECT),
    )(pltpu.with_memory_space_constraint(src, pltpu.HBM),
      pltpu.with_memory_space_constraint(lax.empty(land.shape, land.dtype), pltpu.HBM),
      *(() if after is None else (after,)))
    return (res[:ns], res[ns], res[ns + 1]), res[ns + 2]


def _split_wait(state, after, kind, name):
    sems, src_thru, land_thru = state
    ns = 2 * _SPLIT_PEERS[kind]

    def body(src_ref, land_ref, *rest):
        sends, recvs = _split_copies(src_ref, land_ref, rest[:ns], kind)
        for cp in recvs:
            cp.wait_recv()
        for cp in sends:
            cp.wait_send()

    src_out, got = pl.pallas_call(
        body,
        name=name,
        out_shape=(pltpu.HBM(src_thru.shape, src_thru.dtype), pltpu.HBM(land_thru.shape, land_thru.dtype)),
        in_specs=(_HBM, _HBM) + (_SEM,) * ns + (_ANY,),
        out_specs=(_HBM, _HBM),
        input_output_aliases={0: 0, 1: 1},
        compiler_params=pltpu.CompilerParams(has_side_effects=_EFFECT),
    )(src_thru, land_thru, *sems, after)
    if kind == "core_swap":
        return got, src_out
    if kind == "core_fill":
        return src_out
    if kind == "chip_gather_wide":
        mine = 2 * lax.axis_index("x") + lax.axis_index("y")
        zero = jnp.zeros((), mine.dtype)
        return lax.dynamic_update_slice(got, src_out[None, None], (mine, lax.axis_index("c").astype(mine.dtype))
                                        + (zero,) * src_out.ndim)
    return _fill_own(got, src_out, kind == "chip_gather")


def _core_fill(both, name):
    n = both.shape[0]

    def body(in_ref, out_ref, send_sems, recv_sems):
        x, y, c = lax.axis_index("x"), lax.axis_index("y"), lax.axis_index("c")
        sends = [pltpu.make_async_remote_copy(src_ref=out_ref.at[k, c], dst_ref=out_ref.at[k, c],
                                              send_sem=send_sems.at[k], recv_sem=recv_sems.at[k],
                                              device_id=(x, y, 1 - c), device_id_type=MESH) for k in range(n)]
        recvs = [pltpu.make_async_remote_copy(src_ref=out_ref.at[k, c], dst_ref=out_ref.at[k, 1 - c],
                                              send_sem=send_sems.at[k], recv_sem=recv_sems.at[k],
                                              device_id=(x, y, 1 - c), device_id_type=MESH) for k in range(n)]
        for cp in sends:
            cp.start()
        for cp in recvs:
            cp.wait_recv()
        for cp in sends:
            cp.wait_send()

    return pl.pallas_call(
        body,
        in_specs=[_ANY],
        out_specs=_ANY,
        out_shape=SDS(both.shape, both.dtype),
        scratch_shapes=[pltpu.SemaphoreType.DMA((n,)), pltpu.SemaphoreType.DMA((n,))],
        input_output_aliases={0: 0},
        name=name,
    )(both)


def _core_gather(src, name):
    def body(src_ref, out_ref, send_sem, recv_sem):
        x, y, c = lax.axis_index("x"), lax.axis_index("y"), lax.axis_index("c")
        cp = pltpu.make_async_remote_copy(src_ref=src_ref, dst_ref=out_ref.at[c], send_sem=send_sem,
                                          recv_sem=recv_sem, device_id=(x, y, 1 - c), device_id_type=MESH)
        cp.start()
        pltpu.make_async_remote_copy(src_ref=src_ref, dst_ref=out_ref.at[1 - c], send_sem=send_sem,
                                     recv_sem=recv_sem, device_id=(x, y, 1 - c), device_id_type=MESH).wait_recv()
        cp.wait_send()

    out = pl.pallas_call(
        body,
        in_specs=[_ANY],
        out_specs=_ANY,
        out_shape=SDS((2,) + tuple(src.shape), src.dtype),
        scratch_shapes=[pltpu.SemaphoreType.DMA, pltpu.SemaphoreType.DMA],
        name=name,
    )(src)
    return lax.dynamic_update_index_in_dim(out, src, lax.axis_index("c"), axis=0)


_PACK_A = (("w_in", (1088, 1024)),)
_PACK_B = (("w_ba", (512, 128)), ("w_bh", (512, 128)), ("w_out", (128, 1024)), ("w_up", (704, 1024)),
           ("w_down", (352, 1024)))
_PACK_SIZES = _PACK_A + _PACK_B
_TRANSPOSED = ("w_in", "w_up")


def _slab_rows(sizes):
    return sum(r * c for _, (r, c) in sizes) // D_MODEL


def _pack_lo(key):
    keys = [k for k, _ in _PACK_B]
    return _slab_rows(_PACK_B[:keys.index(key)])


def _pack_rows(d, sizes):
    n = d[sizes[0][0]].shape[0]
    return jnp.concatenate([d[k].reshape(n, -1, D_MODEL) for k, _ in sizes], axis=1)


def _unpack_rows(slab, sizes):
    n = slab.shape[0]
    out, lo = {}, 0
    for key, (r, c) in sizes:
        rows = r * c // D_MODEL
        out[key] = slab[:, lo:lo + rows].reshape(n, r, c)
        lo += rows
    return out


def _by_core(gslab):
    return jnp.swapaxes(gslab.reshape((4, 2) + gslab.shape[1:]), 0, 1)


def _cols_to_full(t):
    return jnp.swapaxes(t, 0, 1).reshape(t.shape[1], -1)


def _full_to_cols(t):
    K = t.shape[0]
    return jnp.swapaxes(t.reshape(K, 8, -1), 0, 1)


_SMALL = (("pre_mix_norm", (1, 1024)), ("rel_bias", (32, 24)), ("hgrn_lb_raw", (2, 512)), ("hgrn_norm", (1, 128)),
          ("post_mix_norm", (1, 1024)), ("pre_ffn_norm", (1, 1024)), ("conv_b", (1, 5632)),
          ("post_ffn_norm", (1, 1024)))
_SMALL_ROWS = 96
_CONVW_ROWS = 136


_SMALL_USED = sum(r * c for _, (r, c) in _SMALL)


def _pack_small(d, extra=None):
    flat = jnp.concatenate([d[k].reshape(-1) for k, _ in _SMALL] + ([] if extra is None else [extra.reshape(-1)]))
    flat = jnp.pad(flat, (0, _SMALL_ROWS * LANE - flat.shape[0]))
    return flat.reshape(_SMALL_ROWS, LANE)


def _unpack_small(p):
    flat = p.reshape(-1)
    out, lo = {}, 0
    for k, shp in _SMALL:
        n = shp[0] * shp[1]
        out[k] = flat[lo:lo + n].reshape(shp)
        lo += n
    return out


def _local_step(x, tgt, P, plan):
    S = x.shape[0]
    P = dict(P)
    lb = _lb_fwd(P["hgrn_lb_raw"])
    hs = _prep(x, P["pre_mix_norm"], plan.start_token())
    h1 = hs[0]
    consts = [_bias_consts(d, plan.start_token()) for d in DILATIONS]
    biases, dep = [], h1
    for g in range(N_GROUPS):
        tab_t = P["rel_bias"][:, 8 * g:8 * g + 8].T
        dep = _bias_build(tab_t, consts[g][0], consts[g][1], f"bias_build{g}", dep)
        biases.append(dep.reshape(8, ATTN_BLOCK, 2 * ATTN_BLOCK))
    W = dict(plan.weights_a(dep))
    qkv0, hg, gc = _mm_fanout(h1, [W["wt_qkv"][0], W["wt_hg"], W["wt_gate"]], "nt", [bf16, f32, bf16], "proj_natural")
    qkv = [qkv0] + [_mm(hs[g], W["wt_qkv"][g], "nt", bf16, f"proj_qkv{g}") for g in (1, 2)]
    y_hgrn, o_raw, ck = _hgrn_fwd(hg, lb, P["hgrn_norm"])
    obuf, lbuf, token = [], [], y_hgrn
    for g, d in enumerate(DILATIONS):
        o_g, l_g = _attn_fwd(qkv[g], biases[g], (S // d) // ATTN_BLOCK, f"attn_fwd{g}", after=token)
        lbuf.append(l_g)
        obuf.append(o_g)
        if g == 0:
            token = plan.forward_b(o_g)
    y_attn, y_attn_b, w0, w1, w2 = _attn_merge(obuf[0], obuf[1], obuf[2], lbuf[0], lbuf[1], lbuf[2])
    wb = plan.weights_b(y_attn)
    P["conv_w"] = wb.pop("conv_w")
    W.update(wb)
    a, b, merged = _gate_fwd(y_attn_b, y_hgrn, W["w_ba"], W["w_bh"], gc)
    mo, x1, h2 = _mid_fwd(x, merged, W["w_out"], P["post_mix_norm"], P["pre_ffn_norm"])
    ug, uv = _mm_fanout(h2, [W["wt_up_g"], W["wt_up_v"]], "nt", [bf16, bf16], "up_proj")
    cw_g, cw_v = P["conv_w"][:, :D_FF], P["conv_w"][:, D_FF:]
    cb_g, cb_v = P["conv_b"][:, :D_FF], P["conv_b"][:, D_FF:]
    act = _conv_fwd(ug, uv, cw_g, cw_v, cb_g, cb_v)
    loss, dy, dfo, g_post_ffn = _final(x1, act, W["w_down"], tgt, P["post_ffn_norm"])
    gslab = lax.empty((8, _slab_rows(_PACK_B), D_MODEL), bf16)
    rows_b = {k: r for k, (r, _) in _PACK_B}
    gslab = _mm(act, dfo, "tn", bf16, "gw_down", into=(gslab, 0, _pack_lo("w_down"), rows_b["w_down"]))
    dact = _mm(dfo, W["w_down"], "nt", bf16, "d_act")
    dug, duv, st_g, st_v = _conv_bwd(ug, uv, dact, cw_g, cw_v, cb_g, cb_v)
    gslab = _mm(dug, h2, "tn", bf16, "gw_up_gate", into=(gslab, 0, _pack_lo("w_up"), rows_b["w_up"]))
    gslab = _mm(duv, h2, "tn", bf16, "gw_up_val", into=(gslab, 4, _pack_lo("w_up"), rows_b["w_up"]))
    dx1, dmo, g_pre_ffn, g_post_mix = _mid_bwd(dy, dug, duv, W["wt_up_g"], W["wt_up_v"], x1, mo, P["pre_ffn_norm"],
                                               P["post_mix_norm"])
    gslab = _mm(merged, dmo, "tn", bf16, "gw_out", into=(gslab, 0, _pack_lo("w_out"), rows_b["w_out"]))
    da, db, dgc, dyattn, dyhgrn = _gate_bwd(dmo, W["w_out"], a, b, gc, W["w_ba"], W["w_bh"])
    gW_ba = _mm(y_attn_b, da, "tn", bf16, "gw_ba")
    gW_bh = _mm(y_hgrn, db, "tn", bf16, "gw_bh")
    big_b = dict(w_ba=gW_ba, w_bh=gW_bh, slab=gslab)
    dos = _attn_merge_bwd(dyattn, y_attn, w0, w1, w2, after=plan.grads_b_start(big_b))
    dq_h, df_h, dv_h, dog_h, glb8, gnw8 = _hgrn_bwd(hg, o_raw, dyhgrn, ck, lb, P["hgrn_norm"],
                                                   after=plan.grads_b_exchange(dos[5]))
    dhg = [dq_h, df_h, dv_h, dog_h]
    g_lb_raw = _lb_bwd(P["hgrn_lb_raw"], glb8[0:1])
    gn = gnw8[0:1]
    g_hgrn_norm = (gn[:, 0:128] + gn[:, 128:256]) + (gn[:, 256:384] + gn[:, 384:512])
    dqkvs, gW_qkv, g_rel = [], [], []
    for g, d in enumerate(DILATIONS):
        dq, dk, dv, dbias = _attn_bwd(qkv[g], biases[g], dos[g], dos[3 + g], lbuf[g], (S // d) // ATTN_BLOCK,
                                      f"attn_bwd{g}")
        dqkvs.append([dq, dk, dv])
        gW_qkv.append(_mm(dqkvs[g], hs[g], "tn", bf16, f"gw_qkv{g}"))
        g_rel.append(_bias_grad(dbias.reshape(8, -1), consts[g][0], f"bias_grad{g}"))
    gW_hg = _mm(dhg, h1, "tn", bf16, "gw_hg")
    gW_gate = _mm(dgc, h1, "tn", bf16, "gw_gate")
    gW_in = gW_qkv + [gW_hg, gW_gate]
    token = plan.grads_a_start(gW_in)
    dh_perm = [_mm(dqkvs[g], W["wt_qkv"][g], "nn", f32, f"dh1_qkv{g}", after=token) for g in (1, 2)]
    token = plan.grads_a_exchange(dh_perm[1])
    dh_main = _mm(dqkvs[0] + dhg + [dgc], [W["wt_qkv"][0], W["wt_hg"], W["wt_gate"]], "nn", f32, "dh1_main",
                  after=token)
    grad_x, g_pre_mix = _first_bwd(x, dx1, dh_main, dh_perm[0], dh_perm[1], P["pre_mix_norm"])

    g_conv_w = jnp.concatenate([st_g[0:3], st_v[0:3]], axis=1)
    g_conv_b = jnp.concatenate([st_g[3:4], st_v[3:4]], axis=1)
    small = dict(pre_mix_norm=g_pre_mix, rel_bias=jnp.concatenate(g_rel, axis=1), hgrn_lb_raw=g_lb_raw,
                 hgrn_norm=g_hgrn_norm, post_mix_norm=g_post_mix, pre_ffn_norm=g_pre_ffn, conv_b=g_conv_b,
                 post_ffn_norm=g_post_ffn, conv_w=g_conv_w)
    return loss, grad_x, gW_in, big_b, small


def _weights_a(both):
    wt = both.reshape(-1, D_MODEL)
    return dict(
        wt_qkv=[_Rows(wt, g * QKV_G, QKV_G) for g in range(N_GROUPS)],
        wt_hg=_Rows(wt, 3 * QKV_G, 4 * HGRN_W),
        wt_gate=_Rows(wt, 3 * QKV_G + 4 * HGRN_W, wt.shape[0] - 3 * QKV_G - 4 * HGRN_W),
    )


def _weights_b(slabs):
    sh = _unpack_rows(slabs, _PACK_B)
    wt_up = sh["w_up"].reshape(-1, D_MODEL)
    return dict(
        w_ba=_cols_to_full(sh["w_ba"]),
        w_bh=_cols_to_full(sh["w_bh"]),
        w_out=sh["w_out"].reshape(D_MODEL, D_MODEL),
        wt_up_g=wt_up[:D_FF],
        wt_up_v=wt_up[D_FF:],
        w_down=sh["w_down"].reshape(D_FF, D_MODEL),
    )


def _dest_rows(sections, height):
    out = []
    for j in range(8):
        lo, hi, off, pieces = j * height, (j + 1) * height, 0, []
        for s in sections:
            a, b = max(lo, off), min(hi, off + s.shape[0])
            if a < b:
                pieces.append(s[a - off:b - off])
            off += s.shape[0]
        out.append(pieces[0] if len(pieces) == 1 else jnp.concatenate(pieces, axis=0))
    return out


def _grad_blocks_a(sections):
    rows = _dest_rows(sections, 1088)
    return jnp.stack([jnp.stack([rows[2 * k + c].astype(bf16) for k in range(4)]) for c in range(2)])


def _grad_slab_b(g):
    shards = dict(w_ba=_full_to_cols(g["w_ba"]), w_bh=_full_to_cols(g["w_bh"]))
    head = _pack_rows({k: v.astype(bf16) for k, v in shards.items()}, _PACK_B[:2])
    assert head.shape[1] == _pack_lo("w_out")
    return lax.dynamic_update_slice(g["slab"], head, (0, 0, 0))


_CONVW_SLAB_ROWS = 16


class _Traffic:
    def __init__(self, slab_a, slab_b, conv_w):
        hi = conv_w.astype(bf16)
        r1 = conv_w - hi.astype(f32)
        mid = r1.astype(bf16)
        lo = (r1 - mid.astype(f32)).astype(bf16)
        bits = jnp.stack([hi, mid, lo]).reshape(-1)
        tail = jnp.pad(bits, (0, _CONVW_SLAB_ROWS * D_MODEL - bits.shape[0])).reshape(_CONVW_SLAB_ROWS, D_MODEL)
        self.slab_b = jnp.concatenate([slab_b, tail], axis=0)
        self.state_a, tok = _split_start(slab_a, "chip_gather_wide", "ag_a_start")
        self.state_b, self.token = _split_start(self.slab_b, "chip_gather_wide", "ag_b_start", after=tok)
        self.state = None
        self.state_gb = None

    def start_token(self):
        return self.token

    def weights_a(self, after):
        half = _split_wait(self.state_a, after, "chip_gather_wide", "ag_a_wait")
        return _weights_a(_core_fill(half, "ag_a_cores"))

    def forward_b(self, after):
        half = _split_wait(self.state_b, after, "chip_gather_wide", "ag_b_wait")
        self.state, token = _split_start(half, "core_fill", "ag_b_cores_start")
        return token

    def weights_b(self, after):
        both = _split_wait(self.state, after, "core_fill", "ag_b_cores_wait")
        slabs = both.reshape((8,) + tuple(self.slab_b.shape))
        rows = _slab_rows(_PACK_B)
        out = _weights_b(slabs[:, :rows])
        pieces = slabs[:, rows:].reshape(8, -1)[:, :3 * 3 * 704].reshape(8, 3, 3, 704).astype(f32)
        out["conv_w"] = _cols_to_full((pieces[:, 0] + pieces[:, 1]) + pieces[:, 2])
        return out

    def grads_b_start(self, grads):
        self.state, token = _split_start(_by_core(_grad_slab_b(grads)), "core_swap", "rs_b_cores_start")
        return token

    def grads_b_exchange(self, after):
        from_sib, by_core = _split_wait(self.state, after, "core_swap", "rs_b_cores_wait")
        self.state_gb, token = _split_start(_pair_add(by_core, from_sib, "rs_b_pair_add"), "chip_xchg", "rs_b_start")
        return token

    def grads_a_start(self, sections):
        self.state, token = _split_start(_grad_blocks_a(sections), "core_swap", "rs_a_cores_start")
        return token

    def grads_a_exchange(self, after):
        from_sib, by_core = _split_wait(self.state, after, "core_swap", "rs_a_cores_wait")
        self.state, token = _split_start(_pair_add(by_core, from_sib, "rs_a_pair_add"), "chip_xchg", "rs_a_start")
        return token

    def parts(self, after):
        slab = _split_wait(self.state_gb, after, "chip_xchg", "rs_b_wait")
        parts, lo = _unpack_rows(slab, _PACK_B), 0
        for key, (r, c) in _PACK_B:
            if c == D_MODEL:
                parts[key] = _Rows(slab, lo, r)
            lo += r * c // D_MODEL
        parts["w_in"] =_split_wait(self.state, after, "chip_xchg", "rs_a_wait")
        return parts


def kernel(x, pre_mix_norm, w_in, rel_bias, hgrn_lb_raw, hgrn_norm, w_branch_attn, w_branch_hgrn, w_out, post_mix_norm, pre_ffn_norm, w_up, conv_w, conv_b, w_down, post_ffn_norm, loss_target, m_pre_mix_norm, m_w_in, m_rel_bias, m_hgrn_lb_raw, m_hgrn_norm, m_w_branch_attn, m_w_branch_hgrn, m_w_out, m_post_mix_norm, m_pre_ffn_norm, m_w_up, m_conv_w, m_conv_b, m_w_down, m_post_ffn_norm, v_pre_mix_norm, v_w_in, v_rel_bias, v_hgrn_lb_raw, v_hgrn_norm, v_w_branch_attn, v_w_branch_hgrn, v_w_out, v_post_mix_norm, v_pre_ffn_norm, v_w_up, v_conv_w, v_conv_b, v_w_down, v_post_ffn_norm):
    ci = lax.axis_index("c")
    dev = 4 * lax.axis_index("x") + 2 * lax.axis_index("y") + ci
    tr = lambda t: jnp.swapaxes(t[0], 0, 1)
    wts = dict(w_in=tr(w_in), w_ba=w_branch_attn[0], w_bh=w_branch_hgrn[0], w_out=w_out[0], w_up=tr(w_up),
               w_down=w_down[0])
    mom = dict(w_in=tr(m_w_in), w_ba=m_w_branch_attn[0], w_bh=m_w_branch_hgrn[0], w_out=m_w_out[0], w_up=tr(m_w_up),
               w_down=m_w_down[0])
    var = dict(w_in=tr(v_w_in), w_ba=v_w_branch_attn[0], w_bh=v_w_branch_hgrn[0], w_out=v_w_out[0], w_up=tr(v_w_up),
               w_down=v_w_down[0])
    small_w = dict(pre_mix_norm=pre_mix_norm, rel_bias=rel_bias, hgrn_lb_raw=hgrn_lb_raw, hgrn_norm=hgrn_norm,
                   post_mix_norm=post_mix_norm, pre_ffn_norm=pre_ffn_norm, conv_b=conv_b, post_ffn_norm=post_ffn_norm)
    small_m = dict(pre_mix_norm=m_pre_mix_norm, rel_bias=m_rel_bias, hgrn_lb_raw=m_hgrn_lb_raw, hgrn_norm=m_hgrn_norm,
                   post_mix_norm=m_post_mix_norm, pre_ffn_norm=m_pre_ffn_norm, conv_b=m_conv_b,
                   post_ffn_norm=m_post_ffn_norm)
    small_v = dict(pre_mix_norm=v_pre_mix_norm, rel_bias=v_rel_bias, hgrn_lb_raw=v_hgrn_lb_raw, hgrn_norm=v_hgrn_norm,
                   post_mix_norm=v_post_mix_norm, pre_ffn_norm=v_pre_ffn_norm, conv_b=v_conv_b,
                   post_ffn_norm=v_post_ffn_norm)

    plan = _Traffic(wts["w_in"].astype(bf16),
                    _pack_rows({k: wts[k].astype(bf16)[None] for k, _ in _PACK_B}, _PACK_B)[0], conv_w[0])

    loss8, grad_x, _, _, small = _local_step(x[0], loss_target[0], small_w, plan)
    spack = jnp.concatenate([_pack_small(small, loss8[0, 0:1]),
                             jnp.pad(small["conv_w"].reshape(-1, LANE), ((0, _CONVW_ROWS - 132), (0, 0)))], axis=0)
    small_state, token = _split_start(spack, "chip_gather", "ag_small_start")

    parts = plan.parts(token)
    outs_big = {}
    for k, _ in _PACK_SIZES:
        outs_big[k] = _adamw(wts[k], mom[k], var[k], parts[k], "adamw_" + k)

    by_chip = _split_wait(small_state, outs_big["w_in"][1], "chip_gather", "ag_small_wait")
    allp = _core_gather(by_chip, "ag_small_cores")
    ssum = _sum8(allp, "small_sum")
    gs = ssum[:_SMALL_ROWS]
    loss = ssum[_SMALL_USED // LANE, _SMALL_USED % LANE]
    res_small = _adamw(_pack_small(small_w), _pack_small(small_m), _pack_small(small_v), gs, "adamw_small")
    sm = [_unpack_small(t) for t in res_small]
    g_cw_full = ssum[_SMALL_ROWS:_SMALL_ROWS + 132].reshape(3, 2 * D_FF)
    g_cw = lax.dynamic_slice_in_dim(g_cw_full, dev * 704, 704, axis=1)
    res_cw = _adamw(conv_w[0], m_conv_w[0], v_conv_w[0], g_cw, "adamw_conv_w")

    def pick(i):
        def big_(k):
            t = outs_big[k][i]
            return (jnp.swapaxes(t, 0, 1) if k in _TRANSPOSED else t)[None]
        return [sm[i]["pre_mix_norm"], big_("w_in"), sm[i]["rel_bias"], sm[i]["hgrn_lb_raw"], sm[i]["hgrn_norm"],
                big_("w_ba"), big_("w_bh"), big_("w_out"), sm[i]["post_mix_norm"], sm[i]["pre_ffn_norm"],
                big_("w_up"), res_cw[i][None], sm[i]["conv_b"], big_("w_down"), sm[i]["post_ffn_norm"]]

    return (loss, grad_x[None], *pick(0), *pick(1), *pick(2), *pick(3))
```

```python
import functools
import math

import jax
import jax.numpy as jnp
from jax import lax
from jax.experimental import pallas as pl
from jax.experimental.pallas import tpu as pltpu

f32 = jnp.float32
bf16 = jnp.bfloat16
SDS = jax.ShapeDtypeStruct
HIGHEST = lax.Precision.HIGHEST
MESH = pl.DeviceIdType.MESH

NN = (((1,), (0,)), ((), ()))
NT = (((1,), (1,)), ((), ()))
TN = (((0,), (0,)), ((), ()))

D_MODEL = 1024
N_GROUPS = 3
DILATIONS = (1, 4, 16)
HEAD_DIM = 64
ATTN_BLOCK = 128
QKV_G = 1536
ATTN_OUT = 512
HGRN_W = 512
HGRN_CHUNK = 32
D_FF = 2816
NUM_BUCKETS = 32
MAX_EXACT = 16
MAX_DISTANCE = 2048
NEG_INF = -1e30
EPS = 1e-6
LANE = 128
SUBLANE = 8
VMEM_BIG = 48 * 1024 * 1024
MM_ROWS = 512
MM_OUT_BYTES = 8 * 1024 * 1024
TN_COLS = 512
ADAM_BLOCK_BYTES = 2304 * 1024

ADAM_LR, ADAM_B1, ADAM_B2, ADAM_EPS, ADAM_WD, ADAM_STEP = 0.001, 0.9, 0.999, 1e-08, 0.01, 10


def _pick(n, pref):
    t = pref
    while t >= LANE:
        if n % t == 0:
            return t
        t //= 2
    return n


def _cparams(sem=None, vmem=None):
    kw = {}
    if sem is not None:
        kw["dimension_semantics"] = sem
    if vmem is not None:
        kw["vmem_limit_bytes"] = vmem
    return pltpu.CompilerParams(**kw)


def _sigmoid(x):
    return jax.nn.sigmoid(x)


def _colsum8(x):
    return x.reshape(x.shape[0] // SUBLANE, SUBLANE, x.shape[1]).sum(axis=0)


class _Rows:
    def __init__(self, full, lo, rows):
        self.full, self.lo, self.shape = full, lo, tuple(full.shape[:-2]) + (rows, full.shape[-1])


def _resident(t):
    if isinstance(t, _Rows):
        return pl.BlockSpec((pl.Element(t.shape[0]), pl.Element(t.shape[1])), lambda i: (t.lo, 0)), t.full
    return pl.BlockSpec(t.shape, lambda i: (0, 0)), t


def _mm(a, b, mode, out_dtype, name, acc=None, after=None, into=None):
    dims = {"nn": NN, "nt": NT, "tn": TN}[mode]
    has_acc = acc is not None
    parts = list(a) if isinstance(a, (list, tuple)) else [a]
    if mode == "tn":
        assert not has_acc
        K, N = b.shape
        widths = [t.shape[1] for t in parts]
        M = sum(widths)
        whole = M * N * 4 <= MM_OUT_BYTES
        assert whole or len(parts) == 1
        tmm = M if whole else M // 2
        ts = _pick(K, 4 * MM_ROWS)
        nk = K // ts

        npart = len(parts)
        narrow = out_dtype != f32
        n_in = npart + (1 if into is None else 2)
        out_spec, out_shape, aliases, extra = pl.BlockSpec((tmm, N), lambda i, k: (i, 0)), SDS((M, N), out_dtype), {}, []
        if into is not None:
            slab, first, lo, shard_rows = into
            assert narrow and slab.dtype == out_dtype and tmm % shard_rows == 0 and slab.shape[2] == N
            per = tmm // shard_rows
            out_spec = pl.BlockSpec((pl.Element(per), pl.Element(shard_rows), pl.Element(N)),
                                    lambda i, k: (first + i * per, lo, 0))
            out_shape, aliases, extra = SDS(slab.shape, out_dtype), {npart + 1: 0}, [slab]

        def body_tn(*refs):
            b_ref, o_ref = refs[npart], refs[n_in]
            acc_ref = refs[n_in + 1] if narrow else o_ref
            k = pl.program_id(1)
            at = 0
            for a_ref, w in zip(refs[:npart], widths if whole else [tmm]):
                for c0 in range(0, w, TN_COLS):
                    cw = min(TN_COLS, w - c0)
                    part = lax.dot_general(a_ref[:, c0:c0 + cw], b_ref[...], dims, preferred_element_type=f32)
                    rows = slice(at + c0, at + c0 + cw)

                    @pl.when(k == 0)
                    def _(part=part, rows=rows):
                        acc_ref[rows, :] = part

                    @pl.when(k > 0)
                    def _(part=part, rows=rows):
                        acc_ref[rows, :] += part
                at += w

            if narrow:
                @pl.when(k == nk - 1)
                def _():
                    o_ref[...] = acc_ref[...].astype(out_dtype).reshape(o_ref.shape)

        return pl.pallas_call(
            body_tn,
            grid=(M // tmm, nk),
            in_specs=[pl.BlockSpec((ts, w if whole else tmm), lambda i, k: (k, i)) for w in widths]
            + [pl.BlockSpec((ts, N), lambda i, k: (k, 0))] + [pl.BlockSpec(memory_space=pl.ANY)] * len(extra),
            out_specs=out_spec,
            out_shape=out_shape,
            input_output_aliases=aliases,
            scratch_shapes=[pltpu.VMEM((tmm, N), f32)] if narrow else [],
            compiler_params=_cparams(("parallel", "arbitrary"), VMEM_BIG),
            name=name,
        )(*parts, b, *extra)

    bs = list(b) if isinstance(b, (list, tuple)) else [b]
    widths = [t.shape[1] for t in parts]
    M = parts[0].shape[0]
    kdim = 0 if mode == "nn" else 1
    N = bs[0].shape[1 - kdim]
    tm = _pick(M, MM_ROWS)
    npart, nb = len(parts), len(bs)
    place, bi, lo = [], 0, 0
    for w in widths:
        place.append((bi, lo))
        lo += w
        if lo == bs[bi].shape[kdim]:
            bi, lo = bi + 1, 0
    assert bi == nb and lo == 0

    def body(*refs):
        a_refs, b_refs = refs[:npart], refs[npart:npart + nb]
        c_ref = refs[npart + nb] if has_acc else None
        o_ref = refs[-1]
        part = None
        for a_ref, w, (bi, lo) in zip(a_refs, widths, place):
            b_ref = b_refs[bi]
            if w == bs[bi].shape[kdim]:
                bk = b_ref[...]
            else:
                bk = b_ref[:, lo:lo + w] if mode == "nt" else b_ref[lo:lo + w, :]
            t = lax.dot_general(a_ref[...], bk, dims, preferred_element_type=f32)
            part = t if part is None else part + t
        if has_acc:
            part = part + c_ref[...]
        o_ref[...] = part.astype(out_dtype)

    specs = [pl.BlockSpec((tm, w), lambda i: (i, 0)) for w in widths] + [_resident(t)[0] for t in bs]
    args = parts + [_resident(t)[1] for t in bs]
    aliases = {}
    if has_acc:
        specs.append(pl.BlockSpec((tm, N), lambda i: (i, 0)))
        args.append(acc)
        aliases = {npart + nb: 0}
    if after is not None:
        specs.append(pl.BlockSpec(memory_space=pl.ANY))
        args.append(after)
    return pl.pallas_call(
        body,
        grid=(M // tm,),
        in_specs=specs,
        out_specs=pl.BlockSpec((tm, N), lambda i: (i, 0)),
        out_shape=SDS((M, N), out_dtype),
        input_output_aliases=aliases,
        compiler_params=_cparams(("parallel",), VMEM_BIG),
        name=name,
    )(*args)


def _mm_fanout(a, bs, mode, out_dtypes, name):
    dims = {"nn": NN, "nt": NT}[mode]
    M, K = a.shape
    ns = [b.shape[1] if mode == "nn" else b.shape[0] for b in bs]
    tm = _pick(M, MM_ROWS)
    nb = len(bs)

    def body(a_ref, *refs):
        av = a_ref[...]
        for b_ref, o_ref, dt in zip(refs[:nb], refs[nb:], out_dtypes):
            o_ref[...] = lax.dot_general(av, b_ref[...], dims, preferred_element_type=f32).astype(dt)

    return pl.pallas_call(
        body,
        grid=(M // tm,),
        in_specs=[pl.BlockSpec((tm, K), lambda i: (i, 0))] + [_resident(b)[0] for b in bs],
        out_specs=[pl.BlockSpec((tm, n), lambda i: (i, 0)) for n in ns],
        out_shape=[SDS((M, n), dt) for n, dt in zip(ns, out_dtypes)],
        compiler_params=_cparams(("parallel",), VMEM_BIG),
        name=name,
    )(a, *[_resident(b)[1] for b in bs])


PERM_ROWS = 2048


def _perm_spec(d, cols=LANE):
    return pl.BlockSpec((d, PERM_ROWS // d, cols), lambda i, j: (0, i, j))


def _to_natural(src_ref, dst_ref, d):
    n = src_ref.shape[1]
    for r in range(d):
        dst_ref[pl.ds(r, n, stride=d), :] = src_ref[r]


def _prep(x, w, after=None):
    S, D = x.shape
    R = PERM_ROWS
    nc = D // LANE
    n_in = nc + 1 + (after is not None)

    def body(*refs):
        x_refs, w_ref = refs[:nc], refs[nc]
        h_ref, h4_ref, h16_ref, rs = refs[n_in:]
        ssq = None
        for xr in x_refs:
            v = xr[...]
            t = jnp.sum(v * v, axis=-1, keepdims=True)
            ssq = t if ssq is None else ssq + t
        rinv = lax.rsqrt(ssq * (1.0 / D) + EPS)
        rs[...] = jnp.broadcast_to(rinv, (R, LANE))
        for j, xr in enumerate(x_refs):
            cols = slice(j * LANE, (j + 1) * LANE)
            wj = w_ref[:, cols]
            h_ref[:, cols] = ((xr[...] * rinv) * wj).astype(bf16)
            for d, o_ref in ((4, h4_ref), (16, h16_ref)):
                n = R // d
                for r in range(d):
                    rows = pl.ds(r, n, stride=d)
                    o_ref[r, :, cols] = ((xr[rows, :] * rs[rows, :]) * wj).astype(bf16)

    col = lambda j: pl.BlockSpec((R, LANE), lambda i, j=j: (i, j))
    h, h4, h16 = pl.pallas_call(
        body,
        grid=(S // R,),
        in_specs=[col(j) for j in range(nc)] + [pl.BlockSpec((1, D), lambda i: (0, 0))]
        + ([] if after is None else [pl.BlockSpec(memory_space=pl.ANY)]),
        out_specs=[pl.BlockSpec((R, D), lambda i: (i, 0)), pl.BlockSpec((4, R // 4, D), lambda i: (0, i, 0)),
                   pl.BlockSpec((16, R // 16, D), lambda i: (0, i, 0))],
        out_shape=[SDS((S, D), bf16), SDS((4, S // 4, D), bf16), SDS((16, S // 16, D), bf16)],
        scratch_shapes=[pltpu.VMEM((R, LANE), f32)],
        compiler_params=_cparams(("parallel",), VMEM_BIG),
        name="prep_norm_perm",
    )(*([x] * nc), w, *([] if after is None else [after]))
    return [h, h4.reshape(S, D), h16.reshape(S, D)]


def _rms_parts(xv):
    r = lax.rsqrt(jnp.mean(xv * xv, axis=-1, keepdims=True) + EPS)
    return r, xv * r


def _rms_bwd(xhat, r, w, dy):
    dyw = dy * w
    return r * (dyw - xhat * jnp.mean(dyw * xhat, axis=-1, keepdims=True))


def _mid_fwd(x, merged, w_out, w_pm, w_pf):
    S, D = x.shape
    tm = _pick(S, MM_ROWS)

    def body(x_ref, m_ref, wo_ref, wpm_ref, wpf_ref, mo_ref, x1_ref, h2_ref):
        mo = jnp.dot(m_ref[...], wo_ref[...], preferred_element_type=f32)
        mo_ref[...] = mo
        _, moh = _rms_parts(mo)
        x1 = x_ref[...] + moh * wpm_ref[...]
        x1_ref[...] = x1
        _, x1h = _rms_parts(x1)
        h2_ref[...] = (x1h * wpf_ref[...]).astype(bf16)

    row = pl.BlockSpec((tm, D), lambda i: (i, 0))
    vec = pl.BlockSpec((1, D), lambda i: (0, 0))
    return pl.pallas_call(
        body,
        grid=(S // tm,),
        in_specs=[row, pl.BlockSpec((tm, merged.shape[1]), lambda i: (i, 0)),
                  pl.BlockSpec(w_out.shape, lambda i: (0, 0)), vec, vec],
        out_specs=[row, row, row],
        out_shape=[SDS((S, D), f32), SDS((S, D), f32), SDS((S, D), bf16)],
        compiler_params=_cparams(("parallel",), VMEM_BIG),
        name="out_proj_mid_fwd",
    )(x, merged, w_out, w_pm, w_pf)


def _final(x1, act, w_down, tgt, w_pfn):
    S, D = x1.shape
    tm = _pick(S, MM_ROWS)
    nt = S // tm

    def body(x1_ref, a_ref, wd_ref, t_ref, w_ref, loss_ref, dy_ref, dfo_ref, gw_ref, lacc, gacc):
        i = pl.program_id(0)

        @pl.when(i == 0)
        def _():
            lacc[...] = jnp.zeros_like(lacc)
            gacc[...] = jnp.zeros_like(gacc)

        w = w_ref[...]
        r, foh = _rms_parts(jnp.dot(a_ref[...], wd_ref[...], preferred_element_type=f32))
        y = x1_ref[...] + foh * w
        err = y - t_ref[...]
        lacc[...] += _colsum8(err * err)
        dy = err * (1.0 / D)
        dy_ref[...] = dy
        gacc[...] += _colsum8(dy * foh)
        dfo_ref[...] = _rms_bwd(foh, r, w, dy).astype(bf16)

        @pl.when(i == nt - 1)
        def _():
            loss_ref[...] = jnp.full((SUBLANE, LANE), 0.5 / D, f32) * jnp.sum(lacc[...])
            gw_ref[...] = jnp.sum(gacc[...], axis=0, keepdims=True)

    row = pl.BlockSpec((tm, D), lambda i: (i, 0))
    vec = pl.BlockSpec((1, D), lambda i: (0, 0))
    return pl.pallas_call(
        body,
        grid=(nt,),
        in_specs=[row, pl.BlockSpec((tm, act.shape[1]), lambda i: (i, 0)),
                  pl.BlockSpec(w_down.shape, lambda i: (0, 0)), row, vec],
        out_specs=[pl.BlockSpec((SUBLANE, LANE), lambda i: (0, 0)), row, row, vec],
        out_shape=[SDS((SUBLANE, LANE), f32), SDS((S, D), f32), SDS((S, D), bf16), SDS((1, D), f32)],
        scratch_shapes=[pltpu.VMEM((SUBLANE, D), f32), pltpu.VMEM((SUBLANE, D), f32)],
        compiler_params=_cparams(("arbitrary",), VMEM_BIG),
        name="down_proj_final_loss",
    )(x1, act, w_down, tgt, w_pfn)


MID_BWD_ROWS = 256


def _mid_bwd(dy, dug, duv, wt_g, wt_v, x1, mo, w_pf, w_pm):
    S, D = dy.shape
    tm = _pick(S, MID_BWD_ROWS)
    nt = S // tm

    def body(dy_ref, dug_ref, duv_ref, wg_ref, wv_ref, x1_ref, mo_ref, wpf_ref, wpm_ref,
             dx1_ref, dmo_ref, gpf_ref, gpm_ref, apf, apm):
        i = pl.program_id(0)

        @pl.when(i == 0)
        def _():
            apf[...] = jnp.zeros_like(apf)
            apm[...] = jnp.zeros_like(apm)

        r1, x1h = _rms_parts(x1_ref[...])
        dh2 = jnp.dot(dug_ref[...], wg_ref[...], preferred_element_type=f32) \
            + jnp.dot(duv_ref[...], wv_ref[...], preferred_element_type=f32)
        apf[...] += _colsum8(dh2 * x1h)
        dx1 = dy_ref[...] + _rms_bwd(x1h, r1, wpf_ref[...], dh2)
        dx1_ref[...] = dx1
        rm, moh = _rms_parts(mo_ref[...])
        apm[...] += _colsum8(dx1 * moh)
        dmo_ref[...] = _rms_bwd(moh, rm, wpm_ref[...], dx1).astype(bf16)

        @pl.when(i == nt - 1)
        def _():
            gpf_ref[...] = jnp.sum(apf[...], axis=0, keepdims=True)
            gpm_ref[...] = jnp.sum(apm[...], axis=0, keepdims=True)

    row = pl.BlockSpec((tm, D), lambda i: (i, 0))
    vec = pl.BlockSpec((1, D), lambda i: (0, 0))
    return pl.pallas_call(
        body,
        grid=(nt,),
        in_specs=[row, pl.BlockSpec((tm, dug.shape[1]), lambda i: (i, 0)), pl.BlockSpec((tm, duv.shape[1]), lambda i: (i, 0)),
                  pl.BlockSpec(wt_g.shape, lambda i: (0, 0)), pl.BlockSpec(wt_v.shape, lambda i: (0, 0)),
                  row, row, vec, vec],
        out_specs=[row, row, vec, vec],
        out_shape=[SDS((S, D), f32), SDS((S, D), bf16), SDS((1, D), f32), SDS((1, D), f32)],
        scratch_shapes=[pltpu.VMEM((SUBLANE, D), f32), pltpu.VMEM((SUBLANE, D), f32)],
        compiler_params=_cparams(("arbitrary",), VMEM_BIG),
        name="dh2_mid_bwd",
    )(dy, dug, duv, wt_g, wt_v, x1, mo, w_pf, w_pm)


def _first_bwd(x, dx1, dh_a, dh_b, dh_c, w_pre):
    S, D = x.shape
    tm = _pick(S, 512)
    nt = S // tm
    nc = D // LANE

    def body(*refs):
        x_ref, dx1_ref, a_ref = refs[:3]
        b_refs, c_refs, w_ref = refs[3:3 + nc], refs[3 + nc:3 + 2 * nc], refs[3 + 2 * nc]
        gx_ref, gw_ref, acc, dh_s, sb, sc = refs[4 + 2 * nc:]
        i = pl.program_id(0)

        @pl.when(i == 0)
        def _():
            acc[...] = jnp.zeros_like(acc)

        for j in range(nc):
            cols = slice(j * LANE, (j + 1) * LANE)
            _to_natural(b_refs[j], sb, 4)
            _to_natural(c_refs[j], sc, 16)
            dh_s[:, cols] = (a_ref[:, cols] + sb[...]) + sc[...]
        r, xh = _rms_parts(x_ref[...])
        dh = dh_s[...]
        acc[...] += _colsum8(dh * xh)
        gx_ref[...] = dx1_ref[...] + _rms_bwd(xh, r, w_ref[...], dh)

        @pl.when(i == nt - 1)
        def _():
            gw_ref[...] = jnp.sum(acc[...], axis=0, keepdims=True)

    row = pl.BlockSpec((tm, D), lambda i: (i, 0))
    vec = pl.BlockSpec((1, D), lambda i: (0, 0))
    perm = lambda d: [pl.BlockSpec((d, tm // d, LANE), lambda i, j=j: (0, i, j)) for j in range(nc)]
    return pl.pallas_call(
        body,
        grid=(nt,),
        in_specs=[row, row, row] + perm(4) + perm(16) + [vec],
        out_specs=[row, vec],
        out_shape=[SDS((S, D), f32), SDS((1, D), f32)],
        scratch_shapes=[pltpu.VMEM((SUBLANE, D), f32), pltpu.VMEM((tm, D), f32), pltpu.VMEM((tm, LANE), f32),
                        pltpu.VMEM((tm, LANE), f32)],
        compiler_params=_cparams(("arbitrary",), VMEM_BIG),
        name="first_bwd",
    )(x, dx1, dh_a, *([dh_b.reshape(4, S // 4, D)] * nc), *([dh_c.reshape(16, S // 16, D)] * nc), w_pre)


def _t5_bucket(dist):
    n = jnp.maximum(dist, 0)
    nf = jnp.maximum(n, 1).astype(f32)
    large = MAX_EXACT + (jnp.log(nf / MAX_EXACT) / math.log(MAX_DISTANCE / MAX_EXACT)
                         * (NUM_BUCKETS - MAX_EXACT)).astype(jnp.int32)
    large = jnp.minimum(large, NUM_BUCKETS - 1)
    return jnp.where(n < MAX_EXACT, n, large)


def _bias_consts(d, after=None):
    if after is not None:
        d, _ = lax.optimization_barrier((jnp.int32(d), after))
    blk = ATTN_BLOCK
    rel = jnp.arange(blk)[:, None] + blk - jnp.arange(2 * blk)[None, :]
    in_win = (rel >= 0) & (rel <= blk)
    bucket = _t5_bucket(rel * d).reshape(1, -1)
    onehot = (bucket == jnp.arange(NUM_BUCKETS)[:, None]).astype(f32)
    return onehot, in_win.astype(f32).reshape(1, -1)


def _bias_build(tab_t, onehot, maskf, name, after):
    H = tab_t.shape[0]

    def body(t_ref, oh_ref, m_ref, after_ref, o_ref):
        b = jnp.dot(t_ref[...], oh_ref[...], precision=HIGHEST, preferred_element_type=f32)
        o_ref[...] = jnp.where(m_ref[...] > 0.5, b, NEG_INF)

    vm = pl.BlockSpec(memory_space=pltpu.VMEM)
    return pl.pallas_call(body, out_shape=SDS((H, onehot.shape[1]), f32), name=name,
                          in_specs=[vm, vm, vm, pl.BlockSpec(memory_space=pl.ANY)], out_specs=vm,
                          )(tab_t, onehot, maskf, after)


def _bias_grad(dbias_flat, onehot, name):
    H = dbias_flat.shape[0]

    def body(g_ref, oh_ref, o_ref):
        o_ref[...] = lax.dot_general(oh_ref[...], g_ref[...], NT, precision=HIGHEST, preferred_element_type=f32)

    return pl.pallas_call(body, out_shape=SDS((NUM_BUCKETS, H), f32), name=name)(dbias_flat, onehot)


ATTN_TILE = 512
ATTN_SUB = ATTN_TILE // ATTN_BLOCK
ATTN_HP = 4
ATTN_WIDE = ATTN_HP * LANE


def _qkv_specs(nt):
    tile = (ATTN_TILE, ATTN_WIDE)
    blk = (ATTN_BLOCK, ATTN_WIDE)
    sec = ATTN_OUT // ATTN_WIDE
    cur = lambda off: (lambda h, t: (jnp.minimum(t, nt - 1), off + h))
    prev = lambda off: (lambda h, t: (jnp.maximum(jnp.minimum(t, nt - 1) * ATTN_SUB - 1, 0), off + h))
    return [pl.BlockSpec(tile, cur(0)), pl.BlockSpec(blk, prev(sec)), pl.BlockSpec(tile, cur(sec)),
            pl.BlockSpec(blk, prev(2 * sec)), pl.BlockSpec(tile, cur(2 * sec))]


def _head_masks():
    lane = lax.broadcasted_iota(jnp.int32, (ATTN_BLOCK, LANE), 1)
    return lane < HEAD_DIM


def _stack_heads(x2, low):
    zero = jnp.zeros_like(x2)
    return jnp.concatenate([jnp.where(low, x2, zero), jnp.where(low, zero, x2)], axis=0)


def _attn_fwd(qkv, bias, bps, name, after=None):
    S = qkv.shape[0]
    nt = S // ATTN_TILE
    scale = HEAD_DIM ** -0.5

    def body(q_ref, kp_ref, kc_ref, vp_ref, vc_ref, b_ref, *rest):
        o_ref, l_ref = rest[-2:]
        t = pl.program_id(1)
        low = _head_masks()
        col = lax.broadcasted_iota(jnp.int32, (2 * ATTN_BLOCK, 2 * ATTN_BLOCK), 1)
        for hp in range(ATTN_HP):
            cols = slice(hp * LANE, (hp + 1) * LANE)
            kk = jnp.concatenate([kp_ref[:, cols], kc_ref[:, cols]], axis=0)
            vv = jnp.concatenate([vp_ref[:, cols], vc_ref[:, cols]], axis=0)
            bias2 = b_ref[2 * hp:2 * hp + 2].reshape(2 * ATTN_BLOCK, 2 * ATTN_BLOCK)
            for b in range(ATTN_SUB):
                lo = b * ATTN_BLOCK
                rows = slice(lo, lo + ATTN_BLOCK)
                keys = slice(lo, lo + 2 * ATTN_BLOCK)
                dead = jnp.logical_and((t * ATTN_SUB + b) % bps == 0, col < ATTN_BLOCK)
                q2 = _stack_heads(q_ref[rows, cols], low)
                kb, vb = kk[keys], vv[keys]
                s = lax.dot_general(q2, kb, NT, preferred_element_type=f32) * scale + bias2
                s = jnp.where(dead, NEG_INF, s)
                m = jnp.max(s, axis=-1, keepdims=True)
                p = jnp.exp(s - m)
                l = jnp.sum(p, axis=-1, keepdims=True)
                o2 = jnp.dot(p.astype(bf16), vb, preferred_element_type=f32) / l
                lse = m + jnp.log(l)
                o_ref[rows, cols] = jnp.where(low, o2[:ATTN_BLOCK], o2[ATTN_BLOCK:])
                l_ref[rows, cols] = jnp.where(low, lse[:ATTN_BLOCK], lse[ATTN_BLOCK:])

    tile = pl.BlockSpec((ATTN_TILE, ATTN_WIDE), lambda h, t: (t, h))
    return pl.pallas_call(
        body,
        grid=(4 // ATTN_HP, nt),
        in_specs=_qkv_specs(nt) + [pl.BlockSpec((2 * ATTN_HP, ATTN_BLOCK, 2 * ATTN_BLOCK), lambda h, t: (h, 0, 0))]
        + ([] if after is None else [pl.BlockSpec(memory_space=pl.ANY)]),
        out_specs=[tile, tile],
        out_shape=[SDS((S, ATTN_OUT), f32), SDS((S, ATTN_OUT), f32)],
        compiler_params=_cparams(("parallel", "parallel")),
        name=name,
    )(qkv, qkv, qkv, qkv, qkv, bias, *([] if after is None else [after]))


def _attn_bwd(qkv, bias, do, dvec, lse, bps, name):
    S = qkv.shape[0]
    nt = S // ATTN_TILE
    scale = HEAD_DIM ** -0.5

    def assemble(parts):
        rows = [parts[0][:ATTN_BLOCK]]
        for b in range(ATTN_SUB - 1):
            rows.append(parts[b][ATTN_BLOCK:] + parts[b + 1][:ATTN_BLOCK])
        rows.append(parts[-1][ATTN_BLOCK:])
        return rows

    def body(q_ref, kp_ref, kc_ref, vp_ref, vc_ref, b_ref, do_ref, dvec_ref, lse_ref,
             dq_ref, dk_ref, dv_ref, db_ref, ck, cv):
        t = pl.program_id(1)
        last = ATTN_TILE - ATTN_BLOCK

        @pl.when(t == 0)
        def _():
            ck[...] = jnp.zeros_like(ck)
            cv[...] = jnp.zeros_like(cv)
            db_ref[...] = jnp.zeros_like(db_ref)

        @pl.when(t < nt)
        def _():
            low = _head_masks()
            col = lax.broadcasted_iota(jnp.int32, (2 * ATTN_BLOCK, 2 * ATTN_BLOCK), 1)
            per_row = lambda t2: jnp.concatenate([t2[:, 0:1], t2[:, HEAD_DIM:HEAD_DIM + 1]], axis=0)
            for hp in range(ATTN_HP):
                cols = slice(hp * LANE, (hp + 1) * LANE)
                kk = jnp.concatenate([kp_ref[:, cols], kc_ref[:, cols]], axis=0)
                vv = jnp.concatenate([vp_ref[:, cols], vc_ref[:, cols]], axis=0)
                bias2 = b_ref[2 * hp:2 * hp + 2].reshape(2 * ATTN_BLOCK, 2 * ATTN_BLOCK)
                dk_parts, dv_parts = [], []
                dsum = None
                for b in range(ATTN_SUB):
                    lo = b * ATTN_BLOCK
                    rows = slice(lo, lo + ATTN_BLOCK)
                    keys = slice(lo, lo + 2 * ATTN_BLOCK)
                    dead = jnp.logical_and((t * ATTN_SUB + b) % bps == 0, col < ATTN_BLOCK)
                    q2 = _stack_heads(q_ref[rows, cols], low)
                    do2 = _stack_heads(do_ref[rows, cols].astype(bf16), low)
                    kb, vb = kk[keys], vv[keys]
                    s = lax.dot_general(q2, kb, NT, preferred_element_type=f32) * scale + bias2
                    s = jnp.where(dead, NEG_INF, s)
                    p = jnp.exp(s - per_row(lse_ref[rows, cols]))
                    dp = lax.dot_general(do2, vb, NT, preferred_element_type=f32)
                    ds = p * (dp - per_row(dvec_ref[rows, cols]))
                    dsum = ds if dsum is None else dsum + ds
                    dsb = ds.astype(bf16)
                    dq2 = jnp.dot(dsb, kb, preferred_element_type=f32) * scale
                    dq_ref[rows, cols] = jnp.where(low, dq2[:ATTN_BLOCK], dq2[ATTN_BLOCK:]).astype(bf16)
                    dk_parts.append(lax.dot_general(dsb, q2, TN, preferred_element_type=f32) * scale)
                    dv_parts.append(lax.dot_general(p.astype(bf16), do2, TN, preferred_element_type=f32))
                db_ref[2 * hp:2 * hp + 2] += dsum.reshape(2, ATTN_BLOCK, 2 * ATTN_BLOCK)
                for parts, carry, out_ref in ((dk_parts, ck, dk_ref), (dv_parts, cv, dv_ref)):
                    rws = assemble(parts)
                    out_ref[:last, cols] = carry[:last, cols].astype(bf16)
                    out_ref[last:, cols] = (carry[last:, cols] + rws[0]).astype(bf16)
                    for b in range(ATTN_SUB):
                        carry[b * ATTN_BLOCK:(b + 1) * ATTN_BLOCK, cols] = rws[b + 1]

        @pl.when(t == nt)
        def _():
            dk_ref[...] = ck[...].astype(bf16)
            dv_ref[...] = cv[...].astype(bf16)

    tile = (ATTN_TILE, ATTN_WIDE)
    cur = pl.BlockSpec(tile, lambda h, t: (jnp.minimum(t, nt - 1), h))
    lag = pl.BlockSpec(tile, lambda h, t: (jnp.maximum(t - 1, 0), h))
    bspec = pl.BlockSpec((2 * ATTN_HP, ATTN_BLOCK, 2 * ATTN_BLOCK), lambda h, t: (h, 0, 0))
    return pl.pallas_call(
        body,
        grid=(4 // ATTN_HP, nt + 1),
        in_specs=_qkv_specs(nt) + [bspec, cur, cur, cur],
        out_specs=[cur, lag, lag, bspec],
        out_shape=[SDS((S, ATTN_OUT), bf16), SDS((S, ATTN_OUT), bf16), SDS((S, ATTN_OUT), bf16),
                   SDS((8, ATTN_BLOCK, 2 * ATTN_BLOCK), f32)],
        scratch_shapes=[pltpu.VMEM(tile, f32), pltpu.VMEM(tile, f32)],
        compiler_params=_cparams(("parallel", "arbitrary")),
        name=name,
    )(qkv, qkv, qkv, qkv, qkv, bias, do, dvec, lse)


def _attn_merge(o0, o1, o2, l0, l1, l2):
    S, W = o0.shape
    R = PERM_ROWS

    def body(o0_ref, o1_ref, o2_ref, l0_ref, l1_ref, l2_ref, y_ref, yb_ref, w0_ref, w1_ref, w2_ref,
             so1, so2, sl1, sl2):
        _to_natural(o1_ref, so1, 4)
        _to_natural(l1_ref, sl1, 4)
        _to_natural(o2_ref, so2, 16)
        _to_natural(l2_ref, sl2, 16)
        a, b, c = l0_ref[...], sl1[...], sl2[...]
        m = jnp.maximum(jnp.maximum(a, b), c)
        ea, eb, ec = jnp.exp(a - m), jnp.exp(b - m), jnp.exp(c - m)
        den = (ea + eb) + ec
        w0, w1, w2 = ea / den, eb / den, ec / den
        y = (w0 * o0_ref[...] + w1 * so1[...]) + w2 * so2[...]
        y_ref[...] = y
        yb_ref[...] = y.astype(bf16)
        w0_ref[...] = w0
        w1_ref[...] = w1
        w2_ref[...] = w2

    nat = pl.BlockSpec((R, LANE), lambda i, j: (i, j))
    v4 = lambda t: t.reshape(4, S // 4, W)
    v16 = lambda t: t.reshape(16, S // 16, W)
    return pl.pallas_call(
        body,
        grid=(S // R, W // LANE),
        in_specs=[nat, _perm_spec(4), _perm_spec(16)] * 2,
        out_specs=[nat] * 5,
        out_shape=[SDS((S, W), f32), SDS((S, W), bf16)] + [SDS((S, W), f32)] * 3,
        scratch_shapes=[pltpu.VMEM((R, LANE), f32)] * 4,
        compiler_params=_cparams(("parallel", "parallel"), VMEM_BIG),
        name="attn_merge",
    )(o0, v4(o1), v16(o2), l0, v4(l1), v16(l2))


def _attn_merge_bwd(dy, y, w0, w1, w2, after=None):
    S, W = dy.shape
    R = PERM_ROWS

    def body(dy_ref, y_ref, w0_ref, w1_ref, w2_ref, *rest):
        a0, a1, a2, b0, b1, b2, sa, sb = rest[-8:]
        dyv = dy_ref[...]
        r = lax.broadcasted_iota(jnp.int32, (LANE, LANE), 0) // HEAD_DIM
        c = lax.broadcasted_iota(jnp.int32, (LANE, LANE), 1) // HEAD_DIM
        seg = jnp.where(r == c, 1.0, 0.0).astype(f32)
        cbar = jnp.dot(dyv * y_ref[...], seg, precision=HIGHEST, preferred_element_type=f32)
        w = w0_ref[...]
        a0[...] = (w * dyv).astype(bf16)
        b0[...] = w * cbar
        for d, w_ref, a_ref, b_ref in ((4, w1_ref, a1, b1), (16, w2_ref, a2, b2)):
            w = w_ref[...]
            sa[...] = w * dyv
            sb[...] = w * cbar
            n = R // d
            for k in range(d):
                rows = pl.ds(k, n, stride=d)
                a_ref[k] = sa[rows, :].astype(bf16)
                b_ref[k] = sb[rows, :]

    nat = pl.BlockSpec((R, LANE), lambda i, j: (i, j))
    shapes = lambda dt: [SDS((S, W), dt), SDS((4, S // 4, W), dt), SDS((16, S // 16, W), dt)]
    outs = pl.pallas_call(
        body,
        grid=(S // R, W // LANE),
        in_specs=[nat] * 5 + ([] if after is None else [pl.BlockSpec(memory_space=pl.ANY)]),
        out_specs=[nat, _perm_spec(4), _perm_spec(16)] * 2,
        out_shape=shapes(bf16) + shapes(f32),
        scratch_shapes=[pltpu.VMEM((R, LANE), f32)] * 2,
        compiler_params=_cparams(("parallel", "parallel"), VMEM_BIG),
        name="attn_merge_bwd",
    )(dy, y, w0, w1, w2, *([] if after is None else [after]))
    return [t.reshape(S, W) for t in outs]


HGRN_SB = 256
HGRN_PAIR = 4


def _chunk_masks():
    r = jnp.arange(HGRN_SB)[:, None]
    c = jnp.arange(HGRN_SB)[None, :]
    same = (r // HGRN_CHUNK) == (c // HGRN_CHUNK)
    return jnp.stack([same & (c <= r), same, same & (c >= r)]).astype(bf16)


def _mask_dot(mask, x):
    hi = x.astype(bf16)
    r1 = x - hi.astype(f32)
    mid = r1.astype(bf16)
    lo = (r1 - mid.astype(f32)).astype(bf16)
    p = jnp.dot(mask, jnp.concatenate([hi, mid, lo], axis=1), preferred_element_type=f32)
    n = x.shape[1]
    return (p[:, :n] + p[:, n:2 * n]) + p[:, 2 * n:]


def _hgrn_prep(q_raw, f_raw, lbv, tril, same):
    sq = _sigmoid(q_raw)
    qs = q_raw * sq
    sig = _sigmoid(f_raw)
    f = lbv + (1.0 - lbv) * sig
    g = jnp.log(f)
    k = 1.0 - f
    G = _mask_dot(tril, g)
    GL = _mask_dot(same, g)
    eG = jnp.exp(G)
    einv = jnp.exp(-G)
    edec = jnp.exp(GL - G)
    return dict(sq=sq, qs=qs, sig=sig, f=f, k=k, eG=eG, einv=einv, edec=edec, eGL=jnp.exp(GL),
                qt=qs * eG, kt=k * einv, kd=k * edec)


def _hgrn_fwd(hg, lb, normw):
    S = hg.shape[0]
    sb = HGRN_SB
    nsb = S // sb
    nch = sb // HGRN_CHUNK

    def body(q_ref, f_ref, v_ref, og_ref, lb_ref, nw_ref, m_ref, y_ref, o_ref, ck_ref, st):
        j = pl.program_id(1)

        @pl.when(j == 0)
        def _():
            st[...] = jnp.zeros_like(st)

        tril_m = m_ref[0]
        tril = tril_m.astype(f32) > 0.5

        def one_head(hh):
            cols = slice(hh * LANE, (hh + 1) * LANE)
            ST = st[hh]
            ck_ref[hh, 0] = ST
            pr = _hgrn_prep(q_ref[:, cols], f_ref[:, cols], lb_ref[:, cols], tril_m, m_ref[1])
            qtb, ktb, kdb = pr["qt"].astype(bf16), pr["kt"].astype(bf16), pr["kd"].astype(bf16)
            eGL = pr["eGL"]
            vb = v_ref[:, cols].astype(bf16)
            A = jnp.where(tril, lax.dot_general(qtb, ktb, NT, preferred_element_type=f32), 0.0)
            o = jnp.dot(A.astype(bf16), vb, preferred_element_type=f32)
            outs = []
            for ci in range(nch):
                lo = ci * HGRN_CHUNK
                sl = slice(lo, lo + HGRN_CHUNK)
                outs.append(o[sl] + lax.dot_general(qtb[sl], ST.astype(bf16), NT, preferred_element_type=f32))
                ST = ST * eGL[lo:lo + 1, :] + lax.dot_general(vb[sl], kdb[sl], TN, preferred_element_type=f32)
            st[hh] = ST
            of = jnp.concatenate(outs, axis=0)
            o_ref[:, cols] = of
            rms = lax.rsqrt(jnp.mean(of * of, axis=-1, keepdims=True) + EPS)
            ogv = og_ref[:, cols]
            y_ref[:, cols] = ((of * rms * nw_ref[...]) * (ogv * _sigmoid(ogv))).astype(bf16)

        for hh in range(HGRN_PAIR):
            one_head(hh)

    wide = HGRN_PAIR * LANE
    col = lambda off: pl.BlockSpec((sb, wide), lambda h, j: (j, off // HGRN_PAIR + h))
    return pl.pallas_call(
        body,
        grid=(4 // HGRN_PAIR, nsb),
        in_specs=[col(0), col(4), col(8), col(12), pl.BlockSpec((1, wide), lambda h, j: (0, h)),
                  pl.BlockSpec((1, LANE), lambda h, j: (0, 0)),
                  pl.BlockSpec((3, sb, sb), lambda h, j: (0, 0, 0))],
        out_specs=[col(0), col(0), pl.BlockSpec((HGRN_PAIR, 1, LANE, LANE), lambda h, j: (h, j, 0, 0))],
        out_shape=[SDS((S, HGRN_W), bf16), SDS((S, HGRN_W), f32), SDS((4, nsb, LANE, LANE), f32)],
        scratch_shapes=[pltpu.VMEM((HGRN_PAIR, LANE, LANE), f32)],
        compiler_params=_cparams(("parallel", "arbitrary")),
        name="hgrn_fwd",
    )(hg, hg, hg, hg, lb, normw, _chunk_masks())


def _hgrn_bwd(hg, o_raw, dy, ck, lb, normw, after=None):
    S = hg.shape[0]
    sb = HGRN_SB
    nsb = S // sb
    nch = sb // HGRN_CHUNK

    def body(q_ref, f_ref, v_ref, og_ref, o_ref, dy_ref, ck_ref, lb_ref, nw_ref, m_ref, *rest):
        dq_ref, df_ref, dv_ref, dog_ref, glb_ref, gnw_ref, dst, alb, anw = rest[-9:]
        j = pl.program_id(1)

        @pl.when(j == 0)
        def _():
            dst[...] = jnp.zeros_like(dst)
            alb[...] = jnp.zeros_like(alb)
            anw[...] = jnp.zeros_like(anw)

        tril_m = m_ref[0]
        tril = tril_m.astype(f32) > 0.5
        nw = nw_ref[...]

        def one_head(hh):
            cols = slice(hh * LANE, (hh + 1) * LANE)
            lbv = lb_ref[:, cols]
            q_raw = q_ref[:, cols]
            pr = _hgrn_prep(q_raw, f_ref[:, cols], lbv, tril_m, m_ref[1])
            qt, kt, kd, eGL = pr["qt"], pr["kt"], pr["kd"], pr["eGL"]
            qtb, ktb, kdb = qt.astype(bf16), kt.astype(bf16), kd.astype(bf16)
            vb = v_ref[:, cols].astype(bf16)

            o = o_ref[:, cols]
            ogv = og_ref[:, cols]
            sog = _sigmoid(ogv)
            rms = lax.rsqrt(jnp.mean(o * o, axis=-1, keepdims=True) + EPS)
            oh = o * rms
            dyv = dy_ref[:, cols]
            dog_ref[:, cols] = (dyv * (oh * nw) * (sog * (1.0 + ogv * (1.0 - sog)))).astype(bf16)
            dohw = dyv * (ogv * sog)
            anw[:, cols] += _colsum8(dohw * oh)
            doh = dohw * nw
            do = rms * (doh - oh * jnp.mean(doh * oh, axis=-1, keepdims=True))
            dob = do.astype(bf16)

            Ab = jnp.where(tril, lax.dot_general(qtb, ktb, NT, preferred_element_type=f32), 0.0).astype(bf16)
            dAb = jnp.where(tril, lax.dot_general(dob, vb, NT, preferred_element_type=f32), 0.0).astype(bf16)
            dv_acc = lax.dot_general(Ab, dob, TN, preferred_element_type=f32)
            dqt = jnp.dot(dAb, ktb, preferred_element_type=f32)
            dkt = lax.dot_general(dAb, qtb, TN, preferred_element_type=f32)

            ST = ck_ref[hh, 0]
            states = []
            for ci in range(nch):
                lo = ci * HGRN_CHUNK
                sl = slice(lo, lo + HGRN_CHUNK)
                states.append(ST)
                ST = ST * eGL[lo:lo + 1, :] + lax.dot_general(vb[sl], kdb[sl], TN, preferred_element_type=f32)

            dST = dst[hh]
            dqt_i, dkd_i, dv_i, deg_i = [None] * nch, [None] * nch, [None] * nch, [None] * nch
            for ci in reversed(range(nch)):
                lo = ci * HGRN_CHUNK
                sl = slice(lo, lo + HGRN_CHUNK)
                ST0 = states[ci]
                dSTb = dST.astype(bf16)
                dv_i[ci] = lax.dot_general(kdb[sl], dSTb, NT, preferred_element_type=f32)
                dqt_i[ci] = jnp.dot(dob[sl], ST0.astype(bf16), preferred_element_type=f32)
                dkd_i[ci] = jnp.dot(vb[sl], dSTb, preferred_element_type=f32)
                deg_i[ci] = jnp.broadcast_to(jnp.sum(dST * ST0, axis=0, keepdims=True), (HGRN_CHUNK, LANE))
                dST = dST * eGL[lo:lo + 1, :] + lax.dot_general(dob[sl], qtb[sl], TN, preferred_element_type=f32)
            dst[hh] = dST

            dqt = dqt + jnp.concatenate(dqt_i, axis=0)
            dkd = jnp.concatenate(dkd_i, axis=0)
            dv_ref[:, cols] = (dv_acc + jnp.concatenate(dv_i, axis=0)).astype(bf16)
            deg = jnp.concatenate(deg_i, axis=0)

            dqs = dqt * pr["eG"]
            dkdkd = dkd * kd
            dG = dqt * qt - dkt * kt - dkdkd
            dk = dkt * pr["einv"] + dkd * pr["edec"]
            dGL = _mask_dot(m_ref[1], dkdkd) + eGL * deg
            dg = _mask_dot(m_ref[2], dG) + dGL
            df = dg / pr["f"] - dk
            sig = pr["sig"]
            df_ref[:, cols] = (df * (1.0 - lbv) * (sig * (1.0 - sig))).astype(bf16)
            alb[:, cols] += _colsum8(df * (1.0 - sig))
            sq = pr["sq"]
            dq_ref[:, cols] = (dqs * (sq * (1.0 + q_raw * (1.0 - sq)))).astype(bf16)

        for hh in range(HGRN_PAIR):
            one_head(hh)

        @pl.when(j == nsb - 1)
        def _():
            glb_ref[...] = jnp.broadcast_to(jnp.sum(alb[...], axis=0, keepdims=True), (SUBLANE, wide))
            gnw_ref[...] = jnp.broadcast_to(jnp.sum(anw[...], axis=0, keepdims=True), (SUBLANE, wide))

    wide = HGRN_PAIR * LANE
    rev = lambda off: pl.BlockSpec((sb, wide), lambda h, j: (nsb - 1 - j, off // HGRN_PAIR + h))
    stat = pl.BlockSpec((SUBLANE, wide), lambda h, j: (0, h))
    return pl.pallas_call(
        body,
        grid=(4 // HGRN_PAIR, nsb),
        in_specs=[rev(0), rev(4), rev(8), rev(12), rev(0), rev(0),
                  pl.BlockSpec((HGRN_PAIR, 1, LANE, LANE), lambda h, j: (h, nsb - 1 - j, 0, 0)),
                  pl.BlockSpec((1, wide), lambda h, j: (0, h)), pl.BlockSpec((1, LANE), lambda h, j: (0, 0)),
                  pl.BlockSpec((3, sb, sb), lambda h, j: (0, 0, 0))]
        + ([] if after is None else [pl.BlockSpec(memory_space=pl.ANY)]),
        out_specs=[rev(0), rev(0), rev(0), rev(0), stat, stat],
        out_shape=[SDS((S, HGRN_W), bf16)] * 4 + [SDS((SUBLANE, HGRN_W), f32)] * 2,
        scratch_shapes=[pltpu.VMEM((HGRN_PAIR, LANE, LANE), f32), pltpu.VMEM((SUBLANE, wide), f32),
                        pltpu.VMEM((SUBLANE, wide), f32)],
        compiler_params=_cparams(("parallel", "arbitrary")),
        name="hgrn_bwd",
    )(hg, hg, hg, hg, o_raw, dy, ck, lb, normw, _chunk_masks(), *([] if after is None else [after]))


def _lb_fwd(raw):
    def body(r_ref, o_ref):
        r = r_ref[...]
        m = jnp.max(r, axis=0, keepdims=True)
        e = jnp.exp(r - m)
        o_ref[...] = (e / jnp.sum(e, axis=0, keepdims=True))[0:1]

    return pl.pallas_call(body, out_shape=SDS((1, raw.shape[1]), f32), name="lb_fwd")(raw)


def _lb_bwd(raw, dlb):
    def body(r_ref, d_ref, o_ref):
        r = r_ref[...]
        m = jnp.max(r, axis=0, keepdims=True)
        e = jnp.exp(r - m)
        s = e / jnp.sum(e, axis=0, keepdims=True)
        s0 = s[0:1]
        onehot0 = jnp.where(lax.broadcasted_iota(jnp.int32, r.shape, 0) == 0, 1.0, 0.0)
        o_ref[...] = d_ref[...] * s0 * (onehot0 - s)

    return pl.pallas_call(body, out_shape=SDS(raw.shape, f32), name="lb_bwd")(raw, dlb)


def _gate_fwd(ya, yh, w_ba, w_bh, gc):
    S = ya.shape[0]
    D = w_ba.shape[1]
    tm = _pick(S, MM_ROWS)

    def body(ya_ref, yh_ref, wa_ref, wh_ref, g0_ref, g1_ref, a_ref, b_ref, o_ref):
        a = jnp.dot(ya_ref[...], wa_ref[...], preferred_element_type=f32).astype(bf16)
        b = jnp.dot(yh_ref[...], wh_ref[...], preferred_element_type=f32).astype(bf16)
        a_ref[...] = a
        b_ref[...] = b
        s0, s1 = _sigmoid(g0_ref[...].astype(f32)), _sigmoid(g1_ref[...].astype(f32))
        o_ref[...] = (s0 * a.astype(f32) + s1 * b.astype(f32)).astype(bf16)

    row = pl.BlockSpec((tm, D), lambda i: (i, 0))
    act = pl.BlockSpec((tm, ya.shape[1]), lambda i: (i, 0))
    wspec = pl.BlockSpec(w_ba.shape, lambda i: (0, 0))
    return pl.pallas_call(
        body,
        grid=(S // tm,),
        in_specs=[act, act, wspec, wspec, row, pl.BlockSpec((tm, D), lambda i: (i, 1))],
        out_specs=[row, row, row],
        out_shape=[SDS((S, D), bf16)] * 3,
        compiler_params=_cparams(("parallel",), VMEM_BIG),
        name="branch_gate_fwd",
    )(ya, yh, w_ba, w_bh, gc, gc)


def _gate_bwd(dmo, w_out, a, b, gc, w_ba, w_bh):
    S, D = a.shape
    W = w_ba.shape[0]
    tm = _pick(S, MM_ROWS)

    def body(dmo_ref, wo_ref, a_ref, b_ref, g0_ref, g1_ref, wa_ref, wh_ref,
             da_ref, db_ref, dg_ref, dya_ref, dyh_ref):
        dm = lax.dot_general(dmo_ref[...], wo_ref[...], NT, preferred_element_type=f32)
        dmv = dm.astype(bf16).astype(f32)
        s0, s1 = _sigmoid(g0_ref[...].astype(f32)), _sigmoid(g1_ref[...].astype(f32))
        da = (dmv * s0).astype(bf16)
        db = (dmv * s1).astype(bf16)
        da_ref[...] = da
        db_ref[...] = db
        dg_ref[:, :D] = (dmv * a_ref[...].astype(f32) * (s0 * (1.0 - s0))).astype(bf16)
        dg_ref[:, D:] = (dmv * b_ref[...].astype(f32) * (s1 * (1.0 - s1))).astype(bf16)
        dya_ref[...] = lax.dot_general(da, wa_ref[...], NT, preferred_element_type=f32)
        dyh_ref[...] = lax.dot_general(db, wh_ref[...], NT, preferred_element_type=f32)

    row = pl.BlockSpec((tm, D), lambda i: (i, 0))
    wide = pl.BlockSpec((tm, 2 * D), lambda i: (i, 0))
    narrow = pl.BlockSpec((tm, W), lambda i: (i, 0))
    whole = lambda t: pl.BlockSpec(t.shape, lambda i: (0, 0))
    return pl.pallas_call(
        body,
        grid=(S // tm,),
        in_specs=[row, whole(w_out), row, row, row, pl.BlockSpec((tm, D), lambda i: (i, 1)), whole(w_ba), whole(w_bh)],
        out_specs=[row, row, wide, narrow, narrow],
        out_shape=[SDS((S, D), bf16), SDS((S, D), bf16), SDS((S, 2 * D), bf16), SDS((S, W), f32), SDS((S, W), f32)],
        compiler_params=_cparams(("parallel",), VMEM_BIG),
        name="gate_bwd_fused",
    )(dmo, w_out, a, b, gc, gc, w_ba, w_bh)


CONV_ROWS = 512
INV_SQRT2 = 0.7071067811865476
INV_SQRT_2PI = 0.3989422804014327


CONV_HALO = 16


def _shift_down(cur, prev, k):
    x = pltpu.roll(cur, k, 0)
    row = lax.broadcasted_iota(jnp.int32, (SUBLANE, LANE), 0)
    head = jnp.where(row < k, pltpu.roll(prev, k, 0)[:SUBLANE], x[:SUBLANE])
    return jnp.concatenate([head, x[SUBLANE:]], axis=0)


def _shift_up(cur, nxt, k):
    R = cur.shape[0]
    x = pltpu.roll(cur, R - k, 0)
    row = lax.broadcasted_iota(jnp.int32, (SUBLANE, LANE), 0)
    tail = jnp.where(row >= SUBLANE - k, pltpu.roll(nxt, SUBLANE - k, 0), x[R - SUBLANE:])
    return jnp.concatenate([x[:R - SUBLANE], tail], axis=0)


def _conv_rows(u_ref, w, b, r0, first):
    R = CONV_ROWS
    cur = u_ref[pl.ds(r0, R), :].astype(f32)
    prev = u_ref[pl.ds(pl.multiple_of(jnp.maximum(r0 - CONV_HALO, 0), CONV_HALO), CONV_HALO), :].astype(f32)
    prev = jnp.where(first, 0.0, prev)
    x1 = _shift_down(cur, prev, 1)
    x2 = _shift_down(cur, prev, 2)
    c = ((b + w[0:1] * x2) + w[1:2] * x1) + w[2:3] * cur
    return c, x2, x1, cur


def _conv_fwd(ug, uv, wg, wv, bg, bv):
    S, F = ug.shape
    nchunk = S // CONV_ROWS

    def body(ug_ref, uv_ref, wg_ref, wv_ref, bg_ref, bv_ref, o_ref):
        wgv, wvv, bgv, bvv = wg_ref[...], wv_ref[...], bg_ref[...], bv_ref[...]

        def step(ci, carry):
            r0 = pl.multiple_of(ci * CONV_ROWS, CONV_ROWS)
            cg = _conv_rows(ug_ref, wgv, bgv, r0, ci == 0)[0]
            cv = _conv_rows(uv_ref, wvv, bvv, r0, ci == 0)[0]
            gelu = 0.5 * cg * (1.0 + lax.erf(cg * INV_SQRT2))
            o_ref[pl.ds(r0, CONV_ROWS), :] = (gelu * cv).astype(bf16)
            return carry

        lax.fori_loop(0, nchunk, step, 0)

    col = pl.BlockSpec((S, LANE), lambda j: (0, j))
    w3 = pl.BlockSpec((3, LANE), lambda j: (0, j))
    b1 = pl.BlockSpec((1, LANE), lambda j: (0, j))
    return pl.pallas_call(
        body,
        grid=(F // LANE,),
        in_specs=[col, col, w3, w3, b1, b1],
        out_specs=col,
        out_shape=SDS((S, F), bf16),
        compiler_params=_cparams(("parallel",), VMEM_BIG),
        name="conv_fwd",
    )(ug, uv, wg, wv, bg, bv)


def _conv_bwd(ug, uv, dact, wg, wv, bg, bv):
    S, F = ug.shape
    R = CONV_ROWS
    nchunk = S // R

    def body(ug_ref, uv_ref, da_ref, wg_ref, wv_ref, bg_ref, bv_ref, dug_ref, duv_ref, sg_ref, sv_ref, dcg, dcv):
        wgv, wvv, bgv, bvv = wg_ref[...], wv_ref[...], bg_ref[...], bv_ref[...]
        zero = jnp.zeros((SUBLANE, LANE), f32)

        def fwd_step(ci, acc):
            r0 = pl.multiple_of(ci * R, R)
            cg, g2, g1, g0 = _conv_rows(ug_ref, wgv, bgv, r0, ci == 0)
            cv, v2, v1, v0 = _conv_rows(uv_ref, wvv, bvv, r0, ci == 0)
            da = da_ref[pl.ds(r0, R), :].astype(f32)
            cdf = 0.5 * (1.0 + lax.erf(cg * INV_SQRT2))
            pdf = INV_SQRT_2PI * jnp.exp(-0.5 * cg * cg)
            dg = da * cv * (cdf + cg * pdf)
            dv = da * (cg * cdf)
            dcg[pl.ds(r0, R), :] = dg
            dcv[pl.ds(r0, R), :] = dv
            new = (acc[0] + _colsum8(dg * g2), acc[1] + _colsum8(dg * g1), acc[2] + _colsum8(dg * g0),
                   acc[3] + _colsum8(dg),
                   acc[4] + _colsum8(dv * v2), acc[5] + _colsum8(dv * v1), acc[6] + _colsum8(dv * v0),
                   acc[7] + _colsum8(dv))
            return new

        acc = lax.fori_loop(0, nchunk, fwd_step, (zero,) * 8)
        rows = lax.broadcasted_iota(jnp.int32, (SUBLANE, LANE), 0)

        def stats(parts):
            out = jnp.zeros((SUBLANE, LANE), f32)
            for k, pt in enumerate(parts):
                out = jnp.where(rows == k, jnp.sum(pt, axis=0, keepdims=True), out)
            return out

        sg_ref[...] = stats(acc[0:4])
        sv_ref[...] = stats(acc[4:8])

        def du_rows(dc, w, r0, last):
            cur = dc[pl.ds(r0, R), :]
            nxt = dc[pl.ds(pl.multiple_of(jnp.minimum(r0 + R, S - SUBLANE), SUBLANE), SUBLANE), :]
            nxt = jnp.where(last, 0.0, nxt)
            return w[2:3] * cur + w[1:2] * _shift_up(cur, nxt, 1) + w[0:1] * _shift_up(cur, nxt, 2)

        def bwd_step(ci, carry):
            r0 = pl.multiple_of(ci * R, R)
            last = ci == nchunk - 1
            dug_ref[pl.ds(r0, R), :] = du_rows(dcg, wgv, r0, last).astype(bf16)
            duv_ref[pl.ds(r0, R), :] = du_rows(dcv, wvv, r0, last).astype(bf16)
            return carry

        lax.fori_loop(0, nchunk, bwd_step, 0)

    col = pl.BlockSpec((S, LANE), lambda j: (0, j))
    w3 = pl.BlockSpec((3, LANE), lambda j: (0, j))
    b1 = pl.BlockSpec((1, LANE), lambda j: (0, j))
    st = pl.BlockSpec((SUBLANE, LANE), lambda j: (0, j))
    return pl.pallas_call(
        body,
        grid=(F // LANE,),
        in_specs=[col, col, col, w3, w3, b1, b1],
        out_specs=[col, col, st, st],
        out_shape=[SDS((S, F), bf16), SDS((S, F), bf16), SDS((SUBLANE, F), f32), SDS((SUBLANE, F), f32)],
        scratch_shapes=[pltpu.VMEM((S, LANE), f32), pltpu.VMEM((S, LANE), f32)],
        compiler_params=_cparams(("parallel",), VMEM_BIG),
        name="conv_bwd",
    )(ug, uv, dact, wg, wv, bg, bv)


def _adam_math(w, g, m, v):
    m = ADAM_B1 * m + (1.0 - ADAM_B1) * g
    v = ADAM_B2 * v + (1.0 - ADAM_B2) * (g * g)
    m_hat = m / (1.0 - ADAM_B1 ** ADAM_STEP)
    v_hat = v / (1.0 - ADAM_B2 ** ADAM_STEP)
    delta = -ADAM_LR * (m_hat / (jnp.sqrt(v_hat) + ADAM_EPS) + ADAM_WD * w)
    return delta, m, v


def _adamw(w, m, v, g, name):
    R, C = w.shape
    parts = len(g.shape) == 3
    tr = R
    if R % 16 == 0:
        for t in range(R, 0, -16):
            if R % t == 0 and t * C * 4 <= ADAM_BLOCK_BYTES:
                tr = t
                break

    def body(w_ref, m_ref, v_ref, g_ref, go_ref, d_ref, mo_ref, vo_ref):
        if parts:
            gv = ((g_ref[0].astype(f32) + g_ref[1].astype(f32)) + g_ref[2].astype(f32)) + g_ref[3].astype(f32)
        else:
            gv = g_ref[...]
        go_ref[...] = gv
        d, mn, vn = _adam_math(w_ref[...], gv, m_ref[...], v_ref[...])
        d_ref[...] = d
        mo_ref[...] = mn
        vo_ref[...] = vn

    row = pl.BlockSpec((tr, C), lambda i: (i, 0))
    gspec = pl.BlockSpec((4, tr, C), lambda i: (0, i, 0)) if parts else row
    if isinstance(g, _Rows):
        lo, g = g.lo, g.full
        assert lo % (2 * SUBLANE) == 0 and tr % (2 * SUBLANE) == 0
        gspec = pl.BlockSpec((pl.Element(4), pl.Element(tr), pl.Element(C)),
                             lambda i: (0, pl.multiple_of(lo + i * tr, 2 * SUBLANE), 0))
    return pl.pallas_call(
        body,
        grid=(R // tr,),
        in_specs=[row, row, row, gspec],
        out_specs=[row] * 4,
        out_shape=[SDS((R, C), f32)] * 4,
        compiler_params=_cparams(("parallel",), VMEM_BIG),
        name=name,
    )(w, m, v, g)


def _sum8(parts, name):
    _, _, R, C = parts.shape

    def body(p_ref, o_ref):
        acc = p_ref[0, 0]
        for c in range(2):
            for k in range(4):
                if c or k:
                    acc = acc + p_ref[c, k]
        o_ref[...] = acc

    return pl.pallas_call(body, out_shape=SDS((R, C), f32), name=name)(parts)


def _pair_add(by_core, b, name):
    _, K, R, C = by_core.shape
    tr = R // 2 if R % 32 == 0 else R

    def body(c_ref, a_ref, b_ref, o_ref):
        o_ref[...] = (a_ref[0].astype(f32) + b_ref[...].astype(f32)).astype(bf16)

    blk = pl.BlockSpec((1, tr, C), lambda k, i, c: (k, i, 0))
    return pl.pallas_call(
        body,
        grid_spec=pltpu.PrefetchScalarGridSpec(
            num_scalar_prefetch=1,
            grid=(K, R // tr),
            in_specs=[pl.BlockSpec((1, 1, tr, C), lambda k, i, c: (c[0], k, i, 0)), blk],
            out_specs=blk,
        ),
        out_shape=SDS((K, R, C), bf16),
        compiler_params=_cparams(("parallel", "parallel")),
        name=name,
    )(lax.axis_index("c").astype(jnp.int32).reshape(1), by_core, b)


_ANY = pl.BlockSpec(memory_space=pl.ANY)


def _chip_out_shape(src, gather):
    return SDS((4,) + tuple(src.shape if gather else src.shape[1:]), src.dtype)


def _fill_own(out, src, gather):
    mine = 2 * lax.axis_index("x") + lax.axis_index("y")
    own = src if gather else lax.dynamic_index_in_dim(src, mine, axis=0, keepdims=False)
    return lax.dynamic_update_index_in_dim(out, own, mine, axis=0)


_HBM = pl.BlockSpec(memory_space=pltpu.HBM)
_SEM = pl.BlockSpec(memory_space=pltpu.SEMAPHORE)
_EFFECT = pltpu.SideEffectType.DATAFLOW_SIDE_EFFECTING
_SPLIT_PEERS = {"chip_gather": 3, "chip_gather_wide": 3, "chip_xchg": 3, "core_fill": 4, "core_swap": 1}


def _split_land(src, kind):
    if kind == "chip_gather_wide":
        return SDS((4, 2) + tuple(src.shape), src.dtype)
    if kind == "core_fill":
        return SDS((SUBLANE, LANE), src.dtype)
    if kind == "core_swap":
        return SDS(tuple(src.shape[1:]), src.dtype)
    return _chip_out_shape(src, kind == "chip_gather")


def _split_copies(src_ref, land_ref, sems, kind):
    x, y, c = lax.axis_index("x"), lax.axis_index("y"), lax.axis_index("c")
    n = _SPLIT_PEERS[kind]
    if kind == "core_fill":
        routes = [((x, y, 1 - c), src_ref.at[k, c], src_ref.at[k, c], src_ref.at[k, 1 - c]) for k in range(n)]
    elif kind == "core_swap":
        routes = [((x, y, 1 - c), src_ref.at[1 - c], land_ref, land_ref)]
    else:
        mine = 2 * x + y
        gather = kind != "chip_xchg"
        slot = (lambda k: land_ref.at[k, c]) if kind == "chip_gather_wide" else (lambda k: land_ref.at[k])
        routes = [((px, py, c), src_ref if gather else src_ref.at[2 * px + py], slot(mine), slot(2 * px + py))
                  for px, py in [(1 - x, y), (x, 1 - y), (1 - x, 1 - y)]]
    sends, recvs = [], []
    for j, (peer, piece, there, here) in enumerate(routes):
        sends.append(pltpu.make_async_remote_copy(src_ref=piece, dst_ref=there, send_sem=sems[j],
                                                  recv_sem=sems[n + j], device_id=peer, device_id_type=MESH))
        recvs.append(pltpu.make_async_remote_copy(src_ref=piece, dst_ref=here, send_sem=sems[j],
                                                  recv_sem=sems[n + j], device_id=peer, device_id_type=MESH))
    return sends, recvs


def _split_start(src, kind, name, after=None):
    land = _split_land(src, kind)
    ns = 2 * _SPLIT_PEERS[kind]
    n_in = 2 if after is None else 3

    def body(*refs):
        src_ref, land_ref = refs[:2]
        outs = refs[n_in:]
        for cp in _split_copies(src_ref, land_ref, outs[:ns], kind)[0]:
            cp.start()
        token = outs[ns + 2]
        token[...] = jnp.zeros_like(token)

    res = pl.pallas_call(
        body,
        name=name,
        out_shape=(pltpu.SemaphoreType.DMA(()),) * ns
        + (pltpu.HBM(src.shape, src.dtype), pltpu.HBM(land.shape, land.dtype), SDS((SUBLANE, LANE), f32)),
        in_specs=(_HBM, _HBM) + (() if after is None else (_ANY,)),
        out_specs=(_SEM,) * ns + (_HBM, _HBM, pl.BlockSpec(memory_space=pltpu.VMEM)),
        input_output_aliases={0: ns, 1: ns + 1},
        compiler_params=pltpu.CompilerParams(has_side_effects=_EFFECT),
    )(pltpu.with_memory_space_constraint(src, pltpu.HBM),
      pltpu.with_memory_space_constraint(lax.empty(land.shape, land.dtype), pltpu.HBM),
      *(() if after is None else (after,)))
    return (res[:ns], res[ns], res[ns + 1]), res[ns + 2]


def _split_wait(state, after, kind, name):
    sems, src_thru, land_thru = state
    ns = 2 * _SPLIT_PEERS[kind]

    def body(src_ref, land_ref, *rest):
        sends, recvs = _split_copies(src_ref, land_ref, rest[:ns], kind)
        for cp in recvs:
            cp.wait_recv()
        for cp in sends:
            cp.wait_send()

    src_out, got = pl.pallas_call(
        body,
        name=name,
        out_shape=(pltpu.HBM(src_thru.shape, src_thru.dtype), pltpu.HBM(land_thru.shape, land_thru.dtype)),
        in_specs=(_HBM, _HBM) + (_SEM,) * ns + (_ANY,),
        out_specs=(_HBM, _HBM),
        input_output_aliases={0: 0, 1: 1},
        compiler_params=pltpu.CompilerParams(has_side_effects=_EFFECT),
    )(src_thru, land_thru, *sems, after)
    if kind == "core_swap":
        return got, src_out
    if kind == "core_fill":
        return src_out
    if kind == "chip_gather_wide":
        mine = 2 * lax.axis_index("x") + lax.axis_index("y")
        zero = jnp.zeros((), mine.dtype)
        return lax.dynamic_update_slice(got, src_out[None, None], (mine, lax.axis_index("c").astype(mine.dtype))
                                        + (zero,) * src_out.ndim)
    return _fill_own(got, src_out, kind == "chip_gather")


def _core_fill(both, name):
    n = both.shape[0]

    def body(in_ref, out_ref, send_sems, recv_sems):
        x, y, c = lax.axis_index("x"), lax.axis_index("y"), lax.axis_index("c")
        sends = [pltpu.make_async_remote_copy(src_ref=out_ref.at[k, c], dst_ref=out_ref.at[k, c],
                                              send_sem=send_sems.at[k], recv_sem=recv_sems.at[k],
                                              device_id=(x, y, 1 - c), device_id_type=MESH) for k in range(n)]
        recvs = [pltpu.make_async_remote_copy(src_ref=out_ref.at[k, c], dst_ref=out_ref.at[k, 1 - c],
                                              send_sem=send_sems.at[k], recv_sem=recv_sems.at[k],
                                              device_id=(x, y, 1 - c), device_id_type=MESH) for k in range(n)]
        for cp in sends:
            cp.start()
        for cp in recvs:
            cp.wait_recv()
        for cp in sends:
            cp.wait_send()

    return pl.pallas_call(
        body,
        in_specs=[_ANY],
        out_specs=_ANY,
        out_shape=SDS(both.shape, both.dtype),
        scratch_shapes=[pltpu.SemaphoreType.DMA((n,)), pltpu.SemaphoreType.DMA((n,))],
        input_output_aliases={0: 0},
        name=name,
    )(both)


def _core_gather(src, name):
    def body(src_ref, out_ref, send_sem, recv_sem):
        x, y, c = lax.axis_index("x"), lax.axis_index("y"), lax.axis_index("c")
        cp = pltpu.make_async_remote_copy(src_ref=src_ref, dst_ref=out_ref.at[c], send_sem=send_sem,
                                          recv_sem=recv_sem, device_id=(x, y, 1 - c), device_id_type=MESH)
        cp.start()
        pltpu.make_async_remote_copy(src_ref=src_ref, dst_ref=out_ref.at[1 - c], send_sem=send_sem,
                                     recv_sem=recv_sem, device_id=(x, y, 1 - c), device_id_type=MESH).wait_recv()
        cp.wait_send()

    out = pl.pallas_call(
        body,
        in_specs=[_ANY],
        out_specs=_ANY,
        out_shape=SDS((2,) + tuple(src.shape), src.dtype),
        scratch_shapes=[pltpu.SemaphoreType.DMA, pltpu.SemaphoreType.DMA],
        name=name,
    )(src)
    return lax.dynamic_update_index_in_dim(out, src, lax.axis_index("c"), axis=0)


_PACK_A = (("w_in", (1088, 1024)),)
_PACK_B = (("w_ba", (512, 128)), ("w_bh", (512, 128)), ("w_out", (128, 1024)), ("w_up", (704, 1024)),
           ("w_down", (352, 1024)))
_PACK_SIZES = _PACK_A + _PACK_B
_TRANSPOSED = ("w_in", "w_up")


def _slab_rows(sizes):
    return sum(r * c for _, (r, c) in sizes) // D_MODEL


def _pack_lo(key):
    keys = [k for k, _ in _PACK_B]
    return _slab_rows(_PACK_B[:keys.index(key)])


def _pack_rows(d, sizes):
    n = d[sizes[0][0]].shape[0]
    return jnp.concatenate([d[k].reshape(n, -1, D_MODEL) for k, _ in sizes], axis=1)


def _unpack_rows(slab, sizes):
    n = slab.shape[0]
    out, lo = {}, 0
    for key, (r, c) in sizes:
        rows = r * c // D_MODEL
        out[key] = slab[:, lo:lo + rows].reshape(n, r, c)
        lo += rows
    return out


def _by_core(gslab):
    return jnp.swapaxes(gslab.reshape((4, 2) + gslab.shape[1:]), 0, 1)


def _cols_to_full(t):
    return jnp.swapaxes(t, 0, 1).reshape(t.shape[1], -1)


def _full_to_cols(t):
    K = t.shape[0]
    return jnp.swapaxes(t.reshape(K, 8, -1), 0, 1)


_SMALL = (("pre_mix_norm", (1, 1024)), ("rel_bias", (32, 24)), ("hgrn_lb_raw", (2, 512)), ("hgrn_norm", (1, 128)),
          ("post_mix_norm", (1, 1024)), ("pre_ffn_norm", (1, 1024)), ("conv_b", (1, 5632)),
          ("post_ffn_norm", (1, 1024)))
_SMALL_ROWS = 96
_CONVW_ROWS = 136


_SMALL_USED = sum(r * c for _, (r, c) in _SMALL)


def _pack_small(d, extra=None):
    flat = jnp.concatenate([d[k].reshape(-1) for k, _ in _SMALL] + ([] if extra is None else [extra.reshape(-1)]))
    flat = jnp.pad(flat, (0, _SMALL_ROWS * LANE - flat.shape[0]))
    return flat.reshape(_SMALL_ROWS, LANE)


def _unpack_small(p):
    flat = p.reshape(-1)
    out, lo = {}, 0
    for k, shp in _SMALL:
        n = shp[0] * shp[1]
        out[k] = flat[lo:lo + n].reshape(shp)
        lo += n
    return out


def _local_step(x, tgt, P, plan):
    S = x.shape[0]
    P = dict(P)
    lb = _lb_fwd(P["hgrn_lb_raw"])
    hs = _prep(x, P["pre_mix_norm"], plan.start_token())
    h1 = hs[0]
    consts = [_bias_consts(d, plan.start_token()) for d in DILATIONS]
    biases, dep = [], h1
    for g in range(N_GROUPS):
        tab_t = P["rel_bias"][:, 8 * g:8 * g + 8].T
        dep = _bias_build(tab_t, consts[g][0], consts[g][1], f"bias_build{g}", dep)
        biases.append(dep.reshape(8, ATTN_BLOCK, 2 * ATTN_BLOCK))
    W = dict(plan.weights_a(dep))
    qkv0, hg, gc = _mm_fanout(h1, [W["wt_qkv"][0], W["wt_hg"], W["wt_gate"]], "nt", [bf16, f32, bf16], "proj_natural")
    qkv = [qkv0] + [_mm(hs[g], W["wt_qkv"][g], "nt", bf16, f"proj_qkv{g}") for g in (1, 2)]
    obuf, lbuf, token = [], [], None
    for g, d in enumerate(DILATIONS):
        o_g, l_g = _attn_fwd(qkv[g], biases[g], (S // d) // ATTN_BLOCK, f"attn_fwd{g}", after=token)
        lbuf.append(l_g)
        obuf.append(o_g)
        if g == 0:
            token = plan.forward_b(o_g)
    y_attn, y_attn_b, w0, w1, w2 = _attn_merge(obuf[0], obuf[1], obuf[2], lbuf[0], lbuf[1], lbuf[2])
    y_hgrn, o_raw, ck = _hgrn_fwd(hg, lb, P["hgrn_norm"])
    wb = plan.weights_b(y_hgrn)
    P["conv_w"] = wb.pop("conv_w")
    W.update(wb)
    a, b, merged = _gate_fwd(y_attn_b, y_hgrn, W["w_ba"], W["w_bh"], gc)
    mo, x1, h2 = _mid_fwd(x, merged, W["w_out"], P["post_mix_norm"], P["pre_ffn_norm"])
    ug, uv = _mm_fanout(h2, [W["wt_up_g"], W["wt_up_v"]], "nt", [bf16, bf16], "up_proj")
    cw_g, cw_v = P["conv_w"][:, :D_FF], P["conv_w"][:, D_FF:]
    cb_g, cb_v = P["conv_b"][:, :D_FF], P["conv_b"][:, D_FF:]
    act = _conv_fwd(ug, uv, cw_g, cw_v, cb_g, cb_v)
    loss, dy, dfo, g_post_ffn = _final(x1, act, W["w_down"], tgt, P["post_ffn_norm"])
    gslab = lax.empty((8, _slab_rows(_PACK_B), D_MODEL), bf16)
    rows_b = {k: r for k, (r, _) in _PACK_B}
    gslab = _mm(act, dfo, "tn", bf16, "gw_down", into=(gslab, 0, _pack_lo("w_down"), rows_b["w_down"]))
    dact = _mm(dfo, W["w_down"], "nt", bf16, "d_act")
    dug, duv, st_g, st_v = _conv_bwd(ug, uv, dact, cw_g, cw_v, cb_g, cb_v)
    gslab = _mm(dug, h2, "tn", bf16, "gw_up_gate", into=(gslab, 0, _pack_lo("w_up"), rows_b["w_up"]))
    gslab = _mm(duv, h2, "tn", bf16, "gw_up_val", into=(gslab, 4, _pack_lo("w_up"), rows_b["w_up"]))
    dx1, dmo, g_pre_ffn, g_post_mix = _mid_bwd(dy, dug, duv, W["wt_up_g"], W["wt_up_v"], x1, mo, P["pre_ffn_norm"],
                                               P["post_mix_norm"])
    gslab = _mm(merged, dmo, "tn", bf16, "gw_out", into=(gslab, 0, _pack_lo("w_out"), rows_b["w_out"]))
    da, db, dgc, dyattn, dyhgrn = _gate_bwd(dmo, W["w_out"], a, b, gc, W["w_ba"], W["w_bh"])
    gW_ba = _mm(y_attn_b, da, "tn", bf16, "gw_ba")
    gW_bh = _mm(y_hgrn, db, "tn", bf16, "gw_bh")
    big_b = dict(w_ba=gW_ba, w_bh=gW_bh, slab=gslab)
    dos = _attn_merge_bwd(dyattn, y_attn, w0, w1, w2, after=plan.grads_b_start(big_b))
    dq_h, df_h, dv_h, dog_h, glb8, gnw8 = _hgrn_bwd(hg, o_raw, dyhgrn, ck, lb, P["hgrn_norm"],
                                                   after=plan.grads_b_exchange(dos[5]))
    dhg = [dq_h, df_h, dv_h, dog_h]
    g_lb_raw = _lb_bwd(P["hgrn_lb_raw"], glb8[0:1])
    gn = gnw8[0:1]
    g_hgrn_norm = (gn[:, 0:128] + gn[:, 128:256]) + (gn[:, 256:384] + gn[:, 384:512])
    dqkvs, gW_qkv, g_rel = [], [], []
    for g, d in enumerate(DILATIONS):
        dq, dk, dv, dbias = _attn_bwd(qkv[g], biases[g], dos[g], dos[3 + g], lbuf[g], (S // d) // ATTN_BLOCK,
                                      f"attn_bwd{g}")
        dqkvs.append([dq, dk, dv])
        gW_qkv.append(_mm(dqkvs[g], hs[g], "tn", bf16, f"gw_qkv{g}"))
        g_rel.append(_bias_grad(dbias.reshape(8, -1), consts[g][0], f"bias_grad{g}"))
    gW_hg = _mm(dhg, h1, "tn", bf16, "gw_hg")
    gW_gate = _mm(dgc, h1, "tn", bf16, "gw_gate")
    gW_in = gW_qkv + [gW_hg, gW_gate]
    token = plan.grads_a_start(gW_in)
    dh_perm = [_mm(dqkvs[g], W["wt_qkv"][g], "nn", f32, f"dh1_qkv{g}", after=token) for g in (1, 2)]
    token = plan.grads_a_exchange(dh_perm[1])
    dh_main = _mm(dqkvs[0] + dhg + [dgc], [W["wt_qkv"][0], W["wt_hg"], W["wt_gate"]], "nn", f32, "dh1_main",
                  after=token)
    grad_x, g_pre_mix = _first_bwd(x, dx1, dh_main, dh_perm[0], dh_perm[1], P["pre_mix_norm"])

    g_conv_w = jnp.concatenate([st_g[0:3], st_v[0:3]], axis=1)
    g_conv_b = jnp.concatenate([st_g[3:4], st_v[3:4]], axis=1)
    small = dict(pre_mix_norm=g_pre_mix, rel_bias=jnp.concatenate(g_rel, axis=1), hgrn_lb_raw=g_lb_raw,
                 hgrn_norm=g_hgrn_norm, post_mix_norm=g_post_mix, pre_ffn_norm=g_pre_ffn, conv_b=g_conv_b,
                 post_ffn_norm=g_post_ffn, conv_w=g_conv_w)
    return loss, grad_x, gW_in, big_b, small


def _weights_a(both):
    wt = both.reshape(-1, D_MODEL)
    return dict(
        wt_qkv=[_Rows(wt, g * QKV_G, QKV_G) for g in range(N_GROUPS)],
        wt_hg=_Rows(wt, 3 * QKV_G, 4 * HGRN_W),
        wt_gate=_Rows(wt, 3 * QKV_G + 4 * HGRN_W, wt.shape[0] - 3 * QKV_G - 4 * HGRN_W),
    )


def _weights_b(slabs):
    sh = _unpack_rows(slabs, _PACK_B)
    wt_up = sh["w_up"].reshape(-1, D_MODEL)
    return dict(
        w_ba=_cols_to_full(sh["w_ba"]),
        w_bh=_cols_to_full(sh["w_bh"]),
        w_out=sh["w_out"].reshape(D_MODEL, D_MODEL),
        wt_up_g=wt_up[:D_FF],
        wt_up_v=wt_up[D_FF:],
        w_down=sh["w_down"].reshape(D_FF, D_MODEL),
    )


def _dest_rows(sections, height):
    out = []
    for j in range(8):
        lo, hi, off, pieces = j * height, (j + 1) * height, 0, []
        for s in sections:
            a, b = max(lo, off), min(hi, off + s.shape[0])
            if a < b:
                pieces.append(s[a - off:b - off])
            off += s.shape[0]
        out.append(pieces[0] if len(pieces) == 1 else jnp.concatenate(pieces, axis=0))
    return out


def _grad_blocks_a(sections):
    rows = _dest_rows(sections, 1088)
    return jnp.stack([jnp.stack([rows[2 * k + c].astype(bf16) for k in range(4)]) for c in range(2)])


def _grad_slab_b(g):
    shards = dict(w_ba=_full_to_cols(g["w_ba"]), w_bh=_full_to_cols(g["w_bh"]))
    head = _pack_rows({k: v.astype(bf16) for k, v in shards.items()}, _PACK_B[:2])
    assert head.shape[1] == _pack_lo("w_out")
    return lax.dynamic_update_slice(g["slab"], head, (0, 0, 0))


_CONVW_SLAB_ROWS = 16


class _Traffic:
    def __init__(self, slab_a, slab_b, conv_w):
        hi = conv_w.astype(bf16)
        r1 = conv_w - hi.astype(f32)
        mid = r1.astype(bf16)
        lo = (r1 - mid.astype(f32)).astype(bf16)
        bits = jnp.stack([hi, mid, lo]).reshape(-1)
        tail = jnp.pad(bits, (0, _CONVW_SLAB_ROWS * D_MODEL - bits.shape[0])).reshape(_CONVW_SLAB_ROWS, D_MODEL)
        self.slab_b = jnp.concatenate([slab_b, tail], axis=0)
        self.state_a, tok = _split_start(slab_a, "chip_gather_wide", "ag_a_start")
        self.state_b, self.token = _split_start(self.slab_b, "chip_gather_wide", "ag_b_start", after=tok)
        self.state = None
        self.state_gb = None

    def start_token(self):
        return self.token

    def weights_a(self, after):
        half = _split_wait(self.state_a, after, "chip_gather_wide", "ag_a_wait")
        return _weights_a(_core_fill(half, "ag_a_cores"))

    def forward_b(self, after):
        half = _split_wait(self.state_b, after, "chip_gather_wide", "ag_b_wait")
        self.state, token = _split_start(half, "core_fill", "ag_b_cores_start")
        return token

    def weights_b(self, after):
        both = _split_wait(self.state, after, "core_fill", "ag_b_cores_wait")
        slabs = both.reshape((8,) + tuple(self.slab_b.shape))
        rows = _slab_rows(_PACK_B)
        out = _weights_b(slabs[:, :rows])
        pieces = slabs[:, rows:].reshape(8, -1)[:, :3 * 3 * 704].reshape(8, 3, 3, 704).astype(f32)
        out["conv_w"] = _cols_to_full((pieces[:, 0] + pieces[:, 1]) + pieces[:, 2])
        return out

    def grads_b_start(self, grads):
        self.state, token = _split_start(_by_core(_grad_slab_b(grads)), "core_swap", "rs_b_cores_start")
        return token

    def grads_b_exchange(self, after):
        from_sib, by_core = _split_wait(self.state, after, "core_swap", "rs_b_cores_wait")
        self.state_gb, token = _split_start(_pair_add(by_core, from_sib, "rs_b_pair_add"), "chip_xchg", "rs_b_start")
        return token

    def grads_a_start(self, sections):
        self.state, token = _split_start(_grad_blocks_a(sections), "core_swap", "rs_a_cores_start")
        return token

    def grads_a_exchange(self, after):
        from_sib, by_core = _split_wait(self.state, after, "core_swap", "rs_a_cores_wait")
        self.state, token = _split_start(_pair_add(by_core, from_sib, "rs_a_pair_add"), "chip_xchg", "rs_a_start")
        return token

    def parts(self, after):
        slab = _split_wait(self.state_gb, after, "chip_xchg", "rs_b_wait")
        parts, lo = _unpack_rows(slab, _PACK_B), 0
        for key, (r, c) in _PACK_B:
            if c == D_MODEL:
                parts[key] = _Rows(slab, lo, r)
            lo += r * c // D_MODEL
        parts["w_in"] =_split_wait(self.state, after, "chip_xchg", "rs_a_wait")
        return parts


def kernel(x, pre_mix_norm, w_in, rel_bias, hgrn_lb_raw, hgrn_norm, w_branch_attn, w_branch_hgrn, w_out, post_mix_norm, pre_ffn_norm, w_up, conv_w, conv_b, w_down, post_ffn_norm, loss_target, m_pre_mix_norm, m_w_in, m_rel_bias, m_hgrn_lb_raw, m_hgrn_norm, m_w_branch_attn, m_w_branch_hgrn, m_w_out, m_post_mix_norm, m_pre_ffn_norm, m_w_up, m_conv_w, m_conv_b, m_w_down, m_post_ffn_norm, v_pre_mix_norm, v_w_in, v_rel_bias, v_hgrn_lb_raw, v_hgrn_norm, v_w_branch_attn, v_w_branch_hgrn, v_w_out, v_post_mix_norm, v_pre_ffn_norm, v_w_up, v_conv_w, v_conv_b, v_w_down, v_post_ffn_norm):
    ci = lax.axis_index("c")
    dev = 4 * lax.axis_index("x") + 2 * lax.axis_index("y") + ci
    tr = lambda t: jnp.swapaxes(t[0], 0, 1)
    wts = dict(w_in=tr(w_in), w_ba=w_branch_attn[0], w_bh=w_branch_hgrn[0], w_out=w_out[0], w_up=tr(w_up),
               w_down=w_down[0])
    mom = dict(w_in=tr(m_w_in), w_ba=m_w_branch_attn[0], w_bh=m_w_branch_hgrn[0], w_out=m_w_out[0], w_up=tr(m_w_up),
               w_down=m_w_down[0])
    var = dict(w_in=tr(v_w_in), w_ba=v_w_branch_attn[0], w_bh=v_w_branch_hgrn[0], w_out=v_w_out[0], w_up=tr(v_w_up),
               w_down=v_w_down[0])
    small_w = dict(pre_mix_norm=pre_mix_norm, rel_bias=rel_bias, hgrn_lb_raw=hgrn_lb_raw, hgrn_norm=hgrn_norm,
                   post_mix_norm=post_mix_norm, pre_ffn_norm=pre_ffn_norm, conv_b=conv_b, post_ffn_norm=post_ffn_norm)
    small_m = dict(pre_mix_norm=m_pre_mix_norm, rel_bias=m_rel_bias, hgrn_lb_raw=m_hgrn_lb_raw, hgrn_norm=m_hgrn_norm,
                   post_mix_norm=m_post_mix_norm, pre_ffn_norm=m_pre_ffn_norm, conv_b=m_conv_b,
                   post_ffn_norm=m_post_ffn_norm)
    small_v = dict(pre_mix_norm=v_pre_mix_norm, rel_bias=v_rel_bias, hgrn_lb_raw=v_hgrn_lb_raw, hgrn_norm=v_hgrn_norm,
                   post_mix_norm=v_post_mix_norm, pre_ffn_norm=v_pre_ffn_norm, conv_b=v_conv_b,
                   post_ffn_norm=v_post_ffn_norm)

    plan = _Traffic(wts["w_in"].astype(bf16),
                    _pack_rows({k: wts[k].astype(bf16)[None] for k, _ in _PACK_B}, _PACK_B)[0], conv_w[0])

    loss8, grad_x, _, _, small = _local_step(x[0], loss_target[0], small_w, plan)
    spack = jnp.concatenate([_pack_small(small, loss8[0, 0:1]),
                             jnp.pad(small["conv_w"].reshape(-1, LANE), ((0, _CONVW_ROWS - 132), (0, 0)))], axis=0)
    small_state, token = _split_start(spack, "chip_gather", "ag_small_start")

    parts = plan.parts(token)
    outs_big = {}
    for k, _ in _PACK_SIZES:
        outs_big[k] = _adamw(wts[k], mom[k], var[k], parts[k], "adamw_" + k)

    by_chip = _split_wait(small_state, outs_big["w_in"][1], "chip_gather", "ag_small_wait")
    allp = _core_gather(by_chip, "ag_small_cores")
    ssum = _sum8(allp, "small_sum")
    gs = ssum[:_SMALL_ROWS]
    loss = ssum[_SMALL_USED // LANE, _SMALL_USED % LANE]
    res_small = _adamw(_pack_small(small_w), _pack_small(small_m), _pack_small(small_v), gs, "adamw_small")
    sm = [_unpack_small(t) for t in res_small]
    g_cw_full = ssum[_SMALL_ROWS:_SMALL_ROWS + 132].reshape(3, 2 * D_FF)
    g_cw = lax.dynamic_slice_in_dim(g_cw_full, dev * 704, 704, axis=1)
    res_cw = _adamw(conv_w[0], m_conv_w[0], v_conv_w[0], g_cw, "adamw_conv_w")

    def pick(i):
        def big_(k):
            t = outs_big[k][i]
            return (jnp.swapaxes(t, 0, 1) if k in _TRANSPOSED else t)[None]
        return [sm[i]["pre_mix_norm"], big_("w_in"), sm[i]["rel_bias"], sm[i]["hgrn_lb_raw"], sm[i]["hgrn_norm"],
                big_("w_ba"), big_("w_bh"), big_("w_out"), sm[i]["post_mix_norm"], sm[i]["pre_ffn_norm"],
                big_("w_up"), res_cw[i][None], sm[i]["conv_b"], big_("w_down"), sm[i]["post_ffn_norm"]]

    return (loss, grad_x[None], *pick(0), *pick(1), *pick(2), *pick(3))
```

```python
import functools
import math

import jax
import jax.numpy as jnp
from jax import lax
from jax.experimental import pallas as pl
from jax.experimental.pallas import tpu as pltpu

f32 = jnp.float32
bf16 = jnp.bfloat16
SDS = jax.ShapeDtypeStruct
HIGHEST = lax.Precision.HIGHEST
MESH = pl.DeviceIdType.MESH

NN = (((1,), (0,)), ((), ()))
NT = (((1,), (1,)), ((), ()))
TN = (((0,), (0,)), ((), ()))

D_MODEL = 1024
N_GROUPS = 3
DILATIONS = (1, 4, 16)
HEAD_DIM = 64
ATTN_BLOCK = 128
QKV_G = 1536
ATTN_OUT = 512
HGRN_W = 512
HGRN_CHUNK = 32
D_FF = 2816
NUM_BUCKETS = 32
MAX_EXACT = 16
MAX_DISTANCE = 2048
NEG_INF = -1e30
EPS = 1e-6
LANE = 128
SUBLANE = 8
VMEM_BIG = 48 * 1024 * 1024
MM_ROWS = 512
MM_OUT_BYTES = 8 * 1024 * 1024
ADAM_BLOCK_BYTES = 2304 * 1024

ADAM_LR, ADAM_B1, ADAM_B2, ADAM_EPS, ADAM_WD, ADAM_STEP = 0.001, 0.9, 0.999, 1e-08, 0.01, 10


def _pick(n, pref):
    t = pref
    while t >= LANE:
        if n % t == 0:
            return t
        t //= 2
    return n


def _cparams(sem=None, vmem=None):
    kw = {}
    if sem is not None:
        kw["dimension_semantics"] = sem
    if vmem is not None:
        kw["vmem_limit_bytes"] = vmem
    return pltpu.CompilerParams(**kw)


def _sigmoid(x):
    return jax.nn.sigmoid(x)


def _colsum8(x):
    return x.reshape(x.shape[0] // SUBLANE, SUBLANE, x.shape[1]).sum(axis=0)


class _Rows:
    def __init__(self, full, lo, rows):
        self.full, self.lo, self.shape = full, lo, tuple(full.shape[:-2]) + (rows, full.shape[-1])


def _resident(t):
    if isinstance(t, _Rows):
        return pl.BlockSpec((pl.Element(t.shape[0]), pl.Element(t.shape[1])), lambda i: (t.lo, 0)), t.full
    return pl.BlockSpec(t.shape, lambda i: (0, 0)), t


def _mm(a, b, mode, out_dtype, name, acc=None, after=None, into=None):
    dims = {"nn": NN, "nt": NT, "tn": TN}[mode]
    has_acc = acc is not None
    parts = list(a) if isinstance(a, (list, tuple)) else [a]
    if mode == "tn":
        assert not has_acc
        K, N = b.shape
        widths = [t.shape[1] for t in parts]
        M = sum(widths)
        whole = M * N * 4 <= MM_OUT_BYTES
        assert whole or len(parts) == 1
        tmm = M if whole else M // 2
        ts = _pick(K, 4 * MM_ROWS)
        nk = K // ts

        npart = len(parts)
        narrow = out_dtype != f32
        n_in = npart + (1 if into is None else 2)
        out_spec, out_shape, aliases, extra = pl.BlockSpec((tmm, N), lambda i, k: (i, 0)), SDS((M, N), out_dtype), {}, []
        if into is not None:
            slab, first, lo, shard_rows = into
            assert narrow and slab.dtype == out_dtype and tmm % shard_rows == 0 and slab.shape[2] == N
            per = tmm // shard_rows
            out_spec = pl.BlockSpec((pl.Element(per), pl.Element(shard_rows), pl.Element(N)),
                                    lambda i, k: (first + i * per, lo, 0))
            out_shape, aliases, extra = SDS(slab.shape, out_dtype), {npart + 1: 0}, [slab]

        def body_tn(*refs):
            b_ref, o_ref = refs[npart], refs[n_in]
            acc_ref = refs[n_in + 1] if narrow else o_ref
            k = pl.program_id(1)
            bv = b_ref[...]
            lo = 0
            for a_ref, w in zip(refs[:npart], widths if whole else [tmm]):
                part = lax.dot_general(a_ref[...], bv, dims, preferred_element_type=f32)
                rows = slice(lo, lo + w)
                lo += w

                @pl.when(k == 0)
                def _(part=part, rows=rows):
                    acc_ref[rows, :] = part

                @pl.when(k > 0)
                def _(part=part, rows=rows):
                    acc_ref[rows, :] += part

            if narrow:
                @pl.when(k == nk - 1)
                def _():
                    o_ref[...] = acc_ref[...].astype(out_dtype).reshape(o_ref.shape)

        return pl.pallas_call(
            body_tn,
            grid=(M // tmm, nk),
            in_specs=[pl.BlockSpec((ts, w if whole else tmm), lambda i, k: (k, i)) for w in widths]
            + [pl.BlockSpec((ts, N), lambda i, k: (k, 0))] + [pl.BlockSpec(memory_space=pl.ANY)] * len(extra),
            out_specs=out_spec,
            out_shape=out_shape,
            input_output_aliases=aliases,
            scratch_shapes=[pltpu.VMEM((tmm, N), f32)] if narrow else [],
            compiler_params=_cparams(("parallel", "arbitrary"), VMEM_BIG),
            name=name,
        )(*parts, b, *extra)

    bs = list(b) if isinstance(b, (list, tuple)) else [b]
    widths = [t.shape[1] for t in parts]
    M = parts[0].shape[0]
    kdim = 0 if mode == "nn" else 1
    N = bs[0].shape[1 - kdim]
    tm = _pick(M, MM_ROWS)
    npart, nb = len(parts), len(bs)
    place, bi, lo = [], 0, 0
    for w in widths:
        place.append((bi, lo))
        lo += w
        if lo == bs[bi].shape[kdim]:
            bi, lo = bi + 1, 0
    assert bi == nb and lo == 0

    def body(*refs):
        a_refs, b_refs = refs[:npart], refs[npart:npart + nb]
        c_ref = refs[npart + nb] if has_acc else None
        o_ref = refs[-1]
        part = None
        for a_ref, w, (bi, lo) in zip(a_refs, widths, place):
            b_ref = b_refs[bi]
            if w == bs[bi].shape[kdim]:
                bk = b_ref[...]
            else:
                bk = b_ref[:, lo:lo + w] if mode == "nt" else b_ref[lo:lo + w, :]
            t = lax.dot_general(a_ref[...], bk, dims, preferred_element_type=f32)
            part = t if part is None else part + t
        if has_acc:
            part = part + c_ref[...]
        o_ref[...] = part.astype(out_dtype)

    specs = [pl.BlockSpec((tm, w), lambda i: (i, 0)) for w in widths] + [_resident(t)[0] for t in bs]
    args = parts + [_resident(t)[1] for t in bs]
    aliases = {}
    if has_acc:
        specs.append(pl.BlockSpec((tm, N), lambda i: (i, 0)))
        args.append(acc)
        aliases = {npart + nb: 0}
    if after is not None:
        specs.append(pl.BlockSpec(memory_space=pl.ANY))
        args.append(after)
    return pl.pallas_call(
        body,
        grid=(M // tm,),
        in_specs=specs,
        out_specs=pl.BlockSpec((tm, N), lambda i: (i, 0)),
        out_shape=SDS((M, N), out_dtype),
        input_output_aliases=aliases,
        compiler_params=_cparams(("parallel",), VMEM_BIG),
        name=name,
    )(*args)


def _mm_fanout(a, bs, mode, out_dtypes, name):
    dims = {"nn": NN, "nt": NT}[mode]
    M, K = a.shape
    ns = [b.shape[1] if mode == "nn" else b.shape[0] for b in bs]
    tm = _pick(M, MM_ROWS)
    nb = len(bs)

    def body(a_ref, *refs):
        av = a_ref[...]
        for b_ref, o_ref, dt in zip(refs[:nb], refs[nb:], out_dtypes):
            o_ref[...] = lax.dot_general(av, b_ref[...], dims, preferred_element_type=f32).astype(dt)

    return pl.pallas_call(
        body,
        grid=(M // tm,),
        in_specs=[pl.BlockSpec((tm, K), lambda i: (i, 0))] + [_resident(b)[0] for b in bs],
        out_specs=[pl.BlockSpec((tm, n), lambda i: (i, 0)) for n in ns],
        out_shape=[SDS((M, n), dt) for n, dt in zip(ns, out_dtypes)],
        compiler_params=_cparams(("parallel",), VMEM_BIG),
        name=name,
    )(a, *[_resident(b)[1] for b in bs])


PERM_ROWS = 2048


def _perm_spec(d, cols=LANE):
    return pl.BlockSpec((d, PERM_ROWS // d, cols), lambda i, j: (0, i, j))


def _to_natural(src_ref, dst_ref, d):
    n = src_ref.shape[1]
    for r in range(d):
        dst_ref[pl.ds(r, n, stride=d), :] = src_ref[r]


def _prep(x, w, after=None):
    S, D = x.shape
    R = PERM_ROWS
    nc = D // LANE
    n_in = nc + 1 + (after is not None)

    def body(*refs):
        x_refs, w_ref = refs[:nc], refs[nc]
        h_ref, h4_ref, h16_ref, rs = refs[n_in:]
        ssq = None
        for xr in x_refs:
            v = xr[...]
            t = jnp.sum(v * v, axis=-1, keepdims=True)
            ssq = t if ssq is None else ssq + t
        rinv = lax.rsqrt(ssq * (1.0 / D) + EPS)
        rs[...] = jnp.broadcast_to(rinv, (R, LANE))
        for j, xr in enumerate(x_refs):
            cols = slice(j * LANE, (j + 1) * LANE)
            wj = w_ref[:, cols]
            h_ref[:, cols] = ((xr[...] * rinv) * wj).astype(bf16)
            for d, o_ref in ((4, h4_ref), (16, h16_ref)):
                n = R // d
                for r in range(d):
                    rows = pl.ds(r, n, stride=d)
                    o_ref[r, :, cols] = ((xr[rows, :] * rs[rows, :]) * wj).astype(bf16)

    col = lambda j: pl.BlockSpec((R, LANE), lambda i, j=j: (i, j))
    h, h4, h16 = pl.pallas_call(
        body,
        grid=(S // R,),
        in_specs=[col(j) for j in range(nc)] + [pl.BlockSpec((1, D), lambda i: (0, 0))]
        + ([] if after is None else [pl.BlockSpec(memory_space=pl.ANY)]),
        out_specs=[pl.BlockSpec((R, D), lambda i: (i, 0)), pl.BlockSpec((4, R // 4, D), lambda i: (0, i, 0)),
                   pl.BlockSpec((16, R // 16, D), lambda i: (0, i, 0))],
        out_shape=[SDS((S, D), bf16), SDS((4, S // 4, D), bf16), SDS((16, S // 16, D), bf16)],
        scratch_shapes=[pltpu.VMEM((R, LANE), f32)],
        compiler_params=_cparams(("parallel",), VMEM_BIG),
        name="prep_norm_perm",
    )(*([x] * nc), w, *([] if after is None else [after]))
    return [h, h4.reshape(S, D), h16.reshape(S, D)]


def _rms_parts(xv):
    r = lax.rsqrt(jnp.mean(xv * xv, axis=-1, keepdims=True) + EPS)
    return r, xv * r


def _rms_bwd(xhat, r, w, dy):
    dyw = dy * w
    return r * (dyw - xhat * jnp.mean(dyw * xhat, axis=-1, keepdims=True))


def _mid_fwd(x, merged, w_out, w_pm, w_pf):
    S, D = x.shape
    tm = _pick(S, MM_ROWS)

    def body(x_ref, m_ref, wo_ref, wpm_ref, wpf_ref, mo_ref, x1_ref, h2_ref):
        mo = jnp.dot(m_ref[...], wo_ref[...], preferred_element_type=f32)
        mo_ref[...] = mo
        _, moh = _rms_parts(mo)
        x1 = x_ref[...] + moh * wpm_ref[...]
        x1_ref[...] = x1
        _, x1h = _rms_parts(x1)
        h2_ref[...] = (x1h * wpf_ref[...]).astype(bf16)

    row = pl.BlockSpec((tm, D), lambda i: (i, 0))
    vec = pl.BlockSpec((1, D), lambda i: (0, 0))
    return pl.pallas_call(
        body,
        grid=(S // tm,),
        in_specs=[row, pl.BlockSpec((tm, merged.shape[1]), lambda i: (i, 0)),
                  pl.BlockSpec(w_out.shape, lambda i: (0, 0)), vec, vec],
        out_specs=[row, row, row],
        out_shape=[SDS((S, D), f32), SDS((S, D), f32), SDS((S, D), bf16)],
        compiler_params=_cparams(("parallel",), VMEM_BIG),
        name="out_proj_mid_fwd",
    )(x, merged, w_out, w_pm, w_pf)


def _final(x1, act, w_down, tgt, w_pfn):
    S, D = x1.shape
    tm = _pick(S, MM_ROWS)
    nt = S // tm

    def body(x1_ref, a_ref, wd_ref, t_ref, w_ref, loss_ref, dy_ref, dfo_ref, gw_ref, lacc, gacc):
        i = pl.program_id(0)

        @pl.when(i == 0)
        def _():
            lacc[...] = jnp.zeros_like(lacc)
            gacc[...] = jnp.zeros_like(gacc)

        w = w_ref[...]
        r, foh = _rms_parts(jnp.dot(a_ref[...], wd_ref[...], preferred_element_type=f32))
        y = x1_ref[...] + foh * w
        err = y - t_ref[...]
        lacc[...] += _colsum8(err * err)
        dy = err * (1.0 / D)
        dy_ref[...] = dy
        gacc[...] += _colsum8(dy * foh)
        dfo_ref[...] = _rms_bwd(foh, r, w, dy).astype(bf16)

        @pl.when(i == nt - 1)
        def _():
            loss_ref[...] = jnp.full((SUBLANE, LANE), 0.5 / D, f32) * jnp.sum(lacc[...])
            gw_ref[...] = jnp.sum(gacc[...], axis=0, keepdims=True)

    row = pl.BlockSpec((tm, D), lambda i: (i, 0))
    vec = pl.BlockSpec((1, D), lambda i: (0, 0))
    return pl.pallas_call(
        body,
        grid=(nt,),
        in_specs=[row, pl.BlockSpec((tm, act.shape[1]), lambda i: (i, 0)),
                  pl.BlockSpec(w_down.shape, lambda i: (0, 0)), row, vec],
        out_specs=[pl.BlockSpec((SUBLANE, LANE), lambda i: (0, 0)), row, row, vec],
        out_shape=[SDS((SUBLANE, LANE), f32), SDS((S, D), f32), SDS((S, D), bf16), SDS((1, D), f32)],
        scratch_shapes=[pltpu.VMEM((SUBLANE, D), f32), pltpu.VMEM((SUBLANE, D), f32)],
        compiler_params=_cparams(("arbitrary",), VMEM_BIG),
        name="down_proj_final_loss",
    )(x1, act, w_down, tgt, w_pfn)


MID_BWD_ROWS = 256


def _mid_bwd(dy, dug, duv, wt_g, wt_v, x1, mo, w_pf, w_pm):
    S, D = dy.shape
    tm = _pick(S, MID_BWD_ROWS)
    nt = S // tm

    def body(dy_ref, dug_ref, duv_ref, wg_ref, wv_ref, x1_ref, mo_ref, wpf_ref, wpm_ref,
             dx1_ref, dmo_ref, gpf_ref, gpm_ref, apf, apm):
        i = pl.program_id(0)

        @pl.when(i == 0)
        def _():
            apf[...] = jnp.zeros_like(apf)
            apm[...] = jnp.zeros_like(apm)

        r1, x1h = _rms_parts(x1_ref[...])
        dh2 = jnp.dot(dug_ref[...], wg_ref[...], preferred_element_type=f32) \
            + jnp.dot(duv_ref[...], wv_ref[...], preferred_element_type=f32)
        apf[...] += _colsum8(dh2 * x1h)
        dx1 = dy_ref[...] + _rms_bwd(x1h, r1, wpf_ref[...], dh2)
        dx1_ref[...] = dx1
        rm, moh = _rms_parts(mo_ref[...])
        apm[...] += _colsum8(dx1 * moh)
        dmo_ref[...] = _rms_bwd(moh, rm, wpm_ref[...], dx1).astype(bf16)

        @pl.when(i == nt - 1)
        def _():
            gpf_ref[...] = jnp.sum(apf[...], axis=0, keepdims=True)
            gpm_ref[...] = jnp.sum(apm[...], axis=0, keepdims=True)

    row = pl.BlockSpec((tm, D), lambda i: (i, 0))
    vec = pl.BlockSpec((1, D), lambda i: (0, 0))
    return pl.pallas_call(
        body,
        grid=(nt,),
        in_specs=[row, pl.BlockSpec((tm, dug.shape[1]), lambda i: (i, 0)), pl.BlockSpec((tm, duv.shape[1]), lambda i: (i, 0)),
                  pl.BlockSpec(wt_g.shape, lambda i: (0, 0)), pl.BlockSpec(wt_v.shape, lambda i: (0, 0)),
                  row, row, vec, vec],
        out_specs=[row, row, vec, vec],
        out_shape=[SDS((S, D), f32), SDS((S, D), bf16), SDS((1, D), f32), SDS((1, D), f32)],
        scratch_shapes=[pltpu.VMEM((SUBLANE, D), f32), pltpu.VMEM((SUBLANE, D), f32)],
        compiler_params=_cparams(("arbitrary",), VMEM_BIG),
        name="dh2_mid_bwd",
    )(dy, dug, duv, wt_g, wt_v, x1, mo, w_pf, w_pm)


def _first_bwd(x, dx1, dh_a, dh_b, dh_c, w_pre):
    S, D = x.shape
    tm = _pick(S, 512)
    nt = S // tm
    nc = D // LANE

    def body(*refs):
        x_ref, dx1_ref, a_ref = refs[:3]
        b_refs, c_refs, w_ref = refs[3:3 + nc], refs[3 + nc:3 + 2 * nc], refs[3 + 2 * nc]
        gx_ref, gw_ref, acc, dh_s, sb, sc = refs[4 + 2 * nc:]
        i = pl.program_id(0)

        @pl.when(i == 0)
        def _():
            acc[...] = jnp.zeros_like(acc)

        for j in range(nc):
            cols = slice(j * LANE, (j + 1) * LANE)
            _to_natural(b_refs[j], sb, 4)
            _to_natural(c_refs[j], sc, 16)
            dh_s[:, cols] = (a_ref[:, cols] + sb[...]) + sc[...]
        r, xh = _rms_parts(x_ref[...])
        dh = dh_s[...]
        acc[...] += _colsum8(dh * xh)
        gx_ref[...] = dx1_ref[...] + _rms_bwd(xh, r, w_ref[...], dh)

        @pl.when(i == nt - 1)
        def _():
            gw_ref[...] = jnp.sum(acc[...], axis=0, keepdims=True)

    row = pl.BlockSpec((tm, D), lambda i: (i, 0))
    vec = pl.BlockSpec((1, D), lambda i: (0, 0))
    perm = lambda d: [pl.BlockSpec((d, tm // d, LANE), lambda i, j=j: (0, i, j)) for j in range(nc)]
    return pl.pallas_call(
        body,
        grid=(nt,),
        in_specs=[row, row, row] + perm(4) + perm(16) + [vec],
        out_specs=[row, vec],
        out_shape=[SDS((S, D), f32), SDS((1, D), f32)],
        scratch_shapes=[pltpu.VMEM((SUBLANE, D), f32), pltpu.VMEM((tm, D), f32), pltpu.VMEM((tm, LANE), f32),
                        pltpu.VMEM((tm, LANE), f32)],
        compiler_params=_cparams(("arbitrary",), VMEM_BIG),
        name="first_bwd",
    )(x, dx1, dh_a, *([dh_b.reshape(4, S // 4, D)] * nc), *([dh_c.reshape(16, S // 16, D)] * nc), w_pre)


def _t5_bucket(dist):
    n = jnp.maximum(dist, 0)
    nf = jnp.maximum(n, 1).astype(f32)
    large = MAX_EXACT + (jnp.log(nf / MAX_EXACT) / math.log(MAX_DISTANCE / MAX_EXACT)
                         * (NUM_BUCKETS - MAX_EXACT)).astype(jnp.int32)
    large = jnp.minimum(large, NUM_BUCKETS - 1)
    return jnp.where(n < MAX_EXACT, n, large)


def _bias_consts(d, after=None):
    if after is not None:
        d, _ = lax.optimization_barrier((jnp.int32(d), after))
    blk = ATTN_BLOCK
    rel = jnp.arange(blk)[:, None] + blk - jnp.arange(2 * blk)[None, :]
    in_win = (rel >= 0) & (rel <= blk)
    bucket = _t5_bucket(rel * d).reshape(1, -1)
    onehot = (bucket == jnp.arange(NUM_BUCKETS)[:, None]).astype(f32)
    return onehot, in_win.astype(f32).reshape(1, -1)


def _bias_build(tab_t, onehot, maskf, name, after):
    H = tab_t.shape[0]

    def body(t_ref, oh_ref, m_ref, after_ref, o_ref):
        b = jnp.dot(t_ref[...], oh_ref[...], precision=HIGHEST, preferred_element_type=f32)
        o_ref[...] = jnp.where(m_ref[...] > 0.5, b, NEG_INF)

    vm = pl.BlockSpec(memory_space=pltpu.VMEM)
    return pl.pallas_call(body, out_shape=SDS((H, onehot.shape[1]), f32), name=name,
                          in_specs=[vm, vm, vm, pl.BlockSpec(memory_space=pl.ANY)], out_specs=vm,
                          )(tab_t, onehot, maskf, after)


def _bias_grad(dbias_flat, onehot, name):
    H = dbias_flat.shape[0]

    def body(g_ref, oh_ref, o_ref):
        o_ref[...] = lax.dot_general(oh_ref[...], g_ref[...], NT, precision=HIGHEST, preferred_element_type=f32)

    return pl.pallas_call(body, out_shape=SDS((NUM_BUCKETS, H), f32), name=name)(dbias_flat, onehot)


ATTN_TILE = 512
ATTN_SUB = ATTN_TILE // ATTN_BLOCK
ATTN_HP = 4
ATTN_WIDE = ATTN_HP * LANE


def _qkv_specs(nt):
    tile = (ATTN_TILE, ATTN_WIDE)
    blk = (ATTN_BLOCK, ATTN_WIDE)
    sec = ATTN_OUT // ATTN_WIDE
    cur = lambda off: (lambda h, t: (jnp.minimum(t, nt - 1), off + h))
    prev = lambda off: (lambda h, t: (jnp.maximum(jnp.minimum(t, nt - 1) * ATTN_SUB - 1, 0), off + h))
    return [pl.BlockSpec(tile, cur(0)), pl.BlockSpec(blk, prev(sec)), pl.BlockSpec(tile, cur(sec)),
            pl.BlockSpec(blk, prev(2 * sec)), pl.BlockSpec(tile, cur(2 * sec))]


def _head_masks():
    lane = lax.broadcasted_iota(jnp.int32, (ATTN_BLOCK, LANE), 1)
    return lane < HEAD_DIM


def _stack_heads(x2, low):
    zero = jnp.zeros_like(x2)
    return jnp.concatenate([jnp.where(low, x2, zero), jnp.where(low, zero, x2)], axis=0)


def _attn_fwd(qkv, bias, bps, name, after=None):
    S = qkv.shape[0]
    nt = S // ATTN_TILE
    scale = HEAD_DIM ** -0.5

    def body(q_ref, kp_ref, kc_ref, vp_ref, vc_ref, b_ref, *rest):
        o_ref, l_ref = rest[-2:]
        t = pl.program_id(1)
        low = _head_masks()
        col = lax.broadcasted_iota(jnp.int32, (2 * ATTN_BLOCK, 2 * ATTN_BLOCK), 1)
        for hp in range(ATTN_HP):
            cols = slice(hp * LANE, (hp + 1) * LANE)
            kk = jnp.concatenate([kp_ref[:, cols], kc_ref[:, cols]], axis=0)
            vv = jnp.concatenate([vp_ref[:, cols], vc_ref[:, cols]], axis=0)
            bias2 = b_ref[2 * hp:2 * hp + 2].reshape(2 * ATTN_BLOCK, 2 * ATTN_BLOCK)
            for b in range(ATTN_SUB):
                lo = b * ATTN_BLOCK
                rows = slice(lo, lo + ATTN_BLOCK)
                keys = slice(lo, lo + 2 * ATTN_BLOCK)
                dead = jnp.logical_and((t * ATTN_SUB + b) % bps == 0, col < ATTN_BLOCK)
                q2 = _stack_heads(q_ref[rows, cols], low)
                kb, vb = kk[keys], vv[keys]
                s = lax.dot_general(q2, kb, NT, preferred_element_type=f32) * scale + bias2
                s = jnp.where(dead, NEG_INF, s)
                m = jnp.max(s, axis=-1, keepdims=True)
                p = jnp.exp(s - m)
                l = jnp.sum(p, axis=-1, keepdims=True)
                o2 = jnp.dot(p.astype(bf16), vb, preferred_element_type=f32) / l
                lse = m + jnp.log(l)
                o_ref[rows, cols] = jnp.where(low, o2[:ATTN_BLOCK], o2[ATTN_BLOCK:])
                l_ref[rows, cols] = jnp.where(low, lse[:ATTN_BLOCK], lse[ATTN_BLOCK:])

    tile = pl.BlockSpec((ATTN_TILE, ATTN_WIDE), lambda h, t: (t, h))
    return pl.pallas_call(
        body,
        grid=(4 // ATTN_HP, nt),
        in_specs=_qkv_specs(nt) + [pl.BlockSpec((2 * ATTN_HP, ATTN_BLOCK, 2 * ATTN_BLOCK), lambda h, t: (h, 0, 0))]
        + ([] if after is None else [pl.BlockSpec(memory_space=pl.ANY)]),
        out_specs=[tile, tile],
        out_shape=[SDS((S, ATTN_OUT), f32), SDS((S, ATTN_OUT), f32)],
        compiler_params=_cparams(("parallel", "parallel")),
        name=name,
    )(qkv, qkv, qkv, qkv, qkv, bias, *([] if after is None else [after]))


def _attn_bwd(qkv, bias, do, dvec, lse, bps, name):
    S = qkv.shape[0]
    nt = S // ATTN_TILE
    scale = HEAD_DIM ** -0.5

    def assemble(parts):
        rows = [parts[0][:ATTN_BLOCK]]
        for b in range(ATTN_SUB - 1):
            rows.append(parts[b][ATTN_BLOCK:] + parts[b + 1][:ATTN_BLOCK])
        rows.append(parts[-1][ATTN_BLOCK:])
        return rows

    def body(q_ref, kp_ref, kc_ref, vp_ref, vc_ref, b_ref, do_ref, dvec_ref, lse_ref,
             dq_ref, dk_ref, dv_ref, db_ref, ck, cv):
        t = pl.program_id(1)
        last = ATTN_TILE - ATTN_BLOCK

        @pl.when(t == 0)
        def _():
            ck[...] = jnp.zeros_like(ck)
            cv[...] = jnp.zeros_like(cv)
            db_ref[...] = jnp.zeros_like(db_ref)

        @pl.when(t < nt)
        def _():
            low = _head_masks()
            col = lax.broadcasted_iota(jnp.int32, (2 * ATTN_BLOCK, 2 * ATTN_BLOCK), 1)
            per_row = lambda t2: jnp.concatenate([t2[:, 0:1], t2[:, HEAD_DIM:HEAD_DIM + 1]], axis=0)
            for hp in range(ATTN_HP):
                cols = slice(hp * LANE, (hp + 1) * LANE)
                kk = jnp.concatenate([kp_ref[:, cols], kc_ref[:, cols]], axis=0)
                vv = jnp.concatenate([vp_ref[:, cols], vc_ref[:, cols]], axis=0)
                bias2 = b_ref[2 * hp:2 * hp + 2].reshape(2 * ATTN_BLOCK, 2 * ATTN_BLOCK)
                dk_parts, dv_parts = [], []
                dsum = None
                for b in range(ATTN_SUB):
                    lo = b * ATTN_BLOCK
                    rows = slice(lo, lo + ATTN_BLOCK)
                    keys = slice(lo, lo + 2 * ATTN_BLOCK)
                    dead = jnp.logical_and((t * ATTN_SUB + b) % bps == 0, col < ATTN_BLOCK)
                    q2 = _stack_heads(q_ref[rows, cols], low)
                    do2 = _stack_heads(do_ref[rows, cols].astype(bf16), low)
                    kb, vb = kk[keys], vv[keys]
                    s = lax.dot_general(q2, kb, NT, preferred_element_type=f32) * scale + bias2
                    s = jnp.where(dead, NEG_INF, s)
                    p = jnp.exp(s - per_row(lse_ref[rows, cols]))
                    dp = lax.dot_general(do2, vb, NT, preferred_element_type=f32)
                    ds = p * (dp - per_row(dvec_ref[rows, cols]))
                    dsum = ds if dsum is None else dsum + ds
                    dsb = ds.astype(bf16)
                    dq2 = jnp.dot(dsb, kb, preferred_element_type=f32) * scale
                    dq_ref[rows, cols] = jnp.where(low, dq2[:ATTN_BLOCK], dq2[ATTN_BLOCK:]).astype(bf16)
                    dk_parts.append(lax.dot_general(dsb, q2, TN, preferred_element_type=f32) * scale)
                    dv_parts.append(lax.dot_general(p.astype(bf16), do2, TN, preferred_element_type=f32))
                db_ref[2 * hp:2 * hp + 2] += dsum.reshape(2, ATTN_BLOCK, 2 * ATTN_BLOCK)
                for parts, carry, out_ref in ((dk_parts, ck, dk_ref), (dv_parts, cv, dv_ref)):
                    rws = assemble(parts)
                    out_ref[:last, cols] = carry[:last, cols].astype(bf16)
                    out_ref[last:, cols] = (carry[last:, cols] + rws[0]).astype(bf16)
                    for b in range(ATTN_SUB):
                        carry[b * ATTN_BLOCK:(b + 1) * ATTN_BLOCK, cols] = rws[b + 1]

        @pl.when(t == nt)
        def _():
            dk_ref[...] = ck[...].astype(bf16)
            dv_ref[...] = cv[...].astype(bf16)

    tile = (ATTN_TILE, ATTN_WIDE)
    cur = pl.BlockSpec(tile, lambda h, t: (jnp.minimum(t, nt - 1), h))
    lag = pl.BlockSpec(tile, lambda h, t: (jnp.maximum(t - 1, 0), h))
    bspec = pl.BlockSpec((2 * ATTN_HP, ATTN_BLOCK, 2 * ATTN_BLOCK), lambda h, t: (h, 0, 0))
    return pl.pallas_call(
        body,
        grid=(4 // ATTN_HP, nt + 1),
        in_specs=_qkv_specs(nt) + [bspec, cur, cur, cur],
        out_specs=[cur, lag, lag, bspec],
        out_shape=[SDS((S, ATTN_OUT), bf16), SDS((S, ATTN_OUT), bf16), SDS((S, ATTN_OUT), bf16),
                   SDS((8, ATTN_BLOCK, 2 * ATTN_BLOCK), f32)],
        scratch_shapes=[pltpu.VMEM(tile, f32), pltpu.VMEM(tile, f32)],
        compiler_params=_cparams(("parallel", "arbitrary")),
        name=name,
    )(qkv, qkv, qkv, qkv, qkv, bias, do, dvec, lse)


def _attn_merge(o0, o1, o2, l0, l1, l2):
    S, W = o0.shape
    R = PERM_ROWS

    def body(o0_ref, o1_ref, o2_ref, l0_ref, l1_ref, l2_ref, y_ref, yb_ref, w0_ref, w1_ref, w2_ref,
             so1, so2, sl1, sl2):
        _to_natural(o1_ref, so1, 4)
        _to_natural(l1_ref, sl1, 4)
        _to_natural(o2_ref, so2, 16)
        _to_natural(l2_ref, sl2, 16)
        a, b, c = l0_ref[...], sl1[...], sl2[...]
        m = jnp.maximum(jnp.maximum(a, b), c)
        ea, eb, ec = jnp.exp(a - m), jnp.exp(b - m), jnp.exp(c - m)
        den = (ea + eb) + ec
        w0, w1, w2 = ea / den, eb / den, ec / den
        y = (w0 * o0_ref[...] + w1 * so1[...]) + w2 * so2[...]
        y_ref[...] = y
        yb_ref[...] = y.astype(bf16)
        w0_ref[...] = w0
        w1_ref[...] = w1
        w2_ref[...] = w2

    nat = pl.BlockSpec((R, LANE), lambda i, j: (i, j))
    v4 = lambda t: t.reshape(4, S // 4, W)
    v16 = lambda t: t.reshape(16, S // 16, W)
    return pl.pallas_call(
        body,
        grid=(S // R, W // LANE),
        in_specs=[nat, _perm_spec(4), _perm_spec(16)] * 2,
        out_specs=[nat] * 5,
        out_shape=[SDS((S, W), f32), SDS((S, W), bf16)] + [SDS((S, W), f32)] * 3,
        scratch_shapes=[pltpu.VMEM((R, LANE), f32)] * 4,
        compiler_params=_cparams(("parallel", "parallel"), VMEM_BIG),
        name="attn_merge",
    )(o0, v4(o1), v16(o2), l0, v4(l1), v16(l2))


def _attn_merge_bwd(dy, y, w0, w1, w2, after=None):
    S, W = dy.shape
    R = PERM_ROWS

    def body(dy_ref, y_ref, w0_ref, w1_ref, w2_ref, *rest):
        a0, a1, a2, b0, b1, b2, sa, sb = rest[-8:]
        dyv = dy_ref[...]
        r = lax.broadcasted_iota(jnp.int32, (LANE, LANE), 0) // HEAD_DIM
        c = lax.broadcasted_iota(jnp.int32, (LANE, LANE), 1) // HEAD_DIM
        seg = jnp.where(r == c, 1.0, 0.0).astype(f32)
        cbar = jnp.dot(dyv * y_ref[...], seg, precision=HIGHEST, preferred_element_type=f32)
        w = w0_ref[...]
        a0[...] = (w * dyv).astype(bf16)
        b0[...] = w * cbar
        for d, w_ref, a_ref, b_ref in ((4, w1_ref, a1, b1), (16, w2_ref, a2, b2)):
            w = w_ref[...]
            sa[...] = w * dyv
            sb[...] = w * cbar
            n = R // d
            for k in range(d):
                rows = pl.ds(k, n, stride=d)
                a_ref[k] = sa[rows, :].astype(bf16)
                b_ref[k] = sb[rows, :]

    nat = pl.BlockSpec((R, LANE), lambda i, j: (i, j))
    shapes = lambda dt: [SDS((S, W), dt), SDS((4, S // 4, W), dt), SDS((16, S // 16, W), dt)]
    outs = pl.pallas_call(
        body,
        grid=(S // R, W // LANE),
        in_specs=[nat] * 5 + ([] if after is None else [pl.BlockSpec(memory_space=pl.ANY)]),
        out_specs=[nat, _perm_spec(4), _perm_spec(16)] * 2,
        out_shape=shapes(bf16) + shapes(f32),
        scratch_shapes=[pltpu.VMEM((R, LANE), f32)] * 2,
        compiler_params=_cparams(("parallel", "parallel"), VMEM_BIG),
        name="attn_merge_bwd",
    )(dy, y, w0, w1, w2, *([] if after is None else [after]))
    return [t.reshape(S, W) for t in outs]


HGRN_SB = 256
HGRN_PAIR = 4


def _chunk_masks():
    r = jnp.arange(HGRN_SB)[:, None]
    c = jnp.arange(HGRN_SB)[None, :]
    same = (r // HGRN_CHUNK) == (c // HGRN_CHUNK)
    return jnp.stack([same & (c <= r), same, same & (c >= r)]).astype(bf16)


def _mask_dot(mask, x):
    hi = x.astype(bf16)
    r1 = x - hi.astype(f32)
    mid = r1.astype(bf16)
    lo = (r1 - mid.astype(f32)).astype(bf16)
    p = jnp.dot(mask, jnp.concatenate([hi, mid, lo], axis=1), preferred_element_type=f32)
    n = x.shape[1]
    return (p[:, :n] + p[:, n:2 * n]) + p[:, 2 * n:]


def _hgrn_prep(q_raw, f_raw, lbv, tril, same):
    sq = _sigmoid(q_raw)
    qs = q_raw * sq
    sig = _sigmoid(f_raw)
    f = lbv + (1.0 - lbv) * sig
    g = jnp.log(f)
    k = 1.0 - f
    G = _mask_dot(tril, g)
    GL = _mask_dot(same, g)
    eG = jnp.exp(G)
    einv = jnp.exp(-G)
    edec = jnp.exp(GL - G)
    return dict(sq=sq, qs=qs, sig=sig, f=f, k=k, eG=eG, einv=einv, edec=edec, eGL=jnp.exp(GL),
                qt=qs * eG, kt=k * einv, kd=k * edec)


def _hgrn_fwd(hg, lb, normw):
    S = hg.shape[0]
    sb = HGRN_SB
    nsb = S // sb
    nch = sb // HGRN_CHUNK

    def body(q_ref, f_ref, v_ref, og_ref, lb_ref, nw_ref, m_ref, y_ref, o_ref, ck_ref, st):
        j = pl.program_id(1)

        @pl.when(j == 0)
        def _():
            st[...] = jnp.zeros_like(st)

        tril_m = m_ref[0]
        tril = tril_m.astype(f32) > 0.5

        def one_head(hh):
            cols = slice(hh * LANE, (hh + 1) * LANE)
            ST = st[hh]
            ck_ref[hh, 0] = ST
            pr = _hgrn_prep(q_ref[:, cols], f_ref[:, cols], lb_ref[:, cols], tril_m, m_ref[1])
            qtb, ktb, kdb = pr["qt"].astype(bf16), pr["kt"].astype(bf16), pr["kd"].astype(bf16)
            eGL = pr["eGL"]
            vb = v_ref[:, cols].astype(bf16)
            A = jnp.where(tril, lax.dot_general(qtb, ktb, NT, preferred_element_type=f32), 0.0)
            o = jnp.dot(A.astype(bf16), vb, preferred_element_type=f32)
            outs = []
            for ci in range(nch):
                lo = ci * HGRN_CHUNK
                sl = slice(lo, lo + HGRN_CHUNK)
                outs.append(o[sl] + lax.dot_general(qtb[sl], ST.astype(bf16), NT, preferred_element_type=f32))
                ST = ST * eGL[lo:lo + 1, :] + lax.dot_general(vb[sl], kdb[sl], TN, preferred_element_type=f32)
            st[hh] = ST
            of = jnp.concatenate(outs, axis=0)
            o_ref[:, cols] = of
            rms = lax.rsqrt(jnp.mean(of * of, axis=-1, keepdims=True) + EPS)
            ogv = og_ref[:, cols]
            y_ref[:, cols] = ((of * rms * nw_ref[...]) * (ogv * _sigmoid(ogv))).astype(bf16)

        for hh in range(HGRN_PAIR):
            one_head(hh)

    wide = HGRN_PAIR * LANE
    col = lambda off: pl.BlockSpec((sb, wide), lambda h, j: (j, off // HGRN_PAIR + h))
    return pl.pallas_call(
        body,
        grid=(4 // HGRN_PAIR, nsb),
        in_specs=[col(0), col(4), col(8), col(12), pl.BlockSpec((1, wide), lambda h, j: (0, h)),
                  pl.BlockSpec((1, LANE), lambda h, j: (0, 0)),
                  pl.BlockSpec((3, sb, sb), lambda h, j: (0, 0, 0))],
        out_specs=[col(0), col(0), pl.BlockSpec((HGRN_PAIR, 1, LANE, LANE), lambda h, j: (h, j, 0, 0))],
        out_shape=[SDS((S, HGRN_W), bf16), SDS((S, HGRN_W), f32), SDS((4, nsb, LANE, LANE), f32)],
        scratch_shapes=[pltpu.VMEM((HGRN_PAIR, LANE, LANE), f32)],
        compiler_params=_cparams(("parallel", "arbitrary")),
        name="hgrn_fwd",
    )(hg, hg, hg, hg, lb, normw, _chunk_masks())


def _hgrn_bwd(hg, o_raw, dy, ck, lb, normw, after=None):
    S = hg.shape[0]
    sb = HGRN_SB
    nsb = S // sb
    nch = sb // HGRN_CHUNK

    def body(q_ref, f_ref, v_ref, og_ref, o_ref, dy_ref, ck_ref, lb_ref, nw_ref, m_ref, *rest):
        dq_ref, df_ref, dv_ref, dog_ref, glb_ref, gnw_ref, dst, alb, anw = rest[-9:]
        j = pl.program_id(1)

        @pl.when(j == 0)
        def _():
            dst[...] = jnp.zeros_like(dst)
            alb[...] = jnp.zeros_like(alb)
            anw[...] = jnp.zeros_like(anw)

        tril_m = m_ref[0]
        tril = tril_m.astype(f32) > 0.5
        nw = nw_ref[...]

        def one_head(hh):
            cols = slice(hh * LANE, (hh + 1) * LANE)
            lbv = lb_ref[:, cols]
            q_raw = q_ref[:, cols]
            pr = _hgrn_prep(q_raw, f_ref[:, cols], lbv, tril_m, m_ref[1])
            qt, kt, kd, eGL = pr["qt"], pr["kt"], pr["kd"], pr["eGL"]
            qtb, ktb, kdb = qt.astype(bf16), kt.astype(bf16), kd.astype(bf16)
            vb = v_ref[:, cols].astype(bf16)

            o = o_ref[:, cols]
            ogv = og_ref[:, cols]
            sog = _sigmoid(ogv)
            rms = lax.rsqrt(jnp.mean(o * o, axis=-1, keepdims=True) + EPS)
            oh = o * rms
            dyv = dy_ref[:, cols]
            dog_ref[:, cols] = (dyv * (oh * nw) * (sog * (1.0 + ogv * (1.0 - sog)))).astype(bf16)
            dohw = dyv * (ogv * sog)
            anw[:, cols] += _colsum8(dohw * oh)
            doh = dohw * nw
            do = rms * (doh - oh * jnp.mean(doh * oh, axis=-1, keepdims=True))
            dob = do.astype(bf16)

            Ab = jnp.where(tril, lax.dot_general(qtb, ktb, NT, preferred_element_type=f32), 0.0).astype(bf16)
            dAb = jnp.where(tril, lax.dot_general(dob, vb, NT, preferred_element_type=f32), 0.0).astype(bf16)
            dv_acc = lax.dot_general(Ab, dob, TN, preferred_element_type=f32)
            dqt = jnp.dot(dAb, ktb, preferred_element_type=f32)
            dkt = lax.dot_general(dAb, qtb, TN, preferred_element_type=f32)

            ST = ck_ref[hh, 0]
            states = []
            for ci in range(nch):
                lo = ci * HGRN_CHUNK
                sl = slice(lo, lo + HGRN_CHUNK)
                states.append(ST)
                ST = ST * eGL[lo:lo + 1, :] + lax.dot_general(vb[sl], kdb[sl], TN, preferred_element_type=f32)

            dST = dst[hh]
            dqt_i, dkd_i, dv_i, deg_i = [None] * nch, [None] * nch, [None] * nch, [None] * nch
            for ci in reversed(range(nch)):
                lo = ci * HGRN_CHUNK
                sl = slice(lo, lo + HGRN_CHUNK)
                ST0 = states[ci]
                dSTb = dST.astype(bf16)
                dv_i[ci] = lax.dot_general(kdb[sl], dSTb, NT, preferred_element_type=f32)
                dqt_i[ci] = jnp.dot(dob[sl], ST0.astype(bf16), preferred_element_type=f32)
                dkd_i[ci] = jnp.dot(vb[sl], dSTb, preferred_element_type=f32)
                deg_i[ci] = jnp.broadcast_to(jnp.sum(dST * ST0, axis=0, keepdims=True), (HGRN_CHUNK, LANE))
                dST = dST * eGL[lo:lo + 1, :] + lax.dot_general(dob[sl], qtb[sl], TN, preferred_element_type=f32)
            dst[hh] = dST

            dqt = dqt + jnp.concatenate(dqt_i, axis=0)
            dkd = jnp.concatenate(dkd_i, axis=0)
            dv_ref[:, cols] = (dv_acc + jnp.concatenate(dv_i, axis=0)).astype(bf16)
            deg = jnp.concatenate(deg_i, axis=0)

            dqs = dqt * pr["eG"]
            dkdkd = dkd * kd
            dG = dqt * qt - dkt * kt - dkdkd
            dk = dkt * pr["einv"] + dkd * pr["edec"]
            dGL = _mask_dot(m_ref[1], dkdkd) + eGL * deg
            dg = _mask_dot(m_ref[2], dG) + dGL
            df = dg / pr["f"] - dk
            sig = pr["sig"]
            df_ref[:, cols] = (df * (1.0 - lbv) * (sig * (1.0 - sig))).astype(bf16)
            alb[:, cols] += _colsum8(df * (1.0 - sig))
            sq = pr["sq"]
            dq_ref[:, cols] = (dqs * (sq * (1.0 + q_raw * (1.0 - sq)))).astype(bf16)

        for hh in range(HGRN_PAIR):
            one_head(hh)

        @pl.when(j == nsb - 1)
        def _():
            glb_ref[...] = jnp.broadcast_to(jnp.sum(alb[...], axis=0, keepdims=True), (SUBLANE, wide))
            gnw_ref[...] = jnp.broadcast_to(jnp.sum(anw[...], axis=0, keepdims=True), (SUBLANE, wide))

    wide = HGRN_PAIR * LANE
    rev = lambda off: pl.BlockSpec((sb, wide), lambda h, j: (nsb - 1 - j, off // HGRN_PAIR + h))
    stat = pl.BlockSpec((SUBLANE, wide), lambda h, j: (0, h))
    return pl.pallas_call(
        body,
        grid=(4 // HGRN_PAIR, nsb),
        in_specs=[rev(0), rev(4), rev(8), rev(12), rev(0), rev(0),
                  pl.BlockSpec((HGRN_PAIR, 1, LANE, LANE), lambda h, j: (h, nsb - 1 - j, 0, 0)),
                  pl.BlockSpec((1, wide), lambda h, j: (0, h)), pl.BlockSpec((1, LANE), lambda h, j: (0, 0)),
                  pl.BlockSpec((3, sb, sb), lambda h, j: (0, 0, 0))]
        + ([] if after is None else [pl.BlockSpec(memory_space=pl.ANY)]),
        out_specs=[rev(0), rev(0), rev(0), rev(0), stat, stat],
        out_shape=[SDS((S, HGRN_W), bf16)] * 4 + [SDS((SUBLANE, HGRN_W), f32)] * 2,
        scratch_shapes=[pltpu.VMEM((HGRN_PAIR, LANE, LANE), f32), pltpu.VMEM((SUBLANE, wide), f32),
                        pltpu.VMEM((SUBLANE, wide), f32)],
        compiler_params=_cparams(("parallel", "arbitrary")),
        name="hgrn_bwd",
    )(hg, hg, hg, hg, o_raw, dy, ck, lb, normw, _chunk_masks(), *([] if after is None else [after]))


def _lb_fwd(raw):
    def body(r_ref, o_ref):
        r = r_ref[...]
        m = jnp.max(r, axis=0, keepdims=True)
        e = jnp.exp(r - m)
        o_ref[...] = (e / jnp.sum(e, axis=0, keepdims=True))[0:1]

    return pl.pallas_call(body, out_shape=SDS((1, raw.shape[1]), f32), name="lb_fwd")(raw)


def _lb_bwd(raw, dlb):
    def body(r_ref, d_ref, o_ref):
        r = r_ref[...]
        m = jnp.max(r, axis=0, keepdims=True)
        e = jnp.exp(r - m)
        s = e / jnp.sum(e, axis=0, keepdims=True)
        s0 = s[0:1]
        onehot0 = jnp.where(lax.broadcasted_iota(jnp.int32, r.shape, 0) == 0, 1.0, 0.0)
        o_ref[...] = d_ref[...] * s0 * (onehot0 - s)

    return pl.pallas_call(body, out_shape=SDS(raw.shape, f32), name="lb_bwd")(raw, dlb)


def _gate_fwd(ya, yh, w_ba, w_bh, gc):
    S = ya.shape[0]
    D = w_ba.shape[1]
    tm = _pick(S, MM_ROWS)

    def body(ya_ref, yh_ref, wa_ref, wh_ref, g0_ref, g1_ref, a_ref, b_ref, o_ref):
        a = jnp.dot(ya_ref[...], wa_ref[...], preferred_element_type=f32).astype(bf16)
        b = jnp.dot(yh_ref[...], wh_ref[...], preferred_element_type=f32).astype(bf16)
        a_ref[...] = a
        b_ref[...] = b
        s0, s1 = _sigmoid(g0_ref[...].astype(f32)), _sigmoid(g1_ref[...].astype(f32))
        o_ref[...] = (s0 * a.astype(f32) + s1 * b.astype(f32)).astype(bf16)

    row = pl.BlockSpec((tm, D), lambda i: (i, 0))
    act = pl.BlockSpec((tm, ya.shape[1]), lambda i: (i, 0))
    wspec = pl.BlockSpec(w_ba.shape, lambda i: (0, 0))
    return pl.pallas_call(
        body,
        grid=(S // tm,),
        in_specs=[act, act, wspec, wspec, row, pl.BlockSpec((tm, D), lambda i: (i, 1))],
        out_specs=[row, row, row],
        out_shape=[SDS((S, D), bf16)] * 3,
        compiler_params=_cparams(("parallel",), VMEM_BIG),
        name="branch_gate_fwd",
    )(ya, yh, w_ba, w_bh, gc, gc)


def _gate_bwd(dmo, w_out, a, b, gc, w_ba, w_bh):
    S, D = a.shape
    W = w_ba.shape[0]
    tm = _pick(S, MM_ROWS)

    def body(dmo_ref, wo_ref, a_ref, b_ref, g0_ref, g1_ref, wa_ref, wh_ref,
             da_ref, db_ref, dg_ref, dya_ref, dyh_ref):
        dm = lax.dot_general(dmo_ref[...], wo_ref[...], NT, preferred_element_type=f32)
        dmv = dm.astype(bf16).astype(f32)
        s0, s1 = _sigmoid(g0_ref[...].astype(f32)), _sigmoid(g1_ref[...].astype(f32))
        da = (dmv * s0).astype(bf16)
        db = (dmv * s1).astype(bf16)
        da_ref[...] = da
        db_ref[...] = db
        dg_ref[:, :D] = (dmv * a_ref[...].astype(f32) * (s0 * (1.0 - s0))).astype(bf16)
        dg_ref[:, D:] = (dmv * b_ref[...].astype(f32) * (s1 * (1.0 - s1))).astype(bf16)
        dya_ref[...] = lax.dot_general(da, wa_ref[...], NT, preferred_element_type=f32)
        dyh_ref[...] = lax.dot_general(db, wh_ref[...], NT, preferred_element_type=f32)

    row = pl.BlockSpec((tm, D), lambda i: (i, 0))
    wide = pl.BlockSpec((tm, 2 * D), lambda i: (i, 0))
    narrow = pl.BlockSpec((tm, W), lambda i: (i, 0))
    whole = lambda t: pl.BlockSpec(t.shape, lambda i: (0, 0))
    return pl.pallas_call(
        body,
        grid=(S // tm,),
        in_specs=[row, whole(w_out), row, row, row, pl.BlockSpec((tm, D), lambda i: (i, 1)), whole(w_ba), whole(w_bh)],
        out_specs=[row, row, wide, narrow, narrow],
        out_shape=[SDS((S, D), bf16), SDS((S, D), bf16), SDS((S, 2 * D), bf16), SDS((S, W), f32), SDS((S, W), f32)],
        compiler_params=_cparams(("parallel",), VMEM_BIG),
        name="gate_bwd_fused",
    )(dmo, w_out, a, b, gc, gc, w_ba, w_bh)


CONV_ROWS = 512
INV_SQRT2 = 0.7071067811865476
INV_SQRT_2PI = 0.3989422804014327


CONV_HALO = 16


def _shift_down(cur, prev, k):
    x = pltpu.roll(cur, k, 0)
    row = lax.broadcasted_iota(jnp.int32, (SUBLANE, LANE), 0)
    head = jnp.where(row < k, pltpu.roll(prev, k, 0)[:SUBLANE], x[:SUBLANE])
    return jnp.concatenate([head, x[SUBLANE:]], axis=0)


def _shift_up(cur, nxt, k):
    R = cur.shape[0]
    x = pltpu.roll(cur, R - k, 0)
    row = lax.broadcasted_iota(jnp.int32, (SUBLANE, LANE), 0)
    tail = jnp.where(row >= SUBLANE - k, pltpu.roll(nxt, SUBLANE - k, 0), x[R - SUBLANE:])
    return jnp.concatenate([x[:R - SUBLANE], tail], axis=0)


def _conv_rows(u_ref, w, b, r0, first):
    R = CONV_ROWS
    cur = u_ref[pl.ds(r0, R), :].astype(f32)
    prev = u_ref[pl.ds(pl.multiple_of(jnp.maximum(r0 - CONV_HALO, 0), CONV_HALO), CONV_HALO), :].astype(f32)
    prev = jnp.where(first, 0.0, prev)
    x1 = _shift_down(cur, prev, 1)
    x2 = _shift_down(cur, prev, 2)
    c = ((b + w[0:1] * x2) + w[1:2] * x1) + w[2:3] * cur
    return c, x2, x1, cur


def _conv_fwd(ug, uv, wg, wv, bg, bv):
    S, F = ug.shape
    nchunk = S // CONV_ROWS

    def body(ug_ref, uv_ref, wg_ref, wv_ref, bg_ref, bv_ref, o_ref):
        wgv, wvv, bgv, bvv = wg_ref[...], wv_ref[...], bg_ref[...], bv_ref[...]

        def step(ci, carry):
            r0 = pl.multiple_of(ci * CONV_ROWS, CONV_ROWS)
            cg = _conv_rows(ug_ref, wgv, bgv, r0, ci == 0)[0]
            cv = _conv_rows(uv_ref, wvv, bvv, r0, ci == 0)[0]
            gelu = 0.5 * cg * (1.0 + lax.erf(cg * INV_SQRT2))
            o_ref[pl.ds(r0, CONV_ROWS), :] = (gelu * cv).astype(bf16)
            return carry

        lax.fori_loop(0, nchunk, step, 0)

    col = pl.BlockSpec((S, LANE), lambda j: (0, j))
    w3 = pl.BlockSpec((3, LANE), lambda j: (0, j))
    b1 = pl.BlockSpec((1, LANE), lambda j: (0, j))
    return pl.pallas_call(
        body,
        grid=(F // LANE,),
        in_specs=[col, col, w3, w3, b1, b1],
        out_specs=col,
        out_shape=SDS((S, F), bf16),
        compiler_params=_cparams(("parallel",), VMEM_BIG),
        name="conv_fwd",
    )(ug, uv, wg, wv, bg, bv)


def _conv_bwd(ug, uv, dact, wg, wv, bg, bv):
    S, F = ug.shape
    R = CONV_ROWS
    nchunk = S // R

    def body(ug_ref, uv_ref, da_ref, wg_ref, wv_ref, bg_ref, bv_ref, dug_ref, duv_ref, sg_ref, sv_ref, dcg, dcv):
        wgv, wvv, bgv, bvv = wg_ref[...], wv_ref[...], bg_ref[...], bv_ref[...]
        zero = jnp.zeros((SUBLANE, LANE), f32)

        def fwd_step(ci, acc):
            r0 = pl.multiple_of(ci * R, R)
            cg, g2, g1, g0 = _conv_rows(ug_ref, wgv, bgv, r0, ci == 0)
            cv, v2, v1, v0 = _conv_rows(uv_ref, wvv, bvv, r0, ci == 0)
            da = da_ref[pl.ds(r0, R), :].astype(f32)
            cdf = 0.5 * (1.0 + lax.erf(cg * INV_SQRT2))
            pdf = INV_SQRT_2PI * jnp.exp(-0.5 * cg * cg)
            dg = da * cv * (cdf + cg * pdf)
            dv = da * (cg * cdf)
            dcg[pl.ds(r0, R), :] = dg
            dcv[pl.ds(r0, R), :] = dv
            new = (acc[0] + _colsum8(dg * g2), acc[1] + _colsum8(dg * g1), acc[2] + _colsum8(dg * g0),
                   acc[3] + _colsum8(dg),
                   acc[4] + _colsum8(dv * v2), acc[5] + _colsum8(dv * v1), acc[6] + _colsum8(dv * v0),
                   acc[7] + _colsum8(dv))
            return new

        acc = lax.fori_loop(0, nchunk, fwd_step, (zero,) * 8)
        rows = lax.broadcasted_iota(jnp.int32, (SUBLANE, LANE), 0)

        def stats(parts):
            out = jnp.zeros((SUBLANE, LANE), f32)
            for k, pt in enumerate(parts):
                out = jnp.where(rows == k, jnp.sum(pt, axis=0, keepdims=True), out)
            return out

        sg_ref[...] = stats(acc[0:4])
        sv_ref[...] = stats(acc[4:8])

        def du_rows(dc, w, r0, last):
            cur = dc[pl.ds(r0, R), :]
            nxt = dc[pl.ds(pl.multiple_of(jnp.minimum(r0 + R, S - SUBLANE), SUBLANE), SUBLANE), :]
            nxt = jnp.where(last, 0.0, nxt)
            return w[2:3] * cur + w[1:2] * _shift_up(cur, nxt, 1) + w[0:1] * _shift_up(cur, nxt, 2)

        def bwd_step(ci, carry):
            r0 = pl.multiple_of(ci * R, R)
            last = ci == nchunk - 1
            dug_ref[pl.ds(r0, R), :] = du_rows(dcg, wgv, r0, last).astype(bf16)
            duv_ref[pl.ds(r0, R), :] = du_rows(dcv, wvv, r0, last).astype(bf16)
            return carry

        lax.fori_loop(0, nchunk, bwd_step, 0)

    col = pl.BlockSpec((S, LANE), lambda j: (0, j))
    w3 = pl.BlockSpec((3, LANE), lambda j: (0, j))
    b1 = pl.BlockSpec((1, LANE), lambda j: (0, j))
    st = pl.BlockSpec((SUBLANE, LANE), lambda j: (0, j))
    return pl.pallas_call(
        body,
        grid=(F // LANE,),
        in_specs=[col, col, col, w3, w3, b1, b1],
        out_specs=[col, col, st, st],
        out_shape=[SDS((S, F), bf16), SDS((S, F), bf16), SDS((SUBLANE, F), f32), SDS((SUBLANE, F), f32)],
        scratch_shapes=[pltpu.VMEM((S, LANE), f32), pltpu.VMEM((S, LANE), f32)],
        compiler_params=_cparams(("parallel",), VMEM_BIG),
        name="conv_bwd",
    )(ug, uv, dact, wg, wv, bg, bv)


def _adam_math(w, g, m, v):
    m = ADAM_B1 * m + (1.0 - ADAM_B1) * g
    v = ADAM_B2 * v + (1.0 - ADAM_B2) * (g * g)
    m_hat = m / (1.0 - ADAM_B1 ** ADAM_STEP)
    v_hat = v / (1.0 - ADAM_B2 ** ADAM_STEP)
    delta = -ADAM_LR * (m_hat / (jnp.sqrt(v_hat) + ADAM_EPS) + ADAM_WD * w)
    return delta, m, v


def _adamw(w, m, v, g, name):
    R, C = w.shape
    parts = len(g.shape) == 3
    tr = R
    if R % 16 == 0:
        for t in range(R, 0, -16):
            if R % t == 0 and t * C * 4 <= ADAM_BLOCK_BYTES:
                tr = t
                break

    def body(w_ref, m_ref, v_ref, g_ref, go_ref, d_ref, mo_ref, vo_ref):
        if parts:
            gv = ((g_ref[0].astype(f32) + g_ref[1].astype(f32)) + g_ref[2].astype(f32)) + g_ref[3].astype(f32)
        else:
            gv = g_ref[...]
        go_ref[...] = gv
        d, mn, vn = _adam_math(w_ref[...], gv, m_ref[...], v_ref[...])
        d_ref[...] = d
        mo_ref[...] = mn
        vo_ref[...] = vn

    row = pl.BlockSpec((tr, C), lambda i: (i, 0))
    gspec = pl.BlockSpec((4, tr, C), lambda i: (0, i, 0)) if parts else row
    if isinstance(g, _Rows):
        lo, g = g.lo, g.full
        assert lo % (2 * SUBLANE) == 0 and tr % (2 * SUBLANE) == 0
        gspec = pl.BlockSpec((pl.Element(4), pl.Element(tr), pl.Element(C)),
                             lambda i: (0, pl.multiple_of(lo + i * tr, 2 * SUBLANE), 0))
    return pl.pallas_call(
        body,
        grid=(R // tr,),
        in_specs=[row, row, row, gspec],
        out_specs=[row] * 4,
        out_shape=[SDS((R, C), f32)] * 4,
        compiler_params=_cparams(("parallel",), VMEM_BIG),
        name=name,
    )(w, m, v, g)


def _sum8(parts, name):
    _, _, R, C = parts.shape

    def body(p_ref, o_ref):
        acc = p_ref[0, 0]
        for c in range(2):
            for k in range(4):
                if c or k:
                    acc = acc + p_ref[c, k]
        o_ref[...] = acc

    return pl.pallas_call(body, out_shape=SDS((R, C), f32), name=name)(parts)


def _pair_add(by_core, b, name):
    _, K, R, C = by_core.shape
    tr = R // 2 if R % 32 == 0 else R

    def body(c_ref, a_ref, b_ref, o_ref):
        o_ref[...] = (a_ref[0].astype(f32) + b_ref[...].astype(f32)).astype(bf16)

    blk = pl.BlockSpec((1, tr, C), lambda k, i, c: (k, i, 0))
    return pl.pallas_call(
        body,
        grid_spec=pltpu.PrefetchScalarGridSpec(
            num_scalar_prefetch=1,
            grid=(K, R // tr),
            in_specs=[pl.BlockSpec((1, 1, tr, C), lambda k, i, c: (c[0], k, i, 0)), blk],
            out_specs=blk,
        ),
        out_shape=SDS((K, R, C), bf16),
        compiler_params=_cparams(("parallel", "parallel")),
        name=name,
    )(lax.axis_index("c").astype(jnp.int32).reshape(1), by_core, b)


_ANY = pl.BlockSpec(memory_space=pl.ANY)


def _chip_out_shape(src, gather):
    return SDS((4,) + tuple(src.shape if gather else src.shape[1:]), src.dtype)


def _fill_own(out, src, gather):
    mine = 2 * lax.axis_index("x") + lax.axis_index("y")
    own = src if gather else lax.dynamic_index_in_dim(src, mine, axis=0, keepdims=False)
    return lax.dynamic_update_index_in_dim(out, own, mine, axis=0)


_HBM = pl.BlockSpec(memory_space=pltpu.HBM)
_SEM = pl.BlockSpec(memory_space=pltpu.SEMAPHORE)
_EFFECT = pltpu.SideEffectType.DATAFLOW_SIDE_EFFECTING
_SPLIT_PEERS = {"chip_gather": 3, "chip_gather_wide": 3, "chip_xchg": 3, "core_fill": 4, "core_swap": 1}


def _split_land(src, kind):
    if kind == "chip_gather_wide":
        return SDS((4, 2) + tuple(src.shape), src.dtype)
    if kind == "core_fill":
        return SDS((SUBLANE, LANE), src.dtype)
    if kind == "core_swap":
        return SDS(tuple(src.shape[1:]), src.dtype)
    return _chip_out_shape(src, kind == "chip_gather")


def _split_copies(src_ref, land_ref, sems, kind):
    x, y, c = lax.axis_index("x"), lax.axis_index("y"), lax.axis_index("c")
    n = _SPLIT_PEERS[kind]
    if kind == "core_fill":
        routes = [((x, y, 1 - c), src_ref.at[k, c], src_ref.at[k, c], src_ref.at[k, 1 - c]) for k in range(n)]
    elif kind == "core_swap":
        routes = [((x, y, 1 - c), src_ref.at[1 - c], land_ref, land_ref)]
    else:
        mine = 2 * x + y
        gather = kind != "chip_xchg"
        slot = (lambda k: land_ref.at[k, c]) if kind == "chip_gather_wide" else (lambda k: land_ref.at[k])
        routes = [((px, py, c), src_ref if gather else src_ref.at[2 * px + py], slot(mine), slot(2 * px + py))
                  for px, py in [(1 - x, y), (x, 1 - y), (1 - x, 1 - y)]]
    sends, recvs = [], []
    for j, (peer, piece, there, here) in enumerate(routes):
        sends.append(pltpu.make_async_remote_copy(src_ref=piece, dst_ref=there, send_sem=sems[j],
                                                  recv_sem=sems[n + j], device_id=peer, device_id_type=MESH))
        recvs.append(pltpu.make_async_remote_copy(src_ref=piece, dst_ref=here, send_sem=sems[j],
                                                  recv_sem=sems[n + j], device_id=peer, device_id_type=MESH))
    return sends, recvs


def _split_start(src, kind, name, after=None):
    land = _split_land(src, kind)
    ns = 2 * _SPLIT_PEERS[kind]
    n_in = 2 if after is None else 3

    def body(*refs):
        src_ref, land_ref = refs[:2]
        outs = refs[n_in:]
        for cp in _split_copies(src_ref, land_ref, outs[:ns], kind)[0]:
            cp.start()
        token = outs[ns + 2]
        token[...] = jnp.zeros_like(token)

    res = pl.pallas_call(
        body,
        name=name,
        out_shape=(pltpu.SemaphoreType.DMA(()),) * ns
        + (pltpu.HBM(src.shape, src.dtype), pltpu.HBM(land.shape, land.dtype), SDS((SUBLANE, LANE), f32)),
        in_specs=(_HBM, _HBM) + (() if after is None else (_ANY,)),
        out_specs=(_SEM,) * ns + (_HBM, _HBM, pl.BlockSpec(memory_space=pltpu.VMEM)),
        input_output_aliases={0: ns, 1: ns + 1},
        compiler_params=pltpu.CompilerParams(has_side_effects=_EFFECT),
    )(pltpu.with_memory_space_constraint(src, pltpu.HBM),
      pltpu.with_memory_space_constraint(lax.empty(land.shape, land.dtype), pltpu.HBM),
      *(() if after is None else (after,)))
    return (res[:ns], res[ns], res[ns + 1]), res[ns + 2]


def _split_wait(state, after, kind, name):
    sems, src_thru, land_thru = state
    ns = 2 * _SPLIT_PEERS[kind]

    def body(src_ref, land_ref, *rest):
        sends, recvs = _split_copies(src_ref, land_ref, rest[:ns], kind)
        for cp in recvs:
            cp.wait_recv()
        for cp in sends:
            cp.wait_send()

    src_out, got = pl.pallas_call(
        body,
        name=name,
        out_shape=(pltpu.HBM(src_thru.shape, src_thru.dtype), pltpu.HBM(land_thru.shape, land_thru.dtype)),
        in_specs=(_HBM, _HBM) + (_SEM,) * ns + (_ANY,),
        out_specs=(_HBM, _HBM),
        input_output_aliases={0: 0, 1: 1},
        compiler_params=pltpu.CompilerParams(has_side_effects=_EFFECT),
    )(src_thru, land_thru, *sems, after)
    if kind == "core_swap":
        return got, src_out
    if kind == "core_fill":
        return src_out
    if kind == "chip_gather_wide":
        mine = 2 * lax.axis_index("x") + lax.axis_index("y")
        zero = jnp.zeros((), mine.dtype)
        return lax.dynamic_update_slice(got, src_out[None, None], (mine, lax.axis_index("c").astype(mine.dtype))
                                        + (zero,) * src_out.ndim)
    return _fill_own(got, src_out, kind == "chip_gather")


def _core_fill(both, name):
    n = both.shape[0]

    def body(in_ref, out_ref, send_sems, recv_sems):
        x, y, c = lax.axis_index("x"), lax.axis_index("y"), lax.axis_index("c")
        sends = [pltpu.make_async_remote_copy(src_ref=out_ref.at[k, c], dst_ref=out_ref.at[k, c],
                                              send_sem=send_sems.at[k], recv_sem=recv_sems.at[k],
                                              device_id=(x, y, 1 - c), device_id_type=MESH) for k in range(n)]
        recvs = [pltpu.make_async_remote_copy(src_ref=out_ref.at[k, c], dst_ref=out_ref.at[k, 1 - c],
                                              send_sem=send_sems.at[k], recv_sem=recv_sems.at[k],
                                              device_id=(x, y, 1 - c), device_id_type=MESH) for k in range(n)]
        for cp in sends:
            cp.start()
        for cp in recvs:
            cp.wait_recv()
        for cp in sends:
            cp.wait_send()

    return pl.pallas_call(
        body,
        in_specs=[_ANY],
        out_specs=_ANY,
        out_shape=SDS(both.shape, both.dtype),
        scratch_shapes=[pltpu.SemaphoreType.DMA((n,)), pltpu.SemaphoreType.DMA((n,))],
        input_output_aliases={0: 0},
        name=name,
    )(both)


def _core_gather(src, name):
    def body(src_ref, out_ref, send_sem, recv_sem):
        x, y, c = lax.axis_index("x"), lax.axis_index("y"), lax.axis_index("c")
        cp = pltpu.make_async_remote_copy(src_ref=src_ref, dst_ref=out_ref.at[c], send_sem=send_sem,
                                          recv_sem=recv_sem, device_id=(x, y, 1 - c), device_id_type=MESH)
        cp.start()
        pltpu.make_async_remote_copy(src_ref=src_ref, dst_ref=out_ref.at[1 - c], send_sem=send_sem,
                                     recv_sem=recv_sem, device_id=(x, y, 1 - c), device_id_type=MESH).wait_recv()
        cp.wait_send()

    out = pl.pallas_call(
        body,
        in_specs=[_ANY],
        out_specs=_ANY,
        out_shape=SDS((2,) + tuple(src.shape), src.dtype),
        scratch_shapes=[pltpu.SemaphoreType.DMA, pltpu.SemaphoreType.DMA],
        name=name,
    )(src)
    return lax.dynamic_update_index_in_dim(out, src, lax.axis_index("c"), axis=0)


_PACK_A = (("w_in", (1088, 1024)),)
_PACK_B = (("w_ba", (512, 128)), ("w_bh", (512, 128)), ("w_out", (128, 1024)), ("w_up", (704, 1024)),
           ("w_down", (352, 1024)))
_PACK_SIZES = _PACK_A + _PACK_B
_TRANSPOSED = ("w_in", "w_up")


def _slab_rows(sizes):
    return sum(r * c for _, (r, c) in sizes) // D_MODEL


def _pack_lo(key):
    keys = [k for k, _ in _PACK_B]
    return _slab_rows(_PACK_B[:keys.index(key)])


def _pack_rows(d, sizes):
    n = d[sizes[0][0]].shape[0]
    return jnp.concatenate([d[k].reshape(n, -1, D_MODEL) for k, _ in sizes], axis=1)


def _unpack_rows(slab, sizes):
    n = slab.shape[0]
    out, lo = {}, 0
    for key, (r, c) in sizes:
        rows = r * c // D_MODEL
        out[key] = slab[:, lo:lo + rows].reshape(n, r, c)
        lo += rows
    return out


def _by_core(gslab):
    return jnp.swapaxes(gslab.reshape((4, 2) + gslab.shape[1:]), 0, 1)


def _cols_to_full(t):
    return jnp.swapaxes(t, 0, 1).reshape(t.shape[1], -1)


def _full_to_cols(t):
    K = t.shape[0]
    return jnp.swapaxes(t.reshape(K, 8, -1), 0, 1)


_SMALL = (("pre_mix_norm", (1, 1024)), ("rel_bias", (32, 24)), ("hgrn_lb_raw", (2, 512)), ("hgrn_norm", (1, 128)),
          ("post_mix_norm", (1, 1024)), ("pre_ffn_norm", (1, 1024)), ("conv_b", (1, 5632)),
          ("post_ffn_norm", (1, 1024)))
_SMALL_ROWS = 96
_CONVW_ROWS = 136


_SMALL_USED = sum(r * c for _, (r, c) in _SMALL)


def _pack_small(d, extra=None):
    flat = jnp.concatenate([d[k].reshape(-1) for k, _ in _SMALL] + ([] if extra is None else [extra.reshape(-1)]))
    flat = jnp.pad(flat, (0, _SMALL_ROWS * LANE - flat.shape[0]))
    return flat.reshape(_SMALL_ROWS, LANE)


def _unpack_small(p):
    flat = p.reshape(-1)
    out, lo = {}, 0
    for k, shp in _SMALL:
        n = shp[0] * shp[1]
        out[k] = flat[lo:lo + n].reshape(shp)
        lo += n
    return out


def _local_step(x, tgt, P, plan):
    S = x.shape[0]
    P = dict(P)
    lb = _lb_fwd(P["hgrn_lb_raw"])
    hs = _prep(x, P["pre_mix_norm"], plan.start_token())
    h1 = hs[0]
    consts = [_bias_consts(d, plan.start_token()) for d in DILATIONS]
    biases, dep = [], h1
    for g in range(N_GROUPS):
        tab_t = P["rel_bias"][:, 8 * g:8 * g + 8].T
        dep = _bias_build(tab_t, consts[g][0], consts[g][1], f"bias_build{g}", dep)
        biases.append(dep.reshape(8, ATTN_BLOCK, 2 * ATTN_BLOCK))
    W = dict(plan.weights_a(dep))
    qkv0, hg, gc = _mm_fanout(h1, [W["wt_qkv"][0], W["wt_hg"], W["wt_gate"]], "nt", [bf16, f32, bf16], "proj_natural")
    qkv = [qkv0] + [_mm(hs[g], W["wt_qkv"][g], "nt", bf16, f"proj_qkv{g}") for g in (1, 2)]
    obuf, lbuf, token = [], [], plan.forward_b(qkv[2])
    for g, d in enumerate(DILATIONS):
        o_g, l_g = _attn_fwd(qkv[g], biases[g], (S // d) // ATTN_BLOCK, f"attn_fwd{g}", after=token)
        lbuf.append(l_g)
        obuf.append(o_g)
    y_attn, y_attn_b, w0, w1, w2 = _attn_merge(obuf[0], obuf[1], obuf[2], lbuf[0], lbuf[1], lbuf[2])
    y_hgrn, o_raw, ck = _hgrn_fwd(hg, lb, P["hgrn_norm"])
    wb = plan.weights_b(y_hgrn)
    P["conv_w"] = wb.pop("conv_w")
    W.update(wb)
    a, b, merged = _gate_fwd(y_attn_b, y_hgrn, W["w_ba"], W["w_bh"], gc)
    mo, x1, h2 = _mid_fwd(x, merged, W["w_out"], P["post_mix_norm"], P["pre_ffn_norm"])
    ug, uv = _mm_fanout(h2, [W["wt_up_g"], W["wt_up_v"]], "nt", [bf16, bf16], "up_proj")
    cw_g, cw_v = P["conv_w"][:, :D_FF], P["conv_w"][:, D_FF:]
    cb_g, cb_v = P["conv_b"][:, :D_FF], P["conv_b"][:, D_FF:]
    act = _conv_fwd(ug, uv, cw_g, cw_v, cb_g, cb_v)
    loss, dy, dfo, g_post_ffn = _final(x1, act, W["w_down"], tgt, P["post_ffn_norm"])
    gslab = lax.empty((8, _slab_rows(_PACK_B), D_MODEL), bf16)
    rows_b = {k: r for k, (r, _) in _PACK_B}
    gslab = _mm(act, dfo, "tn", bf16, "gw_down", into=(gslab, 0, _pack_lo("w_down"), rows_b["w_down"]))
    dact = _mm(dfo, W["w_down"], "nt", bf16, "d_act")
    dug, duv, st_g, st_v = _conv_bwd(ug, uv, dact, cw_g, cw_v, cb_g, cb_v)
    gslab = _mm(dug, h2, "tn", bf16, "gw_up_gate", into=(gslab, 0, _pack_lo("w_up"), rows_b["w_up"]))
    gslab = _mm(duv, h2, "tn", bf16, "gw_up_val", into=(gslab, 4, _pack_lo("w_up"), rows_b["w_up"]))
    dx1, dmo, g_pre_ffn, g_post_mix = _mid_bwd(dy, dug, duv, W["wt_up_g"], W["wt_up_v"], x1, mo, P["pre_ffn_norm"],
                                               P["post_mix_norm"])
    gslab = _mm(merged, dmo, "tn", bf16, "gw_out", into=(gslab, 0, _pack_lo("w_out"), rows_b["w_out"]))
    da, db, dgc, dyattn, dyhgrn = _gate_bwd(dmo, W["w_out"], a, b, gc, W["w_ba"], W["w_bh"])
    gW_ba = _mm(y_attn_b, da, "tn", bf16, "gw_ba")
    gW_bh = _mm(y_hgrn, db, "tn", bf16, "gw_bh")
    big_b = dict(w_ba=gW_ba, w_bh=gW_bh, slab=gslab)
    dos = _attn_merge_bwd(dyattn, y_attn, w0, w1, w2, after=plan.grads_b_start(big_b))
    dq_h, df_h, dv_h, dog_h, glb8, gnw8 = _hgrn_bwd(hg, o_raw, dyhgrn, ck, lb, P["hgrn_norm"],
                                                   after=plan.grads_b_exchange(dos[5]))
    dhg = [dq_h, df_h, dv_h, dog_h]
    g_lb_raw = _lb_bwd(P["hgrn_lb_raw"], glb8[0:1])
    gn = gnw8[0:1]
    g_hgrn_norm = (gn[:, 0:128] + gn[:, 128:256]) + (gn[:, 256:384] + gn[:, 384:512])
    dqkvs, gW_qkv, g_rel = [], [], []
    for g, d in enumerate(DILATIONS):
        dq, dk, dv, dbias = _attn_bwd(qkv[g], biases[g], dos[g], dos[3 + g], lbuf[g], (S // d) // ATTN_BLOCK,
                                      f"attn_bwd{g}")
        dqkvs.append([dq, dk, dv])
        gW_qkv.append(_mm(dqkvs[g], hs[g], "tn", bf16, f"gw_qkv{g}"))
        g_rel.append(_bias_grad(dbias.reshape(8, -1), consts[g][0], f"bias_grad{g}"))
    gW_hg = _mm(dhg, h1, "tn", bf16, "gw_hg")
    gW_gate = _mm(dgc, h1, "tn", bf16, "gw_gate")
    gW_in = gW_qkv + [gW_hg, gW_gate]
    token = plan.grads_a_start(gW_in)
    dh_perm = [_mm(dqkvs[g], W["wt_qkv"][g], "nn", f32, f"dh1_qkv{g}", after=token) for g in (1, 2)]
    token = plan.grads_a_exchange(dh_perm[1])
    dh_main = _mm(dqkvs[0] + dhg + [dgc], [W["wt_qkv"][0], W["wt_hg"], W["wt_gate"]], "nn", f32, "dh1_main",
                  after=token)
    grad_x, g_pre_mix = _first_bwd(x, dx1, dh_main, dh_perm[0], dh_perm[1], P["pre_mix_norm"])

    g_conv_w = jnp.concatenate([st_g[0:3], st_v[0:3]], axis=1)
    g_conv_b = jnp.concatenate([st_g[3:4], st_v[3:4]], axis=1)
    small = dict(pre_mix_norm=g_pre_mix, rel_bias=jnp.concatenate(g_rel, axis=1), hgrn_lb_raw=g_lb_raw,
                 hgrn_norm=g_hgrn_norm, post_mix_norm=g_post_mix, pre_ffn_norm=g_pre_ffn, conv_b=g_conv_b,
                 post_ffn_norm=g_post_ffn, conv_w=g_conv_w)
    return loss, grad_x, gW_in, big_b, small


def _weights_a(both):
    wt = both.reshape(-1, D_MODEL)
    return dict(
        wt_qkv=[_Rows(wt, g * QKV_G, QKV_G) for g in range(N_GROUPS)],
        wt_hg=_Rows(wt, 3 * QKV_G, 4 * HGRN_W),
        wt_gate=_Rows(wt, 3 * QKV_G + 4 * HGRN_W, wt.shape[0] - 3 * QKV_G - 4 * HGRN_W),
    )


def _weights_b(slabs):
    sh = _unpack_rows(slabs, _PACK_B)
    wt_up = sh["w_up"].reshape(-1, D_MODEL)
    return dict(
        w_ba=_cols_to_full(sh["w_ba"]),
        w_bh=_cols_to_full(sh["w_bh"]),
        w_out=sh["w_out"].reshape(D_MODEL, D_MODEL),
        wt_up_g=wt_up[:D_FF],
        wt_up_v=wt_up[D_FF:],
        w_down=sh["w_down"].reshape(D_FF, D_MODEL),
    )


def _dest_rows(sections, height):
    out = []
    for j in range(8):
        lo, hi, off, pieces = j * height, (j + 1) * height, 0, []
        for s in sections:
            a, b = max(lo, off), min(hi, off + s.shape[0])
            if a < b:
                pieces.append(s[a - off:b - off])
            off += s.shape[0]
        out.append(pieces[0] if len(pieces) == 1 else jnp.concatenate(pieces, axis=0))
    return out


def _grad_blocks_a(sections):
    rows = _dest_rows(sections, 1088)
    return jnp.stack([jnp.stack([rows[2 * k + c].astype(bf16) for k in range(4)]) for c in range(2)])


def _grad_slab_b(g):
    shards = dict(w_ba=_full_to_cols(g["w_ba"]), w_bh=_full_to_cols(g["w_bh"]))
    head = _pack_rows({k: v.astype(bf16) for k, v in shards.items()}, _PACK_B[:2])
    assert head.shape[1] == _pack_lo("w_out")
    return lax.dynamic_update_slice(g["slab"], head, (0, 0, 0))


_CONVW_SLAB_ROWS = 16


class _Traffic:
    def __init__(self, slab_a, slab_b, conv_w):
        hi = conv_w.astype(bf16)
        r1 = conv_w - hi.astype(f32)
        mid = r1.astype(bf16)
        lo = (r1 - mid.astype(f32)).astype(bf16)
        bits = jnp.stack([hi, mid, lo]).reshape(-1)
        tail = jnp.pad(bits, (0, _CONVW_SLAB_ROWS * D_MODEL - bits.shape[0])).reshape(_CONVW_SLAB_ROWS, D_MODEL)
        self.slab_b = jnp.concatenate([slab_b, tail], axis=0)
        self.state_a, tok = _split_start(slab_a, "chip_gather_wide", "ag_a_start")
        self.state_b, self.token = _split_start(self.slab_b, "chip_gather_wide", "ag_b_start", after=tok)
        self.state = None
        self.state_gb = None

    def start_token(self):
        return self.token

    def weights_a(self, after):
        half = _split_wait(self.state_a, after, "chip_gather_wide", "ag_a_wait")
        return _weights_a(_core_fill(half, "ag_a_cores"))

    def forward_b(self, after):
        half = _split_wait(self.state_b, after, "chip_gather_wide", "ag_b_wait")
        self.state, token = _split_start(half, "core_fill", "ag_b_cores_start")
        return token

    def weights_b(self, after):
        both = _split_wait(self.state, after, "core_fill", "ag_b_cores_wait")
        slabs = both.reshape((8,) + tuple(self.slab_b.shape))
        rows = _slab_rows(_PACK_B)
        out = _weights_b(slabs[:, :rows])
        pieces = slabs[:, rows:].reshape(8, -1)[:, :3 * 3 * 704].reshape(8, 3, 3, 704).astype(f32)
        out["conv_w"] = _cols_to_full((pieces[:, 0] + pieces[:, 1]) + pieces[:, 2])
        return out

    def grads_b_start(self, grads):
        self.state, token = _split_start(_by_core(_grad_slab_b(grads)), "core_swap", "rs_b_cores_start")
        return token

    def grads_b_exchange(self, after):
        from_sib, by_core = _split_wait(self.state, after, "core_swap", "rs_b_cores_wait")
        self.state_gb, token = _split_start(_pair_add(by_core, from_sib, "rs_b_pair_add"), "chip_xchg", "rs_b_start")
        return token

    def grads_a_start(self, sections):
        self.state, token = _split_start(_grad_blocks_a(sections), "core_swap", "rs_a_cores_start")
        return token

    def grads_a_exchange(self, after):
        from_sib, by_core = _split_wait(self.state, after, "core_swap", "rs_a_cores_wait")
        self.state, token = _split_start(_pair_add(by_core, from_sib, "rs_a_pair_add"), "chip_xchg", "rs_a_start")
        return token

    def parts(self, after):
        slab = _split_wait(self.state_gb, after, "chip_xchg", "rs_b_wait")
        parts, lo = _unpack_rows(slab, _PACK_B), 0
        for key, (r, c) in _PACK_B:
            if c == D_MODEL:
                parts[key] = _Rows(slab, lo, r)
            lo += r * c // D_MODEL
        parts["w_in"] =_split_wait(self.state, after, "chip_xchg", "rs_a_wait")
        return parts


def kernel(x, pre_mix_norm, w_in, rel_bias, hgrn_lb_raw, hgrn_norm, w_branch_attn, w_branch_hgrn, w_out, post_mix_norm, pre_ffn_norm, w_up, conv_w, conv_b, w_down, post_ffn_norm, loss_target, m_pre_mix_norm, m_w_in, m_rel_bias, m_hgrn_lb_raw, m_hgrn_norm, m_w_branch_attn, m_w_branch_hgrn, m_w_out, m_post_mix_norm, m_pre_ffn_norm, m_w_up, m_conv_w, m_conv_b, m_w_down, m_post_ffn_norm, v_pre_mix_norm, v_w_in, v_rel_bias, v_hgrn_lb_raw, v_hgrn_norm, v_w_branch_attn, v_w_branch_hgrn, v_w_out, v_post_mix_norm, v_pre_ffn_norm, v_w_up, v_conv_w, v_conv_b, v_w_down, v_post_ffn_norm):
    ci = lax.axis_index("c")
    dev = 4 * lax.axis_index("x") + 2 * lax.axis_index("y") + ci
    tr = lambda t: jnp.swapaxes(t[0], 0, 1)
    wts = dict(w_in=tr(w_in), w_ba=w_branch_attn[0], w_bh=w_branch_hgrn[0], w_out=w_out[0], w_up=tr(w_up),
               w_down=w_down[0])
    mom = dict(w_in=tr(m_w_in), w_ba=m_w_branch_attn[0], w_bh=m_w_branch_hgrn[0], w_out=m_w_out[0], w_up=tr(m_w_up),
               w_down=m_w_down[0])
    var = dict(w_in=tr(v_w_in), w_ba=v_w_branch_attn[0], w_bh=v_w_branch_hgrn[0], w_out=v_w_out[0], w_up=tr(v_w_up),
               w_down=v_w_down[0])
    small_w = dict(pre_mix_norm=pre_mix_norm, rel_bias=rel_bias, hgrn_lb_raw=hgrn_lb_raw, hgrn_norm=hgrn_norm,
                   post_mix_norm=post_mix_norm, pre_ffn_norm=pre_ffn_norm, conv_b=conv_b, post_ffn_norm=post_ffn_norm)
    small_m = dict(pre_mix_norm=m_pre_mix_norm, rel_bias=m_rel_bias, hgrn_lb_raw=m_hgrn_lb_raw, hgrn_norm=m_hgrn_norm,
                   post_mix_norm=m_post_mix_norm, pre_ffn_norm=m_pre_ffn_norm, conv_b=m_conv_b,
                   post_ffn_norm=m_post_ffn_norm)
    small_v = dict(pre_mix_norm=v_pre_mix_norm, rel_bias=v_rel_bias, hgrn_lb_raw=v_hgrn_lb_raw, hgrn_norm=v_hgrn_norm,
                   post_mix_norm=v_post_mix_norm, pre_ffn_norm=v_pre_ffn_norm, conv_b=v_conv_b,
                   post_ffn_norm=v_post_ffn_norm)

    plan = _Traffic(wts["w_in"].astype(bf16),
                    _pack_rows({k: wts[k].astype(bf16)[None] for k, _ in _PACK_B}, _PACK_B)[0], conv_w[0])

    loss8, grad_x, _, _, small = _local_step(x[0], loss_target[0], small_w, plan)
    spack = jnp.concatenate([_pack_small(small, loss8[0, 0:1]),
                             jnp.pad(small["conv_w"].reshape(-1, LANE), ((0, _CONVW_ROWS - 132), (0, 0)))], axis=0)
    small_state, token = _split_start(spack, "chip_gather", "ag_small_start")

    parts = plan.parts(token)
    outs_big = {}
    for k, _ in _PACK_SIZES:
        outs_big[k] = _adamw(wts[k], mom[k], var[k], parts[k], "adamw_" + k)

    by_chip = _split_wait(small_state, outs_big["w_in"][1], "chip_gather", "ag_small_wait")
    allp = _core_gather(by_chip, "ag_small_cores")
    ssum = _sum8(allp, "small_sum")
    gs = ssum[:_SMALL_ROWS]
    loss = ssum[_SMALL_USED // LANE, _SMALL_USED % LANE]
    res_small = _adamw(_pack_small(small_w), _pack_small(small_m), _pack_small(small_v), gs, "adamw_small")
    sm = [_unpack_small(t) for t in res_small]
    g_cw_full = ssum[_SMALL_ROWS:_SMALL_ROWS + 132].reshape(3, 2 * D_FF)
    g_cw = lax.dynamic_slice_in_dim(g_cw_full, dev * 704, 704, axis=1)
    res_cw = _adamw(conv_w[0], m_conv_w[0], v_conv_w[0], g_cw, "adamw_conv_w")

    def pick(i):
        def big_(k):
            t = outs_big[k][i]
            return (jnp.swapaxes(t, 0, 1) if k in _TRANSPOSED else t)[None]
        return [sm[i]["pre_mix_norm"], big_("w_in"), sm[i]["rel_bias"], sm[i]["hgrn_lb_raw"], sm[i]["hgrn_norm"],
                big_("w_ba"), big_("w_bh"), big_("w_out"), sm[i]["post_mix_norm"], sm[i]["pre_ffn_norm"],
                big_("w_up"), res_cw[i][None], sm[i]["conv_b"], big_("w_down"), sm[i]["post_ffn_norm"]]

    return (loss, grad_x[None], *pick(0), *pick(1), *pick(2), *pick(3))
```
